```python
import math
import jax, jax.numpy as jnp
from jax import lax
import numpy as np

D_MODEL = 2048
BATCH = 8
SEQ = 2048
DEPTH = 1

CHUNK = 64
Q_BLOCK = 128

N_HEADS = 16
QK_NOPE = 128
QK_ROPE = 64
V_HEAD = 128
Q_LORA = 512
KV_LORA = 512
ROPE_THETA = 10000.0
ATTN_SCALE = (QK_NOPE + QK_ROPE) ** -0.5

CONV_WIDTH = D_MODEL
CONV_K = 3

D_FF = ((8 * D_MODEL + 3 * 256 - 1) // (3 * 256)) * 256

COL_Q_A = Q_LORA
COL_KV_A = KV_LORA
COL_K_ROPE = QK_ROPE
COL_CONV_B = CONV_WIDTH
COL_CONV_C = CONV_WIDTH
COL_CONV_X = CONV_WIDTH
COL_GATE_A = D_MODEL
COL_GATE_B = D_MODEL
D_IN_ALL = COL_Q_A + COL_KV_A + COL_K_ROPE + COL_CONV_B + COL_CONV_C + COL_CONV_X + COL_GATE_A + COL_GATE_B
SPLIT_POINTS = (
    COL_Q_A,
    COL_Q_A + COL_KV_A,
    COL_Q_A + COL_KV_A + COL_K_ROPE,
    COL_Q_A + COL_KV_A + COL_K_ROPE + COL_CONV_B,
    COL_Q_A + COL_KV_A + COL_K_ROPE + COL_CONV_B + COL_CONV_C,
    COL_Q_A + COL_KV_A + COL_K_ROPE + COL_CONV_B + COL_CONV_C + COL_CONV_X,
    COL_Q_A + COL_KV_A + COL_K_ROPE + COL_CONV_B + COL_CONV_C + COL_CONV_X + COL_GATE_A,
)

DEEPNORM_ALPHA = (2.0 * DEPTH) ** 0.25
DEEPNORM_BETA = (8.0 * DEPTH) ** -0.25
LN_EPS = 1e-5
RMS_EPS = 1e-6

kernel_name = "hybrid_mla_shortconv_swiglu_deepnorm_adaln"


def layer_norm(x, g, b):
    xf = x.astype(jnp.float32)
    mu = jnp.mean(xf, axis=-1, keepdims=True)
    var = jnp.mean(jnp.square(xf - mu), axis=-1, keepdims=True)
    y = (xf - mu) * lax.rsqrt(var + LN_EPS)
    return (y * g.astype(jnp.float32) + b.astype(jnp.float32)).astype(x.dtype)


def rms_norm(x, g):
    xf = x.astype(jnp.float32)
    y = xf * lax.rsqrt(jnp.mean(jnp.square(xf), axis=-1, keepdims=True) + RMS_EPS)
    return (y * g.astype(jnp.float32)).astype(x.dtype)


def rope_tables(positions):
    inv_freq = 1.0 / (ROPE_THETA ** (jnp.arange(0, QK_ROPE, 2, dtype=jnp.float32) / QK_ROPE))
    ang = positions.astype(jnp.float32)[..., None] * inv_freq
    return jnp.cos(ang), jnp.sin(ang)


def apply_rope(x, cos, sin):
    xf = x.astype(jnp.float32)
    x1, x2 = jnp.split(xf, 2, axis=-1)
    out = jnp.concatenate([x1 * cos - x2 * sin, x2 * cos + x1 * sin], axis=-1)
    return out.astype(x.dtype)


def chunk_causal_mla_attention(q_nope, q_rope, k_nope, k_rope, v):
    S = q_nope.shape[1]
    outs = []
    for i in range(S // Q_BLOCK):
        q0 = i * Q_BLOCK
        L = q0 + Q_BLOCK
        s = (jnp.einsum('bqhd,bkhd->bhqk', q_nope[:, q0:L], k_nope[:, :L])
             + jnp.einsum('bqhr,bkr->bhqk', q_rope[:, q0:L], k_rope[:, :L]))
        s = s.astype(jnp.float32) * ATTN_SCALE
        q_chunk = (q0 + jnp.arange(Q_BLOCK)) // CHUNK
        k_chunk = jnp.arange(L) // CHUNK
        allowed = k_chunk[None, :] <= q_chunk[:, None]
        s = jnp.where(allowed[None, None], s, jnp.float32(-1e30))
        p = jax.nn.softmax(s, axis=-1).astype(v.dtype)
        outs.append(jnp.einsum('bhqk,bkhd->bqhd', p, v[:, :L]))
    return jnp.concatenate(outs, axis=1)


def causal_depthwise_conv(z, w_conv):
    S = z.shape[1]
    zp = jnp.pad(z, ((0, 0), (CONV_K - 1, 0), (0, 0)))
    out = zp[:, 0:S] * w_conv[0]
    for k in range(1, CONV_K):
        out = out + zp[:, k:k + S] * w_conv[k]
    return out


def _fwd_setup_inputs(seed: int = 0) -> dict:
    key = jax.random.key(seed)
    ks = jax.random.split(key, 24)
    f32 = jnp.float32

    def nrm(k, shape, scale):
        return jax.random.normal(k, shape, f32) * scale

    x = nrm(ks[0], (BATCH, SEQ, D_MODEL), 1.0)
    c = nrm(ks[1], (BATCH, D_MODEL), 1.0)
    offsets = jax.random.randint(ks[2], (BATCH, 1), 0, 64, dtype=jnp.int32) * CHUNK
    positions = (offsets + jnp.arange(SEQ, dtype=jnp.int32)[None, :]).astype(jnp.int32)

    inputs = {
        "x": x,
        "c": c,
        "positions": positions,
        "w_ada": nrm(ks[3], (DEPTH, D_MODEL, 6 * D_MODEL), 0.5 * D_MODEL ** -0.5),
        "b_ada": nrm(ks[4], (DEPTH, 6 * D_MODEL), 0.01),
        "w_in": nrm(ks[5], (DEPTH, D_MODEL, D_IN_ALL), D_MODEL ** -0.5),
        "g_q_a": 1.0 + nrm(ks[6], (DEPTH, Q_LORA), 0.02),
        "w_q_b": nrm(ks[7], (DEPTH, Q_LORA, N_HEADS * (QK_NOPE + QK_ROPE)), Q_LORA ** -0.5),
        "g_kv_a": 1.0 + nrm(ks[8], (DEPTH, KV_LORA), 0.02),
        "w_kv_b": nrm(ks[9], (DEPTH, KV_LORA, N_HEADS * (QK_NOPE + V_HEAD)), KV_LORA ** -0.5),
        "w_o_a": nrm(ks[10], (DEPTH, N_HEADS * V_HEAD, D_MODEL), (N_HEADS * V_HEAD) ** -0.5 * DEEPNORM_BETA),
        "w_conv": nrm(ks[11], (DEPTH, CONV_K, CONV_WIDTH), CONV_K ** -0.5),
        "w_o_b": nrm(ks[12], (DEPTH, CONV_WIDTH, D_MODEL), CONV_WIDTH ** -0.5 * DEEPNORM_BETA),
        "w_o": nrm(ks[13], (DEPTH, D_MODEL, D_MODEL), D_MODEL ** -0.5 * DEEPNORM_BETA),
        "ln1_g": 1.0 + nrm(ks[14], (DEPTH, D_MODEL), 0.02),
        "ln1_b": nrm(ks[15], (DEPTH, D_MODEL), 0.02),
        "w_ffn_in": nrm(ks[16], (DEPTH, D_MODEL, 2 * D_FF), D_MODEL ** -0.5),
        "w_ffn_out": nrm(ks[17], (DEPTH, D_FF, D_MODEL), D_FF ** -0.5 * DEEPNORM_BETA),
        "ln2_g": 1.0 + nrm(ks[18], (DEPTH, D_MODEL), 0.02),
        "ln2_b": nrm(ks[19], (DEPTH, D_MODEL), 0.02),
    }
    return inputs


def _fwd_reference(x, c, positions, w_ada, b_ada, w_in, g_q_a, w_q_b, g_kv_a, w_kv_b, w_o_a,
              w_conv, w_o_b, w_o, ln1_g, ln1_b, w_ffn_in, w_ffn_out, ln2_g, ln2_b):
    B, S, D = x.shape
    cos, sin = rope_tables(positions)
    cos_q, sin_q = cos[:, :, None, :], sin[:, :, None, :]
    c_act = jax.nn.silu(c)

    for l in range(DEPTH):
        mod = c_act @ w_ada[l] + b_ada[l]
        shift1, scale1, gate1, shift2, scale2, gate2 = [m[:, None, :] for m in jnp.split(mod, 6, axis=-1)]

        u = x * (1.0 + scale1) + shift1
        proj = u @ w_in[l]
        q_a, kv_a, k_rope, conv_b, conv_c, conv_x, gate_a, gate_b = jnp.split(proj, SPLIT_POINTS, axis=-1)

        q = (rms_norm(q_a, g_q_a[l]) @ w_q_b[l]).reshape(B, S, N_HEADS, QK_NOPE + QK_ROPE)
        q_nope, q_rope = q[..., :QK_NOPE], apply_rope(q[..., QK_NOPE:], cos_q, sin_q)
        kv = (rms_norm(kv_a, g_kv_a[l]) @ w_kv_b[l]).reshape(B, S, N_HEADS, QK_NOPE + V_HEAD)
        k_nope, v = kv[..., :QK_NOPE], kv[..., QK_NOPE:]
        k_rope = apply_rope(k_rope, cos, sin)
        attn = chunk_causal_mla_attention(q_nope, q_rope, k_nope, k_rope, v)
        y_a = attn.reshape(B, S, N_HEADS * V_HEAD) @ w_o_a[l]

        z = conv_c * conv_x
        y_b = (conv_b * causal_depthwise_conv(z, w_conv[l])) @ w_o_b[l]

        merged = jax.nn.sigmoid(gate_a) * y_a + jax.nn.sigmoid(gate_b) * y_b
        mix_out = merged @ w_o[l]
        x = layer_norm(DEEPNORM_ALPHA * x + gate1 * mix_out, ln1_g[l], ln1_b[l])

        u2 = x * (1.0 + scale2) + shift2
        h_gate, h_up = jnp.split(u2 @ w_ffn_in[l], 2, axis=-1)
        ffn_out = (jax.nn.silu(h_gate) * h_up) @ w_ffn_out[l]
        x = layer_norm(DEEPNORM_ALPHA * x + gate2 * ffn_out, ln2_g[l], ln2_b[l])

    return x


import jax as _jax
import jax.numpy as _jnp

TWIN_FORMAT = 'train_step'
FWD_PARAMS = ['x', 'c', 'positions', 'w_ada', 'b_ada', 'w_in', 'g_q_a', 'w_q_b', 'g_kv_a', 'w_kv_b', 'w_o_a', 'w_conv', 'w_o_b', 'w_o', 'ln1_g', 'ln1_b', 'w_ffn_in', 'w_ffn_out', 'ln2_g', 'ln2_b']
TWIN_WEIGHTS = ['w_ada', 'b_ada', 'w_in', 'g_q_a', 'w_q_b', 'g_kv_a', 'w_kv_b', 'w_o_a', 'w_conv', 'w_o_b', 'w_o', 'ln1_g', 'ln1_b', 'w_ffn_in', 'w_ffn_out', 'ln2_g', 'ln2_b']
TWIN_DIFF_INPUT = 'x'
TWIN_INPUTS = ['x', 'c', 'positions', 'w_ada', 'b_ada', 'w_in', 'g_q_a', 'w_q_b', 'g_kv_a', 'w_kv_b', 'w_o_a', 'w_conv', 'w_o_b', 'w_o', 'ln1_g', 'ln1_b', 'w_ffn_in', 'w_ffn_out', 'ln2_g', 'ln2_b', 'loss_target', 'm_w_ada', 'm_b_ada', 'm_w_in', 'm_g_q_a', 'm_w_q_b', 'm_g_kv_a', 'm_w_kv_b', 'm_w_o_a', 'm_w_conv', 'm_w_o_b', 'm_w_o', 'm_ln1_g', 'm_ln1_b', 'm_w_ffn_in', 'm_w_ffn_out', 'm_ln2_g', 'm_ln2_b', 'v_w_ada', 'v_b_ada', 'v_w_in', 'v_g_q_a', 'v_w_q_b', 'v_g_kv_a', 'v_w_kv_b', 'v_w_o_a', 'v_w_conv', 'v_w_o_b', 'v_w_o', 'v_ln1_g', 'v_ln1_b', 'v_w_ffn_in', 'v_w_ffn_out', 'v_ln2_g', 'v_ln2_b']
TWIN_OUTPUTS = ['loss', 'grad_x', 'grad_w_ada', 'grad_b_ada', 'grad_w_in', 'grad_g_q_a', 'grad_w_q_b', 'grad_g_kv_a', 'grad_w_kv_b', 'grad_w_o_a', 'grad_w_conv', 'grad_w_o_b', 'grad_w_o', 'grad_ln1_g', 'grad_ln1_b', 'grad_w_ffn_in', 'grad_w_ffn_out', 'grad_ln2_g', 'grad_ln2_b', 'delta_w_ada', 'delta_b_ada', 'delta_w_in', 'delta_g_q_a', 'delta_w_q_b', 'delta_g_kv_a', 'delta_w_kv_b', 'delta_w_o_a', 'delta_w_conv', 'delta_w_o_b', 'delta_w_o', 'delta_ln1_g', 'delta_ln1_b', 'delta_w_ffn_in', 'delta_w_ffn_out', 'delta_ln2_g', 'delta_ln2_b', 'new_m_w_ada', 'new_m_b_ada', 'new_m_w_in', 'new_m_g_q_a', 'new_m_w_q_b', 'new_m_g_kv_a', 'new_m_w_kv_b', 'new_m_w_o_a', 'new_m_w_conv', 'new_m_w_o_b', 'new_m_w_o', 'new_m_ln1_g', 'new_m_ln1_b', 'new_m_w_ffn_in', 'new_m_w_ffn_out', 'new_m_ln2_g', 'new_m_ln2_b', 'new_v_w_ada', 'new_v_b_ada', 'new_v_w_in', 'new_v_g_q_a', 'new_v_w_q_b', 'new_v_g_kv_a', 'new_v_w_kv_b', 'new_v_w_o_a', 'new_v_w_conv', 'new_v_w_o_b', 'new_v_w_o', 'new_v_ln1_g', 'new_v_ln1_b', 'new_v_w_ffn_in', 'new_v_w_ffn_out', 'new_v_ln2_g', 'new_v_ln2_b']
TWIN_LEAF_KINDS = {'loss': 'loss', 'grad_x': 'grad_x', 'grad_w_ada': 'grad_w', 'grad_b_ada': 'grad_w', 'grad_w_in': 'grad_w', 'grad_g_q_a': 'grad_w', 'grad_w_q_b': 'grad_w', 'grad_g_kv_a': 'grad_w', 'grad_w_kv_b': 'grad_w', 'grad_w_o_a': 'grad_w', 'grad_w_conv': 'grad_w', 'grad_w_o_b': 'grad_w', 'grad_w_o': 'grad_w', 'grad_ln1_g': 'grad_w', 'grad_ln1_b': 'grad_w', 'grad_w_ffn_in': 'grad_w', 'grad_w_ffn_out': 'grad_w', 'grad_ln2_g': 'grad_w', 'grad_ln2_b': 'grad_w', 'delta_w_ada': 'delta_w', 'delta_b_ada': 'delta_w', 'delta_w_in': 'delta_w', 'delta_g_q_a': 'delta_w', 'delta_w_q_b': 'delta_w', 'delta_g_kv_a': 'delta_w', 'delta_w_kv_b': 'delta_w', 'delta_w_o_a': 'delta_w', 'delta_w_conv': 'delta_w', 'delta_w_o_b': 'delta_w', 'delta_w_o': 'delta_w', 'delta_ln1_g': 'delta_w', 'delta_ln1_b': 'delta_w', 'delta_w_ffn_in': 'delta_w', 'delta_w_ffn_out': 'delta_w', 'delta_ln2_g': 'delta_w', 'delta_ln2_b': 'delta_w', 'new_m_w_ada': 'new_m', 'new_m_b_ada': 'new_m', 'new_m_w_in': 'new_m', 'new_m_g_q_a': 'new_m', 'new_m_w_q_b': 'new_m', 'new_m_g_kv_a': 'new_m', 'new_m_w_kv_b': 'new_m', 'new_m_w_o_a': 'new_m', 'new_m_w_conv': 'new_m', 'new_m_w_o_b': 'new_m', 'new_m_w_o': 'new_m', 'new_m_ln1_g': 'new_m', 'new_m_ln1_b': 'new_m', 'new_m_w_ffn_in': 'new_m', 'new_m_w_ffn_out': 'new_m', 'new_m_ln2_g': 'new_m', 'new_m_ln2_b': 'new_m', 'new_v_w_ada': 'new_v', 'new_v_b_ada': 'new_v', 'new_v_w_in': 'new_v', 'new_v_g_q_a': 'new_v', 'new_v_w_q_b': 'new_v', 'new_v_g_kv_a': 'new_v', 'new_v_w_kv_b': 'new_v', 'new_v_w_o_a': 'new_v', 'new_v_w_conv': 'new_v', 'new_v_w_o_b': 'new_v', 'new_v_w_o': 'new_v', 'new_v_ln1_g': 'new_v', 'new_v_ln1_b': 'new_v', 'new_v_w_ffn_in': 'new_v', 'new_v_w_ffn_out': 'new_v', 'new_v_ln2_g': 'new_v', 'new_v_ln2_b': 'new_v'}


def _forward(args):
    return _fwd_reference(*[args[k] for k in FWD_PARAMS])


def _output_shape():
    out = _jax.eval_shape(lambda: _forward(_fwd_setup_inputs(0)))
    return out.shape, out.dtype

N_MICROBATCH = 1
ADAM_LR = 0.001
ADAM_B1 = 0.9
ADAM_B2 = 0.999
ADAM_EPS = 1e-08
ADAM_WD = 0.01
ADAM_STEP = 10
PER_EXAMPLE_BATCH_AXIS = {'x': 0, 'c': 0, 'positions': 0, 'loss_target': 0}
SHARED_INPUTS = []
_WEIGHT_DTYPES = {'w_ada': _jnp.float32, 'b_ada': _jnp.float32, 'w_in': _jnp.float32, 'g_q_a': _jnp.float32, 'w_q_b': _jnp.float32, 'g_kv_a': _jnp.float32, 'w_kv_b': _jnp.float32, 'w_o_a': _jnp.float32, 'w_conv': _jnp.float32, 'w_o_b': _jnp.float32, 'w_o': _jnp.float32, 'ln1_g': _jnp.float32, 'ln1_b': _jnp.float32, 'w_ffn_in': _jnp.float32, 'w_ffn_out': _jnp.float32, 'ln2_g': _jnp.float32, 'ln2_b': _jnp.float32}
MOMENT_SCALE = {'w_ada': 7.702271e-03, 'b_ada': 1.265092e-02, 'w_in': 3.067973e-03, 'g_q_a': 8.619743e-04, 'w_q_b': 3.478129e-04, 'g_kv_a': 2.502186e-03, 'w_kv_b': 7.578144e-04, 'w_o_a': 1.847214e-03, 'w_conv': 4.094223e-03, 'w_o_b': 6.698073e-03, 'w_o': 6.973318e-03, 'ln1_g': 2.928238e-01, 'ln1_b': 1.401769e-01, 'w_ffn_in': 4.101233e-03, 'w_ffn_out': 1.124803e-02, 'ln2_g': 8.007227e+00, 'ln2_b': 2.047889e-01}


def _to_microbatches(a, axis):
    t = _jnp.moveaxis(a, axis, 0)
    t = t.reshape((N_MICROBATCH, t.shape[0] // N_MICROBATCH) + t.shape[1:])
    return _jnp.moveaxis(t, 1, axis + 1)


def setup_inputs(seed: int = 0) -> dict:
    inp = _fwd_setup_inputs(seed)
    key = _jax.random.fold_in(_jax.random.key(seed), 7919)
    shape, _ = _output_shape()
    out = dict(inp)
    out["loss_target"] = _jax.random.normal(_jax.random.fold_in(key, 0), shape, _jnp.float32)
    for i, name in enumerate(TWIN_WEIGHTS):
        w = inp[name].astype(_jnp.float32)
        if MOMENT_SCALE is None:
            s = _jnp.sqrt(_jnp.mean(_jnp.square(w)) + 1e-30)
        else:
            s = MOMENT_SCALE[name]
        km, kv = _jax.random.split(_jax.random.fold_in(key, i + 1))
        out[name] = w
        out["m_" + name] = s * _jax.random.normal(km, w.shape, _jnp.float32)
        out["v_" + name] = (s * s) * _jax.random.uniform(kv, w.shape, _jnp.float32, 0.5, 1.5)
    if N_MICROBATCH > 1:
        for name, axis in PER_EXAMPLE_BATCH_AXIS.items():
            out[name] = _to_microbatches(out[name], axis)
    return {'x': out['x'], 'c': out['c'], 'positions': out['positions'], 'w_ada': out['w_ada'], 'b_ada': out['b_ada'], 'w_in': out['w_in'], 'g_q_a': out['g_q_a'], 'w_q_b': out['w_q_b'], 'g_kv_a': out['g_kv_a'], 'w_kv_b': out['w_kv_b'], 'w_o_a': out['w_o_a'], 'w_conv': out['w_conv'], 'w_o_b': out['w_o_b'], 'w_o': out['w_o'], 'ln1_g': out['ln1_g'], 'ln1_b': out['ln1_b'], 'w_ffn_in': out['w_ffn_in'], 'w_ffn_out': out['w_ffn_out'], 'ln2_g': out['ln2_g'], 'ln2_b': out['ln2_b'], 'loss_target': out['loss_target'], 'm_w_ada': out['m_w_ada'], 'm_b_ada': out['m_b_ada'], 'm_w_in': out['m_w_in'], 'm_g_q_a': out['m_g_q_a'], 'm_w_q_b': out['m_w_q_b'], 'm_g_kv_a': out['m_g_kv_a'], 'm_w_kv_b': out['m_w_kv_b'], 'm_w_o_a': out['m_w_o_a'], 'm_w_conv': out['m_w_conv'], 'm_w_o_b': out['m_w_o_b'], 'm_w_o': out['m_w_o'], 'm_ln1_g': out['m_ln1_g'], 'm_ln1_b': out['m_ln1_b'], 'm_w_ffn_in': out['m_w_ffn_in'], 'm_w_ffn_out': out['m_w_ffn_out'], 'm_ln2_g': out['m_ln2_g'], 'm_ln2_b': out['m_ln2_b'], 'v_w_ada': out['v_w_ada'], 'v_b_ada': out['v_b_ada'], 'v_w_in': out['v_w_in'], 'v_g_q_a': out['v_g_q_a'], 'v_w_q_b': out['v_w_q_b'], 'v_g_kv_a': out['v_g_kv_a'], 'v_w_kv_b': out['v_w_kv_b'], 'v_w_o_a': out['v_w_o_a'], 'v_w_conv': out['v_w_conv'], 'v_w_o_b': out['v_w_o_b'], 'v_w_o': out['v_w_o'], 'v_ln1_g': out['v_ln1_g'], 'v_ln1_b': out['v_ln1_b'], 'v_w_ffn_in': out['v_w_ffn_in'], 'v_w_ffn_out': out['v_w_ffn_out'], 'v_ln2_g': out['v_ln2_g'], 'v_ln2_b': out['v_ln2_b']}


def _loss(weights, diff, rest, loss_target):
    with _jax.named_scope("forward"):
        args = {**rest, TWIN_DIFF_INPUT: diff, **{k: w.astype(_WEIGHT_DTYPES[k]) for k, w in weights.items()}}
        y = _forward(args)
    with _jax.named_scope("loss_head"):
        err = _jnp.square(y.astype(_jnp.float32) - loss_target)
        return 0.5 * _jnp.sum(_jnp.mean(err, axis=-1)) if err.ndim else 0.5 * err


def _adamw(w, g, m, v):
    m = ADAM_B1 * m + (1.0 - ADAM_B1) * g
    v = ADAM_B2 * v + (1.0 - ADAM_B2) * _jnp.square(g)
    m_hat = m / (1.0 - ADAM_B1 ** ADAM_STEP)
    v_hat = v / (1.0 - ADAM_B2 ** ADAM_STEP)
    delta = -ADAM_LR * (m_hat / (_jnp.sqrt(v_hat) + ADAM_EPS) + ADAM_WD * w)
    return delta, m, v


def reference(x, c, positions, w_ada, b_ada, w_in, g_q_a, w_q_b, g_kv_a, w_kv_b, w_o_a, w_conv, w_o_b, w_o, ln1_g, ln1_b, w_ffn_in, w_ffn_out, ln2_g, ln2_b, loss_target, m_w_ada, m_b_ada, m_w_in, m_g_q_a, m_w_q_b, m_g_kv_a, m_w_kv_b, m_w_o_a, m_w_conv, m_w_o_b, m_w_o, m_ln1_g, m_ln1_b, m_w_ffn_in, m_w_ffn_out, m_ln2_g, m_ln2_b, v_w_ada, v_b_ada, v_w_in, v_g_q_a, v_w_q_b, v_g_kv_a, v_w_kv_b, v_w_o_a, v_w_conv, v_w_o_b, v_w_o, v_ln1_g, v_ln1_b, v_w_ffn_in, v_w_ffn_out, v_ln2_g, v_ln2_b):
    given = dict(x=x, c=c, positions=positions, w_ada=w_ada, b_ada=b_ada, w_in=w_in, g_q_a=g_q_a, w_q_b=w_q_b, g_kv_a=g_kv_a, w_kv_b=w_kv_b, w_o_a=w_o_a, w_conv=w_conv, w_o_b=w_o_b, w_o=w_o, ln1_g=ln1_g, ln1_b=ln1_b, w_ffn_in=w_ffn_in, w_ffn_out=w_ffn_out, ln2_g=ln2_g, ln2_b=ln2_b, loss_target=loss_target, m_w_ada=m_w_ada, m_b_ada=m_b_ada, m_w_in=m_w_in, m_g_q_a=m_g_q_a, m_w_q_b=m_w_q_b, m_g_kv_a=m_g_kv_a, m_w_kv_b=m_w_kv_b, m_w_o_a=m_w_o_a, m_w_conv=m_w_conv, m_w_o_b=m_w_o_b, m_w_o=m_w_o, m_ln1_g=m_ln1_g, m_ln1_b=m_ln1_b, m_w_ffn_in=m_w_ffn_in, m_w_ffn_out=m_w_ffn_out, m_ln2_g=m_ln2_g, m_ln2_b=m_ln2_b, v_w_ada=v_w_ada, v_b_ada=v_b_ada, v_w_in=v_w_in, v_g_q_a=v_g_q_a, v_w_q_b=v_w_q_b, v_g_kv_a=v_g_kv_a, v_w_kv_b=v_w_kv_b, v_w_o_a=v_w_o_a, v_w_conv=v_w_conv, v_w_o_b=v_w_o_b, v_w_o=v_w_o, v_ln1_g=v_ln1_g, v_ln1_b=v_ln1_b, v_w_ffn_in=v_w_ffn_in, v_w_ffn_out=v_w_ffn_out, v_ln2_g=v_ln2_g, v_ln2_b=v_ln2_b)
    weights = {n: given[n] for n in TWIN_WEIGHTS}
    shared = {n: given[n] for n in SHARED_INPUTS}
    per_example = {n: given[n] for n in ['x', 'c', 'positions']}
    grad_fn = _jax.value_and_grad(_loss, argnums=(0, 1))

    def one_microbatch(ex, loss_target):
        ex = dict(ex)
        diff = ex.pop(TWIN_DIFF_INPUT)
        return grad_fn(weights, diff, {**shared, **ex}, loss_target)

    if N_MICROBATCH == 1:
        loss, (grad_w, grad_x) = one_microbatch(per_example, given["loss_target"])
    else:
        def body(carry, xs):
            loss_sum, grad_sum = carry
            l_k, (gw_k, gx_k) = one_microbatch(xs[0], xs[1])
            with _jax.named_scope("update"):
                return (loss_sum + l_k, _jax.tree.map(_jnp.add, grad_sum, gw_k)), gx_k

        init = (_jnp.zeros((), _jnp.float32), _jax.tree.map(_jnp.zeros_like, weights))
        (loss, grad_w), grad_x = _jax.lax.scan(body, init, (per_example, given["loss_target"]))
    with _jax.named_scope("update"):
        delta_w, new_m, new_v = {}, {}, {}
        for n in TWIN_WEIGHTS:
            delta_w[n], new_m[n], new_v[n] = _adamw(weights[n], grad_w[n], given["m_" + n], given["v_" + n])
    return (loss, grad_x, *[grad_w[n] for n in TWIN_WEIGHTS], *[delta_w[n] for n in TWIN_WEIGHTS],
            *[new_m[n] for n in TWIN_WEIGHTS], *[new_v[n] for n in TWIN_WEIGHTS])
```

```python
import functools

import jax
import jax.numpy as jnp
from jax import lax
from jax.experimental import pallas as pl
from jax.experimental.pallas import tpu as pltpu

F32 = jnp.float32
BF16 = jnp.bfloat16
MESH_ID = pl.DeviceIdType.MESH
AXES = ("x", "y", "c")
N_DEV = 8

CHUNK = 64
QK_NOPE = 128
QK_ROPE = 64
V_HEAD = 128
QK_CAT = QK_NOPE + QK_ROPE
ROPE_THETA = 10000.0
ATTN_SCALE = (QK_NOPE + QK_ROPE) ** -0.5
CONV_K = 3
DEEPNORM_ALPHA = 2.0 ** 0.25
LN_EPS = 1e-5
RMS_EPS = 1e-6
NEG_INF = -1e30

ADAM_LR = 0.001
ADAM_B1 = 0.9
ADAM_B2 = 0.999
ADAM_EPS = 1e-08
ADAM_WD = 0.01
ADAM_STEP = 10

LANE = 128
COL_BLOCK = 256
PACK_COLS = 512
PACK_ROW_ALIGN = 16
VMEM_LIMIT = 48 * 1024 * 1024


def _round_up(n, m):
    return (n + m - 1) // m * m


def _tile(n, pref, align=LANE):
    best = None
    t = align
    while t <= min(n, pref):
        if n % t == 0:
            best = t
        t += align
    return best if best is not None else n


def _cparams(sem=None):
    return pltpu.CompilerParams(dimension_semantics=sem, vmem_limit_bytes=VMEM_LIMIT)


def _sigmoid(x):
    return 1.0 / (1.0 + jnp.exp(-x))


def _matmul(a, b, mode, out_dtype, name, tm=1024, tn=1024, tk=512):
    if mode == "nn":
        (M, K), (K2, N) = a.shape, b.shape
    elif mode == "nt":
        (M, K), (N, K2) = a.shape, b.shape
    else:
        (K, M), (K2, N) = a.shape, b.shape
    assert K == K2, (a.shape, b.shape, mode)
    tm, tn, tk = _tile(M, tm), _tile(N, tn), _tile(K, tk)
    nk = K // tk
    if mode == "nn":
        a_spec = pl.BlockSpec((tm, tk), lambda i, j, k: (i, k))
        b_spec = pl.BlockSpec((tk, tn), lambda i, j, k: (k, j))
        dims = (((1,), (0,)), ((), ()))
    elif mode == "nt":
        a_spec = pl.BlockSpec((tm, tk), lambda i, j, k: (i, k))
        b_spec = pl.BlockSpec((tn, tk), lambda i, j, k: (j, k))
        dims = (((1,), (1,)), ((), ()))
    else:
        a_spec = pl.BlockSpec((tk, tm), lambda i, j, k: (k, i))
        b_spec = pl.BlockSpec((tk, tn), lambda i, j, k: (k, j))
        dims = (((0,), (0,)), ((), ()))

    def body(a_ref, b_ref, o_ref, acc_ref):
        k = pl.program_id(2)

        @pl.when(k == 0)
        def _():
            acc_ref[...] = jnp.zeros_like(acc_ref)

        acc_ref[...] += lax.dot_general(a_ref[...].astype(BF16), b_ref[...].astype(BF16), dims,
                                        preferred_element_type=F32)

        @pl.when(k == nk - 1)
        def _():
            o_ref[...] = acc_ref[...].astype(o_ref.dtype)

    return pl.pallas_call(
        body, name=name, grid=(M // tm, N // tn, nk),
        in_specs=[a_spec, b_spec],
        out_specs=pl.BlockSpec((tm, tn), lambda i, j, k: (i, j)),
        out_shape=jax.ShapeDtypeStruct((M, N), out_dtype),
        scratch_shapes=[pltpu.VMEM((tm, tn), F32)],
        compiler_params=_cparams(("parallel", "parallel", "arbitrary")),
    )(a, b)


def _modulate_in(x, mod, ts):
    S, D = x.shape

    def body(x_ref, mod_ref, u_ref):
        u_ref[...] = (x_ref[...] * (1.0 + mod_ref[1:2, :]) + mod_ref[0:1, :]).astype(BF16)

    return pl.pallas_call(
        body, name="modulate_in", grid=(S // ts,),
        in_specs=[pl.BlockSpec((ts, D), lambda i: (i, 0)), pl.BlockSpec((6, D), lambda i: (0, 0))],
        out_specs=pl.BlockSpec((ts, D), lambda i: (i, 0)),
        out_shape=jax.ShapeDtypeStruct((S, D), BF16),
        compiler_params=_cparams(("parallel",)),
    )(x, mod)


def _rms_fwd(proj, g, blk, L, ts, name):
    S = proj.shape[0]

    def body(a_ref, g_ref, y_ref):
        a = a_ref[...]
        r = lax.rsqrt(jnp.mean(a * a, axis=-1, keepdims=True) + RMS_EPS)
        y_ref[...] = (a * r * g_ref[...]).astype(BF16)

    return pl.pallas_call(
        body, name=name, grid=(S // ts,),
        in_specs=[pl.BlockSpec((ts, L), lambda i: (i, blk)), pl.BlockSpec((1, L), lambda i: (0, 0))],
        out_specs=pl.BlockSpec((ts, L), lambda i: (i, 0)),
        out_shape=jax.ShapeDtypeStruct((S, L), BF16),
        compiler_params=_cparams(("parallel",)),
    )(proj, g)


def _rotate_half_pairs(x, sign):
    w = x.shape[-1]
    lane = lax.broadcasted_iota(jnp.int32, x.shape, x.ndim - 1)
    first = (lane % QK_ROPE) < (QK_ROPE // 2)
    from_right = pltpu.roll(x, w - QK_ROPE // 2, axis=x.ndim - 1)
    from_left = pltpu.roll(x, QK_ROPE // 2, axis=x.ndim - 1)
    return jnp.where(first, -sign * from_right, sign * from_left)


def _qk_prep(q, kv, proj, kr_blk, cos_q, sin_q, cos_k, sin_k, H, ts):
    S = q.shape[0]
    nope_w, rope_w = H * QK_NOPE, H * QK_ROPE

    def body(q_ref, kv_ref, kr_ref, cq_ref, sq_ref, ck_ref, sk_ref, qc_ref, kc_ref, vh_ref):
        qr = q_ref[:, nope_w:]
        qr = qr * cq_ref[...] + _rotate_half_pairs(qr, 1.0) * sq_ref[...]
        kr = kr_ref[...]
        kr = kr * ck_ref[...] + _rotate_half_pairs(kr, 1.0) * sk_ref[...]
        kr = kr[:, :QK_ROPE].astype(BF16)
        for h in range(H):
            qc_ref[h, :, 0:QK_NOPE] = q_ref[:, h * QK_NOPE:(h + 1) * QK_NOPE].astype(BF16)
            qc_ref[h, :, QK_NOPE:QK_CAT] = qr[:, h * QK_ROPE:(h + 1) * QK_ROPE].astype(BF16)
            kc_ref[h, :, 0:QK_NOPE] = kv_ref[:, h * QK_NOPE:(h + 1) * QK_NOPE].astype(BF16)
            kc_ref[h, :, QK_NOPE:QK_CAT] = kr
            vh_ref[h, :, :] = kv_ref[:, nope_w + h * V_HEAD:nope_w + (h + 1) * V_HEAD].astype(BF16)

    row = lambda w: pl.BlockSpec((ts, w), lambda i: (i, 0))
    return pl.pallas_call(
        body, name="qk_prep", grid=(S // ts,),
        in_specs=[row(nope_w + rope_w), row(nope_w + H * V_HEAD),
                  pl.BlockSpec((ts, COL_BLOCK), lambda i: (i, kr_blk)),
                  row(rope_w), row(rope_w), row(COL_BLOCK), row(COL_BLOCK)],
        out_specs=[pl.BlockSpec((H, ts, QK_CAT), lambda i: (0, i, 0)),
                   pl.BlockSpec((H, ts, QK_CAT), lambda i: (0, i, 0)),
                   pl.BlockSpec((H, ts, V_HEAD), lambda i: (0, i, 0))],
        out_shape=[jax.ShapeDtypeStruct((H, S, QK_CAT), BF16), jax.ShapeDtypeStruct((H, S, QK_CAT), BF16),
                   jax.ShapeDtypeStruct((H, S, V_HEAD), BF16)],
        compiler_params=_cparams(("parallel",)),
    )(q, kv, proj, cos_q, sin_q, cos_k, sin_k)


def _chunk_mask(i, j, T):
    rows = (i * T + lax.broadcasted_iota(jnp.int32, (T, T), 0)) // CHUNK
    cols = (j * T + lax.broadcasted_iota(jnp.int32, (T, T), 1)) // CHUNK
    return cols <= rows


NT_DIMS = (((1,), (1,)), ((), ()))
TN_DIMS = (((0,), (0,)), ((), ()))


def _attn_fwd(qc, kc, vh, T):
    H, S, _ = qc.shape
    n = S // T

    def body(q_ref, k_ref, v_ref, o_ref, lse_ref, m_ref, l_ref, acc_ref):
        i, j = pl.program_id(1), pl.program_id(2)

        @pl.when(j == 0)
        def _():
            m_ref[...] = jnp.full_like(m_ref, NEG_INF)
            l_ref[...] = jnp.zeros_like(l_ref)
            acc_ref[...] = jnp.zeros_like(acc_ref)

        @pl.when(j <= i)
        def _():
            s = lax.dot_general(q_ref[0], k_ref[0], NT_DIMS, preferred_element_type=F32) * ATTN_SCALE
            s = jnp.where(_chunk_mask(i, j, T), s, NEG_INF)
            m_old = m_ref[...]
            m_new = jnp.maximum(m_old, jnp.max(s, axis=-1, keepdims=True))
            alpha = jnp.exp(m_old - m_new)
            p = jnp.exp(s - m_new)
            l_ref[...] = alpha * l_ref[...] + jnp.sum(p, axis=-1, keepdims=True)
            acc_ref[...] = alpha * acc_ref[...] + jnp.dot(p.astype(BF16), v_ref[0], preferred_element_type=F32)
            m_ref[...] = m_new

        @pl.when(j == i)
        def _():
            o_ref[...] = acc_ref[...] / l_ref[...]
            lse_ref[0] = m_ref[...] + jnp.log(l_ref[...])

    return pl.pallas_call(
        body, name="attn_fwd", grid=(H, n, n),
        in_specs=[pl.BlockSpec((1, T, QK_CAT), lambda h, i, j: (h, i, 0)),
                  pl.BlockSpec((1, T, QK_CAT), lambda h, i, j: (h, jnp.minimum(j, i), 0)),
                  pl.BlockSpec((1, T, V_HEAD), lambda h, i, j: (h, jnp.minimum(j, i), 0))],
        out_specs=[pl.BlockSpec((T, V_HEAD), lambda h, i, j: (i, h)),
                   pl.BlockSpec((1, T, 1), lambda h, i, j: (h, i, 0))],
        out_shape=[jax.ShapeDtypeStruct((S, H * V_HEAD), F32), jax.ShapeDtypeStruct((H, S, 1), F32)],
        scratch_shapes=[pltpu.VMEM((T, 1), F32), pltpu.VMEM((T, 1), F32), pltpu.VMEM((T, V_HEAD), F32)],
        compiler_params=_cparams(("parallel", "parallel", "arbitrary")),
    )(qc, kc, vh)


def _shift_rows(z, k):
    if k == 0:
        return z
    n = z.shape[0]
    row = lax.broadcasted_iota(jnp.int32, z.shape, 0)
    if k > 0:
        return jnp.where(row >= k, pltpu.roll(z, k, axis=0), 0.0)
    return jnp.where(row < n + k, pltpu.roll(z, n + k, axis=0), 0.0)


def _conv_fwd(proj, w_conv, blk_b, blk_c, blk_x):
    S = proj.shape[0]
    D = w_conv.shape[1]
    nb = D // COL_BLOCK

    def body(cb_ref, cc_ref, cx_ref, w_ref, o_ref):
        z = cc_ref[...] * cx_ref[...]
        conv = w_ref[2:3, :] * z + w_ref[1:2, :] * _shift_rows(z, 1) + w_ref[0:1, :] * _shift_rows(z, 2)
        o_ref[...] = (cb_ref[...] * conv).astype(BF16)

    col = lambda off: pl.BlockSpec((S, COL_BLOCK), lambda j: (0, off + j))
    return pl.pallas_call(
        body, name="conv_fwd", grid=(nb,),
        in_specs=[col(blk_b), col(blk_c), col(blk_x), pl.BlockSpec((CONV_K, COL_BLOCK), lambda j: (0, j))],
        out_specs=pl.BlockSpec((S, COL_BLOCK), lambda j: (0, j)),
        out_shape=jax.ShapeDtypeStruct((S, D), BF16),
        compiler_params=_cparams(("parallel",)),
    )(proj, proj, proj, w_conv)


def _merge_fwd(proj, ya, yb, blk_ga, blk_gb, ts):
    S, D = ya.shape
    nb = D // COL_BLOCK

    def body(ga_ref, gb_ref, ya_ref, yb_ref, o_ref):
        o_ref[...] = (_sigmoid(ga_ref[...]) * ya_ref[...] + _sigmoid(gb_ref[...]) * yb_ref[...]).astype(BF16)

    blk = lambda off: pl.BlockSpec((ts, COL_BLOCK), lambda i, j: (i, off + j))
    return pl.pallas_call(
        body, name="merge_fwd", grid=(S // ts, nb),
        in_specs=[blk(blk_ga), blk(blk_gb), blk(0), blk(0)],
        out_specs=blk(0),
        out_shape=jax.ShapeDtypeStruct((S, D), BF16),
        compiler_params=_cparams(("parallel", "parallel")),
    )(proj, proj, ya, yb)


def _ln1_fwd(x, mix, mod, g, b, ts):
    S, D = x.shape

    def body(x_ref, mix_ref, mod_ref, g_ref, b_ref, xhat_ref, rstd_ref, u2_ref):
        r = DEEPNORM_ALPHA * x_ref[...] + mod_ref[2:3, :] * mix_ref[...]
        mu = jnp.mean(r, axis=-1, keepdims=True)
        d = r - mu
        rstd = lax.rsqrt(jnp.mean(d * d, axis=-1, keepdims=True) + LN_EPS)
        xhat = d * rstd
        xhat_ref[...] = xhat
        rstd_ref[...] = rstd
        x1 = xhat * g_ref[...] + b_ref[...]
        u2_ref[...] = (x1 * (1.0 + mod_ref[4:5, :]) + mod_ref[3:4, :]).astype(BF16)

    row = pl.BlockSpec((ts, D), lambda i: (i, 0))
    vec = lambda r: pl.BlockSpec((r, D), lambda i: (0, 0))
    return pl.pallas_call(
        body, name="ln1_fwd", grid=(S // ts,),
        in_specs=[row, row, vec(6), vec(1), vec(1)],
        out_specs=[row, pl.BlockSpec((ts, 1), lambda i: (i, 0)), row],
        out_shape=[jax.ShapeDtypeStruct((S, D), F32), jax.ShapeDtypeStruct((S, 1), F32),
                   jax.ShapeDtypeStruct((S, D), BF16)],
        compiler_params=_cparams(("parallel",)),
    )(x, mix, mod, g, b)


def _swiglu_fwd(h, ts, tb):
    S, F2 = h.shape
    F = F2 // 2
    nb = F // tb

    def body(hg_ref, hu_ref, a_ref):
        hg = hg_ref[...]
        a_ref[...] = (hg * _sigmoid(hg) * hu_ref[...]).astype(BF16)

    return pl.pallas_call(
        body, name="swiglu_fwd", grid=(S // ts, nb),
        in_specs=[pl.BlockSpec((ts, tb), lambda i, j: (i, j)), pl.BlockSpec((ts, tb), lambda i, j: (i, j + nb))],
        out_specs=pl.BlockSpec((ts, tb), lambda i, j: (i, j)),
        out_shape=jax.ShapeDtypeStruct((S, F), BF16),
        compiler_params=_cparams(("parallel", "parallel")),
    )(h, h)


def _ln2_loss(xhat1, ffn, tgt, mod, g1, b1, g2, b2, ts):
    S, D = xhat1.shape

    def body(xh_ref, ffn_ref, t_ref, mod_ref, g1_ref, b1_ref, g2_ref, b2_ref, loss_ref, dffn_ref, dx1_ref, vec_ref):
        i = pl.program_id(0)

        @pl.when(i == 0)
        def _():
            loss_ref[...] = jnp.zeros_like(loss_ref)
            vec_ref[...] = jnp.zeros_like(vec_ref)

        x1 = xh_ref[...] * g1_ref[...] + b1_ref[...]
        ffn = ffn_ref[...]
        r = DEEPNORM_ALPHA * x1 + mod_ref[5:6, :] * ffn
        mu = jnp.mean(r, axis=-1, keepdims=True)
        d = r - mu
        rstd = lax.rsqrt(jnp.mean(d * d, axis=-1, keepdims=True) + LN_EPS)
        xhat = d * rstd
        e = xhat * g2_ref[...] + b2_ref[...] - t_ref[...]
        loss_ref[...] += 0.5 * jnp.sum(jnp.mean(e * e, axis=-1, keepdims=True))
        dy = e * (1.0 / D)
        dxhat = dy * g2_ref[...]
        dr = rstd * (dxhat - jnp.mean(dxhat, axis=-1, keepdims=True)
                     - xhat * jnp.mean(dxhat * xhat, axis=-1, keepdims=True))
        dffn_ref[...] = (dr * mod_ref[5:6, :]).astype(BF16)
        dx1_ref[...] = DEEPNORM_ALPHA * dr
        vec_ref[0:1, :] += jnp.sum(dy * xhat, axis=0, keepdims=True)
        vec_ref[1:2, :] += jnp.sum(dy, axis=0, keepdims=True)
        vec_ref[2:3, :] += jnp.sum(dr * ffn, axis=0, keepdims=True)

    row = pl.BlockSpec((ts, D), lambda i: (i, 0))
    vec = lambda r: pl.BlockSpec((r, D), lambda i: (0, 0))
    return pl.pallas_call(
        body, name="ln2_loss", grid=(S // ts,),
        in_specs=[row, row, row, vec(6), vec(1), vec(1), vec(1), vec(1)],
        out_specs=[pl.BlockSpec((1, LANE), lambda i: (0, 0)), row, row, vec(8)],
        out_shape=[jax.ShapeDtypeStruct((1, LANE), F32), jax.ShapeDtypeStruct((S, D), BF16),
                   jax.ShapeDtypeStruct((S, D), F32), jax.ShapeDtypeStruct((8, D), F32)],
        compiler_params=_cparams(("arbitrary",)),
    )(xhat1, ffn, tgt, mod, g1, b1, g2, b2)


def _swiglu_bwd(da, h, ts, tb):
    S, F2 = h.shape
    nb = (F2 // 2) // tb

    def body(da_ref, hg_ref, hu_ref, dg_ref, du_ref):
        hg, da = hg_ref[...], da_ref[...]
        sg = _sigmoid(hg)
        dg_ref[...] = (da * hu_ref[...] * (sg * (1.0 + hg * (1.0 - sg)))).astype(BF16)
        du_ref[...] = (da * hg * sg).astype(BF16)

    lo = pl.BlockSpec((ts, tb), lambda i, j: (i, j))
    hi = pl.BlockSpec((ts, tb), lambda i, j: (i, j + nb))
    dg, du = pl.pallas_call(
        body, name="swiglu_bwd", grid=(S // ts, nb),
        in_specs=[lo, lo, hi],
        out_specs=[lo, lo],
        out_shape=[jax.ShapeDtypeStruct((S, F2 // 2), BF16), jax.ShapeDtypeStruct((S, F2 // 2), BF16)],
        compiler_params=_cparams(("parallel", "parallel")),
    )(da, h, h)
    return dg, du


def _ln1_bwd(du2, dx1a, xhat1, rstd1, mix, mod, g1, b1, ts):
    S, D = xhat1.shape

    def body(du2_ref, dx1a_ref, xh_ref, rstd_ref, mix_ref, mod_ref, g_ref, b_ref, dxa_ref, dmix_ref, vec_ref):
        i = pl.program_id(0)

        @pl.when(i == 0)
        def _():
            vec_ref[...] = jnp.zeros_like(vec_ref)

        xhat, du2, mix = xh_ref[...], du2_ref[...], mix_ref[...]
        x1 = xhat * g_ref[...] + b_ref[...]
        dx1 = dx1a_ref[...] + du2 * (1.0 + mod_ref[4:5, :])
        dxhat = dx1 * g_ref[...]
        dr = rstd_ref[...] * (dxhat - jnp.mean(dxhat, axis=-1, keepdims=True)
                              - xhat * jnp.mean(dxhat * xhat, axis=-1, keepdims=True))
        dxa_ref[...] = DEEPNORM_ALPHA * dr
        dmix_ref[...] = (dr * mod_ref[2:3, :]).astype(BF16)
        vec_ref[0:1, :] += jnp.sum(du2, axis=0, keepdims=True)
        vec_ref[1:2, :] += jnp.sum(du2 * x1, axis=0, keepdims=True)
        vec_ref[2:3, :] += jnp.sum(dx1 * xhat, axis=0, keepdims=True)
        vec_ref[3:4, :] += jnp.sum(dx1, axis=0, keepdims=True)
        vec_ref[4:5, :] += jnp.sum(dr * mix, axis=0, keepdims=True)

    row = pl.BlockSpec((ts, D), lambda i: (i, 0))
    vec = lambda r: pl.BlockSpec((r, D), lambda i: (0, 0))
    return pl.pallas_call(
        body, name="ln1_bwd", grid=(S // ts,),
        in_specs=[row, row, row, pl.BlockSpec((ts, 1), lambda i: (i, 0)), row, vec(6), vec(1), vec(1)],
        out_specs=[row, row, vec(8)],
        out_shape=[jax.ShapeDtypeStruct((S, D), F32), jax.ShapeDtypeStruct((S, D), BF16),
                   jax.ShapeDtypeStruct((8, D), F32)],
        compiler_params=_cparams(("arbitrary",)),
    )(du2, dx1a, xhat1, rstd1, mix, mod, g1, b1)


def _merge_bwd(dmerged, proj, ya, yb, blk_ga, blk_gb, ts):
    S, D = ya.shape
    nb = D // COL_BLOCK

    def body(dm_ref, ga_ref, gb_ref, ya_ref, yb_ref, dya_ref, dyb_ref, dga_ref, dgb_ref):
        dm = dm_ref[...]
        sa, sb = _sigmoid(ga_ref[...]), _sigmoid(gb_ref[...])
        dya_ref[...] = (dm * sa).astype(BF16)
        dyb_ref[...] = (dm * sb).astype(BF16)
        dga_ref[...] = (dm * ya_ref[...] * sa * (1.0 - sa)).astype(BF16)
        dgb_ref[...] = (dm * yb_ref[...] * sb * (1.0 - sb)).astype(BF16)

    blk = lambda off: pl.BlockSpec((ts, COL_BLOCK), lambda i, j: (i, off + j))
    out = jax.ShapeDtypeStruct((S, D), BF16)
    return pl.pallas_call(
        body, name="merge_bwd", grid=(S // ts, nb),
        in_specs=[blk(0), blk(blk_ga), blk(blk_gb), blk(0), blk(0)],
        out_specs=[blk(0)] * 4,
        out_shape=[out] * 4,
        compiler_params=_cparams(("parallel", "parallel")),
    )(dmerged, proj, proj, ya, yb)


def _conv_bwd(dcbc, proj, w_conv, blk_b, blk_c, blk_x):
    S = proj.shape[0]
    D = w_conv.shape[1]
    nb = D // COL_BLOCK

    def body(d_ref, cb_ref, cc_ref, cx_ref, w_ref, dcb_ref, dcc_ref, dcx_ref, dw_ref):
        d, cc, cx = d_ref[...], cc_ref[...], cx_ref[...]
        z = cc * cx
        z1, z2 = _shift_rows(z, 1), _shift_rows(z, 2)
        conv = w_ref[2:3, :] * z + w_ref[1:2, :] * z1 + w_ref[0:1, :] * z2
        dcb_ref[...] = (d * conv).astype(BF16)
        dconv = d * cb_ref[...]
        dz = w_ref[2:3, :] * dconv + w_ref[1:2, :] * _shift_rows(dconv, -1) + w_ref[0:1, :] * _shift_rows(dconv, -2)
        dcc_ref[...] = (dz * cx).astype(BF16)
        dcx_ref[...] = (dz * cc).astype(BF16)
        dw_ref[...] = jnp.zeros_like(dw_ref)
        dw_ref[0:1, :] = jnp.sum(dconv * z2, axis=0, keepdims=True)
        dw_ref[1:2, :] = jnp.sum(dconv * z1, axis=0, keepdims=True)
        dw_ref[2:3, :] = jnp.sum(dconv * z, axis=0, keepdims=True)

    col = lambda off: pl.BlockSpec((S, COL_BLOCK), lambda j: (0, off + j))
    out = jax.ShapeDtypeStruct((S, D), BF16)
    return pl.pallas_call(
        body, name="conv_bwd", grid=(nb,),
        in_specs=[col(0), col(blk_b), col(blk_c), col(blk_x), pl.BlockSpec((CONV_K, COL_BLOCK), lambda j: (0, j))],
        out_specs=[col(0), col(0), col(0), pl.BlockSpec((8, COL_BLOCK), lambda j: (0, j))],
        out_shape=[out, out, out, jax.ShapeDtypeStruct((8, D), F32)],
        compiler_params=_cparams(("parallel",)),
    )(dcbc, proj, proj, proj, w_conv)


def _attn_bwd_dq(qc, kc, vh, do, o, lse, T):
    H, S, _ = qc.shape
    n = S // T

    def body(q_ref, k_ref, v_ref, do_ref, o_ref, lse_ref, dq_ref, dsum_ref, acc_ref, d_ref):
        i, j = pl.program_id(1), pl.program_id(2)

        @pl.when(j == 0)
        def _():
            acc_ref[...] = jnp.zeros_like(acc_ref)
            d_ref[...] = jnp.sum(do_ref[...] * o_ref[...], axis=-1, keepdims=True)

        @pl.when(j <= i)
        def _():
            s = lax.dot_general(q_ref[0], k_ref[0], NT_DIMS, preferred_element_type=F32) * ATTN_SCALE
            s = jnp.where(_chunk_mask(i, j, T), s, NEG_INF)
            p = jnp.exp(s - lse_ref[0])
            dp = lax.dot_general(do_ref[...].astype(BF16), v_ref[0], NT_DIMS, preferred_element_type=F32)
            ds = (p * (dp - d_ref[...]) * ATTN_SCALE).astype(BF16)
            acc_ref[...] += jnp.dot(ds, k_ref[0], preferred_element_type=F32)

        @pl.when(j == i)
        def _():
            dq_ref[0] = acc_ref[...]
            dsum_ref[0] = d_ref[...]

    qspec = pl.BlockSpec((1, T, QK_CAT), lambda h, i, j: (h, i, 0))
    ospec = pl.BlockSpec((T, V_HEAD), lambda h, i, j: (i, h))
    stat = pl.BlockSpec((1, T, 1), lambda h, i, j: (h, i, 0))
    return pl.pallas_call(
        body, name="attn_bwd_dq", grid=(H, n, n),
        in_specs=[qspec,
                  pl.BlockSpec((1, T, QK_CAT), lambda h, i, j: (h, jnp.minimum(j, i), 0)),
                  pl.BlockSpec((1, T, V_HEAD), lambda h, i, j: (h, jnp.minimum(j, i), 0)),
                  ospec, ospec, stat],
        out_specs=[qspec, stat],
        out_shape=[jax.ShapeDtypeStruct((H, S, QK_CAT), F32), jax.ShapeDtypeStruct((H, S, 1), F32)],
        scratch_shapes=[pltpu.VMEM((T, QK_CAT), F32), pltpu.VMEM((T, 1), F32)],
        compiler_params=_cparams(("parallel", "parallel", "arbitrary")),
    )(qc, kc, vh, do, o, lse)


def _attn_bwd_dkv(qc, kc, vh, do, lse, dsum, T):
    H, S, _ = qc.shape
    n = S // T

    def body(q_ref, k_ref, v_ref, do_ref, lse_ref, d_ref, dk_ref, dv_ref, dk_acc, dv_acc):
        j, i = pl.program_id(1), pl.program_id(2)

        @pl.when(i == 0)
        def _():
            dk_acc[...] = jnp.zeros_like(dk_acc)
            dv_acc[...] = jnp.zeros_like(dv_acc)

        @pl.when(i >= j)
        def _():
            q = q_ref[0]
            do = do_ref[...].astype(BF16)
            s = lax.dot_general(q, k_ref[0], NT_DIMS, preferred_element_type=F32) * ATTN_SCALE
            s = jnp.where(_chunk_mask(i, j, T), s, NEG_INF)
            p = jnp.exp(s - lse_ref[0])
            dv_acc[...] += lax.dot_general(p.astype(BF16), do, TN_DIMS, preferred_element_type=F32)
            dp = lax.dot_general(do, v_ref[0], NT_DIMS, preferred_element_type=F32)
            ds = (p * (dp - d_ref[0]) * ATTN_SCALE).astype(BF16)
            dk_acc[...] += lax.dot_general(ds, q, TN_DIMS, preferred_element_type=F32)

        @pl.when(i == n - 1)
        def _():
            dk_ref[0] = dk_acc[...]
            dv_ref[0] = dv_acc[...]

    kspec = pl.BlockSpec((1, T, QK_CAT), lambda h, j, i: (h, j, 0))
    vspec = pl.BlockSpec((1, T, V_HEAD), lambda h, j, i: (h, j, 0))
    stat = pl.BlockSpec((1, T, 1), lambda h, j, i: (h, jnp.maximum(i, j), 0))
    return pl.pallas_call(
        body, name="attn_bwd_dkv", grid=(H, n, n),
        in_specs=[pl.BlockSpec((1, T, QK_CAT), lambda h, j, i: (h, jnp.maximum(i, j), 0)),
                  kspec, vspec,
                  pl.BlockSpec((T, V_HEAD), lambda h, j, i: (jnp.maximum(i, j), h)),
                  stat, stat],
        out_specs=[kspec, vspec],
        out_shape=[jax.ShapeDtypeStruct((H, S, QK_CAT), F32), jax.ShapeDtypeStruct((H, S, V_HEAD), F32)],
        scratch_shapes=[pltpu.VMEM((T, QK_CAT), F32), pltpu.VMEM((T, V_HEAD), F32)],
        compiler_params=_cparams(("parallel", "parallel", "arbitrary")),
    )(qc, kc, vh, do, lse, dsum)


def _qk_bwd(dqc, dkc, dvh, cos_q, sin_q, cos_k, sin_k, ts):
    H, S, _ = dqc.shape
    nope_w, rope_w = H * QK_NOPE, H * QK_ROPE

    def body(dqc_ref, dkc_ref, dvh_ref, cq_ref, sq_ref, ck_ref, sk_ref, dq_ref, dkv_ref, dkr_ref, qr_buf, kr_buf):
        kr_sum = jnp.zeros((ts, QK_ROPE), F32)
        for h in range(H):
            dq_ref[:, h * QK_NOPE:(h + 1) * QK_NOPE] = dqc_ref[h, :, 0:QK_NOPE].astype(BF16)
            qr_buf[:, h * QK_ROPE:(h + 1) * QK_ROPE] = dqc_ref[h, :, QK_NOPE:QK_CAT]
            dkv_ref[:, h * QK_NOPE:(h + 1) * QK_NOPE] = dkc_ref[h, :, 0:QK_NOPE].astype(BF16)
            dkv_ref[:, nope_w + h * V_HEAD:nope_w + (h + 1) * V_HEAD] = dvh_ref[h].astype(BF16)
            kr_sum = kr_sum + dkc_ref[h, :, QK_NOPE:QK_CAT]
        qr = qr_buf[...]
        dq_ref[:, nope_w:] = (qr * cq_ref[...] + _rotate_half_pairs(qr, -1.0) * sq_ref[...]).astype(BF16)
        kr_buf[...] = jnp.zeros_like(kr_buf)
        kr_buf[:, 0:QK_ROPE] = kr_sum
        kr = kr_buf[...]
        dkr_ref[...] = (kr * ck_ref[...] + _rotate_half_pairs(kr, -1.0) * sk_ref[...]).astype(BF16)

    row = lambda w: pl.BlockSpec((ts, w), lambda i: (i, 0))
    head = lambda w: pl.BlockSpec((H, ts, w), lambda i: (0, i, 0))
    return pl.pallas_call(
        body, name="qk_bwd", grid=(S // ts,),
        in_specs=[head(QK_CAT), head(QK_CAT), head(V_HEAD), row(rope_w), row(rope_w), row(COL_BLOCK), row(COL_BLOCK)],
        out_specs=[row(nope_w + rope_w), row(nope_w + H * V_HEAD), row(COL_BLOCK)],
        out_shape=[jax.ShapeDtypeStruct((S, nope_w + rope_w), BF16), jax.ShapeDtypeStruct((S, nope_w + H * V_HEAD), BF16),
                   jax.ShapeDtypeStruct((S, COL_BLOCK), BF16)],
        scratch_shapes=[pltpu.VMEM((ts, rope_w), F32), pltpu.VMEM((ts, COL_BLOCK), F32)],
        compiler_params=_cparams(("parallel",)),
    )(dqc, dkc, dvh, cos_q, sin_q, cos_k, sin_k)


def _rms_bwd(dy, proj, g, blk, L, ts, name):
    S = proj.shape[0]

    def body(dy_ref, a_ref, g_ref, da_ref, dg_ref):
        i = pl.program_id(0)

        @pl.when(i == 0)
        def _():
            dg_ref[...] = jnp.zeros_like(dg_ref)

        a, dy = a_ref[...], dy_ref[...]
        r = lax.rsqrt(jnp.mean(a * a, axis=-1, keepdims=True) + RMS_EPS)
        dyh = dy * g_ref[...]
        da = r * dyh - a * (r * r * r) * jnp.mean(dyh * a, axis=-1, keepdims=True)
        da_ref[...] = da.astype(BF16)
        dg_ref[0:1, :] += jnp.sum(dy * a * r, axis=0, keepdims=True)

    return pl.pallas_call(
        body, name=name, grid=(S // ts,),
        in_specs=[pl.BlockSpec((ts, L), lambda i: (i, 0)), pl.BlockSpec((ts, L), lambda i: (i, blk)),
                  pl.BlockSpec((1, L), lambda i: (0, 0))],
        out_specs=[pl.BlockSpec((ts, L), lambda i: (i, 0)), pl.BlockSpec((8, L), lambda i: (0, 0))],
        out_shape=[jax.ShapeDtypeStruct((S, L), BF16), jax.ShapeDtypeStruct((8, L), F32)],
        compiler_params=_cparams(("arbitrary",)),
    )(dy, proj, g)


def _grad_x(du, dxa, x, mod, ts):
    S, D = x.shape

    def body(du_ref, dxa_ref, x_ref, mod_ref, dx_ref, vec_ref):
        i = pl.program_id(0)

        @pl.when(i == 0)
        def _():
            vec_ref[...] = jnp.zeros_like(vec_ref)

        du = du_ref[...]
        dx_ref[...] = dxa_ref[...] + du * (1.0 + mod_ref[1:2, :])
        vec_ref[0:1, :] += jnp.sum(du, axis=0, keepdims=True)
        vec_ref[1:2, :] += jnp.sum(du * x_ref[...], axis=0, keepdims=True)

    row = pl.BlockSpec((ts, D), lambda i: (i, 0))
    vec = lambda r: pl.BlockSpec((r, D), lambda i: (0, 0))
    return pl.pallas_call(
        body, name="grad_x", grid=(S // ts,),
        in_specs=[row, row, row, vec(6)],
        out_specs=[row, vec(8)],
        out_shape=[jax.ShapeDtypeStruct((S, D), F32), jax.ShapeDtypeStruct((8, D), F32)],
        compiler_params=_cparams(("arbitrary",)),
    )(du, dxa, x, mod)


def _adamw(w, g, m, v, name):
    R, C = w.shape
    tr = _tile(R, max(8, (1 << 19) // C), 8)
    c1 = 1.0 / (1.0 - ADAM_B1 ** ADAM_STEP)
    c2 = 1.0 / (1.0 - ADAM_B2 ** ADAM_STEP)

    def body(w_ref, g_ref, m_ref, v_ref, d_ref, nm_ref, nv_ref):
        g = g_ref[...]
        m = ADAM_B1 * m_ref[...] + (1.0 - ADAM_B1) * g
        v = ADAM_B2 * v_ref[...] + (1.0 - ADAM_B2) * (g * g)
        nm_ref[...] = m
        nv_ref[...] = v
        d_ref[...] = -ADAM_LR * ((m * c1) / (jnp.sqrt(v * c2) + ADAM_EPS) + ADAM_WD * w_ref[...])

    spec = pl.BlockSpec((tr, C), lambda i: (i, 0))
    out = jax.ShapeDtypeStruct((R, C), F32)
    return pl.pallas_call(
        body, name=name, grid=(R // tr,),
        in_specs=[spec] * 4, out_specs=[spec] * 3, out_shape=[out] * 3,
        compiler_params=_cparams(("parallel",)),
    )(w, g, m, v)


def _my_place():
    return lax.axis_index("x"), lax.axis_index("y"), lax.axis_index("c")


def _peer(k):
    x, y, c = _my_place()
    return (x ^ ((k >> 2) & 1), y ^ ((k >> 1) & 1), c ^ (k & 1))


def _linear(place):
    return 4 * place[0] + 2 * place[1] + place[2]


def _ada_fwd(c_row, wconv_row, w_ada, b_row):
    D, CW = w_ada.shape
    WC = wconv_row.shape[-1]

    def body(c_ref, wc_ref, w_ref, b_ref, mod_ref, cact_ref, wcall_ref, send_buf, sems):
        me = _linear(_my_place())
        c = c_ref[0]
        cact_ref[me] = c * _sigmoid(c)
        wcall_ref[me] = wc_ref[0]

        def gather_copy(buf, k, grp):
            return pltpu.make_async_remote_copy(
                src_ref=buf.at[me], dst_ref=buf.at[me], send_sem=sems.at[0, grp, k], recv_sem=sems.at[1, grp, k],
                device_id=_peer(k), device_id_type=MESH_ID)

        def gather_recv(buf, k, grp):
            src = _linear(_peer(k))
            return pltpu.make_async_remote_copy(
                src_ref=buf.at[src], dst_ref=buf.at[src], send_sem=sems.at[0, grp, k], recv_sem=sems.at[1, grp, k],
                device_id=_peer(k), device_id_type=MESH_ID)

        for k in range(1, N_DEV):
            gather_copy(cact_ref, k, 0).start()
            gather_copy(wcall_ref, k, 1).start()
        for k in range(1, N_DEV):
            gather_recv(cact_ref, k, 0).wait_recv()
            gather_recv(wcall_ref, k, 1).wait_recv()
        for k in range(1, N_DEV):
            gather_copy(cact_ref, k, 0).wait_send()
            gather_copy(wcall_ref, k, 1).wait_send()

        cact = jnp.concatenate([cact_ref[b] for b in range(N_DEV)], axis=0)
        mod_all = jnp.dot(cact.astype(BF16), w_ref[...].astype(BF16), preferred_element_type=F32) + b_ref[0]
        for b in range(N_DEV):
            send_buf[b] = mod_all[b:b + 1, :]
        mod_ref[me] = send_buf[me]

        def scatter_copy(k):
            dst = _linear(_peer(k))
            return pltpu.make_async_remote_copy(
                src_ref=send_buf.at[dst], dst_ref=mod_ref.at[me], send_sem=sems.at[0, 2, k], recv_sem=sems.at[1, 2, k],
                device_id=_peer(k), device_id_type=MESH_ID)

        def scatter_recv(k):
            src = _linear(_peer(k))
            return pltpu.make_async_remote_copy(
                src_ref=send_buf.at[src], dst_ref=mod_ref.at[src], send_sem=sems.at[0, 2, k], recv_sem=sems.at[1, 2, k],
                device_id=_peer(k), device_id_type=MESH_ID)

        for k in range(1, N_DEV):
            scatter_copy(k).start()
        for k in range(1, N_DEV):
            scatter_recv(k).wait_recv()
        for k in range(1, N_DEV):
            scatter_copy(k).wait_send()

    vmem = pl.BlockSpec(memory_space=pltpu.VMEM)
    return pl.pallas_call(
        body, name="ada_fwd",
        in_specs=[vmem] * 4, out_specs=[vmem] * 3,
        out_shape=[jax.ShapeDtypeStruct((N_DEV, 1, CW), F32), jax.ShapeDtypeStruct((N_DEV, 1, D), F32),
                   jax.ShapeDtypeStruct((N_DEV, 1, WC), F32)],
        scratch_shapes=[pltpu.VMEM((N_DEV, 1, CW), F32), pltpu.SemaphoreType.DMA((2, 3, N_DEV))],
        compiler_params=pltpu.CompilerParams(vmem_limit_bytes=VMEM_LIMIT),
    )(c_row, wconv_row, w_ada, b_row)


def _ada_bwd(payload, cact_t, n_mod):
    NCH, _, CW = payload.shape
    D = cact_t.shape[0]

    def body(p_ref, ct_ref, sum_ref, gw_ref, all_ref, sems):
        me = _linear(_my_place())
        all_ref[me] = p_ref[...]

        def copy(k, slot):
            return pltpu.make_async_remote_copy(
                src_ref=all_ref.at[slot], dst_ref=all_ref.at[slot], send_sem=sems.at[0, k], recv_sem=sems.at[1, k],
                device_id=_peer(k), device_id_type=MESH_ID)

        for k in range(1, N_DEV):
            copy(k, me).start()
        for k in range(1, N_DEV):
            copy(k, _linear(_peer(k))).wait_recv()
        for k in range(1, N_DEV):
            copy(k, me).wait_send()

        total = all_ref[0]
        for b in range(1, N_DEV):
            total = total + all_ref[b]
        sum_ref[...] = total

        ct = ct_ref[...].astype(BF16).astype(F32)
        gw = jnp.zeros((D, CW), F32)
        for b in range(N_DEV):
            dm = all_ref[b, me].astype(BF16).astype(F32)
            gw = gw + ct[:, b:b + 1] * dm
        gw_ref[...] = gw

    vmem = pl.BlockSpec(memory_space=pltpu.VMEM)
    return pl.pallas_call(
        body, name="ada_bwd",
        in_specs=[vmem, vmem], out_specs=[vmem, vmem],
        out_shape=[jax.ShapeDtypeStruct((NCH, 1, CW), F32), jax.ShapeDtypeStruct((D, CW), F32)],
        scratch_shapes=[pltpu.VMEM((N_DEV, NCH, 1, CW), F32), pltpu.SemaphoreType.DMA((2, N_DEV))],
        compiler_params=pltpu.CompilerParams(vmem_limit_bytes=VMEM_LIMIT),
    )(payload, cact_t)


def _all_gather(shard):
    R, C = shard.shape

    def body(x_ref, out_ref, send_sems, recv_sems, local_sem):
        x, y, c = _my_place()
        me, sibling = (x, y, c), (x, y, 1 - c)
        chips = [(1 - x, y), (x, 1 - y), (1 - x, 1 - y)]

        def copy(k, block, to, src=None):
            slot = out_ref.at[_linear(block)]
            return pltpu.make_async_remote_copy(
                src_ref=slot if src is None else src, dst_ref=slot,
                send_sem=send_sems.at[k], recv_sem=recv_sems.at[k], device_id=to, device_id_type=MESH_ID)

        mine = pltpu.make_async_copy(x_ref, out_ref.at[_linear(me)], local_sem)
        mine.start()
        first = [copy(0, me, sibling, src=x_ref)]
        first += [copy(1 + j, me, (*chip, c), src=x_ref) for j, chip in enumerate(chips)]
        for cp in first:
            cp.start()
        passed = [copy(4 + j, (*chip, c), sibling) for j, chip in enumerate(chips)]
        for j, chip in enumerate(chips):
            copy(1 + j, (*chip, c), me).wait_recv()
            passed[j].start()
        copy(0, sibling, me).wait_recv()
        for j, chip in enumerate(chips):
            copy(4 + j, (*chip, 1 - c), me).wait_recv()
        for cp in first + passed:
            cp.wait_send()
        mine.wait()

    hbm = pl.BlockSpec(memory_space=pltpu.HBM)
    return pl.pallas_call(
        body, name="weight_all_gather",
        in_specs=[hbm], out_specs=hbm,
        out_shape=jax.ShapeDtypeStruct((N_DEV, R, C), shard.dtype),
        scratch_shapes=[pltpu.SemaphoreType.DMA((7,)), pltpu.SemaphoreType.DMA((7,)), pltpu.SemaphoreType.DMA],
    )(shard)


def _exchange_in_chip(parts):
    _, R, C = parts.shape

    def body(p_ref, got_ref, send_sems, recv_sems):
        x, y, c = _my_place()
        sibling = (x, y, 1 - c)
        copies = []
        for q in range(4):
            copies.append(pltpu.make_async_remote_copy(
                src_ref=p_ref.at[2 * q + (1 - c)], dst_ref=got_ref.at[q],
                send_sem=send_sems.at[q], recv_sem=recv_sems.at[q], device_id=sibling, device_id_type=MESH_ID))
        for cp in copies:
            cp.start()
        for cp in copies:
            cp.wait_recv()
        for cp in copies:
            cp.wait_send()

    hbm = pl.BlockSpec(memory_space=pltpu.HBM)
    return pl.pallas_call(
        body, name="grad_exchange_in_chip",
        in_specs=[hbm], out_specs=hbm,
        out_shape=jax.ShapeDtypeStruct((4, R, C), parts.dtype),
        scratch_shapes=[pltpu.SemaphoreType.DMA((4,)), pltpu.SemaphoreType.DMA((4,))],
    )(parts)


def _pair_sum(parts, got, core):
    _, R, C = parts.shape
    tr = _tile(R, 1024, PACK_ROW_ALIGN)

    def body(c_ref, p_ref, g_ref, o_ref):
        o_ref[...] = (p_ref[...].astype(F32) + g_ref[...].astype(F32)).astype(o_ref.dtype)

    return pl.pallas_call(
        body, name="grad_pair_sum",
        grid_spec=pltpu.PrefetchScalarGridSpec(
            num_scalar_prefetch=1, grid=(4, R // tr),
            in_specs=[pl.BlockSpec((1, tr, C), lambda q, i, c_ref: (2 * q + c_ref[0], i, 0)),
                      pl.BlockSpec((1, tr, C), lambda q, i, c_ref: (q, i, 0))],
            out_specs=pl.BlockSpec((1, tr, C), lambda q, i, c_ref: (q, i, 0))),
        out_shape=jax.ShapeDtypeStruct((4, R, C), parts.dtype),
        compiler_params=_cparams(("parallel", "parallel")),
    )(core, parts, got)


def _exchange_across_chips(chip_parts):
    _, R, C = chip_parts.shape

    def body(p_ref, got_ref, send_sems, recv_sems, local_sem):
        x, y, c = _my_place()
        my_chip = 2 * x + y
        mine = pltpu.make_async_copy(p_ref.at[my_chip], got_ref.at[my_chip], local_sem)
        mine.start()
        copies, arrivals = [], []
        for k in range(1, 4):
            px, py = x ^ (k >> 1), y ^ (k & 1)
            copies.append(pltpu.make_async_remote_copy(
                src_ref=p_ref.at[2 * px + py], dst_ref=got_ref.at[my_chip],
                send_sem=send_sems.at[k - 1], recv_sem=recv_sems.at[k - 1],
                device_id=(px, py, c), device_id_type=MESH_ID))
            arrivals.append(pltpu.make_async_remote_copy(
                src_ref=p_ref.at[2 * px + py], dst_ref=got_ref.at[2 * px + py],
                send_sem=send_sems.at[k - 1], recv_sem=recv_sems.at[k - 1],
                device_id=(px, py, c), device_id_type=MESH_ID))
        for cp in copies:
            cp.start()
        for cp in arrivals:
            cp.wait_recv()
        for cp in copies:
            cp.wait_send()
        mine.wait()

    hbm = pl.BlockSpec(memory_space=pltpu.HBM)
    return pl.pallas_call(
        body, name="grad_exchange_across_chips",
        in_specs=[hbm], out_specs=hbm,
        out_shape=jax.ShapeDtypeStruct((4, R, C), chip_parts.dtype),
        scratch_shapes=[pltpu.SemaphoreType.DMA((3,)), pltpu.SemaphoreType.DMA((3,)), pltpu.SemaphoreType.DMA],
    )(chip_parts)


def _chip_sum(got):
    _, R, C = got.shape
    tr = _tile(R, 1024, PACK_ROW_ALIGN)

    def body(g_ref, o_ref):
        acc = g_ref[0].astype(F32)
        for q in range(1, 4):
            acc = acc + g_ref[q].astype(F32)
        o_ref[...] = acc

    return pl.pallas_call(
        body, name="grad_chip_sum", grid=(R // tr,),
        in_specs=[pl.BlockSpec((4, tr, C), lambda i: (0, i, 0))],
        out_specs=pl.BlockSpec((tr, C), lambda i: (i, 0)),
        out_shape=jax.ShapeDtypeStruct((R, C), F32),
        compiler_params=_cparams(("parallel",)),
    )(got)


def _pack_rows(n_elems):
    return _round_up(_round_up(n_elems, PACK_COLS) // PACK_COLS, PACK_ROW_ALIGN)


def _pack(flat_list, dtype):
    out = []
    for a in flat_list:
        n = a.shape[-1]
        rows = _pack_rows(n)
        pad = rows * PACK_COLS - n
        a = a.astype(dtype)
        if pad:
            a = jnp.pad(a, [(0, 0)] * (a.ndim - 1) + [(0, pad)])
        out.append(a.reshape(a.shape[:-1] + (rows, PACK_COLS)))
    return jnp.concatenate(out, axis=-2)


def _unpack(buf, sizes):
    out, r0 = [], 0
    for n in sizes:
        rows = _pack_rows(n)
        a = buf[..., r0:r0 + rows, :]
        out.append(a.reshape(a.shape[:-2] + (rows * PACK_COLS,))[..., :n])
        r0 += rows
    return out


def _cols_to_shards(w):
    K, N = w.shape
    return w.reshape(K, N_DEV, N // N_DEV).transpose(1, 0, 2).reshape(N_DEV, -1)


def _shards_to_cols(s, K):
    return s.reshape(N_DEV, K, -1).transpose(1, 0, 2).reshape(K, -1)


def kernel(x, c, positions, w_ada, b_ada, w_in, g_q_a, w_q_b, g_kv_a, w_kv_b, w_o_a, w_conv, w_o_b, w_o, ln1_g, ln1_b, w_ffn_in, w_ffn_out, ln2_g, ln2_b, loss_target, m_w_ada, m_b_ada, m_w_in, m_g_q_a, m_w_q_b, m_g_kv_a, m_w_kv_b, m_w_o_a, m_w_conv, m_w_o_b, m_w_o, m_ln1_g, m_ln1_b, m_w_ffn_in, m_w_ffn_out, m_ln2_g, m_ln2_b, v_w_ada, v_b_ada, v_w_in, v_g_q_a, v_w_q_b, v_g_kv_a, v_w_kv_b, v_w_o_a, v_w_conv, v_w_o_b, v_w_o, v_ln1_g, v_ln1_b, v_w_ffn_in, v_w_ffn_out, v_ln2_g, v_ln2_b):
    x2, tgt = x[0], loss_target[0]
    S, D = x2.shape
    Lq, Lkv = g_q_a.shape[1], g_kv_a.shape[1]
    H = w_q_b.shape[2] * N_DEV // QK_CAT
    F = w_ffn_out.shape[1] * N_DEV
    n_in = w_in.shape[2] * N_DEV
    assert Lq == Lkv and (Lq + Lkv) % COL_BLOCK == 0 and D % COL_BLOCK == 0
    front = Lq + Lkv + QK_ROPE
    front_pad = _round_up(front, COL_BLOCK)
    kr_blk = (Lq + Lkv) // COL_BLOCK
    blk_b = front_pad // COL_BLOCK
    nblk = D // COL_BLOCK
    blk_c, blk_x, blk_ga, blk_gb = blk_b + nblk, blk_b + 2 * nblk, blk_b + 3 * nblk, blk_b + 4 * nblk
    ts = _tile(S, 256, 8)
    T = _tile(S, 256, CHUNK)
    tb = _tile(F, 512)
    me = _linear(_my_place())

    cw = w_ada.shape[2]
    b_mine = lax.dynamic_slice(b_ada, (0, me * cw), (1, cw)).reshape(1, 1, cw)
    mod_blocks, cact_all, wconv_all = _ada_fwd(c.reshape(1, 1, D), w_conv[0].reshape(1, 1, -1), w_ada[0], b_mine)
    mod = mod_blocks.reshape(6, D)
    cact_all = cact_all.reshape(N_DEV, D)
    w_conv_full = _shards_to_cols(wconv_all.reshape(N_DEV, -1), CONV_K)

    big = [w_in, w_q_b, w_kv_b, w_o_a, w_o_b, w_o, w_ffn_in, w_ffn_out]
    sizes = [w.shape[1] * w.shape[2] for w in big]
    gathered = _all_gather(_pack([w[0].reshape(-1) for w in big], BF16))
    g_in, g_qb, g_kvb, g_oa, g_ob, g_o, g_fi, g_fo = _unpack(gathered, sizes)
    w_in_f = _shards_to_cols(g_in, D)
    w_in_p = jnp.concatenate([w_in_f[:, :front], jnp.zeros((D, front_pad - front), BF16), w_in_f[:, front:]], axis=1)
    wq = _shards_to_cols(g_qb, Lq).reshape(Lq, H, QK_CAT)
    wq_p = jnp.concatenate([wq[:, :, :QK_NOPE].reshape(Lq, -1), wq[:, :, QK_NOPE:].reshape(Lq, -1)], axis=1)
    wkv = _shards_to_cols(g_kvb, Lkv).reshape(Lkv, H, QK_NOPE + V_HEAD)
    wkv_p = jnp.concatenate([wkv[:, :, :QK_NOPE].reshape(Lkv, -1), wkv[:, :, QK_NOPE:].reshape(Lkv, -1)], axis=1)
    w_oa_f, w_ob_f, w_o_f = g_oa.reshape(-1, D), g_ob.reshape(-1, D), g_o.reshape(-1, D)
    w_fi_f = _shards_to_cols(g_fi, D)
    w_fo_f = g_fo.reshape(F, D)

    inv_freq = 1.0 / (ROPE_THETA ** (jnp.arange(0, QK_ROPE, 2, dtype=F32) / QK_ROPE))
    ang = positions[0].astype(F32)[:, None] * inv_freq
    cos2 = jnp.concatenate([jnp.cos(ang), jnp.cos(ang)], axis=-1)
    sin2 = jnp.concatenate([jnp.sin(ang), jnp.sin(ang)], axis=-1)
    cos_q, sin_q = jnp.tile(cos2, (1, H)), jnp.tile(sin2, (1, H))
    cos_k, sin_k = jnp.tile(cos2, (1, COL_BLOCK // QK_ROPE)), jnp.tile(sin2, (1, COL_BLOCK // QK_ROPE))

    u = _modulate_in(x2, mod, ts)
    proj = _matmul(u, w_in_p, "nn", F32, "proj")
    qn = _rms_fwd(proj, g_q_a, 0, Lq, ts, "rms_q")
    kvn = _rms_fwd(proj, g_kv_a, 1, Lkv, ts, "rms_kv")
    q = _matmul(qn, wq_p, "nn", F32, "q_up")
    kv = _matmul(kvn, wkv_p, "nn", F32, "kv_up")
    qc, kc, vh = _qk_prep(q, kv, proj, kr_blk, cos_q, sin_q, cos_k, sin_k, H, ts)
    attn, lse = _attn_fwd(qc, kc, vh, T)
    ya = _matmul(attn, w_oa_f, "nn", F32, "attn_out")
    cbc = _conv_fwd(proj, w_conv_full, blk_b, blk_c, blk_x)
    yb = _matmul(cbc, w_ob_f, "nn", F32, "conv_out")
    merged = _merge_fwd(proj, ya, yb, blk_ga, blk_gb, ts)
    mix = _matmul(merged, w_o_f, "nn", F32, "mix_out")
    xhat1, rstd1, u2 = _ln1_fwd(x2, mix, mod, ln1_g, ln1_b, ts)
    hh = _matmul(u2, w_fi_f, "nn", F32, "ffn_in")
    act = _swiglu_fwd(hh, ts, tb)
    ffn = _matmul(act, w_fo_f, "nn", F32, "ffn_out")
    loss_part, dffn, dx1a, vec2 = _ln2_loss(xhat1, ffn, tgt, mod, ln1_g, ln1_b, ln2_g, ln2_b, ts)
    loss = lax.psum(loss_part[0, 0], AXES)

    gw_fo = _matmul(act, dffn, "tn", BF16, "grad_w_ffn_out")
    da = _matmul(dffn, w_fo_f, "nt", F32, "d_act")
    dhg, dhu = _swiglu_bwd(da, hh, ts, tb)
    dh = jnp.concatenate([dhg, dhu], axis=1)
    gw_fi = _matmul(u2, dh, "tn", BF16, "grad_w_ffn_in")
    du2 = _matmul(dh, w_fi_f, "nt", F32, "d_u2")
    dxa, dmix, vec1 = _ln1_bwd(du2, dx1a, xhat1, rstd1, mix, mod, ln1_g, ln1_b, ts)
    gw_o = _matmul(merged, dmix, "tn", BF16, "grad_w_o")
    dmerged = _matmul(dmix, w_o_f, "nt", F32, "d_merged")
    dya, dyb, dga, dgb = _merge_bwd(dmerged, proj, ya, yb, blk_ga, blk_gb, ts)
    gw_ob = _matmul(cbc, dyb, "tn", BF16, "grad_w_o_b")
    dcbc = _matmul(dyb, w_ob_f, "nt", F32, "d_conv")
    dcb, dcc, dcx, dwconv = _conv_bwd(dcbc, proj, w_conv_full, blk_b, blk_c, blk_x)
    gw_oa = _matmul(attn, dya, "tn", BF16, "grad_w_o_a")
    dattn = _matmul(dya, w_oa_f, "nt", F32, "d_attn")
    dqc, dsum = _attn_bwd_dq(qc, kc, vh, dattn, attn, lse, T)
    dkc, dvh = _attn_bwd_dkv(qc, kc, vh, dattn, lse, dsum, T)
    dq, dkv, dkr = _qk_bwd(dqc, dkc, dvh, cos_q, sin_q, cos_k, sin_k, ts)
    gw_qb_p = _matmul(qn, dq, "tn", BF16, "grad_w_q_b")
    dqn = _matmul(dq, wq_p, "nt", F32, "d_qn")
    gw_kvb_p = _matmul(kvn, dkv, "tn", BF16, "grad_w_kv_b")
    dkvn = _matmul(dkv, wkv_p, "nt", F32, "d_kvn")
    dqa, dgq = _rms_bwd(dqn, proj, g_q_a, 0, Lq, ts, "rms_q_bwd")
    dkva, dgkv = _rms_bwd(dkvn, proj, g_kv_a, 1, Lkv, ts, "rms_kv_bwd")
    dproj = jnp.concatenate([dqa, dkva, dkr, dcb, dcc, dcx, dga, dgb], axis=1)
    gw_in_p = _matmul(u, dproj, "tn", BF16, "grad_w_in")
    du = _matmul(dproj, w_in_p, "nt", F32, "d_u")
    grad_x, vec0 = _grad_x(du, dxa, x2, mod, ts)

    nq, nk = H * QK_NOPE, H * QK_NOPE
    gw_in = jnp.concatenate([gw_in_p[:, :front], gw_in_p[:, front_pad:]], axis=1)
    gw_qb = jnp.concatenate([gw_qb_p[:, :nq].reshape(Lq, H, QK_NOPE), gw_qb_p[:, nq:].reshape(Lq, H, QK_ROPE)],
                            axis=2).reshape(Lq, -1)
    gw_kvb = jnp.concatenate([gw_kvb_p[:, :nk].reshape(Lkv, H, QK_NOPE), gw_kvb_p[:, nk:].reshape(Lkv, H, V_HEAD)],
                             axis=2).reshape(Lkv, -1)
    parts = _pack([_cols_to_shards(gw_in), _cols_to_shards(gw_qb), _cols_to_shards(gw_kvb),
                   gw_oa.reshape(N_DEV, -1), gw_ob.reshape(N_DEV, -1), gw_o.reshape(N_DEV, -1),
                   _cols_to_shards(gw_fi), gw_fo.reshape(N_DEV, -1)], BF16)
    got = _exchange_in_chip(parts)
    chip_parts = _pair_sum(parts, got, lax.axis_index("c").astype(jnp.int32).reshape(1))
    grads = _unpack(_chip_sum(_exchange_across_chips(chip_parts)), sizes)
    grads = [g.reshape(w.shape[1:]) for g, w in zip(grads, big)]

    n_mod = 6 * D // cw
    dmod = jnp.concatenate([vec0[0], vec0[1], vec1[4], vec1[0], vec1[1], vec2[2]])
    small = jnp.concatenate([dmod, dgq[0], dgkv[0], vec1[2], vec1[3], vec2[0], vec2[1], dwconv[:CONV_K].reshape(-1)])
    n_small = small.shape[0]
    nch = _round_up(n_small, cw) // cw
    payload = jnp.pad(small, (0, nch * cw - n_small)).reshape(nch, 1, cw)
    summed, g_w_ada = _ada_bwd(payload, cact_all.T, n_mod)
    summed = summed.reshape(-1)
    offs = [0, 6 * D, 6 * D + Lq, 6 * D + Lq + Lkv]
    offs += [offs[-1] + D * k for k in range(1, 5)]
    g_b_ada = summed[offs[0]:offs[1]].reshape(1, -1)
    g_gq = summed[offs[1]:offs[2]].reshape(1, -1)
    g_gkv = summed[offs[2]:offs[3]].reshape(1, -1)
    g_ln1g, g_ln1b, g_ln2g, g_ln2b = [summed[offs[3 + k]:offs[4 + k]].reshape(1, -1) for k in range(4)]
    wc = w_conv.shape[2]
    g_wconv = lax.dynamic_slice(summed[offs[7]:offs[7] + CONV_K * D].reshape(CONV_K, D), (0, me * wc), (CONV_K, wc))

    names = ["w_ada", "b_ada", "w_in", "g_q_a", "w_q_b", "g_kv_a", "w_kv_b", "w_o_a", "w_conv", "w_o_b", "w_o",
             "ln1_g", "ln1_b", "w_ffn_in", "w_ffn_out", "ln2_g", "ln2_b"]
    weights = [w_ada, b_ada, w_in, g_q_a, w_q_b, g_kv_a, w_kv_b, w_o_a, w_conv, w_o_b, w_o, ln1_g, ln1_b,
               w_ffn_in, w_ffn_out, ln2_g, ln2_b]
    moms = [m_w_ada, m_b_ada, m_w_in, m_g_q_a, m_w_q_b, m_g_kv_a, m_w_kv_b, m_w_o_a, m_w_conv, m_w_o_b, m_w_o,
            m_ln1_g, m_ln1_b, m_w_ffn_in, m_w_ffn_out, m_ln2_g, m_ln2_b]
    vels = [v_w_ada, v_b_ada, v_w_in, v_g_q_a, v_w_q_b, v_g_kv_a, v_w_kv_b, v_w_o_a, v_w_conv, v_w_o_b, v_w_o,
            v_ln1_g, v_ln1_b, v_w_ffn_in, v_w_ffn_out, v_ln2_g, v_ln2_b]
    g_in_s, g_qb_s, g_kvb_s, g_oa_s, g_ob_s, g_o_s, g_fi_s, g_fo_s = grads
    grad_list = [g_w_ada, g_b_ada, g_in_s, g_gq, g_qb_s, g_gkv, g_kvb_s, g_oa_s, g_wconv, g_ob_s, g_o_s,
                 g_ln1g, g_ln1b, g_fi_s, g_fo_s, g_ln2g, g_ln2b]
    out_g, out_d, out_m, out_v = [], [], [], []
    for nm, w, g, m, v in zip(names, weights, grad_list, moms, vels):
        shp = w.shape
        w2 = w.reshape(shp[-2], shp[-1]) if w.ndim == 3 else w
        g2 = g.reshape(w2.shape)
        d, nm_, nv_ = _adamw(w2, g2, m.reshape(w2.shape), v.reshape(w2.shape), "adamw_" + nm)
        out_g.append(g2.reshape(shp))
        out_d.append(d.reshape(shp))
        out_m.append(nm_.reshape(shp))
        out_v.append(nv_.reshape(shp))
    return (loss, grad_x.reshape(x.shape), *out_g, *out_d, *out_m, *out_v)
```

```python
import functools

import jax
import jax.numpy as jnp
from jax import lax
from jax.experimental import pallas as pl
from jax.experimental.pallas import tpu as pltpu

F32 = jnp.float32
BF16 = jnp.bfloat16
MESH_ID = pl.DeviceIdType.MESH
AXES = ("x", "y", "c")
N_DEV = 8

CHUNK = 64
QK_NOPE = 128
QK_ROPE = 64
V_HEAD = 128
QK_CAT = QK_NOPE + QK_ROPE
ROPE_THETA = 10000.0
ATTN_SCALE = (QK_NOPE + QK_ROPE) ** -0.5
CONV_K = 3
DEEPNORM_ALPHA = 2.0 ** 0.25
LN_EPS = 1e-5
RMS_EPS = 1e-6
NEG_INF = -1e30

ADAM_LR = 0.001
ADAM_B1 = 0.9
ADAM_B2 = 0.999
ADAM_EPS = 1e-08
ADAM_WD = 0.01
ADAM_STEP = 10

LANE = 128
COL_BLOCK = 256
PACK_COLS = 512
PACK_ROW_ALIGN = 16
VMEM_LIMIT = 48 * 1024 * 1024


def _round_up(n, m):
    return (n + m - 1) // m * m


def _tile(n, pref, align=LANE):
    best = None
    t = align
    while t <= min(n, pref):
        if n % t == 0:
            best = t
        t += align
    return best if best is not None else n


def _cparams(sem=None):
    return pltpu.CompilerParams(dimension_semantics=sem, vmem_limit_bytes=VMEM_LIMIT)


def _sigmoid(x):
    return 1.0 / (1.0 + jnp.exp(-x))


def _matmul(a, b, mode, out_dtype, name, tm=1024, tn=1024, tk=512):
    if mode == "nn":
        (M, K), (K2, N) = a.shape, b.shape
    elif mode == "nt":
        (M, K), (N, K2) = a.shape, b.shape
    else:
        (K, M), (K2, N) = a.shape, b.shape
    assert K == K2, (a.shape, b.shape, mode)
    tm, tn, tk = _tile(M, tm), _tile(N, tn), _tile(K, tk)
    nk = K // tk
    if mode == "nn":
        a_spec = pl.BlockSpec((tm, tk), lambda i, j, k: (i, k))
        b_spec = pl.BlockSpec((tk, tn), lambda i, j, k: (k, j))
        dims = (((1,), (0,)), ((), ()))
    elif mode == "nt":
        a_spec = pl.BlockSpec((tm, tk), lambda i, j, k: (i, k))
        b_spec = pl.BlockSpec((tn, tk), lambda i, j, k: (j, k))
        dims = (((1,), (1,)), ((), ()))
    else:
        a_spec = pl.BlockSpec((tk, tm), lambda i, j, k: (k, i))
        b_spec = pl.BlockSpec((tk, tn), lambda i, j, k: (k, j))
        dims = (((0,), (0,)), ((), ()))

    def body(a_ref, b_ref, o_ref, acc_ref):
        k = pl.program_id(2)

        @pl.when(k == 0)
        def _():
            acc_ref[...] = jnp.zeros_like(acc_ref)

        acc_ref[...] += lax.dot_general(a_ref[...].astype(BF16), b_ref[...].astype(BF16), dims,
                                        preferred_element_type=F32)

        @pl.when(k == nk - 1)
        def _():
            o_ref[...] = acc_ref[...].astype(o_ref.dtype)

    return pl.pallas_call(
        body, name=name, grid=(M // tm, N // tn, nk),
        in_specs=[a_spec, b_spec],
        out_specs=pl.BlockSpec((tm, tn), lambda i, j, k: (i, j)),
        out_shape=jax.ShapeDtypeStruct((M, N), out_dtype),
        scratch_shapes=[pltpu.VMEM((tm, tn), F32)],
        compiler_params=_cparams(("parallel", "parallel", "arbitrary")),
    )(a, b)


def _modulate_in(x, mod, ts):
    S, D = x.shape

    def body(x_ref, mod_ref, u_ref):
        u_ref[...] = (x_ref[...] * (1.0 + mod_ref[1:2, :]) + mod_ref[0:1, :]).astype(BF16)

    return pl.pallas_call(
        body, name="modulate_in", grid=(S // ts,),
        in_specs=[pl.BlockSpec((ts, D), lambda i: (i, 0)), pl.BlockSpec((6, D), lambda i: (0, 0))],
        out_specs=pl.BlockSpec((ts, D), lambda i: (i, 0)),
        out_shape=jax.ShapeDtypeStruct((S, D), BF16),
        compiler_params=_cparams(("parallel",)),
    )(x, mod)


def _rms_fwd(proj, g, blk, L, ts, name):
    S = proj.shape[0]

    def body(a_ref, g_ref, y_ref):
        a = a_ref[...]
        r = lax.rsqrt(jnp.mean(a * a, axis=-1, keepdims=True) + RMS_EPS)
        y_ref[...] = (a * r * g_ref[...]).astype(BF16)

    return pl.pallas_call(
        body, name=name, grid=(S // ts,),
        in_specs=[pl.BlockSpec((ts, L), lambda i: (i, blk)), pl.BlockSpec((1, L), lambda i: (0, 0))],
        out_specs=pl.BlockSpec((ts, L), lambda i: (i, 0)),
        out_shape=jax.ShapeDtypeStruct((S, L), BF16),
        compiler_params=_cparams(("parallel",)),
    )(proj, g)


def _rotate_half_pairs(x, sign):
    w = x.shape[-1]
    lane = lax.broadcasted_iota(jnp.int32, x.shape, x.ndim - 1)
    first = (lane % QK_ROPE) < (QK_ROPE // 2)
    from_right = pltpu.roll(x, w - QK_ROPE // 2, axis=x.ndim - 1)
    from_left = pltpu.roll(x, QK_ROPE // 2, axis=x.ndim - 1)
    return jnp.where(first, -sign * from_right, sign * from_left)


def _qk_prep(q, kv, proj, kr_blk, cos_q, sin_q, cos_k, sin_k, H, ts):
    S = q.shape[0]
    nope_w, rope_w = H * QK_NOPE, H * QK_ROPE

    def body(q_ref, kv_ref, kr_ref, cq_ref, sq_ref, ck_ref, sk_ref, qc_ref, kc_ref, vh_ref):
        qr = q_ref[:, nope_w:]
        qr = qr * cq_ref[...] + _rotate_half_pairs(qr, 1.0) * sq_ref[...]
        kr = kr_ref[...]
        kr = kr * ck_ref[...] + _rotate_half_pairs(kr, 1.0) * sk_ref[...]
        kr = kr[:, :QK_ROPE].astype(BF16)
        for h in range(H):
            qc_ref[h, :, 0:QK_NOPE] = q_ref[:, h * QK_NOPE:(h + 1) * QK_NOPE].astype(BF16)
            qc_ref[h, :, QK_NOPE:QK_CAT] = qr[:, h * QK_ROPE:(h + 1) * QK_ROPE].astype(BF16)
            kc_ref[h, :, 0:QK_NOPE] = kv_ref[:, h * QK_NOPE:(h + 1) * QK_NOPE].astype(BF16)
            kc_ref[h, :, QK_NOPE:QK_CAT] = kr
            vh_ref[h, :, :] = kv_ref[:, nope_w + h * V_HEAD:nope_w + (h + 1) * V_HEAD].astype(BF16)

    row = lambda w: pl.BlockSpec((ts, w), lambda i: (i, 0))
    return pl.pallas_call(
        body, name="qk_prep", grid=(S // ts,),
        in_specs=[row(nope_w + rope_w), row(nope_w + H * V_HEAD),
                  pl.BlockSpec((ts, COL_BLOCK), lambda i: (i, kr_blk)),
                  row(rope_w), row(rope_w), row(COL_BLOCK), row(COL_BLOCK)],
        out_specs=[pl.BlockSpec((H, ts, QK_CAT), lambda i: (0, i, 0)),
                   pl.BlockSpec((H, ts, QK_CAT), lambda i: (0, i, 0)),
                   pl.BlockSpec((H, ts, V_HEAD), lambda i: (0, i, 0))],
        out_shape=[jax.ShapeDtypeStruct((H, S, QK_CAT), BF16), jax.ShapeDtypeStruct((H, S, QK_CAT), BF16),
                   jax.ShapeDtypeStruct((H, S, V_HEAD), BF16)],
        compiler_params=_cparams(("parallel",)),
    )(q, kv, proj, cos_q, sin_q, cos_k, sin_k)


NT_DIMS = (((1,), (1,)), ((), ()))
TN_DIMS = (((0,), (0,)), ((), ()))


def _diag_mask(T):
    rows = lax.broadcasted_iota(jnp.int32, (T, T), 0) // CHUNK
    cols = lax.broadcasted_iota(jnp.int32, (T, T), 1) // CHUNK
    return cols <= rows


def _attn_fwd(qc, kc, vh, T):
    H, S, _ = qc.shape
    n = S // T

    def body(q_ref, k_ref, v_ref, o_ref, lse_ref, m_ref, l_ref, acc_ref):
        i = pl.program_id(1)
        q = q_ref[0]
        m_ref[...] = jnp.full_like(m_ref, NEG_INF)
        l_ref[...] = jnp.zeros_like(l_ref)
        acc_ref[...] = jnp.zeros_like(acc_ref)

        def step(j, masked):
            rows = pl.ds(pl.multiple_of(j * T, T), T)
            s = lax.dot_general(q, k_ref[0, rows, :], NT_DIMS, preferred_element_type=F32) * ATTN_SCALE
            if masked:
                s = jnp.where(_diag_mask(T), s, NEG_INF)
            m_old = m_ref[...]
            m_new = jnp.maximum(m_old, jnp.max(s, axis=-1, keepdims=True))
            alpha = jnp.exp(m_old - m_new)
            p = jnp.exp(s - m_new)
            l_ref[...] = alpha * l_ref[...] + jnp.sum(p, axis=-1, keepdims=True)
            acc_ref[...] = alpha * acc_ref[...] + jnp.dot(p.astype(BF16), v_ref[0, rows, :],
                                                          preferred_element_type=F32)
            m_ref[...] = m_new

        def below(j, carry):
            step(j, False)
            return carry

        lax.fori_loop(0, i, below, 0)
        step(i, True)
        o_ref[...] = acc_ref[...] / l_ref[...]
        lse_ref[0] = m_ref[...] + jnp.log(l_ref[...])

    return pl.pallas_call(
        body, name="attn_fwd", grid=(H, n),
        in_specs=[pl.BlockSpec((1, T, QK_CAT), lambda h, i: (h, i, 0)),
                  pl.BlockSpec((1, S, QK_CAT), lambda h, i: (h, 0, 0)),
                  pl.BlockSpec((1, S, V_HEAD), lambda h, i: (h, 0, 0))],
        out_specs=[pl.BlockSpec((T, V_HEAD), lambda h, i: (i, h)),
                   pl.BlockSpec((1, T, 1), lambda h, i: (h, i, 0))],
        out_shape=[jax.ShapeDtypeStruct((S, H * V_HEAD), F32), jax.ShapeDtypeStruct((H, S, 1), F32)],
        scratch_shapes=[pltpu.VMEM((T, 1), F32), pltpu.VMEM((T, 1), F32), pltpu.VMEM((T, V_HEAD), F32)],
        compiler_params=_cparams(("parallel", "arbitrary")),
    )(qc, kc, vh)


def _shift_rows(z, k):
    if k == 0:
        return z
    n = z.shape[0]
    row = lax.broadcasted_iota(jnp.int32, z.shape, 0)
    if k > 0:
        return jnp.where(row >= k, pltpu.roll(z, k, axis=0), 0.0)
    return jnp.where(row < n + k, pltpu.roll(z, n + k, axis=0), 0.0)


def _conv_fwd(proj, w_conv, blk_b, blk_c, blk_x):
    S = proj.shape[0]
    D = w_conv.shape[1]
    nb = D // COL_BLOCK

    def body(cb_ref, cc_ref, cx_ref, w_ref, o_ref):
        z = cc_ref[...] * cx_ref[...]
        conv = w_ref[2:3, :] * z + w_ref[1:2, :] * _shift_rows(z, 1) + w_ref[0:1, :] * _shift_rows(z, 2)
        o_ref[...] = (cb_ref[...] * conv).astype(BF16)

    col = lambda off: pl.BlockSpec((S, COL_BLOCK), lambda j: (0, off + j))
    return pl.pallas_call(
        body, name="conv_fwd", grid=(nb,),
        in_specs=[col(blk_b), col(blk_c), col(blk_x), pl.BlockSpec((CONV_K, COL_BLOCK), lambda j: (0, j))],
        out_specs=pl.BlockSpec((S, COL_BLOCK), lambda j: (0, j)),
        out_shape=jax.ShapeDtypeStruct((S, D), BF16),
        compiler_params=_cparams(("parallel",)),
    )(proj, proj, proj, w_conv)


def _merge_fwd(proj, ya, yb, blk_ga, blk_gb, ts):
    S, D = ya.shape
    nb = D // COL_BLOCK

    def body(ga_ref, gb_ref, ya_ref, yb_ref, o_ref):
        o_ref[...] = (_sigmoid(ga_ref[...]) * ya_ref[...] + _sigmoid(gb_ref[...]) * yb_ref[...]).astype(BF16)

    blk = lambda off: pl.BlockSpec((ts, COL_BLOCK), lambda i, j: (i, off + j))
    return pl.pallas_call(
        body, name="merge_fwd", grid=(S // ts, nb),
        in_specs=[blk(blk_ga), blk(blk_gb), blk(0), blk(0)],
        out_specs=blk(0),
        out_shape=jax.ShapeDtypeStruct((S, D), BF16),
        compiler_params=_cparams(("parallel", "parallel")),
    )(proj, proj, ya, yb)


def _ln1_fwd(x, mix, mod, g, b, ts):
    S, D = x.shape

    def body(x_ref, mix_ref, mod_ref, g_ref, b_ref, xhat_ref, rstd_ref, u2_ref):
        r = DEEPNORM_ALPHA * x_ref[...] + mod_ref[2:3, :] * mix_ref[...]
        mu = jnp.mean(r, axis=-1, keepdims=True)
        d = r - mu
        rstd = lax.rsqrt(jnp.mean(d * d, axis=-1, keepdims=True) + LN_EPS)
        xhat = d * rstd
        xhat_ref[...] = xhat
        rstd_ref[...] = rstd
        x1 = xhat * g_ref[...] + b_ref[...]
        u2_ref[...] = (x1 * (1.0 + mod_ref[4:5, :]) + mod_ref[3:4, :]).astype(BF16)

    row = pl.BlockSpec((ts, D), lambda i: (i, 0))
    vec = lambda r: pl.BlockSpec((r, D), lambda i: (0, 0))
    return pl.pallas_call(
        body, name="ln1_fwd", grid=(S // ts,),
        in_specs=[row, row, vec(6), vec(1), vec(1)],
        out_specs=[row, pl.BlockSpec((ts, 1), lambda i: (i, 0)), row],
        out_shape=[jax.ShapeDtypeStruct((S, D), F32), jax.ShapeDtypeStruct((S, 1), F32),
                   jax.ShapeDtypeStruct((S, D), BF16)],
        compiler_params=_cparams(("parallel",)),
    )(x, mix, mod, g, b)


def _swiglu_fwd(h, ts, tb):
    S, F2 = h.shape
    F = F2 // 2
    nb = F // tb

    def body(hg_ref, hu_ref, a_ref):
        hg = hg_ref[...]
        a_ref[...] = (hg * _sigmoid(hg) * hu_ref[...]).astype(BF16)

    return pl.pallas_call(
        body, name="swiglu_fwd", grid=(S // ts, nb),
        in_specs=[pl.BlockSpec((ts, tb), lambda i, j: (i, j)), pl.BlockSpec((ts, tb), lambda i, j: (i, j + nb))],
        out_specs=pl.BlockSpec((ts, tb), lambda i, j: (i, j)),
        out_shape=jax.ShapeDtypeStruct((S, F), BF16),
        compiler_params=_cparams(("parallel", "parallel")),
    )(h, h)


def _ln2_loss(xhat1, ffn, tgt, mod, g1, b1, g2, b2, ts):
    S, D = xhat1.shape

    def body(xh_ref, ffn_ref, t_ref, mod_ref, g1_ref, b1_ref, g2_ref, b2_ref, loss_ref, dffn_ref, dx1_ref, vec_ref):
        i = pl.program_id(0)

        @pl.when(i == 0)
        def _():
            loss_ref[...] = jnp.zeros_like(loss_ref)
            vec_ref[...] = jnp.zeros_like(vec_ref)

        x1 = xh_ref[...] * g1_ref[...] + b1_ref[...]
        ffn = ffn_ref[...]
        r = DEEPNORM_ALPHA * x1 + mod_ref[5:6, :] * ffn
        mu = jnp.mean(r, axis=-1, keepdims=True)
        d = r - mu
        rstd = lax.rsqrt(jnp.mean(d * d, axis=-1, keepdims=True) + LN_EPS)
        xhat = d * rstd
        e = xhat * g2_ref[...] + b2_ref[...] - t_ref[...]
        loss_ref[...] += 0.5 * jnp.sum(jnp.mean(e * e, axis=-1, keepdims=True))
        dy = e * (1.0 / D)
        dxhat = dy * g2_ref[...]
        dr = rstd * (dxhat - jnp.mean(dxhat, axis=-1, keepdims=True)
                     - xhat * jnp.mean(dxhat * xhat, axis=-1, keepdims=True))
        dffn_ref[...] = (dr * mod_ref[5:6, :]).astype(BF16)
        dx1_ref[...] = DEEPNORM_ALPHA * dr
        vec_ref[0:1, :] += jnp.sum(dy * xhat, axis=0, keepdims=True)
        vec_ref[1:2, :] += jnp.sum(dy, axis=0, keepdims=True)
        vec_ref[2:3, :] += jnp.sum(dr * ffn, axis=0, keepdims=True)

    row = pl.BlockSpec((ts, D), lambda i: (i, 0))
    vec = lambda r: pl.BlockSpec((r, D), lambda i: (0, 0))
    return pl.pallas_call(
        body, name="ln2_loss", grid=(S // ts,),
        in_specs=[row, row, row, vec(6), vec(1), vec(1), vec(1), vec(1)],
        out_specs=[pl.BlockSpec((1, LANE), lambda i: (0, 0)), row, row, vec(8)],
        out_shape=[jax.ShapeDtypeStruct((1, LANE), F32), jax.ShapeDtypeStruct((S, D), BF16),
                   jax.ShapeDtypeStruct((S, D), F32), jax.ShapeDtypeStruct((8, D), F32)],
        compiler_params=_cparams(("arbitrary",)),
    )(xhat1, ffn, tgt, mod, g1, b1, g2, b2)


def _swiglu_bwd(da, h, ts, tb):
    S, F2 = h.shape
    nb = (F2 // 2) // tb

    def body(da_ref, hg_ref, hu_ref, dg_ref, du_ref):
        hg, da = hg_ref[...], da_ref[...]
        sg = _sigmoid(hg)
        dg_ref[...] = (da * hu_ref[...] * (sg * (1.0 + hg * (1.0 - sg)))).astype(BF16)
        du_ref[...] = (da * hg * sg).astype(BF16)

    lo = pl.BlockSpec((ts, tb), lambda i, j: (i, j))
    hi = pl.BlockSpec((ts, tb), lambda i, j: (i, j + nb))
    dg, du = pl.pallas_call(
        body, name="swiglu_bwd", grid=(S // ts, nb),
        in_specs=[lo, lo, hi],
        out_specs=[lo, lo],
        out_shape=[jax.ShapeDtypeStruct((S, F2 // 2), BF16), jax.ShapeDtypeStruct((S, F2 // 2), BF16)],
        compiler_params=_cparams(("parallel", "parallel")),
    )(da, h, h)
    return dg, du


def _ln1_bwd(du2, dx1a, xhat1, rstd1, mix, mod, g1, b1, ts):
    S, D = xhat1.shape

    def body(du2_ref, dx1a_ref, xh_ref, rstd_ref, mix_ref, mod_ref, g_ref, b_ref, dxa_ref, dmix_ref, vec_ref):
        i = pl.program_id(0)

        @pl.when(i == 0)
        def _():
            vec_ref[...] = jnp.zeros_like(vec_ref)

        xhat, du2, mix = xh_ref[...], du2_ref[...], mix_ref[...]
        x1 = xhat * g_ref[...] + b_ref[...]
        dx1 = dx1a_ref[...] + du2 * (1.0 + mod_ref[4:5, :])
        dxhat = dx1 * g_ref[...]
        dr = rstd_ref[...] * (dxhat - jnp.mean(dxhat, axis=-1, keepdims=True)
                              - xhat * jnp.mean(dxhat * xhat, axis=-1, keepdims=True))
        dxa_ref[...] = DEEPNORM_ALPHA * dr
        dmix_ref[...] = (dr * mod_ref[2:3, :]).astype(BF16)
        vec_ref[0:1, :] += jnp.sum(du2, axis=0, keepdims=True)
        vec_ref[1:2, :] += jnp.sum(du2 * x1, axis=0, keepdims=True)
        vec_ref[2:3, :] += jnp.sum(dx1 * xhat, axis=0, keepdims=True)
        vec_ref[3:4, :] += jnp.sum(dx1, axis=0, keepdims=True)
        vec_ref[4:5, :] += jnp.sum(dr * mix, axis=0, keepdims=True)

    row = pl.BlockSpec((ts, D), lambda i: (i, 0))
    vec = lambda r: pl.BlockSpec((r, D), lambda i: (0, 0))
    return pl.pallas_call(
        body, name="ln1_bwd", grid=(S // ts,),
        in_specs=[row, row, row, pl.BlockSpec((ts, 1), lambda i: (i, 0)), row, vec(6), vec(1), vec(1)],
        out_specs=[row, row, vec(8)],
        out_shape=[jax.ShapeDtypeStruct((S, D), F32), jax.ShapeDtypeStruct((S, D), BF16),
                   jax.ShapeDtypeStruct((8, D), F32)],
        compiler_params=_cparams(("arbitrary",)),
    )(du2, dx1a, xhat1, rstd1, mix, mod, g1, b1)


def _merge_bwd(dmerged, proj, ya, yb, blk_ga, blk_gb, ts):
    S, D = ya.shape
    nb = D // COL_BLOCK

    def body(dm_ref, ga_ref, gb_ref, ya_ref, yb_ref, dya_ref, dyb_ref, dga_ref, dgb_ref):
        dm = dm_ref[...]
        sa, sb = _sigmoid(ga_ref[...]), _sigmoid(gb_ref[...])
        dya_ref[...] = (dm * sa).astype(BF16)
        dyb_ref[...] = (dm * sb).astype(BF16)
        dga_ref[...] = (dm * ya_ref[...] * sa * (1.0 - sa)).astype(BF16)
        dgb_ref[...] = (dm * yb_ref[...] * sb * (1.0 - sb)).astype(BF16)

    blk = lambda off: pl.BlockSpec((ts, COL_BLOCK), lambda i, j: (i, off + j))
    out = jax.ShapeDtypeStruct((S, D), BF16)
    return pl.pallas_call(
        body, name="merge_bwd", grid=(S // ts, nb),
        in_specs=[blk(0), blk(blk_ga), blk(blk_gb), blk(0), blk(0)],
        out_specs=[blk(0)] * 4,
        out_shape=[out] * 4,
        compiler_params=_cparams(("parallel", "parallel")),
    )(dmerged, proj, proj, ya, yb)


def _conv_bwd(dcbc, proj, w_conv, blk_b, blk_c, blk_x):
    S = proj.shape[0]
    D = w_conv.shape[1]
    nb = D // COL_BLOCK

    def body(d_ref, cb_ref, cc_ref, cx_ref, w_ref, dcb_ref, dcc_ref, dcx_ref, dw_ref):
        d, cc, cx = d_ref[...], cc_ref[...], cx_ref[...]
        z = cc * cx
        z1, z2 = _shift_rows(z, 1), _shift_rows(z, 2)
        conv = w_ref[2:3, :] * z + w_ref[1:2, :] * z1 + w_ref[0:1, :] * z2
        dcb_ref[...] = (d * conv).astype(BF16)
        dconv = d * cb_ref[...]
        dz = w_ref[2:3, :] * dconv + w_ref[1:2, :] * _shift_rows(dconv, -1) + w_ref[0:1, :] * _shift_rows(dconv, -2)
        dcc_ref[...] = (dz * cx).astype(BF16)
        dcx_ref[...] = (dz * cc).astype(BF16)
        dw_ref[...] = jnp.zeros_like(dw_ref)
        dw_ref[0:1, :] = jnp.sum(dconv * z2, axis=0, keepdims=True)
        dw_ref[1:2, :] = jnp.sum(dconv * z1, axis=0, keepdims=True)
        dw_ref[2:3, :] = jnp.sum(dconv * z, axis=0, keepdims=True)

    col = lambda off: pl.BlockSpec((S, COL_BLOCK), lambda j: (0, off + j))
    out = jax.ShapeDtypeStruct((S, D), BF16)
    return pl.pallas_call(
        body, name="conv_bwd", grid=(nb,),
        in_specs=[col(0), col(blk_b), col(blk_c), col(blk_x), pl.BlockSpec((CONV_K, COL_BLOCK), lambda j: (0, j))],
        out_specs=[col(0), col(0), col(0), pl.BlockSpec((8, COL_BLOCK), lambda j: (0, j))],
        out_shape=[out, out, out, jax.ShapeDtypeStruct((8, D), F32)],
        compiler_params=_cparams(("parallel",)),
    )(dcbc, proj, proj, proj, w_conv)


def _attn_bwd(qc, kc, vh, do, o, lse, T):
    H, S, _ = qc.shape
    n = S // T

    def body(q_ref, k_ref, v_ref, do_ref, o_ref, lse_ref, dq_ref, dk_ref, dv_ref, d_ref, dk_acc, dv_acc):
        j = pl.program_id(1)

        @pl.when(j == 0)
        def _():
            dq_ref[...] = jnp.zeros_like(dq_ref)
            d_ref[...] = jnp.sum(do_ref[...] * o_ref[...], axis=-1, keepdims=True)

        dk_acc[...] = jnp.zeros_like(dk_acc)
        dv_acc[...] = jnp.zeros_like(dv_acc)
        k, v = k_ref[0], v_ref[0]

        def step(i, masked):
            rows = pl.ds(pl.multiple_of(i * T, T), T)
            q = q_ref[0, rows, :]
            do = do_ref[rows, :].astype(BF16)
            s = lax.dot_general(q, k, NT_DIMS, preferred_element_type=F32) * ATTN_SCALE
            if masked:
                s = jnp.where(_diag_mask(T), s, NEG_INF)
            p = jnp.exp(s - lse_ref[0, rows, :])
            dv_acc[...] += lax.dot_general(p.astype(BF16), do, TN_DIMS, preferred_element_type=F32)
            dp = lax.dot_general(do, v, NT_DIMS, preferred_element_type=F32)
            ds = (p * (dp - d_ref[rows, :]) * ATTN_SCALE).astype(BF16)
            dk_acc[...] += lax.dot_general(ds, q, TN_DIMS, preferred_element_type=F32)
            dq_ref[0, rows, :] += jnp.dot(ds, k, preferred_element_type=F32)

        def above(i, carry):
            step(i, False)
            return carry

        step(j, True)
        lax.fori_loop(j + 1, n, above, 0)
        dk_ref[0] = dk_acc[...]
        dv_ref[0] = dv_acc[...]

    head = lambda w: pl.BlockSpec((1, S, w), lambda h, j: (h, 0, 0))
    blk = lambda w: pl.BlockSpec((1, T, w), lambda h, j: (h, j, 0))
    ospec = pl.BlockSpec((S, V_HEAD), lambda h, j: (0, h))
    return pl.pallas_call(
        body, name="attn_bwd", grid=(H, n),
        in_specs=[head(QK_CAT), blk(QK_CAT), blk(V_HEAD), ospec, ospec, head(1)],
        out_specs=[head(QK_CAT), blk(QK_CAT), blk(V_HEAD)],
        out_shape=[jax.ShapeDtypeStruct((H, S, QK_CAT), F32), jax.ShapeDtypeStruct((H, S, QK_CAT), F32),
                   jax.ShapeDtypeStruct((H, S, V_HEAD), F32)],
        scratch_shapes=[pltpu.VMEM((S, 1), F32), pltpu.VMEM((T, QK_CAT), F32), pltpu.VMEM((T, V_HEAD), F32)],
        compiler_params=_cparams(("parallel", "arbitrary")),
    )(qc, kc, vh, do, o, lse)


def _qk_bwd(dqc, dkc, dvh, cos_q, sin_q, cos_k, sin_k, ts):
    H, S, _ = dqc.shape
    nope_w, rope_w = H * QK_NOPE, H * QK_ROPE

    def body(dqc_ref, dkc_ref, dvh_ref, cq_ref, sq_ref, ck_ref, sk_ref, dq_ref, dkv_ref, dkr_ref, qr_buf, kr_buf):
        kr_sum = jnp.zeros((ts, QK_ROPE), F32)
        for h in range(H):
            dq_ref[:, h * QK_NOPE:(h + 1) * QK_NOPE] = dqc_ref[h, :, 0:QK_NOPE].astype(BF16)
            qr_buf[:, h * QK_ROPE:(h + 1) * QK_ROPE] = dqc_ref[h, :, QK_NOPE:QK_CAT]
            dkv_ref[:, h * QK_NOPE:(h + 1) * QK_NOPE] = dkc_ref[h, :, 0:QK_NOPE].astype(BF16)
            dkv_ref[:, nope_w + h * V_HEAD:nope_w + (h + 1) * V_HEAD] = dvh_ref[h].astype(BF16)
            kr_sum = kr_sum + dkc_ref[h, :, QK_NOPE:QK_CAT]
        qr = qr_buf[...]
        dq_ref[:, nope_w:] = (qr * cq_ref[...] + _rotate_half_pairs(qr, -1.0) * sq_ref[...]).astype(BF16)
        kr_buf[...] = jnp.zeros_like(kr_buf)
        kr_buf[:, 0:QK_ROPE] = kr_sum
        kr = kr_buf[...]
        dkr_ref[...] = (kr * ck_ref[...] + _rotate_half_pairs(kr, -1.0) * sk_ref[...]).astype(BF16)

    row = lambda w: pl.BlockSpec((ts, w), lambda i: (i, 0))
    head = lambda w: pl.BlockSpec((H, ts, w), lambda i: (0, i, 0))
    return pl.pallas_call(
        body, name="qk_bwd", grid=(S // ts,),
        in_specs=[head(QK_CAT), head(QK_CAT), head(V_HEAD), row(rope_w), row(rope_w), row(COL_BLOCK), row(COL_BLOCK)],
        out_specs=[row(nope_w + rope_w), row(nope_w + H * V_HEAD), row(COL_BLOCK)],
        out_shape=[jax.ShapeDtypeStruct((S, nope_w + rope_w), BF16), jax.ShapeDtypeStruct((S, nope_w + H * V_HEAD), BF16),
                   jax.ShapeDtypeStruct((S, COL_BLOCK), BF16)],
        scratch_shapes=[pltpu.VMEM((ts, rope_w), F32), pltpu.VMEM((ts, COL_BLOCK), F32)],
        compiler_params=_cparams(("parallel",)),
    )(dqc, dkc, dvh, cos_q, sin_q, cos_k, sin_k)


def _rms_bwd(dy, proj, g, blk, L, ts, name):
    S = proj.shape[0]

    def body(dy_ref, a_ref, g_ref, da_ref, dg_ref):
        i = pl.program_id(0)

        @pl.when(i == 0)
        def _():
            dg_ref[...] = jnp.zeros_like(dg_ref)

        a, dy = a_ref[...], dy_ref[...]
        r = lax.rsqrt(jnp.mean(a * a, axis=-1, keepdims=True) + RMS_EPS)
        dyh = dy * g_ref[...]
        da = r * dyh - a * (r * r * r) * jnp.mean(dyh * a, axis=-1, keepdims=True)
        da_ref[...] = da.astype(BF16)
        dg_ref[0:1, :] += jnp.sum(dy * a * r, axis=0, keepdims=True)

    return pl.pallas_call(
        body, name=name, grid=(S // ts,),
        in_specs=[pl.BlockSpec((ts, L), lambda i: (i, 0)), pl.BlockSpec((ts, L), lambda i: (i, blk)),
                  pl.BlockSpec((1, L), lambda i: (0, 0))],
        out_specs=[pl.BlockSpec((ts, L), lambda i: (i, 0)), pl.BlockSpec((8, L), lambda i: (0, 0))],
        out_shape=[jax.ShapeDtypeStruct((S, L), BF16), jax.ShapeDtypeStruct((8, L), F32)],
        compiler_params=_cparams(("arbitrary",)),
    )(dy, proj, g)


def _grad_x(du, dxa, x, mod, ts):
    S, D = x.shape

    def body(du_ref, dxa_ref, x_ref, mod_ref, dx_ref, vec_ref):
        i = pl.program_id(0)

        @pl.when(i == 0)
        def _():
            vec_ref[...] = jnp.zeros_like(vec_ref)

        du = du_ref[...]
        dx_ref[...] = dxa_ref[...] + du * (1.0 + mod_ref[1:2, :])
        vec_ref[0:1, :] += jnp.sum(du, axis=0, keepdims=True)
        vec_ref[1:2, :] += jnp.sum(du * x_ref[...], axis=0, keepdims=True)

    row = pl.BlockSpec((ts, D), lambda i: (i, 0))
    vec = lambda r: pl.BlockSpec((r, D), lambda i: (0, 0))
    return pl.pallas_call(
        body, name="grad_x", grid=(S // ts,),
        in_specs=[row, row, row, vec(6)],
        out_specs=[row, vec(8)],
        out_shape=[jax.ShapeDtypeStruct((S, D), F32), jax.ShapeDtypeStruct((8, D), F32)],
        compiler_params=_cparams(("arbitrary",)),
    )(du, dxa, x, mod)


def _adamw(w, g, m, v, name):
    R, C = w.shape
    tr = _tile(R, max(8, (1 << 19) // C), 8)
    c1 = 1.0 / (1.0 - ADAM_B1 ** ADAM_STEP)
    c2 = 1.0 / (1.0 - ADAM_B2 ** ADAM_STEP)

    def body(w_ref, g_ref, m_ref, v_ref, d_ref, nm_ref, nv_ref):
        g = g_ref[...]
        m = ADAM_B1 * m_ref[...] + (1.0 - ADAM_B1) * g
        v = ADAM_B2 * v_ref[...] + (1.0 - ADAM_B2) * (g * g)
        nm_ref[...] = m
        nv_ref[...] = v
        d_ref[...] = -ADAM_LR * ((m * c1) / (jnp.sqrt(v * c2) + ADAM_EPS) + ADAM_WD * w_ref[...])

    spec = pl.BlockSpec((tr, C), lambda i: (i, 0))
    out = jax.ShapeDtypeStruct((R, C), F32)
    return pl.pallas_call(
        body, name=name, grid=(R // tr,),
        in_specs=[spec] * 4, out_specs=[spec] * 3, out_shape=[out] * 3,
        compiler_params=_cparams(("parallel",)),
    )(w, g, m, v)


def _my_place():
    return lax.axis_index("x"), lax.axis_index("y"), lax.axis_index("c")


def _peer(k):
    x, y, c = _my_place()
    return (x ^ ((k >> 2) & 1), y ^ ((k >> 1) & 1), c ^ (k & 1))


def _linear(place):
    return 4 * place[0] + 2 * place[1] + place[2]


def _ada_fwd(c_row, wconv_row, w_ada, b_row):
    D, CW = w_ada.shape
    WC = wconv_row.shape[-1]

    def body(c_ref, wc_ref, w_ref, b_ref, mod_ref, cact_ref, wcall_ref, send_buf, sems):
        me = _linear(_my_place())
        c = c_ref[0]
        cact_ref[me] = c * _sigmoid(c)
        wcall_ref[me] = wc_ref[0]

        def gather_copy(buf, k, grp):
            return pltpu.make_async_remote_copy(
                src_ref=buf.at[me], dst_ref=buf.at[me], send_sem=sems.at[0, grp, k], recv_sem=sems.at[1, grp, k],
                device_id=_peer(k), device_id_type=MESH_ID)

        def gather_recv(buf, k, grp):
            src = _linear(_peer(k))
            return pltpu.make_async_remote_copy(
                src_ref=buf.at[src], dst_ref=buf.at[src], send_sem=sems.at[0, grp, k], recv_sem=sems.at[1, grp, k],
                device_id=_peer(k), device_id_type=MESH_ID)

        for k in range(1, N_DEV):
            gather_copy(cact_ref, k, 0).start()
            gather_copy(wcall_ref, k, 1).start()
        for k in range(1, N_DEV):
            gather_recv(cact_ref, k, 0).wait_recv()
            gather_recv(wcall_ref, k, 1).wait_recv()
        for k in range(1, N_DEV):
            gather_copy(cact_ref, k, 0).wait_send()
            gather_copy(wcall_ref, k, 1).wait_send()

        cact = jnp.concatenate([cact_ref[b] for b in range(N_DEV)], axis=0)
        mod_all = jnp.dot(cact.astype(BF16), w_ref[...].astype(BF16), preferred_element_type=F32) + b_ref[0]
        for b in range(N_DEV):
            send_buf[b] = mod_all[b:b + 1, :]
        mod_ref[me] = send_buf[me]

        def scatter_copy(k):
            dst = _linear(_peer(k))
            return pltpu.make_async_remote_copy(
                src_ref=send_buf.at[dst], dst_ref=mod_ref.at[me], send_sem=sems.at[0, 2, k], recv_sem=sems.at[1, 2, k],
                device_id=_peer(k), device_id_type=MESH_ID)

        def scatter_recv(k):
            src = _linear(_peer(k))
            return pltpu.make_async_remote_copy(
                src_ref=send_buf.at[src], dst_ref=mod_ref.at[src], send_sem=sems.at[0, 2, k], recv_sem=sems.at[1, 2, k],
                device_id=_peer(k), device_id_type=MESH_ID)

        for k in range(1, N_DEV):
            scatter_copy(k).start()
        for k in range(1, N_DEV):
            scatter_recv(k).wait_recv()
        for k in range(1, N_DEV):
            scatter_copy(k).wait_send()

    vmem = pl.BlockSpec(memory_space=pltpu.VMEM)
    return pl.pallas_call(
        body, name="ada_fwd",
        in_specs=[vmem] * 4, out_specs=[vmem] * 3,
        out_shape=[jax.ShapeDtypeStruct((N_DEV, 1, CW), F32), jax.ShapeDtypeStruct((N_DEV, 1, D), F32),
                   jax.ShapeDtypeStruct((N_DEV, 1, WC), F32)],
        scratch_shapes=[pltpu.VMEM((N_DEV, 1, CW), F32), pltpu.SemaphoreType.DMA((2, 3, N_DEV))],
        compiler_params=pltpu.CompilerParams(vmem_limit_bytes=VMEM_LIMIT),
    )(c_row, wconv_row, w_ada, b_row)


def _ada_bwd(payload, cact_t, n_mod):
    NCH, _, CW = payload.shape
    D = cact_t.shape[0]

    def body(p_ref, ct_ref, sum_ref, gw_ref, all_ref, sems):
        me = _linear(_my_place())
        all_ref[me] = p_ref[...]

        def copy(k, slot):
            return pltpu.make_async_remote_copy(
                src_ref=all_ref.at[slot], dst_ref=all_ref.at[slot], send_sem=sems.at[0, k], recv_sem=sems.at[1, k],
                device_id=_peer(k), device_id_type=MESH_ID)

        for k in range(1, N_DEV):
            copy(k, me).start()
        for k in range(1, N_DEV):
            copy(k, _linear(_peer(k))).wait_recv()
        for k in range(1, N_DEV):
            copy(k, me).wait_send()

        total = all_ref[0]
        for b in range(1, N_DEV):
            total = total + all_ref[b]
        sum_ref[...] = total

        ct = ct_ref[...].astype(BF16).astype(F32)
        gw = jnp.zeros((D, CW), F32)
        for b in range(N_DEV):
            dm = all_ref[b, me].astype(BF16).astype(F32)
            gw = gw + ct[:, b:b + 1] * dm
        gw_ref[...] = gw

    vmem = pl.BlockSpec(memory_space=pltpu.VMEM)
    return pl.pallas_call(
        body, name="ada_bwd",
        in_specs=[vmem, vmem], out_specs=[vmem, vmem],
        out_shape=[jax.ShapeDtypeStruct((NCH, 1, CW), F32), jax.ShapeDtypeStruct((D, CW), F32)],
        scratch_shapes=[pltpu.VMEM((N_DEV, NCH, 1, CW), F32), pltpu.SemaphoreType.DMA((2, N_DEV))],
        compiler_params=pltpu.CompilerParams(vmem_limit_bytes=VMEM_LIMIT),
    )(payload, cact_t)


def _all_gather(shard):
    R, C = shard.shape

    def body(x_ref, out_ref, send_sems, recv_sems, local_sem):
        x, y, c = _my_place()
        me, sibling = (x, y, c), (x, y, 1 - c)
        chips = [(1 - x, y), (x, 1 - y), (1 - x, 1 - y)]

        def copy(k, block, to, src=None):
            slot = out_ref.at[_linear(block)]
            return pltpu.make_async_remote_copy(
                src_ref=slot if src is None else src, dst_ref=slot,
                send_sem=send_sems.at[k], recv_sem=recv_sems.at[k], device_id=to, device_id_type=MESH_ID)

        mine = pltpu.make_async_copy(x_ref, out_ref.at[_linear(me)], local_sem)
        mine.start()
        first = [copy(0, me, sibling, src=x_ref)]
        first += [copy(1 + j, me, (*chip, c), src=x_ref) for j, chip in enumerate(chips)]
        for cp in first:
            cp.start()
        passed = [copy(4 + j, (*chip, c), sibling) for j, chip in enumerate(chips)]
        for j, chip in enumerate(chips):
            copy(1 + j, (*chip, c), me).wait_recv()
            passed[j].start()
        copy(0, sibling, me).wait_recv()
        for j, chip in enumerate(chips):
            copy(4 + j, (*chip, 1 - c), me).wait_recv()
        for cp in first + passed:
            cp.wait_send()
        mine.wait()

    hbm = pl.BlockSpec(memory_space=pltpu.HBM)
    return pl.pallas_call(
        body, name="weight_all_gather",
        in_specs=[hbm], out_specs=hbm,
        out_shape=jax.ShapeDtypeStruct((N_DEV, R, C), shard.dtype),
        scratch_shapes=[pltpu.SemaphoreType.DMA((7,)), pltpu.SemaphoreType.DMA((7,)), pltpu.SemaphoreType.DMA],
    )(shard)


def _exchange_in_chip(parts):
    _, R, C = parts.shape

    def body(p_ref, got_ref, send_sems, recv_sems):
        x, y, c = _my_place()
        sibling = (x, y, 1 - c)
        copies = []
        for q in range(4):
            copies.append(pltpu.make_async_remote_copy(
                src_ref=p_ref.at[2 * q + (1 - c)], dst_ref=got_ref.at[q],
                send_sem=send_sems.at[q], recv_sem=recv_sems.at[q], device_id=sibling, device_id_type=MESH_ID))
        for cp in copies:
            cp.start()
        for cp in copies:
            cp.wait_recv()
        for cp in copies:
            cp.wait_send()

    hbm = pl.BlockSpec(memory_space=pltpu.HBM)
    return pl.pallas_call(
        body, name="grad_exchange_in_chip",
        in_specs=[hbm], out_specs=hbm,
        out_shape=jax.ShapeDtypeStruct((4, R, C), parts.dtype),
        scratch_shapes=[pltpu.SemaphoreType.DMA((4,)), pltpu.SemaphoreType.DMA((4,))],
    )(parts)


def _pair_sum(parts, got, core):
    _, R, C = parts.shape
    tr = _tile(R, 1024, PACK_ROW_ALIGN)

    def body(c_ref, p_ref, g_ref, o_ref):
        o_ref[...] = (p_ref[...].astype(F32) + g_ref[...].astype(F32)).astype(o_ref.dtype)

    return pl.pallas_call(
        body, name="grad_pair_sum",
        grid_spec=pltpu.PrefetchScalarGridSpec(
            num_scalar_prefetch=1, grid=(4, R // tr),
            in_specs=[pl.BlockSpec((1, tr, C), lambda q, i, c_ref: (2 * q + c_ref[0], i, 0)),
                      pl.BlockSpec((1, tr, C), lambda q, i, c_ref: (q, i, 0))],
            out_specs=pl.BlockSpec((1, tr, C), lambda q, i, c_ref: (q, i, 0))),
        out_shape=jax.ShapeDtypeStruct((4, R, C), parts.dtype),
        compiler_params=_cparams(("parallel", "parallel")),
    )(core, parts, got)


def _exchange_across_chips(chip_parts):
    _, R, C = chip_parts.shape

    def body(p_ref, got_ref, send_sems, recv_sems, local_sem):
        x, y, c = _my_place()
        my_chip = 2 * x + y
        mine = pltpu.make_async_copy(p_ref.at[my_chip], got_ref.at[my_chip], local_sem)
        mine.start()
        copies, arrivals = [], []
        for k in range(1, 4):
            px, py = x ^ (k >> 1), y ^ (k & 1)
            copies.append(pltpu.make_async_remote_copy(
                src_ref=p_ref.at[2 * px + py], dst_ref=got_ref.at[my_chip],
                send_sem=send_sems.at[k - 1], recv_sem=recv_sems.at[k - 1],
                device_id=(px, py, c), device_id_type=MESH_ID))
            arrivals.append(pltpu.make_async_remote_copy(
                src_ref=p_ref.at[2 * px + py], dst_ref=got_ref.at[2 * px + py],
                send_sem=send_sems.at[k - 1], recv_sem=recv_sems.at[k - 1],
                device_id=(px, py, c), device_id_type=MESH_ID))
        for cp in copies:
            cp.start()
        for cp in arrivals:
            cp.wait_recv()
        for cp in copies:
            cp.wait_send()
        mine.wait()

    hbm = pl.BlockSpec(memory_space=pltpu.HBM)
    return pl.pallas_call(
        body, name="grad_exchange_across_chips",
        in_specs=[hbm], out_specs=hbm,
        out_shape=jax.ShapeDtypeStruct((4, R, C), chip_parts.dtype),
        scratch_shapes=[pltpu.SemaphoreType.DMA((3,)), pltpu.SemaphoreType.DMA((3,)), pltpu.SemaphoreType.DMA],
    )(chip_parts)


def _chip_sum(got):
    _, R, C = got.shape
    tr = _tile(R, 1024, PACK_ROW_ALIGN)

    def body(g_ref, o_ref):
        acc = g_ref[0].astype(F32)
        for q in range(1, 4):
            acc = acc + g_ref[q].astype(F32)
        o_ref[...] = acc

    return pl.pallas_call(
        body, name="grad_chip_sum", grid=(R // tr,),
        in_specs=[pl.BlockSpec((4, tr, C), lambda i: (0, i, 0))],
        out_specs=pl.BlockSpec((tr, C), lambda i: (i, 0)),
        out_shape=jax.ShapeDtypeStruct((R, C), F32),
        compiler_params=_cparams(("parallel",)),
    )(got)


def _pack_rows(n_elems):
    return _round_up(_round_up(n_elems, PACK_COLS) // PACK_COLS, PACK_ROW_ALIGN)


def _pack(flat_list, dtype):
    out = []
    for a in flat_list:
        n = a.shape[-1]
        rows = _pack_rows(n)
        pad = rows * PACK_COLS - n
        a = a.astype(dtype)
        if pad:
            a = jnp.pad(a, [(0, 0)] * (a.ndim - 1) + [(0, pad)])
        out.append(a.reshape(a.shape[:-1] + (rows, PACK_COLS)))
    return jnp.concatenate(out, axis=-2)


def _unpack(buf, sizes):
    out, r0 = [], 0
    for n in sizes:
        rows = _pack_rows(n)
        a = buf[..., r0:r0 + rows, :]
        out.append(a.reshape(a.shape[:-2] + (rows * PACK_COLS,))[..., :n])
        r0 += rows
    return out


def _cols_to_shards(w):
    K, N = w.shape
    return w.reshape(K, N_DEV, N // N_DEV).transpose(1, 0, 2).reshape(N_DEV, -1)


def _shards_to_cols(s, K):
    return s.reshape(N_DEV, K, -1).transpose(1, 0, 2).reshape(K, -1)


def kernel(x, c, positions, w_ada, b_ada, w_in, g_q_a, w_q_b, g_kv_a, w_kv_b, w_o_a, w_conv, w_o_b, w_o, ln1_g, ln1_b, w_ffn_in, w_ffn_out, ln2_g, ln2_b, loss_target, m_w_ada, m_b_ada, m_w_in, m_g_q_a, m_w_q_b, m_g_kv_a, m_w_kv_b, m_w_o_a, m_w_conv, m_w_o_b, m_w_o, m_ln1_g, m_ln1_b, m_w_ffn_in, m_w_ffn_out, m_ln2_g, m_ln2_b, v_w_ada, v_b_ada, v_w_in, v_g_q_a, v_w_q_b, v_g_kv_a, v_w_kv_b, v_w_o_a, v_w_conv, v_w_o_b, v_w_o, v_ln1_g, v_ln1_b, v_w_ffn_in, v_w_ffn_out, v_ln2_g, v_ln2_b):
    x2, tgt = x[0], loss_target[0]
    S, D = x2.shape
    Lq, Lkv = g_q_a.shape[1], g_kv_a.shape[1]
    H = w_q_b.shape[2] * N_DEV // QK_CAT
    F = w_ffn_out.shape[1] * N_DEV
    n_in = w_in.shape[2] * N_DEV
    assert Lq == Lkv and (Lq + Lkv) % COL_BLOCK == 0 and D % COL_BLOCK == 0
    front = Lq + Lkv + QK_ROPE
    front_pad = _round_up(front, COL_BLOCK)
    kr_blk = (Lq + Lkv) // COL_BLOCK
    blk_b = front_pad // COL_BLOCK
    nblk = D // COL_BLOCK
    blk_c, blk_x, blk_ga, blk_gb = blk_b + nblk, blk_b + 2 * nblk, blk_b + 3 * nblk, blk_b + 4 * nblk
    ts = _tile(S, 256, 8)
    T = _tile(S, min(512, S // 2), CHUNK)
    tb = _tile(F, 512)
    me = _linear(_my_place())

    cw = w_ada.shape[2]
    b_mine = lax.dynamic_slice(b_ada, (0, me * cw), (1, cw)).reshape(1, 1, cw)
    mod_blocks, cact_all, wconv_all = _ada_fwd(c.reshape(1, 1, D), w_conv[0].reshape(1, 1, -1), w_ada[0], b_mine)
    mod = mod_blocks.reshape(6, D)
    cact_all = cact_all.reshape(N_DEV, D)
    w_conv_full = _shards_to_cols(wconv_all.reshape(N_DEV, -1), CONV_K)

    big = [w_in, w_q_b, w_kv_b, w_o_a, w_o_b, w_o, w_ffn_in, w_ffn_out]
    sizes = [w.shape[1] * w.shape[2] for w in big]
    gathered = _all_gather(_pack([w[0].reshape(-1) for w in big], BF16))
    g_in, g_qb, g_kvb, g_oa, g_ob, g_o, g_fi, g_fo = _unpack(gathered, sizes)
    w_in_f = _shards_to_cols(g_in, D)
    w_in_p = jnp.concatenate([w_in_f[:, :front], jnp.zeros((D, front_pad - front), BF16), w_in_f[:, front:]], axis=1)
    wq = _shards_to_cols(g_qb, Lq).reshape(Lq, H, QK_CAT)
    wq_p = jnp.concatenate([wq[:, :, :QK_NOPE].reshape(Lq, -1), wq[:, :, QK_NOPE:].reshape(Lq, -1)], axis=1)
    wkv = _shards_to_cols(g_kvb, Lkv).reshape(Lkv, H, QK_NOPE + V_HEAD)
    wkv_p = jnp.concatenate([wkv[:, :, :QK_NOPE].reshape(Lkv, -1), wkv[:, :, QK_NOPE:].reshape(Lkv, -1)], axis=1)
    w_oa_f, w_ob_f, w_o_f = g_oa.reshape(-1, D), g_ob.reshape(-1, D), g_o.reshape(-1, D)
    w_fi_f = _shards_to_cols(g_fi, D)
    w_fo_f = g_fo.reshape(F, D)

    inv_freq = 1.0 / (ROPE_THETA ** (jnp.arange(0, QK_ROPE, 2, dtype=F32) / QK_ROPE))
    ang = positions[0].astype(F32)[:, None] * inv_freq
    cos2 = jnp.concatenate([jnp.cos(ang), jnp.cos(ang)], axis=-1)
    sin2 = jnp.concatenate([jnp.sin(ang), jnp.sin(ang)], axis=-1)
    cos_q, sin_q = jnp.tile(cos2, (1, H)), jnp.tile(sin2, (1, H))
    cos_k, sin_k = jnp.tile(cos2, (1, COL_BLOCK // QK_ROPE)), jnp.tile(sin2, (1, COL_BLOCK // QK_ROPE))

    u = _modulate_in(x2, mod, ts)
    proj = _matmul(u, w_in_p, "nn", F32, "proj")
    qn = _rms_fwd(proj, g_q_a, 0, Lq, ts, "rms_q")
    kvn = _rms_fwd(proj, g_kv_a, 1, Lkv, ts, "rms_kv")
    q = _matmul(qn, wq_p, "nn", F32, "q_up")
    kv = _matmul(kvn, wkv_p, "nn", F32, "kv_up")
    qc, kc, vh = _qk_prep(q, kv, proj, kr_blk, cos_q, sin_q, cos_k, sin_k, H, ts)
    attn, lse = _attn_fwd(qc, kc, vh, T)
    ya = _matmul(attn, w_oa_f, "nn", F32, "attn_out")
    cbc = _conv_fwd(proj, w_conv_full, blk_b, blk_c, blk_x)
    yb = _matmul(cbc, w_ob_f, "nn", F32, "conv_out")
    merged = _merge_fwd(proj, ya, yb, blk_ga, blk_gb, ts)
    mix = _matmul(merged, w_o_f, "nn", F32, "mix_out")
    xhat1, rstd1, u2 = _ln1_fwd(x2, mix, mod, ln1_g, ln1_b, ts)
    hh = _matmul(u2, w_fi_f, "nn", F32, "ffn_in")
    act = _swiglu_fwd(hh, ts, tb)
    ffn = _matmul(act, w_fo_f, "nn", F32, "ffn_out")
    loss_part, dffn, dx1a, vec2 = _ln2_loss(xhat1, ffn, tgt, mod, ln1_g, ln1_b, ln2_g, ln2_b, ts)
    loss = lax.psum(loss_part[0, 0], AXES)

    gw_fo = _matmul(act, dffn, "tn", BF16, "grad_w_ffn_out")
    da = _matmul(dffn, w_fo_f, "nt", F32, "d_act")
    dhg, dhu = _swiglu_bwd(da, hh, ts, tb)
    dh = jnp.concatenate([dhg, dhu], axis=1)
    gw_fi = _matmul(u2, dh, "tn", BF16, "grad_w_ffn_in")
    du2 = _matmul(dh, w_fi_f, "nt", F32, "d_u2")
    dxa, dmix, vec1 = _ln1_bwd(du2, dx1a, xhat1, rstd1, mix, mod, ln1_g, ln1_b, ts)
    gw_o = _matmul(merged, dmix, "tn", BF16, "grad_w_o")
    dmerged = _matmul(dmix, w_o_f, "nt", F32, "d_merged")
    dya, dyb, dga, dgb = _merge_bwd(dmerged, proj, ya, yb, blk_ga, blk_gb, ts)
    gw_ob = _matmul(cbc, dyb, "tn", BF16, "grad_w_o_b")
    dcbc = _matmul(dyb, w_ob_f, "nt", F32, "d_conv")
    dcb, dcc, dcx, dwconv = _conv_bwd(dcbc, proj, w_conv_full, blk_b, blk_c, blk_x)
    gw_oa = _matmul(attn, dya, "tn", BF16, "grad_w_o_a")
    dattn = _matmul(dya, w_oa_f, "nt", F32, "d_attn")
    dqc, dkc, dvh = _attn_bwd(qc, kc, vh, dattn, attn, lse, T)
    dq, dkv, dkr = _qk_bwd(dqc, dkc, dvh, cos_q, sin_q, cos_k, sin_k, ts)
    gw_qb_p = _matmul(qn, dq, "tn", BF16, "grad_w_q_b")
    dqn = _matmul(dq, wq_p, "nt", F32, "d_qn")
    gw_kvb_p = _matmul(kvn, dkv, "tn", BF16, "grad_w_kv_b")
    dkvn = _matmul(dkv, wkv_p, "nt", F32, "d_kvn")
    dqa, dgq = _rms_bwd(dqn, proj, g_q_a, 0, Lq, ts, "rms_q_bwd")
    dkva, dgkv = _rms_bwd(dkvn, proj, g_kv_a, 1, Lkv, ts, "rms_kv_bwd")
    dproj = jnp.concatenate([dqa, dkva, dkr, dcb, dcc, dcx, dga, dgb], axis=1)
    gw_in_p = _matmul(u, dproj, "tn", BF16, "grad_w_in")
    du = _matmul(dproj, w_in_p, "nt", F32, "d_u")
    grad_x, vec0 = _grad_x(du, dxa, x2, mod, ts)

    nq, nk = H * QK_NOPE, H * QK_NOPE
    gw_in = jnp.concatenate([gw_in_p[:, :front], gw_in_p[:, front_pad:]], axis=1)
    gw_qb = jnp.concatenate([gw_qb_p[:, :nq].reshape(Lq, H, QK_NOPE), gw_qb_p[:, nq:].reshape(Lq, H, QK_ROPE)],
                            axis=2).reshape(Lq, -1)
    gw_kvb = jnp.concatenate([gw_kvb_p[:, :nk].reshape(Lkv, H, QK_NOPE), gw_kvb_p[:, nk:].reshape(Lkv, H, V_HEAD)],
                             axis=2).reshape(Lkv, -1)
    parts = _pack([_cols_to_shards(gw_in), _cols_to_shards(gw_qb), _cols_to_shards(gw_kvb),
                   gw_oa.reshape(N_DEV, -1), gw_ob.reshape(N_DEV, -1), gw_o.reshape(N_DEV, -1),
                   _cols_to_shards(gw_fi), gw_fo.reshape(N_DEV, -1)], BF16)
    got = _exchange_in_chip(parts)
    chip_parts = _pair_sum(parts, got, lax.axis_index("c").astype(jnp.int32).reshape(1))
    grads = _unpack(_chip_sum(_exchange_across_chips(chip_parts)), sizes)
    grads = [g.reshape(w.shape[1:]) for g, w in zip(grads, big)]

    n_mod = 6 * D // cw
    dmod = jnp.concatenate([vec0[0], vec0[1], vec1[4], vec1[0], vec1[1], vec2[2]])
    small = jnp.concatenate([dmod, dgq[0], dgkv[0], vec1[2], vec1[3], vec2[0], vec2[1], dwconv[:CONV_K].reshape(-1)])
    n_small = small.shape[0]
    nch = _round_up(n_small, cw) // cw
    payload = jnp.pad(small, (0, nch * cw - n_small)).reshape(nch, 1, cw)
    summed, g_w_ada = _ada_bwd(payload, cact_all.T, n_mod)
    summed = summed.reshape(-1)
    offs = [0, 6 * D, 6 * D + Lq, 6 * D + Lq + Lkv]
    offs += [offs[-1] + D * k for k in range(1, 5)]
    g_b_ada = summed[offs[0]:offs[1]].reshape(1, -1)
    g_gq = summed[offs[1]:offs[2]].reshape(1, -1)
    g_gkv = summed[offs[2]:offs[3]].reshape(1, -1)
    g_ln1g, g_ln1b, g_ln2g, g_ln2b = [summed[offs[3 + k]:offs[4 + k]].reshape(1, -1) for k in range(4)]
    wc = w_conv.shape[2]
    g_wconv = lax.dynamic_slice(summed[offs[7]:offs[7] + CONV_K * D].reshape(CONV_K, D), (0, me * wc), (CONV_K, wc))

    names = ["w_ada", "b_ada", "w_in", "g_q_a", "w_q_b", "g_kv_a", "w_kv_b", "w_o_a", "w_conv", "w_o_b", "w_o",
             "ln1_g", "ln1_b", "w_ffn_in", "w_ffn_out", "ln2_g", "ln2_b"]
    weights = [w_ada, b_ada, w_in, g_q_a, w_q_b, g_kv_a, w_kv_b, w_o_a, w_conv, w_o_b, w_o, ln1_g, ln1_b,
               w_ffn_in, w_ffn_out, ln2_g, ln2_b]
    moms = [m_w_ada, m_b_ada, m_w_in, m_g_q_a, m_w_q_b, m_g_kv_a, m_w_kv_b, m_w_o_a, m_w_conv, m_w_o_b, m_w_o,
            m_ln1_g, m_ln1_b, m_w_ffn_in, m_w_ffn_out, m_ln2_g, m_ln2_b]
    vels = [v_w_ada, v_b_ada, v_w_in, v_g_q_a, v_w_q_b, v_g_kv_a, v_w_kv_b, v_w_o_a, v_w_conv, v_w_o_b, v_w_o,
            v_ln1_g, v_ln1_b, v_w_ffn_in, v_w_ffn_out, v_ln2_g, v_ln2_b]
    g_in_s, g_qb_s, g_kvb_s, g_oa_s, g_ob_s, g_o_s, g_fi_s, g_fo_s = grads
    grad_list = [g_w_ada, g_b_ada, g_in_s, g_gq, g_qb_s, g_gkv, g_kvb_s, g_oa_s, g_wconv, g_ob_s, g_o_s,
                 g_ln1g, g_ln1b, g_fi_s, g_fo_s, g_ln2g, g_ln2b]
    out_g, out_d, out_m, out_v = [], [], [], []
    for nm, w, g, m, v in zip(names, weights, grad_list, moms, vels):
        shp = w.shape
        w2 = w.reshape(shp[-2], shp[-1]) if w.ndim == 3 else w
        g2 = g.reshape(w2.shape)
        d, nm_, nv_ = _adamw(w2, g2, m.reshape(w2.shape), v.reshape(w2.shape), "adamw_" + nm)
        out_g.append(g2.reshape(shp))
        out_d.append(d.reshape(shp))
        out_m.append(nm_.reshape(shp))
        out_v.append(nv_.reshape(shp))
    return (loss, grad_x.reshape(x.shape), *out_g, *out_d, *out_m, *out_v)
```

```python
import functools

import jax
import jax.numpy as jnp
from jax import lax
from jax.experimental import pallas as pl
from jax.experimental.pallas import tpu as pltpu

F32 = jnp.float32
BF16 = jnp.bfloat16
MESH_ID = pl.DeviceIdType.MESH
AXES = ("x", "y", "c")
N_DEV = 8

CHUNK = 64
QK_NOPE = 128
QK_ROPE = 64
V_HEAD = 128
QK_CAT = QK_NOPE + QK_ROPE
ROPE_THETA = 10000.0
ATTN_SCALE = (QK_NOPE + QK_ROPE) ** -0.5
CONV_K = 3
DEEPNORM_ALPHA = 2.0 ** 0.25
LN_EPS = 1e-5
RMS_EPS = 1e-6
NEG_INF = -1e30

ADAM_LR = 0.001
ADAM_B1 = 0.9
ADAM_B2 = 0.999
ADAM_EPS = 1e-08
ADAM_WD = 0.01
ADAM_STEP = 10

LANE = 128
COL_BLOCK = 256
PACK_COLS = 512
PACK_ROW_ALIGN = 16
PACK_ROW_BLOCK = 512
PACK_TILE_ROWS = 2048
VMEM_LIMIT = 48 * 1024 * 1024


def _round_up(n, m):
    return (n + m - 1) // m * m


def _tile(n, pref, align=LANE):
    best = None
    t = align
    while t <= min(n, pref):
        if n % t == 0:
            best = t
        t += align
    return best if best is not None else n


def _cparams(sem=None):
    return pltpu.CompilerParams(dimension_semantics=sem, vmem_limit_bytes=VMEM_LIMIT)


def _sigmoid(x):
    return 1.0 / (1.0 + jnp.exp(-x))


def _matmul(a, b, mode, out_dtype, name, tm=1024, tn=1024, tk=512, deps=()):
    if mode == "nn":
        (M, K), (K2, N) = a.shape, b.shape
    elif mode == "nt":
        (M, K), (N, K2) = a.shape, b.shape
    else:
        (K, M), (K2, N) = a.shape, b.shape
    assert K == K2, (a.shape, b.shape, mode)
    tm, tn, tk = _tile(M, tm), _tile(N, tn), _tile(K, tk)
    nk = K // tk
    if mode == "nn":
        a_spec = pl.BlockSpec((tm, tk), lambda i, j, k: (i, k))
        b_spec = pl.BlockSpec((tk, tn), lambda i, j, k: (k, j))
        dims = (((1,), (0,)), ((), ()))
    elif mode == "nt":
        a_spec = pl.BlockSpec((tm, tk), lambda i, j, k: (i, k))
        b_spec = pl.BlockSpec((tn, tk), lambda i, j, k: (j, k))
        dims = (((1,), (1,)), ((), ()))
    else:
        a_spec = pl.BlockSpec((tk, tm), lambda i, j, k: (k, i))
        b_spec = pl.BlockSpec((tk, tn), lambda i, j, k: (k, j))
        dims = (((0,), (0,)), ((), ()))

    def body(a_ref, b_ref, *rest):
        o_ref, acc_ref = rest[-2:]
        k = pl.program_id(2)

        @pl.when(k == 0)
        def _():
            acc_ref[...] = jnp.zeros_like(acc_ref)

        acc_ref[...] += lax.dot_general(a_ref[...].astype(BF16), b_ref[...].astype(BF16), dims,
                                        preferred_element_type=F32)

        @pl.when(k == nk - 1)
        def _():
            o_ref[...] = acc_ref[...].astype(o_ref.dtype)

    return pl.pallas_call(
        body, name=name, grid=(M // tm, N // tn, nk),
        in_specs=[a_spec, b_spec] + [ANY_SPEC] * len(deps),
        out_specs=pl.BlockSpec((tm, tn), lambda i, j, k: (i, j)),
        out_shape=jax.ShapeDtypeStruct((M, N), out_dtype),
        scratch_shapes=[pltpu.VMEM((tm, tn), F32)],
        compiler_params=_cparams(("parallel", "parallel", "arbitrary")),
    )(a, b, *deps)


def _modulate_in(x, mod, ts):
    S, D = x.shape

    def body(x_ref, mod_ref, u_ref):
        u_ref[...] = (x_ref[...] * (1.0 + mod_ref[1:2, :]) + mod_ref[0:1, :]).astype(BF16)

    return pl.pallas_call(
        body, name="modulate_in", grid=(S // ts,),
        in_specs=[pl.BlockSpec((ts, D), lambda i: (i, 0)), pl.BlockSpec((6, D), lambda i: (0, 0))],
        out_specs=pl.BlockSpec((ts, D), lambda i: (i, 0)),
        out_shape=jax.ShapeDtypeStruct((S, D), BF16),
        compiler_params=_cparams(("parallel",)),
    )(x, mod)


def _rms_fwd(proj, g, blk, L, ts, name):
    S = proj.shape[0]

    def body(a_ref, g_ref, y_ref):
        a = a_ref[...]
        r = lax.rsqrt(jnp.mean(a * a, axis=-1, keepdims=True) + RMS_EPS)
        y_ref[...] = (a * r * g_ref[...]).astype(BF16)

    return pl.pallas_call(
        body, name=name, grid=(S // ts,),
        in_specs=[pl.BlockSpec((ts, L), lambda i: (i, blk)), pl.BlockSpec((1, L), lambda i: (0, 0))],
        out_specs=pl.BlockSpec((ts, L), lambda i: (i, 0)),
        out_shape=jax.ShapeDtypeStruct((S, L), BF16),
        compiler_params=_cparams(("parallel",)),
    )(proj, g)


def _rotate_half_pairs(x, sign):
    w = x.shape[-1]
    lane = lax.broadcasted_iota(jnp.int32, x.shape, x.ndim - 1)
    first = (lane % QK_ROPE) < (QK_ROPE // 2)
    from_right = pltpu.roll(x, w - QK_ROPE // 2, axis=x.ndim - 1)
    from_left = pltpu.roll(x, QK_ROPE // 2, axis=x.ndim - 1)
    return jnp.where(first, -sign * from_right, sign * from_left)


def _qk_prep(q, kv, proj, kr_blk, cos_q, sin_q, cos_k, sin_k, H, ts):
    S = q.shape[0]
    nope_w, rope_w = H * QK_NOPE, H * QK_ROPE

    def body(q_ref, kv_ref, kr_ref, cq_ref, sq_ref, ck_ref, sk_ref, qc_ref, kc_ref, vh_ref):
        qr = q_ref[:, nope_w:]
        qr = qr * cq_ref[...] + _rotate_half_pairs(qr, 1.0) * sq_ref[...]
        kr = kr_ref[...]
        kr = kr * ck_ref[...] + _rotate_half_pairs(kr, 1.0) * sk_ref[...]
        kr = kr[:, :QK_ROPE].astype(BF16)
        for h in range(H):
            qc_ref[h, :, 0:QK_NOPE] = q_ref[:, h * QK_NOPE:(h + 1) * QK_NOPE].astype(BF16)
            qc_ref[h, :, QK_NOPE:QK_CAT] = qr[:, h * QK_ROPE:(h + 1) * QK_ROPE].astype(BF16)
            kc_ref[h, :, 0:QK_NOPE] = kv_ref[:, h * QK_NOPE:(h + 1) * QK_NOPE].astype(BF16)
            kc_ref[h, :, QK_NOPE:QK_CAT] = kr
            vh_ref[h, :, :] = kv_ref[:, nope_w + h * V_HEAD:nope_w + (h + 1) * V_HEAD].astype(BF16)

    row = lambda w: pl.BlockSpec((ts, w), lambda i: (i, 0))
    return pl.pallas_call(
        body, name="qk_prep", grid=(S // ts,),
        in_specs=[row(nope_w + rope_w), row(nope_w + H * V_HEAD),
                  pl.BlockSpec((ts, COL_BLOCK), lambda i: (i, kr_blk)),
                  row(rope_w), row(rope_w), row(COL_BLOCK), row(COL_BLOCK)],
        out_specs=[pl.BlockSpec((H, ts, QK_CAT), lambda i: (0, i, 0)),
                   pl.BlockSpec((H, ts, QK_CAT), lambda i: (0, i, 0)),
                   pl.BlockSpec((H, ts, V_HEAD), lambda i: (0, i, 0))],
        out_shape=[jax.ShapeDtypeStruct((H, S, QK_CAT), BF16), jax.ShapeDtypeStruct((H, S, QK_CAT), BF16),
                   jax.ShapeDtypeStruct((H, S, V_HEAD), BF16)],
        compiler_params=_cparams(("parallel",)),
    )(q, kv, proj, cos_q, sin_q, cos_k, sin_k)


NT_DIMS = (((1,), (1,)), ((), ()))
TN_DIMS = (((0,), (0,)), ((), ()))


def _diag_mask(T):
    rows = lax.broadcasted_iota(jnp.int32, (T, T), 0) // CHUNK
    cols = lax.broadcasted_iota(jnp.int32, (T, T), 1) // CHUNK
    return cols <= rows


def _attn_fwd(qc, kc, vh, T):
    H, S, _ = qc.shape
    n = S // T

    def body(q_ref, k_ref, v_ref, o_ref, lse_ref, m_ref, l_ref, acc_ref):
        i = pl.program_id(1)
        q = q_ref[0]
        m_ref[...] = jnp.full_like(m_ref, NEG_INF)
        l_ref[...] = jnp.zeros_like(l_ref)
        acc_ref[...] = jnp.zeros_like(acc_ref)

        def step(j, masked):
            rows = pl.ds(pl.multiple_of(j * T, T), T)
            s = lax.dot_general(q, k_ref[0, rows, :], NT_DIMS, preferred_element_type=F32) * ATTN_SCALE
            if masked:
                s = jnp.where(_diag_mask(T), s, NEG_INF)
            m_old = m_ref[...]
            m_new = jnp.maximum(m_old, jnp.max(s, axis=-1, keepdims=True))
            alpha = jnp.exp(m_old - m_new)
            p = jnp.exp(s - m_new)
            l_ref[...] = alpha * l_ref[...] + jnp.sum(p, axis=-1, keepdims=True)
            acc_ref[...] = alpha * acc_ref[...] + jnp.dot(p.astype(BF16), v_ref[0, rows, :],
                                                          preferred_element_type=F32)
            m_ref[...] = m_new

        def below(j, carry):
            step(j, False)
            return carry

        lax.fori_loop(0, i, below, 0)
        step(i, True)
        o_ref[...] = acc_ref[...] / l_ref[...]
        lse_ref[0] = m_ref[...] + jnp.log(l_ref[...])

    return pl.pallas_call(
        body, name="attn_fwd", grid=(H, n),
        in_specs=[pl.BlockSpec((1, T, QK_CAT), lambda h, i: (h, i, 0)),
                  pl.BlockSpec((1, S, QK_CAT), lambda h, i: (h, 0, 0)),
                  pl.BlockSpec((1, S, V_HEAD), lambda h, i: (h, 0, 0))],
        out_specs=[pl.BlockSpec((T, V_HEAD), lambda h, i: (i, h)),
                   pl.BlockSpec((1, T, 1), lambda h, i: (h, i, 0))],
        out_shape=[jax.ShapeDtypeStruct((S, H * V_HEAD), F32), jax.ShapeDtypeStruct((H, S, 1), F32)],
        scratch_shapes=[pltpu.VMEM((T, 1), F32), pltpu.VMEM((T, 1), F32), pltpu.VMEM((T, V_HEAD), F32)],
        compiler_params=_cparams(("parallel", "arbitrary")),
    )(qc, kc, vh)


def _shift_rows(z, k):
    if k == 0:
        return z
    n = z.shape[0]
    row = lax.broadcasted_iota(jnp.int32, z.shape, 0)
    if k > 0:
        return jnp.where(row >= k, pltpu.roll(z, k, axis=0), 0.0)
    return jnp.where(row < n + k, pltpu.roll(z, n + k, axis=0), 0.0)


def _conv_fwd(proj, w_conv, blk_b, blk_c, blk_x):
    S = proj.shape[0]
    D = w_conv.shape[1]
    nb = D // COL_BLOCK

    def body(cb_ref, cc_ref, cx_ref, w_ref, o_ref):
        z = cc_ref[...] * cx_ref[...]
        conv = w_ref[2:3, :] * z + w_ref[1:2, :] * _shift_rows(z, 1) + w_ref[0:1, :] * _shift_rows(z, 2)
        o_ref[...] = (cb_ref[...] * conv).astype(BF16)

    col = lambda off: pl.BlockSpec((S, COL_BLOCK), lambda j: (0, off + j))
    return pl.pallas_call(
        body, name="conv_fwd", grid=(nb,),
        in_specs=[col(blk_b), col(blk_c), col(blk_x), pl.BlockSpec((CONV_K, COL_BLOCK), lambda j: (0, j))],
        out_specs=pl.BlockSpec((S, COL_BLOCK), lambda j: (0, j)),
        out_shape=jax.ShapeDtypeStruct((S, D), BF16),
        compiler_params=_cparams(("parallel",)),
    )(proj, proj, proj, w_conv)


def _merge_fwd(proj, ya, yb, blk_ga, blk_gb, ts):
    S, D = ya.shape
    nb = D // COL_BLOCK

    def body(ga_ref, gb_ref, ya_ref, yb_ref, o_ref):
        o_ref[...] = (_sigmoid(ga_ref[...]) * ya_ref[...] + _sigmoid(gb_ref[...]) * yb_ref[...]).astype(BF16)

    blk = lambda off: pl.BlockSpec((ts, COL_BLOCK), lambda i, j: (i, off + j))
    return pl.pallas_call(
        body, name="merge_fwd", grid=(S // ts, nb),
        in_specs=[blk(blk_ga), blk(blk_gb), blk(0), blk(0)],
        out_specs=blk(0),
        out_shape=jax.ShapeDtypeStruct((S, D), BF16),
        compiler_params=_cparams(("parallel", "parallel")),
    )(proj, proj, ya, yb)


def _ln1_fwd(x, mix, mod, g, b, ts):
    S, D = x.shape

    def body(x_ref, mix_ref, mod_ref, g_ref, b_ref, xhat_ref, rstd_ref, u2_ref):
        r = DEEPNORM_ALPHA * x_ref[...] + mod_ref[2:3, :] * mix_ref[...]
        mu = jnp.mean(r, axis=-1, keepdims=True)
        d = r - mu
        rstd = lax.rsqrt(jnp.mean(d * d, axis=-1, keepdims=True) + LN_EPS)
        xhat = d * rstd
        xhat_ref[...] = xhat
        rstd_ref[...] = rstd
        x1 = xhat * g_ref[...] + b_ref[...]
        u2_ref[...] = (x1 * (1.0 + mod_ref[4:5, :]) + mod_ref[3:4, :]).astype(BF16)

    row = pl.BlockSpec((ts, D), lambda i: (i, 0))
    vec = lambda r: pl.BlockSpec((r, D), lambda i: (0, 0))
    return pl.pallas_call(
        body, name="ln1_fwd", grid=(S // ts,),
        in_specs=[row, row, vec(6), vec(1), vec(1)],
        out_specs=[row, pl.BlockSpec((ts, 1), lambda i: (i, 0)), row],
        out_shape=[jax.ShapeDtypeStruct((S, D), F32), jax.ShapeDtypeStruct((S, 1), F32),
                   jax.ShapeDtypeStruct((S, D), BF16)],
        compiler_params=_cparams(("parallel",)),
    )(x, mix, mod, g, b)


def _swiglu_fwd(h, ts, tb):
    S, F2 = h.shape
    F = F2 // 2
    nb = F // tb

    def body(hg_ref, hu_ref, a_ref):
        hg = hg_ref[...]
        a_ref[...] = (hg * _sigmoid(hg) * hu_ref[...]).astype(BF16)

    return pl.pallas_call(
        body, name="swiglu_fwd", grid=(S // ts, nb),
        in_specs=[pl.BlockSpec((ts, tb), lambda i, j: (i, j)), pl.BlockSpec((ts, tb), lambda i, j: (i, j + nb))],
        out_specs=pl.BlockSpec((ts, tb), lambda i, j: (i, j)),
        out_shape=jax.ShapeDtypeStruct((S, F), BF16),
        compiler_params=_cparams(("parallel", "parallel")),
    )(h, h)


def _ln2_loss(xhat1, ffn, tgt, mod, g1, b1, g2, b2, ts):
    S, D = xhat1.shape

    def body(xh_ref, ffn_ref, t_ref, mod_ref, g1_ref, b1_ref, g2_ref, b2_ref, loss_ref, dffn_ref, dx1_ref, vec_ref):
        i = pl.program_id(0)

        @pl.when(i == 0)
        def _():
            loss_ref[...] = jnp.zeros_like(loss_ref)
            vec_ref[...] = jnp.zeros_like(vec_ref)

        x1 = xh_ref[...] * g1_ref[...] + b1_ref[...]
        ffn = ffn_ref[...]
        r = DEEPNORM_ALPHA * x1 + mod_ref[5:6, :] * ffn
        mu = jnp.mean(r, axis=-1, keepdims=True)
        d = r - mu
        rstd = lax.rsqrt(jnp.mean(d * d, axis=-1, keepdims=True) + LN_EPS)
        xhat = d * rstd
        e = xhat * g2_ref[...] + b2_ref[...] - t_ref[...]
        loss_ref[...] += 0.5 * jnp.sum(jnp.mean(e * e, axis=-1, keepdims=True))
        dy = e * (1.0 / D)
        dxhat = dy * g2_ref[...]
        dr = rstd * (dxhat - jnp.mean(dxhat, axis=-1, keepdims=True)
                     - xhat * jnp.mean(dxhat * xhat, axis=-1, keepdims=True))
        dffn_ref[...] = (dr * mod_ref[5:6, :]).astype(BF16)
        dx1_ref[...] = DEEPNORM_ALPHA * dr
        vec_ref[0:1, :] += jnp.sum(dy * xhat, axis=0, keepdims=True)
        vec_ref[1:2, :] += jnp.sum(dy, axis=0, keepdims=True)
        vec_ref[2:3, :] += jnp.sum(dr * ffn, axis=0, keepdims=True)

    row = pl.BlockSpec((ts, D), lambda i: (i, 0))
    vec = lambda r: pl.BlockSpec((r, D), lambda i: (0, 0))
    return pl.pallas_call(
        body, name="ln2_loss", grid=(S // ts,),
        in_specs=[row, row, row, vec(6), vec(1), vec(1), vec(1), vec(1)],
        out_specs=[pl.BlockSpec((1, LANE), lambda i: (0, 0)), row, row, vec(8)],
        out_shape=[jax.ShapeDtypeStruct((1, LANE), F32), jax.ShapeDtypeStruct((S, D), BF16),
                   jax.ShapeDtypeStruct((S, D), F32), jax.ShapeDtypeStruct((8, D), F32)],
        compiler_params=_cparams(("arbitrary",)),
    )(xhat1, ffn, tgt, mod, g1, b1, g2, b2)


def _swiglu_bwd(da, h, ts, tb):
    S, F2 = h.shape
    nb = (F2 // 2) // tb

    def body(da_ref, hg_ref, hu_ref, dg_ref, du_ref):
        hg, da = hg_ref[...], da_ref[...]
        sg = _sigmoid(hg)
        dg_ref[...] = (da * hu_ref[...] * (sg * (1.0 + hg * (1.0 - sg)))).astype(BF16)
        du_ref[...] = (da * hg * sg).astype(BF16)

    lo = pl.BlockSpec((ts, tb), lambda i, j: (i, j))
    hi = pl.BlockSpec((ts, tb), lambda i, j: (i, j + nb))
    dg, du = pl.pallas_call(
        body, name="swiglu_bwd", grid=(S // ts, nb),
        in_specs=[lo, lo, hi],
        out_specs=[lo, lo],
        out_shape=[jax.ShapeDtypeStruct((S, F2 // 2), BF16), jax.ShapeDtypeStruct((S, F2 // 2), BF16)],
        compiler_params=_cparams(("parallel", "parallel")),
    )(da, h, h)
    return dg, du


def _ln1_bwd(du2, dx1a, xhat1, rstd1, mix, mod, g1, b1, ts):
    S, D = xhat1.shape

    def body(du2_ref, dx1a_ref, xh_ref, rstd_ref, mix_ref, mod_ref, g_ref, b_ref, dxa_ref, dmix_ref, vec_ref):
        i = pl.program_id(0)

        @pl.when(i == 0)
        def _():
            vec_ref[...] = jnp.zeros_like(vec_ref)

        xhat, du2, mix = xh_ref[...], du2_ref[...], mix_ref[...]
        x1 = xhat * g_ref[...] + b_ref[...]
        dx1 = dx1a_ref[...] + du2 * (1.0 + mod_ref[4:5, :])
        dxhat = dx1 * g_ref[...]
        dr = rstd_ref[...] * (dxhat - jnp.mean(dxhat, axis=-1, keepdims=True)
                              - xhat * jnp.mean(dxhat * xhat, axis=-1, keepdims=True))
        dxa_ref[...] = DEEPNORM_ALPHA * dr
        dmix_ref[...] = (dr * mod_ref[2:3, :]).astype(BF16)
        vec_ref[0:1, :] += jnp.sum(du2, axis=0, keepdims=True)
        vec_ref[1:2, :] += jnp.sum(du2 * x1, axis=0, keepdims=True)
        vec_ref[2:3, :] += jnp.sum(dx1 * xhat, axis=0, keepdims=True)
        vec_ref[3:4, :] += jnp.sum(dx1, axis=0, keepdims=True)
        vec_ref[4:5, :] += jnp.sum(dr * mix, axis=0, keepdims=True)

    row = pl.BlockSpec((ts, D), lambda i: (i, 0))
    vec = lambda r: pl.BlockSpec((r, D), lambda i: (0, 0))
    return pl.pallas_call(
        body, name="ln1_bwd", grid=(S // ts,),
        in_specs=[row, row, row, pl.BlockSpec((ts, 1), lambda i: (i, 0)), row, vec(6), vec(1), vec(1)],
        out_specs=[row, row, vec(8)],
        out_shape=[jax.ShapeDtypeStruct((S, D), F32), jax.ShapeDtypeStruct((S, D), BF16),
                   jax.ShapeDtypeStruct((8, D), F32)],
        compiler_params=_cparams(("arbitrary",)),
    )(du2, dx1a, xhat1, rstd1, mix, mod, g1, b1)


def _merge_bwd(dmerged, proj, ya, yb, blk_ga, blk_gb, ts):
    S, D = ya.shape
    nb = D // COL_BLOCK

    def body(dm_ref, ga_ref, gb_ref, ya_ref, yb_ref, dya_ref, dyb_ref, dga_ref, dgb_ref):
        dm = dm_ref[...]
        sa, sb = _sigmoid(ga_ref[...]), _sigmoid(gb_ref[...])
        dya_ref[...] = (dm * sa).astype(BF16)
        dyb_ref[...] = (dm * sb).astype(BF16)
        dga_ref[...] = (dm * ya_ref[...] * sa * (1.0 - sa)).astype(BF16)
        dgb_ref[...] = (dm * yb_ref[...] * sb * (1.0 - sb)).astype(BF16)

    blk = lambda off: pl.BlockSpec((ts, COL_BLOCK), lambda i, j: (i, off + j))
    out = jax.ShapeDtypeStruct((S, D), BF16)
    return pl.pallas_call(
        body, name="merge_bwd", grid=(S // ts, nb),
        in_specs=[blk(0), blk(blk_ga), blk(blk_gb), blk(0), blk(0)],
        out_specs=[blk(0)] * 4,
        out_shape=[out] * 4,
        compiler_params=_cparams(("parallel", "parallel")),
    )(dmerged, proj, proj, ya, yb)


def _conv_bwd(dcbc, proj, w_conv, blk_b, blk_c, blk_x):
    S = proj.shape[0]
    D = w_conv.shape[1]
    nb = D // COL_BLOCK

    def body(d_ref, cb_ref, cc_ref, cx_ref, w_ref, dcb_ref, dcc_ref, dcx_ref, dw_ref):
        d, cc, cx = d_ref[...], cc_ref[...], cx_ref[...]
        z = cc * cx
        z1, z2 = _shift_rows(z, 1), _shift_rows(z, 2)
        conv = w_ref[2:3, :] * z + w_ref[1:2, :] * z1 + w_ref[0:1, :] * z2
        dcb_ref[...] = (d * conv).astype(BF16)
        dconv = d * cb_ref[...]
        dz = w_ref[2:3, :] * dconv + w_ref[1:2, :] * _shift_rows(dconv, -1) + w_ref[0:1, :] * _shift_rows(dconv, -2)
        dcc_ref[...] = (dz * cx).astype(BF16)
        dcx_ref[...] = (dz * cc).astype(BF16)
        dw_ref[...] = jnp.zeros_like(dw_ref)
        dw_ref[0:1, :] = jnp.sum(dconv * z2, axis=0, keepdims=True)
        dw_ref[1:2, :] = jnp.sum(dconv * z1, axis=0, keepdims=True)
        dw_ref[2:3, :] = jnp.sum(dconv * z, axis=0, keepdims=True)

    col = lambda off: pl.BlockSpec((S, COL_BLOCK), lambda j: (0, off + j))
    out = jax.ShapeDtypeStruct((S, D), BF16)
    return pl.pallas_call(
        body, name="conv_bwd", grid=(nb,),
        in_specs=[col(0), col(blk_b), col(blk_c), col(blk_x), pl.BlockSpec((CONV_K, COL_BLOCK), lambda j: (0, j))],
        out_specs=[col(0), col(0), col(0), pl.BlockSpec((8, COL_BLOCK), lambda j: (0, j))],
        out_shape=[out, out, out, jax.ShapeDtypeStruct((8, D), F32)],
        compiler_params=_cparams(("parallel",)),
    )(dcbc, proj, proj, proj, w_conv)


def _attn_bwd(qc, kc, vh, do, o, lse, T):
    H, S, _ = qc.shape
    n = S // T

    def body(q_ref, k_ref, v_ref, do_ref, o_ref, lse_ref, dq_ref, dk_ref, dv_ref, d_ref, dk_acc, dv_acc):
        j = pl.program_id(1)

        @pl.when(j == 0)
        def _():
            dq_ref[...] = jnp.zeros_like(dq_ref)
            d_ref[...] = jnp.sum(do_ref[...] * o_ref[...], axis=-1, keepdims=True)

        dk_acc[...] = jnp.zeros_like(dk_acc)
        dv_acc[...] = jnp.zeros_like(dv_acc)
        k, v = k_ref[0], v_ref[0]

        def step(i, masked):
            rows = pl.ds(pl.multiple_of(i * T, T), T)
            q = q_ref[0, rows, :]
            do = do_ref[rows, :].astype(BF16)
            s = lax.dot_general(q, k, NT_DIMS, preferred_element_type=F32) * ATTN_SCALE
            if masked:
                s = jnp.where(_diag_mask(T), s, NEG_INF)
            p = jnp.exp(s - lse_ref[0, rows, :])
            dv_acc[...] += lax.dot_general(p.astype(BF16), do, TN_DIMS, preferred_element_type=F32)
            dp = lax.dot_general(do, v, NT_DIMS, preferred_element_type=F32)
            ds = (p * (dp - d_ref[rows, :]) * ATTN_SCALE).astype(BF16)
            dk_acc[...] += lax.dot_general(ds, q, TN_DIMS, preferred_element_type=F32)
            dq_ref[0, rows, :] += jnp.dot(ds, k, preferred_element_type=F32)

        def above(i, carry):
            step(i, False)
            return carry

        step(j, True)
        lax.fori_loop(j + 1, n, above, 0)
        dk_ref[0] = dk_acc[...]
        dv_ref[0] = dv_acc[...]

    head = lambda w: pl.BlockSpec((1, S, w), lambda h, j: (h, 0, 0))
    blk = lambda w: pl.BlockSpec((1, T, w), lambda h, j: (h, j, 0))
    ospec = pl.BlockSpec((S, V_HEAD), lambda h, j: (0, h))
    return pl.pallas_call(
        body, name="attn_bwd", grid=(H, n),
        in_specs=[head(QK_CAT), blk(QK_CAT), blk(V_HEAD), ospec, ospec, head(1)],
        out_specs=[head(QK_CAT), blk(QK_CAT), blk(V_HEAD)],
        out_shape=[jax.ShapeDtypeStruct((H, S, QK_CAT), F32), jax.ShapeDtypeStruct((H, S, QK_CAT), F32),
                   jax.ShapeDtypeStruct((H, S, V_HEAD), F32)],
        scratch_shapes=[pltpu.VMEM((S, 1), F32), pltpu.VMEM((T, QK_CAT), F32), pltpu.VMEM((T, V_HEAD), F32)],
        compiler_params=_cparams(("parallel", "arbitrary")),
    )(qc, kc, vh, do, o, lse)


def _qk_bwd(dqc, dkc, dvh, cos_q, sin_q, cos_k, sin_k, ts):
    H, S, _ = dqc.shape
    nope_w, rope_w = H * QK_NOPE, H * QK_ROPE

    def body(dqc_ref, dkc_ref, dvh_ref, cq_ref, sq_ref, ck_ref, sk_ref, dq_ref, dkv_ref, dkr_ref, qr_buf, kr_buf):
        kr_sum = jnp.zeros((ts, QK_ROPE), F32)
        for h in range(H):
            dq_ref[:, h * QK_NOPE:(h + 1) * QK_NOPE] = dqc_ref[h, :, 0:QK_NOPE].astype(BF16)
            qr_buf[:, h * QK_ROPE:(h + 1) * QK_ROPE] = dqc_ref[h, :, QK_NOPE:QK_CAT]
            dkv_ref[:, h * QK_NOPE:(h + 1) * QK_NOPE] = dkc_ref[h, :, 0:QK_NOPE].astype(BF16)
            dkv_ref[:, nope_w + h * V_HEAD:nope_w + (h + 1) * V_HEAD] = dvh_ref[h].astype(BF16)
            kr_sum = kr_sum + dkc_ref[h, :, QK_NOPE:QK_CAT]
        qr = qr_buf[...]
        dq_ref[:, nope_w:] = (qr * cq_ref[...] + _rotate_half_pairs(qr, -1.0) * sq_ref[...]).astype(BF16)
        kr_buf[...] = jnp.zeros_like(kr_buf)
        kr_buf[:, 0:QK_ROPE] = kr_sum
        kr = kr_buf[...]
        dkr_ref[...] = (kr * ck_ref[...] + _rotate_half_pairs(kr, -1.0) * sk_ref[...]).astype(BF16)

    row = lambda w: pl.BlockSpec((ts, w), lambda i: (i, 0))
    head = lambda w: pl.BlockSpec((H, ts, w), lambda i: (0, i, 0))
    return pl.pallas_call(
        body, name="qk_bwd", grid=(S // ts,),
        in_specs=[head(QK_CAT), head(QK_CAT), head(V_HEAD), row(rope_w), row(rope_w), row(COL_BLOCK), row(COL_BLOCK)],
        out_specs=[row(nope_w + rope_w), row(nope_w + H * V_HEAD), row(COL_BLOCK)],
        out_shape=[jax.ShapeDtypeStruct((S, nope_w + rope_w), BF16), jax.ShapeDtypeStruct((S, nope_w + H * V_HEAD), BF16),
                   jax.ShapeDtypeStruct((S, COL_BLOCK), BF16)],
        scratch_shapes=[pltpu.VMEM((ts, rope_w), F32), pltpu.VMEM((ts, COL_BLOCK), F32)],
        compiler_params=_cparams(("parallel",)),
    )(dqc, dkc, dvh, cos_q, sin_q, cos_k, sin_k)


def _rms_bwd(dy, proj, g, blk, L, ts, name):
    S = proj.shape[0]

    def body(dy_ref, a_ref, g_ref, da_ref, dg_ref):
        i = pl.program_id(0)

        @pl.when(i == 0)
        def _():
            dg_ref[...] = jnp.zeros_like(dg_ref)

        a, dy = a_ref[...], dy_ref[...]
        r = lax.rsqrt(jnp.mean(a * a, axis=-1, keepdims=True) + RMS_EPS)
        dyh = dy * g_ref[...]
        da = r * dyh - a * (r * r * r) * jnp.mean(dyh * a, axis=-1, keepdims=True)
        da_ref[...] = da.astype(BF16)
        dg_ref[0:1, :] += jnp.sum(dy * a * r, axis=0, keepdims=True)

    return pl.pallas_call(
        body, name=name, grid=(S // ts,),
        in_specs=[pl.BlockSpec((ts, L), lambda i: (i, 0)), pl.BlockSpec((ts, L), lambda i: (i, blk)),
                  pl.BlockSpec((1, L), lambda i: (0, 0))],
        out_specs=[pl.BlockSpec((ts, L), lambda i: (i, 0)), pl.BlockSpec((8, L), lambda i: (0, 0))],
        out_shape=[jax.ShapeDtypeStruct((S, L), BF16), jax.ShapeDtypeStruct((8, L), F32)],
        compiler_params=_cparams(("arbitrary",)),
    )(dy, proj, g)


def _grad_x(du, dxa, x, mod, ts):
    S, D = x.shape

    def body(du_ref, dxa_ref, x_ref, mod_ref, dx_ref, vec_ref):
        i = pl.program_id(0)

        @pl.when(i == 0)
        def _():
            vec_ref[...] = jnp.zeros_like(vec_ref)

        du = du_ref[...]
        dx_ref[...] = dxa_ref[...] + du * (1.0 + mod_ref[1:2, :])
        vec_ref[0:1, :] += jnp.sum(du, axis=0, keepdims=True)
        vec_ref[1:2, :] += jnp.sum(du * x_ref[...], axis=0, keepdims=True)

    row = pl.BlockSpec((ts, D), lambda i: (i, 0))
    vec = lambda r: pl.BlockSpec((r, D), lambda i: (0, 0))
    return pl.pallas_call(
        body, name="grad_x", grid=(S // ts,),
        in_specs=[row, row, row, vec(6)],
        out_specs=[row, vec(8)],
        out_shape=[jax.ShapeDtypeStruct((S, D), F32), jax.ShapeDtypeStruct((8, D), F32)],
        compiler_params=_cparams(("arbitrary",)),
    )(du, dxa, x, mod)


def _adamw(w, g, m, v, name):
    R, C = w.shape
    tr = _tile(R, max(8, (1 << 19) // C), 8)
    c1 = 1.0 / (1.0 - ADAM_B1 ** ADAM_STEP)
    c2 = 1.0 / (1.0 - ADAM_B2 ** ADAM_STEP)

    def body(w_ref, g_ref, m_ref, v_ref, d_ref, nm_ref, nv_ref):
        g = g_ref[...]
        m = ADAM_B1 * m_ref[...] + (1.0 - ADAM_B1) * g
        v = ADAM_B2 * v_ref[...] + (1.0 - ADAM_B2) * (g * g)
        nm_ref[...] = m
        nv_ref[...] = v
        d_ref[...] = -ADAM_LR * ((m * c1) / (jnp.sqrt(v * c2) + ADAM_EPS) + ADAM_WD * w_ref[...])

    spec = pl.BlockSpec((tr, C), lambda i: (i, 0))
    out = jax.ShapeDtypeStruct((R, C), F32)
    return pl.pallas_call(
        body, name=name, grid=(R // tr,),
        in_specs=[spec] * 4, out_specs=[spec] * 3, out_shape=[out] * 3,
        compiler_params=_cparams(("parallel",)),
    )(w, g, m, v)


def _my_place():
    return lax.axis_index("x"), lax.axis_index("y"), lax.axis_index("c")


def _peer(k):
    x, y, c = _my_place()
    return (x ^ ((k >> 2) & 1), y ^ ((k >> 1) & 1), c ^ (k & 1))


def _linear(place):
    return 4 * place[0] + 2 * place[1] + place[2]


def _ada_fwd(c_row, wconv_row, w_ada, b_row):
    D, CW = w_ada.shape
    WC = wconv_row.shape[-1]

    def body(c_ref, wc_ref, w_ref, b_ref, mod_ref, cact_ref, wcall_ref, send_buf, sems):
        me = _linear(_my_place())
        c = c_ref[0]
        cact_ref[me] = c * _sigmoid(c)
        wcall_ref[me] = wc_ref[0]

        def gather_copy(buf, k, grp):
            return pltpu.make_async_remote_copy(
                src_ref=buf.at[me], dst_ref=buf.at[me], send_sem=sems.at[0, grp, k], recv_sem=sems.at[1, grp, k],
                device_id=_peer(k), device_id_type=MESH_ID)

        def gather_recv(buf, k, grp):
            src = _linear(_peer(k))
            return pltpu.make_async_remote_copy(
                src_ref=buf.at[src], dst_ref=buf.at[src], send_sem=sems.at[0, grp, k], recv_sem=sems.at[1, grp, k],
                device_id=_peer(k), device_id_type=MESH_ID)

        for k in range(1, N_DEV):
            gather_copy(cact_ref, k, 0).start()
            gather_copy(wcall_ref, k, 1).start()
        for k in range(1, N_DEV):
            gather_recv(cact_ref, k, 0).wait_recv()
            gather_recv(wcall_ref, k, 1).wait_recv()
        for k in range(1, N_DEV):
            gather_copy(cact_ref, k, 0).wait_send()
            gather_copy(wcall_ref, k, 1).wait_send()

        cact = jnp.concatenate([cact_ref[b] for b in range(N_DEV)], axis=0)
        mod_all = jnp.dot(cact.astype(BF16), w_ref[...].astype(BF16), preferred_element_type=F32) + b_ref[0]
        for b in range(N_DEV):
            send_buf[b] = mod_all[b:b + 1, :]
        mod_ref[me] = send_buf[me]

        def scatter_copy(k):
            dst = _linear(_peer(k))
            return pltpu.make_async_remote_copy(
                src_ref=send_buf.at[dst], dst_ref=mod_ref.at[me], send_sem=sems.at[0, 2, k], recv_sem=sems.at[1, 2, k],
                device_id=_peer(k), device_id_type=MESH_ID)

        def scatter_recv(k):
            src = _linear(_peer(k))
            return pltpu.make_async_remote_copy(
                src_ref=send_buf.at[src], dst_ref=mod_ref.at[src], send_sem=sems.at[0, 2, k], recv_sem=sems.at[1, 2, k],
                device_id=_peer(k), device_id_type=MESH_ID)

        for k in range(1, N_DEV):
            scatter_copy(k).start()
        for k in range(1, N_DEV):
            scatter_recv(k).wait_recv()
        for k in range(1, N_DEV):
            scatter_copy(k).wait_send()

    vmem = pl.BlockSpec(memory_space=pltpu.VMEM)
    return pl.pallas_call(
        body, name="ada_fwd",
        in_specs=[vmem] * 4, out_specs=[vmem] * 3,
        out_shape=[jax.ShapeDtypeStruct((N_DEV, 1, CW), F32), jax.ShapeDtypeStruct((N_DEV, 1, D), F32),
                   jax.ShapeDtypeStruct((N_DEV, 1, WC), F32)],
        scratch_shapes=[pltpu.VMEM((N_DEV, 1, CW), F32), pltpu.SemaphoreType.DMA((2, 3, N_DEV))],
        compiler_params=pltpu.CompilerParams(vmem_limit_bytes=VMEM_LIMIT),
    )(c_row, wconv_row, w_ada, b_row)


def _ada_bwd(payload, cact_t, n_mod):
    NCH, _, CW = payload.shape
    D = cact_t.shape[0]

    def body(p_ref, ct_ref, sum_ref, gw_ref, all_ref, sems):
        me = _linear(_my_place())
        all_ref[me] = p_ref[...]

        def copy(k, slot):
            return pltpu.make_async_remote_copy(
                src_ref=all_ref.at[slot], dst_ref=all_ref.at[slot], send_sem=sems.at[0, k], recv_sem=sems.at[1, k],
                device_id=_peer(k), device_id_type=MESH_ID)

        for k in range(1, N_DEV):
            copy(k, me).start()
        for k in range(1, N_DEV):
            copy(k, _linear(_peer(k))).wait_recv()
        for k in range(1, N_DEV):
            copy(k, me).wait_send()

        total = all_ref[0]
        for b in range(1, N_DEV):
            total = total + all_ref[b]
        sum_ref[...] = total

        ct = ct_ref[...].astype(BF16).astype(F32)
        gw = jnp.zeros((D, CW), F32)
        for b in range(N_DEV):
            dm = all_ref[b, me].astype(BF16).astype(F32)
            gw = gw + ct[:, b:b + 1] * dm
        gw_ref[...] = gw

    vmem = pl.BlockSpec(memory_space=pltpu.VMEM)
    return pl.pallas_call(
        body, name="ada_bwd",
        in_specs=[vmem, vmem], out_specs=[vmem, vmem],
        out_shape=[jax.ShapeDtypeStruct((NCH, 1, CW), F32), jax.ShapeDtypeStruct((D, CW), F32)],
        scratch_shapes=[pltpu.VMEM((N_DEV, NCH, 1, CW), F32), pltpu.SemaphoreType.DMA((2, N_DEV))],
        compiler_params=pltpu.CompilerParams(vmem_limit_bytes=VMEM_LIMIT),
    )(payload, cact_t)


def _all_gather(shard):
    R, C = shard.shape

    def body(x_ref, out_ref, send_sems, recv_sems, local_sem):
        x, y, c = _my_place()
        me, sibling = (x, y, c), (x, y, 1 - c)
        chips = [(1 - x, y), (x, 1 - y), (1 - x, 1 - y)]

        def copy(k, block, to, src=None):
            slot = out_ref.at[_linear(block)]
            return pltpu.make_async_remote_copy(
                src_ref=slot if src is None else src, dst_ref=slot,
                send_sem=send_sems.at[k], recv_sem=recv_sems.at[k], device_id=to, device_id_type=MESH_ID)

        mine = pltpu.make_async_copy(x_ref, out_ref.at[_linear(me)], local_sem)
        mine.start()
        first = [copy(0, me, sibling, src=x_ref)]
        first += [copy(1 + j, me, (*chip, c), src=x_ref) for j, chip in enumerate(chips)]
        for cp in first:
            cp.start()
        passed = [copy(4 + j, (*chip, c), sibling) for j, chip in enumerate(chips)]
        for j, chip in enumerate(chips):
            copy(1 + j, (*chip, c), me).wait_recv()
            passed[j].start()
        copy(0, sibling, me).wait_recv()
        for j, chip in enumerate(chips):
            copy(4 + j, (*chip, 1 - c), me).wait_recv()
        for cp in first + passed:
            cp.wait_send()
        mine.wait()

    hbm = pl.BlockSpec(memory_space=pltpu.HBM)
    return pl.pallas_call(
        body, name="weight_all_gather",
        in_specs=[hbm], out_specs=hbm,
        out_shape=jax.ShapeDtypeStruct((N_DEV, R, C), shard.dtype),
        scratch_shapes=[pltpu.SemaphoreType.DMA((7,)), pltpu.SemaphoreType.DMA((7,)), pltpu.SemaphoreType.DMA],
    )(shard)


def _exchange_in_chip(parts):
    _, R, C = parts.shape

    def body(p_ref, got_ref, send_sems, recv_sems):
        x, y, c = _my_place()
        sibling = (x, y, 1 - c)
        copies = []
        for q in range(4):
            copies.append(pltpu.make_async_remote_copy(
                src_ref=p_ref.at[2 * q + (1 - c)], dst_ref=got_ref.at[q],
                send_sem=send_sems.at[q], recv_sem=recv_sems.at[q], device_id=sibling, device_id_type=MESH_ID))
        for cp in copies:
            cp.start()
        for cp in copies:
            cp.wait_recv()
        for cp in copies:
            cp.wait_send()

    hbm = pl.BlockSpec(memory_space=pltpu.HBM)
    return pl.pallas_call(
        body, name="grad_exchange_in_chip",
        in_specs=[hbm], out_specs=hbm,
        out_shape=jax.ShapeDtypeStruct((4, R, C), parts.dtype),
        scratch_shapes=[pltpu.SemaphoreType.DMA((4,)), pltpu.SemaphoreType.DMA((4,))],
    )(parts)


def _pair_sum(parts, got, core):
    _, R, C = parts.shape
    tr = _tile(R, PACK_TILE_ROWS, PACK_ROW_ALIGN)

    def body(c_ref, p_ref, g_ref, o_ref):
        o_ref[...] = (p_ref[...].astype(F32) + g_ref[...].astype(F32)).astype(o_ref.dtype)

    return pl.pallas_call(
        body, name="grad_pair_sum",
        grid_spec=pltpu.PrefetchScalarGridSpec(
            num_scalar_prefetch=1, grid=(4, R // tr),
            in_specs=[pl.BlockSpec((1, tr, C), lambda q, i, c_ref: (2 * q + c_ref[0], i, 0)),
                      pl.BlockSpec((1, tr, C), lambda q, i, c_ref: (q, i, 0))],
            out_specs=pl.BlockSpec((1, tr, C), lambda q, i, c_ref: (q, i, 0))),
        out_shape=jax.ShapeDtypeStruct((4, R, C), parts.dtype),
        compiler_params=_cparams(("parallel", "parallel")),
    )(core, parts, got)


HBM_SPEC = pl.BlockSpec(memory_space=pltpu.HBM)
SEM_SPEC = pl.BlockSpec(memory_space=pltpu.SEMAPHORE)
ANY_SPEC = pl.BlockSpec(memory_space=pl.ANY)
SPLIT_EFFECT = pltpu.SideEffectType.DATAFLOW_SIDE_EFFECTING


def _landing_zone(shape, dtype):
    return pltpu.with_memory_space_constraint(lax.empty(shape, dtype), pltpu.HBM)


def _chip_peers():
    x, y, c = _my_place()
    return [(2 * (x ^ (k >> 1)) + (y ^ (k & 1)), (x ^ (k >> 1), y ^ (k & 1), c)) for k in range(1, 4)]


def _scatter_start(chip_parts, after, name):
    shape, dtype = chip_parts.shape, chip_parts.dtype

    def body(p_ref, land_ref, after_ref, send_sems, recv_sems, p_thru, land_thru, token):
        x, y, _ = _my_place()
        my_chip = 2 * x + y
        for k, (slot, dev) in enumerate(_chip_peers()):
            pltpu.make_async_remote_copy(
                src_ref=p_ref.at[slot], dst_ref=land_ref.at[my_chip], send_sem=send_sems.at[k], recv_sem=recv_sems.at[k],
                device_id=dev, device_id_type=MESH_ID).start()
        token[...] = jnp.zeros_like(token)

    return pl.pallas_call(
        body, name=name,
        out_shape=(pltpu.SemaphoreType.DMA((3,)), pltpu.SemaphoreType.DMA((3,)), pltpu.HBM(shape, dtype),
                   pltpu.HBM(shape, dtype), jax.ShapeDtypeStruct((8, LANE), F32)),
        in_specs=(HBM_SPEC, HBM_SPEC, ANY_SPEC),
        out_specs=(SEM_SPEC, SEM_SPEC, HBM_SPEC, HBM_SPEC, pl.BlockSpec(memory_space=pltpu.VMEM)),
        input_output_aliases={0: 2, 1: 3},
        compiler_params=pltpu.CompilerParams(has_side_effects=SPLIT_EFFECT),
    )(pltpu.with_memory_space_constraint(chip_parts, pltpu.HBM), _landing_zone(shape, dtype), after)


def _scatter_wait(send_sems, recv_sems, p_thru, land_thru, after, name):
    def body(p_ref, land_ref, send_sems, recv_sems, after_ref, p_dead, got_ref):
        for k, (slot, dev) in enumerate(_chip_peers()):
            cp = pltpu.make_async_remote_copy(
                src_ref=p_ref.at[slot], dst_ref=land_ref.at[slot], send_sem=send_sems.at[k], recv_sem=recv_sems.at[k],
                device_id=dev, device_id_type=MESH_ID)
            cp.wait_send()
            cp.wait_recv()

    return pl.pallas_call(
        body, name=name,
        out_shape=(pltpu.HBM(p_thru.shape, p_thru.dtype), pltpu.HBM(land_thru.shape, land_thru.dtype)),
        in_specs=(HBM_SPEC, HBM_SPEC, SEM_SPEC, SEM_SPEC, ANY_SPEC), out_specs=(HBM_SPEC, HBM_SPEC),
        input_output_aliases={0: 0, 1: 1},
        compiler_params=pltpu.CompilerParams(has_side_effects=SPLIT_EFFECT),
    )(p_thru, land_thru, send_sems, recv_sems, after)


def _gather_peers():
    x, y, c = _my_place()
    devs = [(x, y, 1 - c)] + [(x ^ (k >> 1), y ^ (k & 1), c) for k in range(1, 4)]
    return [(_linear(d), d) for d in devs]


def _gather_start(shard, after, name):
    R, C = shard.shape
    dtype = shard.dtype

    def body(x_ref, land_ref, after_ref, send_sems, recv_sems, x_thru, land_thru, token):
        me = _linear(_my_place())
        for k, (_, dev) in enumerate(_gather_peers()):
            pltpu.make_async_remote_copy(
                src_ref=x_ref, dst_ref=land_ref.at[me], send_sem=send_sems.at[k], recv_sem=recv_sems.at[k],
                device_id=dev, device_id_type=MESH_ID).start()
        token[...] = jnp.zeros_like(token)

    return pl.pallas_call(
        body, name=name,
        out_shape=(pltpu.SemaphoreType.DMA((4,)), pltpu.SemaphoreType.DMA((4,)), pltpu.HBM((R, C), dtype),
                   pltpu.HBM((N_DEV, R, C), dtype), jax.ShapeDtypeStruct((8, LANE), F32)),
        in_specs=(HBM_SPEC, HBM_SPEC, ANY_SPEC),
        out_specs=(SEM_SPEC, SEM_SPEC, HBM_SPEC, HBM_SPEC, pl.BlockSpec(memory_space=pltpu.VMEM)),
        input_output_aliases={0: 2, 1: 3},
        compiler_params=pltpu.CompilerParams(has_side_effects=SPLIT_EFFECT),
    )(pltpu.with_memory_space_constraint(shard, pltpu.HBM), _landing_zone((N_DEV, R, C), dtype), after)


def _gather_wait(send_sems, recv_sems, x_thru, land_thru, after, name):
    def body(x_ref, land_ref, send_sems, recv_sems, after_ref, x_out, got_ref):
        for k, (slot, dev) in enumerate(_gather_peers()):
            cp = pltpu.make_async_remote_copy(
                src_ref=x_ref, dst_ref=land_ref.at[slot], send_sem=send_sems.at[k], recv_sem=recv_sems.at[k],
                device_id=dev, device_id_type=MESH_ID)
            cp.wait_send()
            cp.wait_recv()

    return pl.pallas_call(
        body, name=name,
        out_shape=(pltpu.HBM(x_thru.shape, x_thru.dtype), pltpu.HBM(land_thru.shape, land_thru.dtype)),
        in_specs=(HBM_SPEC, HBM_SPEC, SEM_SPEC, SEM_SPEC, ANY_SPEC), out_specs=(HBM_SPEC, HBM_SPEC),
        input_output_aliases={0: 0, 1: 1},
        compiler_params=pltpu.CompilerParams(has_side_effects=SPLIT_EFFECT),
    )(x_thru, land_thru, send_sems, recv_sems, after)


def _gather_forward(shard, land, name):
    def body(x_ref, land_ref, out_ref, send_sems, recv_sems, local_sem):
        x, y, c = _my_place()
        sibling = (x, y, 1 - c)
        mine = pltpu.make_async_copy(x_ref, out_ref.at[_linear((x, y, c))], local_sem)
        mine.start()
        sends, arrivals = [], []
        for k in range(1, 4):
            px, py = x ^ (k >> 1), y ^ (k & 1)
            landed, theirs = _linear((px, py, c)), out_ref.at[_linear((px, py, 1 - c))]
            sends.append(pltpu.make_async_remote_copy(
                src_ref=land_ref.at[landed], dst_ref=out_ref.at[landed],
                send_sem=send_sems.at[k - 1], recv_sem=recv_sems.at[k - 1],
                device_id=sibling, device_id_type=MESH_ID))
            arrivals.append(pltpu.make_async_remote_copy(
                src_ref=theirs, dst_ref=theirs, send_sem=send_sems.at[k - 1], recv_sem=recv_sems.at[k - 1],
                device_id=sibling, device_id_type=MESH_ID))
        for cp in sends:
            cp.start()
        for cp in arrivals:
            cp.wait_recv()
        for cp in sends:
            cp.wait_send()
        mine.wait()

    return pl.pallas_call(
        body, name=name,
        in_specs=[HBM_SPEC, HBM_SPEC], out_specs=HBM_SPEC,
        out_shape=jax.ShapeDtypeStruct(land.shape, land.dtype),
        input_output_aliases={1: 0},
        scratch_shapes=[pltpu.SemaphoreType.DMA((3,)), pltpu.SemaphoreType.DMA((3,)), pltpu.SemaphoreType.DMA],
    )(shard, land)


def _chip_sum(got):
    _, R, C = got.shape
    tr = _tile(R, PACK_TILE_ROWS, PACK_ROW_ALIGN)

    def body(g_ref, o_ref):
        acc = g_ref[0].astype(F32)
        for q in range(1, 4):
            acc = acc + g_ref[q].astype(F32)
        o_ref[...] = acc

    return pl.pallas_call(
        body, name="grad_chip_sum", grid=(R // tr,),
        in_specs=[pl.BlockSpec((4, tr, C), lambda i: (0, i, 0))],
        out_specs=pl.BlockSpec((tr, C), lambda i: (i, 0)),
        out_shape=jax.ShapeDtypeStruct((R, C), F32),
        compiler_params=_cparams(("parallel",)),
    )(got)


def _pack_rows(n_elems):
    return _round_up(_round_up(n_elems, PACK_COLS) // PACK_COLS, PACK_ROW_ALIGN)


def _pack(flat_list, dtype):
    out = []
    for a in flat_list:
        n = a.shape[-1]
        rows = _pack_rows(n)
        pad = rows * PACK_COLS - n
        a = a.astype(dtype)
        if pad:
            a = jnp.pad(a, [(0, 0)] * (a.ndim - 1) + [(0, pad)])
        out.append(a.reshape(a.shape[:-1] + (rows, PACK_COLS)))
    total = sum(a.shape[-2] for a in out)
    fill = _round_up(total, PACK_ROW_BLOCK) - total
    if fill:
        out.append(jnp.zeros(out[0].shape[:-2] + (fill, PACK_COLS), dtype))
    return jnp.concatenate(out, axis=-2)


def _unpack(buf, sizes):
    out, r0 = [], 0
    for n in sizes:
        rows = _pack_rows(n)
        a = buf[..., r0:r0 + rows, :]
        out.append(a.reshape(a.shape[:-2] + (rows * PACK_COLS,))[..., :n])
        r0 += rows
    return out


def _cols_to_shards(w):
    K, N = w.shape
    return w.reshape(K, N_DEV, N // N_DEV).transpose(1, 0, 2).reshape(N_DEV, -1)


def _shards_to_cols(s, K):
    return s.reshape(N_DEV, K, -1).transpose(1, 0, 2).reshape(K, -1)


def _reduce_scatter_begin(shard_lists, tag):
    parts = _pack(shard_lists, BF16)
    got = _exchange_in_chip(parts)
    chip_parts = _pair_sum(parts, got, lax.axis_index("c").astype(jnp.int32).reshape(1))
    return _scatter_start(chip_parts, got, "grad_scatter_start_" + tag)


def _reduce_scatter_end(state, after, sizes, tag):
    send_sems, recv_sems, p_thru, land_thru, _ = state
    chip_parts, got = _scatter_wait(send_sems, recv_sems, p_thru, land_thru, after, "grad_scatter_wait_" + tag)
    x, y, _ = _my_place()
    my_chip = 2 * x + y
    own = lax.dynamic_index_in_dim(chip_parts, my_chip, axis=0, keepdims=True)
    got = lax.dynamic_update_index_in_dim(got, own, my_chip, axis=0)
    return _unpack(_chip_sum(got), sizes)


def kernel(x, c, positions, w_ada, b_ada, w_in, g_q_a, w_q_b, g_kv_a, w_kv_b, w_o_a, w_conv, w_o_b, w_o, ln1_g, ln1_b, w_ffn_in, w_ffn_out, ln2_g, ln2_b, loss_target, m_w_ada, m_b_ada, m_w_in, m_g_q_a, m_w_q_b, m_g_kv_a, m_w_kv_b, m_w_o_a, m_w_conv, m_w_o_b, m_w_o, m_ln1_g, m_ln1_b, m_w_ffn_in, m_w_ffn_out, m_ln2_g, m_ln2_b, v_w_ada, v_b_ada, v_w_in, v_g_q_a, v_w_q_b, v_g_kv_a, v_w_kv_b, v_w_o_a, v_w_conv, v_w_o_b, v_w_o, v_ln1_g, v_ln1_b, v_w_ffn_in, v_w_ffn_out, v_ln2_g, v_ln2_b):
    x2, tgt = x[0], loss_target[0]
    S, D = x2.shape
    Lq, Lkv = g_q_a.shape[1], g_kv_a.shape[1]
    H = w_q_b.shape[2] * N_DEV // QK_CAT
    F = w_ffn_out.shape[1] * N_DEV
    n_in = w_in.shape[2] * N_DEV
    assert Lq == Lkv and (Lq + Lkv) % COL_BLOCK == 0 and D % COL_BLOCK == 0
    front = Lq + Lkv + QK_ROPE
    front_pad = _round_up(front, COL_BLOCK)
    kr_blk = (Lq + Lkv) // COL_BLOCK
    blk_b = front_pad // COL_BLOCK
    nblk = D // COL_BLOCK
    blk_c, blk_x, blk_ga, blk_gb = blk_b + nblk, blk_b + 2 * nblk, blk_b + 3 * nblk, blk_b + 4 * nblk
    ts = _tile(S, 256, 8)
    T = _tile(S, min(512, S // 2), CHUNK)
    tb = _tile(F, 512)
    me = _linear(_my_place())

    cw = w_ada.shape[2]
    b_mine = lax.dynamic_slice(b_ada, (0, me * cw), (1, cw)).reshape(1, 1, cw)
    mod_blocks, cact_all, wconv_all = _ada_fwd(c.reshape(1, 1, D), w_conv[0].reshape(1, 1, -1), w_ada[0], b_mine)
    mod = mod_blocks.reshape(6, D)
    cact_all = cact_all.reshape(N_DEV, D)
    w_conv_full = _shards_to_cols(wconv_all.reshape(N_DEV, -1), CONV_K)

    first, later = [w_in, w_q_b, w_kv_b], [w_o_a, w_o_b, w_o, w_ffn_in, w_ffn_out]
    size_of = lambda ws: [w.shape[1] * w.shape[2] for w in ws]
    gathered = _all_gather(_pack([w[0].reshape(-1) for w in first], BF16))
    later_state = _gather_start(_pack([w[0].reshape(-1) for w in later], BF16), gathered, "weight_gather_start")
    later_token = later_state[4]
    g_in, g_qb, g_kvb = _unpack(gathered, size_of(first))
    w_in_f = _shards_to_cols(g_in, D)
    w_in_p = jnp.concatenate([w_in_f[:, :front], jnp.zeros((D, front_pad - front), BF16), w_in_f[:, front:]], axis=1)
    wq = _shards_to_cols(g_qb, Lq).reshape(Lq, H, QK_CAT)
    wq_p = jnp.concatenate([wq[:, :, :QK_NOPE].reshape(Lq, -1), wq[:, :, QK_NOPE:].reshape(Lq, -1)], axis=1)
    wkv = _shards_to_cols(g_kvb, Lkv).reshape(Lkv, H, QK_NOPE + V_HEAD)
    wkv_p = jnp.concatenate([wkv[:, :, :QK_NOPE].reshape(Lkv, -1), wkv[:, :, QK_NOPE:].reshape(Lkv, -1)], axis=1)

    inv_freq = 1.0 / (ROPE_THETA ** (jnp.arange(0, QK_ROPE, 2, dtype=F32) / QK_ROPE))
    ang = positions[0].astype(F32)[:, None] * inv_freq
    cos2 = jnp.concatenate([jnp.cos(ang), jnp.cos(ang)], axis=-1)
    sin2 = jnp.concatenate([jnp.sin(ang), jnp.sin(ang)], axis=-1)
    cos_q, sin_q = jnp.tile(cos2, (1, H)), jnp.tile(sin2, (1, H))
    cos_k, sin_k = jnp.tile(cos2, (1, COL_BLOCK // QK_ROPE)), jnp.tile(sin2, (1, COL_BLOCK // QK_ROPE))

    u = _modulate_in(x2, mod, ts)
    proj = _matmul(u, w_in_p, "nn", F32, "proj", deps=(later_token,))
    qn = _rms_fwd(proj, g_q_a, 0, Lq, ts, "rms_q")
    kvn = _rms_fwd(proj, g_kv_a, 1, Lkv, ts, "rms_kv")
    q = _matmul(qn, wq_p, "nn", F32, "q_up")
    kv = _matmul(kvn, wkv_p, "nn", F32, "kv_up")
    qc, kc, vh = _qk_prep(q, kv, proj, kr_blk, cos_q, sin_q, cos_k, sin_k, H, ts)
    attn, lse = _attn_fwd(qc, kc, vh, T)
    later_shard, later_land = _gather_wait(*later_state[:4], lse, "weight_gather_wait")
    g_oa, g_ob, g_o, g_fi, g_fo = _unpack(_gather_forward(later_shard, later_land, "weight_gather_forward"),
                                          size_of(later))
    w_oa_f, w_ob_f, w_o_f = g_oa.reshape(-1, D), g_ob.reshape(-1, D), g_o.reshape(-1, D)
    w_fi_f = _shards_to_cols(g_fi, D)
    w_fo_f = g_fo.reshape(F, D)
    ya = _matmul(attn, w_oa_f, "nn", F32, "attn_out")
    cbc = _conv_fwd(proj, w_conv_full, blk_b, blk_c, blk_x)
    yb = _matmul(cbc, w_ob_f, "nn", F32, "conv_out")
    merged = _merge_fwd(proj, ya, yb, blk_ga, blk_gb, ts)
    mix = _matmul(merged, w_o_f, "nn", F32, "mix_out")
    xhat1, rstd1, u2 = _ln1_fwd(x2, mix, mod, ln1_g, ln1_b, ts)
    hh = _matmul(u2, w_fi_f, "nn", F32, "ffn_in")
    act = _swiglu_fwd(hh, ts, tb)
    ffn = _matmul(act, w_fo_f, "nn", F32, "ffn_out")
    loss_part, dffn, dx1a, vec2 = _ln2_loss(xhat1, ffn, tgt, mod, ln1_g, ln1_b, ln2_g, ln2_b, ts)
    loss = lax.psum(loss_part[0, 0], AXES)

    gw_fo = _matmul(act, dffn, "tn", BF16, "grad_w_ffn_out")
    da = _matmul(dffn, w_fo_f, "nt", F32, "d_act")
    dhg, dhu = _swiglu_bwd(da, hh, ts, tb)
    dh = jnp.concatenate([dhg, dhu], axis=1)
    gw_fi = _matmul(u2, dh, "tn", BF16, "grad_w_ffn_in")
    ffn_state = _reduce_scatter_begin([_cols_to_shards(gw_fi), gw_fo.reshape(N_DEV, -1)], "ffn")
    du2 = _matmul(dh, w_fi_f, "nt", F32, "d_u2", deps=(ffn_state[4],))
    dxa, dmix, vec1 = _ln1_bwd(du2, dx1a, xhat1, rstd1, mix, mod, ln1_g, ln1_b, ts)
    gw_o = _matmul(merged, dmix, "tn", BF16, "grad_w_o")
    dmerged = _matmul(dmix, w_o_f, "nt", F32, "d_merged")
    dya, dyb, dga, dgb = _merge_bwd(dmerged, proj, ya, yb, blk_ga, blk_gb, ts)
    gw_ob = _matmul(cbc, dyb, "tn", BF16, "grad_w_o_b")
    dcbc = _matmul(dyb, w_ob_f, "nt", F32, "d_conv")
    dcb, dcc, dcx, dwconv = _conv_bwd(dcbc, proj, w_conv_full, blk_b, blk_c, blk_x)
    gw_oa = _matmul(attn, dya, "tn", BF16, "grad_w_o_a")
    mix_state = _reduce_scatter_begin([gw_oa.reshape(N_DEV, -1), gw_ob.reshape(N_DEV, -1), gw_o.reshape(N_DEV, -1)],
                                      "mix")
    dattn = _matmul(dya, w_oa_f, "nt", F32, "d_attn", deps=(mix_state[4],))
    dqc, dkc, dvh = _attn_bwd(qc, kc, vh, dattn, attn, lse, T)
    g_fi_s, g_fo_s = _reduce_scatter_end(ffn_state, dqc, size_of([w_ffn_in, w_ffn_out]), "ffn")
    g_oa_s, g_ob_s, g_o_s = _reduce_scatter_end(mix_state, dqc, size_of([w_o_a, w_o_b, w_o]), "mix")
    dq, dkv, dkr = _qk_bwd(dqc, dkc, dvh, cos_q, sin_q, cos_k, sin_k, ts)
    gw_qb_p = _matmul(qn, dq, "tn", BF16, "grad_w_q_b")
    dqn = _matmul(dq, wq_p, "nt", F32, "d_qn")
    gw_kvb_p = _matmul(kvn, dkv, "tn", BF16, "grad_w_kv_b")
    dkvn = _matmul(dkv, wkv_p, "nt", F32, "d_kvn")
    dqa, dgq = _rms_bwd(dqn, proj, g_q_a, 0, Lq, ts, "rms_q_bwd")
    dkva, dgkv = _rms_bwd(dkvn, proj, g_kv_a, 1, Lkv, ts, "rms_kv_bwd")
    dproj = jnp.concatenate([dqa, dkva, dkr, dcb, dcc, dcx, dga, dgb], axis=1)
    gw_in_p = _matmul(u, dproj, "tn", BF16, "grad_w_in")
    nq, nk = H * QK_NOPE, H * QK_NOPE
    gw_in = jnp.concatenate([gw_in_p[:, :front], gw_in_p[:, front_pad:]], axis=1)
    gw_qb = jnp.concatenate([gw_qb_p[:, :nq].reshape(Lq, H, QK_NOPE), gw_qb_p[:, nq:].reshape(Lq, H, QK_ROPE)],
                            axis=2).reshape(Lq, -1)
    gw_kvb = jnp.concatenate([gw_kvb_p[:, :nk].reshape(Lkv, H, QK_NOPE), gw_kvb_p[:, nk:].reshape(Lkv, H, V_HEAD)],
                             axis=2).reshape(Lkv, -1)
    in_state = _reduce_scatter_begin([_cols_to_shards(gw_in), _cols_to_shards(gw_qb), _cols_to_shards(gw_kvb)], "in")
    du = _matmul(dproj, w_in_p, "nt", F32, "d_u", deps=(in_state[4],))
    grad_x, vec0 = _grad_x(du, dxa, x2, mod, ts)

    n_mod = 6 * D // cw
    dmod = jnp.concatenate([vec0[0], vec0[1], vec1[4], vec1[0], vec1[1], vec2[2]])
    small = jnp.concatenate([dmod, dgq[0], dgkv[0], vec1[2], vec1[3], vec2[0], vec2[1], dwconv[:CONV_K].reshape(-1)])
    n_small = small.shape[0]
    nch = _round_up(n_small, cw) // cw
    payload = jnp.pad(small, (0, nch * cw - n_small)).reshape(nch, 1, cw)
    summed, g_w_ada = _ada_bwd(payload, cact_all.T, n_mod)
    summed = summed.reshape(-1)
    offs = [0, 6 * D, 6 * D + Lq, 6 * D + Lq + Lkv]
    offs += [offs[-1] + D * k for k in range(1, 5)]
    g_b_ada = summed[offs[0]:offs[1]].reshape(1, -1)
    g_gq = summed[offs[1]:offs[2]].reshape(1, -1)
    g_gkv = summed[offs[2]:offs[3]].reshape(1, -1)
    g_ln1g, g_ln1b, g_ln2g, g_ln2b = [summed[offs[3 + k]:offs[4 + k]].reshape(1, -1) for k in range(4)]
    wc = w_conv.shape[2]
    g_wconv = lax.dynamic_slice(summed[offs[7]:offs[7] + CONV_K * D].reshape(CONV_K, D), (0, me * wc), (CONV_K, wc))

    names = ["w_ada", "b_ada", "w_in", "g_q_a", "w_q_b", "g_kv_a", "w_kv_b", "w_o_a", "w_conv", "w_o_b", "w_o",
             "ln1_g", "ln1_b", "w_ffn_in", "w_ffn_out", "ln2_g", "ln2_b"]
    weights = [w_ada, b_ada, w_in, g_q_a, w_q_b, g_kv_a, w_kv_b, w_o_a, w_conv, w_o_b, w_o, ln1_g, ln1_b,
               w_ffn_in, w_ffn_out, ln2_g, ln2_b]
    moms = [m_w_ada, m_b_ada, m_w_in, m_g_q_a, m_w_q_b, m_g_kv_a, m_w_kv_b, m_w_o_a, m_w_conv, m_w_o_b, m_w_o,
            m_ln1_g, m_ln1_b, m_w_ffn_in, m_w_ffn_out, m_ln2_g, m_ln2_b]
    vels = [v_w_ada, v_b_ada, v_w_in, v_g_q_a, v_w_q_b, v_g_kv_a, v_w_kv_b, v_w_o_a, v_w_conv, v_w_o_b, v_w_o,
            v_ln1_g, v_ln1_b, v_w_ffn_in, v_w_ffn_out, v_ln2_g, v_ln2_b]
    grad_of = {"w_ada": g_w_ada, "b_ada": g_b_ada, "g_q_a": g_gq, "g_kv_a": g_gkv, "w_o_a": g_oa_s, "w_conv": g_wconv,
               "w_o_b": g_ob_s, "w_o": g_o_s, "ln1_g": g_ln1g, "ln1_b": g_ln1b, "w_ffn_in": g_fi_s,
               "w_ffn_out": g_fo_s, "ln2_g": g_ln2g, "ln2_b": g_ln2b}
    state_of = dict(zip(names, zip(weights, moms, vels)))
    results = {}

    def update(nm):
        w, m, v = state_of[nm]
        shp = w.shape
        w2 = w.reshape(shp[-2], shp[-1]) if w.ndim == 3 else w
        g2 = grad_of[nm].reshape(w2.shape)
        d, new_m, new_v = _adamw(w2, g2, m.reshape(w2.shape), v.reshape(w2.shape), "adamw_" + nm)
        results[nm] = [a.reshape(shp) for a in (g2, d, new_m, new_v)]

    last = ("w_in", "w_q_b", "w_kv_b")
    for nm in names:
        if nm not in last:
            update(nm)
    g_in_s, g_qb_s, g_kvb_s = _reduce_scatter_end(in_state, results["w_ffn_in"][1], size_of(first), "in")
    grad_of.update({"w_in": g_in_s, "w_q_b": g_qb_s, "w_kv_b": g_kvb_s})
    for nm in last:
        update(nm)
    outs = [[results[nm][k] for nm in names] for k in range(4)]
    return (loss, grad_x.reshape(x.shape), *outs[0], *outs[1], *outs[2], *outs[3])
```

```python
import functools

import jax
import jax.numpy as jnp
from jax import lax
from jax.experimental import pallas as pl
from jax.experimental.pallas import tpu as pltpu

F32 = jnp.float32
BF16 = jnp.bfloat16
MESH_ID = pl.DeviceIdType.MESH
AXES = ("x", "y", "c")
N_DEV = 8

CHUNK = 64
QK_NOPE = 128
QK_ROPE = 64
V_HEAD = 128
QK_CAT = QK_NOPE + QK_ROPE
ROPE_THETA = 10000.0
ATTN_SCALE = (QK_NOPE + QK_ROPE) ** -0.5
CONV_K = 3
DEEPNORM_ALPHA = 2.0 ** 0.25
LN_EPS = 1e-5
RMS_EPS = 1e-6
NEG_INF = -1e30

ADAM_LR = 0.001
ADAM_B1 = 0.9
ADAM_B2 = 0.999
ADAM_EPS = 1e-08
ADAM_WD = 0.01
ADAM_STEP = 10

LANE = 128
COL_BLOCK = 256
PACK_COLS = 512
PACK_ROW_ALIGN = 16
PACK_ROW_BLOCK = 512
PACK_TILE_ROWS = 2048
VMEM_LIMIT = 48 * 1024 * 1024


def _round_up(n, m):
    return (n + m - 1) // m * m


def _tile(n, pref, align=LANE):
    best = None
    t = align
    while t <= min(n, pref):
        if n % t == 0:
            best = t
        t += align
    return best if best is not None else n


def _cparams(sem=None):
    return pltpu.CompilerParams(dimension_semantics=sem, vmem_limit_bytes=VMEM_LIMIT)


def _sigmoid(x):
    return 1.0 / (1.0 + jnp.exp(-x))


def _matmul(a, b, mode, out_dtype, name, tm=1024, tn=1024, tk=512, deps=()):
    if mode == "nn":
        (M, K), (K2, N) = a.shape, b.shape
    elif mode == "nt":
        (M, K), (N, K2) = a.shape, b.shape
    else:
        (K, M), (K2, N) = a.shape, b.shape
    assert K == K2, (a.shape, b.shape, mode)
    tm, tn, tk = _tile(M, tm), _tile(N, tn), _tile(K, tk)
    nk = K // tk
    if mode == "nn":
        a_spec = pl.BlockSpec((tm, tk), lambda i, j, k: (i, k))
        b_spec = pl.BlockSpec((tk, tn), lambda i, j, k: (k, j))
        dims = (((1,), (0,)), ((), ()))
    elif mode == "nt":
        a_spec = pl.BlockSpec((tm, tk), lambda i, j, k: (i, k))
        b_spec = pl.BlockSpec((tn, tk), lambda i, j, k: (j, k))
        dims = (((1,), (1,)), ((), ()))
    else:
        a_spec = pl.BlockSpec((tk, tm), lambda i, j, k: (k, i))
        b_spec = pl.BlockSpec((tk, tn), lambda i, j, k: (k, j))
        dims = (((0,), (0,)), ((), ()))

    def body(a_ref, b_ref, *rest):
        o_ref, acc_ref = rest[-2:]
        k = pl.program_id(2)

        @pl.when(k == 0)
        def _():
            acc_ref[...] = jnp.zeros_like(acc_ref)

        acc_ref[...] += lax.dot_general(a_ref[...].astype(BF16), b_ref[...].astype(BF16), dims,
                                        preferred_element_type=F32)

        @pl.when(k == nk - 1)
        def _():
            o_ref[...] = acc_ref[...].astype(o_ref.dtype)

    return pl.pallas_call(
        body, name=name, grid=(M // tm, N // tn, nk),
        in_specs=[a_spec, b_spec] + [ANY_SPEC] * len(deps),
        out_specs=pl.BlockSpec((tm, tn), lambda i, j, k: (i, j)),
        out_shape=jax.ShapeDtypeStruct((M, N), out_dtype),
        scratch_shapes=[pltpu.VMEM((tm, tn), F32)],
        compiler_params=_cparams(("parallel", "parallel", "arbitrary")),
    )(a, b, *deps)


def _modulate_in(x, mod, ts):
    S, D = x.shape

    def body(x_ref, mod_ref, u_ref):
        u_ref[...] = (x_ref[...] * (1.0 + mod_ref[1:2, :]) + mod_ref[0:1, :]).astype(BF16)

    return pl.pallas_call(
        body, name="modulate_in", grid=(S // ts,),
        in_specs=[pl.BlockSpec((ts, D), lambda i: (i, 0)), pl.BlockSpec((6, D), lambda i: (0, 0))],
        out_specs=pl.BlockSpec((ts, D), lambda i: (i, 0)),
        out_shape=jax.ShapeDtypeStruct((S, D), BF16),
        compiler_params=_cparams(("parallel",)),
    )(x, mod)


def _rms_fwd(proj, g, blk, L, ts, name):
    S = proj.shape[0]

    def body(a_ref, g_ref, y_ref):
        a = a_ref[...]
        r = lax.rsqrt(jnp.mean(a * a, axis=-1, keepdims=True) + RMS_EPS)
        y_ref[...] = (a * r * g_ref[...]).astype(BF16)

    return pl.pallas_call(
        body, name=name, grid=(S // ts,),
        in_specs=[pl.BlockSpec((ts, L), lambda i: (i, blk)), pl.BlockSpec((1, L), lambda i: (0, 0))],
        out_specs=pl.BlockSpec((ts, L), lambda i: (i, 0)),
        out_shape=jax.ShapeDtypeStruct((S, L), BF16),
        compiler_params=_cparams(("parallel",)),
    )(proj, g)


def _rotate_half_pairs(x, sign):
    w = x.shape[-1]
    lane = lax.broadcasted_iota(jnp.int32, x.shape, x.ndim - 1)
    first = (lane % QK_ROPE) < (QK_ROPE // 2)
    from_right = pltpu.roll(x, w - QK_ROPE // 2, axis=x.ndim - 1)
    from_left = pltpu.roll(x, QK_ROPE // 2, axis=x.ndim - 1)
    return jnp.where(first, -sign * from_right, sign * from_left)


def _qk_prep(q, kv, proj, kr_blk, cos_q, sin_q, cos_k, sin_k, H, ts):
    S = q.shape[0]
    nope_w, rope_w = H * QK_NOPE, H * QK_ROPE

    def body(q_ref, kv_ref, kr_ref, cq_ref, sq_ref, ck_ref, sk_ref, qc_ref, kc_ref, vh_ref):
        qr = q_ref[:, nope_w:]
        qr = qr * cq_ref[...] + _rotate_half_pairs(qr, 1.0) * sq_ref[...]
        kr = kr_ref[...]
        kr = kr * ck_ref[...] + _rotate_half_pairs(kr, 1.0) * sk_ref[...]
        kr = kr[:, :QK_ROPE].astype(BF16)
        for h in range(H):
            qc_ref[h, :, 0:QK_NOPE] = q_ref[:, h * QK_NOPE:(h + 1) * QK_NOPE].astype(BF16)
            qc_ref[h, :, QK_NOPE:QK_CAT] = qr[:, h * QK_ROPE:(h + 1) * QK_ROPE].astype(BF16)
            kc_ref[h, :, 0:QK_NOPE] = kv_ref[:, h * QK_NOPE:(h + 1) * QK_NOPE].astype(BF16)
            kc_ref[h, :, QK_NOPE:QK_CAT] = kr
            vh_ref[h, :, :] = kv_ref[:, nope_w + h * V_HEAD:nope_w + (h + 1) * V_HEAD].astype(BF16)

    row = lambda w: pl.BlockSpec((ts, w), lambda i: (i, 0))
    return pl.pallas_call(
        body, name="qk_prep", grid=(S // ts,),
        in_specs=[row(nope_w + rope_w), row(nope_w + H * V_HEAD),
                  pl.BlockSpec((ts, COL_BLOCK), lambda i: (i, kr_blk)),
                  row(rope_w), row(rope_w), row(COL_BLOCK), row(COL_BLOCK)],
        out_specs=[pl.BlockSpec((H, ts, QK_CAT), lambda i: (0, i, 0)),
                   pl.BlockSpec((H, ts, QK_CAT), lambda i: (0, i, 0)),
                   pl.BlockSpec((H, ts, V_HEAD), lambda i: (0, i, 0))],
        out_shape=[jax.ShapeDtypeStruct((H, S, QK_CAT), BF16), jax.ShapeDtypeStruct((H, S, QK_CAT), BF16),
                   jax.ShapeDtypeStruct((H, S, V_HEAD), BF16)],
        compiler_params=_cparams(("parallel",)),
    )(q, kv, proj, cos_q, sin_q, cos_k, sin_k)


NT_DIMS = (((1,), (1,)), ((), ()))
TN_DIMS = (((0,), (0,)), ((), ()))


def _diag_mask(T):
    rows = lax.broadcasted_iota(jnp.int32, (T, T), 0) // CHUNK
    cols = lax.broadcasted_iota(jnp.int32, (T, T), 1) // CHUNK
    return cols <= rows


def _attn_fwd(qc, kc, vh, T):
    H, S, _ = qc.shape
    n = S // T

    def body(q_ref, k_ref, v_ref, o_ref, lse_ref, m_ref, l_ref, acc_ref):
        i = pl.program_id(1)
        q = q_ref[0]
        m_ref[...] = jnp.full_like(m_ref, NEG_INF)
        l_ref[...] = jnp.zeros_like(l_ref)
        acc_ref[...] = jnp.zeros_like(acc_ref)

        def step(j, masked):
            rows = pl.ds(pl.multiple_of(j * T, T), T)
            s = lax.dot_general(q, k_ref[0, rows, :], NT_DIMS, preferred_element_type=F32) * ATTN_SCALE
            if masked:
                s = jnp.where(_diag_mask(T), s, NEG_INF)
            m_old = m_ref[...]
            m_new = jnp.maximum(m_old, jnp.max(s, axis=-1, keepdims=True))
            alpha = jnp.exp(m_old - m_new)
            p = jnp.exp(s - m_new)
            l_ref[...] = alpha * l_ref[...] + jnp.sum(p, axis=-1, keepdims=True)
            acc_ref[...] = alpha * acc_ref[...] + jnp.dot(p.astype(BF16), v_ref[0, rows, :],
                                                          preferred_element_type=F32)
            m_ref[...] = m_new

        def below(j, carry):
            step(j, False)
            return carry

        lax.fori_loop(0, i, below, 0)
        step(i, True)
        o_ref[...] = acc_ref[...] / l_ref[...]
        lse_ref[0] = m_ref[...] + jnp.log(l_ref[...])

    return pl.pallas_call(
        body, name="attn_fwd", grid=(H, n),
        in_specs=[pl.BlockSpec((1, T, QK_CAT), lambda h, i: (h, i, 0)),
                  pl.BlockSpec((1, S, QK_CAT), lambda h, i: (h, 0, 0)),
                  pl.BlockSpec((1, S, V_HEAD), lambda h, i: (h, 0, 0))],
        out_specs=[pl.BlockSpec((T, V_HEAD), lambda h, i: (i, h)),
                   pl.BlockSpec((1, T, 1), lambda h, i: (h, i, 0))],
        out_shape=[jax.ShapeDtypeStruct((S, H * V_HEAD), F32), jax.ShapeDtypeStruct((H, S, 1), F32)],
        scratch_shapes=[pltpu.VMEM((T, 1), F32), pltpu.VMEM((T, 1), F32), pltpu.VMEM((T, V_HEAD), F32)],
        compiler_params=_cparams(("parallel", "arbitrary")),
    )(qc, kc, vh)


def _shift_rows(z, k):
    if k == 0:
        return z
    n = z.shape[0]
    row = lax.broadcasted_iota(jnp.int32, z.shape, 0)
    if k > 0:
        return jnp.where(row >= k, pltpu.roll(z, k, axis=0), 0.0)
    return jnp.where(row < n + k, pltpu.roll(z, n + k, axis=0), 0.0)


def _conv_fwd(proj, w_conv, blk_b, blk_c, blk_x):
    S = proj.shape[0]
    D = w_conv.shape[1]
    nb = D // COL_BLOCK

    def body(cb_ref, cc_ref, cx_ref, w_ref, o_ref):
        z = cc_ref[...] * cx_ref[...]
        conv = w_ref[2:3, :] * z + w_ref[1:2, :] * _shift_rows(z, 1) + w_ref[0:1, :] * _shift_rows(z, 2)
        o_ref[...] = (cb_ref[...] * conv).astype(BF16)

    col = lambda off: pl.BlockSpec((S, COL_BLOCK), lambda j: (0, off + j))
    return pl.pallas_call(
        body, name="conv_fwd", grid=(nb,),
        in_specs=[col(blk_b), col(blk_c), col(blk_x), pl.BlockSpec((CONV_K, COL_BLOCK), lambda j: (0, j))],
        out_specs=pl.BlockSpec((S, COL_BLOCK), lambda j: (0, j)),
        out_shape=jax.ShapeDtypeStruct((S, D), BF16),
        compiler_params=_cparams(("parallel",)),
    )(proj, proj, proj, w_conv)


def _merge_fwd(proj, ya, yb, blk_ga, blk_gb, ts):
    S, D = ya.shape
    nb = D // COL_BLOCK

    def body(ga_ref, gb_ref, ya_ref, yb_ref, o_ref):
        o_ref[...] = (_sigmoid(ga_ref[...]) * ya_ref[...] + _sigmoid(gb_ref[...]) * yb_ref[...]).astype(BF16)

    blk = lambda off: pl.BlockSpec((ts, COL_BLOCK), lambda i, j: (i, off + j))
    return pl.pallas_call(
        body, name="merge_fwd", grid=(S // ts, nb),
        in_specs=[blk(blk_ga), blk(blk_gb), blk(0), blk(0)],
        out_specs=blk(0),
        out_shape=jax.ShapeDtypeStruct((S, D), BF16),
        compiler_params=_cparams(("parallel", "parallel")),
    )(proj, proj, ya, yb)


def _ln1_fwd(x, mix, mod, g, b, ts):
    S, D = x.shape

    def body(x_ref, mix_ref, mod_ref, g_ref, b_ref, xhat_ref, rstd_ref, u2_ref):
        r = DEEPNORM_ALPHA * x_ref[...] + mod_ref[2:3, :] * mix_ref[...]
        mu = jnp.mean(r, axis=-1, keepdims=True)
        d = r - mu
        rstd = lax.rsqrt(jnp.mean(d * d, axis=-1, keepdims=True) + LN_EPS)
        xhat = d * rstd
        xhat_ref[...] = xhat
        rstd_ref[...] = rstd
        x1 = xhat * g_ref[...] + b_ref[...]
        u2_ref[...] = (x1 * (1.0 + mod_ref[4:5, :]) + mod_ref[3:4, :]).astype(BF16)

    row = pl.BlockSpec((ts, D), lambda i: (i, 0))
    vec = lambda r: pl.BlockSpec((r, D), lambda i: (0, 0))
    return pl.pallas_call(
        body, name="ln1_fwd", grid=(S // ts,),
        in_specs=[row, row, vec(6), vec(1), vec(1)],
        out_specs=[row, pl.BlockSpec((ts, 1), lambda i: (i, 0)), row],
        out_shape=[jax.ShapeDtypeStruct((S, D), F32), jax.ShapeDtypeStruct((S, 1), F32),
                   jax.ShapeDtypeStruct((S, D), BF16)],
        compiler_params=_cparams(("parallel",)),
    )(x, mix, mod, g, b)


def _swiglu_fwd(h, ts, tb):
    S, F2 = h.shape
    F = F2 // 2
    nb = F // tb

    def body(hg_ref, hu_ref, a_ref):
        hg = hg_ref[...]
        a_ref[...] = (hg * _sigmoid(hg) * hu_ref[...]).astype(BF16)

    return pl.pallas_call(
        body, name="swiglu_fwd", grid=(S // ts, nb),
        in_specs=[pl.BlockSpec((ts, tb), lambda i, j: (i, j)), pl.BlockSpec((ts, tb), lambda i, j: (i, j + nb))],
        out_specs=pl.BlockSpec((ts, tb), lambda i, j: (i, j)),
        out_shape=jax.ShapeDtypeStruct((S, F), BF16),
        compiler_params=_cparams(("parallel", "parallel")),
    )(h, h)


def _ln2_loss(xhat1, ffn, tgt, mod, g1, b1, g2, b2, ts):
    S, D = xhat1.shape

    def body(xh_ref, ffn_ref, t_ref, mod_ref, g1_ref, b1_ref, g2_ref, b2_ref, loss_ref, dffn_ref, dx1_ref, vec_ref):
        i = pl.program_id(0)

        @pl.when(i == 0)
        def _():
            loss_ref[...] = jnp.zeros_like(loss_ref)
            vec_ref[...] = jnp.zeros_like(vec_ref)

        x1 = xh_ref[...] * g1_ref[...] + b1_ref[...]
        ffn = ffn_ref[...]
        r = DEEPNORM_ALPHA * x1 + mod_ref[5:6, :] * ffn
        mu = jnp.mean(r, axis=-1, keepdims=True)
        d = r - mu
        rstd = lax.rsqrt(jnp.mean(d * d, axis=-1, keepdims=True) + LN_EPS)
        xhat = d * rstd
        e = xhat * g2_ref[...] + b2_ref[...] - t_ref[...]
        loss_ref[...] += 0.5 * jnp.sum(jnp.mean(e * e, axis=-1, keepdims=True))
        dy = e * (1.0 / D)
        dxhat = dy * g2_ref[...]
        dr = rstd * (dxhat - jnp.mean(dxhat, axis=-1, keepdims=True)
                     - xhat * jnp.mean(dxhat * xhat, axis=-1, keepdims=True))
        dffn_ref[...] = (dr * mod_ref[5:6, :]).astype(BF16)
        dx1_ref[...] = DEEPNORM_ALPHA * dr
        vec_ref[0:1, :] += jnp.sum(dy * xhat, axis=0, keepdims=True)
        vec_ref[1:2, :] += jnp.sum(dy, axis=0, keepdims=True)
        vec_ref[2:3, :] += jnp.sum(dr * ffn, axis=0, keepdims=True)

    row = pl.BlockSpec((ts, D), lambda i: (i, 0))
    vec = lambda r: pl.BlockSpec((r, D), lambda i: (0, 0))
    return pl.pallas_call(
        body, name="ln2_loss", grid=(S // ts,),
        in_specs=[row, row, row, vec(6), vec(1), vec(1), vec(1), vec(1)],
        out_specs=[pl.BlockSpec((1, LANE), lambda i: (0, 0)), row, row, vec(8)],
        out_shape=[jax.ShapeDtypeStruct((1, LANE), F32), jax.ShapeDtypeStruct((S, D), BF16),
                   jax.ShapeDtypeStruct((S, D), F32), jax.ShapeDtypeStruct((8, D), F32)],
        compiler_params=_cparams(("arbitrary",)),
    )(xhat1, ffn, tgt, mod, g1, b1, g2, b2)


def _swiglu_bwd(da, h, ts, tb):
    S, F2 = h.shape
    nb = (F2 // 2) // tb

    def body(da_ref, hg_ref, hu_ref, dg_ref, du_ref):
        hg, da = hg_ref[...], da_ref[...]
        sg = _sigmoid(hg)
        dg_ref[...] = (da * hu_ref[...] * (sg * (1.0 + hg * (1.0 - sg)))).astype(BF16)
        du_ref[...] = (da * hg * sg).astype(BF16)

    lo = pl.BlockSpec((ts, tb), lambda i, j: (i, j))
    hi = pl.BlockSpec((ts, tb), lambda i, j: (i, j + nb))
    dg, du = pl.pallas_call(
        body, name="swiglu_bwd", grid=(S // ts, nb),
        in_specs=[lo, lo, hi],
        out_specs=[lo, lo],
        out_shape=[jax.ShapeDtypeStruct((S, F2 // 2), BF16), jax.ShapeDtypeStruct((S, F2 // 2), BF16)],
        compiler_params=_cparams(("parallel", "parallel")),
    )(da, h, h)
    return dg, du


def _ln1_bwd(du2, dx1a, xhat1, rstd1, mix, mod, g1, b1, ts):
    S, D = xhat1.shape

    def body(du2_ref, dx1a_ref, xh_ref, rstd_ref, mix_ref, mod_ref, g_ref, b_ref, dxa_ref, dmix_ref, vec_ref):
        i = pl.program_id(0)

        @pl.when(i == 0)
        def _():
            vec_ref[...] = jnp.zeros_like(vec_ref)

        xhat, du2, mix = xh_ref[...], du2_ref[...], mix_ref[...]
        x1 = xhat * g_ref[...] + b_ref[...]
        dx1 = dx1a_ref[...] + du2 * (1.0 + mod_ref[4:5, :])
        dxhat = dx1 * g_ref[...]
        dr = rstd_ref[...] * (dxhat - jnp.mean(dxhat, axis=-1, keepdims=True)
                              - xhat * jnp.mean(dxhat * xhat, axis=-1, keepdims=True))
        dxa_ref[...] = DEEPNORM_ALPHA * dr
        dmix_ref[...] = (dr * mod_ref[2:3, :]).astype(BF16)
        vec_ref[0:1, :] += jnp.sum(du2, axis=0, keepdims=True)
        vec_ref[1:2, :] += jnp.sum(du2 * x1, axis=0, keepdims=True)
        vec_ref[2:3, :] += jnp.sum(dx1 * xhat, axis=0, keepdims=True)
        vec_ref[3:4, :] += jnp.sum(dx1, axis=0, keepdims=True)
        vec_ref[4:5, :] += jnp.sum(dr * mix, axis=0, keepdims=True)

    row = pl.BlockSpec((ts, D), lambda i: (i, 0))
    vec = lambda r: pl.BlockSpec((r, D), lambda i: (0, 0))
    return pl.pallas_call(
        body, name="ln1_bwd", grid=(S // ts,),
        in_specs=[row, row, row, pl.BlockSpec((ts, 1), lambda i: (i, 0)), row, vec(6), vec(1), vec(1)],
        out_specs=[row, row, vec(8)],
        out_shape=[jax.ShapeDtypeStruct((S, D), F32), jax.ShapeDtypeStruct((S, D), BF16),
                   jax.ShapeDtypeStruct((8, D), F32)],
        compiler_params=_cparams(("arbitrary",)),
    )(du2, dx1a, xhat1, rstd1, mix, mod, g1, b1)


def _merge_bwd(dmerged, proj, ya, yb, blk_ga, blk_gb, ts):
    S, D = ya.shape
    nb = D // COL_BLOCK

    def body(dm_ref, ga_ref, gb_ref, ya_ref, yb_ref, dya_ref, dyb_ref, dga_ref, dgb_ref):
        dm = dm_ref[...]
        sa, sb = _sigmoid(ga_ref[...]), _sigmoid(gb_ref[...])
        dya_ref[...] = (dm * sa).astype(BF16)
        dyb_ref[...] = (dm * sb).astype(BF16)
        dga_ref[...] = (dm * ya_ref[...] * sa * (1.0 - sa)).astype(BF16)
        dgb_ref[...] = (dm * yb_ref[...] * sb * (1.0 - sb)).astype(BF16)

    blk = lambda off: pl.BlockSpec((ts, COL_BLOCK), lambda i, j: (i, off + j))
    out = jax.ShapeDtypeStruct((S, D), BF16)
    return pl.pallas_call(
        body, name="merge_bwd", grid=(S // ts, nb),
        in_specs=[blk(0), blk(blk_ga), blk(blk_gb), blk(0), blk(0)],
        out_specs=[blk(0)] * 4,
        out_shape=[out] * 4,
        compiler_params=_cparams(("parallel", "parallel")),
    )(dmerged, proj, proj, ya, yb)


def _conv_bwd(dcbc, proj, w_conv, blk_b, blk_c, blk_x):
    S = proj.shape[0]
    D = w_conv.shape[1]
    nb = D // COL_BLOCK

    def body(d_ref, cb_ref, cc_ref, cx_ref, w_ref, dcb_ref, dcc_ref, dcx_ref, dw_ref):
        d, cc, cx = d_ref[...], cc_ref[...], cx_ref[...]
        z = cc * cx
        z1, z2 = _shift_rows(z, 1), _shift_rows(z, 2)
        conv = w_ref[2:3, :] * z + w_ref[1:2, :] * z1 + w_ref[0:1, :] * z2
        dcb_ref[...] = (d * conv).astype(BF16)
        dconv = d * cb_ref[...]
        dz = w_ref[2:3, :] * dconv + w_ref[1:2, :] * _shift_rows(dconv, -1) + w_ref[0:1, :] * _shift_rows(dconv, -2)
        dcc_ref[...] = (dz * cx).astype(BF16)
        dcx_ref[...] = (dz * cc).astype(BF16)
        dw_ref[...] = jnp.zeros_like(dw_ref)
        dw_ref[0:1, :] = jnp.sum(dconv * z2, axis=0, keepdims=True)
        dw_ref[1:2, :] = jnp.sum(dconv * z1, axis=0, keepdims=True)
        dw_ref[2:3, :] = jnp.sum(dconv * z, axis=0, keepdims=True)

    col = lambda off: pl.BlockSpec((S, COL_BLOCK), lambda j: (0, off + j))
    out = jax.ShapeDtypeStruct((S, D), BF16)
    return pl.pallas_call(
        body, name="conv_bwd", grid=(nb,),
        in_specs=[col(0), col(blk_b), col(blk_c), col(blk_x), pl.BlockSpec((CONV_K, COL_BLOCK), lambda j: (0, j))],
        out_specs=[col(0), col(0), col(0), pl.BlockSpec((8, COL_BLOCK), lambda j: (0, j))],
        out_shape=[out, out, out, jax.ShapeDtypeStruct((8, D), F32)],
        compiler_params=_cparams(("parallel",)),
    )(dcbc, proj, proj, proj, w_conv)


def _attn_bwd(qc, kc, vh, do, o, lse, T):
    H, S, _ = qc.shape
    n = S // T

    def body(q_ref, k_ref, v_ref, do_ref, o_ref, lse_ref, dq_ref, dk_ref, dv_ref, d_ref, dk_acc, dv_acc):
        j = pl.program_id(1)

        @pl.when(j == 0)
        def _():
            dq_ref[...] = jnp.zeros_like(dq_ref)
            d_ref[...] = jnp.sum(do_ref[...] * o_ref[...], axis=-1, keepdims=True)

        dk_acc[...] = jnp.zeros_like(dk_acc)
        dv_acc[...] = jnp.zeros_like(dv_acc)
        k, v = k_ref[0], v_ref[0]

        def step(i, masked):
            rows = pl.ds(pl.multiple_of(i * T, T), T)
            q = q_ref[0, rows, :]
            do = do_ref[rows, :].astype(BF16)
            s = lax.dot_general(q, k, NT_DIMS, preferred_element_type=F32) * ATTN_SCALE
            if masked:
                s = jnp.where(_diag_mask(T), s, NEG_INF)
            p = jnp.exp(s - lse_ref[0, rows, :])
            dv_acc[...] += lax.dot_general(p.astype(BF16), do, TN_DIMS, preferred_element_type=F32)
            dp = lax.dot_general(do, v, NT_DIMS, preferred_element_type=F32)
            ds = (p * (dp - d_ref[rows, :]) * ATTN_SCALE).astype(BF16)
            dk_acc[...] += lax.dot_general(ds, q, TN_DIMS, preferred_element_type=F32)
            dq_ref[0, rows, :] += jnp.dot(ds, k, preferred_element_type=F32)

        def above(i, carry):
            step(i, False)
            return carry

        step(j, True)
        lax.fori_loop(j + 1, n, above, 0)
        dk_ref[0] = dk_acc[...]
        dv_ref[0] = dv_acc[...]

    head = lambda w: pl.BlockSpec((1, S, w), lambda h, j: (h, 0, 0))
    blk = lambda w: pl.BlockSpec((1, T, w), lambda h, j: (h, j, 0))
    ospec = pl.BlockSpec((S, V_HEAD), lambda h, j: (0, h))
    return pl.pallas_call(
        body, name="attn_bwd", grid=(H, n),
        in_specs=[head(QK_CAT), blk(QK_CAT), blk(V_HEAD), ospec, ospec, head(1)],
        out_specs=[head(QK_CAT), blk(QK_CAT), blk(V_HEAD)],
        out_shape=[jax.ShapeDtypeStruct((H, S, QK_CAT), F32), jax.ShapeDtypeStruct((H, S, QK_CAT), F32),
                   jax.ShapeDtypeStruct((H, S, V_HEAD), F32)],
        scratch_shapes=[pltpu.VMEM((S, 1), F32), pltpu.VMEM((T, QK_CAT), F32), pltpu.VMEM((T, V_HEAD), F32)],
        compiler_params=_cparams(("parallel", "arbitrary")),
    )(qc, kc, vh, do, o, lse)


def _qk_bwd(dqc, dkc, dvh, cos_q, sin_q, cos_k, sin_k, ts):
    H, S, _ = dqc.shape
    nope_w, rope_w = H * QK_NOPE, H * QK_ROPE

    def body(dqc_ref, dkc_ref, dvh_ref, cq_ref, sq_ref, ck_ref, sk_ref, dq_ref, dkv_ref, dkr_ref, qr_buf, kr_buf):
        kr_sum = jnp.zeros((ts, QK_ROPE), F32)
        for h in range(H):
            dq_ref[:, h * QK_NOPE:(h + 1) * QK_NOPE] = dqc_ref[h, :, 0:QK_NOPE].astype(BF16)
            qr_buf[:, h * QK_ROPE:(h + 1) * QK_ROPE] = dqc_ref[h, :, QK_NOPE:QK_CAT]
            dkv_ref[:, h * QK_NOPE:(h + 1) * QK_NOPE] = dkc_ref[h, :, 0:QK_NOPE].astype(BF16)
            dkv_ref[:, nope_w + h * V_HEAD:nope_w + (h + 1) * V_HEAD] = dvh_ref[h].astype(BF16)
            kr_sum = kr_sum + dkc_ref[h, :, QK_NOPE:QK_CAT]
        qr = qr_buf[...]
        dq_ref[:, nope_w:] = (qr * cq_ref[...] + _rotate_half_pairs(qr, -1.0) * sq_ref[...]).astype(BF16)
        kr_buf[...] = jnp.zeros_like(kr_buf)
        kr_buf[:, 0:QK_ROPE] = kr_sum
        kr = kr_buf[...]
        dkr_ref[...] = (kr * ck_ref[...] + _rotate_half_pairs(kr, -1.0) * sk_ref[...]).astype(BF16)

    row = lambda w: pl.BlockSpec((ts, w), lambda i: (i, 0))
    head = lambda w: pl.BlockSpec((H, ts, w), lambda i: (0, i, 0))
    return pl.pallas_call(
        body, name="qk_bwd", grid=(S // ts,),
        in_specs=[head(QK_CAT), head(QK_CAT), head(V_HEAD), row(rope_w), row(rope_w), row(COL_BLOCK), row(COL_BLOCK)],
        out_specs=[row(nope_w + rope_w), row(nope_w + H * V_HEAD), row(COL_BLOCK)],
        out_shape=[jax.ShapeDtypeStruct((S, nope_w + rope_w), BF16), jax.ShapeDtypeStruct((S, nope_w + H * V_HEAD), BF16),
                   jax.ShapeDtypeStruct((S, COL_BLOCK), BF16)],
        scratch_shapes=[pltpu.VMEM((ts, rope_w), F32), pltpu.VMEM((ts, COL_BLOCK), F32)],
        compiler_params=_cparams(("parallel",)),
    )(dqc, dkc, dvh, cos_q, sin_q, cos_k, sin_k)


def _rms_bwd(dy, proj, g, blk, L, ts, name):
    S = proj.shape[0]

    def body(dy_ref, a_ref, g_ref, da_ref, dg_ref):
        i = pl.program_id(0)

        @pl.when(i == 0)
        def _():
            dg_ref[...] = jnp.zeros_like(dg_ref)

        a, dy = a_ref[...], dy_ref[...]
        r = lax.rsqrt(jnp.mean(a * a, axis=-1, keepdims=True) + RMS_EPS)
        dyh = dy * g_ref[...]
        da = r * dyh - a * (r * r * r) * jnp.mean(dyh * a, axis=-1, keepdims=True)
        da_ref[...] = da.astype(BF16)
        dg_ref[0:1, :] += jnp.sum(dy * a * r, axis=0, keepdims=True)

    return pl.pallas_call(
        body, name=name, grid=(S // ts,),
        in_specs=[pl.BlockSpec((ts, L), lambda i: (i, 0)), pl.BlockSpec((ts, L), lambda i: (i, blk)),
                  pl.BlockSpec((1, L), lambda i: (0, 0))],
        out_specs=[pl.BlockSpec((ts, L), lambda i: (i, 0)), pl.BlockSpec((8, L), lambda i: (0, 0))],
        out_shape=[jax.ShapeDtypeStruct((S, L), BF16), jax.ShapeDtypeStruct((8, L), F32)],
        compiler_params=_cparams(("arbitrary",)),
    )(dy, proj, g)


def _grad_x(du, dxa, x, mod, ts):
    S, D = x.shape

    def body(du_ref, dxa_ref, x_ref, mod_ref, dx_ref, vec_ref):
        i = pl.program_id(0)

        @pl.when(i == 0)
        def _():
            vec_ref[...] = jnp.zeros_like(vec_ref)

        du = du_ref[...]
        dx_ref[...] = dxa_ref[...] + du * (1.0 + mod_ref[1:2, :])
        vec_ref[0:1, :] += jnp.sum(du, axis=0, keepdims=True)
        vec_ref[1:2, :] += jnp.sum(du * x_ref[...], axis=0, keepdims=True)

    row = pl.BlockSpec((ts, D), lambda i: (i, 0))
    vec = lambda r: pl.BlockSpec((r, D), lambda i: (0, 0))
    return pl.pallas_call(
        body, name="grad_x", grid=(S // ts,),
        in_specs=[row, row, row, vec(6)],
        out_specs=[row, vec(8)],
        out_shape=[jax.ShapeDtypeStruct((S, D), F32), jax.ShapeDtypeStruct((8, D), F32)],
        compiler_params=_cparams(("arbitrary",)),
    )(du, dxa, x, mod)


def _adamw(w, g, m, v, name):
    R, C = w.shape
    tr = _tile(R, max(8, (1 << 19) // C), 8)
    c1 = 1.0 / (1.0 - ADAM_B1 ** ADAM_STEP)
    c2 = 1.0 / (1.0 - ADAM_B2 ** ADAM_STEP)

    def body(w_ref, g_ref, m_ref, v_ref, d_ref, nm_ref, nv_ref):
        g = g_ref[...]
        m = ADAM_B1 * m_ref[...] + (1.0 - ADAM_B1) * g
        v = ADAM_B2 * v_ref[...] + (1.0 - ADAM_B2) * (g * g)
        nm_ref[...] = m
        nv_ref[...] = v
        d_ref[...] = -ADAM_LR * ((m * c1) / (jnp.sqrt(v * c2) + ADAM_EPS) + ADAM_WD * w_ref[...])

    spec = pl.BlockSpec((tr, C), lambda i: (i, 0))
    out = jax.ShapeDtypeStruct((R, C), F32)
    return pl.pallas_call(
        body, name=name, grid=(R // tr,),
        in_specs=[spec] * 4, out_specs=[spec] * 3, out_shape=[out] * 3,
        compiler_params=_cparams(("parallel",)),
    )(w, g, m, v)


def _my_place():
    return lax.axis_index("x"), lax.axis_index("y"), lax.axis_index("c")


def _peer(k):
    x, y, c = _my_place()
    return (x ^ ((k >> 2) & 1), y ^ ((k >> 1) & 1), c ^ (k & 1))


def _linear(place):
    return 4 * place[0] + 2 * place[1] + place[2]


def _ada_fwd(c_row, wconv_row, w_ada, b_row):
    D, CW = w_ada.shape
    WC = wconv_row.shape[-1]

    def body(c_ref, wc_ref, w_ref, b_ref, mod_ref, cact_ref, wcall_ref, send_buf, sems):
        me = _linear(_my_place())
        c = c_ref[0]
        cact_ref[me] = c * _sigmoid(c)
        wcall_ref[me] = wc_ref[0]

        def gather_copy(buf, k, grp):
            return pltpu.make_async_remote_copy(
                src_ref=buf.at[me], dst_ref=buf.at[me], send_sem=sems.at[0, grp, k], recv_sem=sems.at[1, grp, k],
                device_id=_peer(k), device_id_type=MESH_ID)

        def gather_recv(buf, k, grp):
            src = _linear(_peer(k))
            return pltpu.make_async_remote_copy(
                src_ref=buf.at[src], dst_ref=buf.at[src], send_sem=sems.at[0, grp, k], recv_sem=sems.at[1, grp, k],
                device_id=_peer(k), device_id_type=MESH_ID)

        for k in range(1, N_DEV):
            gather_copy(cact_ref, k, 0).start()
            gather_copy(wcall_ref, k, 1).start()
        for k in range(1, N_DEV):
            gather_recv(cact_ref, k, 0).wait_recv()
            gather_recv(wcall_ref, k, 1).wait_recv()
        for k in range(1, N_DEV):
            gather_copy(cact_ref, k, 0).wait_send()
            gather_copy(wcall_ref, k, 1).wait_send()

        cact = jnp.concatenate([cact_ref[b] for b in range(N_DEV)], axis=0)
        mod_all = jnp.dot(cact.astype(BF16), w_ref[...].astype(BF16), preferred_element_type=F32) + b_ref[0]
        for b in range(N_DEV):
            send_buf[b] = mod_all[b:b + 1, :]
        mod_ref[me] = send_buf[me]

        def scatter_copy(k):
            dst = _linear(_peer(k))
            return pltpu.make_async_remote_copy(
                src_ref=send_buf.at[dst], dst_ref=mod_ref.at[me], send_sem=sems.at[0, 2, k], recv_sem=sems.at[1, 2, k],
                device_id=_peer(k), device_id_type=MESH_ID)

        def scatter_recv(k):
            src = _linear(_peer(k))
            return pltpu.make_async_remote_copy(
                src_ref=send_buf.at[src], dst_ref=mod_ref.at[src], send_sem=sems.at[0, 2, k], recv_sem=sems.at[1, 2, k],
                device_id=_peer(k), device_id_type=MESH_ID)

        for k in range(1, N_DEV):
            scatter_copy(k).start()
        for k in range(1, N_DEV):
            scatter_recv(k).wait_recv()
        for k in range(1, N_DEV):
            scatter_copy(k).wait_send()

    vmem = pl.BlockSpec(memory_space=pltpu.VMEM)
    return pl.pallas_call(
        body, name="ada_fwd",
        in_specs=[vmem] * 4, out_specs=[vmem] * 3,
        out_shape=[jax.ShapeDtypeStruct((N_DEV, 1, CW), F32), jax.ShapeDtypeStruct((N_DEV, 1, D), F32),
                   jax.ShapeDtypeStruct((N_DEV, 1, WC), F32)],
        scratch_shapes=[pltpu.VMEM((N_DEV, 1, CW), F32), pltpu.SemaphoreType.DMA((2, 3, N_DEV))],
        compiler_params=pltpu.CompilerParams(vmem_limit_bytes=VMEM_LIMIT),
    )(c_row, wconv_row, w_ada, b_row)


def _ada_bwd(payload, cact_t, n_mod):
    NCH, _, CW = payload.shape
    D = cact_t.shape[0]

    def body(p_ref, ct_ref, sum_ref, gw_ref, all_ref, sems):
        me = _linear(_my_place())
        all_ref[me] = p_ref[...]

        def copy(k, slot):
            return pltpu.make_async_remote_copy(
                src_ref=all_ref.at[slot], dst_ref=all_ref.at[slot], send_sem=sems.at[0, k], recv_sem=sems.at[1, k],
                device_id=_peer(k), device_id_type=MESH_ID)

        for k in range(1, N_DEV):
            copy(k, me).start()
        for k in range(1, N_DEV):
            copy(k, _linear(_peer(k))).wait_recv()
        for k in range(1, N_DEV):
            copy(k, me).wait_send()

        total = all_ref[0]
        for b in range(1, N_DEV):
            total = total + all_ref[b]
        sum_ref[...] = total

        ct = ct_ref[...].astype(BF16).astype(F32)
        gw = jnp.zeros((D, CW), F32)
        for b in range(N_DEV):
            dm = all_ref[b, me].astype(BF16).astype(F32)
            gw = gw + ct[:, b:b + 1] * dm
        gw_ref[...] = gw

    vmem = pl.BlockSpec(memory_space=pltpu.VMEM)
    return pl.pallas_call(
        body, name="ada_bwd",
        in_specs=[vmem, vmem], out_specs=[vmem, vmem],
        out_shape=[jax.ShapeDtypeStruct((NCH, 1, CW), F32), jax.ShapeDtypeStruct((D, CW), F32)],
        scratch_shapes=[pltpu.VMEM((N_DEV, NCH, 1, CW), F32), pltpu.SemaphoreType.DMA((2, N_DEV))],
        compiler_params=pltpu.CompilerParams(vmem_limit_bytes=VMEM_LIMIT),
    )(payload, cact_t)


def _all_gather(shard):
    R, C = shard.shape

    def body(x_ref, out_ref, send_sems, recv_sems):
        x, y, c = _my_place()
        me, sibling = (x, y, c), (x, y, 1 - c)
        chips = [(1 - x, y), (x, 1 - y), (1 - x, 1 - y)]

        def copy(k, block, to, src=None):
            slot = out_ref.at[_linear(block)]
            return pltpu.make_async_remote_copy(
                src_ref=slot if src is None else src, dst_ref=slot,
                send_sem=send_sems.at[k], recv_sem=recv_sems.at[k], device_id=to, device_id_type=MESH_ID)

        first = [copy(0, me, sibling, src=x_ref)]
        first += [copy(1 + j, me, (*chip, c), src=x_ref) for j, chip in enumerate(chips)]
        for cp in first:
            cp.start()
        passed = [copy(4 + j, (*chip, c), sibling) for j, chip in enumerate(chips)]
        for j, chip in enumerate(chips):
            copy(1 + j, (*chip, c), me).wait_recv()
            passed[j].start()
        copy(0, sibling, me).wait_recv()
        for j, chip in enumerate(chips):
            copy(4 + j, (*chip, 1 - c), me).wait_recv()
        for cp in first + passed:
            cp.wait_send()

    hbm = pl.BlockSpec(memory_space=pltpu.HBM)
    return pl.pallas_call(
        body, name="weight_all_gather",
        in_specs=[hbm], out_specs=hbm,
        out_shape=jax.ShapeDtypeStruct((N_DEV, R, C), shard.dtype),
        scratch_shapes=[pltpu.SemaphoreType.DMA((7,)), pltpu.SemaphoreType.DMA((7,))],
    )(shard)


def _exchange_in_chip(parts):
    _, R, C = parts.shape

    def body(p_ref, got_ref, send_sems, recv_sems):
        x, y, c = _my_place()
        sibling = (x, y, 1 - c)
        copies = []
        for q in range(4):
            copies.append(pltpu.make_async_remote_copy(
                src_ref=p_ref.at[2 * q + (1 - c)], dst_ref=got_ref.at[q],
                send_sem=send_sems.at[q], recv_sem=recv_sems.at[q], device_id=sibling, device_id_type=MESH_ID))
        for cp in copies:
            cp.start()
        for cp in copies:
            cp.wait_recv()
        for cp in copies:
            cp.wait_send()

    hbm = pl.BlockSpec(memory_space=pltpu.HBM)
    return pl.pallas_call(
        body, name="grad_exchange_in_chip",
        in_specs=[hbm], out_specs=hbm,
        out_shape=jax.ShapeDtypeStruct((4, R, C), parts.dtype),
        scratch_shapes=[pltpu.SemaphoreType.DMA((4,)), pltpu.SemaphoreType.DMA((4,))],
    )(parts)


def _pair_sum(parts, got, core):
    _, R, C = parts.shape
    tr = _tile(R, PACK_TILE_ROWS, PACK_ROW_ALIGN)

    def body(c_ref, p_ref, g_ref, o_ref):
        o_ref[...] = (p_ref[...].astype(F32) + g_ref[...].astype(F32)).astype(o_ref.dtype)

    return pl.pallas_call(
        body, name="grad_pair_sum",
        grid_spec=pltpu.PrefetchScalarGridSpec(
            num_scalar_prefetch=1, grid=(4, R // tr),
            in_specs=[pl.BlockSpec((1, tr, C), lambda q, i, c_ref: (2 * q + c_ref[0], i, 0)),
                      pl.BlockSpec((1, tr, C), lambda q, i, c_ref: (q, i, 0))],
            out_specs=pl.BlockSpec((1, tr, C), lambda q, i, c_ref: (q, i, 0))),
        out_shape=jax.ShapeDtypeStruct((4, R, C), parts.dtype),
        compiler_params=_cparams(("parallel", "parallel")),
    )(core, parts, got)


HBM_SPEC = pl.BlockSpec(memory_space=pltpu.HBM)
SEM_SPEC = pl.BlockSpec(memory_space=pltpu.SEMAPHORE)
ANY_SPEC = pl.BlockSpec(memory_space=pl.ANY)
SPLIT_EFFECT = pltpu.SideEffectType.DATAFLOW_SIDE_EFFECTING


def _landing_zone(shape, dtype):
    return pltpu.with_memory_space_constraint(lax.empty(shape, dtype), pltpu.HBM)


def _chip_peers():
    x, y, c = _my_place()
    return [(2 * (x ^ (k >> 1)) + (y ^ (k & 1)), (x ^ (k >> 1), y ^ (k & 1), c)) for k in range(1, 4)]


def _scatter_start(chip_parts, after, name):
    shape, dtype = chip_parts.shape, chip_parts.dtype

    def body(p_ref, land_ref, after_ref, send_sems, recv_sems, p_thru, land_thru, token):
        x, y, _ = _my_place()
        my_chip = 2 * x + y
        for k, (slot, dev) in enumerate(_chip_peers()):
            pltpu.make_async_remote_copy(
                src_ref=p_ref.at[slot], dst_ref=land_ref.at[my_chip], send_sem=send_sems.at[k], recv_sem=recv_sems.at[k],
                device_id=dev, device_id_type=MESH_ID).start()
        token[...] = jnp.zeros_like(token)

    return pl.pallas_call(
        body, name=name,
        out_shape=(pltpu.SemaphoreType.DMA((3,)), pltpu.SemaphoreType.DMA((3,)), pltpu.HBM(shape, dtype),
                   pltpu.HBM(shape, dtype), jax.ShapeDtypeStruct((8, LANE), F32)),
        in_specs=(HBM_SPEC, HBM_SPEC, ANY_SPEC),
        out_specs=(SEM_SPEC, SEM_SPEC, HBM_SPEC, HBM_SPEC, pl.BlockSpec(memory_space=pltpu.VMEM)),
        input_output_aliases={0: 2, 1: 3},
        compiler_params=pltpu.CompilerParams(has_side_effects=SPLIT_EFFECT),
    )(pltpu.with_memory_space_constraint(chip_parts, pltpu.HBM), _landing_zone(shape, dtype), after)


def _scatter_wait(send_sems, recv_sems, p_thru, land_thru, after, name):
    def body(p_ref, land_ref, send_sems, recv_sems, after_ref, p_dead, got_ref):
        for k, (slot, dev) in enumerate(_chip_peers()):
            cp = pltpu.make_async_remote_copy(
                src_ref=p_ref.at[slot], dst_ref=land_ref.at[slot], send_sem=send_sems.at[k], recv_sem=recv_sems.at[k],
                device_id=dev, device_id_type=MESH_ID)
            cp.wait_send()
            cp.wait_recv()

    return pl.pallas_call(
        body, name=name,
        out_shape=(pltpu.HBM(p_thru.shape, p_thru.dtype), pltpu.HBM(land_thru.shape, land_thru.dtype)),
        in_specs=(HBM_SPEC, HBM_SPEC, SEM_SPEC, SEM_SPEC, ANY_SPEC), out_specs=(HBM_SPEC, HBM_SPEC),
        input_output_aliases={0: 0, 1: 1},
        compiler_params=pltpu.CompilerParams(has_side_effects=SPLIT_EFFECT),
    )(p_thru, land_thru, send_sems, recv_sems, after)


def _gather_peers():
    x, y, c = _my_place()
    devs = [(x, y, 1 - c)] + [(x ^ (k >> 1), y ^ (k & 1), c) for k in range(1, 4)]
    return [(_linear(d), d) for d in devs]


def _gather_start(shard, after, name):
    R, C = shard.shape
    dtype = shard.dtype

    def body(x_ref, land_ref, after_ref, send_sems, recv_sems, x_thru, land_thru, token):
        me = _linear(_my_place())
        for k, (_, dev) in enumerate(_gather_peers()):
            pltpu.make_async_remote_copy(
                src_ref=x_ref, dst_ref=land_ref.at[me], send_sem=send_sems.at[k], recv_sem=recv_sems.at[k],
                device_id=dev, device_id_type=MESH_ID).start()
        token[...] = jnp.zeros_like(token)

    return pl.pallas_call(
        body, name=name,
        out_shape=(pltpu.SemaphoreType.DMA((4,)), pltpu.SemaphoreType.DMA((4,)), pltpu.HBM((R, C), dtype),
                   pltpu.HBM((N_DEV, R, C), dtype), jax.ShapeDtypeStruct((8, LANE), F32)),
        in_specs=(HBM_SPEC, HBM_SPEC, ANY_SPEC),
        out_specs=(SEM_SPEC, SEM_SPEC, HBM_SPEC, HBM_SPEC, pl.BlockSpec(memory_space=pltpu.VMEM)),
        input_output_aliases={0: 2, 1: 3},
        compiler_params=pltpu.CompilerParams(has_side_effects=SPLIT_EFFECT),
    )(pltpu.with_memory_space_constraint(shard, pltpu.HBM), _landing_zone((N_DEV, R, C), dtype), after)


def _gather_wait(send_sems, recv_sems, x_thru, land_thru, after, name):
    def body(x_ref, land_ref, send_sems, recv_sems, after_ref, x_out, got_ref):
        for k, (slot, dev) in enumerate(_gather_peers()):
            cp = pltpu.make_async_remote_copy(
                src_ref=x_ref, dst_ref=land_ref.at[slot], send_sem=send_sems.at[k], recv_sem=recv_sems.at[k],
                device_id=dev, device_id_type=MESH_ID)
            cp.wait_send()
            cp.wait_recv()

    return pl.pallas_call(
        body, name=name,
        out_shape=(pltpu.HBM(x_thru.shape, x_thru.dtype), pltpu.HBM(land_thru.shape, land_thru.dtype)),
        in_specs=(HBM_SPEC, HBM_SPEC, SEM_SPEC, SEM_SPEC, ANY_SPEC), out_specs=(HBM_SPEC, HBM_SPEC),
        input_output_aliases={0: 0, 1: 1},
        compiler_params=pltpu.CompilerParams(has_side_effects=SPLIT_EFFECT),
    )(x_thru, land_thru, send_sems, recv_sems, after)


def _gather_forward(land, name):
    def body(land_ref, out_ref, send_sems, recv_sems):
        x, y, c = _my_place()
        sibling = (x, y, 1 - c)
        sends, arrivals = [], []
        for k in range(1, 4):
            px, py = x ^ (k >> 1), y ^ (k & 1)
            landed, theirs = _linear((px, py, c)), out_ref.at[_linear((px, py, 1 - c))]
            sends.append(pltpu.make_async_remote_copy(
                src_ref=land_ref.at[landed], dst_ref=out_ref.at[landed],
                send_sem=send_sems.at[k - 1], recv_sem=recv_sems.at[k - 1],
                device_id=sibling, device_id_type=MESH_ID))
            arrivals.append(pltpu.make_async_remote_copy(
                src_ref=theirs, dst_ref=theirs, send_sem=send_sems.at[k - 1], recv_sem=recv_sems.at[k - 1],
                device_id=sibling, device_id_type=MESH_ID))
        for cp in sends:
            cp.start()
        for cp in arrivals:
            cp.wait_recv()
        for cp in sends:
            cp.wait_send()

    return pl.pallas_call(
        body, name=name,
        in_specs=[HBM_SPEC], out_specs=HBM_SPEC,
        out_shape=jax.ShapeDtypeStruct(land.shape, land.dtype),
        input_output_aliases={0: 0},
        scratch_shapes=[pltpu.SemaphoreType.DMA((3,)), pltpu.SemaphoreType.DMA((3,))],
    )(land)


def _chip_sum(got):
    _, R, C = got.shape
    tr = _tile(R, PACK_TILE_ROWS, PACK_ROW_ALIGN)

    def body(g_ref, o_ref):
        acc = g_ref[0].astype(F32)
        for q in range(1, 4):
            acc = acc + g_ref[q].astype(F32)
        o_ref[...] = acc

    return pl.pallas_call(
        body, name="grad_chip_sum", grid=(R // tr,),
        in_specs=[pl.BlockSpec((4, tr, C), lambda i: (0, i, 0))],
        out_specs=pl.BlockSpec((tr, C), lambda i: (i, 0)),
        out_shape=jax.ShapeDtypeStruct((R, C), F32),
        compiler_params=_cparams(("parallel",)),
    )(got)


def _pack_rows(n_elems):
    return _round_up(_round_up(n_elems, PACK_COLS) // PACK_COLS, PACK_ROW_ALIGN)


def _pack(flat_list, dtype):
    out = []
    for a in flat_list:
        n = a.shape[-1]
        rows = _pack_rows(n)
        pad = rows * PACK_COLS - n
        a = a.astype(dtype)
        if pad:
            a = jnp.pad(a, [(0, 0)] * (a.ndim - 1) + [(0, pad)])
        out.append(a.reshape(a.shape[:-1] + (rows, PACK_COLS)))
    total = sum(a.shape[-2] for a in out)
    fill = _round_up(total, PACK_ROW_BLOCK) - total
    if fill:
        out.append(jnp.zeros(out[0].shape[:-2] + (fill, PACK_COLS), dtype))
    return jnp.concatenate(out, axis=-2)


def _unpack(buf, sizes):
    out, r0 = [], 0
    for n in sizes:
        rows = _pack_rows(n)
        a = buf[..., r0:r0 + rows, :]
        out.append(a.reshape(a.shape[:-2] + (rows * PACK_COLS,))[..., :n])
        r0 += rows
    return out


def _cols_to_shards(w):
    K, N = w.shape
    return w.reshape(K, N_DEV, N // N_DEV).transpose(1, 0, 2).reshape(N_DEV, -1)


def _shards_to_cols(s, K):
    return s.reshape(N_DEV, K, -1).transpose(1, 0, 2).reshape(K, -1)


def _with_own_slot(gathered, shard):
    return lax.dynamic_update_index_in_dim(gathered, shard[None], _linear(_my_place()), axis=0)


def _reduce_scatter_begin(shard_lists, tag):
    parts = _pack(shard_lists, BF16)
    got = _exchange_in_chip(parts)
    chip_parts = _pair_sum(parts, got, lax.axis_index("c").astype(jnp.int32).reshape(1))
    return _scatter_start(chip_parts, got, "grad_scatter_start_" + tag)


def _reduce_scatter_end(state, after, sizes, tag):
    send_sems, recv_sems, p_thru, land_thru, _ = state
    chip_parts, got = _scatter_wait(send_sems, recv_sems, p_thru, land_thru, after, "grad_scatter_wait_" + tag)
    x, y, _ = _my_place()
    my_chip = 2 * x + y
    own = lax.dynamic_index_in_dim(chip_parts, my_chip, axis=0, keepdims=True)
    got = lax.dynamic_update_index_in_dim(got, own, my_chip, axis=0)
    return _unpack(_chip_sum(got), sizes)


def kernel(x, c, positions, w_ada, b_ada, w_in, g_q_a, w_q_b, g_kv_a, w_kv_b, w_o_a, w_conv, w_o_b, w_o, ln1_g, ln1_b, w_ffn_in, w_ffn_out, ln2_g, ln2_b, loss_target, m_w_ada, m_b_ada, m_w_in, m_g_q_a, m_w_q_b, m_g_kv_a, m_w_kv_b, m_w_o_a, m_w_conv, m_w_o_b, m_w_o, m_ln1_g, m_ln1_b, m_w_ffn_in, m_w_ffn_out, m_ln2_g, m_ln2_b, v_w_ada, v_b_ada, v_w_in, v_g_q_a, v_w_q_b, v_g_kv_a, v_w_kv_b, v_w_o_a, v_w_conv, v_w_o_b, v_w_o, v_ln1_g, v_ln1_b, v_w_ffn_in, v_w_ffn_out, v_ln2_g, v_ln2_b):
    x2, tgt = x[0], loss_target[0]
    S, D = x2.shape
    Lq, Lkv = g_q_a.shape[1], g_kv_a.shape[1]
    H = w_q_b.shape[2] * N_DEV // QK_CAT
    F = w_ffn_out.shape[1] * N_DEV
    n_in = w_in.shape[2] * N_DEV
    assert Lq == Lkv and (Lq + Lkv) % COL_BLOCK == 0 and D % COL_BLOCK == 0
    front = Lq + Lkv + QK_ROPE
    front_pad = _round_up(front, COL_BLOCK)
    kr_blk = (Lq + Lkv) // COL_BLOCK
    blk_b = front_pad // COL_BLOCK
    nblk = D // COL_BLOCK
    blk_c, blk_x, blk_ga, blk_gb = blk_b + nblk, blk_b + 2 * nblk, blk_b + 3 * nblk, blk_b + 4 * nblk
    ts = _tile(S, 256, 8)
    T = _tile(S, min(512, S // 2), CHUNK)
    tb = _tile(F, 512)
    me = _linear(_my_place())

    cw = w_ada.shape[2]
    b_mine = lax.dynamic_slice(b_ada, (0, me * cw), (1, cw)).reshape(1, 1, cw)
    mod_blocks, cact_all, wconv_all = _ada_fwd(c.reshape(1, 1, D), w_conv[0].reshape(1, 1, -1), w_ada[0], b_mine)
    mod = mod_blocks.reshape(6, D)
    cact_all = cact_all.reshape(N_DEV, D)
    w_conv_full = _shards_to_cols(wconv_all.reshape(N_DEV, -1), CONV_K)

    first, later = [w_in, w_q_b, w_kv_b], [w_o_a, w_o_b, w_o, w_ffn_in, w_ffn_out]
    size_of = lambda ws: [w.shape[1] * w.shape[2] for w in ws]
    first_shard = _pack([w[0].reshape(-1) for w in first], BF16)
    gathered = _with_own_slot(_all_gather(first_shard), first_shard)
    later_state = _gather_start(_pack([w[0].reshape(-1) for w in later], BF16), gathered, "weight_gather_start")
    later_token = later_state[4]
    g_in, g_qb, g_kvb = _unpack(gathered, size_of(first))
    w_in_f = _shards_to_cols(g_in, D)
    w_in_p = jnp.concatenate([w_in_f[:, :front], jnp.zeros((D, front_pad - front), BF16), w_in_f[:, front:]], axis=1)
    wq = _shards_to_cols(g_qb, Lq).reshape(Lq, H, QK_CAT)
    wq_p = jnp.concatenate([wq[:, :, :QK_NOPE].reshape(Lq, -1), wq[:, :, QK_NOPE:].reshape(Lq, -1)], axis=1)
    wkv = _shards_to_cols(g_kvb, Lkv).reshape(Lkv, H, QK_NOPE + V_HEAD)
    wkv_p = jnp.concatenate([wkv[:, :, :QK_NOPE].reshape(Lkv, -1), wkv[:, :, QK_NOPE:].reshape(Lkv, -1)], axis=1)

    inv_freq = 1.0 / (ROPE_THETA ** (jnp.arange(0, QK_ROPE, 2, dtype=F32) / QK_ROPE))
    ang = positions[0].astype(F32)[:, None] * inv_freq
    cos2 = jnp.concatenate([jnp.cos(ang), jnp.cos(ang)], axis=-1)
    sin2 = jnp.concatenate([jnp.sin(ang), jnp.sin(ang)], axis=-1)
    cos_q, sin_q = jnp.tile(cos2, (1, H)), jnp.tile(sin2, (1, H))
    cos_k, sin_k = jnp.tile(cos2, (1, COL_BLOCK // QK_ROPE)), jnp.tile(sin2, (1, COL_BLOCK // QK_ROPE))

    u = _modulate_in(x2, mod, ts)
    proj = _matmul(u, w_in_p, "nn", F32, "proj", deps=(later_token,))
    qn = _rms_fwd(proj, g_q_a, 0, Lq, ts, "rms_q")
    kvn = _rms_fwd(proj, g_kv_a, 1, Lkv, ts, "rms_kv")
    q = _matmul(qn, wq_p, "nn", F32, "q_up")
    kv = _matmul(kvn, wkv_p, "nn", F32, "kv_up")
    qc, kc, vh = _qk_prep(q, kv, proj, kr_blk, cos_q, sin_q, cos_k, sin_k, H, ts)
    attn, lse = _attn_fwd(qc, kc, vh, T)
    later_shard, later_land = _gather_wait(*later_state[:4], lse, "weight_gather_wait")
    later_all = _with_own_slot(_gather_forward(later_land, "weight_gather_forward"), later_shard)
    g_oa, g_ob, g_o, g_fi, g_fo = _unpack(later_all, size_of(later))
    w_oa_f, w_ob_f, w_o_f = g_oa.reshape(-1, D), g_ob.reshape(-1, D), g_o.reshape(-1, D)
    w_fi_f = _shards_to_cols(g_fi, D)
    w_fo_f = g_fo.reshape(F, D)
    ya = _matmul(attn, w_oa_f, "nn", F32, "attn_out")
    cbc = _conv_fwd(proj, w_conv_full, blk_b, blk_c, blk_x)
    yb = _matmul(cbc, w_ob_f, "nn", F32, "conv_out")
    merged = _merge_fwd(proj, ya, yb, blk_ga, blk_gb, ts)
    mix = _matmul(merged, w_o_f, "nn", F32, "mix_out")
    xhat1, rstd1, u2 = _ln1_fwd(x2, mix, mod, ln1_g, ln1_b, ts)
    hh = _matmul(u2, w_fi_f, "nn", F32, "ffn_in")
    act = _swiglu_fwd(hh, ts, tb)
    ffn = _matmul(act, w_fo_f, "nn", F32, "ffn_out")
    loss_part, dffn, dx1a, vec2 = _ln2_loss(xhat1, ffn, tgt, mod, ln1_g, ln1_b, ln2_g, ln2_b, ts)
    loss = lax.psum(loss_part[0, 0], AXES)

    gw_fo = _matmul(act, dffn, "tn", BF16, "grad_w_ffn_out")
    da = _matmul(dffn, w_fo_f, "nt", F32, "d_act")
    dhg, dhu = _swiglu_bwd(da, hh, ts, tb)
    dh = jnp.concatenate([dhg, dhu], axis=1)
    gw_fi = _matmul(u2, dh, "tn", BF16, "grad_w_ffn_in")
    ffn_state = _reduce_scatter_begin([_cols_to_shards(gw_fi), gw_fo.reshape(N_DEV, -1)], "ffn")
    du2 = _matmul(dh, w_fi_f, "nt", F32, "d_u2", deps=(ffn_state[4],))
    dxa, dmix, vec1 = _ln1_bwd(du2, dx1a, xhat1, rstd1, mix, mod, ln1_g, ln1_b, ts)
    gw_o = _matmul(merged, dmix, "tn", BF16, "grad_w_o")
    dmerged = _matmul(dmix, w_o_f, "nt", F32, "d_merged")
    dya, dyb, dga, dgb = _merge_bwd(dmerged, proj, ya, yb, blk_ga, blk_gb, ts)
    gw_ob = _matmul(cbc, dyb, "tn", BF16, "grad_w_o_b")
    dcbc = _matmul(dyb, w_ob_f, "nt", F32, "d_conv")
    dcb, dcc, dcx, dwconv = _conv_bwd(dcbc, proj, w_conv_full, blk_b, blk_c, blk_x)
    gw_oa = _matmul(attn, dya, "tn", BF16, "grad_w_o_a")
    mix_state = _reduce_scatter_begin([gw_oa.reshape(N_DEV, -1), gw_ob.reshape(N_DEV, -1), gw_o.reshape(N_DEV, -1)],
                                      "mix")
    dattn = _matmul(dya, w_oa_f, "nt", F32, "d_attn", deps=(mix_state[4],))
    dqc, dkc, dvh = _attn_bwd(qc, kc, vh, dattn, attn, lse, T)
    g_fi_s, g_fo_s = _reduce_scatter_end(ffn_state, dqc, size_of([w_ffn_in, w_ffn_out]), "ffn")
    g_oa_s, g_ob_s, g_o_s = _reduce_scatter_end(mix_state, dqc, size_of([w_o_a, w_o_b, w_o]), "mix")
    dq, dkv, dkr = _qk_bwd(dqc, dkc, dvh, cos_q, sin_q, cos_k, sin_k, ts)
    gw_qb_p = _matmul(qn, dq, "tn", BF16, "grad_w_q_b")
    dqn = _matmul(dq, wq_p, "nt", F32, "d_qn")
    gw_kvb_p = _matmul(kvn, dkv, "tn", BF16, "grad_w_kv_b")
    dkvn = _matmul(dkv, wkv_p, "nt", F32, "d_kvn")
    dqa, dgq = _rms_bwd(dqn, proj, g_q_a, 0, Lq, ts, "rms_q_bwd")
    dkva, dgkv = _rms_bwd(dkvn, proj, g_kv_a, 1, Lkv, ts, "rms_kv_bwd")
    dproj = jnp.concatenate([dqa, dkva, dkr, dcb, dcc, dcx, dga, dgb], axis=1)
    gw_in_p = _matmul(u, dproj, "tn", BF16, "grad_w_in")
    nq, nk = H * QK_NOPE, H * QK_NOPE
    gw_in = jnp.concatenate([gw_in_p[:, :front], gw_in_p[:, front_pad:]], axis=1)
    gw_qb = jnp.concatenate([gw_qb_p[:, :nq].reshape(Lq, H, QK_NOPE), gw_qb_p[:, nq:].reshape(Lq, H, QK_ROPE)],
                            axis=2).reshape(Lq, -1)
    gw_kvb = jnp.concatenate([gw_kvb_p[:, :nk].reshape(Lkv, H, QK_NOPE), gw_kvb_p[:, nk:].reshape(Lkv, H, V_HEAD)],
                             axis=2).reshape(Lkv, -1)
    in_state = _reduce_scatter_begin([_cols_to_shards(gw_in), _cols_to_shards(gw_qb), _cols_to_shards(gw_kvb)], "in")
    du = _matmul(dproj, w_in_p, "nt", F32, "d_u", deps=(in_state[4],))
    grad_x, vec0 = _grad_x(du, dxa, x2, mod, ts)

    n_mod = 6 * D // cw
    dmod = jnp.concatenate([vec0[0], vec0[1], vec1[4], vec1[0], vec1[1], vec2[2]])
    small = jnp.concatenate([dmod, dgq[0], dgkv[0], vec1[2], vec1[3], vec2[0], vec2[1], dwconv[:CONV_K].reshape(-1)])
    n_small = small.shape[0]
    nch = _round_up(n_small, cw) // cw
    payload = jnp.pad(small, (0, nch * cw - n_small)).reshape(nch, 1, cw)
    summed, g_w_ada = _ada_bwd(payload, cact_all.T, n_mod)
    summed = summed.reshape(-1)
    offs = [0, 6 * D, 6 * D + Lq, 6 * D + Lq + Lkv]
    offs += [offs[-1] + D * k for k in range(1, 5)]
    g_b_ada = summed[offs[0]:offs[1]].reshape(1, -1)
    g_gq = summed[offs[1]:offs[2]].reshape(1, -1)
    g_gkv = summed[offs[2]:offs[3]].reshape(1, -1)
    g_ln1g, g_ln1b, g_ln2g, g_ln2b = [summed[offs[3 + k]:offs[4 + k]].reshape(1, -1) for k in range(4)]
    wc = w_conv.shape[2]
    g_wconv = lax.dynamic_slice(summed[offs[7]:offs[7] + CONV_K * D].reshape(CONV_K, D), (0, me * wc), (CONV_K, wc))

    names = ["w_ada", "b_ada", "w_in", "g_q_a", "w_q_b", "g_kv_a", "w_kv_b", "w_o_a", "w_conv", "w_o_b", "w_o",
             "ln1_g", "ln1_b", "w_ffn_in", "w_ffn_out", "ln2_g", "ln2_b"]
    weights = [w_ada, b_ada, w_in, g_q_a, w_q_b, g_kv_a, w_kv_b, w_o_a, w_conv, w_o_b, w_o, ln1_g, ln1_b,
               w_ffn_in, w_ffn_out, ln2_g, ln2_b]
    moms = [m_w_ada, m_b_ada, m_w_in, m_g_q_a, m_w_q_b, m_g_kv_a, m_w_kv_b, m_w_o_a, m_w_conv, m_w_o_b, m_w_o,
            m_ln1_g, m_ln1_b, m_w_ffn_in, m_w_ffn_out, m_ln2_g, m_ln2_b]
    vels = [v_w_ada, v_b_ada, v_w_in, v_g_q_a, v_w_q_b, v_g_kv_a, v_w_kv_b, v_w_o_a, v_w_conv, v_w_o_b, v_w_o,
            v_ln1_g, v_ln1_b, v_w_ffn_in, v_w_ffn_out, v_ln2_g, v_ln2_b]
    grad_of = {"w_ada": g_w_ada, "b_ada": g_b_ada, "g_q_a": g_gq, "g_kv_a": g_gkv, "w_o_a": g_oa_s, "w_conv": g_wconv,
               "w_o_b": g_ob_s, "w_o": g_o_s, "ln1_g": g_ln1g, "ln1_b": g_ln1b, "w_ffn_in": g_fi_s,
               "w_ffn_out": g_fo_s, "ln2_g": g_ln2g, "ln2_b": g_ln2b}
    state_of = dict(zip(names, zip(weights, moms, vels)))
    results = {}

    def update(nm):
        w, m, v = state_of[nm]
        shp = w.shape
        w2 = w.reshape(shp[-2], shp[-1]) if w.ndim == 3 else w
        g2 = grad_of[nm].reshape(w2.shape)
        d, new_m, new_v = _adamw(w2, g2, m.reshape(w2.shape), v.reshape(w2.shape), "adamw_" + nm)
        results[nm] = [a.reshape(shp) for a in (g2, d, new_m, new_v)]

    last = ("w_in", "w_q_b", "w_kv_b")
    for nm in names:
        if nm not in last:
            update(nm)
    g_in_s, g_qb_s, g_kvb_s = _reduce_scatter_end(in_state, results["w_ada"][1], size_of(first), "in")
    grad_of.update({"w_in": g_in_s, "w_q_b": g_qb_s, "w_kv_b": g_kvb_s})
    for nm in last:
        update(nm)
    outs = [[results[nm][k] for nm in names] for k in range(4)]
    return (loss, grad_x.reshape(x.shape), *outs[0], *outs[1], *outs[2], *outs[3])
```

```python
import functools

import jax
import jax.numpy as jnp
from jax import lax
from jax.experimental import pallas as pl
from jax.experimental.pallas import tpu as pltpu

F32 = jnp.float32
BF16 = jnp.bfloat16
MESH_ID = pl.DeviceIdType.MESH
AXES = ("x", "y", "c")
N_DEV = 8

CHUNK = 64
QK_NOPE = 128
QK_ROPE = 64
V_HEAD = 128
QK_CAT = QK_NOPE + QK_ROPE
ROPE_THETA = 10000.0
ATTN_SCALE = (QK_NOPE + QK_ROPE) ** -0.5
CONV_K = 3
DEEPNORM_ALPHA = 2.0 ** 0.25
LN_EPS = 1e-5
RMS_EPS = 1e-6
NEG_INF = -1e30

ADAM_LR = 0.001
ADAM_B1 = 0.9
ADAM_B2 = 0.999
ADAM_EPS = 1e-08
ADAM_WD = 0.01
ADAM_STEP = 10

LANE = 128
COL_BLOCK = 256
PACK_ROW_ALIGN = 16
PAIR_SUM_BLOCK = 1 << 20
VMEM_LIMIT = 48 * 1024 * 1024


def _round_up(n, m):
    return (n + m - 1) // m * m


def _tile(n, pref, align=LANE):
    best = None
    t = align
    while t <= min(n, pref):
        if n % t == 0:
            best = t
        t += align
    return best if best is not None else n


def _cparams(sem=None):
    return pltpu.CompilerParams(dimension_semantics=sem, vmem_limit_bytes=VMEM_LIMIT)


def _sigmoid(x):
    return 1.0 / (1.0 + jnp.exp(-x))


def _matmul(a, b, mode, out_dtype, name, tm=1024, tn=1024, tk=512, deps=(), out_shards=False):
    b_shards = b.ndim == 3
    n = b.shape[2] if b_shards else (b.shape[1] // N_DEV if out_shards else None)
    if mode == "nn":
        (M, K), (K2, N) = a.shape, (b.shape[1], N_DEV * n) if b_shards else b.shape
    elif mode == "nt":
        (M, K), (N, K2) = a.shape, (b.shape[1], N_DEV * n) if b_shards else b.shape
    else:
        (K, M), (K2, N) = a.shape, b.shape
    assert K == K2, (a.shape, b.shape, mode)
    tm = _tile(M, tm)
    tn = n if (mode != "nt" and n is not None) else _tile(N, tn)
    tk = n if (mode == "nt" and b_shards) else _tile(K, tk)
    nk = K // tk
    if mode == "nn":
        a_spec = pl.BlockSpec((tm, tk), lambda i, j, k: (i, k))
        b_spec = (pl.BlockSpec((1, tk, n), lambda i, j, k: (j, k, 0)) if b_shards
                  else pl.BlockSpec((tk, tn), lambda i, j, k: (k, j)))
        dims = (((1,), (0,)), ((), ()))
    elif mode == "nt":
        a_spec = pl.BlockSpec((tm, tk), lambda i, j, k: (i, k))
        b_spec = (pl.BlockSpec((1, tn, n), lambda i, j, k: (k, j, 0)) if b_shards
                  else pl.BlockSpec((tn, tk), lambda i, j, k: (j, k)))
        dims = (((1,), (1,)), ((), ()))
    else:
        a_spec = pl.BlockSpec((tk, tm), lambda i, j, k: (k, i))
        b_spec = pl.BlockSpec((tk, tn), lambda i, j, k: (k, j))
        dims = (((0,), (0,)), ((), ()))
    if out_shards:
        out_spec = pl.BlockSpec((1, tm, n), lambda i, j, k: (j, i, 0))
        out_shape = jax.ShapeDtypeStruct((N_DEV, M, n), out_dtype)
    else:
        out_spec = pl.BlockSpec((tm, tn), lambda i, j, k: (i, j))
        out_shape = jax.ShapeDtypeStruct((M, N), out_dtype)

    def body(a_ref, b_ref, *rest):
        o_ref, acc_ref = rest[-2:]
        k = pl.program_id(2)

        @pl.when(k == 0)
        def _():
            acc_ref[...] = jnp.zeros_like(acc_ref)

        b_blk = b_ref[0] if b_shards else b_ref[...]
        acc_ref[...] += lax.dot_general(a_ref[...].astype(BF16), b_blk.astype(BF16), dims, preferred_element_type=F32)

        @pl.when(k == nk - 1)
        def _():
            if out_shards:
                o_ref[0] = acc_ref[...].astype(o_ref.dtype)
            else:
                o_ref[...] = acc_ref[...].astype(o_ref.dtype)

    return pl.pallas_call(
        body, name=name, grid=(M // tm, N // tn, nk),
        in_specs=[a_spec, b_spec] + [ANY_SPEC] * len(deps),
        out_specs=out_spec, out_shape=out_shape,
        scratch_shapes=[pltpu.VMEM((tm, tn), F32)],
        compiler_params=_cparams(("parallel", "parallel", "arbitrary")),
    )(a, b, *deps)


def _assemble_w_in(shards, front, front_pad):
    _, K, n = shards.shape
    gap = front_pad - front
    tk = _tile(K, 256, PACK_ROW_ALIGN)

    def body(g_ref, o_ref):
        if gap:
            o_ref[:, front:front_pad] = jnp.zeros((tk, gap), o_ref.dtype)
        for j in range(N_DEV):
            lo, hi = j * n, (j + 1) * n
            if lo < front < hi:
                o_ref[:, lo:front] = g_ref[j, :, 0:front - lo]
                o_ref[:, front_pad:hi + gap] = g_ref[j, :, front - lo:n]
            else:
                off = 0 if hi <= front else gap
                o_ref[:, lo + off:hi + off] = g_ref[j]

    return pl.pallas_call(
        body, name="assemble_w_in", grid=(K // tk,),
        in_specs=[pl.BlockSpec((N_DEV, tk, n), lambda i: (0, i, 0))],
        out_specs=pl.BlockSpec((tk, N_DEV * n + gap), lambda i: (i, 0)),
        out_shape=jax.ShapeDtypeStruct((K, N_DEV * n + gap), shards.dtype),
        compiler_params=_cparams(("parallel",)),
    )(shards)


def _split_w_in(w, front, front_pad):
    K, NP = w.shape
    gap = front_pad - front
    n = (NP - gap) // N_DEV
    tk = _tile(K, 256, PACK_ROW_ALIGN)

    def body(w_ref, o_ref):
        for j in range(N_DEV):
            lo, hi = j * n, (j + 1) * n
            if lo < front < hi:
                o_ref[j, :, 0:front - lo] = w_ref[:, lo:front]
                o_ref[j, :, front - lo:n] = w_ref[:, front_pad:hi + gap]
            else:
                off = 0 if hi <= front else gap
                o_ref[j] = w_ref[:, lo + off:hi + off]

    return pl.pallas_call(
        body, name="split_grad_w_in", grid=(K // tk,),
        in_specs=[pl.BlockSpec((tk, NP), lambda i: (i, 0))],
        out_specs=pl.BlockSpec((N_DEV, tk, n), lambda i: (0, i, 0)),
        out_shape=jax.ShapeDtypeStruct((N_DEV, K, n), w.dtype),
        compiler_params=_cparams(("parallel",)),
    )(w)


def _modulate_in(x, mod, ts):
    S, D = x.shape

    def body(x_ref, mod_ref, u_ref):
        u_ref[...] = (x_ref[...] * (1.0 + mod_ref[1:2, :]) + mod_ref[0:1, :]).astype(BF16)

    return pl.pallas_call(
        body, name="modulate_in", grid=(S // ts,),
        in_specs=[pl.BlockSpec((ts, D), lambda i: (i, 0)), pl.BlockSpec((6, D), lambda i: (0, 0))],
        out_specs=pl.BlockSpec((ts, D), lambda i: (i, 0)),
        out_shape=jax.ShapeDtypeStruct((S, D), BF16),
        compiler_params=_cparams(("parallel",)),
    )(x, mod)


def _rms_fwd(proj, g, blk, L, ts, name):
    S = proj.shape[0]

    def body(a_ref, g_ref, y_ref):
        a = a_ref[...]
        r = lax.rsqrt(jnp.mean(a * a, axis=-1, keepdims=True) + RMS_EPS)
        y_ref[...] = (a * r * g_ref[...]).astype(BF16)

    return pl.pallas_call(
        body, name=name, grid=(S // ts,),
        in_specs=[pl.BlockSpec((ts, L), lambda i: (i, blk)), pl.BlockSpec((1, L), lambda i: (0, 0))],
        out_specs=pl.BlockSpec((ts, L), lambda i: (i, 0)),
        out_shape=jax.ShapeDtypeStruct((S, L), BF16),
        compiler_params=_cparams(("parallel",)),
    )(proj, g)


def _rope_partner(x, period, start):
    w = x.shape[-1]
    lane = lax.broadcasted_iota(jnp.int32, x.shape, x.ndim - 1) % period
    first = (lane >= start) & (lane < start + QK_ROPE // 2)
    from_right = pltpu.roll(x, w - QK_ROPE // 2, axis=x.ndim - 1)
    from_left = pltpu.roll(x, QK_ROPE // 2, axis=x.ndim - 1)
    return jnp.where(first, -from_right, from_left)


def _qk_prep(q, kv, proj, kr_blk, cos_q, sin_q, cos_k, sin_k, H, ts):
    S = q.shape[0]
    pair = 2 * QK_CAT
    kv_w = QK_NOPE + V_HEAD

    def body(q_ref, kv_ref, kr_ref, cq_ref, sq_ref, ck_ref, sk_ref, qc_ref, kc_ref, vh_ref):
        kr = kr_ref[...]
        kr = kr * ck_ref[...] + _rope_partner(kr, QK_ROPE, 0) * sk_ref[...]
        kr = kr[:, :QK_ROPE].astype(BF16)
        for p in range(H // 2):
            x = q_ref[:, p * pair:(p + 1) * pair]
            x = x * cq_ref[...] + _rope_partner(x, QK_CAT, QK_NOPE) * sq_ref[...]
            qc_ref[2 * p] = x[:, :QK_CAT].astype(BF16)
            qc_ref[2 * p + 1] = x[:, QK_CAT:].astype(BF16)
        for h in range(H):
            kc_ref[h, :, 0:QK_NOPE] = kv_ref[:, h * kv_w:h * kv_w + QK_NOPE].astype(BF16)
            kc_ref[h, :, QK_NOPE:QK_CAT] = kr
            vh_ref[h, :, :] = kv_ref[:, h * kv_w + QK_NOPE:(h + 1) * kv_w].astype(BF16)

    row = lambda w: pl.BlockSpec((ts, w), lambda i: (i, 0))
    return pl.pallas_call(
        body, name="qk_prep", grid=(S // ts,),
        in_specs=[row(H * QK_CAT), row(H * kv_w),
                  pl.BlockSpec((ts, COL_BLOCK), lambda i: (i, kr_blk)),
                  row(pair), row(pair), row(COL_BLOCK), row(COL_BLOCK)],
        out_specs=[pl.BlockSpec((H, ts, QK_CAT), lambda i: (0, i, 0)),
                   pl.BlockSpec((H, ts, QK_CAT), lambda i: (0, i, 0)),
                   pl.BlockSpec((H, ts, V_HEAD), lambda i: (0, i, 0))],
        out_shape=[jax.ShapeDtypeStruct((H, S, QK_CAT), BF16), jax.ShapeDtypeStruct((H, S, QK_CAT), BF16),
                   jax.ShapeDtypeStruct((H, S, V_HEAD), BF16)],
        compiler_params=_cparams(("parallel",)),
    )(q, kv, proj, cos_q, sin_q, cos_k, sin_k)


NT_DIMS = (((1,), (1,)), ((), ()))
TN_DIMS = (((0,), (0,)), ((), ()))


def _diag_mask(T):
    rows = lax.broadcasted_iota(jnp.int32, (T, T), 0) // CHUNK
    cols = lax.broadcasted_iota(jnp.int32, (T, T), 1) // CHUNK
    return cols <= rows


def _attn_fwd(qc, kc, vh, T):
    H, S, _ = qc.shape
    n = S // T

    def body(q_ref, k_ref, v_ref, o_ref, lse_ref, m_ref, l_ref, acc_ref):
        i = pl.program_id(1)
        q = q_ref[0]
        m_ref[...] = jnp.full_like(m_ref, NEG_INF)
        l_ref[...] = jnp.zeros_like(l_ref)
        acc_ref[...] = jnp.zeros_like(acc_ref)

        def step(j, masked):
            rows = pl.ds(pl.multiple_of(j * T, T), T)
            s = lax.dot_general(q, k_ref[0, rows, :], NT_DIMS, preferred_element_type=F32) * ATTN_SCALE
            if masked:
                s = jnp.where(_diag_mask(T), s, NEG_INF)
            m_old = m_ref[...]
            m_new = jnp.maximum(m_old, jnp.max(s, axis=-1, keepdims=True))
            alpha = jnp.exp(m_old - m_new)
            p = jnp.exp(s - m_new)
            l_ref[...] = alpha * l_ref[...] + jnp.sum(p, axis=-1, keepdims=True)
            acc_ref[...] = alpha * acc_ref[...] + jnp.dot(p.astype(BF16), v_ref[0, rows, :],
                                                          preferred_element_type=F32)
            m_ref[...] = m_new

        def below(j, carry):
            step(j, False)
            return carry

        lax.fori_loop(0, i, below, 0)
        step(i, True)
        o_ref[...] = acc_ref[...] / l_ref[...]
        lse_ref[0] = m_ref[...] + jnp.log(l_ref[...])

    return pl.pallas_call(
        body, name="attn_fwd", grid=(H, n),
        in_specs=[pl.BlockSpec((1, T, QK_CAT), lambda h, i: (h, i, 0)),
                  pl.BlockSpec((1, S, QK_CAT), lambda h, i: (h, 0, 0)),
                  pl.BlockSpec((1, S, V_HEAD), lambda h, i: (h, 0, 0))],
        out_specs=[pl.BlockSpec((T, V_HEAD), lambda h, i: (i, h)),
                   pl.BlockSpec((1, T, 1), lambda h, i: (h, i, 0))],
        out_shape=[jax.ShapeDtypeStruct((S, H * V_HEAD), F32), jax.ShapeDtypeStruct((H, S, 1), F32)],
        scratch_shapes=[pltpu.VMEM((T, 1), F32), pltpu.VMEM((T, 1), F32), pltpu.VMEM((T, V_HEAD), F32)],
        compiler_params=_cparams(("parallel", "arbitrary")),
    )(qc, kc, vh)


def _shift_rows(z, k):
    if k == 0:
        return z
    n = z.shape[0]
    row = lax.broadcasted_iota(jnp.int32, z.shape, 0)
    if k > 0:
        return jnp.where(row >= k, pltpu.roll(z, k, axis=0), 0.0)
    return jnp.where(row < n + k, pltpu.roll(z, n + k, axis=0), 0.0)


def _conv_fwd(proj, w_conv, blk_b, blk_c, blk_x):
    S = proj.shape[0]
    D = w_conv.shape[1]
    nb = D // COL_BLOCK

    def body(cb_ref, cc_ref, cx_ref, w_ref, o_ref):
        z = cc_ref[...] * cx_ref[...]
        conv = w_ref[2:3, :] * z + w_ref[1:2, :] * _shift_rows(z, 1) + w_ref[0:1, :] * _shift_rows(z, 2)
        o_ref[...] = (cb_ref[...] * conv).astype(BF16)

    col = lambda off: pl.BlockSpec((S, COL_BLOCK), lambda j: (0, off + j))
    return pl.pallas_call(
        body, name="conv_fwd", grid=(nb,),
        in_specs=[col(blk_b), col(blk_c), col(blk_x), pl.BlockSpec((CONV_K, COL_BLOCK), lambda j: (0, j))],
        out_specs=pl.BlockSpec((S, COL_BLOCK), lambda j: (0, j)),
        out_shape=jax.ShapeDtypeStruct((S, D), BF16),
        compiler_params=_cparams(("parallel",)),
    )(proj, proj, proj, w_conv)


def _merge_fwd(proj, ya, yb, blk_ga, blk_gb, ts):
    S, D = ya.shape
    nb = D // COL_BLOCK

    def body(ga_ref, gb_ref, ya_ref, yb_ref, o_ref):
        o_ref[...] = (_sigmoid(ga_ref[...]) * ya_ref[...] + _sigmoid(gb_ref[...]) * yb_ref[...]).astype(BF16)

    blk = lambda off: pl.BlockSpec((ts, COL_BLOCK), lambda i, j: (i, off + j))
    return pl.pallas_call(
        body, name="merge_fwd", grid=(S // ts, nb),
        in_specs=[blk(blk_ga), blk(blk_gb), blk(0), blk(0)],
        out_specs=blk(0),
        out_shape=jax.ShapeDtypeStruct((S, D), BF16),
        compiler_params=_cparams(("parallel", "parallel")),
    )(proj, proj, ya, yb)


def _ln1_fwd(x, mix, mod, g, b, ts):
    S, D = x.shape

    def body(x_ref, mix_ref, mod_ref, g_ref, b_ref, xhat_ref, rstd_ref, u2_ref):
        r = DEEPNORM_ALPHA * x_ref[...] + mod_ref[2:3, :] * mix_ref[...]
        mu = jnp.mean(r, axis=-1, keepdims=True)
        d = r - mu
        rstd = lax.rsqrt(jnp.mean(d * d, axis=-1, keepdims=True) + LN_EPS)
        xhat = d * rstd
        xhat_ref[...] = xhat
        rstd_ref[...] = rstd
        x1 = xhat * g_ref[...] + b_ref[...]
        u2_ref[...] = (x1 * (1.0 + mod_ref[4:5, :]) + mod_ref[3:4, :]).astype(BF16)

    row = pl.BlockSpec((ts, D), lambda i: (i, 0))
    vec = lambda r: pl.BlockSpec((r, D), lambda i: (0, 0))
    return pl.pallas_call(
        body, name="ln1_fwd", grid=(S // ts,),
        in_specs=[row, row, vec(6), vec(1), vec(1)],
        out_specs=[row, pl.BlockSpec((ts, 1), lambda i: (i, 0)), row],
        out_shape=[jax.ShapeDtypeStruct((S, D), F32), jax.ShapeDtypeStruct((S, 1), F32),
                   jax.ShapeDtypeStruct((S, D), BF16)],
        compiler_params=_cparams(("parallel",)),
    )(x, mix, mod, g, b)


def _swiglu_fwd(h, ts, tb):
    S, F2 = h.shape
    F = F2 // 2
    nb = F // tb

    def body(hg_ref, hu_ref, a_ref):
        hg = hg_ref[...]
        a_ref[...] = (hg * _sigmoid(hg) * hu_ref[...]).astype(BF16)

    return pl.pallas_call(
        body, name="swiglu_fwd", grid=(S // ts, nb),
        in_specs=[pl.BlockSpec((ts, tb), lambda i, j: (i, j)), pl.BlockSpec((ts, tb), lambda i, j: (i, j + nb))],
        out_specs=pl.BlockSpec((ts, tb), lambda i, j: (i, j)),
        out_shape=jax.ShapeDtypeStruct((S, F), BF16),
        compiler_params=_cparams(("parallel", "parallel")),
    )(h, h)


def _ln2_loss(xhat1, ffn, tgt, mod, g1, b1, g2, b2, ts):
    S, D = xhat1.shape

    def body(xh_ref, ffn_ref, t_ref, mod_ref, g1_ref, b1_ref, g2_ref, b2_ref, loss_ref, dffn_ref, dx1_ref, vec_ref):
        i = pl.program_id(0)

        @pl.when(i == 0)
        def _():
            loss_ref[...] = jnp.zeros_like(loss_ref)
            vec_ref[...] = jnp.zeros_like(vec_ref)

        x1 = xh_ref[...] * g1_ref[...] + b1_ref[...]
        ffn = ffn_ref[...]
        r = DEEPNORM_ALPHA * x1 + mod_ref[5:6, :] * ffn
        mu = jnp.mean(r, axis=-1, keepdims=True)
        d = r - mu
        rstd = lax.rsqrt(jnp.mean(d * d, axis=-1, keepdims=True) + LN_EPS)
        xhat = d * rstd
        e = xhat * g2_ref[...] + b2_ref[...] - t_ref[...]
        loss_ref[...] += 0.5 * jnp.sum(jnp.mean(e * e, axis=-1, keepdims=True))
        dy = e * (1.0 / D)
        dxhat = dy * g2_ref[...]
        dr = rstd * (dxhat - jnp.mean(dxhat, axis=-1, keepdims=True)
                     - xhat * jnp.mean(dxhat * xhat, axis=-1, keepdims=True))
        dffn_ref[...] = (dr * mod_ref[5:6, :]).astype(BF16)
        dx1_ref[...] = DEEPNORM_ALPHA * dr
        vec_ref[0:1, :] += jnp.sum(dy * xhat, axis=0, keepdims=True)
        vec_ref[1:2, :] += jnp.sum(dy, axis=0, keepdims=True)
        vec_ref[2:3, :] += jnp.sum(dr * ffn, axis=0, keepdims=True)

    row = pl.BlockSpec((ts, D), lambda i: (i, 0))
    vec = lambda r: pl.BlockSpec((r, D), lambda i: (0, 0))
    return pl.pallas_call(
        body, name="ln2_loss", grid=(S // ts,),
        in_specs=[row, row, row, vec(6), vec(1), vec(1), vec(1), vec(1)],
        out_specs=[pl.BlockSpec((1, LANE), lambda i: (0, 0)), row, row, vec(8)],
        out_shape=[jax.ShapeDtypeStruct((1, LANE), F32), jax.ShapeDtypeStruct((S, D), BF16),
                   jax.ShapeDtypeStruct((S, D), F32), jax.ShapeDtypeStruct((8, D), F32)],
        compiler_params=_cparams(("arbitrary",)),
    )(xhat1, ffn, tgt, mod, g1, b1, g2, b2)


def _swiglu_bwd(da, h, ts, tb):
    S, F2 = h.shape
    nb = (F2 // 2) // tb

    def body(da_ref, hg_ref, hu_ref, dh_ref):
        hg, da = hg_ref[...], da_ref[...]
        sg = _sigmoid(hg)

        @pl.when(pl.program_id(2) == 0)
        def _():
            dh_ref[...] = (da * hu_ref[...] * (sg * (1.0 + hg * (1.0 - sg)))).astype(BF16)

        @pl.when(pl.program_id(2) == 1)
        def _():
            dh_ref[...] = (da * hg * sg).astype(BF16)

    lo = pl.BlockSpec((ts, tb), lambda i, j, k: (i, j))
    hi = pl.BlockSpec((ts, tb), lambda i, j, k: (i, j + nb))
    return pl.pallas_call(
        body, name="swiglu_bwd", grid=(S // ts, nb, 2),
        in_specs=[lo, lo, hi],
        out_specs=pl.BlockSpec((ts, tb), lambda i, j, k: (i, j + nb * k)),
        out_shape=jax.ShapeDtypeStruct((S, F2), BF16),
        compiler_params=_cparams(("parallel", "parallel", "arbitrary")),
    )(da, h, h)


def _ln1_bwd(du2, dx1a, xhat1, rstd1, mix, mod, g1, b1, ts):
    S, D = xhat1.shape

    def body(du2_ref, dx1a_ref, xh_ref, rstd_ref, mix_ref, mod_ref, g_ref, b_ref, dxa_ref, dmix_ref, vec_ref):
        i = pl.program_id(0)

        @pl.when(i == 0)
        def _():
            vec_ref[...] = jnp.zeros_like(vec_ref)

        xhat, du2, mix = xh_ref[...], du2_ref[...], mix_ref[...]
        x1 = xhat * g_ref[...] + b_ref[...]
        dx1 = dx1a_ref[...] + du2 * (1.0 + mod_ref[4:5, :])
        dxhat = dx1 * g_ref[...]
        dr = rstd_ref[...] * (dxhat - jnp.mean(dxhat, axis=-1, keepdims=True)
                              - xhat * jnp.mean(dxhat * xhat, axis=-1, keepdims=True))
        dxa_ref[...] = DEEPNORM_ALPHA * dr
        dmix_ref[...] = (dr * mod_ref[2:3, :]).astype(BF16)
        vec_ref[0:1, :] += jnp.sum(du2, axis=0, keepdims=True)
        vec_ref[1:2, :] += jnp.sum(du2 * x1, axis=0, keepdims=True)
        vec_ref[2:3, :] += jnp.sum(dx1 * xhat, axis=0, keepdims=True)
        vec_ref[3:4, :] += jnp.sum(dx1, axis=0, keepdims=True)
        vec_ref[4:5, :] += jnp.sum(dr * mix, axis=0, keepdims=True)

    row = pl.BlockSpec((ts, D), lambda i: (i, 0))
    vec = lambda r: pl.BlockSpec((r, D), lambda i: (0, 0))
    return pl.pallas_call(
        body, name="ln1_bwd", grid=(S // ts,),
        in_specs=[row, row, row, pl.BlockSpec((ts, 1), lambda i: (i, 0)), row, vec(6), vec(1), vec(1)],
        out_specs=[row, row, vec(8)],
        out_shape=[jax.ShapeDtypeStruct((S, D), F32), jax.ShapeDtypeStruct((S, D), BF16),
                   jax.ShapeDtypeStruct((8, D), F32)],
        compiler_params=_cparams(("arbitrary",)),
    )(du2, dx1a, xhat1, rstd1, mix, mod, g1, b1)


def _merge_bwd(dmerged, proj, ya, yb, blk_ga, blk_gb, ts):
    S, D = ya.shape
    nb = D // COL_BLOCK

    def body(dm_ref, ga_ref, gb_ref, ya_ref, yb_ref, dya_ref, dyb_ref, dga_ref, dgb_ref):
        dm = dm_ref[...]
        sa, sb = _sigmoid(ga_ref[...]), _sigmoid(gb_ref[...])
        dya_ref[...] = (dm * sa).astype(BF16)
        dyb_ref[...] = (dm * sb).astype(BF16)
        dga_ref[...] = (dm * ya_ref[...] * sa * (1.0 - sa)).astype(BF16)
        dgb_ref[...] = (dm * yb_ref[...] * sb * (1.0 - sb)).astype(BF16)

    blk = lambda off: pl.BlockSpec((ts, COL_BLOCK), lambda i, j: (i, off + j))
    out = jax.ShapeDtypeStruct((S, D), BF16)
    return pl.pallas_call(
        body, name="merge_bwd", grid=(S // ts, nb),
        in_specs=[blk(0), blk(blk_ga), blk(blk_gb), blk(0), blk(0)],
        out_specs=[blk(0)] * 4,
        out_shape=[out] * 4,
        compiler_params=_cparams(("parallel", "parallel")),
    )(dmerged, proj, proj, ya, yb)


def _conv_bwd(dcbc, proj, w_conv, blk_b, blk_c, blk_x):
    S = proj.shape[0]
    D = w_conv.shape[1]
    nb = D // COL_BLOCK

    def body(d_ref, cb_ref, cc_ref, cx_ref, w_ref, dcb_ref, dcc_ref, dcx_ref, dw_ref):
        d, cc, cx = d_ref[...], cc_ref[...], cx_ref[...]
        z = cc * cx
        z1, z2 = _shift_rows(z, 1), _shift_rows(z, 2)
        conv = w_ref[2:3, :] * z + w_ref[1:2, :] * z1 + w_ref[0:1, :] * z2
        dcb_ref[...] = (d * conv).astype(BF16)
        dconv = d * cb_ref[...]
        dz = w_ref[2:3, :] * dconv + w_ref[1:2, :] * _shift_rows(dconv, -1) + w_ref[0:1, :] * _shift_rows(dconv, -2)
        dcc_ref[...] = (dz * cx).astype(BF16)
        dcx_ref[...] = (dz * cc).astype(BF16)
        dw_ref[...] = jnp.zeros_like(dw_ref)
        dw_ref[0:1, :] = jnp.sum(dconv * z2, axis=0, keepdims=True)
        dw_ref[1:2, :] = jnp.sum(dconv * z1, axis=0, keepdims=True)
        dw_ref[2:3, :] = jnp.sum(dconv * z, axis=0, keepdims=True)

    col = lambda off: pl.BlockSpec((S, COL_BLOCK), lambda j: (0, off + j))
    out = jax.ShapeDtypeStruct((S, D), BF16)
    return pl.pallas_call(
        body, name="conv_bwd", grid=(nb,),
        in_specs=[col(0), col(blk_b), col(blk_c), col(blk_x), pl.BlockSpec((CONV_K, COL_BLOCK), lambda j: (0, j))],
        out_specs=[col(0), col(0), col(0), pl.BlockSpec((8, COL_BLOCK), lambda j: (0, j))],
        out_shape=[out, out, out, jax.ShapeDtypeStruct((8, D), F32)],
        compiler_params=_cparams(("parallel",)),
    )(dcbc, proj, proj, proj, w_conv)


def _attn_bwd(qc, kc, vh, do, o, lse, T):
    H, S, _ = qc.shape
    n = S // T

    def body(q_ref, k_ref, v_ref, do_ref, o_ref, lse_ref, dq_ref, dk_ref, dv_ref, d_ref, dk_acc, dv_acc):
        j = pl.program_id(1)

        @pl.when(j == 0)
        def _():
            dq_ref[...] = jnp.zeros_like(dq_ref)
            d_ref[...] = jnp.sum(do_ref[...] * o_ref[...], axis=-1, keepdims=True)

        dk_acc[...] = jnp.zeros_like(dk_acc)
        dv_acc[...] = jnp.zeros_like(dv_acc)
        k, v = k_ref[0], v_ref[0]

        def step(i, masked):
            rows = pl.ds(pl.multiple_of(i * T, T), T)
            q = q_ref[0, rows, :]
            do = do_ref[rows, :].astype(BF16)
            s = lax.dot_general(q, k, NT_DIMS, preferred_element_type=F32) * ATTN_SCALE
            if masked:
                s = jnp.where(_diag_mask(T), s, NEG_INF)
            p = jnp.exp(s - lse_ref[0, rows, :])
            dv_acc[...] += lax.dot_general(p.astype(BF16), do, TN_DIMS, preferred_element_type=F32)
            dp = lax.dot_general(do, v, NT_DIMS, preferred_element_type=F32)
            ds = (p * (dp - d_ref[rows, :]) * ATTN_SCALE).astype(BF16)
            dk_acc[...] += lax.dot_general(ds, q, TN_DIMS, preferred_element_type=F32)
            dq_ref[0, rows, :] += jnp.dot(ds, k, preferred_element_type=F32)

        def above(i, carry):
            step(i, False)
            return carry

        step(j, True)
        lax.fori_loop(j + 1, n, above, 0)
        dk_ref[0] = dk_acc[...]
        dv_ref[0] = dv_acc[...]

    head = lambda w: pl.BlockSpec((1, S, w), lambda h, j: (h, 0, 0))
    blk = lambda w: pl.BlockSpec((1, T, w), lambda h, j: (h, j, 0))
    ospec = pl.BlockSpec((S, V_HEAD), lambda h, j: (0, h))
    return pl.pallas_call(
        body, name="attn_bwd", grid=(H, n),
        in_specs=[head(QK_CAT), blk(QK_CAT), blk(V_HEAD), ospec, ospec, head(1)],
        out_specs=[head(QK_CAT), blk(QK_CAT), blk(V_HEAD)],
        out_shape=[jax.ShapeDtypeStruct((H, S, QK_CAT), F32), jax.ShapeDtypeStruct((H, S, QK_CAT), F32),
                   jax.ShapeDtypeStruct((H, S, V_HEAD), F32)],
        scratch_shapes=[pltpu.VMEM((S, 1), F32), pltpu.VMEM((T, QK_CAT), F32), pltpu.VMEM((T, V_HEAD), F32)],
        compiler_params=_cparams(("parallel", "arbitrary")),
    )(qc, kc, vh, do, o, lse)


def _qk_bwd(dqc, dkc, dvh, cos_q, sin_q, cos_k, sin_k, ts):
    H, S, _ = dqc.shape
    pair = 2 * QK_CAT
    kv_w = QK_NOPE + V_HEAD

    def body(dqc_ref, dkc_ref, dvh_ref, cq_ref, sq_ref, ck_ref, sk_ref, dq_ref, dkv_ref, dkr_ref, q_buf, kr_buf):
        for p in range(H // 2):
            q_buf[:, :QK_CAT] = dqc_ref[2 * p]
            q_buf[:, QK_CAT:] = dqc_ref[2 * p + 1]
            g = q_buf[...]
            dq_ref[:, p * pair:(p + 1) * pair] = (
                g * cq_ref[...] - _rope_partner(g, QK_CAT, QK_NOPE) * sq_ref[...]).astype(BF16)
        kr_sum = jnp.zeros((ts, QK_ROPE), F32)
        for h in range(H):
            dkv_ref[:, h * kv_w:h * kv_w + QK_NOPE] = dkc_ref[h, :, 0:QK_NOPE].astype(BF16)
            dkv_ref[:, h * kv_w + QK_NOPE:(h + 1) * kv_w] = dvh_ref[h].astype(BF16)
            kr_sum = kr_sum + dkc_ref[h, :, QK_NOPE:QK_CAT]
        kr_buf[...] = jnp.zeros_like(kr_buf)
        kr_buf[:, 0:QK_ROPE] = kr_sum
        kr = kr_buf[...]
        dkr_ref[...] = (kr * ck_ref[...] - _rope_partner(kr, QK_ROPE, 0) * sk_ref[...]).astype(BF16)

    row = lambda w: pl.BlockSpec((ts, w), lambda i: (i, 0))
    head = lambda w: pl.BlockSpec((H, ts, w), lambda i: (0, i, 0))
    return pl.pallas_call(
        body, name="qk_bwd", grid=(S // ts,),
        in_specs=[head(QK_CAT), head(QK_CAT), head(V_HEAD), row(pair), row(pair), row(COL_BLOCK), row(COL_BLOCK)],
        out_specs=[row(H * QK_CAT), row(H * kv_w), row(COL_BLOCK)],
        out_shape=[jax.ShapeDtypeStruct((S, H * QK_CAT), BF16), jax.ShapeDtypeStruct((S, H * kv_w), BF16),
                   jax.ShapeDtypeStruct((S, COL_BLOCK), BF16)],
        scratch_shapes=[pltpu.VMEM((ts, pair), F32), pltpu.VMEM((ts, COL_BLOCK), F32)],
        compiler_params=_cparams(("parallel",)),
    )(dqc, dkc, dvh, cos_q, sin_q, cos_k, sin_k)


def _rms_bwd(dy, proj, g, blk, L, ts, name):
    S = proj.shape[0]

    def body(dy_ref, a_ref, g_ref, da_ref, dg_ref):
        i = pl.program_id(0)

        @pl.when(i == 0)
        def _():
            dg_ref[...] = jnp.zeros_like(dg_ref)

        a, dy = a_ref[...], dy_ref[...]
        r = lax.rsqrt(jnp.mean(a * a, axis=-1, keepdims=True) + RMS_EPS)
        dyh = dy * g_ref[...]
        da = r * dyh - a * (r * r * r) * jnp.mean(dyh * a, axis=-1, keepdims=True)
        da_ref[...] = da.astype(BF16)
        dg_ref[0:1, :] += jnp.sum(dy * a * r, axis=0, keepdims=True)

    return pl.pallas_call(
        body, name=name, grid=(S // ts,),
        in_specs=[pl.BlockSpec((ts, L), lambda i: (i, 0)), pl.BlockSpec((ts, L), lambda i: (i, blk)),
                  pl.BlockSpec((1, L), lambda i: (0, 0))],
        out_specs=[pl.BlockSpec((ts, L), lambda i: (i, 0)), pl.BlockSpec((8, L), lambda i: (0, 0))],
        out_shape=[jax.ShapeDtypeStruct((S, L), BF16), jax.ShapeDtypeStruct((8, L), F32)],
        compiler_params=_cparams(("arbitrary",)),
    )(dy, proj, g)


def _grad_x(du, dxa, x, mod, ts):
    S, D = x.shape

    def body(du_ref, dxa_ref, x_ref, mod_ref, dx_ref, vec_ref):
        i = pl.program_id(0)

        @pl.when(i == 0)
        def _():
            vec_ref[...] = jnp.zeros_like(vec_ref)

        du = du_ref[...]
        dx_ref[...] = dxa_ref[...] + du * (1.0 + mod_ref[1:2, :])
        vec_ref[0:1, :] += jnp.sum(du, axis=0, keepdims=True)
        vec_ref[1:2, :] += jnp.sum(du * x_ref[...], axis=0, keepdims=True)

    row = pl.BlockSpec((ts, D), lambda i: (i, 0))
    vec = lambda r: pl.BlockSpec((r, D), lambda i: (0, 0))
    return pl.pallas_call(
        body, name="grad_x", grid=(S // ts,),
        in_specs=[row, row, row, vec(6)],
        out_specs=[row, vec(8)],
        out_shape=[jax.ShapeDtypeStruct((S, D), F32), jax.ShapeDtypeStruct((8, D), F32)],
        compiler_params=_cparams(("arbitrary",)),
    )(du, dxa, x, mod)


def _adamw(w, g, m, v, name):
    R, C = w.shape
    tr = _tile(R, max(8, (1 << 19) // C), 8)
    c1 = 1.0 / (1.0 - ADAM_B1 ** ADAM_STEP)
    c2 = 1.0 / (1.0 - ADAM_B2 ** ADAM_STEP)

    def body(w_ref, g_ref, m_ref, v_ref, d_ref, nm_ref, nv_ref):
        g = g_ref[...]
        m = ADAM_B1 * m_ref[...] + (1.0 - ADAM_B1) * g
        v = ADAM_B2 * v_ref[...] + (1.0 - ADAM_B2) * (g * g)
        nm_ref[...] = m
        nv_ref[...] = v
        d_ref[...] = -ADAM_LR * ((m * c1) / (jnp.sqrt(v * c2) + ADAM_EPS) + ADAM_WD * w_ref[...])

    spec = pl.BlockSpec((tr, C), lambda i: (i, 0))
    out = jax.ShapeDtypeStruct((R, C), F32)
    return pl.pallas_call(
        body, name=name, grid=(R // tr,),
        in_specs=[spec] * 4, out_specs=[spec] * 3, out_shape=[out] * 3,
        compiler_params=_cparams(("parallel",)),
    )(w, g, m, v)


def _adamw_reduced(w, own, got, m, v, my_chip, name):
    R, C = w.shape
    tr = _tile(R, max(PACK_ROW_ALIGN, (1 << 18) // C), PACK_ROW_ALIGN)
    c1 = 1.0 / (1.0 - ADAM_B1 ** ADAM_STEP)
    c2 = 1.0 / (1.0 - ADAM_B2 ** ADAM_STEP)

    def body(chip_ref, w_ref, own_ref, g1_ref, g2_ref, g3_ref, m_ref, v_ref, g_ref, d_ref, nm_ref, nv_ref):
        g = own_ref[0].astype(F32) + g1_ref[0].astype(F32) + g2_ref[0].astype(F32) + g3_ref[0].astype(F32)
        m = ADAM_B1 * m_ref[...] + (1.0 - ADAM_B1) * g
        v = ADAM_B2 * v_ref[...] + (1.0 - ADAM_B2) * (g * g)
        g_ref[...] = g
        nm_ref[...] = m
        nv_ref[...] = v
        d_ref[...] = -ADAM_LR * ((m * c1) / (jnp.sqrt(v * c2) + ADAM_EPS) + ADAM_WD * w_ref[...])

    spec = pl.BlockSpec((tr, C), lambda i, chip: (i, 0))
    slot = lambda k: pl.BlockSpec((1, tr, C), lambda i, chip: (chip[0] ^ k, i, 0))
    out = jax.ShapeDtypeStruct((R, C), F32)
    return pl.pallas_call(
        body, name=name,
        grid_spec=pltpu.PrefetchScalarGridSpec(
            num_scalar_prefetch=1, grid=(R // tr,),
            in_specs=[spec, slot(0), slot(1), slot(2), slot(3), spec, spec],
            out_specs=[spec] * 4),
        out_shape=[out] * 4,
        compiler_params=_cparams(("parallel",)),
    )(my_chip, w, own, got, got, got, m, v)


def _my_place():
    return lax.axis_index("x"), lax.axis_index("y"), lax.axis_index("c")


def _peer(k):
    x, y, c = _my_place()
    return (x ^ ((k >> 2) & 1), y ^ ((k >> 1) & 1), c ^ (k & 1))


def _linear(place):
    return 4 * place[0] + 2 * place[1] + place[2]


def _ada_fwd(c_row, wconv_row, w_ada, b_row):
    D, CW = w_ada.shape
    WC = wconv_row.shape[-1]

    def body(c_ref, wc_ref, w_ref, b_ref, mod_ref, cact_ref, wcall_ref, send_buf, sems):
        me = _linear(_my_place())
        c = c_ref[0]
        cact_ref[me] = c * _sigmoid(c)
        wcall_ref[me] = wc_ref[0]

        def gather_copy(buf, k, grp):
            return pltpu.make_async_remote_copy(
                src_ref=buf.at[me], dst_ref=buf.at[me], send_sem=sems.at[0, grp, k], recv_sem=sems.at[1, grp, k],
                device_id=_peer(k), device_id_type=MESH_ID)

        def gather_recv(buf, k, grp):
            src = _linear(_peer(k))
            return pltpu.make_async_remote_copy(
                src_ref=buf.at[src], dst_ref=buf.at[src], send_sem=sems.at[0, grp, k], recv_sem=sems.at[1, grp, k],
                device_id=_peer(k), device_id_type=MESH_ID)

        for k in range(1, N_DEV):
            gather_copy(cact_ref, k, 0).start()
            gather_copy(wcall_ref, k, 1).start()
        for k in range(1, N_DEV):
            gather_recv(cact_ref, k, 0).wait_recv()
            gather_recv(wcall_ref, k, 1).wait_recv()
        for k in range(1, N_DEV):
            gather_copy(cact_ref, k, 0).wait_send()
            gather_copy(wcall_ref, k, 1).wait_send()

        cact = jnp.concatenate([cact_ref[b] for b in range(N_DEV)], axis=0)
        mod_all = jnp.dot(cact.astype(BF16), w_ref[...].astype(BF16), preferred_element_type=F32) + b_ref[0]
        for b in range(N_DEV):
            send_buf[b] = mod_all[b:b + 1, :]
        mod_ref[me] = send_buf[me]

        def scatter_copy(k):
            dst = _linear(_peer(k))
            return pltpu.make_async_remote_copy(
                src_ref=send_buf.at[dst], dst_ref=mod_ref.at[me], send_sem=sems.at[0, 2, k], recv_sem=sems.at[1, 2, k],
                device_id=_peer(k), device_id_type=MESH_ID)

        def scatter_recv(k):
            src = _linear(_peer(k))
            return pltpu.make_async_remote_copy(
                src_ref=send_buf.at[src], dst_ref=mod_ref.at[src], send_sem=sems.at[0, 2, k], recv_sem=sems.at[1, 2, k],
                device_id=_peer(k), device_id_type=MESH_ID)

        for k in range(1, N_DEV):
            scatter_copy(k).start()
        for k in range(1, N_DEV):
            scatter_recv(k).wait_recv()
        for k in range(1, N_DEV):
            scatter_copy(k).wait_send()

    vmem = pl.BlockSpec(memory_space=pltpu.VMEM)
    return pl.pallas_call(
        body, name="ada_fwd",
        in_specs=[vmem] * 4, out_specs=[vmem] * 3,
        out_shape=[jax.ShapeDtypeStruct((N_DEV, 1, CW), F32), jax.ShapeDtypeStruct((N_DEV, 1, D), F32),
                   jax.ShapeDtypeStruct((N_DEV, 1, WC), F32)],
        scratch_shapes=[pltpu.VMEM((N_DEV, 1, CW), F32), pltpu.SemaphoreType.DMA((2, 3, N_DEV))],
        compiler_params=pltpu.CompilerParams(vmem_limit_bytes=VMEM_LIMIT),
    )(c_row, wconv_row, w_ada, b_row)


def _ada_bwd(payload, cact_t, n_mod):
    NCH, _, CW = payload.shape
    D = cact_t.shape[0]

    def body(p_ref, ct_ref, sum_ref, gw_ref, all_ref, sems):
        me = _linear(_my_place())
        all_ref[me] = p_ref[...]

        def copy(k, slot):
            return pltpu.make_async_remote_copy(
                src_ref=all_ref.at[slot], dst_ref=all_ref.at[slot], send_sem=sems.at[0, k], recv_sem=sems.at[1, k],
                device_id=_peer(k), device_id_type=MESH_ID)

        for k in range(1, N_DEV):
            copy(k, me).start()
        for k in range(1, N_DEV):
            copy(k, _linear(_peer(k))).wait_recv()
        for k in range(1, N_DEV):
            copy(k, me).wait_send()

        total = all_ref[0]
        for b in range(1, N_DEV):
            total = total + all_ref[b]
        sum_ref[...] = total

        ct = ct_ref[...].astype(BF16).astype(F32)
        gw = jnp.zeros((D, CW), F32)
        for b in range(N_DEV):
            dm = all_ref[b, me].astype(BF16).astype(F32)
            gw = gw + ct[:, b:b + 1] * dm
        gw_ref[...] = gw

    vmem = pl.BlockSpec(memory_space=pltpu.VMEM)
    return pl.pallas_call(
        body, name="ada_bwd",
        in_specs=[vmem, vmem], out_specs=[vmem, vmem],
        out_shape=[jax.ShapeDtypeStruct((NCH, 1, CW), F32), jax.ShapeDtypeStruct((D, CW), F32)],
        scratch_shapes=[pltpu.VMEM((N_DEV, NCH, 1, CW), F32), pltpu.SemaphoreType.DMA((2, N_DEV))],
        compiler_params=pltpu.CompilerParams(vmem_limit_bytes=VMEM_LIMIT),
    )(payload, cact_t)


def _all_gather(shards):
    W = len(shards)

    def body(*refs):
        x_refs, out_refs, (send_sems, recv_sems) = refs[:W], refs[W:2 * W], refs[2 * W:]
        x, y, c = _my_place()
        me, sibling = (x, y, c), (x, y, 1 - c)
        chips = [(1 - x, y), (x, 1 - y), (1 - x, 1 - y)]

        def copy(w, k, block, to, own=False):
            slot = out_refs[w].at[_linear(block)]
            return pltpu.make_async_remote_copy(
                src_ref=x_refs[w] if own else slot, dst_ref=slot,
                send_sem=send_sems.at[7 * w + k], recv_sem=recv_sems.at[7 * w + k], device_id=to, device_id_type=MESH_ID)

        first = [copy(w, 0, me, sibling, own=True) for w in range(W)]
        first += [copy(w, 1 + j, me, (*chip, c), own=True) for j, chip in enumerate(chips) for w in range(W)]
        for cp in first:
            cp.start()
        passed = []
        for j, chip in enumerate(chips):
            for w in range(W):
                copy(w, 1 + j, (*chip, c), me).wait_recv()
                passed.append(copy(w, 4 + j, (*chip, c), sibling))
                passed[-1].start()
        for w in range(W):
            copy(w, 0, sibling, me).wait_recv()
            for j, chip in enumerate(chips):
                copy(w, 4 + j, (*chip, 1 - c), me).wait_recv()
        for cp in first + passed:
            cp.wait_send()

    return pl.pallas_call(
        body, name="weight_all_gather",
        in_specs=[HBM_SPEC] * W, out_specs=[HBM_SPEC] * W,
        out_shape=[jax.ShapeDtypeStruct((N_DEV,) + s.shape, s.dtype) for s in shards],
        scratch_shapes=[pltpu.SemaphoreType.DMA((7 * W,)), pltpu.SemaphoreType.DMA((7 * W,))],
    )(*shards)


def _exchange_in_chip(parts):
    W = len(parts)

    def body(*refs):
        p_refs, got_refs, (send_sems, recv_sems) = refs[:W], refs[W:2 * W], refs[2 * W:]
        x, y, c = _my_place()
        sibling = (x, y, 1 - c)
        copies = []
        for w in range(W):
            for q in range(4):
                copies.append(pltpu.make_async_remote_copy(
                    src_ref=p_refs[w].at[2 * q + (1 - c)], dst_ref=got_refs[w].at[q],
                    send_sem=send_sems.at[4 * w + q], recv_sem=recv_sems.at[4 * w + q],
                    device_id=sibling, device_id_type=MESH_ID))
        for cp in copies:
            cp.start()
        for cp in copies:
            cp.wait_recv()
        for cp in copies:
            cp.wait_send()

    return pl.pallas_call(
        body, name="grad_exchange_in_chip",
        in_specs=[HBM_SPEC] * W, out_specs=[HBM_SPEC] * W,
        out_shape=[jax.ShapeDtypeStruct((4,) + p.shape[1:], p.dtype) for p in parts],
        scratch_shapes=[pltpu.SemaphoreType.DMA((4 * W,)), pltpu.SemaphoreType.DMA((4 * W,))],
    )(*parts)


def _pair_sum(parts, got, core):
    _, R, C = parts.shape
    tr = _tile(R, max(PACK_ROW_ALIGN, PAIR_SUM_BLOCK // C), PACK_ROW_ALIGN)

    def body(c_ref, p_ref, g_ref, o_ref):
        o_ref[...] = (p_ref[...].astype(F32) + g_ref[...].astype(F32)).astype(o_ref.dtype)

    return pl.pallas_call(
        body, name="grad_pair_sum",
        grid_spec=pltpu.PrefetchScalarGridSpec(
            num_scalar_prefetch=1, grid=(4, R // tr),
            in_specs=[pl.BlockSpec((1, tr, C), lambda q, i, c_ref: (2 * q + c_ref[0], i, 0)),
                      pl.BlockSpec((1, tr, C), lambda q, i, c_ref: (q, i, 0))],
            out_specs=pl.BlockSpec((1, tr, C), lambda q, i, c_ref: (q, i, 0))),
        out_shape=jax.ShapeDtypeStruct((4, R, C), parts.dtype),
        compiler_params=_cparams(("parallel", "parallel")),
    )(core, parts, got)


HBM_SPEC = pl.BlockSpec(memory_space=pltpu.HBM)
SEM_SPEC = pl.BlockSpec(memory_space=pltpu.SEMAPHORE)
ANY_SPEC = pl.BlockSpec(memory_space=pl.ANY)
SPLIT_EFFECT = pltpu.SideEffectType.DATAFLOW_SIDE_EFFECTING


def _landing_zone(shape, dtype):
    return pltpu.with_memory_space_constraint(lax.empty(shape, dtype), pltpu.HBM)


def _split_start(name, arrays, lands, after, copies_of, per_array):
    W = len(arrays)

    def body(*refs):
        x_refs, land_refs = refs[:W], refs[W:2 * W]
        send_sems, recv_sems = refs[2 * W + 1], refs[2 * W + 2]
        token = refs[-1]
        k = 0
        for w in range(W):
            for src, dst, dev in copies_of(w, x_refs[w], land_refs[w]):
                pltpu.make_async_remote_copy(src_ref=src, dst_ref=dst, send_sem=send_sems.at[k], recv_sem=recv_sems.at[k],
                                             device_id=dev, device_id_type=MESH_ID).start()
                k += 1
        token[...] = jnp.zeros_like(token)

    n_copies = per_array * W
    hbm_of = lambda xs: tuple(pltpu.HBM(a.shape, a.dtype) for a in xs)
    out = pl.pallas_call(
        body, name=name,
        out_shape=(pltpu.SemaphoreType.DMA((n_copies,)), pltpu.SemaphoreType.DMA((n_copies,)))
        + hbm_of(arrays) + hbm_of(lands) + (jax.ShapeDtypeStruct((8, LANE), F32),),
        in_specs=(HBM_SPEC,) * (2 * W) + (ANY_SPEC,),
        out_specs=(SEM_SPEC, SEM_SPEC) + (HBM_SPEC,) * (2 * W) + (pl.BlockSpec(memory_space=pltpu.VMEM),),
        input_output_aliases={i: 2 + i for i in range(2 * W)},
        compiler_params=pltpu.CompilerParams(has_side_effects=SPLIT_EFFECT),
    )(*[pltpu.with_memory_space_constraint(a, pltpu.HBM) for a in arrays], *lands, after)
    return out[0], out[1], list(out[2:2 + W]), list(out[2 + W:2 + 2 * W]), out[-1]


def _split_wait(name, state, after, copies_of):
    send_sems, recv_sems, arrays, lands, _ = state
    W = len(arrays)

    def body(*refs):
        x_refs, land_refs = refs[:W], refs[W:2 * W]
        send_sems, recv_sems = refs[2 * W], refs[2 * W + 1]
        k = 0
        for w in range(W):
            for src, dst, dev in copies_of(w, x_refs[w], land_refs[w]):
                cp = pltpu.make_async_remote_copy(src_ref=src, dst_ref=dst, send_sem=send_sems.at[k],
                                                  recv_sem=recv_sems.at[k], device_id=dev, device_id_type=MESH_ID)
                cp.wait_send()
                cp.wait_recv()
                k += 1

    out = pl.pallas_call(
        body, name=name,
        out_shape=tuple(pltpu.HBM(a.shape, a.dtype) for a in arrays + lands),
        in_specs=(HBM_SPEC,) * (2 * W) + (SEM_SPEC, SEM_SPEC, ANY_SPEC), out_specs=(HBM_SPEC,) * (2 * W),
        input_output_aliases={i: i for i in range(2 * W)},
        compiler_params=pltpu.CompilerParams(has_side_effects=SPLIT_EFFECT),
    )(*arrays, *lands, send_sems, recv_sems, after)
    return list(out[:W]), list(out[W:])


def _scatter_copies(w, p_ref, land_ref):
    x, y, c = _my_place()
    my_chip = 2 * x + y
    return [(p_ref.at[2 * (x ^ (k >> 1)) + (y ^ (k & 1))], land_ref.at[my_chip], (x ^ (k >> 1), y ^ (k & 1), c))
            for k in range(1, 4)]


def _gather_copies(w, x_ref, land_ref):
    x, y, c = _my_place()
    me = _linear((x, y, c))
    devs = [(x, y, 1 - c)] + [(x ^ (k >> 1), y ^ (k & 1), c) for k in range(1, 4)]
    return [(x_ref, land_ref.at[me], d) for d in devs]


def _gather_forward(lands, name):
    W = len(lands)

    def body(*refs):
        land_refs, out_refs, (send_sems, recv_sems) = refs[:W], refs[W:2 * W], refs[2 * W:]
        x, y, c = _my_place()
        sibling = (x, y, 1 - c)
        sends, arrivals = [], []
        for w in range(W):
            for k in range(1, 4):
                px, py = x ^ (k >> 1), y ^ (k & 1)
                landed, theirs = _linear((px, py, c)), out_refs[w].at[_linear((px, py, 1 - c))]
                sem = 3 * w + k - 1
                sends.append(pltpu.make_async_remote_copy(
                    src_ref=land_refs[w].at[landed], dst_ref=out_refs[w].at[landed],
                    send_sem=send_sems.at[sem], recv_sem=recv_sems.at[sem], device_id=sibling, device_id_type=MESH_ID))
                arrivals.append(pltpu.make_async_remote_copy(
                    src_ref=theirs, dst_ref=theirs, send_sem=send_sems.at[sem], recv_sem=recv_sems.at[sem],
                    device_id=sibling, device_id_type=MESH_ID))
        for cp in sends:
            cp.start()
        for cp in arrivals:
            cp.wait_recv()
        for cp in sends:
            cp.wait_send()

    return pl.pallas_call(
        body, name=name,
        in_specs=[HBM_SPEC] * W, out_specs=[HBM_SPEC] * W,
        out_shape=[jax.ShapeDtypeStruct(l.shape, l.dtype) for l in lands],
        input_output_aliases={i: i for i in range(W)},
        scratch_shapes=[pltpu.SemaphoreType.DMA((3 * W,)), pltpu.SemaphoreType.DMA((3 * W,))],
    )(*lands)


def _with_own_slot(gathered, shard):
    return lax.dynamic_update_index_in_dim(gathered, shard[None], _linear(_my_place()), axis=0)


def _reduce_scatter_begin(parts, tag):
    got = _exchange_in_chip(parts)
    core = lax.axis_index("c").astype(jnp.int32).reshape(1)
    chip_parts = [_pair_sum(p, g, core) for p, g in zip(parts, got)]
    lands = [_landing_zone(p.shape, p.dtype) for p in chip_parts]
    return _split_start("grad_scatter_start_" + tag, chip_parts, lands, got[0], _scatter_copies, 3)


def _reduce_scatter_end(state, after, tag):
    return _split_wait("grad_scatter_wait_" + tag, state, after, _scatter_copies)


def kernel(x, c, positions, w_ada, b_ada, w_in, g_q_a, w_q_b, g_kv_a, w_kv_b, w_o_a, w_conv, w_o_b, w_o, ln1_g, ln1_b, w_ffn_in, w_ffn_out, ln2_g, ln2_b, loss_target, m_w_ada, m_b_ada, m_w_in, m_g_q_a, m_w_q_b, m_g_kv_a, m_w_kv_b, m_w_o_a, m_w_conv, m_w_o_b, m_w_o, m_ln1_g, m_ln1_b, m_w_ffn_in, m_w_ffn_out, m_ln2_g, m_ln2_b, v_w_ada, v_b_ada, v_w_in, v_g_q_a, v_w_q_b, v_g_kv_a, v_w_kv_b, v_w_o_a, v_w_conv, v_w_o_b, v_w_o, v_ln1_g, v_ln1_b, v_w_ffn_in, v_w_ffn_out, v_ln2_g, v_ln2_b):
    x2, tgt = x[0], loss_target[0]
    S, D = x2.shape
    Lq, Lkv = g_q_a.shape[1], g_kv_a.shape[1]
    H = w_q_b.shape[2] * N_DEV // QK_CAT
    F = w_ffn_out.shape[1] * N_DEV
    assert Lq == Lkv and (Lq + Lkv) % COL_BLOCK == 0 and D % COL_BLOCK == 0
    front = Lq + Lkv + QK_ROPE
    front_pad = _round_up(front, COL_BLOCK)
    kr_blk = (Lq + Lkv) // COL_BLOCK
    blk_b = front_pad // COL_BLOCK
    nblk = D // COL_BLOCK
    blk_c, blk_x, blk_ga, blk_gb = blk_b + nblk, blk_b + 2 * nblk, blk_b + 3 * nblk, blk_b + 4 * nblk
    ts = _tile(S, 256, 8)
    T = _tile(S, min(512, S // 2), CHUNK)
    tb = _tile(F, 512)
    me = _linear(_my_place())

    cw = w_ada.shape[2]
    b_mine = lax.dynamic_slice(b_ada, (0, me * cw), (1, cw)).reshape(1, 1, cw)
    mod_blocks, cact_all, wconv_all = _ada_fwd(c.reshape(1, 1, D), w_conv[0].reshape(1, 1, -1), w_ada[0], b_mine)
    mod = mod_blocks.reshape(6, D)
    cact_all = cact_all.reshape(N_DEV, D)
    w_conv_full = wconv_all.reshape(N_DEV, CONV_K, -1).transpose(1, 0, 2).reshape(CONV_K, D)

    first = [w[0].astype(BF16) for w in (w_in, w_q_b, w_kv_b)]
    later = [w[0].astype(BF16) for w in (w_o_a, w_o_b, w_o, w_ffn_in, w_ffn_out)]
    g_in, wq_s, wkv_s = [_with_own_slot(g, s) for g, s in zip(_all_gather(first), first)]
    later_state = _split_start("weight_gather_start", later, [_landing_zone((N_DEV,) + s.shape, BF16) for s in later],
                               g_in, _gather_copies, 4)
    later_token = later_state[4]
    w_in_p = _assemble_w_in(g_in, front, front_pad)

    inv_freq = 1.0 / (ROPE_THETA ** (jnp.arange(0, QK_ROPE, 2, dtype=F32) / QK_ROPE))
    ang = positions[0].astype(F32)[:, None] * inv_freq
    cos2 = jnp.concatenate([jnp.cos(ang), jnp.cos(ang)], axis=-1)
    sin2 = jnp.concatenate([jnp.sin(ang), jnp.sin(ang)], axis=-1)
    one, zero = jnp.ones((S, QK_NOPE), F32), jnp.zeros((S, QK_NOPE), F32)
    cos_q, sin_q = jnp.concatenate([one, cos2, one, cos2], axis=-1), jnp.concatenate([zero, sin2, zero, sin2], axis=-1)
    cos_k, sin_k = jnp.tile(cos2, (1, COL_BLOCK // QK_ROPE)), jnp.tile(sin2, (1, COL_BLOCK // QK_ROPE))

    u = _modulate_in(x2, mod, ts)
    proj = _matmul(u, w_in_p, "nn", F32, "proj", deps=(later_token,))
    qn = _rms_fwd(proj, g_q_a, 0, Lq, ts, "rms_q")
    kvn = _rms_fwd(proj, g_kv_a, 1, Lkv, ts, "rms_kv")
    q = _matmul(qn, wq_s, "nn", F32, "q_up")
    kv = _matmul(kvn, wkv_s, "nn", F32, "kv_up")
    qc, kc, vh = _qk_prep(q, kv, proj, kr_blk, cos_q, sin_q, cos_k, sin_k, H, ts)
    attn, lse = _attn_fwd(qc, kc, vh, T)
    later_shards, later_lands = _split_wait("weight_gather_wait", later_state, lse, _gather_copies)
    later_all = _gather_forward(later_lands, "weight_gather_forward")
    g_oa, g_ob, g_o, w_fi_s, g_fo = [_with_own_slot(g, s) for g, s in zip(later_all, later_shards)]
    w_oa_f, w_ob_f, w_o_f = g_oa.reshape(-1, D), g_ob.reshape(-1, D), g_o.reshape(-1, D)
    w_fo_f = g_fo.reshape(F, D)
    ya = _matmul(attn, w_oa_f, "nn", F32, "attn_out")
    cbc = _conv_fwd(proj, w_conv_full, blk_b, blk_c, blk_x)
    yb = _matmul(cbc, w_ob_f, "nn", F32, "conv_out")
    merged = _merge_fwd(proj, ya, yb, blk_ga, blk_gb, ts)
    mix = _matmul(merged, w_o_f, "nn", F32, "mix_out")
    xhat1, rstd1, u2 = _ln1_fwd(x2, mix, mod, ln1_g, ln1_b, ts)
    hh = _matmul(u2, w_fi_s, "nn", F32, "ffn_in")
    act = _swiglu_fwd(hh, ts, tb)
    ffn = _matmul(act, w_fo_f, "nn", F32, "ffn_out")
    loss_part, dffn, dx1a, vec2 = _ln2_loss(xhat1, ffn, tgt, mod, ln1_g, ln1_b, ln2_g, ln2_b, ts)
    loss = lax.psum(loss_part[0, 0], AXES)

    gw_fo = _matmul(act, dffn, "tn", BF16, "grad_w_ffn_out")
    da = _matmul(dffn, w_fo_f, "nt", F32, "d_act")
    dh = _swiglu_bwd(da, hh, ts, tb)
    gw_fi = _matmul(u2, dh, "tn", BF16, "grad_w_ffn_in", out_shards=True)
    ffn_state = _reduce_scatter_begin([gw_fi, gw_fo.reshape(N_DEV, -1, D)], "ffn")
    du2 = _matmul(dh, w_fi_s, "nt", F32, "d_u2", deps=(ffn_state[4],))
    dxa, dmix, vec1 = _ln1_bwd(du2, dx1a, xhat1, rstd1, mix, mod, ln1_g, ln1_b, ts)
    gw_o = _matmul(merged, dmix, "tn", BF16, "grad_w_o")
    dmerged = _matmul(dmix, w_o_f, "nt", F32, "d_merged")
    dya, dyb, dga, dgb = _merge_bwd(dmerged, proj, ya, yb, blk_ga, blk_gb, ts)
    gw_ob = _matmul(cbc, dyb, "tn", BF16, "grad_w_o_b")
    dcbc = _matmul(dyb, w_ob_f, "nt", F32, "d_conv")
    dcb, dcc, dcx, dwconv = _conv_bwd(dcbc, proj, w_conv_full, blk_b, blk_c, blk_x)
    gw_oa = _matmul(attn, dya, "tn", BF16, "grad_w_o_a")
    mix_state = _reduce_scatter_begin([g.reshape(N_DEV, -1, D) for g in (gw_oa, gw_ob, gw_o)], "mix")
    dattn = _matmul(dya, w_oa_f, "nt", F32, "d_attn", deps=(mix_state[4],))
    dqc, dkc, dvh = _attn_bwd(qc, kc, vh, dattn, attn, lse, T)
    ffn_own, ffn_got = _reduce_scatter_end(ffn_state, dqc, "ffn")
    mix_own, mix_got = _reduce_scatter_end(mix_state, dqc, "mix")
    dq, dkv, dkr = _qk_bwd(dqc, dkc, dvh, cos_q, sin_q, cos_k, sin_k, ts)
    gw_qb = _matmul(qn, dq, "tn", BF16, "grad_w_q_b", out_shards=True)
    dqn = _matmul(dq, wq_s, "nt", F32, "d_qn")
    gw_kvb = _matmul(kvn, dkv, "tn", BF16, "grad_w_kv_b", out_shards=True)
    dkvn = _matmul(dkv, wkv_s, "nt", F32, "d_kvn")
    dqa, dgq = _rms_bwd(dqn, proj, g_q_a, 0, Lq, ts, "rms_q_bwd")
    dkva, dgkv = _rms_bwd(dkvn, proj, g_kv_a, 1, Lkv, ts, "rms_kv_bwd")
    dproj = jnp.concatenate([dqa, dkva, dkr, dcb, dcc, dcx, dga, dgb], axis=1)
    gw_in_p = _matmul(u, dproj, "tn", BF16, "grad_w_in")
    in_state = _reduce_scatter_begin([_split_w_in(gw_in_p, front, front_pad), gw_qb, gw_kvb], "in")
    du = _matmul(dproj, w_in_p, "nt", F32, "d_u", deps=(in_state[4],))
    grad_x, vec0 = _grad_x(du, dxa, x2, mod, ts)

    n_mod = 6 * D // cw
    dmod = jnp.concatenate([vec0[0], vec0[1], vec1[4], vec1[0], vec1[1], vec2[2]])
    small = jnp.concatenate([dmod, dgq[0], dgkv[0], vec1[2], vec1[3], vec2[0], vec2[1], dwconv[:CONV_K].reshape(-1)])
    n_small = small.shape[0]
    nch = _round_up(n_small, cw) // cw
    payload = jnp.pad(small, (0, nch * cw - n_small)).reshape(nch, 1, cw)
    summed, g_w_ada = _ada_bwd(payload, cact_all.T, n_mod)
    summed = summed.reshape(-1)
    offs = [0, 6 * D, 6 * D + Lq, 6 * D + Lq + Lkv]
    offs += [offs[-1] + D * k for k in range(1, 5)]
    g_b_ada = summed[offs[0]:offs[1]].reshape(1, -1)
    g_gq = summed[offs[1]:offs[2]].reshape(1, -1)
    g_gkv = summed[offs[2]:offs[3]].reshape(1, -1)
    g_ln1g, g_ln1b, g_ln2g, g_ln2b = [summed[offs[3 + k]:offs[4 + k]].reshape(1, -1) for k in range(4)]
    wc = w_conv.shape[2]
    g_wconv = lax.dynamic_slice(summed[offs[7]:offs[7] + CONV_K * D].reshape(CONV_K, D), (0, me * wc), (CONV_K, wc))

    names = ["w_ada", "b_ada", "w_in", "g_q_a", "w_q_b", "g_kv_a", "w_kv_b", "w_o_a", "w_conv", "w_o_b", "w_o",
             "ln1_g", "ln1_b", "w_ffn_in", "w_ffn_out", "ln2_g", "ln2_b"]
    weights = [w_ada, b_ada, w_in, g_q_a, w_q_b, g_kv_a, w_kv_b, w_o_a, w_conv, w_o_b, w_o, ln1_g, ln1_b,
               w_ffn_in, w_ffn_out, ln2_g, ln2_b]
    moms = [m_w_ada, m_b_ada, m_w_in, m_g_q_a, m_w_q_b, m_g_kv_a, m_w_kv_b, m_w_o_a, m_w_conv, m_w_o_b, m_w_o,
            m_ln1_g, m_ln1_b, m_w_ffn_in, m_w_ffn_out, m_ln2_g, m_ln2_b]
    vels = [v_w_ada, v_b_ada, v_w_in, v_g_q_a, v_w_q_b, v_g_kv_a, v_w_kv_b, v_w_o_a, v_w_conv, v_w_o_b, v_w_o,
            v_ln1_g, v_ln1_b, v_w_ffn_in, v_w_ffn_out, v_ln2_g, v_ln2_b]
    grad_of = {"w_ada": g_w_ada, "b_ada": g_b_ada, "g_q_a": g_gq, "g_kv_a": g_gkv, "w_conv": g_wconv,
               "ln1_g": g_ln1g, "ln1_b": g_ln1b, "ln2_g": g_ln2g, "ln2_b": g_ln2b}
    state_of = dict(zip(names, zip(weights, moms, vels)))
    results = {}
    my_chip = (2 * lax.axis_index("x") + lax.axis_index("y")).astype(jnp.int32).reshape(1)

    def update(nm, reduced=None):
        w, m, v = state_of[nm]
        shp = w.shape
        w2 = w.reshape(shp[-2], shp[-1]) if w.ndim == 3 else w
        m2, v2 = m.reshape(w2.shape), v.reshape(w2.shape)
        if reduced is None:
            g2 = grad_of[nm].reshape(w2.shape)
            res = (g2,) + tuple(_adamw(w2, g2, m2, v2, "adamw_" + nm))
        else:
            res = _adamw_reduced(w2, reduced[0], reduced[1], m2, v2, my_chip, "adamw_" + nm)
        results[nm] = [a.reshape(shp) for a in res]

    for nm in grad_of:
        update(nm)
    for nm, own, got in zip(("w_ffn_in", "w_ffn_out"), ffn_own, ffn_got):
        update(nm, (own, got))
    for nm, own, got in zip(("w_o_a", "w_o_b", "w_o"), mix_own, mix_got):
        update(nm, (own, got))
    in_own, in_got = _reduce_scatter_end(in_state, results["w_ada"][1], "in")
    for nm, own, got in zip(("w_in", "w_q_b", "w_kv_b"), in_own, in_got):
        update(nm, (own, got))
    outs = [[results[nm][k] for nm in names] for k in range(4)]
    return (loss, grad_x.reshape(x.shape), *outs[0], *outs[1], *outs[2], *outs[3])
```

```python
import functools

import jax
import jax.numpy as jnp
from jax import lax
from jax.experimental import pallas as pl
from jax.experimental.pallas import tpu as pltpu

F32 = jnp.float32
BF16 = jnp.bfloat16
MESH_ID = pl.DeviceIdType.MESH
AXES = ("x", "y", "c")
N_DEV = 8

CHUNK = 64
QK_NOPE = 128
QK_ROPE = 64
V_HEAD = 128
QK_CAT = QK_NOPE + QK_ROPE
ROPE_THETA = 10000.0
ATTN_SCALE = (QK_NOPE + QK_ROPE) ** -0.5
CONV_K = 3
DEEPNORM_ALPHA = 2.0 ** 0.25
LN_EPS = 1e-5
RMS_EPS = 1e-6
NEG_INF = -1e30

ADAM_LR = 0.001
ADAM_B1 = 0.9
ADAM_B2 = 0.999
ADAM_EPS = 1e-08
ADAM_WD = 0.01
ADAM_STEP = 10

LANE = 128
COL_BLOCK = 256
PACK_ROW_ALIGN = 16
PAIR_SUM_BLOCK = 1 << 20
VMEM_LIMIT = 48 * 1024 * 1024


def _round_up(n, m):
    return (n + m - 1) // m * m


def _tile(n, pref, align=LANE):
    best = None
    t = align
    while t <= min(n, pref):
        if n % t == 0:
            best = t
        t += align
    return best if best is not None else n


def _cparams(sem=None):
    return pltpu.CompilerParams(dimension_semantics=sem, vmem_limit_bytes=VMEM_LIMIT)


def _sigmoid(x):
    return 0.5 * jnp.tanh(0.5 * x) + 0.5


def _matmul(a, b, mode, out_dtype, name, tm=1024, tn=1024, tk=512, deps=(), out_shards=False):
    b_shards = b.ndim == 3
    n = b.shape[2] if b_shards else (b.shape[1] // N_DEV if out_shards else None)
    if mode == "nn":
        (M, K), (K2, N) = a.shape, (b.shape[1], N_DEV * n) if b_shards else b.shape
    elif mode == "nt":
        (M, K), (N, K2) = a.shape, (b.shape[1], N_DEV * n) if b_shards else b.shape
    else:
        (K, M), (K2, N) = a.shape, b.shape
    assert K == K2, (a.shape, b.shape, mode)
    tm = _tile(M, tm)
    tn = n if (mode != "nt" and n is not None) else _tile(N, tn)
    tk = n if (mode == "nt" and b_shards) else _tile(K, tk)
    nk = K // tk
    if mode == "nn":
        a_spec = pl.BlockSpec((tm, tk), lambda i, j, k: (i, k))
        b_spec = (pl.BlockSpec((1, tk, n), lambda i, j, k: (j, k, 0)) if b_shards
                  else pl.BlockSpec((tk, tn), lambda i, j, k: (k, j)))
        dims = (((1,), (0,)), ((), ()))
    elif mode == "nt":
        a_spec = pl.BlockSpec((tm, tk), lambda i, j, k: (i, k))
        b_spec = (pl.BlockSpec((1, tn, n), lambda i, j, k: (k, j, 0)) if b_shards
                  else pl.BlockSpec((tn, tk), lambda i, j, k: (j, k)))
        dims = (((1,), (1,)), ((), ()))
    else:
        a_spec = pl.BlockSpec((tk, tm), lambda i, j, k: (k, i))
        b_spec = pl.BlockSpec((tk, tn), lambda i, j, k: (k, j))
        dims = (((0,), (0,)), ((), ()))
    if out_shards:
        out_spec = pl.BlockSpec((1, tm, n), lambda i, j, k: (j, i, 0))
        out_shape = jax.ShapeDtypeStruct((N_DEV, M, n), out_dtype)
    else:
        out_spec = pl.BlockSpec((tm, tn), lambda i, j, k: (i, j))
        out_shape = jax.ShapeDtypeStruct((M, N), out_dtype)

    def body(a_ref, b_ref, *rest):
        o_ref, acc_ref = rest[-2:]
        k = pl.program_id(2)

        @pl.when(k == 0)
        def _():
            acc_ref[...] = jnp.zeros_like(acc_ref)

        b_blk = b_ref[0] if b_shards else b_ref[...]
        acc_ref[...] += lax.dot_general(a_ref[...].astype(BF16), b_blk.astype(BF16), dims, preferred_element_type=F32)

        @pl.when(k == nk - 1)
        def _():
            if out_shards:
                o_ref[0] = acc_ref[...].astype(o_ref.dtype)
            else:
                o_ref[...] = acc_ref[...].astype(o_ref.dtype)

    return pl.pallas_call(
        body, name=name, grid=(M // tm, N // tn, nk),
        in_specs=[a_spec, b_spec] + [ANY_SPEC] * len(deps),
        out_specs=out_spec, out_shape=out_shape,
        scratch_shapes=[pltpu.VMEM((tm, tn), F32)],
        compiler_params=_cparams(("parallel", "parallel", "arbitrary")),
    )(a, b, *deps)


def _assemble_w_in(shards, front, front_pad):
    _, K, n = shards.shape
    gap = front_pad - front
    tk = _tile(K, 256, PACK_ROW_ALIGN)

    def body(g_ref, o_ref):
        if gap:
            o_ref[:, front:front_pad] = jnp.zeros((tk, gap), o_ref.dtype)
        for j in range(N_DEV):
            lo, hi = j * n, (j + 1) * n
            if lo < front < hi:
                o_ref[:, lo:front] = g_ref[j, :, 0:front - lo]
                o_ref[:, front_pad:hi + gap] = g_ref[j, :, front - lo:n]
            else:
                off = 0 if hi <= front else gap
                o_ref[:, lo + off:hi + off] = g_ref[j]

    return pl.pallas_call(
        body, name="assemble_w_in", grid=(K // tk,),
        in_specs=[pl.BlockSpec((N_DEV, tk, n), lambda i: (0, i, 0))],
        out_specs=pl.BlockSpec((tk, N_DEV * n + gap), lambda i: (i, 0)),
        out_shape=jax.ShapeDtypeStruct((K, N_DEV * n + gap), shards.dtype),
        compiler_params=_cparams(("parallel",)),
    )(shards)


def _split_w_in(w, front, front_pad):
    K, NP = w.shape
    gap = front_pad - front
    n = (NP - gap) // N_DEV
    tk = _tile(K, 256, PACK_ROW_ALIGN)

    def body(w_ref, o_ref):
        for j in range(N_DEV):
            lo, hi = j * n, (j + 1) * n
            if lo < front < hi:
                o_ref[j, :, 0:front - lo] = w_ref[:, lo:front]
                o_ref[j, :, front - lo:n] = w_ref[:, front_pad:hi + gap]
            else:
                off = 0 if hi <= front else gap
                o_ref[j] = w_ref[:, lo + off:hi + off]

    return pl.pallas_call(
        body, name="split_grad_w_in", grid=(K // tk,),
        in_specs=[pl.BlockSpec((tk, NP), lambda i: (i, 0))],
        out_specs=pl.BlockSpec((N_DEV, tk, n), lambda i: (0, i, 0)),
        out_shape=jax.ShapeDtypeStruct((N_DEV, K, n), w.dtype),
        compiler_params=_cparams(("parallel",)),
    )(w)


def _modulate_in(x, mod, ts):
    S, D = x.shape

    def body(x_ref, mod_ref, u_ref):
        u_ref[...] = (x_ref[...] * (1.0 + mod_ref[1:2, :]) + mod_ref[0:1, :]).astype(BF16)

    return pl.pallas_call(
        body, name="modulate_in", grid=(S // ts,),
        in_specs=[pl.BlockSpec((ts, D), lambda i: (i, 0)), pl.BlockSpec((6, D), lambda i: (0, 0))],
        out_specs=pl.BlockSpec((ts, D), lambda i: (i, 0)),
        out_shape=jax.ShapeDtypeStruct((S, D), BF16),
        compiler_params=_cparams(("parallel",)),
    )(x, mod)


def _rms_fwd(proj, g, blk, L, ts, name):
    S = proj.shape[0]

    def body(a_ref, g_ref, y_ref):
        a = a_ref[...]
        r = lax.rsqrt(jnp.mean(a * a, axis=-1, keepdims=True) + RMS_EPS)
        y_ref[...] = (a * r * g_ref[...]).astype(BF16)

    return pl.pallas_call(
        body, name=name, grid=(S // ts,),
        in_specs=[pl.BlockSpec((ts, L), lambda i: (i, blk)), pl.BlockSpec((1, L), lambda i: (0, 0))],
        out_specs=pl.BlockSpec((ts, L), lambda i: (i, 0)),
        out_shape=jax.ShapeDtypeStruct((S, L), BF16),
        compiler_params=_cparams(("parallel",)),
    )(proj, g)


def _rope_partner(x, period, start):
    w = x.shape[-1]
    lane = lax.broadcasted_iota(jnp.int32, x.shape, x.ndim - 1) % period
    first = (lane >= start) & (lane < start + QK_ROPE // 2)
    from_right = pltpu.roll(x, w - QK_ROPE // 2, axis=x.ndim - 1)
    from_left = pltpu.roll(x, QK_ROPE // 2, axis=x.ndim - 1)
    return jnp.where(first, -from_right, from_left)


def _qk_prep(q, kv, proj, kr_blk, cos_q, sin_q, cos_k, sin_k, H, ts):
    S = q.shape[0]
    pair = 2 * QK_CAT
    kv_w = QK_NOPE + V_HEAD

    def body(q_ref, kv_ref, kr_ref, cq_ref, sq_ref, ck_ref, sk_ref, qc_ref, kc_ref, vh_ref):
        kr = kr_ref[...]
        kr = kr * ck_ref[...] + _rope_partner(kr, QK_ROPE, 0) * sk_ref[...]
        kr = kr[:, :QK_ROPE].astype(BF16)
        for p in range(H // 2):
            x = q_ref[:, p * pair:(p + 1) * pair]
            x = x * cq_ref[...] + _rope_partner(x, QK_CAT, QK_NOPE) * sq_ref[...]
            qc_ref[2 * p] = x[:, :QK_CAT].astype(BF16)
            qc_ref[2 * p + 1] = x[:, QK_CAT:].astype(BF16)
        for h in range(H):
            kc_ref[h, :, 0:QK_NOPE] = kv_ref[:, h * kv_w:h * kv_w + QK_NOPE].astype(BF16)
            kc_ref[h, :, QK_NOPE:QK_CAT] = kr
            vh_ref[h, :, :] = kv_ref[:, h * kv_w + QK_NOPE:(h + 1) * kv_w].astype(BF16)

    row = lambda w: pl.BlockSpec((ts, w), lambda i: (i, 0))
    return pl.pallas_call(
        body, name="qk_prep", grid=(S // ts,),
        in_specs=[row(H * QK_CAT), row(H * kv_w),
                  pl.BlockSpec((ts, COL_BLOCK), lambda i: (i, kr_blk)),
                  row(pair), row(pair), row(COL_BLOCK), row(COL_BLOCK)],
        out_specs=[pl.BlockSpec((H, ts, QK_CAT), lambda i: (0, i, 0)),
                   pl.BlockSpec((H, ts, QK_CAT), lambda i: (0, i, 0)),
                   pl.BlockSpec((H, ts, V_HEAD), lambda i: (0, i, 0))],
        out_shape=[jax.ShapeDtypeStruct((H, S, QK_CAT), BF16), jax.ShapeDtypeStruct((H, S, QK_CAT), BF16),
                   jax.ShapeDtypeStruct((H, S, V_HEAD), BF16)],
        compiler_params=_cparams(("parallel",)),
    )(q, kv, proj, cos_q, sin_q, cos_k, sin_k)


NT_DIMS = (((1,), (1,)), ((), ()))
TN_DIMS = (((0,), (0,)), ((), ()))


def _diag_mask(T):
    rows = lax.broadcasted_iota(jnp.int32, (T, T), 0) // CHUNK
    cols = lax.broadcasted_iota(jnp.int32, (T, T), 1) // CHUNK
    return cols <= rows


def _attn_fwd(qc, kc, vh, T):
    H, S, _ = qc.shape
    n = S // T

    def body(q_ref, k_ref, v_ref, o_ref, lse_ref, m_ref, l_ref, acc_ref):
        i = pl.program_id(1)
        q = q_ref[0]
        m_ref[...] = jnp.full_like(m_ref, NEG_INF)
        l_ref[...] = jnp.zeros_like(l_ref)
        acc_ref[...] = jnp.zeros_like(acc_ref)

        def step(j, masked):
            rows = pl.ds(pl.multiple_of(j * T, T), T)
            s = lax.dot_general(q, k_ref[0, rows, :], NT_DIMS, preferred_element_type=F32) * ATTN_SCALE
            if masked:
                s = jnp.where(_diag_mask(T), s, NEG_INF)
            m_old = m_ref[...]
            m_new = jnp.maximum(m_old, jnp.max(s, axis=-1, keepdims=True))
            alpha = jnp.exp(m_old - m_new)
            p = jnp.exp(s - m_new)
            l_ref[...] = alpha * l_ref[...] + jnp.sum(p, axis=-1, keepdims=True)
            acc_ref[...] = alpha * acc_ref[...] + jnp.dot(p.astype(BF16), v_ref[0, rows, :],
                                                          preferred_element_type=F32)
            m_ref[...] = m_new

        def below(j, carry):
            step(j, False)
            return carry

        lax.fori_loop(0, i, below, 0)
        step(i, True)
        o_ref[...] = acc_ref[...] / l_ref[...]
        lse_ref[0] = m_ref[...] + jnp.log(l_ref[...])

    return pl.pallas_call(
        body, name="attn_fwd", grid=(H, n),
        in_specs=[pl.BlockSpec((1, T, QK_CAT), lambda h, i: (h, i, 0)),
                  pl.BlockSpec((1, S, QK_CAT), lambda h, i: (h, 0, 0)),
                  pl.BlockSpec((1, S, V_HEAD), lambda h, i: (h, 0, 0))],
        out_specs=[pl.BlockSpec((T, V_HEAD), lambda h, i: (i, h)),
                   pl.BlockSpec((1, T, 1), lambda h, i: (h, i, 0))],
        out_shape=[jax.ShapeDtypeStruct((S, H * V_HEAD), F32), jax.ShapeDtypeStruct((H, S, 1), F32)],
        scratch_shapes=[pltpu.VMEM((T, 1), F32), pltpu.VMEM((T, 1), F32), pltpu.VMEM((T, V_HEAD), F32)],
        compiler_params=_cparams(("parallel", "arbitrary")),
    )(qc, kc, vh)


def _shift_rows(z, k):
    if k == 0:
        return z
    n = z.shape[0]
    row = lax.broadcasted_iota(jnp.int32, z.shape, 0)
    if k > 0:
        return jnp.where(row >= k, pltpu.roll(z, k, axis=0), 0.0)
    return jnp.where(row < n + k, pltpu.roll(z, n + k, axis=0), 0.0)


def _conv_fwd(proj, w_conv, blk_b, blk_c, blk_x):
    S = proj.shape[0]
    D = w_conv.shape[1]
    nb = D // COL_BLOCK

    def body(cb_ref, cc_ref, cx_ref, w_ref, o_ref):
        z = cc_ref[...] * cx_ref[...]
        conv = w_ref[2:3, :] * z + w_ref[1:2, :] * _shift_rows(z, 1) + w_ref[0:1, :] * _shift_rows(z, 2)
        o_ref[...] = (cb_ref[...] * conv).astype(BF16)

    col = lambda off: pl.BlockSpec((S, COL_BLOCK), lambda j: (0, off + j))
    return pl.pallas_call(
        body, name="conv_fwd", grid=(nb,),
        in_specs=[col(blk_b), col(blk_c), col(blk_x), pl.BlockSpec((CONV_K, COL_BLOCK), lambda j: (0, j))],
        out_specs=pl.BlockSpec((S, COL_BLOCK), lambda j: (0, j)),
        out_shape=jax.ShapeDtypeStruct((S, D), BF16),
        compiler_params=_cparams(("parallel",)),
    )(proj, proj, proj, w_conv)


def _merge_fwd(proj, ya, yb, blk_ga, blk_gb, ts):
    S, D = ya.shape
    nb = D // COL_BLOCK

    def body(ga_ref, gb_ref, ya_ref, yb_ref, o_ref):
        o_ref[...] = (_sigmoid(ga_ref[...]) * ya_ref[...] + _sigmoid(gb_ref[...]) * yb_ref[...]).astype(BF16)

    blk = lambda off: pl.BlockSpec((ts, COL_BLOCK), lambda i, j: (i, off + j))
    return pl.pallas_call(
        body, name="merge_fwd", grid=(S // ts, nb),
        in_specs=[blk(blk_ga), blk(blk_gb), blk(0), blk(0)],
        out_specs=blk(0),
        out_shape=jax.ShapeDtypeStruct((S, D), BF16),
        compiler_params=_cparams(("parallel", "parallel")),
    )(proj, proj, ya, yb)


def _ln1_fwd(x, mix, mod, g, b, ts):
    S, D = x.shape

    def body(x_ref, mix_ref, mod_ref, g_ref, b_ref, xhat_ref, rstd_ref, u2_ref):
        r = DEEPNORM_ALPHA * x_ref[...] + mod_ref[2:3, :] * mix_ref[...]
        mu = jnp.mean(r, axis=-1, keepdims=True)
        d = r - mu
        rstd = lax.rsqrt(jnp.mean(d * d, axis=-1, keepdims=True) + LN_EPS)
        xhat = d * rstd
        xhat_ref[...] = xhat
        rstd_ref[...] = rstd
        x1 = xhat * g_ref[...] + b_ref[...]
        u2_ref[...] = (x1 * (1.0 + mod_ref[4:5, :]) + mod_ref[3:4, :]).astype(BF16)

    row = pl.BlockSpec((ts, D), lambda i: (i, 0))
    vec = lambda r: pl.BlockSpec((r, D), lambda i: (0, 0))
    return pl.pallas_call(
        body, name="ln1_fwd", grid=(S // ts,),
        in_specs=[row, row, vec(6), vec(1), vec(1)],
        out_specs=[row, pl.BlockSpec((ts, 1), lambda i: (i, 0)), row],
        out_shape=[jax.ShapeDtypeStruct((S, D), F32), jax.ShapeDtypeStruct((S, 1), F32),
                   jax.ShapeDtypeStruct((S, D), BF16)],
        compiler_params=_cparams(("parallel",)),
    )(x, mix, mod, g, b)


def _swiglu_fwd(h, ts, tb):
    S, F2 = h.shape
    F = F2 // 2
    nb = F // tb

    def body(hg_ref, hu_ref, a_ref):
        hg = hg_ref[...]
        a_ref[...] = (hg * _sigmoid(hg) * hu_ref[...]).astype(BF16)

    return pl.pallas_call(
        body, name="swiglu_fwd", grid=(S // ts, nb),
        in_specs=[pl.BlockSpec((ts, tb), lambda i, j: (i, j)), pl.BlockSpec((ts, tb), lambda i, j: (i, j + nb))],
        out_specs=pl.BlockSpec((ts, tb), lambda i, j: (i, j)),
        out_shape=jax.ShapeDtypeStruct((S, F), BF16),
        compiler_params=_cparams(("parallel", "parallel")),
    )(h, h)


def _ln2_loss(xhat1, ffn, tgt, mod, g1, b1, g2, b2, ts):
    S, D = xhat1.shape

    def body(xh_ref, ffn_ref, t_ref, mod_ref, g1_ref, b1_ref, g2_ref, b2_ref, loss_ref, dffn_ref, dx1_ref, vec_ref):
        i = pl.program_id(0)

        @pl.when(i == 0)
        def _():
            loss_ref[...] = jnp.zeros_like(loss_ref)
            vec_ref[...] = jnp.zeros_like(vec_ref)

        x1 = xh_ref[...] * g1_ref[...] + b1_ref[...]
        ffn = ffn_ref[...]
        r = DEEPNORM_ALPHA * x1 + mod_ref[5:6, :] * ffn
        mu = jnp.mean(r, axis=-1, keepdims=True)
        d = r - mu
        rstd = lax.rsqrt(jnp.mean(d * d, axis=-1, keepdims=True) + LN_EPS)
        xhat = d * rstd
        e = xhat * g2_ref[...] + b2_ref[...] - t_ref[...]
        loss_ref[...] += 0.5 * jnp.sum(jnp.mean(e * e, axis=-1, keepdims=True))
        dy = e * (1.0 / D)
        dxhat = dy * g2_ref[...]
        dr = rstd * (dxhat - jnp.mean(dxhat, axis=-1, keepdims=True)
                     - xhat * jnp.mean(dxhat * xhat, axis=-1, keepdims=True))
        dffn_ref[...] = (dr * mod_ref[5:6, :]).astype(BF16)
        dx1_ref[...] = DEEPNORM_ALPHA * dr
        vec_ref[0:1, :] += jnp.sum(dy * xhat, axis=0, keepdims=True)
        vec_ref[1:2, :] += jnp.sum(dy, axis=0, keepdims=True)
        vec_ref[2:3, :] += jnp.sum(dr * ffn, axis=0, keepdims=True)

    row = pl.BlockSpec((ts, D), lambda i: (i, 0))
    vec = lambda r: pl.BlockSpec((r, D), lambda i: (0, 0))
    return pl.pallas_call(
        body, name="ln2_loss", grid=(S // ts,),
        in_specs=[row, row, row, vec(6), vec(1), vec(1), vec(1), vec(1)],
        out_specs=[pl.BlockSpec((1, LANE), lambda i: (0, 0)), row, row, vec(8)],
        out_shape=[jax.ShapeDtypeStruct((1, LANE), F32), jax.ShapeDtypeStruct((S, D), BF16),
                   jax.ShapeDtypeStruct((S, D), F32), jax.ShapeDtypeStruct((8, D), F32)],
        compiler_params=_cparams(("arbitrary",)),
    )(xhat1, ffn, tgt, mod, g1, b1, g2, b2)


def _swiglu_bwd(da, h, ts, tb):
    S, F2 = h.shape
    nb = (F2 // 2) // tb

    def body(da_ref, hg_ref, hu_ref, dh_ref):
        hg, da = hg_ref[...], da_ref[...]
        sg = _sigmoid(hg)

        @pl.when(pl.program_id(2) == 0)
        def _():
            dh_ref[...] = (da * hu_ref[...] * (sg * (1.0 + hg * (1.0 - sg)))).astype(BF16)

        @pl.when(pl.program_id(2) == 1)
        def _():
            dh_ref[...] = (da * hg * sg).astype(BF16)

    lo = pl.BlockSpec((ts, tb), lambda i, j, k: (i, j))
    hi = pl.BlockSpec((ts, tb), lambda i, j, k: (i, j + nb))
    return pl.pallas_call(
        body, name="swiglu_bwd", grid=(S // ts, nb, 2),
        in_specs=[lo, lo, hi],
        out_specs=pl.BlockSpec((ts, tb), lambda i, j, k: (i, j + nb * k)),
        out_shape=jax.ShapeDtypeStruct((S, F2), BF16),
        compiler_params=_cparams(("parallel", "parallel", "arbitrary")),
    )(da, h, h)


def _ln1_bwd(du2, dx1a, xhat1, rstd1, mix, mod, g1, b1, ts):
    S, D = xhat1.shape

    def body(du2_ref, dx1a_ref, xh_ref, rstd_ref, mix_ref, mod_ref, g_ref, b_ref, dxa_ref, dmix_ref, vec_ref):
        i = pl.program_id(0)

        @pl.when(i == 0)
        def _():
            vec_ref[...] = jnp.zeros_like(vec_ref)

        xhat, du2, mix = xh_ref[...], du2_ref[...], mix_ref[...]
        x1 = xhat * g_ref[...] + b_ref[...]
        dx1 = dx1a_ref[...] + du2 * (1.0 + mod_ref[4:5, :])
        dxhat = dx1 * g_ref[...]
        dr = rstd_ref[...] * (dxhat - jnp.mean(dxhat, axis=-1, keepdims=True)
                              - xhat * jnp.mean(dxhat * xhat, axis=-1, keepdims=True))
        dxa_ref[...] = DEEPNORM_ALPHA * dr
        dmix_ref[...] = (dr * mod_ref[2:3, :]).astype(BF16)
        vec_ref[0:1, :] += jnp.sum(du2, axis=0, keepdims=True)
        vec_ref[1:2, :] += jnp.sum(du2 * x1, axis=0, keepdims=True)
        vec_ref[2:3, :] += jnp.sum(dx1 * xhat, axis=0, keepdims=True)
        vec_ref[3:4, :] += jnp.sum(dx1, axis=0, keepdims=True)
        vec_ref[4:5, :] += jnp.sum(dr * mix, axis=0, keepdims=True)

    row = pl.BlockSpec((ts, D), lambda i: (i, 0))
    vec = lambda r: pl.BlockSpec((r, D), lambda i: (0, 0))
    return pl.pallas_call(
        body, name="ln1_bwd", grid=(S // ts,),
        in_specs=[row, row, row, pl.BlockSpec((ts, 1), lambda i: (i, 0)), row, vec(6), vec(1), vec(1)],
        out_specs=[row, row, vec(8)],
        out_shape=[jax.ShapeDtypeStruct((S, D), F32), jax.ShapeDtypeStruct((S, D), BF16),
                   jax.ShapeDtypeStruct((8, D), F32)],
        compiler_params=_cparams(("arbitrary",)),
    )(du2, dx1a, xhat1, rstd1, mix, mod, g1, b1)


def _merge_bwd(dmerged, proj, ya, yb, blk_ga, blk_gb, ts):
    S, D = ya.shape
    nb = D // COL_BLOCK

    def body(dm_ref, ga_ref, gb_ref, ya_ref, yb_ref, dya_ref, dyb_ref, dga_ref, dgb_ref):
        dm = dm_ref[...]
        sa, sb = _sigmoid(ga_ref[...]), _sigmoid(gb_ref[...])
        dya_ref[...] = (dm * sa).astype(BF16)
        dyb_ref[...] = (dm * sb).astype(BF16)
        dga_ref[...] = (dm * ya_ref[...] * sa * (1.0 - sa)).astype(BF16)
        dgb_ref[...] = (dm * yb_ref[...] * sb * (1.0 - sb)).astype(BF16)

    blk = lambda off: pl.BlockSpec((ts, COL_BLOCK), lambda i, j: (i, off + j))
    out = jax.ShapeDtypeStruct((S, D), BF16)
    return pl.pallas_call(
        body, name="merge_bwd", grid=(S // ts, nb),
        in_specs=[blk(0), blk(blk_ga), blk(blk_gb), blk(0), blk(0)],
        out_specs=[blk(0)] * 4,
        out_shape=[out] * 4,
        compiler_params=_cparams(("parallel", "parallel")),
    )(dmerged, proj, proj, ya, yb)


def _conv_bwd(dcbc, proj, w_conv, blk_b, blk_c, blk_x):
    S = proj.shape[0]
    D = w_conv.shape[1]
    nb = D // COL_BLOCK

    def body(d_ref, cb_ref, cc_ref, cx_ref, w_ref, dcb_ref, dcc_ref, dcx_ref, dw_ref):
        d, cc, cx = d_ref[...], cc_ref[...], cx_ref[...]
        z = cc * cx
        z1, z2 = _shift_rows(z, 1), _shift_rows(z, 2)
        conv = w_ref[2:3, :] * z + w_ref[1:2, :] * z1 + w_ref[0:1, :] * z2
        dcb_ref[...] = (d * conv).astype(BF16)
        dconv = d * cb_ref[...]
        dz = w_ref[2:3, :] * dconv + w_ref[1:2, :] * _shift_rows(dconv, -1) + w_ref[0:1, :] * _shift_rows(dconv, -2)
        dcc_ref[...] = (dz * cx).astype(BF16)
        dcx_ref[...] = (dz * cc).astype(BF16)
        dw_ref[...] = jnp.zeros_like(dw_ref)
        dw_ref[0:1, :] = jnp.sum(dconv * z2, axis=0, keepdims=True)
        dw_ref[1:2, :] = jnp.sum(dconv * z1, axis=0, keepdims=True)
        dw_ref[2:3, :] = jnp.sum(dconv * z, axis=0, keepdims=True)

    col = lambda off: pl.BlockSpec((S, COL_BLOCK), lambda j: (0, off + j))
    out = jax.ShapeDtypeStruct((S, D), BF16)
    return pl.pallas_call(
        body, name="conv_bwd", grid=(nb,),
        in_specs=[col(0), col(blk_b), col(blk_c), col(blk_x), pl.BlockSpec((CONV_K, COL_BLOCK), lambda j: (0, j))],
        out_specs=[col(0), col(0), col(0), pl.BlockSpec((8, COL_BLOCK), lambda j: (0, j))],
        out_shape=[out, out, out, jax.ShapeDtypeStruct((8, D), F32)],
        compiler_params=_cparams(("parallel",)),
    )(dcbc, proj, proj, proj, w_conv)


def _attn_bwd(qc, kc, vh, do, o, lse, T):
    H, S, _ = qc.shape
    n = S // T

    def body(q_ref, k_ref, v_ref, do_ref, o_ref, lse_ref, dq_ref, dk_ref, dv_ref, d_ref, dk_acc, dv_acc):
        j = pl.program_id(1)

        @pl.when(j == 0)
        def _():
            dq_ref[...] = jnp.zeros_like(dq_ref)
            d_ref[...] = jnp.sum(do_ref[...] * o_ref[...], axis=-1, keepdims=True)

        dk_acc[...] = jnp.zeros_like(dk_acc)
        dv_acc[...] = jnp.zeros_like(dv_acc)
        k, v = k_ref[0], v_ref[0]

        def step(i, masked):
            rows = pl.ds(pl.multiple_of(i * T, T), T)
            q = q_ref[0, rows, :]
            do = do_ref[rows, :].astype(BF16)
            s = lax.dot_general(q, k, NT_DIMS, preferred_element_type=F32) * ATTN_SCALE
            if masked:
                s = jnp.where(_diag_mask(T), s, NEG_INF)
            p = jnp.exp(s - lse_ref[0, rows, :])
            dv_acc[...] += lax.dot_general(p.astype(BF16), do, TN_DIMS, preferred_element_type=F32)
            dp = lax.dot_general(do, v, NT_DIMS, preferred_element_type=F32)
            ds = (p * (dp - d_ref[rows, :]) * ATTN_SCALE).astype(BF16)
            dk_acc[...] += lax.dot_general(ds, q, TN_DIMS, preferred_element_type=F32)
            dq_ref[0, rows, :] += jnp.dot(ds, k, preferred_element_type=F32)

        def above(i, carry):
            step(i, False)
            return carry

        step(j, True)
        lax.fori_loop(j + 1, n, above, 0)
        dk_ref[0] = dk_acc[...]
        dv_ref[0] = dv_acc[...]

    head = lambda w: pl.BlockSpec((1, S, w), lambda h, j: (h, 0, 0))
    blk = lambda w: pl.BlockSpec((1, T, w), lambda h, j: (h, j, 0))
    ospec = pl.BlockSpec((S, V_HEAD), lambda h, j: (0, h))
    return pl.pallas_call(
        body, name="attn_bwd", grid=(H, n),
        in_specs=[head(QK_CAT), blk(QK_CAT), blk(V_HEAD), ospec, ospec, head(1)],
        out_specs=[head(QK_CAT), blk(QK_CAT), blk(V_HEAD)],
        out_shape=[jax.ShapeDtypeStruct((H, S, QK_CAT), F32), jax.ShapeDtypeStruct((H, S, QK_CAT), F32),
                   jax.ShapeDtypeStruct((H, S, V_HEAD), F32)],
        scratch_shapes=[pltpu.VMEM((S, 1), F32), pltpu.VMEM((T, QK_CAT), F32), pltpu.VMEM((T, V_HEAD), F32)],
        compiler_params=_cparams(("parallel", "arbitrary")),
    )(qc, kc, vh, do, o, lse)


def _qk_bwd(dqc, dkc, dvh, cos_q, sin_q, cos_k, sin_k, ts):
    H, S, _ = dqc.shape
    pair = 2 * QK_CAT
    kv_w = QK_NOPE + V_HEAD

    def body(dqc_ref, dkc_ref, dvh_ref, cq_ref, sq_ref, ck_ref, sk_ref, dq_ref, dkv_ref, dkr_ref, q_buf, kr_buf):
        for p in range(H // 2):
            q_buf[:, :QK_CAT] = dqc_ref[2 * p]
            q_buf[:, QK_CAT:] = dqc_ref[2 * p + 1]
            g = q_buf[...]
            dq_ref[:, p * pair:(p + 1) * pair] = (
                g * cq_ref[...] - _rope_partner(g, QK_CAT, QK_NOPE) * sq_ref[...]).astype(BF16)
        kr_sum = jnp.zeros((ts, QK_ROPE), F32)
        for h in range(H):
            dkv_ref[:, h * kv_w:h * kv_w + QK_NOPE] = dkc_ref[h, :, 0:QK_NOPE].astype(BF16)
            dkv_ref[:, h * kv_w + QK_NOPE:(h + 1) * kv_w] = dvh_ref[h].astype(BF16)
            kr_sum = kr_sum + dkc_ref[h, :, QK_NOPE:QK_CAT]
        kr_buf[...] = jnp.zeros_like(kr_buf)
        kr_buf[:, 0:QK_ROPE] = kr_sum
        kr = kr_buf[...]
        dkr_ref[...] = (kr * ck_ref[...] - _rope_partner(kr, QK_ROPE, 0) * sk_ref[...]).astype(BF16)

    row = lambda w: pl.BlockSpec((ts, w), lambda i: (i, 0))
    head = lambda w: pl.BlockSpec((H, ts, w), lambda i: (0, i, 0))
    return pl.pallas_call(
        body, name="qk_bwd", grid=(S // ts,),
        in_specs=[head(QK_CAT), head(QK_CAT), head(V_HEAD), row(pair), row(pair), row(COL_BLOCK), row(COL_BLOCK)],
        out_specs=[row(H * QK_CAT), row(H * kv_w), row(COL_BLOCK)],
        out_shape=[jax.ShapeDtypeStruct((S, H * QK_CAT), BF16), jax.ShapeDtypeStruct((S, H * kv_w), BF16),
                   jax.ShapeDtypeStruct((S, COL_BLOCK), BF16)],
        scratch_shapes=[pltpu.VMEM((ts, pair), F32), pltpu.VMEM((ts, COL_BLOCK), F32)],
        compiler_params=_cparams(("parallel",)),
    )(dqc, dkc, dvh, cos_q, sin_q, cos_k, sin_k)


def _rms_bwd(dy, proj, g, blk, L, ts, name):
    S = proj.shape[0]

    def body(dy_ref, a_ref, g_ref, da_ref, dg_ref):
        i = pl.program_id(0)

        @pl.when(i == 0)
        def _():
            dg_ref[...] = jnp.zeros_like(dg_ref)

        a, dy = a_ref[...], dy_ref[...]
        r = lax.rsqrt(jnp.mean(a * a, axis=-1, keepdims=True) + RMS_EPS)
        dyh = dy * g_ref[...]
        da = r * dyh - a * (r * r * r) * jnp.mean(dyh * a, axis=-1, keepdims=True)
        da_ref[...] = da.astype(BF16)
        dg_ref[0:1, :] += jnp.sum(dy * a * r, axis=0, keepdims=True)

    return pl.pallas_call(
        body, name=name, grid=(S // ts,),
        in_specs=[pl.BlockSpec((ts, L), lambda i: (i, 0)), pl.BlockSpec((ts, L), lambda i: (i, blk)),
                  pl.BlockSpec((1, L), lambda i: (0, 0))],
        out_specs=[pl.BlockSpec((ts, L), lambda i: (i, 0)), pl.BlockSpec((8, L), lambda i: (0, 0))],
        out_shape=[jax.ShapeDtypeStruct((S, L), BF16), jax.ShapeDtypeStruct((8, L), F32)],
        compiler_params=_cparams(("arbitrary",)),
    )(dy, proj, g)


def _grad_x(du, dxa, x, mod, ts):
    S, D = x.shape

    def body(du_ref, dxa_ref, x_ref, mod_ref, dx_ref, vec_ref):
        i = pl.program_id(0)

        @pl.when(i == 0)
        def _():
            vec_ref[...] = jnp.zeros_like(vec_ref)

        du = du_ref[...]
        dx_ref[...] = dxa_ref[...] + du * (1.0 + mod_ref[1:2, :])
        vec_ref[0:1, :] += jnp.sum(du, axis=0, keepdims=True)
        vec_ref[1:2, :] += jnp.sum(du * x_ref[...], axis=0, keepdims=True)

    row = pl.BlockSpec((ts, D), lambda i: (i, 0))
    vec = lambda r: pl.BlockSpec((r, D), lambda i: (0, 0))
    return pl.pallas_call(
        body, name="grad_x", grid=(S // ts,),
        in_specs=[row, row, row, vec(6)],
        out_specs=[row, vec(8)],
        out_shape=[jax.ShapeDtypeStruct((S, D), F32), jax.ShapeDtypeStruct((8, D), F32)],
        compiler_params=_cparams(("arbitrary",)),
    )(du, dxa, x, mod)


def _adamw(w, g, m, v, name):
    R, C = w.shape
    tr = _tile(R, max(8, (1 << 19) // C), 8)
    c1 = 1.0 / (1.0 - ADAM_B1 ** ADAM_STEP)
    c2 = 1.0 / (1.0 - ADAM_B2 ** ADAM_STEP)

    def body(w_ref, g_ref, m_ref, v_ref, d_ref, nm_ref, nv_ref):
        g = g_ref[...]
        m = ADAM_B1 * m_ref[...] + (1.0 - ADAM_B1) * g
        v = ADAM_B2 * v_ref[...] + (1.0 - ADAM_B2) * (g * g)
        nm_ref[...] = m
        nv_ref[...] = v
        d_ref[...] = -ADAM_LR * ((m * c1) / (jnp.sqrt(v * c2) + ADAM_EPS) + ADAM_WD * w_ref[...])

    spec = pl.BlockSpec((tr, C), lambda i: (i, 0))
    out = jax.ShapeDtypeStruct((R, C), F32)
    return pl.pallas_call(
        body, name=name, grid=(R // tr,),
        in_specs=[spec] * 4, out_specs=[spec] * 3, out_shape=[out] * 3,
        compiler_params=_cparams(("parallel",)),
    )(w, g, m, v)


def _adamw_reduced(w, own, got, m, v, my_chip, name):
    R, C = w.shape
    tr = _tile(R, max(PACK_ROW_ALIGN, (1 << 18) // C), PACK_ROW_ALIGN)
    c1 = 1.0 / (1.0 - ADAM_B1 ** ADAM_STEP)
    c2 = 1.0 / (1.0 - ADAM_B2 ** ADAM_STEP)

    def body(chip_ref, w_ref, own_ref, g1_ref, g2_ref, g3_ref, m_ref, v_ref, g_ref, d_ref, nm_ref, nv_ref):
        g = own_ref[0].astype(F32) + g1_ref[0].astype(F32) + g2_ref[0].astype(F32) + g3_ref[0].astype(F32)
        m = ADAM_B1 * m_ref[...] + (1.0 - ADAM_B1) * g
        v = ADAM_B2 * v_ref[...] + (1.0 - ADAM_B2) * (g * g)
        g_ref[...] = g
        nm_ref[...] = m
        nv_ref[...] = v
        d_ref[...] = -ADAM_LR * ((m * c1) / (jnp.sqrt(v * c2) + ADAM_EPS) + ADAM_WD * w_ref[...])

    spec = pl.BlockSpec((tr, C), lambda i, chip: (i, 0))
    slot = lambda k: pl.BlockSpec((1, tr, C), lambda i, chip: (chip[0] ^ k, i, 0))
    out = jax.ShapeDtypeStruct((R, C), F32)
    return pl.pallas_call(
        body, name=name,
        grid_spec=pltpu.PrefetchScalarGridSpec(
            num_scalar_prefetch=1, grid=(R // tr,),
            in_specs=[spec, slot(0), slot(1), slot(2), slot(3), spec, spec],
            out_specs=[spec] * 4),
        out_shape=[out] * 4,
        compiler_params=_cparams(("parallel",)),
    )(my_chip, w, own, got, got, got, m, v)


def _my_place():
    return lax.axis_index("x"), lax.axis_index("y"), lax.axis_index("c")


def _peer(k):
    x, y, c = _my_place()
    return (x ^ ((k >> 2) & 1), y ^ ((k >> 1) & 1), c ^ (k & 1))


def _linear(place):
    return 4 * place[0] + 2 * place[1] + place[2]


def _ada_fwd(c_row, wconv_row, w_ada, b_row):
    D, CW = w_ada.shape
    WC = wconv_row.shape[-1]

    def body(c_ref, wc_ref, w_ref, b_ref, mod_ref, cact_ref, wcall_ref, send_buf, sems):
        me = _linear(_my_place())
        c = c_ref[0]
        cact_ref[me] = c * _sigmoid(c)
        wcall_ref[me] = wc_ref[0]

        def gather_copy(buf, k, grp):
            return pltpu.make_async_remote_copy(
                src_ref=buf.at[me], dst_ref=buf.at[me], send_sem=sems.at[0, grp, k], recv_sem=sems.at[1, grp, k],
                device_id=_peer(k), device_id_type=MESH_ID)

        def gather_recv(buf, k, grp):
            src = _linear(_peer(k))
            return pltpu.make_async_remote_copy(
                src_ref=buf.at[src], dst_ref=buf.at[src], send_sem=sems.at[0, grp, k], recv_sem=sems.at[1, grp, k],
                device_id=_peer(k), device_id_type=MESH_ID)

        for k in range(1, N_DEV):
            gather_copy(cact_ref, k, 0).start()
            gather_copy(wcall_ref, k, 1).start()
        for k in range(1, N_DEV):
            gather_recv(cact_ref, k, 0).wait_recv()
            gather_recv(wcall_ref, k, 1).wait_recv()
        for k in range(1, N_DEV):
            gather_copy(cact_ref, k, 0).wait_send()
            gather_copy(wcall_ref, k, 1).wait_send()

        cact = jnp.concatenate([cact_ref[b] for b in range(N_DEV)], axis=0)
        mod_all = jnp.dot(cact.astype(BF16), w_ref[...].astype(BF16), preferred_element_type=F32) + b_ref[0]
        for b in range(N_DEV):
            send_buf[b] = mod_all[b:b + 1, :]
        mod_ref[me] = send_buf[me]

        def scatter_copy(k):
            dst = _linear(_peer(k))
            return pltpu.make_async_remote_copy(
                src_ref=send_buf.at[dst], dst_ref=mod_ref.at[me], send_sem=sems.at[0, 2, k], recv_sem=sems.at[1, 2, k],
                device_id=_peer(k), device_id_type=MESH_ID)

        def scatter_recv(k):
            src = _linear(_peer(k))
            return pltpu.make_async_remote_copy(
                src_ref=send_buf.at[src], dst_ref=mod_ref.at[src], send_sem=sems.at[0, 2, k], recv_sem=sems.at[1, 2, k],
                device_id=_peer(k), device_id_type=MESH_ID)

        for k in range(1, N_DEV):
            scatter_copy(k).start()
        for k in range(1, N_DEV):
            scatter_recv(k).wait_recv()
        for k in range(1, N_DEV):
            scatter_copy(k).wait_send()

    vmem = pl.BlockSpec(memory_space=pltpu.VMEM)
    return pl.pallas_call(
        body, name="ada_fwd",
        in_specs=[vmem] * 4, out_specs=[vmem] * 3,
        out_shape=[jax.ShapeDtypeStruct((N_DEV, 1, CW), F32), jax.ShapeDtypeStruct((N_DEV, 1, D), F32),
                   jax.ShapeDtypeStruct((N_DEV, 1, WC), F32)],
        scratch_shapes=[pltpu.VMEM((N_DEV, 1, CW), F32), pltpu.SemaphoreType.DMA((2, 3, N_DEV))],
        compiler_params=pltpu.CompilerParams(vmem_limit_bytes=VMEM_LIMIT),
    )(c_row, wconv_row, w_ada, b_row)


def _ada_bwd(payload, cact_t, n_mod):
    NCH, _, CW = payload.shape
    D = cact_t.shape[0]

    def body(p_ref, ct_ref, sum_ref, gw_ref, all_ref, sems):
        me = _linear(_my_place())
        all_ref[me] = p_ref[...]

        def copy(k, slot):
            return pltpu.make_async_remote_copy(
                src_ref=all_ref.at[slot], dst_ref=all_ref.at[slot], send_sem=sems.at[0, k], recv_sem=sems.at[1, k],
                device_id=_peer(k), device_id_type=MESH_ID)

        for k in range(1, N_DEV):
            copy(k, me).start()
        for k in range(1, N_DEV):
            copy(k, _linear(_peer(k))).wait_recv()
        for k in range(1, N_DEV):
            copy(k, me).wait_send()

        total = all_ref[0]
        for b in range(1, N_DEV):
            total = total + all_ref[b]
        sum_ref[...] = total

        ct = ct_ref[...].astype(BF16).astype(F32)
        gw = jnp.zeros((D, CW), F32)
        for b in range(N_DEV):
            dm = all_ref[b, me].astype(BF16).astype(F32)
            gw = gw + ct[:, b:b + 1] * dm
        gw_ref[...] = gw

    vmem = pl.BlockSpec(memory_space=pltpu.VMEM)
    return pl.pallas_call(
        body, name="ada_bwd",
        in_specs=[vmem, vmem], out_specs=[vmem, vmem],
        out_shape=[jax.ShapeDtypeStruct((NCH, 1, CW), F32), jax.ShapeDtypeStruct((D, CW), F32)],
        scratch_shapes=[pltpu.VMEM((N_DEV, NCH, 1, CW), F32), pltpu.SemaphoreType.DMA((2, N_DEV))],
        compiler_params=pltpu.CompilerParams(vmem_limit_bytes=VMEM_LIMIT),
    )(payload, cact_t)


def _exchange_in_chip(parts):
    W = len(parts)

    def body(*refs):
        p_refs, got_refs, (send_sems, recv_sems) = refs[:W], refs[W:2 * W], refs[2 * W:]
        x, y, c = _my_place()
        sibling = (x, y, 1 - c)
        copies = []
        for w in range(W):
            for q in range(4):
                copies.append(pltpu.make_async_remote_copy(
                    src_ref=p_refs[w].at[2 * q + (1 - c)], dst_ref=got_refs[w].at[q],
                    send_sem=send_sems.at[4 * w + q], recv_sem=recv_sems.at[4 * w + q],
                    device_id=sibling, device_id_type=MESH_ID))
        for cp in copies:
            cp.start()
        for cp in copies:
            cp.wait_recv()
        for cp in copies:
            cp.wait_send()

    return pl.pallas_call(
        body, name="grad_exchange_in_chip",
        in_specs=[HBM_SPEC] * W, out_specs=[HBM_SPEC] * W,
        out_shape=[jax.ShapeDtypeStruct((4,) + p.shape[1:], p.dtype) for p in parts],
        scratch_shapes=[pltpu.SemaphoreType.DMA((4 * W,)), pltpu.SemaphoreType.DMA((4 * W,))],
    )(*parts)


def _pair_sum(parts, got, core):
    _, R, C = parts.shape
    tr = _tile(R, max(PACK_ROW_ALIGN, PAIR_SUM_BLOCK // C), PACK_ROW_ALIGN)

    def body(c_ref, p_ref, g_ref, o_ref):
        o_ref[...] = (p_ref[...].astype(F32) + g_ref[...].astype(F32)).astype(o_ref.dtype)

    return pl.pallas_call(
        body, name="grad_pair_sum",
        grid_spec=pltpu.PrefetchScalarGridSpec(
            num_scalar_prefetch=1, grid=(4, R // tr),
            in_specs=[pl.BlockSpec((1, tr, C), lambda q, i, c_ref: (2 * q + c_ref[0], i, 0)),
                      pl.BlockSpec((1, tr, C), lambda q, i, c_ref: (q, i, 0))],
            out_specs=pl.BlockSpec((1, tr, C), lambda q, i, c_ref: (q, i, 0))),
        out_shape=jax.ShapeDtypeStruct((4, R, C), parts.dtype),
        compiler_params=_cparams(("parallel", "parallel")),
    )(core, parts, got)


HBM_SPEC = pl.BlockSpec(memory_space=pltpu.HBM)
SEM_SPEC = pl.BlockSpec(memory_space=pltpu.SEMAPHORE)
ANY_SPEC = pl.BlockSpec(memory_space=pl.ANY)
SPLIT_EFFECT = pltpu.SideEffectType.DATAFLOW_SIDE_EFFECTING


def _landing_zone(shape, dtype):
    return pltpu.with_memory_space_constraint(lax.empty(shape, dtype), pltpu.HBM)


def _split_start(name, arrays, lands, after, copies_of, per_array):
    W = len(arrays)

    def body(*refs):
        x_refs, land_refs = refs[:W], refs[W:2 * W]
        send_sems, recv_sems = refs[2 * W + 1], refs[2 * W + 2]
        token = refs[-1]
        k = 0
        for w in range(W):
            for src, dst, dev in copies_of(w, x_refs[w], land_refs[w]):
                pltpu.make_async_remote_copy(src_ref=src, dst_ref=dst, send_sem=send_sems.at[k], recv_sem=recv_sems.at[k],
                                             device_id=dev, device_id_type=MESH_ID).start()
                k += 1
        token[...] = jnp.zeros_like(token)

    n_copies = per_array * W
    hbm_of = lambda xs: tuple(pltpu.HBM(a.shape, a.dtype) for a in xs)
    out = pl.pallas_call(
        body, name=name,
        out_shape=(pltpu.SemaphoreType.DMA((n_copies,)), pltpu.SemaphoreType.DMA((n_copies,)))
        + hbm_of(arrays) + hbm_of(lands) + (jax.ShapeDtypeStruct((8, LANE), F32),),
        in_specs=(HBM_SPEC,) * (2 * W) + (ANY_SPEC,),
        out_specs=(SEM_SPEC, SEM_SPEC) + (HBM_SPEC,) * (2 * W) + (pl.BlockSpec(memory_space=pltpu.VMEM),),
        input_output_aliases={i: 2 + i for i in range(2 * W)},
        compiler_params=pltpu.CompilerParams(has_side_effects=SPLIT_EFFECT),
    )(*[pltpu.with_memory_space_constraint(a, pltpu.HBM) for a in arrays], *lands, after)
    return out[0], out[1], list(out[2:2 + W]), list(out[2 + W:2 + 2 * W]), out[-1]


def _split_wait(name, state, after, copies_of):
    send_sems, recv_sems, arrays, lands, _ = state
    W = len(arrays)
    after = tuple(after) if isinstance(after, (tuple, list)) else (after,)

    def body(*refs):
        x_refs, land_refs = refs[:W], refs[W:2 * W]
        send_sems, recv_sems = refs[2 * W], refs[2 * W + 1]
        k = 0
        for w in range(W):
            for src, dst, dev in copies_of(w, x_refs[w], land_refs[w]):
                cp = pltpu.make_async_remote_copy(src_ref=src, dst_ref=dst, send_sem=send_sems.at[k],
                                                  recv_sem=recv_sems.at[k], device_id=dev, device_id_type=MESH_ID)
                cp.wait_send()
                cp.wait_recv()
                k += 1

    out = pl.pallas_call(
        body, name=name,
        out_shape=tuple(pltpu.HBM(a.shape, a.dtype) for a in arrays + lands),
        in_specs=(HBM_SPEC,) * (2 * W) + (SEM_SPEC, SEM_SPEC) + (ANY_SPEC,) * len(after),
        out_specs=(HBM_SPEC,) * (2 * W),
        input_output_aliases={i: i for i in range(2 * W)},
        compiler_params=pltpu.CompilerParams(has_side_effects=SPLIT_EFFECT),
    )(*arrays, *lands, send_sems, recv_sems, *after)
    return list(out[:W]), list(out[W:])


def _scatter_copies(w, p_ref, land_ref):
    x, y, c = _my_place()
    my_chip = 2 * x + y
    return [(p_ref.at[2 * (x ^ (k >> 1)) + (y ^ (k & 1))], land_ref.at[my_chip], (x ^ (k >> 1), y ^ (k & 1), c))
            for k in range(1, 4)]


def _gather_copies(w, x_ref, land_ref):
    x, y, c = _my_place()
    me = _linear((x, y, c))
    devs = [(x, y, 1 - c)] + [(x ^ (k >> 1), y ^ (k & 1), c) for k in range(1, 4)]
    return [(x_ref, land_ref.at[me], d) for d in devs]


def _gather_forward(lands, name):
    W = len(lands)

    def body(*refs):
        land_refs, out_refs, (send_sems, recv_sems) = refs[:W], refs[W:2 * W], refs[2 * W:]
        x, y, c = _my_place()
        sibling = (x, y, 1 - c)
        sends, arrivals = [], []
        for w in range(W):
            for k in range(1, 4):
                px, py = x ^ (k >> 1), y ^ (k & 1)
                landed, theirs = _linear((px, py, c)), out_refs[w].at[_linear((px, py, 1 - c))]
                sem = 3 * w + k - 1
                sends.append(pltpu.make_async_remote_copy(
                    src_ref=land_refs[w].at[landed], dst_ref=out_refs[w].at[landed],
                    send_sem=send_sems.at[sem], recv_sem=recv_sems.at[sem], device_id=sibling, device_id_type=MESH_ID))
                arrivals.append(pltpu.make_async_remote_copy(
                    src_ref=theirs, dst_ref=theirs, send_sem=send_sems.at[sem], recv_sem=recv_sems.at[sem],
                    device_id=sibling, device_id_type=MESH_ID))
        for cp in sends:
            cp.start()
        for cp in arrivals:
            cp.wait_recv()
        for cp in sends:
            cp.wait_send()

    return pl.pallas_call(
        body, name=name,
        in_specs=[HBM_SPEC] * W, out_specs=[HBM_SPEC] * W,
        out_shape=[jax.ShapeDtypeStruct(l.shape, l.dtype) for l in lands],
        input_output_aliases={i: i for i in range(W)},
        scratch_shapes=[pltpu.SemaphoreType.DMA((3 * W,)), pltpu.SemaphoreType.DMA((3 * W,))],
    )(*lands)


def _with_own_slot(gathered, shard):
    return lax.dynamic_update_index_in_dim(gathered, shard[None], _linear(_my_place()), axis=0)


def _reduce_scatter_begin(parts, tag):
    got = _exchange_in_chip(parts)
    core = lax.axis_index("c").astype(jnp.int32).reshape(1)
    chip_parts = [_pair_sum(p, g, core) for p, g in zip(parts, got)]
    lands = [_landing_zone(p.shape, p.dtype) for p in chip_parts]
    return _split_start("grad_scatter_start_" + tag, chip_parts, lands, got[0], _scatter_copies, 3)


def _reduce_scatter_end(state, after, tag):
    return _split_wait("grad_scatter_wait_" + tag, state, after, _scatter_copies)


def kernel(x, c, positions, w_ada, b_ada, w_in, g_q_a, w_q_b, g_kv_a, w_kv_b, w_o_a, w_conv, w_o_b, w_o, ln1_g, ln1_b, w_ffn_in, w_ffn_out, ln2_g, ln2_b, loss_target, m_w_ada, m_b_ada, m_w_in, m_g_q_a, m_w_q_b, m_g_kv_a, m_w_kv_b, m_w_o_a, m_w_conv, m_w_o_b, m_w_o, m_ln1_g, m_ln1_b, m_w_ffn_in, m_w_ffn_out, m_ln2_g, m_ln2_b, v_w_ada, v_b_ada, v_w_in, v_g_q_a, v_w_q_b, v_g_kv_a, v_w_kv_b, v_w_o_a, v_w_conv, v_w_o_b, v_w_o, v_ln1_g, v_ln1_b, v_w_ffn_in, v_w_ffn_out, v_ln2_g, v_ln2_b):
    x2, tgt = x[0], loss_target[0]
    S, D = x2.shape
    Lq, Lkv = g_q_a.shape[1], g_kv_a.shape[1]
    H = w_q_b.shape[2] * N_DEV // QK_CAT
    F = w_ffn_out.shape[1] * N_DEV
    assert Lq == Lkv and (Lq + Lkv) % COL_BLOCK == 0 and D % COL_BLOCK == 0
    front = Lq + Lkv + QK_ROPE
    front_pad = _round_up(front, COL_BLOCK)
    kr_blk = (Lq + Lkv) // COL_BLOCK
    blk_b = front_pad // COL_BLOCK
    nblk = D // COL_BLOCK
    blk_c, blk_x, blk_ga, blk_gb = blk_b + nblk, blk_b + 2 * nblk, blk_b + 3 * nblk, blk_b + 4 * nblk
    ts = _tile(S, 256, 8)
    T = _tile(S, min(512, S // 2), CHUNK)
    tb = _tile(F, 512)
    me = _linear(_my_place())

    landing = lambda shards: [_landing_zone((N_DEV,) + s.shape, BF16) for s in shards]
    first = [w[0].astype(BF16) for w in (w_in, w_q_b, w_kv_b)]
    first_state = _split_start("first_gather_start", first, landing(first), c, _gather_copies, 4)
    later = [w[0].astype(BF16) for w in (w_o_a, w_o_b, w_o, w_ffn_in, w_ffn_out)]

    cw = w_ada.shape[2]
    b_mine = lax.dynamic_slice(b_ada, (0, me * cw), (1, cw)).reshape(1, 1, cw)
    c_row = c.reshape(1, 1, D) + first_state[4][0, 0]
    mod_blocks, cact_all, wconv_all = _ada_fwd(c_row, w_conv[0].reshape(1, 1, -1), w_ada[0], b_mine)
    mod = mod_blocks.reshape(6, D)
    cact_all = cact_all.reshape(N_DEV, D)
    w_conv_full = wconv_all.reshape(N_DEV, CONV_K, -1).transpose(1, 0, 2).reshape(CONV_K, D)
    u = _modulate_in(x2, mod, ts)

    first_shards, first_lands = _split_wait("first_gather_wait", first_state, (u, *later), _gather_copies)
    g_in, wq_s, wkv_s = [_with_own_slot(g, s) for g, s in
                         zip(_gather_forward(first_lands, "first_gather_forward"), first_shards)]
    later_state = _split_start("weight_gather_start", later, landing(later), g_in, _gather_copies, 4)
    later_token = later_state[4]
    w_in_p = _assemble_w_in(g_in, front, front_pad)

    inv_freq = 1.0 / (ROPE_THETA ** (jnp.arange(0, QK_ROPE, 2, dtype=F32) / QK_ROPE))
    ang = positions[0].astype(F32)[:, None] * inv_freq
    cos2 = jnp.concatenate([jnp.cos(ang), jnp.cos(ang)], axis=-1)
    sin2 = jnp.concatenate([jnp.sin(ang), jnp.sin(ang)], axis=-1)
    one, zero = jnp.ones((S, QK_NOPE), F32), jnp.zeros((S, QK_NOPE), F32)
    cos_q, sin_q = jnp.concatenate([one, cos2, one, cos2], axis=-1), jnp.concatenate([zero, sin2, zero, sin2], axis=-1)
    cos_k, sin_k = jnp.tile(cos2, (1, COL_BLOCK // QK_ROPE)), jnp.tile(sin2, (1, COL_BLOCK // QK_ROPE))

    proj = _matmul(u, w_in_p, "nn", F32, "proj", deps=(later_token,))
    qn = _rms_fwd(proj, g_q_a, 0, Lq, ts, "rms_q")
    kvn = _rms_fwd(proj, g_kv_a, 1, Lkv, ts, "rms_kv")
    q = _matmul(qn, wq_s, "nn", F32, "q_up")
    kv = _matmul(kvn, wkv_s, "nn", F32, "kv_up")
    qc, kc, vh = _qk_prep(q, kv, proj, kr_blk, cos_q, sin_q, cos_k, sin_k, H, ts)
    attn, lse = _attn_fwd(qc, kc, vh, T)
    later_shards, later_lands = _split_wait("weight_gather_wait", later_state, lse, _gather_copies)
    later_all = _gather_forward(later_lands, "weight_gather_forward")
    g_oa, g_ob, g_o, w_fi_s, g_fo = [_with_own_slot(g, s) for g, s in zip(later_all, later_shards)]
    w_oa_f, w_ob_f, w_o_f = g_oa.reshape(-1, D), g_ob.reshape(-1, D), g_o.reshape(-1, D)
    w_fo_f = g_fo.reshape(F, D)
    ya = _matmul(attn, w_oa_f, "nn", F32, "attn_out")
    cbc = _conv_fwd(proj, w_conv_full, blk_b, blk_c, blk_x)
    yb = _matmul(cbc, w_ob_f, "nn", F32, "conv_out")
    merged = _merge_fwd(proj, ya, yb, blk_ga, blk_gb, ts)
    mix = _matmul(merged, w_o_f, "nn", F32, "mix_out")
    xhat1, rstd1, u2 = _ln1_fwd(x2, mix, mod, ln1_g, ln1_b, ts)
    hh = _matmul(u2, w_fi_s, "nn", F32, "ffn_in")
    act = _swiglu_fwd(hh, ts, tb)
    ffn = _matmul(act, w_fo_f, "nn", F32, "ffn_out")
    loss_part, dffn, dx1a, vec2 = _ln2_loss(xhat1, ffn, tgt, mod, ln1_g, ln1_b, ln2_g, ln2_b, ts)
    loss = lax.psum(loss_part[0, 0], AXES)

    gw_fo = _matmul(act, dffn, "tn", BF16, "grad_w_ffn_out")
    da = _matmul(dffn, w_fo_f, "nt", F32, "d_act")
    dh = _swiglu_bwd(da, hh, ts, tb)
    gw_fi = _matmul(u2, dh, "tn", BF16, "grad_w_ffn_in", out_shards=True)
    ffn_state = _reduce_scatter_begin([gw_fi, gw_fo.reshape(N_DEV, -1, D)], "ffn")
    du2 = _matmul(dh, w_fi_s, "nt", F32, "d_u2", deps=(ffn_state[4],))
    dxa, dmix, vec1 = _ln1_bwd(du2, dx1a, xhat1, rstd1, mix, mod, ln1_g, ln1_b, ts)
    gw_o = _matmul(merged, dmix, "tn", BF16, "grad_w_o")
    dmerged = _matmul(dmix, w_o_f, "nt", F32, "d_merged")
    dya, dyb, dga, dgb = _merge_bwd(dmerged, proj, ya, yb, blk_ga, blk_gb, ts)
    gw_ob = _matmul(cbc, dyb, "tn", BF16, "grad_w_o_b")
    dcbc = _matmul(dyb, w_ob_f, "nt", F32, "d_conv")
    dcb, dcc, dcx, dwconv = _conv_bwd(dcbc, proj, w_conv_full, blk_b, blk_c, blk_x)
    gw_oa = _matmul(attn, dya, "tn", BF16, "grad_w_o_a")
    mix_state = _reduce_scatter_begin([g.reshape(N_DEV, -1, D) for g in (gw_oa, gw_ob, gw_o)], "mix")
    dattn = _matmul(dya, w_oa_f, "nt", F32, "d_attn", deps=(mix_state[4],))
    dqc, dkc, dvh = _attn_bwd(qc, kc, vh, dattn, attn, lse, T)
    ffn_own, ffn_got = _reduce_scatter_end(ffn_state, dqc, "ffn")
    mix_own, mix_got = _reduce_scatter_end(mix_state, dqc, "mix")
    dq, dkv, dkr = _qk_bwd(dqc, dkc, dvh, cos_q, sin_q, cos_k, sin_k, ts)
    gw_qb = _matmul(qn, dq, "tn", BF16, "grad_w_q_b", out_shards=True)
    dqn = _matmul(dq, wq_s, "nt", F32, "d_qn")
    gw_kvb = _matmul(kvn, dkv, "tn", BF16, "grad_w_kv_b", out_shards=True)
    dkvn = _matmul(dkv, wkv_s, "nt", F32, "d_kvn")
    dqa, dgq = _rms_bwd(dqn, proj, g_q_a, 0, Lq, ts, "rms_q_bwd")
    dkva, dgkv = _rms_bwd(dkvn, proj, g_kv_a, 1, Lkv, ts, "rms_kv_bwd")
    dproj = jnp.concatenate([dqa, dkva, dkr, dcb, dcc, dcx, dga, dgb], axis=1)
    gw_in_p = _matmul(u, dproj, "tn", BF16, "grad_w_in")
    in_state = _reduce_scatter_begin([_split_w_in(gw_in_p, front, front_pad), gw_qb, gw_kvb], "in")
    du = _matmul(dproj, w_in_p, "nt", F32, "d_u", deps=(in_state[4],))
    grad_x, vec0 = _grad_x(du, dxa, x2, mod, ts)

    n_mod = 6 * D // cw
    dmod = jnp.concatenate([vec0[0], vec0[1], vec1[4], vec1[0], vec1[1], vec2[2]])
    small = jnp.concatenate([dmod, dgq[0], dgkv[0], vec1[2], vec1[3], vec2[0], vec2[1], dwconv[:CONV_K].reshape(-1)])
    n_small = small.shape[0]
    nch = _round_up(n_small, cw) // cw
    payload = jnp.pad(small, (0, nch * cw - n_small)).reshape(nch, 1, cw)
    summed, g_w_ada = _ada_bwd(payload, cact_all.T, n_mod)
    summed = summed.reshape(-1)
    offs = [0, 6 * D, 6 * D + Lq, 6 * D + Lq + Lkv]
    offs += [offs[-1] + D * k for k in range(1, 5)]
    g_b_ada = summed[offs[0]:offs[1]].reshape(1, -1)
    g_gq = summed[offs[1]:offs[2]].reshape(1, -1)
    g_gkv = summed[offs[2]:offs[3]].reshape(1, -1)
    g_ln1g, g_ln1b, g_ln2g, g_ln2b = [summed[offs[3 + k]:offs[4 + k]].reshape(1, -1) for k in range(4)]
    wc = w_conv.shape[2]
    g_wconv = lax.dynamic_slice(summed[offs[7]:offs[7] + CONV_K * D].reshape(CONV_K, D), (0, me * wc), (CONV_K, wc))

    names = ["w_ada", "b_ada", "w_in", "g_q_a", "w_q_b", "g_kv_a", "w_kv_b", "w_o_a", "w_conv", "w_o_b", "w_o",
             "ln1_g", "ln1_b", "w_ffn_in", "w_ffn_out", "ln2_g", "ln2_b"]
    weights = [w_ada, b_ada, w_in, g_q_a, w_q_b, g_kv_a, w_kv_b, w_o_a, w_conv, w_o_b, w_o, ln1_g, ln1_b,
               w_ffn_in, w_ffn_out, ln2_g, ln2_b]
    moms = [m_w_ada, m_b_ada, m_w_in, m_g_q_a, m_w_q_b, m_g_kv_a, m_w_kv_b, m_w_o_a, m_w_conv, m_w_o_b, m_w_o,
            m_ln1_g, m_ln1_b, m_w_ffn_in, m_w_ffn_out, m_ln2_g, m_ln2_b]
    vels = [v_w_ada, v_b_ada, v_w_in, v_g_q_a, v_w_q_b, v_g_kv_a, v_w_kv_b, v_w_o_a, v_w_conv, v_w_o_b, v_w_o,
            v_ln1_g, v_ln1_b, v_w_ffn_in, v_w_ffn_out, v_ln2_g, v_ln2_b]
    grad_of = {"w_ada": g_w_ada, "b_ada": g_b_ada, "g_q_a": g_gq, "g_kv_a": g_gkv, "w_conv": g_wconv,
               "ln1_g": g_ln1g, "ln1_b": g_ln1b, "ln2_g": g_ln2g, "ln2_b": g_ln2b}
    state_of = dict(zip(names, zip(weights, moms, vels)))
    results = {}
    my_chip = (2 * lax.axis_index("x") + lax.axis_index("y")).astype(jnp.int32).reshape(1)

    def update(nm, reduced=None):
        w, m, v = state_of[nm]
        shp = w.shape
        w2 = w.reshape(shp[-2], shp[-1]) if w.ndim == 3 else w
        m2, v2 = m.reshape(w2.shape), v.reshape(w2.shape)
        if reduced is None:
            g2 = grad_of[nm].reshape(w2.shape)
            res = (g2,) + tuple(_adamw(w2, g2, m2, v2, "adamw_" + nm))
        else:
            res = _adamw_reduced(w2, reduced[0], reduced[1], m2, v2, my_chip, "adamw_" + nm)
        results[nm] = [a.reshape(shp) for a in res]

    for nm in grad_of:
        update(nm)
    for nm, own, got in zip(("w_ffn_in", "w_ffn_out"), ffn_own, ffn_got):
        update(nm, (own, got))
    for nm, own, got in zip(("w_o_a", "w_o_b", "w_o"), mix_own, mix_got):
        update(nm, (own, got))
    in_own, in_got = _reduce_scatter_end(in_state, [res[1] for res in results.values()], "in")
    for nm, own, got in zip(("w_in", "w_q_b", "w_kv_b"), in_own, in_got):
        update(nm, (own, got))
    outs = [[results[nm][k] for nm in names] for k in range(4)]
    return (loss, grad_x.reshape(x.shape), *outs[0], *outs[1], *outs[2], *outs[3])
```

```python
import functools

import jax
import jax.numpy as jnp
from jax import lax
from jax.experimental import pallas as pl
from jax.experimental.pallas import tpu as pltpu

F32 = jnp.float32
BF16 = jnp.bfloat16
MESH_ID = pl.DeviceIdType.MESH
AXES = ("x", "y", "c")
N_DEV = 8

CHUNK = 64
QK_NOPE = 128
QK_ROPE = 64
V_HEAD = 128
QK_CAT = QK_NOPE + QK_ROPE
ROPE_THETA = 10000.0
ATTN_SCALE = (QK_NOPE + QK_ROPE) ** -0.5
CONV_K = 3
DEEPNORM_ALPHA = 2.0 ** 0.25
LN_EPS = 1e-5
RMS_EPS = 1e-6
NEG_INF = -1e30

ADAM_LR = 0.001
ADAM_B1 = 0.9
ADAM_B2 = 0.999
ADAM_EPS = 1e-08
ADAM_WD = 0.01
ADAM_STEP = 10

LANE = 128
COL_BLOCK = 256
PACK_ROW_ALIGN = 16
PAIR_SUM_BLOCK = 1 << 20
VMEM_LIMIT = 48 * 1024 * 1024


def _round_up(n, m):
    return (n + m - 1) // m * m


def _tile(n, pref, align=LANE):
    best = None
    t = align
    while t <= min(n, pref):
        if n % t == 0:
            best = t
        t += align
    return best if best is not None else n


def _cparams(sem=None):
    return pltpu.CompilerParams(dimension_semantics=sem, vmem_limit_bytes=VMEM_LIMIT)


def _sigmoid(x):
    return 0.5 * jnp.tanh(0.5 * x) + 0.5


def _matmul(a, b, mode, out_dtype, name, tm=1024, tn=1024, tk=2048, deps=(), out_shards=False):
    b_shards = b.ndim == 3
    n = b.shape[2] if b_shards else (b.shape[1] // N_DEV if out_shards else None)
    if mode == "nn":
        (M, K), (K2, N) = a.shape, (b.shape[1], N_DEV * n) if b_shards else b.shape
    elif mode == "nt":
        (M, K), (N, K2) = a.shape, (b.shape[1], N_DEV * n) if b_shards else b.shape
    else:
        (K, M), (K2, N) = a.shape, b.shape
    assert K == K2, (a.shape, b.shape, mode)
    tm = _tile(M, tm)
    tn = n if (mode != "nt" and n is not None) else _tile(N, tn)
    tk = n if (mode == "nt" and b_shards) else _tile(K, tk)
    nk = K // tk
    if mode == "nn":
        a_spec = pl.BlockSpec((tm, tk), lambda i, j, k: (i, k))
        b_spec = (pl.BlockSpec((1, tk, n), lambda i, j, k: (j, k, 0)) if b_shards
                  else pl.BlockSpec((tk, tn), lambda i, j, k: (k, j)))
        dims = (((1,), (0,)), ((), ()))
    elif mode == "nt":
        a_spec = pl.BlockSpec((tm, tk), lambda i, j, k: (i, k))
        b_spec = (pl.BlockSpec((1, tn, n), lambda i, j, k: (k, j, 0)) if b_shards
                  else pl.BlockSpec((tn, tk), lambda i, j, k: (j, k)))
        dims = (((1,), (1,)), ((), ()))
    else:
        a_spec = pl.BlockSpec((tk, tm), lambda i, j, k: (k, i))
        b_spec = pl.BlockSpec((tk, tn), lambda i, j, k: (k, j))
        dims = (((0,), (0,)), ((), ()))
    if out_shards:
        out_spec = pl.BlockSpec((1, tm, n), lambda i, j, k: (j, i, 0))
        out_shape = jax.ShapeDtypeStruct((N_DEV, M, n), out_dtype)
    else:
        out_spec = pl.BlockSpec((tm, tn), lambda i, j, k: (i, j))
        out_shape = jax.ShapeDtypeStruct((M, N), out_dtype)

    def product(a_ref, b_ref):
        b_blk = b_ref[0] if b_shards else b_ref[...]
        return lax.dot_general(a_ref[...].astype(BF16), b_blk.astype(BF16), dims, preferred_element_type=F32)

    def write(o_ref, value):
        if out_shards:
            o_ref[0] = value.astype(o_ref.dtype)
        else:
            o_ref[...] = value.astype(o_ref.dtype)

    def body_whole_k(a_ref, b_ref, *rest):
        write(rest[-1], product(a_ref, b_ref))

    def body_split_k(a_ref, b_ref, *rest):
        o_ref, acc_ref = rest[-2:]
        k = pl.program_id(2)

        @pl.when(k == 0)
        def _():
            acc_ref[...] = jnp.zeros_like(acc_ref)

        acc_ref[...] += product(a_ref, b_ref)

        @pl.when(k == nk - 1)
        def _():
            write(o_ref, acc_ref[...])

    return pl.pallas_call(
        body_whole_k if nk == 1 else body_split_k, name=name, grid=(M // tm, N // tn, nk),
        in_specs=[a_spec, b_spec] + [ANY_SPEC] * len(deps),
        out_specs=out_spec, out_shape=out_shape,
        scratch_shapes=[] if nk == 1 else [pltpu.VMEM((tm, tn), F32)],
        compiler_params=_cparams(("parallel", "parallel", "arbitrary")),
    )(a, b, *deps)


def _assemble_w_in(shards, front, front_pad):
    _, K, n = shards.shape
    gap = front_pad - front
    tk = _tile(K, 256, PACK_ROW_ALIGN)

    def body(g_ref, o_ref):
        if gap:
            o_ref[:, front:front_pad] = jnp.zeros((tk, gap), o_ref.dtype)
        for j in range(N_DEV):
            lo, hi = j * n, (j + 1) * n
            if lo < front < hi:
                o_ref[:, lo:front] = g_ref[j, :, 0:front - lo]
                o_ref[:, front_pad:hi + gap] = g_ref[j, :, front - lo:n]
            else:
                off = 0 if hi <= front else gap
                o_ref[:, lo + off:hi + off] = g_ref[j]

    return pl.pallas_call(
        body, name="assemble_w_in", grid=(K // tk,),
        in_specs=[pl.BlockSpec((N_DEV, tk, n), lambda i: (0, i, 0))],
        out_specs=pl.BlockSpec((tk, N_DEV * n + gap), lambda i: (i, 0)),
        out_shape=jax.ShapeDtypeStruct((K, N_DEV * n + gap), shards.dtype),
        compiler_params=_cparams(("parallel",)),
    )(shards)


def _split_w_in(w, front, front_pad):
    K, NP = w.shape
    gap = front_pad - front
    n = (NP - gap) // N_DEV
    tk = _tile(K, 256, PACK_ROW_ALIGN)

    def body(w_ref, o_ref):
        for j in range(N_DEV):
            lo, hi = j * n, (j + 1) * n
            if lo < front < hi:
                o_ref[j, :, 0:front - lo] = w_ref[:, lo:front]
                o_ref[j, :, front - lo:n] = w_ref[:, front_pad:hi + gap]
            else:
                off = 0 if hi <= front else gap
                o_ref[j] = w_ref[:, lo + off:hi + off]

    return pl.pallas_call(
        body, name="split_grad_w_in", grid=(K // tk,),
        in_specs=[pl.BlockSpec((tk, NP), lambda i: (i, 0))],
        out_specs=pl.BlockSpec((N_DEV, tk, n), lambda i: (0, i, 0)),
        out_shape=jax.ShapeDtypeStruct((N_DEV, K, n), w.dtype),
        compiler_params=_cparams(("parallel",)),
    )(w)


def _modulate_in(x, mod, ts):
    S, D = x.shape

    def body(x_ref, mod_ref, u_ref):
        u_ref[...] = (x_ref[...] * (1.0 + mod_ref[1:2, :]) + mod_ref[0:1, :]).astype(BF16)

    return pl.pallas_call(
        body, name="modulate_in", grid=(S // ts,),
        in_specs=[pl.BlockSpec((ts, D), lambda i: (i, 0)), pl.BlockSpec((6, D), lambda i: (0, 0))],
        out_specs=pl.BlockSpec((ts, D), lambda i: (i, 0)),
        out_shape=jax.ShapeDtypeStruct((S, D), BF16),
        compiler_params=_cparams(("parallel",)),
    )(x, mod)


def _rms_fwd(proj, g, blk, L, ts, name):
    S = proj.shape[0]

    def body(a_ref, g_ref, y_ref):
        a = a_ref[...]
        r = lax.rsqrt(jnp.mean(a * a, axis=-1, keepdims=True) + RMS_EPS)
        y_ref[...] = (a * r * g_ref[...]).astype(BF16)

    return pl.pallas_call(
        body, name=name, grid=(S // ts,),
        in_specs=[pl.BlockSpec((ts, L), lambda i: (i, blk)), pl.BlockSpec((1, L), lambda i: (0, 0))],
        out_specs=pl.BlockSpec((ts, L), lambda i: (i, 0)),
        out_shape=jax.ShapeDtypeStruct((S, L), BF16),
        compiler_params=_cparams(("parallel",)),
    )(proj, g)


def _rope_partner(x, period, start):
    w = x.shape[-1]
    lane = lax.broadcasted_iota(jnp.int32, x.shape, x.ndim - 1) % period
    first = (lane >= start) & (lane < start + QK_ROPE // 2)
    from_right = pltpu.roll(x, w - QK_ROPE // 2, axis=x.ndim - 1)
    from_left = pltpu.roll(x, QK_ROPE // 2, axis=x.ndim - 1)
    return jnp.where(first, -from_right, from_left)


def _qk_prep(q, kv, proj, kr_blk, cos_q, sin_q, cos_k, sin_k, H, ts):
    S = q.shape[0]
    pair = 2 * QK_CAT
    kv_w = QK_NOPE + V_HEAD

    def body(q_ref, kv_ref, kr_ref, cq_ref, sq_ref, ck_ref, sk_ref, qc_ref, kc_ref, vh_ref):
        kr = kr_ref[...]
        kr = kr * ck_ref[...] + _rope_partner(kr, QK_ROPE, 0) * sk_ref[...]
        kr = kr[:, :QK_ROPE].astype(BF16)
        for p in range(H // 2):
            x = q_ref[:, p * pair:(p + 1) * pair]
            x = x * cq_ref[...] + _rope_partner(x, QK_CAT, QK_NOPE) * sq_ref[...]
            qc_ref[2 * p] = x[:, :QK_CAT].astype(BF16)
            qc_ref[2 * p + 1] = x[:, QK_CAT:].astype(BF16)
        for h in range(H):
            kc_ref[h, :, 0:QK_NOPE] = kv_ref[:, h * kv_w:h * kv_w + QK_NOPE].astype(BF16)
            kc_ref[h, :, QK_NOPE:QK_CAT] = kr
            vh_ref[h, :, :] = kv_ref[:, h * kv_w + QK_NOPE:(h + 1) * kv_w].astype(BF16)

    row = lambda w: pl.BlockSpec((ts, w), lambda i: (i, 0))
    return pl.pallas_call(
        body, name="qk_prep", grid=(S // ts,),
        in_specs=[row(H * QK_CAT), row(H * kv_w),
                  pl.BlockSpec((ts, COL_BLOCK), lambda i: (i, kr_blk)),
                  row(pair), row(pair), row(COL_BLOCK), row(COL_BLOCK)],
        out_specs=[pl.BlockSpec((H, ts, QK_CAT), lambda i: (0, i, 0)),
                   pl.BlockSpec((H, ts, QK_CAT), lambda i: (0, i, 0)),
                   pl.BlockSpec((H, ts, V_HEAD), lambda i: (0, i, 0))],
        out_shape=[jax.ShapeDtypeStruct((H, S, QK_CAT), BF16), jax.ShapeDtypeStruct((H, S, QK_CAT), BF16),
                   jax.ShapeDtypeStruct((H, S, V_HEAD), BF16)],
        compiler_params=_cparams(("parallel",)),
    )(q, kv, proj, cos_q, sin_q, cos_k, sin_k)


NT_DIMS = (((1,), (1,)), ((), ()))
TN_DIMS = (((0,), (0,)), ((), ()))


def _diag_mask(T):
    rows = lax.broadcasted_iota(jnp.int32, (T, T), 0) // CHUNK
    cols = lax.broadcasted_iota(jnp.int32, (T, T), 1) // CHUNK
    return cols <= rows


def _attn_fwd(qc, kc, vh, T):
    H, S, _ = qc.shape
    n = S // T

    def body(q_ref, k_ref, v_ref, o_ref, lse_ref, m_ref, l_ref, acc_ref):
        i = pl.program_id(1)
        q = q_ref[0]
        m_ref[...] = jnp.full_like(m_ref, NEG_INF)
        l_ref[...] = jnp.zeros_like(l_ref)
        acc_ref[...] = jnp.zeros_like(acc_ref)

        def step(j, masked):
            rows = pl.ds(pl.multiple_of(j * T, T), T)
            s = lax.dot_general(q, k_ref[0, rows, :], NT_DIMS, preferred_element_type=F32) * ATTN_SCALE
            if masked:
                s = jnp.where(_diag_mask(T), s, NEG_INF)
            m_old = m_ref[...]
            m_new = jnp.maximum(m_old, jnp.max(s, axis=-1, keepdims=True))
            alpha = jnp.exp(m_old - m_new)
            p = jnp.exp(s - m_new)
            l_ref[...] = alpha * l_ref[...] + jnp.sum(p, axis=-1, keepdims=True)
            acc_ref[...] = alpha * acc_ref[...] + jnp.dot(p.astype(BF16), v_ref[0, rows, :],
                                                          preferred_element_type=F32)
            m_ref[...] = m_new

        def below(j, carry):
            step(j, False)
            return carry

        lax.fori_loop(0, i, below, 0)
        step(i, True)
        o_ref[...] = acc_ref[...] / l_ref[...]
        lse_ref[0] = m_ref[...] + jnp.log(l_ref[...])

    return pl.pallas_call(
        body, name="attn_fwd", grid=(H, n),
        in_specs=[pl.BlockSpec((1, T, QK_CAT), lambda h, i: (h, i, 0)),
                  pl.BlockSpec((1, S, QK_CAT), lambda h, i: (h, 0, 0)),
                  pl.BlockSpec((1, S, V_HEAD), lambda h, i: (h, 0, 0))],
        out_specs=[pl.BlockSpec((T, V_HEAD), lambda h, i: (i, h)),
                   pl.BlockSpec((1, T, 1), lambda h, i: (h, i, 0))],
        out_shape=[jax.ShapeDtypeStruct((S, H * V_HEAD), F32), jax.ShapeDtypeStruct((H, S, 1), F32)],
        scratch_shapes=[pltpu.VMEM((T, 1), F32), pltpu.VMEM((T, 1), F32), pltpu.VMEM((T, V_HEAD), F32)],
        compiler_params=_cparams(("parallel", "arbitrary")),
    )(qc, kc, vh)


def _shift_rows(z, k):
    if k == 0:
        return z
    n = z.shape[0]
    row = lax.broadcasted_iota(jnp.int32, z.shape, 0)
    if k > 0:
        return jnp.where(row >= k, pltpu.roll(z, k, axis=0), 0.0)
    return jnp.where(row < n + k, pltpu.roll(z, n + k, axis=0), 0.0)


def _conv_fwd(proj, w_conv, blk_b, blk_c, blk_x):
    S = proj.shape[0]
    D = w_conv.shape[1]
    nb = D // COL_BLOCK

    def body(cb_ref, cc_ref, cx_ref, w_ref, o_ref):
        z = cc_ref[...] * cx_ref[...]
        conv = w_ref[2:3, :] * z + w_ref[1:2, :] * _shift_rows(z, 1) + w_ref[0:1, :] * _shift_rows(z, 2)
        o_ref[...] = (cb_ref[...] * conv).astype(BF16)

    col = lambda off: pl.BlockSpec((S, COL_BLOCK), lambda j: (0, off + j))
    return pl.pallas_call(
        body, name="conv_fwd", grid=(nb,),
        in_specs=[col(blk_b), col(blk_c), col(blk_x), pl.BlockSpec((CONV_K, COL_BLOCK), lambda j: (0, j))],
        out_specs=pl.BlockSpec((S, COL_BLOCK), lambda j: (0, j)),
        out_shape=jax.ShapeDtypeStruct((S, D), BF16),
        compiler_params=_cparams(("parallel",)),
    )(proj, proj, proj, w_conv)


def _merge_fwd(proj, ya, yb, blk_ga, blk_gb, ts):
    S, D = ya.shape
    nb = D // COL_BLOCK

    def body(ga_ref, gb_ref, ya_ref, yb_ref, o_ref):
        o_ref[...] = (_sigmoid(ga_ref[...]) * ya_ref[...] + _sigmoid(gb_ref[...]) * yb_ref[...]).astype(BF16)

    row = pl.BlockSpec((ts, D), lambda i: (i, 0))
    seg = lambda blk: pl.BlockSpec((pl.Element(ts), pl.Element(D)), lambda i: (i * ts, blk * COL_BLOCK))
    return pl.pallas_call(
        body, name="merge_fwd", grid=(S // ts,),
        in_specs=[seg(blk_ga), seg(blk_gb), row, row],
        out_specs=row,
        out_shape=jax.ShapeDtypeStruct((S, D), BF16),
        compiler_params=_cparams(("parallel",)),
    )(proj, proj, ya, yb)


def _ln1_fwd(x, mix, mod, g, b, ts):
    S, D = x.shape

    def body(x_ref, mix_ref, mod_ref, g_ref, b_ref, xhat_ref, rstd_ref, u2_ref):
        r = DEEPNORM_ALPHA * x_ref[...] + mod_ref[2:3, :] * mix_ref[...]
        mu = jnp.mean(r, axis=-1, keepdims=True)
        d = r - mu
        rstd = lax.rsqrt(jnp.mean(d * d, axis=-1, keepdims=True) + LN_EPS)
        xhat = d * rstd
        xhat_ref[...] = xhat
        rstd_ref[...] = rstd
        x1 = xhat * g_ref[...] + b_ref[...]
        u2_ref[...] = (x1 * (1.0 + mod_ref[4:5, :]) + mod_ref[3:4, :]).astype(BF16)

    row = pl.BlockSpec((ts, D), lambda i: (i, 0))
    vec = lambda r: pl.BlockSpec((r, D), lambda i: (0, 0))
    return pl.pallas_call(
        body, name="ln1_fwd", grid=(S // ts,),
        in_specs=[row, row, vec(6), vec(1), vec(1)],
        out_specs=[row, pl.BlockSpec((ts, 1), lambda i: (i, 0)), row],
        out_shape=[jax.ShapeDtypeStruct((S, D), F32), jax.ShapeDtypeStruct((S, 1), F32),
                   jax.ShapeDtypeStruct((S, D), BF16)],
        compiler_params=_cparams(("parallel",)),
    )(x, mix, mod, g, b)


def _swiglu_fwd(h, ts, tb):
    S, F2 = h.shape
    F = F2 // 2
    nb = F // tb

    def body(hg_ref, hu_ref, a_ref):
        hg = hg_ref[...]
        a_ref[...] = (hg * _sigmoid(hg) * hu_ref[...]).astype(BF16)

    return pl.pallas_call(
        body, name="swiglu_fwd", grid=(S // ts, nb),
        in_specs=[pl.BlockSpec((ts, tb), lambda i, j: (i, j)), pl.BlockSpec((ts, tb), lambda i, j: (i, j + nb))],
        out_specs=pl.BlockSpec((ts, tb), lambda i, j: (i, j)),
        out_shape=jax.ShapeDtypeStruct((S, F), BF16),
        compiler_params=_cparams(("parallel", "parallel")),
    )(h, h)


def _ln2_loss(xhat1, ffn, tgt, mod, g1, b1, g2, b2, ts):
    S, D = xhat1.shape

    def body(xh_ref, ffn_ref, t_ref, mod_ref, g1_ref, b1_ref, g2_ref, b2_ref, loss_ref, dffn_ref, dx1_ref, vec_ref):
        i = pl.program_id(0)

        @pl.when(i == 0)
        def _():
            loss_ref[...] = jnp.zeros_like(loss_ref)
            vec_ref[...] = jnp.zeros_like(vec_ref)

        x1 = xh_ref[...] * g1_ref[...] + b1_ref[...]
        ffn = ffn_ref[...]
        r = DEEPNORM_ALPHA * x1 + mod_ref[5:6, :] * ffn
        mu = jnp.mean(r, axis=-1, keepdims=True)
        d = r - mu
        rstd = lax.rsqrt(jnp.mean(d * d, axis=-1, keepdims=True) + LN_EPS)
        xhat = d * rstd
        e = xhat * g2_ref[...] + b2_ref[...] - t_ref[...]
        loss_ref[...] += 0.5 * jnp.sum(jnp.mean(e * e, axis=-1, keepdims=True))
        dy = e * (1.0 / D)
        dxhat = dy * g2_ref[...]
        dr = rstd * (dxhat - jnp.mean(dxhat, axis=-1, keepdims=True)
                     - xhat * jnp.mean(dxhat * xhat, axis=-1, keepdims=True))
        dffn_ref[...] = (dr * mod_ref[5:6, :]).astype(BF16)
        dx1_ref[...] = DEEPNORM_ALPHA * dr
        vec_ref[0:1, :] += jnp.sum(dy * xhat, axis=0, keepdims=True)
        vec_ref[1:2, :] += jnp.sum(dy, axis=0, keepdims=True)
        vec_ref[2:3, :] += jnp.sum(dr * ffn, axis=0, keepdims=True)

    row = pl.BlockSpec((ts, D), lambda i: (i, 0))
    vec = lambda r: pl.BlockSpec((r, D), lambda i: (0, 0))
    return pl.pallas_call(
        body, name="ln2_loss", grid=(S // ts,),
        in_specs=[row, row, row, vec(6), vec(1), vec(1), vec(1), vec(1)],
        out_specs=[pl.BlockSpec((1, LANE), lambda i: (0, 0)), row, row, vec(8)],
        out_shape=[jax.ShapeDtypeStruct((1, LANE), F32), jax.ShapeDtypeStruct((S, D), BF16),
                   jax.ShapeDtypeStruct((S, D), F32), jax.ShapeDtypeStruct((8, D), F32)],
        compiler_params=_cparams(("arbitrary",)),
    )(xhat1, ffn, tgt, mod, g1, b1, g2, b2)


def _swiglu_bwd(da, h, ts, tb):
    S, F2 = h.shape
    nb = (F2 // 2) // tb

    def body(da_ref, hg_ref, hu_ref, dh_ref):
        hg, da = hg_ref[...], da_ref[...]
        sg = _sigmoid(hg)

        @pl.when(pl.program_id(2) == 0)
        def _():
            dh_ref[...] = (da * hu_ref[...] * (sg * (1.0 + hg * (1.0 - sg)))).astype(BF16)

        @pl.when(pl.program_id(2) == 1)
        def _():
            dh_ref[...] = (da * hg * sg).astype(BF16)

    lo = pl.BlockSpec((ts, tb), lambda i, j, k: (i, j))
    hi = pl.BlockSpec((ts, tb), lambda i, j, k: (i, j + nb))
    return pl.pallas_call(
        body, name="swiglu_bwd", grid=(S // ts, nb, 2),
        in_specs=[lo, lo, hi],
        out_specs=pl.BlockSpec((ts, tb), lambda i, j, k: (i, j + nb * k)),
        out_shape=jax.ShapeDtypeStruct((S, F2), BF16),
        compiler_params=_cparams(("parallel", "parallel", "arbitrary")),
    )(da, h, h)


def _ln1_bwd(du2, dx1a, xhat1, rstd1, mix, mod, g1, b1, ts):
    S, D = xhat1.shape

    def body(du2_ref, dx1a_ref, xh_ref, rstd_ref, mix_ref, mod_ref, g_ref, b_ref, dxa_ref, dmix_ref, vec_ref):
        i = pl.program_id(0)

        @pl.when(i == 0)
        def _():
            vec_ref[...] = jnp.zeros_like(vec_ref)

        xhat, du2, mix = xh_ref[...], du2_ref[...], mix_ref[...]
        x1 = xhat * g_ref[...] + b_ref[...]
        dx1 = dx1a_ref[...] + du2 * (1.0 + mod_ref[4:5, :])
        dxhat = dx1 * g_ref[...]
        dr = rstd_ref[...] * (dxhat - jnp.mean(dxhat, axis=-1, keepdims=True)
                              - xhat * jnp.mean(dxhat * xhat, axis=-1, keepdims=True))
        dxa_ref[...] = DEEPNORM_ALPHA * dr
        dmix_ref[...] = (dr * mod_ref[2:3, :]).astype(BF16)
        vec_ref[0:1, :] += jnp.sum(du2, axis=0, keepdims=True)
        vec_ref[1:2, :] += jnp.sum(du2 * x1, axis=0, keepdims=True)
        vec_ref[2:3, :] += jnp.sum(dx1 * xhat, axis=0, keepdims=True)
        vec_ref[3:4, :] += jnp.sum(dx1, axis=0, keepdims=True)
        vec_ref[4:5, :] += jnp.sum(dr * mix, axis=0, keepdims=True)

    row = pl.BlockSpec((ts, D), lambda i: (i, 0))
    vec = lambda r: pl.BlockSpec((r, D), lambda i: (0, 0))
    return pl.pallas_call(
        body, name="ln1_bwd", grid=(S // ts,),
        in_specs=[row, row, row, pl.BlockSpec((ts, 1), lambda i: (i, 0)), row, vec(6), vec(1), vec(1)],
        out_specs=[row, row, vec(8)],
        out_shape=[jax.ShapeDtypeStruct((S, D), F32), jax.ShapeDtypeStruct((S, D), BF16),
                   jax.ShapeDtypeStruct((8, D), F32)],
        compiler_params=_cparams(("arbitrary",)),
    )(du2, dx1a, xhat1, rstd1, mix, mod, g1, b1)


def _merge_bwd(dmerged, proj, ya, yb, blk_ga, blk_gb, ts):
    S, D = ya.shape
    nb = D // COL_BLOCK

    def body(dm_ref, ga_ref, gb_ref, ya_ref, yb_ref, dya_ref, dyb_ref, dga_ref, dgb_ref):
        dm = dm_ref[...]
        sa, sb = _sigmoid(ga_ref[...]), _sigmoid(gb_ref[...])
        dya_ref[...] = (dm * sa).astype(BF16)
        dyb_ref[...] = (dm * sb).astype(BF16)
        dga_ref[...] = (dm * ya_ref[...] * sa * (1.0 - sa)).astype(BF16)
        dgb_ref[...] = (dm * yb_ref[...] * sb * (1.0 - sb)).astype(BF16)

    row = pl.BlockSpec((ts, D), lambda i: (i, 0))
    seg = lambda blk: pl.BlockSpec((pl.Element(ts), pl.Element(D)), lambda i: (i * ts, blk * COL_BLOCK))
    out = jax.ShapeDtypeStruct((S, D), BF16)
    return pl.pallas_call(
        body, name="merge_bwd", grid=(S // ts,),
        in_specs=[row, seg(blk_ga), seg(blk_gb), row, row],
        out_specs=[row] * 4,
        out_shape=[out] * 4,
        compiler_params=_cparams(("parallel",)),
    )(dmerged, proj, proj, ya, yb)


def _conv_bwd(dcbc, proj, w_conv, blk_b, blk_c, blk_x):
    S = proj.shape[0]
    D = w_conv.shape[1]
    nb = D // COL_BLOCK

    def body(d_ref, cb_ref, cc_ref, cx_ref, w_ref, dcb_ref, dcc_ref, dcx_ref, dw_ref):
        d, cc, cx = d_ref[...], cc_ref[...], cx_ref[...]
        z = cc * cx
        z1, z2 = _shift_rows(z, 1), _shift_rows(z, 2)
        conv = w_ref[2:3, :] * z + w_ref[1:2, :] * z1 + w_ref[0:1, :] * z2
        dcb_ref[...] = (d * conv).astype(BF16)
        dconv = d * cb_ref[...]
        dz = w_ref[2:3, :] * dconv + w_ref[1:2, :] * _shift_rows(dconv, -1) + w_ref[0:1, :] * _shift_rows(dconv, -2)
        dcc_ref[...] = (dz * cx).astype(BF16)
        dcx_ref[...] = (dz * cc).astype(BF16)
        dw_ref[...] = jnp.zeros_like(dw_ref)
        dw_ref[0:1, :] = jnp.sum(dconv * z2, axis=0, keepdims=True)
        dw_ref[1:2, :] = jnp.sum(dconv * z1, axis=0, keepdims=True)
        dw_ref[2:3, :] = jnp.sum(dconv * z, axis=0, keepdims=True)

    col = lambda off: pl.BlockSpec((S, COL_BLOCK), lambda j: (0, off + j))
    out = jax.ShapeDtypeStruct((S, D), BF16)
    return pl.pallas_call(
        body, name="conv_bwd", grid=(nb,),
        in_specs=[col(0), col(blk_b), col(blk_c), col(blk_x), pl.BlockSpec((CONV_K, COL_BLOCK), lambda j: (0, j))],
        out_specs=[col(0), col(0), col(0), pl.BlockSpec((8, COL_BLOCK), lambda j: (0, j))],
        out_shape=[out, out, out, jax.ShapeDtypeStruct((8, D), F32)],
        compiler_params=_cparams(("parallel",)),
    )(dcbc, proj, proj, proj, w_conv)


def _attn_bwd(qc, kc, vh, do, o, lse, T):
    H, S, _ = qc.shape
    n = S // T

    def body(q_ref, k_ref, v_ref, do_ref, o_ref, lse_ref, dq_ref, dk_ref, dv_ref, d_ref, dk_acc, dv_acc):
        j = pl.program_id(1)

        @pl.when(j == 0)
        def _():
            dq_ref[...] = jnp.zeros_like(dq_ref)
            d_ref[...] = jnp.sum(do_ref[...] * o_ref[...], axis=-1, keepdims=True)

        dk_acc[...] = jnp.zeros_like(dk_acc)
        dv_acc[...] = jnp.zeros_like(dv_acc)
        k, v = k_ref[0], v_ref[0]

        def step(i, masked):
            rows = pl.ds(pl.multiple_of(i * T, T), T)
            q = q_ref[0, rows, :]
            do = do_ref[rows, :].astype(BF16)
            s = lax.dot_general(q, k, NT_DIMS, preferred_element_type=F32) * ATTN_SCALE
            if masked:
                s = jnp.where(_diag_mask(T), s, NEG_INF)
            p = jnp.exp(s - lse_ref[0, rows, :])
            dv_acc[...] += lax.dot_general(p.astype(BF16), do, TN_DIMS, preferred_element_type=F32)
            dp = lax.dot_general(do, v, NT_DIMS, preferred_element_type=F32)
            ds = (p * (dp - d_ref[rows, :]) * ATTN_SCALE).astype(BF16)
            dk_acc[...] += lax.dot_general(ds, q, TN_DIMS, preferred_element_type=F32)
            dq_ref[0, rows, :] += jnp.dot(ds, k, preferred_element_type=F32)

        def above(i, carry):
            step(i, False)
            return carry

        step(j, True)
        lax.fori_loop(j + 1, n, above, 0)
        dk_ref[0] = dk_acc[...]
        dv_ref[0] = dv_acc[...]

    head = lambda w: pl.BlockSpec((1, S, w), lambda h, j: (h, 0, 0))
    blk = lambda w: pl.BlockSpec((1, T, w), lambda h, j: (h, j, 0))
    ospec = pl.BlockSpec((S, V_HEAD), lambda h, j: (0, h))
    return pl.pallas_call(
        body, name="attn_bwd", grid=(H, n),
        in_specs=[head(QK_CAT), blk(QK_CAT), blk(V_HEAD), ospec, ospec, head(1)],
        out_specs=[head(QK_CAT), blk(QK_CAT), blk(V_HEAD)],
        out_shape=[jax.ShapeDtypeStruct((H, S, QK_CAT), F32), jax.ShapeDtypeStruct((H, S, QK_CAT), F32),
                   jax.ShapeDtypeStruct((H, S, V_HEAD), F32)],
        scratch_shapes=[pltpu.VMEM((S, 1), F32), pltpu.VMEM((T, QK_CAT), F32), pltpu.VMEM((T, V_HEAD), F32)],
        compiler_params=_cparams(("parallel", "arbitrary")),
    )(qc, kc, vh, do, o, lse)


def _qk_bwd(dqc, dkc, dvh, cos_q, sin_q, cos_k, sin_k, ts):
    H, S, _ = dqc.shape
    pair = 2 * QK_CAT
    kv_w = QK_NOPE + V_HEAD

    def body(dqc_ref, dkc_ref, dvh_ref, cq_ref, sq_ref, ck_ref, sk_ref, dq_ref, dkv_ref, dkr_ref, q_buf, kr_buf):
        for p in range(H // 2):
            q_buf[:, :QK_CAT] = dqc_ref[2 * p]
            q_buf[:, QK_CAT:] = dqc_ref[2 * p + 1]
            g = q_buf[...]
            dq_ref[:, p * pair:(p + 1) * pair] = (
                g * cq_ref[...] - _rope_partner(g, QK_CAT, QK_NOPE) * sq_ref[...]).astype(BF16)
        kr_sum = jnp.zeros((ts, QK_ROPE), F32)
        for h in range(H):
            dkv_ref[:, h * kv_w:h * kv_w + QK_NOPE] = dkc_ref[h, :, 0:QK_NOPE].astype(BF16)
            dkv_ref[:, h * kv_w + QK_NOPE:(h + 1) * kv_w] = dvh_ref[h].astype(BF16)
            kr_sum = kr_sum + dkc_ref[h, :, QK_NOPE:QK_CAT]
        kr_buf[...] = jnp.zeros_like(kr_buf)
        kr_buf[:, 0:QK_ROPE] = kr_sum
        kr = kr_buf[...]
        dkr_ref[...] = (kr * ck_ref[...] - _rope_partner(kr, QK_ROPE, 0) * sk_ref[...]).astype(BF16)

    row = lambda w: pl.BlockSpec((ts, w), lambda i: (i, 0))
    head = lambda w: pl.BlockSpec((H, ts, w), lambda i: (0, i, 0))
    return pl.pallas_call(
        body, name="qk_bwd", grid=(S // ts,),
        in_specs=[head(QK_CAT), head(QK_CAT), head(V_HEAD), row(pair), row(pair), row(COL_BLOCK), row(COL_BLOCK)],
        out_specs=[row(H * QK_CAT), row(H * kv_w), row(COL_BLOCK)],
        out_shape=[jax.ShapeDtypeStruct((S, H * QK_CAT), BF16), jax.ShapeDtypeStruct((S, H * kv_w), BF16),
                   jax.ShapeDtypeStruct((S, COL_BLOCK), BF16)],
        scratch_shapes=[pltpu.VMEM((ts, pair), F32), pltpu.VMEM((ts, COL_BLOCK), F32)],
        compiler_params=_cparams(("parallel",)),
    )(dqc, dkc, dvh, cos_q, sin_q, cos_k, sin_k)


def _rms_bwd(dy, proj, g, blk, L, ts, name):
    S = proj.shape[0]

    def body(dy_ref, a_ref, g_ref, da_ref, dg_ref):
        i = pl.program_id(0)

        @pl.when(i == 0)
        def _():
            dg_ref[...] = jnp.zeros_like(dg_ref)

        a, dy = a_ref[...], dy_ref[...]
        r = lax.rsqrt(jnp.mean(a * a, axis=-1, keepdims=True) + RMS_EPS)
        dyh = dy * g_ref[...]
        da = r * dyh - a * (r * r * r) * jnp.mean(dyh * a, axis=-1, keepdims=True)
        da_ref[...] = da.astype(BF16)
        dg_ref[0:1, :] += jnp.sum(dy * a * r, axis=0, keepdims=True)

    return pl.pallas_call(
        body, name=name, grid=(S // ts,),
        in_specs=[pl.BlockSpec((ts, L), lambda i: (i, 0)), pl.BlockSpec((ts, L), lambda i: (i, blk)),
                  pl.BlockSpec((1, L), lambda i: (0, 0))],
        out_specs=[pl.BlockSpec((ts, L), lambda i: (i, 0)), pl.BlockSpec((8, L), lambda i: (0, 0))],
        out_shape=[jax.ShapeDtypeStruct((S, L), BF16), jax.ShapeDtypeStruct((8, L), F32)],
        compiler_params=_cparams(("arbitrary",)),
    )(dy, proj, g)


def _grad_x(du, dxa, x, mod, ts):
    S, D = x.shape

    def body(du_ref, dxa_ref, x_ref, mod_ref, dx_ref, vec_ref):
        i = pl.program_id(0)

        @pl.when(i == 0)
        def _():
            vec_ref[...] = jnp.zeros_like(vec_ref)

        du = du_ref[...]
        dx_ref[...] = dxa_ref[...] + du * (1.0 + mod_ref[1:2, :])
        vec_ref[0:1, :] += jnp.sum(du, axis=0, keepdims=True)
        vec_ref[1:2, :] += jnp.sum(du * x_ref[...], axis=0, keepdims=True)

    row = pl.BlockSpec((ts, D), lambda i: (i, 0))
    vec = lambda r: pl.BlockSpec((r, D), lambda i: (0, 0))
    return pl.pallas_call(
        body, name="grad_x", grid=(S // ts,),
        in_specs=[row, row, row, vec(6)],
        out_specs=[row, vec(8)],
        out_shape=[jax.ShapeDtypeStruct((S, D), F32), jax.ShapeDtypeStruct((8, D), F32)],
        compiler_params=_cparams(("arbitrary",)),
    )(du, dxa, x, mod)


def _adamw(w, g, m, v, name):
    R, C = w.shape
    tr = _tile(R, max(8, (1 << 19) // C), 8)
    c1 = 1.0 / (1.0 - ADAM_B1 ** ADAM_STEP)
    c2 = 1.0 / (1.0 - ADAM_B2 ** ADAM_STEP)

    def body(w_ref, g_ref, m_ref, v_ref, d_ref, nm_ref, nv_ref):
        g = g_ref[...]
        m = ADAM_B1 * m_ref[...] + (1.0 - ADAM_B1) * g
        v = ADAM_B2 * v_ref[...] + (1.0 - ADAM_B2) * (g * g)
        nm_ref[...] = m
        nv_ref[...] = v
        d_ref[...] = -ADAM_LR * ((m * c1) / (jnp.sqrt(v * c2) + ADAM_EPS) + ADAM_WD * w_ref[...])

    spec = pl.BlockSpec((tr, C), lambda i: (i, 0))
    out = jax.ShapeDtypeStruct((R, C), F32)
    return pl.pallas_call(
        body, name=name, grid=(R // tr,),
        in_specs=[spec] * 4, out_specs=[spec] * 3, out_shape=[out] * 3,
        compiler_params=_cparams(("parallel",)),
    )(w, g, m, v)


def _adamw_reduced(w, own, got, m, v, my_chip, name):
    R, C = w.shape
    tr = _tile(R, max(PACK_ROW_ALIGN, (1 << 18) // C), PACK_ROW_ALIGN)
    c1 = 1.0 / (1.0 - ADAM_B1 ** ADAM_STEP)
    c2 = 1.0 / (1.0 - ADAM_B2 ** ADAM_STEP)

    def body(chip_ref, w_ref, own_ref, g1_ref, g2_ref, g3_ref, m_ref, v_ref, g_ref, d_ref, nm_ref, nv_ref):
        g = own_ref[0].astype(F32) + g1_ref[0].astype(F32) + g2_ref[0].astype(F32) + g3_ref[0].astype(F32)
        m = ADAM_B1 * m_ref[...] + (1.0 - ADAM_B1) * g
        v = ADAM_B2 * v_ref[...] + (1.0 - ADAM_B2) * (g * g)
        g_ref[...] = g
        nm_ref[...] = m
        nv_ref[...] = v
        d_ref[...] = -ADAM_LR * ((m * c1) / (jnp.sqrt(v * c2) + ADAM_EPS) + ADAM_WD * w_ref[...])

    spec = pl.BlockSpec((tr, C), lambda i, chip: (i, 0))
    slot = lambda k: pl.BlockSpec((1, tr, C), lambda i, chip: (chip[0] ^ k, i, 0))
    out = jax.ShapeDtypeStruct((R, C), F32)
    return pl.pallas_call(
        body, name=name,
        grid_spec=pltpu.PrefetchScalarGridSpec(
            num_scalar_prefetch=1, grid=(R // tr,),
            in_specs=[spec, slot(0), slot(1), slot(2), slot(3), spec, spec],
            out_specs=[spec] * 4),
        out_shape=[out] * 4,
        compiler_params=_cparams(("parallel",)),
    )(my_chip, w, own, got, got, got, m, v)


def _my_place():
    return lax.axis_index("x"), lax.axis_index("y"), lax.axis_index("c")


def _peer(k):
    x, y, c = _my_place()
    return (x ^ ((k >> 2) & 1), y ^ ((k >> 1) & 1), c ^ (k & 1))


def _linear(place):
    return 4 * place[0] + 2 * place[1] + place[2]


def _ada_fwd(c_row, wconv_row, w_ada, b_row):
    D, CW = w_ada.shape
    WC = wconv_row.shape[-1]

    def body(c_ref, wc_ref, w_ref, b_ref, mod_ref, cact_ref, wcall_ref, send_buf, sems):
        me = _linear(_my_place())
        c = c_ref[0]
        cact_ref[me] = c * _sigmoid(c)
        wcall_ref[me] = wc_ref[0]

        def gather_copy(buf, k, grp):
            return pltpu.make_async_remote_copy(
                src_ref=buf.at[me], dst_ref=buf.at[me], send_sem=sems.at[0, grp, k], recv_sem=sems.at[1, grp, k],
                device_id=_peer(k), device_id_type=MESH_ID)

        def gather_recv(buf, k, grp):
            src = _linear(_peer(k))
            return pltpu.make_async_remote_copy(
                src_ref=buf.at[src], dst_ref=buf.at[src], send_sem=sems.at[0, grp, k], recv_sem=sems.at[1, grp, k],
                device_id=_peer(k), device_id_type=MESH_ID)

        for k in range(1, N_DEV):
            gather_copy(cact_ref, k, 0).start()
            gather_copy(wcall_ref, k, 1).start()
        for k in range(1, N_DEV):
            gather_recv(cact_ref, k, 0).wait_recv()
            gather_recv(wcall_ref, k, 1).wait_recv()
        for k in range(1, N_DEV):
            gather_copy(cact_ref, k, 0).wait_send()
            gather_copy(wcall_ref, k, 1).wait_send()

        cact = jnp.concatenate([cact_ref[b] for b in range(N_DEV)], axis=0)
        mod_all = jnp.dot(cact.astype(BF16), w_ref[...].astype(BF16), preferred_element_type=F32) + b_ref[0]
        for b in range(N_DEV):
            send_buf[b] = mod_all[b:b + 1, :]
        mod_ref[me] = send_buf[me]

        def scatter_copy(k):
            dst = _linear(_peer(k))
            return pltpu.make_async_remote_copy(
                src_ref=send_buf.at[dst], dst_ref=mod_ref.at[me], send_sem=sems.at[0, 2, k], recv_sem=sems.at[1, 2, k],
                device_id=_peer(k), device_id_type=MESH_ID)

        def scatter_recv(k):
            src = _linear(_peer(k))
            return pltpu.make_async_remote_copy(
                src_ref=send_buf.at[src], dst_ref=mod_ref.at[src], send_sem=sems.at[0, 2, k], recv_sem=sems.at[1, 2, k],
                device_id=_peer(k), device_id_type=MESH_ID)

        for k in range(1, N_DEV):
            scatter_copy(k).start()
        for k in range(1, N_DEV):
            scatter_recv(k).wait_recv()
        for k in range(1, N_DEV):
            scatter_copy(k).wait_send()

    vmem = pl.BlockSpec(memory_space=pltpu.VMEM)
    return pl.pallas_call(
        body, name="ada_fwd",
        in_specs=[vmem] * 4, out_specs=[vmem] * 3,
        out_shape=[jax.ShapeDtypeStruct((N_DEV, 1, CW), F32), jax.ShapeDtypeStruct((N_DEV, 1, D), F32),
                   jax.ShapeDtypeStruct((N_DEV, 1, WC), F32)],
        scratch_shapes=[pltpu.VMEM((N_DEV, 1, CW), F32), pltpu.SemaphoreType.DMA((2, 3, N_DEV))],
        compiler_params=pltpu.CompilerParams(vmem_limit_bytes=VMEM_LIMIT),
    )(c_row, wconv_row, w_ada, b_row)


def _ada_bwd(payload, cact_t, n_mod):
    NCH, _, CW = payload.shape
    D = cact_t.shape[0]

    def body(p_ref, ct_ref, sum_ref, gw_ref, all_ref, sems):
        me = _linear(_my_place())
        all_ref[me] = p_ref[...]

        def copy(k, slot):
            return pltpu.make_async_remote_copy(
                src_ref=all_ref.at[slot], dst_ref=all_ref.at[slot], send_sem=sems.at[0, k], recv_sem=sems.at[1, k],
                device_id=_peer(k), device_id_type=MESH_ID)

        for k in range(1, N_DEV):
            copy(k, me).start()
        for k in range(1, N_DEV):
            copy(k, _linear(_peer(k))).wait_recv()
        for k in range(1, N_DEV):
            copy(k, me).wait_send()

        total = all_ref[0]
        for b in range(1, N_DEV):
            total = total + all_ref[b]
        sum_ref[...] = total

        ct = ct_ref[...].astype(BF16).astype(F32)
        gw = jnp.zeros((D, CW), F32)
        for b in range(N_DEV):
            dm = all_ref[b, me].astype(BF16).astype(F32)
            gw = gw + ct[:, b:b + 1] * dm
        gw_ref[...] = gw

    vmem = pl.BlockSpec(memory_space=pltpu.VMEM)
    return pl.pallas_call(
        body, name="ada_bwd",
        in_specs=[vmem, vmem], out_specs=[vmem, vmem],
        out_shape=[jax.ShapeDtypeStruct((NCH, 1, CW), F32), jax.ShapeDtypeStruct((D, CW), F32)],
        scratch_shapes=[pltpu.VMEM((N_DEV, NCH, 1, CW), F32), pltpu.SemaphoreType.DMA((2, N_DEV))],
        compiler_params=pltpu.CompilerParams(vmem_limit_bytes=VMEM_LIMIT),
    )(payload, cact_t)


def _exchange_in_chip(parts):
    W = len(parts)

    def body(*refs):
        p_refs, got_refs, (send_sems, recv_sems) = refs[:W], refs[W:2 * W], refs[2 * W:]
        x, y, c = _my_place()
        sibling = (x, y, 1 - c)
        copies = []
        for w in range(W):
            for q in range(4):
                copies.append(pltpu.make_async_remote_copy(
                    src_ref=p_refs[w].at[2 * q + (1 - c)], dst_ref=got_refs[w].at[q],
                    send_sem=send_sems.at[4 * w + q], recv_sem=recv_sems.at[4 * w + q],
                    device_id=sibling, device_id_type=MESH_ID))
        for cp in copies:
            cp.start()
        for cp in copies:
            cp.wait_recv()
        for cp in copies:
            cp.wait_send()

    return pl.pallas_call(
        body, name="grad_exchange_in_chip",
        in_specs=[HBM_SPEC] * W, out_specs=[HBM_SPEC] * W,
        out_shape=[jax.ShapeDtypeStruct((4,) + p.shape[1:], p.dtype) for p in parts],
        scratch_shapes=[pltpu.SemaphoreType.DMA((4 * W,)), pltpu.SemaphoreType.DMA((4 * W,))],
    )(*parts)


def _pair_sum(parts, got, core):
    _, R, C = parts.shape
    tr = _tile(R, max(PACK_ROW_ALIGN, PAIR_SUM_BLOCK // C), PACK_ROW_ALIGN)

    def body(c_ref, p_ref, g_ref, o_ref):
        o_ref[...] = (p_ref[...].astype(F32) + g_ref[...].astype(F32)).astype(o_ref.dtype)

    return pl.pallas_call(
        body, name="grad_pair_sum",
        grid_spec=pltpu.PrefetchScalarGridSpec(
            num_scalar_prefetch=1, grid=(4, R // tr),
            in_specs=[pl.BlockSpec((1, tr, C), lambda q, i, c_ref: (2 * q + c_ref[0], i, 0)),
                      pl.BlockSpec((1, tr, C), lambda q, i, c_ref: (q, i, 0))],
            out_specs=pl.BlockSpec((1, tr, C), lambda q, i, c_ref: (q, i, 0))),
        out_shape=jax.ShapeDtypeStruct((4, R, C), parts.dtype),
        compiler_params=_cparams(("parallel", "parallel")),
    )(core, parts, got)


HBM_SPEC = pl.BlockSpec(memory_space=pltpu.HBM)
SEM_SPEC = pl.BlockSpec(memory_space=pltpu.SEMAPHORE)
ANY_SPEC = pl.BlockSpec(memory_space=pl.ANY)
SPLIT_EFFECT = pltpu.SideEffectType.DATAFLOW_SIDE_EFFECTING


def _landing_zone(shape, dtype):
    return pltpu.with_memory_space_constraint(lax.empty(shape, dtype), pltpu.HBM)


def _split_start(name, arrays, lands, after, copies_of, per_array):
    W = len(arrays)

    def body(*refs):
        x_refs, land_refs = refs[:W], refs[W:2 * W]
        send_sems, recv_sems = refs[2 * W + 1], refs[2 * W + 2]
        token = refs[-1]
        k = 0
        for w in range(W):
            for src, dst, dev in copies_of(w, x_refs[w], land_refs[w]):
                pltpu.make_async_remote_copy(src_ref=src, dst_ref=dst, send_sem=send_sems.at[k], recv_sem=recv_sems.at[k],
                                             device_id=dev, device_id_type=MESH_ID).start()
                k += 1
        token[...] = jnp.zeros_like(token)

    n_copies = per_array * W
    hbm_of = lambda xs: tuple(pltpu.HBM(a.shape, a.dtype) for a in xs)
    out = pl.pallas_call(
        body, name=name,
        out_shape=(pltpu.SemaphoreType.DMA((n_copies,)), pltpu.SemaphoreType.DMA((n_copies,)))
        + hbm_of(arrays) + hbm_of(lands) + (jax.ShapeDtypeStruct((8, LANE), F32),),
        in_specs=(HBM_SPEC,) * (2 * W) + (ANY_SPEC,),
        out_specs=(SEM_SPEC, SEM_SPEC) + (HBM_SPEC,) * (2 * W) + (pl.BlockSpec(memory_space=pltpu.VMEM),),
        input_output_aliases={i: 2 + i for i in range(2 * W)},
        compiler_params=pltpu.CompilerParams(has_side_effects=SPLIT_EFFECT),
    )(*[pltpu.with_memory_space_constraint(a, pltpu.HBM) for a in arrays], *lands, after)
    return out[0], out[1], list(out[2:2 + W]), list(out[2 + W:2 + 2 * W]), out[-1]


def _split_wait(name, state, after, copies_of):
    send_sems, recv_sems, arrays, lands, _ = state
    W = len(arrays)
    after = tuple(after) if isinstance(after, (tuple, list)) else (after,)

    def body(*refs):
        x_refs, land_refs = refs[:W], refs[W:2 * W]
        send_sems, recv_sems = refs[2 * W], refs[2 * W + 1]
        k = 0
        for w in range(W):
            for src, dst, dev in copies_of(w, x_refs[w], land_refs[w]):
                cp = pltpu.make_async_remote_copy(src_ref=src, dst_ref=dst, send_sem=send_sems.at[k],
                                                  recv_sem=recv_sems.at[k], device_id=dev, device_id_type=MESH_ID)
                cp.wait_send()
                cp.wait_recv()
                k += 1

    out = pl.pallas_call(
        body, name=name,
        out_shape=tuple(pltpu.HBM(a.shape, a.dtype) for a in arrays + lands),
        in_specs=(HBM_SPEC,) * (2 * W) + (SEM_SPEC, SEM_SPEC) + (ANY_SPEC,) * len(after),
        out_specs=(HBM_SPEC,) * (2 * W),
        input_output_aliases={i: i for i in range(2 * W)},
        compiler_params=pltpu.CompilerParams(has_side_effects=SPLIT_EFFECT),
    )(*arrays, *lands, send_sems, recv_sems, *after)
    return list(out[:W]), list(out[W:])


def _scatter_copies(w, p_ref, land_ref):
    x, y, c = _my_place()
    my_chip = 2 * x + y
    return [(p_ref.at[2 * (x ^ (k >> 1)) + (y ^ (k & 1))], land_ref.at[my_chip], (x ^ (k >> 1), y ^ (k & 1), c))
            for k in range(1, 4)]


def _gather_copies(w, x_ref, land_ref):
    x, y, c = _my_place()
    me = _linear((x, y, c))
    devs = [(x, y, 1 - c)] + [(x ^ (k >> 1), y ^ (k & 1), c) for k in range(1, 4)]
    return [(x_ref, land_ref.at[me], d) for d in devs]


def _gather_forward(lands, name):
    W = len(lands)

    def body(*refs):
        land_refs, out_refs, (send_sems, recv_sems) = refs[:W], refs[W:2 * W], refs[2 * W:]
        x, y, c = _my_place()
        sibling = (x, y, 1 - c)
        sends, arrivals = [], []
        for w in range(W):
            for k in range(1, 4):
                px, py = x ^ (k >> 1), y ^ (k & 1)
                landed, theirs = _linear((px, py, c)), out_refs[w].at[_linear((px, py, 1 - c))]
                sem = 3 * w + k - 1
                sends.append(pltpu.make_async_remote_copy(
                    src_ref=land_refs[w].at[landed], dst_ref=out_refs[w].at[landed],
                    send_sem=send_sems.at[sem], recv_sem=recv_sems.at[sem], device_id=sibling, device_id_type=MESH_ID))
                arrivals.append(pltpu.make_async_remote_copy(
                    src_ref=theirs, dst_ref=theirs, send_sem=send_sems.at[sem], recv_sem=recv_sems.at[sem],
                    device_id=sibling, device_id_type=MESH_ID))
        for cp in sends:
            cp.start()
        for cp in arrivals:
            cp.wait_recv()
        for cp in sends:
            cp.wait_send()

    return pl.pallas_call(
        body, name=name,
        in_specs=[HBM_SPEC] * W, out_specs=[HBM_SPEC] * W,
        out_shape=[jax.ShapeDtypeStruct(l.shape, l.dtype) for l in lands],
        input_output_aliases={i: i for i in range(W)},
        scratch_shapes=[pltpu.SemaphoreType.DMA((3 * W,)), pltpu.SemaphoreType.DMA((3 * W,))],
    )(*lands)


def _with_own_slot(gathered, shard):
    return lax.dynamic_update_index_in_dim(gathered, shard[None], _linear(_my_place()), axis=0)


def _reduce_scatter_begin(parts, tag):
    got = _exchange_in_chip(parts)
    core = lax.axis_index("c").astype(jnp.int32).reshape(1)
    chip_parts = [_pair_sum(p, g, core) for p, g in zip(parts, got)]
    lands = [_landing_zone(p.shape, p.dtype) for p in chip_parts]
    return _split_start("grad_scatter_start_" + tag, chip_parts, lands, got[0], _scatter_copies, 3)


def _reduce_scatter_end(state, after, tag):
    return _split_wait("grad_scatter_wait_" + tag, state, after, _scatter_copies)


def kernel(x, c, positions, w_ada, b_ada, w_in, g_q_a, w_q_b, g_kv_a, w_kv_b, w_o_a, w_conv, w_o_b, w_o, ln1_g, ln1_b, w_ffn_in, w_ffn_out, ln2_g, ln2_b, loss_target, m_w_ada, m_b_ada, m_w_in, m_g_q_a, m_w_q_b, m_g_kv_a, m_w_kv_b, m_w_o_a, m_w_conv, m_w_o_b, m_w_o, m_ln1_g, m_ln1_b, m_w_ffn_in, m_w_ffn_out, m_ln2_g, m_ln2_b, v_w_ada, v_b_ada, v_w_in, v_g_q_a, v_w_q_b, v_g_kv_a, v_w_kv_b, v_w_o_a, v_w_conv, v_w_o_b, v_w_o, v_ln1_g, v_ln1_b, v_w_ffn_in, v_w_ffn_out, v_ln2_g, v_ln2_b):
    x2, tgt = x[0], loss_target[0]
    S, D = x2.shape
    Lq, Lkv = g_q_a.shape[1], g_kv_a.shape[1]
    H = w_q_b.shape[2] * N_DEV // QK_CAT
    F = w_ffn_out.shape[1] * N_DEV
    assert Lq == Lkv and (Lq + Lkv) % COL_BLOCK == 0 and D % COL_BLOCK == 0
    front = Lq + Lkv + QK_ROPE
    front_pad = _round_up(front, COL_BLOCK)
    kr_blk = (Lq + Lkv) // COL_BLOCK
    blk_b = front_pad // COL_BLOCK
    nblk = D // COL_BLOCK
    blk_c, blk_x, blk_ga, blk_gb = blk_b + nblk, blk_b + 2 * nblk, blk_b + 3 * nblk, blk_b + 4 * nblk
    ts = _tile(S, 256, 8)
    T = _tile(S, min(512, S // 2), CHUNK)
    tb = _tile(F, 2816)
    me = _linear(_my_place())

    landing = lambda shards: [_landing_zone((N_DEV,) + s.shape, BF16) for s in shards]
    first = [w[0].astype(BF16) for w in (w_in, w_q_b, w_kv_b)]
    first_state = _split_start("first_gather_start", first, landing(first), c, _gather_copies, 4)
    later = [w[0].astype(BF16) for w in (w_o_a, w_o_b, w_o, w_ffn_in, w_ffn_out)]

    cw = w_ada.shape[2]
    b_mine = lax.dynamic_slice(b_ada, (0, me * cw), (1, cw)).reshape(1, 1, cw)
    c_row = c.reshape(1, 1, D) + first_state[4][0, 0]
    mod_blocks, cact_all, wconv_all = _ada_fwd(c_row, w_conv[0].reshape(1, 1, -1), w_ada[0], b_mine)
    mod = mod_blocks.reshape(6, D)
    cact_all = cact_all.reshape(N_DEV, D)
    w_conv_full = wconv_all.reshape(N_DEV, CONV_K, -1).transpose(1, 0, 2).reshape(CONV_K, D)
    u = _modulate_in(x2, mod, ts)

    first_shards, first_lands = _split_wait("first_gather_wait", first_state, (u, *later), _gather_copies)
    g_in, wq_s, wkv_s = [_with_own_slot(g, s) for g, s in
                         zip(_gather_forward(first_lands, "first_gather_forward"), first_shards)]
    later_state = _split_start("weight_gather_start", later, landing(later), g_in, _gather_copies, 4)
    later_token = later_state[4]
    w_in_p = _assemble_w_in(g_in, front, front_pad)

    inv_freq = 1.0 / (ROPE_THETA ** (jnp.arange(0, QK_ROPE, 2, dtype=F32) / QK_ROPE))
    ang = positions[0].astype(F32)[:, None] * inv_freq
    cos2 = jnp.concatenate([jnp.cos(ang), jnp.cos(ang)], axis=-1)
    sin2 = jnp.concatenate([jnp.sin(ang), jnp.sin(ang)], axis=-1)
    one, zero = jnp.ones((S, QK_NOPE), F32), jnp.zeros((S, QK_NOPE), F32)
    cos_q, sin_q = jnp.concatenate([one, cos2, one, cos2], axis=-1), jnp.concatenate([zero, sin2, zero, sin2], axis=-1)
    cos_k, sin_k = jnp.tile(cos2, (1, COL_BLOCK // QK_ROPE)), jnp.tile(sin2, (1, COL_BLOCK // QK_ROPE))

    proj = _matmul(u, w_in_p, "nn", F32, "proj", deps=(later_token,))
    qn = _rms_fwd(proj, g_q_a, 0, Lq, ts, "rms_q")
    kvn = _rms_fwd(proj, g_kv_a, 1, Lkv, ts, "rms_kv")
    q = _matmul(qn, wq_s, "nn", F32, "q_up")
    kv = _matmul(kvn, wkv_s, "nn", F32, "kv_up")
    qc, kc, vh = _qk_prep(q, kv, proj, kr_blk, cos_q, sin_q, cos_k, sin_k, H, ts)
    attn, lse = _attn_fwd(qc, kc, vh, T)
    later_shards, later_lands = _split_wait("weight_gather_wait", later_state, lse, _gather_copies)
    later_all = _gather_forward(later_lands, "weight_gather_forward")
    g_oa, g_ob, g_o, w_fi_s, g_fo = [_with_own_slot(g, s) for g, s in zip(later_all, later_shards)]
    w_oa_f, w_ob_f, w_o_f = g_oa.reshape(-1, D), g_ob.reshape(-1, D), g_o.reshape(-1, D)
    w_fo_f = g_fo.reshape(F, D)
    ya = _matmul(attn, w_oa_f, "nn", F32, "attn_out")
    cbc = _conv_fwd(proj, w_conv_full, blk_b, blk_c, blk_x)
    yb = _matmul(cbc, w_ob_f, "nn", F32, "conv_out")
    merged = _merge_fwd(proj, ya, yb, blk_ga, blk_gb, ts)
    mix = _matmul(merged, w_o_f, "nn", F32, "mix_out")
    xhat1, rstd1, u2 = _ln1_fwd(x2, mix, mod, ln1_g, ln1_b, ts)
    hh = _matmul(u2, w_fi_s, "nn", F32, "ffn_in")
    act = _swiglu_fwd(hh, ts, tb)
    ffn = _matmul(act, w_fo_f, "nn", F32, "ffn_out")
    loss_part, dffn, dx1a, vec2 = _ln2_loss(xhat1, ffn, tgt, mod, ln1_g, ln1_b, ln2_g, ln2_b, ts)
    loss = lax.psum(loss_part[0, 0], AXES)

    gw_fo = _matmul(act, dffn, "tn", BF16, "grad_w_ffn_out")
    da = _matmul(dffn, w_fo_f, "nt", F32, "d_act")
    dh = _swiglu_bwd(da, hh, ts, tb)
    gw_fi = _matmul(u2, dh, "tn", BF16, "grad_w_ffn_in", out_shards=True)
    ffn_state = _reduce_scatter_begin([gw_fi, gw_fo.reshape(N_DEV, -1, D)], "ffn")
    du2 = _matmul(dh, w_fi_s, "nt", F32, "d_u2", deps=(ffn_state[4],))
    dxa, dmix, vec1 = _ln1_bwd(du2, dx1a, xhat1, rstd1, mix, mod, ln1_g, ln1_b, ts)
    gw_o = _matmul(merged, dmix, "tn", BF16, "grad_w_o")
    dmerged = _matmul(dmix, w_o_f, "nt", F32, "d_merged")
    dya, dyb, dga, dgb = _merge_bwd(dmerged, proj, ya, yb, blk_ga, blk_gb, ts)
    gw_ob = _matmul(cbc, dyb, "tn", BF16, "grad_w_o_b")
    dcbc = _matmul(dyb, w_ob_f, "nt", F32, "d_conv")
    dcb, dcc, dcx, dwconv = _conv_bwd(dcbc, proj, w_conv_full, blk_b, blk_c, blk_x)
    gw_oa = _matmul(attn, dya, "tn", BF16, "grad_w_o_a")
    mix_state = _reduce_scatter_begin([g.reshape(N_DEV, -1, D) for g in (gw_oa, gw_ob, gw_o)], "mix")
    dattn = _matmul(dya, w_oa_f, "nt", F32, "d_attn", deps=(mix_state[4],))
    dqc, dkc, dvh = _attn_bwd(qc, kc, vh, dattn, attn, lse, T)
    ffn_own, ffn_got = _reduce_scatter_end(ffn_state, dqc, "ffn")
    mix_own, mix_got = _reduce_scatter_end(mix_state, dqc, "mix")
    dq, dkv, dkr = _qk_bwd(dqc, dkc, dvh, cos_q, sin_q, cos_k, sin_k, ts)
    gw_qb = _matmul(qn, dq, "tn", BF16, "grad_w_q_b", out_shards=True)
    dqn = _matmul(dq, wq_s, "nt", F32, "d_qn")
    gw_kvb = _matmul(kvn, dkv, "tn", BF16, "grad_w_kv_b", out_shards=True)
    dkvn = _matmul(dkv, wkv_s, "nt", F32, "d_kvn")
    dqa, dgq = _rms_bwd(dqn, proj, g_q_a, 0, Lq, ts, "rms_q_bwd")
    dkva, dgkv = _rms_bwd(dkvn, proj, g_kv_a, 1, Lkv, ts, "rms_kv_bwd")
    dproj = jnp.concatenate([dqa, dkva, dkr, dcb, dcc, dcx, dga, dgb], axis=1)
    gw_in_p = _matmul(u, dproj, "tn", BF16, "grad_w_in")
    in_state = _reduce_scatter_begin([_split_w_in(gw_in_p, front, front_pad), gw_qb, gw_kvb], "in")
    du = _matmul(dproj, w_in_p, "nt", F32, "d_u", deps=(in_state[4],))
    grad_x, vec0 = _grad_x(du, dxa, x2, mod, ts)

    n_mod = 6 * D // cw
    dmod = jnp.concatenate([vec0[0], vec0[1], vec1[4], vec1[0], vec1[1], vec2[2]])
    small = jnp.concatenate([dmod, dgq[0], dgkv[0], vec1[2], vec1[3], vec2[0], vec2[1], dwconv[:CONV_K].reshape(-1)])
    n_small = small.shape[0]
    nch = _round_up(n_small, cw) // cw
    payload = jnp.pad(small, (0, nch * cw - n_small)).reshape(nch, 1, cw)
    summed, g_w_ada = _ada_bwd(payload, cact_all.T, n_mod)
    summed = summed.reshape(-1)
    offs = [0, 6 * D, 6 * D + Lq, 6 * D + Lq + Lkv]
    offs += [offs[-1] + D * k for k in range(1, 5)]
    g_b_ada = summed[offs[0]:offs[1]].reshape(1, -1)
    g_gq = summed[offs[1]:offs[2]].reshape(1, -1)
    g_gkv = summed[offs[2]:offs[3]].reshape(1, -1)
    g_ln1g, g_ln1b, g_ln2g, g_ln2b = [summed[offs[3 + k]:offs[4 + k]].reshape(1, -1) for k in range(4)]
    wc = w_conv.shape[2]
    g_wconv = lax.dynamic_slice(summed[offs[7]:offs[7] + CONV_K * D].reshape(CONV_K, D), (0, me * wc), (CONV_K, wc))

    names = ["w_ada", "b_ada", "w_in", "g_q_a", "w_q_b", "g_kv_a", "w_kv_b", "w_o_a", "w_conv", "w_o_b", "w_o",
             "ln1_g", "ln1_b", "w_ffn_in", "w_ffn_out", "ln2_g", "ln2_b"]
    weights = [w_ada, b_ada, w_in, g_q_a, w_q_b, g_kv_a, w_kv_b, w_o_a, w_conv, w_o_b, w_o, ln1_g, ln1_b,
               w_ffn_in, w_ffn_out, ln2_g, ln2_b]
    moms = [m_w_ada, m_b_ada, m_w_in, m_g_q_a, m_w_q_b, m_g_kv_a, m_w_kv_b, m_w_o_a, m_w_conv, m_w_o_b, m_w_o,
            m_ln1_g, m_ln1_b, m_w_ffn_in, m_w_ffn_out, m_ln2_g, m_ln2_b]
    vels = [v_w_ada, v_b_ada, v_w_in, v_g_q_a, v_w_q_b, v_g_kv_a, v_w_kv_b, v_w_o_a, v_w_conv, v_w_o_b, v_w_o,
            v_ln1_g, v_ln1_b, v_w_ffn_in, v_w_ffn_out, v_ln2_g, v_ln2_b]
    grad_of = {"w_ada": g_w_ada, "b_ada": g_b_ada, "g_q_a": g_gq, "g_kv_a": g_gkv, "w_conv": g_wconv,
               "ln1_g": g_ln1g, "ln1_b": g_ln1b, "ln2_g": g_ln2g, "ln2_b": g_ln2b}
    state_of = dict(zip(names, zip(weights, moms, vels)))
    results = {}
    my_chip = (2 * lax.axis_index("x") + lax.axis_index("y")).astype(jnp.int32).reshape(1)

    def update(nm, reduced=None):
        w, m, v = state_of[nm]
        shp = w.shape
        w2 = w.reshape(shp[-2], shp[-1]) if w.ndim == 3 else w
        m2, v2 = m.reshape(w2.shape), v.reshape(w2.shape)
        if reduced is None:
            g2 = grad_of[nm].reshape(w2.shape)
            res = (g2,) + tuple(_adamw(w2, g2, m2, v2, "adamw_" + nm))
        else:
            res = _adamw_reduced(w2, reduced[0], reduced[1], m2, v2, my_chip, "adamw_" + nm)
        results[nm] = [a.reshape(shp) for a in res]

    for nm in grad_of:
        update(nm)
    for nm, own, got in zip(("w_ffn_in", "w_ffn_out"), ffn_own, ffn_got):
        update(nm, (own, got))
    for nm, own, got in zip(("w_o_a", "w_o_b", "w_o"), mix_own, mix_got):
        update(nm, (own, got))
    in_own, in_got = _reduce_scatter_end(in_state, [res[1] for res in results.values()], "in")
    for nm, own, got in zip(("w_in", "w_q_b", "w_kv_b"), in_own, in_got):
        update(nm, (own, got))
    outs = [[results[nm][k] for nm in names] for k in range(4)]
    return (loss, grad_x.reshape(x.shape), *outs[0], *outs[1], *outs[2], *outs[3])
```

```python
import functools

import jax
import jax.numpy as jnp
from jax import lax
from jax.experimental import pallas as pl
from jax.experimental.pallas import tpu as pltpu

F32 = jnp.float32
BF16 = jnp.bfloat16
MESH_ID = pl.DeviceIdType.MESH
AXES = ("x", "y", "c")
N_DEV = 8

CHUNK = 64
QK_NOPE = 128
QK_ROPE = 64
V_HEAD = 128
QK_CAT = QK_NOPE + QK_ROPE
ROPE_THETA = 10000.0
ATTN_SCALE = (QK_NOPE + QK_ROPE) ** -0.5
CONV_K = 3
DEEPNORM_ALPHA = 2.0 ** 0.25
LN_EPS = 1e-5
RMS_EPS = 1e-6
NEG_INF = -1e30

ADAM_LR = 0.001
ADAM_B1 = 0.9
ADAM_B2 = 0.999
ADAM_EPS = 1e-08
ADAM_WD = 0.01
ADAM_STEP = 10

LANE = 128
COL_BLOCK = 256
PACK_ROW_ALIGN = 16
PAIR_SUM_BLOCK = 1 << 20
VMEM_LIMIT = 48 * 1024 * 1024


def _round_up(n, m):
    return (n + m - 1) // m * m


def _tile(n, pref, align=LANE):
    best = None
    t = align
    while t <= min(n, pref):
        if n % t == 0:
            best = t
        t += align
    return best if best is not None else n


def _cparams(sem=None):
    return pltpu.CompilerParams(dimension_semantics=sem, vmem_limit_bytes=VMEM_LIMIT)


def _sigmoid(x):
    return 0.5 * jnp.tanh(0.5 * x) + 0.5


def _matmul(a, b, mode, out_dtype, name, tm=1024, tn=1024, tk=2048, deps=(), out_shards=False):
    b_shards = b.ndim == 3
    n = b.shape[2] if b_shards else (b.shape[1] // N_DEV if out_shards else None)
    if mode == "nn":
        (M, K), (K2, N) = a.shape, (b.shape[1], N_DEV * n) if b_shards else b.shape
    elif mode == "nt":
        (M, K), (N, K2) = a.shape, (b.shape[1], N_DEV * n) if b_shards else b.shape
    else:
        (K, M), (K2, N) = a.shape, b.shape
    assert K == K2, (a.shape, b.shape, mode)
    tm = _tile(M, tm)
    tn = n if (mode != "nt" and n is not None) else _tile(N, tn)
    tk = n if (mode == "nt" and b_shards) else _tile(K, tk)
    nk = K // tk
    if mode == "nn":
        a_spec = pl.BlockSpec((tm, tk), lambda i, j, k: (i, k))
        b_spec = (pl.BlockSpec((1, tk, n), lambda i, j, k: (j, k, 0)) if b_shards
                  else pl.BlockSpec((tk, tn), lambda i, j, k: (k, j)))
        dims = (((1,), (0,)), ((), ()))
    elif mode == "nt":
        a_spec = pl.BlockSpec((tm, tk), lambda i, j, k: (i, k))
        b_spec = (pl.BlockSpec((1, tn, n), lambda i, j, k: (k, j, 0)) if b_shards
                  else pl.BlockSpec((tn, tk), lambda i, j, k: (j, k)))
        dims = (((1,), (1,)), ((), ()))
    else:
        a_spec = pl.BlockSpec((tk, tm), lambda i, j, k: (k, i))
        b_spec = pl.BlockSpec((tk, tn), lambda i, j, k: (k, j))
        dims = (((0,), (0,)), ((), ()))
    if out_shards:
        out_spec = pl.BlockSpec((1, tm, n), lambda i, j, k: (j, i, 0))
        out_shape = jax.ShapeDtypeStruct((N_DEV, M, n), out_dtype)
    else:
        out_spec = pl.BlockSpec((tm, tn), lambda i, j, k: (i, j))
        out_shape = jax.ShapeDtypeStruct((M, N), out_dtype)

    def product(a_ref, b_ref):
        b_blk = b_ref[0] if b_shards else b_ref[...]
        return lax.dot_general(a_ref[...].astype(BF16), b_blk.astype(BF16), dims, preferred_element_type=F32)

    def write(o_ref, value):
        if out_shards:
            o_ref[0] = value.astype(o_ref.dtype)
        else:
            o_ref[...] = value.astype(o_ref.dtype)

    def body_whole_k(a_ref, b_ref, *rest):
        write(rest[-1], product(a_ref, b_ref))

    def body_split_k(a_ref, b_ref, *rest):
        o_ref, acc_ref = rest[-2:]
        k = pl.program_id(2)

        @pl.when(k == 0)
        def _():
            acc_ref[...] = jnp.zeros_like(acc_ref)

        acc_ref[...] += product(a_ref, b_ref)

        @pl.when(k == nk - 1)
        def _():
            write(o_ref, acc_ref[...])

    return pl.pallas_call(
        body_whole_k if nk == 1 else body_split_k, name=name, grid=(M // tm, N // tn, nk),
        in_specs=[a_spec, b_spec] + [ANY_SPEC] * len(deps),
        out_specs=out_spec, out_shape=out_shape,
        scratch_shapes=[] if nk == 1 else [pltpu.VMEM((tm, tn), F32)],
        compiler_params=_cparams(("parallel", "parallel", "arbitrary")),
    )(a, b, *deps)


def _assemble_w_in(shards, front, front_pad):
    _, K, n = shards.shape
    gap = front_pad - front
    tk = _tile(K, 256, PACK_ROW_ALIGN)

    def body(g_ref, o_ref):
        if gap:
            o_ref[:, front:front_pad] = jnp.zeros((tk, gap), o_ref.dtype)
        for j in range(N_DEV):
            lo, hi = j * n, (j + 1) * n
            if lo < front < hi:
                o_ref[:, lo:front] = g_ref[j, :, 0:front - lo]
                o_ref[:, front_pad:hi + gap] = g_ref[j, :, front - lo:n]
            else:
                off = 0 if hi <= front else gap
                o_ref[:, lo + off:hi + off] = g_ref[j]

    return pl.pallas_call(
        body, name="assemble_w_in", grid=(K // tk,),
        in_specs=[pl.BlockSpec((N_DEV, tk, n), lambda i: (0, i, 0))],
        out_specs=pl.BlockSpec((tk, N_DEV * n + gap), lambda i: (i, 0)),
        out_shape=jax.ShapeDtypeStruct((K, N_DEV * n + gap), shards.dtype),
        compiler_params=_cparams(("parallel",)),
    )(shards)


def _split_w_in(w, front, front_pad):
    K, NP = w.shape
    gap = front_pad - front
    n = (NP - gap) // N_DEV
    tk = _tile(K, 256, PACK_ROW_ALIGN)

    def body(w_ref, o_ref):
        for j in range(N_DEV):
            lo, hi = j * n, (j + 1) * n
            if lo < front < hi:
                o_ref[j, :, 0:front - lo] = w_ref[:, lo:front]
                o_ref[j, :, front - lo:n] = w_ref[:, front_pad:hi + gap]
            else:
                off = 0 if hi <= front else gap
                o_ref[j] = w_ref[:, lo + off:hi + off]

    return pl.pallas_call(
        body, name="split_grad_w_in", grid=(K // tk,),
        in_specs=[pl.BlockSpec((tk, NP), lambda i: (i, 0))],
        out_specs=pl.BlockSpec((N_DEV, tk, n), lambda i: (0, i, 0)),
        out_shape=jax.ShapeDtypeStruct((N_DEV, K, n), w.dtype),
        compiler_params=_cparams(("parallel",)),
    )(w)


def _modulate_in(x, mod, ts):
    S, D = x.shape

    def body(x_ref, mod_ref, u_ref):
        u_ref[...] = (x_ref[...] * (1.0 + mod_ref[1:2, :]) + mod_ref[0:1, :]).astype(BF16)

    return pl.pallas_call(
        body, name="modulate_in", grid=(S // ts,),
        in_specs=[pl.BlockSpec((ts, D), lambda i: (i, 0)), pl.BlockSpec((6, D), lambda i: (0, 0))],
        out_specs=pl.BlockSpec((ts, D), lambda i: (i, 0)),
        out_shape=jax.ShapeDtypeStruct((S, D), BF16),
        compiler_params=_cparams(("parallel",)),
    )(x, mod)


def _rms_fwd(proj, g, blk, L, ts, name):
    S = proj.shape[0]

    def body(a_ref, g_ref, y_ref):
        a = a_ref[...]
        r = lax.rsqrt(jnp.mean(a * a, axis=-1, keepdims=True) + RMS_EPS)
        y_ref[...] = (a * r * g_ref[...]).astype(BF16)

    return pl.pallas_call(
        body, name=name, grid=(S // ts,),
        in_specs=[pl.BlockSpec((ts, L), lambda i: (i, blk)), pl.BlockSpec((1, L), lambda i: (0, 0))],
        out_specs=pl.BlockSpec((ts, L), lambda i: (i, 0)),
        out_shape=jax.ShapeDtypeStruct((S, L), BF16),
        compiler_params=_cparams(("parallel",)),
    )(proj, g)


def _rope_partner(x, period, start):
    w = x.shape[-1]
    lane = lax.broadcasted_iota(jnp.int32, x.shape, x.ndim - 1) % period
    first = (lane >= start) & (lane < start + QK_ROPE // 2)
    from_right = pltpu.roll(x, w - QK_ROPE // 2, axis=x.ndim - 1)
    from_left = pltpu.roll(x, QK_ROPE // 2, axis=x.ndim - 1)
    return jnp.where(first, -from_right, from_left)


def _qk_prep(q, kv, proj, kr_blk, cos_q, sin_q, cos_k, sin_k, H, ts):
    S = q.shape[0]
    pair = 2 * QK_CAT
    kv_w = QK_NOPE + V_HEAD

    def body(q_ref, kv_ref, kr_ref, cq_ref, sq_ref, ck_ref, sk_ref, qc_ref, kc_ref, vh_ref):
        kr = kr_ref[...]
        kr = kr * ck_ref[...] + _rope_partner(kr, QK_ROPE, 0) * sk_ref[...]
        kr = kr[:, :QK_ROPE].astype(BF16)
        for p in range(H // 2):
            x = q_ref[:, p * pair:(p + 1) * pair]
            x = x * cq_ref[...] + _rope_partner(x, QK_CAT, QK_NOPE) * sq_ref[...]
            qc_ref[2 * p] = x[:, :QK_CAT].astype(BF16)
            qc_ref[2 * p + 1] = x[:, QK_CAT:].astype(BF16)
        for h in range(H):
            kc_ref[h, :, 0:QK_NOPE] = kv_ref[:, h * kv_w:h * kv_w + QK_NOPE].astype(BF16)
            kc_ref[h, :, QK_NOPE:QK_CAT] = kr
            vh_ref[h, :, :] = kv_ref[:, h * kv_w + QK_NOPE:(h + 1) * kv_w].astype(BF16)

    row = lambda w: pl.BlockSpec((ts, w), lambda i: (i, 0))
    return pl.pallas_call(
        body, name="qk_prep", grid=(S // ts,),
        in_specs=[row(H * QK_CAT), row(H * kv_w),
                  pl.BlockSpec((ts, COL_BLOCK), lambda i: (i, kr_blk)),
                  row(pair), row(pair), row(COL_BLOCK), row(COL_BLOCK)],
        out_specs=[pl.BlockSpec((H, ts, QK_CAT), lambda i: (0, i, 0)),
                   pl.BlockSpec((H, ts, QK_CAT), lambda i: (0, i, 0)),
                   pl.BlockSpec((H, ts, V_HEAD), lambda i: (0, i, 0))],
        out_shape=[jax.ShapeDtypeStruct((H, S, QK_CAT), BF16), jax.ShapeDtypeStruct((H, S, QK_CAT), BF16),
                   jax.ShapeDtypeStruct((H, S, V_HEAD), BF16)],
        compiler_params=_cparams(("parallel",)),
    )(q, kv, proj, cos_q, sin_q, cos_k, sin_k)


NT_DIMS = (((1,), (1,)), ((), ()))
TN_DIMS = (((0,), (0,)), ((), ()))


def _diag_mask(T):
    rows = lax.broadcasted_iota(jnp.int32, (T, T), 0) // CHUNK
    cols = lax.broadcasted_iota(jnp.int32, (T, T), 1) // CHUNK
    return cols <= rows


def _attn_fwd(qc, kc, vh, T):
    H, S, _ = qc.shape
    n = S // T

    def body(q_ref, k_ref, v_ref, o_ref, lse_ref, m_ref, l_ref, acc_ref):
        i = pl.program_id(1)
        q = q_ref[0]
        m_ref[...] = jnp.full_like(m_ref, NEG_INF)
        l_ref[...] = jnp.zeros_like(l_ref)
        acc_ref[...] = jnp.zeros_like(acc_ref)

        def step(j, masked):
            rows = pl.ds(pl.multiple_of(j * T, T), T)
            s = lax.dot_general(q, k_ref[0, rows, :], NT_DIMS, preferred_element_type=F32) * ATTN_SCALE
            if masked:
                s = jnp.where(_diag_mask(T), s, NEG_INF)
            m_old = m_ref[...]
            m_new = jnp.maximum(m_old, jnp.max(s, axis=-1, keepdims=True))
            alpha = jnp.exp(m_old - m_new)
            p = jnp.exp(s - m_new)
            l_ref[...] = alpha * l_ref[...] + jnp.sum(p, axis=-1, keepdims=True)
            acc_ref[...] = alpha * acc_ref[...] + jnp.dot(p.astype(BF16), v_ref[0, rows, :],
                                                          preferred_element_type=F32)
            m_ref[...] = m_new

        def below(j, carry):
            step(j, False)
            return carry

        lax.fori_loop(0, i, below, 0)
        step(i, True)
        o_ref[...] = acc_ref[...] / l_ref[...]
        lse_ref[0] = m_ref[...] + jnp.log(l_ref[...])

    return pl.pallas_call(
        body, name="attn_fwd", grid=(H, n),
        in_specs=[pl.BlockSpec((1, T, QK_CAT), lambda h, i: (h, i, 0)),
                  pl.BlockSpec((1, S, QK_CAT), lambda h, i: (h, 0, 0)),
                  pl.BlockSpec((1, S, V_HEAD), lambda h, i: (h, 0, 0))],
        out_specs=[pl.BlockSpec((T, V_HEAD), lambda h, i: (i, h)),
                   pl.BlockSpec((1, T, 1), lambda h, i: (h, i, 0))],
        out_shape=[jax.ShapeDtypeStruct((S, H * V_HEAD), F32), jax.ShapeDtypeStruct((H, S, 1), F32)],
        scratch_shapes=[pltpu.VMEM((T, 1), F32), pltpu.VMEM((T, 1), F32), pltpu.VMEM((T, V_HEAD), F32)],
        compiler_params=_cparams(("parallel", "arbitrary")),
    )(qc, kc, vh)


def _shift_rows(z, k):
    if k == 0:
        return z
    n = z.shape[0]
    row = lax.broadcasted_iota(jnp.int32, z.shape, 0)
    if k > 0:
        return jnp.where(row >= k, pltpu.roll(z, k, axis=0), 0.0)
    return jnp.where(row < n + k, pltpu.roll(z, n + k, axis=0), 0.0)


def _conv_fwd(proj, w_conv, blk_b, blk_c, blk_x):
    S = proj.shape[0]
    D = w_conv.shape[1]
    nb = D // COL_BLOCK

    def body(cb_ref, cc_ref, cx_ref, w_ref, o_ref):
        z = cc_ref[...] * cx_ref[...]
        conv = w_ref[2:3, :] * z + w_ref[1:2, :] * _shift_rows(z, 1) + w_ref[0:1, :] * _shift_rows(z, 2)
        o_ref[...] = (cb_ref[...] * conv).astype(BF16)

    col = lambda off: pl.BlockSpec((S, COL_BLOCK), lambda j: (0, off + j))
    return pl.pallas_call(
        body, name="conv_fwd", grid=(nb,),
        in_specs=[col(blk_b), col(blk_c), col(blk_x), pl.BlockSpec((CONV_K, COL_BLOCK), lambda j: (0, j))],
        out_specs=pl.BlockSpec((S, COL_BLOCK), lambda j: (0, j)),
        out_shape=jax.ShapeDtypeStruct((S, D), BF16),
        compiler_params=_cparams(("parallel",)),
    )(proj, proj, proj, w_conv)


def _merge_fwd(proj, ya, yb, blk_ga, blk_gb, ts):
    S, D = ya.shape
    nb = D // COL_BLOCK

    def body(ga_ref, gb_ref, ya_ref, yb_ref, o_ref):
        o_ref[...] = (_sigmoid(ga_ref[...]) * ya_ref[...] + _sigmoid(gb_ref[...]) * yb_ref[...]).astype(BF16)

    row = pl.BlockSpec((ts, D), lambda i: (i, 0))
    seg = lambda blk: pl.BlockSpec((pl.Element(ts), pl.Element(D)), lambda i: (i * ts, blk * COL_BLOCK))
    return pl.pallas_call(
        body, name="merge_fwd", grid=(S // ts,),
        in_specs=[seg(blk_ga), seg(blk_gb), row, row],
        out_specs=row,
        out_shape=jax.ShapeDtypeStruct((S, D), BF16),
        compiler_params=_cparams(("parallel",)),
    )(proj, proj, ya, yb)


def _ln1_fwd(x, mix, mod, g, b, ts):
    S, D = x.shape

    def body(x_ref, mix_ref, mod_ref, g_ref, b_ref, xhat_ref, rstd_ref, u2_ref):
        r = DEEPNORM_ALPHA * x_ref[...] + mod_ref[2:3, :] * mix_ref[...]
        mu = jnp.mean(r, axis=-1, keepdims=True)
        d = r - mu
        rstd = lax.rsqrt(jnp.mean(d * d, axis=-1, keepdims=True) + LN_EPS)
        xhat = d * rstd
        xhat_ref[...] = xhat
        rstd_ref[...] = rstd
        x1 = xhat * g_ref[...] + b_ref[...]
        u2_ref[...] = (x1 * (1.0 + mod_ref[4:5, :]) + mod_ref[3:4, :]).astype(BF16)

    row = pl.BlockSpec((ts, D), lambda i: (i, 0))
    vec = lambda r: pl.BlockSpec((r, D), lambda i: (0, 0))
    return pl.pallas_call(
        body, name="ln1_fwd", grid=(S // ts,),
        in_specs=[row, row, vec(6), vec(1), vec(1)],
        out_specs=[row, pl.BlockSpec((ts, 1), lambda i: (i, 0)), row],
        out_shape=[jax.ShapeDtypeStruct((S, D), F32), jax.ShapeDtypeStruct((S, 1), F32),
                   jax.ShapeDtypeStruct((S, D), BF16)],
        compiler_params=_cparams(("parallel",)),
    )(x, mix, mod, g, b)


def _swiglu_fwd(h, ts, tb):
    S, F2 = h.shape
    F = F2 // 2
    nb = F // tb

    def body(hg_ref, hu_ref, a_ref):
        hg = hg_ref[...]
        a_ref[...] = (hg * _sigmoid(hg) * hu_ref[...]).astype(BF16)

    return pl.pallas_call(
        body, name="swiglu_fwd", grid=(S // ts, nb),
        in_specs=[pl.BlockSpec((ts, tb), lambda i, j: (i, j)), pl.BlockSpec((ts, tb), lambda i, j: (i, j + nb))],
        out_specs=pl.BlockSpec((ts, tb), lambda i, j: (i, j)),
        out_shape=jax.ShapeDtypeStruct((S, F), BF16),
        compiler_params=_cparams(("parallel", "parallel")),
    )(h, h)


def _ln2_loss(xhat1, ffn, tgt, mod, g1, b1, g2, b2, ts):
    S, D = xhat1.shape

    def body(xh_ref, ffn_ref, t_ref, mod_ref, g1_ref, b1_ref, g2_ref, b2_ref, loss_ref, dffn_ref, dx1_ref, vec_ref):
        i = pl.program_id(0)

        @pl.when(i == 0)
        def _():
            loss_ref[...] = jnp.zeros_like(loss_ref)
            vec_ref[...] = jnp.zeros_like(vec_ref)

        x1 = xh_ref[...] * g1_ref[...] + b1_ref[...]
        ffn = ffn_ref[...]
        r = DEEPNORM_ALPHA * x1 + mod_ref[5:6, :] * ffn
        mu = jnp.mean(r, axis=-1, keepdims=True)
        d = r - mu
        rstd = lax.rsqrt(jnp.mean(d * d, axis=-1, keepdims=True) + LN_EPS)
        xhat = d * rstd
        e = xhat * g2_ref[...] + b2_ref[...] - t_ref[...]
        loss_ref[...] += 0.5 * jnp.sum(jnp.mean(e * e, axis=-1, keepdims=True))
        dy = e * (1.0 / D)
        dxhat = dy * g2_ref[...]
        dr = rstd * (dxhat - jnp.mean(dxhat, axis=-1, keepdims=True)
                     - xhat * jnp.mean(dxhat * xhat, axis=-1, keepdims=True))
        dffn_ref[...] = (dr * mod_ref[5:6, :]).astype(BF16)
        dx1_ref[...] = DEEPNORM_ALPHA * dr
        vec_ref[0:1, :] += jnp.sum(dy * xhat, axis=0, keepdims=True)
        vec_ref[1:2, :] += jnp.sum(dy, axis=0, keepdims=True)
        vec_ref[2:3, :] += jnp.sum(dr * ffn, axis=0, keepdims=True)

    row = pl.BlockSpec((ts, D), lambda i: (i, 0))
    vec = lambda r: pl.BlockSpec((r, D), lambda i: (0, 0))
    return pl.pallas_call(
        body, name="ln2_loss", grid=(S // ts,),
        in_specs=[row, row, row, vec(6), vec(1), vec(1), vec(1), vec(1)],
        out_specs=[pl.BlockSpec((1, LANE), lambda i: (0, 0)), row, row, vec(8)],
        out_shape=[jax.ShapeDtypeStruct((1, LANE), F32), jax.ShapeDtypeStruct((S, D), BF16),
                   jax.ShapeDtypeStruct((S, D), F32), jax.ShapeDtypeStruct((8, D), F32)],
        compiler_params=_cparams(("arbitrary",)),
    )(xhat1, ffn, tgt, mod, g1, b1, g2, b2)


def _swiglu_bwd(da, h, ts, tb):
    S, F2 = h.shape
    nb = (F2 // 2) // tb

    def body(da_ref, hg_ref, hu_ref, dh_ref):
        hg, da = hg_ref[...], da_ref[...]
        sg = _sigmoid(hg)

        @pl.when(pl.program_id(2) == 0)
        def _():
            dh_ref[...] = (da * hu_ref[...] * (sg * (1.0 + hg * (1.0 - sg)))).astype(BF16)

        @pl.when(pl.program_id(2) == 1)
        def _():
            dh_ref[...] = (da * hg * sg).astype(BF16)

    lo = pl.BlockSpec((ts, tb), lambda i, j, k: (i, j))
    hi = pl.BlockSpec((ts, tb), lambda i, j, k: (i, j + nb))
    return pl.pallas_call(
        body, name="swiglu_bwd", grid=(S // ts, nb, 2),
        in_specs=[lo, lo, hi],
        out_specs=pl.BlockSpec((ts, tb), lambda i, j, k: (i, j + nb * k)),
        out_shape=jax.ShapeDtypeStruct((S, F2), BF16),
        compiler_params=_cparams(("parallel", "parallel", "arbitrary")),
    )(da, h, h)


def _ln1_bwd(du2, dx1a, xhat1, rstd1, mix, mod, g1, b1, ts):
    S, D = xhat1.shape

    def body(du2_ref, dx1a_ref, xh_ref, rstd_ref, mix_ref, mod_ref, g_ref, b_ref, dxa_ref, dmix_ref, vec_ref):
        i = pl.program_id(0)

        @pl.when(i == 0)
        def _():
            vec_ref[...] = jnp.zeros_like(vec_ref)

        xhat, du2, mix = xh_ref[...], du2_ref[...], mix_ref[...]
        x1 = xhat * g_ref[...] + b_ref[...]
        dx1 = dx1a_ref[...] + du2 * (1.0 + mod_ref[4:5, :])
        dxhat = dx1 * g_ref[...]
        dr = rstd_ref[...] * (dxhat - jnp.mean(dxhat, axis=-1, keepdims=True)
                              - xhat * jnp.mean(dxhat * xhat, axis=-1, keepdims=True))
        dxa_ref[...] = DEEPNORM_ALPHA * dr
        dmix_ref[...] = (dr * mod_ref[2:3, :]).astype(BF16)
        vec_ref[0:1, :] += jnp.sum(du2, axis=0, keepdims=True)
        vec_ref[1:2, :] += jnp.sum(du2 * x1, axis=0, keepdims=True)
        vec_ref[2:3, :] += jnp.sum(dx1 * xhat, axis=0, keepdims=True)
        vec_ref[3:4, :] += jnp.sum(dx1, axis=0, keepdims=True)
        vec_ref[4:5, :] += jnp.sum(dr * mix, axis=0, keepdims=True)

    row = pl.BlockSpec((ts, D), lambda i: (i, 0))
    vec = lambda r: pl.BlockSpec((r, D), lambda i: (0, 0))
    return pl.pallas_call(
        body, name="ln1_bwd", grid=(S // ts,),
        in_specs=[row, row, row, pl.BlockSpec((ts, 1), lambda i: (i, 0)), row, vec(6), vec(1), vec(1)],
        out_specs=[row, row, vec(8)],
        out_shape=[jax.ShapeDtypeStruct((S, D), F32), jax.ShapeDtypeStruct((S, D), BF16),
                   jax.ShapeDtypeStruct((8, D), F32)],
        compiler_params=_cparams(("arbitrary",)),
    )(du2, dx1a, xhat1, rstd1, mix, mod, g1, b1)


def _merge_bwd(dmerged, proj, ya, yb, blk_ga, blk_gb, ts):
    S, D = ya.shape
    nb = D // COL_BLOCK

    def body(dm_ref, ga_ref, gb_ref, ya_ref, yb_ref, dya_ref, dyb_ref, dga_ref, dgb_ref):
        dm = dm_ref[...]
        sa, sb = _sigmoid(ga_ref[...]), _sigmoid(gb_ref[...])
        dya_ref[...] = (dm * sa).astype(BF16)
        dyb_ref[...] = (dm * sb).astype(BF16)
        dga_ref[...] = (dm * ya_ref[...] * sa * (1.0 - sa)).astype(BF16)
        dgb_ref[...] = (dm * yb_ref[...] * sb * (1.0 - sb)).astype(BF16)

    row = pl.BlockSpec((ts, D), lambda i: (i, 0))
    seg = lambda blk: pl.BlockSpec((pl.Element(ts), pl.Element(D)), lambda i: (i * ts, blk * COL_BLOCK))
    out = jax.ShapeDtypeStruct((S, D), BF16)
    return pl.pallas_call(
        body, name="merge_bwd", grid=(S // ts,),
        in_specs=[row, seg(blk_ga), seg(blk_gb), row, row],
        out_specs=[row] * 4,
        out_shape=[out] * 4,
        compiler_params=_cparams(("parallel",)),
    )(dmerged, proj, proj, ya, yb)


def _conv_bwd(dcbc, proj, w_conv, blk_b, blk_c, blk_x):
    S = proj.shape[0]
    D = w_conv.shape[1]
    nb = D // COL_BLOCK

    def body(d_ref, cb_ref, cc_ref, cx_ref, w_ref, dcb_ref, dcc_ref, dcx_ref, dw_ref):
        d, cc, cx = d_ref[...], cc_ref[...], cx_ref[...]
        z = cc * cx
        z1, z2 = _shift_rows(z, 1), _shift_rows(z, 2)
        conv = w_ref[2:3, :] * z + w_ref[1:2, :] * z1 + w_ref[0:1, :] * z2
        dcb_ref[...] = (d * conv).astype(BF16)
        dconv = d * cb_ref[...]
        dz = w_ref[2:3, :] * dconv + w_ref[1:2, :] * _shift_rows(dconv, -1) + w_ref[0:1, :] * _shift_rows(dconv, -2)
        dcc_ref[...] = (dz * cx).astype(BF16)
        dcx_ref[...] = (dz * cc).astype(BF16)
        dw_ref[...] = jnp.zeros_like(dw_ref)
        dw_ref[0:1, :] = jnp.sum(dconv * z2, axis=0, keepdims=True)
        dw_ref[1:2, :] = jnp.sum(dconv * z1, axis=0, keepdims=True)
        dw_ref[2:3, :] = jnp.sum(dconv * z, axis=0, keepdims=True)

    col = lambda off: pl.BlockSpec((S, COL_BLOCK), lambda j: (0, off + j))
    out = jax.ShapeDtypeStruct((S, D), BF16)
    return pl.pallas_call(
        body, name="conv_bwd", grid=(nb,),
        in_specs=[col(0), col(blk_b), col(blk_c), col(blk_x), pl.BlockSpec((CONV_K, COL_BLOCK), lambda j: (0, j))],
        out_specs=[col(0), col(0), col(0), pl.BlockSpec((8, COL_BLOCK), lambda j: (0, j))],
        out_shape=[out, out, out, jax.ShapeDtypeStruct((8, D), F32)],
        compiler_params=_cparams(("parallel",)),
    )(dcbc, proj, proj, proj, w_conv)


def _attn_bwd(qc, kc, vh, do, o, lse, T):
    H, S, _ = qc.shape
    n = S // T

    def body(q_ref, k_ref, v_ref, do_ref, o_ref, lse_ref, dq_ref, dk_ref, dv_ref, d_ref, dk_acc, dv_acc):
        j = pl.program_id(1)

        @pl.when(j == 0)
        def _():
            dq_ref[...] = jnp.zeros_like(dq_ref)
            d_ref[...] = jnp.sum(do_ref[...] * o_ref[...], axis=-1, keepdims=True)

        dk_acc[...] = jnp.zeros_like(dk_acc)
        dv_acc[...] = jnp.zeros_like(dv_acc)
        k, v = k_ref[0], v_ref[0]

        def step(i, masked):
            rows = pl.ds(pl.multiple_of(i * T, T), T)
            q = q_ref[0, rows, :]
            do = do_ref[rows, :].astype(BF16)
            s = lax.dot_general(q, k, NT_DIMS, preferred_element_type=F32) * ATTN_SCALE
            if masked:
                s = jnp.where(_diag_mask(T), s, NEG_INF)
            p = jnp.exp(s - lse_ref[0, rows, :])
            dv_acc[...] += lax.dot_general(p.astype(BF16), do, TN_DIMS, preferred_element_type=F32)
            dp = lax.dot_general(do, v, NT_DIMS, preferred_element_type=F32)
            ds = (p * (dp - d_ref[rows, :]) * ATTN_SCALE).astype(BF16)
            dk_acc[...] += lax.dot_general(ds, q, TN_DIMS, preferred_element_type=F32)
            dq_ref[0, rows, :] += jnp.dot(ds, k, preferred_element_type=F32)

        def above(i, carry):
            step(i, False)
            return carry

        step(j, True)
        lax.fori_loop(j + 1, n, above, 0)
        dk_ref[0] = dk_acc[...]
        dv_ref[0] = dv_acc[...]

    head = lambda w: pl.BlockSpec((1, S, w), lambda h, j: (h, 0, 0))
    blk = lambda w: pl.BlockSpec((1, T, w), lambda h, j: (h, j, 0))
    ospec = pl.BlockSpec((S, V_HEAD), lambda h, j: (0, h))
    return pl.pallas_call(
        body, name="attn_bwd", grid=(H, n),
        in_specs=[head(QK_CAT), blk(QK_CAT), blk(V_HEAD), ospec, ospec, head(1)],
        out_specs=[head(QK_CAT), blk(QK_CAT), blk(V_HEAD)],
        out_shape=[jax.ShapeDtypeStruct((H, S, QK_CAT), F32), jax.ShapeDtypeStruct((H, S, QK_CAT), F32),
                   jax.ShapeDtypeStruct((H, S, V_HEAD), F32)],
        scratch_shapes=[pltpu.VMEM((S, 1), F32), pltpu.VMEM((T, QK_CAT), F32), pltpu.VMEM((T, V_HEAD), F32)],
        compiler_params=_cparams(("parallel", "arbitrary")),
    )(qc, kc, vh, do, o, lse)


def _qk_bwd(dqc, dkc, dvh, cos_q, sin_q, cos_k, sin_k, ts):
    H, S, _ = dqc.shape
    pair = 2 * QK_CAT
    kv_w = QK_NOPE + V_HEAD

    def body(dqc_ref, dkc_ref, dvh_ref, cq_ref, sq_ref, ck_ref, sk_ref, dq_ref, dkv_ref, dkr_ref, q_buf, kr_buf):
        for p in range(H // 2):
            q_buf[:, :QK_CAT] = dqc_ref[2 * p]
            q_buf[:, QK_CAT:] = dqc_ref[2 * p + 1]
            g = q_buf[...]
            dq_ref[:, p * pair:(p + 1) * pair] = (
                g * cq_ref[...] - _rope_partner(g, QK_CAT, QK_NOPE) * sq_ref[...]).astype(BF16)
        kr_sum = jnp.zeros((ts, QK_ROPE), F32)
        for h in range(H):
            dkv_ref[:, h * kv_w:h * kv_w + QK_NOPE] = dkc_ref[h, :, 0:QK_NOPE].astype(BF16)
            dkv_ref[:, h * kv_w + QK_NOPE:(h + 1) * kv_w] = dvh_ref[h].astype(BF16)
            kr_sum = kr_sum + dkc_ref[h, :, QK_NOPE:QK_CAT]
        kr_buf[...] = jnp.zeros_like(kr_buf)
        kr_buf[:, 0:QK_ROPE] = kr_sum
        kr = kr_buf[...]
        dkr_ref[...] = (kr * ck_ref[...] - _rope_partner(kr, QK_ROPE, 0) * sk_ref[...]).astype(BF16)

    row = lambda w: pl.BlockSpec((ts, w), lambda i: (i, 0))
    head = lambda w: pl.BlockSpec((H, ts, w), lambda i: (0, i, 0))
    return pl.pallas_call(
        body, name="qk_bwd", grid=(S // ts,),
        in_specs=[head(QK_CAT), head(QK_CAT), head(V_HEAD), row(pair), row(pair), row(COL_BLOCK), row(COL_BLOCK)],
        out_specs=[row(H * QK_CAT), row(H * kv_w), row(COL_BLOCK)],
        out_shape=[jax.ShapeDtypeStruct((S, H * QK_CAT), BF16), jax.ShapeDtypeStruct((S, H * kv_w), BF16),
                   jax.ShapeDtypeStruct((S, COL_BLOCK), BF16)],
        scratch_shapes=[pltpu.VMEM((ts, pair), F32), pltpu.VMEM((ts, COL_BLOCK), F32)],
        compiler_params=_cparams(("parallel",)),
    )(dqc, dkc, dvh, cos_q, sin_q, cos_k, sin_k)


def _rms_bwd(dy, proj, g, blk, L, ts, name):
    S = proj.shape[0]

    def body(dy_ref, a_ref, g_ref, da_ref, dg_ref):
        i = pl.program_id(0)

        @pl.when(i == 0)
        def _():
            dg_ref[...] = jnp.zeros_like(dg_ref)

        a, dy = a_ref[...], dy_ref[...]
        r = lax.rsqrt(jnp.mean(a * a, axis=-1, keepdims=True) + RMS_EPS)
        dyh = dy * g_ref[...]
        da = r * dyh - a * (r * r * r) * jnp.mean(dyh * a, axis=-1, keepdims=True)
        da_ref[...] = da.astype(BF16)
        dg_ref[0:1, :] += jnp.sum(dy * a * r, axis=0, keepdims=True)

    return pl.pallas_call(
        body, name=name, grid=(S // ts,),
        in_specs=[pl.BlockSpec((ts, L), lambda i: (i, 0)), pl.BlockSpec((ts, L), lambda i: (i, blk)),
                  pl.BlockSpec((1, L), lambda i: (0, 0))],
        out_specs=[pl.BlockSpec((ts, L), lambda i: (i, 0)), pl.BlockSpec((8, L), lambda i: (0, 0))],
        out_shape=[jax.ShapeDtypeStruct((S, L), BF16), jax.ShapeDtypeStruct((8, L), F32)],
        compiler_params=_cparams(("arbitrary",)),
    )(dy, proj, g)


def _grad_x(du, dxa, x, mod, ts):
    S, D = x.shape

    def body(du_ref, dxa_ref, x_ref, mod_ref, dx_ref, vec_ref):
        i = pl.program_id(0)

        @pl.when(i == 0)
        def _():
            vec_ref[...] = jnp.zeros_like(vec_ref)

        du = du_ref[...]
        dx_ref[...] = dxa_ref[...] + du * (1.0 + mod_ref[1:2, :])
        vec_ref[0:1, :] += jnp.sum(du, axis=0, keepdims=True)
        vec_ref[1:2, :] += jnp.sum(du * x_ref[...], axis=0, keepdims=True)

    row = pl.BlockSpec((ts, D), lambda i: (i, 0))
    vec = lambda r: pl.BlockSpec((r, D), lambda i: (0, 0))
    return pl.pallas_call(
        body, name="grad_x", grid=(S // ts,),
        in_specs=[row, row, row, vec(6)],
        out_specs=[row, vec(8)],
        out_shape=[jax.ShapeDtypeStruct((S, D), F32), jax.ShapeDtypeStruct((8, D), F32)],
        compiler_params=_cparams(("arbitrary",)),
    )(du, dxa, x, mod)


def _adamw(w, g, m, v, name):
    R, C = w.shape
    tr = _tile(R, max(8, (1 << 19) // C), 8)
    c1 = 1.0 / (1.0 - ADAM_B1 ** ADAM_STEP)
    c2 = 1.0 / (1.0 - ADAM_B2 ** ADAM_STEP)

    def body(w_ref, g_ref, m_ref, v_ref, d_ref, nm_ref, nv_ref):
        g = g_ref[...]
        m = ADAM_B1 * m_ref[...] + (1.0 - ADAM_B1) * g
        v = ADAM_B2 * v_ref[...] + (1.0 - ADAM_B2) * (g * g)
        nm_ref[...] = m
        nv_ref[...] = v
        d_ref[...] = -ADAM_LR * ((m * c1) / (jnp.sqrt(v * c2) + ADAM_EPS) + ADAM_WD * w_ref[...])

    spec = pl.BlockSpec((tr, C), lambda i: (i, 0))
    out = jax.ShapeDtypeStruct((R, C), F32)
    return pl.pallas_call(
        body, name=name, grid=(R // tr,),
        in_specs=[spec] * 4, out_specs=[spec] * 3, out_shape=[out] * 3,
        compiler_params=_cparams(("parallel",)),
    )(w, g, m, v)


def _adamw_reduced(w, own, got, m, v, my_chip, name):
    R, C = w.shape
    tr = _tile(R, max(PACK_ROW_ALIGN, (1 << 18) // C), PACK_ROW_ALIGN)
    c1 = 1.0 / (1.0 - ADAM_B1 ** ADAM_STEP)
    c2 = 1.0 / (1.0 - ADAM_B2 ** ADAM_STEP)

    def body(chip_ref, w_ref, own_ref, g1_ref, g2_ref, g3_ref, m_ref, v_ref, g_ref, d_ref, nm_ref, nv_ref):
        g = own_ref[0].astype(F32) + g1_ref[0].astype(F32) + g2_ref[0].astype(F32) + g3_ref[0].astype(F32)
        m = ADAM_B1 * m_ref[...] + (1.0 - ADAM_B1) * g
        v = ADAM_B2 * v_ref[...] + (1.0 - ADAM_B2) * (g * g)
        g_ref[...] = g
        nm_ref[...] = m
        nv_ref[...] = v
        d_ref[...] = -ADAM_LR * ((m * c1) / (jnp.sqrt(v * c2) + ADAM_EPS) + ADAM_WD * w_ref[...])

    spec = pl.BlockSpec((tr, C), lambda i, chip: (i, 0))
    slot = lambda k: pl.BlockSpec((1, tr, C), lambda i, chip: (chip[0] ^ k, i, 0))
    out = jax.ShapeDtypeStruct((R, C), F32)
    return pl.pallas_call(
        body, name=name,
        grid_spec=pltpu.PrefetchScalarGridSpec(
            num_scalar_prefetch=1, grid=(R // tr,),
            in_specs=[spec, slot(0), slot(1), slot(2), slot(3), spec, spec],
            out_specs=[spec] * 4),
        out_shape=[out] * 4,
        compiler_params=_cparams(("parallel",)),
    )(my_chip, w, own, got, got, got, m, v)


def _my_place():
    return lax.axis_index("x"), lax.axis_index("y"), lax.axis_index("c")


def _peer(k):
    x, y, c = _my_place()
    return (x ^ ((k >> 2) & 1), y ^ ((k >> 1) & 1), c ^ (k & 1))


def _linear(place):
    return 4 * place[0] + 2 * place[1] + place[2]


def _ada_fwd(c_row, wconv_row, w_ada, b_row):
    D, CW = w_ada.shape
    WC = wconv_row.shape[-1]

    def body(c_ref, wc_ref, w_ref, b_ref, mod_ref, cact_ref, wcall_ref, send_buf, sems):
        me = _linear(_my_place())
        c = c_ref[0]
        cact_ref[me] = c * _sigmoid(c)
        wcall_ref[me] = wc_ref[0]

        def gather_copy(buf, k, grp):
            return pltpu.make_async_remote_copy(
                src_ref=buf.at[me], dst_ref=buf.at[me], send_sem=sems.at[0, grp, k], recv_sem=sems.at[1, grp, k],
                device_id=_peer(k), device_id_type=MESH_ID)

        def gather_recv(buf, k, grp):
            src = _linear(_peer(k))
            return pltpu.make_async_remote_copy(
                src_ref=buf.at[src], dst_ref=buf.at[src], send_sem=sems.at[0, grp, k], recv_sem=sems.at[1, grp, k],
                device_id=_peer(k), device_id_type=MESH_ID)

        for k in range(1, N_DEV):
            gather_copy(cact_ref, k, 0).start()
            gather_copy(wcall_ref, k, 1).start()
        for k in range(1, N_DEV):
            gather_recv(cact_ref, k, 0).wait_recv()
            gather_recv(wcall_ref, k, 1).wait_recv()
        for k in range(1, N_DEV):
            gather_copy(cact_ref, k, 0).wait_send()
            gather_copy(wcall_ref, k, 1).wait_send()

        cact = jnp.concatenate([cact_ref[b] for b in range(N_DEV)], axis=0)
        mod_all = jnp.dot(cact.astype(BF16), w_ref[...].astype(BF16), preferred_element_type=F32) + b_ref[0]
        for b in range(N_DEV):
            send_buf[b] = mod_all[b:b + 1, :]
        mod_ref[me] = send_buf[me]

        def scatter_copy(k):
            dst = _linear(_peer(k))
            return pltpu.make_async_remote_copy(
                src_ref=send_buf.at[dst], dst_ref=mod_ref.at[me], send_sem=sems.at[0, 2, k], recv_sem=sems.at[1, 2, k],
                device_id=_peer(k), device_id_type=MESH_ID)

        def scatter_recv(k):
            src = _linear(_peer(k))
            return pltpu.make_async_remote_copy(
                src_ref=send_buf.at[src], dst_ref=mod_ref.at[src], send_sem=sems.at[0, 2, k], recv_sem=sems.at[1, 2, k],
                device_id=_peer(k), device_id_type=MESH_ID)

        for k in range(1, N_DEV):
            scatter_copy(k).start()
        for k in range(1, N_DEV):
            scatter_recv(k).wait_recv()
        for k in range(1, N_DEV):
            scatter_copy(k).wait_send()

    vmem = pl.BlockSpec(memory_space=pltpu.VMEM)
    return pl.pallas_call(
        body, name="ada_fwd",
        in_specs=[vmem] * 4, out_specs=[vmem] * 3,
        out_shape=[jax.ShapeDtypeStruct((N_DEV, 1, CW), F32), jax.ShapeDtypeStruct((N_DEV, 1, D), F32),
                   jax.ShapeDtypeStruct((N_DEV, 1, WC), F32)],
        scratch_shapes=[pltpu.VMEM((N_DEV, 1, CW), F32), pltpu.SemaphoreType.DMA((2, 3, N_DEV))],
        compiler_params=pltpu.CompilerParams(vmem_limit_bytes=VMEM_LIMIT),
    )(c_row, wconv_row, w_ada, b_row)


def _ada_bwd(payload, cact_t, deps=()):
    NCH, _, CW = payload.shape
    D = cact_t.shape[0]

    def body(p_ref, ct_ref, *rest):
        sum_ref, gw_ref, all_ref, sems = rest[-4:]
        me = _linear(_my_place())
        all_ref[me] = p_ref[...]

        def copy(k, slot):
            return pltpu.make_async_remote_copy(
                src_ref=all_ref.at[slot], dst_ref=all_ref.at[slot], send_sem=sems.at[0, k], recv_sem=sems.at[1, k],
                device_id=_peer(k), device_id_type=MESH_ID)

        for k in range(1, N_DEV):
            copy(k, me).start()
        for k in range(1, N_DEV):
            copy(k, _linear(_peer(k))).wait_recv()
        for k in range(1, N_DEV):
            copy(k, me).wait_send()

        total = all_ref[0]
        for b in range(1, N_DEV):
            total = total + all_ref[b]
        sum_ref[...] = total

        ct = ct_ref[...].astype(BF16).astype(F32)
        gw = jnp.zeros((D, CW), F32)
        for b in range(N_DEV):
            dm = all_ref[b, me].astype(BF16).astype(F32)
            gw = gw + ct[:, b:b + 1] * dm
        gw_ref[...] = gw

    vmem = pl.BlockSpec(memory_space=pltpu.VMEM)
    return pl.pallas_call(
        body, name="ada_bwd",
        in_specs=[vmem, vmem] + [ANY_SPEC] * len(deps), out_specs=[vmem, vmem],
        out_shape=[jax.ShapeDtypeStruct((NCH, 1, CW), F32), jax.ShapeDtypeStruct((D, CW), F32)],
        scratch_shapes=[pltpu.VMEM((N_DEV, NCH, 1, CW), F32), pltpu.SemaphoreType.DMA((2, N_DEV))],
        compiler_params=pltpu.CompilerParams(vmem_limit_bytes=VMEM_LIMIT),
    )(payload, cact_t, *deps)


def _exchange_in_chip(parts):
    W = len(parts)

    def body(*refs):
        p_refs, got_refs, (send_sems, recv_sems) = refs[:W], refs[W:2 * W], refs[2 * W:]
        x, y, c = _my_place()
        sibling = (x, y, 1 - c)
        copies = []
        for w in range(W):
            for q in range(4):
                copies.append(pltpu.make_async_remote_copy(
                    src_ref=p_refs[w].at[2 * q + (1 - c)], dst_ref=got_refs[w].at[q],
                    send_sem=send_sems.at[4 * w + q], recv_sem=recv_sems.at[4 * w + q],
                    device_id=sibling, device_id_type=MESH_ID))
        for cp in copies:
            cp.start()
        for cp in copies:
            cp.wait_recv()
        for cp in copies:
            cp.wait_send()

    return pl.pallas_call(
        body, name="grad_exchange_in_chip",
        in_specs=[HBM_SPEC] * W, out_specs=[HBM_SPEC] * W,
        out_shape=[jax.ShapeDtypeStruct((4,) + p.shape[1:], p.dtype) for p in parts],
        scratch_shapes=[pltpu.SemaphoreType.DMA((4 * W,)), pltpu.SemaphoreType.DMA((4 * W,))],
    )(*parts)


def _pair_sum(parts, got, core):
    _, R, C = parts.shape
    tr = _tile(R, max(PACK_ROW_ALIGN, PAIR_SUM_BLOCK // C), PACK_ROW_ALIGN)

    def body(c_ref, p_ref, g_ref, o_ref):
        o_ref[...] = (p_ref[...].astype(F32) + g_ref[...].astype(F32)).astype(o_ref.dtype)

    return pl.pallas_call(
        body, name="grad_pair_sum",
        grid_spec=pltpu.PrefetchScalarGridSpec(
            num_scalar_prefetch=1, grid=(4, R // tr),
            in_specs=[pl.BlockSpec((1, tr, C), lambda q, i, c_ref: (2 * q + c_ref[0], i, 0)),
                      pl.BlockSpec((1, tr, C), lambda q, i, c_ref: (q, i, 0))],
            out_specs=pl.BlockSpec((1, tr, C), lambda q, i, c_ref: (q, i, 0))),
        out_shape=jax.ShapeDtypeStruct((4, R, C), parts.dtype),
        compiler_params=_cparams(("parallel", "parallel")),
    )(core, parts, got)


HBM_SPEC = pl.BlockSpec(memory_space=pltpu.HBM)
SEM_SPEC = pl.BlockSpec(memory_space=pltpu.SEMAPHORE)
ANY_SPEC = pl.BlockSpec(memory_space=pl.ANY)
SPLIT_EFFECT = pltpu.SideEffectType.DATAFLOW_SIDE_EFFECTING


def _landing_zone(shape, dtype):
    return pltpu.with_memory_space_constraint(lax.empty(shape, dtype), pltpu.HBM)


def _split_start(name, arrays, lands, after, copies_of, per_array):
    W = len(arrays)

    def body(*refs):
        x_refs, land_refs = refs[:W], refs[W:2 * W]
        send_sems, recv_sems = refs[2 * W + 1], refs[2 * W + 2]
        token = refs[-1]
        k = 0
        for w in range(W):
            for src, dst, dev in copies_of(w, x_refs[w], land_refs[w]):
                pltpu.make_async_remote_copy(src_ref=src, dst_ref=dst, send_sem=send_sems.at[k], recv_sem=recv_sems.at[k],
                                             device_id=dev, device_id_type=MESH_ID).start()
                k += 1
        token[...] = jnp.zeros_like(token)

    n_copies = per_array * W
    hbm_of = lambda xs: tuple(pltpu.HBM(a.shape, a.dtype) for a in xs)
    out = pl.pallas_call(
        body, name=name,
        out_shape=(pltpu.SemaphoreType.DMA((n_copies,)), pltpu.SemaphoreType.DMA((n_copies,)))
        + hbm_of(arrays) + hbm_of(lands) + (jax.ShapeDtypeStruct((8, LANE), F32),),
        in_specs=(HBM_SPEC,) * (2 * W) + (ANY_SPEC,),
        out_specs=(SEM_SPEC, SEM_SPEC) + (HBM_SPEC,) * (2 * W) + (pl.BlockSpec(memory_space=pltpu.VMEM),),
        input_output_aliases={i: 2 + i for i in range(2 * W)},
        compiler_params=pltpu.CompilerParams(has_side_effects=SPLIT_EFFECT),
    )(*[pltpu.with_memory_space_constraint(a, pltpu.HBM) for a in arrays], *lands, after)
    return out[0], out[1], list(out[2:2 + W]), list(out[2 + W:2 + 2 * W]), out[-1]


def _split_wait(name, state, after, copies_of):
    send_sems, recv_sems, arrays, lands, _ = state
    W = len(arrays)
    after = tuple(after) if isinstance(after, (tuple, list)) else (after,)

    def body(*refs):
        x_refs, land_refs = refs[:W], refs[W:2 * W]
        send_sems, recv_sems = refs[2 * W], refs[2 * W + 1]
        k = 0
        for w in range(W):
            for src, dst, dev in copies_of(w, x_refs[w], land_refs[w]):
                cp = pltpu.make_async_remote_copy(src_ref=src, dst_ref=dst, send_sem=send_sems.at[k],
                                                  recv_sem=recv_sems.at[k], device_id=dev, device_id_type=MESH_ID)
                cp.wait_send()
                cp.wait_recv()
                k += 1

    out = pl.pallas_call(
        body, name=name,
        out_shape=tuple(pltpu.HBM(a.shape, a.dtype) for a in arrays + lands),
        in_specs=(HBM_SPEC,) * (2 * W) + (SEM_SPEC, SEM_SPEC) + (ANY_SPEC,) * len(after),
        out_specs=(HBM_SPEC,) * (2 * W),
        input_output_aliases={i: i for i in range(2 * W)},
        compiler_params=pltpu.CompilerParams(has_side_effects=SPLIT_EFFECT),
    )(*arrays, *lands, send_sems, recv_sems, *after)
    return list(out[:W]), list(out[W:])


def _scatter_copies(w, p_ref, land_ref):
    x, y, c = _my_place()
    my_chip = 2 * x + y
    return [(p_ref.at[2 * (x ^ (k >> 1)) + (y ^ (k & 1))], land_ref.at[my_chip], (x ^ (k >> 1), y ^ (k & 1), c))
            for k in range(1, 4)]


def _gather_copies(w, x_ref, land_ref):
    x, y, c = _my_place()
    me = _linear((x, y, c))
    devs = [(x, y, 1 - c)] + [(x ^ (k >> 1), y ^ (k & 1), c) for k in range(1, 4)]
    return [(x_ref, land_ref.at[me], d) for d in devs]


def _gather_forward(lands, name):
    W = len(lands)

    def body(*refs):
        land_refs, out_refs, (send_sems, recv_sems) = refs[:W], refs[W:2 * W], refs[2 * W:]
        x, y, c = _my_place()
        sibling = (x, y, 1 - c)
        sends, arrivals = [], []
        for w in range(W):
            for k in range(1, 4):
                px, py = x ^ (k >> 1), y ^ (k & 1)
                landed, theirs = _linear((px, py, c)), out_refs[w].at[_linear((px, py, 1 - c))]
                sem = 3 * w + k - 1
                sends.append(pltpu.make_async_remote_copy(
                    src_ref=land_refs[w].at[landed], dst_ref=out_refs[w].at[landed],
                    send_sem=send_sems.at[sem], recv_sem=recv_sems.at[sem], device_id=sibling, device_id_type=MESH_ID))
                arrivals.append(pltpu.make_async_remote_copy(
                    src_ref=theirs, dst_ref=theirs, send_sem=send_sems.at[sem], recv_sem=recv_sems.at[sem],
                    device_id=sibling, device_id_type=MESH_ID))
        for cp in sends:
            cp.start()
        for cp in arrivals:
            cp.wait_recv()
        for cp in sends:
            cp.wait_send()

    return pl.pallas_call(
        body, name=name,
        in_specs=[HBM_SPEC] * W, out_specs=[HBM_SPEC] * W,
        out_shape=[jax.ShapeDtypeStruct(l.shape, l.dtype) for l in lands],
        input_output_aliases={i: i for i in range(W)},
        scratch_shapes=[pltpu.SemaphoreType.DMA((3 * W,)), pltpu.SemaphoreType.DMA((3 * W,))],
    )(*lands)


def _with_own_slot(gathered, shard):
    return lax.dynamic_update_index_in_dim(gathered, shard[None], _linear(_my_place()), axis=0)


def _in_chip_copies(w, p_ref, land_ref):
    x, y, c = _my_place()
    return [(p_ref.at[2 * q + (1 - c)], land_ref.at[q], (x, y, 1 - c)) for q in range(4)]


def _in_chip_start(parts, tag):
    lands = [_landing_zone((4,) + p.shape[1:], p.dtype) for p in parts]
    return _split_start("grad_in_chip_start_" + tag, parts, lands, parts[0], _in_chip_copies, 4)


def _reduce_scatter_begin(parts, tag, in_chip_state=None, after=()):
    if in_chip_state is None:
        got = _exchange_in_chip(parts)
    else:
        parts, got = _split_wait("grad_in_chip_wait_" + tag, in_chip_state, after, _in_chip_copies)
    core = lax.axis_index("c").astype(jnp.int32).reshape(1)
    chip_parts = [_pair_sum(p, g, core) for p, g in zip(parts, got)]
    lands = [_landing_zone(p.shape, p.dtype) for p in chip_parts]
    return _split_start("grad_scatter_start_" + tag, chip_parts, lands, got[0], _scatter_copies, 3)


def _reduce_scatter_end(state, after, tag):
    return _split_wait("grad_scatter_wait_" + tag, state, after, _scatter_copies)


def kernel(x, c, positions, w_ada, b_ada, w_in, g_q_a, w_q_b, g_kv_a, w_kv_b, w_o_a, w_conv, w_o_b, w_o, ln1_g, ln1_b, w_ffn_in, w_ffn_out, ln2_g, ln2_b, loss_target, m_w_ada, m_b_ada, m_w_in, m_g_q_a, m_w_q_b, m_g_kv_a, m_w_kv_b, m_w_o_a, m_w_conv, m_w_o_b, m_w_o, m_ln1_g, m_ln1_b, m_w_ffn_in, m_w_ffn_out, m_ln2_g, m_ln2_b, v_w_ada, v_b_ada, v_w_in, v_g_q_a, v_w_q_b, v_g_kv_a, v_w_kv_b, v_w_o_a, v_w_conv, v_w_o_b, v_w_o, v_ln1_g, v_ln1_b, v_w_ffn_in, v_w_ffn_out, v_ln2_g, v_ln2_b):
    x2, tgt = x[0], loss_target[0]
    S, D = x2.shape
    Lq, Lkv = g_q_a.shape[1], g_kv_a.shape[1]
    H = w_q_b.shape[2] * N_DEV // QK_CAT
    F = w_ffn_out.shape[1] * N_DEV
    assert Lq == Lkv and (Lq + Lkv) % COL_BLOCK == 0 and D % COL_BLOCK == 0
    front = Lq + Lkv + QK_ROPE
    front_pad = _round_up(front, COL_BLOCK)
    kr_blk = (Lq + Lkv) // COL_BLOCK
    blk_b = front_pad // COL_BLOCK
    nblk = D // COL_BLOCK
    blk_c, blk_x, blk_ga, blk_gb = blk_b + nblk, blk_b + 2 * nblk, blk_b + 3 * nblk, blk_b + 4 * nblk
    ts = _tile(S, 256, 8)
    T = _tile(S, min(512, S // 2), CHUNK)
    tb = _tile(F, 2816)
    me = _linear(_my_place())

    landing = lambda shards: [_landing_zone((N_DEV,) + s.shape, BF16) for s in shards]
    first = [w[0].astype(BF16) for w in (w_in, w_q_b, w_kv_b)]
    first_state = _split_start("first_gather_start", first, landing(first), c, _gather_copies, 4)
    later = [w[0].astype(BF16) for w in (w_o_a, w_o_b, w_o, w_ffn_in, w_ffn_out)]

    cw = w_ada.shape[2]
    b_mine = lax.dynamic_slice(b_ada, (0, me * cw), (1, cw)).reshape(1, 1, cw)
    c_row = c.reshape(1, 1, D) + first_state[4][0, 0]
    mod_blocks, cact_all, wconv_all = _ada_fwd(c_row, w_conv[0].reshape(1, 1, -1), w_ada[0], b_mine)
    mod = mod_blocks.reshape(6, D)
    cact_all = cact_all.reshape(N_DEV, D)
    w_conv_full = wconv_all.reshape(N_DEV, CONV_K, -1).transpose(1, 0, 2).reshape(CONV_K, D)
    u = _modulate_in(x2, mod, ts)

    first_shards, first_lands = _split_wait("first_gather_wait", first_state, (u, *later), _gather_copies)
    g_in, wq_s, wkv_s = [_with_own_slot(g, s) for g, s in
                         zip(_gather_forward(first_lands, "first_gather_forward"), first_shards)]
    later_state = _split_start("weight_gather_start", later, landing(later), g_in, _gather_copies, 4)
    later_token = later_state[4]
    w_in_p = _assemble_w_in(g_in, front, front_pad)

    inv_freq = 1.0 / (ROPE_THETA ** (jnp.arange(0, QK_ROPE, 2, dtype=F32) / QK_ROPE))
    ang = positions[0].astype(F32)[:, None] * inv_freq
    cos2 = jnp.concatenate([jnp.cos(ang), jnp.cos(ang)], axis=-1)
    sin2 = jnp.concatenate([jnp.sin(ang), jnp.sin(ang)], axis=-1)
    one, zero = jnp.ones((S, QK_NOPE), F32), jnp.zeros((S, QK_NOPE), F32)
    cos_q, sin_q = jnp.concatenate([one, cos2, one, cos2], axis=-1), jnp.concatenate([zero, sin2, zero, sin2], axis=-1)
    cos_k, sin_k = jnp.tile(cos2, (1, COL_BLOCK // QK_ROPE)), jnp.tile(sin2, (1, COL_BLOCK // QK_ROPE))

    proj = _matmul(u, w_in_p, "nn", F32, "proj", deps=(later_token,))
    qn = _rms_fwd(proj, g_q_a, 0, Lq, ts, "rms_q")
    kvn = _rms_fwd(proj, g_kv_a, 1, Lkv, ts, "rms_kv")
    q = _matmul(qn, wq_s, "nn", F32, "q_up")
    kv = _matmul(kvn, wkv_s, "nn", F32, "kv_up")
    qc, kc, vh = _qk_prep(q, kv, proj, kr_blk, cos_q, sin_q, cos_k, sin_k, H, ts)
    attn, lse = _attn_fwd(qc, kc, vh, T)
    later_shards, later_lands = _split_wait("weight_gather_wait", later_state, lse, _gather_copies)
    later_all = _gather_forward(later_lands, "weight_gather_forward")
    g_oa, g_ob, g_o, w_fi_s, g_fo = [_with_own_slot(g, s) for g, s in zip(later_all, later_shards)]
    w_oa_f, w_ob_f, w_o_f = g_oa.reshape(-1, D), g_ob.reshape(-1, D), g_o.reshape(-1, D)
    w_fo_f = g_fo.reshape(F, D)
    ya = _matmul(attn, w_oa_f, "nn", F32, "attn_out")
    cbc = _conv_fwd(proj, w_conv_full, blk_b, blk_c, blk_x)
    yb = _matmul(cbc, w_ob_f, "nn", F32, "conv_out")
    merged = _merge_fwd(proj, ya, yb, blk_ga, blk_gb, ts)
    mix = _matmul(merged, w_o_f, "nn", F32, "mix_out")
    xhat1, rstd1, u2 = _ln1_fwd(x2, mix, mod, ln1_g, ln1_b, ts)
    hh = _matmul(u2, w_fi_s, "nn", F32, "ffn_in")
    act = _swiglu_fwd(hh, ts, tb)
    ffn = _matmul(act, w_fo_f, "nn", F32, "ffn_out")
    loss_part, dffn, dx1a, vec2 = _ln2_loss(xhat1, ffn, tgt, mod, ln1_g, ln1_b, ln2_g, ln2_b, ts)
    loss = lax.psum(loss_part[0, 0], AXES)

    gw_fo = _matmul(act, dffn, "tn", BF16, "grad_w_ffn_out")
    da = _matmul(dffn, w_fo_f, "nt", F32, "d_act")
    dh = _swiglu_bwd(da, hh, ts, tb)
    gw_fi = _matmul(u2, dh, "tn", BF16, "grad_w_ffn_in", out_shards=True)
    ffn_in_chip = _in_chip_start([gw_fi, gw_fo.reshape(N_DEV, -1, D)], "ffn")
    du2 = _matmul(dh, w_fi_s, "nt", F32, "d_u2", deps=(ffn_in_chip[4],))
    ffn_state = _reduce_scatter_begin(None, "ffn", ffn_in_chip, after=(du2,))
    dxa, dmix, vec1 = _ln1_bwd(du2, dx1a, xhat1, rstd1, mix, mod, ln1_g, ln1_b, ts)
    gw_o = _matmul(merged, dmix, "tn", BF16, "grad_w_o", deps=(ffn_state[4],))
    dmerged = _matmul(dmix, w_o_f, "nt", F32, "d_merged")
    dya, dyb, dga, dgb = _merge_bwd(dmerged, proj, ya, yb, blk_ga, blk_gb, ts)
    gw_ob = _matmul(cbc, dyb, "tn", BF16, "grad_w_o_b")
    dcbc = _matmul(dyb, w_ob_f, "nt", F32, "d_conv")
    dcb, dcc, dcx, dwconv = _conv_bwd(dcbc, proj, w_conv_full, blk_b, blk_c, blk_x)
    gw_oa = _matmul(attn, dya, "tn", BF16, "grad_w_o_a")
    mix_in_chip = _in_chip_start([g.reshape(N_DEV, -1, D) for g in (gw_oa, gw_ob, gw_o)], "mix")
    dattn = _matmul(dya, w_oa_f, "nt", F32, "d_attn", deps=(mix_in_chip[4],))
    dqc, dkc, dvh = _attn_bwd(qc, kc, vh, dattn, attn, lse, T)
    ffn_own, ffn_got = _reduce_scatter_end(ffn_state, dqc, "ffn")
    mix_state = _reduce_scatter_begin(None, "mix", mix_in_chip, after=(dqc,))
    dq, dkv, dkr = _qk_bwd(dqc, dkc, dvh, cos_q, sin_q, cos_k, sin_k, ts)
    gw_qb = _matmul(qn, dq, "tn", BF16, "grad_w_q_b", out_shards=True, deps=(mix_state[4],))
    dqn = _matmul(dq, wq_s, "nt", F32, "d_qn")
    gw_kvb = _matmul(kvn, dkv, "tn", BF16, "grad_w_kv_b", out_shards=True)
    dkvn = _matmul(dkv, wkv_s, "nt", F32, "d_kvn")
    dqa, dgq = _rms_bwd(dqn, proj, g_q_a, 0, Lq, ts, "rms_q_bwd")
    dkva, dgkv = _rms_bwd(dkvn, proj, g_kv_a, 1, Lkv, ts, "rms_kv_bwd")
    dproj = jnp.concatenate([dqa, dkva, dkr, dcb, dcc, dcx, dga, dgb], axis=1)
    gw_in_p = _matmul(u, dproj, "tn", BF16, "grad_w_in")
    mix_own, mix_got = _reduce_scatter_end(mix_state, gw_in_p, "mix")
    in_state = _reduce_scatter_begin([_split_w_in(gw_in_p, front, front_pad), gw_qb, gw_kvb], "in")
    du = _matmul(dproj, w_in_p, "nt", F32, "d_u", deps=(in_state[4],))
    grad_x, vec0 = _grad_x(du, dxa, x2, mod, ts)

    my_chip = (2 * lax.axis_index("x") + lax.axis_index("y")).astype(jnp.int32).reshape(1)
    arrived = {}
    for nm, w, m, v, own, got in (
            ("w_ffn_in", w_ffn_in, m_w_ffn_in, v_w_ffn_in, ffn_own[0], ffn_got[0]),
            ("w_ffn_out", w_ffn_out, m_w_ffn_out, v_w_ffn_out, ffn_own[1], ffn_got[1]),
            ("w_o_a", w_o_a, m_w_o_a, v_w_o_a, mix_own[0], mix_got[0]),
            ("w_o_b", w_o_b, m_w_o_b, v_w_o_b, mix_own[1], mix_got[1]),
            ("w_o", w_o, m_w_o, v_w_o, mix_own[2], mix_got[2])):
        arrived[nm] = [a[None] for a in _adamw_reduced(w[0], own, got, m[0], v[0], my_chip, "adamw_" + nm)]

    dmod = jnp.concatenate([vec0[0], vec0[1], vec1[4], vec1[0], vec1[1], vec2[2]])
    small = jnp.concatenate([dmod, dgq[0], dgkv[0], vec1[2], vec1[3], vec2[0], vec2[1], dwconv[:CONV_K].reshape(-1)])
    n_small = small.shape[0]
    nch = _round_up(n_small, cw) // cw
    payload = jnp.pad(small, (0, nch * cw - n_small)).reshape(nch, 1, cw)
    summed, g_w_ada = _ada_bwd(payload, cact_all.T, deps=[res[1] for res in arrived.values()])
    summed = summed.reshape(-1)
    offs = [0, 6 * D, 6 * D + Lq, 6 * D + Lq + Lkv]
    offs += [offs[-1] + D * k for k in range(1, 5)]
    g_b_ada = summed[offs[0]:offs[1]].reshape(1, -1)
    g_gq = summed[offs[1]:offs[2]].reshape(1, -1)
    g_gkv = summed[offs[2]:offs[3]].reshape(1, -1)
    g_ln1g, g_ln1b, g_ln2g, g_ln2b = [summed[offs[3 + k]:offs[4 + k]].reshape(1, -1) for k in range(4)]
    wc = w_conv.shape[2]
    g_wconv = lax.dynamic_slice(summed[offs[7]:offs[7] + CONV_K * D].reshape(CONV_K, D), (0, me * wc), (CONV_K, wc))

    names = ["w_ada", "b_ada", "w_in", "g_q_a", "w_q_b", "g_kv_a", "w_kv_b", "w_o_a", "w_conv", "w_o_b", "w_o",
             "ln1_g", "ln1_b", "w_ffn_in", "w_ffn_out", "ln2_g", "ln2_b"]
    weights = [w_ada, b_ada, w_in, g_q_a, w_q_b, g_kv_a, w_kv_b, w_o_a, w_conv, w_o_b, w_o, ln1_g, ln1_b,
               w_ffn_in, w_ffn_out, ln2_g, ln2_b]
    moms = [m_w_ada, m_b_ada, m_w_in, m_g_q_a, m_w_q_b, m_g_kv_a, m_w_kv_b, m_w_o_a, m_w_conv, m_w_o_b, m_w_o,
            m_ln1_g, m_ln1_b, m_w_ffn_in, m_w_ffn_out, m_ln2_g, m_ln2_b]
    vels = [v_w_ada, v_b_ada, v_w_in, v_g_q_a, v_w_q_b, v_g_kv_a, v_w_kv_b, v_w_o_a, v_w_conv, v_w_o_b, v_w_o,
            v_ln1_g, v_ln1_b, v_w_ffn_in, v_w_ffn_out, v_ln2_g, v_ln2_b]
    grad_of = {"w_ada": g_w_ada, "b_ada": g_b_ada, "g_q_a": g_gq, "g_kv_a": g_gkv, "w_conv": g_wconv,
               "ln1_g": g_ln1g, "ln1_b": g_ln1b, "ln2_g": g_ln2g, "ln2_b": g_ln2b}
    state_of = dict(zip(names, zip(weights, moms, vels)))
    results = dict(arrived)

    def update(nm, reduced=None):
        w, m, v = state_of[nm]
        shp = w.shape
        w2 = w.reshape(shp[-2], shp[-1]) if w.ndim == 3 else w
        m2, v2 = m.reshape(w2.shape), v.reshape(w2.shape)
        if reduced is None:
            g2 = grad_of[nm].reshape(w2.shape)
            res = (g2,) + tuple(_adamw(w2, g2, m2, v2, "adamw_" + nm))
        else:
            res = _adamw_reduced(w2, reduced[0], reduced[1], m2, v2, my_chip, "adamw_" + nm)
        results[nm] = [a.reshape(shp) for a in res]

    for nm in grad_of:
        update(nm)
    in_own, in_got = _reduce_scatter_end(in_state, [res[1] for res in results.values()], "in")
    for nm, own, got in zip(("w_in", "w_q_b", "w_kv_b"), in_own, in_got):
        update(nm, (own, got))
    outs = [[results[nm][k] for nm in names] for k in range(4)]
    return (loss, grad_x.reshape(x.shape), *outs[0], *outs[1], *outs[2], *outs[3])
```

```python
import functools

import jax
import jax.numpy as jnp
from jax import lax
from jax.experimental import pallas as pl
from jax.experimental.pallas import tpu as pltpu

F32 = jnp.float32
BF16 = jnp.bfloat16
MESH_ID = pl.DeviceIdType.MESH
AXES = ("x", "y", "c")
N_DEV = 8

CHUNK = 64
QK_NOPE = 128
QK_ROPE = 64
V_HEAD = 128
QK_CAT = QK_NOPE + QK_ROPE
ROPE_THETA = 10000.0
ATTN_SCALE = (QK_NOPE + QK_ROPE) ** -0.5
CONV_K = 3
DEEPNORM_ALPHA = 2.0 ** 0.25
LN_EPS = 1e-5
RMS_EPS = 1e-6
NEG_INF = -1e30

ADAM_LR = 0.001
ADAM_B1 = 0.9
ADAM_B2 = 0.999
ADAM_EPS = 1e-08
ADAM_WD = 0.01
ADAM_STEP = 10

LANE = 128
COL_BLOCK = 256
PACK_ROW_ALIGN = 16
PAIR_SUM_BLOCK = 1 << 20
VMEM_LIMIT = 48 * 1024 * 1024


def _round_up(n, m):
    return (n + m - 1) // m * m


def _tile(n, pref, align=LANE):
    best = None
    t = align
    while t <= min(n, pref):
        if n % t == 0:
            best = t
        t += align
    return best if best is not None else n


def _cparams(sem=None):
    return pltpu.CompilerParams(dimension_semantics=sem, vmem_limit_bytes=VMEM_LIMIT)


def _sigmoid(x):
    return 0.5 * jnp.tanh(0.5 * x) + 0.5


def _matmul(a, b, mode, out_dtype, name, tm=1024, tn=1024, tk=2048, deps=(), out_shards=False):
    b_shards = b.ndim == 3
    n = b.shape[2] if b_shards else (b.shape[1] // N_DEV if out_shards else None)
    if mode == "nn":
        (M, K), (K2, N) = a.shape, (b.shape[1], N_DEV * n) if b_shards else b.shape
    elif mode == "nt":
        (M, K), (N, K2) = a.shape, (b.shape[1], N_DEV * n) if b_shards else b.shape
    else:
        (K, M), (K2, N) = a.shape, b.shape
    assert K == K2, (a.shape, b.shape, mode)
    tm = _tile(M, tm)
    tn = n if (mode != "nt" and n is not None) else _tile(N, tn)
    tk = n if (mode == "nt" and b_shards) else _tile(K, tk)
    nk = K // tk
    if mode == "nn":
        a_spec = pl.BlockSpec((tm, tk), lambda i, j, k: (i, k))
        b_spec = (pl.BlockSpec((1, tk, n), lambda i, j, k: (j, k, 0)) if b_shards
                  else pl.BlockSpec((tk, tn), lambda i, j, k: (k, j)))
        dims = (((1,), (0,)), ((), ()))
    elif mode == "nt":
        a_spec = pl.BlockSpec((tm, tk), lambda i, j, k: (i, k))
        b_spec = (pl.BlockSpec((1, tn, n), lambda i, j, k: (k, j, 0)) if b_shards
                  else pl.BlockSpec((tn, tk), lambda i, j, k: (j, k)))
        dims = (((1,), (1,)), ((), ()))
    else:
        a_spec = pl.BlockSpec((tk, tm), lambda i, j, k: (k, i))
        b_spec = pl.BlockSpec((tk, tn), lambda i, j, k: (k, j))
        dims = (((0,), (0,)), ((), ()))
    if out_shards:
        out_spec = pl.BlockSpec((1, tm, n), lambda i, j, k: (j, i, 0))
        out_shape = jax.ShapeDtypeStruct((N_DEV, M, n), out_dtype)
    else:
        out_spec = pl.BlockSpec((tm, tn), lambda i, j, k: (i, j))
        out_shape = jax.ShapeDtypeStruct((M, N), out_dtype)

    def product(a_ref, b_ref):
        b_blk = b_ref[0] if b_shards else b_ref[...]
        return lax.dot_general(a_ref[...].astype(BF16), b_blk.astype(BF16), dims, preferred_element_type=F32)

    def write(o_ref, value):
        if out_shards:
            o_ref[0] = value.astype(o_ref.dtype)
        else:
            o_ref[...] = value.astype(o_ref.dtype)

    def body_whole_k(a_ref, b_ref, *rest):
        write(rest[-1], product(a_ref, b_ref))

    def body_split_k(a_ref, b_ref, *rest):
        o_ref, acc_ref = rest[-2:]
        k = pl.program_id(2)

        @pl.when(k == 0)
        def _():
            acc_ref[...] = jnp.zeros_like(acc_ref)

        acc_ref[...] += product(a_ref, b_ref)

        @pl.when(k == nk - 1)
        def _():
            write(o_ref, acc_ref[...])

    return pl.pallas_call(
        body_whole_k if nk == 1 else body_split_k, name=name, grid=(M // tm, N // tn, nk),
        in_specs=[a_spec, b_spec] + [ANY_SPEC] * len(deps),
        out_specs=out_spec, out_shape=out_shape,
        scratch_shapes=[] if nk == 1 else [pltpu.VMEM((tm, tn), F32)],
        compiler_params=_cparams(("parallel", "parallel", "arbitrary")),
    )(a, b, *deps)


def _assemble_w_in(shards, front, front_pad):
    _, K, n = shards.shape
    gap = front_pad - front
    tk = _tile(K, 256, PACK_ROW_ALIGN)

    def body(g_ref, o_ref):
        if gap:
            o_ref[:, front:front_pad] = jnp.zeros((tk, gap), o_ref.dtype)
        for j in range(N_DEV):
            lo, hi = j * n, (j + 1) * n
            if lo < front < hi:
                o_ref[:, lo:front] = g_ref[j, :, 0:front - lo]
                o_ref[:, front_pad:hi + gap] = g_ref[j, :, front - lo:n]
            else:
                off = 0 if hi <= front else gap
                o_ref[:, lo + off:hi + off] = g_ref[j]

    return pl.pallas_call(
        body, name="assemble_w_in", grid=(K // tk,),
        in_specs=[pl.BlockSpec((N_DEV, tk, n), lambda i: (0, i, 0))],
        out_specs=pl.BlockSpec((tk, N_DEV * n + gap), lambda i: (i, 0)),
        out_shape=jax.ShapeDtypeStruct((K, N_DEV * n + gap), shards.dtype),
        compiler_params=_cparams(("parallel",)),
    )(shards)


def _split_w_in(w, front, front_pad):
    K, NP = w.shape
    gap = front_pad - front
    n = (NP - gap) // N_DEV
    tk = _tile(K, 256, PACK_ROW_ALIGN)

    def body(w_ref, o_ref):
        for j in range(N_DEV):
            lo, hi = j * n, (j + 1) * n
            if lo < front < hi:
                o_ref[j, :, 0:front - lo] = w_ref[:, lo:front]
                o_ref[j, :, front - lo:n] = w_ref[:, front_pad:hi + gap]
            else:
                off = 0 if hi <= front else gap
                o_ref[j] = w_ref[:, lo + off:hi + off]

    return pl.pallas_call(
        body, name="split_grad_w_in", grid=(K // tk,),
        in_specs=[pl.BlockSpec((tk, NP), lambda i: (i, 0))],
        out_specs=pl.BlockSpec((N_DEV, tk, n), lambda i: (0, i, 0)),
        out_shape=jax.ShapeDtypeStruct((N_DEV, K, n), w.dtype),
        compiler_params=_cparams(("parallel",)),
    )(w)


def _modulate_in(x, mod, ts):
    S, D = x.shape

    def body(x_ref, mod_ref, u_ref):
        u_ref[...] = (x_ref[...] * (1.0 + mod_ref[1:2, :]) + mod_ref[0:1, :]).astype(BF16)

    return pl.pallas_call(
        body, name="modulate_in", grid=(S // ts,),
        in_specs=[pl.BlockSpec((ts, D), lambda i: (i, 0)), pl.BlockSpec((6, D), lambda i: (0, 0))],
        out_specs=pl.BlockSpec((ts, D), lambda i: (i, 0)),
        out_shape=jax.ShapeDtypeStruct((S, D), BF16),
        compiler_params=_cparams(("parallel",)),
    )(x, mod)


def _rms_fwd(proj, g, blk, L, ts, name):
    S = proj.shape[0]

    def body(a_ref, g_ref, y_ref):
        a = a_ref[...]
        r = lax.rsqrt(jnp.mean(a * a, axis=-1, keepdims=True) + RMS_EPS)
        y_ref[...] = (a * r * g_ref[...]).astype(BF16)

    return pl.pallas_call(
        body, name=name, grid=(S // ts,),
        in_specs=[pl.BlockSpec((ts, L), lambda i: (i, blk)), pl.BlockSpec((1, L), lambda i: (0, 0))],
        out_specs=pl.BlockSpec((ts, L), lambda i: (i, 0)),
        out_shape=jax.ShapeDtypeStruct((S, L), BF16),
        compiler_params=_cparams(("parallel",)),
    )(proj, g)


def _rope_partner(x, period, start):
    w = x.shape[-1]
    lane = lax.broadcasted_iota(jnp.int32, x.shape, x.ndim - 1) % period
    first = (lane >= start) & (lane < start + QK_ROPE // 2)
    from_right = pltpu.roll(x, w - QK_ROPE // 2, axis=x.ndim - 1)
    from_left = pltpu.roll(x, QK_ROPE // 2, axis=x.ndim - 1)
    return jnp.where(first, -from_right, from_left)


def _qk_prep(q, kv, proj, kr_blk, cos_q, sin_q, cos_k, sin_k, H, ts):
    S = q.shape[0]
    pair = 2 * QK_CAT
    kv_w = QK_NOPE + V_HEAD

    def body(q_ref, kv_ref, kr_ref, cq_ref, sq_ref, ck_ref, sk_ref, qc_ref, kc_ref, vh_ref):
        kr = kr_ref[...]
        kr = kr * ck_ref[...] + _rope_partner(kr, QK_ROPE, 0) * sk_ref[...]
        kr = kr[:, :QK_ROPE].astype(BF16)
        for p in range(H // 2):
            x = q_ref[:, p * pair:(p + 1) * pair]
            x = x * cq_ref[...] + _rope_partner(x, QK_CAT, QK_NOPE) * sq_ref[...]
            qc_ref[2 * p] = x[:, :QK_CAT].astype(BF16)
            qc_ref[2 * p + 1] = x[:, QK_CAT:].astype(BF16)
        for h in range(H):
            kc_ref[h, :, 0:QK_NOPE] = kv_ref[:, h * kv_w:h * kv_w + QK_NOPE].astype(BF16)
            kc_ref[h, :, QK_NOPE:QK_CAT] = kr
            vh_ref[h, :, :] = kv_ref[:, h * kv_w + QK_NOPE:(h + 1) * kv_w].astype(BF16)

    row = lambda w: pl.BlockSpec((ts, w), lambda i: (i, 0))
    return pl.pallas_call(
        body, name="qk_prep", grid=(S // ts,),
        in_specs=[row(H * QK_CAT), row(H * kv_w),
                  pl.BlockSpec((ts, COL_BLOCK), lambda i: (i, kr_blk)),
                  row(pair), row(pair), row(COL_BLOCK), row(COL_BLOCK)],
        out_specs=[pl.BlockSpec((H, ts, QK_CAT), lambda i: (0, i, 0)),
                   pl.BlockSpec((H, ts, QK_CAT), lambda i: (0, i, 0)),
                   pl.BlockSpec((H, ts, V_HEAD), lambda i: (0, i, 0))],
        out_shape=[jax.ShapeDtypeStruct((H, S, QK_CAT), BF16), jax.ShapeDtypeStruct((H, S, QK_CAT), BF16),
                   jax.ShapeDtypeStruct((H, S, V_HEAD), BF16)],
        compiler_params=_cparams(("parallel",)),
    )(q, kv, proj, cos_q, sin_q, cos_k, sin_k)


NT_DIMS = (((1,), (1,)), ((), ()))
TN_DIMS = (((0,), (0,)), ((), ()))


def _diag_mask(T):
    rows = lax.broadcasted_iota(jnp.int32, (T, T), 0) // CHUNK
    cols = lax.broadcasted_iota(jnp.int32, (T, T), 1) // CHUNK
    return cols <= rows


def _attn_fwd(qc, kc, vh, T):
    H, S, _ = qc.shape
    n = S // T

    def body(q_ref, k_ref, v_ref, o_ref, lse_ref):
        q = q_ref[0]

        def block(i):
            L = (i + 1) * T
            s_old = lax.dot_general(q, k_ref[0, 0:i * T, :], NT_DIMS, preferred_element_type=F32) if i else None
            s_diag = lax.dot_general(q, k_ref[0, i * T:L, :], NT_DIMS, preferred_element_type=F32)
            s_diag = jnp.where(_diag_mask(T), s_diag, NEG_INF)
            m = jnp.max(s_diag, axis=-1, keepdims=True)
            if i:
                m = jnp.maximum(m, jnp.max(s_old, axis=-1, keepdims=True))
            p_diag = jnp.exp((s_diag - m) * ATTN_SCALE)
            l = jnp.sum(p_diag, axis=-1, keepdims=True)
            acc = jnp.dot(p_diag.astype(BF16), v_ref[0, i * T:L, :], preferred_element_type=F32)
            if i:
                p_old = jnp.exp((s_old - m) * ATTN_SCALE)
                l = l + jnp.sum(p_old, axis=-1, keepdims=True)
                acc = acc + jnp.dot(p_old.astype(BF16), v_ref[0, 0:i * T, :], preferred_element_type=F32)
            o_ref[...] = acc / l
            lse_ref[0] = m * ATTN_SCALE + jnp.log(l)

        for i in range(n):
            pl.when(pl.program_id(1) == i)(functools.partial(block, i))

    return pl.pallas_call(
        body, name="attn_fwd", grid=(H, n),
        in_specs=[pl.BlockSpec((1, T, QK_CAT), lambda h, i: (h, i, 0)),
                  pl.BlockSpec((1, S, QK_CAT), lambda h, i: (h, 0, 0)),
                  pl.BlockSpec((1, S, V_HEAD), lambda h, i: (h, 0, 0))],
        out_specs=[pl.BlockSpec((T, V_HEAD), lambda h, i: (i, h)),
                   pl.BlockSpec((1, T, 1), lambda h, i: (h, i, 0))],
        out_shape=[jax.ShapeDtypeStruct((S, H * V_HEAD), F32), jax.ShapeDtypeStruct((H, S, 1), F32)],
        compiler_params=_cparams(("parallel", "arbitrary")),
    )(qc, kc, vh)


def _shift_rows(z, k):
    if k == 0:
        return z
    n = z.shape[0]
    row = lax.broadcasted_iota(jnp.int32, z.shape, 0)
    if k > 0:
        return jnp.where(row >= k, pltpu.roll(z, k, axis=0), 0.0)
    return jnp.where(row < n + k, pltpu.roll(z, n + k, axis=0), 0.0)


def _conv_fwd(proj, w_conv, blk_b, blk_c, blk_x):
    S = proj.shape[0]
    D = w_conv.shape[1]
    nb = D // COL_BLOCK

    def body(cb_ref, cc_ref, cx_ref, w_ref, o_ref):
        z = cc_ref[...] * cx_ref[...]
        conv = w_ref[2:3, :] * z + w_ref[1:2, :] * _shift_rows(z, 1) + w_ref[0:1, :] * _shift_rows(z, 2)
        o_ref[...] = (cb_ref[...] * conv).astype(BF16)

    col = lambda off: pl.BlockSpec((S, COL_BLOCK), lambda j: (0, off + j))
    return pl.pallas_call(
        body, name="conv_fwd", grid=(nb,),
        in_specs=[col(blk_b), col(blk_c), col(blk_x), pl.BlockSpec((CONV_K, COL_BLOCK), lambda j: (0, j))],
        out_specs=pl.BlockSpec((S, COL_BLOCK), lambda j: (0, j)),
        out_shape=jax.ShapeDtypeStruct((S, D), BF16),
        compiler_params=_cparams(("parallel",)),
    )(proj, proj, proj, w_conv)


def _merge_fwd(proj, ya, yb, blk_ga, blk_gb, ts):
    S, D = ya.shape
    nb = D // COL_BLOCK

    def body(ga_ref, gb_ref, ya_ref, yb_ref, o_ref):
        o_ref[...] = (_sigmoid(ga_ref[...]) * ya_ref[...] + _sigmoid(gb_ref[...]) * yb_ref[...]).astype(BF16)

    row = pl.BlockSpec((ts, D), lambda i: (i, 0))
    seg = lambda blk: pl.BlockSpec((pl.Element(ts), pl.Element(D)), lambda i: (i * ts, blk * COL_BLOCK))
    return pl.pallas_call(
        body, name="merge_fwd", grid=(S // ts,),
        in_specs=[seg(blk_ga), seg(blk_gb), row, row],
        out_specs=row,
        out_shape=jax.ShapeDtypeStruct((S, D), BF16),
        compiler_params=_cparams(("parallel",)),
    )(proj, proj, ya, yb)


def _ln1_fwd(x, mix, mod, g, b, ts):
    S, D = x.shape

    def body(x_ref, mix_ref, mod_ref, g_ref, b_ref, xhat_ref, rstd_ref, u2_ref):
        r = DEEPNORM_ALPHA * x_ref[...] + mod_ref[2:3, :] * mix_ref[...]
        mu = jnp.mean(r, axis=-1, keepdims=True)
        d = r - mu
        rstd = lax.rsqrt(jnp.mean(d * d, axis=-1, keepdims=True) + LN_EPS)
        xhat = d * rstd
        xhat_ref[...] = xhat
        rstd_ref[...] = rstd
        x1 = xhat * g_ref[...] + b_ref[...]
        u2_ref[...] = (x1 * (1.0 + mod_ref[4:5, :]) + mod_ref[3:4, :]).astype(BF16)

    row = pl.BlockSpec((ts, D), lambda i: (i, 0))
    vec = lambda r: pl.BlockSpec((r, D), lambda i: (0, 0))
    return pl.pallas_call(
        body, name="ln1_fwd", grid=(S // ts,),
        in_specs=[row, row, vec(6), vec(1), vec(1)],
        out_specs=[row, pl.BlockSpec((ts, 1), lambda i: (i, 0)), row],
        out_shape=[jax.ShapeDtypeStruct((S, D), F32), jax.ShapeDtypeStruct((S, 1), F32),
                   jax.ShapeDtypeStruct((S, D), BF16)],
        compiler_params=_cparams(("parallel",)),
    )(x, mix, mod, g, b)


def _swiglu_fwd(h, ts, tb):
    S, F2 = h.shape
    F = F2 // 2
    nb = F // tb

    def body(hg_ref, hu_ref, a_ref):
        hg = hg_ref[...]
        a_ref[...] = (hg * _sigmoid(hg) * hu_ref[...]).astype(BF16)

    return pl.pallas_call(
        body, name="swiglu_fwd", grid=(S // ts, nb),
        in_specs=[pl.BlockSpec((ts, tb), lambda i, j: (i, j)), pl.BlockSpec((ts, tb), lambda i, j: (i, j + nb))],
        out_specs=pl.BlockSpec((ts, tb), lambda i, j: (i, j)),
        out_shape=jax.ShapeDtypeStruct((S, F), BF16),
        compiler_params=_cparams(("parallel", "parallel")),
    )(h, h)


def _ln2_loss(xhat1, ffn, tgt, mod, g1, b1, g2, b2, ts):
    S, D = xhat1.shape

    def body(xh_ref, ffn_ref, t_ref, mod_ref, g1_ref, b1_ref, g2_ref, b2_ref, loss_ref, dffn_ref, dx1_ref, vec_ref):
        i = pl.program_id(0)

        @pl.when(i == 0)
        def _():
            loss_ref[...] = jnp.zeros_like(loss_ref)
            vec_ref[...] = jnp.zeros_like(vec_ref)

        x1 = xh_ref[...] * g1_ref[...] + b1_ref[...]
        ffn = ffn_ref[...]
        r = DEEPNORM_ALPHA * x1 + mod_ref[5:6, :] * ffn
        mu = jnp.mean(r, axis=-1, keepdims=True)
        d = r - mu
        rstd = lax.rsqrt(jnp.mean(d * d, axis=-1, keepdims=True) + LN_EPS)
        xhat = d * rstd
        e = xhat * g2_ref[...] + b2_ref[...] - t_ref[...]
        loss_ref[...] += 0.5 * jnp.sum(jnp.mean(e * e, axis=-1, keepdims=True))
        dy = e * (1.0 / D)
        dxhat = dy * g2_ref[...]
        dr = rstd * (dxhat - jnp.mean(dxhat, axis=-1, keepdims=True)
                     - xhat * jnp.mean(dxhat * xhat, axis=-1, keepdims=True))
        dffn_ref[...] = (dr * mod_ref[5:6, :]).astype(BF16)
        dx1_ref[...] = DEEPNORM_ALPHA * dr
        vec_ref[0:1, :] += jnp.sum(dy * xhat, axis=0, keepdims=True)
        vec_ref[1:2, :] += jnp.sum(dy, axis=0, keepdims=True)
        vec_ref[2:3, :] += jnp.sum(dr * ffn, axis=0, keepdims=True)

    row = pl.BlockSpec((ts, D), lambda i: (i, 0))
    vec = lambda r: pl.BlockSpec((r, D), lambda i: (0, 0))
    return pl.pallas_call(
        body, name="ln2_loss", grid=(S // ts,),
        in_specs=[row, row, row, vec(6), vec(1), vec(1), vec(1), vec(1)],
        out_specs=[pl.BlockSpec((1, LANE), lambda i: (0, 0)), row, row, vec(8)],
        out_shape=[jax.ShapeDtypeStruct((1, LANE), F32), jax.ShapeDtypeStruct((S, D), BF16),
                   jax.ShapeDtypeStruct((S, D), F32), jax.ShapeDtypeStruct((8, D), F32)],
        compiler_params=_cparams(("arbitrary",)),
    )(xhat1, ffn, tgt, mod, g1, b1, g2, b2)


def _swiglu_bwd(da, h, ts, tb):
    S, F2 = h.shape
    nb = (F2 // 2) // tb

    def body(da_ref, hg_ref, hu_ref, dh_ref):
        hg, da = hg_ref[...], da_ref[...]
        sg = _sigmoid(hg)

        @pl.when(pl.program_id(2) == 0)
        def _():
            dh_ref[...] = (da * hu_ref[...] * (sg * (1.0 + hg * (1.0 - sg)))).astype(BF16)

        @pl.when(pl.program_id(2) == 1)
        def _():
            dh_ref[...] = (da * hg * sg).astype(BF16)

    lo = pl.BlockSpec((ts, tb), lambda i, j, k: (i, j))
    hi = pl.BlockSpec((ts, tb), lambda i, j, k: (i, j + nb))
    return pl.pallas_call(
        body, name="swiglu_bwd", grid=(S // ts, nb, 2),
        in_specs=[lo, lo, hi],
        out_specs=pl.BlockSpec((ts, tb), lambda i, j, k: (i, j + nb * k)),
        out_shape=jax.ShapeDtypeStruct((S, F2), BF16),
        compiler_params=_cparams(("parallel", "parallel", "arbitrary")),
    )(da, h, h)


def _ln1_bwd(du2, dx1a, xhat1, rstd1, mix, mod, g1, b1, ts):
    S, D = xhat1.shape

    def body(du2_ref, dx1a_ref, xh_ref, rstd_ref, mix_ref, mod_ref, g_ref, b_ref, dxa_ref, dmix_ref, vec_ref):
        i = pl.program_id(0)

        @pl.when(i == 0)
        def _():
            vec_ref[...] = jnp.zeros_like(vec_ref)

        xhat, du2, mix = xh_ref[...], du2_ref[...], mix_ref[...]
        x1 = xhat * g_ref[...] + b_ref[...]
        dx1 = dx1a_ref[...] + du2 * (1.0 + mod_ref[4:5, :])
        dxhat = dx1 * g_ref[...]
        dr = rstd_ref[...] * (dxhat - jnp.mean(dxhat, axis=-1, keepdims=True)
                              - xhat * jnp.mean(dxhat * xhat, axis=-1, keepdims=True))
        dxa_ref[...] = DEEPNORM_ALPHA * dr
        dmix_ref[...] = (dr * mod_ref[2:3, :]).astype(BF16)
        vec_ref[0:1, :] += jnp.sum(du2, axis=0, keepdims=True)
        vec_ref[1:2, :] += jnp.sum(du2 * x1, axis=0, keepdims=True)
        vec_ref[2:3, :] += jnp.sum(dx1 * xhat, axis=0, keepdims=True)
        vec_ref[3:4, :] += jnp.sum(dx1, axis=0, keepdims=True)
        vec_ref[4:5, :] += jnp.sum(dr * mix, axis=0, keepdims=True)

    row = pl.BlockSpec((ts, D), lambda i: (i, 0))
    vec = lambda r: pl.BlockSpec((r, D), lambda i: (0, 0))
    return pl.pallas_call(
        body, name="ln1_bwd", grid=(S // ts,),
        in_specs=[row, row, row, pl.BlockSpec((ts, 1), lambda i: (i, 0)), row, vec(6), vec(1), vec(1)],
        out_specs=[row, row, vec(8)],
        out_shape=[jax.ShapeDtypeStruct((S, D), F32), jax.ShapeDtypeStruct((S, D), BF16),
                   jax.ShapeDtypeStruct((8, D), F32)],
        compiler_params=_cparams(("arbitrary",)),
    )(du2, dx1a, xhat1, rstd1, mix, mod, g1, b1)


def _merge_bwd(dmerged, proj, ya, yb, blk_ga, blk_gb, ts):
    S, D = ya.shape
    nb = D // COL_BLOCK

    def body(dm_ref, ga_ref, gb_ref, ya_ref, yb_ref, dya_ref, dyb_ref, dga_ref, dgb_ref):
        dm = dm_ref[...]
        sa, sb = _sigmoid(ga_ref[...]), _sigmoid(gb_ref[...])
        dya_ref[...] = (dm * sa).astype(BF16)
        dyb_ref[...] = (dm * sb).astype(BF16)
        dga_ref[...] = (dm * ya_ref[...] * sa * (1.0 - sa)).astype(BF16)
        dgb_ref[...] = (dm * yb_ref[...] * sb * (1.0 - sb)).astype(BF16)

    row = pl.BlockSpec((ts, D), lambda i: (i, 0))
    seg = lambda blk: pl.BlockSpec((pl.Element(ts), pl.Element(D)), lambda i: (i * ts, blk * COL_BLOCK))
    out = jax.ShapeDtypeStruct((S, D), BF16)
    return pl.pallas_call(
        body, name="merge_bwd", grid=(S // ts,),
        in_specs=[row, seg(blk_ga), seg(blk_gb), row, row],
        out_specs=[row] * 4,
        out_shape=[out] * 4,
        compiler_params=_cparams(("parallel",)),
    )(dmerged, proj, proj, ya, yb)


def _conv_bwd(dcbc, proj, w_conv, blk_b, blk_c, blk_x):
    S = proj.shape[0]
    D = w_conv.shape[1]
    nb = D // COL_BLOCK

    def body(d_ref, cb_ref, cc_ref, cx_ref, w_ref, dcb_ref, dcc_ref, dcx_ref, dw_ref):
        d, cc, cx = d_ref[...], cc_ref[...], cx_ref[...]
        z = cc * cx
        z1, z2 = _shift_rows(z, 1), _shift_rows(z, 2)
        conv = w_ref[2:3, :] * z + w_ref[1:2, :] * z1 + w_ref[0:1, :] * z2
        dcb_ref[...] = (d * conv).astype(BF16)
        dconv = d * cb_ref[...]
        dz = w_ref[2:3, :] * dconv + w_ref[1:2, :] * _shift_rows(dconv, -1) + w_ref[0:1, :] * _shift_rows(dconv, -2)
        dcc_ref[...] = (dz * cx).astype(BF16)
        dcx_ref[...] = (dz * cc).astype(BF16)
        dw_ref[...] = jnp.zeros_like(dw_ref)
        dw_ref[0:1, :] = jnp.sum(dconv * z2, axis=0, keepdims=True)
        dw_ref[1:2, :] = jnp.sum(dconv * z1, axis=0, keepdims=True)
        dw_ref[2:3, :] = jnp.sum(dconv * z, axis=0, keepdims=True)

    col = lambda off: pl.BlockSpec((S, COL_BLOCK), lambda j: (0, off + j))
    out = jax.ShapeDtypeStruct((S, D), BF16)
    return pl.pallas_call(
        body, name="conv_bwd", grid=(nb,),
        in_specs=[col(0), col(blk_b), col(blk_c), col(blk_x), pl.BlockSpec((CONV_K, COL_BLOCK), lambda j: (0, j))],
        out_specs=[col(0), col(0), col(0), pl.BlockSpec((8, COL_BLOCK), lambda j: (0, j))],
        out_shape=[out, out, out, jax.ShapeDtypeStruct((8, D), F32)],
        compiler_params=_cparams(("parallel",)),
    )(dcbc, proj, proj, proj, w_conv)


def _attn_bwd(qc, kc, vh, do, o, lse, T):
    H, S, _ = qc.shape
    n = S // T

    def body(q_ref, k_ref, v_ref, do_ref, o_ref, lse_ref, dq_ref, dk_ref, dv_ref, d_ref, dk_acc, dv_acc):
        j = pl.program_id(1)

        @pl.when(j == 0)
        def _():
            dq_ref[...] = jnp.zeros_like(dq_ref)
            d_ref[...] = jnp.sum(do_ref[...] * o_ref[...], axis=-1, keepdims=True)

        dk_acc[...] = jnp.zeros_like(dk_acc)
        dv_acc[...] = jnp.zeros_like(dv_acc)
        k, v = k_ref[0], v_ref[0]

        def step(i, masked):
            rows = pl.ds(pl.multiple_of(i * T, T), T)
            q = q_ref[0, rows, :]
            do = do_ref[rows, :].astype(BF16)
            s = lax.dot_general(q, k, NT_DIMS, preferred_element_type=F32) * ATTN_SCALE
            if masked:
                s = jnp.where(_diag_mask(T), s, NEG_INF)
            p = jnp.exp(s - lse_ref[0, rows, :])
            dv_acc[...] += lax.dot_general(p.astype(BF16), do, TN_DIMS, preferred_element_type=F32)
            dp = lax.dot_general(do, v, NT_DIMS, preferred_element_type=F32)
            ds = (p * (dp - d_ref[rows, :]) * ATTN_SCALE).astype(BF16)
            dk_acc[...] += lax.dot_general(ds, q, TN_DIMS, preferred_element_type=F32)
            dq_ref[0, rows, :] += jnp.dot(ds, k, preferred_element_type=F32)

        def above(i, carry):
            step(i, False)
            return carry

        step(j, True)
        lax.fori_loop(j + 1, n, above, 0)
        dk_ref[0] = dk_acc[...]
        dv_ref[0] = dv_acc[...]

    head = lambda w: pl.BlockSpec((1, S, w), lambda h, j: (h, 0, 0))
    blk = lambda w: pl.BlockSpec((1, T, w), lambda h, j: (h, j, 0))
    ospec = pl.BlockSpec((S, V_HEAD), lambda h, j: (0, h))
    return pl.pallas_call(
        body, name="attn_bwd", grid=(H, n),
        in_specs=[head(QK_CAT), blk(QK_CAT), blk(V_HEAD), ospec, ospec, head(1)],
        out_specs=[head(QK_CAT), blk(QK_CAT), blk(V_HEAD)],
        out_shape=[jax.ShapeDtypeStruct((H, S, QK_CAT), F32), jax.ShapeDtypeStruct((H, S, QK_CAT), F32),
                   jax.ShapeDtypeStruct((H, S, V_HEAD), F32)],
        scratch_shapes=[pltpu.VMEM((S, 1), F32), pltpu.VMEM((T, QK_CAT), F32), pltpu.VMEM((T, V_HEAD), F32)],
        compiler_params=_cparams(("parallel", "arbitrary")),
    )(qc, kc, vh, do, o, lse)


def _qk_bwd(dqc, dkc, dvh, cos_q, sin_q, cos_k, sin_k, ts):
    H, S, _ = dqc.shape
    pair = 2 * QK_CAT
    kv_w = QK_NOPE + V_HEAD

    def body(dqc_ref, dkc_ref, dvh_ref, cq_ref, sq_ref, ck_ref, sk_ref, dq_ref, dkv_ref, dkr_ref, q_buf, kr_buf):
        for p in range(H // 2):
            q_buf[:, :QK_CAT] = dqc_ref[2 * p]
            q_buf[:, QK_CAT:] = dqc_ref[2 * p + 1]
            g = q_buf[...]
            dq_ref[:, p * pair:(p + 1) * pair] = (
                g * cq_ref[...] - _rope_partner(g, QK_CAT, QK_NOPE) * sq_ref[...]).astype(BF16)
        kr_sum = jnp.zeros((ts, QK_ROPE), F32)
        for h in range(H):
            dkv_ref[:, h * kv_w:h * kv_w + QK_NOPE] = dkc_ref[h, :, 0:QK_NOPE].astype(BF16)
            dkv_ref[:, h * kv_w + QK_NOPE:(h + 1) * kv_w] = dvh_ref[h].astype(BF16)
            kr_sum = kr_sum + dkc_ref[h, :, QK_NOPE:QK_CAT]
        kr_buf[...] = jnp.zeros_like(kr_buf)
        kr_buf[:, 0:QK_ROPE] = kr_sum
        kr = kr_buf[...]
        dkr_ref[...] = (kr * ck_ref[...] - _rope_partner(kr, QK_ROPE, 0) * sk_ref[...]).astype(BF16)

    row = lambda w: pl.BlockSpec((ts, w), lambda i: (i, 0))
    head = lambda w: pl.BlockSpec((H, ts, w), lambda i: (0, i, 0))
    return pl.pallas_call(
        body, name="qk_bwd", grid=(S // ts,),
        in_specs=[head(QK_CAT), head(QK_CAT), head(V_HEAD), row(pair), row(pair), row(COL_BLOCK), row(COL_BLOCK)],
        out_specs=[row(H * QK_CAT), row(H * kv_w), row(COL_BLOCK)],
        out_shape=[jax.ShapeDtypeStruct((S, H * QK_CAT), BF16), jax.ShapeDtypeStruct((S, H * kv_w), BF16),
                   jax.ShapeDtypeStruct((S, COL_BLOCK), BF16)],
        scratch_shapes=[pltpu.VMEM((ts, pair), F32), pltpu.VMEM((ts, COL_BLOCK), F32)],
        compiler_params=_cparams(("parallel",)),
    )(dqc, dkc, dvh, cos_q, sin_q, cos_k, sin_k)


def _rms_bwd(dy, proj, g, blk, L, ts, name):
    S = proj.shape[0]

    def body(dy_ref, a_ref, g_ref, da_ref, dg_ref):
        i = pl.program_id(0)

        @pl.when(i == 0)
        def _():
            dg_ref[...] = jnp.zeros_like(dg_ref)

        a, dy = a_ref[...], dy_ref[...]
        r = lax.rsqrt(jnp.mean(a * a, axis=-1, keepdims=True) + RMS_EPS)
        dyh = dy * g_ref[...]
        da = r * dyh - a * (r * r * r) * jnp.mean(dyh * a, axis=-1, keepdims=True)
        da_ref[...] = da.astype(BF16)
        dg_ref[0:1, :] += jnp.sum(dy * a * r, axis=0, keepdims=True)

    return pl.pallas_call(
        body, name=name, grid=(S // ts,),
        in_specs=[pl.BlockSpec((ts, L), lambda i: (i, 0)), pl.BlockSpec((ts, L), lambda i: (i, blk)),
                  pl.BlockSpec((1, L), lambda i: (0, 0))],
        out_specs=[pl.BlockSpec((ts, L), lambda i: (i, 0)), pl.BlockSpec((8, L), lambda i: (0, 0))],
        out_shape=[jax.ShapeDtypeStruct((S, L), BF16), jax.ShapeDtypeStruct((8, L), F32)],
        compiler_params=_cparams(("arbitrary",)),
    )(dy, proj, g)


def _grad_x(du, dxa, x, mod, ts):
    S, D = x.shape

    def body(du_ref, dxa_ref, x_ref, mod_ref, dx_ref, vec_ref):
        i = pl.program_id(0)

        @pl.when(i == 0)
        def _():
            vec_ref[...] = jnp.zeros_like(vec_ref)

        du = du_ref[...]
        dx_ref[...] = dxa_ref[...] + du * (1.0 + mod_ref[1:2, :])
        vec_ref[0:1, :] += jnp.sum(du, axis=0, keepdims=True)
        vec_ref[1:2, :] += jnp.sum(du * x_ref[...], axis=0, keepdims=True)

    row = pl.BlockSpec((ts, D), lambda i: (i, 0))
    vec = lambda r: pl.BlockSpec((r, D), lambda i: (0, 0))
    return pl.pallas_call(
        body, name="grad_x", grid=(S // ts,),
        in_specs=[row, row, row, vec(6)],
        out_specs=[row, vec(8)],
        out_shape=[jax.ShapeDtypeStruct((S, D), F32), jax.ShapeDtypeStruct((8, D), F32)],
        compiler_params=_cparams(("arbitrary",)),
    )(du, dxa, x, mod)


def _adamw(w, g, m, v, name):
    R, C = w.shape
    tr = _tile(R, max(8, (1 << 19) // C), 8)
    c1 = 1.0 / (1.0 - ADAM_B1 ** ADAM_STEP)
    c2 = 1.0 / (1.0 - ADAM_B2 ** ADAM_STEP)

    def body(w_ref, g_ref, m_ref, v_ref, d_ref, nm_ref, nv_ref):
        g = g_ref[...]
        m = ADAM_B1 * m_ref[...] + (1.0 - ADAM_B1) * g
        v = ADAM_B2 * v_ref[...] + (1.0 - ADAM_B2) * (g * g)
        nm_ref[...] = m
        nv_ref[...] = v
        d_ref[...] = -ADAM_LR * ((m * c1) / (jnp.sqrt(v * c2) + ADAM_EPS) + ADAM_WD * w_ref[...])

    spec = pl.BlockSpec((tr, C), lambda i: (i, 0))
    out = jax.ShapeDtypeStruct((R, C), F32)
    return pl.pallas_call(
        body, name=name, grid=(R // tr,),
        in_specs=[spec] * 4, out_specs=[spec] * 3, out_shape=[out] * 3,
        compiler_params=_cparams(("parallel",)),
    )(w, g, m, v)


def _adamw_reduced(w, own, got, m, v, my_chip, name):
    R, C = w.shape
    tr = _tile(R, max(PACK_ROW_ALIGN, (1 << 18) // C), PACK_ROW_ALIGN)
    c1 = 1.0 / (1.0 - ADAM_B1 ** ADAM_STEP)
    c2 = 1.0 / (1.0 - ADAM_B2 ** ADAM_STEP)

    def body(chip_ref, w_ref, own_ref, g1_ref, g2_ref, g3_ref, m_ref, v_ref, g_ref, d_ref, nm_ref, nv_ref):
        g = own_ref[0].astype(F32) + g1_ref[0].astype(F32) + g2_ref[0].astype(F32) + g3_ref[0].astype(F32)
        m = ADAM_B1 * m_ref[...] + (1.0 - ADAM_B1) * g
        v = ADAM_B2 * v_ref[...] + (1.0 - ADAM_B2) * (g * g)
        g_ref[...] = g
        nm_ref[...] = m
        nv_ref[...] = v
        d_ref[...] = -ADAM_LR * ((m * c1) / (jnp.sqrt(v * c2) + ADAM_EPS) + ADAM_WD * w_ref[...])

    spec = pl.BlockSpec((tr, C), lambda i, chip: (i, 0))
    slot = lambda k: pl.BlockSpec((1, tr, C), lambda i, chip: (chip[0] ^ k, i, 0))
    out = jax.ShapeDtypeStruct((R, C), F32)
    return pl.pallas_call(
        body, name=name,
        grid_spec=pltpu.PrefetchScalarGridSpec(
            num_scalar_prefetch=1, grid=(R // tr,),
            in_specs=[spec, slot(0), slot(1), slot(2), slot(3), spec, spec],
            out_specs=[spec] * 4),
        out_shape=[out] * 4,
        compiler_params=_cparams(("parallel",)),
    )(my_chip, w, own, got, got, got, m, v)


def _my_place():
    return lax.axis_index("x"), lax.axis_index("y"), lax.axis_index("c")


def _peer(k):
    x, y, c = _my_place()
    return (x ^ ((k >> 2) & 1), y ^ ((k >> 1) & 1), c ^ (k & 1))


def _linear(place):
    return 4 * place[0] + 2 * place[1] + place[2]


def _ada_fwd(c_row, wconv_row, w_ada, b_row):
    D, CW = w_ada.shape
    WC = wconv_row.shape[-1]

    def body(c_ref, wc_ref, w_ref, b_ref, mod_ref, cact_ref, wcall_ref, send_buf, sems):
        me = _linear(_my_place())
        c = c_ref[0]
        cact_ref[me] = c * _sigmoid(c)
        wcall_ref[me] = wc_ref[0]

        def gather_copy(buf, k, grp):
            return pltpu.make_async_remote_copy(
                src_ref=buf.at[me], dst_ref=buf.at[me], send_sem=sems.at[0, grp, k], recv_sem=sems.at[1, grp, k],
                device_id=_peer(k), device_id_type=MESH_ID)

        def gather_recv(buf, k, grp):
            src = _linear(_peer(k))
            return pltpu.make_async_remote_copy(
                src_ref=buf.at[src], dst_ref=buf.at[src], send_sem=sems.at[0, grp, k], recv_sem=sems.at[1, grp, k],
                device_id=_peer(k), device_id_type=MESH_ID)

        for k in range(1, N_DEV):
            gather_copy(cact_ref, k, 0).start()
            gather_copy(wcall_ref, k, 1).start()
        for k in range(1, N_DEV):
            gather_recv(cact_ref, k, 0).wait_recv()
            gather_recv(wcall_ref, k, 1).wait_recv()
        for k in range(1, N_DEV):
            gather_copy(cact_ref, k, 0).wait_send()
            gather_copy(wcall_ref, k, 1).wait_send()

        cact = jnp.concatenate([cact_ref[b] for b in range(N_DEV)], axis=0)
        mod_all = jnp.dot(cact.astype(BF16), w_ref[...].astype(BF16), preferred_element_type=F32) + b_ref[0]
        for b in range(N_DEV):
            send_buf[b] = mod_all[b:b + 1, :]
        mod_ref[me] = send_buf[me]

        def scatter_copy(k):
            dst = _linear(_peer(k))
            return pltpu.make_async_remote_copy(
                src_ref=send_buf.at[dst], dst_ref=mod_ref.at[me], send_sem=sems.at[0, 2, k], recv_sem=sems.at[1, 2, k],
                device_id=_peer(k), device_id_type=MESH_ID)

        def scatter_recv(k):
            src = _linear(_peer(k))
            return pltpu.make_async_remote_copy(
                src_ref=send_buf.at[src], dst_ref=mod_ref.at[src], send_sem=sems.at[0, 2, k], recv_sem=sems.at[1, 2, k],
                device_id=_peer(k), device_id_type=MESH_ID)

        for k in range(1, N_DEV):
            scatter_copy(k).start()
        for k in range(1, N_DEV):
            scatter_recv(k).wait_recv()
        for k in range(1, N_DEV):
            scatter_copy(k).wait_send()

    vmem = pl.BlockSpec(memory_space=pltpu.VMEM)
    return pl.pallas_call(
        body, name="ada_fwd",
        in_specs=[vmem] * 4, out_specs=[vmem] * 3,
        out_shape=[jax.ShapeDtypeStruct((N_DEV, 1, CW), F32), jax.ShapeDtypeStruct((N_DEV, 1, D), F32),
                   jax.ShapeDtypeStruct((N_DEV, 1, WC), F32)],
        scratch_shapes=[pltpu.VMEM((N_DEV, 1, CW), F32), pltpu.SemaphoreType.DMA((2, 3, N_DEV))],
        compiler_params=pltpu.CompilerParams(vmem_limit_bytes=VMEM_LIMIT),
    )(c_row, wconv_row, w_ada, b_row)


def _ada_bwd(payload, cact_t, deps=()):
    NCH, _, CW = payload.shape
    D = cact_t.shape[0]

    def body(p_ref, ct_ref, *rest):
        sum_ref, gw_ref, all_ref, sems = rest[-4:]
        me = _linear(_my_place())
        all_ref[me] = p_ref[...]

        def copy(k, slot):
            return pltpu.make_async_remote_copy(
                src_ref=all_ref.at[slot], dst_ref=all_ref.at[slot], send_sem=sems.at[0, k], recv_sem=sems.at[1, k],
                device_id=_peer(k), device_id_type=MESH_ID)

        for k in range(1, N_DEV):
            copy(k, me).start()
        for k in range(1, N_DEV):
            copy(k, _linear(_peer(k))).wait_recv()
        for k in range(1, N_DEV):
            copy(k, me).wait_send()

        total = all_ref[0]
        for b in range(1, N_DEV):
            total = total + all_ref[b]
        sum_ref[...] = total

        ct = ct_ref[...].astype(BF16).astype(F32)
        gw = jnp.zeros((D, CW), F32)
        for b in range(N_DEV):
            dm = all_ref[b, me].astype(BF16).astype(F32)
            gw = gw + ct[:, b:b + 1] * dm
        gw_ref[...] = gw

    vmem = pl.BlockSpec(memory_space=pltpu.VMEM)
    return pl.pallas_call(
        body, name="ada_bwd",
        in_specs=[vmem, vmem] + [ANY_SPEC] * len(deps), out_specs=[vmem, vmem],
        out_shape=[jax.ShapeDtypeStruct((NCH, 1, CW), F32), jax.ShapeDtypeStruct((D, CW), F32)],
        scratch_shapes=[pltpu.VMEM((N_DEV, NCH, 1, CW), F32), pltpu.SemaphoreType.DMA((2, N_DEV))],
        compiler_params=pltpu.CompilerParams(vmem_limit_bytes=VMEM_LIMIT),
    )(payload, cact_t, *deps)


def _exchange_in_chip(parts):
    W = len(parts)

    def body(*refs):
        p_refs, got_refs, (send_sems, recv_sems) = refs[:W], refs[W:2 * W], refs[2 * W:]
        x, y, c = _my_place()
        sibling = (x, y, 1 - c)
        copies = []
        for w in range(W):
            for q in range(4):
                copies.append(pltpu.make_async_remote_copy(
                    src_ref=p_refs[w].at[2 * q + (1 - c)], dst_ref=got_refs[w].at[q],
                    send_sem=send_sems.at[4 * w + q], recv_sem=recv_sems.at[4 * w + q],
                    device_id=sibling, device_id_type=MESH_ID))
        for cp in copies:
            cp.start()
        for cp in copies:
            cp.wait_recv()
        for cp in copies:
            cp.wait_send()

    return pl.pallas_call(
        body, name="grad_exchange_in_chip",
        in_specs=[HBM_SPEC] * W, out_specs=[HBM_SPEC] * W,
        out_shape=[jax.ShapeDtypeStruct((4,) + p.shape[1:], p.dtype) for p in parts],
        scratch_shapes=[pltpu.SemaphoreType.DMA((4 * W,)), pltpu.SemaphoreType.DMA((4 * W,))],
    )(*parts)


def _pair_sum(parts, got, core):
    _, R, C = parts.shape
    tr = _tile(R, max(PACK_ROW_ALIGN, PAIR_SUM_BLOCK // C), PACK_ROW_ALIGN)

    def body(c_ref, p_ref, g_ref, o_ref):
        o_ref[...] = (p_ref[...].astype(F32) + g_ref[...].astype(F32)).astype(o_ref.dtype)

    return pl.pallas_call(
        body, name="grad_pair_sum",
        grid_spec=pltpu.PrefetchScalarGridSpec(
            num_scalar_prefetch=1, grid=(4, R // tr),
            in_specs=[pl.BlockSpec((1, tr, C), lambda q, i, c_ref: (2 * q + c_ref[0], i, 0)),
                      pl.BlockSpec((1, tr, C), lambda q, i, c_ref: (q, i, 0))],
            out_specs=pl.BlockSpec((1, tr, C), lambda q, i, c_ref: (q, i, 0))),
        out_shape=jax.ShapeDtypeStruct((4, R, C), parts.dtype),
        compiler_params=_cparams(("parallel", "parallel")),
    )(core, parts, got)


HBM_SPEC = pl.BlockSpec(memory_space=pltpu.HBM)
SEM_SPEC = pl.BlockSpec(memory_space=pltpu.SEMAPHORE)
ANY_SPEC = pl.BlockSpec(memory_space=pl.ANY)
SPLIT_EFFECT = pltpu.SideEffectType.DATAFLOW_SIDE_EFFECTING


def _landing_zone(shape, dtype):
    return pltpu.with_memory_space_constraint(lax.empty(shape, dtype), pltpu.HBM)


def _split_start(name, arrays, lands, after, copies_of, per_array):
    W = len(arrays)

    def body(*refs):
        x_refs, land_refs = refs[:W], refs[W:2 * W]
        send_sems, recv_sems = refs[2 * W + 1], refs[2 * W + 2]
        token = refs[-1]
        k = 0
        for w in range(W):
            for src, dst, dev in copies_of(w, x_refs[w], land_refs[w]):
                pltpu.make_async_remote_copy(src_ref=src, dst_ref=dst, send_sem=send_sems.at[k], recv_sem=recv_sems.at[k],
                                             device_id=dev, device_id_type=MESH_ID).start()
                k += 1
        token[...] = jnp.zeros_like(token)

    n_copies = per_array * W
    hbm_of = lambda xs: tuple(pltpu.HBM(a.shape, a.dtype) for a in xs)
    out = pl.pallas_call(
        body, name=name,
        out_shape=(pltpu.SemaphoreType.DMA((n_copies,)), pltpu.SemaphoreType.DMA((n_copies,)))
        + hbm_of(arrays) + hbm_of(lands) + (jax.ShapeDtypeStruct((8, LANE), F32),),
        in_specs=(HBM_SPEC,) * (2 * W) + (ANY_SPEC,),
        out_specs=(SEM_SPEC, SEM_SPEC) + (HBM_SPEC,) * (2 * W) + (pl.BlockSpec(memory_space=pltpu.VMEM),),
        input_output_aliases={i: 2 + i for i in range(2 * W)},
        compiler_params=pltpu.CompilerParams(has_side_effects=SPLIT_EFFECT),
    )(*[pltpu.with_memory_space_constraint(a, pltpu.HBM) for a in arrays], *lands, after)
    return out[0], out[1], list(out[2:2 + W]), list(out[2 + W:2 + 2 * W]), out[-1]


def _split_wait(name, state, after, copies_of):
    send_sems, recv_sems, arrays, lands, _ = state
    W = len(arrays)
    after = tuple(after) if isinstance(after, (tuple, list)) else (after,)

    def body(*refs):
        x_refs, land_refs = refs[:W], refs[W:2 * W]
        send_sems, recv_sems = refs[2 * W], refs[2 * W + 1]
        k = 0
        for w in range(W):
            for src, dst, dev in copies_of(w, x_refs[w], land_refs[w]):
                cp = pltpu.make_async_remote_copy(src_ref=src, dst_ref=dst, send_sem=send_sems.at[k],
                                                  recv_sem=recv_sems.at[k], device_id=dev, device_id_type=MESH_ID)
                cp.wait_send()
                cp.wait_recv()
                k += 1

    out = pl.pallas_call(
        body, name=name,
        out_shape=tuple(pltpu.HBM(a.shape, a.dtype) for a in arrays + lands),
        in_specs=(HBM_SPEC,) * (2 * W) + (SEM_SPEC, SEM_SPEC) + (ANY_SPEC,) * len(after),
        out_specs=(HBM_SPEC,) * (2 * W),
        input_output_aliases={i: i for i in range(2 * W)},
        compiler_params=pltpu.CompilerParams(has_side_effects=SPLIT_EFFECT),
    )(*arrays, *lands, send_sems, recv_sems, *after)
    return list(out[:W]), list(out[W:])


def _scatter_copies(w, p_ref, land_ref):
    x, y, c = _my_place()
    my_chip = 2 * x + y
    return [(p_ref.at[2 * (x ^ (k >> 1)) + (y ^ (k & 1))], land_ref.at[my_chip], (x ^ (k >> 1), y ^ (k & 1), c))
            for k in range(1, 4)]


def _gather_copies(w, x_ref, land_ref):
    x, y, c = _my_place()
    me = _linear((x, y, c))
    devs = [(x, y, 1 - c)] + [(x ^ (k >> 1), y ^ (k & 1), c) for k in range(1, 4)]
    return [(x_ref, land_ref.at[me], d) for d in devs]


def _gather_forward(lands, name):
    W = len(lands)

    def body(*refs):
        land_refs, out_refs, (send_sems, recv_sems) = refs[:W], refs[W:2 * W], refs[2 * W:]
        x, y, c = _my_place()
        sibling = (x, y, 1 - c)
        sends, arrivals = [], []
        for w in range(W):
            for k in range(1, 4):
                px, py = x ^ (k >> 1), y ^ (k & 1)
                landed, theirs = _linear((px, py, c)), out_refs[w].at[_linear((px, py, 1 - c))]
                sem = 3 * w + k - 1
                sends.append(pltpu.make_async_remote_copy(
                    src_ref=land_refs[w].at[landed], dst_ref=out_refs[w].at[landed],
                    send_sem=send_sems.at[sem], recv_sem=recv_sems.at[sem], device_id=sibling, device_id_type=MESH_ID))
                arrivals.append(pltpu.make_async_remote_copy(
                    src_ref=theirs, dst_ref=theirs, send_sem=send_sems.at[sem], recv_sem=recv_sems.at[sem],
                    device_id=sibling, device_id_type=MESH_ID))
        for cp in sends:
            cp.start()
        for cp in arrivals:
            cp.wait_recv()
        for cp in sends:
            cp.wait_send()

    return pl.pallas_call(
        body, name=name,
        in_specs=[HBM_SPEC] * W, out_specs=[HBM_SPEC] * W,
        out_shape=[jax.ShapeDtypeStruct(l.shape, l.dtype) for l in lands],
        input_output_aliases={i: i for i in range(W)},
        scratch_shapes=[pltpu.SemaphoreType.DMA((3 * W,)), pltpu.SemaphoreType.DMA((3 * W,))],
    )(*lands)


def _with_own_slot(gathered, shard):
    return lax.dynamic_update_index_in_dim(gathered, shard[None], _linear(_my_place()), axis=0)


def _in_chip_copies(w, p_ref, land_ref):
    x, y, c = _my_place()
    return [(p_ref.at[2 * q + (1 - c)], land_ref.at[q], (x, y, 1 - c)) for q in range(4)]


def _in_chip_start(parts, tag):
    lands = [_landing_zone((4,) + p.shape[1:], p.dtype) for p in parts]
    return _split_start("grad_in_chip_start_" + tag, parts, lands, parts[0], _in_chip_copies, 4)


def _reduce_scatter_begin(parts, tag, in_chip_state=None, after=()):
    if in_chip_state is None:
        got = _exchange_in_chip(parts)
    else:
        parts, got = _split_wait("grad_in_chip_wait_" + tag, in_chip_state, after, _in_chip_copies)
    core = lax.axis_index("c").astype(jnp.int32).reshape(1)
    chip_parts = [_pair_sum(p, g, core) for p, g in zip(parts, got)]
    lands = [_landing_zone(p.shape, p.dtype) for p in chip_parts]
    return _split_start("grad_scatter_start_" + tag, chip_parts, lands, got[0], _scatter_copies, 3)


def _reduce_scatter_end(state, after, tag):
    return _split_wait("grad_scatter_wait_" + tag, state, after, _scatter_copies)


def kernel(x, c, positions, w_ada, b_ada, w_in, g_q_a, w_q_b, g_kv_a, w_kv_b, w_o_a, w_conv, w_o_b, w_o, ln1_g, ln1_b, w_ffn_in, w_ffn_out, ln2_g, ln2_b, loss_target, m_w_ada, m_b_ada, m_w_in, m_g_q_a, m_w_q_b, m_g_kv_a, m_w_kv_b, m_w_o_a, m_w_conv, m_w_o_b, m_w_o, m_ln1_g, m_ln1_b, m_w_ffn_in, m_w_ffn_out, m_ln2_g, m_ln2_b, v_w_ada, v_b_ada, v_w_in, v_g_q_a, v_w_q_b, v_g_kv_a, v_w_kv_b, v_w_o_a, v_w_conv, v_w_o_b, v_w_o, v_ln1_g, v_ln1_b, v_w_ffn_in, v_w_ffn_out, v_ln2_g, v_ln2_b):
    x2, tgt = x[0], loss_target[0]
    S, D = x2.shape
    Lq, Lkv = g_q_a.shape[1], g_kv_a.shape[1]
    H = w_q_b.shape[2] * N_DEV // QK_CAT
    F = w_ffn_out.shape[1] * N_DEV
    assert Lq == Lkv and (Lq + Lkv) % COL_BLOCK == 0 and D % COL_BLOCK == 0
    front = Lq + Lkv + QK_ROPE
    front_pad = _round_up(front, COL_BLOCK)
    kr_blk = (Lq + Lkv) // COL_BLOCK
    blk_b = front_pad // COL_BLOCK
    nblk = D // COL_BLOCK
    blk_c, blk_x, blk_ga, blk_gb = blk_b + nblk, blk_b + 2 * nblk, blk_b + 3 * nblk, blk_b + 4 * nblk
    ts = _tile(S, 256, 8)
    T = _tile(S, min(512, S // 2), CHUNK)
    tb = _tile(F, 2816)
    me = _linear(_my_place())

    landing = lambda shards: [_landing_zone((N_DEV,) + s.shape, BF16) for s in shards]
    first = [w[0].astype(BF16) for w in (w_in, w_q_b, w_kv_b)]
    first_state = _split_start("first_gather_start", first, landing(first), c, _gather_copies, 4)
    later = [w[0].astype(BF16) for w in (w_o_a, w_o_b, w_o, w_ffn_in, w_ffn_out)]

    cw = w_ada.shape[2]
    b_mine = lax.dynamic_slice(b_ada, (0, me * cw), (1, cw)).reshape(1, 1, cw)
    c_row = c.reshape(1, 1, D) + first_state[4][0, 0]
    mod_blocks, cact_all, wconv_all = _ada_fwd(c_row, w_conv[0].reshape(1, 1, -1), w_ada[0], b_mine)
    mod = mod_blocks.reshape(6, D)
    cact_all = cact_all.reshape(N_DEV, D)
    w_conv_full = wconv_all.reshape(N_DEV, CONV_K, -1).transpose(1, 0, 2).reshape(CONV_K, D)
    u = _modulate_in(x2, mod, ts)

    first_shards, first_lands = _split_wait("first_gather_wait", first_state, (u, *later), _gather_copies)
    g_in, wq_s, wkv_s = [_with_own_slot(g, s) for g, s in
                         zip(_gather_forward(first_lands, "first_gather_forward"), first_shards)]
    later_state = _split_start("weight_gather_start", later, landing(later), g_in, _gather_copies, 4)
    later_token = later_state[4]
    w_in_p = _assemble_w_in(g_in, front, front_pad)

    inv_freq = 1.0 / (ROPE_THETA ** (jnp.arange(0, QK_ROPE, 2, dtype=F32) / QK_ROPE))
    ang = positions[0].astype(F32)[:, None] * inv_freq
    cos2 = jnp.concatenate([jnp.cos(ang), jnp.cos(ang)], axis=-1)
    sin2 = jnp.concatenate([jnp.sin(ang), jnp.sin(ang)], axis=-1)
    one, zero = jnp.ones((S, QK_NOPE), F32), jnp.zeros((S, QK_NOPE), F32)
    cos_q, sin_q = jnp.concatenate([one, cos2, one, cos2], axis=-1), jnp.concatenate([zero, sin2, zero, sin2], axis=-1)
    cos_k, sin_k = jnp.tile(cos2, (1, COL_BLOCK // QK_ROPE)), jnp.tile(sin2, (1, COL_BLOCK // QK_ROPE))

    proj = _matmul(u, w_in_p, "nn", F32, "proj", deps=(later_token,))
    qn = _rms_fwd(proj, g_q_a, 0, Lq, ts, "rms_q")
    kvn = _rms_fwd(proj, g_kv_a, 1, Lkv, ts, "rms_kv")
    q = _matmul(qn, wq_s, "nn", F32, "q_up")
    kv = _matmul(kvn, wkv_s, "nn", F32, "kv_up")
    qc, kc, vh = _qk_prep(q, kv, proj, kr_blk, cos_q, sin_q, cos_k, sin_k, H, ts)
    attn, lse = _attn_fwd(qc, kc, vh, T)
    later_shards, later_lands = _split_wait("weight_gather_wait", later_state, lse, _gather_copies)
    later_all = _gather_forward(later_lands, "weight_gather_forward")
    g_oa, g_ob, g_o, w_fi_s, g_fo = [_with_own_slot(g, s) for g, s in zip(later_all, later_shards)]
    w_oa_f, w_ob_f, w_o_f = g_oa.reshape(-1, D), g_ob.reshape(-1, D), g_o.reshape(-1, D)
    w_fo_f = g_fo.reshape(F, D)
    ya = _matmul(attn, w_oa_f, "nn", F32, "attn_out")
    cbc = _conv_fwd(proj, w_conv_full, blk_b, blk_c, blk_x)
    yb = _matmul(cbc, w_ob_f, "nn", F32, "conv_out")
    merged = _merge_fwd(proj, ya, yb, blk_ga, blk_gb, ts)
    mix = _matmul(merged, w_o_f, "nn", F32, "mix_out")
    xhat1, rstd1, u2 = _ln1_fwd(x2, mix, mod, ln1_g, ln1_b, ts)
    hh = _matmul(u2, w_fi_s, "nn", F32, "ffn_in")
    act = _swiglu_fwd(hh, ts, tb)
    ffn = _matmul(act, w_fo_f, "nn", F32, "ffn_out")
    loss_part, dffn, dx1a, vec2 = _ln2_loss(xhat1, ffn, tgt, mod, ln1_g, ln1_b, ln2_g, ln2_b, ts)
    loss = lax.psum(loss_part[0, 0], AXES)

    gw_fo = _matmul(act, dffn, "tn", BF16, "grad_w_ffn_out")
    da = _matmul(dffn, w_fo_f, "nt", F32, "d_act")
    dh = _swiglu_bwd(da, hh, ts, tb)
    gw_fi = _matmul(u2, dh, "tn", BF16, "grad_w_ffn_in", out_shards=True)
    ffn_in_chip = _in_chip_start([gw_fi, gw_fo.reshape(N_DEV, -1, D)], "ffn")
    du2 = _matmul(dh, w_fi_s, "nt", F32, "d_u2", deps=(ffn_in_chip[4],))
    ffn_state = _reduce_scatter_begin(None, "ffn", ffn_in_chip, after=(du2,))
    dxa, dmix, vec1 = _ln1_bwd(du2, dx1a, xhat1, rstd1, mix, mod, ln1_g, ln1_b, ts)
    gw_o = _matmul(merged, dmix, "tn", BF16, "grad_w_o", deps=(ffn_state[4],))
    dmerged = _matmul(dmix, w_o_f, "nt", F32, "d_merged")
    dya, dyb, dga, dgb = _merge_bwd(dmerged, proj, ya, yb, blk_ga, blk_gb, ts)
    gw_ob = _matmul(cbc, dyb, "tn", BF16, "grad_w_o_b")
    dcbc = _matmul(dyb, w_ob_f, "nt", F32, "d_conv")
    dcb, dcc, dcx, dwconv = _conv_bwd(dcbc, proj, w_conv_full, blk_b, blk_c, blk_x)
    gw_oa = _matmul(attn, dya, "tn", BF16, "grad_w_o_a")
    mix_in_chip = _in_chip_start([g.reshape(N_DEV, -1, D) for g in (gw_oa, gw_ob, gw_o)], "mix")
    dattn = _matmul(dya, w_oa_f, "nt", F32, "d_attn", deps=(mix_in_chip[4],))
    dqc, dkc, dvh = _attn_bwd(qc, kc, vh, dattn, attn, lse, T)
    ffn_own, ffn_got = _reduce_scatter_end(ffn_state, dqc, "ffn")
    mix_state = _reduce_scatter_begin(None, "mix", mix_in_chip, after=(dqc,))
    dq, dkv, dkr = _qk_bwd(dqc, dkc, dvh, cos_q, sin_q, cos_k, sin_k, ts)
    gw_qb = _matmul(qn, dq, "tn", BF16, "grad_w_q_b", out_shards=True, deps=(mix_state[4],))
    dqn = _matmul(dq, wq_s, "nt", F32, "d_qn")
    gw_kvb = _matmul(kvn, dkv, "tn", BF16, "grad_w_kv_b", out_shards=True)
    dkvn = _matmul(dkv, wkv_s, "nt", F32, "d_kvn")
    dqa, dgq = _rms_bwd(dqn, proj, g_q_a, 0, Lq, ts, "rms_q_bwd")
    dkva, dgkv = _rms_bwd(dkvn, proj, g_kv_a, 1, Lkv, ts, "rms_kv_bwd")
    dproj = jnp.concatenate([dqa, dkva, dkr, dcb, dcc, dcx, dga, dgb], axis=1)
    gw_in_p = _matmul(u, dproj, "tn", BF16, "grad_w_in")
    mix_own, mix_got = _reduce_scatter_end(mix_state, gw_in_p, "mix")
    in_state = _reduce_scatter_begin([_split_w_in(gw_in_p, front, front_pad), gw_qb, gw_kvb], "in")
    du = _matmul(dproj, w_in_p, "nt", F32, "d_u", deps=(in_state[4],))
    grad_x, vec0 = _grad_x(du, dxa, x2, mod, ts)

    my_chip = (2 * lax.axis_index("x") + lax.axis_index("y")).astype(jnp.int32).reshape(1)
    arrived = {}
    for nm, w, m, v, own, got in (
            ("w_ffn_in", w_ffn_in, m_w_ffn_in, v_w_ffn_in, ffn_own[0], ffn_got[0]),
            ("w_ffn_out", w_ffn_out, m_w_ffn_out, v_w_ffn_out, ffn_own[1], ffn_got[1]),
            ("w_o_a", w_o_a, m_w_o_a, v_w_o_a, mix_own[0], mix_got[0]),
            ("w_o_b", w_o_b, m_w_o_b, v_w_o_b, mix_own[1], mix_got[1]),
            ("w_o", w_o, m_w_o, v_w_o, mix_own[2], mix_got[2])):
        arrived[nm] = [a[None] for a in _adamw_reduced(w[0], own, got, m[0], v[0], my_chip, "adamw_" + nm)]

    dmod = jnp.concatenate([vec0[0], vec0[1], vec1[4], vec1[0], vec1[1], vec2[2]])
    small = jnp.concatenate([dmod, dgq[0], dgkv[0], vec1[2], vec1[3], vec2[0], vec2[1], dwconv[:CONV_K].reshape(-1)])
    n_small = small.shape[0]
    nch = _round_up(n_small, cw) // cw
    payload = jnp.pad(small, (0, nch * cw - n_small)).reshape(nch, 1, cw)
    summed, g_w_ada = _ada_bwd(payload, cact_all.T, deps=[res[1] for res in arrived.values()])
    summed = summed.reshape(-1)
    offs = [0, 6 * D, 6 * D + Lq, 6 * D + Lq + Lkv]
    offs += [offs[-1] + D * k for k in range(1, 5)]
    g_b_ada = summed[offs[0]:offs[1]].reshape(1, -1)
    g_gq = summed[offs[1]:offs[2]].reshape(1, -1)
    g_gkv = summed[offs[2]:offs[3]].reshape(1, -1)
    g_ln1g, g_ln1b, g_ln2g, g_ln2b = [summed[offs[3 + k]:offs[4 + k]].reshape(1, -1) for k in range(4)]
    wc = w_conv.shape[2]
    g_wconv = lax.dynamic_slice(summed[offs[7]:offs[7] + CONV_K * D].reshape(CONV_K, D), (0, me * wc), (CONV_K, wc))

    names = ["w_ada", "b_ada", "w_in", "g_q_a", "w_q_b", "g_kv_a", "w_kv_b", "w_o_a", "w_conv", "w_o_b", "w_o",
             "ln1_g", "ln1_b", "w_ffn_in", "w_ffn_out", "ln2_g", "ln2_b"]
    weights = [w_ada, b_ada, w_in, g_q_a, w_q_b, g_kv_a, w_kv_b, w_o_a, w_conv, w_o_b, w_o, ln1_g, ln1_b,
               w_ffn_in, w_ffn_out, ln2_g, ln2_b]
    moms = [m_w_ada, m_b_ada, m_w_in, m_g_q_a, m_w_q_b, m_g_kv_a, m_w_kv_b, m_w_o_a, m_w_conv, m_w_o_b, m_w_o,
            m_ln1_g, m_ln1_b, m_w_ffn_in, m_w_ffn_out, m_ln2_g, m_ln2_b]
    vels = [v_w_ada, v_b_ada, v_w_in, v_g_q_a, v_w_q_b, v_g_kv_a, v_w_kv_b, v_w_o_a, v_w_conv, v_w_o_b, v_w_o,
            v_ln1_g, v_ln1_b, v_w_ffn_in, v_w_ffn_out, v_ln2_g, v_ln2_b]
    grad_of = {"w_ada": g_w_ada, "b_ada": g_b_ada, "g_q_a": g_gq, "g_kv_a": g_gkv, "w_conv": g_wconv,
               "ln1_g": g_ln1g, "ln1_b": g_ln1b, "ln2_g": g_ln2g, "ln2_b": g_ln2b}
    state_of = dict(zip(names, zip(weights, moms, vels)))
    results = dict(arrived)

    def update(nm, reduced=None):
        w, m, v = state_of[nm]
        shp = w.shape
        w2 = w.reshape(shp[-2], shp[-1]) if w.ndim == 3 else w
        m2, v2 = m.reshape(w2.shape), v.reshape(w2.shape)
        if reduced is None:
            g2 = grad_of[nm].reshape(w2.shape)
            res = (g2,) + tuple(_adamw(w2, g2, m2, v2, "adamw_" + nm))
        else:
            res = _adamw_reduced(w2, reduced[0], reduced[1], m2, v2, my_chip, "adamw_" + nm)
        results[nm] = [a.reshape(shp) for a in res]

    for nm in grad_of:
        update(nm)
    in_own, in_got = _reduce_scatter_end(in_state, [res[1] for res in results.values()], "in")
    for nm, own, got in zip(("w_in", "w_q_b", "w_kv_b"), in_own, in_got):
        update(nm, (own, got))
    outs = [[results[nm][k] for nm in names] for k in range(4)]
    return (loss, grad_x.reshape(x.shape), *outs[0], *outs[1], *outs[2], *outs[3])
```

```python
import functools

import jax
import jax.numpy as jnp
from jax import lax
from jax.experimental import pallas as pl
from jax.experimental.pallas import tpu as pltpu

F32 = jnp.float32
BF16 = jnp.bfloat16
MESH_ID = pl.DeviceIdType.MESH
AXES = ("x", "y", "c")
N_DEV = 8

CHUNK = 64
QK_NOPE = 128
QK_ROPE = 64
V_HEAD = 128
QK_CAT = QK_NOPE + QK_ROPE
ROPE_THETA = 10000.0
ATTN_SCALE = (QK_NOPE + QK_ROPE) ** -0.5
CONV_K = 3
DEEPNORM_ALPHA = 2.0 ** 0.25
LN_EPS = 1e-5
RMS_EPS = 1e-6
NEG_INF = -1e30

ADAM_LR = 0.001
ADAM_B1 = 0.9
ADAM_B2 = 0.999
ADAM_EPS = 1e-08
ADAM_WD = 0.01
ADAM_STEP = 10

LANE = 128
COL_BLOCK = 256
PACK_ROW_ALIGN = 16
PAIR_SUM_BLOCK = 1 << 20
VMEM_LIMIT = 48 * 1024 * 1024


def _round_up(n, m):
    return (n + m - 1) // m * m


def _tile(n, pref, align=LANE):
    best = None
    t = align
    while t <= min(n, pref):
        if n % t == 0:
            best = t
        t += align
    return best if best is not None else n


def _cparams(sem=None):
    return pltpu.CompilerParams(dimension_semantics=sem, vmem_limit_bytes=VMEM_LIMIT)


def _sigmoid(x):
    return 0.5 * jnp.tanh(0.5 * x) + 0.5


def _matmul(a, b, mode, out_dtype, name, tm=1024, tn=1024, tk=2048, deps=(), out_shards=False, k_rows=None,
            init=None):
    b_shards = b.ndim == 3
    n = b.shape[2] if b_shards else (b.shape[1] // N_DEV if out_shards else None)
    if mode == "nn":
        (M, K), (K2, N) = a.shape, (b.shape[1], N_DEV * n) if b_shards else b.shape
    elif mode == "nt":
        (M, K), (N, K2) = a.shape, (b.shape[1], N_DEV * n) if b_shards else b.shape
    else:
        (K, M), (K2, N) = a.shape, b.shape
    assert K == K2, (a.shape, b.shape, mode)
    tm = _tile(M, tm)
    tn = n if (mode != "nt" and n is not None) else _tile(N, tn)
    k_row0, k_len = k_rows if k_rows is not None else (0, K)
    tk = n if (mode == "nt" and b_shards) else _tile(k_len, tk)
    nk, k0 = k_len // tk, k_row0 // tk
    if mode == "nn":
        a_spec = pl.BlockSpec((tm, tk), lambda i, j, k: (i, k0 + k))
        b_spec = (pl.BlockSpec((1, tk, n), lambda i, j, k: (j, k, 0)) if b_shards
                  else pl.BlockSpec((tk, tn), lambda i, j, k: (k0 + k, j)))
        dims = (((1,), (0,)), ((), ()))
    elif mode == "nt":
        a_spec = pl.BlockSpec((tm, tk), lambda i, j, k: (i, k))
        b_spec = (pl.BlockSpec((1, tn, n), lambda i, j, k: (k, j, 0)) if b_shards
                  else pl.BlockSpec((tn, tk), lambda i, j, k: (j, k)))
        dims = (((1,), (1,)), ((), ()))
    else:
        a_spec = pl.BlockSpec((tk, tm), lambda i, j, k: (k, i))
        b_spec = pl.BlockSpec((tk, tn), lambda i, j, k: (k, j))
        dims = (((0,), (0,)), ((), ()))
    if out_shards:
        out_spec = pl.BlockSpec((1, tm, n), lambda i, j, k: (j, i, 0))
        out_shape = jax.ShapeDtypeStruct((N_DEV, M, n), out_dtype)
    else:
        out_spec = pl.BlockSpec((tm, tn), lambda i, j, k: (i, j))
        out_shape = jax.ShapeDtypeStruct((M, N), out_dtype)

    def product(a_ref, b_ref):
        b_blk = b_ref[0] if b_shards else b_ref[...]
        return lax.dot_general(a_ref[...].astype(BF16), b_blk.astype(BF16), dims, preferred_element_type=F32)

    def write(o_ref, value):
        if out_shards:
            o_ref[0] = value.astype(o_ref.dtype)
        else:
            o_ref[...] = value.astype(o_ref.dtype)

    def body_whole_k(a_ref, b_ref, *rest):
        value = product(a_ref, b_ref)
        write(rest[-1], value if init is None else value + rest[0][...])

    def body_split_k(a_ref, b_ref, *rest):
        o_ref, acc_ref = rest[-2:]
        k = pl.program_id(2)

        @pl.when(k == 0)
        def _():
            acc_ref[...] = jnp.zeros_like(acc_ref) if init is None else rest[0][...]

        acc_ref[...] += product(a_ref, b_ref)

        @pl.when(k == nk - 1)
        def _():
            write(o_ref, acc_ref[...])

    return pl.pallas_call(
        body_whole_k if nk == 1 else body_split_k, name=name, grid=(M // tm, N // tn, nk),
        in_specs=[a_spec, b_spec] + ([] if init is None else [out_spec]) + [ANY_SPEC] * len(deps),
        out_specs=out_spec, out_shape=out_shape,
        scratch_shapes=[] if nk == 1 else [pltpu.VMEM((tm, tn), F32)],
        compiler_params=_cparams(("parallel", "parallel", "arbitrary")),
    )(a, b, *(() if init is None else (init,)), *deps)


def _assemble_w_in(shards, front, front_pad, rows, row0, into=None):
    _, K, n = shards.shape
    gap = front_pad - front
    tk = _tile(K, 256, PACK_ROW_ALIGN)
    blk0 = row0 // tk

    def body(g_ref, *rest):
        o_ref = rest[-1]
        if gap:
            o_ref[:, front:front_pad] = jnp.zeros((tk, gap), o_ref.dtype)
        for j in range(N_DEV):
            lo, hi = j * n, (j + 1) * n
            if lo < front < hi:
                o_ref[:, lo:front] = g_ref[j, :, 0:front - lo]
                o_ref[:, front_pad:hi + gap] = g_ref[j, :, front - lo:n]
            else:
                off = 0 if hi <= front else gap
                o_ref[:, lo + off:hi + off] = g_ref[j]

    return pl.pallas_call(
        body, name="assemble_w_in", grid=(K // tk,),
        in_specs=[pl.BlockSpec((N_DEV, tk, n), lambda i: (0, i, 0))] + ([] if into is None else [ANY_SPEC]),
        out_specs=pl.BlockSpec((tk, N_DEV * n + gap), lambda i: (blk0 + i, 0)),
        out_shape=jax.ShapeDtypeStruct((rows, N_DEV * n + gap), shards.dtype),
        input_output_aliases={} if into is None else {1: 0},
        compiler_params=_cparams(("parallel",)),
    )(*([shards] if into is None else [shards, into]))


def _split_w_in(w, front, front_pad):
    K, NP = w.shape
    gap = front_pad - front
    n = (NP - gap) // N_DEV
    tk = _tile(K, 256, PACK_ROW_ALIGN)

    def body(w_ref, o_ref):
        for j in range(N_DEV):
            lo, hi = j * n, (j + 1) * n
            if lo < front < hi:
                o_ref[j, :, 0:front - lo] = w_ref[:, lo:front]
                o_ref[j, :, front - lo:n] = w_ref[:, front_pad:hi + gap]
            else:
                off = 0 if hi <= front else gap
                o_ref[j] = w_ref[:, lo + off:hi + off]

    return pl.pallas_call(
        body, name="split_grad_w_in", grid=(K // tk,),
        in_specs=[pl.BlockSpec((tk, NP), lambda i: (i, 0))],
        out_specs=pl.BlockSpec((N_DEV, tk, n), lambda i: (0, i, 0)),
        out_shape=jax.ShapeDtypeStruct((N_DEV, K, n), w.dtype),
        compiler_params=_cparams(("parallel",)),
    )(w)


def _modulate_in(x, mod, ts):
    S, D = x.shape

    def body(x_ref, mod_ref, u_ref):
        u_ref[...] = (x_ref[...] * (1.0 + mod_ref[1:2, :]) + mod_ref[0:1, :]).astype(BF16)

    return pl.pallas_call(
        body, name="modulate_in", grid=(S // ts,),
        in_specs=[pl.BlockSpec((ts, D), lambda i: (i, 0)), pl.BlockSpec((6, D), lambda i: (0, 0))],
        out_specs=pl.BlockSpec((ts, D), lambda i: (i, 0)),
        out_shape=jax.ShapeDtypeStruct((S, D), BF16),
        compiler_params=_cparams(("parallel",)),
    )(x, mod)


def _rms_fwd(proj, g, blk, L, ts, name):
    S = proj.shape[0]

    def body(a_ref, g_ref, y_ref):
        a = a_ref[...]
        r = lax.rsqrt(jnp.mean(a * a, axis=-1, keepdims=True) + RMS_EPS)
        y_ref[...] = (a * r * g_ref[...]).astype(BF16)

    return pl.pallas_call(
        body, name=name, grid=(S // ts,),
        in_specs=[pl.BlockSpec((ts, L), lambda i: (i, blk)), pl.BlockSpec((1, L), lambda i: (0, 0))],
        out_specs=pl.BlockSpec((ts, L), lambda i: (i, 0)),
        out_shape=jax.ShapeDtypeStruct((S, L), BF16),
        compiler_params=_cparams(("parallel",)),
    )(proj, g)


def _rope_partner(x, period, start):
    w = x.shape[-1]
    lane = lax.broadcasted_iota(jnp.int32, x.shape, x.ndim - 1) % period
    first = (lane >= start) & (lane < start + QK_ROPE // 2)
    from_right = pltpu.roll(x, w - QK_ROPE // 2, axis=x.ndim - 1)
    from_left = pltpu.roll(x, QK_ROPE // 2, axis=x.ndim - 1)
    return jnp.where(first, -from_right, from_left)


def _qk_prep(q, kv, proj, kr_blk, cos_q, sin_q, cos_k, sin_k, H, ts):
    S = q.shape[0]
    pair = 2 * QK_CAT
    kv_w = QK_NOPE + V_HEAD

    def body(q_ref, kv_ref, kr_ref, cq_ref, sq_ref, ck_ref, sk_ref, qc_ref, kc_ref, vh_ref):
        kr = kr_ref[...]
        kr = kr * ck_ref[...] + _rope_partner(kr, QK_ROPE, 0) * sk_ref[...]
        kr = kr[:, :QK_ROPE].astype(BF16)
        for p in range(H // 2):
            x = q_ref[:, p * pair:(p + 1) * pair]
            x = x * cq_ref[...] + _rope_partner(x, QK_CAT, QK_NOPE) * sq_ref[...]
            qc_ref[2 * p] = x[:, :QK_CAT].astype(BF16)
            qc_ref[2 * p + 1] = x[:, QK_CAT:].astype(BF16)
        for h in range(H):
            kc_ref[h, :, 0:QK_NOPE] = kv_ref[:, h * kv_w:h * kv_w + QK_NOPE].astype(BF16)
            kc_ref[h, :, QK_NOPE:QK_CAT] = kr
            vh_ref[h, :, :] = kv_ref[:, h * kv_w + QK_NOPE:(h + 1) * kv_w].astype(BF16)

    row = lambda w: pl.BlockSpec((ts, w), lambda i: (i, 0))
    return pl.pallas_call(
        body, name="qk_prep", grid=(S // ts,),
        in_specs=[row(H * QK_CAT), row(H * kv_w),
                  pl.BlockSpec((ts, COL_BLOCK), lambda i: (i, kr_blk)),
                  row(pair), row(pair), row(COL_BLOCK), row(COL_BLOCK)],
        out_specs=[pl.BlockSpec((H, ts, QK_CAT), lambda i: (0, i, 0)),
                   pl.BlockSpec((H, ts, QK_CAT), lambda i: (0, i, 0)),
                   pl.BlockSpec((H, ts, V_HEAD), lambda i: (0, i, 0))],
        out_shape=[jax.ShapeDtypeStruct((H, S, QK_CAT), BF16), jax.ShapeDtypeStruct((H, S, QK_CAT), BF16),
                   jax.ShapeDtypeStruct((H, S, V_HEAD), BF16)],
        compiler_params=_cparams(("parallel",)),
    )(q, kv, proj, cos_q, sin_q, cos_k, sin_k)


NT_DIMS = (((1,), (1,)), ((), ()))
TN_DIMS = (((0,), (0,)), ((), ()))


def _diag_mask(T):
    rows = lax.broadcasted_iota(jnp.int32, (T, T), 0) // CHUNK
    cols = lax.broadcasted_iota(jnp.int32, (T, T), 1) // CHUNK
    return cols <= rows


def _attn_fwd(qc, kc, vh, T):
    H, S, _ = qc.shape
    n = S // T

    def body(q_ref, k_ref, v_ref, o_ref, lse_ref):
        q = q_ref[0]

        def block(i):
            L = (i + 1) * T
            s_old = lax.dot_general(q, k_ref[0, 0:i * T, :], NT_DIMS, preferred_element_type=F32) if i else None
            s_diag = lax.dot_general(q, k_ref[0, i * T:L, :], NT_DIMS, preferred_element_type=F32)
            s_diag = jnp.where(_diag_mask(T), s_diag, NEG_INF)
            m = jnp.max(s_diag, axis=-1, keepdims=True)
            if i:
                m = jnp.maximum(m, jnp.max(s_old, axis=-1, keepdims=True))
            p_diag = jnp.exp((s_diag - m) * ATTN_SCALE)
            l = jnp.sum(p_diag, axis=-1, keepdims=True)
            acc = jnp.dot(p_diag.astype(BF16), v_ref[0, i * T:L, :], preferred_element_type=F32)
            if i:
                p_old = jnp.exp((s_old - m) * ATTN_SCALE)
                l = l + jnp.sum(p_old, axis=-1, keepdims=True)
                acc = acc + jnp.dot(p_old.astype(BF16), v_ref[0, 0:i * T, :], preferred_element_type=F32)
            o_ref[...] = acc / l
            lse_ref[0] = m * ATTN_SCALE + jnp.log(l)

        for i in range(n):
            pl.when(pl.program_id(1) == i)(functools.partial(block, i))

    return pl.pallas_call(
        body, name="attn_fwd", grid=(H, n),
        in_specs=[pl.BlockSpec((1, T, QK_CAT), lambda h, i: (h, i, 0)),
                  pl.BlockSpec((1, S, QK_CAT), lambda h, i: (h, 0, 0)),
                  pl.BlockSpec((1, S, V_HEAD), lambda h, i: (h, 0, 0))],
        out_specs=[pl.BlockSpec((T, V_HEAD), lambda h, i: (i, h)),
                   pl.BlockSpec((1, T, 1), lambda h, i: (h, i, 0))],
        out_shape=[jax.ShapeDtypeStruct((S, H * V_HEAD), F32), jax.ShapeDtypeStruct((H, S, 1), F32)],
        compiler_params=_cparams(("parallel", "arbitrary")),
    )(qc, kc, vh)


def _shift_rows(z, k):
    if k == 0:
        return z
    n = z.shape[0]
    row = lax.broadcasted_iota(jnp.int32, z.shape, 0)
    if k > 0:
        return jnp.where(row >= k, pltpu.roll(z, k, axis=0), 0.0)
    return jnp.where(row < n + k, pltpu.roll(z, n + k, axis=0), 0.0)


def _conv_fwd(proj, w_conv, blk_b, blk_c, blk_x):
    S = proj.shape[0]
    D = w_conv.shape[1]
    nb = D // COL_BLOCK

    def body(cb_ref, cc_ref, cx_ref, w_ref, o_ref):
        z = cc_ref[...] * cx_ref[...]
        conv = w_ref[2:3, :] * z + w_ref[1:2, :] * _shift_rows(z, 1) + w_ref[0:1, :] * _shift_rows(z, 2)
        o_ref[...] = (cb_ref[...] * conv).astype(BF16)

    col = lambda off: pl.BlockSpec((S, COL_BLOCK), lambda j: (0, off + j))
    return pl.pallas_call(
        body, name="conv_fwd", grid=(nb,),
        in_specs=[col(blk_b), col(blk_c), col(blk_x), pl.BlockSpec((CONV_K, COL_BLOCK), lambda j: (0, j))],
        out_specs=pl.BlockSpec((S, COL_BLOCK), lambda j: (0, j)),
        out_shape=jax.ShapeDtypeStruct((S, D), BF16),
        compiler_params=_cparams(("parallel",)),
    )(proj, proj, proj, w_conv)


def _merge_fwd(proj, ya, yb, blk_ga, blk_gb, ts):
    S, D = ya.shape
    nb = D // COL_BLOCK

    def body(ga_ref, gb_ref, ya_ref, yb_ref, o_ref):
        o_ref[...] = (_sigmoid(ga_ref[...]) * ya_ref[...] + _sigmoid(gb_ref[...]) * yb_ref[...]).astype(BF16)

    row = pl.BlockSpec((ts, D), lambda i: (i, 0))
    seg = lambda blk: pl.BlockSpec((pl.Element(ts), pl.Element(D)), lambda i: (i * ts, blk * COL_BLOCK))
    return pl.pallas_call(
        body, name="merge_fwd", grid=(S // ts,),
        in_specs=[seg(blk_ga), seg(blk_gb), row, row],
        out_specs=row,
        out_shape=jax.ShapeDtypeStruct((S, D), BF16),
        compiler_params=_cparams(("parallel",)),
    )(proj, proj, ya, yb)


def _ln1_fwd(x, mix, mod, g, b, ts):
    S, D = x.shape

    def body(x_ref, mix_ref, mod_ref, g_ref, b_ref, xhat_ref, rstd_ref, u2_ref):
        r = DEEPNORM_ALPHA * x_ref[...] + mod_ref[2:3, :] * mix_ref[...]
        mu = jnp.mean(r, axis=-1, keepdims=True)
        d = r - mu
        rstd = lax.rsqrt(jnp.mean(d * d, axis=-1, keepdims=True) + LN_EPS)
        xhat = d * rstd
        xhat_ref[...] = xhat
        rstd_ref[...] = rstd
        x1 = xhat * g_ref[...] + b_ref[...]
        u2_ref[...] = (x1 * (1.0 + mod_ref[4:5, :]) + mod_ref[3:4, :]).astype(BF16)

    row = pl.BlockSpec((ts, D), lambda i: (i, 0))
    vec = lambda r: pl.BlockSpec((r, D), lambda i: (0, 0))
    return pl.pallas_call(
        body, name="ln1_fwd", grid=(S // ts,),
        in_specs=[row, row, vec(6), vec(1), vec(1)],
        out_specs=[row, pl.BlockSpec((ts, 1), lambda i: (i, 0)), row],
        out_shape=[jax.ShapeDtypeStruct((S, D), F32), jax.ShapeDtypeStruct((S, 1), F32),
                   jax.ShapeDtypeStruct((S, D), BF16)],
        compiler_params=_cparams(("parallel",)),
    )(x, mix, mod, g, b)


def _swiglu_fwd(h, ts, tb):
    S, F2 = h.shape
    F = F2 // 2
    nb = F // tb

    def body(hg_ref, hu_ref, a_ref):
        hg = hg_ref[...]
        a_ref[...] = (hg * _sigmoid(hg) * hu_ref[...]).astype(BF16)

    return pl.pallas_call(
        body, name="swiglu_fwd", grid=(S // ts, nb),
        in_specs=[pl.BlockSpec((ts, tb), lambda i, j: (i, j)), pl.BlockSpec((ts, tb), lambda i, j: (i, j + nb))],
        out_specs=pl.BlockSpec((ts, tb), lambda i, j: (i, j)),
        out_shape=jax.ShapeDtypeStruct((S, F), BF16),
        compiler_params=_cparams(("parallel", "parallel")),
    )(h, h)


def _ln2_loss(xhat1, ffn, tgt, mod, g1, b1, g2, b2, ts):
    S, D = xhat1.shape

    def body(xh_ref, ffn_ref, t_ref, mod_ref, g1_ref, b1_ref, g2_ref, b2_ref, loss_ref, dffn_ref, dx1_ref, vec_ref):
        i = pl.program_id(0)

        @pl.when(i == 0)
        def _():
            loss_ref[...] = jnp.zeros_like(loss_ref)
            vec_ref[...] = jnp.zeros_like(vec_ref)

        x1 = xh_ref[...] * g1_ref[...] + b1_ref[...]
        ffn = ffn_ref[...]
        r = DEEPNORM_ALPHA * x1 + mod_ref[5:6, :] * ffn
        mu = jnp.mean(r, axis=-1, keepdims=True)
        d = r - mu
        rstd = lax.rsqrt(jnp.mean(d * d, axis=-1, keepdims=True) + LN_EPS)
        xhat = d * rstd
        e = xhat * g2_ref[...] + b2_ref[...] - t_ref[...]
        loss_ref[...] += 0.5 * jnp.sum(jnp.mean(e * e, axis=-1, keepdims=True))
        dy = e * (1.0 / D)
        dxhat = dy * g2_ref[...]
        dr = rstd * (dxhat - jnp.mean(dxhat, axis=-1, keepdims=True)
                     - xhat * jnp.mean(dxhat * xhat, axis=-1, keepdims=True))
        dffn_ref[...] = (dr * mod_ref[5:6, :]).astype(BF16)
        dx1_ref[...] = DEEPNORM_ALPHA * dr
        vec_ref[0:1, :] += jnp.sum(dy * xhat, axis=0, keepdims=True)
        vec_ref[1:2, :] += jnp.sum(dy, axis=0, keepdims=True)
        vec_ref[2:3, :] += jnp.sum(dr * ffn, axis=0, keepdims=True)

    row = pl.BlockSpec((ts, D), lambda i: (i, 0))
    vec = lambda r: pl.BlockSpec((r, D), lambda i: (0, 0))
    return pl.pallas_call(
        body, name="ln2_loss", grid=(S // ts,),
        in_specs=[row, row, row, vec(6), vec(1), vec(1), vec(1), vec(1)],
        out_specs=[pl.BlockSpec((1, LANE), lambda i: (0, 0)), row, row, vec(8)],
        out_shape=[jax.ShapeDtypeStruct((1, LANE), F32), jax.ShapeDtypeStruct((S, D), BF16),
                   jax.ShapeDtypeStruct((S, D), F32), jax.ShapeDtypeStruct((8, D), F32)],
        compiler_params=_cparams(("arbitrary",)),
    )(xhat1, ffn, tgt, mod, g1, b1, g2, b2)


def _swiglu_bwd(da, h, ts, tb):
    S, F2 = h.shape
    nb = (F2 // 2) // tb

    def body(da_ref, hg_ref, hu_ref, dh_ref):
        hg, da = hg_ref[...], da_ref[...]
        sg = _sigmoid(hg)

        @pl.when(pl.program_id(2) == 0)
        def _():
            dh_ref[...] = (da * hu_ref[...] * (sg * (1.0 + hg * (1.0 - sg)))).astype(BF16)

        @pl.when(pl.program_id(2) == 1)
        def _():
            dh_ref[...] = (da * hg * sg).astype(BF16)

    lo = pl.BlockSpec((ts, tb), lambda i, j, k: (i, j))
    hi = pl.BlockSpec((ts, tb), lambda i, j, k: (i, j + nb))
    return pl.pallas_call(
        body, name="swiglu_bwd", grid=(S // ts, nb, 2),
        in_specs=[lo, lo, hi],
        out_specs=pl.BlockSpec((ts, tb), lambda i, j, k: (i, j + nb * k)),
        out_shape=jax.ShapeDtypeStruct((S, F2), BF16),
        compiler_params=_cparams(("parallel", "parallel", "arbitrary")),
    )(da, h, h)


def _ln1_bwd(du2, dx1a, xhat1, rstd1, mix, mod, g1, b1, ts):
    S, D = xhat1.shape

    def body(du2_ref, dx1a_ref, xh_ref, rstd_ref, mix_ref, mod_ref, g_ref, b_ref, dxa_ref, dmix_ref, vec_ref):
        i = pl.program_id(0)

        @pl.when(i == 0)
        def _():
            vec_ref[...] = jnp.zeros_like(vec_ref)

        xhat, du2, mix = xh_ref[...], du2_ref[...], mix_ref[...]
        x1 = xhat * g_ref[...] + b_ref[...]
        dx1 = dx1a_ref[...] + du2 * (1.0 + mod_ref[4:5, :])
        dxhat = dx1 * g_ref[...]
        dr = rstd_ref[...] * (dxhat - jnp.mean(dxhat, axis=-1, keepdims=True)
                              - xhat * jnp.mean(dxhat * xhat, axis=-1, keepdims=True))
        dxa_ref[...] = DEEPNORM_ALPHA * dr
        dmix_ref[...] = (dr * mod_ref[2:3, :]).astype(BF16)
        vec_ref[0:1, :] += jnp.sum(du2, axis=0, keepdims=True)
        vec_ref[1:2, :] += jnp.sum(du2 * x1, axis=0, keepdims=True)
        vec_ref[2:3, :] += jnp.sum(dx1 * xhat, axis=0, keepdims=True)
        vec_ref[3:4, :] += jnp.sum(dx1, axis=0, keepdims=True)
        vec_ref[4:5, :] += jnp.sum(dr * mix, axis=0, keepdims=True)

    row = pl.BlockSpec((ts, D), lambda i: (i, 0))
    vec = lambda r: pl.BlockSpec((r, D), lambda i: (0, 0))
    return pl.pallas_call(
        body, name="ln1_bwd", grid=(S // ts,),
        in_specs=[row, row, row, pl.BlockSpec((ts, 1), lambda i: (i, 0)), row, vec(6), vec(1), vec(1)],
        out_specs=[row, row, vec(8)],
        out_shape=[jax.ShapeDtypeStruct((S, D), F32), jax.ShapeDtypeStruct((S, D), BF16),
                   jax.ShapeDtypeStruct((8, D), F32)],
        compiler_params=_cparams(("arbitrary",)),
    )(du2, dx1a, xhat1, rstd1, mix, mod, g1, b1)


def _merge_bwd(dmerged, proj, ya, yb, blk_ga, blk_gb, ts):
    S, D = ya.shape
    nb = D // COL_BLOCK

    def body(dm_ref, ga_ref, gb_ref, ya_ref, yb_ref, dya_ref, dyb_ref, dga_ref, dgb_ref):
        dm = dm_ref[...]
        sa, sb = _sigmoid(ga_ref[...]), _sigmoid(gb_ref[...])
        dya_ref[...] = (dm * sa).astype(BF16)
        dyb_ref[...] = (dm * sb).astype(BF16)
        dga_ref[...] = (dm * ya_ref[...] * sa * (1.0 - sa)).astype(BF16)
        dgb_ref[...] = (dm * yb_ref[...] * sb * (1.0 - sb)).astype(BF16)

    row = pl.BlockSpec((ts, D), lambda i: (i, 0))
    seg = lambda blk: pl.BlockSpec((pl.Element(ts), pl.Element(D)), lambda i: (i * ts, blk * COL_BLOCK))
    out = jax.ShapeDtypeStruct((S, D), BF16)
    return pl.pallas_call(
        body, name="merge_bwd", grid=(S // ts,),
        in_specs=[row, seg(blk_ga), seg(blk_gb), row, row],
        out_specs=[row] * 4,
        out_shape=[out] * 4,
        compiler_params=_cparams(("parallel",)),
    )(dmerged, proj, proj, ya, yb)


def _conv_bwd(dcbc, proj, w_conv, blk_b, blk_c, blk_x):
    S = proj.shape[0]
    D = w_conv.shape[1]
    nb = D // COL_BLOCK

    def body(d_ref, cb_ref, cc_ref, cx_ref, w_ref, dcb_ref, dcc_ref, dcx_ref, dw_ref):
        d, cc, cx = d_ref[...], cc_ref[...], cx_ref[...]
        z = cc * cx
        z1, z2 = _shift_rows(z, 1), _shift_rows(z, 2)
        conv = w_ref[2:3, :] * z + w_ref[1:2, :] * z1 + w_ref[0:1, :] * z2
        dcb_ref[...] = (d * conv).astype(BF16)
        dconv = d * cb_ref[...]
        dz = w_ref[2:3, :] * dconv + w_ref[1:2, :] * _shift_rows(dconv, -1) + w_ref[0:1, :] * _shift_rows(dconv, -2)
        dcc_ref[...] = (dz * cx).astype(BF16)
        dcx_ref[...] = (dz * cc).astype(BF16)
        dw_ref[...] = jnp.zeros_like(dw_ref)
        dw_ref[0:1, :] = jnp.sum(dconv * z2, axis=0, keepdims=True)
        dw_ref[1:2, :] = jnp.sum(dconv * z1, axis=0, keepdims=True)
        dw_ref[2:3, :] = jnp.sum(dconv * z, axis=0, keepdims=True)

    col = lambda off: pl.BlockSpec((S, COL_BLOCK), lambda j: (0, off + j))
    out = jax.ShapeDtypeStruct((S, D), BF16)
    return pl.pallas_call(
        body, name="conv_bwd", grid=(nb,),
        in_specs=[col(0), col(blk_b), col(blk_c), col(blk_x), pl.BlockSpec((CONV_K, COL_BLOCK), lambda j: (0, j))],
        out_specs=[col(0), col(0), col(0), pl.BlockSpec((8, COL_BLOCK), lambda j: (0, j))],
        out_shape=[out, out, out, jax.ShapeDtypeStruct((8, D), F32)],
        compiler_params=_cparams(("parallel",)),
    )(dcbc, proj, proj, proj, w_conv)


def _attn_bwd(qc, kc, vh, do, o, lse, T):
    H, S, _ = qc.shape
    n = S // T

    def body(q_ref, k_ref, v_ref, do_ref, o_ref, lse_ref, dq_ref, dk_ref, dv_ref, d_ref, dk_acc, dv_acc):
        j = pl.program_id(1)

        @pl.when(j == 0)
        def _():
            dq_ref[...] = jnp.zeros_like(dq_ref)
            d_ref[...] = jnp.sum(do_ref[...] * o_ref[...], axis=-1, keepdims=True)

        dk_acc[...] = jnp.zeros_like(dk_acc)
        dv_acc[...] = jnp.zeros_like(dv_acc)
        k, v = k_ref[0], v_ref[0]

        def step(i, masked):
            rows = pl.ds(pl.multiple_of(i * T, T), T)
            q = q_ref[0, rows, :]
            do = do_ref[rows, :].astype(BF16)
            s = lax.dot_general(q, k, NT_DIMS, preferred_element_type=F32) * ATTN_SCALE
            if masked:
                s = jnp.where(_diag_mask(T), s, NEG_INF)
            p = jnp.exp(s - lse_ref[0, rows, :])
            dv_acc[...] += lax.dot_general(p.astype(BF16), do, TN_DIMS, preferred_element_type=F32)
            dp = lax.dot_general(do, v, NT_DIMS, preferred_element_type=F32)
            ds = (p * (dp - d_ref[rows, :]) * ATTN_SCALE).astype(BF16)
            dk_acc[...] += lax.dot_general(ds, q, TN_DIMS, preferred_element_type=F32)
            dq_ref[0, rows, :] += jnp.dot(ds, k, preferred_element_type=F32)

        def above(i, carry):
            step(i, False)
            return carry

        step(j, True)
        lax.fori_loop(j + 1, n, above, 0)
        dk_ref[0] = dk_acc[...]
        dv_ref[0] = dv_acc[...]

    head = lambda w: pl.BlockSpec((1, S, w), lambda h, j: (h, 0, 0))
    blk = lambda w: pl.BlockSpec((1, T, w), lambda h, j: (h, j, 0))
    ospec = pl.BlockSpec((S, V_HEAD), lambda h, j: (0, h))
    return pl.pallas_call(
        body, name="attn_bwd", grid=(H, n),
        in_specs=[head(QK_CAT), blk(QK_CAT), blk(V_HEAD), ospec, ospec, head(1)],
        out_specs=[head(QK_CAT), blk(QK_CAT), blk(V_HEAD)],
        out_shape=[jax.ShapeDtypeStruct((H, S, QK_CAT), F32), jax.ShapeDtypeStruct((H, S, QK_CAT), F32),
                   jax.ShapeDtypeStruct((H, S, V_HEAD), F32)],
        scratch_shapes=[pltpu.VMEM((S, 1), F32), pltpu.VMEM((T, QK_CAT), F32), pltpu.VMEM((T, V_HEAD), F32)],
        compiler_params=_cparams(("parallel", "arbitrary")),
    )(qc, kc, vh, do, o, lse)


def _qk_bwd(dqc, dkc, dvh, cos_q, sin_q, cos_k, sin_k, ts):
    H, S, _ = dqc.shape
    pair = 2 * QK_CAT
    kv_w = QK_NOPE + V_HEAD

    def body(dqc_ref, dkc_ref, dvh_ref, cq_ref, sq_ref, ck_ref, sk_ref, dq_ref, dkv_ref, dkr_ref, q_buf, kr_buf):
        for p in range(H // 2):
            q_buf[:, :QK_CAT] = dqc_ref[2 * p]
            q_buf[:, QK_CAT:] = dqc_ref[2 * p + 1]
            g = q_buf[...]
            dq_ref[:, p * pair:(p + 1) * pair] = (
                g * cq_ref[...] - _rope_partner(g, QK_CAT, QK_NOPE) * sq_ref[...]).astype(BF16)
        kr_sum = jnp.zeros((ts, QK_ROPE), F32)
        for h in range(H):
            dkv_ref[:, h * kv_w:h * kv_w + QK_NOPE] = dkc_ref[h, :, 0:QK_NOPE].astype(BF16)
            dkv_ref[:, h * kv_w + QK_NOPE:(h + 1) * kv_w] = dvh_ref[h].astype(BF16)
            kr_sum = kr_sum + dkc_ref[h, :, QK_NOPE:QK_CAT]
        kr_buf[...] = jnp.zeros_like(kr_buf)
        kr_buf[:, 0:QK_ROPE] = kr_sum
        kr = kr_buf[...]
        dkr_ref[...] = (kr * ck_ref[...] - _rope_partner(kr, QK_ROPE, 0) * sk_ref[...]).astype(BF16)

    row = lambda w: pl.BlockSpec((ts, w), lambda i: (i, 0))
    head = lambda w: pl.BlockSpec((H, ts, w), lambda i: (0, i, 0))
    return pl.pallas_call(
        body, name="qk_bwd", grid=(S // ts,),
        in_specs=[head(QK_CAT), head(QK_CAT), head(V_HEAD), row(pair), row(pair), row(COL_BLOCK), row(COL_BLOCK)],
        out_specs=[row(H * QK_CAT), row(H * kv_w), row(COL_BLOCK)],
        out_shape=[jax.ShapeDtypeStruct((S, H * QK_CAT), BF16), jax.ShapeDtypeStruct((S, H * kv_w), BF16),
                   jax.ShapeDtypeStruct((S, COL_BLOCK), BF16)],
        scratch_shapes=[pltpu.VMEM((ts, pair), F32), pltpu.VMEM((ts, COL_BLOCK), F32)],
        compiler_params=_cparams(("parallel",)),
    )(dqc, dkc, dvh, cos_q, sin_q, cos_k, sin_k)


def _rms_bwd(dy, proj, g, blk, L, ts, name):
    S = proj.shape[0]

    def body(dy_ref, a_ref, g_ref, da_ref, dg_ref):
        i = pl.program_id(0)

        @pl.when(i == 0)
        def _():
            dg_ref[...] = jnp.zeros_like(dg_ref)

        a, dy = a_ref[...], dy_ref[...]
        r = lax.rsqrt(jnp.mean(a * a, axis=-1, keepdims=True) + RMS_EPS)
        dyh = dy * g_ref[...]
        da = r * dyh - a * (r * r * r) * jnp.mean(dyh * a, axis=-1, keepdims=True)
        da_ref[...] = da.astype(BF16)
        dg_ref[0:1, :] += jnp.sum(dy * a * r, axis=0, keepdims=True)

    return pl.pallas_call(
        body, name=name, grid=(S // ts,),
        in_specs=[pl.BlockSpec((ts, L), lambda i: (i, 0)), pl.BlockSpec((ts, L), lambda i: (i, blk)),
                  pl.BlockSpec((1, L), lambda i: (0, 0))],
        out_specs=[pl.BlockSpec((ts, L), lambda i: (i, 0)), pl.BlockSpec((8, L), lambda i: (0, 0))],
        out_shape=[jax.ShapeDtypeStruct((S, L), BF16), jax.ShapeDtypeStruct((8, L), F32)],
        compiler_params=_cparams(("arbitrary",)),
    )(dy, proj, g)


def _grad_x(du, dxa, x, mod, ts):
    S, D = x.shape

    def body(du_ref, dxa_ref, x_ref, mod_ref, dx_ref, vec_ref):
        i = pl.program_id(0)

        @pl.when(i == 0)
        def _():
            vec_ref[...] = jnp.zeros_like(vec_ref)

        du = du_ref[...]
        dx_ref[...] = dxa_ref[...] + du * (1.0 + mod_ref[1:2, :])
        vec_ref[0:1, :] += jnp.sum(du, axis=0, keepdims=True)
        vec_ref[1:2, :] += jnp.sum(du * x_ref[...], axis=0, keepdims=True)

    row = pl.BlockSpec((ts, D), lambda i: (i, 0))
    vec = lambda r: pl.BlockSpec((r, D), lambda i: (0, 0))
    return pl.pallas_call(
        body, name="grad_x", grid=(S // ts,),
        in_specs=[row, row, row, vec(6)],
        out_specs=[row, vec(8)],
        out_shape=[jax.ShapeDtypeStruct((S, D), F32), jax.ShapeDtypeStruct((8, D), F32)],
        compiler_params=_cparams(("arbitrary",)),
    )(du, dxa, x, mod)


def _adamw(w, g, m, v, name):
    R, C = w.shape
    tr = _tile(R, max(8, (1 << 19) // C), 8)
    c1 = 1.0 / (1.0 - ADAM_B1 ** ADAM_STEP)
    c2 = 1.0 / (1.0 - ADAM_B2 ** ADAM_STEP)

    def body(w_ref, g_ref, m_ref, v_ref, d_ref, nm_ref, nv_ref):
        g = g_ref[...]
        m = ADAM_B1 * m_ref[...] + (1.0 - ADAM_B1) * g
        v = ADAM_B2 * v_ref[...] + (1.0 - ADAM_B2) * (g * g)
        nm_ref[...] = m
        nv_ref[...] = v
        d_ref[...] = -ADAM_LR * ((m * c1) / (jnp.sqrt(v * c2) + ADAM_EPS) + ADAM_WD * w_ref[...])

    spec = pl.BlockSpec((tr, C), lambda i: (i, 0))
    out = jax.ShapeDtypeStruct((R, C), F32)
    return pl.pallas_call(
        body, name=name, grid=(R // tr,),
        in_specs=[spec] * 4, out_specs=[spec] * 3, out_shape=[out] * 3,
        compiler_params=_cparams(("parallel",)),
    )(w, g, m, v)


def _adamw_reduced(w, own, got, m, v, my_chip, name):
    R, C = w.shape
    tr = _tile(R, max(PACK_ROW_ALIGN, (1 << 18) // C), PACK_ROW_ALIGN)
    c1 = 1.0 / (1.0 - ADAM_B1 ** ADAM_STEP)
    c2 = 1.0 / (1.0 - ADAM_B2 ** ADAM_STEP)

    def body(chip_ref, w_ref, own_ref, g1_ref, g2_ref, g3_ref, m_ref, v_ref, g_ref, d_ref, nm_ref, nv_ref):
        g = own_ref[0].astype(F32) + g1_ref[0].astype(F32) + g2_ref[0].astype(F32) + g3_ref[0].astype(F32)
        m = ADAM_B1 * m_ref[...] + (1.0 - ADAM_B1) * g
        v = ADAM_B2 * v_ref[...] + (1.0 - ADAM_B2) * (g * g)
        g_ref[...] = g
        nm_ref[...] = m
        nv_ref[...] = v
        d_ref[...] = -ADAM_LR * ((m * c1) / (jnp.sqrt(v * c2) + ADAM_EPS) + ADAM_WD * w_ref[...])

    spec = pl.BlockSpec((tr, C), lambda i, chip: (i, 0))
    slot = lambda k: pl.BlockSpec((1, tr, C), lambda i, chip: (chip[0] ^ k, i, 0))
    out = jax.ShapeDtypeStruct((R, C), F32)
    return pl.pallas_call(
        body, name=name,
        grid_spec=pltpu.PrefetchScalarGridSpec(
            num_scalar_prefetch=1, grid=(R // tr,),
            in_specs=[spec, slot(0), slot(1), slot(2), slot(3), spec, spec],
            out_specs=[spec] * 4),
        out_shape=[out] * 4,
        compiler_params=_cparams(("parallel",)),
    )(my_chip, w, own, got, got, got, m, v)


def _my_place():
    return lax.axis_index("x"), lax.axis_index("y"), lax.axis_index("c")


def _peer(k):
    x, y, c = _my_place()
    return (x ^ ((k >> 2) & 1), y ^ ((k >> 1) & 1), c ^ (k & 1))


def _linear(place):
    return 4 * place[0] + 2 * place[1] + place[2]


def _ada_fwd(c_row, wconv_row, w_ada, b_row):
    D, CW = w_ada.shape
    WC = wconv_row.shape[-1]

    def body(c_ref, wc_ref, w_ref, b_ref, mod_ref, cact_ref, wcall_ref, send_buf, sems):
        me = _linear(_my_place())
        c = c_ref[0]
        cact_ref[me] = c * _sigmoid(c)
        wcall_ref[me] = wc_ref[0]

        def gather_copy(buf, k, grp):
            return pltpu.make_async_remote_copy(
                src_ref=buf.at[me], dst_ref=buf.at[me], send_sem=sems.at[0, grp, k], recv_sem=sems.at[1, grp, k],
                device_id=_peer(k), device_id_type=MESH_ID)

        def gather_recv(buf, k, grp):
            src = _linear(_peer(k))
            return pltpu.make_async_remote_copy(
                src_ref=buf.at[src], dst_ref=buf.at[src], send_sem=sems.at[0, grp, k], recv_sem=sems.at[1, grp, k],
                device_id=_peer(k), device_id_type=MESH_ID)

        for k in range(1, N_DEV):
            gather_copy(cact_ref, k, 0).start()
            gather_copy(wcall_ref, k, 1).start()
        for k in range(1, N_DEV):
            gather_recv(cact_ref, k, 0).wait_recv()
            gather_recv(wcall_ref, k, 1).wait_recv()
        for k in range(1, N_DEV):
            gather_copy(cact_ref, k, 0).wait_send()
            gather_copy(wcall_ref, k, 1).wait_send()

        cact = jnp.concatenate([cact_ref[b] for b in range(N_DEV)], axis=0)
        mod_all = jnp.dot(cact.astype(BF16), w_ref[...].astype(BF16), preferred_element_type=F32) + b_ref[0]
        for b in range(N_DEV):
            send_buf[b] = mod_all[b:b + 1, :]
        mod_ref[me] = send_buf[me]

        def scatter_copy(k):
            dst = _linear(_peer(k))
            return pltpu.make_async_remote_copy(
                src_ref=send_buf.at[dst], dst_ref=mod_ref.at[me], send_sem=sems.at[0, 2, k], recv_sem=sems.at[1, 2, k],
                device_id=_peer(k), device_id_type=MESH_ID)

        def scatter_recv(k):
            src = _linear(_peer(k))
            return pltpu.make_async_remote_copy(
                src_ref=send_buf.at[src], dst_ref=mod_ref.at[src], send_sem=sems.at[0, 2, k], recv_sem=sems.at[1, 2, k],
                device_id=_peer(k), device_id_type=MESH_ID)

        for k in range(1, N_DEV):
            scatter_copy(k).start()
        for k in range(1, N_DEV):
            scatter_recv(k).wait_recv()
        for k in range(1, N_DEV):
            scatter_copy(k).wait_send()

    vmem = pl.BlockSpec(memory_space=pltpu.VMEM)
    return pl.pallas_call(
        body, name="ada_fwd",
        in_specs=[vmem] * 4, out_specs=[vmem] * 3,
        out_shape=[jax.ShapeDtypeStruct((N_DEV, 1, CW), F32), jax.ShapeDtypeStruct((N_DEV, 1, D), F32),
                   jax.ShapeDtypeStruct((N_DEV, 1, WC), F32)],
        scratch_shapes=[pltpu.VMEM((N_DEV, 1, CW), F32), pltpu.SemaphoreType.DMA((2, 3, N_DEV))],
        compiler_params=pltpu.CompilerParams(vmem_limit_bytes=VMEM_LIMIT),
    )(c_row, wconv_row, w_ada, b_row)


def _ada_bwd(payload, cact_t, deps=()):
    NCH, _, CW = payload.shape
    D = cact_t.shape[0]

    def body(p_ref, ct_ref, *rest):
        sum_ref, gw_ref, all_ref, sems = rest[-4:]
        me = _linear(_my_place())
        all_ref[me] = p_ref[...]

        def copy(k, slot):
            return pltpu.make_async_remote_copy(
                src_ref=all_ref.at[slot], dst_ref=all_ref.at[slot], send_sem=sems.at[0, k], recv_sem=sems.at[1, k],
                device_id=_peer(k), device_id_type=MESH_ID)

        for k in range(1, N_DEV):
            copy(k, me).start()
        for k in range(1, N_DEV):
            copy(k, _linear(_peer(k))).wait_recv()
        for k in range(1, N_DEV):
            copy(k, me).wait_send()

        total = all_ref[0]
        for b in range(1, N_DEV):
            total = total + all_ref[b]
        sum_ref[...] = total

        ct = ct_ref[...].astype(BF16).astype(F32)
        gw = jnp.zeros((D, CW), F32)
        for b in range(N_DEV):
            dm = all_ref[b, me].astype(BF16).astype(F32)
            gw = gw + ct[:, b:b + 1] * dm
        gw_ref[...] = gw

    vmem = pl.BlockSpec(memory_space=pltpu.VMEM)
    return pl.pallas_call(
        body, name="ada_bwd",
        in_specs=[vmem, vmem] + [ANY_SPEC] * len(deps), out_specs=[vmem, vmem],
        out_shape=[jax.ShapeDtypeStruct((NCH, 1, CW), F32), jax.ShapeDtypeStruct((D, CW), F32)],
        scratch_shapes=[pltpu.VMEM((N_DEV, NCH, 1, CW), F32), pltpu.SemaphoreType.DMA((2, N_DEV))],
        compiler_params=pltpu.CompilerParams(vmem_limit_bytes=VMEM_LIMIT),
    )(payload, cact_t, *deps)


def _exchange_in_chip(parts):
    W = len(parts)

    def body(*refs):
        p_refs, got_refs, (send_sems, recv_sems) = refs[:W], refs[W:2 * W], refs[2 * W:]
        x, y, c = _my_place()
        sibling = (x, y, 1 - c)
        copies = []
        for w in range(W):
            for q in range(4):
                copies.append(pltpu.make_async_remote_copy(
                    src_ref=p_refs[w].at[2 * q + (1 - c)], dst_ref=got_refs[w].at[q],
                    send_sem=send_sems.at[4 * w + q], recv_sem=recv_sems.at[4 * w + q],
                    device_id=sibling, device_id_type=MESH_ID))
        for cp in copies:
            cp.start()
        for cp in copies:
            cp.wait_recv()
        for cp in copies:
            cp.wait_send()

    return pl.pallas_call(
        body, name="grad_exchange_in_chip",
        in_specs=[HBM_SPEC] * W, out_specs=[HBM_SPEC] * W,
        out_shape=[jax.ShapeDtypeStruct((4,) + p.shape[1:], p.dtype) for p in parts],
        scratch_shapes=[pltpu.SemaphoreType.DMA((4 * W,)), pltpu.SemaphoreType.DMA((4 * W,))],
    )(*parts)


def _pair_sum(parts, got, core):
    _, R, C = parts.shape
    tr = _tile(R, max(PACK_ROW_ALIGN, PAIR_SUM_BLOCK // C), PACK_ROW_ALIGN)

    def body(c_ref, p_ref, g_ref, o_ref):
        o_ref[...] = (p_ref[...].astype(F32) + g_ref[...].astype(F32)).astype(o_ref.dtype)

    return pl.pallas_call(
        body, name="grad_pair_sum",
        grid_spec=pltpu.PrefetchScalarGridSpec(
            num_scalar_prefetch=1, grid=(4, R // tr),
            in_specs=[pl.BlockSpec((1, tr, C), lambda q, i, c_ref: (2 * q + c_ref[0], i, 0)),
                      pl.BlockSpec((1, tr, C), lambda q, i, c_ref: (q, i, 0))],
            out_specs=pl.BlockSpec((1, tr, C), lambda q, i, c_ref: (q, i, 0))),
        out_shape=jax.ShapeDtypeStruct((4, R, C), parts.dtype),
        compiler_params=_cparams(("parallel", "parallel")),
    )(core, parts, got)


HBM_SPEC = pl.BlockSpec(memory_space=pltpu.HBM)
SEM_SPEC = pl.BlockSpec(memory_space=pltpu.SEMAPHORE)
ANY_SPEC = pl.BlockSpec(memory_space=pl.ANY)
SPLIT_EFFECT = pltpu.SideEffectType.DATAFLOW_SIDE_EFFECTING


def _landing_zone(shape, dtype):
    return pltpu.with_memory_space_constraint(lax.empty(shape, dtype), pltpu.HBM)


def _split_start(name, arrays, lands, after, copies_of, per_array):
    W = len(arrays)

    def body(*refs):
        x_refs, land_refs = refs[:W], refs[W:2 * W]
        send_sems, recv_sems = refs[2 * W + 1], refs[2 * W + 2]
        token = refs[-1]
        k = 0
        for w in range(W):
            for src, dst, dev in copies_of(w, x_refs[w], land_refs[w]):
                pltpu.make_async_remote_copy(src_ref=src, dst_ref=dst, send_sem=send_sems.at[k], recv_sem=recv_sems.at[k],
                                             device_id=dev, device_id_type=MESH_ID).start()
                k += 1
        token[...] = jnp.zeros_like(token)

    n_copies = per_array * W
    hbm_of = lambda xs: tuple(pltpu.HBM(a.shape, a.dtype) for a in xs)
    out = pl.pallas_call(
        body, name=name,
        out_shape=(pltpu.SemaphoreType.DMA((n_copies,)), pltpu.SemaphoreType.DMA((n_copies,)))
        + hbm_of(arrays) + hbm_of(lands) + (jax.ShapeDtypeStruct((8, LANE), F32),),
        in_specs=(HBM_SPEC,) * (2 * W) + (ANY_SPEC,),
        out_specs=(SEM_SPEC, SEM_SPEC) + (HBM_SPEC,) * (2 * W) + (pl.BlockSpec(memory_space=pltpu.VMEM),),
        input_output_aliases={i: 2 + i for i in range(2 * W)},
        compiler_params=pltpu.CompilerParams(has_side_effects=SPLIT_EFFECT),
    )(*[pltpu.with_memory_space_constraint(a, pltpu.HBM) for a in arrays], *lands, after)
    return out[0], out[1], list(out[2:2 + W]), list(out[2 + W:2 + 2 * W]), out[-1]


def _split_wait(name, state, after, copies_of):
    send_sems, recv_sems, arrays, lands, _ = state
    W = len(arrays)
    after = tuple(after) if isinstance(after, (tuple, list)) else (after,)

    def body(*refs):
        x_refs, land_refs = refs[:W], refs[W:2 * W]
        send_sems, recv_sems = refs[2 * W], refs[2 * W + 1]
        k = 0
        for w in range(W):
            for src, dst, dev in copies_of(w, x_refs[w], land_refs[w]):
                cp = pltpu.make_async_remote_copy(src_ref=src, dst_ref=dst, send_sem=send_sems.at[k],
                                                  recv_sem=recv_sems.at[k], device_id=dev, device_id_type=MESH_ID)
                cp.wait_send()
                cp.wait_recv()
                k += 1

    out = pl.pallas_call(
        body, name=name,
        out_shape=tuple(pltpu.HBM(a.shape, a.dtype) for a in arrays + lands),
        in_specs=(HBM_SPEC,) * (2 * W) + (SEM_SPEC, SEM_SPEC) + (ANY_SPEC,) * len(after),
        out_specs=(HBM_SPEC,) * (2 * W),
        input_output_aliases={i: i for i in range(2 * W)},
        compiler_params=pltpu.CompilerParams(has_side_effects=SPLIT_EFFECT),
    )(*arrays, *lands, send_sems, recv_sems, *after)
    return list(out[:W]), list(out[W:])


def _scatter_copies(w, p_ref, land_ref):
    x, y, c = _my_place()
    my_chip = 2 * x + y
    return [(p_ref.at[2 * (x ^ (k >> 1)) + (y ^ (k & 1))], land_ref.at[my_chip], (x ^ (k >> 1), y ^ (k & 1), c))
            for k in range(1, 4)]


def _gather_copies(w, x_ref, land_ref):
    x, y, c = _my_place()
    me = _linear((x, y, c))
    devs = [(x, y, 1 - c)] + [(x ^ (k >> 1), y ^ (k & 1), c) for k in range(1, 4)]
    return [(x_ref, land_ref.at[me], d) for d in devs]


def _gather_forward(lands, name):
    W = len(lands)

    def body(*refs):
        land_refs, out_refs, (send_sems, recv_sems) = refs[:W], refs[W:2 * W], refs[2 * W:]
        x, y, c = _my_place()
        sibling = (x, y, 1 - c)
        sends, arrivals = [], []
        for w in range(W):
            for k in range(1, 4):
                px, py = x ^ (k >> 1), y ^ (k & 1)
                landed, theirs = _linear((px, py, c)), out_refs[w].at[_linear((px, py, 1 - c))]
                sem = 3 * w + k - 1
                sends.append(pltpu.make_async_remote_copy(
                    src_ref=land_refs[w].at[landed], dst_ref=out_refs[w].at[landed],
                    send_sem=send_sems.at[sem], recv_sem=recv_sems.at[sem], device_id=sibling, device_id_type=MESH_ID))
                arrivals.append(pltpu.make_async_remote_copy(
                    src_ref=theirs, dst_ref=theirs, send_sem=send_sems.at[sem], recv_sem=recv_sems.at[sem],
                    device_id=sibling, device_id_type=MESH_ID))
        for cp in sends:
            cp.start()
        for cp in arrivals:
            cp.wait_recv()
        for cp in sends:
            cp.wait_send()

    return pl.pallas_call(
        body, name=name,
        in_specs=[HBM_SPEC] * W, out_specs=[HBM_SPEC] * W,
        out_shape=[jax.ShapeDtypeStruct(l.shape, l.dtype) for l in lands],
        input_output_aliases={i: i for i in range(W)},
        scratch_shapes=[pltpu.SemaphoreType.DMA((3 * W,)), pltpu.SemaphoreType.DMA((3 * W,))],
    )(*lands)


def _with_own_slot(gathered, shard):
    return lax.dynamic_update_index_in_dim(gathered, shard[None], _linear(_my_place()), axis=0)


def _in_chip_copies(w, p_ref, land_ref):
    x, y, c = _my_place()
    return [(p_ref.at[2 * q + (1 - c)], land_ref.at[q], (x, y, 1 - c)) for q in range(4)]


def _in_chip_start(parts, tag):
    lands = [_landing_zone((4,) + p.shape[1:], p.dtype) for p in parts]
    return _split_start("grad_in_chip_start_" + tag, parts, lands, parts[0], _in_chip_copies, 4)


def _reduce_scatter_begin(parts, tag, in_chip_state=None, after=()):
    if in_chip_state is None:
        got = _exchange_in_chip(parts)
    else:
        parts, got = _split_wait("grad_in_chip_wait_" + tag, in_chip_state, after, _in_chip_copies)
    core = lax.axis_index("c").astype(jnp.int32).reshape(1)
    chip_parts = [_pair_sum(p, g, core) for p, g in zip(parts, got)]
    lands = [_landing_zone(p.shape, p.dtype) for p in chip_parts]
    return _split_start("grad_scatter_start_" + tag, chip_parts, lands, got[0], _scatter_copies, 3)


def _reduce_scatter_end(state, after, tag):
    return _split_wait("grad_scatter_wait_" + tag, state, after, _scatter_copies)


def kernel(x, c, positions, w_ada, b_ada, w_in, g_q_a, w_q_b, g_kv_a, w_kv_b, w_o_a, w_conv, w_o_b, w_o, ln1_g, ln1_b, w_ffn_in, w_ffn_out, ln2_g, ln2_b, loss_target, m_w_ada, m_b_ada, m_w_in, m_g_q_a, m_w_q_b, m_g_kv_a, m_w_kv_b, m_w_o_a, m_w_conv, m_w_o_b, m_w_o, m_ln1_g, m_ln1_b, m_w_ffn_in, m_w_ffn_out, m_ln2_g, m_ln2_b, v_w_ada, v_b_ada, v_w_in, v_g_q_a, v_w_q_b, v_g_kv_a, v_w_kv_b, v_w_o_a, v_w_conv, v_w_o_b, v_w_o, v_ln1_g, v_ln1_b, v_w_ffn_in, v_w_ffn_out, v_ln2_g, v_ln2_b):
    x2, tgt = x[0], loss_target[0]
    S, D = x2.shape
    Lq, Lkv = g_q_a.shape[1], g_kv_a.shape[1]
    H = w_q_b.shape[2] * N_DEV // QK_CAT
    F = w_ffn_out.shape[1] * N_DEV
    assert Lq == Lkv and (Lq + Lkv) % COL_BLOCK == 0 and D % COL_BLOCK == 0
    front = Lq + Lkv + QK_ROPE
    front_pad = _round_up(front, COL_BLOCK)
    kr_blk = (Lq + Lkv) // COL_BLOCK
    blk_b = front_pad // COL_BLOCK
    nblk = D // COL_BLOCK
    blk_c, blk_x, blk_ga, blk_gb = blk_b + nblk, blk_b + 2 * nblk, blk_b + 3 * nblk, blk_b + 4 * nblk
    ts = _tile(S, 256, 8)
    T = _tile(S, min(512, S // 2), CHUNK)
    tb = _tile(F, 2816)
    me = _linear(_my_place())

    cw = w_ada.shape[2]
    b_mine = lax.dynamic_slice(b_ada, (0, me * cw), (1, cw)).reshape(1, 1, cw)
    mod_blocks, cact_all, wconv_all = _ada_fwd(c.reshape(1, 1, D), w_conv[0].reshape(1, 1, -1), w_ada[0], b_mine)
    mod = mod_blocks.reshape(6, D)
    cact_all = cact_all.reshape(N_DEV, D)
    w_conv_full = wconv_all.reshape(N_DEV, CONV_K, -1).transpose(1, 0, 2).reshape(CONV_K, D)

    landing = lambda shards: [_landing_zone((N_DEV,) + s.shape, BF16) for s in shards]
    gathered = lambda lands, shards, tag: [_with_own_slot(g, s) for g, s in
                                           zip(_gather_forward(lands, tag + "_gather_forward"), shards)]
    half = D // 2
    w_in_b = w_in[0].astype(BF16)
    first, second = [w_in_b[:half]], [w_in_b[half:], w_q_b[0].astype(BF16), w_kv_b[0].astype(BF16)]
    mid = [w[0].astype(BF16) for w in (w_o_a, w_o_b, w_o)]
    last = [w[0].astype(BF16) for w in (w_ffn_in, w_ffn_out)]
    first_state = _split_start("first_gather_start", first, landing(first), mod_blocks, _gather_copies, 4)
    second_state = _split_start("second_gather_start", second, landing(second), first_state[4], _gather_copies, 4)
    u = _modulate_in(x2, mod, ts)

    first_shards, first_lands = _split_wait("first_gather_wait", first_state, (u, second_state[4]), _gather_copies)
    (g_in_top,) = gathered(first_lands, first_shards, "first")
    w_in_top = _assemble_w_in(g_in_top, front, front_pad, D, 0)
    proj_top = _matmul(u, w_in_top, "nn", F32, "proj_top", k_rows=(0, half))
    second_shards, second_lands = _split_wait("second_gather_wait", second_state, (proj_top,), _gather_copies)
    g_in_bottom, wq_s, wkv_s = gathered(second_lands, second_shards, "second")
    mid_state = _split_start("mid_gather_start", mid, landing(mid), g_in_bottom, _gather_copies, 4)
    last_state = _split_start("last_gather_start", last, landing(last), mid_state[4], _gather_copies, 4)
    w_in_p = _assemble_w_in(g_in_bottom, front, front_pad, D, half, into=w_in_top)

    inv_freq = 1.0 / (ROPE_THETA ** (jnp.arange(0, QK_ROPE, 2, dtype=F32) / QK_ROPE))
    ang = positions[0].astype(F32)[:, None] * inv_freq
    cos2 = jnp.concatenate([jnp.cos(ang), jnp.cos(ang)], axis=-1)
    sin2 = jnp.concatenate([jnp.sin(ang), jnp.sin(ang)], axis=-1)
    one, zero = jnp.ones((S, QK_NOPE), F32), jnp.zeros((S, QK_NOPE), F32)
    cos_q, sin_q = jnp.concatenate([one, cos2, one, cos2], axis=-1), jnp.concatenate([zero, sin2, zero, sin2], axis=-1)
    cos_k, sin_k = jnp.tile(cos2, (1, COL_BLOCK // QK_ROPE)), jnp.tile(sin2, (1, COL_BLOCK // QK_ROPE))

    proj = _matmul(u, w_in_p, "nn", F32, "proj", k_rows=(half, half), init=proj_top, deps=(last_state[4],))
    qn = _rms_fwd(proj, g_q_a, 0, Lq, ts, "rms_q")
    kvn = _rms_fwd(proj, g_kv_a, 1, Lkv, ts, "rms_kv")
    q = _matmul(qn, wq_s, "nn", F32, "q_up")
    kv = _matmul(kvn, wkv_s, "nn", F32, "kv_up")
    qc, kc, vh = _qk_prep(q, kv, proj, kr_blk, cos_q, sin_q, cos_k, sin_k, H, ts)
    attn, lse = _attn_fwd(qc, kc, vh, T)
    mid_shards, mid_lands = _split_wait("mid_gather_wait", mid_state, lse, _gather_copies)
    w_oa_f, w_ob_f, w_o_f = [g.reshape(-1, D) for g in gathered(mid_lands, mid_shards, "mid")]
    ya = _matmul(attn, w_oa_f, "nn", F32, "attn_out")
    cbc = _conv_fwd(proj, w_conv_full, blk_b, blk_c, blk_x)
    yb = _matmul(cbc, w_ob_f, "nn", F32, "conv_out")
    merged = _merge_fwd(proj, ya, yb, blk_ga, blk_gb, ts)
    mix = _matmul(merged, w_o_f, "nn", F32, "mix_out")
    xhat1, rstd1, u2 = _ln1_fwd(x2, mix, mod, ln1_g, ln1_b, ts)
    last_shards, last_lands = _split_wait("last_gather_wait", last_state, u2, _gather_copies)
    w_fi_s, g_fo = gathered(last_lands, last_shards, "last")
    w_fo_f = g_fo.reshape(F, D)
    hh = _matmul(u2, w_fi_s, "nn", F32, "ffn_in")
    act = _swiglu_fwd(hh, ts, tb)
    ffn = _matmul(act, w_fo_f, "nn", F32, "ffn_out")
    loss_part, dffn, dx1a, vec2 = _ln2_loss(xhat1, ffn, tgt, mod, ln1_g, ln1_b, ln2_g, ln2_b, ts)
    loss = lax.psum(loss_part[0, 0], AXES)

    gw_fo = _matmul(act, dffn, "tn", BF16, "grad_w_ffn_out")
    da = _matmul(dffn, w_fo_f, "nt", F32, "d_act")
    dh = _swiglu_bwd(da, hh, ts, tb)
    gw_fi = _matmul(u2, dh, "tn", BF16, "grad_w_ffn_in", out_shards=True)
    ffn_in_chip = _in_chip_start([gw_fi, gw_fo.reshape(N_DEV, -1, D)], "ffn")
    du2 = _matmul(dh, w_fi_s, "nt", F32, "d_u2", deps=(ffn_in_chip[4],))
    ffn_state = _reduce_scatter_begin(None, "ffn", ffn_in_chip, after=(du2,))
    dxa, dmix, vec1 = _ln1_bwd(du2, dx1a, xhat1, rstd1, mix, mod, ln1_g, ln1_b, ts)
    gw_o = _matmul(merged, dmix, "tn", BF16, "grad_w_o", deps=(ffn_state[4],))
    dmerged = _matmul(dmix, w_o_f, "nt", F32, "d_merged")
    dya, dyb, dga, dgb = _merge_bwd(dmerged, proj, ya, yb, blk_ga, blk_gb, ts)
    gw_ob = _matmul(cbc, dyb, "tn", BF16, "grad_w_o_b")
    dcbc = _matmul(dyb, w_ob_f, "nt", F32, "d_conv")
    dcb, dcc, dcx, dwconv = _conv_bwd(dcbc, proj, w_conv_full, blk_b, blk_c, blk_x)
    gw_oa = _matmul(attn, dya, "tn", BF16, "grad_w_o_a")
    mix_in_chip = _in_chip_start([g.reshape(N_DEV, -1, D) for g in (gw_oa, gw_ob, gw_o)], "mix")
    dattn = _matmul(dya, w_oa_f, "nt", F32, "d_attn", deps=(mix_in_chip[4],))
    dqc, dkc, dvh = _attn_bwd(qc, kc, vh, dattn, attn, lse, T)
    ffn_own, ffn_got = _reduce_scatter_end(ffn_state, dqc, "ffn")
    mix_state = _reduce_scatter_begin(None, "mix", mix_in_chip, after=(dqc,))
    dq, dkv, dkr = _qk_bwd(dqc, dkc, dvh, cos_q, sin_q, cos_k, sin_k, ts)
    gw_qb = _matmul(qn, dq, "tn", BF16, "grad_w_q_b", out_shards=True, deps=(mix_state[4],))
    dqn = _matmul(dq, wq_s, "nt", F32, "d_qn")
    gw_kvb = _matmul(kvn, dkv, "tn", BF16, "grad_w_kv_b", out_shards=True)
    dkvn = _matmul(dkv, wkv_s, "nt", F32, "d_kvn")
    dqa, dgq = _rms_bwd(dqn, proj, g_q_a, 0, Lq, ts, "rms_q_bwd")
    dkva, dgkv = _rms_bwd(dkvn, proj, g_kv_a, 1, Lkv, ts, "rms_kv_bwd")
    dproj = jnp.concatenate([dqa, dkva, dkr, dcb, dcc, dcx, dga, dgb], axis=1)
    gw_in_p = _matmul(u, dproj, "tn", BF16, "grad_w_in")
    mix_own, mix_got = _reduce_scatter_end(mix_state, gw_in_p, "mix")
    in_state = _reduce_scatter_begin([_split_w_in(gw_in_p, front, front_pad), gw_qb, gw_kvb], "in")
    du = _matmul(dproj, w_in_p, "nt", F32, "d_u", deps=(in_state[4],))
    grad_x, vec0 = _grad_x(du, dxa, x2, mod, ts)

    my_chip = (2 * lax.axis_index("x") + lax.axis_index("y")).astype(jnp.int32).reshape(1)
    arrived = {}
    for nm, w, m, v, own, got in (
            ("w_ffn_in", w_ffn_in, m_w_ffn_in, v_w_ffn_in, ffn_own[0], ffn_got[0]),
            ("w_ffn_out", w_ffn_out, m_w_ffn_out, v_w_ffn_out, ffn_own[1], ffn_got[1]),
            ("w_o_a", w_o_a, m_w_o_a, v_w_o_a, mix_own[0], mix_got[0]),
            ("w_o_b", w_o_b, m_w_o_b, v_w_o_b, mix_own[1], mix_got[1]),
            ("w_o", w_o, m_w_o, v_w_o, mix_own[2], mix_got[2])):
        arrived[nm] = [a[None] for a in _adamw_reduced(w[0], own, got, m[0], v[0], my_chip, "adamw_" + nm)]

    dmod = jnp.concatenate([vec0[0], vec0[1], vec1[4], vec1[0], vec1[1], vec2[2]])
    small = jnp.concatenate([dmod, dgq[0], dgkv[0], vec1[2], vec1[3], vec2[0], vec2[1], dwconv[:CONV_K].reshape(-1)])
    n_small = small.shape[0]
    nch = _round_up(n_small, cw) // cw
    payload = jnp.pad(small, (0, nch * cw - n_small)).reshape(nch, 1, cw)
    summed, g_w_ada = _ada_bwd(payload, cact_all.T, deps=[res[1] for res in arrived.values()])
    summed = summed.reshape(-1)
    offs = [0, 6 * D, 6 * D + Lq, 6 * D + Lq + Lkv]
    offs += [offs[-1] + D * k for k in range(1, 5)]
    g_b_ada = summed[offs[0]:offs[1]].reshape(1, -1)
    g_gq = summed[offs[1]:offs[2]].reshape(1, -1)
    g_gkv = summed[offs[2]:offs[3]].reshape(1, -1)
    g_ln1g, g_ln1b, g_ln2g, g_ln2b = [summed[offs[3 + k]:offs[4 + k]].reshape(1, -1) for k in range(4)]
    wc = w_conv.shape[2]
    g_wconv = lax.dynamic_slice(summed[offs[7]:offs[7] + CONV_K * D].reshape(CONV_K, D), (0, me * wc), (CONV_K, wc))

    names = ["w_ada", "b_ada", "w_in", "g_q_a", "w_q_b", "g_kv_a", "w_kv_b", "w_o_a", "w_conv", "w_o_b", "w_o",
             "ln1_g", "ln1_b", "w_ffn_in", "w_ffn_out", "ln2_g", "ln2_b"]
    weights = [w_ada, b_ada, w_in, g_q_a, w_q_b, g_kv_a, w_kv_b, w_o_a, w_conv, w_o_b, w_o, ln1_g, ln1_b,
               w_ffn_in, w_ffn_out, ln2_g, ln2_b]
    moms = [m_w_ada, m_b_ada, m_w_in, m_g_q_a, m_w_q_b, m_g_kv_a, m_w_kv_b, m_w_o_a, m_w_conv, m_w_o_b, m_w_o,
            m_ln1_g, m_ln1_b, m_w_ffn_in, m_w_ffn_out, m_ln2_g, m_ln2_b]
    vels = [v_w_ada, v_b_ada, v_w_in, v_g_q_a, v_w_q_b, v_g_kv_a, v_w_kv_b, v_w_o_a, v_w_conv, v_w_o_b, v_w_o,
            v_ln1_g, v_ln1_b, v_w_ffn_in, v_w_ffn_out, v_ln2_g, v_ln2_b]
    grad_of = {"w_ada": g_w_ada, "b_ada": g_b_ada, "g_q_a": g_gq, "g_kv_a": g_gkv, "w_conv": g_wconv,
               "ln1_g": g_ln1g, "ln1_b": g_ln1b, "ln2_g": g_ln2g, "ln2_b": g_ln2b}
    state_of = dict(zip(names, zip(weights, moms, vels)))
    results = dict(arrived)

    def update(nm, reduced=None):
        w, m, v = state_of[nm]
        shp = w.shape
        w2 = w.reshape(shp[-2], shp[-1]) if w.ndim == 3 else w
        m2, v2 = m.reshape(w2.shape), v.reshape(w2.shape)
        if reduced is None:
            g2 = grad_of[nm].reshape(w2.shape)
            res = (g2,) + tuple(_adamw(w2, g2, m2, v2, "adamw_" + nm))
        else:
            res = _adamw_reduced(w2, reduced[0], reduced[1], m2, v2, my_chip, "adamw_" + nm)
        results[nm] = [a.reshape(shp) for a in res]

    for nm in grad_of:
        update(nm)
    in_own, in_got = _reduce_scatter_end(in_state, [res[1] for res in results.values()], "in")
    for nm, own, got in zip(("w_in", "w_q_b", "w_kv_b"), in_own, in_got):
        update(nm, (own, got))
    outs = [[results[nm][k] for nm in names] for k in range(4)]
    return (loss, grad_x.reshape(x.shape), *outs[0], *outs[1], *outs[2], *outs[3])
```

```python
import functools

import jax
import jax.numpy as jnp
from jax import lax
from jax.experimental import pallas as pl
from jax.experimental.pallas import tpu as pltpu

F32 = jnp.float32
BF16 = jnp.bfloat16
MESH_ID = pl.DeviceIdType.MESH
AXES = ("x", "y", "c")
N_DEV = 8

CHUNK = 64
QK_NOPE = 128
QK_ROPE = 64
V_HEAD = 128
QK_CAT = QK_NOPE + QK_ROPE
ROPE_THETA = 10000.0
ATTN_SCALE = (QK_NOPE + QK_ROPE) ** -0.5
CONV_K = 3
DEEPNORM_ALPHA = 2.0 ** 0.25
LN_EPS = 1e-5
RMS_EPS = 1e-6
NEG_INF = -1e30

ADAM_LR = 0.001
ADAM_B1 = 0.9
ADAM_B2 = 0.999
ADAM_EPS = 1e-08
ADAM_WD = 0.01
ADAM_STEP = 10

LANE = 128
COL_BLOCK = 256
PACK_ROW_ALIGN = 16
PAIR_SUM_BLOCK = 1 << 20
VMEM_LIMIT = 48 * 1024 * 1024


def _round_up(n, m):
    return (n + m - 1) // m * m


def _tile(n, pref, align=LANE):
    best = None
    t = align
    while t <= min(n, pref):
        if n % t == 0:
            best = t
        t += align
    return best if best is not None else n


def _cparams(sem=None):
    return pltpu.CompilerParams(dimension_semantics=sem, vmem_limit_bytes=VMEM_LIMIT)


def _sigmoid(x):
    return 0.5 * jnp.tanh(0.5 * x) + 0.5


def _matmul(a, b, mode, out_dtype, name, tm=1024, tn=1024, tk=2048, deps=(), out_shards=False, k_rows=None,
            init=None):
    b_shards = b.ndim == 3
    n = b.shape[2] if b_shards else (b.shape[1] // N_DEV if out_shards else None)
    if mode == "nn":
        (M, K), (K2, N) = a.shape, (b.shape[1], N_DEV * n) if b_shards else b.shape
    elif mode == "nt":
        (M, K), (N, K2) = a.shape, (b.shape[1], N_DEV * n) if b_shards else b.shape
    else:
        (K, M), (K2, N) = a.shape, b.shape
    assert K == K2, (a.shape, b.shape, mode)
    tm = _tile(M, tm)
    tn = n if (mode != "nt" and n is not None) else _tile(N, tn)
    k_row0, k_len = k_rows if k_rows is not None else (0, K)
    tk = n if (mode == "nt" and b_shards) else _tile(k_len, tk)
    nk, k0 = k_len // tk, k_row0 // tk
    if mode == "nn":
        a_spec = pl.BlockSpec((tm, tk), lambda i, j, k: (i, k0 + k))
        b_spec = (pl.BlockSpec((1, tk, n), lambda i, j, k: (j, k, 0)) if b_shards
                  else pl.BlockSpec((tk, tn), lambda i, j, k: (k0 + k, j)))
        dims = (((1,), (0,)), ((), ()))
    elif mode == "nt":
        a_spec = pl.BlockSpec((tm, tk), lambda i, j, k: (i, k))
        b_spec = (pl.BlockSpec((1, tn, n), lambda i, j, k: (k, j, 0)) if b_shards
                  else pl.BlockSpec((tn, tk), lambda i, j, k: (j, k)))
        dims = (((1,), (1,)), ((), ()))
    else:
        a_spec = pl.BlockSpec((tk, tm), lambda i, j, k: (k, i))
        b_spec = pl.BlockSpec((tk, tn), lambda i, j, k: (k, j))
        dims = (((0,), (0,)), ((), ()))
    if out_shards:
        out_spec = pl.BlockSpec((1, tm, n), lambda i, j, k: (j, i, 0))
        out_shape = jax.ShapeDtypeStruct((N_DEV, M, n), out_dtype)
    else:
        out_spec = pl.BlockSpec((tm, tn), lambda i, j, k: (i, j))
        out_shape = jax.ShapeDtypeStruct((M, N), out_dtype)

    def product(a_ref, b_ref):
        b_blk = b_ref[0] if b_shards else b_ref[...]
        return lax.dot_general(a_ref[...].astype(BF16), b_blk.astype(BF16), dims, preferred_element_type=F32)

    def write(o_ref, value):
        if out_shards:
            o_ref[0] = value.astype(o_ref.dtype)
        else:
            o_ref[...] = value.astype(o_ref.dtype)

    def body_whole_k(a_ref, b_ref, *rest):
        value = product(a_ref, b_ref)
        write(rest[-1], value if init is None else value + rest[0][...])

    def body_split_k(a_ref, b_ref, *rest):
        o_ref, acc_ref = rest[-2:]
        k = pl.program_id(2)

        @pl.when(k == 0)
        def _():
            acc_ref[...] = jnp.zeros_like(acc_ref) if init is None else rest[0][...]

        acc_ref[...] += product(a_ref, b_ref)

        @pl.when(k == nk - 1)
        def _():
            write(o_ref, acc_ref[...])

    return pl.pallas_call(
        body_whole_k if nk == 1 else body_split_k, name=name, grid=(M // tm, N // tn, nk),
        in_specs=[a_spec, b_spec] + ([] if init is None else [out_spec]) + [ANY_SPEC] * len(deps),
        out_specs=out_spec, out_shape=out_shape,
        scratch_shapes=[] if nk == 1 else [pltpu.VMEM((tm, tn), F32)],
        compiler_params=_cparams(("parallel", "parallel", "arbitrary")),
    )(a, b, *(() if init is None else (init,)), *deps)


def _assemble_w_in(shards, front, front_pad, rows, row0, into=None):
    _, K, n = shards.shape
    gap = front_pad - front
    tk = _tile(K, 256, PACK_ROW_ALIGN)
    blk0 = row0 // tk

    def body(g_ref, *rest):
        o_ref = rest[-1]
        if gap:
            o_ref[:, front:front_pad] = jnp.zeros((tk, gap), o_ref.dtype)
        for j in range(N_DEV):
            lo, hi = j * n, (j + 1) * n
            if lo < front < hi:
                o_ref[:, lo:front] = g_ref[j, :, 0:front - lo]
                o_ref[:, front_pad:hi + gap] = g_ref[j, :, front - lo:n]
            else:
                off = 0 if hi <= front else gap
                o_ref[:, lo + off:hi + off] = g_ref[j]

    return pl.pallas_call(
        body, name="assemble_w_in", grid=(K // tk,),
        in_specs=[pl.BlockSpec((N_DEV, tk, n), lambda i: (0, i, 0))] + ([] if into is None else [ANY_SPEC]),
        out_specs=pl.BlockSpec((tk, N_DEV * n + gap), lambda i: (blk0 + i, 0)),
        out_shape=jax.ShapeDtypeStruct((rows, N_DEV * n + gap), shards.dtype),
        input_output_aliases={} if into is None else {1: 0},
        compiler_params=_cparams(("parallel",)),
    )(*([shards] if into is None else [shards, into]))


def _split_w_in(w, front, front_pad):
    K, NP = w.shape
    gap = front_pad - front
    n = (NP - gap) // N_DEV
    tk = _tile(K, 256, PACK_ROW_ALIGN)

    def body(w_ref, o_ref):
        for j in range(N_DEV):
            lo, hi = j * n, (j + 1) * n
            if lo < front < hi:
                o_ref[j, :, 0:front - lo] = w_ref[:, lo:front]
                o_ref[j, :, front - lo:n] = w_ref[:, front_pad:hi + gap]
            else:
                off = 0 if hi <= front else gap
                o_ref[j] = w_ref[:, lo + off:hi + off]

    return pl.pallas_call(
        body, name="split_grad_w_in", grid=(K // tk,),
        in_specs=[pl.BlockSpec((tk, NP), lambda i: (i, 0))],
        out_specs=pl.BlockSpec((N_DEV, tk, n), lambda i: (0, i, 0)),
        out_shape=jax.ShapeDtypeStruct((N_DEV, K, n), w.dtype),
        compiler_params=_cparams(("parallel",)),
    )(w)


def _modulate_in(x, mod, ts):
    S, D = x.shape

    def body(x_ref, mod_ref, u_ref):
        u_ref[...] = (x_ref[...] * (1.0 + mod_ref[1:2, :]) + mod_ref[0:1, :]).astype(BF16)

    return pl.pallas_call(
        body, name="modulate_in", grid=(S // ts,),
        in_specs=[pl.BlockSpec((ts, D), lambda i: (i, 0)), pl.BlockSpec((6, D), lambda i: (0, 0))],
        out_specs=pl.BlockSpec((ts, D), lambda i: (i, 0)),
        out_shape=jax.ShapeDtypeStruct((S, D), BF16),
        compiler_params=_cparams(("parallel",)),
    )(x, mod)


def _rms_fwd(proj, g, blk, L, ts, name):
    S = proj.shape[0]

    def body(a_ref, g_ref, y_ref):
        a = a_ref[...]
        r = lax.rsqrt(jnp.mean(a * a, axis=-1, keepdims=True) + RMS_EPS)
        y_ref[...] = (a * r * g_ref[...]).astype(BF16)

    return pl.pallas_call(
        body, name=name, grid=(S // ts,),
        in_specs=[pl.BlockSpec((ts, L), lambda i: (i, blk)), pl.BlockSpec((1, L), lambda i: (0, 0))],
        out_specs=pl.BlockSpec((ts, L), lambda i: (i, 0)),
        out_shape=jax.ShapeDtypeStruct((S, L), BF16),
        compiler_params=_cparams(("parallel",)),
    )(proj, g)


def _rope_partner(x, period, start):
    w = x.shape[-1]
    lane = lax.broadcasted_iota(jnp.int32, x.shape, x.ndim - 1) % period
    first = (lane >= start) & (lane < start + QK_ROPE // 2)
    from_right = pltpu.roll(x, w - QK_ROPE // 2, axis=x.ndim - 1)
    from_left = pltpu.roll(x, QK_ROPE // 2, axis=x.ndim - 1)
    return jnp.where(first, -from_right, from_left)


def _qk_prep(q, kv, proj, kr_blk, cos_q, sin_q, cos_k, sin_k, H, ts):
    S = q.shape[0]
    pair = 2 * QK_CAT
    kv_w = QK_NOPE + V_HEAD

    def body(q_ref, kv_ref, kr_ref, cq_ref, sq_ref, ck_ref, sk_ref, qc_ref, kc_ref, vh_ref):
        kr = kr_ref[...]
        kr = kr * ck_ref[...] + _rope_partner(kr, QK_ROPE, 0) * sk_ref[...]
        kr = kr[:, :QK_ROPE].astype(BF16)
        for p in range(H // 2):
            x = q_ref[:, p * pair:(p + 1) * pair]
            x = x * cq_ref[...] + _rope_partner(x, QK_CAT, QK_NOPE) * sq_ref[...]
            qc_ref[2 * p] = x[:, :QK_CAT].astype(BF16)
            qc_ref[2 * p + 1] = x[:, QK_CAT:].astype(BF16)
        for h in range(H):
            kc_ref[h, :, 0:QK_NOPE] = kv_ref[:, h * kv_w:h * kv_w + QK_NOPE].astype(BF16)
            kc_ref[h, :, QK_NOPE:QK_CAT] = kr
            vh_ref[h, :, :] = kv_ref[:, h * kv_w + QK_NOPE:(h + 1) * kv_w].astype(BF16)

    row = lambda w: pl.BlockSpec((ts, w), lambda i: (i, 0))
    return pl.pallas_call(
        body, name="qk_prep", grid=(S // ts,),
        in_specs=[row(H * QK_CAT), row(H * kv_w),
                  pl.BlockSpec((ts, COL_BLOCK), lambda i: (i, kr_blk)),
                  row(pair), row(pair), row(COL_BLOCK), row(COL_BLOCK)],
        out_specs=[pl.BlockSpec((H, ts, QK_CAT), lambda i: (0, i, 0)),
                   pl.BlockSpec((H, ts, QK_CAT), lambda i: (0, i, 0)),
                   pl.BlockSpec((H, ts, V_HEAD), lambda i: (0, i, 0))],
        out_shape=[jax.ShapeDtypeStruct((H, S, QK_CAT), BF16), jax.ShapeDtypeStruct((H, S, QK_CAT), BF16),
                   jax.ShapeDtypeStruct((H, S, V_HEAD), BF16)],
        compiler_params=_cparams(("parallel",)),
    )(q, kv, proj, cos_q, sin_q, cos_k, sin_k)


NT_DIMS = (((1,), (1,)), ((), ()))
TN_DIMS = (((0,), (0,)), ((), ()))


def _diag_mask(T):
    rows = lax.broadcasted_iota(jnp.int32, (T, T), 0) // CHUNK
    cols = lax.broadcasted_iota(jnp.int32, (T, T), 1) // CHUNK
    return cols <= rows


def _attn_fwd(qc, kc, vh, T):
    H, S, _ = qc.shape
    n = S // T

    def body(q_ref, k_ref, v_ref, o_ref, lse_ref):
        q = q_ref[0]

        def block(i):
            L = (i + 1) * T
            s_old = lax.dot_general(q, k_ref[0, 0:i * T, :], NT_DIMS, preferred_element_type=F32) if i else None
            s_diag = lax.dot_general(q, k_ref[0, i * T:L, :], NT_DIMS, preferred_element_type=F32)
            s_diag = jnp.where(_diag_mask(T), s_diag, NEG_INF)
            m = jnp.max(s_diag, axis=-1, keepdims=True)
            if i:
                m = jnp.maximum(m, jnp.max(s_old, axis=-1, keepdims=True))
            p_diag = jnp.exp((s_diag - m) * ATTN_SCALE)
            l = jnp.sum(p_diag, axis=-1, keepdims=True)
            acc = jnp.dot(p_diag.astype(BF16), v_ref[0, i * T:L, :], preferred_element_type=F32)
            if i:
                p_old = jnp.exp((s_old - m) * ATTN_SCALE)
                l = l + jnp.sum(p_old, axis=-1, keepdims=True)
                acc = acc + jnp.dot(p_old.astype(BF16), v_ref[0, 0:i * T, :], preferred_element_type=F32)
            o_ref[...] = acc / l
            lse_ref[0] = m * ATTN_SCALE + jnp.log(l)

        for i in range(n):
            pl.when(pl.program_id(1) == i)(functools.partial(block, i))

    return pl.pallas_call(
        body, name="attn_fwd", grid=(H, n),
        in_specs=[pl.BlockSpec((1, T, QK_CAT), lambda h, i: (h, i, 0)),
                  pl.BlockSpec((1, S, QK_CAT), lambda h, i: (h, 0, 0)),
                  pl.BlockSpec((1, S, V_HEAD), lambda h, i: (h, 0, 0))],
        out_specs=[pl.BlockSpec((T, V_HEAD), lambda h, i: (i, h)),
                   pl.BlockSpec((1, T, 1), lambda h, i: (h, i, 0))],
        out_shape=[jax.ShapeDtypeStruct((S, H * V_HEAD), F32), jax.ShapeDtypeStruct((H, S, 1), F32)],
        compiler_params=_cparams(("parallel", "arbitrary")),
    )(qc, kc, vh)


def _shift_rows(z, k):
    if k == 0:
        return z
    n = z.shape[0]
    row = lax.broadcasted_iota(jnp.int32, z.shape, 0)
    if k > 0:
        return jnp.where(row >= k, pltpu.roll(z, k, axis=0), 0.0)
    return jnp.where(row < n + k, pltpu.roll(z, n + k, axis=0), 0.0)


def _conv_fwd(proj, w_conv, blk_b, blk_c, blk_x):
    S = proj.shape[0]
    D = w_conv.shape[1]
    nb = D // COL_BLOCK

    def body(cb_ref, cc_ref, cx_ref, w_ref, o_ref):
        z = cc_ref[...] * cx_ref[...]
        conv = w_ref[2:3, :] * z + w_ref[1:2, :] * _shift_rows(z, 1) + w_ref[0:1, :] * _shift_rows(z, 2)
        o_ref[...] = (cb_ref[...] * conv).astype(BF16)

    col = lambda off: pl.BlockSpec((S, COL_BLOCK), lambda j: (0, off + j))
    return pl.pallas_call(
        body, name="conv_fwd", grid=(nb,),
        in_specs=[col(blk_b), col(blk_c), col(blk_x), pl.BlockSpec((CONV_K, COL_BLOCK), lambda j: (0, j))],
        out_specs=pl.BlockSpec((S, COL_BLOCK), lambda j: (0, j)),
        out_shape=jax.ShapeDtypeStruct((S, D), BF16),
        compiler_params=_cparams(("parallel",)),
    )(proj, proj, proj, w_conv)


def _merge_fwd(proj, ya, yb, blk_ga, blk_gb, ts):
    S, D = ya.shape
    nb = D // COL_BLOCK

    def body(ga_ref, gb_ref, ya_ref, yb_ref, o_ref):
        o_ref[...] = (_sigmoid(ga_ref[...]) * ya_ref[...] + _sigmoid(gb_ref[...]) * yb_ref[...]).astype(BF16)

    row = pl.BlockSpec((ts, D), lambda i: (i, 0))
    seg = lambda blk: pl.BlockSpec((pl.Element(ts), pl.Element(D)), lambda i: (i * ts, blk * COL_BLOCK))
    return pl.pallas_call(
        body, name="merge_fwd", grid=(S // ts,),
        in_specs=[seg(blk_ga), seg(blk_gb), row, row],
        out_specs=row,
        out_shape=jax.ShapeDtypeStruct((S, D), BF16),
        compiler_params=_cparams(("parallel",)),
    )(proj, proj, ya, yb)


def _ln1_fwd(x, mix, mod, g, b, ts):
    S, D = x.shape

    def body(x_ref, mix_ref, mod_ref, g_ref, b_ref, xhat_ref, rstd_ref, u2_ref):
        r = DEEPNORM_ALPHA * x_ref[...] + mod_ref[2:3, :] * mix_ref[...]
        mu = jnp.mean(r, axis=-1, keepdims=True)
        d = r - mu
        rstd = lax.rsqrt(jnp.mean(d * d, axis=-1, keepdims=True) + LN_EPS)
        xhat = d * rstd
        xhat_ref[...] = xhat
        rstd_ref[...] = rstd
        x1 = xhat * g_ref[...] + b_ref[...]
        u2_ref[...] = (x1 * (1.0 + mod_ref[4:5, :]) + mod_ref[3:4, :]).astype(BF16)

    row = pl.BlockSpec((ts, D), lambda i: (i, 0))
    vec = lambda r: pl.BlockSpec((r, D), lambda i: (0, 0))
    return pl.pallas_call(
        body, name="ln1_fwd", grid=(S // ts,),
        in_specs=[row, row, vec(6), vec(1), vec(1)],
        out_specs=[row, pl.BlockSpec((ts, 1), lambda i: (i, 0)), row],
        out_shape=[jax.ShapeDtypeStruct((S, D), F32), jax.ShapeDtypeStruct((S, 1), F32),
                   jax.ShapeDtypeStruct((S, D), BF16)],
        compiler_params=_cparams(("parallel",)),
    )(x, mix, mod, g, b)


def _swiglu_fwd(h, ts, tb):
    S, F2 = h.shape
    F = F2 // 2
    nb = F // tb

    def body(hg_ref, hu_ref, a_ref):
        hg = hg_ref[...]
        a_ref[...] = (hg * _sigmoid(hg) * hu_ref[...]).astype(BF16)

    return pl.pallas_call(
        body, name="swiglu_fwd", grid=(S // ts, nb),
        in_specs=[pl.BlockSpec((ts, tb), lambda i, j: (i, j)), pl.BlockSpec((ts, tb), lambda i, j: (i, j + nb))],
        out_specs=pl.BlockSpec((ts, tb), lambda i, j: (i, j)),
        out_shape=jax.ShapeDtypeStruct((S, F), BF16),
        compiler_params=_cparams(("parallel", "parallel")),
    )(h, h)


def _ln2_loss(xhat1, ffn, tgt, mod, g1, b1, g2, b2, ts):
    S, D = xhat1.shape

    def body(xh_ref, ffn_ref, t_ref, mod_ref, g1_ref, b1_ref, g2_ref, b2_ref, loss_ref, dffn_ref, dx1_ref, vec_ref):
        i = pl.program_id(0)

        @pl.when(i == 0)
        def _():
            loss_ref[...] = jnp.zeros_like(loss_ref)
            vec_ref[...] = jnp.zeros_like(vec_ref)

        x1 = xh_ref[...] * g1_ref[...] + b1_ref[...]
        ffn = ffn_ref[...]
        r = DEEPNORM_ALPHA * x1 + mod_ref[5:6, :] * ffn
        mu = jnp.mean(r, axis=-1, keepdims=True)
        d = r - mu
        rstd = lax.rsqrt(jnp.mean(d * d, axis=-1, keepdims=True) + LN_EPS)
        xhat = d * rstd
        e = xhat * g2_ref[...] + b2_ref[...] - t_ref[...]
        loss_ref[...] += 0.5 * jnp.sum(jnp.mean(e * e, axis=-1, keepdims=True))
        dy = e * (1.0 / D)
        dxhat = dy * g2_ref[...]
        dr = rstd * (dxhat - jnp.mean(dxhat, axis=-1, keepdims=True)
                     - xhat * jnp.mean(dxhat * xhat, axis=-1, keepdims=True))
        dffn_ref[...] = (dr * mod_ref[5:6, :]).astype(BF16)
        dx1_ref[...] = DEEPNORM_ALPHA * dr
        vec_ref[0:1, :] += jnp.sum(dy * xhat, axis=0, keepdims=True)
        vec_ref[1:2, :] += jnp.sum(dy, axis=0, keepdims=True)
        vec_ref[2:3, :] += jnp.sum(dr * ffn, axis=0, keepdims=True)

    row = pl.BlockSpec((ts, D), lambda i: (i, 0))
    vec = lambda r: pl.BlockSpec((r, D), lambda i: (0, 0))
    return pl.pallas_call(
        body, name="ln2_loss", grid=(S // ts,),
        in_specs=[row, row, row, vec(6), vec(1), vec(1), vec(1), vec(1)],
        out_specs=[pl.BlockSpec((1, LANE), lambda i: (0, 0)), row, row, vec(8)],
        out_shape=[jax.ShapeDtypeStruct((1, LANE), F32), jax.ShapeDtypeStruct((S, D), BF16),
                   jax.ShapeDtypeStruct((S, D), F32), jax.ShapeDtypeStruct((8, D), F32)],
        compiler_params=_cparams(("arbitrary",)),
    )(xhat1, ffn, tgt, mod, g1, b1, g2, b2)


def _swiglu_bwd(da, h, ts, tb):
    S, F2 = h.shape
    nb = (F2 // 2) // tb

    def body(da_ref, hg_ref, hu_ref, dh_ref):
        hg, da = hg_ref[...], da_ref[...]
        sg = _sigmoid(hg)

        @pl.when(pl.program_id(2) == 0)
        def _():
            dh_ref[...] = (da * hu_ref[...] * (sg * (1.0 + hg * (1.0 - sg)))).astype(BF16)

        @pl.when(pl.program_id(2) == 1)
        def _():
            dh_ref[...] = (da * hg * sg).astype(BF16)

    lo = pl.BlockSpec((ts, tb), lambda i, j, k: (i, j))
    hi = pl.BlockSpec((ts, tb), lambda i, j, k: (i, j + nb))
    return pl.pallas_call(
        body, name="swiglu_bwd", grid=(S // ts, nb, 2),
        in_specs=[lo, lo, hi],
        out_specs=pl.BlockSpec((ts, tb), lambda i, j, k: (i, j + nb * k)),
        out_shape=jax.ShapeDtypeStruct((S, F2), BF16),
        compiler_params=_cparams(("parallel", "parallel", "arbitrary")),
    )(da, h, h)


def _ln1_bwd(du2, dx1a, xhat1, rstd1, mix, mod, g1, b1, ts):
    S, D = xhat1.shape

    def body(du2_ref, dx1a_ref, xh_ref, rstd_ref, mix_ref, mod_ref, g_ref, b_ref, dxa_ref, dmix_ref, vec_ref):
        i = pl.program_id(0)

        @pl.when(i == 0)
        def _():
            vec_ref[...] = jnp.zeros_like(vec_ref)

        xhat, du2, mix = xh_ref[...], du2_ref[...], mix_ref[...]
        x1 = xhat * g_ref[...] + b_ref[...]
        dx1 = dx1a_ref[...] + du2 * (1.0 + mod_ref[4:5, :])
        dxhat = dx1 * g_ref[...]
        dr = rstd_ref[...] * (dxhat - jnp.mean(dxhat, axis=-1, keepdims=True)
                              - xhat * jnp.mean(dxhat * xhat, axis=-1, keepdims=True))
        dxa_ref[...] = DEEPNORM_ALPHA * dr
        dmix_ref[...] = (dr * mod_ref[2:3, :]).astype(BF16)
        vec_ref[0:1, :] += jnp.sum(du2, axis=0, keepdims=True)
        vec_ref[1:2, :] += jnp.sum(du2 * x1, axis=0, keepdims=True)
        vec_ref[2:3, :] += jnp.sum(dx1 * xhat, axis=0, keepdims=True)
        vec_ref[3:4, :] += jnp.sum(dx1, axis=0, keepdims=True)
        vec_ref[4:5, :] += jnp.sum(dr * mix, axis=0, keepdims=True)

    row = pl.BlockSpec((ts, D), lambda i: (i, 0))
    vec = lambda r: pl.BlockSpec((r, D), lambda i: (0, 0))
    return pl.pallas_call(
        body, name="ln1_bwd", grid=(S // ts,),
        in_specs=[row, row, row, pl.BlockSpec((ts, 1), lambda i: (i, 0)), row, vec(6), vec(1), vec(1)],
        out_specs=[row, row, vec(8)],
        out_shape=[jax.ShapeDtypeStruct((S, D), F32), jax.ShapeDtypeStruct((S, D), BF16),
                   jax.ShapeDtypeStruct((8, D), F32)],
        compiler_params=_cparams(("arbitrary",)),
    )(du2, dx1a, xhat1, rstd1, mix, mod, g1, b1)


def _merge_bwd(dmerged, proj, ya, yb, blk_ga, blk_gb, ts):
    S, D = ya.shape
    nb = D // COL_BLOCK

    def body(dm_ref, ga_ref, gb_ref, ya_ref, yb_ref, dya_ref, dyb_ref, dga_ref, dgb_ref):
        dm = dm_ref[...]
        sa, sb = _sigmoid(ga_ref[...]), _sigmoid(gb_ref[...])
        dya_ref[...] = (dm * sa).astype(BF16)
        dyb_ref[...] = (dm * sb).astype(BF16)
        dga_ref[...] = (dm * ya_ref[...] * sa * (1.0 - sa)).astype(BF16)
        dgb_ref[...] = (dm * yb_ref[...] * sb * (1.0 - sb)).astype(BF16)

    row = pl.BlockSpec((ts, D), lambda i: (i, 0))
    seg = lambda blk: pl.BlockSpec((pl.Element(ts), pl.Element(D)), lambda i: (i * ts, blk * COL_BLOCK))
    out = jax.ShapeDtypeStruct((S, D), BF16)
    return pl.pallas_call(
        body, name="merge_bwd", grid=(S // ts,),
        in_specs=[row, seg(blk_ga), seg(blk_gb), row, row],
        out_specs=[row] * 4,
        out_shape=[out] * 4,
        compiler_params=_cparams(("parallel",)),
    )(dmerged, proj, proj, ya, yb)


def _conv_bwd(dcbc, proj, w_conv, blk_b, blk_c, blk_x):
    S = proj.shape[0]
    D = w_conv.shape[1]
    nb = D // COL_BLOCK

    def body(d_ref, cb_ref, cc_ref, cx_ref, w_ref, dcb_ref, dcc_ref, dcx_ref, dw_ref):
        d, cc, cx = d_ref[...], cc_ref[...], cx_ref[...]
        z = cc * cx
        z1, z2 = _shift_rows(z, 1), _shift_rows(z, 2)
        conv = w_ref[2:3, :] * z + w_ref[1:2, :] * z1 + w_ref[0:1, :] * z2
        dcb_ref[...] = (d * conv).astype(BF16)
        dconv = d * cb_ref[...]
        dz = w_ref[2:3, :] * dconv + w_ref[1:2, :] * _shift_rows(dconv, -1) + w_ref[0:1, :] * _shift_rows(dconv, -2)
        dcc_ref[...] = (dz * cx).astype(BF16)
        dcx_ref[...] = (dz * cc).astype(BF16)
        dw_ref[...] = jnp.zeros_like(dw_ref)
        dw_ref[0:1, :] = jnp.sum(dconv * z2, axis=0, keepdims=True)
        dw_ref[1:2, :] = jnp.sum(dconv * z1, axis=0, keepdims=True)
        dw_ref[2:3, :] = jnp.sum(dconv * z, axis=0, keepdims=True)

    col = lambda off: pl.BlockSpec((S, COL_BLOCK), lambda j: (0, off + j))
    out = jax.ShapeDtypeStruct((S, D), BF16)
    return pl.pallas_call(
        body, name="conv_bwd", grid=(nb,),
        in_specs=[col(0), col(blk_b), col(blk_c), col(blk_x), pl.BlockSpec((CONV_K, COL_BLOCK), lambda j: (0, j))],
        out_specs=[col(0), col(0), col(0), pl.BlockSpec((8, COL_BLOCK), lambda j: (0, j))],
        out_shape=[out, out, out, jax.ShapeDtypeStruct((8, D), F32)],
        compiler_params=_cparams(("parallel",)),
    )(dcbc, proj, proj, proj, w_conv)


def _attn_bwd(qc, kc, vh, do, o, lse, T):
    H, S, _ = qc.shape
    n = S // T

    def body(q_ref, k_ref, v_ref, do_ref, o_ref, lse_ref, dq_ref, dk_ref, dv_ref, d_ref, dk_acc, dv_acc):
        j = pl.program_id(1)

        @pl.when(j == 0)
        def _():
            dq_ref[...] = jnp.zeros_like(dq_ref)
            d_ref[...] = jnp.sum(do_ref[...] * o_ref[...], axis=-1, keepdims=True)

        dk_acc[...] = jnp.zeros_like(dk_acc)
        dv_acc[...] = jnp.zeros_like(dv_acc)
        k, v = k_ref[0], v_ref[0]

        def step(i, masked):
            rows = pl.ds(pl.multiple_of(i * T, T), T)
            q = q_ref[0, rows, :]
            do = do_ref[rows, :].astype(BF16)
            s = lax.dot_general(q, k, NT_DIMS, preferred_element_type=F32) * ATTN_SCALE
            if masked:
                s = jnp.where(_diag_mask(T), s, NEG_INF)
            p = jnp.exp(s - lse_ref[0, rows, :])
            dv_acc[...] += lax.dot_general(p.astype(BF16), do, TN_DIMS, preferred_element_type=F32)
            dp = lax.dot_general(do, v, NT_DIMS, preferred_element_type=F32)
            ds = (p * (dp - d_ref[rows, :]) * ATTN_SCALE).astype(BF16)
            dk_acc[...] += lax.dot_general(ds, q, TN_DIMS, preferred_element_type=F32)
            dq_ref[0, rows, :] += jnp.dot(ds, k, preferred_element_type=F32)

        def above(i, carry):
            step(i, False)
            return carry

        step(j, True)
        lax.fori_loop(j + 1, n, above, 0)
        dk_ref[0] = dk_acc[...]
        dv_ref[0] = dv_acc[...]

    head = lambda w: pl.BlockSpec((1, S, w), lambda h, j: (h, 0, 0))
    blk = lambda w: pl.BlockSpec((1, T, w), lambda h, j: (h, j, 0))
    ospec = pl.BlockSpec((S, V_HEAD), lambda h, j: (0, h))
    return pl.pallas_call(
        body, name="attn_bwd", grid=(H, n),
        in_specs=[head(QK_CAT), blk(QK_CAT), blk(V_HEAD), ospec, ospec, head(1)],
        out_specs=[head(QK_CAT), blk(QK_CAT), blk(V_HEAD)],
        out_shape=[jax.ShapeDtypeStruct((H, S, QK_CAT), F32), jax.ShapeDtypeStruct((H, S, QK_CAT), F32),
                   jax.ShapeDtypeStruct((H, S, V_HEAD), F32)],
        scratch_shapes=[pltpu.VMEM((S, 1), F32), pltpu.VMEM((T, QK_CAT), F32), pltpu.VMEM((T, V_HEAD), F32)],
        compiler_params=_cparams(("parallel", "arbitrary")),
    )(qc, kc, vh, do, o, lse)


def _qk_bwd(dqc, dkc, dvh, cos_q, sin_q, cos_k, sin_k, ts):
    H, S, _ = dqc.shape
    pair = 2 * QK_CAT
    kv_w = QK_NOPE + V_HEAD

    def body(dqc_ref, dkc_ref, dvh_ref, cq_ref, sq_ref, ck_ref, sk_ref, dq_ref, dkv_ref, dkr_ref, q_buf, kr_buf):
        for p in range(H // 2):
            q_buf[:, :QK_CAT] = dqc_ref[2 * p]
            q_buf[:, QK_CAT:] = dqc_ref[2 * p + 1]
            g = q_buf[...]
            dq_ref[:, p * pair:(p + 1) * pair] = (
                g * cq_ref[...] - _rope_partner(g, QK_CAT, QK_NOPE) * sq_ref[...]).astype(BF16)
        kr_sum = jnp.zeros((ts, QK_ROPE), F32)
        for h in range(H):
            dkv_ref[:, h * kv_w:h * kv_w + QK_NOPE] = dkc_ref[h, :, 0:QK_NOPE].astype(BF16)
            dkv_ref[:, h * kv_w + QK_NOPE:(h + 1) * kv_w] = dvh_ref[h].astype(BF16)
            kr_sum = kr_sum + dkc_ref[h, :, QK_NOPE:QK_CAT]
        kr_buf[...] = jnp.zeros_like(kr_buf)
        kr_buf[:, 0:QK_ROPE] = kr_sum
        kr = kr_buf[...]
        dkr_ref[...] = (kr * ck_ref[...] - _rope_partner(kr, QK_ROPE, 0) * sk_ref[...]).astype(BF16)

    row = lambda w: pl.BlockSpec((ts, w), lambda i: (i, 0))
    head = lambda w: pl.BlockSpec((H, ts, w), lambda i: (0, i, 0))
    return pl.pallas_call(
        body, name="qk_bwd", grid=(S // ts,),
        in_specs=[head(QK_CAT), head(QK_CAT), head(V_HEAD), row(pair), row(pair), row(COL_BLOCK), row(COL_BLOCK)],
        out_specs=[row(H * QK_CAT), row(H * kv_w), row(COL_BLOCK)],
        out_shape=[jax.ShapeDtypeStruct((S, H * QK_CAT), BF16), jax.ShapeDtypeStruct((S, H * kv_w), BF16),
                   jax.ShapeDtypeStruct((S, COL_BLOCK), BF16)],
        scratch_shapes=[pltpu.VMEM((ts, pair), F32), pltpu.VMEM((ts, COL_BLOCK), F32)],
        compiler_params=_cparams(("parallel",)),
    )(dqc, dkc, dvh, cos_q, sin_q, cos_k, sin_k)


def _rms_bwd(dy, proj, g, blk, L, ts, name):
    S = proj.shape[0]

    def body(dy_ref, a_ref, g_ref, da_ref, dg_ref):
        i = pl.program_id(0)

        @pl.when(i == 0)
        def _():
            dg_ref[...] = jnp.zeros_like(dg_ref)

        a, dy = a_ref[...], dy_ref[...]
        r = lax.rsqrt(jnp.mean(a * a, axis=-1, keepdims=True) + RMS_EPS)
        dyh = dy * g_ref[...]
        da = r * dyh - a * (r * r * r) * jnp.mean(dyh * a, axis=-1, keepdims=True)
        da_ref[...] = da.astype(BF16)
        dg_ref[0:1, :] += jnp.sum(dy * a * r, axis=0, keepdims=True)

    return pl.pallas_call(
        body, name=name, grid=(S // ts,),
        in_specs=[pl.BlockSpec((ts, L), lambda i: (i, 0)), pl.BlockSpec((ts, L), lambda i: (i, blk)),
                  pl.BlockSpec((1, L), lambda i: (0, 0))],
        out_specs=[pl.BlockSpec((ts, L), lambda i: (i, 0)), pl.BlockSpec((8, L), lambda i: (0, 0))],
        out_shape=[jax.ShapeDtypeStruct((S, L), BF16), jax.ShapeDtypeStruct((8, L), F32)],
        compiler_params=_cparams(("arbitrary",)),
    )(dy, proj, g)


def _grad_x(du, dxa, x, mod, ts):
    S, D = x.shape

    def body(du_ref, dxa_ref, x_ref, mod_ref, dx_ref, vec_ref):
        i = pl.program_id(0)

        @pl.when(i == 0)
        def _():
            vec_ref[...] = jnp.zeros_like(vec_ref)

        du = du_ref[...]
        dx_ref[...] = dxa_ref[...] + du * (1.0 + mod_ref[1:2, :])
        vec_ref[0:1, :] += jnp.sum(du, axis=0, keepdims=True)
        vec_ref[1:2, :] += jnp.sum(du * x_ref[...], axis=0, keepdims=True)

    row = pl.BlockSpec((ts, D), lambda i: (i, 0))
    vec = lambda r: pl.BlockSpec((r, D), lambda i: (0, 0))
    return pl.pallas_call(
        body, name="grad_x", grid=(S // ts,),
        in_specs=[row, row, row, vec(6)],
        out_specs=[row, vec(8)],
        out_shape=[jax.ShapeDtypeStruct((S, D), F32), jax.ShapeDtypeStruct((8, D), F32)],
        compiler_params=_cparams(("arbitrary",)),
    )(du, dxa, x, mod)


def _adamw(w, g, m, v, name):
    R, C = w.shape
    tr = _tile(R, max(8, (1 << 19) // C), 8)
    c1 = 1.0 / (1.0 - ADAM_B1 ** ADAM_STEP)
    c2 = 1.0 / (1.0 - ADAM_B2 ** ADAM_STEP)

    def body(w_ref, g_ref, m_ref, v_ref, d_ref, nm_ref, nv_ref):
        g = g_ref[...]
        m = ADAM_B1 * m_ref[...] + (1.0 - ADAM_B1) * g
        v = ADAM_B2 * v_ref[...] + (1.0 - ADAM_B2) * (g * g)
        nm_ref[...] = m
        nv_ref[...] = v
        d_ref[...] = -ADAM_LR * ((m * c1) / (jnp.sqrt(v * c2) + ADAM_EPS) + ADAM_WD * w_ref[...])

    spec = pl.BlockSpec((tr, C), lambda i: (i, 0))
    out = jax.ShapeDtypeStruct((R, C), F32)
    return pl.pallas_call(
        body, name=name, grid=(R // tr,),
        in_specs=[spec] * 4, out_specs=[spec] * 3, out_shape=[out] * 3,
        compiler_params=_cparams(("parallel",)),
    )(w, g, m, v)


def _adamw_ada(w, cact_t, dmod, m, v):
    R, C = w.shape
    tr = _tile(R, max(8, (1 << 18) // C), 8)
    c1 = 1.0 / (1.0 - ADAM_B1 ** ADAM_STEP)
    c2 = 1.0 / (1.0 - ADAM_B2 ** ADAM_STEP)

    def body(w_ref, ct_ref, dm_ref, m_ref, v_ref, g_ref, d_ref, nm_ref, nv_ref):
        ct = ct_ref[...].astype(BF16).astype(F32)
        dm = dm_ref[...].astype(BF16).astype(F32)
        g = ct[:, 0:1] * dm[0:1, :]
        for b in range(1, N_DEV):
            g = g + ct[:, b:b + 1] * dm[b:b + 1, :]
        m = ADAM_B1 * m_ref[...] + (1.0 - ADAM_B1) * g
        v = ADAM_B2 * v_ref[...] + (1.0 - ADAM_B2) * (g * g)
        g_ref[...] = g
        nm_ref[...] = m
        nv_ref[...] = v
        d_ref[...] = -ADAM_LR * ((m * c1) / (jnp.sqrt(v * c2) + ADAM_EPS) + ADAM_WD * w_ref[...])

    spec = pl.BlockSpec((tr, C), lambda i: (i, 0))
    out = jax.ShapeDtypeStruct((R, C), F32)
    return pl.pallas_call(
        body, name="adamw_w_ada", grid=(R // tr,),
        in_specs=[spec, pl.BlockSpec((tr, N_DEV), lambda i: (i, 0)), pl.BlockSpec((N_DEV, C), lambda i: (0, 0)),
                  spec, spec],
        out_specs=[spec] * 4, out_shape=[out] * 4,
        compiler_params=_cparams(("parallel",)),
    )(w, cact_t, dmod, m, v)


def _adamw_reduced(w, own, got, m, v, my_chip, name):
    R, C = w.shape
    tr = _tile(R, max(PACK_ROW_ALIGN, (1 << 18) // C), PACK_ROW_ALIGN)
    c1 = 1.0 / (1.0 - ADAM_B1 ** ADAM_STEP)
    c2 = 1.0 / (1.0 - ADAM_B2 ** ADAM_STEP)

    def body(chip_ref, w_ref, own_ref, g1_ref, g2_ref, g3_ref, m_ref, v_ref, g_ref, d_ref, nm_ref, nv_ref):
        g = own_ref[0].astype(F32) + g1_ref[0].astype(F32) + g2_ref[0].astype(F32) + g3_ref[0].astype(F32)
        m = ADAM_B1 * m_ref[...] + (1.0 - ADAM_B1) * g
        v = ADAM_B2 * v_ref[...] + (1.0 - ADAM_B2) * (g * g)
        g_ref[...] = g
        nm_ref[...] = m
        nv_ref[...] = v
        d_ref[...] = -ADAM_LR * ((m * c1) / (jnp.sqrt(v * c2) + ADAM_EPS) + ADAM_WD * w_ref[...])

    spec = pl.BlockSpec((tr, C), lambda i, chip: (i, 0))
    slot = lambda k: pl.BlockSpec((1, tr, C), lambda i, chip: (chip[0] ^ k, i, 0))
    out = jax.ShapeDtypeStruct((R, C), F32)
    return pl.pallas_call(
        body, name=name,
        grid_spec=pltpu.PrefetchScalarGridSpec(
            num_scalar_prefetch=1, grid=(R // tr,),
            in_specs=[spec, slot(0), slot(1), slot(2), slot(3), spec, spec],
            out_specs=[spec] * 4),
        out_shape=[out] * 4,
        compiler_params=_cparams(("parallel",)),
    )(my_chip, w, own, got, got, got, m, v)


def _my_place():
    return lax.axis_index("x"), lax.axis_index("y"), lax.axis_index("c")


def _peer(k):
    x, y, c = _my_place()
    return (x ^ ((k >> 2) & 1), y ^ ((k >> 1) & 1), c ^ (k & 1))


def _linear(place):
    return 4 * place[0] + 2 * place[1] + place[2]


def _ada_fwd(c_row, wconv_row, w_ada, b_row):
    D, CW = w_ada.shape
    WC = wconv_row.shape[-1]

    def body(c_ref, wc_ref, w_ref, b_ref, mod_ref, cact_ref, wcall_ref, send_buf, sems):
        me = _linear(_my_place())
        c = c_ref[0]
        cact_ref[me] = c * _sigmoid(c)
        wcall_ref[me] = wc_ref[0]

        def gather_copy(buf, k, grp):
            return pltpu.make_async_remote_copy(
                src_ref=buf.at[me], dst_ref=buf.at[me], send_sem=sems.at[0, grp, k], recv_sem=sems.at[1, grp, k],
                device_id=_peer(k), device_id_type=MESH_ID)

        def gather_recv(buf, k, grp):
            src = _linear(_peer(k))
            return pltpu.make_async_remote_copy(
                src_ref=buf.at[src], dst_ref=buf.at[src], send_sem=sems.at[0, grp, k], recv_sem=sems.at[1, grp, k],
                device_id=_peer(k), device_id_type=MESH_ID)

        for k in range(1, N_DEV):
            gather_copy(cact_ref, k, 0).start()
            gather_copy(wcall_ref, k, 1).start()
        for k in range(1, N_DEV):
            gather_recv(cact_ref, k, 0).wait_recv()
            gather_recv(wcall_ref, k, 1).wait_recv()
        for k in range(1, N_DEV):
            gather_copy(cact_ref, k, 0).wait_send()
            gather_copy(wcall_ref, k, 1).wait_send()

        cact = jnp.concatenate([cact_ref[b] for b in range(N_DEV)], axis=0)
        mod_all = jnp.dot(cact.astype(BF16), w_ref[...].astype(BF16), preferred_element_type=F32) + b_ref[0]
        for b in range(N_DEV):
            send_buf[b] = mod_all[b:b + 1, :]
        mod_ref[me] = send_buf[me]

        def scatter_copy(k):
            dst = _linear(_peer(k))
            return pltpu.make_async_remote_copy(
                src_ref=send_buf.at[dst], dst_ref=mod_ref.at[me], send_sem=sems.at[0, 2, k], recv_sem=sems.at[1, 2, k],
                device_id=_peer(k), device_id_type=MESH_ID)

        def scatter_recv(k):
            src = _linear(_peer(k))
            return pltpu.make_async_remote_copy(
                src_ref=send_buf.at[src], dst_ref=mod_ref.at[src], send_sem=sems.at[0, 2, k], recv_sem=sems.at[1, 2, k],
                device_id=_peer(k), device_id_type=MESH_ID)

        for k in range(1, N_DEV):
            scatter_copy(k).start()
        for k in range(1, N_DEV):
            scatter_recv(k).wait_recv()
        for k in range(1, N_DEV):
            scatter_copy(k).wait_send()

    vmem = pl.BlockSpec(memory_space=pltpu.VMEM)
    return pl.pallas_call(
        body, name="ada_fwd",
        in_specs=[vmem] * 4, out_specs=[vmem] * 3,
        out_shape=[jax.ShapeDtypeStruct((N_DEV, 1, CW), F32), jax.ShapeDtypeStruct((N_DEV, 1, D), F32),
                   jax.ShapeDtypeStruct((N_DEV, 1, WC), F32)],
        scratch_shapes=[pltpu.VMEM((N_DEV, 1, CW), F32), pltpu.SemaphoreType.DMA((2, 3, N_DEV))],
        compiler_params=pltpu.CompilerParams(vmem_limit_bytes=VMEM_LIMIT),
    )(c_row, wconv_row, w_ada, b_row)


def _ada_bwd(payload, deps=()):
    NCH, _, CW = payload.shape

    def body(p_ref, *rest):
        sum_ref, mine_ref, all_ref, sems = rest[-4:]
        me = _linear(_my_place())
        all_ref[me] = p_ref[...]

        def copy(k, slot):
            return pltpu.make_async_remote_copy(
                src_ref=all_ref.at[slot], dst_ref=all_ref.at[slot], send_sem=sems.at[0, k], recv_sem=sems.at[1, k],
                device_id=_peer(k), device_id_type=MESH_ID)

        for k in range(1, N_DEV):
            copy(k, me).start()
        for k in range(1, N_DEV):
            copy(k, _linear(_peer(k))).wait_recv()
        for k in range(1, N_DEV):
            copy(k, me).wait_send()

        total = all_ref[0]
        for b in range(1, N_DEV):
            total = total + all_ref[b]
        sum_ref[...] = total

        for b in range(N_DEV):
            mine_ref[b] = all_ref[b, me]

    vmem = pl.BlockSpec(memory_space=pltpu.VMEM)
    return pl.pallas_call(
        body, name="ada_bwd",
        in_specs=[vmem] + [ANY_SPEC] * len(deps), out_specs=[vmem, vmem],
        out_shape=[jax.ShapeDtypeStruct((NCH, 1, CW), F32), jax.ShapeDtypeStruct((N_DEV, 1, CW), F32)],
        scratch_shapes=[pltpu.VMEM((N_DEV, NCH, 1, CW), F32), pltpu.SemaphoreType.DMA((2, N_DEV))],
        compiler_params=pltpu.CompilerParams(vmem_limit_bytes=VMEM_LIMIT),
    )(payload, *deps)


def _exchange_in_chip(parts):
    W = len(parts)

    def body(*refs):
        p_refs, got_refs, (send_sems, recv_sems) = refs[:W], refs[W:2 * W], refs[2 * W:]
        x, y, c = _my_place()
        sibling = (x, y, 1 - c)
        copies = []
        for w in range(W):
            for q in range(4):
                copies.append(pltpu.make_async_remote_copy(
                    src_ref=p_refs[w].at[2 * q + (1 - c)], dst_ref=got_refs[w].at[q],
                    send_sem=send_sems.at[4 * w + q], recv_sem=recv_sems.at[4 * w + q],
                    device_id=sibling, device_id_type=MESH_ID))
        for cp in copies:
            cp.start()
        for cp in copies:
            cp.wait_recv()
        for cp in copies:
            cp.wait_send()

    return pl.pallas_call(
        body, name="grad_exchange_in_chip",
        in_specs=[HBM_SPEC] * W, out_specs=[HBM_SPEC] * W,
        out_shape=[jax.ShapeDtypeStruct((4,) + p.shape[1:], p.dtype) for p in parts],
        scratch_shapes=[pltpu.SemaphoreType.DMA((4 * W,)), pltpu.SemaphoreType.DMA((4 * W,))],
    )(*parts)


def _pair_sum(parts, got, core):
    _, R, C = parts.shape
    tr = _tile(R, max(PACK_ROW_ALIGN, PAIR_SUM_BLOCK // C), PACK_ROW_ALIGN)

    def body(c_ref, p_ref, g_ref, o_ref):
        o_ref[...] = (p_ref[...].astype(F32) + g_ref[...].astype(F32)).astype(o_ref.dtype)

    return pl.pallas_call(
        body, name="grad_pair_sum",
        grid_spec=pltpu.PrefetchScalarGridSpec(
            num_scalar_prefetch=1, grid=(4, R // tr),
            in_specs=[pl.BlockSpec((1, tr, C), lambda q, i, c_ref: (2 * q + c_ref[0], i, 0)),
                      pl.BlockSpec((1, tr, C), lambda q, i, c_ref: (q, i, 0))],
            out_specs=pl.BlockSpec((1, tr, C), lambda q, i, c_ref: (q, i, 0))),
        out_shape=jax.ShapeDtypeStruct((4, R, C), parts.dtype),
        compiler_params=_cparams(("parallel", "parallel")),
    )(core, parts, got)


HBM_SPEC = pl.BlockSpec(memory_space=pltpu.HBM)
SEM_SPEC = pl.BlockSpec(memory_space=pltpu.SEMAPHORE)
ANY_SPEC = pl.BlockSpec(memory_space=pl.ANY)
SPLIT_EFFECT = pltpu.SideEffectType.DATAFLOW_SIDE_EFFECTING


def _landing_zone(shape, dtype):
    return pltpu.with_memory_space_constraint(lax.empty(shape, dtype), pltpu.HBM)


def _split_start(name, arrays, lands, after, copies_of, per_array):
    W = len(arrays)
    after = tuple(after) if isinstance(after, (tuple, list)) else (after,)

    def body(*refs):
        x_refs, land_refs = refs[:W], refs[W:2 * W]
        send_sems, recv_sems = refs[2 * W + len(after)], refs[2 * W + len(after) + 1]
        token = refs[-1]
        k = 0
        for w in range(W):
            for src, dst, dev in copies_of(w, x_refs[w], land_refs[w]):
                pltpu.make_async_remote_copy(src_ref=src, dst_ref=dst, send_sem=send_sems.at[k], recv_sem=recv_sems.at[k],
                                             device_id=dev, device_id_type=MESH_ID).start()
                k += 1
        token[...] = jnp.zeros_like(token)

    n_copies = per_array * W
    hbm_of = lambda xs: tuple(pltpu.HBM(a.shape, a.dtype) for a in xs)
    out = pl.pallas_call(
        body, name=name,
        out_shape=(pltpu.SemaphoreType.DMA((n_copies,)), pltpu.SemaphoreType.DMA((n_copies,)))
        + hbm_of(arrays) + hbm_of(lands) + (jax.ShapeDtypeStruct((8, LANE), F32),),
        in_specs=(HBM_SPEC,) * (2 * W) + (ANY_SPEC,) * len(after),
        out_specs=(SEM_SPEC, SEM_SPEC) + (HBM_SPEC,) * (2 * W) + (pl.BlockSpec(memory_space=pltpu.VMEM),),
        input_output_aliases={i: 2 + i for i in range(2 * W)},
        compiler_params=pltpu.CompilerParams(has_side_effects=SPLIT_EFFECT),
    )(*[pltpu.with_memory_space_constraint(a, pltpu.HBM) for a in arrays], *lands, *after)
    return out[0], out[1], list(out[2:2 + W]), list(out[2 + W:2 + 2 * W]), out[-1]


def _split_wait(name, state, after, copies_of):
    send_sems, recv_sems, arrays, lands, _ = state
    W = len(arrays)
    after = tuple(after) if isinstance(after, (tuple, list)) else (after,)

    def body(*refs):
        x_refs, land_refs = refs[:W], refs[W:2 * W]
        send_sems, recv_sems = refs[2 * W], refs[2 * W + 1]
        k = 0
        for w in range(W):
            for src, dst, dev in copies_of(w, x_refs[w], land_refs[w]):
                cp = pltpu.make_async_remote_copy(src_ref=src, dst_ref=dst, send_sem=send_sems.at[k],
                                                  recv_sem=recv_sems.at[k], device_id=dev, device_id_type=MESH_ID)
                cp.wait_send()
                cp.wait_recv()
                k += 1

    out = pl.pallas_call(
        body, name=name,
        out_shape=tuple(pltpu.HBM(a.shape, a.dtype) for a in arrays + lands),
        in_specs=(HBM_SPEC,) * (2 * W) + (SEM_SPEC, SEM_SPEC) + (ANY_SPEC,) * len(after),
        out_specs=(HBM_SPEC,) * (2 * W),
        input_output_aliases={i: i for i in range(2 * W)},
        compiler_params=pltpu.CompilerParams(has_side_effects=SPLIT_EFFECT),
    )(*arrays, *lands, send_sems, recv_sems, *after)
    return list(out[:W]), list(out[W:])


def _scatter_copies(w, p_ref, land_ref):
    x, y, c = _my_place()
    my_chip = 2 * x + y
    return [(p_ref.at[2 * (x ^ (k >> 1)) + (y ^ (k & 1))], land_ref.at[my_chip], (x ^ (k >> 1), y ^ (k & 1), c))
            for k in range(1, 4)]


def _gather_copies(w, x_ref, land_ref):
    x, y, c = _my_place()
    me = _linear((x, y, c))
    devs = [(x, y, 1 - c)] + [(x ^ (k >> 1), y ^ (k & 1), c) for k in range(1, 4)]
    return [(x_ref, land_ref.at[me], d) for d in devs]


def _gather_forward(lands, name):
    W = len(lands)

    def body(*refs):
        land_refs, out_refs, (send_sems, recv_sems) = refs[:W], refs[W:2 * W], refs[2 * W:]
        x, y, c = _my_place()
        sibling = (x, y, 1 - c)
        sends, arrivals = [], []
        for w in range(W):
            for k in range(1, 4):
                px, py = x ^ (k >> 1), y ^ (k & 1)
                landed, theirs = _linear((px, py, c)), out_refs[w].at[_linear((px, py, 1 - c))]
                sem = 3 * w + k - 1
                sends.append(pltpu.make_async_remote_copy(
                    src_ref=land_refs[w].at[landed], dst_ref=out_refs[w].at[landed],
                    send_sem=send_sems.at[sem], recv_sem=recv_sems.at[sem], device_id=sibling, device_id_type=MESH_ID))
                arrivals.append(pltpu.make_async_remote_copy(
                    src_ref=theirs, dst_ref=theirs, send_sem=send_sems.at[sem], recv_sem=recv_sems.at[sem],
                    device_id=sibling, device_id_type=MESH_ID))
        for cp in sends:
            cp.start()
        for cp in arrivals:
            cp.wait_recv()
        for cp in sends:
            cp.wait_send()

    return pl.pallas_call(
        body, name=name,
        in_specs=[HBM_SPEC] * W, out_specs=[HBM_SPEC] * W,
        out_shape=[jax.ShapeDtypeStruct(l.shape, l.dtype) for l in lands],
        input_output_aliases={i: i for i in range(W)},
        scratch_shapes=[pltpu.SemaphoreType.DMA((3 * W,)), pltpu.SemaphoreType.DMA((3 * W,))],
    )(*lands)


def _with_own_slot(gathered, shard):
    return lax.dynamic_update_index_in_dim(gathered, shard[None], _linear(_my_place()), axis=0)


def _in_chip_copies(w, p_ref, land_ref):
    x, y, c = _my_place()
    return [(p_ref.at[2 * q + (1 - c)], land_ref.at[q], (x, y, 1 - c)) for q in range(4)]


def _in_chip_start(parts, tag):
    lands = [_landing_zone((4,) + p.shape[1:], p.dtype) for p in parts]
    return _split_start("grad_in_chip_start_" + tag, parts, lands, (), _in_chip_copies, 4)


def _reduce_scatter_begin(parts, tag, in_chip_state=None, after=()):
    if in_chip_state is None:
        got = _exchange_in_chip(parts)
    else:
        parts, got = _split_wait("grad_in_chip_wait_" + tag, in_chip_state, after, _in_chip_copies)
    core = lax.axis_index("c").astype(jnp.int32).reshape(1)
    chip_parts = [_pair_sum(p, g, core) for p, g in zip(parts, got)]
    lands = [_landing_zone(p.shape, p.dtype) for p in chip_parts]
    return _split_start("grad_scatter_start_" + tag, chip_parts, lands, got[0], _scatter_copies, 3)


def _reduce_scatter_end(state, after, tag):
    return _split_wait("grad_scatter_wait_" + tag, state, after, _scatter_copies)


def kernel(x, c, positions, w_ada, b_ada, w_in, g_q_a, w_q_b, g_kv_a, w_kv_b, w_o_a, w_conv, w_o_b, w_o, ln1_g, ln1_b, w_ffn_in, w_ffn_out, ln2_g, ln2_b, loss_target, m_w_ada, m_b_ada, m_w_in, m_g_q_a, m_w_q_b, m_g_kv_a, m_w_kv_b, m_w_o_a, m_w_conv, m_w_o_b, m_w_o, m_ln1_g, m_ln1_b, m_w_ffn_in, m_w_ffn_out, m_ln2_g, m_ln2_b, v_w_ada, v_b_ada, v_w_in, v_g_q_a, v_w_q_b, v_g_kv_a, v_w_kv_b, v_w_o_a, v_w_conv, v_w_o_b, v_w_o, v_ln1_g, v_ln1_b, v_w_ffn_in, v_w_ffn_out, v_ln2_g, v_ln2_b):
    x2, tgt = x[0], loss_target[0]
    S, D = x2.shape
    Lq, Lkv = g_q_a.shape[1], g_kv_a.shape[1]
    H = w_q_b.shape[2] * N_DEV // QK_CAT
    F = w_ffn_out.shape[1] * N_DEV
    assert Lq == Lkv and (Lq + Lkv) % COL_BLOCK == 0 and D % COL_BLOCK == 0
    front = Lq + Lkv + QK_ROPE
    front_pad = _round_up(front, COL_BLOCK)
    kr_blk = (Lq + Lkv) // COL_BLOCK
    blk_b = front_pad // COL_BLOCK
    nblk = D // COL_BLOCK
    blk_c, blk_x, blk_ga, blk_gb = blk_b + nblk, blk_b + 2 * nblk, blk_b + 3 * nblk, blk_b + 4 * nblk
    ts = _tile(S, 256, 8)
    T = _tile(S, min(512, S // 2), CHUNK)
    tb = _tile(F, 2816)
    me = _linear(_my_place())

    cw = w_ada.shape[2]
    b_mine = lax.dynamic_slice(b_ada, (0, me * cw), (1, cw)).reshape(1, 1, cw)
    mod_blocks, cact_all, wconv_all = _ada_fwd(c.reshape(1, 1, D), w_conv[0].reshape(1, 1, -1), w_ada[0], b_mine)
    mod = mod_blocks.reshape(6, D)
    cact_all = cact_all.reshape(N_DEV, D)
    w_conv_full = wconv_all.reshape(N_DEV, CONV_K, -1).transpose(1, 0, 2).reshape(CONV_K, D)

    landing = lambda shards: [_landing_zone((N_DEV,) + s.shape, BF16) for s in shards]
    gathered = lambda lands, shards, tag: [_with_own_slot(g, s) for g, s in
                                           zip(_gather_forward(lands, tag + "_gather_forward"), shards)]
    half = D // 2
    w_in_b = w_in[0].astype(BF16)
    first, second = [w_in_b[:half]], [w_in_b[half:], w_q_b[0].astype(BF16), w_kv_b[0].astype(BF16)]
    mid = [w[0].astype(BF16) for w in (w_o_a, w_o_b, w_o)]
    last = [w[0].astype(BF16) for w in (w_ffn_in, w_ffn_out)]
    first_state = _split_start("first_gather_start", first, landing(first), mod_blocks, _gather_copies, 4)
    second_state = _split_start("second_gather_start", second, landing(second), first_state[4], _gather_copies, 4)
    u = _modulate_in(x2, mod, ts)

    first_shards, first_lands = _split_wait("first_gather_wait", first_state, (u, second_state[4]), _gather_copies)
    (g_in_top,) = gathered(first_lands, first_shards, "first")
    w_in_top = _assemble_w_in(g_in_top, front, front_pad, D, 0)
    proj_top = _matmul(u, w_in_top, "nn", F32, "proj_top", k_rows=(0, half))
    second_shards, second_lands = _split_wait("second_gather_wait", second_state, (proj_top,), _gather_copies)
    g_in_bottom, wq_s, wkv_s = gathered(second_lands, second_shards, "second")
    mid_state = _split_start("mid_gather_start", mid, landing(mid), g_in_bottom, _gather_copies, 4)
    last_state = _split_start("last_gather_start", last, landing(last), mid_state[4], _gather_copies, 4)
    w_in_p = _assemble_w_in(g_in_bottom, front, front_pad, D, half, into=w_in_top)

    inv_freq = 1.0 / (ROPE_THETA ** (jnp.arange(0, QK_ROPE, 2, dtype=F32) / QK_ROPE))
    ang = positions[0].astype(F32)[:, None] * inv_freq
    cos2 = jnp.concatenate([jnp.cos(ang), jnp.cos(ang)], axis=-1)
    sin2 = jnp.concatenate([jnp.sin(ang), jnp.sin(ang)], axis=-1)
    one, zero = jnp.ones((S, QK_NOPE), F32), jnp.zeros((S, QK_NOPE), F32)
    cos_q, sin_q = jnp.concatenate([one, cos2, one, cos2], axis=-1), jnp.concatenate([zero, sin2, zero, sin2], axis=-1)
    cos_k, sin_k = jnp.tile(cos2, (1, COL_BLOCK // QK_ROPE)), jnp.tile(sin2, (1, COL_BLOCK // QK_ROPE))

    proj = _matmul(u, w_in_p, "nn", F32, "proj", k_rows=(half, half), init=proj_top, deps=(last_state[4],))
    qn = _rms_fwd(proj, g_q_a, 0, Lq, ts, "rms_q")
    kvn = _rms_fwd(proj, g_kv_a, 1, Lkv, ts, "rms_kv")
    q = _matmul(qn, wq_s, "nn", F32, "q_up")
    kv = _matmul(kvn, wkv_s, "nn", F32, "kv_up")
    qc, kc, vh = _qk_prep(q, kv, proj, kr_blk, cos_q, sin_q, cos_k, sin_k, H, ts)
    attn, lse = _attn_fwd(qc, kc, vh, T)
    mid_shards, mid_lands = _split_wait("mid_gather_wait", mid_state, lse, _gather_copies)
    w_oa_f, w_ob_f, w_o_f = [g.reshape(-1, D) for g in gathered(mid_lands, mid_shards, "mid")]
    ya = _matmul(attn, w_oa_f, "nn", F32, "attn_out")
    cbc = _conv_fwd(proj, w_conv_full, blk_b, blk_c, blk_x)
    yb = _matmul(cbc, w_ob_f, "nn", F32, "conv_out")
    merged = _merge_fwd(proj, ya, yb, blk_ga, blk_gb, ts)
    mix = _matmul(merged, w_o_f, "nn", F32, "mix_out")
    xhat1, rstd1, u2 = _ln1_fwd(x2, mix, mod, ln1_g, ln1_b, ts)
    last_shards, last_lands = _split_wait("last_gather_wait", last_state, u2, _gather_copies)
    w_fi_s, g_fo = gathered(last_lands, last_shards, "last")
    w_fo_f = g_fo.reshape(F, D)
    hh = _matmul(u2, w_fi_s, "nn", F32, "ffn_in")
    act = _swiglu_fwd(hh, ts, tb)
    ffn = _matmul(act, w_fo_f, "nn", F32, "ffn_out")
    loss_part, dffn, dx1a, vec2 = _ln2_loss(xhat1, ffn, tgt, mod, ln1_g, ln1_b, ln2_g, ln2_b, ts)
    loss = lax.psum(loss_part[0, 0], AXES)

    gw_fo = _matmul(act, dffn, "tn", BF16, "grad_w_ffn_out")
    da = _matmul(dffn, w_fo_f, "nt", F32, "d_act")
    dh = _swiglu_bwd(da, hh, ts, tb)
    gw_fi = _matmul(u2, dh, "tn", BF16, "grad_w_ffn_in", out_shards=True)
    ffn_in_chip = _in_chip_start([gw_fi, gw_fo.reshape(N_DEV, -1, D)], "ffn")
    du2 = _matmul(dh, w_fi_s, "nt", F32, "d_u2", deps=(ffn_in_chip[4],))
    ffn_state = _reduce_scatter_begin(None, "ffn", ffn_in_chip, after=(du2,))
    dxa, dmix, vec1 = _ln1_bwd(du2, dx1a, xhat1, rstd1, mix, mod, ln1_g, ln1_b, ts)
    gw_o = _matmul(merged, dmix, "tn", BF16, "grad_w_o", deps=(ffn_state[4],))
    dmerged = _matmul(dmix, w_o_f, "nt", F32, "d_merged")
    dya, dyb, dga, dgb = _merge_bwd(dmerged, proj, ya, yb, blk_ga, blk_gb, ts)
    gw_ob = _matmul(cbc, dyb, "tn", BF16, "grad_w_o_b")
    dcbc = _matmul(dyb, w_ob_f, "nt", F32, "d_conv")
    dcb, dcc, dcx, dwconv = _conv_bwd(dcbc, proj, w_conv_full, blk_b, blk_c, blk_x)
    gw_oa = _matmul(attn, dya, "tn", BF16, "grad_w_o_a")
    mix_in_chip = _in_chip_start([g.reshape(N_DEV, -1, D) for g in (gw_oa, gw_ob, gw_o)], "mix")
    dattn = _matmul(dya, w_oa_f, "nt", F32, "d_attn", deps=(mix_in_chip[4],))
    dqc, dkc, dvh = _attn_bwd(qc, kc, vh, dattn, attn, lse, T)
    ffn_own, ffn_got = _reduce_scatter_end(ffn_state, dqc, "ffn")
    mix_state = _reduce_scatter_begin(None, "mix", mix_in_chip, after=(dqc,))
    dq, dkv, dkr = _qk_bwd(dqc, dkc, dvh, cos_q, sin_q, cos_k, sin_k, ts)
    gw_qb = _matmul(qn, dq, "tn", BF16, "grad_w_q_b", out_shards=True, deps=(mix_state[4],))
    dqn = _matmul(dq, wq_s, "nt", F32, "d_qn")
    gw_kvb = _matmul(kvn, dkv, "tn", BF16, "grad_w_kv_b", out_shards=True)
    dkvn = _matmul(dkv, wkv_s, "nt", F32, "d_kvn")
    dqa, dgq = _rms_bwd(dqn, proj, g_q_a, 0, Lq, ts, "rms_q_bwd")
    dkva, dgkv = _rms_bwd(dkvn, proj, g_kv_a, 1, Lkv, ts, "rms_kv_bwd")
    dproj = jnp.concatenate([dqa, dkva, dkr, dcb, dcc, dcx, dga, dgb], axis=1)
    gw_in_p = _matmul(u, dproj, "tn", BF16, "grad_w_in")
    mix_own, mix_got = _reduce_scatter_end(mix_state, gw_in_p, "mix")
    in_state = _reduce_scatter_begin([_split_w_in(gw_in_p, front, front_pad), gw_qb, gw_kvb], "in")
    du = _matmul(dproj, w_in_p, "nt", F32, "d_u", deps=(in_state[4],))
    grad_x, vec0 = _grad_x(du, dxa, x2, mod, ts)

    my_chip = (2 * lax.axis_index("x") + lax.axis_index("y")).astype(jnp.int32).reshape(1)
    arrived = {}
    for nm, w, m, v, own, got in (
            ("w_ffn_in", w_ffn_in, m_w_ffn_in, v_w_ffn_in, ffn_own[0], ffn_got[0]),
            ("w_ffn_out", w_ffn_out, m_w_ffn_out, v_w_ffn_out, ffn_own[1], ffn_got[1]),
            ("w_o_a", w_o_a, m_w_o_a, v_w_o_a, mix_own[0], mix_got[0]),
            ("w_o_b", w_o_b, m_w_o_b, v_w_o_b, mix_own[1], mix_got[1]),
            ("w_o", w_o, m_w_o, v_w_o, mix_own[2], mix_got[2])):
        arrived[nm] = [a[None] for a in _adamw_reduced(w[0], own, got, m[0], v[0], my_chip, "adamw_" + nm)]

    dmod = jnp.concatenate([vec0[0], vec0[1], vec1[4], vec1[0], vec1[1], vec2[2]])
    small = jnp.concatenate([dmod, dgq[0], dgkv[0], vec1[2], vec1[3], vec2[0], vec2[1], dwconv[:CONV_K].reshape(-1)])
    n_small = small.shape[0]
    nch = _round_up(n_small, cw) // cw
    payload = jnp.pad(small, (0, nch * cw - n_small)).reshape(nch, 1, cw)
    summed, dmod_mine = _ada_bwd(payload, deps=[res[1] for res in arrived.values()])
    arrived["w_ada"] = [a[None] for a in _adamw_ada(w_ada[0], cact_all.T, dmod_mine.reshape(N_DEV, cw),
                                                    m_w_ada[0], v_w_ada[0])]
    summed = summed.reshape(-1)
    offs = [0, 6 * D, 6 * D + Lq, 6 * D + Lq + Lkv]
    offs += [offs[-1] + D * k for k in range(1, 5)]
    g_b_ada = summed[offs[0]:offs[1]].reshape(1, -1)
    g_gq = summed[offs[1]:offs[2]].reshape(1, -1)
    g_gkv = summed[offs[2]:offs[3]].reshape(1, -1)
    g_ln1g, g_ln1b, g_ln2g, g_ln2b = [summed[offs[3 + k]:offs[4 + k]].reshape(1, -1) for k in range(4)]
    wc = w_conv.shape[2]
    g_wconv = lax.dynamic_slice(summed[offs[7]:offs[7] + CONV_K * D].reshape(CONV_K, D), (0, me * wc), (CONV_K, wc))

    names = ["w_ada", "b_ada", "w_in", "g_q_a", "w_q_b", "g_kv_a", "w_kv_b", "w_o_a", "w_conv", "w_o_b", "w_o",
             "ln1_g", "ln1_b", "w_ffn_in", "w_ffn_out", "ln2_g", "ln2_b"]
    weights = [w_ada, b_ada, w_in, g_q_a, w_q_b, g_kv_a, w_kv_b, w_o_a, w_conv, w_o_b, w_o, ln1_g, ln1_b,
               w_ffn_in, w_ffn_out, ln2_g, ln2_b]
    moms = [m_w_ada, m_b_ada, m_w_in, m_g_q_a, m_w_q_b, m_g_kv_a, m_w_kv_b, m_w_o_a, m_w_conv, m_w_o_b, m_w_o,
            m_ln1_g, m_ln1_b, m_w_ffn_in, m_w_ffn_out, m_ln2_g, m_ln2_b]
    vels = [v_w_ada, v_b_ada, v_w_in, v_g_q_a, v_w_q_b, v_g_kv_a, v_w_kv_b, v_w_o_a, v_w_conv, v_w_o_b, v_w_o,
            v_ln1_g, v_ln1_b, v_w_ffn_in, v_w_ffn_out, v_ln2_g, v_ln2_b]
    grad_of = {"b_ada": g_b_ada, "g_q_a": g_gq, "g_kv_a": g_gkv, "w_conv": g_wconv,
               "ln1_g": g_ln1g, "ln1_b": g_ln1b, "ln2_g": g_ln2g, "ln2_b": g_ln2b}
    state_of = dict(zip(names, zip(weights, moms, vels)))
    results = dict(arrived)

    def update(nm, reduced=None):
        w, m, v = state_of[nm]
        shp = w.shape
        w2 = w.reshape(shp[-2], shp[-1]) if w.ndim == 3 else w
        m2, v2 = m.reshape(w2.shape), v.reshape(w2.shape)
        if reduced is None:
            g2 = grad_of[nm].reshape(w2.shape)
            res = (g2,) + tuple(_adamw(w2, g2, m2, v2, "adamw_" + nm))
        else:
            res = _adamw_reduced(w2, reduced[0], reduced[1], m2, v2, my_chip, "adamw_" + nm)
        results[nm] = [a.reshape(shp) for a in res]

    for nm in grad_of:
        update(nm)
    in_own, in_got = _reduce_scatter_end(in_state, [res[1] for res in results.values()], "in")
    for nm, own, got in zip(("w_in", "w_q_b", "w_kv_b"), in_own, in_got):
        update(nm, (own, got))
    outs = [[results[nm][k] for nm in names] for k in range(4)]
    return (loss, grad_x.reshape(x.shape), *outs[0], *outs[1], *outs[2], *outs[3])
```

```python
import functools

import jax
import jax.numpy as jnp
from jax import lax
from jax.experimental import pallas as pl
from jax.experimental.pallas import tpu as pltpu

F32 = jnp.float32
BF16 = jnp.bfloat16
MESH_ID = pl.DeviceIdType.MESH
AXES = ("x", "y", "c")
N_DEV = 8

CHUNK = 64
QK_NOPE = 128
QK_ROPE = 64
V_HEAD = 128
QK_CAT = QK_NOPE + QK_ROPE
ROPE_THETA = 10000.0
ATTN_SCALE = (QK_NOPE + QK_ROPE) ** -0.5
CONV_K = 3
DEEPNORM_ALPHA = 2.0 ** 0.25
LN_EPS = 1e-5
RMS_EPS = 1e-6
NEG_INF = -1e30

ADAM_LR = 0.001
ADAM_B1 = 0.9
ADAM_B2 = 0.999
ADAM_EPS = 1e-08
ADAM_WD = 0.01
ADAM_STEP = 10

LANE = 128
COL_BLOCK = 256
PACK_ROW_ALIGN = 16
PAIR_SUM_BLOCK = 1 << 20
VMEM_LIMIT = 48 * 1024 * 1024


def _round_up(n, m):
    return (n + m - 1) // m * m


def _tile(n, pref, align=LANE):
    best = None
    t = align
    while t <= min(n, pref):
        if n % t == 0:
            best = t
        t += align
    return best if best is not None else n


def _cparams(sem=None):
    return pltpu.CompilerParams(dimension_semantics=sem, vmem_limit_bytes=VMEM_LIMIT)


def _sigmoid(x):
    return 0.5 * jnp.tanh(0.5 * x) + 0.5


def _matmul(a, b, mode, out_dtype, name, tm=1024, tn=1024, tk=2048, deps=(), out_shards=False, k_rows=None,
            init=None):
    b_shards = b.ndim == 3
    n = b.shape[2] if b_shards else (b.shape[1] // N_DEV if out_shards else None)
    if mode == "nn":
        (M, K), (K2, N) = a.shape, (b.shape[1], N_DEV * n) if b_shards else b.shape
    elif mode == "nt":
        (M, K), (N, K2) = a.shape, (b.shape[1], N_DEV * n) if b_shards else b.shape
    else:
        (K, M), (K2, N) = a.shape, b.shape
    assert K == K2, (a.shape, b.shape, mode)
    tm = _tile(M, tm)
    tn = n if (mode != "nt" and n is not None) else _tile(N, tn)
    k_row0, k_len = k_rows if k_rows is not None else (0, K)
    tk = n if (mode == "nt" and b_shards) else _tile(k_len, tk)
    nk, k0 = k_len // tk, k_row0 // tk
    if mode == "nn":
        a_spec = pl.BlockSpec((tm, tk), lambda i, j, k: (i, k0 + k))
        b_spec = (pl.BlockSpec((1, tk, n), lambda i, j, k: (j, k, 0)) if b_shards
                  else pl.BlockSpec((tk, tn), lambda i, j, k: (k0 + k, j)))
        dims = (((1,), (0,)), ((), ()))
    elif mode == "nt":
        a_spec = pl.BlockSpec((tm, tk), lambda i, j, k: (i, k))
        b_spec = (pl.BlockSpec((1, tn, n), lambda i, j, k: (k, j, 0)) if b_shards
                  else pl.BlockSpec((tn, tk), lambda i, j, k: (j, k)))
        dims = (((1,), (1,)), ((), ()))
    else:
        a_spec = pl.BlockSpec((tk, tm), lambda i, j, k: (k, i))
        b_spec = pl.BlockSpec((tk, tn), lambda i, j, k: (k, j))
        dims = (((0,), (0,)), ((), ()))
    if out_shards:
        out_spec = pl.BlockSpec((1, tm, n), lambda i, j, k: (j, i, 0))
        out_shape = jax.ShapeDtypeStruct((N_DEV, M, n), out_dtype)
    else:
        out_spec = pl.BlockSpec((tm, tn), lambda i, j, k: (i, j))
        out_shape = jax.ShapeDtypeStruct((M, N), out_dtype)

    def product(a_ref, b_ref):
        b_blk = b_ref[0] if b_shards else b_ref[...]
        return lax.dot_general(a_ref[...].astype(BF16), b_blk.astype(BF16), dims, preferred_element_type=F32)

    def write(o_ref, value):
        if out_shards:
            o_ref[0] = value.astype(o_ref.dtype)
        else:
            o_ref[...] = value.astype(o_ref.dtype)

    def body_whole_k(a_ref, b_ref, *rest):
        value = product(a_ref, b_ref)
        write(rest[-1], value if init is None else value + rest[0][...])

    def body_split_k(a_ref, b_ref, *rest):
        o_ref, acc_ref = rest[-2:]
        k = pl.program_id(2)

        @pl.when(k == 0)
        def _():
            acc_ref[...] = jnp.zeros_like(acc_ref) if init is None else rest[0][...]

        acc_ref[...] += product(a_ref, b_ref)

        @pl.when(k == nk - 1)
        def _():
            write(o_ref, acc_ref[...])

    return pl.pallas_call(
        body_whole_k if nk == 1 else body_split_k, name=name, grid=(M // tm, N // tn, nk),
        in_specs=[a_spec, b_spec] + ([] if init is None else [out_spec]) + [ANY_SPEC] * len(deps),
        out_specs=out_spec, out_shape=out_shape,
        scratch_shapes=[] if nk == 1 else [pltpu.VMEM((tm, tn), F32)],
        compiler_params=_cparams(("parallel", "parallel", "arbitrary")),
    )(a, b, *(() if init is None else (init,)), *deps)


def _assemble_w_in(shards, front, front_pad, rows, row0, into=None):
    _, K, n = shards.shape
    gap = front_pad - front
    tk = _tile(K, 256, PACK_ROW_ALIGN)
    blk0 = row0 // tk

    def body(g_ref, *rest):
        o_ref = rest[-1]
        if gap:
            o_ref[:, front:front_pad] = jnp.zeros((tk, gap), o_ref.dtype)
        for j in range(N_DEV):
            lo, hi = j * n, (j + 1) * n
            if lo < front < hi:
                o_ref[:, lo:front] = g_ref[j, :, 0:front - lo]
                o_ref[:, front_pad:hi + gap] = g_ref[j, :, front - lo:n]
            else:
                off = 0 if hi <= front else gap
                o_ref[:, lo + off:hi + off] = g_ref[j]

    return pl.pallas_call(
        body, name="assemble_w_in", grid=(K // tk,),
        in_specs=[pl.BlockSpec((N_DEV, tk, n), lambda i: (0, i, 0))] + ([] if into is None else [ANY_SPEC]),
        out_specs=pl.BlockSpec((tk, N_DEV * n + gap), lambda i: (blk0 + i, 0)),
        out_shape=jax.ShapeDtypeStruct((rows, N_DEV * n + gap), shards.dtype),
        input_output_aliases={} if into is None else {1: 0},
        compiler_params=_cparams(("parallel",)),
    )(*([shards] if into is None else [shards, into]))


def _split_w_in(w, front, front_pad):
    K, NP = w.shape
    gap = front_pad - front
    n = (NP - gap) // N_DEV
    tk = _tile(K, 256, PACK_ROW_ALIGN)

    def body(w_ref, o_ref):
        for j in range(N_DEV):
            lo, hi = j * n, (j + 1) * n
            if lo < front < hi:
                o_ref[j, :, 0:front - lo] = w_ref[:, lo:front]
                o_ref[j, :, front - lo:n] = w_ref[:, front_pad:hi + gap]
            else:
                off = 0 if hi <= front else gap
                o_ref[j] = w_ref[:, lo + off:hi + off]

    return pl.pallas_call(
        body, name="split_grad_w_in", grid=(K // tk,),
        in_specs=[pl.BlockSpec((tk, NP), lambda i: (i, 0))],
        out_specs=pl.BlockSpec((N_DEV, tk, n), lambda i: (0, i, 0)),
        out_shape=jax.ShapeDtypeStruct((N_DEV, K, n), w.dtype),
        compiler_params=_cparams(("parallel",)),
    )(w)


def _modulate_in(x, mod, ts):
    S, D = x.shape

    def body(x_ref, mod_ref, u_ref):
        u_ref[...] = (x_ref[...] * (1.0 + mod_ref[1:2, :]) + mod_ref[0:1, :]).astype(BF16)

    return pl.pallas_call(
        body, name="modulate_in", grid=(S // ts,),
        in_specs=[pl.BlockSpec((ts, D), lambda i: (i, 0)), pl.BlockSpec((6, D), lambda i: (0, 0))],
        out_specs=pl.BlockSpec((ts, D), lambda i: (i, 0)),
        out_shape=jax.ShapeDtypeStruct((S, D), BF16),
        compiler_params=_cparams(("parallel",)),
    )(x, mod)


def _rms_fwd(proj, g, blk, L, ts, name):
    S = proj.shape[0]

    def body(a_ref, g_ref, y_ref):
        a = a_ref[...]
        r = lax.rsqrt(jnp.mean(a * a, axis=-1, keepdims=True) + RMS_EPS)
        y_ref[...] = (a * r * g_ref[...]).astype(BF16)

    return pl.pallas_call(
        body, name=name, grid=(S // ts,),
        in_specs=[pl.BlockSpec((ts, L), lambda i: (i, blk)), pl.BlockSpec((1, L), lambda i: (0, 0))],
        out_specs=pl.BlockSpec((ts, L), lambda i: (i, 0)),
        out_shape=jax.ShapeDtypeStruct((S, L), BF16),
        compiler_params=_cparams(("parallel",)),
    )(proj, g)


def _rope_partner(x, period, start):
    w = x.shape[-1]
    lane = lax.broadcasted_iota(jnp.int32, x.shape, x.ndim - 1) % period
    first = (lane >= start) & (lane < start + QK_ROPE // 2)
    from_right = pltpu.roll(x, w - QK_ROPE // 2, axis=x.ndim - 1)
    from_left = pltpu.roll(x, QK_ROPE // 2, axis=x.ndim - 1)
    return jnp.where(first, -from_right, from_left)


def _qk_prep(q, kv, proj, kr_blk, cos_q, sin_q, cos_k, sin_k, H, ts):
    S = q.shape[0]
    pair = 2 * QK_CAT
    kv_w = QK_NOPE + V_HEAD

    def body(q_ref, kv_ref, kr_ref, cq_ref, sq_ref, ck_ref, sk_ref, qc_ref, kc_ref, vh_ref):
        kr = kr_ref[...]
        kr = kr * ck_ref[...] + _rope_partner(kr, QK_ROPE, 0) * sk_ref[...]
        kr = kr[:, :QK_ROPE].astype(BF16)
        for p in range(H // 2):
            x = q_ref[:, p * pair:(p + 1) * pair]
            x = x * cq_ref[...] + _rope_partner(x, QK_CAT, QK_NOPE) * sq_ref[...]
            qc_ref[2 * p] = x[:, :QK_CAT].astype(BF16)
            qc_ref[2 * p + 1] = x[:, QK_CAT:].astype(BF16)
        for h in range(H):
            kc_ref[h, :, 0:QK_NOPE] = kv_ref[:, h * kv_w:h * kv_w + QK_NOPE].astype(BF16)
            kc_ref[h, :, QK_NOPE:QK_CAT] = kr
            vh_ref[h, :, :] = kv_ref[:, h * kv_w + QK_NOPE:(h + 1) * kv_w].astype(BF16)

    row = lambda w: pl.BlockSpec((ts, w), lambda i: (i, 0))
    return pl.pallas_call(
        body, name="qk_prep", grid=(S // ts,),
        in_specs=[row(H * QK_CAT), row(H * kv_w),
                  pl.BlockSpec((ts, COL_BLOCK), lambda i: (i, kr_blk)),
                  row(pair), row(pair), row(COL_BLOCK), row(COL_BLOCK)],
        out_specs=[pl.BlockSpec((H, ts, QK_CAT), lambda i: (0, i, 0)),
                   pl.BlockSpec((H, ts, QK_CAT), lambda i: (0, i, 0)),
                   pl.BlockSpec((H, ts, V_HEAD), lambda i: (0, i, 0))],
        out_shape=[jax.ShapeDtypeStruct((H, S, QK_CAT), BF16), jax.ShapeDtypeStruct((H, S, QK_CAT), BF16),
                   jax.ShapeDtypeStruct((H, S, V_HEAD), BF16)],
        compiler_params=_cparams(("parallel",)),
    )(q, kv, proj, cos_q, sin_q, cos_k, sin_k)


NT_DIMS = (((1,), (1,)), ((), ()))
TN_DIMS = (((0,), (0,)), ((), ()))


def _diag_mask(T):
    rows = lax.broadcasted_iota(jnp.int32, (T, T), 0) // CHUNK
    cols = lax.broadcasted_iota(jnp.int32, (T, T), 1) // CHUNK
    return cols <= rows


def _attn_fwd(qc, kc, vh, T):
    H, S, _ = qc.shape
    n = S // T

    def body(q_ref, k_ref, v_ref, o_ref, lse_ref):
        q = q_ref[0]

        def block(i):
            L = (i + 1) * T
            s_old = lax.dot_general(q, k_ref[0, 0:i * T, :], NT_DIMS, preferred_element_type=F32) if i else None
            s_diag = lax.dot_general(q, k_ref[0, i * T:L, :], NT_DIMS, preferred_element_type=F32)
            s_diag = jnp.where(_diag_mask(T), s_diag, NEG_INF)
            m = jnp.max(s_diag, axis=-1, keepdims=True)
            if i:
                m = jnp.maximum(m, jnp.max(s_old, axis=-1, keepdims=True))
            p_diag = jnp.exp((s_diag - m) * ATTN_SCALE)
            l = jnp.sum(p_diag, axis=-1, keepdims=True)
            acc = jnp.dot(p_diag.astype(BF16), v_ref[0, i * T:L, :], preferred_element_type=F32)
            if i:
                p_old = jnp.exp((s_old - m) * ATTN_SCALE)
                l = l + jnp.sum(p_old, axis=-1, keepdims=True)
                acc = acc + jnp.dot(p_old.astype(BF16), v_ref[0, 0:i * T, :], preferred_element_type=F32)
            o_ref[...] = acc / l
            lse_ref[0] = m * ATTN_SCALE + jnp.log(l)

        for i in range(n):
            pl.when(pl.program_id(1) == i)(functools.partial(block, i))

    return pl.pallas_call(
        body, name="attn_fwd", grid=(H, n),
        in_specs=[pl.BlockSpec((1, T, QK_CAT), lambda h, i: (h, i, 0)),
                  pl.BlockSpec((1, S, QK_CAT), lambda h, i: (h, 0, 0)),
                  pl.BlockSpec((1, S, V_HEAD), lambda h, i: (h, 0, 0))],
        out_specs=[pl.BlockSpec((T, V_HEAD), lambda h, i: (i, h)),
                   pl.BlockSpec((1, T, 1), lambda h, i: (h, i, 0))],
        out_shape=[jax.ShapeDtypeStruct((S, H * V_HEAD), F32), jax.ShapeDtypeStruct((H, S, 1), F32)],
        compiler_params=_cparams(("parallel", "arbitrary")),
    )(qc, kc, vh)


def _shift_rows(z, k):
    if k == 0:
        return z
    n = z.shape[0]
    row = lax.broadcasted_iota(jnp.int32, z.shape, 0)
    if k > 0:
        return jnp.where(row >= k, pltpu.roll(z, k, axis=0), 0.0)
    return jnp.where(row < n + k, pltpu.roll(z, n + k, axis=0), 0.0)


def _conv_fwd(proj, w_conv, blk_b, blk_c, blk_x):
    S = proj.shape[0]
    D = w_conv.shape[1]
    nb = D // COL_BLOCK

    def body(cb_ref, cc_ref, cx_ref, w_ref, o_ref):
        z = cc_ref[...] * cx_ref[...]
        conv = w_ref[2:3, :] * z + w_ref[1:2, :] * _shift_rows(z, 1) + w_ref[0:1, :] * _shift_rows(z, 2)
        o_ref[...] = (cb_ref[...] * conv).astype(BF16)

    col = lambda off: pl.BlockSpec((S, COL_BLOCK), lambda j: (0, off + j))
    return pl.pallas_call(
        body, name="conv_fwd", grid=(nb,),
        in_specs=[col(blk_b), col(blk_c), col(blk_x), pl.BlockSpec((CONV_K, COL_BLOCK), lambda j: (0, j))],
        out_specs=pl.BlockSpec((S, COL_BLOCK), lambda j: (0, j)),
        out_shape=jax.ShapeDtypeStruct((S, D), BF16),
        compiler_params=_cparams(("parallel",)),
    )(proj, proj, proj, w_conv)


def _merge_fwd(proj, ya, yb, blk_ga, blk_gb, ts):
    S, D = ya.shape
    nb = D // COL_BLOCK

    def body(ga_ref, gb_ref, ya_ref, yb_ref, o_ref):
        o_ref[...] = (_sigmoid(ga_ref[...]) * ya_ref[...] + _sigmoid(gb_ref[...]) * yb_ref[...]).astype(BF16)

    row = pl.BlockSpec((ts, D), lambda i: (i, 0))
    seg = lambda blk: pl.BlockSpec((pl.Element(ts), pl.Element(D)), lambda i: (i * ts, blk * COL_BLOCK))
    return pl.pallas_call(
        body, name="merge_fwd", grid=(S // ts,),
        in_specs=[seg(blk_ga), seg(blk_gb), row, row],
        out_specs=row,
        out_shape=jax.ShapeDtypeStruct((S, D), BF16),
        compiler_params=_cparams(("parallel",)),
    )(proj, proj, ya, yb)


def _ln1_fwd(x, mix, mod, g, b, ts):
    S, D = x.shape

    def body(x_ref, mix_ref, mod_ref, g_ref, b_ref, xhat_ref, rstd_ref, u2_ref):
        r = DEEPNORM_ALPHA * x_ref[...] + mod_ref[2:3, :] * mix_ref[...]
        mu = jnp.mean(r, axis=-1, keepdims=True)
        d = r - mu
        rstd = lax.rsqrt(jnp.mean(d * d, axis=-1, keepdims=True) + LN_EPS)
        xhat = d * rstd
        xhat_ref[...] = xhat
        rstd_ref[...] = rstd
        x1 = xhat * g_ref[...] + b_ref[...]
        u2_ref[...] = (x1 * (1.0 + mod_ref[4:5, :]) + mod_ref[3:4, :]).astype(BF16)

    row = pl.BlockSpec((ts, D), lambda i: (i, 0))
    vec = lambda r: pl.BlockSpec((r, D), lambda i: (0, 0))
    return pl.pallas_call(
        body, name="ln1_fwd", grid=(S // ts,),
        in_specs=[row, row, vec(6), vec(1), vec(1)],
        out_specs=[row, pl.BlockSpec((ts, 1), lambda i: (i, 0)), row],
        out_shape=[jax.ShapeDtypeStruct((S, D), F32), jax.ShapeDtypeStruct((S, 1), F32),
                   jax.ShapeDtypeStruct((S, D), BF16)],
        compiler_params=_cparams(("parallel",)),
    )(x, mix, mod, g, b)


def _swiglu_fwd(h, ts, tb):
    S, F2 = h.shape
    F = F2 // 2
    nb = F // tb

    def body(hg_ref, hu_ref, a_ref):
        hg = hg_ref[...]
        a_ref[...] = (hg * _sigmoid(hg) * hu_ref[...]).astype(BF16)

    return pl.pallas_call(
        body, name="swiglu_fwd", grid=(S // ts, nb),
        in_specs=[pl.BlockSpec((ts, tb), lambda i, j: (i, j)), pl.BlockSpec((ts, tb), lambda i, j: (i, j + nb))],
        out_specs=pl.BlockSpec((ts, tb), lambda i, j: (i, j)),
        out_shape=jax.ShapeDtypeStruct((S, F), BF16),
        compiler_params=_cparams(("parallel", "parallel")),
    )(h, h)


def _ln2_loss(xhat1, ffn, tgt, mod, g1, b1, g2, b2, ts):
    S, D = xhat1.shape

    def body(xh_ref, ffn_ref, t_ref, mod_ref, g1_ref, b1_ref, g2_ref, b2_ref, loss_ref, dffn_ref, dx1_ref, vec_ref):
        i = pl.program_id(0)

        @pl.when(i == 0)
        def _():
            loss_ref[...] = jnp.zeros_like(loss_ref)
            vec_ref[...] = jnp.zeros_like(vec_ref)

        x1 = xh_ref[...] * g1_ref[...] + b1_ref[...]
        ffn = ffn_ref[...]
        r = DEEPNORM_ALPHA * x1 + mod_ref[5:6, :] * ffn
        mu = jnp.mean(r, axis=-1, keepdims=True)
        d = r - mu
        rstd = lax.rsqrt(jnp.mean(d * d, axis=-1, keepdims=True) + LN_EPS)
        xhat = d * rstd
        e = xhat * g2_ref[...] + b2_ref[...] - t_ref[...]
        loss_ref[...] += 0.5 * jnp.sum(jnp.mean(e * e, axis=-1, keepdims=True))
        dy = e * (1.0 / D)
        dxhat = dy * g2_ref[...]
        dr = rstd * (dxhat - jnp.mean(dxhat, axis=-1, keepdims=True)
                     - xhat * jnp.mean(dxhat * xhat, axis=-1, keepdims=True))
        dffn_ref[...] = (dr * mod_ref[5:6, :]).astype(BF16)
        dx1_ref[...] = DEEPNORM_ALPHA * dr
        vec_ref[0:1, :] += jnp.sum(dy * xhat, axis=0, keepdims=True)
        vec_ref[1:2, :] += jnp.sum(dy, axis=0, keepdims=True)
        vec_ref[2:3, :] += jnp.sum(dr * ffn, axis=0, keepdims=True)

    row = pl.BlockSpec((ts, D), lambda i: (i, 0))
    vec = lambda r: pl.BlockSpec((r, D), lambda i: (0, 0))
    return pl.pallas_call(
        body, name="ln2_loss", grid=(S // ts,),
        in_specs=[row, row, row, vec(6), vec(1), vec(1), vec(1), vec(1)],
        out_specs=[pl.BlockSpec((1, LANE), lambda i: (0, 0)), row, row, vec(8)],
        out_shape=[jax.ShapeDtypeStruct((1, LANE), F32), jax.ShapeDtypeStruct((S, D), BF16),
                   jax.ShapeDtypeStruct((S, D), F32), jax.ShapeDtypeStruct((8, D), F32)],
        compiler_params=_cparams(("arbitrary",)),
    )(xhat1, ffn, tgt, mod, g1, b1, g2, b2)


SWIGLU_ROWS = 16
SWIGLU_LANES = 256


def _swiglu_bwd(dffn, w_out, h, tm=1024, tn=512):
    S, D = dffn.shape
    F = w_out.shape[0]
    tm, tn = _tile(S, tm, SWIGLU_ROWS), _tile(F, tn, SWIGLU_LANES)
    nb = F // tn

    def body(d_ref, w_ref, hg_ref, hu_ref, dh_ref, da_ref):
        half = pl.program_id(2)

        @pl.when(half == 0)
        def _():
            da_ref[...] = lax.dot_general(d_ref[...], w_ref[...], NT_DIMS, preferred_element_type=F32)

        def pieces(gate_half):
            def rows(r, carry):
                sl = pl.ds(pl.multiple_of(r * SWIGLU_ROWS, SWIGLU_ROWS), SWIGLU_ROWS)
                for c0 in range(0, tn, SWIGLU_LANES):
                    cols = slice(c0, c0 + SWIGLU_LANES)
                    hg, da = hg_ref[sl, cols], da_ref[sl, cols]
                    sg = _sigmoid(hg)
                    if gate_half:
                        val = da * hu_ref[sl, cols] * (sg * (1.0 + hg * (1.0 - sg)))
                    else:
                        val = da * hg * sg
                    dh_ref[sl, cols] = val.astype(BF16)
                return carry
            lax.fori_loop(0, tm // SWIGLU_ROWS, rows, 0)

        pl.when(half == 0)(functools.partial(pieces, True))
        pl.when(half == 1)(functools.partial(pieces, False))

    return pl.pallas_call(
        body, name="swiglu_bwd", grid=(S // tm, nb, 2),
        in_specs=[pl.BlockSpec((tm, D), lambda i, j, k: (i, 0)), pl.BlockSpec((tn, D), lambda i, j, k: (j, 0)),
                  pl.BlockSpec((tm, tn), lambda i, j, k: (i, j)), pl.BlockSpec((tm, tn), lambda i, j, k: (i, j + nb))],
        out_specs=pl.BlockSpec((tm, tn), lambda i, j, k: (i, j + nb * k)),
        out_shape=jax.ShapeDtypeStruct((S, 2 * F), BF16),
        scratch_shapes=[pltpu.VMEM((tm, tn), F32)],
        compiler_params=_cparams(("parallel", "parallel", "arbitrary")),
    )(dffn, w_out, h, h)


def _ln1_bwd(du2, dx1a, xhat1, rstd1, mix, mod, g1, b1, ts):
    S, D = xhat1.shape

    def body(du2_ref, dx1a_ref, xh_ref, rstd_ref, mix_ref, mod_ref, g_ref, b_ref, dxa_ref, dmix_ref, vec_ref):
        i = pl.program_id(0)

        @pl.when(i == 0)
        def _():
            vec_ref[...] = jnp.zeros_like(vec_ref)

        xhat, du2, mix = xh_ref[...], du2_ref[...], mix_ref[...]
        x1 = xhat * g_ref[...] + b_ref[...]
        dx1 = dx1a_ref[...] + du2 * (1.0 + mod_ref[4:5, :])
        dxhat = dx1 * g_ref[...]
        dr = rstd_ref[...] * (dxhat - jnp.mean(dxhat, axis=-1, keepdims=True)
                              - xhat * jnp.mean(dxhat * xhat, axis=-1, keepdims=True))
        dxa_ref[...] = DEEPNORM_ALPHA * dr
        dmix_ref[...] = (dr * mod_ref[2:3, :]).astype(BF16)
        vec_ref[0:1, :] += jnp.sum(du2, axis=0, keepdims=True)
        vec_ref[1:2, :] += jnp.sum(du2 * x1, axis=0, keepdims=True)
        vec_ref[2:3, :] += jnp.sum(dx1 * xhat, axis=0, keepdims=True)
        vec_ref[3:4, :] += jnp.sum(dx1, axis=0, keepdims=True)
        vec_ref[4:5, :] += jnp.sum(dr * mix, axis=0, keepdims=True)

    row = pl.BlockSpec((ts, D), lambda i: (i, 0))
    vec = lambda r: pl.BlockSpec((r, D), lambda i: (0, 0))
    return pl.pallas_call(
        body, name="ln1_bwd", grid=(S // ts,),
        in_specs=[row, row, row, pl.BlockSpec((ts, 1), lambda i: (i, 0)), row, vec(6), vec(1), vec(1)],
        out_specs=[row, row, vec(8)],
        out_shape=[jax.ShapeDtypeStruct((S, D), F32), jax.ShapeDtypeStruct((S, D), BF16),
                   jax.ShapeDtypeStruct((8, D), F32)],
        compiler_params=_cparams(("arbitrary",)),
    )(du2, dx1a, xhat1, rstd1, mix, mod, g1, b1)


def _merge_bwd(dmerged, proj, ya, yb, blk_ga, blk_gb, ts):
    S, D = ya.shape
    nb = D // COL_BLOCK

    def body(dm_ref, ga_ref, gb_ref, ya_ref, yb_ref, dya_ref, dyb_ref, dga_ref, dgb_ref):
        dm = dm_ref[...]
        sa, sb = _sigmoid(ga_ref[...]), _sigmoid(gb_ref[...])
        dya_ref[...] = (dm * sa).astype(BF16)
        dyb_ref[...] = (dm * sb).astype(BF16)
        dga_ref[...] = (dm * ya_ref[...] * sa * (1.0 - sa)).astype(BF16)
        dgb_ref[...] = (dm * yb_ref[...] * sb * (1.0 - sb)).astype(BF16)

    row = pl.BlockSpec((ts, D), lambda i: (i, 0))
    seg = lambda blk: pl.BlockSpec((pl.Element(ts), pl.Element(D)), lambda i: (i * ts, blk * COL_BLOCK))
    out = jax.ShapeDtypeStruct((S, D), BF16)
    return pl.pallas_call(
        body, name="merge_bwd", grid=(S // ts,),
        in_specs=[row, seg(blk_ga), seg(blk_gb), row, row],
        out_specs=[row] * 4,
        out_shape=[out] * 4,
        compiler_params=_cparams(("parallel",)),
    )(dmerged, proj, proj, ya, yb)


def _conv_bwd(dcbc, proj, w_conv, blk_b, blk_c, blk_x):
    S = proj.shape[0]
    D = w_conv.shape[1]
    nb = D // COL_BLOCK

    def body(d_ref, cb_ref, cc_ref, cx_ref, w_ref, dcb_ref, dcc_ref, dcx_ref, dw_ref):
        d, cc, cx = d_ref[...], cc_ref[...], cx_ref[...]
        z = cc * cx
        z1, z2 = _shift_rows(z, 1), _shift_rows(z, 2)
        conv = w_ref[2:3, :] * z + w_ref[1:2, :] * z1 + w_ref[0:1, :] * z2
        dcb_ref[...] = (d * conv).astype(BF16)
        dconv = d * cb_ref[...]
        dz = w_ref[2:3, :] * dconv + w_ref[1:2, :] * _shift_rows(dconv, -1) + w_ref[0:1, :] * _shift_rows(dconv, -2)
        dcc_ref[...] = (dz * cx).astype(BF16)
        dcx_ref[...] = (dz * cc).astype(BF16)
        dw_ref[...] = jnp.zeros_like(dw_ref)
        dw_ref[0:1, :] = jnp.sum(dconv * z2, axis=0, keepdims=True)
        dw_ref[1:2, :] = jnp.sum(dconv * z1, axis=0, keepdims=True)
        dw_ref[2:3, :] = jnp.sum(dconv * z, axis=0, keepdims=True)

    col = lambda off: pl.BlockSpec((S, COL_BLOCK), lambda j: (0, off + j))
    out = jax.ShapeDtypeStruct((S, D), BF16)
    return pl.pallas_call(
        body, name="conv_bwd", grid=(nb,),
        in_specs=[col(0), col(blk_b), col(blk_c), col(blk_x), pl.BlockSpec((CONV_K, COL_BLOCK), lambda j: (0, j))],
        out_specs=[col(0), col(0), col(0), pl.BlockSpec((8, COL_BLOCK), lambda j: (0, j))],
        out_shape=[out, out, out, jax.ShapeDtypeStruct((8, D), F32)],
        compiler_params=_cparams(("parallel",)),
    )(dcbc, proj, proj, proj, w_conv)


def _attn_bwd(qc, kc, vh, do, o, lse, T):
    H, S, _ = qc.shape
    n = S // T

    def body(q_ref, k_ref, v_ref, do_ref, o_ref, lse_ref, dq_ref, dk_ref, dv_ref, d_ref, dk_acc, dv_acc):
        j = pl.program_id(1)

        @pl.when(j == 0)
        def _():
            dq_ref[...] = jnp.zeros_like(dq_ref)
            d_ref[...] = jnp.sum(do_ref[...] * o_ref[...], axis=-1, keepdims=True)

        dk_acc[...] = jnp.zeros_like(dk_acc)
        dv_acc[...] = jnp.zeros_like(dv_acc)
        k, v = k_ref[0], v_ref[0]

        def step(i, masked):
            rows = pl.ds(pl.multiple_of(i * T, T), T)
            q = q_ref[0, rows, :]
            do = do_ref[rows, :].astype(BF16)
            s = lax.dot_general(q, k, NT_DIMS, preferred_element_type=F32) * ATTN_SCALE
            if masked:
                s = jnp.where(_diag_mask(T), s, NEG_INF)
            p = jnp.exp(s - lse_ref[0, rows, :])
            dv_acc[...] += lax.dot_general(p.astype(BF16), do, TN_DIMS, preferred_element_type=F32)
            dp = lax.dot_general(do, v, NT_DIMS, preferred_element_type=F32)
            ds = (p * (dp - d_ref[rows, :]) * ATTN_SCALE).astype(BF16)
            dk_acc[...] += lax.dot_general(ds, q, TN_DIMS, preferred_element_type=F32)
            dq_ref[0, rows, :] += jnp.dot(ds, k, preferred_element_type=F32)

        def above(i, carry):
            step(i, False)
            return carry

        step(j, True)
        lax.fori_loop(j + 1, n, above, 0)
        dk_ref[0] = dk_acc[...]
        dv_ref[0] = dv_acc[...]

    head = lambda w: pl.BlockSpec((1, S, w), lambda h, j: (h, 0, 0))
    blk = lambda w: pl.BlockSpec((1, T, w), lambda h, j: (h, j, 0))
    ospec = pl.BlockSpec((S, V_HEAD), lambda h, j: (0, h))
    return pl.pallas_call(
        body, name="attn_bwd", grid=(H, n),
        in_specs=[head(QK_CAT), blk(QK_CAT), blk(V_HEAD), ospec, ospec, head(1)],
        out_specs=[head(QK_CAT), blk(QK_CAT), blk(V_HEAD)],
        out_shape=[jax.ShapeDtypeStruct((H, S, QK_CAT), F32), jax.ShapeDtypeStruct((H, S, QK_CAT), F32),
                   jax.ShapeDtypeStruct((H, S, V_HEAD), F32)],
        scratch_shapes=[pltpu.VMEM((S, 1), F32), pltpu.VMEM((T, QK_CAT), F32), pltpu.VMEM((T, V_HEAD), F32)],
        compiler_params=_cparams(("parallel", "arbitrary")),
    )(qc, kc, vh, do, o, lse)


def _qk_bwd(dqc, dkc, dvh, cos_q, sin_q, cos_k, sin_k, ts):
    H, S, _ = dqc.shape
    pair = 2 * QK_CAT
    kv_w = QK_NOPE + V_HEAD

    def body(dqc_ref, dkc_ref, dvh_ref, cq_ref, sq_ref, ck_ref, sk_ref, dq_ref, dkv_ref, dkr_ref, q_buf, kr_buf):
        for p in range(H // 2):
            q_buf[:, :QK_CAT] = dqc_ref[2 * p]
            q_buf[:, QK_CAT:] = dqc_ref[2 * p + 1]
            g = q_buf[...]
            dq_ref[:, p * pair:(p + 1) * pair] = (
                g * cq_ref[...] - _rope_partner(g, QK_CAT, QK_NOPE) * sq_ref[...]).astype(BF16)
        kr_sum = jnp.zeros((ts, QK_ROPE), F32)
        for h in range(H):
            dkv_ref[:, h * kv_w:h * kv_w + QK_NOPE] = dkc_ref[h, :, 0:QK_NOPE].astype(BF16)
            dkv_ref[:, h * kv_w + QK_NOPE:(h + 1) * kv_w] = dvh_ref[h].astype(BF16)
            kr_sum = kr_sum + dkc_ref[h, :, QK_NOPE:QK_CAT]
        kr_buf[...] = jnp.zeros_like(kr_buf)
        kr_buf[:, 0:QK_ROPE] = kr_sum
        kr = kr_buf[...]
        dkr_ref[...] = (kr * ck_ref[...] - _rope_partner(kr, QK_ROPE, 0) * sk_ref[...]).astype(BF16)

    row = lambda w: pl.BlockSpec((ts, w), lambda i: (i, 0))
    head = lambda w: pl.BlockSpec((H, ts, w), lambda i: (0, i, 0))
    return pl.pallas_call(
        body, name="qk_bwd", grid=(S // ts,),
        in_specs=[head(QK_CAT), head(QK_CAT), head(V_HEAD), row(pair), row(pair), row(COL_BLOCK), row(COL_BLOCK)],
        out_specs=[row(H * QK_CAT), row(H * kv_w), row(COL_BLOCK)],
        out_shape=[jax.ShapeDtypeStruct((S, H * QK_CAT), BF16), jax.ShapeDtypeStruct((S, H * kv_w), BF16),
                   jax.ShapeDtypeStruct((S, COL_BLOCK), BF16)],
        scratch_shapes=[pltpu.VMEM((ts, pair), F32), pltpu.VMEM((ts, COL_BLOCK), F32)],
        compiler_params=_cparams(("parallel",)),
    )(dqc, dkc, dvh, cos_q, sin_q, cos_k, sin_k)


def _rms_bwd(dy, proj, g, blk, L, ts, name):
    S = proj.shape[0]

    def body(dy_ref, a_ref, g_ref, da_ref, dg_ref):
        i = pl.program_id(0)

        @pl.when(i == 0)
        def _():
            dg_ref[...] = jnp.zeros_like(dg_ref)

        a, dy = a_ref[...], dy_ref[...]
        r = lax.rsqrt(jnp.mean(a * a, axis=-1, keepdims=True) + RMS_EPS)
        dyh = dy * g_ref[...]
        da = r * dyh - a * (r * r * r) * jnp.mean(dyh * a, axis=-1, keepdims=True)
        da_ref[...] = da.astype(BF16)
        dg_ref[0:1, :] += jnp.sum(dy * a * r, axis=0, keepdims=True)

    return pl.pallas_call(
        body, name=name, grid=(S // ts,),
        in_specs=[pl.BlockSpec((ts, L), lambda i: (i, 0)), pl.BlockSpec((ts, L), lambda i: (i, blk)),
                  pl.BlockSpec((1, L), lambda i: (0, 0))],
        out_specs=[pl.BlockSpec((ts, L), lambda i: (i, 0)), pl.BlockSpec((8, L), lambda i: (0, 0))],
        out_shape=[jax.ShapeDtypeStruct((S, L), BF16), jax.ShapeDtypeStruct((8, L), F32)],
        compiler_params=_cparams(("arbitrary",)),
    )(dy, proj, g)


def _grad_x(du, dxa, x, mod, ts):
    S, D = x.shape

    def body(du_ref, dxa_ref, x_ref, mod_ref, dx_ref, vec_ref):
        i = pl.program_id(0)

        @pl.when(i == 0)
        def _():
            vec_ref[...] = jnp.zeros_like(vec_ref)

        du = du_ref[...]
        dx_ref[...] = dxa_ref[...] + du * (1.0 + mod_ref[1:2, :])
        vec_ref[0:1, :] += jnp.sum(du, axis=0, keepdims=True)
        vec_ref[1:2, :] += jnp.sum(du * x_ref[...], axis=0, keepdims=True)

    row = pl.BlockSpec((ts, D), lambda i: (i, 0))
    vec = lambda r: pl.BlockSpec((r, D), lambda i: (0, 0))
    return pl.pallas_call(
        body, name="grad_x", grid=(S // ts,),
        in_specs=[row, row, row, vec(6)],
        out_specs=[row, vec(8)],
        out_shape=[jax.ShapeDtypeStruct((S, D), F32), jax.ShapeDtypeStruct((8, D), F32)],
        compiler_params=_cparams(("arbitrary",)),
    )(du, dxa, x, mod)


def _adamw(w, g, m, v, name):
    R, C = w.shape
    tr = _tile(R, max(8, (1 << 19) // C), 8)
    c1 = 1.0 / (1.0 - ADAM_B1 ** ADAM_STEP)
    c2 = 1.0 / (1.0 - ADAM_B2 ** ADAM_STEP)

    def body(w_ref, g_ref, m_ref, v_ref, d_ref, nm_ref, nv_ref):
        g = g_ref[...]
        m = ADAM_B1 * m_ref[...] + (1.0 - ADAM_B1) * g
        v = ADAM_B2 * v_ref[...] + (1.0 - ADAM_B2) * (g * g)
        nm_ref[...] = m
        nv_ref[...] = v
        d_ref[...] = -ADAM_LR * ((m * c1) / (jnp.sqrt(v * c2) + ADAM_EPS) + ADAM_WD * w_ref[...])

    spec = pl.BlockSpec((tr, C), lambda i: (i, 0))
    out = jax.ShapeDtypeStruct((R, C), F32)
    return pl.pallas_call(
        body, name=name, grid=(R // tr,),
        in_specs=[spec] * 4, out_specs=[spec] * 3, out_shape=[out] * 3,
        compiler_params=_cparams(("parallel",)),
    )(w, g, m, v)


def _adamw_ada(w, cact_t, dmod, m, v):
    R, C = w.shape
    tr = _tile(R, max(8, (1 << 18) // C), 8)
    c1 = 1.0 / (1.0 - ADAM_B1 ** ADAM_STEP)
    c2 = 1.0 / (1.0 - ADAM_B2 ** ADAM_STEP)

    def body(w_ref, ct_ref, dm_ref, m_ref, v_ref, g_ref, d_ref, nm_ref, nv_ref):
        ct = ct_ref[...].astype(BF16).astype(F32)
        dm = dm_ref[...].astype(BF16).astype(F32)
        g = ct[:, 0:1] * dm[0:1, :]
        for b in range(1, N_DEV):
            g = g + ct[:, b:b + 1] * dm[b:b + 1, :]
        m = ADAM_B1 * m_ref[...] + (1.0 - ADAM_B1) * g
        v = ADAM_B2 * v_ref[...] + (1.0 - ADAM_B2) * (g * g)
        g_ref[...] = g
        nm_ref[...] = m
        nv_ref[...] = v
        d_ref[...] = -ADAM_LR * ((m * c1) / (jnp.sqrt(v * c2) + ADAM_EPS) + ADAM_WD * w_ref[...])

    spec = pl.BlockSpec((tr, C), lambda i: (i, 0))
    out = jax.ShapeDtypeStruct((R, C), F32)
    return pl.pallas_call(
        body, name="adamw_w_ada", grid=(R // tr,),
        in_specs=[spec, pl.BlockSpec((tr, N_DEV), lambda i: (i, 0)), pl.BlockSpec((N_DEV, C), lambda i: (0, 0)),
                  spec, spec],
        out_specs=[spec] * 4, out_shape=[out] * 4,
        compiler_params=_cparams(("parallel",)),
    )(w, cact_t, dmod, m, v)


def _adamw_reduced(w, own, got, m, v, my_chip, name):
    R, C = w.shape
    tr = _tile(R, max(PACK_ROW_ALIGN, (1 << 18) // C), PACK_ROW_ALIGN)
    c1 = 1.0 / (1.0 - ADAM_B1 ** ADAM_STEP)
    c2 = 1.0 / (1.0 - ADAM_B2 ** ADAM_STEP)

    def body(chip_ref, w_ref, own_ref, g1_ref, g2_ref, g3_ref, m_ref, v_ref, g_ref, d_ref, nm_ref, nv_ref):
        g = own_ref[0].astype(F32) + g1_ref[0].astype(F32) + g2_ref[0].astype(F32) + g3_ref[0].astype(F32)
        m = ADAM_B1 * m_ref[...] + (1.0 - ADAM_B1) * g
        v = ADAM_B2 * v_ref[...] + (1.0 - ADAM_B2) * (g * g)
        g_ref[...] = g
        nm_ref[...] = m
        nv_ref[...] = v
        d_ref[...] = -ADAM_LR * ((m * c1) / (jnp.sqrt(v * c2) + ADAM_EPS) + ADAM_WD * w_ref[...])

    spec = pl.BlockSpec((tr, C), lambda i, chip: (i, 0))
    slot = lambda k: pl.BlockSpec((1, tr, C), lambda i, chip: (chip[0] ^ k, i, 0))
    out = jax.ShapeDtypeStruct((R, C), F32)
    return pl.pallas_call(
        body, name=name,
        grid_spec=pltpu.PrefetchScalarGridSpec(
            num_scalar_prefetch=1, grid=(R // tr,),
            in_specs=[spec, slot(0), slot(1), slot(2), slot(3), spec, spec],
            out_specs=[spec] * 4),
        out_shape=[out] * 4,
        compiler_params=_cparams(("parallel",)),
    )(my_chip, w, own, got, got, got, m, v)


def _my_place():
    return lax.axis_index("x"), lax.axis_index("y"), lax.axis_index("c")


def _peer(k):
    x, y, c = _my_place()
    return (x ^ ((k >> 2) & 1), y ^ ((k >> 1) & 1), c ^ (k & 1))


def _linear(place):
    return 4 * place[0] + 2 * place[1] + place[2]


def _ada_fwd(c_row, wconv_row, w_ada, b_row):
    D, CW = w_ada.shape
    WC = wconv_row.shape[-1]

    def body(c_ref, wc_ref, w_ref, b_ref, mod_ref, cact_ref, wcall_ref, send_buf, sems):
        me = _linear(_my_place())
        c = c_ref[0]
        cact_ref[me] = c * _sigmoid(c)
        wcall_ref[me] = wc_ref[0]

        def gather_copy(buf, k, grp):
            return pltpu.make_async_remote_copy(
                src_ref=buf.at[me], dst_ref=buf.at[me], send_sem=sems.at[0, grp, k], recv_sem=sems.at[1, grp, k],
                device_id=_peer(k), device_id_type=MESH_ID)

        def gather_recv(buf, k, grp):
            src = _linear(_peer(k))
            return pltpu.make_async_remote_copy(
                src_ref=buf.at[src], dst_ref=buf.at[src], send_sem=sems.at[0, grp, k], recv_sem=sems.at[1, grp, k],
                device_id=_peer(k), device_id_type=MESH_ID)

        for k in range(1, N_DEV):
            gather_copy(cact_ref, k, 0).start()
            gather_copy(wcall_ref, k, 1).start()
        for k in range(1, N_DEV):
            gather_recv(cact_ref, k, 0).wait_recv()
            gather_recv(wcall_ref, k, 1).wait_recv()
        for k in range(1, N_DEV):
            gather_copy(cact_ref, k, 0).wait_send()
            gather_copy(wcall_ref, k, 1).wait_send()

        cact = jnp.concatenate([cact_ref[b] for b in range(N_DEV)], axis=0)
        mod_all = jnp.dot(cact.astype(BF16), w_ref[...].astype(BF16), preferred_element_type=F32) + b_ref[0]
        for b in range(N_DEV):
            send_buf[b] = mod_all[b:b + 1, :]
        mod_ref[me] = send_buf[me]

        def scatter_copy(k):
            dst = _linear(_peer(k))
            return pltpu.make_async_remote_copy(
                src_ref=send_buf.at[dst], dst_ref=mod_ref.at[me], send_sem=sems.at[0, 2, k], recv_sem=sems.at[1, 2, k],
                device_id=_peer(k), device_id_type=MESH_ID)

        def scatter_recv(k):
            src = _linear(_peer(k))
            return pltpu.make_async_remote_copy(
                src_ref=send_buf.at[src], dst_ref=mod_ref.at[src], send_sem=sems.at[0, 2, k], recv_sem=sems.at[1, 2, k],
                device_id=_peer(k), device_id_type=MESH_ID)

        for k in range(1, N_DEV):
            scatter_copy(k).start()
        for k in range(1, N_DEV):
            scatter_recv(k).wait_recv()
        for k in range(1, N_DEV):
            scatter_copy(k).wait_send()

    vmem = pl.BlockSpec(memory_space=pltpu.VMEM)
    return pl.pallas_call(
        body, name="ada_fwd",
        in_specs=[vmem] * 4, out_specs=[vmem] * 3,
        out_shape=[jax.ShapeDtypeStruct((N_DEV, 1, CW), F32), jax.ShapeDtypeStruct((N_DEV, 1, D), F32),
                   jax.ShapeDtypeStruct((N_DEV, 1, WC), F32)],
        scratch_shapes=[pltpu.VMEM((N_DEV, 1, CW), F32), pltpu.SemaphoreType.DMA((2, 3, N_DEV))],
        compiler_params=pltpu.CompilerParams(vmem_limit_bytes=VMEM_LIMIT),
    )(c_row, wconv_row, w_ada, b_row)


def _ada_bwd(payload, deps=()):
    NCH, _, CW = payload.shape

    def body(p_ref, *rest):
        sum_ref, mine_ref, all_ref, sems = rest[-4:]
        me = _linear(_my_place())
        all_ref[me] = p_ref[...]

        def copy(k, slot):
            return pltpu.make_async_remote_copy(
                src_ref=all_ref.at[slot], dst_ref=all_ref.at[slot], send_sem=sems.at[0, k], recv_sem=sems.at[1, k],
                device_id=_peer(k), device_id_type=MESH_ID)

        for k in range(1, N_DEV):
            copy(k, me).start()
        for k in range(1, N_DEV):
            copy(k, _linear(_peer(k))).wait_recv()
        for k in range(1, N_DEV):
            copy(k, me).wait_send()

        total = all_ref[0]
        for b in range(1, N_DEV):
            total = total + all_ref[b]
        sum_ref[...] = total

        for b in range(N_DEV):
            mine_ref[b] = all_ref[b, me]

    vmem = pl.BlockSpec(memory_space=pltpu.VMEM)
    return pl.pallas_call(
        body, name="ada_bwd",
        in_specs=[vmem] + [ANY_SPEC] * len(deps), out_specs=[vmem, vmem],
        out_shape=[jax.ShapeDtypeStruct((NCH, 1, CW), F32), jax.ShapeDtypeStruct((N_DEV, 1, CW), F32)],
        scratch_shapes=[pltpu.VMEM((N_DEV, NCH, 1, CW), F32), pltpu.SemaphoreType.DMA((2, N_DEV))],
        compiler_params=pltpu.CompilerParams(vmem_limit_bytes=VMEM_LIMIT),
    )(payload, *deps)


def _exchange_in_chip(parts):
    W = len(parts)

    def body(*refs):
        p_refs, got_refs, (send_sems, recv_sems) = refs[:W], refs[W:2 * W], refs[2 * W:]
        x, y, c = _my_place()
        sibling = (x, y, 1 - c)
        copies = []
        for w in range(W):
            for q in range(4):
                copies.append(pltpu.make_async_remote_copy(
                    src_ref=p_refs[w].at[2 * q + (1 - c)], dst_ref=got_refs[w].at[q],
                    send_sem=send_sems.at[4 * w + q], recv_sem=recv_sems.at[4 * w + q],
                    device_id=sibling, device_id_type=MESH_ID))
        for cp in copies:
            cp.start()
        for cp in copies:
            cp.wait_recv()
        for cp in copies:
            cp.wait_send()

    return pl.pallas_call(
        body, name="grad_exchange_in_chip",
        in_specs=[HBM_SPEC] * W, out_specs=[HBM_SPEC] * W,
        out_shape=[jax.ShapeDtypeStruct((4,) + p.shape[1:], p.dtype) for p in parts],
        scratch_shapes=[pltpu.SemaphoreType.DMA((4 * W,)), pltpu.SemaphoreType.DMA((4 * W,))],
    )(*parts)


def _pair_sum(parts, got, core):
    _, R, C = parts.shape
    tr = _tile(R, max(PACK_ROW_ALIGN, PAIR_SUM_BLOCK // C), PACK_ROW_ALIGN)

    def body(c_ref, p_ref, g_ref, o_ref):
        o_ref[...] = (p_ref[...].astype(F32) + g_ref[...].astype(F32)).astype(o_ref.dtype)

    return pl.pallas_call(
        body, name="grad_pair_sum",
        grid_spec=pltpu.PrefetchScalarGridSpec(
            num_scalar_prefetch=1, grid=(4, R // tr),
            in_specs=[pl.BlockSpec((1, tr, C), lambda q, i, c_ref: (2 * q + c_ref[0], i, 0)),
                      pl.BlockSpec((1, tr, C), lambda q, i, c_ref: (q, i, 0))],
            out_specs=pl.BlockSpec((1, tr, C), lambda q, i, c_ref: (q, i, 0))),
        out_shape=jax.ShapeDtypeStruct((4, R, C), parts.dtype),
        compiler_params=_cparams(("parallel", "parallel")),
    )(core, parts, got)


HBM_SPEC = pl.BlockSpec(memory_space=pltpu.HBM)
SEM_SPEC = pl.BlockSpec(memory_space=pltpu.SEMAPHORE)
ANY_SPEC = pl.BlockSpec(memory_space=pl.ANY)
SPLIT_EFFECT = pltpu.SideEffectType.DATAFLOW_SIDE_EFFECTING


def _landing_zone(shape, dtype):
    return pltpu.with_memory_space_constraint(lax.empty(shape, dtype), pltpu.HBM)


def _split_start(name, arrays, lands, after, copies_of, per_array):
    W = len(arrays)
    after = tuple(after) if isinstance(after, (tuple, list)) else (after,)

    def body(*refs):
        x_refs, land_refs = refs[:W], refs[W:2 * W]
        send_sems, recv_sems = refs[2 * W + len(after)], refs[2 * W + len(after) + 1]
        token = refs[-1]
        k = 0
        for w in range(W):
            for src, dst, dev in copies_of(w, x_refs[w], land_refs[w]):
                pltpu.make_async_remote_copy(src_ref=src, dst_ref=dst, send_sem=send_sems.at[k], recv_sem=recv_sems.at[k],
                                             device_id=dev, device_id_type=MESH_ID).start()
                k += 1
        token[...] = jnp.zeros_like(token)

    n_copies = per_array * W
    hbm_of = lambda xs: tuple(pltpu.HBM(a.shape, a.dtype) for a in xs)
    out = pl.pallas_call(
        body, name=name,
        out_shape=(pltpu.SemaphoreType.DMA((n_copies,)), pltpu.SemaphoreType.DMA((n_copies,)))
        + hbm_of(arrays) + hbm_of(lands) + (jax.ShapeDtypeStruct((8, LANE), F32),),
        in_specs=(HBM_SPEC,) * (2 * W) + (ANY_SPEC,) * len(after),
        out_specs=(SEM_SPEC, SEM_SPEC) + (HBM_SPEC,) * (2 * W) + (pl.BlockSpec(memory_space=pltpu.VMEM),),
        input_output_aliases={i: 2 + i for i in range(2 * W)},
        compiler_params=pltpu.CompilerParams(has_side_effects=SPLIT_EFFECT),
    )(*[pltpu.with_memory_space_constraint(a, pltpu.HBM) for a in arrays], *lands, *after)
    return out[0], out[1], list(out[2:2 + W]), list(out[2 + W:2 + 2 * W]), out[-1]


def _split_wait(name, state, after, copies_of):
    send_sems, recv_sems, arrays, lands, _ = state
    W = len(arrays)
    after = tuple(after) if isinstance(after, (tuple, list)) else (after,)

    def body(*refs):
        x_refs, land_refs = refs[:W], refs[W:2 * W]
        send_sems, recv_sems = refs[2 * W], refs[2 * W + 1]
        k = 0
        for w in range(W):
            for src, dst, dev in copies_of(w, x_refs[w], land_refs[w]):
                cp = pltpu.make_async_remote_copy(src_ref=src, dst_ref=dst, send_sem=send_sems.at[k],
                                                  recv_sem=recv_sems.at[k], device_id=dev, device_id_type=MESH_ID)
                cp.wait_send()
                cp.wait_recv()
                k += 1

    out = pl.pallas_call(
        body, name=name,
        out_shape=tuple(pltpu.HBM(a.shape, a.dtype) for a in arrays + lands),
        in_specs=(HBM_SPEC,) * (2 * W) + (SEM_SPEC, SEM_SPEC) + (ANY_SPEC,) * len(after),
        out_specs=(HBM_SPEC,) * (2 * W),
        input_output_aliases={i: i for i in range(2 * W)},
        compiler_params=pltpu.CompilerParams(has_side_effects=SPLIT_EFFECT),
    )(*arrays, *lands, send_sems, recv_sems, *after)
    return list(out[:W]), list(out[W:])


def _scatter_copies(w, p_ref, land_ref):
    x, y, c = _my_place()
    my_chip = 2 * x + y
    return [(p_ref.at[2 * (x ^ (k >> 1)) + (y ^ (k & 1))], land_ref.at[my_chip], (x ^ (k >> 1), y ^ (k & 1), c))
            for k in range(1, 4)]


def _gather_copies(w, x_ref, land_ref):
    x, y, c = _my_place()
    me = _linear((x, y, c))
    devs = [(x, y, 1 - c)] + [(x ^ (k >> 1), y ^ (k & 1), c) for k in range(1, 4)]
    return [(x_ref, land_ref.at[me], d) for d in devs]


def _gather_forward(lands, name):
    W = len(lands)

    def body(*refs):
        land_refs, out_refs, (send_sems, recv_sems) = refs[:W], refs[W:2 * W], refs[2 * W:]
        x, y, c = _my_place()
        sibling = (x, y, 1 - c)
        sends, arrivals = [], []
        for w in range(W):
            for k in range(1, 4):
                px, py = x ^ (k >> 1), y ^ (k & 1)
                landed, theirs = _linear((px, py, c)), out_refs[w].at[_linear((px, py, 1 - c))]
                sem = 3 * w + k - 1
                sends.append(pltpu.make_async_remote_copy(
                    src_ref=land_refs[w].at[landed], dst_ref=out_refs[w].at[landed],
                    send_sem=send_sems.at[sem], recv_sem=recv_sems.at[sem], device_id=sibling, device_id_type=MESH_ID))
                arrivals.append(pltpu.make_async_remote_copy(
                    src_ref=theirs, dst_ref=theirs, send_sem=send_sems.at[sem], recv_sem=recv_sems.at[sem],
                    device_id=sibling, device_id_type=MESH_ID))
        for cp in sends:
            cp.start()
        for cp in arrivals:
            cp.wait_recv()
        for cp in sends:
            cp.wait_send()

    return pl.pallas_call(
        body, name=name,
        in_specs=[HBM_SPEC] * W, out_specs=[HBM_SPEC] * W,
        out_shape=[jax.ShapeDtypeStruct(l.shape, l.dtype) for l in lands],
        input_output_aliases={i: i for i in range(W)},
        scratch_shapes=[pltpu.SemaphoreType.DMA((3 * W,)), pltpu.SemaphoreType.DMA((3 * W,))],
    )(*lands)


def _with_own_slot(gathered, shard):
    return lax.dynamic_update_index_in_dim(gathered, shard[None], _linear(_my_place()), axis=0)


def _in_chip_copies(w, p_ref, land_ref):
    x, y, c = _my_place()
    return [(p_ref.at[2 * q + (1 - c)], land_ref.at[q], (x, y, 1 - c)) for q in range(4)]


def _in_chip_start(parts, tag):
    lands = [_landing_zone((4,) + p.shape[1:], p.dtype) for p in parts]
    return _split_start("grad_in_chip_start_" + tag, parts, lands, (), _in_chip_copies, 4)


def _reduce_scatter_begin(parts, tag, in_chip_state=None, after=()):
    if in_chip_state is None:
        got = _exchange_in_chip(parts)
    else:
        parts, got = _split_wait("grad_in_chip_wait_" + tag, in_chip_state, after, _in_chip_copies)
    core = lax.axis_index("c").astype(jnp.int32).reshape(1)
    chip_parts = [_pair_sum(p, g, core) for p, g in zip(parts, got)]
    lands = [_landing_zone(p.shape, p.dtype) for p in chip_parts]
    return _split_start("grad_scatter_start_" + tag, chip_parts, lands, got[0], _scatter_copies, 3)


def _reduce_scatter_end(state, after, tag):
    return _split_wait("grad_scatter_wait_" + tag, state, after, _scatter_copies)


def kernel(x, c, positions, w_ada, b_ada, w_in, g_q_a, w_q_b, g_kv_a, w_kv_b, w_o_a, w_conv, w_o_b, w_o, ln1_g, ln1_b, w_ffn_in, w_ffn_out, ln2_g, ln2_b, loss_target, m_w_ada, m_b_ada, m_w_in, m_g_q_a, m_w_q_b, m_g_kv_a, m_w_kv_b, m_w_o_a, m_w_conv, m_w_o_b, m_w_o, m_ln1_g, m_ln1_b, m_w_ffn_in, m_w_ffn_out, m_ln2_g, m_ln2_b, v_w_ada, v_b_ada, v_w_in, v_g_q_a, v_w_q_b, v_g_kv_a, v_w_kv_b, v_w_o_a, v_w_conv, v_w_o_b, v_w_o, v_ln1_g, v_ln1_b, v_w_ffn_in, v_w_ffn_out, v_ln2_g, v_ln2_b):
    x2, tgt = x[0], loss_target[0]
    S, D = x2.shape
    Lq, Lkv = g_q_a.shape[1], g_kv_a.shape[1]
    H = w_q_b.shape[2] * N_DEV // QK_CAT
    F = w_ffn_out.shape[1] * N_DEV
    assert Lq == Lkv and (Lq + Lkv) % COL_BLOCK == 0 and D % COL_BLOCK == 0
    front = Lq + Lkv + QK_ROPE
    front_pad = _round_up(front, COL_BLOCK)
    kr_blk = (Lq + Lkv) // COL_BLOCK
    blk_b = front_pad // COL_BLOCK
    nblk = D // COL_BLOCK
    blk_c, blk_x, blk_ga, blk_gb = blk_b + nblk, blk_b + 2 * nblk, blk_b + 3 * nblk, blk_b + 4 * nblk
    ts = _tile(S, 256, 8)
    T = _tile(S, min(512, S // 2), CHUNK)
    tb = _tile(F, 2816)
    me = _linear(_my_place())

    cw = w_ada.shape[2]
    b_mine = lax.dynamic_slice(b_ada, (0, me * cw), (1, cw)).reshape(1, 1, cw)
    mod_blocks, cact_all, wconv_all = _ada_fwd(c.reshape(1, 1, D), w_conv[0].reshape(1, 1, -1), w_ada[0], b_mine)
    mod = mod_blocks.reshape(6, D)
    cact_all = cact_all.reshape(N_DEV, D)
    w_conv_full = wconv_all.reshape(N_DEV, CONV_K, -1).transpose(1, 0, 2).reshape(CONV_K, D)

    landing = lambda shards: [_landing_zone((N_DEV,) + s.shape, BF16) for s in shards]
    gathered = lambda lands, shards, tag: [_with_own_slot(g, s) for g, s in
                                           zip(_gather_forward(lands, tag + "_gather_forward"), shards)]
    half = D // 2
    w_in_b = w_in[0].astype(BF16)
    first, second = [w_in_b[:half]], [w_in_b[half:], w_q_b[0].astype(BF16), w_kv_b[0].astype(BF16)]
    mid = [w[0].astype(BF16) for w in (w_o_a, w_o_b, w_o)]
    last = [w[0].astype(BF16) for w in (w_ffn_in, w_ffn_out)]
    first_state = _split_start("first_gather_start", first, landing(first), mod_blocks, _gather_copies, 4)
    second_state = _split_start("second_gather_start", second, landing(second), first_state[4], _gather_copies, 4)
    u = _modulate_in(x2, mod, ts)

    first_shards, first_lands = _split_wait("first_gather_wait", first_state, (u, second_state[4]), _gather_copies)
    (g_in_top,) = gathered(first_lands, first_shards, "first")
    w_in_top = _assemble_w_in(g_in_top, front, front_pad, D, 0)
    proj_top = _matmul(u, w_in_top, "nn", F32, "proj_top", k_rows=(0, half))
    second_shards, second_lands = _split_wait("second_gather_wait", second_state, (proj_top,), _gather_copies)
    g_in_bottom, wq_s, wkv_s = gathered(second_lands, second_shards, "second")
    mid_state = _split_start("mid_gather_start", mid, landing(mid), g_in_bottom, _gather_copies, 4)
    last_state = _split_start("last_gather_start", last, landing(last), mid_state[4], _gather_copies, 4)
    w_in_p = _assemble_w_in(g_in_bottom, front, front_pad, D, half, into=w_in_top)

    inv_freq = 1.0 / (ROPE_THETA ** (jnp.arange(0, QK_ROPE, 2, dtype=F32) / QK_ROPE))
    ang = positions[0].astype(F32)[:, None] * inv_freq
    cos2 = jnp.concatenate([jnp.cos(ang), jnp.cos(ang)], axis=-1)
    sin2 = jnp.concatenate([jnp.sin(ang), jnp.sin(ang)], axis=-1)
    one, zero = jnp.ones((S, QK_NOPE), F32), jnp.zeros((S, QK_NOPE), F32)
    cos_q, sin_q = jnp.concatenate([one, cos2, one, cos2], axis=-1), jnp.concatenate([zero, sin2, zero, sin2], axis=-1)
    cos_k, sin_k = jnp.tile(cos2, (1, COL_BLOCK // QK_ROPE)), jnp.tile(sin2, (1, COL_BLOCK // QK_ROPE))

    proj = _matmul(u, w_in_p, "nn", F32, "proj", k_rows=(half, half), init=proj_top, deps=(last_state[4],))
    qn = _rms_fwd(proj, g_q_a, 0, Lq, ts, "rms_q")
    kvn = _rms_fwd(proj, g_kv_a, 1, Lkv, ts, "rms_kv")
    q = _matmul(qn, wq_s, "nn", F32, "q_up")
    kv = _matmul(kvn, wkv_s, "nn", F32, "kv_up")
    qc, kc, vh = _qk_prep(q, kv, proj, kr_blk, cos_q, sin_q, cos_k, sin_k, H, ts)
    attn, lse = _attn_fwd(qc, kc, vh, T)
    mid_shards, mid_lands = _split_wait("mid_gather_wait", mid_state, lse, _gather_copies)
    w_oa_f, w_ob_f, w_o_f = [g.reshape(-1, D) for g in gathered(mid_lands, mid_shards, "mid")]
    ya = _matmul(attn, w_oa_f, "nn", F32, "attn_out")
    cbc = _conv_fwd(proj, w_conv_full, blk_b, blk_c, blk_x)
    yb = _matmul(cbc, w_ob_f, "nn", F32, "conv_out")
    merged = _merge_fwd(proj, ya, yb, blk_ga, blk_gb, ts)
    mix = _matmul(merged, w_o_f, "nn", F32, "mix_out")
    xhat1, rstd1, u2 = _ln1_fwd(x2, mix, mod, ln1_g, ln1_b, ts)
    last_shards, last_lands = _split_wait("last_gather_wait", last_state, u2, _gather_copies)
    w_fi_s, g_fo = gathered(last_lands, last_shards, "last")
    w_fo_f = g_fo.reshape(F, D)
    hh = _matmul(u2, w_fi_s, "nn", F32, "ffn_in")
    act = _swiglu_fwd(hh, ts, tb)
    ffn = _matmul(act, w_fo_f, "nn", F32, "ffn_out")
    loss_part, dffn, dx1a, vec2 = _ln2_loss(xhat1, ffn, tgt, mod, ln1_g, ln1_b, ln2_g, ln2_b, ts)
    loss = lax.psum(loss_part[0, 0], AXES)

    gw_fo = _matmul(act, dffn, "tn", BF16, "grad_w_ffn_out")
    dh = _swiglu_bwd(dffn, w_fo_f, hh)
    gw_fi = _matmul(u2, dh, "tn", BF16, "grad_w_ffn_in", out_shards=True)
    ffn_in_chip = _in_chip_start([gw_fi, gw_fo.reshape(N_DEV, -1, D)], "ffn")
    du2 = _matmul(dh, w_fi_s, "nt", F32, "d_u2", deps=(ffn_in_chip[4],))
    ffn_state = _reduce_scatter_begin(None, "ffn", ffn_in_chip, after=(du2,))
    dxa, dmix, vec1 = _ln1_bwd(du2, dx1a, xhat1, rstd1, mix, mod, ln1_g, ln1_b, ts)
    gw_o = _matmul(merged, dmix, "tn", BF16, "grad_w_o", deps=(ffn_state[4],))
    dmerged = _matmul(dmix, w_o_f, "nt", F32, "d_merged")
    dya, dyb, dga, dgb = _merge_bwd(dmerged, proj, ya, yb, blk_ga, blk_gb, ts)
    gw_ob = _matmul(cbc, dyb, "tn", BF16, "grad_w_o_b")
    dcbc = _matmul(dyb, w_ob_f, "nt", F32, "d_conv")
    dcb, dcc, dcx, dwconv = _conv_bwd(dcbc, proj, w_conv_full, blk_b, blk_c, blk_x)
    gw_oa = _matmul(attn, dya, "tn", BF16, "grad_w_o_a")
    mix_in_chip = _in_chip_start([g.reshape(N_DEV, -1, D) for g in (gw_oa, gw_ob, gw_o)], "mix")
    dattn = _matmul(dya, w_oa_f, "nt", F32, "d_attn", deps=(mix_in_chip[4],))
    dqc, dkc, dvh = _attn_bwd(qc, kc, vh, dattn, attn, lse, T)
    ffn_own, ffn_got = _reduce_scatter_end(ffn_state, dqc, "ffn")
    mix_state = _reduce_scatter_begin(None, "mix", mix_in_chip, after=(dqc,))
    dq, dkv, dkr = _qk_bwd(dqc, dkc, dvh, cos_q, sin_q, cos_k, sin_k, ts)
    gw_qb = _matmul(qn, dq, "tn", BF16, "grad_w_q_b", out_shards=True, deps=(mix_state[4],))
    dqn = _matmul(dq, wq_s, "nt", F32, "d_qn")
    gw_kvb = _matmul(kvn, dkv, "tn", BF16, "grad_w_kv_b", out_shards=True)
    dkvn = _matmul(dkv, wkv_s, "nt", F32, "d_kvn")
    dqa, dgq = _rms_bwd(dqn, proj, g_q_a, 0, Lq, ts, "rms_q_bwd")
    dkva, dgkv = _rms_bwd(dkvn, proj, g_kv_a, 1, Lkv, ts, "rms_kv_bwd")
    dproj = jnp.concatenate([dqa, dkva, dkr, dcb, dcc, dcx, dga, dgb], axis=1)
    gw_in_p = _matmul(u, dproj, "tn", BF16, "grad_w_in")
    mix_own, mix_got = _reduce_scatter_end(mix_state, gw_in_p, "mix")
    in_state = _reduce_scatter_begin([_split_w_in(gw_in_p, front, front_pad), gw_qb, gw_kvb], "in")
    du = _matmul(dproj, w_in_p, "nt", F32, "d_u", deps=(in_state[4],))
    grad_x, vec0 = _grad_x(du, dxa, x2, mod, ts)

    my_chip = (2 * lax.axis_index("x") + lax.axis_index("y")).astype(jnp.int32).reshape(1)
    arrived = {}
    for nm, w, m, v, own, got in (
            ("w_ffn_in", w_ffn_in, m_w_ffn_in, v_w_ffn_in, ffn_own[0], ffn_got[0]),
            ("w_ffn_out", w_ffn_out, m_w_ffn_out, v_w_ffn_out, ffn_own[1], ffn_got[1]),
            ("w_o_a", w_o_a, m_w_o_a, v_w_o_a, mix_own[0], mix_got[0]),
            ("w_o_b", w_o_b, m_w_o_b, v_w_o_b, mix_own[1], mix_got[1]),
            ("w_o", w_o, m_w_o, v_w_o, mix_own[2], mix_got[2])):
        arrived[nm] = [a[None] for a in _adamw_reduced(w[0], own, got, m[0], v[0], my_chip, "adamw_" + nm)]

    dmod = jnp.concatenate([vec0[0], vec0[1], vec1[4], vec1[0], vec1[1], vec2[2]])
    small = jnp.concatenate([dmod, dgq[0], dgkv[0], vec1[2], vec1[3], vec2[0], vec2[1], dwconv[:CONV_K].reshape(-1)])
    n_small = small.shape[0]
    nch = _round_up(n_small, cw) // cw
    payload = jnp.pad(small, (0, nch * cw - n_small)).reshape(nch, 1, cw)
    summed, dmod_mine = _ada_bwd(payload, deps=[res[1] for res in arrived.values()])
    arrived["w_ada"] = [a[None] for a in _adamw_ada(w_ada[0], cact_all.T, dmod_mine.reshape(N_DEV, cw),
                                                    m_w_ada[0], v_w_ada[0])]
    summed = summed.reshape(-1)
    offs = [0, 6 * D, 6 * D + Lq, 6 * D + Lq + Lkv]
    offs += [offs[-1] + D * k for k in range(1, 5)]
    g_b_ada = summed[offs[0]:offs[1]].reshape(1, -1)
    g_gq = summed[offs[1]:offs[2]].reshape(1, -1)
    g_gkv = summed[offs[2]:offs[3]].reshape(1, -1)
    g_ln1g, g_ln1b, g_ln2g, g_ln2b = [summed[offs[3 + k]:offs[4 + k]].reshape(1, -1) for k in range(4)]
    wc = w_conv.shape[2]
    g_wconv = lax.dynamic_slice(summed[offs[7]:offs[7] + CONV_K * D].reshape(CONV_K, D), (0, me * wc), (CONV_K, wc))

    names = ["w_ada", "b_ada", "w_in", "g_q_a", "w_q_b", "g_kv_a", "w_kv_b", "w_o_a", "w_conv", "w_o_b", "w_o",
             "ln1_g", "ln1_b", "w_ffn_in", "w_ffn_out", "ln2_g", "ln2_b"]
    weights = [w_ada, b_ada, w_in, g_q_a, w_q_b, g_kv_a, w_kv_b, w_o_a, w_conv, w_o_b, w_o, ln1_g, ln1_b,
               w_ffn_in, w_ffn_out, ln2_g, ln2_b]
    moms = [m_w_ada, m_b_ada, m_w_in, m_g_q_a, m_w_q_b, m_g_kv_a, m_w_kv_b, m_w_o_a, m_w_conv, m_w_o_b, m_w_o,
            m_ln1_g, m_ln1_b, m_w_ffn_in, m_w_ffn_out, m_ln2_g, m_ln2_b]
    vels = [v_w_ada, v_b_ada, v_w_in, v_g_q_a, v_w_q_b, v_g_kv_a, v_w_kv_b, v_w_o_a, v_w_conv, v_w_o_b, v_w_o,
            v_ln1_g, v_ln1_b, v_w_ffn_in, v_w_ffn_out, v_ln2_g, v_ln2_b]
    grad_of = {"b_ada": g_b_ada, "g_q_a": g_gq, "g_kv_a": g_gkv, "w_conv": g_wconv,
               "ln1_g": g_ln1g, "ln1_b": g_ln1b, "ln2_g": g_ln2g, "ln2_b": g_ln2b}
    state_of = dict(zip(names, zip(weights, moms, vels)))
    results = dict(arrived)

    def update(nm, reduced=None):
        w, m, v = state_of[nm]
        shp = w.shape
        w2 = w.reshape(shp[-2], shp[-1]) if w.ndim == 3 else w
        m2, v2 = m.reshape(w2.shape), v.reshape(w2.shape)
        if reduced is None:
            g2 = grad_of[nm].reshape(w2.shape)
            res = (g2,) + tuple(_adamw(w2, g2, m2, v2, "adamw_" + nm))
        else:
            res = _adamw_reduced(w2, reduced[0], reduced[1], m2, v2, my_chip, "adamw_" + nm)
        results[nm] = [a.reshape(shp) for a in res]

    for nm in grad_of:
        update(nm)
    in_own, in_got = _reduce_scatter_end(in_state, [res[1] for res in results.values()], "in")
    for nm, own, got in zip(("w_in", "w_q_b", "w_kv_b"), in_own, in_got):
        update(nm, (own, got))
    outs = [[results[nm][k] for nm in names] for k in range(4)]
    return (loss, grad_x.reshape(x.shape), *outs[0], *outs[1], *outs[2], *outs[3])
```

```python
import functools

import jax
import jax.numpy as jnp
from jax import lax
from jax.experimental import pallas as pl
from jax.experimental.pallas import tpu as pltpu

F32 = jnp.float32
BF16 = jnp.bfloat16
MESH_ID = pl.DeviceIdType.MESH
AXES = ("x", "y", "c")
N_DEV = 8

CHUNK = 64
QK_NOPE = 128
QK_ROPE = 64
V_HEAD = 128
QK_CAT = QK_NOPE + QK_ROPE
ROPE_THETA = 10000.0
ATTN_SCALE = (QK_NOPE + QK_ROPE) ** -0.5
CONV_K = 3
DEEPNORM_ALPHA = 2.0 ** 0.25
LN_EPS = 1e-5
RMS_EPS = 1e-6
NEG_INF = -1e30

ADAM_LR = 0.001
ADAM_B1 = 0.9
ADAM_B2 = 0.999
ADAM_EPS = 1e-08
ADAM_WD = 0.01
ADAM_STEP = 10

LANE = 128
COL_BLOCK = 256
PACK_ROW_ALIGN = 16
PAIR_SUM_BLOCK = 1 << 20
VMEM_LIMIT = 48 * 1024 * 1024


def _round_up(n, m):
    return (n + m - 1) // m * m


def _tile(n, pref, align=LANE):
    best = None
    t = align
    while t <= min(n, pref):
        if n % t == 0:
            best = t
        t += align
    return best if best is not None else n


def _cparams(sem=None):
    return pltpu.CompilerParams(dimension_semantics=sem, vmem_limit_bytes=VMEM_LIMIT)


def _sigmoid(x):
    return 0.5 * jnp.tanh(0.5 * x) + 0.5


def _matmul(a, b, mode, out_dtype, name, tm=1024, tn=1024, tk=2048, deps=(), out_shards=False, k_rows=None,
            init=None):
    b_shards = b.ndim == 3
    n = b.shape[2] if b_shards else (b.shape[1] // N_DEV if out_shards else None)
    if mode == "nn":
        (M, K), (K2, N) = a.shape, (b.shape[1], N_DEV * n) if b_shards else b.shape
    elif mode == "nt":
        (M, K), (N, K2) = a.shape, (b.shape[1], N_DEV * n) if b_shards else b.shape
    else:
        (K, M), (K2, N) = a.shape, b.shape
    assert K == K2, (a.shape, b.shape, mode)
    tm = _tile(M, tm)
    tn = n if (mode != "nt" and n is not None) else _tile(N, tn)
    k_row0, k_len = k_rows if k_rows is not None else (0, K)
    tk = n if (mode == "nt" and b_shards) else _tile(k_len, tk)
    nk, k0 = k_len // tk, k_row0 // tk
    if mode == "nn":
        a_spec = pl.BlockSpec((tm, tk), lambda i, j, k: (i, k0 + k))
        b_spec = (pl.BlockSpec((1, tk, n), lambda i, j, k: (j, k, 0)) if b_shards
                  else pl.BlockSpec((tk, tn), lambda i, j, k: (k0 + k, j)))
        dims = (((1,), (0,)), ((), ()))
    elif mode == "nt":
        a_spec = pl.BlockSpec((tm, tk), lambda i, j, k: (i, k))
        b_spec = (pl.BlockSpec((1, tn, n), lambda i, j, k: (k, j, 0)) if b_shards
                  else pl.BlockSpec((tn, tk), lambda i, j, k: (j, k)))
        dims = (((1,), (1,)), ((), ()))
    else:
        a_spec = pl.BlockSpec((tk, tm), lambda i, j, k: (k, i))
        b_spec = pl.BlockSpec((tk, tn), lambda i, j, k: (k, j))
        dims = (((0,), (0,)), ((), ()))
    if out_shards:
        out_spec = pl.BlockSpec((1, tm, n), lambda i, j, k: (j, i, 0))
        out_shape = jax.ShapeDtypeStruct((N_DEV, M, n), out_dtype)
    else:
        out_spec = pl.BlockSpec((tm, tn), lambda i, j, k: (i, j))
        out_shape = jax.ShapeDtypeStruct((M, N), out_dtype)

    def product(a_ref, b_ref):
        b_blk = b_ref[0] if b_shards else b_ref[...]
        return lax.dot_general(a_ref[...].astype(BF16), b_blk.astype(BF16), dims, preferred_element_type=F32)

    def write(o_ref, value):
        if out_shards:
            o_ref[0] = value.astype(o_ref.dtype)
        else:
            o_ref[...] = value.astype(o_ref.dtype)

    def body_whole_k(a_ref, b_ref, *rest):
        value = product(a_ref, b_ref)
        write(rest[-1], value if init is None else value + rest[0][...])

    def body_split_k(a_ref, b_ref, *rest):
        o_ref, acc_ref = rest[-2:]
        k = pl.program_id(2)

        @pl.when(k == 0)
        def _():
            acc_ref[...] = jnp.zeros_like(acc_ref) if init is None else rest[0][...]

        acc_ref[...] += product(a_ref, b_ref)

        @pl.when(k == nk - 1)
        def _():
            write(o_ref, acc_ref[...])

    return pl.pallas_call(
        body_whole_k if nk == 1 else body_split_k, name=name, grid=(M // tm, N // tn, nk),
        in_specs=[a_spec, b_spec] + ([] if init is None else [out_spec]) + [ANY_SPEC] * len(deps),
        out_specs=out_spec, out_shape=out_shape,
        scratch_shapes=[] if nk == 1 else [pltpu.VMEM((tm, tn), F32)],
        compiler_params=_cparams(("parallel", "parallel", "arbitrary")),
    )(a, b, *(() if init is None else (init,)), *deps)


def _assemble_w_in(shards, front, front_pad, rows, row0, into=None):
    _, K, n = shards.shape
    gap = front_pad - front
    tk = _tile(K, 256, PACK_ROW_ALIGN)
    blk0 = row0 // tk

    def body(g_ref, *rest):
        o_ref = rest[-1]
        if gap:
            o_ref[:, front:front_pad] = jnp.zeros((tk, gap), o_ref.dtype)
        for j in range(N_DEV):
            lo, hi = j * n, (j + 1) * n
            if lo < front < hi:
                o_ref[:, lo:front] = g_ref[j, :, 0:front - lo]
                o_ref[:, front_pad:hi + gap] = g_ref[j, :, front - lo:n]
            else:
                off = 0 if hi <= front else gap
                o_ref[:, lo + off:hi + off] = g_ref[j]

    return pl.pallas_call(
        body, name="assemble_w_in", grid=(K // tk,),
        in_specs=[pl.BlockSpec((N_DEV, tk, n), lambda i: (0, i, 0))] + ([] if into is None else [ANY_SPEC]),
        out_specs=pl.BlockSpec((tk, N_DEV * n + gap), lambda i: (blk0 + i, 0)),
        out_shape=jax.ShapeDtypeStruct((rows, N_DEV * n + gap), shards.dtype),
        input_output_aliases={} if into is None else {1: 0},
        compiler_params=_cparams(("parallel",)),
    )(*([shards] if into is None else [shards, into]))


def _split_w_in(w, front, front_pad):
    K, NP = w.shape
    gap = front_pad - front
    n = (NP - gap) // N_DEV
    tk = _tile(K, 256, PACK_ROW_ALIGN)

    def body(w_ref, o_ref):
        for j in range(N_DEV):
            lo, hi = j * n, (j + 1) * n
            if lo < front < hi:
                o_ref[j, :, 0:front - lo] = w_ref[:, lo:front]
                o_ref[j, :, front - lo:n] = w_ref[:, front_pad:hi + gap]
            else:
                off = 0 if hi <= front else gap
                o_ref[j] = w_ref[:, lo + off:hi + off]

    return pl.pallas_call(
        body, name="split_grad_w_in", grid=(K // tk,),
        in_specs=[pl.BlockSpec((tk, NP), lambda i: (i, 0))],
        out_specs=pl.BlockSpec((N_DEV, tk, n), lambda i: (0, i, 0)),
        out_shape=jax.ShapeDtypeStruct((N_DEV, K, n), w.dtype),
        compiler_params=_cparams(("parallel",)),
    )(w)


def _modulate_in(x, mod, ts):
    S, D = x.shape

    def body(x_ref, mod_ref, u_ref):
        u_ref[...] = (x_ref[...] * (1.0 + mod_ref[1:2, :]) + mod_ref[0:1, :]).astype(BF16)

    return pl.pallas_call(
        body, name="modulate_in", grid=(S // ts,),
        in_specs=[pl.BlockSpec((ts, D), lambda i: (i, 0)), pl.BlockSpec((6, D), lambda i: (0, 0))],
        out_specs=pl.BlockSpec((ts, D), lambda i: (i, 0)),
        out_shape=jax.ShapeDtypeStruct((S, D), BF16),
        compiler_params=_cparams(("parallel",)),
    )(x, mod)


def _rms_fwd(proj, g, blk, L, ts, name):
    S = proj.shape[0]

    def body(a_ref, g_ref, y_ref):
        a = a_ref[...]
        r = lax.rsqrt(jnp.mean(a * a, axis=-1, keepdims=True) + RMS_EPS)
        y_ref[...] = (a * r * g_ref[...]).astype(BF16)

    return pl.pallas_call(
        body, name=name, grid=(S // ts,),
        in_specs=[pl.BlockSpec((ts, L), lambda i: (i, blk)), pl.BlockSpec((1, L), lambda i: (0, 0))],
        out_specs=pl.BlockSpec((ts, L), lambda i: (i, 0)),
        out_shape=jax.ShapeDtypeStruct((S, L), BF16),
        compiler_params=_cparams(("parallel",)),
    )(proj, g)


def _rope_partner(x, period, start):
    w = x.shape[-1]
    lane = lax.broadcasted_iota(jnp.int32, x.shape, x.ndim - 1) % period
    first = (lane >= start) & (lane < start + QK_ROPE // 2)
    from_right = pltpu.roll(x, w - QK_ROPE // 2, axis=x.ndim - 1)
    from_left = pltpu.roll(x, QK_ROPE // 2, axis=x.ndim - 1)
    return jnp.where(first, -from_right, from_left)


def _qk_prep(q, kv, proj, kr_blk, cos_q, sin_q, cos_k, sin_k, H, ts):
    S = q.shape[0]
    pair = 2 * QK_CAT
    kv_w = QK_NOPE + V_HEAD

    def body(q_ref, kv_ref, kr_ref, cq_ref, sq_ref, ck_ref, sk_ref, qc_ref, kc_ref, vh_ref):
        kr = kr_ref[...]
        kr = kr * ck_ref[...] + _rope_partner(kr, QK_ROPE, 0) * sk_ref[...]
        kr = kr[:, :QK_ROPE].astype(BF16)
        for p in range(H // 2):
            x = q_ref[:, p * pair:(p + 1) * pair]
            x = x * cq_ref[...] + _rope_partner(x, QK_CAT, QK_NOPE) * sq_ref[...]
            qc_ref[2 * p] = x[:, :QK_CAT].astype(BF16)
            qc_ref[2 * p + 1] = x[:, QK_CAT:].astype(BF16)
        for h in range(H):
            kc_ref[h, :, 0:QK_NOPE] = kv_ref[:, h * kv_w:h * kv_w + QK_NOPE].astype(BF16)
            kc_ref[h, :, QK_NOPE:QK_CAT] = kr
            vh_ref[h, :, :] = kv_ref[:, h * kv_w + QK_NOPE:(h + 1) * kv_w].astype(BF16)

    row = lambda w: pl.BlockSpec((ts, w), lambda i: (i, 0))
    return pl.pallas_call(
        body, name="qk_prep", grid=(S // ts,),
        in_specs=[row(H * QK_CAT), row(H * kv_w),
                  pl.BlockSpec((ts, COL_BLOCK), lambda i: (i, kr_blk)),
                  row(pair), row(pair), row(COL_BLOCK), row(COL_BLOCK)],
        out_specs=[pl.BlockSpec((H, ts, QK_CAT), lambda i: (0, i, 0)),
                   pl.BlockSpec((H, ts, QK_CAT), lambda i: (0, i, 0)),
                   pl.BlockSpec((H, ts, V_HEAD), lambda i: (0, i, 0))],
        out_shape=[jax.ShapeDtypeStruct((H, S, QK_CAT), BF16), jax.ShapeDtypeStruct((H, S, QK_CAT), BF16),
                   jax.ShapeDtypeStruct((H, S, V_HEAD), BF16)],
        compiler_params=_cparams(("parallel",)),
    )(q, kv, proj, cos_q, sin_q, cos_k, sin_k)


NT_DIMS = (((1,), (1,)), ((), ()))
TN_DIMS = (((0,), (0,)), ((), ()))


def _diag_mask(T):
    rows = lax.broadcasted_iota(jnp.int32, (T, T), 0) // CHUNK
    cols = lax.broadcasted_iota(jnp.int32, (T, T), 1) // CHUNK
    return cols <= rows


def _attn_fwd(qc, kc, vh, T):
    H, S, _ = qc.shape
    n = S // T

    def body(q_ref, k_ref, v_ref, o_ref, lse_ref):
        q = q_ref[0]

        def block(i):
            L = (i + 1) * T
            s_old = lax.dot_general(q, k_ref[0, 0:i * T, :], NT_DIMS, preferred_element_type=F32) if i else None
            s_diag = lax.dot_general(q, k_ref[0, i * T:L, :], NT_DIMS, preferred_element_type=F32)
            s_diag = jnp.where(_diag_mask(T), s_diag, NEG_INF)
            m = jnp.max(s_diag, axis=-1, keepdims=True)
            if i:
                m = jnp.maximum(m, jnp.max(s_old, axis=-1, keepdims=True))
            p_diag = jnp.exp((s_diag - m) * ATTN_SCALE)
            l = jnp.sum(p_diag, axis=-1, keepdims=True)
            acc = jnp.dot(p_diag.astype(BF16), v_ref[0, i * T:L, :], preferred_element_type=F32)
            if i:
                p_old = jnp.exp((s_old - m) * ATTN_SCALE)
                l = l + jnp.sum(p_old, axis=-1, keepdims=True)
                acc = acc + jnp.dot(p_old.astype(BF16), v_ref[0, 0:i * T, :], preferred_element_type=F32)
            o_ref[...] = acc / l
            lse_ref[0] = m * ATTN_SCALE + jnp.log(l)

        for i in range(n):
            pl.when(pl.program_id(1) == i)(functools.partial(block, i))

    return pl.pallas_call(
        body, name="attn_fwd", grid=(H, n),
        in_specs=[pl.BlockSpec((1, T, QK_CAT), lambda h, i: (h, i, 0)),
                  pl.BlockSpec((1, S, QK_CAT), lambda h, i: (h, 0, 0)),
                  pl.BlockSpec((1, S, V_HEAD), lambda h, i: (h, 0, 0))],
        out_specs=[pl.BlockSpec((T, V_HEAD), lambda h, i: (i, h)),
                   pl.BlockSpec((1, T, 1), lambda h, i: (h, i, 0))],
        out_shape=[jax.ShapeDtypeStruct((S, H * V_HEAD), F32), jax.ShapeDtypeStruct((H, S, 1), F32)],
        compiler_params=_cparams(("parallel", "arbitrary")),
    )(qc, kc, vh)


def _shift_rows(z, k):
    if k == 0:
        return z
    n = z.shape[0]
    row = lax.broadcasted_iota(jnp.int32, z.shape, 0)
    if k > 0:
        return jnp.where(row >= k, pltpu.roll(z, k, axis=0), 0.0)
    return jnp.where(row < n + k, pltpu.roll(z, n + k, axis=0), 0.0)


def _conv_fwd(proj, w_conv, blk_b, blk_c, blk_x):
    S = proj.shape[0]
    D = w_conv.shape[1]
    nb = D // COL_BLOCK

    def body(cb_ref, cc_ref, cx_ref, w_ref, o_ref):
        z = cc_ref[...] * cx_ref[...]
        conv = w_ref[2:3, :] * z + w_ref[1:2, :] * _shift_rows(z, 1) + w_ref[0:1, :] * _shift_rows(z, 2)
        o_ref[...] = (cb_ref[...] * conv).astype(BF16)

    col = lambda off: pl.BlockSpec((S, COL_BLOCK), lambda j: (0, off + j))
    return pl.pallas_call(
        body, name="conv_fwd", grid=(nb,),
        in_specs=[col(blk_b), col(blk_c), col(blk_x), pl.BlockSpec((CONV_K, COL_BLOCK), lambda j: (0, j))],
        out_specs=pl.BlockSpec((S, COL_BLOCK), lambda j: (0, j)),
        out_shape=jax.ShapeDtypeStruct((S, D), BF16),
        compiler_params=_cparams(("parallel",)),
    )(proj, proj, proj, w_conv)


def _merge_fwd(proj, ya, yb, blk_ga, blk_gb, ts):
    S, D = ya.shape
    nb = D // COL_BLOCK

    def body(ga_ref, gb_ref, ya_ref, yb_ref, o_ref):
        o_ref[...] = (_sigmoid(ga_ref[...]) * ya_ref[...] + _sigmoid(gb_ref[...]) * yb_ref[...]).astype(BF16)

    row = pl.BlockSpec((ts, D), lambda i: (i, 0))
    seg = lambda blk: pl.BlockSpec((pl.Element(ts), pl.Element(D)), lambda i: (i * ts, blk * COL_BLOCK))
    return pl.pallas_call(
        body, name="merge_fwd", grid=(S // ts,),
        in_specs=[seg(blk_ga), seg(blk_gb), row, row],
        out_specs=row,
        out_shape=jax.ShapeDtypeStruct((S, D), BF16),
        compiler_params=_cparams(("parallel",)),
    )(proj, proj, ya, yb)


def _ln1_fwd(x, mix, mod, g, b, ts):
    S, D = x.shape

    def body(x_ref, mix_ref, mod_ref, g_ref, b_ref, xhat_ref, rstd_ref, u2_ref):
        r = DEEPNORM_ALPHA * x_ref[...] + mod_ref[2:3, :] * mix_ref[...]
        mu = jnp.mean(r, axis=-1, keepdims=True)
        d = r - mu
        rstd = lax.rsqrt(jnp.mean(d * d, axis=-1, keepdims=True) + LN_EPS)
        xhat = d * rstd
        xhat_ref[...] = xhat
        rstd_ref[...] = rstd
        x1 = xhat * g_ref[...] + b_ref[...]
        u2_ref[...] = (x1 * (1.0 + mod_ref[4:5, :]) + mod_ref[3:4, :]).astype(BF16)

    row = pl.BlockSpec((ts, D), lambda i: (i, 0))
    vec = lambda r: pl.BlockSpec((r, D), lambda i: (0, 0))
    return pl.pallas_call(
        body, name="ln1_fwd", grid=(S // ts,),
        in_specs=[row, row, vec(6), vec(1), vec(1)],
        out_specs=[row, pl.BlockSpec((ts, 1), lambda i: (i, 0)), row],
        out_shape=[jax.ShapeDtypeStruct((S, D), F32), jax.ShapeDtypeStruct((S, 1), F32),
                   jax.ShapeDtypeStruct((S, D), BF16)],
        compiler_params=_cparams(("parallel",)),
    )(x, mix, mod, g, b)


def _swiglu_fwd(h, ts, tb):
    S, F2 = h.shape
    F = F2 // 2
    nb = F // tb

    def body(hg_ref, hu_ref, a_ref):
        hg = hg_ref[...]
        a_ref[...] = (hg * _sigmoid(hg) * hu_ref[...]).astype(BF16)

    return pl.pallas_call(
        body, name="swiglu_fwd", grid=(S // ts, nb),
        in_specs=[pl.BlockSpec((ts, tb), lambda i, j: (i, j)), pl.BlockSpec((ts, tb), lambda i, j: (i, j + nb))],
        out_specs=pl.BlockSpec((ts, tb), lambda i, j: (i, j)),
        out_shape=jax.ShapeDtypeStruct((S, F), BF16),
        compiler_params=_cparams(("parallel", "parallel")),
    )(h, h)


def _ln2_loss(xhat1, ffn, tgt, mod, g1, b1, g2, b2, ts):
    S, D = xhat1.shape

    def body(xh_ref, ffn_ref, t_ref, mod_ref, g1_ref, b1_ref, g2_ref, b2_ref, loss_ref, dffn_ref, dx1_ref, vec_ref):
        i = pl.program_id(0)

        @pl.when(i == 0)
        def _():
            loss_ref[...] = jnp.zeros_like(loss_ref)
            vec_ref[...] = jnp.zeros_like(vec_ref)

        x1 = xh_ref[...] * g1_ref[...] + b1_ref[...]
        ffn = ffn_ref[...]
        r = DEEPNORM_ALPHA * x1 + mod_ref[5:6, :] * ffn
        mu = jnp.mean(r, axis=-1, keepdims=True)
        d = r - mu
        rstd = lax.rsqrt(jnp.mean(d * d, axis=-1, keepdims=True) + LN_EPS)
        xhat = d * rstd
        e = xhat * g2_ref[...] + b2_ref[...] - t_ref[...]
        loss_ref[...] += 0.5 * jnp.sum(jnp.mean(e * e, axis=-1, keepdims=True))
        dy = e * (1.0 / D)
        dxhat = dy * g2_ref[...]
        dr = rstd * (dxhat - jnp.mean(dxhat, axis=-1, keepdims=True)
                     - xhat * jnp.mean(dxhat * xhat, axis=-1, keepdims=True))
        dffn_ref[...] = (dr * mod_ref[5:6, :]).astype(BF16)
        dx1_ref[...] = DEEPNORM_ALPHA * dr
        vec_ref[0:1, :] += jnp.sum(dy * xhat, axis=0, keepdims=True)
        vec_ref[1:2, :] += jnp.sum(dy, axis=0, keepdims=True)
        vec_ref[2:3, :] += jnp.sum(dr * ffn, axis=0, keepdims=True)

    row = pl.BlockSpec((ts, D), lambda i: (i, 0))
    vec = lambda r: pl.BlockSpec((r, D), lambda i: (0, 0))
    return pl.pallas_call(
        body, name="ln2_loss", grid=(S // ts,),
        in_specs=[row, row, row, vec(6), vec(1), vec(1), vec(1), vec(1)],
        out_specs=[pl.BlockSpec((1, LANE), lambda i: (0, 0)), row, row, vec(8)],
        out_shape=[jax.ShapeDtypeStruct((1, LANE), F32), jax.ShapeDtypeStruct((S, D), BF16),
                   jax.ShapeDtypeStruct((S, D), F32), jax.ShapeDtypeStruct((8, D), F32)],
        compiler_params=_cparams(("arbitrary",)),
    )(xhat1, ffn, tgt, mod, g1, b1, g2, b2)


def _swiglu_bwd(da, h, ts, tb):
    S, F2 = h.shape
    nb = (F2 // 2) // tb

    def body(da_ref, hg_ref, hu_ref, dh_ref):
        hg, da = hg_ref[...], da_ref[...]
        sg = _sigmoid(hg)

        @pl.when(pl.program_id(2) == 0)
        def _():
            dh_ref[...] = (da * hu_ref[...] * (sg * (1.0 + hg * (1.0 - sg)))).astype(BF16)

        @pl.when(pl.program_id(2) == 1)
        def _():
            dh_ref[...] = (da * hg * sg).astype(BF16)

    lo = pl.BlockSpec((ts, tb), lambda i, j, k: (i, j))
    hi = pl.BlockSpec((ts, tb), lambda i, j, k: (i, j + nb))
    return pl.pallas_call(
        body, name="swiglu_bwd", grid=(S // ts, nb, 2),
        in_specs=[lo, lo, hi],
        out_specs=pl.BlockSpec((ts, tb), lambda i, j, k: (i, j + nb * k)),
        out_shape=jax.ShapeDtypeStruct((S, F2), BF16),
        compiler_params=_cparams(("parallel", "parallel", "arbitrary")),
    )(da, h, h)


def _ln1_bwd(du2, dx1a, xhat1, rstd1, mix, mod, g1, b1, ts):
    S, D = xhat1.shape

    def body(du2_ref, dx1a_ref, xh_ref, rstd_ref, mix_ref, mod_ref, g_ref, b_ref, dxa_ref, dmix_ref, vec_ref):
        i = pl.program_id(0)

        @pl.when(i == 0)
        def _():
            vec_ref[...] = jnp.zeros_like(vec_ref)

        xhat, du2, mix = xh_ref[...], du2_ref[...], mix_ref[...]
        x1 = xhat * g_ref[...] + b_ref[...]
        dx1 = dx1a_ref[...] + du2 * (1.0 + mod_ref[4:5, :])
        dxhat = dx1 * g_ref[...]
        dr = rstd_ref[...] * (dxhat - jnp.mean(dxhat, axis=-1, keepdims=True)
                              - xhat * jnp.mean(dxhat * xhat, axis=-1, keepdims=True))
        dxa_ref[...] = DEEPNORM_ALPHA * dr
        dmix_ref[...] = (dr * mod_ref[2:3, :]).astype(BF16)
        vec_ref[0:1, :] += jnp.sum(du2, axis=0, keepdims=True)
        vec_ref[1:2, :] += jnp.sum(du2 * x1, axis=0, keepdims=True)
        vec_ref[2:3, :] += jnp.sum(dx1 * xhat, axis=0, keepdims=True)
        vec_ref[3:4, :] += jnp.sum(dx1, axis=0, keepdims=True)
        vec_ref[4:5, :] += jnp.sum(dr * mix, axis=0, keepdims=True)

    row = pl.BlockSpec((ts, D), lambda i: (i, 0))
    vec = lambda r: pl.BlockSpec((r, D), lambda i: (0, 0))
    return pl.pallas_call(
        body, name="ln1_bwd", grid=(S // ts,),
        in_specs=[row, row, row, pl.BlockSpec((ts, 1), lambda i: (i, 0)), row, vec(6), vec(1), vec(1)],
        out_specs=[row, row, vec(8)],
        out_shape=[jax.ShapeDtypeStruct((S, D), F32), jax.ShapeDtypeStruct((S, D), BF16),
                   jax.ShapeDtypeStruct((8, D), F32)],
        compiler_params=_cparams(("arbitrary",)),
    )(du2, dx1a, xhat1, rstd1, mix, mod, g1, b1)


def _merge_bwd(dmerged, proj, ya, yb, blk_ga, blk_gb, ts):
    S, D = ya.shape
    nb = D // COL_BLOCK

    def body(dm_ref, ga_ref, gb_ref, ya_ref, yb_ref, dya_ref, dyb_ref, dga_ref, dgb_ref):
        dm = dm_ref[...]
        sa, sb = _sigmoid(ga_ref[...]), _sigmoid(gb_ref[...])
        dya_ref[...] = (dm * sa).astype(BF16)
        dyb_ref[...] = (dm * sb).astype(BF16)
        dga_ref[...] = (dm * ya_ref[...] * sa * (1.0 - sa)).astype(BF16)
        dgb_ref[...] = (dm * yb_ref[...] * sb * (1.0 - sb)).astype(BF16)

    row = pl.BlockSpec((ts, D), lambda i: (i, 0))
    seg = lambda blk: pl.BlockSpec((pl.Element(ts), pl.Element(D)), lambda i: (i * ts, blk * COL_BLOCK))
    out = jax.ShapeDtypeStruct((S, D), BF16)
    return pl.pallas_call(
        body, name="merge_bwd", grid=(S // ts,),
        in_specs=[row, seg(blk_ga), seg(blk_gb), row, row],
        out_specs=[row] * 4,
        out_shape=[out] * 4,
        compiler_params=_cparams(("parallel",)),
    )(dmerged, proj, proj, ya, yb)


def _conv_bwd(dcbc, proj, w_conv, blk_b, blk_c, blk_x):
    S = proj.shape[0]
    D = w_conv.shape[1]
    nb = D // COL_BLOCK

    def body(d_ref, cb_ref, cc_ref, cx_ref, w_ref, dcb_ref, dcc_ref, dcx_ref, dw_ref):
        d, cc, cx = d_ref[...], cc_ref[...], cx_ref[...]
        z = cc * cx
        z1, z2 = _shift_rows(z, 1), _shift_rows(z, 2)
        conv = w_ref[2:3, :] * z + w_ref[1:2, :] * z1 + w_ref[0:1, :] * z2
        dcb_ref[...] = (d * conv).astype(BF16)
        dconv = d * cb_ref[...]
        dz = w_ref[2:3, :] * dconv + w_ref[1:2, :] * _shift_rows(dconv, -1) + w_ref[0:1, :] * _shift_rows(dconv, -2)
        dcc_ref[...] = (dz * cx).astype(BF16)
        dcx_ref[...] = (dz * cc).astype(BF16)
        dw_ref[...] = jnp.zeros_like(dw_ref)
        dw_ref[0:1, :] = jnp.sum(dconv * z2, axis=0, keepdims=True)
        dw_ref[1:2, :] = jnp.sum(dconv * z1, axis=0, keepdims=True)
        dw_ref[2:3, :] = jnp.sum(dconv * z, axis=0, keepdims=True)

    col = lambda off: pl.BlockSpec((S, COL_BLOCK), lambda j: (0, off + j))
    out = jax.ShapeDtypeStruct((S, D), BF16)
    return pl.pallas_call(
        body, name="conv_bwd", grid=(nb,),
        in_specs=[col(0), col(blk_b), col(blk_c), col(blk_x), pl.BlockSpec((CONV_K, COL_BLOCK), lambda j: (0, j))],
        out_specs=[col(0), col(0), col(0), pl.BlockSpec((8, COL_BLOCK), lambda j: (0, j))],
        out_shape=[out, out, out, jax.ShapeDtypeStruct((8, D), F32)],
        compiler_params=_cparams(("parallel",)),
    )(dcbc, proj, proj, proj, w_conv)


def _attn_bwd(qc, kc, vh, do, o, lse, T):
    H, S, _ = qc.shape
    n = S // T

    def body(q_ref, k_ref, v_ref, do_ref, o_ref, lse_ref, dq_ref, dk_ref, dv_ref, d_ref, dk_acc, dv_acc):
        j = pl.program_id(1)

        @pl.when(j == 0)
        def _():
            dq_ref[...] = jnp.zeros_like(dq_ref)
            d_ref[...] = jnp.sum(do_ref[...] * o_ref[...], axis=-1, keepdims=True)

        dk_acc[...] = jnp.zeros_like(dk_acc)
        dv_acc[...] = jnp.zeros_like(dv_acc)
        k, v = k_ref[0], v_ref[0]

        def step(i, masked):
            rows = pl.ds(pl.multiple_of(i * T, T), T)
            q = q_ref[0, rows, :]
            do = do_ref[rows, :].astype(BF16)
            s = lax.dot_general(q, k, NT_DIMS, preferred_element_type=F32) * ATTN_SCALE
            if masked:
                s = jnp.where(_diag_mask(T), s, NEG_INF)
            p = jnp.exp(s - lse_ref[0, rows, :])
            dv_acc[...] += lax.dot_general(p.astype(BF16), do, TN_DIMS, preferred_element_type=F32)
            dp = lax.dot_general(do, v, NT_DIMS, preferred_element_type=F32)
            ds = (p * (dp - d_ref[rows, :]) * ATTN_SCALE).astype(BF16)
            dk_acc[...] += lax.dot_general(ds, q, TN_DIMS, preferred_element_type=F32)
            dq_ref[0, rows, :] += jnp.dot(ds, k, preferred_element_type=F32)

        def above(i, carry):
            step(i, False)
            return carry

        step(j, True)
        lax.fori_loop(j + 1, n, above, 0)
        dk_ref[0] = dk_acc[...]
        dv_ref[0] = dv_acc[...]

    head = lambda w: pl.BlockSpec((1, S, w), lambda h, j: (h, 0, 0))
    blk = lambda w: pl.BlockSpec((1, T, w), lambda h, j: (h, j, 0))
    ospec = pl.BlockSpec((S, V_HEAD), lambda h, j: (0, h))
    return pl.pallas_call(
        body, name="attn_bwd", grid=(H, n),
        in_specs=[head(QK_CAT), blk(QK_CAT), blk(V_HEAD), ospec, ospec, head(1)],
        out_specs=[head(QK_CAT), blk(QK_CAT), blk(V_HEAD)],
        out_shape=[jax.ShapeDtypeStruct((H, S, QK_CAT), F32), jax.ShapeDtypeStruct((H, S, QK_CAT), F32),
                   jax.ShapeDtypeStruct((H, S, V_HEAD), F32)],
        scratch_shapes=[pltpu.VMEM((S, 1), F32), pltpu.VMEM((T, QK_CAT), F32), pltpu.VMEM((T, V_HEAD), F32)],
        compiler_params=_cparams(("parallel", "arbitrary")),
    )(qc, kc, vh, do, o, lse)


def _qk_bwd(dqc, dkc, dvh, cos_q, sin_q, cos_k, sin_k, ts):
    H, S, _ = dqc.shape
    pair = 2 * QK_CAT
    kv_w = QK_NOPE + V_HEAD

    def body(dqc_ref, dkc_ref, dvh_ref, cq_ref, sq_ref, ck_ref, sk_ref, dq_ref, dkv_ref, dkr_ref, q_buf, kr_buf):
        for p in range(H // 2):
            q_buf[:, :QK_CAT] = dqc_ref[2 * p]
            q_buf[:, QK_CAT:] = dqc_ref[2 * p + 1]
            g = q_buf[...]
            dq_ref[:, p * pair:(p + 1) * pair] = (
                g * cq_ref[...] - _rope_partner(g, QK_CAT, QK_NOPE) * sq_ref[...]).astype(BF16)
        kr_sum = jnp.zeros((ts, QK_ROPE), F32)
        for h in range(H):
            dkv_ref[:, h * kv_w:h * kv_w + QK_NOPE] = dkc_ref[h, :, 0:QK_NOPE].astype(BF16)
            dkv_ref[:, h * kv_w + QK_NOPE:(h + 1) * kv_w] = dvh_ref[h].astype(BF16)
            kr_sum = kr_sum + dkc_ref[h, :, QK_NOPE:QK_CAT]
        kr_buf[...] = jnp.zeros_like(kr_buf)
        kr_buf[:, 0:QK_ROPE] = kr_sum
        kr = kr_buf[...]
        dkr_ref[...] = (kr * ck_ref[...] - _rope_partner(kr, QK_ROPE, 0) * sk_ref[...]).astype(BF16)

    row = lambda w: pl.BlockSpec((ts, w), lambda i: (i, 0))
    head = lambda w: pl.BlockSpec((H, ts, w), lambda i: (0, i, 0))
    return pl.pallas_call(
        body, name="qk_bwd", grid=(S // ts,),
        in_specs=[head(QK_CAT), head(QK_CAT), head(V_HEAD), row(pair), row(pair), row(COL_BLOCK), row(COL_BLOCK)],
        out_specs=[row(H * QK_CAT), row(H * kv_w), row(COL_BLOCK)],
        out_shape=[jax.ShapeDtypeStruct((S, H * QK_CAT), BF16), jax.ShapeDtypeStruct((S, H * kv_w), BF16),
                   jax.ShapeDtypeStruct((S, COL_BLOCK), BF16)],
        scratch_shapes=[pltpu.VMEM((ts, pair), F32), pltpu.VMEM((ts, COL_BLOCK), F32)],
        compiler_params=_cparams(("parallel",)),
    )(dqc, dkc, dvh, cos_q, sin_q, cos_k, sin_k)


def _rms_bwd(dy, proj, g, blk, L, ts, name):
    S = proj.shape[0]

    def body(dy_ref, a_ref, g_ref, da_ref, dg_ref):
        i = pl.program_id(0)

        @pl.when(i == 0)
        def _():
            dg_ref[...] = jnp.zeros_like(dg_ref)

        a, dy = a_ref[...], dy_ref[...]
        r = lax.rsqrt(jnp.mean(a * a, axis=-1, keepdims=True) + RMS_EPS)
        dyh = dy * g_ref[...]
        da = r * dyh - a * (r * r * r) * jnp.mean(dyh * a, axis=-1, keepdims=True)
        da_ref[...] = da.astype(BF16)
        dg_ref[0:1, :] += jnp.sum(dy * a * r, axis=0, keepdims=True)

    return pl.pallas_call(
        body, name=name, grid=(S // ts,),
        in_specs=[pl.BlockSpec((ts, L), lambda i: (i, 0)), pl.BlockSpec((ts, L), lambda i: (i, blk)),
                  pl.BlockSpec((1, L), lambda i: (0, 0))],
        out_specs=[pl.BlockSpec((ts, L), lambda i: (i, 0)), pl.BlockSpec((8, L), lambda i: (0, 0))],
        out_shape=[jax.ShapeDtypeStruct((S, L), BF16), jax.ShapeDtypeStruct((8, L), F32)],
        compiler_params=_cparams(("arbitrary",)),
    )(dy, proj, g)


def _grad_x(du, dxa, x, mod, ts):
    S, D = x.shape

    def body(du_ref, dxa_ref, x_ref, mod_ref, dx_ref, vec_ref):
        i = pl.program_id(0)

        @pl.when(i == 0)
        def _():
            vec_ref[...] = jnp.zeros_like(vec_ref)

        du = du_ref[...]
        dx_ref[...] = dxa_ref[...] + du * (1.0 + mod_ref[1:2, :])
        vec_ref[0:1, :] += jnp.sum(du, axis=0, keepdims=True)
        vec_ref[1:2, :] += jnp.sum(du * x_ref[...], axis=0, keepdims=True)

    row = pl.BlockSpec((ts, D), lambda i: (i, 0))
    vec = lambda r: pl.BlockSpec((r, D), lambda i: (0, 0))
    return pl.pallas_call(
        body, name="grad_x", grid=(S // ts,),
        in_specs=[row, row, row, vec(6)],
        out_specs=[row, vec(8)],
        out_shape=[jax.ShapeDtypeStruct((S, D), F32), jax.ShapeDtypeStruct((8, D), F32)],
        compiler_params=_cparams(("arbitrary",)),
    )(du, dxa, x, mod)


def _adamw(w, g, m, v, name):
    R, C = w.shape
    tr = _tile(R, max(8, (1 << 19) // C), 8)
    c1 = 1.0 / (1.0 - ADAM_B1 ** ADAM_STEP)
    c2 = 1.0 / (1.0 - ADAM_B2 ** ADAM_STEP)

    def body(w_ref, g_ref, m_ref, v_ref, d_ref, nm_ref, nv_ref):
        g = g_ref[...]
        m = ADAM_B1 * m_ref[...] + (1.0 - ADAM_B1) * g
        v = ADAM_B2 * v_ref[...] + (1.0 - ADAM_B2) * (g * g)
        nm_ref[...] = m
        nv_ref[...] = v
        d_ref[...] = -ADAM_LR * ((m * c1) / (jnp.sqrt(v * c2) + ADAM_EPS) + ADAM_WD * w_ref[...])

    spec = pl.BlockSpec((tr, C), lambda i: (i, 0))
    out = jax.ShapeDtypeStruct((R, C), F32)
    return pl.pallas_call(
        body, name=name, grid=(R // tr,),
        in_specs=[spec] * 4, out_specs=[spec] * 3, out_shape=[out] * 3,
        compiler_params=_cparams(("parallel",)),
    )(w, g, m, v)


def _adamw_ada(w, cact_t, dmod, m, v):
    R, C = w.shape
    tr = _tile(R, max(8, (1 << 18) // C), 8)
    c1 = 1.0 / (1.0 - ADAM_B1 ** ADAM_STEP)
    c2 = 1.0 / (1.0 - ADAM_B2 ** ADAM_STEP)

    def body(w_ref, ct_ref, dm_ref, m_ref, v_ref, g_ref, d_ref, nm_ref, nv_ref):
        ct = ct_ref[...].astype(BF16).astype(F32)
        dm = dm_ref[...].astype(BF16).astype(F32)
        g = ct[:, 0:1] * dm[0:1, :]
        for b in range(1, N_DEV):
            g = g + ct[:, b:b + 1] * dm[b:b + 1, :]
        m = ADAM_B1 * m_ref[...] + (1.0 - ADAM_B1) * g
        v = ADAM_B2 * v_ref[...] + (1.0 - ADAM_B2) * (g * g)
        g_ref[...] = g
        nm_ref[...] = m
        nv_ref[...] = v
        d_ref[...] = -ADAM_LR * ((m * c1) / (jnp.sqrt(v * c2) + ADAM_EPS) + ADAM_WD * w_ref[...])

    spec = pl.BlockSpec((tr, C), lambda i: (i, 0))
    out = jax.ShapeDtypeStruct((R, C), F32)
    return pl.pallas_call(
        body, name="adamw_w_ada", grid=(R // tr,),
        in_specs=[spec, pl.BlockSpec((tr, N_DEV), lambda i: (i, 0)), pl.BlockSpec((N_DEV, C), lambda i: (0, 0)),
                  spec, spec],
        out_specs=[spec] * 4, out_shape=[out] * 4,
        compiler_params=_cparams(("parallel",)),
    )(w, cact_t, dmod, m, v)


def _adamw_reduced(w, own, got, m, v, my_chip, name, row0=0, into=None):
    R, C = w.shape
    rows = own.shape[1]
    tr = _tile(rows, max(PACK_ROW_ALIGN, (1 << 18) // C), PACK_ROW_ALIGN)
    blk0 = row0 // tr
    c1 = 1.0 / (1.0 - ADAM_B1 ** ADAM_STEP)
    c2 = 1.0 / (1.0 - ADAM_B2 ** ADAM_STEP)

    def body(chip_ref, w_ref, own_ref, g1_ref, g2_ref, g3_ref, m_ref, v_ref, *rest):
        g_ref, d_ref, nm_ref, nv_ref = rest[-4:]
        g = own_ref[0].astype(F32) + g1_ref[0].astype(F32) + g2_ref[0].astype(F32) + g3_ref[0].astype(F32)
        m = ADAM_B1 * m_ref[...] + (1.0 - ADAM_B1) * g
        v = ADAM_B2 * v_ref[...] + (1.0 - ADAM_B2) * (g * g)
        g_ref[...] = g
        nm_ref[...] = m
        nv_ref[...] = v
        d_ref[...] = -ADAM_LR * ((m * c1) / (jnp.sqrt(v * c2) + ADAM_EPS) + ADAM_WD * w_ref[...])

    spec = pl.BlockSpec((tr, C), lambda i, chip: (blk0 + i, 0))
    slot = lambda k: pl.BlockSpec((1, tr, C), lambda i, chip: (chip[0] ^ k, i, 0))
    out = jax.ShapeDtypeStruct((R, C), F32)
    into = () if into is None else tuple(into)
    return pl.pallas_call(
        body, name=name,
        grid_spec=pltpu.PrefetchScalarGridSpec(
            num_scalar_prefetch=1, grid=(rows // tr,),
            in_specs=[spec, slot(0), slot(1), slot(2), slot(3), spec, spec] + [ANY_SPEC] * len(into),
            out_specs=[spec] * 4),
        out_shape=[out] * 4,
        input_output_aliases={8 + k: k for k in range(len(into))},
        compiler_params=_cparams(("parallel",)),
    )(my_chip, w, own, got, got, got, m, v, *into)


def _my_place():
    return lax.axis_index("x"), lax.axis_index("y"), lax.axis_index("c")


def _peer(k):
    x, y, c = _my_place()
    return (x ^ ((k >> 2) & 1), y ^ ((k >> 1) & 1), c ^ (k & 1))


def _linear(place):
    return 4 * place[0] + 2 * place[1] + place[2]


def _ada_fwd(c_row, wconv_row, w_ada, b_row):
    D, CW = w_ada.shape
    WC = wconv_row.shape[-1]

    def body(c_ref, wc_ref, w_ref, b_ref, mod_ref, cact_ref, wcall_ref, send_buf, sems):
        me = _linear(_my_place())
        c = c_ref[0]
        cact_ref[me] = c * _sigmoid(c)
        wcall_ref[me] = wc_ref[0]

        def gather_copy(buf, k, grp):
            return pltpu.make_async_remote_copy(
                src_ref=buf.at[me], dst_ref=buf.at[me], send_sem=sems.at[0, grp, k], recv_sem=sems.at[1, grp, k],
                device_id=_peer(k), device_id_type=MESH_ID)

        def gather_recv(buf, k, grp):
            src = _linear(_peer(k))
            return pltpu.make_async_remote_copy(
                src_ref=buf.at[src], dst_ref=buf.at[src], send_sem=sems.at[0, grp, k], recv_sem=sems.at[1, grp, k],
                device_id=_peer(k), device_id_type=MESH_ID)

        for k in range(1, N_DEV):
            gather_copy(cact_ref, k, 0).start()
            gather_copy(wcall_ref, k, 1).start()
        for k in range(1, N_DEV):
            gather_recv(cact_ref, k, 0).wait_recv()
            gather_recv(wcall_ref, k, 1).wait_recv()
        for k in range(1, N_DEV):
            gather_copy(cact_ref, k, 0).wait_send()
            gather_copy(wcall_ref, k, 1).wait_send()

        cact = jnp.concatenate([cact_ref[b] for b in range(N_DEV)], axis=0)
        mod_all = jnp.dot(cact.astype(BF16), w_ref[...].astype(BF16), preferred_element_type=F32) + b_ref[0]
        for b in range(N_DEV):
            send_buf[b] = mod_all[b:b + 1, :]
        mod_ref[me] = send_buf[me]

        def scatter_copy(k):
            dst = _linear(_peer(k))
            return pltpu.make_async_remote_copy(
                src_ref=send_buf.at[dst], dst_ref=mod_ref.at[me], send_sem=sems.at[0, 2, k], recv_sem=sems.at[1, 2, k],
                device_id=_peer(k), device_id_type=MESH_ID)

        def scatter_recv(k):
            src = _linear(_peer(k))
            return pltpu.make_async_remote_copy(
                src_ref=send_buf.at[src], dst_ref=mod_ref.at[src], send_sem=sems.at[0, 2, k], recv_sem=sems.at[1, 2, k],
                device_id=_peer(k), device_id_type=MESH_ID)

        for k in range(1, N_DEV):
            scatter_copy(k).start()
        for k in range(1, N_DEV):
            scatter_recv(k).wait_recv()
        for k in range(1, N_DEV):
            scatter_copy(k).wait_send()

    vmem = pl.BlockSpec(memory_space=pltpu.VMEM)
    return pl.pallas_call(
        body, name="ada_fwd",
        in_specs=[vmem] * 4, out_specs=[vmem] * 3,
        out_shape=[jax.ShapeDtypeStruct((N_DEV, 1, CW), F32), jax.ShapeDtypeStruct((N_DEV, 1, D), F32),
                   jax.ShapeDtypeStruct((N_DEV, 1, WC), F32)],
        scratch_shapes=[pltpu.VMEM((N_DEV, 1, CW), F32), pltpu.SemaphoreType.DMA((2, 3, N_DEV))],
        compiler_params=pltpu.CompilerParams(vmem_limit_bytes=VMEM_LIMIT),
    )(c_row, wconv_row, w_ada, b_row)


def _ada_bwd(payload, deps=()):
    NCH, _, CW = payload.shape

    def body(p_ref, *rest):
        sum_ref, mine_ref, all_ref, sems = rest[-4:]
        me = _linear(_my_place())
        all_ref[me] = p_ref[...]

        def copy(k, slot):
            return pltpu.make_async_remote_copy(
                src_ref=all_ref.at[slot], dst_ref=all_ref.at[slot], send_sem=sems.at[0, k], recv_sem=sems.at[1, k],
                device_id=_peer(k), device_id_type=MESH_ID)

        for k in range(1, N_DEV):
            copy(k, me).start()
        for k in range(1, N_DEV):
            copy(k, _linear(_peer(k))).wait_recv()
        for k in range(1, N_DEV):
            copy(k, me).wait_send()

        total = all_ref[0]
        for b in range(1, N_DEV):
            total = total + all_ref[b]
        sum_ref[...] = total

        for b in range(N_DEV):
            mine_ref[b] = all_ref[b, me]

    vmem = pl.BlockSpec(memory_space=pltpu.VMEM)
    return pl.pallas_call(
        body, name="ada_bwd",
        in_specs=[vmem] + [ANY_SPEC] * len(deps), out_specs=[vmem, vmem],
        out_shape=[jax.ShapeDtypeStruct((NCH, 1, CW), F32), jax.ShapeDtypeStruct((N_DEV, 1, CW), F32)],
        scratch_shapes=[pltpu.VMEM((N_DEV, NCH, 1, CW), F32), pltpu.SemaphoreType.DMA((2, N_DEV))],
        compiler_params=pltpu.CompilerParams(vmem_limit_bytes=VMEM_LIMIT),
    )(payload, *deps)


def _exchange_in_chip(parts):
    W = len(parts)

    def body(*refs):
        p_refs, got_refs, (send_sems, recv_sems) = refs[:W], refs[W:2 * W], refs[2 * W:]
        x, y, c = _my_place()
        sibling = (x, y, 1 - c)
        copies = []
        for w in range(W):
            for q in range(4):
                copies.append(pltpu.make_async_remote_copy(
                    src_ref=p_refs[w].at[2 * q + (1 - c)], dst_ref=got_refs[w].at[q],
                    send_sem=send_sems.at[4 * w + q], recv_sem=recv_sems.at[4 * w + q],
                    device_id=sibling, device_id_type=MESH_ID))
        for cp in copies:
            cp.start()
        for cp in copies:
            cp.wait_recv()
        for cp in copies:
            cp.wait_send()

    return pl.pallas_call(
        body, name="grad_exchange_in_chip",
        in_specs=[HBM_SPEC] * W, out_specs=[HBM_SPEC] * W,
        out_shape=[jax.ShapeDtypeStruct((4,) + p.shape[1:], p.dtype) for p in parts],
        scratch_shapes=[pltpu.SemaphoreType.DMA((4 * W,)), pltpu.SemaphoreType.DMA((4 * W,))],
    )(*parts)


def _pair_sum(parts, got, core):
    _, R, C = parts.shape
    tr = _tile(R, max(PACK_ROW_ALIGN, PAIR_SUM_BLOCK // C), PACK_ROW_ALIGN)

    def body(c_ref, p_ref, g_ref, o_ref):
        o_ref[...] = (p_ref[...].astype(F32) + g_ref[...].astype(F32)).astype(o_ref.dtype)

    return pl.pallas_call(
        body, name="grad_pair_sum",
        grid_spec=pltpu.PrefetchScalarGridSpec(
            num_scalar_prefetch=1, grid=(4, R // tr),
            in_specs=[pl.BlockSpec((1, tr, C), lambda q, i, c_ref: (2 * q + c_ref[0], i, 0)),
                      pl.BlockSpec((1, tr, C), lambda q, i, c_ref: (q, i, 0))],
            out_specs=pl.BlockSpec((1, tr, C), lambda q, i, c_ref: (q, i, 0))),
        out_shape=jax.ShapeDtypeStruct((4, R, C), parts.dtype),
        compiler_params=_cparams(("parallel", "parallel")),
    )(core, parts, got)


HBM_SPEC = pl.BlockSpec(memory_space=pltpu.HBM)
SEM_SPEC = pl.BlockSpec(memory_space=pltpu.SEMAPHORE)
ANY_SPEC = pl.BlockSpec(memory_space=pl.ANY)
SPLIT_EFFECT = pltpu.SideEffectType.DATAFLOW_SIDE_EFFECTING


def _landing_zone(shape, dtype):
    return pltpu.with_memory_space_constraint(lax.empty(shape, dtype), pltpu.HBM)


def _split_start(name, arrays, lands, after, copies_of, per_array):
    W = len(arrays)
    after = tuple(after) if isinstance(after, (tuple, list)) else (after,)

    def body(*refs):
        x_refs, land_refs = refs[:W], refs[W:2 * W]
        send_sems, recv_sems = refs[2 * W + len(after)], refs[2 * W + len(after) + 1]
        token = refs[-1]
        k = 0
        for w in range(W):
            for src, dst, dev in copies_of(w, x_refs[w], land_refs[w]):
                pltpu.make_async_remote_copy(src_ref=src, dst_ref=dst, send_sem=send_sems.at[k], recv_sem=recv_sems.at[k],
                                             device_id=dev, device_id_type=MESH_ID).start()
                k += 1
        token[...] = jnp.zeros_like(token)

    n_copies = per_array * W
    hbm_of = lambda xs: tuple(pltpu.HBM(a.shape, a.dtype) for a in xs)
    out = pl.pallas_call(
        body, name=name,
        out_shape=(pltpu.SemaphoreType.DMA((n_copies,)), pltpu.SemaphoreType.DMA((n_copies,)))
        + hbm_of(arrays) + hbm_of(lands) + (jax.ShapeDtypeStruct((8, LANE), F32),),
        in_specs=(HBM_SPEC,) * (2 * W) + (ANY_SPEC,) * len(after),
        out_specs=(SEM_SPEC, SEM_SPEC) + (HBM_SPEC,) * (2 * W) + (pl.BlockSpec(memory_space=pltpu.VMEM),),
        input_output_aliases={i: 2 + i for i in range(2 * W)},
        compiler_params=pltpu.CompilerParams(has_side_effects=SPLIT_EFFECT),
    )(*[pltpu.with_memory_space_constraint(a, pltpu.HBM) for a in arrays], *lands, *after)
    return out[0], out[1], list(out[2:2 + W]), list(out[2 + W:2 + 2 * W]), out[-1]


def _split_wait(name, state, after, copies_of):
    send_sems, recv_sems, arrays, lands, _ = state
    W = len(arrays)
    after = tuple(after) if isinstance(after, (tuple, list)) else (after,)

    def body(*refs):
        x_refs, land_refs = refs[:W], refs[W:2 * W]
        send_sems, recv_sems = refs[2 * W], refs[2 * W + 1]
        k = 0
        for w in range(W):
            for src, dst, dev in copies_of(w, x_refs[w], land_refs[w]):
                cp = pltpu.make_async_remote_copy(src_ref=src, dst_ref=dst, send_sem=send_sems.at[k],
                                                  recv_sem=recv_sems.at[k], device_id=dev, device_id_type=MESH_ID)
                cp.wait_send()
                cp.wait_recv()
                k += 1

    out = pl.pallas_call(
        body, name=name,
        out_shape=tuple(pltpu.HBM(a.shape, a.dtype) for a in arrays + lands),
        in_specs=(HBM_SPEC,) * (2 * W) + (SEM_SPEC, SEM_SPEC) + (ANY_SPEC,) * len(after),
        out_specs=(HBM_SPEC,) * (2 * W),
        input_output_aliases={i: i for i in range(2 * W)},
        compiler_params=pltpu.CompilerParams(has_side_effects=SPLIT_EFFECT),
    )(*arrays, *lands, send_sems, recv_sems, *after)
    return list(out[:W]), list(out[W:])


def _scatter_copies(w, p_ref, land_ref):
    x, y, c = _my_place()
    my_chip = 2 * x + y
    return [(p_ref.at[2 * (x ^ (k >> 1)) + (y ^ (k & 1))], land_ref.at[my_chip], (x ^ (k >> 1), y ^ (k & 1), c))
            for k in range(1, 4)]


def _gather_copies(w, x_ref, land_ref):
    x, y, c = _my_place()
    me = _linear((x, y, c))
    devs = [(x, y, 1 - c)] + [(x ^ (k >> 1), y ^ (k & 1), c) for k in range(1, 4)]
    return [(x_ref, land_ref.at[me], d) for d in devs]


def _gather_forward(lands, name):
    W = len(lands)

    def body(*refs):
        land_refs, out_refs, (send_sems, recv_sems) = refs[:W], refs[W:2 * W], refs[2 * W:]
        x, y, c = _my_place()
        sibling = (x, y, 1 - c)
        sends, arrivals = [], []
        for w in range(W):
            for k in range(1, 4):
                px, py = x ^ (k >> 1), y ^ (k & 1)
                landed, theirs = _linear((px, py, c)), out_refs[w].at[_linear((px, py, 1 - c))]
                sem = 3 * w + k - 1
                sends.append(pltpu.make_async_remote_copy(
                    src_ref=land_refs[w].at[landed], dst_ref=out_refs[w].at[landed],
                    send_sem=send_sems.at[sem], recv_sem=recv_sems.at[sem], device_id=sibling, device_id_type=MESH_ID))
                arrivals.append(pltpu.make_async_remote_copy(
                    src_ref=theirs, dst_ref=theirs, send_sem=send_sems.at[sem], recv_sem=recv_sems.at[sem],
                    device_id=sibling, device_id_type=MESH_ID))
        for cp in sends:
            cp.start()
        for cp in arrivals:
            cp.wait_recv()
        for cp in sends:
            cp.wait_send()

    return pl.pallas_call(
        body, name=name,
        in_specs=[HBM_SPEC] * W, out_specs=[HBM_SPEC] * W,
        out_shape=[jax.ShapeDtypeStruct(l.shape, l.dtype) for l in lands],
        input_output_aliases={i: i for i in range(W)},
        scratch_shapes=[pltpu.SemaphoreType.DMA((3 * W,)), pltpu.SemaphoreType.DMA((3 * W,))],
    )(*lands)


def _with_own_slot(gathered, shard):
    return lax.dynamic_update_index_in_dim(gathered, shard[None], _linear(_my_place()), axis=0)


def _in_chip_copies(w, p_ref, land_ref):
    x, y, c = _my_place()
    return [(p_ref.at[2 * q + (1 - c)], land_ref.at[q], (x, y, 1 - c)) for q in range(4)]


def _in_chip_start(parts, tag):
    lands = [_landing_zone((4,) + p.shape[1:], p.dtype) for p in parts]
    return _split_start("grad_in_chip_start_" + tag, parts, lands, (), _in_chip_copies, 4)


def _reduce_scatter_begin(parts, tag, in_chip_state=None, after=()):
    if in_chip_state is None:
        got = _exchange_in_chip(parts)
    else:
        parts, got = _split_wait("grad_in_chip_wait_" + tag, in_chip_state, after, _in_chip_copies)
    core = lax.axis_index("c").astype(jnp.int32).reshape(1)
    chip_parts = [_pair_sum(p, g, core) for p, g in zip(parts, got)]
    lands = [_landing_zone(p.shape, p.dtype) for p in chip_parts]
    return _split_start("grad_scatter_start_" + tag, chip_parts, lands, got[0], _scatter_copies, 3)


def _reduce_scatter_end(state, after, tag):
    return _split_wait("grad_scatter_wait_" + tag, state, after, _scatter_copies)


def kernel(x, c, positions, w_ada, b_ada, w_in, g_q_a, w_q_b, g_kv_a, w_kv_b, w_o_a, w_conv, w_o_b, w_o, ln1_g, ln1_b, w_ffn_in, w_ffn_out, ln2_g, ln2_b, loss_target, m_w_ada, m_b_ada, m_w_in, m_g_q_a, m_w_q_b, m_g_kv_a, m_w_kv_b, m_w_o_a, m_w_conv, m_w_o_b, m_w_o, m_ln1_g, m_ln1_b, m_w_ffn_in, m_w_ffn_out, m_ln2_g, m_ln2_b, v_w_ada, v_b_ada, v_w_in, v_g_q_a, v_w_q_b, v_g_kv_a, v_w_kv_b, v_w_o_a, v_w_conv, v_w_o_b, v_w_o, v_ln1_g, v_ln1_b, v_w_ffn_in, v_w_ffn_out, v_ln2_g, v_ln2_b):
    x2, tgt = x[0], loss_target[0]
    S, D = x2.shape
    Lq, Lkv = g_q_a.shape[1], g_kv_a.shape[1]
    H = w_q_b.shape[2] * N_DEV // QK_CAT
    F = w_ffn_out.shape[1] * N_DEV
    assert Lq == Lkv and (Lq + Lkv) % COL_BLOCK == 0 and D % COL_BLOCK == 0
    front = Lq + Lkv + QK_ROPE
    front_pad = _round_up(front, COL_BLOCK)
    kr_blk = (Lq + Lkv) // COL_BLOCK
    blk_b = front_pad // COL_BLOCK
    nblk = D // COL_BLOCK
    blk_c, blk_x, blk_ga, blk_gb = blk_b + nblk, blk_b + 2 * nblk, blk_b + 3 * nblk, blk_b + 4 * nblk
    ts = _tile(S, 256, 8)
    T = _tile(S, min(512, S // 2), CHUNK)
    tb = _tile(F, 2816)
    me = _linear(_my_place())

    cw = w_ada.shape[2]
    b_mine = lax.dynamic_slice(b_ada, (0, me * cw), (1, cw)).reshape(1, 1, cw)
    mod_blocks, cact_all, wconv_all = _ada_fwd(c.reshape(1, 1, D), w_conv[0].reshape(1, 1, -1), w_ada[0], b_mine)
    mod = mod_blocks.reshape(6, D)
    cact_all = cact_all.reshape(N_DEV, D)
    w_conv_full = wconv_all.reshape(N_DEV, CONV_K, -1).transpose(1, 0, 2).reshape(CONV_K, D)

    landing = lambda shards: [_landing_zone((N_DEV,) + s.shape, BF16) for s in shards]
    gathered = lambda lands, shards, tag: [_with_own_slot(g, s) for g, s in
                                           zip(_gather_forward(lands, tag + "_gather_forward"), shards)]
    half = D // 2
    w_in_b = w_in[0].astype(BF16)
    first, second = [w_in_b[:half]], [w_in_b[half:], w_q_b[0].astype(BF16), w_kv_b[0].astype(BF16)]
    mid = [w[0].astype(BF16) for w in (w_o_a, w_o_b, w_o)]
    last = [w[0].astype(BF16) for w in (w_ffn_in, w_ffn_out)]
    first_state = _split_start("first_gather_start", first, landing(first), mod_blocks, _gather_copies, 4)
    second_state = _split_start("second_gather_start", second, landing(second), first_state[4], _gather_copies, 4)
    u = _modulate_in(x2, mod, ts)

    first_shards, first_lands = _split_wait("first_gather_wait", first_state, (u, second_state[4]), _gather_copies)
    (g_in_top,) = gathered(first_lands, first_shards, "first")
    w_in_top = _assemble_w_in(g_in_top, front, front_pad, D, 0)
    proj_top = _matmul(u, w_in_top, "nn", F32, "proj_top", k_rows=(0, half))
    second_shards, second_lands = _split_wait("second_gather_wait", second_state, (proj_top,), _gather_copies)
    g_in_bottom, wq_s, wkv_s = gathered(second_lands, second_shards, "second")
    mid_state = _split_start("mid_gather_start", mid, landing(mid), g_in_bottom, _gather_copies, 4)
    last_state = _split_start("last_gather_start", last, landing(last), mid_state[4], _gather_copies, 4)
    w_in_p = _assemble_w_in(g_in_bottom, front, front_pad, D, half, into=w_in_top)

    inv_freq = 1.0 / (ROPE_THETA ** (jnp.arange(0, QK_ROPE, 2, dtype=F32) / QK_ROPE))
    ang = positions[0].astype(F32)[:, None] * inv_freq
    cos2 = jnp.concatenate([jnp.cos(ang), jnp.cos(ang)], axis=-1)
    sin2 = jnp.concatenate([jnp.sin(ang), jnp.sin(ang)], axis=-1)
    one, zero = jnp.ones((S, QK_NOPE), F32), jnp.zeros((S, QK_NOPE), F32)
    cos_q, sin_q = jnp.concatenate([one, cos2, one, cos2], axis=-1), jnp.concatenate([zero, sin2, zero, sin2], axis=-1)
    cos_k, sin_k = jnp.tile(cos2, (1, COL_BLOCK // QK_ROPE)), jnp.tile(sin2, (1, COL_BLOCK // QK_ROPE))

    proj = _matmul(u, w_in_p, "nn", F32, "proj", k_rows=(half, half), init=proj_top, deps=(last_state[4],))
    qn = _rms_fwd(proj, g_q_a, 0, Lq, ts, "rms_q")
    kvn = _rms_fwd(proj, g_kv_a, 1, Lkv, ts, "rms_kv")
    q = _matmul(qn, wq_s, "nn", F32, "q_up")
    kv = _matmul(kvn, wkv_s, "nn", F32, "kv_up")
    qc, kc, vh = _qk_prep(q, kv, proj, kr_blk, cos_q, sin_q, cos_k, sin_k, H, ts)
    attn, lse = _attn_fwd(qc, kc, vh, T)
    mid_shards, mid_lands = _split_wait("mid_gather_wait", mid_state, lse, _gather_copies)
    w_oa_f, w_ob_f, w_o_f = [g.reshape(-1, D) for g in gathered(mid_lands, mid_shards, "mid")]
    ya = _matmul(attn, w_oa_f, "nn", F32, "attn_out")
    cbc = _conv_fwd(proj, w_conv_full, blk_b, blk_c, blk_x)
    yb = _matmul(cbc, w_ob_f, "nn", F32, "conv_out")
    merged = _merge_fwd(proj, ya, yb, blk_ga, blk_gb, ts)
    mix = _matmul(merged, w_o_f, "nn", F32, "mix_out")
    xhat1, rstd1, u2 = _ln1_fwd(x2, mix, mod, ln1_g, ln1_b, ts)
    last_shards, last_lands = _split_wait("last_gather_wait", last_state, u2, _gather_copies)
    w_fi_s, g_fo = gathered(last_lands, last_shards, "last")
    w_fo_f = g_fo.reshape(F, D)
    hh = _matmul(u2, w_fi_s, "nn", F32, "ffn_in")
    act = _swiglu_fwd(hh, ts, tb)
    ffn = _matmul(act, w_fo_f, "nn", F32, "ffn_out")
    loss_part, dffn, dx1a, vec2 = _ln2_loss(xhat1, ffn, tgt, mod, ln1_g, ln1_b, ln2_g, ln2_b, ts)
    loss = lax.psum(loss_part[0, 0], AXES)

    gw_fo = _matmul(act, dffn, "tn", BF16, "grad_w_ffn_out")
    da = _matmul(dffn, w_fo_f, "nt", F32, "d_act")
    dh = _swiglu_bwd(da, hh, ts, tb)
    gw_fi = _matmul(u2, dh, "tn", BF16, "grad_w_ffn_in", out_shards=True)
    ffn_in_chip = _in_chip_start([gw_fi, gw_fo.reshape(N_DEV, -1, D)], "ffn")
    du2 = _matmul(dh, w_fi_s, "nt", F32, "d_u2", deps=(ffn_in_chip[4],))
    ffn_state = _reduce_scatter_begin(None, "ffn", ffn_in_chip, after=(du2,))
    dxa, dmix, vec1 = _ln1_bwd(du2, dx1a, xhat1, rstd1, mix, mod, ln1_g, ln1_b, ts)
    gw_o = _matmul(merged, dmix, "tn", BF16, "grad_w_o", deps=(ffn_state[4],))
    dmerged = _matmul(dmix, w_o_f, "nt", F32, "d_merged")
    dya, dyb, dga, dgb = _merge_bwd(dmerged, proj, ya, yb, blk_ga, blk_gb, ts)
    gw_ob = _matmul(cbc, dyb, "tn", BF16, "grad_w_o_b")
    dcbc = _matmul(dyb, w_ob_f, "nt", F32, "d_conv")
    dcb, dcc, dcx, dwconv = _conv_bwd(dcbc, proj, w_conv_full, blk_b, blk_c, blk_x)
    gw_oa = _matmul(attn, dya, "tn", BF16, "grad_w_o_a")
    mix_in_chip = _in_chip_start([g.reshape(N_DEV, -1, D) for g in (gw_oa, gw_ob, gw_o)], "mix")
    dattn = _matmul(dya, w_oa_f, "nt", F32, "d_attn", deps=(mix_in_chip[4],))
    dqc, dkc, dvh = _attn_bwd(qc, kc, vh, dattn, attn, lse, T)
    ffn_own, ffn_got = _reduce_scatter_end(ffn_state, dqc, "ffn")
    mix_state = _reduce_scatter_begin(None, "mix", mix_in_chip, after=(dqc,))
    dq, dkv, dkr = _qk_bwd(dqc, dkc, dvh, cos_q, sin_q, cos_k, sin_k, ts)
    gw_qb = _matmul(qn, dq, "tn", BF16, "grad_w_q_b", out_shards=True, deps=(mix_state[4],))
    dqn = _matmul(dq, wq_s, "nt", F32, "d_qn")
    gw_kvb = _matmul(kvn, dkv, "tn", BF16, "grad_w_kv_b", out_shards=True)
    dkvn = _matmul(dkv, wkv_s, "nt", F32, "d_kvn")
    dqa, dgq = _rms_bwd(dqn, proj, g_q_a, 0, Lq, ts, "rms_q_bwd")
    dkva, dgkv = _rms_bwd(dkvn, proj, g_kv_a, 1, Lkv, ts, "rms_kv_bwd")
    dproj = jnp.concatenate([dqa, dkva, dkr, dcb, dcc, dcx, dga, dgb], axis=1)
    gw_in_top = _matmul(u[:, :half], dproj, "tn", BF16, "grad_w_in_top")
    top_state = _reduce_scatter_begin([_split_w_in(gw_in_top, front, front_pad)], "in_top")
    gw_in_bottom = _matmul(u[:, half:], dproj, "tn", BF16, "grad_w_in_bottom", deps=(top_state[4],))
    mix_own, mix_got = _reduce_scatter_end(mix_state, gw_in_bottom, "mix")
    in_state = _reduce_scatter_begin([_split_w_in(gw_in_bottom, front, front_pad), gw_qb, gw_kvb], "in")
    du = _matmul(dproj, w_in_p, "nt", F32, "d_u", deps=(in_state[4],))
    grad_x, vec0 = _grad_x(du, dxa, x2, mod, ts)

    my_chip = (2 * lax.axis_index("x") + lax.axis_index("y")).astype(jnp.int32).reshape(1)
    arrived = {}
    for nm, w, m, v, own, got in (
            ("w_ffn_in", w_ffn_in, m_w_ffn_in, v_w_ffn_in, ffn_own[0], ffn_got[0]),
            ("w_ffn_out", w_ffn_out, m_w_ffn_out, v_w_ffn_out, ffn_own[1], ffn_got[1]),
            ("w_o_a", w_o_a, m_w_o_a, v_w_o_a, mix_own[0], mix_got[0]),
            ("w_o_b", w_o_b, m_w_o_b, v_w_o_b, mix_own[1], mix_got[1]),
            ("w_o", w_o, m_w_o, v_w_o, mix_own[2], mix_got[2])):
        arrived[nm] = [a[None] for a in _adamw_reduced(w[0], own, got, m[0], v[0], my_chip, "adamw_" + nm)]

    dmod = jnp.concatenate([vec0[0], vec0[1], vec1[4], vec1[0], vec1[1], vec2[2]])
    small = jnp.concatenate([dmod, dgq[0], dgkv[0], vec1[2], vec1[3], vec2[0], vec2[1], dwconv[:CONV_K].reshape(-1)])
    n_small = small.shape[0]
    nch = _round_up(n_small, cw) // cw
    payload = jnp.pad(small, (0, nch * cw - n_small)).reshape(nch, 1, cw)
    summed, dmod_mine = _ada_bwd(payload, deps=[res[1] for res in arrived.values()])
    arrived["w_ada"] = [a[None] for a in _adamw_ada(w_ada[0], cact_all.T, dmod_mine.reshape(N_DEV, cw),
                                                    m_w_ada[0], v_w_ada[0])]
    summed = summed.reshape(-1)
    offs = [0, 6 * D, 6 * D + Lq, 6 * D + Lq + Lkv]
    offs += [offs[-1] + D * k for k in range(1, 5)]
    g_b_ada = summed[offs[0]:offs[1]].reshape(1, -1)
    g_gq = summed[offs[1]:offs[2]].reshape(1, -1)
    g_gkv = summed[offs[2]:offs[3]].reshape(1, -1)
    g_ln1g, g_ln1b, g_ln2g, g_ln2b = [summed[offs[3 + k]:offs[4 + k]].reshape(1, -1) for k in range(4)]
    wc = w_conv.shape[2]
    g_wconv = lax.dynamic_slice(summed[offs[7]:offs[7] + CONV_K * D].reshape(CONV_K, D), (0, me * wc), (CONV_K, wc))

    names = ["w_ada", "b_ada", "w_in", "g_q_a", "w_q_b", "g_kv_a", "w_kv_b", "w_o_a", "w_conv", "w_o_b", "w_o",
             "ln1_g", "ln1_b", "w_ffn_in", "w_ffn_out", "ln2_g", "ln2_b"]
    weights = [w_ada, b_ada, w_in, g_q_a, w_q_b, g_kv_a, w_kv_b, w_o_a, w_conv, w_o_b, w_o, ln1_g, ln1_b,
               w_ffn_in, w_ffn_out, ln2_g, ln2_b]
    moms = [m_w_ada, m_b_ada, m_w_in, m_g_q_a, m_w_q_b, m_g_kv_a, m_w_kv_b, m_w_o_a, m_w_conv, m_w_o_b, m_w_o,
            m_ln1_g, m_ln1_b, m_w_ffn_in, m_w_ffn_out, m_ln2_g, m_ln2_b]
    vels = [v_w_ada, v_b_ada, v_w_in, v_g_q_a, v_w_q_b, v_g_kv_a, v_w_kv_b, v_w_o_a, v_w_conv, v_w_o_b, v_w_o,
            v_ln1_g, v_ln1_b, v_w_ffn_in, v_w_ffn_out, v_ln2_g, v_ln2_b]
    grad_of = {"b_ada": g_b_ada, "g_q_a": g_gq, "g_kv_a": g_gkv, "w_conv": g_wconv,
               "ln1_g": g_ln1g, "ln1_b": g_ln1b, "ln2_g": g_ln2g, "ln2_b": g_ln2b}
    state_of = dict(zip(names, zip(weights, moms, vels)))
    results = dict(arrived)

    def update(nm, reduced=None):
        w, m, v = state_of[nm]
        shp = w.shape
        w2 = w.reshape(shp[-2], shp[-1]) if w.ndim == 3 else w
        m2, v2 = m.reshape(w2.shape), v.reshape(w2.shape)
        if reduced is None:
            g2 = grad_of[nm].reshape(w2.shape)
            res = (g2,) + tuple(_adamw(w2, g2, m2, v2, "adamw_" + nm))
        else:
            res = _adamw_reduced(w2, reduced[0], reduced[1], m2, v2, my_chip, "adamw_" + nm)
        results[nm] = [a.reshape(shp) for a in res]

    for nm in grad_of:
        update(nm)
    done = [res[1] for res in results.values()]
    top_own, top_got = _reduce_scatter_end(top_state, done, "in_top")
    w_in_top_res = _adamw_reduced(w_in[0], top_own[0], top_got[0], m_w_in[0], v_w_in[0], my_chip, "adamw_w_in_top")
    in_own, in_got = _reduce_scatter_end(in_state, w_in_top_res[1], "in")
    results["w_in"] = [a[None] for a in _adamw_reduced(w_in[0], in_own[0], in_got[0], m_w_in[0], v_w_in[0], my_chip,
                                                       "adamw_w_in_bottom", row0=half, into=w_in_top_res)]
    for nm, own, got in zip(("w_q_b", "w_kv_b"), in_own[1:], in_got[1:]):
        update(nm, (own, got))
    outs = [[results[nm][k] for nm in names] for k in range(4)]
    return (loss, grad_x.reshape(x.shape), *outs[0], *outs[1], *outs[2], *outs[3])
```

```python
import functools

import jax
import jax.numpy as jnp
from jax import lax
from jax.experimental import pallas as pl
from jax.experimental.pallas import tpu as pltpu

F32 = jnp.float32
BF16 = jnp.bfloat16
MESH_ID = pl.DeviceIdType.MESH
AXES = ("x", "y", "c")
N_DEV = 8

CHUNK = 64
QK_NOPE = 128
QK_ROPE = 64
V_HEAD = 128
QK_CAT = QK_NOPE + QK_ROPE
ROPE_THETA = 10000.0
ATTN_SCALE = (QK_NOPE + QK_ROPE) ** -0.5
CONV_K = 3
DEEPNORM_ALPHA = 2.0 ** 0.25
LN_EPS = 1e-5
RMS_EPS = 1e-6
NEG_INF = -1e30

ADAM_LR = 0.001
ADAM_B1 = 0.9
ADAM_B2 = 0.999
ADAM_EPS = 1e-08
ADAM_WD = 0.01
ADAM_STEP = 10

LANE = 128
COL_BLOCK = 256
PACK_ROW_ALIGN = 16
PAIR_SUM_BLOCK = 1 << 20
VMEM_LIMIT = 48 * 1024 * 1024


def _round_up(n, m):
    return (n + m - 1) // m * m


def _tile(n, pref, align=LANE):
    best = None
    t = align
    while t <= min(n, pref):
        if n % t == 0:
            best = t
        t += align
    return best if best is not None else n


def _cparams(sem=None):
    return pltpu.CompilerParams(dimension_semantics=sem, vmem_limit_bytes=VMEM_LIMIT)


def _sigmoid(x):
    return 0.5 * jnp.tanh(0.5 * x) + 0.5


def _matmul(a, b, mode, out_dtype, name, tm=1024, tn=1024, tk=2048, deps=(), out_shards=False, k_rows=None,
            init=None):
    b_shards = b.ndim == 3
    n = b.shape[2] if b_shards else (b.shape[1] // N_DEV if out_shards else None)
    if mode == "nn":
        (M, K), (K2, N) = a.shape, (b.shape[1], N_DEV * n) if b_shards else b.shape
    elif mode == "nt":
        (M, K), (N, K2) = a.shape, (b.shape[1], N_DEV * n) if b_shards else b.shape
    else:
        (K, M), (K2, N) = a.shape, b.shape
    assert K == K2, (a.shape, b.shape, mode)
    tm = _tile(M, tm)
    tn = n if (mode != "nt" and n is not None) else _tile(N, tn)
    k_row0, k_len = k_rows if k_rows is not None else (0, K)
    tk = n if (mode == "nt" and b_shards) else _tile(k_len, tk)
    nk, k0 = k_len // tk, k_row0 // tk
    if mode == "nn":
        a_spec = pl.BlockSpec((tm, tk), lambda i, j, k: (i, k0 + k))
        b_spec = (pl.BlockSpec((1, tk, n), lambda i, j, k: (j, k, 0)) if b_shards
                  else pl.BlockSpec((tk, tn), lambda i, j, k: (k0 + k, j)))
        dims = (((1,), (0,)), ((), ()))
    elif mode == "nt":
        a_spec = pl.BlockSpec((tm, tk), lambda i, j, k: (i, k))
        b_spec = (pl.BlockSpec((1, tn, n), lambda i, j, k: (k, j, 0)) if b_shards
                  else pl.BlockSpec((tn, tk), lambda i, j, k: (j, k)))
        dims = (((1,), (1,)), ((), ()))
    else:
        a_spec = pl.BlockSpec((tk, tm), lambda i, j, k: (k, i))
        b_spec = pl.BlockSpec((tk, tn), lambda i, j, k: (k, j))
        dims = (((0,), (0,)), ((), ()))
    if out_shards:
        out_spec = pl.BlockSpec((1, tm, n), lambda i, j, k: (j, i, 0))
        out_shape = jax.ShapeDtypeStruct((N_DEV, M, n), out_dtype)
    else:
        out_spec = pl.BlockSpec((tm, tn), lambda i, j, k: (i, j))
        out_shape = jax.ShapeDtypeStruct((M, N), out_dtype)

    def product(a_ref, b_ref):
        b_blk = b_ref[0] if b_shards else b_ref[...]
        return lax.dot_general(a_ref[...].astype(BF16), b_blk.astype(BF16), dims, preferred_element_type=F32)

    def write(o_ref, value):
        if out_shards:
            o_ref[0] = value.astype(o_ref.dtype)
        else:
            o_ref[...] = value.astype(o_ref.dtype)

    def body_whole_k(a_ref, b_ref, *rest):
        value = product(a_ref, b_ref)
        write(rest[-1], value if init is None else value + rest[0][...])

    def body_split_k(a_ref, b_ref, *rest):
        o_ref, acc_ref = rest[-2:]
        k = pl.program_id(2)

        @pl.when(k == 0)
        def _():
            acc_ref[...] = jnp.zeros_like(acc_ref) if init is None else rest[0][...]

        acc_ref[...] += product(a_ref, b_ref)

        @pl.when(k == nk - 1)
        def _():
            write(o_ref, acc_ref[...])

    return pl.pallas_call(
        body_whole_k if nk == 1 else body_split_k, name=name, grid=(M // tm, N // tn, nk),
        in_specs=[a_spec, b_spec] + ([] if init is None else [out_spec]) + [ANY_SPEC] * len(deps),
        out_specs=out_spec, out_shape=out_shape,
        scratch_shapes=[] if nk == 1 else [pltpu.VMEM((tm, tn), F32)],
        compiler_params=_cparams(("parallel", "parallel", "arbitrary")),
    )(a, b, *(() if init is None else (init,)), *deps)


def _assemble_w_in(shards, front, front_pad, rows, row0, into=None):
    _, K, n = shards.shape
    gap = front_pad - front
    tk = _tile(K, 256, PACK_ROW_ALIGN)
    blk0 = row0 // tk

    def body(g_ref, *rest):
        o_ref = rest[-1]
        if gap:
            o_ref[:, front:front_pad] = jnp.zeros((tk, gap), o_ref.dtype)
        for j in range(N_DEV):
            lo, hi = j * n, (j + 1) * n
            if lo < front < hi:
                o_ref[:, lo:front] = g_ref[j, :, 0:front - lo]
                o_ref[:, front_pad:hi + gap] = g_ref[j, :, front - lo:n]
            else:
                off = 0 if hi <= front else gap
                o_ref[:, lo + off:hi + off] = g_ref[j]

    return pl.pallas_call(
        body, name="assemble_w_in", grid=(K // tk,),
        in_specs=[pl.BlockSpec((N_DEV, tk, n), lambda i: (0, i, 0))] + ([] if into is None else [ANY_SPEC]),
        out_specs=pl.BlockSpec((tk, N_DEV * n + gap), lambda i: (blk0 + i, 0)),
        out_shape=jax.ShapeDtypeStruct((rows, N_DEV * n + gap), shards.dtype),
        input_output_aliases={} if into is None else {1: 0},
        compiler_params=_cparams(("parallel",)),
    )(*([shards] if into is None else [shards, into]))


def _split_w_in(w, front, front_pad):
    K, NP = w.shape
    gap = front_pad - front
    n = (NP - gap) // N_DEV
    tk = _tile(K, 256, PACK_ROW_ALIGN)

    def body(w_ref, o_ref):
        for j in range(N_DEV):
            lo, hi = j * n, (j + 1) * n
            if lo < front < hi:
                o_ref[j, :, 0:front - lo] = w_ref[:, lo:front]
                o_ref[j, :, front - lo:n] = w_ref[:, front_pad:hi + gap]
            else:
                off = 0 if hi <= front else gap
                o_ref[j] = w_ref[:, lo + off:hi + off]

    return pl.pallas_call(
        body, name="split_grad_w_in", grid=(K // tk,),
        in_specs=[pl.BlockSpec((tk, NP), lambda i: (i, 0))],
        out_specs=pl.BlockSpec((N_DEV, tk, n), lambda i: (0, i, 0)),
        out_shape=jax.ShapeDtypeStruct((N_DEV, K, n), w.dtype),
        compiler_params=_cparams(("parallel",)),
    )(w)


def _modulate_in(x, mod, ts):
    S, D = x.shape

    def body(x_ref, mod_ref, u_ref):
        u_ref[...] = (x_ref[...] * (1.0 + mod_ref[1:2, :]) + mod_ref[0:1, :]).astype(BF16)

    return pl.pallas_call(
        body, name="modulate_in", grid=(S // ts,),
        in_specs=[pl.BlockSpec((ts, D), lambda i: (i, 0)), pl.BlockSpec((6, D), lambda i: (0, 0))],
        out_specs=pl.BlockSpec((ts, D), lambda i: (i, 0)),
        out_shape=jax.ShapeDtypeStruct((S, D), BF16),
        compiler_params=_cparams(("parallel",)),
    )(x, mod)


def _rms_fwd(proj, g, blk, L, ts, name):
    S = proj.shape[0]

    def body(a_ref, g_ref, y_ref):
        a = a_ref[...]
        r = lax.rsqrt(jnp.mean(a * a, axis=-1, keepdims=True) + RMS_EPS)
        y_ref[...] = (a * r * g_ref[...]).astype(BF16)

    return pl.pallas_call(
        body, name=name, grid=(S // ts,),
        in_specs=[pl.BlockSpec((ts, L), lambda i: (i, blk)), pl.BlockSpec((1, L), lambda i: (0, 0))],
        out_specs=pl.BlockSpec((ts, L), lambda i: (i, 0)),
        out_shape=jax.ShapeDtypeStruct((S, L), BF16),
        compiler_params=_cparams(("parallel",)),
    )(proj, g)


def _rope_partner(x, period, start):
    w = x.shape[-1]
    lane = lax.broadcasted_iota(jnp.int32, x.shape, x.ndim - 1) % period
    first = (lane >= start) & (lane < start + QK_ROPE // 2)
    from_right = pltpu.roll(x, w - QK_ROPE // 2, axis=x.ndim - 1)
    from_left = pltpu.roll(x, QK_ROPE // 2, axis=x.ndim - 1)
    return jnp.where(first, -from_right, from_left)


def _qk_prep(q, kv, proj, kr_blk, cos_q, sin_q, cos_k, sin_k, H, ts):
    S = q.shape[0]
    pair = 2 * QK_CAT
    kv_w = QK_NOPE + V_HEAD

    def body(q_ref, kv_ref, kr_ref, cq_ref, sq_ref, ck_ref, sk_ref, qc_ref, kc_ref, vh_ref):
        kr = kr_ref[...]
        kr = kr * ck_ref[...] + _rope_partner(kr, QK_ROPE, 0) * sk_ref[...]
        kr = kr[:, :QK_ROPE].astype(BF16)
        for p in range(H // 2):
            x = q_ref[:, p * pair:(p + 1) * pair]
            x = x * cq_ref[...] + _rope_partner(x, QK_CAT, QK_NOPE) * sq_ref[...]
            qc_ref[2 * p] = x[:, :QK_CAT].astype(BF16)
            qc_ref[2 * p + 1] = x[:, QK_CAT:].astype(BF16)
        for h in range(H):
            kc_ref[h, :, 0:QK_NOPE] = kv_ref[:, h * kv_w:h * kv_w + QK_NOPE].astype(BF16)
            kc_ref[h, :, QK_NOPE:QK_CAT] = kr
            vh_ref[h, :, :] = kv_ref[:, h * kv_w + QK_NOPE:(h + 1) * kv_w].astype(BF16)

    row = lambda w: pl.BlockSpec((ts, w), lambda i: (i, 0))
    return pl.pallas_call(
        body, name="qk_prep", grid=(S // ts,),
        in_specs=[row(H * QK_CAT), row(H * kv_w),
                  pl.BlockSpec((ts, COL_BLOCK), lambda i: (i, kr_blk)),
                  row(pair), row(pair), row(COL_BLOCK), row(COL_BLOCK)],
        out_specs=[pl.BlockSpec((H, ts, QK_CAT), lambda i: (0, i, 0)),
                   pl.BlockSpec((H, ts, QK_CAT), lambda i: (0, i, 0)),
                   pl.BlockSpec((H, ts, V_HEAD), lambda i: (0, i, 0))],
        out_shape=[jax.ShapeDtypeStruct((H, S, QK_CAT), BF16), jax.ShapeDtypeStruct((H, S, QK_CAT), BF16),
                   jax.ShapeDtypeStruct((H, S, V_HEAD), BF16)],
        compiler_params=_cparams(("parallel",)),
    )(q, kv, proj, cos_q, sin_q, cos_k, sin_k)


NT_DIMS = (((1,), (1,)), ((), ()))
TN_DIMS = (((0,), (0,)), ((), ()))


def _diag_mask(T):
    rows = lax.broadcasted_iota(jnp.int32, (T, T), 0) // CHUNK
    cols = lax.broadcasted_iota(jnp.int32, (T, T), 1) // CHUNK
    return cols <= rows


def _attn_fwd(qc, kc, vh, T):
    H, S, _ = qc.shape
    n = S // T

    def body(q_ref, k_ref, v_ref, o_ref, lse_ref):
        q = q_ref[0]

        def block(i):
            L = (i + 1) * T
            s_old = lax.dot_general(q, k_ref[0, 0:i * T, :], NT_DIMS, preferred_element_type=F32) if i else None
            s_diag = lax.dot_general(q, k_ref[0, i * T:L, :], NT_DIMS, preferred_element_type=F32)
            s_diag = jnp.where(_diag_mask(T), s_diag, NEG_INF)
            m = jnp.max(s_diag, axis=-1, keepdims=True)
            if i:
                m = jnp.maximum(m, jnp.max(s_old, axis=-1, keepdims=True))
            p_diag = jnp.exp((s_diag - m) * ATTN_SCALE)
            l = jnp.sum(p_diag, axis=-1, keepdims=True)
            acc = jnp.dot(p_diag.astype(BF16), v_ref[0, i * T:L, :], preferred_element_type=F32)
            if i:
                p_old = jnp.exp((s_old - m) * ATTN_SCALE)
                l = l + jnp.sum(p_old, axis=-1, keepdims=True)
                acc = acc + jnp.dot(p_old.astype(BF16), v_ref[0, 0:i * T, :], preferred_element_type=F32)
            o_ref[...] = acc / l
            lse_ref[0] = m * ATTN_SCALE + jnp.log(l)

        for i in range(n):
            pl.when(pl.program_id(1) == i)(functools.partial(block, i))

    return pl.pallas_call(
        body, name="attn_fwd", grid=(H, n),
        in_specs=[pl.BlockSpec((1, T, QK_CAT), lambda h, i: (h, i, 0)),
                  pl.BlockSpec((1, S, QK_CAT), lambda h, i: (h, 0, 0)),
                  pl.BlockSpec((1, S, V_HEAD), lambda h, i: (h, 0, 0))],
        out_specs=[pl.BlockSpec((T, V_HEAD), lambda h, i: (i, h)),
                   pl.BlockSpec((1, T, 1), lambda h, i: (h, i, 0))],
        out_shape=[jax.ShapeDtypeStruct((S, H * V_HEAD), F32), jax.ShapeDtypeStruct((H, S, 1), F32)],
        compiler_params=_cparams(("parallel", "arbitrary")),
    )(qc, kc, vh)


def _shift_rows(z, k):
    if k == 0:
        return z
    n = z.shape[0]
    row = lax.broadcasted_iota(jnp.int32, z.shape, 0)
    if k > 0:
        return jnp.where(row >= k, pltpu.roll(z, k, axis=0), 0.0)
    return jnp.where(row < n + k, pltpu.roll(z, n + k, axis=0), 0.0)


def _conv_fwd(proj, w_conv, blk_b, blk_c, blk_x):
    S = proj.shape[0]
    D = w_conv.shape[1]
    nb = D // COL_BLOCK

    def body(cb_ref, cc_ref, cx_ref, w_ref, o_ref):
        z = cc_ref[...] * cx_ref[...]
        conv = w_ref[2:3, :] * z + w_ref[1:2, :] * _shift_rows(z, 1) + w_ref[0:1, :] * _shift_rows(z, 2)
        o_ref[...] = (cb_ref[...] * conv).astype(BF16)

    col = lambda off: pl.BlockSpec((S, COL_BLOCK), lambda j: (0, off + j))
    return pl.pallas_call(
        body, name="conv_fwd", grid=(nb,),
        in_specs=[col(blk_b), col(blk_c), col(blk_x), pl.BlockSpec((CONV_K, COL_BLOCK), lambda j: (0, j))],
        out_specs=pl.BlockSpec((S, COL_BLOCK), lambda j: (0, j)),
        out_shape=jax.ShapeDtypeStruct((S, D), BF16),
        compiler_params=_cparams(("parallel",)),
    )(proj, proj, proj, w_conv)


def _merge_fwd(proj, ya, yb, blk_ga, blk_gb, ts):
    S, D = ya.shape
    nb = D // COL_BLOCK

    def body(ga_ref, gb_ref, ya_ref, yb_ref, o_ref):
        o_ref[...] = (_sigmoid(ga_ref[...]) * ya_ref[...] + _sigmoid(gb_ref[...]) * yb_ref[...]).astype(BF16)

    row = pl.BlockSpec((ts, D), lambda i: (i, 0))
    seg = lambda blk: pl.BlockSpec((pl.Element(ts), pl.Element(D)), lambda i: (i * ts, blk * COL_BLOCK))
    return pl.pallas_call(
        body, name="merge_fwd", grid=(S // ts,),
        in_specs=[seg(blk_ga), seg(blk_gb), row, row],
        out_specs=row,
        out_shape=jax.ShapeDtypeStruct((S, D), BF16),
        compiler_params=_cparams(("parallel",)),
    )(proj, proj, ya, yb)


def _ln1_fwd(x, mix, mod, g, b, ts):
    S, D = x.shape

    def body(x_ref, mix_ref, mod_ref, g_ref, b_ref, xhat_ref, rstd_ref, u2_ref):
        r = DEEPNORM_ALPHA * x_ref[...] + mod_ref[2:3, :] * mix_ref[...]
        mu = jnp.mean(r, axis=-1, keepdims=True)
        d = r - mu
        rstd = lax.rsqrt(jnp.mean(d * d, axis=-1, keepdims=True) + LN_EPS)
        xhat = d * rstd
        xhat_ref[...] = xhat
        rstd_ref[...] = rstd
        x1 = xhat * g_ref[...] + b_ref[...]
        u2_ref[...] = (x1 * (1.0 + mod_ref[4:5, :]) + mod_ref[3:4, :]).astype(BF16)

    row = pl.BlockSpec((ts, D), lambda i: (i, 0))
    vec = lambda r: pl.BlockSpec((r, D), lambda i: (0, 0))
    return pl.pallas_call(
        body, name="ln1_fwd", grid=(S // ts,),
        in_specs=[row, row, vec(6), vec(1), vec(1)],
        out_specs=[row, pl.BlockSpec((ts, 1), lambda i: (i, 0)), row],
        out_shape=[jax.ShapeDtypeStruct((S, D), F32), jax.ShapeDtypeStruct((S, 1), F32),
                   jax.ShapeDtypeStruct((S, D), BF16)],
        compiler_params=_cparams(("parallel",)),
    )(x, mix, mod, g, b)


def _swiglu_fwd(h, ts, tb):
    S, F2 = h.shape
    F = F2 // 2
    nb = F // tb

    def body(hg_ref, hu_ref, a_ref):
        hg = hg_ref[...]
        a_ref[...] = (hg * _sigmoid(hg) * hu_ref[...]).astype(BF16)

    return pl.pallas_call(
        body, name="swiglu_fwd", grid=(S // ts, nb),
        in_specs=[pl.BlockSpec((ts, tb), lambda i, j: (i, j)), pl.BlockSpec((ts, tb), lambda i, j: (i, j + nb))],
        out_specs=pl.BlockSpec((ts, tb), lambda i, j: (i, j)),
        out_shape=jax.ShapeDtypeStruct((S, F), BF16),
        compiler_params=_cparams(("parallel", "parallel")),
    )(h, h)


def _ln2_loss(xhat1, ffn, tgt, mod, g1, b1, g2, b2, ts):
    S, D = xhat1.shape

    def body(xh_ref, ffn_ref, t_ref, mod_ref, g1_ref, b1_ref, g2_ref, b2_ref, loss_ref, dffn_ref, dx1_ref, vec_ref):
        i = pl.program_id(0)

        @pl.when(i == 0)
        def _():
            loss_ref[...] = jnp.zeros_like(loss_ref)
            vec_ref[...] = jnp.zeros_like(vec_ref)

        x1 = xh_ref[...] * g1_ref[...] + b1_ref[...]
        ffn = ffn_ref[...]
        r = DEEPNORM_ALPHA * x1 + mod_ref[5:6, :] * ffn
        mu = jnp.mean(r, axis=-1, keepdims=True)
        d = r - mu
        rstd = lax.rsqrt(jnp.mean(d * d, axis=-1, keepdims=True) + LN_EPS)
        xhat = d * rstd
        e = xhat * g2_ref[...] + b2_ref[...] - t_ref[...]
        loss_ref[...] += 0.5 * jnp.sum(jnp.mean(e * e, axis=-1, keepdims=True))
        dy = e * (1.0 / D)
        dxhat = dy * g2_ref[...]
        dr = rstd * (dxhat - jnp.mean(dxhat, axis=-1, keepdims=True)
                     - xhat * jnp.mean(dxhat * xhat, axis=-1, keepdims=True))
        dffn_ref[...] = (dr * mod_ref[5:6, :]).astype(BF16)
        dx1_ref[...] = DEEPNORM_ALPHA * dr
        vec_ref[0:1, :] += jnp.sum(dy * xhat, axis=0, keepdims=True)
        vec_ref[1:2, :] += jnp.sum(dy, axis=0, keepdims=True)
        vec_ref[2:3, :] += jnp.sum(dr * ffn, axis=0, keepdims=True)

    row = pl.BlockSpec((ts, D), lambda i: (i, 0))
    vec = lambda r: pl.BlockSpec((r, D), lambda i: (0, 0))
    return pl.pallas_call(
        body, name="ln2_loss", grid=(S // ts,),
        in_specs=[row, row, row, vec(6), vec(1), vec(1), vec(1), vec(1)],
        out_specs=[pl.BlockSpec((1, LANE), lambda i: (0, 0)), row, row, vec(8)],
        out_shape=[jax.ShapeDtypeStruct((1, LANE), F32), jax.ShapeDtypeStruct((S, D), BF16),
                   jax.ShapeDtypeStruct((S, D), F32), jax.ShapeDtypeStruct((8, D), F32)],
        compiler_params=_cparams(("arbitrary",)),
    )(xhat1, ffn, tgt, mod, g1, b1, g2, b2)


def _swiglu_bwd(da, h, ts, tb):
    S, F2 = h.shape
    nb = (F2 // 2) // tb

    def body(da_ref, hg_ref, hu_ref, dh_ref):
        hg, da = hg_ref[...], da_ref[...]
        sg = _sigmoid(hg)

        @pl.when(pl.program_id(2) == 0)
        def _():
            dh_ref[...] = (da * hu_ref[...] * (sg * (1.0 + hg * (1.0 - sg)))).astype(BF16)

        @pl.when(pl.program_id(2) == 1)
        def _():
            dh_ref[...] = (da * hg * sg).astype(BF16)

    lo = pl.BlockSpec((ts, tb), lambda i, j, k: (i, j))
    hi = pl.BlockSpec((ts, tb), lambda i, j, k: (i, j + nb))
    return pl.pallas_call(
        body, name="swiglu_bwd", grid=(S // ts, nb, 2),
        in_specs=[lo, lo, hi],
        out_specs=pl.BlockSpec((ts, tb), lambda i, j, k: (i, j + nb * k)),
        out_shape=jax.ShapeDtypeStruct((S, F2), BF16),
        compiler_params=_cparams(("parallel", "parallel", "arbitrary")),
    )(da, h, h)


def _ln1_bwd(du2, dx1a, xhat1, rstd1, mix, mod, g1, b1, ts):
    S, D = xhat1.shape

    def body(du2_ref, dx1a_ref, xh_ref, rstd_ref, mix_ref, mod_ref, g_ref, b_ref, dxa_ref, dmix_ref, vec_ref):
        i = pl.program_id(0)

        @pl.when(i == 0)
        def _():
            vec_ref[...] = jnp.zeros_like(vec_ref)

        xhat, du2, mix = xh_ref[...], du2_ref[...], mix_ref[...]
        x1 = xhat * g_ref[...] + b_ref[...]
        dx1 = dx1a_ref[...] + du2 * (1.0 + mod_ref[4:5, :])
        dxhat = dx1 * g_ref[...]
        dr = rstd_ref[...] * (dxhat - jnp.mean(dxhat, axis=-1, keepdims=True)
                              - xhat * jnp.mean(dxhat * xhat, axis=-1, keepdims=True))
        dxa_ref[...] = DEEPNORM_ALPHA * dr
        dmix_ref[...] = (dr * mod_ref[2:3, :]).astype(BF16)
        vec_ref[0:1, :] += jnp.sum(du2, axis=0, keepdims=True)
        vec_ref[1:2, :] += jnp.sum(du2 * x1, axis=0, keepdims=True)
        vec_ref[2:3, :] += jnp.sum(dx1 * xhat, axis=0, keepdims=True)
        vec_ref[3:4, :] += jnp.sum(dx1, axis=0, keepdims=True)
        vec_ref[4:5, :] += jnp.sum(dr * mix, axis=0, keepdims=True)

    row = pl.BlockSpec((ts, D), lambda i: (i, 0))
    vec = lambda r: pl.BlockSpec((r, D), lambda i: (0, 0))
    return pl.pallas_call(
        body, name="ln1_bwd", grid=(S // ts,),
        in_specs=[row, row, row, pl.BlockSpec((ts, 1), lambda i: (i, 0)), row, vec(6), vec(1), vec(1)],
        out_specs=[row, row, vec(8)],
        out_shape=[jax.ShapeDtypeStruct((S, D), F32), jax.ShapeDtypeStruct((S, D), BF16),
                   jax.ShapeDtypeStruct((8, D), F32)],
        compiler_params=_cparams(("arbitrary",)),
    )(du2, dx1a, xhat1, rstd1, mix, mod, g1, b1)


def _merge_bwd(dmerged, proj, ya, yb, blk_ga, blk_gb, ts):
    S, D = ya.shape
    nb = D // COL_BLOCK

    def body(dm_ref, ga_ref, gb_ref, ya_ref, yb_ref, dya_ref, dyb_ref, dga_ref, dgb_ref):
        dm = dm_ref[...]
        sa, sb = _sigmoid(ga_ref[...]), _sigmoid(gb_ref[...])
        dya_ref[...] = (dm * sa).astype(BF16)
        dyb_ref[...] = (dm * sb).astype(BF16)
        dga_ref[...] = (dm * ya_ref[...] * sa * (1.0 - sa)).astype(BF16)
        dgb_ref[...] = (dm * yb_ref[...] * sb * (1.0 - sb)).astype(BF16)

    row = pl.BlockSpec((ts, D), lambda i: (i, 0))
    seg = lambda blk: pl.BlockSpec((pl.Element(ts), pl.Element(D)), lambda i: (i * ts, blk * COL_BLOCK))
    out = jax.ShapeDtypeStruct((S, D), BF16)
    return pl.pallas_call(
        body, name="merge_bwd", grid=(S // ts,),
        in_specs=[row, seg(blk_ga), seg(blk_gb), row, row],
        out_specs=[row] * 4,
        out_shape=[out] * 4,
        compiler_params=_cparams(("parallel",)),
    )(dmerged, proj, proj, ya, yb)


def _conv_bwd(dcbc, proj, w_conv, blk_b, blk_c, blk_x):
    S = proj.shape[0]
    D = w_conv.shape[1]
    nb = D // COL_BLOCK

    def body(d_ref, cb_ref, cc_ref, cx_ref, w_ref, dcb_ref, dcc_ref, dcx_ref, dw_ref):
        d, cc, cx = d_ref[...], cc_ref[...], cx_ref[...]
        z = cc * cx
        z1, z2 = _shift_rows(z, 1), _shift_rows(z, 2)
        conv = w_ref[2:3, :] * z + w_ref[1:2, :] * z1 + w_ref[0:1, :] * z2
        dcb_ref[...] = (d * conv).astype(BF16)
        dconv = d * cb_ref[...]
        dz = w_ref[2:3, :] * dconv + w_ref[1:2, :] * _shift_rows(dconv, -1) + w_ref[0:1, :] * _shift_rows(dconv, -2)
        dcc_ref[...] = (dz * cx).astype(BF16)
        dcx_ref[...] = (dz * cc).astype(BF16)
        dw_ref[...] = jnp.zeros_like(dw_ref)
        dw_ref[0:1, :] = jnp.sum(dconv * z2, axis=0, keepdims=True)
        dw_ref[1:2, :] = jnp.sum(dconv * z1, axis=0, keepdims=True)
        dw_ref[2:3, :] = jnp.sum(dconv * z, axis=0, keepdims=True)

    col = lambda off: pl.BlockSpec((S, COL_BLOCK), lambda j: (0, off + j))
    out = jax.ShapeDtypeStruct((S, D), BF16)
    return pl.pallas_call(
        body, name="conv_bwd", grid=(nb,),
        in_specs=[col(0), col(blk_b), col(blk_c), col(blk_x), pl.BlockSpec((CONV_K, COL_BLOCK), lambda j: (0, j))],
        out_specs=[col(0), col(0), col(0), pl.BlockSpec((8, COL_BLOCK), lambda j: (0, j))],
        out_shape=[out, out, out, jax.ShapeDtypeStruct((8, D), F32)],
        compiler_params=_cparams(("parallel",)),
    )(dcbc, proj, proj, proj, w_conv)


def _attn_bwd(qc, kc, vh, do, o, lse, T):
    H, S, _ = qc.shape
    n = S // T

    def body(q_ref, k_ref, v_ref, do_ref, o_ref, lse_ref, dq_ref, dk_ref, dv_ref, d_ref, dk_acc, dv_acc):
        j = pl.program_id(1)

        @pl.when(j == 0)
        def _():
            dq_ref[...] = jnp.zeros_like(dq_ref)
            d_ref[...] = jnp.sum(do_ref[...] * o_ref[...], axis=-1, keepdims=True)

        dk_acc[...] = jnp.zeros_like(dk_acc)
        dv_acc[...] = jnp.zeros_like(dv_acc)
        k, v = k_ref[0], v_ref[0]

        def step(i, q0, k0, size, masked):
            rows = pl.ds(pl.multiple_of(i * T + q0, size), size)
            keys = slice(k0, k0 + size)
            q = q_ref[0, rows, :]
            do = do_ref[rows, :].astype(BF16)
            s = lax.dot_general(q, k[keys], NT_DIMS, preferred_element_type=F32) * ATTN_SCALE
            if masked:
                s = jnp.where(_diag_mask(size), s, NEG_INF)
            p = jnp.exp(s - lse_ref[0, rows, :])
            dv_acc[keys, :] += lax.dot_general(p.astype(BF16), do, TN_DIMS, preferred_element_type=F32)
            dp = lax.dot_general(do, v[keys], NT_DIMS, preferred_element_type=F32)
            ds = (p * (dp - d_ref[rows, :]) * ATTN_SCALE).astype(BF16)
            dk_acc[keys, :] += lax.dot_general(ds, q, TN_DIMS, preferred_element_type=F32)
            dq_ref[0, rows, :] += jnp.dot(ds, k[keys], preferred_element_type=F32)

        def above(i, carry):
            step(i, 0, 0, T, False)
            return carry

        half = T // 2
        if half % CHUNK == 0:
            step(j, 0, 0, half, True)
            step(j, half, 0, half, False)
            step(j, half, half, half, True)
        else:
            step(j, 0, 0, T, True)
        lax.fori_loop(j + 1, n, above, 0)
        dk_ref[0] = dk_acc[...]
        dv_ref[0] = dv_acc[...]

    head = lambda w: pl.BlockSpec((1, S, w), lambda h, j: (h, 0, 0))
    blk = lambda w: pl.BlockSpec((1, T, w), lambda h, j: (h, j, 0))
    ospec = pl.BlockSpec((S, V_HEAD), lambda h, j: (0, h))
    return pl.pallas_call(
        body, name="attn_bwd", grid=(H, n),
        in_specs=[head(QK_CAT), blk(QK_CAT), blk(V_HEAD), ospec, ospec, head(1)],
        out_specs=[head(QK_CAT), blk(QK_CAT), blk(V_HEAD)],
        out_shape=[jax.ShapeDtypeStruct((H, S, QK_CAT), F32), jax.ShapeDtypeStruct((H, S, QK_CAT), F32),
                   jax.ShapeDtypeStruct((H, S, V_HEAD), F32)],
        scratch_shapes=[pltpu.VMEM((S, 1), F32), pltpu.VMEM((T, QK_CAT), F32), pltpu.VMEM((T, V_HEAD), F32)],
        compiler_params=_cparams(("parallel", "arbitrary")),
    )(qc, kc, vh, do, o, lse)


def _qk_bwd(dqc, dkc, dvh, cos_q, sin_q, cos_k, sin_k, ts):
    H, S, _ = dqc.shape
    pair = 2 * QK_CAT
    kv_w = QK_NOPE + V_HEAD

    def body(dqc_ref, dkc_ref, dvh_ref, cq_ref, sq_ref, ck_ref, sk_ref, dq_ref, dkv_ref, dkr_ref, q_buf, kr_buf):
        for p in range(H // 2):
            q_buf[:, :QK_CAT] = dqc_ref[2 * p]
            q_buf[:, QK_CAT:] = dqc_ref[2 * p + 1]
            g = q_buf[...]
            dq_ref[:, p * pair:(p + 1) * pair] = (
                g * cq_ref[...] - _rope_partner(g, QK_CAT, QK_NOPE) * sq_ref[...]).astype(BF16)
        kr_sum = jnp.zeros((ts, QK_ROPE), F32)
        for h in range(H):
            dkv_ref[:, h * kv_w:h * kv_w + QK_NOPE] = dkc_ref[h, :, 0:QK_NOPE].astype(BF16)
            dkv_ref[:, h * kv_w + QK_NOPE:(h + 1) * kv_w] = dvh_ref[h].astype(BF16)
            kr_sum = kr_sum + dkc_ref[h, :, QK_NOPE:QK_CAT]
        kr_buf[...] = jnp.zeros_like(kr_buf)
        kr_buf[:, 0:QK_ROPE] = kr_sum
        kr = kr_buf[...]
        dkr_ref[...] = (kr * ck_ref[...] - _rope_partner(kr, QK_ROPE, 0) * sk_ref[...]).astype(BF16)

    row = lambda w: pl.BlockSpec((ts, w), lambda i: (i, 0))
    head = lambda w: pl.BlockSpec((H, ts, w), lambda i: (0, i, 0))
    return pl.pallas_call(
        body, name="qk_bwd", grid=(S // ts,),
        in_specs=[head(QK_CAT), head(QK_CAT), head(V_HEAD), row(pair), row(pair), row(COL_BLOCK), row(COL_BLOCK)],
        out_specs=[row(H * QK_CAT), row(H * kv_w), row(COL_BLOCK)],
        out_shape=[jax.ShapeDtypeStruct((S, H * QK_CAT), BF16), jax.ShapeDtypeStruct((S, H * kv_w), BF16),
                   jax.ShapeDtypeStruct((S, COL_BLOCK), BF16)],
        scratch_shapes=[pltpu.VMEM((ts, pair), F32), pltpu.VMEM((ts, COL_BLOCK), F32)],
        compiler_params=_cparams(("parallel",)),
    )(dqc, dkc, dvh, cos_q, sin_q, cos_k, sin_k)


def _rms_bwd(dy, proj, g, blk, L, ts, name):
    S = proj.shape[0]

    def body(dy_ref, a_ref, g_ref, da_ref, dg_ref):
        i = pl.program_id(0)

        @pl.when(i == 0)
        def _():
            dg_ref[...] = jnp.zeros_like(dg_ref)

        a, dy = a_ref[...], dy_ref[...]
        r = lax.rsqrt(jnp.mean(a * a, axis=-1, keepdims=True) + RMS_EPS)
        dyh = dy * g_ref[...]
        da = r * dyh - a * (r * r * r) * jnp.mean(dyh * a, axis=-1, keepdims=True)
        da_ref[...] = da.astype(BF16)
        dg_ref[0:1, :] += jnp.sum(dy * a * r, axis=0, keepdims=True)

    return pl.pallas_call(
        body, name=name, grid=(S // ts,),
        in_specs=[pl.BlockSpec((ts, L), lambda i: (i, 0)), pl.BlockSpec((ts, L), lambda i: (i, blk)),
                  pl.BlockSpec((1, L), lambda i: (0, 0))],
        out_specs=[pl.BlockSpec((ts, L), lambda i: (i, 0)), pl.BlockSpec((8, L), lambda i: (0, 0))],
        out_shape=[jax.ShapeDtypeStruct((S, L), BF16), jax.ShapeDtypeStruct((8, L), F32)],
        compiler_params=_cparams(("arbitrary",)),
    )(dy, proj, g)


def _grad_x(du, dxa, x, mod, ts):
    S, D = x.shape

    def body(du_ref, dxa_ref, x_ref, mod_ref, dx_ref, vec_ref):
        i = pl.program_id(0)

        @pl.when(i == 0)
        def _():
            vec_ref[...] = jnp.zeros_like(vec_ref)

        du = du_ref[...]
        dx_ref[...] = dxa_ref[...] + du * (1.0 + mod_ref[1:2, :])
        vec_ref[0:1, :] += jnp.sum(du, axis=0, keepdims=True)
        vec_ref[1:2, :] += jnp.sum(du * x_ref[...], axis=0, keepdims=True)

    row = pl.BlockSpec((ts, D), lambda i: (i, 0))
    vec = lambda r: pl.BlockSpec((r, D), lambda i: (0, 0))
    return pl.pallas_call(
        body, name="grad_x", grid=(S // ts,),
        in_specs=[row, row, row, vec(6)],
        out_specs=[row, vec(8)],
        out_shape=[jax.ShapeDtypeStruct((S, D), F32), jax.ShapeDtypeStruct((8, D), F32)],
        compiler_params=_cparams(("arbitrary",)),
    )(du, dxa, x, mod)


def _adamw(w, g, m, v, name):
    R, C = w.shape
    tr = _tile(R, max(8, (1 << 19) // C), 8)
    c1 = 1.0 / (1.0 - ADAM_B1 ** ADAM_STEP)
    c2 = 1.0 / (1.0 - ADAM_B2 ** ADAM_STEP)

    def body(w_ref, g_ref, m_ref, v_ref, d_ref, nm_ref, nv_ref):
        g = g_ref[...]
        m = ADAM_B1 * m_ref[...] + (1.0 - ADAM_B1) * g
        v = ADAM_B2 * v_ref[...] + (1.0 - ADAM_B2) * (g * g)
        nm_ref[...] = m
        nv_ref[...] = v
        d_ref[...] = -ADAM_LR * ((m * c1) / (jnp.sqrt(v * c2) + ADAM_EPS) + ADAM_WD * w_ref[...])

    spec = pl.BlockSpec((tr, C), lambda i: (i, 0))
    out = jax.ShapeDtypeStruct((R, C), F32)
    return pl.pallas_call(
        body, name=name, grid=(R // tr,),
        in_specs=[spec] * 4, out_specs=[spec] * 3, out_shape=[out] * 3,
        compiler_params=_cparams(("parallel",)),
    )(w, g, m, v)


def _adamw_ada(w, cact_t, dmod, m, v):
    R, C = w.shape
    tr = _tile(R, max(8, (1 << 18) // C), 8)
    c1 = 1.0 / (1.0 - ADAM_B1 ** ADAM_STEP)
    c2 = 1.0 / (1.0 - ADAM_B2 ** ADAM_STEP)

    def body(w_ref, ct_ref, dm_ref, m_ref, v_ref, g_ref, d_ref, nm_ref, nv_ref):
        ct = ct_ref[...].astype(BF16).astype(F32)
        dm = dm_ref[...].astype(BF16).astype(F32)
        g = ct[:, 0:1] * dm[0:1, :]
        for b in range(1, N_DEV):
            g = g + ct[:, b:b + 1] * dm[b:b + 1, :]
        m = ADAM_B1 * m_ref[...] + (1.0 - ADAM_B1) * g
        v = ADAM_B2 * v_ref[...] + (1.0 - ADAM_B2) * (g * g)
        g_ref[...] = g
        nm_ref[...] = m
        nv_ref[...] = v
        d_ref[...] = -ADAM_LR * ((m * c1) / (jnp.sqrt(v * c2) + ADAM_EPS) + ADAM_WD * w_ref[...])

    spec = pl.BlockSpec((tr, C), lambda i: (i, 0))
    out = jax.ShapeDtypeStruct((R, C), F32)
    return pl.pallas_call(
        body, name="adamw_w_ada", grid=(R // tr,),
        in_specs=[spec, pl.BlockSpec((tr, N_DEV), lambda i: (i, 0)), pl.BlockSpec((N_DEV, C), lambda i: (0, 0)),
                  spec, spec],
        out_specs=[spec] * 4, out_shape=[out] * 4,
        compiler_params=_cparams(("parallel",)),
    )(w, cact_t, dmod, m, v)


def _adamw_reduced(w, own, got, m, v, my_chip, name):
    R, C = w.shape
    tr = _tile(R, max(PACK_ROW_ALIGN, (1 << 18) // C), PACK_ROW_ALIGN)
    c1 = 1.0 / (1.0 - ADAM_B1 ** ADAM_STEP)
    c2 = 1.0 / (1.0 - ADAM_B2 ** ADAM_STEP)

    def body(chip_ref, w_ref, own_ref, g1_ref, g2_ref, g3_ref, m_ref, v_ref, g_ref, d_ref, nm_ref, nv_ref):
        g = own_ref[0].astype(F32) + g1_ref[0].astype(F32) + g2_ref[0].astype(F32) + g3_ref[0].astype(F32)
        m = ADAM_B1 * m_ref[...] + (1.0 - ADAM_B1) * g
        v = ADAM_B2 * v_ref[...] + (1.0 - ADAM_B2) * (g * g)
        g_ref[...] = g
        nm_ref[...] = m
        nv_ref[...] = v
        d_ref[...] = -ADAM_LR * ((m * c1) / (jnp.sqrt(v * c2) + ADAM_EPS) + ADAM_WD * w_ref[...])

    spec = pl.BlockSpec((tr, C), lambda i, chip: (i, 0))
    slot = lambda k: pl.BlockSpec((1, tr, C), lambda i, chip: (chip[0] ^ k, i, 0))
    out = jax.ShapeDtypeStruct((R, C), F32)
    return pl.pallas_call(
        body, name=name,
        grid_spec=pltpu.PrefetchScalarGridSpec(
            num_scalar_prefetch=1, grid=(R // tr,),
            in_specs=[spec, slot(0), slot(1), slot(2), slot(3), spec, spec],
            out_specs=[spec] * 4),
        out_shape=[out] * 4,
        compiler_params=_cparams(("parallel",)),
    )(my_chip, w, own, got, got, got, m, v)


def _my_place():
    return lax.axis_index("x"), lax.axis_index("y"), lax.axis_index("c")


def _peer(k):
    x, y, c = _my_place()
    return (x ^ ((k >> 2) & 1), y ^ ((k >> 1) & 1), c ^ (k & 1))


def _linear(place):
    return 4 * place[0] + 2 * place[1] + place[2]


def _ada_fwd(c_row, wconv_row, w_ada, b_row):
    D, CW = w_ada.shape
    WC = wconv_row.shape[-1]

    def body(c_ref, wc_ref, w_ref, b_ref, mod_ref, cact_ref, wcall_ref, send_buf, sems):
        me = _linear(_my_place())
        c = c_ref[0]
        cact_ref[me] = c * _sigmoid(c)
        wcall_ref[me] = wc_ref[0]

        def gather_copy(buf, k, grp):
            return pltpu.make_async_remote_copy(
                src_ref=buf.at[me], dst_ref=buf.at[me], send_sem=sems.at[0, grp, k], recv_sem=sems.at[1, grp, k],
                device_id=_peer(k), device_id_type=MESH_ID)

        def gather_recv(buf, k, grp):
            src = _linear(_peer(k))
            return pltpu.make_async_remote_copy(
                src_ref=buf.at[src], dst_ref=buf.at[src], send_sem=sems.at[0, grp, k], recv_sem=sems.at[1, grp, k],
                device_id=_peer(k), device_id_type=MESH_ID)

        for k in range(1, N_DEV):
            gather_copy(cact_ref, k, 0).start()
            gather_copy(wcall_ref, k, 1).start()
        for k in range(1, N_DEV):
            gather_recv(cact_ref, k, 0).wait_recv()
            gather_recv(wcall_ref, k, 1).wait_recv()
        for k in range(1, N_DEV):
            gather_copy(cact_ref, k, 0).wait_send()
            gather_copy(wcall_ref, k, 1).wait_send()

        cact = jnp.concatenate([cact_ref[b] for b in range(N_DEV)], axis=0)
        mod_all = jnp.dot(cact.astype(BF16), w_ref[...].astype(BF16), preferred_element_type=F32) + b_ref[0]
        for b in range(N_DEV):
            send_buf[b] = mod_all[b:b + 1, :]
        mod_ref[me] = send_buf[me]

        def scatter_copy(k):
            dst = _linear(_peer(k))
            return pltpu.make_async_remote_copy(
                src_ref=send_buf.at[dst], dst_ref=mod_ref.at[me], send_sem=sems.at[0, 2, k], recv_sem=sems.at[1, 2, k],
                device_id=_peer(k), device_id_type=MESH_ID)

        def scatter_recv(k):
            src = _linear(_peer(k))
            return pltpu.make_async_remote_copy(
                src_ref=send_buf.at[src], dst_ref=mod_ref.at[src], send_sem=sems.at[0, 2, k], recv_sem=sems.at[1, 2, k],
                device_id=_peer(k), device_id_type=MESH_ID)

        for k in range(1, N_DEV):
            scatter_copy(k).start()
        for k in range(1, N_DEV):
            scatter_recv(k).wait_recv()
        for k in range(1, N_DEV):
            scatter_copy(k).wait_send()

    vmem = pl.BlockSpec(memory_space=pltpu.VMEM)
    return pl.pallas_call(
        body, name="ada_fwd",
        in_specs=[vmem] * 4, out_specs=[vmem] * 3,
        out_shape=[jax.ShapeDtypeStruct((N_DEV, 1, CW), F32), jax.ShapeDtypeStruct((N_DEV, 1, D), F32),
                   jax.ShapeDtypeStruct((N_DEV, 1, WC), F32)],
        scratch_shapes=[pltpu.VMEM((N_DEV, 1, CW), F32), pltpu.SemaphoreType.DMA((2, 3, N_DEV))],
        compiler_params=pltpu.CompilerParams(vmem_limit_bytes=VMEM_LIMIT),
    )(c_row, wconv_row, w_ada, b_row)


def _ada_bwd(payload, deps=()):
    NCH, _, CW = payload.shape

    def body(p_ref, *rest):
        sum_ref, mine_ref, all_ref, sems = rest[-4:]
        me = _linear(_my_place())
        all_ref[me] = p_ref[...]

        def copy(k, slot):
            return pltpu.make_async_remote_copy(
                src_ref=all_ref.at[slot], dst_ref=all_ref.at[slot], send_sem=sems.at[0, k], recv_sem=sems.at[1, k],
                device_id=_peer(k), device_id_type=MESH_ID)

        for k in range(1, N_DEV):
            copy(k, me).start()
        for k in range(1, N_DEV):
            copy(k, _linear(_peer(k))).wait_recv()
        for k in range(1, N_DEV):
            copy(k, me).wait_send()

        total = all_ref[0]
        for b in range(1, N_DEV):
            total = total + all_ref[b]
        sum_ref[...] = total

        for b in range(N_DEV):
            mine_ref[b] = all_ref[b, me]

    vmem = pl.BlockSpec(memory_space=pltpu.VMEM)
    return pl.pallas_call(
        body, name="ada_bwd",
        in_specs=[vmem] + [ANY_SPEC] * len(deps), out_specs=[vmem, vmem],
        out_shape=[jax.ShapeDtypeStruct((NCH, 1, CW), F32), jax.ShapeDtypeStruct((N_DEV, 1, CW), F32)],
        scratch_shapes=[pltpu.VMEM((N_DEV, NCH, 1, CW), F32), pltpu.SemaphoreType.DMA((2, N_DEV))],
        compiler_params=pltpu.CompilerParams(vmem_limit_bytes=VMEM_LIMIT),
    )(payload, *deps)


def _exchange_in_chip(parts):
    W = len(parts)

    def body(*refs):
        p_refs, got_refs, (send_sems, recv_sems) = refs[:W], refs[W:2 * W], refs[2 * W:]
        x, y, c = _my_place()
        sibling = (x, y, 1 - c)
        copies = []
        for w in range(W):
            for q in range(4):
                copies.append(pltpu.make_async_remote_copy(
                    src_ref=p_refs[w].at[2 * q + (1 - c)], dst_ref=got_refs[w].at[q],
                    send_sem=send_sems.at[4 * w + q], recv_sem=recv_sems.at[4 * w + q],
                    device_id=sibling, device_id_type=MESH_ID))
        for cp in copies:
            cp.start()
        for cp in copies:
            cp.wait_recv()
        for cp in copies:
            cp.wait_send()

    return pl.pallas_call(
        body, name="grad_exchange_in_chip",
        in_specs=[HBM_SPEC] * W, out_specs=[HBM_SPEC] * W,
        out_shape=[jax.ShapeDtypeStruct((4,) + p.shape[1:], p.dtype) for p in parts],
        scratch_shapes=[pltpu.SemaphoreType.DMA((4 * W,)), pltpu.SemaphoreType.DMA((4 * W,))],
    )(*parts)


def _pair_sum(parts, got, core):
    _, R, C = parts.shape
    tr = _tile(R, max(PACK_ROW_ALIGN, PAIR_SUM_BLOCK // C), PACK_ROW_ALIGN)

    def body(c_ref, p_ref, g_ref, o_ref):
        o_ref[...] = (p_ref[...].astype(F32) + g_ref[...].astype(F32)).astype(o_ref.dtype)

    return pl.pallas_call(
        body, name="grad_pair_sum",
        grid_spec=pltpu.PrefetchScalarGridSpec(
            num_scalar_prefetch=1, grid=(4, R // tr),
            in_specs=[pl.BlockSpec((1, tr, C), lambda q, i, c_ref: (2 * q + c_ref[0], i, 0)),
                      pl.BlockSpec((1, tr, C), lambda q, i, c_ref: (q, i, 0))],
            out_specs=pl.BlockSpec((1, tr, C), lambda q, i, c_ref: (q, i, 0))),
        out_shape=jax.ShapeDtypeStruct((4, R, C), parts.dtype),
        compiler_params=_cparams(("parallel", "parallel")),
    )(core, parts, got)


HBM_SPEC = pl.BlockSpec(memory_space=pltpu.HBM)
SEM_SPEC = pl.BlockSpec(memory_space=pltpu.SEMAPHORE)
ANY_SPEC = pl.BlockSpec(memory_space=pl.ANY)
SPLIT_EFFECT = pltpu.SideEffectType.DATAFLOW_SIDE_EFFECTING


def _landing_zone(shape, dtype):
    return pltpu.with_memory_space_constraint(lax.empty(shape, dtype), pltpu.HBM)


def _split_start(name, arrays, lands, after, copies_of, per_array):
    W = len(arrays)
    after = tuple(after) if isinstance(after, (tuple, list)) else (after,)

    def body(*refs):
        x_refs, land_refs = refs[:W], refs[W:2 * W]
        send_sems, recv_sems = refs[2 * W + len(after)], refs[2 * W + len(after) + 1]
        token = refs[-1]
        k = 0
        for w in range(W):
            for src, dst, dev in copies_of(w, x_refs[w], land_refs[w]):
                pltpu.make_async_remote_copy(src_ref=src, dst_ref=dst, send_sem=send_sems.at[k], recv_sem=recv_sems.at[k],
                                             device_id=dev, device_id_type=MESH_ID).start()
                k += 1
        token[...] = jnp.zeros_like(token)

    n_copies = per_array * W
    hbm_of = lambda xs: tuple(pltpu.HBM(a.shape, a.dtype) for a in xs)
    out = pl.pallas_call(
        body, name=name,
        out_shape=(pltpu.SemaphoreType.DMA((n_copies,)), pltpu.SemaphoreType.DMA((n_copies,)))
        + hbm_of(arrays) + hbm_of(lands) + (jax.ShapeDtypeStruct((8, LANE), F32),),
        in_specs=(HBM_SPEC,) * (2 * W) + (ANY_SPEC,) * len(after),
        out_specs=(SEM_SPEC, SEM_SPEC) + (HBM_SPEC,) * (2 * W) + (pl.BlockSpec(memory_space=pltpu.VMEM),),
        input_output_aliases={i: 2 + i for i in range(2 * W)},
        compiler_params=pltpu.CompilerParams(has_side_effects=SPLIT_EFFECT),
    )(*[pltpu.with_memory_space_constraint(a, pltpu.HBM) for a in arrays], *lands, *after)
    return out[0], out[1], list(out[2:2 + W]), list(out[2 + W:2 + 2 * W]), out[-1]


def _split_wait(name, state, after, copies_of):
    send_sems, recv_sems, arrays, lands, _ = state
    W = len(arrays)
    after = tuple(after) if isinstance(after, (tuple, list)) else (after,)

    def body(*refs):
        x_refs, land_refs = refs[:W], refs[W:2 * W]
        send_sems, recv_sems = refs[2 * W], refs[2 * W + 1]
        k = 0
        for w in range(W):
            for src, dst, dev in copies_of(w, x_refs[w], land_refs[w]):
                cp = pltpu.make_async_remote_copy(src_ref=src, dst_ref=dst, send_sem=send_sems.at[k],
                                                  recv_sem=recv_sems.at[k], device_id=dev, device_id_type=MESH_ID)
                cp.wait_send()
                cp.wait_recv()
                k += 1

    out = pl.pallas_call(
        body, name=name,
        out_shape=tuple(pltpu.HBM(a.shape, a.dtype) for a in arrays + lands),
        in_specs=(HBM_SPEC,) * (2 * W) + (SEM_SPEC, SEM_SPEC) + (ANY_SPEC,) * len(after),
        out_specs=(HBM_SPEC,) * (2 * W),
        input_output_aliases={i: i for i in range(2 * W)},
        compiler_params=pltpu.CompilerParams(has_side_effects=SPLIT_EFFECT),
    )(*arrays, *lands, send_sems, recv_sems, *after)
    return list(out[:W]), list(out[W:])


def _scatter_copies(w, p_ref, land_ref):
    x, y, c = _my_place()
    my_chip = 2 * x + y
    return [(p_ref.at[2 * (x ^ (k >> 1)) + (y ^ (k & 1))], land_ref.at[my_chip], (x ^ (k >> 1), y ^ (k & 1), c))
            for k in range(1, 4)]


def _gather_copies(w, x_ref, land_ref):
    x, y, c = _my_place()
    me = _linear((x, y, c))
    devs = [(x, y, 1 - c)] + [(x ^ (k >> 1), y ^ (k & 1), c) for k in range(1, 4)]
    return [(x_ref, land_ref.at[me], d) for d in devs]


def _gather_forward(lands, name):
    W = len(lands)

    def body(*refs):
        land_refs, out_refs, (send_sems, recv_sems) = refs[:W], refs[W:2 * W], refs[2 * W:]
        x, y, c = _my_place()
        sibling = (x, y, 1 - c)
        sends, arrivals = [], []
        for w in range(W):
            for k in range(1, 4):
                px, py = x ^ (k >> 1), y ^ (k & 1)
                landed, theirs = _linear((px, py, c)), out_refs[w].at[_linear((px, py, 1 - c))]
                sem = 3 * w + k - 1
                sends.append(pltpu.make_async_remote_copy(
                    src_ref=land_refs[w].at[landed], dst_ref=out_refs[w].at[landed],
                    send_sem=send_sems.at[sem], recv_sem=recv_sems.at[sem], device_id=sibling, device_id_type=MESH_ID))
                arrivals.append(pltpu.make_async_remote_copy(
                    src_ref=theirs, dst_ref=theirs, send_sem=send_sems.at[sem], recv_sem=recv_sems.at[sem],
                    device_id=sibling, device_id_type=MESH_ID))
        for cp in sends:
            cp.start()
        for cp in arrivals:
            cp.wait_recv()
        for cp in sends:
            cp.wait_send()

    return pl.pallas_call(
        body, name=name,
        in_specs=[HBM_SPEC] * W, out_specs=[HBM_SPEC] * W,
        out_shape=[jax.ShapeDtypeStruct(l.shape, l.dtype) for l in lands],
        input_output_aliases={i: i for i in range(W)},
        scratch_shapes=[pltpu.SemaphoreType.DMA((3 * W,)), pltpu.SemaphoreType.DMA((3 * W,))],
    )(*lands)


def _with_own_slot(gathered, shard):
    return lax.dynamic_update_index_in_dim(gathered, shard[None], _linear(_my_place()), axis=0)


def _in_chip_copies(w, p_ref, land_ref):
    x, y, c = _my_place()
    return [(p_ref.at[2 * q + (1 - c)], land_ref.at[q], (x, y, 1 - c)) for q in range(4)]


def _in_chip_start(parts, tag):
    lands = [_landing_zone((4,) + p.shape[1:], p.dtype) for p in parts]
    return _split_start("grad_in_chip_start_" + tag, parts, lands, (), _in_chip_copies, 4)


def _reduce_scatter_begin(parts, tag, in_chip_state=None, after=()):
    if in_chip_state is None:
        got = _exchange_in_chip(parts)
    else:
        parts, got = _split_wait("grad_in_chip_wait_" + tag, in_chip_state, after, _in_chip_copies)
    core = lax.axis_index("c").astype(jnp.int32).reshape(1)
    chip_parts = [_pair_sum(p, g, core) for p, g in zip(parts, got)]
    lands = [_landing_zone(p.shape, p.dtype) for p in chip_parts]
    return _split_start("grad_scatter_start_" + tag, chip_parts, lands, got[0], _scatter_copies, 3)


def _reduce_scatter_end(state, after, tag):
    return _split_wait("grad_scatter_wait_" + tag, state, after, _scatter_copies)


def kernel(x, c, positions, w_ada, b_ada, w_in, g_q_a, w_q_b, g_kv_a, w_kv_b, w_o_a, w_conv, w_o_b, w_o, ln1_g, ln1_b, w_ffn_in, w_ffn_out, ln2_g, ln2_b, loss_target, m_w_ada, m_b_ada, m_w_in, m_g_q_a, m_w_q_b, m_g_kv_a, m_w_kv_b, m_w_o_a, m_w_conv, m_w_o_b, m_w_o, m_ln1_g, m_ln1_b, m_w_ffn_in, m_w_ffn_out, m_ln2_g, m_ln2_b, v_w_ada, v_b_ada, v_w_in, v_g_q_a, v_w_q_b, v_g_kv_a, v_w_kv_b, v_w_o_a, v_w_conv, v_w_o_b, v_w_o, v_ln1_g, v_ln1_b, v_w_ffn_in, v_w_ffn_out, v_ln2_g, v_ln2_b):
    x2, tgt = x[0], loss_target[0]
    S, D = x2.shape
    Lq, Lkv = g_q_a.shape[1], g_kv_a.shape[1]
    H = w_q_b.shape[2] * N_DEV // QK_CAT
    F = w_ffn_out.shape[1] * N_DEV
    assert Lq == Lkv and (Lq + Lkv) % COL_BLOCK == 0 and D % COL_BLOCK == 0
    front = Lq + Lkv + QK_ROPE
    front_pad = _round_up(front, COL_BLOCK)
    kr_blk = (Lq + Lkv) // COL_BLOCK
    blk_b = front_pad // COL_BLOCK
    nblk = D // COL_BLOCK
    blk_c, blk_x, blk_ga, blk_gb = blk_b + nblk, blk_b + 2 * nblk, blk_b + 3 * nblk, blk_b + 4 * nblk
    ts = _tile(S, 256, 8)
    T = _tile(S, min(512, S // 2), CHUNK)
    tb = _tile(F, 2816)
    me = _linear(_my_place())

    cw = w_ada.shape[2]
    b_mine = lax.dynamic_slice(b_ada, (0, me * cw), (1, cw)).reshape(1, 1, cw)
    mod_blocks, cact_all, wconv_all = _ada_fwd(c.reshape(1, 1, D), w_conv[0].reshape(1, 1, -1), w_ada[0], b_mine)
    mod = mod_blocks.reshape(6, D)
    cact_all = cact_all.reshape(N_DEV, D)
    w_conv_full = wconv_all.reshape(N_DEV, CONV_K, -1).transpose(1, 0, 2).reshape(CONV_K, D)

    landing = lambda shards: [_landing_zone((N_DEV,) + s.shape, BF16) for s in shards]
    gathered = lambda lands, shards, tag: [_with_own_slot(g, s) for g, s in
                                           zip(_gather_forward(lands, tag + "_gather_forward"), shards)]
    half = D // 2
    w_in_b = w_in[0].astype(BF16)
    first, second = [w_in_b[:half]], [w_in_b[half:], w_q_b[0].astype(BF16), w_kv_b[0].astype(BF16)]
    mid = [w[0].astype(BF16) for w in (w_o_a, w_o_b, w_o)]
    last = [w[0].astype(BF16) for w in (w_ffn_in, w_ffn_out)]
    first_state = _split_start("first_gather_start", first, landing(first), mod_blocks, _gather_copies, 4)
    second_state = _split_start("second_gather_start", second, landing(second), first_state[4], _gather_copies, 4)
    u = _modulate_in(x2, mod, ts)

    first_shards, first_lands = _split_wait("first_gather_wait", first_state, (u, second_state[4]), _gather_copies)
    (g_in_top,) = gathered(first_lands, first_shards, "first")
    w_in_top = _assemble_w_in(g_in_top, front, front_pad, D, 0)
    proj_top = _matmul(u, w_in_top, "nn", F32, "proj_top", k_rows=(0, half))
    second_shards, second_lands = _split_wait("second_gather_wait", second_state, (proj_top,), _gather_copies)
    g_in_bottom, wq_s, wkv_s = gathered(second_lands, second_shards, "second")
    mid_state = _split_start("mid_gather_start", mid, landing(mid), g_in_bottom, _gather_copies, 4)
    last_state = _split_start("last_gather_start", last, landing(last), mid_state[4], _gather_copies, 4)
    w_in_p = _assemble_w_in(g_in_bottom, front, front_pad, D, half, into=w_in_top)

    inv_freq = 1.0 / (ROPE_THETA ** (jnp.arange(0, QK_ROPE, 2, dtype=F32) / QK_ROPE))
    ang = positions[0].astype(F32)[:, None] * inv_freq
    cos2 = jnp.concatenate([jnp.cos(ang), jnp.cos(ang)], axis=-1)
    sin2 = jnp.concatenate([jnp.sin(ang), jnp.sin(ang)], axis=-1)
    one, zero = jnp.ones((S, QK_NOPE), F32), jnp.zeros((S, QK_NOPE), F32)
    cos_q, sin_q = jnp.concatenate([one, cos2, one, cos2], axis=-1), jnp.concatenate([zero, sin2, zero, sin2], axis=-1)
    cos_k, sin_k = jnp.tile(cos2, (1, COL_BLOCK // QK_ROPE)), jnp.tile(sin2, (1, COL_BLOCK // QK_ROPE))

    proj = _matmul(u, w_in_p, "nn", F32, "proj", k_rows=(half, half), init=proj_top, deps=(last_state[4],))
    qn = _rms_fwd(proj, g_q_a, 0, Lq, ts, "rms_q")
    kvn = _rms_fwd(proj, g_kv_a, 1, Lkv, ts, "rms_kv")
    q = _matmul(qn, wq_s, "nn", F32, "q_up")
    kv = _matmul(kvn, wkv_s, "nn", F32, "kv_up")
    qc, kc, vh = _qk_prep(q, kv, proj, kr_blk, cos_q, sin_q, cos_k, sin_k, H, ts)
    attn, lse = _attn_fwd(qc, kc, vh, _tile(S, min(256, S // 2), CHUNK))
    mid_shards, mid_lands = _split_wait("mid_gather_wait", mid_state, lse, _gather_copies)
    w_oa_f, w_ob_f, w_o_f = [g.reshape(-1, D) for g in gathered(mid_lands, mid_shards, "mid")]
    ya = _matmul(attn, w_oa_f, "nn", F32, "attn_out")
    cbc = _conv_fwd(proj, w_conv_full, blk_b, blk_c, blk_x)
    yb = _matmul(cbc, w_ob_f, "nn", F32, "conv_out")
    merged = _merge_fwd(proj, ya, yb, blk_ga, blk_gb, ts)
    mix = _matmul(merged, w_o_f, "nn", F32, "mix_out")
    xhat1, rstd1, u2 = _ln1_fwd(x2, mix, mod, ln1_g, ln1_b, ts)
    last_shards, last_lands = _split_wait("last_gather_wait", last_state, u2, _gather_copies)
    w_fi_s, g_fo = gathered(last_lands, last_shards, "last")
    w_fo_f = g_fo.reshape(F, D)
    hh = _matmul(u2, w_fi_s, "nn", F32, "ffn_in")
    act = _swiglu_fwd(hh, ts, tb)
    ffn = _matmul(act, w_fo_f, "nn", F32, "ffn_out")
    loss_part, dffn, dx1a, vec2 = _ln2_loss(xhat1, ffn, tgt, mod, ln1_g, ln1_b, ln2_g, ln2_b, ts)
    loss = lax.psum(loss_part[0, 0], AXES)

    gw_fo = _matmul(act, dffn, "tn", BF16, "grad_w_ffn_out")
    da = _matmul(dffn, w_fo_f, "nt", F32, "d_act")
    dh = _swiglu_bwd(da, hh, ts, tb)
    gw_fi = _matmul(u2, dh, "tn", BF16, "grad_w_ffn_in", out_shards=True)
    ffn_in_chip = _in_chip_start([gw_fi, gw_fo.reshape(N_DEV, -1, D)], "ffn")
    du2 = _matmul(dh, w_fi_s, "nt", F32, "d_u2", deps=(ffn_in_chip[4],))
    ffn_state = _reduce_scatter_begin(None, "ffn", ffn_in_chip, after=(du2,))
    dxa, dmix, vec1 = _ln1_bwd(du2, dx1a, xhat1, rstd1, mix, mod, ln1_g, ln1_b, ts)
    gw_o = _matmul(merged, dmix, "tn", BF16, "grad_w_o", deps=(ffn_state[4],))
    dmerged = _matmul(dmix, w_o_f, "nt", F32, "d_merged")
    dya, dyb, dga, dgb = _merge_bwd(dmerged, proj, ya, yb, blk_ga, blk_gb, ts)
    gw_ob = _matmul(cbc, dyb, "tn", BF16, "grad_w_o_b")
    dcbc = _matmul(dyb, w_ob_f, "nt", F32, "d_conv")
    dcb, dcc, dcx, dwconv = _conv_bwd(dcbc, proj, w_conv_full, blk_b, blk_c, blk_x)
    gw_oa = _matmul(attn, dya, "tn", BF16, "grad_w_o_a")
    mix_in_chip = _in_chip_start([g.reshape(N_DEV, -1, D) for g in (gw_oa, gw_ob, gw_o)], "mix")
    dattn = _matmul(dya, w_oa_f, "nt", F32, "d_attn", deps=(mix_in_chip[4],))
    dqc, dkc, dvh = _attn_bwd(qc, kc, vh, dattn, attn, lse, T)
    ffn_own, ffn_got = _reduce_scatter_end(ffn_state, dqc, "ffn")
    mix_state = _reduce_scatter_begin(None, "mix", mix_in_chip, after=(dqc,))
    dq, dkv, dkr = _qk_bwd(dqc, dkc, dvh, cos_q, sin_q, cos_k, sin_k, ts)
    gw_qb = _matmul(qn, dq, "tn", BF16, "grad_w_q_b", out_shards=True, deps=(mix_state[4],))
    dqn = _matmul(dq, wq_s, "nt", F32, "d_qn")
    gw_kvb = _matmul(kvn, dkv, "tn", BF16, "grad_w_kv_b", out_shards=True)
    dkvn = _matmul(dkv, wkv_s, "nt", F32, "d_kvn")
    dqa, dgq = _rms_bwd(dqn, proj, g_q_a, 0, Lq, ts, "rms_q_bwd")
    dkva, dgkv = _rms_bwd(dkvn, proj, g_kv_a, 1, Lkv, ts, "rms_kv_bwd")
    dproj = jnp.concatenate([dqa, dkva, dkr, dcb, dcc, dcx, dga, dgb], axis=1)
    gw_in_p = _matmul(u, dproj, "tn", BF16, "grad_w_in")
    mix_own, mix_got = _reduce_scatter_end(mix_state, gw_in_p, "mix")
    in_state = _reduce_scatter_begin([_split_w_in(gw_in_p, front, front_pad), gw_qb, gw_kvb], "in")
    du = _matmul(dproj, w_in_p, "nt", F32, "d_u", deps=(in_state[4],))
    grad_x, vec0 = _grad_x(du, dxa, x2, mod, ts)

    my_chip = (2 * lax.axis_index("x") + lax.axis_index("y")).astype(jnp.int32).reshape(1)
    arrived = {}
    for nm, w, m, v, own, got in (
            ("w_ffn_in", w_ffn_in, m_w_ffn_in, v_w_ffn_in, ffn_own[0], ffn_got[0]),
            ("w_ffn_out", w_ffn_out, m_w_ffn_out, v_w_ffn_out, ffn_own[1], ffn_got[1]),
            ("w_o_a", w_o_a, m_w_o_a, v_w_o_a, mix_own[0], mix_got[0]),
            ("w_o_b", w_o_b, m_w_o_b, v_w_o_b, mix_own[1], mix_got[1]),
            ("w_o", w_o, m_w_o, v_w_o, mix_own[2], mix_got[2])):
        arrived[nm] = [a[None] for a in _adamw_reduced(w[0], own, got, m[0], v[0], my_chip, "adamw_" + nm)]

    dmod = jnp.concatenate([vec0[0], vec0[1], vec1[4], vec1[0], vec1[1], vec2[2]])
    small = jnp.concatenate([dmod, dgq[0], dgkv[0], vec1[2], vec1[3], vec2[0], vec2[1], dwconv[:CONV_K].reshape(-1)])
    n_small = small.shape[0]
    nch = _round_up(n_small, cw) // cw
    payload = jnp.pad(small, (0, nch * cw - n_small)).reshape(nch, 1, cw)
    summed, dmod_mine = _ada_bwd(payload, deps=[res[1] for res in arrived.values()])
    arrived["w_ada"] = [a[None] for a in _adamw_ada(w_ada[0], cact_all.T, dmod_mine.reshape(N_DEV, cw),
                                                    m_w_ada[0], v_w_ada[0])]
    summed = summed.reshape(-1)
    offs = [0, 6 * D, 6 * D + Lq, 6 * D + Lq + Lkv]
    offs += [offs[-1] + D * k for k in range(1, 5)]
    g_b_ada = summed[offs[0]:offs[1]].reshape(1, -1)
    g_gq = summed[offs[1]:offs[2]].reshape(1, -1)
    g_gkv = summed[offs[2]:offs[3]].reshape(1, -1)
    g_ln1g, g_ln1b, g_ln2g, g_ln2b = [summed[offs[3 + k]:offs[4 + k]].reshape(1, -1) for k in range(4)]
    wc = w_conv.shape[2]
    g_wconv = lax.dynamic_slice(summed[offs[7]:offs[7] + CONV_K * D].reshape(CONV_K, D), (0, me * wc), (CONV_K, wc))

    names = ["w_ada", "b_ada", "w_in", "g_q_a", "w_q_b", "g_kv_a", "w_kv_b", "w_o_a", "w_conv", "w_o_b", "w_o",
             "ln1_g", "ln1_b", "w_ffn_in", "w_ffn_out", "ln2_g", "ln2_b"]
    weights = [w_ada, b_ada, w_in, g_q_a, w_q_b, g_kv_a, w_kv_b, w_o_a, w_conv, w_o_b, w_o, ln1_g, ln1_b,
               w_ffn_in, w_ffn_out, ln2_g, ln2_b]
    moms = [m_w_ada, m_b_ada, m_w_in, m_g_q_a, m_w_q_b, m_g_kv_a, m_w_kv_b, m_w_o_a, m_w_conv, m_w_o_b, m_w_o,
            m_ln1_g, m_ln1_b, m_w_ffn_in, m_w_ffn_out, m_ln2_g, m_ln2_b]
    vels = [v_w_ada, v_b_ada, v_w_in, v_g_q_a, v_w_q_b, v_g_kv_a, v_w_kv_b, v_w_o_a, v_w_conv, v_w_o_b, v_w_o,
            v_ln1_g, v_ln1_b, v_w_ffn_in, v_w_ffn_out, v_ln2_g, v_ln2_b]
    grad_of = {"b_ada": g_b_ada, "g_q_a": g_gq, "g_kv_a": g_gkv, "w_conv": g_wconv,
               "ln1_g": g_ln1g, "ln1_b": g_ln1b, "ln2_g": g_ln2g, "ln2_b": g_ln2b}
    state_of = dict(zip(names, zip(weights, moms, vels)))
    results = dict(arrived)

    def update(nm, reduced=None):
        w, m, v = state_of[nm]
        shp = w.shape
        w2 = w.reshape(shp[-2], shp[-1]) if w.ndim == 3 else w
        m2, v2 = m.reshape(w2.shape), v.reshape(w2.shape)
        if reduced is None:
            g2 = grad_of[nm].reshape(w2.shape)
            res = (g2,) + tuple(_adamw(w2, g2, m2, v2, "adamw_" + nm))
        else:
            res = _adamw_reduced(w2, reduced[0], reduced[1], m2, v2, my_chip, "adamw_" + nm)
        results[nm] = [a.reshape(shp) for a in res]

    for nm in grad_of:
        update(nm)
    in_own, in_got = _reduce_scatter_end(in_state, [res[1] for res in results.values()], "in")
    for nm, own, got in zip(("w_in", "w_q_b", "w_kv_b"), in_own, in_got):
        update(nm, (own, got))
    outs = [[results[nm][k] for nm in names] for k in range(4)]
    return (loss, grad_x.reshape(x.shape), *outs[0], *outs[1], *outs[2], *outs[3])
```

```python
import functools

import jax
import jax.numpy as jnp
from jax import lax
from jax.experimental import pallas as pl
from jax.experimental.pallas import tpu as pltpu

F32 = jnp.float32
BF16 = jnp.bfloat16
MESH_ID = pl.DeviceIdType.MESH
AXES = ("x", "y", "c")
N_DEV = 8

CHUNK = 64
QK_NOPE = 128
QK_ROPE = 64
V_HEAD = 128
QK_CAT = QK_NOPE + QK_ROPE
ROPE_THETA = 10000.0
ATTN_SCALE = (QK_NOPE + QK_ROPE) ** -0.5
CONV_K = 3
DEEPNORM_ALPHA = 2.0 ** 0.25
LN_EPS = 1e-5
RMS_EPS = 1e-6
NEG_INF = -1e30

ADAM_LR = 0.001
ADAM_B1 = 0.9
ADAM_B2 = 0.999
ADAM_EPS = 1e-08
ADAM_WD = 0.01
ADAM_STEP = 10

LANE = 128
COL_BLOCK = 256
PACK_ROW_ALIGN = 16
PAIR_SUM_BLOCK = 1 << 20
VMEM_LIMIT = 48 * 1024 * 1024


def _round_up(n, m):
    return (n + m - 1) // m * m


def _tile(n, pref, align=LANE):
    best = None
    t = align
    while t <= min(n, pref):
        if n % t == 0:
            best = t
        t += align
    return best if best is not None else n


def _cparams(sem=None):
    return pltpu.CompilerParams(dimension_semantics=sem, vmem_limit_bytes=VMEM_LIMIT)


def _sigmoid(x):
    return 0.5 * jnp.tanh(0.5 * x) + 0.5


def _matmul(a, b, mode, out_dtype, name, tm=1024, tn=1024, tk=2048, deps=(), out_shards=False, k_rows=None,
            init=None):
    b_shards = b.ndim == 3
    n = b.shape[2] if b_shards else (b.shape[1] // N_DEV if out_shards else None)
    if mode == "nn":
        (M, K), (K2, N) = a.shape, (b.shape[1], N_DEV * n) if b_shards else b.shape
    elif mode == "nt":
        (M, K), (N, K2) = a.shape, (b.shape[1], N_DEV * n) if b_shards else b.shape
    else:
        (K, M), (K2, N) = a.shape, b.shape
    assert K == K2, (a.shape, b.shape, mode)
    tm = _tile(M, tm)
    tn = n if (mode != "nt" and n is not None) else _tile(N, tn)
    k_row0, k_len = k_rows if k_rows is not None else (0, K)
    tk = n if (mode == "nt" and b_shards) else _tile(k_len, tk)
    nk, k0 = k_len // tk, k_row0 // tk
    if mode == "nn":
        a_spec = pl.BlockSpec((tm, tk), lambda i, j, k: (i, k0 + k))
        b_spec = (pl.BlockSpec((1, tk, n), lambda i, j, k: (j, k, 0)) if b_shards
                  else pl.BlockSpec((tk, tn), lambda i, j, k: (k0 + k, j)))
        dims = (((1,), (0,)), ((), ()))
    elif mode == "nt":
        a_spec = pl.BlockSpec((tm, tk), lambda i, j, k: (i, k))
        b_spec = (pl.BlockSpec((1, tn, n), lambda i, j, k: (k, j, 0)) if b_shards
                  else pl.BlockSpec((tn, tk), lambda i, j, k: (j, k)))
        dims = (((1,), (1,)), ((), ()))
    else:
        a_spec = pl.BlockSpec((tk, tm), lambda i, j, k: (k, i))
        b_spec = pl.BlockSpec((tk, tn), lambda i, j, k: (k, j))
        dims = (((0,), (0,)), ((), ()))
    if out_shards:
        out_spec = pl.BlockSpec((1, tm, n), lambda i, j, k: (j, i, 0))
        out_shape = jax.ShapeDtypeStruct((N_DEV, M, n), out_dtype)
    else:
        out_spec = pl.BlockSpec((tm, tn), lambda i, j, k: (i, j))
        out_shape = jax.ShapeDtypeStruct((M, N), out_dtype)

    def product(a_ref, b_ref):
        b_blk = b_ref[0] if b_shards else b_ref[...]
        return lax.dot_general(a_ref[...].astype(BF16), b_blk.astype(BF16), dims, preferred_element_type=F32)

    def write(o_ref, value):
        if out_shards:
            o_ref[0] = value.astype(o_ref.dtype)
        else:
            o_ref[...] = value.astype(o_ref.dtype)

    def body_whole_k(a_ref, b_ref, *rest):
        value = product(a_ref, b_ref)
        write(rest[-1], value if init is None else value + rest[0][...])

    def body_split_k(a_ref, b_ref, *rest):
        o_ref, acc_ref = rest[-2:]
        k = pl.program_id(2)

        @pl.when(k == 0)
        def _():
            acc_ref[...] = jnp.zeros_like(acc_ref) if init is None else rest[0][...]

        acc_ref[...] += product(a_ref, b_ref)

        @pl.when(k == nk - 1)
        def _():
            write(o_ref, acc_ref[...])

    return pl.pallas_call(
        body_whole_k if nk == 1 else body_split_k, name=name, grid=(M // tm, N // tn, nk),
        in_specs=[a_spec, b_spec] + ([] if init is None else [out_spec]) + [ANY_SPEC] * len(deps),
        out_specs=out_spec, out_shape=out_shape,
        scratch_shapes=[] if nk == 1 else [pltpu.VMEM((tm, tn), F32)],
        compiler_params=_cparams(("parallel", "parallel", "arbitrary")),
    )(a, b, *(() if init is None else (init,)), *deps)


def _assemble_w_in(shards, front, front_pad, rows, row0, into=None):
    _, K, n = shards.shape
    gap = front_pad - front
    tk = _tile(K, 256, PACK_ROW_ALIGN)
    blk0 = row0 // tk

    def body(g_ref, *rest):
        o_ref = rest[-1]
        if gap:
            o_ref[:, front:front_pad] = jnp.zeros((tk, gap), o_ref.dtype)
        for j in range(N_DEV):
            lo, hi = j * n, (j + 1) * n
            if lo < front < hi:
                o_ref[:, lo:front] = g_ref[j, :, 0:front - lo]
                o_ref[:, front_pad:hi + gap] = g_ref[j, :, front - lo:n]
            else:
                off = 0 if hi <= front else gap
                o_ref[:, lo + off:hi + off] = g_ref[j]

    return pl.pallas_call(
        body, name="assemble_w_in", grid=(K // tk,),
        in_specs=[pl.BlockSpec((N_DEV, tk, n), lambda i: (0, i, 0))] + ([] if into is None else [ANY_SPEC]),
        out_specs=pl.BlockSpec((tk, N_DEV * n + gap), lambda i: (blk0 + i, 0)),
        out_shape=jax.ShapeDtypeStruct((rows, N_DEV * n + gap), shards.dtype),
        input_output_aliases={} if into is None else {1: 0},
        compiler_params=_cparams(("parallel",)),
    )(*([shards] if into is None else [shards, into]))


def _split_w_in(w, front, front_pad):
    K, NP = w.shape
    gap = front_pad - front
    n = (NP - gap) // N_DEV
    tk = _tile(K, 256, PACK_ROW_ALIGN)

    def body(w_ref, o_ref):
        for j in range(N_DEV):
            lo, hi = j * n, (j + 1) * n
            if lo < front < hi:
                o_ref[j, :, 0:front - lo] = w_ref[:, lo:front]
                o_ref[j, :, front - lo:n] = w_ref[:, front_pad:hi + gap]
            else:
                off = 0 if hi <= front else gap
                o_ref[j] = w_ref[:, lo + off:hi + off]

    return pl.pallas_call(
        body, name="split_grad_w_in", grid=(K // tk,),
        in_specs=[pl.BlockSpec((tk, NP), lambda i: (i, 0))],
        out_specs=pl.BlockSpec((N_DEV, tk, n), lambda i: (0, i, 0)),
        out_shape=jax.ShapeDtypeStruct((N_DEV, K, n), w.dtype),
        compiler_params=_cparams(("parallel",)),
    )(w)


def _modulate_in(x, mod, ts):
    S, D = x.shape

    def body(x_ref, mod_ref, u_ref):
        u_ref[...] = (x_ref[...] * (1.0 + mod_ref[1:2, :]) + mod_ref[0:1, :]).astype(BF16)

    return pl.pallas_call(
        body, name="modulate_in", grid=(S // ts,),
        in_specs=[pl.BlockSpec((ts, D), lambda i: (i, 0)), pl.BlockSpec((6, D), lambda i: (0, 0))],
        out_specs=pl.BlockSpec((ts, D), lambda i: (i, 0)),
        out_shape=jax.ShapeDtypeStruct((S, D), BF16),
        compiler_params=_cparams(("parallel",)),
    )(x, mod)


def _rms_fwd(proj, g, blk, L, ts, name):
    S = proj.shape[0]

    def body(a_ref, g_ref, y_ref):
        a = a_ref[...].astype(F32)
        r = lax.rsqrt(jnp.mean(a * a, axis=-1, keepdims=True) + RMS_EPS)
        y_ref[...] = (a * r * g_ref[...]).astype(BF16)

    return pl.pallas_call(
        body, name=name, grid=(S // ts,),
        in_specs=[pl.BlockSpec((ts, L), lambda i: (i, blk)), pl.BlockSpec((1, L), lambda i: (0, 0))],
        out_specs=pl.BlockSpec((ts, L), lambda i: (i, 0)),
        out_shape=jax.ShapeDtypeStruct((S, L), BF16),
        compiler_params=_cparams(("parallel",)),
    )(proj, g)


def _rope_partner(x, period, start):
    w = x.shape[-1]
    lane = lax.broadcasted_iota(jnp.int32, x.shape, x.ndim - 1) % period
    first = (lane >= start) & (lane < start + QK_ROPE // 2)
    from_right = pltpu.roll(x, w - QK_ROPE // 2, axis=x.ndim - 1)
    from_left = pltpu.roll(x, QK_ROPE // 2, axis=x.ndim - 1)
    return jnp.where(first, -from_right, from_left)


def _qk_prep(q, kv, proj, kr_blk, cos_q, sin_q, cos_k, sin_k, H, ts):
    S = q.shape[0]
    pair = 2 * QK_CAT
    kv_w = QK_NOPE + V_HEAD

    def body(q_ref, kv_ref, kr_ref, cq_ref, sq_ref, ck_ref, sk_ref, qc_ref, kc_ref, vh_ref):
        kr = kr_ref[...].astype(F32)
        kr = kr * ck_ref[...] + _rope_partner(kr, QK_ROPE, 0) * sk_ref[...]
        kr = kr[:, :QK_ROPE].astype(BF16)
        for p in range(H // 2):
            x = q_ref[:, p * pair:(p + 1) * pair].astype(F32)
            x = x * cq_ref[...] + _rope_partner(x, QK_CAT, QK_NOPE) * sq_ref[...]
            qc_ref[2 * p] = x[:, :QK_CAT].astype(BF16)
            qc_ref[2 * p + 1] = x[:, QK_CAT:].astype(BF16)
        for h in range(H):
            kc_ref[h, :, 0:QK_NOPE] = kv_ref[:, h * kv_w:h * kv_w + QK_NOPE].astype(BF16)
            kc_ref[h, :, QK_NOPE:QK_CAT] = kr
            vh_ref[h, :, :] = kv_ref[:, h * kv_w + QK_NOPE:(h + 1) * kv_w].astype(BF16)

    row = lambda w: pl.BlockSpec((ts, w), lambda i: (i, 0))
    return pl.pallas_call(
        body, name="qk_prep", grid=(S // ts,),
        in_specs=[row(H * QK_CAT), row(H * kv_w),
                  pl.BlockSpec((ts, COL_BLOCK), lambda i: (i, kr_blk)),
                  row(pair), row(pair), row(COL_BLOCK), row(COL_BLOCK)],
        out_specs=[pl.BlockSpec((H, ts, QK_CAT), lambda i: (0, i, 0)),
                   pl.BlockSpec((H, ts, QK_CAT), lambda i: (0, i, 0)),
                   pl.BlockSpec((H, ts, V_HEAD), lambda i: (0, i, 0))],
        out_shape=[jax.ShapeDtypeStruct((H, S, QK_CAT), BF16), jax.ShapeDtypeStruct((H, S, QK_CAT), BF16),
                   jax.ShapeDtypeStruct((H, S, V_HEAD), BF16)],
        compiler_params=_cparams(("parallel",)),
    )(q, kv, proj, cos_q, sin_q, cos_k, sin_k)


NT_DIMS = (((1,), (1,)), ((), ()))
TN_DIMS = (((0,), (0,)), ((), ()))


def _diag_mask(T):
    rows = lax.broadcasted_iota(jnp.int32, (T, T), 0) // CHUNK
    cols = lax.broadcasted_iota(jnp.int32, (T, T), 1) // CHUNK
    return cols <= rows


def _attn_fwd(qc, kc, vh, T):
    H, S, _ = qc.shape
    n = S // T

    def body(q_ref, k_ref, v_ref, o_ref, lse_ref):
        q = q_ref[0]

        def block(i):
            L = (i + 1) * T
            s_old = lax.dot_general(q, k_ref[0, 0:i * T, :], NT_DIMS, preferred_element_type=F32) if i else None
            s_diag = lax.dot_general(q, k_ref[0, i * T:L, :], NT_DIMS, preferred_element_type=F32)
            s_diag = jnp.where(_diag_mask(T), s_diag, NEG_INF)
            m = jnp.max(s_diag, axis=-1, keepdims=True)
            if i:
                m = jnp.maximum(m, jnp.max(s_old, axis=-1, keepdims=True))
            p_diag = jnp.exp((s_diag - m) * ATTN_SCALE)
            l = jnp.sum(p_diag, axis=-1, keepdims=True)
            acc = jnp.dot(p_diag.astype(BF16), v_ref[0, i * T:L, :], preferred_element_type=F32)
            if i:
                p_old = jnp.exp((s_old - m) * ATTN_SCALE)
                l = l + jnp.sum(p_old, axis=-1, keepdims=True)
                acc = acc + jnp.dot(p_old.astype(BF16), v_ref[0, 0:i * T, :], preferred_element_type=F32)
            o_ref[...] = acc / l
            lse_ref[0] = m * ATTN_SCALE + jnp.log(l)

        for i in range(n):
            pl.when(pl.program_id(1) == i)(functools.partial(block, i))

    return pl.pallas_call(
        body, name="attn_fwd", grid=(H, n),
        in_specs=[pl.BlockSpec((1, T, QK_CAT), lambda h, i: (h, i, 0)),
                  pl.BlockSpec((1, S, QK_CAT), lambda h, i: (h, 0, 0)),
                  pl.BlockSpec((1, S, V_HEAD), lambda h, i: (h, 0, 0))],
        out_specs=[pl.BlockSpec((T, V_HEAD), lambda h, i: (i, h)),
                   pl.BlockSpec((1, T, 1), lambda h, i: (h, i, 0))],
        out_shape=[jax.ShapeDtypeStruct((S, H * V_HEAD), F32), jax.ShapeDtypeStruct((H, S, 1), F32)],
        compiler_params=_cparams(("parallel", "arbitrary")),
    )(qc, kc, vh)


def _shift_rows(z, k):
    if k == 0:
        return z
    n = z.shape[0]
    row = lax.broadcasted_iota(jnp.int32, z.shape, 0)
    if k > 0:
        return jnp.where(row >= k, pltpu.roll(z, k, axis=0), 0.0)
    return jnp.where(row < n + k, pltpu.roll(z, n + k, axis=0), 0.0)


def _conv_fwd(proj, w_conv, blk_b, blk_c, blk_x):
    S = proj.shape[0]
    D = w_conv.shape[1]
    nb = D // COL_BLOCK

    def body(cb_ref, cc_ref, cx_ref, w_ref, o_ref):
        z = cc_ref[...].astype(F32) * cx_ref[...].astype(F32)
        conv = w_ref[2:3, :] * z + w_ref[1:2, :] * _shift_rows(z, 1) + w_ref[0:1, :] * _shift_rows(z, 2)
        o_ref[...] = (cb_ref[...].astype(F32) * conv).astype(BF16)

    col = lambda off: pl.BlockSpec((S, COL_BLOCK), lambda j: (0, off + j))
    return pl.pallas_call(
        body, name="conv_fwd", grid=(nb,),
        in_specs=[col(blk_b), col(blk_c), col(blk_x), pl.BlockSpec((CONV_K, COL_BLOCK), lambda j: (0, j))],
        out_specs=pl.BlockSpec((S, COL_BLOCK), lambda j: (0, j)),
        out_shape=jax.ShapeDtypeStruct((S, D), BF16),
        compiler_params=_cparams(("parallel",)),
    )(proj, proj, proj, w_conv)


def _merge_fwd(proj, ya, yb, blk_ga, blk_gb, ts):
    S, D = ya.shape
    nb = D // COL_BLOCK

    def body(ga_ref, gb_ref, ya_ref, yb_ref, o_ref):
        sa, sb = _sigmoid(ga_ref[...].astype(F32)), _sigmoid(gb_ref[...].astype(F32))
        o_ref[...] = (sa * ya_ref[...] + sb * yb_ref[...]).astype(BF16)

    row = pl.BlockSpec((ts, D), lambda i: (i, 0))
    seg = lambda blk: pl.BlockSpec((pl.Element(ts), pl.Element(D)), lambda i: (i * ts, blk * COL_BLOCK))
    return pl.pallas_call(
        body, name="merge_fwd", grid=(S // ts,),
        in_specs=[seg(blk_ga), seg(blk_gb), row, row],
        out_specs=row,
        out_shape=jax.ShapeDtypeStruct((S, D), BF16),
        compiler_params=_cparams(("parallel",)),
    )(proj, proj, ya, yb)


def _ln1_fwd(x, mix, mod, g, b, ts):
    S, D = x.shape

    def body(x_ref, mix_ref, mod_ref, g_ref, b_ref, xhat_ref, rstd_ref, u2_ref):
        r = DEEPNORM_ALPHA * x_ref[...] + mod_ref[2:3, :] * mix_ref[...]
        mu = jnp.mean(r, axis=-1, keepdims=True)
        d = r - mu
        rstd = lax.rsqrt(jnp.mean(d * d, axis=-1, keepdims=True) + LN_EPS)
        xhat = d * rstd
        xhat_ref[...] = xhat
        rstd_ref[...] = rstd
        x1 = xhat * g_ref[...] + b_ref[...]
        u2_ref[...] = (x1 * (1.0 + mod_ref[4:5, :]) + mod_ref[3:4, :]).astype(BF16)

    row = pl.BlockSpec((ts, D), lambda i: (i, 0))
    vec = lambda r: pl.BlockSpec((r, D), lambda i: (0, 0))
    return pl.pallas_call(
        body, name="ln1_fwd", grid=(S // ts,),
        in_specs=[row, row, vec(6), vec(1), vec(1)],
        out_specs=[row, pl.BlockSpec((ts, 1), lambda i: (i, 0)), row],
        out_shape=[jax.ShapeDtypeStruct((S, D), F32), jax.ShapeDtypeStruct((S, 1), F32),
                   jax.ShapeDtypeStruct((S, D), BF16)],
        compiler_params=_cparams(("parallel",)),
    )(x, mix, mod, g, b)


def _swiglu_fwd(h, ts, tb):
    S, F2 = h.shape
    F = F2 // 2
    nb = F // tb

    def body(hg_ref, hu_ref, a_ref):
        hg = hg_ref[...].astype(F32)
        a_ref[...] = (hg * _sigmoid(hg) * hu_ref[...].astype(F32)).astype(BF16)

    return pl.pallas_call(
        body, name="swiglu_fwd", grid=(S // ts, nb),
        in_specs=[pl.BlockSpec((ts, tb), lambda i, j: (i, j)), pl.BlockSpec((ts, tb), lambda i, j: (i, j + nb))],
        out_specs=pl.BlockSpec((ts, tb), lambda i, j: (i, j)),
        out_shape=jax.ShapeDtypeStruct((S, F), BF16),
        compiler_params=_cparams(("parallel", "parallel")),
    )(h, h)


def _ln2_loss(xhat1, ffn, tgt, mod, g1, b1, g2, b2, ts):
    S, D = xhat1.shape

    def body(xh_ref, ffn_ref, t_ref, mod_ref, g1_ref, b1_ref, g2_ref, b2_ref, loss_ref, dffn_ref, dx1_ref, vec_ref):
        i = pl.program_id(0)

        @pl.when(i == 0)
        def _():
            loss_ref[...] = jnp.zeros_like(loss_ref)
            vec_ref[...] = jnp.zeros_like(vec_ref)

        x1 = xh_ref[...] * g1_ref[...] + b1_ref[...]
        ffn = ffn_ref[...]
        r = DEEPNORM_ALPHA * x1 + mod_ref[5:6, :] * ffn
        mu = jnp.mean(r, axis=-1, keepdims=True)
        d = r - mu
        rstd = lax.rsqrt(jnp.mean(d * d, axis=-1, keepdims=True) + LN_EPS)
        xhat = d * rstd
        e = xhat * g2_ref[...] + b2_ref[...] - t_ref[...]
        loss_ref[...] += 0.5 * jnp.sum(jnp.mean(e * e, axis=-1, keepdims=True))
        dy = e * (1.0 / D)
        dxhat = dy * g2_ref[...]
        dr = rstd * (dxhat - jnp.mean(dxhat, axis=-1, keepdims=True)
                     - xhat * jnp.mean(dxhat * xhat, axis=-1, keepdims=True))
        dffn_ref[...] = (dr * mod_ref[5:6, :]).astype(BF16)
        dx1_ref[...] = DEEPNORM_ALPHA * dr
        vec_ref[0:1, :] += jnp.sum(dy * xhat, axis=0, keepdims=True)
        vec_ref[1:2, :] += jnp.sum(dy, axis=0, keepdims=True)
        vec_ref[2:3, :] += jnp.sum(dr * ffn, axis=0, keepdims=True)

    row = pl.BlockSpec((ts, D), lambda i: (i, 0))
    vec = lambda r: pl.BlockSpec((r, D), lambda i: (0, 0))
    return pl.pallas_call(
        body, name="ln2_loss", grid=(S // ts,),
        in_specs=[row, row, row, vec(6), vec(1), vec(1), vec(1), vec(1)],
        out_specs=[pl.BlockSpec((1, LANE), lambda i: (0, 0)), row, row, vec(8)],
        out_shape=[jax.ShapeDtypeStruct((1, LANE), F32), jax.ShapeDtypeStruct((S, D), BF16),
                   jax.ShapeDtypeStruct((S, D), F32), jax.ShapeDtypeStruct((8, D), F32)],
        compiler_params=_cparams(("arbitrary",)),
    )(xhat1, ffn, tgt, mod, g1, b1, g2, b2)


def _swiglu_bwd(da, h, ts, tb):
    S, F2 = h.shape
    nb = (F2 // 2) // tb

    def body(da_ref, hg_ref, hu_ref, dh_ref):
        hg, da = hg_ref[...].astype(F32), da_ref[...].astype(F32)
        sg = _sigmoid(hg)

        @pl.when(pl.program_id(2) == 0)
        def _():
            dh_ref[...] = (da * hu_ref[...].astype(F32) * (sg * (1.0 + hg * (1.0 - sg)))).astype(BF16)

        @pl.when(pl.program_id(2) == 1)
        def _():
            dh_ref[...] = (da * hg * sg).astype(BF16)

    lo = pl.BlockSpec((ts, tb), lambda i, j, k: (i, j))
    hi = pl.BlockSpec((ts, tb), lambda i, j, k: (i, j + nb))
    return pl.pallas_call(
        body, name="swiglu_bwd", grid=(S // ts, nb, 2),
        in_specs=[lo, lo, hi],
        out_specs=pl.BlockSpec((ts, tb), lambda i, j, k: (i, j + nb * k)),
        out_shape=jax.ShapeDtypeStruct((S, F2), BF16),
        compiler_params=_cparams(("parallel", "parallel", "arbitrary")),
    )(da, h, h)


def _ln1_bwd(du2, dx1a, xhat1, rstd1, mix, mod, g1, b1, ts):
    S, D = xhat1.shape

    def body(du2_ref, dx1a_ref, xh_ref, rstd_ref, mix_ref, mod_ref, g_ref, b_ref, dxa_ref, dmix_ref, vec_ref):
        i = pl.program_id(0)

        @pl.when(i == 0)
        def _():
            vec_ref[...] = jnp.zeros_like(vec_ref)

        xhat, du2, mix = xh_ref[...], du2_ref[...], mix_ref[...]
        x1 = xhat * g_ref[...] + b_ref[...]
        dx1 = dx1a_ref[...] + du2 * (1.0 + mod_ref[4:5, :])
        dxhat = dx1 * g_ref[...]
        dr = rstd_ref[...] * (dxhat - jnp.mean(dxhat, axis=-1, keepdims=True)
                              - xhat * jnp.mean(dxhat * xhat, axis=-1, keepdims=True))
        dxa_ref[...] = DEEPNORM_ALPHA * dr
        dmix_ref[...] = (dr * mod_ref[2:3, :]).astype(BF16)
        vec_ref[0:1, :] += jnp.sum(du2, axis=0, keepdims=True)
        vec_ref[1:2, :] += jnp.sum(du2 * x1, axis=0, keepdims=True)
        vec_ref[2:3, :] += jnp.sum(dx1 * xhat, axis=0, keepdims=True)
        vec_ref[3:4, :] += jnp.sum(dx1, axis=0, keepdims=True)
        vec_ref[4:5, :] += jnp.sum(dr * mix, axis=0, keepdims=True)

    row = pl.BlockSpec((ts, D), lambda i: (i, 0))
    vec = lambda r: pl.BlockSpec((r, D), lambda i: (0, 0))
    return pl.pallas_call(
        body, name="ln1_bwd", grid=(S // ts,),
        in_specs=[row, row, row, pl.BlockSpec((ts, 1), lambda i: (i, 0)), row, vec(6), vec(1), vec(1)],
        out_specs=[row, row, vec(8)],
        out_shape=[jax.ShapeDtypeStruct((S, D), F32), jax.ShapeDtypeStruct((S, D), BF16),
                   jax.ShapeDtypeStruct((8, D), F32)],
        compiler_params=_cparams(("arbitrary",)),
    )(du2, dx1a, xhat1, rstd1, mix, mod, g1, b1)


def _merge_bwd(dmerged, proj, ya, yb, blk_ga, blk_gb, ts):
    S, D = ya.shape
    nb = D // COL_BLOCK

    def body(dm_ref, ga_ref, gb_ref, ya_ref, yb_ref, dya_ref, dyb_ref, dga_ref, dgb_ref):
        dm = dm_ref[...]
        sa, sb = _sigmoid(ga_ref[...].astype(F32)), _sigmoid(gb_ref[...].astype(F32))
        dya_ref[...] = (dm * sa).astype(BF16)
        dyb_ref[...] = (dm * sb).astype(BF16)
        dga_ref[...] = (dm * ya_ref[...] * sa * (1.0 - sa)).astype(BF16)
        dgb_ref[...] = (dm * yb_ref[...] * sb * (1.0 - sb)).astype(BF16)

    row = pl.BlockSpec((ts, D), lambda i: (i, 0))
    seg = lambda blk: pl.BlockSpec((pl.Element(ts), pl.Element(D)), lambda i: (i * ts, blk * COL_BLOCK))
    out = jax.ShapeDtypeStruct((S, D), BF16)
    return pl.pallas_call(
        body, name="merge_bwd", grid=(S // ts,),
        in_specs=[row, seg(blk_ga), seg(blk_gb), row, row],
        out_specs=[row] * 4,
        out_shape=[out] * 4,
        compiler_params=_cparams(("parallel",)),
    )(dmerged, proj, proj, ya, yb)


def _conv_bwd(dcbc, proj, w_conv, blk_b, blk_c, blk_x):
    S = proj.shape[0]
    D = w_conv.shape[1]
    nb = D // COL_BLOCK

    def body(d_ref, cb_ref, cc_ref, cx_ref, w_ref, dcb_ref, dcc_ref, dcx_ref, dw_ref):
        d, cc, cx = d_ref[...], cc_ref[...].astype(F32), cx_ref[...].astype(F32)
        z = cc * cx
        z1, z2 = _shift_rows(z, 1), _shift_rows(z, 2)
        conv = w_ref[2:3, :] * z + w_ref[1:2, :] * z1 + w_ref[0:1, :] * z2
        dcb_ref[...] = (d * conv).astype(BF16)
        dconv = d * cb_ref[...].astype(F32)
        dz = w_ref[2:3, :] * dconv + w_ref[1:2, :] * _shift_rows(dconv, -1) + w_ref[0:1, :] * _shift_rows(dconv, -2)
        dcc_ref[...] = (dz * cx).astype(BF16)
        dcx_ref[...] = (dz * cc).astype(BF16)
        dw_ref[...] = jnp.zeros_like(dw_ref)
        dw_ref[0:1, :] = jnp.sum(dconv * z2, axis=0, keepdims=True)
        dw_ref[1:2, :] = jnp.sum(dconv * z1, axis=0, keepdims=True)
        dw_ref[2:3, :] = jnp.sum(dconv * z, axis=0, keepdims=True)

    col = lambda off: pl.BlockSpec((S, COL_BLOCK), lambda j: (0, off + j))
    out = jax.ShapeDtypeStruct((S, D), BF16)
    return pl.pallas_call(
        body, name="conv_bwd", grid=(nb,),
        in_specs=[col(0), col(blk_b), col(blk_c), col(blk_x), pl.BlockSpec((CONV_K, COL_BLOCK), lambda j: (0, j))],
        out_specs=[col(0), col(0), col(0), pl.BlockSpec((8, COL_BLOCK), lambda j: (0, j))],
        out_shape=[out, out, out, jax.ShapeDtypeStruct((8, D), F32)],
        compiler_params=_cparams(("parallel",)),
    )(dcbc, proj, proj, proj, w_conv)


def _attn_bwd(qc, kc, vh, do, o, lse, T):
    H, S, _ = qc.shape
    n = S // T

    def body(q_ref, k_ref, v_ref, do_ref, o_ref, lse_ref, dq_ref, dk_ref, dv_ref, d_ref, dk_acc, dv_acc):
        j = pl.program_id(1)

        @pl.when(j == 0)
        def _():
            dq_ref[...] = jnp.zeros_like(dq_ref)
            d_ref[...] = jnp.sum(do_ref[...] * o_ref[...], axis=-1, keepdims=True)

        dk_acc[...] = jnp.zeros_like(dk_acc)
        dv_acc[...] = jnp.zeros_like(dv_acc)
        k, v = k_ref[0], v_ref[0]

        def step(i, masked):
            rows = pl.ds(pl.multiple_of(i * T, T), T)
            q = q_ref[0, rows, :]
            do = do_ref[rows, :].astype(BF16)
            s = lax.dot_general(q, k, NT_DIMS, preferred_element_type=F32) * ATTN_SCALE
            if masked:
                s = jnp.where(_diag_mask(T), s, NEG_INF)
            p = jnp.exp(s - lse_ref[0, rows, :])
            dv_acc[...] += lax.dot_general(p.astype(BF16), do, TN_DIMS, preferred_element_type=F32)
            dp = lax.dot_general(do, v, NT_DIMS, preferred_element_type=F32)
            ds = (p * (dp - d_ref[rows, :]) * ATTN_SCALE).astype(BF16)
            dk_acc[...] += lax.dot_general(ds, q, TN_DIMS, preferred_element_type=F32)
            dq_ref[0, rows, :] += jnp.dot(ds, k, preferred_element_type=F32)

        def above(i, carry):
            step(i, False)
            return carry

        step(j, True)
        lax.fori_loop(j + 1, n, above, 0)
        dk_ref[0] = dk_acc[...]
        dv_ref[0] = dv_acc[...]

    head = lambda w: pl.BlockSpec((1, S, w), lambda h, j: (h, 0, 0))
    blk = lambda w: pl.BlockSpec((1, T, w), lambda h, j: (h, j, 0))
    ospec = pl.BlockSpec((S, V_HEAD), lambda h, j: (0, h))
    return pl.pallas_call(
        body, name="attn_bwd", grid=(H, n),
        in_specs=[head(QK_CAT), blk(QK_CAT), blk(V_HEAD), ospec, ospec, head(1)],
        out_specs=[head(QK_CAT), blk(QK_CAT), blk(V_HEAD)],
        out_shape=[jax.ShapeDtypeStruct((H, S, QK_CAT), F32), jax.ShapeDtypeStruct((H, S, QK_CAT), F32),
                   jax.ShapeDtypeStruct((H, S, V_HEAD), F32)],
        scratch_shapes=[pltpu.VMEM((S, 1), F32), pltpu.VMEM((T, QK_CAT), F32), pltpu.VMEM((T, V_HEAD), F32)],
        compiler_params=_cparams(("parallel", "arbitrary")),
    )(qc, kc, vh, do, o, lse)


def _qk_bwd(dqc, dkc, dvh, cos_q, sin_q, cos_k, sin_k, ts):
    H, S, _ = dqc.shape
    pair = 2 * QK_CAT
    kv_w = QK_NOPE + V_HEAD

    def body(dqc_ref, dkc_ref, dvh_ref, cq_ref, sq_ref, ck_ref, sk_ref, dq_ref, dkv_ref, dkr_ref, q_buf, kr_buf):
        for p in range(H // 2):
            q_buf[:, :QK_CAT] = dqc_ref[2 * p]
            q_buf[:, QK_CAT:] = dqc_ref[2 * p + 1]
            g = q_buf[...]
            dq_ref[:, p * pair:(p + 1) * pair] = (
                g * cq_ref[...] - _rope_partner(g, QK_CAT, QK_NOPE) * sq_ref[...]).astype(BF16)
        kr_sum = jnp.zeros((ts, QK_ROPE), F32)
        for h in range(H):
            dkv_ref[:, h * kv_w:h * kv_w + QK_NOPE] = dkc_ref[h, :, 0:QK_NOPE].astype(BF16)
            dkv_ref[:, h * kv_w + QK_NOPE:(h + 1) * kv_w] = dvh_ref[h].astype(BF16)
            kr_sum = kr_sum + dkc_ref[h, :, QK_NOPE:QK_CAT]
        kr_buf[...] = jnp.zeros_like(kr_buf)
        kr_buf[:, 0:QK_ROPE] = kr_sum
        kr = kr_buf[...]
        dkr_ref[...] = (kr * ck_ref[...] - _rope_partner(kr, QK_ROPE, 0) * sk_ref[...]).astype(BF16)

    row = lambda w: pl.BlockSpec((ts, w), lambda i: (i, 0))
    head = lambda w: pl.BlockSpec((H, ts, w), lambda i: (0, i, 0))
    return pl.pallas_call(
        body, name="qk_bwd", grid=(S // ts,),
        in_specs=[head(QK_CAT), head(QK_CAT), head(V_HEAD), row(pair), row(pair), row(COL_BLOCK), row(COL_BLOCK)],
        out_specs=[row(H * QK_CAT), row(H * kv_w), row(COL_BLOCK)],
        out_shape=[jax.ShapeDtypeStruct((S, H * QK_CAT), BF16), jax.ShapeDtypeStruct((S, H * kv_w), BF16),
                   jax.ShapeDtypeStruct((S, COL_BLOCK), BF16)],
        scratch_shapes=[pltpu.VMEM((ts, pair), F32), pltpu.VMEM((ts, COL_BLOCK), F32)],
        compiler_params=_cparams(("parallel",)),
    )(dqc, dkc, dvh, cos_q, sin_q, cos_k, sin_k)


def _rms_bwd(dy, proj, g, blk, L, ts, name):
    S = proj.shape[0]

    def body(dy_ref, a_ref, g_ref, da_ref, dg_ref):
        i = pl.program_id(0)

        @pl.when(i == 0)
        def _():
            dg_ref[...] = jnp.zeros_like(dg_ref)

        a, dy = a_ref[...].astype(F32), dy_ref[...]
        r = lax.rsqrt(jnp.mean(a * a, axis=-1, keepdims=True) + RMS_EPS)
        dyh = dy * g_ref[...]
        da = r * dyh - a * (r * r * r) * jnp.mean(dyh * a, axis=-1, keepdims=True)
        da_ref[...] = da.astype(BF16)
        dg_ref[0:1, :] += jnp.sum(dy * a * r, axis=0, keepdims=True)

    return pl.pallas_call(
        body, name=name, grid=(S // ts,),
        in_specs=[pl.BlockSpec((ts, L), lambda i: (i, 0)), pl.BlockSpec((ts, L), lambda i: (i, blk)),
                  pl.BlockSpec((1, L), lambda i: (0, 0))],
        out_specs=[pl.BlockSpec((ts, L), lambda i: (i, 0)), pl.BlockSpec((8, L), lambda i: (0, 0))],
        out_shape=[jax.ShapeDtypeStruct((S, L), BF16), jax.ShapeDtypeStruct((8, L), F32)],
        compiler_params=_cparams(("arbitrary",)),
    )(dy, proj, g)


def _grad_x(du, dxa, x, mod, ts):
    S, D = x.shape

    def body(du_ref, dxa_ref, x_ref, mod_ref, dx_ref, vec_ref):
        i = pl.program_id(0)

        @pl.when(i == 0)
        def _():
            vec_ref[...] = jnp.zeros_like(vec_ref)

        du = du_ref[...]
        dx_ref[...] = dxa_ref[...] + du * (1.0 + mod_ref[1:2, :])
        vec_ref[0:1, :] += jnp.sum(du, axis=0, keepdims=True)
        vec_ref[1:2, :] += jnp.sum(du * x_ref[...], axis=0, keepdims=True)

    row = pl.BlockSpec((ts, D), lambda i: (i, 0))
    vec = lambda r: pl.BlockSpec((r, D), lambda i: (0, 0))
    return pl.pallas_call(
        body, name="grad_x", grid=(S // ts,),
        in_specs=[row, row, row, vec(6)],
        out_specs=[row, vec(8)],
        out_shape=[jax.ShapeDtypeStruct((S, D), F32), jax.ShapeDtypeStruct((8, D), F32)],
        compiler_params=_cparams(("arbitrary",)),
    )(du, dxa, x, mod)


def _adamw(w, g, m, v, name):
    R, C = w.shape
    tr = _tile(R, max(8, (1 << 19) // C), 8)
    c1 = 1.0 / (1.0 - ADAM_B1 ** ADAM_STEP)
    c2 = 1.0 / (1.0 - ADAM_B2 ** ADAM_STEP)

    def body(w_ref, g_ref, m_ref, v_ref, d_ref, nm_ref, nv_ref):
        g = g_ref[...]
        m = ADAM_B1 * m_ref[...] + (1.0 - ADAM_B1) * g
        v = ADAM_B2 * v_ref[...] + (1.0 - ADAM_B2) * (g * g)
        nm_ref[...] = m
        nv_ref[...] = v
        d_ref[...] = -ADAM_LR * ((m * c1) / (jnp.sqrt(v * c2) + ADAM_EPS) + ADAM_WD * w_ref[...])

    spec = pl.BlockSpec((tr, C), lambda i: (i, 0))
    out = jax.ShapeDtypeStruct((R, C), F32)
    return pl.pallas_call(
        body, name=name, grid=(R // tr,),
        in_specs=[spec] * 4, out_specs=[spec] * 3, out_shape=[out] * 3,
        compiler_params=_cparams(("parallel",)),
    )(w, g, m, v)


def _adamw_ada(w, cact_t, dmod, m, v):
    R, C = w.shape
    tr = _tile(R, max(8, (1 << 18) // C), 8)
    c1 = 1.0 / (1.0 - ADAM_B1 ** ADAM_STEP)
    c2 = 1.0 / (1.0 - ADAM_B2 ** ADAM_STEP)

    def body(w_ref, ct_ref, dm_ref, m_ref, v_ref, g_ref, d_ref, nm_ref, nv_ref):
        ct = ct_ref[...].astype(BF16).astype(F32)
        dm = dm_ref[...].astype(BF16).astype(F32)
        g = ct[:, 0:1] * dm[0:1, :]
        for b in range(1, N_DEV):
            g = g + ct[:, b:b + 1] * dm[b:b + 1, :]
        m = ADAM_B1 * m_ref[...] + (1.0 - ADAM_B1) * g
        v = ADAM_B2 * v_ref[...] + (1.0 - ADAM_B2) * (g * g)
        g_ref[...] = g
        nm_ref[...] = m
        nv_ref[...] = v
        d_ref[...] = -ADAM_LR * ((m * c1) / (jnp.sqrt(v * c2) + ADAM_EPS) + ADAM_WD * w_ref[...])

    spec = pl.BlockSpec((tr, C), lambda i: (i, 0))
    out = jax.ShapeDtypeStruct((R, C), F32)
    return pl.pallas_call(
        body, name="adamw_w_ada", grid=(R // tr,),
        in_specs=[spec, pl.BlockSpec((tr, N_DEV), lambda i: (i, 0)), pl.BlockSpec((N_DEV, C), lambda i: (0, 0)),
                  spec, spec],
        out_specs=[spec] * 4, out_shape=[out] * 4,
        compiler_params=_cparams(("parallel",)),
    )(w, cact_t, dmod, m, v)


def _adamw_reduced(w, own, got, m, v, my_chip, name):
    R, C = w.shape
    tr = _tile(R, max(PACK_ROW_ALIGN, (1 << 18) // C), PACK_ROW_ALIGN)
    c1 = 1.0 / (1.0 - ADAM_B1 ** ADAM_STEP)
    c2 = 1.0 / (1.0 - ADAM_B2 ** ADAM_STEP)

    def body(chip_ref, w_ref, own_ref, g1_ref, g2_ref, g3_ref, m_ref, v_ref, g_ref, d_ref, nm_ref, nv_ref):
        g = own_ref[0].astype(F32) + g1_ref[0].astype(F32) + g2_ref[0].astype(F32) + g3_ref[0].astype(F32)
        m = ADAM_B1 * m_ref[...] + (1.0 - ADAM_B1) * g
        v = ADAM_B2 * v_ref[...] + (1.0 - ADAM_B2) * (g * g)
        g_ref[...] = g
        nm_ref[...] = m
        nv_ref[...] = v
        d_ref[...] = -ADAM_LR * ((m * c1) / (jnp.sqrt(v * c2) + ADAM_EPS) + ADAM_WD * w_ref[...])

    spec = pl.BlockSpec((tr, C), lambda i, chip: (i, 0))
    slot = lambda k: pl.BlockSpec((1, tr, C), lambda i, chip: (chip[0] ^ k, i, 0))
    out = jax.ShapeDtypeStruct((R, C), F32)
    return pl.pallas_call(
        body, name=name,
        grid_spec=pltpu.PrefetchScalarGridSpec(
            num_scalar_prefetch=1, grid=(R // tr,),
            in_specs=[spec, slot(0), slot(1), slot(2), slot(3), spec, spec],
            out_specs=[spec] * 4),
        out_shape=[out] * 4,
        compiler_params=_cparams(("parallel",)),
    )(my_chip, w, own, got, got, got, m, v)


def _my_place():
    return lax.axis_index("x"), lax.axis_index("y"), lax.axis_index("c")


def _peer(k):
    x, y, c = _my_place()
    return (x ^ ((k >> 2) & 1), y ^ ((k >> 1) & 1), c ^ (k & 1))


def _linear(place):
    return 4 * place[0] + 2 * place[1] + place[2]


def _ada_fwd(c_row, wconv_row, w_ada, b_row):
    D, CW = w_ada.shape
    WC = wconv_row.shape[-1]

    def body(c_ref, wc_ref, w_ref, b_ref, mod_ref, cact_ref, wcall_ref, send_buf, sems):
        me = _linear(_my_place())
        c = c_ref[0]
        cact_ref[me] = c * _sigmoid(c)
        wcall_ref[me] = wc_ref[0]

        def gather_copy(buf, k, grp):
            return pltpu.make_async_remote_copy(
                src_ref=buf.at[me], dst_ref=buf.at[me], send_sem=sems.at[0, grp, k], recv_sem=sems.at[1, grp, k],
                device_id=_peer(k), device_id_type=MESH_ID)

        def gather_recv(buf, k, grp):
            src = _linear(_peer(k))
            return pltpu.make_async_remote_copy(
                src_ref=buf.at[src], dst_ref=buf.at[src], send_sem=sems.at[0, grp, k], recv_sem=sems.at[1, grp, k],
                device_id=_peer(k), device_id_type=MESH_ID)

        for k in range(1, N_DEV):
            gather_copy(cact_ref, k, 0).start()
            gather_copy(wcall_ref, k, 1).start()
        for k in range(1, N_DEV):
            gather_recv(cact_ref, k, 0).wait_recv()
            gather_recv(wcall_ref, k, 1).wait_recv()
        for k in range(1, N_DEV):
            gather_copy(cact_ref, k, 0).wait_send()
            gather_copy(wcall_ref, k, 1).wait_send()

        cact = jnp.concatenate([cact_ref[b] for b in range(N_DEV)], axis=0)
        mod_all = jnp.dot(cact.astype(BF16), w_ref[...].astype(BF16), preferred_element_type=F32) + b_ref[0]
        for b in range(N_DEV):
            send_buf[b] = mod_all[b:b + 1, :]
        mod_ref[me] = send_buf[me]

        def scatter_copy(k):
            dst = _linear(_peer(k))
            return pltpu.make_async_remote_copy(
                src_ref=send_buf.at[dst], dst_ref=mod_ref.at[me], send_sem=sems.at[0, 2, k], recv_sem=sems.at[1, 2, k],
                device_id=_peer(k), device_id_type=MESH_ID)

        def scatter_recv(k):
            src = _linear(_peer(k))
            return pltpu.make_async_remote_copy(
                src_ref=send_buf.at[src], dst_ref=mod_ref.at[src], send_sem=sems.at[0, 2, k], recv_sem=sems.at[1, 2, k],
                device_id=_peer(k), device_id_type=MESH_ID)

        for k in range(1, N_DEV):
            scatter_copy(k).start()
        for k in range(1, N_DEV):
            scatter_recv(k).wait_recv()
        for k in range(1, N_DEV):
            scatter_copy(k).wait_send()

    vmem = pl.BlockSpec(memory_space=pltpu.VMEM)
    return pl.pallas_call(
        body, name="ada_fwd",
        in_specs=[vmem] * 4, out_specs=[vmem] * 3,
        out_shape=[jax.ShapeDtypeStruct((N_DEV, 1, CW), F32), jax.ShapeDtypeStruct((N_DEV, 1, D), F32),
                   jax.ShapeDtypeStruct((N_DEV, 1, WC), F32)],
        scratch_shapes=[pltpu.VMEM((N_DEV, 1, CW), F32), pltpu.SemaphoreType.DMA((2, 3, N_DEV))],
        compiler_params=pltpu.CompilerParams(vmem_limit_bytes=VMEM_LIMIT),
    )(c_row, wconv_row, w_ada, b_row)


def _ada_bwd(payload, deps=()):
    NCH, _, CW = payload.shape

    def body(p_ref, *rest):
        sum_ref, mine_ref, all_ref, sems = rest[-4:]
        me = _linear(_my_place())
        all_ref[me] = p_ref[...]

        def copy(k, slot):
            return pltpu.make_async_remote_copy(
                src_ref=all_ref.at[slot], dst_ref=all_ref.at[slot], send_sem=sems.at[0, k], recv_sem=sems.at[1, k],
                device_id=_peer(k), device_id_type=MESH_ID)

        for k in range(1, N_DEV):
            copy(k, me).start()
        for k in range(1, N_DEV):
            copy(k, _linear(_peer(k))).wait_recv()
        for k in range(1, N_DEV):
            copy(k, me).wait_send()

        total = all_ref[0]
        for b in range(1, N_DEV):
            total = total + all_ref[b]
        sum_ref[...] = total

        for b in range(N_DEV):
            mine_ref[b] = all_ref[b, me]

    vmem = pl.BlockSpec(memory_space=pltpu.VMEM)
    return pl.pallas_call(
        body, name="ada_bwd",
        in_specs=[vmem] + [ANY_SPEC] * len(deps), out_specs=[vmem, vmem],
        out_shape=[jax.ShapeDtypeStruct((NCH, 1, CW), F32), jax.ShapeDtypeStruct((N_DEV, 1, CW), F32)],
        scratch_shapes=[pltpu.VMEM((N_DEV, NCH, 1, CW), F32), pltpu.SemaphoreType.DMA((2, N_DEV))],
        compiler_params=pltpu.CompilerParams(vmem_limit_bytes=VMEM_LIMIT),
    )(payload, *deps)


def _exchange_in_chip(parts):
    W = len(parts)

    def body(*refs):
        p_refs, got_refs, (send_sems, recv_sems) = refs[:W], refs[W:2 * W], refs[2 * W:]
        x, y, c = _my_place()
        sibling = (x, y, 1 - c)
        copies = []
        for w in range(W):
            for q in range(4):
                copies.append(pltpu.make_async_remote_copy(
                    src_ref=p_refs[w].at[2 * q + (1 - c)], dst_ref=got_refs[w].at[q],
                    send_sem=send_sems.at[4 * w + q], recv_sem=recv_sems.at[4 * w + q],
                    device_id=sibling, device_id_type=MESH_ID))
        for cp in copies:
            cp.start()
        for cp in copies:
            cp.wait_recv()
        for cp in copies:
            cp.wait_send()

    return pl.pallas_call(
        body, name="grad_exchange_in_chip",
        in_specs=[HBM_SPEC] * W, out_specs=[HBM_SPEC] * W,
        out_shape=[jax.ShapeDtypeStruct((4,) + p.shape[1:], p.dtype) for p in parts],
        scratch_shapes=[pltpu.SemaphoreType.DMA((4 * W,)), pltpu.SemaphoreType.DMA((4 * W,))],
    )(*parts)


def _pair_sum(parts, got, core):
    _, R, C = parts.shape
    tr = _tile(R, max(PACK_ROW_ALIGN, PAIR_SUM_BLOCK // C), PACK_ROW_ALIGN)

    def body(c_ref, p_ref, g_ref, o_ref):
        o_ref[...] = (p_ref[...].astype(F32) + g_ref[...].astype(F32)).astype(o_ref.dtype)

    return pl.pallas_call(
        body, name="grad_pair_sum",
        grid_spec=pltpu.PrefetchScalarGridSpec(
            num_scalar_prefetch=1, grid=(4, R // tr),
            in_specs=[pl.BlockSpec((1, tr, C), lambda q, i, c_ref: (2 * q + c_ref[0], i, 0)),
                      pl.BlockSpec((1, tr, C), lambda q, i, c_ref: (q, i, 0))],
            out_specs=pl.BlockSpec((1, tr, C), lambda q, i, c_ref: (q, i, 0))),
        out_shape=jax.ShapeDtypeStruct((4, R, C), parts.dtype),
        compiler_params=_cparams(("parallel", "parallel")),
    )(core, parts, got)


HBM_SPEC = pl.BlockSpec(memory_space=pltpu.HBM)
SEM_SPEC = pl.BlockSpec(memory_space=pltpu.SEMAPHORE)
ANY_SPEC = pl.BlockSpec(memory_space=pl.ANY)
SPLIT_EFFECT = pltpu.SideEffectType.DATAFLOW_SIDE_EFFECTING


def _landing_zone(shape, dtype):
    return pltpu.with_memory_space_constraint(lax.empty(shape, dtype), pltpu.HBM)


def _split_start(name, arrays, lands, after, copies_of, per_array):
    W = len(arrays)
    after = tuple(after) if isinstance(after, (tuple, list)) else (after,)

    def body(*refs):
        x_refs, land_refs = refs[:W], refs[W:2 * W]
        send_sems, recv_sems = refs[2 * W + len(after)], refs[2 * W + len(after) + 1]
        token = refs[-1]
        k = 0
        for w in range(W):
            for src, dst, dev in copies_of(w, x_refs[w], land_refs[w]):
                pltpu.make_async_remote_copy(src_ref=src, dst_ref=dst, send_sem=send_sems.at[k], recv_sem=recv_sems.at[k],
                                             device_id=dev, device_id_type=MESH_ID).start()
                k += 1
        token[...] = jnp.zeros_like(token)

    n_copies = per_array * W
    hbm_of = lambda xs: tuple(pltpu.HBM(a.shape, a.dtype) for a in xs)
    out = pl.pallas_call(
        body, name=name,
        out_shape=(pltpu.SemaphoreType.DMA((n_copies,)), pltpu.SemaphoreType.DMA((n_copies,)))
        + hbm_of(arrays) + hbm_of(lands) + (jax.ShapeDtypeStruct((8, LANE), F32),),
        in_specs=(HBM_SPEC,) * (2 * W) + (ANY_SPEC,) * len(after),
        out_specs=(SEM_SPEC, SEM_SPEC) + (HBM_SPEC,) * (2 * W) + (pl.BlockSpec(memory_space=pltpu.VMEM),),
        input_output_aliases={i: 2 + i for i in range(2 * W)},
        compiler_params=pltpu.CompilerParams(has_side_effects=SPLIT_EFFECT),
    )(*[pltpu.with_memory_space_constraint(a, pltpu.HBM) for a in arrays], *lands, *after)
    return out[0], out[1], list(out[2:2 + W]), list(out[2 + W:2 + 2 * W]), out[-1]


def _split_wait(name, state, after, copies_of):
    send_sems, recv_sems, arrays, lands, _ = state
    W = len(arrays)
    after = tuple(after) if isinstance(after, (tuple, list)) else (after,)

    def body(*refs):
        x_refs, land_refs = refs[:W], refs[W:2 * W]
        send_sems, recv_sems = refs[2 * W], refs[2 * W + 1]
        k = 0
        for w in range(W):
            for src, dst, dev in copies_of(w, x_refs[w], land_refs[w]):
                cp = pltpu.make_async_remote_copy(src_ref=src, dst_ref=dst, send_sem=send_sems.at[k],
                                                  recv_sem=recv_sems.at[k], device_id=dev, device_id_type=MESH_ID)
                cp.wait_send()
                cp.wait_recv()
                k += 1

    out = pl.pallas_call(
        body, name=name,
        out_shape=tuple(pltpu.HBM(a.shape, a.dtype) for a in arrays + lands),
        in_specs=(HBM_SPEC,) * (2 * W) + (SEM_SPEC, SEM_SPEC) + (ANY_SPEC,) * len(after),
        out_specs=(HBM_SPEC,) * (2 * W),
        input_output_aliases={i: i for i in range(2 * W)},
        compiler_params=pltpu.CompilerParams(has_side_effects=SPLIT_EFFECT),
    )(*arrays, *lands, send_sems, recv_sems, *after)
    return list(out[:W]), list(out[W:])


def _scatter_copies(w, p_ref, land_ref):
    x, y, c = _my_place()
    my_chip = 2 * x + y
    return [(p_ref.at[2 * (x ^ (k >> 1)) + (y ^ (k & 1))], land_ref.at[my_chip], (x ^ (k >> 1), y ^ (k & 1), c))
            for k in range(1, 4)]


def _gather_copies(w, x_ref, land_ref):
    x, y, c = _my_place()
    me = _linear((x, y, c))
    devs = [(x, y, 1 - c)] + [(x ^ (k >> 1), y ^ (k & 1), c) for k in range(1, 4)]
    return [(x_ref, land_ref.at[me], d) for d in devs]


def _gather_forward(lands, name):
    W = len(lands)

    def body(*refs):
        land_refs, out_refs, (send_sems, recv_sems) = refs[:W], refs[W:2 * W], refs[2 * W:]
        x, y, c = _my_place()
        sibling = (x, y, 1 - c)
        sends, arrivals = [], []
        for w in range(W):
            for k in range(1, 4):
                px, py = x ^ (k >> 1), y ^ (k & 1)
                landed, theirs = _linear((px, py, c)), out_refs[w].at[_linear((px, py, 1 - c))]
                sem = 3 * w + k - 1
                sends.append(pltpu.make_async_remote_copy(
                    src_ref=land_refs[w].at[landed], dst_ref=out_refs[w].at[landed],
                    send_sem=send_sems.at[sem], recv_sem=recv_sems.at[sem], device_id=sibling, device_id_type=MESH_ID))
                arrivals.append(pltpu.make_async_remote_copy(
                    src_ref=theirs, dst_ref=theirs, send_sem=send_sems.at[sem], recv_sem=recv_sems.at[sem],
                    device_id=sibling, device_id_type=MESH_ID))
        for cp in sends:
            cp.start()
        for cp in arrivals:
            cp.wait_recv()
        for cp in sends:
            cp.wait_send()

    return pl.pallas_call(
        body, name=name,
        in_specs=[HBM_SPEC] * W, out_specs=[HBM_SPEC] * W,
        out_shape=[jax.ShapeDtypeStruct(l.shape, l.dtype) for l in lands],
        input_output_aliases={i: i for i in range(W)},
        scratch_shapes=[pltpu.SemaphoreType.DMA((3 * W,)), pltpu.SemaphoreType.DMA((3 * W,))],
    )(*lands)


def _with_own_slot(gathered, shard):
    return lax.dynamic_update_index_in_dim(gathered, shard[None], _linear(_my_place()), axis=0)


def _in_chip_copies(w, p_ref, land_ref):
    x, y, c = _my_place()
    return [(p_ref.at[2 * q + (1 - c)], land_ref.at[q], (x, y, 1 - c)) for q in range(4)]


def _in_chip_start(parts, tag):
    lands = [_landing_zone((4,) + p.shape[1:], p.dtype) for p in parts]
    return _split_start("grad_in_chip_start_" + tag, parts, lands, (), _in_chip_copies, 4)


def _reduce_scatter_begin(parts, tag, in_chip_state=None, after=()):
    if in_chip_state is None:
        got = _exchange_in_chip(parts)
    else:
        parts, got = _split_wait("grad_in_chip_wait_" + tag, in_chip_state, after, _in_chip_copies)
    core = lax.axis_index("c").astype(jnp.int32).reshape(1)
    chip_parts = [_pair_sum(p, g, core) for p, g in zip(parts, got)]
    lands = [_landing_zone(p.shape, p.dtype) for p in chip_parts]
    return _split_start("grad_scatter_start_" + tag, chip_parts, lands, got[0], _scatter_copies, 3)


def _reduce_scatter_end(state, after, tag):
    return _split_wait("grad_scatter_wait_" + tag, state, after, _scatter_copies)


def kernel(x, c, positions, w_ada, b_ada, w_in, g_q_a, w_q_b, g_kv_a, w_kv_b, w_o_a, w_conv, w_o_b, w_o, ln1_g, ln1_b, w_ffn_in, w_ffn_out, ln2_g, ln2_b, loss_target, m_w_ada, m_b_ada, m_w_in, m_g_q_a, m_w_q_b, m_g_kv_a, m_w_kv_b, m_w_o_a, m_w_conv, m_w_o_b, m_w_o, m_ln1_g, m_ln1_b, m_w_ffn_in, m_w_ffn_out, m_ln2_g, m_ln2_b, v_w_ada, v_b_ada, v_w_in, v_g_q_a, v_w_q_b, v_g_kv_a, v_w_kv_b, v_w_o_a, v_w_conv, v_w_o_b, v_w_o, v_ln1_g, v_ln1_b, v_w_ffn_in, v_w_ffn_out, v_ln2_g, v_ln2_b):
    x2, tgt = x[0], loss_target[0]
    S, D = x2.shape
    Lq, Lkv = g_q_a.shape[1], g_kv_a.shape[1]
    H = w_q_b.shape[2] * N_DEV // QK_CAT
    F = w_ffn_out.shape[1] * N_DEV
    assert Lq == Lkv and (Lq + Lkv) % COL_BLOCK == 0 and D % COL_BLOCK == 0
    front = Lq + Lkv + QK_ROPE
    front_pad = _round_up(front, COL_BLOCK)
    kr_blk = (Lq + Lkv) // COL_BLOCK
    blk_b = front_pad // COL_BLOCK
    nblk = D // COL_BLOCK
    blk_c, blk_x, blk_ga, blk_gb = blk_b + nblk, blk_b + 2 * nblk, blk_b + 3 * nblk, blk_b + 4 * nblk
    ts = _tile(S, 256, 8)
    T = _tile(S, min(512, S // 2), CHUNK)
    tb = _tile(F, 2816)
    me = _linear(_my_place())

    cw = w_ada.shape[2]
    b_mine = lax.dynamic_slice(b_ada, (0, me * cw), (1, cw)).reshape(1, 1, cw)
    mod_blocks, cact_all, wconv_all = _ada_fwd(c.reshape(1, 1, D), w_conv[0].reshape(1, 1, -1), w_ada[0], b_mine)
    mod = mod_blocks.reshape(6, D)
    cact_all = cact_all.reshape(N_DEV, D)
    w_conv_full = wconv_all.reshape(N_DEV, CONV_K, -1).transpose(1, 0, 2).reshape(CONV_K, D)

    landing = lambda shards: [_landing_zone((N_DEV,) + s.shape, BF16) for s in shards]
    gathered = lambda lands, shards, tag: [_with_own_slot(g, s) for g, s in
                                           zip(_gather_forward(lands, tag + "_gather_forward"), shards)]
    half = D // 2
    w_in_b = w_in[0].astype(BF16)
    first, second = [w_in_b[:half]], [w_in_b[half:], w_q_b[0].astype(BF16), w_kv_b[0].astype(BF16)]
    mid = [w[0].astype(BF16) for w in (w_o_a, w_o_b, w_o)]
    last = [w[0].astype(BF16) for w in (w_ffn_in, w_ffn_out)]
    first_state = _split_start("first_gather_start", first, landing(first), mod_blocks, _gather_copies, 4)
    second_state = _split_start("second_gather_start", second, landing(second), first_state[4], _gather_copies, 4)
    u = _modulate_in(x2, mod, ts)

    first_shards, first_lands = _split_wait("first_gather_wait", first_state, (u, second_state[4]), _gather_copies)
    (g_in_top,) = gathered(first_lands, first_shards, "first")
    w_in_top = _assemble_w_in(g_in_top, front, front_pad, D, 0)
    proj_top = _matmul(u, w_in_top, "nn", F32, "proj_top", k_rows=(0, half))
    second_shards, second_lands = _split_wait("second_gather_wait", second_state, (proj_top,), _gather_copies)
    g_in_bottom, wq_s, wkv_s = gathered(second_lands, second_shards, "second")
    mid_state = _split_start("mid_gather_start", mid, landing(mid), g_in_bottom, _gather_copies, 4)
    last_state = _split_start("last_gather_start", last, landing(last), mid_state[4], _gather_copies, 4)
    w_in_p = _assemble_w_in(g_in_bottom, front, front_pad, D, half, into=w_in_top)

    inv_freq = 1.0 / (ROPE_THETA ** (jnp.arange(0, QK_ROPE, 2, dtype=F32) / QK_ROPE))
    ang = positions[0].astype(F32)[:, None] * inv_freq
    cos2 = jnp.concatenate([jnp.cos(ang), jnp.cos(ang)], axis=-1)
    sin2 = jnp.concatenate([jnp.sin(ang), jnp.sin(ang)], axis=-1)
    one, zero = jnp.ones((S, QK_NOPE), F32), jnp.zeros((S, QK_NOPE), F32)
    cos_q, sin_q = jnp.concatenate([one, cos2, one, cos2], axis=-1), jnp.concatenate([zero, sin2, zero, sin2], axis=-1)
    cos_k, sin_k = jnp.tile(cos2, (1, COL_BLOCK // QK_ROPE)), jnp.tile(sin2, (1, COL_BLOCK // QK_ROPE))

    proj = _matmul(u, w_in_p, "nn", BF16, "proj", k_rows=(half, half), init=proj_top, deps=(last_state[4],))
    qn = _rms_fwd(proj, g_q_a, 0, Lq, ts, "rms_q")
    kvn = _rms_fwd(proj, g_kv_a, 1, Lkv, ts, "rms_kv")
    q = _matmul(qn, wq_s, "nn", BF16, "q_up")
    kv = _matmul(kvn, wkv_s, "nn", BF16, "kv_up")
    qc, kc, vh = _qk_prep(q, kv, proj, kr_blk, cos_q, sin_q, cos_k, sin_k, H, ts)
    attn, lse = _attn_fwd(qc, kc, vh, T)
    mid_shards, mid_lands = _split_wait("mid_gather_wait", mid_state, lse, _gather_copies)
    w_oa_f, w_ob_f, w_o_f = [g.reshape(-1, D) for g in gathered(mid_lands, mid_shards, "mid")]
    ya = _matmul(attn, w_oa_f, "nn", F32, "attn_out")
    cbc = _conv_fwd(proj, w_conv_full, blk_b, blk_c, blk_x)
    yb = _matmul(cbc, w_ob_f, "nn", F32, "conv_out")
    merged = _merge_fwd(proj, ya, yb, blk_ga, blk_gb, ts)
    mix = _matmul(merged, w_o_f, "nn", F32, "mix_out")
    xhat1, rstd1, u2 = _ln1_fwd(x2, mix, mod, ln1_g, ln1_b, ts)
    last_shards, last_lands = _split_wait("last_gather_wait", last_state, u2, _gather_copies)
    w_fi_s, g_fo = gathered(last_lands, last_shards, "last")
    w_fo_f = g_fo.reshape(F, D)
    hh = _matmul(u2, w_fi_s, "nn", BF16, "ffn_in")
    act = _swiglu_fwd(hh, ts, tb)
    ffn = _matmul(act, w_fo_f, "nn", F32, "ffn_out")
    loss_part, dffn, dx1a, vec2 = _ln2_loss(xhat1, ffn, tgt, mod, ln1_g, ln1_b, ln2_g, ln2_b, ts)
    loss = lax.psum(loss_part[0, 0], AXES)

    gw_fo = _matmul(act, dffn, "tn", BF16, "grad_w_ffn_out")
    da = _matmul(dffn, w_fo_f, "nt", BF16, "d_act")
    dh = _swiglu_bwd(da, hh, ts, tb)
    gw_fi = _matmul(u2, dh, "tn", BF16, "grad_w_ffn_in", out_shards=True)
    ffn_in_chip = _in_chip_start([gw_fi, gw_fo.reshape(N_DEV, -1, D)], "ffn")
    du2 = _matmul(dh, w_fi_s, "nt", F32, "d_u2", deps=(ffn_in_chip[4],))
    ffn_state = _reduce_scatter_begin(None, "ffn", ffn_in_chip, after=(du2,))
    dxa, dmix, vec1 = _ln1_bwd(du2, dx1a, xhat1, rstd1, mix, mod, ln1_g, ln1_b, ts)
    gw_o = _matmul(merged, dmix, "tn", BF16, "grad_w_o", deps=(ffn_state[4],))
    dmerged = _matmul(dmix, w_o_f, "nt", F32, "d_merged")
    dya, dyb, dga, dgb = _merge_bwd(dmerged, proj, ya, yb, blk_ga, blk_gb, ts)
    gw_ob = _matmul(cbc, dyb, "tn", BF16, "grad_w_o_b")
    dcbc = _matmul(dyb, w_ob_f, "nt", F32, "d_conv")
    dcb, dcc, dcx, dwconv = _conv_bwd(dcbc, proj, w_conv_full, blk_b, blk_c, blk_x)
    gw_oa = _matmul(attn, dya, "tn", BF16, "grad_w_o_a")
    mix_in_chip = _in_chip_start([g.reshape(N_DEV, -1, D) for g in (gw_oa, gw_ob, gw_o)], "mix")
    dattn = _matmul(dya, w_oa_f, "nt", F32, "d_attn", deps=(mix_in_chip[4],))
    dqc, dkc, dvh = _attn_bwd(qc, kc, vh, dattn, attn, lse, T)
    ffn_own, ffn_got = _reduce_scatter_end(ffn_state, dqc, "ffn")
    mix_state = _reduce_scatter_begin(None, "mix", mix_in_chip, after=(dqc,))
    dq, dkv, dkr = _qk_bwd(dqc, dkc, dvh, cos_q, sin_q, cos_k, sin_k, ts)
    gw_qb = _matmul(qn, dq, "tn", BF16, "grad_w_q_b", out_shards=True, deps=(mix_state[4],))
    dqn = _matmul(dq, wq_s, "nt", F32, "d_qn")
    gw_kvb = _matmul(kvn, dkv, "tn", BF16, "grad_w_kv_b", out_shards=True)
    dkvn = _matmul(dkv, wkv_s, "nt", F32, "d_kvn")
    dqa, dgq = _rms_bwd(dqn, proj, g_q_a, 0, Lq, ts, "rms_q_bwd")
    dkva, dgkv = _rms_bwd(dkvn, proj, g_kv_a, 1, Lkv, ts, "rms_kv_bwd")
    dproj = jnp.concatenate([dqa, dkva, dkr, dcb, dcc, dcx, dga, dgb], axis=1)
    gw_in_p = _matmul(u, dproj, "tn", BF16, "grad_w_in")
    mix_own, mix_got = _reduce_scatter_end(mix_state, gw_in_p, "mix")
    in_state = _reduce_scatter_begin([_split_w_in(gw_in_p, front, front_pad), gw_qb, gw_kvb], "in")
    du = _matmul(dproj, w_in_p, "nt", F32, "d_u", deps=(in_state[4],))
    grad_x, vec0 = _grad_x(du, dxa, x2, mod, ts)

    my_chip = (2 * lax.axis_index("x") + lax.axis_index("y")).astype(jnp.int32).reshape(1)
    arrived = {}
    for nm, w, m, v, own, got in (
            ("w_ffn_in", w_ffn_in, m_w_ffn_in, v_w_ffn_in, ffn_own[0], ffn_got[0]),
            ("w_ffn_out", w_ffn_out, m_w_ffn_out, v_w_ffn_out, ffn_own[1], ffn_got[1]),
            ("w_o_a", w_o_a, m_w_o_a, v_w_o_a, mix_own[0], mix_got[0]),
            ("w_o_b", w_o_b, m_w_o_b, v_w_o_b, mix_own[1], mix_got[1]),
            ("w_o", w_o, m_w_o, v_w_o, mix_own[2], mix_got[2])):
        arrived[nm] = [a[None] for a in _adamw_reduced(w[0], own, got, m[0], v[0], my_chip, "adamw_" + nm)]

    dmod = jnp.concatenate([vec0[0], vec0[1], vec1[4], vec1[0], vec1[1], vec2[2]])
    small = jnp.concatenate([dmod, dgq[0], dgkv[0], vec1[2], vec1[3], vec2[0], vec2[1], dwconv[:CONV_K].reshape(-1)])
    n_small = small.shape[0]
    nch = _round_up(n_small, cw) // cw
    payload = jnp.pad(small, (0, nch * cw - n_small)).reshape(nch, 1, cw)
    summed, dmod_mine = _ada_bwd(payload, deps=[res[1] for res in arrived.values()])
    arrived["w_ada"] = [a[None] for a in _adamw_ada(w_ada[0], cact_all.T, dmod_mine.reshape(N_DEV, cw),
                                                    m_w_ada[0], v_w_ada[0])]
    summed = summed.reshape(-1)
    offs = [0, 6 * D, 6 * D + Lq, 6 * D + Lq + Lkv]
    offs += [offs[-1] + D * k for k in range(1, 5)]
    g_b_ada = summed[offs[0]:offs[1]].reshape(1, -1)
    g_gq = summed[offs[1]:offs[2]].reshape(1, -1)
    g_gkv = summed[offs[2]:offs[3]].reshape(1, -1)
    g_ln1g, g_ln1b, g_ln2g, g_ln2b = [summed[offs[3 + k]:offs[4 + k]].reshape(1, -1) for k in range(4)]
    wc = w_conv.shape[2]
    g_wconv = lax.dynamic_slice(summed[offs[7]:offs[7] + CONV_K * D].reshape(CONV_K, D), (0, me * wc), (CONV_K, wc))

    names = ["w_ada", "b_ada", "w_in", "g_q_a", "w_q_b", "g_kv_a", "w_kv_b", "w_o_a", "w_conv", "w_o_b", "w_o",
             "ln1_g", "ln1_b", "w_ffn_in", "w_ffn_out", "ln2_g", "ln2_b"]
    weights = [w_ada, b_ada, w_in, g_q_a, w_q_b, g_kv_a, w_kv_b, w_o_a, w_conv, w_o_b, w_o, ln1_g, ln1_b,
               w_ffn_in, w_ffn_out, ln2_g, ln2_b]
    moms = [m_w_ada, m_b_ada, m_w_in, m_g_q_a, m_w_q_b, m_g_kv_a, m_w_kv_b, m_w_o_a, m_w_conv, m_w_o_b, m_w_o,
            m_ln1_g, m_ln1_b, m_w_ffn_in, m_w_ffn_out, m_ln2_g, m_ln2_b]
    vels = [v_w_ada, v_b_ada, v_w_in, v_g_q_a, v_w_q_b, v_g_kv_a, v_w_kv_b, v_w_o_a, v_w_conv, v_w_o_b, v_w_o,
            v_ln1_g, v_ln1_b, v_w_ffn_in, v_w_ffn_out, v_ln2_g, v_ln2_b]
    grad_of = {"b_ada": g_b_ada, "g_q_a": g_gq, "g_kv_a": g_gkv, "w_conv": g_wconv,
               "ln1_g": g_ln1g, "ln1_b": g_ln1b, "ln2_g": g_ln2g, "ln2_b": g_ln2b}
    state_of = dict(zip(names, zip(weights, moms, vels)))
    results = dict(arrived)

    def update(nm, reduced=None):
        w, m, v = state_of[nm]
        shp = w.shape
        w2 = w.reshape(shp[-2], shp[-1]) if w.ndim == 3 else w
        m2, v2 = m.reshape(w2.shape), v.reshape(w2.shape)
        if reduced is None:
            g2 = grad_of[nm].reshape(w2.shape)
            res = (g2,) + tuple(_adamw(w2, g2, m2, v2, "adamw_" + nm))
        else:
            res = _adamw_reduced(w2, reduced[0], reduced[1], m2, v2, my_chip, "adamw_" + nm)
        results[nm] = [a.reshape(shp) for a in res]

    for nm in grad_of:
        update(nm)
    in_own, in_got = _reduce_scatter_end(in_state, [res[1] for res in results.values()], "in")
    for nm, own, got in zip(("w_in", "w_q_b", "w_kv_b"), in_own, in_got):
        update(nm, (own, got))
    outs = [[results[nm][k] for nm in names] for k in range(4)]
    return (loss, grad_x.reshape(x.shape), *outs[0], *outs[1], *outs[2], *outs[3])
```

```python
import functools

import jax
import jax.numpy as jnp
from jax import lax
from jax.experimental import pallas as pl
from jax.experimental.pallas import tpu as pltpu

F32 = jnp.float32
BF16 = jnp.bfloat16
MESH_ID = pl.DeviceIdType.MESH
AXES = ("x", "y", "c")
N_DEV = 8

CHUNK = 64
QK_NOPE = 128
QK_ROPE = 64
V_HEAD = 128
QK_CAT = QK_NOPE + QK_ROPE
ROPE_THETA = 10000.0
ATTN_SCALE = (QK_NOPE + QK_ROPE) ** -0.5
CONV_K = 3
DEEPNORM_ALPHA = 2.0 ** 0.25
LN_EPS = 1e-5
RMS_EPS = 1e-6
NEG_INF = -1e30

ADAM_LR = 0.001
ADAM_B1 = 0.9
ADAM_B2 = 0.999
ADAM_EPS = 1e-08
ADAM_WD = 0.01
ADAM_STEP = 10

LANE = 128
COL_BLOCK = 256
PACK_ROW_ALIGN = 16
PAIR_SUM_BLOCK = 1 << 20
VMEM_LIMIT = 48 * 1024 * 1024


def _round_up(n, m):
    return (n + m - 1) // m * m


def _tile(n, pref, align=LANE):
    best = None
    t = align
    while t <= min(n, pref):
        if n % t == 0:
            best = t
        t += align
    return best if best is not None else n


def _cparams(sem=None):
    return pltpu.CompilerParams(dimension_semantics=sem, vmem_limit_bytes=VMEM_LIMIT)


def _sigmoid(x):
    return 0.5 * jnp.tanh(0.5 * x) + 0.5


def _matmul(a, b, mode, out_dtype, name, tm=1024, tn=1024, tk=2048, deps=(), out_shards=False, k_rows=None,
            init=None):
    b_shards = b.ndim == 3
    n = b.shape[2] if b_shards else (b.shape[1] // N_DEV if out_shards else None)
    if mode == "nn":
        (M, K), (K2, N) = a.shape, (b.shape[1], N_DEV * n) if b_shards else b.shape
    elif mode == "nt":
        (M, K), (N, K2) = a.shape, (b.shape[1], N_DEV * n) if b_shards else b.shape
    else:
        (K, M), (K2, N) = a.shape, b.shape
    assert K == K2, (a.shape, b.shape, mode)
    tm = _tile(M, tm)
    tn = n if (mode != "nt" and n is not None) else _tile(N, tn)
    k_row0, k_len = k_rows if k_rows is not None else (0, K)
    tk = n if (mode == "nt" and b_shards) else _tile(k_len, tk)
    nk, k0 = k_len // tk, k_row0 // tk
    if mode == "nn":
        a_spec = pl.BlockSpec((tm, tk), lambda i, j, k: (i, k0 + k))
        b_spec = (pl.BlockSpec((1, tk, n), lambda i, j, k: (j, k, 0)) if b_shards
                  else pl.BlockSpec((tk, tn), lambda i, j, k: (k0 + k, j)))
        dims = (((1,), (0,)), ((), ()))
    elif mode == "nt":
        a_spec = pl.BlockSpec((tm, tk), lambda i, j, k: (i, k))
        b_spec = (pl.BlockSpec((1, tn, n), lambda i, j, k: (k, j, 0)) if b_shards
                  else pl.BlockSpec((tn, tk), lambda i, j, k: (j, k)))
        dims = (((1,), (1,)), ((), ()))
    else:
        a_spec = pl.BlockSpec((tk, tm), lambda i, j, k: (k, i))
        b_spec = pl.BlockSpec((tk, tn), lambda i, j, k: (k, j))
        dims = (((0,), (0,)), ((), ()))
    if out_shards:
        out_spec = pl.BlockSpec((1, tm, n), lambda i, j, k: (j, i, 0))
        out_shape = jax.ShapeDtypeStruct((N_DEV, M, n), out_dtype)
    else:
        out_spec = pl.BlockSpec((tm, tn), lambda i, j, k: (i, j))
        out_shape = jax.ShapeDtypeStruct((M, N), out_dtype)

    def product(a_ref, b_ref):
        b_blk = b_ref[0] if b_shards else b_ref[...]
        return lax.dot_general(a_ref[...].astype(BF16), b_blk.astype(BF16), dims, preferred_element_type=F32)

    def write(o_ref, value):
        if out_shards:
            o_ref[0] = value.astype(o_ref.dtype)
        else:
            o_ref[...] = value.astype(o_ref.dtype)

    def body_whole_k(a_ref, b_ref, *rest):
        value = product(a_ref, b_ref)
        write(rest[-1], value if init is None else value + rest[0][...])

    def body_split_k(a_ref, b_ref, *rest):
        o_ref, acc_ref = rest[-2:]
        k = pl.program_id(2)

        @pl.when(k == 0)
        def _():
            acc_ref[...] = jnp.zeros_like(acc_ref) if init is None else rest[0][...]

        acc_ref[...] += product(a_ref, b_ref)

        @pl.when(k == nk - 1)
        def _():
            write(o_ref, acc_ref[...])

    return pl.pallas_call(
        body_whole_k if nk == 1 else body_split_k, name=name, grid=(M // tm, N // tn, nk),
        in_specs=[a_spec, b_spec] + ([] if init is None else [out_spec]) + [ANY_SPEC] * len(deps),
        out_specs=out_spec, out_shape=out_shape,
        scratch_shapes=[] if nk == 1 else [pltpu.VMEM((tm, tn), F32)],
        compiler_params=_cparams(("parallel", "parallel", "arbitrary")),
    )(a, b, *(() if init is None else (init,)), *deps)


def _assemble_w_in(shards, front, front_pad, rows, row0, into=None):
    _, K, n = shards.shape
    gap = front_pad - front
    tk = _tile(K, 256, PACK_ROW_ALIGN)
    blk0 = row0 // tk

    def body(g_ref, *rest):
        o_ref = rest[-1]
        if gap:
            o_ref[:, front:front_pad] = jnp.zeros((tk, gap), o_ref.dtype)
        for j in range(N_DEV):
            lo, hi = j * n, (j + 1) * n
            if lo < front < hi:
                o_ref[:, lo:front] = g_ref[j, :, 0:front - lo]
                o_ref[:, front_pad:hi + gap] = g_ref[j, :, front - lo:n]
            else:
                off = 0 if hi <= front else gap
                o_ref[:, lo + off:hi + off] = g_ref[j]

    return pl.pallas_call(
        body, name="assemble_w_in", grid=(K // tk,),
        in_specs=[pl.BlockSpec((N_DEV, tk, n), lambda i: (0, i, 0))] + ([] if into is None else [ANY_SPEC]),
        out_specs=pl.BlockSpec((tk, N_DEV * n + gap), lambda i: (blk0 + i, 0)),
        out_shape=jax.ShapeDtypeStruct((rows, N_DEV * n + gap), shards.dtype),
        input_output_aliases={} if into is None else {1: 0},
        compiler_params=_cparams(("parallel",)),
    )(*([shards] if into is None else [shards, into]))


def _split_w_in(w, front, front_pad):
    K, NP = w.shape
    gap = front_pad - front
    n = (NP - gap) // N_DEV
    tk = _tile(K, 256, PACK_ROW_ALIGN)

    def body(w_ref, o_ref):
        for j in range(N_DEV):
            lo, hi = j * n, (j + 1) * n
            if lo < front < hi:
                o_ref[j, :, 0:front - lo] = w_ref[:, lo:front]
                o_ref[j, :, front - lo:n] = w_ref[:, front_pad:hi + gap]
            else:
                off = 0 if hi <= front else gap
                o_ref[j] = w_ref[:, lo + off:hi + off]

    return pl.pallas_call(
        body, name="split_grad_w_in", grid=(K // tk,),
        in_specs=[pl.BlockSpec((tk, NP), lambda i: (i, 0))],
        out_specs=pl.BlockSpec((N_DEV, tk, n), lambda i: (0, i, 0)),
        out_shape=jax.ShapeDtypeStruct((N_DEV, K, n), w.dtype),
        compiler_params=_cparams(("parallel",)),
    )(w)


def _modulate_in(x, mod, ts):
    S, D = x.shape

    def body(x_ref, mod_ref, u_ref):
        u_ref[...] = (x_ref[...] * (1.0 + mod_ref[1:2, :]) + mod_ref[0:1, :]).astype(BF16)

    return pl.pallas_call(
        body, name="modulate_in", grid=(S // ts,),
        in_specs=[pl.BlockSpec((ts, D), lambda i: (i, 0)), pl.BlockSpec((6, D), lambda i: (0, 0))],
        out_specs=pl.BlockSpec((ts, D), lambda i: (i, 0)),
        out_shape=jax.ShapeDtypeStruct((S, D), BF16),
        compiler_params=_cparams(("parallel",)),
    )(x, mod)


def _rms_fwd(proj, g, blk, L, ts, name):
    S = proj.shape[0]

    def body(a_ref, g_ref, y_ref):
        a = a_ref[...].astype(F32)
        r = lax.rsqrt(jnp.mean(a * a, axis=-1, keepdims=True) + RMS_EPS)
        y_ref[...] = (a * r * g_ref[...]).astype(BF16)

    return pl.pallas_call(
        body, name=name, grid=(S // ts,),
        in_specs=[pl.BlockSpec((ts, L), lambda i: (i, blk)), pl.BlockSpec((1, L), lambda i: (0, 0))],
        out_specs=pl.BlockSpec((ts, L), lambda i: (i, 0)),
        out_shape=jax.ShapeDtypeStruct((S, L), BF16),
        compiler_params=_cparams(("parallel",)),
    )(proj, g)


def _rope_partner(x, period, start):
    w = x.shape[-1]
    lane = lax.broadcasted_iota(jnp.int32, x.shape, x.ndim - 1) % period
    first = (lane >= start) & (lane < start + QK_ROPE // 2)
    from_right = pltpu.roll(x, w - QK_ROPE // 2, axis=x.ndim - 1)
    from_left = pltpu.roll(x, QK_ROPE // 2, axis=x.ndim - 1)
    return jnp.where(first, -from_right, from_left)


def _qk_prep(q, kv, proj, kr_blk, cos_q, sin_q, cos_k, sin_k, H, ts):
    S = q.shape[0]
    pair = 2 * QK_CAT
    kv_w = QK_NOPE + V_HEAD

    def body(q_ref, kv_ref, kr_ref, cq_ref, sq_ref, ck_ref, sk_ref, qc_ref, kc_ref, vh_ref):
        kr = kr_ref[...].astype(F32)
        kr = kr * ck_ref[...] + _rope_partner(kr, QK_ROPE, 0) * sk_ref[...]
        kr = kr[:, :QK_ROPE].astype(BF16)
        for p in range(H // 2):
            x = q_ref[:, p * pair:(p + 1) * pair].astype(F32)
            x = x * cq_ref[...] + _rope_partner(x, QK_CAT, QK_NOPE) * sq_ref[...]
            qc_ref[2 * p] = x[:, :QK_CAT].astype(BF16)
            qc_ref[2 * p + 1] = x[:, QK_CAT:].astype(BF16)
        for h in range(H):
            kc_ref[h, :, 0:QK_NOPE] = kv_ref[:, h * kv_w:h * kv_w + QK_NOPE].astype(BF16)
            kc_ref[h, :, QK_NOPE:QK_CAT] = kr
            vh_ref[h, :, :] = kv_ref[:, h * kv_w + QK_NOPE:(h + 1) * kv_w].astype(BF16)

    row = lambda w: pl.BlockSpec((ts, w), lambda i: (i, 0))
    return pl.pallas_call(
        body, name="qk_prep", grid=(S // ts,),
        in_specs=[row(H * QK_CAT), row(H * kv_w),
                  pl.BlockSpec((ts, COL_BLOCK), lambda i: (i, kr_blk)),
                  row(pair), row(pair), row(COL_BLOCK), row(COL_BLOCK)],
        out_specs=[pl.BlockSpec((H, ts, QK_CAT), lambda i: (0, i, 0)),
                   pl.BlockSpec((H, ts, QK_CAT), lambda i: (0, i, 0)),
                   pl.BlockSpec((H, ts, V_HEAD), lambda i: (0, i, 0))],
        out_shape=[jax.ShapeDtypeStruct((H, S, QK_CAT), BF16), jax.ShapeDtypeStruct((H, S, QK_CAT), BF16),
                   jax.ShapeDtypeStruct((H, S, V_HEAD), BF16)],
        compiler_params=_cparams(("parallel",)),
    )(q, kv, proj, cos_q, sin_q, cos_k, sin_k)


NT_DIMS = (((1,), (1,)), ((), ()))
TN_DIMS = (((0,), (0,)), ((), ()))


def _diag_mask(T):
    rows = lax.broadcasted_iota(jnp.int32, (T, T), 0) // CHUNK
    cols = lax.broadcasted_iota(jnp.int32, (T, T), 1) // CHUNK
    return cols <= rows


def _attn_fwd(qc, kc, vh, T):
    H, S, _ = qc.shape
    n = S // T

    def body(q_ref, k_ref, v_ref, o_ref, lse_ref):
        q = q_ref[0]

        def block(i):
            L = (i + 1) * T
            s_old = lax.dot_general(q, k_ref[0, 0:i * T, :], NT_DIMS, preferred_element_type=F32) if i else None
            s_diag = lax.dot_general(q, k_ref[0, i * T:L, :], NT_DIMS, preferred_element_type=F32)
            s_diag = jnp.where(_diag_mask(T), s_diag, NEG_INF)
            m = jnp.max(s_diag, axis=-1, keepdims=True)
            if i:
                m = jnp.maximum(m, jnp.max(s_old, axis=-1, keepdims=True))
            p_diag = jnp.exp((s_diag - m) * ATTN_SCALE)
            l = jnp.sum(p_diag, axis=-1, keepdims=True)
            acc = jnp.dot(p_diag.astype(BF16), v_ref[0, i * T:L, :], preferred_element_type=F32)
            if i:
                p_old = jnp.exp((s_old - m) * ATTN_SCALE)
                l = l + jnp.sum(p_old, axis=-1, keepdims=True)
                acc = acc + jnp.dot(p_old.astype(BF16), v_ref[0, 0:i * T, :], preferred_element_type=F32)
            o_ref[...] = acc / l
            lse_ref[0] = m * ATTN_SCALE + jnp.log(l)

        for i in range(n):
            pl.when(pl.program_id(1) == i)(functools.partial(block, i))

    return pl.pallas_call(
        body, name="attn_fwd", grid=(H, n),
        in_specs=[pl.BlockSpec((1, T, QK_CAT), lambda h, i: (h, i, 0)),
                  pl.BlockSpec((1, S, QK_CAT), lambda h, i: (h, 0, 0)),
                  pl.BlockSpec((1, S, V_HEAD), lambda h, i: (h, 0, 0))],
        out_specs=[pl.BlockSpec((T, V_HEAD), lambda h, i: (i, h)),
                   pl.BlockSpec((1, T, 1), lambda h, i: (h, i, 0))],
        out_shape=[jax.ShapeDtypeStruct((S, H * V_HEAD), F32), jax.ShapeDtypeStruct((H, S, 1), F32)],
        compiler_params=_cparams(("parallel", "arbitrary")),
    )(qc, kc, vh)


def _shift_rows(z, k):
    if k == 0:
        return z
    n = z.shape[0]
    row = lax.broadcasted_iota(jnp.int32, z.shape, 0)
    if k > 0:
        return jnp.where(row >= k, pltpu.roll(z, k, axis=0), 0.0)
    return jnp.where(row < n + k, pltpu.roll(z, n + k, axis=0), 0.0)


def _conv_fwd(proj, w_conv, blk_b, blk_c, blk_x):
    S = proj.shape[0]
    D = w_conv.shape[1]
    nb = D // COL_BLOCK

    def body(cb_ref, cc_ref, cx_ref, w_ref, o_ref):
        z = cc_ref[...].astype(F32) * cx_ref[...].astype(F32)
        conv = w_ref[2:3, :] * z + w_ref[1:2, :] * _shift_rows(z, 1) + w_ref[0:1, :] * _shift_rows(z, 2)
        o_ref[...] = (cb_ref[...].astype(F32) * conv).astype(BF16)

    col = lambda off: pl.BlockSpec((S, COL_BLOCK), lambda j: (0, off + j))
    return pl.pallas_call(
        body, name="conv_fwd", grid=(nb,),
        in_specs=[col(blk_b), col(blk_c), col(blk_x), pl.BlockSpec((CONV_K, COL_BLOCK), lambda j: (0, j))],
        out_specs=pl.BlockSpec((S, COL_BLOCK), lambda j: (0, j)),
        out_shape=jax.ShapeDtypeStruct((S, D), BF16),
        compiler_params=_cparams(("parallel",)),
    )(proj, proj, proj, w_conv)


def _merge_fwd(proj, ya, yb, blk_ga, blk_gb, ts):
    S, D = ya.shape
    nb = D // COL_BLOCK

    def body(ga_ref, gb_ref, ya_ref, yb_ref, o_ref):
        sa, sb = _sigmoid(ga_ref[...].astype(F32)), _sigmoid(gb_ref[...].astype(F32))
        o_ref[...] = (sa * ya_ref[...].astype(F32) + sb * yb_ref[...].astype(F32)).astype(BF16)

    row = pl.BlockSpec((ts, D), lambda i: (i, 0))
    seg = lambda blk: pl.BlockSpec((pl.Element(ts), pl.Element(D)), lambda i: (i * ts, blk * COL_BLOCK))
    return pl.pallas_call(
        body, name="merge_fwd", grid=(S // ts,),
        in_specs=[seg(blk_ga), seg(blk_gb), row, row],
        out_specs=row,
        out_shape=jax.ShapeDtypeStruct((S, D), BF16),
        compiler_params=_cparams(("parallel",)),
    )(proj, proj, ya, yb)


def _ln1_fwd(x, mix, mod, g, b, ts):
    S, D = x.shape

    def body(x_ref, mix_ref, mod_ref, g_ref, b_ref, xhat_ref, rstd_ref, u2_ref):
        r = DEEPNORM_ALPHA * x_ref[...] + mod_ref[2:3, :] * mix_ref[...]
        mu = jnp.mean(r, axis=-1, keepdims=True)
        d = r - mu
        rstd = lax.rsqrt(jnp.mean(d * d, axis=-1, keepdims=True) + LN_EPS)
        xhat = d * rstd
        xhat_ref[...] = xhat
        rstd_ref[...] = rstd
        x1 = xhat * g_ref[...] + b_ref[...]
        u2_ref[...] = (x1 * (1.0 + mod_ref[4:5, :]) + mod_ref[3:4, :]).astype(BF16)

    row = pl.BlockSpec((ts, D), lambda i: (i, 0))
    vec = lambda r: pl.BlockSpec((r, D), lambda i: (0, 0))
    return pl.pallas_call(
        body, name="ln1_fwd", grid=(S // ts,),
        in_specs=[row, row, vec(6), vec(1), vec(1)],
        out_specs=[row, pl.BlockSpec((ts, 1), lambda i: (i, 0)), row],
        out_shape=[jax.ShapeDtypeStruct((S, D), F32), jax.ShapeDtypeStruct((S, 1), F32),
                   jax.ShapeDtypeStruct((S, D), BF16)],
        compiler_params=_cparams(("parallel",)),
    )(x, mix, mod, g, b)


def _swiglu_fwd(h, ts, tb):
    S, F2 = h.shape
    F = F2 // 2
    nb = F // tb

    def body(hg_ref, hu_ref, a_ref):
        hg = hg_ref[...].astype(F32)
        a_ref[...] = (hg * _sigmoid(hg) * hu_ref[...].astype(F32)).astype(BF16)

    return pl.pallas_call(
        body, name="swiglu_fwd", grid=(S // ts, nb),
        in_specs=[pl.BlockSpec((ts, tb), lambda i, j: (i, j)), pl.BlockSpec((ts, tb), lambda i, j: (i, j + nb))],
        out_specs=pl.BlockSpec((ts, tb), lambda i, j: (i, j)),
        out_shape=jax.ShapeDtypeStruct((S, F), BF16),
        compiler_params=_cparams(("parallel", "parallel")),
    )(h, h)


def _ln2_loss(xhat1, ffn, tgt, mod, g1, b1, g2, b2, ts):
    S, D = xhat1.shape

    def body(xh_ref, ffn_ref, t_ref, mod_ref, g1_ref, b1_ref, g2_ref, b2_ref, loss_ref, dffn_ref, dx1_ref, vec_ref):
        i = pl.program_id(0)

        @pl.when(i == 0)
        def _():
            loss_ref[...] = jnp.zeros_like(loss_ref)
            vec_ref[...] = jnp.zeros_like(vec_ref)

        x1 = xh_ref[...] * g1_ref[...] + b1_ref[...]
        ffn = ffn_ref[...]
        r = DEEPNORM_ALPHA * x1 + mod_ref[5:6, :] * ffn
        mu = jnp.mean(r, axis=-1, keepdims=True)
        d = r - mu
        rstd = lax.rsqrt(jnp.mean(d * d, axis=-1, keepdims=True) + LN_EPS)
        xhat = d * rstd
        e = xhat * g2_ref[...] + b2_ref[...] - t_ref[...]
        loss_ref[...] += 0.5 * jnp.sum(jnp.mean(e * e, axis=-1, keepdims=True))
        dy = e * (1.0 / D)
        dxhat = dy * g2_ref[...]
        dr = rstd * (dxhat - jnp.mean(dxhat, axis=-1, keepdims=True)
                     - xhat * jnp.mean(dxhat * xhat, axis=-1, keepdims=True))
        dffn_ref[...] = (dr * mod_ref[5:6, :]).astype(BF16)
        dx1_ref[...] = DEEPNORM_ALPHA * dr
        vec_ref[0:1, :] += jnp.sum(dy * xhat, axis=0, keepdims=True)
        vec_ref[1:2, :] += jnp.sum(dy, axis=0, keepdims=True)
        vec_ref[2:3, :] += jnp.sum(dr * ffn, axis=0, keepdims=True)

    row = pl.BlockSpec((ts, D), lambda i: (i, 0))
    vec = lambda r: pl.BlockSpec((r, D), lambda i: (0, 0))
    return pl.pallas_call(
        body, name="ln2_loss", grid=(S // ts,),
        in_specs=[row, row, row, vec(6), vec(1), vec(1), vec(1), vec(1)],
        out_specs=[pl.BlockSpec((1, LANE), lambda i: (0, 0)), row, row, vec(8)],
        out_shape=[jax.ShapeDtypeStruct((1, LANE), F32), jax.ShapeDtypeStruct((S, D), BF16),
                   jax.ShapeDtypeStruct((S, D), F32), jax.ShapeDtypeStruct((8, D), F32)],
        compiler_params=_cparams(("arbitrary",)),
    )(xhat1, ffn, tgt, mod, g1, b1, g2, b2)


def _swiglu_bwd(da, h, ts, tb):
    S, F2 = h.shape
    nb = (F2 // 2) // tb

    def body(da_ref, hg_ref, hu_ref, dh_ref):
        hg, da = hg_ref[...].astype(F32), da_ref[...].astype(F32)
        sg = _sigmoid(hg)

        @pl.when(pl.program_id(2) == 0)
        def _():
            dh_ref[...] = (da * hu_ref[...].astype(F32) * (sg * (1.0 + hg * (1.0 - sg)))).astype(BF16)

        @pl.when(pl.program_id(2) == 1)
        def _():
            dh_ref[...] = (da * hg * sg).astype(BF16)

    lo = pl.BlockSpec((ts, tb), lambda i, j, k: (i, j))
    hi = pl.BlockSpec((ts, tb), lambda i, j, k: (i, j + nb))
    return pl.pallas_call(
        body, name="swiglu_bwd", grid=(S // ts, nb, 2),
        in_specs=[lo, lo, hi],
        out_specs=pl.BlockSpec((ts, tb), lambda i, j, k: (i, j + nb * k)),
        out_shape=jax.ShapeDtypeStruct((S, F2), BF16),
        compiler_params=_cparams(("parallel", "parallel", "arbitrary")),
    )(da, h, h)


def _ln1_bwd(du2, dx1a, xhat1, rstd1, mix, mod, g1, b1, ts):
    S, D = xhat1.shape

    def body(du2_ref, dx1a_ref, xh_ref, rstd_ref, mix_ref, mod_ref, g_ref, b_ref, dxa_ref, dmix_ref, vec_ref):
        i = pl.program_id(0)

        @pl.when(i == 0)
        def _():
            vec_ref[...] = jnp.zeros_like(vec_ref)

        xhat, du2, mix = xh_ref[...], du2_ref[...], mix_ref[...]
        x1 = xhat * g_ref[...] + b_ref[...]
        dx1 = dx1a_ref[...] + du2 * (1.0 + mod_ref[4:5, :])
        dxhat = dx1 * g_ref[...]
        dr = rstd_ref[...] * (dxhat - jnp.mean(dxhat, axis=-1, keepdims=True)
                              - xhat * jnp.mean(dxhat * xhat, axis=-1, keepdims=True))
        dxa_ref[...] = DEEPNORM_ALPHA * dr
        dmix_ref[...] = (dr * mod_ref[2:3, :]).astype(BF16)
        vec_ref[0:1, :] += jnp.sum(du2, axis=0, keepdims=True)
        vec_ref[1:2, :] += jnp.sum(du2 * x1, axis=0, keepdims=True)
        vec_ref[2:3, :] += jnp.sum(dx1 * xhat, axis=0, keepdims=True)
        vec_ref[3:4, :] += jnp.sum(dx1, axis=0, keepdims=True)
        vec_ref[4:5, :] += jnp.sum(dr * mix, axis=0, keepdims=True)

    row = pl.BlockSpec((ts, D), lambda i: (i, 0))
    vec = lambda r: pl.BlockSpec((r, D), lambda i: (0, 0))
    return pl.pallas_call(
        body, name="ln1_bwd", grid=(S // ts,),
        in_specs=[row, row, row, pl.BlockSpec((ts, 1), lambda i: (i, 0)), row, vec(6), vec(1), vec(1)],
        out_specs=[row, row, vec(8)],
        out_shape=[jax.ShapeDtypeStruct((S, D), F32), jax.ShapeDtypeStruct((S, D), BF16),
                   jax.ShapeDtypeStruct((8, D), F32)],
        compiler_params=_cparams(("arbitrary",)),
    )(du2, dx1a, xhat1, rstd1, mix, mod, g1, b1)


def _merge_bwd(dmerged, proj, ya, yb, blk_ga, blk_gb, ts):
    S, D = ya.shape
    nb = D // COL_BLOCK

    def body(dm_ref, ga_ref, gb_ref, ya_ref, yb_ref, dya_ref, dyb_ref, dga_ref, dgb_ref):
        dm = dm_ref[...].astype(F32)
        sa, sb = _sigmoid(ga_ref[...].astype(F32)), _sigmoid(gb_ref[...].astype(F32))
        dya_ref[...] = (dm * sa).astype(BF16)
        dyb_ref[...] = (dm * sb).astype(BF16)
        dga_ref[...] = (dm * ya_ref[...].astype(F32) * sa * (1.0 - sa)).astype(BF16)
        dgb_ref[...] = (dm * yb_ref[...].astype(F32) * sb * (1.0 - sb)).astype(BF16)

    row = pl.BlockSpec((ts, D), lambda i: (i, 0))
    seg = lambda blk: pl.BlockSpec((pl.Element(ts), pl.Element(D)), lambda i: (i * ts, blk * COL_BLOCK))
    out = jax.ShapeDtypeStruct((S, D), BF16)
    return pl.pallas_call(
        body, name="merge_bwd", grid=(S // ts,),
        in_specs=[row, seg(blk_ga), seg(blk_gb), row, row],
        out_specs=[row] * 4,
        out_shape=[out] * 4,
        compiler_params=_cparams(("parallel",)),
    )(dmerged, proj, proj, ya, yb)


def _conv_bwd(dcbc, proj, w_conv, blk_b, blk_c, blk_x):
    S = proj.shape[0]
    D = w_conv.shape[1]
    nb = D // COL_BLOCK

    def body(d_ref, cb_ref, cc_ref, cx_ref, w_ref, dcb_ref, dcc_ref, dcx_ref, dw_ref):
        d, cc, cx = d_ref[...].astype(F32), cc_ref[...].astype(F32), cx_ref[...].astype(F32)
        z = cc * cx
        z1, z2 = _shift_rows(z, 1), _shift_rows(z, 2)
        conv = w_ref[2:3, :] * z + w_ref[1:2, :] * z1 + w_ref[0:1, :] * z2
        dcb_ref[...] = (d * conv).astype(BF16)
        dconv = d * cb_ref[...].astype(F32)
        dz = w_ref[2:3, :] * dconv + w_ref[1:2, :] * _shift_rows(dconv, -1) + w_ref[0:1, :] * _shift_rows(dconv, -2)
        dcc_ref[...] = (dz * cx).astype(BF16)
        dcx_ref[...] = (dz * cc).astype(BF16)
        dw_ref[...] = jnp.zeros_like(dw_ref)
        dw_ref[0:1, :] = jnp.sum(dconv * z2, axis=0, keepdims=True)
        dw_ref[1:2, :] = jnp.sum(dconv * z1, axis=0, keepdims=True)
        dw_ref[2:3, :] = jnp.sum(dconv * z, axis=0, keepdims=True)

    col = lambda off: pl.BlockSpec((S, COL_BLOCK), lambda j: (0, off + j))
    out = jax.ShapeDtypeStruct((S, D), BF16)
    return pl.pallas_call(
        body, name="conv_bwd", grid=(nb,),
        in_specs=[col(0), col(blk_b), col(blk_c), col(blk_x), pl.BlockSpec((CONV_K, COL_BLOCK), lambda j: (0, j))],
        out_specs=[col(0), col(0), col(0), pl.BlockSpec((8, COL_BLOCK), lambda j: (0, j))],
        out_shape=[out, out, out, jax.ShapeDtypeStruct((8, D), F32)],
        compiler_params=_cparams(("parallel",)),
    )(dcbc, proj, proj, proj, w_conv)


def _attn_bwd(qc, kc, vh, do, o, lse, T):
    H, S, _ = qc.shape
    n = S // T

    def body(q_ref, k_ref, v_ref, do_ref, o_ref, lse_ref, dq_ref, dk_ref, dv_ref, d_ref, dq_acc, dk_acc, dv_acc):
        j = pl.program_id(1)

        @pl.when(j == 0)
        def _():
            dq_acc[...] = jnp.zeros_like(dq_acc)
            d_ref[...] = jnp.sum(do_ref[...] * o_ref[...], axis=-1, keepdims=True)

        dk_acc[...] = jnp.zeros_like(dk_acc)
        dv_acc[...] = jnp.zeros_like(dv_acc)
        k, v = k_ref[0], v_ref[0]

        def step(i, masked):
            rows = pl.ds(pl.multiple_of(i * T, T), T)
            q = q_ref[0, rows, :]
            do = do_ref[rows, :].astype(BF16)
            s = lax.dot_general(q, k, NT_DIMS, preferred_element_type=F32) * ATTN_SCALE
            if masked:
                s = jnp.where(_diag_mask(T), s, NEG_INF)
            p = jnp.exp(s - lse_ref[0, rows, :])
            dv_acc[...] += lax.dot_general(p.astype(BF16), do, TN_DIMS, preferred_element_type=F32)
            dp = lax.dot_general(do, v, NT_DIMS, preferred_element_type=F32)
            ds = (p * (dp - d_ref[rows, :]) * ATTN_SCALE).astype(BF16)
            dk_acc[...] += lax.dot_general(ds, q, TN_DIMS, preferred_element_type=F32)
            dq_acc[rows, :] += jnp.dot(ds, k, preferred_element_type=F32)

        def above(i, carry):
            step(i, False)
            return carry

        step(j, True)
        lax.fori_loop(j + 1, n, above, 0)
        dk_ref[0] = dk_acc[...].astype(BF16)
        dv_ref[0] = dv_acc[...].astype(BF16)

        @pl.when(j == n - 1)
        def _():
            dq_ref[0] = dq_acc[...].astype(BF16)

    head = lambda w: pl.BlockSpec((1, S, w), lambda h, j: (h, 0, 0))
    blk = lambda w: pl.BlockSpec((1, T, w), lambda h, j: (h, j, 0))
    ospec = pl.BlockSpec((S, V_HEAD), lambda h, j: (0, h))
    return pl.pallas_call(
        body, name="attn_bwd", grid=(H, n),
        in_specs=[head(QK_CAT), blk(QK_CAT), blk(V_HEAD), ospec, ospec, head(1)],
        out_specs=[head(QK_CAT), blk(QK_CAT), blk(V_HEAD)],
        out_shape=[jax.ShapeDtypeStruct((H, S, QK_CAT), BF16), jax.ShapeDtypeStruct((H, S, QK_CAT), BF16),
                   jax.ShapeDtypeStruct((H, S, V_HEAD), BF16)],
        scratch_shapes=[pltpu.VMEM((S, 1), F32), pltpu.VMEM((S, QK_CAT), F32), pltpu.VMEM((T, QK_CAT), F32),
                        pltpu.VMEM((T, V_HEAD), F32)],
        compiler_params=_cparams(("parallel", "arbitrary")),
    )(qc, kc, vh, do, o, lse)


def _qk_bwd(dqc, dkc, dvh, cos_q, sin_q, cos_k, sin_k, ts):
    H, S, _ = dqc.shape
    pair = 2 * QK_CAT
    kv_w = QK_NOPE + V_HEAD

    def body(dqc_ref, dkc_ref, dvh_ref, cq_ref, sq_ref, ck_ref, sk_ref, dq_ref, dkv_ref, dkr_ref, q_buf, kr_buf):
        for p in range(H // 2):
            q_buf[:, :QK_CAT] = dqc_ref[2 * p].astype(F32)
            q_buf[:, QK_CAT:] = dqc_ref[2 * p + 1].astype(F32)
            g = q_buf[...]
            dq_ref[:, p * pair:(p + 1) * pair] = (
                g * cq_ref[...] - _rope_partner(g, QK_CAT, QK_NOPE) * sq_ref[...]).astype(BF16)
        kr_sum = jnp.zeros((ts, QK_ROPE), F32)
        for h in range(H):
            dkv_ref[:, h * kv_w:h * kv_w + QK_NOPE] = dkc_ref[h, :, 0:QK_NOPE].astype(BF16)
            dkv_ref[:, h * kv_w + QK_NOPE:(h + 1) * kv_w] = dvh_ref[h].astype(BF16)
            kr_sum = kr_sum + dkc_ref[h, :, QK_NOPE:QK_CAT]
        kr_buf[...] = jnp.zeros_like(kr_buf)
        kr_buf[:, 0:QK_ROPE] = kr_sum
        kr = kr_buf[...]
        dkr_ref[...] = (kr * ck_ref[...] - _rope_partner(kr, QK_ROPE, 0) * sk_ref[...]).astype(BF16)

    row = lambda w: pl.BlockSpec((ts, w), lambda i: (i, 0))
    head = lambda w: pl.BlockSpec((H, ts, w), lambda i: (0, i, 0))
    return pl.pallas_call(
        body, name="qk_bwd", grid=(S // ts,),
        in_specs=[head(QK_CAT), head(QK_CAT), head(V_HEAD), row(pair), row(pair), row(COL_BLOCK), row(COL_BLOCK)],
        out_specs=[row(H * QK_CAT), row(H * kv_w), row(COL_BLOCK)],
        out_shape=[jax.ShapeDtypeStruct((S, H * QK_CAT), BF16), jax.ShapeDtypeStruct((S, H * kv_w), BF16),
                   jax.ShapeDtypeStruct((S, COL_BLOCK), BF16)],
        scratch_shapes=[pltpu.VMEM((ts, pair), F32), pltpu.VMEM((ts, COL_BLOCK), F32)],
        compiler_params=_cparams(("parallel",)),
    )(dqc, dkc, dvh, cos_q, sin_q, cos_k, sin_k)


def _rms_bwd(dy, proj, g, blk, L, ts, name):
    S = proj.shape[0]

    def body(dy_ref, a_ref, g_ref, da_ref, dg_ref):
        i = pl.program_id(0)

        @pl.when(i == 0)
        def _():
            dg_ref[...] = jnp.zeros_like(dg_ref)

        a, dy = a_ref[...].astype(F32), dy_ref[...]
        r = lax.rsqrt(jnp.mean(a * a, axis=-1, keepdims=True) + RMS_EPS)
        dyh = dy * g_ref[...]
        da = r * dyh - a * (r * r * r) * jnp.mean(dyh * a, axis=-1, keepdims=True)
        da_ref[...] = da.astype(BF16)
        dg_ref[0:1, :] += jnp.sum(dy * a * r, axis=0, keepdims=True)

    return pl.pallas_call(
        body, name=name, grid=(S // ts,),
        in_specs=[pl.BlockSpec((ts, L), lambda i: (i, 0)), pl.BlockSpec((ts, L), lambda i: (i, blk)),
                  pl.BlockSpec((1, L), lambda i: (0, 0))],
        out_specs=[pl.BlockSpec((ts, L), lambda i: (i, 0)), pl.BlockSpec((8, L), lambda i: (0, 0))],
        out_shape=[jax.ShapeDtypeStruct((S, L), BF16), jax.ShapeDtypeStruct((8, L), F32)],
        compiler_params=_cparams(("arbitrary",)),
    )(dy, proj, g)


def _grad_x(du, dxa, x, mod, ts):
    S, D = x.shape

    def body(du_ref, dxa_ref, x_ref, mod_ref, dx_ref, vec_ref):
        i = pl.program_id(0)

        @pl.when(i == 0)
        def _():
            vec_ref[...] = jnp.zeros_like(vec_ref)

        du = du_ref[...]
        dx_ref[...] = dxa_ref[...] + du * (1.0 + mod_ref[1:2, :])
        vec_ref[0:1, :] += jnp.sum(du, axis=0, keepdims=True)
        vec_ref[1:2, :] += jnp.sum(du * x_ref[...], axis=0, keepdims=True)

    row = pl.BlockSpec((ts, D), lambda i: (i, 0))
    vec = lambda r: pl.BlockSpec((r, D), lambda i: (0, 0))
    return pl.pallas_call(
        body, name="grad_x", grid=(S // ts,),
        in_specs=[row, row, row, vec(6)],
        out_specs=[row, vec(8)],
        out_shape=[jax.ShapeDtypeStruct((S, D), F32), jax.ShapeDtypeStruct((8, D), F32)],
        compiler_params=_cparams(("arbitrary",)),
    )(du, dxa, x, mod)


def _adamw(w, g, m, v, name):
    R, C = w.shape
    tr = _tile(R, max(8, (1 << 19) // C), 8)
    c1 = 1.0 / (1.0 - ADAM_B1 ** ADAM_STEP)
    c2 = 1.0 / (1.0 - ADAM_B2 ** ADAM_STEP)

    def body(w_ref, g_ref, m_ref, v_ref, d_ref, nm_ref, nv_ref):
        g = g_ref[...]
        m = ADAM_B1 * m_ref[...] + (1.0 - ADAM_B1) * g
        v = ADAM_B2 * v_ref[...] + (1.0 - ADAM_B2) * (g * g)
        nm_ref[...] = m
        nv_ref[...] = v
        d_ref[...] = -ADAM_LR * ((m * c1) / (jnp.sqrt(v * c2) + ADAM_EPS) + ADAM_WD * w_ref[...])

    spec = pl.BlockSpec((tr, C), lambda i: (i, 0))
    out = jax.ShapeDtypeStruct((R, C), F32)
    return pl.pallas_call(
        body, name=name, grid=(R // tr,),
        in_specs=[spec] * 4, out_specs=[spec] * 3, out_shape=[out] * 3,
        compiler_params=_cparams(("parallel",)),
    )(w, g, m, v)


def _adamw_ada(w, cact_t, dmod, m, v):
    R, C = w.shape
    tr = _tile(R, max(8, (1 << 18) // C), 8)
    c1 = 1.0 / (1.0 - ADAM_B1 ** ADAM_STEP)
    c2 = 1.0 / (1.0 - ADAM_B2 ** ADAM_STEP)

    def body(w_ref, ct_ref, dm_ref, m_ref, v_ref, g_ref, d_ref, nm_ref, nv_ref):
        ct = ct_ref[...].astype(BF16).astype(F32)
        dm = dm_ref[...].astype(BF16).astype(F32)
        g = ct[:, 0:1] * dm[0:1, :]
        for b in range(1, N_DEV):
            g = g + ct[:, b:b + 1] * dm[b:b + 1, :]
        m = ADAM_B1 * m_ref[...] + (1.0 - ADAM_B1) * g
        v = ADAM_B2 * v_ref[...] + (1.0 - ADAM_B2) * (g * g)
        g_ref[...] = g
        nm_ref[...] = m
        nv_ref[...] = v
        d_ref[...] = -ADAM_LR * ((m * c1) / (jnp.sqrt(v * c2) + ADAM_EPS) + ADAM_WD * w_ref[...])

    spec = pl.BlockSpec((tr, C), lambda i: (i, 0))
    out = jax.ShapeDtypeStruct((R, C), F32)
    return pl.pallas_call(
        body, name="adamw_w_ada", grid=(R // tr,),
        in_specs=[spec, pl.BlockSpec((tr, N_DEV), lambda i: (i, 0)), pl.BlockSpec((N_DEV, C), lambda i: (0, 0)),
                  spec, spec],
        out_specs=[spec] * 4, out_shape=[out] * 4,
        compiler_params=_cparams(("parallel",)),
    )(w, cact_t, dmod, m, v)


def _adamw_reduced(w, own, got, m, v, my_chip, name):
    R, C = w.shape
    tr = _tile(R, max(PACK_ROW_ALIGN, (1 << 18) // C), PACK_ROW_ALIGN)
    c1 = 1.0 / (1.0 - ADAM_B1 ** ADAM_STEP)
    c2 = 1.0 / (1.0 - ADAM_B2 ** ADAM_STEP)

    def body(chip_ref, w_ref, own_ref, g1_ref, g2_ref, g3_ref, m_ref, v_ref, g_ref, d_ref, nm_ref, nv_ref):
        g = own_ref[0].astype(F32) + g1_ref[0].astype(F32) + g2_ref[0].astype(F32) + g3_ref[0].astype(F32)
        m = ADAM_B1 * m_ref[...] + (1.0 - ADAM_B1) * g
        v = ADAM_B2 * v_ref[...] + (1.0 - ADAM_B2) * (g * g)
        g_ref[...] = g
        nm_ref[...] = m
        nv_ref[...] = v
        d_ref[...] = -ADAM_LR * ((m * c1) / (jnp.sqrt(v * c2) + ADAM_EPS) + ADAM_WD * w_ref[...])

    spec = pl.BlockSpec((tr, C), lambda i, chip: (i, 0))
    slot = lambda k: pl.BlockSpec((1, tr, C), lambda i, chip: (chip[0] ^ k, i, 0))
    out = jax.ShapeDtypeStruct((R, C), F32)
    return pl.pallas_call(
        body, name=name,
        grid_spec=pltpu.PrefetchScalarGridSpec(
            num_scalar_prefetch=1, grid=(R // tr,),
            in_specs=[spec, slot(0), slot(1), slot(2), slot(3), spec, spec],
            out_specs=[spec] * 4),
        out_shape=[out] * 4,
        compiler_params=_cparams(("parallel",)),
    )(my_chip, w, own, got, got, got, m, v)


def _my_place():
    return lax.axis_index("x"), lax.axis_index("y"), lax.axis_index("c")


def _peer(k):
    x, y, c = _my_place()
    return (x ^ ((k >> 2) & 1), y ^ ((k >> 1) & 1), c ^ (k & 1))


def _linear(place):
    return 4 * place[0] + 2 * place[1] + place[2]


def _ada_fwd(c_row, wconv_row, w_ada, b_row):
    D, CW = w_ada.shape
    WC = wconv_row.shape[-1]

    def body(c_ref, wc_ref, w_ref, b_ref, mod_ref, cact_ref, wcall_ref, send_buf, sems):
        me = _linear(_my_place())
        c = c_ref[0]
        cact_ref[me] = c * _sigmoid(c)
        wcall_ref[me] = wc_ref[0]

        def gather_copy(buf, k, grp):
            return pltpu.make_async_remote_copy(
                src_ref=buf.at[me], dst_ref=buf.at[me], send_sem=sems.at[0, grp, k], recv_sem=sems.at[1, grp, k],
                device_id=_peer(k), device_id_type=MESH_ID)

        def gather_recv(buf, k, grp):
            src = _linear(_peer(k))
            return pltpu.make_async_remote_copy(
                src_ref=buf.at[src], dst_ref=buf.at[src], send_sem=sems.at[0, grp, k], recv_sem=sems.at[1, grp, k],
                device_id=_peer(k), device_id_type=MESH_ID)

        for k in range(1, N_DEV):
            gather_copy(cact_ref, k, 0).start()
            gather_copy(wcall_ref, k, 1).start()
        for k in range(1, N_DEV):
            gather_recv(cact_ref, k, 0).wait_recv()
            gather_recv(wcall_ref, k, 1).wait_recv()
        for k in range(1, N_DEV):
            gather_copy(cact_ref, k, 0).wait_send()
            gather_copy(wcall_ref, k, 1).wait_send()

        cact = jnp.concatenate([cact_ref[b] for b in range(N_DEV)], axis=0)
        mod_all = jnp.dot(cact.astype(BF16), w_ref[...].astype(BF16), preferred_element_type=F32) + b_ref[0]
        for b in range(N_DEV):
            send_buf[b] = mod_all[b:b + 1, :]
        mod_ref[me] = send_buf[me]

        def scatter_copy(k):
            dst = _linear(_peer(k))
            return pltpu.make_async_remote_copy(
                src_ref=send_buf.at[dst], dst_ref=mod_ref.at[me], send_sem=sems.at[0, 2, k], recv_sem=sems.at[1, 2, k],
                device_id=_peer(k), device_id_type=MESH_ID)

        def scatter_recv(k):
            src = _linear(_peer(k))
            return pltpu.make_async_remote_copy(
                src_ref=send_buf.at[src], dst_ref=mod_ref.at[src], send_sem=sems.at[0, 2, k], recv_sem=sems.at[1, 2, k],
                device_id=_peer(k), device_id_type=MESH_ID)

        for k in range(1, N_DEV):
            scatter_copy(k).start()
        for k in range(1, N_DEV):
            scatter_recv(k).wait_recv()
        for k in range(1, N_DEV):
            scatter_copy(k).wait_send()

    vmem = pl.BlockSpec(memory_space=pltpu.VMEM)
    return pl.pallas_call(
        body, name="ada_fwd",
        in_specs=[vmem] * 4, out_specs=[vmem] * 3,
        out_shape=[jax.ShapeDtypeStruct((N_DEV, 1, CW), F32), jax.ShapeDtypeStruct((N_DEV, 1, D), F32),
                   jax.ShapeDtypeStruct((N_DEV, 1, WC), F32)],
        scratch_shapes=[pltpu.VMEM((N_DEV, 1, CW), F32), pltpu.SemaphoreType.DMA((2, 3, N_DEV))],
        compiler_params=pltpu.CompilerParams(vmem_limit_bytes=VMEM_LIMIT),
    )(c_row, wconv_row, w_ada, b_row)


def _ada_bwd(payload, deps=()):
    NCH, _, CW = payload.shape

    def body(p_ref, *rest):
        sum_ref, mine_ref, all_ref, sems = rest[-4:]
        me = _linear(_my_place())
        all_ref[me] = p_ref[...]

        def copy(k, slot):
            return pltpu.make_async_remote_copy(
                src_ref=all_ref.at[slot], dst_ref=all_ref.at[slot], send_sem=sems.at[0, k], recv_sem=sems.at[1, k],
                device_id=_peer(k), device_id_type=MESH_ID)

        for k in range(1, N_DEV):
            copy(k, me).start()
        for k in range(1, N_DEV):
            copy(k, _linear(_peer(k))).wait_recv()
        for k in range(1, N_DEV):
            copy(k, me).wait_send()

        total = all_ref[0]
        for b in range(1, N_DEV):
            total = total + all_ref[b]
        sum_ref[...] = total

        for b in range(N_DEV):
            mine_ref[b] = all_ref[b, me]

    vmem = pl.BlockSpec(memory_space=pltpu.VMEM)
    return pl.pallas_call(
        body, name="ada_bwd",
        in_specs=[vmem] + [ANY_SPEC] * len(deps), out_specs=[vmem, vmem],
        out_shape=[jax.ShapeDtypeStruct((NCH, 1, CW), F32), jax.ShapeDtypeStruct((N_DEV, 1, CW), F32)],
        scratch_shapes=[pltpu.VMEM((N_DEV, NCH, 1, CW), F32), pltpu.SemaphoreType.DMA((2, N_DEV))],
        compiler_params=pltpu.CompilerParams(vmem_limit_bytes=VMEM_LIMIT),
    )(payload, *deps)


def _exchange_in_chip(parts):
    W = len(parts)

    def body(*refs):
        p_refs, got_refs, (send_sems, recv_sems) = refs[:W], refs[W:2 * W], refs[2 * W:]
        x, y, c = _my_place()
        sibling = (x, y, 1 - c)
        copies = []
        for w in range(W):
            for q in range(4):
                copies.append(pltpu.make_async_remote_copy(
                    src_ref=p_refs[w].at[2 * q + (1 - c)], dst_ref=got_refs[w].at[q],
                    send_sem=send_sems.at[4 * w + q], recv_sem=recv_sems.at[4 * w + q],
                    device_id=sibling, device_id_type=MESH_ID))
        for cp in copies:
            cp.start()
        for cp in copies:
            cp.wait_recv()
        for cp in copies:
            cp.wait_send()

    return pl.pallas_call(
        body, name="grad_exchange_in_chip",
        in_specs=[HBM_SPEC] * W, out_specs=[HBM_SPEC] * W,
        out_shape=[jax.ShapeDtypeStruct((4,) + p.shape[1:], p.dtype) for p in parts],
        scratch_shapes=[pltpu.SemaphoreType.DMA((4 * W,)), pltpu.SemaphoreType.DMA((4 * W,))],
    )(*parts)


def _pair_sum(parts, got, core):
    _, R, C = parts.shape
    tr = _tile(R, max(PACK_ROW_ALIGN, PAIR_SUM_BLOCK // C), PACK_ROW_ALIGN)

    def body(c_ref, p_ref, g_ref, o_ref):
        o_ref[...] = (p_ref[...].astype(F32) + g_ref[...].astype(F32)).astype(o_ref.dtype)

    return pl.pallas_call(
        body, name="grad_pair_sum",
        grid_spec=pltpu.PrefetchScalarGridSpec(
            num_scalar_prefetch=1, grid=(4, R // tr),
            in_specs=[pl.BlockSpec((1, tr, C), lambda q, i, c_ref: (2 * q + c_ref[0], i, 0)),
                      pl.BlockSpec((1, tr, C), lambda q, i, c_ref: (q, i, 0))],
            out_specs=pl.BlockSpec((1, tr, C), lambda q, i, c_ref: (q, i, 0))),
        out_shape=jax.ShapeDtypeStruct((4, R, C), parts.dtype),
        compiler_params=_cparams(("parallel", "parallel")),
    )(core, parts, got)


HBM_SPEC = pl.BlockSpec(memory_space=pltpu.HBM)
SEM_SPEC = pl.BlockSpec(memory_space=pltpu.SEMAPHORE)
ANY_SPEC = pl.BlockSpec(memory_space=pl.ANY)
SPLIT_EFFECT = pltpu.SideEffectType.DATAFLOW_SIDE_EFFECTING


def _landing_zone(shape, dtype):
    return pltpu.with_memory_space_constraint(lax.empty(shape, dtype), pltpu.HBM)


def _split_start(name, arrays, lands, after, copies_of, per_array):
    W = len(arrays)
    after = tuple(after) if isinstance(after, (tuple, list)) else (after,)

    def body(*refs):
        x_refs, land_refs = refs[:W], refs[W:2 * W]
        send_sems, recv_sems = refs[2 * W + len(after)], refs[2 * W + len(after) + 1]
        token = refs[-1]
        k = 0
        for w in range(W):
            for src, dst, dev in copies_of(w, x_refs[w], land_refs[w]):
                pltpu.make_async_remote_copy(src_ref=src, dst_ref=dst, send_sem=send_sems.at[k], recv_sem=recv_sems.at[k],
                                             device_id=dev, device_id_type=MESH_ID).start()
                k += 1
        token[...] = jnp.zeros_like(token)

    n_copies = per_array * W
    hbm_of = lambda xs: tuple(pltpu.HBM(a.shape, a.dtype) for a in xs)
    out = pl.pallas_call(
        body, name=name,
        out_shape=(pltpu.SemaphoreType.DMA((n_copies,)), pltpu.SemaphoreType.DMA((n_copies,)))
        + hbm_of(arrays) + hbm_of(lands) + (jax.ShapeDtypeStruct((8, LANE), F32),),
        in_specs=(HBM_SPEC,) * (2 * W) + (ANY_SPEC,) * len(after),
        out_specs=(SEM_SPEC, SEM_SPEC) + (HBM_SPEC,) * (2 * W) + (pl.BlockSpec(memory_space=pltpu.VMEM),),
        input_output_aliases={i: 2 + i for i in range(2 * W)},
        compiler_params=pltpu.CompilerParams(has_side_effects=SPLIT_EFFECT),
    )(*[pltpu.with_memory_space_constraint(a, pltpu.HBM) for a in arrays], *lands, *after)
    return out[0], out[1], list(out[2:2 + W]), list(out[2 + W:2 + 2 * W]), out[-1]


def _split_wait(name, state, after, copies_of):
    send_sems, recv_sems, arrays, lands, _ = state
    W = len(arrays)
    after = tuple(after) if isinstance(after, (tuple, list)) else (after,)

    def body(*refs):
        x_refs, land_refs = refs[:W], refs[W:2 * W]
        send_sems, recv_sems = refs[2 * W], refs[2 * W + 1]
        k = 0
        for w in range(W):
            for src, dst, dev in copies_of(w, x_refs[w], land_refs[w]):
                cp = pltpu.make_async_remote_copy(src_ref=src, dst_ref=dst, send_sem=send_sems.at[k],
                                                  recv_sem=recv_sems.at[k], device_id=dev, device_id_type=MESH_ID)
                cp.wait_send()
                cp.wait_recv()
                k += 1

    out = pl.pallas_call(
        body, name=name,
        out_shape=tuple(pltpu.HBM(a.shape, a.dtype) for a in arrays + lands),
        in_specs=(HBM_SPEC,) * (2 * W) + (SEM_SPEC, SEM_SPEC) + (ANY_SPEC,) * len(after),
        out_specs=(HBM_SPEC,) * (2 * W),
        input_output_aliases={i: i for i in range(2 * W)},
        compiler_params=pltpu.CompilerParams(has_side_effects=SPLIT_EFFECT),
    )(*arrays, *lands, send_sems, recv_sems, *after)
    return list(out[:W]), list(out[W:])


def _scatter_copies(w, p_ref, land_ref):
    x, y, c = _my_place()
    my_chip = 2 * x + y
    return [(p_ref.at[2 * (x ^ (k >> 1)) + (y ^ (k & 1))], land_ref.at[my_chip], (x ^ (k >> 1), y ^ (k & 1), c))
            for k in range(1, 4)]


def _gather_copies(w, x_ref, land_ref):
    x, y, c = _my_place()
    me = _linear((x, y, c))
    devs = [(x, y, 1 - c)] + [(x ^ (k >> 1), y ^ (k & 1), c) for k in range(1, 4)]
    return [(x_ref, land_ref.at[me], d) for d in devs]


def _gather_forward(lands, name):
    W = len(lands)

    def body(*refs):
        land_refs, out_refs, (send_sems, recv_sems) = refs[:W], refs[W:2 * W], refs[2 * W:]
        x, y, c = _my_place()
        sibling = (x, y, 1 - c)
        sends, arrivals = [], []
        for w in range(W):
            for k in range(1, 4):
                px, py = x ^ (k >> 1), y ^ (k & 1)
                landed, theirs = _linear((px, py, c)), out_refs[w].at[_linear((px, py, 1 - c))]
                sem = 3 * w + k - 1
                sends.append(pltpu.make_async_remote_copy(
                    src_ref=land_refs[w].at[landed], dst_ref=out_refs[w].at[landed],
                    send_sem=send_sems.at[sem], recv_sem=recv_sems.at[sem], device_id=sibling, device_id_type=MESH_ID))
                arrivals.append(pltpu.make_async_remote_copy(
                    src_ref=theirs, dst_ref=theirs, send_sem=send_sems.at[sem], recv_sem=recv_sems.at[sem],
                    device_id=sibling, device_id_type=MESH_ID))
        for cp in sends:
            cp.start()
        for cp in arrivals:
            cp.wait_recv()
        for cp in sends:
            cp.wait_send()

    return pl.pallas_call(
        body, name=name,
        in_specs=[HBM_SPEC] * W, out_specs=[HBM_SPEC] * W,
        out_shape=[jax.ShapeDtypeStruct(l.shape, l.dtype) for l in lands],
        input_output_aliases={i: i for i in range(W)},
        scratch_shapes=[pltpu.SemaphoreType.DMA((3 * W,)), pltpu.SemaphoreType.DMA((3 * W,))],
    )(*lands)


def _with_own_slot(gathered, shard):
    return lax.dynamic_update_index_in_dim(gathered, shard[None], _linear(_my_place()), axis=0)


def _in_chip_copies(w, p_ref, land_ref):
    x, y, c = _my_place()
    return [(p_ref.at[2 * q + (1 - c)], land_ref.at[q], (x, y, 1 - c)) for q in range(4)]


def _in_chip_start(parts, tag):
    lands = [_landing_zone((4,) + p.shape[1:], p.dtype) for p in parts]
    return _split_start("grad_in_chip_start_" + tag, parts, lands, (), _in_chip_copies, 4)


def _reduce_scatter_begin(parts, tag, in_chip_state=None, after=()):
    if in_chip_state is None:
        got = _exchange_in_chip(parts)
    else:
        parts, got = _split_wait("grad_in_chip_wait_" + tag, in_chip_state, after, _in_chip_copies)
    core = lax.axis_index("c").astype(jnp.int32).reshape(1)
    chip_parts = [_pair_sum(p, g, core) for p, g in zip(parts, got)]
    lands = [_landing_zone(p.shape, p.dtype) for p in chip_parts]
    return _split_start("grad_scatter_start_" + tag, chip_parts, lands, got[0], _scatter_copies, 3)


def _reduce_scatter_end(state, after, tag):
    return _split_wait("grad_scatter_wait_" + tag, state, after, _scatter_copies)


def kernel(x, c, positions, w_ada, b_ada, w_in, g_q_a, w_q_b, g_kv_a, w_kv_b, w_o_a, w_conv, w_o_b, w_o, ln1_g, ln1_b, w_ffn_in, w_ffn_out, ln2_g, ln2_b, loss_target, m_w_ada, m_b_ada, m_w_in, m_g_q_a, m_w_q_b, m_g_kv_a, m_w_kv_b, m_w_o_a, m_w_conv, m_w_o_b, m_w_o, m_ln1_g, m_ln1_b, m_w_ffn_in, m_w_ffn_out, m_ln2_g, m_ln2_b, v_w_ada, v_b_ada, v_w_in, v_g_q_a, v_w_q_b, v_g_kv_a, v_w_kv_b, v_w_o_a, v_w_conv, v_w_o_b, v_w_o, v_ln1_g, v_ln1_b, v_w_ffn_in, v_w_ffn_out, v_ln2_g, v_ln2_b):
    x2, tgt = x[0], loss_target[0]
    S, D = x2.shape
    Lq, Lkv = g_q_a.shape[1], g_kv_a.shape[1]
    H = w_q_b.shape[2] * N_DEV // QK_CAT
    F = w_ffn_out.shape[1] * N_DEV
    assert Lq == Lkv and (Lq + Lkv) % COL_BLOCK == 0 and D % COL_BLOCK == 0
    front = Lq + Lkv + QK_ROPE
    front_pad = _round_up(front, COL_BLOCK)
    kr_blk = (Lq + Lkv) // COL_BLOCK
    blk_b = front_pad // COL_BLOCK
    nblk = D // COL_BLOCK
    blk_c, blk_x, blk_ga, blk_gb = blk_b + nblk, blk_b + 2 * nblk, blk_b + 3 * nblk, blk_b + 4 * nblk
    ts = _tile(S, 256, 8)
    T = _tile(S, min(512, S // 2), CHUNK)
    tb = _tile(F, 2816)
    me = _linear(_my_place())

    cw = w_ada.shape[2]
    b_mine = lax.dynamic_slice(b_ada, (0, me * cw), (1, cw)).reshape(1, 1, cw)
    mod_blocks, cact_all, wconv_all = _ada_fwd(c.reshape(1, 1, D), w_conv[0].reshape(1, 1, -1), w_ada[0], b_mine)
    mod = mod_blocks.reshape(6, D)
    cact_all = cact_all.reshape(N_DEV, D)
    w_conv_full = wconv_all.reshape(N_DEV, CONV_K, -1).transpose(1, 0, 2).reshape(CONV_K, D)

    landing = lambda shards: [_landing_zone((N_DEV,) + s.shape, BF16) for s in shards]
    gathered = lambda lands, shards, tag: [_with_own_slot(g, s) for g, s in
                                           zip(_gather_forward(lands, tag + "_gather_forward"), shards)]
    half = D // 2
    w_in_b = w_in[0].astype(BF16)
    first, second = [w_in_b[:half]], [w_in_b[half:], w_q_b[0].astype(BF16), w_kv_b[0].astype(BF16)]
    mid = [w[0].astype(BF16) for w in (w_o_a, w_o_b, w_o)]
    last = [w[0].astype(BF16) for w in (w_ffn_in, w_ffn_out)]
    first_state = _split_start("first_gather_start", first, landing(first), mod_blocks, _gather_copies, 4)
    second_state = _split_start("second_gather_start", second, landing(second), first_state[4], _gather_copies, 4)
    u = _modulate_in(x2, mod, ts)

    first_shards, first_lands = _split_wait("first_gather_wait", first_state, (u, second_state[4]), _gather_copies)
    (g_in_top,) = gathered(first_lands, first_shards, "first")
    w_in_top = _assemble_w_in(g_in_top, front, front_pad, D, 0)
    proj_top = _matmul(u, w_in_top, "nn", F32, "proj_top", k_rows=(0, half))
    second_shards, second_lands = _split_wait("second_gather_wait", second_state, (proj_top,), _gather_copies)
    g_in_bottom, wq_s, wkv_s = gathered(second_lands, second_shards, "second")
    mid_state = _split_start("mid_gather_start", mid, landing(mid), g_in_bottom, _gather_copies, 4)
    last_state = _split_start("last_gather_start", last, landing(last), mid_state[4], _gather_copies, 4)
    w_in_p = _assemble_w_in(g_in_bottom, front, front_pad, D, half, into=w_in_top)

    inv_freq = 1.0 / (ROPE_THETA ** (jnp.arange(0, QK_ROPE, 2, dtype=F32) / QK_ROPE))
    ang = positions[0].astype(F32)[:, None] * inv_freq
    cos2 = jnp.concatenate([jnp.cos(ang), jnp.cos(ang)], axis=-1)
    sin2 = jnp.concatenate([jnp.sin(ang), jnp.sin(ang)], axis=-1)
    one, zero = jnp.ones((S, QK_NOPE), F32), jnp.zeros((S, QK_NOPE), F32)
    cos_q, sin_q = jnp.concatenate([one, cos2, one, cos2], axis=-1), jnp.concatenate([zero, sin2, zero, sin2], axis=-1)
    cos_k, sin_k = jnp.tile(cos2, (1, COL_BLOCK // QK_ROPE)), jnp.tile(sin2, (1, COL_BLOCK // QK_ROPE))

    proj = _matmul(u, w_in_p, "nn", BF16, "proj", k_rows=(half, half), init=proj_top, deps=(last_state[4],))
    qn = _rms_fwd(proj, g_q_a, 0, Lq, ts, "rms_q")
    kvn = _rms_fwd(proj, g_kv_a, 1, Lkv, ts, "rms_kv")
    q = _matmul(qn, wq_s, "nn", BF16, "q_up")
    kv = _matmul(kvn, wkv_s, "nn", BF16, "kv_up")
    qc, kc, vh = _qk_prep(q, kv, proj, kr_blk, cos_q, sin_q, cos_k, sin_k, H, ts)
    attn, lse = _attn_fwd(qc, kc, vh, T)
    mid_shards, mid_lands = _split_wait("mid_gather_wait", mid_state, lse, _gather_copies)
    w_oa_f, w_ob_f, w_o_f = [g.reshape(-1, D) for g in gathered(mid_lands, mid_shards, "mid")]
    ya = _matmul(attn, w_oa_f, "nn", BF16, "attn_out")
    cbc = _conv_fwd(proj, w_conv_full, blk_b, blk_c, blk_x)
    yb = _matmul(cbc, w_ob_f, "nn", BF16, "conv_out")
    merged = _merge_fwd(proj, ya, yb, blk_ga, blk_gb, ts)
    mix = _matmul(merged, w_o_f, "nn", F32, "mix_out")
    xhat1, rstd1, u2 = _ln1_fwd(x2, mix, mod, ln1_g, ln1_b, ts)
    last_shards, last_lands = _split_wait("last_gather_wait", last_state, u2, _gather_copies)
    w_fi_s, g_fo = gathered(last_lands, last_shards, "last")
    w_fo_f = g_fo.reshape(F, D)
    hh = _matmul(u2, w_fi_s, "nn", BF16, "ffn_in")
    act = _swiglu_fwd(hh, ts, tb)
    ffn = _matmul(act, w_fo_f, "nn", F32, "ffn_out")
    loss_part, dffn, dx1a, vec2 = _ln2_loss(xhat1, ffn, tgt, mod, ln1_g, ln1_b, ln2_g, ln2_b, ts)
    loss = lax.psum(loss_part[0, 0], AXES)

    gw_fo = _matmul(act, dffn, "tn", BF16, "grad_w_ffn_out")
    da = _matmul(dffn, w_fo_f, "nt", BF16, "d_act")
    dh = _swiglu_bwd(da, hh, ts, tb)
    gw_fi = _matmul(u2, dh, "tn", BF16, "grad_w_ffn_in", out_shards=True)
    ffn_in_chip = _in_chip_start([gw_fi, gw_fo.reshape(N_DEV, -1, D)], "ffn")
    du2 = _matmul(dh, w_fi_s, "nt", F32, "d_u2", deps=(ffn_in_chip[4],))
    ffn_state = _reduce_scatter_begin(None, "ffn", ffn_in_chip, after=(du2,))
    dxa, dmix, vec1 = _ln1_bwd(du2, dx1a, xhat1, rstd1, mix, mod, ln1_g, ln1_b, ts)
    gw_o = _matmul(merged, dmix, "tn", BF16, "grad_w_o", deps=(ffn_state[4],))
    dmerged = _matmul(dmix, w_o_f, "nt", BF16, "d_merged")
    dya, dyb, dga, dgb = _merge_bwd(dmerged, proj, ya, yb, blk_ga, blk_gb, ts)
    gw_ob = _matmul(cbc, dyb, "tn", BF16, "grad_w_o_b")
    dcbc = _matmul(dyb, w_ob_f, "nt", BF16, "d_conv")
    dcb, dcc, dcx, dwconv = _conv_bwd(dcbc, proj, w_conv_full, blk_b, blk_c, blk_x)
    gw_oa = _matmul(attn, dya, "tn", BF16, "grad_w_o_a")
    mix_in_chip = _in_chip_start([g.reshape(N_DEV, -1, D) for g in (gw_oa, gw_ob, gw_o)], "mix")
    dattn = _matmul(dya, w_oa_f, "nt", F32, "d_attn", deps=(mix_in_chip[4],))
    dqc, dkc, dvh = _attn_bwd(qc, kc, vh, dattn, attn, lse, T)
    ffn_own, ffn_got = _reduce_scatter_end(ffn_state, dqc, "ffn")
    mix_state = _reduce_scatter_begin(None, "mix", mix_in_chip, after=(dqc,))
    dq, dkv, dkr = _qk_bwd(dqc, dkc, dvh, cos_q, sin_q, cos_k, sin_k, ts)
    gw_qb = _matmul(qn, dq, "tn", BF16, "grad_w_q_b", out_shards=True, deps=(mix_state[4],))
    dqn = _matmul(dq, wq_s, "nt", F32, "d_qn")
    gw_kvb = _matmul(kvn, dkv, "tn", BF16, "grad_w_kv_b", out_shards=True)
    dkvn = _matmul(dkv, wkv_s, "nt", F32, "d_kvn")
    dqa, dgq = _rms_bwd(dqn, proj, g_q_a, 0, Lq, ts, "rms_q_bwd")
    dkva, dgkv = _rms_bwd(dkvn, proj, g_kv_a, 1, Lkv, ts, "rms_kv_bwd")
    dproj = jnp.concatenate([dqa, dkva, dkr, dcb, dcc, dcx, dga, dgb], axis=1)
    gw_in_p = _matmul(u, dproj, "tn", BF16, "grad_w_in")
    mix_own, mix_got = _reduce_scatter_end(mix_state, gw_in_p, "mix")
    in_state = _reduce_scatter_begin([_split_w_in(gw_in_p, front, front_pad), gw_qb, gw_kvb], "in")
    du = _matmul(dproj, w_in_p, "nt", F32, "d_u", deps=(in_state[4],))
    grad_x, vec0 = _grad_x(du, dxa, x2, mod, ts)

    my_chip = (2 * lax.axis_index("x") + lax.axis_index("y")).astype(jnp.int32).reshape(1)
    arrived = {}
    for nm, w, m, v, own, got in (
            ("w_ffn_in", w_ffn_in, m_w_ffn_in, v_w_ffn_in, ffn_own[0], ffn_got[0]),
            ("w_ffn_out", w_ffn_out, m_w_ffn_out, v_w_ffn_out, ffn_own[1], ffn_got[1]),
            ("w_o_a", w_o_a, m_w_o_a, v_w_o_a, mix_own[0], mix_got[0]),
            ("w_o_b", w_o_b, m_w_o_b, v_w_o_b, mix_own[1], mix_got[1]),
            ("w_o", w_o, m_w_o, v_w_o, mix_own[2], mix_got[2])):
        arrived[nm] = [a[None] for a in _adamw_reduced(w[0], own, got, m[0], v[0], my_chip, "adamw_" + nm)]

    dmod = jnp.concatenate([vec0[0], vec0[1], vec1[4], vec1[0], vec1[1], vec2[2]])
    small = jnp.concatenate([dmod, dgq[0], dgkv[0], vec1[2], vec1[3], vec2[0], vec2[1], dwconv[:CONV_K].reshape(-1)])
    n_small = small.shape[0]
    nch = _round_up(n_small, cw) // cw
    payload = jnp.pad(small, (0, nch * cw - n_small)).reshape(nch, 1, cw)
    summed, dmod_mine = _ada_bwd(payload, deps=[res[1] for res in arrived.values()])
    arrived["w_ada"] = [a[None] for a in _adamw_ada(w_ada[0], cact_all.T, dmod_mine.reshape(N_DEV, cw),
                                                    m_w_ada[0], v_w_ada[0])]
    summed = summed.reshape(-1)
    offs = [0, 6 * D, 6 * D + Lq, 6 * D + Lq + Lkv]
    offs += [offs[-1] + D * k for k in range(1, 5)]
    g_b_ada = summed[offs[0]:offs[1]].reshape(1, -1)
    g_gq = summed[offs[1]:offs[2]].reshape(1, -1)
    g_gkv = summed[offs[2]:offs[3]].reshape(1, -1)
    g_ln1g, g_ln1b, g_ln2g, g_ln2b = [summed[offs[3 + k]:offs[4 + k]].reshape(1, -1) for k in range(4)]
    wc = w_conv.shape[2]
    g_wconv = lax.dynamic_slice(summed[offs[7]:offs[7] + CONV_K * D].reshape(CONV_K, D), (0, me * wc), (CONV_K, wc))

    names = ["w_ada", "b_ada", "w_in", "g_q_a", "w_q_b", "g_kv_a", "w_kv_b", "w_o_a", "w_conv", "w_o_b", "w_o",
             "ln1_g", "ln1_b", "w_ffn_in", "w_ffn_out", "ln2_g", "ln2_b"]
    weights = [w_ada, b_ada, w_in, g_q_a, w_q_b, g_kv_a, w_kv_b, w_o_a, w_conv, w_o_b, w_o, ln1_g, ln1_b,
               w_ffn_in, w_ffn_out, ln2_g, ln2_b]
    moms = [m_w_ada, m_b_ada, m_w_in, m_g_q_a, m_w_q_b, m_g_kv_a, m_w_kv_b, m_w_o_a, m_w_conv, m_w_o_b, m_w_o,
            m_ln1_g, m_ln1_b, m_w_ffn_in, m_w_ffn_out, m_ln2_g, m_ln2_b]
    vels = [v_w_ada, v_b_ada, v_w_in, v_g_q_a, v_w_q_b, v_g_kv_a, v_w_kv_b, v_w_o_a, v_w_conv, v_w_o_b, v_w_o,
            v_ln1_g, v_ln1_b, v_w_ffn_in, v_w_ffn_out, v_ln2_g, v_ln2_b]
    grad_of = {"b_ada": g_b_ada, "g_q_a": g_gq, "g_kv_a": g_gkv, "w_conv": g_wconv,
               "ln1_g": g_ln1g, "ln1_b": g_ln1b, "ln2_g": g_ln2g, "ln2_b": g_ln2b}
    state_of = dict(zip(names, zip(weights, moms, vels)))
    results = dict(arrived)

    def update(nm, reduced=None):
        w, m, v = state_of[nm]
        shp = w.shape
        w2 = w.reshape(shp[-2], shp[-1]) if w.ndim == 3 else w
        m2, v2 = m.reshape(w2.shape), v.reshape(w2.shape)
        if reduced is None:
            g2 = grad_of[nm].reshape(w2.shape)
            res = (g2,) + tuple(_adamw(w2, g2, m2, v2, "adamw_" + nm))
        else:
            res = _adamw_reduced(w2, reduced[0], reduced[1], m2, v2, my_chip, "adamw_" + nm)
        results[nm] = [a.reshape(shp) for a in res]

    for nm in grad_of:
        update(nm)
    in_own, in_got = _reduce_scatter_end(in_state, [res[1] for res in results.values()], "in")
    for nm, own, got in zip(("w_in", "w_q_b", "w_kv_b"), in_own, in_got):
        update(nm, (own, got))
    outs = [[results[nm][k] for nm in names] for k in range(4)]
    return (loss, grad_x.reshape(x.shape), *outs[0], *outs[1], *outs[2], *outs[3])
```

```python
import functools

import jax
import jax.numpy as jnp
from jax import lax
from jax.experimental import pallas as pl
from jax.experimental.pallas import tpu as pltpu

F32 = jnp.float32
BF16 = jnp.bfloat16
MESH_ID = pl.DeviceIdType.MESH
AXES = ("x", "y", "c")
N_DEV = 8

CHUNK = 64
QK_NOPE = 128
QK_ROPE = 64
V_HEAD = 128
QK_CAT = QK_NOPE + QK_ROPE
ROPE_THETA = 10000.0
ATTN_SCALE = (QK_NOPE + QK_ROPE) ** -0.5
CONV_K = 3
DEEPNORM_ALPHA = 2.0 ** 0.25
LN_EPS = 1e-5
RMS_EPS = 1e-6
NEG_INF = -1e30

ADAM_LR = 0.001
ADAM_B1 = 0.9
ADAM_B2 = 0.999
ADAM_EPS = 1e-08
ADAM_WD = 0.01
ADAM_STEP = 10

LANE = 128
COL_BLOCK = 256
PACK_ROW_ALIGN = 16
PAIR_SUM_BLOCK = 1 << 20
VMEM_LIMIT = 48 * 1024 * 1024


def _round_up(n, m):
    return (n + m - 1) // m * m


def _tile(n, pref, align=LANE):
    best = None
    t = align
    while t <= min(n, pref):
        if n % t == 0:
            best = t
        t += align
    return best if best is not None else n


def _cparams(sem=None):
    return pltpu.CompilerParams(dimension_semantics=sem, vmem_limit_bytes=VMEM_LIMIT)


def _sigmoid(x):
    return 0.5 * jnp.tanh(0.5 * x) + 0.5


def _matmul(a, b, mode, out_dtype, name, tm=1024, tn=1024, tk=2048, deps=(), out_shards=False, k_rows=None,
            init=None):
    b_shards = b.ndim == 3
    n = b.shape[2] if b_shards else (b.shape[1] // N_DEV if out_shards else None)
    if mode == "nn":
        (M, K), (K2, N) = a.shape, (b.shape[1], N_DEV * n) if b_shards else b.shape
    elif mode == "nt":
        (M, K), (N, K2) = a.shape, (b.shape[1], N_DEV * n) if b_shards else b.shape
    else:
        (K, M), (K2, N) = a.shape, b.shape
    assert K == K2, (a.shape, b.shape, mode)
    tm = _tile(M, tm)
    tn = n if (mode != "nt" and n is not None) else _tile(N, tn)
    k_row0, k_len = k_rows if k_rows is not None else (0, K)
    tk = n if (mode == "nt" and b_shards) else _tile(k_len, tk)
    nk, k0 = k_len // tk, k_row0 // tk
    if mode == "nn":
        a_spec = pl.BlockSpec((tm, tk), lambda i, j, k: (i, k0 + k))
        b_spec = (pl.BlockSpec((1, tk, n), lambda i, j, k: (j, k, 0)) if b_shards
                  else pl.BlockSpec((tk, tn), lambda i, j, k: (k0 + k, j)))
        dims = (((1,), (0,)), ((), ()))
    elif mode == "nt":
        a_spec = pl.BlockSpec((tm, tk), lambda i, j, k: (i, k))
        b_spec = (pl.BlockSpec((1, tn, n), lambda i, j, k: (k, j, 0)) if b_shards
                  else pl.BlockSpec((tn, tk), lambda i, j, k: (j, k)))
        dims = (((1,), (1,)), ((), ()))
    else:
        a_spec = pl.BlockSpec((tk, tm), lambda i, j, k: (k, i))
        b_spec = pl.BlockSpec((tk, tn), lambda i, j, k: (k, j))
        dims = (((0,), (0,)), ((), ()))
    if out_shards:
        out_spec = pl.BlockSpec((1, tm, n), lambda i, j, k: (j, i, 0))
        out_shape = jax.ShapeDtypeStruct((N_DEV, M, n), out_dtype)
    else:
        out_spec = pl.BlockSpec((tm, tn), lambda i, j, k: (i, j))
        out_shape = jax.ShapeDtypeStruct((M, N), out_dtype)

    def product(a_ref, b_ref):
        b_blk = b_ref[0] if b_shards else b_ref[...]
        return lax.dot_general(a_ref[...].astype(BF16), b_blk.astype(BF16), dims, preferred_element_type=F32)

    def write(o_ref, value):
        if out_shards:
            o_ref[0] = value.astype(o_ref.dtype)
        else:
            o_ref[...] = value.astype(o_ref.dtype)

    def body_whole_k(a_ref, b_ref, *rest):
        value = product(a_ref, b_ref)
        write(rest[-1], value if init is None else value + rest[0][...])

    def body_split_k(a_ref, b_ref, *rest):
        o_ref, acc_ref = rest[-2:]
        k = pl.program_id(2)

        @pl.when(k == 0)
        def _():
            acc_ref[...] = jnp.zeros_like(acc_ref) if init is None else rest[0][...].astype(F32)

        acc_ref[...] += product(a_ref, b_ref)

        @pl.when(k == nk - 1)
        def _():
            write(o_ref, acc_ref[...])

    return pl.pallas_call(
        body_whole_k if nk == 1 else body_split_k, name=name, grid=(M // tm, N // tn, nk),
        in_specs=[a_spec, b_spec] + ([] if init is None else [out_spec]) + [ANY_SPEC] * len(deps),
        out_specs=out_spec, out_shape=out_shape,
        scratch_shapes=[] if nk == 1 else [pltpu.VMEM((tm, tn), F32)],
        compiler_params=_cparams(("parallel", "parallel", "arbitrary")),
    )(a, b, *(() if init is None else (init,)), *deps)


def _assemble_w_in(shards, front, front_pad, rows, row0, into=None):
    _, K, n = shards.shape
    gap = front_pad - front
    tk = _tile(K, 256, PACK_ROW_ALIGN)
    blk0 = row0 // tk

    def body(g_ref, *rest):
        o_ref = rest[-1]
        if gap:
            o_ref[:, front:front_pad] = jnp.zeros((tk, gap), o_ref.dtype)
        for j in range(N_DEV):
            lo, hi = j * n, (j + 1) * n
            if lo < front < hi:
                o_ref[:, lo:front] = g_ref[j, :, 0:front - lo]
                o_ref[:, front_pad:hi + gap] = g_ref[j, :, front - lo:n]
            else:
                off = 0 if hi <= front else gap
                o_ref[:, lo + off:hi + off] = g_ref[j]

    return pl.pallas_call(
        body, name="assemble_w_in", grid=(K // tk,),
        in_specs=[pl.BlockSpec((N_DEV, tk, n), lambda i: (0, i, 0))] + ([] if into is None else [ANY_SPEC]),
        out_specs=pl.BlockSpec((tk, N_DEV * n + gap), lambda i: (blk0 + i, 0)),
        out_shape=jax.ShapeDtypeStruct((rows, N_DEV * n + gap), shards.dtype),
        input_output_aliases={} if into is None else {1: 0},
        compiler_params=_cparams(("parallel",)),
    )(*([shards] if into is None else [shards, into]))


def _split_w_in(w, front, front_pad):
    K, NP = w.shape
    gap = front_pad - front
    n = (NP - gap) // N_DEV
    tk = _tile(K, 256, PACK_ROW_ALIGN)

    def body(w_ref, o_ref):
        for j in range(N_DEV):
            lo, hi = j * n, (j + 1) * n
            if lo < front < hi:
                o_ref[j, :, 0:front - lo] = w_ref[:, lo:front]
                o_ref[j, :, front - lo:n] = w_ref[:, front_pad:hi + gap]
            else:
                off = 0 if hi <= front else gap
                o_ref[j] = w_ref[:, lo + off:hi + off]

    return pl.pallas_call(
        body, name="split_grad_w_in", grid=(K // tk,),
        in_specs=[pl.BlockSpec((tk, NP), lambda i: (i, 0))],
        out_specs=pl.BlockSpec((N_DEV, tk, n), lambda i: (0, i, 0)),
        out_shape=jax.ShapeDtypeStruct((N_DEV, K, n), w.dtype),
        compiler_params=_cparams(("parallel",)),
    )(w)


def _modulate_in(x, mod, ts):
    S, D = x.shape

    def body(x_ref, mod_ref, u_ref):
        u_ref[...] = (x_ref[...] * (1.0 + mod_ref[1:2, :]) + mod_ref[0:1, :]).astype(BF16)

    return pl.pallas_call(
        body, name="modulate_in", grid=(S // ts,),
        in_specs=[pl.BlockSpec((ts, D), lambda i: (i, 0)), pl.BlockSpec((6, D), lambda i: (0, 0))],
        out_specs=pl.BlockSpec((ts, D), lambda i: (i, 0)),
        out_shape=jax.ShapeDtypeStruct((S, D), BF16),
        compiler_params=_cparams(("parallel",)),
    )(x, mod)


def _rms_fwd(proj, g, blk, L, ts, name):
    S = proj.shape[0]

    def body(a_ref, g_ref, y_ref):
        a = a_ref[...].astype(F32)
        r = lax.rsqrt(jnp.mean(a * a, axis=-1, keepdims=True) + RMS_EPS)
        y_ref[...] = (a * r * g_ref[...]).astype(BF16)

    return pl.pallas_call(
        body, name=name, grid=(S // ts,),
        in_specs=[pl.BlockSpec((ts, L), lambda i: (i, blk)), pl.BlockSpec((1, L), lambda i: (0, 0))],
        out_specs=pl.BlockSpec((ts, L), lambda i: (i, 0)),
        out_shape=jax.ShapeDtypeStruct((S, L), BF16),
        compiler_params=_cparams(("parallel",)),
    )(proj, g)


def _rope_partner(x, period, start):
    w = x.shape[-1]
    lane = lax.broadcasted_iota(jnp.int32, x.shape, x.ndim - 1) % period
    first = (lane >= start) & (lane < start + QK_ROPE // 2)
    from_right = pltpu.roll(x, w - QK_ROPE // 2, axis=x.ndim - 1)
    from_left = pltpu.roll(x, QK_ROPE // 2, axis=x.ndim - 1)
    return jnp.where(first, -from_right, from_left)


def _qk_prep(q, kv, proj, kr_blk, cos_q, sin_q, cos_k, sin_k, H, ts):
    S = q.shape[0]
    pair = 2 * QK_CAT
    kv_w = QK_NOPE + V_HEAD

    def body(q_ref, kv_ref, kr_ref, cq_ref, sq_ref, ck_ref, sk_ref, qc_ref, kc_ref, vh_ref):
        kr = kr_ref[...].astype(F32)
        kr = kr * ck_ref[...] + _rope_partner(kr, QK_ROPE, 0) * sk_ref[...]
        kr = kr[:, :QK_ROPE].astype(BF16)
        for p in range(H // 2):
            x = q_ref[:, p * pair:(p + 1) * pair].astype(F32)
            x = x * cq_ref[...] + _rope_partner(x, QK_CAT, QK_NOPE) * sq_ref[...]
            qc_ref[2 * p] = x[:, :QK_CAT].astype(BF16)
            qc_ref[2 * p + 1] = x[:, QK_CAT:].astype(BF16)
        for h in range(H):
            kc_ref[h, :, 0:QK_NOPE] = kv_ref[:, h * kv_w:h * kv_w + QK_NOPE].astype(BF16)
            kc_ref[h, :, QK_NOPE:QK_CAT] = kr
            vh_ref[h, :, :] = kv_ref[:, h * kv_w + QK_NOPE:(h + 1) * kv_w].astype(BF16)

    row = lambda w: pl.BlockSpec((ts, w), lambda i: (i, 0))
    return pl.pallas_call(
        body, name="qk_prep", grid=(S // ts,),
        in_specs=[row(H * QK_CAT), row(H * kv_w),
                  pl.BlockSpec((ts, COL_BLOCK), lambda i: (i, kr_blk)),
                  row(pair), row(pair), row(COL_BLOCK), row(COL_BLOCK)],
        out_specs=[pl.BlockSpec((H, ts, QK_CAT), lambda i: (0, i, 0)),
                   pl.BlockSpec((H, ts, QK_CAT), lambda i: (0, i, 0)),
                   pl.BlockSpec((H, ts, V_HEAD), lambda i: (0, i, 0))],
        out_shape=[jax.ShapeDtypeStruct((H, S, QK_CAT), BF16), jax.ShapeDtypeStruct((H, S, QK_CAT), BF16),
                   jax.ShapeDtypeStruct((H, S, V_HEAD), BF16)],
        compiler_params=_cparams(("parallel",)),
    )(q, kv, proj, cos_q, sin_q, cos_k, sin_k)


NT_DIMS = (((1,), (1,)), ((), ()))
TN_DIMS = (((0,), (0,)), ((), ()))


def _diag_mask(T):
    rows = lax.broadcasted_iota(jnp.int32, (T, T), 0) // CHUNK
    cols = lax.broadcasted_iota(jnp.int32, (T, T), 1) // CHUNK
    return cols <= rows


def _attn_fwd(qc, kc, vh, T):
    H, S, _ = qc.shape
    n = S // T

    def body(q_ref, k_ref, v_ref, o_ref, lse_ref):
        q = q_ref[0]

        def block(i):
            L = (i + 1) * T
            s_old = lax.dot_general(q, k_ref[0, 0:i * T, :], NT_DIMS, preferred_element_type=F32) if i else None
            s_diag = lax.dot_general(q, k_ref[0, i * T:L, :], NT_DIMS, preferred_element_type=F32)
            s_diag = jnp.where(_diag_mask(T), s_diag, NEG_INF)
            m = jnp.max(s_diag, axis=-1, keepdims=True)
            if i:
                m = jnp.maximum(m, jnp.max(s_old, axis=-1, keepdims=True))
            p_diag = jnp.exp((s_diag - m) * ATTN_SCALE)
            l = jnp.sum(p_diag, axis=-1, keepdims=True)
            acc = jnp.dot(p_diag.astype(BF16), v_ref[0, i * T:L, :], preferred_element_type=F32)
            if i:
                p_old = jnp.exp((s_old - m) * ATTN_SCALE)
                l = l + jnp.sum(p_old, axis=-1, keepdims=True)
                acc = acc + jnp.dot(p_old.astype(BF16), v_ref[0, 0:i * T, :], preferred_element_type=F32)
            o_ref[...] = (acc / l).astype(o_ref.dtype)
            lse_ref[0] = m * ATTN_SCALE + jnp.log(l)

        for i in range(n):
            pl.when(pl.program_id(1) == i)(functools.partial(block, i))

    return pl.pallas_call(
        body, name="attn_fwd", grid=(H, n),
        in_specs=[pl.BlockSpec((1, T, QK_CAT), lambda h, i: (h, i, 0)),
                  pl.BlockSpec((1, S, QK_CAT), lambda h, i: (h, 0, 0)),
                  pl.BlockSpec((1, S, V_HEAD), lambda h, i: (h, 0, 0))],
        out_specs=[pl.BlockSpec((T, V_HEAD), lambda h, i: (i, h)),
                   pl.BlockSpec((1, T, 1), lambda h, i: (h, i, 0))],
        out_shape=[jax.ShapeDtypeStruct((S, H * V_HEAD), BF16), jax.ShapeDtypeStruct((H, S, 1), F32)],
        compiler_params=_cparams(("parallel", "arbitrary")),
    )(qc, kc, vh)


def _shift_rows(z, k):
    if k == 0:
        return z
    n = z.shape[0]
    row = lax.broadcasted_iota(jnp.int32, z.shape, 0)
    if k > 0:
        return jnp.where(row >= k, pltpu.roll(z, k, axis=0), 0.0)
    return jnp.where(row < n + k, pltpu.roll(z, n + k, axis=0), 0.0)


def _conv_fwd(proj, w_conv, blk_b, blk_c, blk_x):
    S = proj.shape[0]
    D = w_conv.shape[1]
    nb = D // COL_BLOCK

    def body(cb_ref, cc_ref, cx_ref, w_ref, o_ref):
        z = cc_ref[...].astype(F32) * cx_ref[...].astype(F32)
        conv = w_ref[2:3, :] * z + w_ref[1:2, :] * _shift_rows(z, 1) + w_ref[0:1, :] * _shift_rows(z, 2)
        o_ref[...] = (cb_ref[...].astype(F32) * conv).astype(BF16)

    col = lambda off: pl.BlockSpec((S, COL_BLOCK), lambda j: (0, off + j))
    return pl.pallas_call(
        body, name="conv_fwd", grid=(nb,),
        in_specs=[col(blk_b), col(blk_c), col(blk_x), pl.BlockSpec((CONV_K, COL_BLOCK), lambda j: (0, j))],
        out_specs=pl.BlockSpec((S, COL_BLOCK), lambda j: (0, j)),
        out_shape=jax.ShapeDtypeStruct((S, D), BF16),
        compiler_params=_cparams(("parallel",)),
    )(proj, proj, proj, w_conv)


def _merge_fwd(proj, ya, yb, blk_ga, blk_gb, ts):
    S, D = ya.shape
    nb = D // COL_BLOCK

    def body(ga_ref, gb_ref, ya_ref, yb_ref, o_ref):
        sa, sb = _sigmoid(ga_ref[...].astype(F32)), _sigmoid(gb_ref[...].astype(F32))
        o_ref[...] = (sa * ya_ref[...].astype(F32) + sb * yb_ref[...].astype(F32)).astype(BF16)

    row = pl.BlockSpec((ts, D), lambda i: (i, 0))
    seg = lambda blk: pl.BlockSpec((pl.Element(ts), pl.Element(D)), lambda i: (i * ts, blk * COL_BLOCK))
    return pl.pallas_call(
        body, name="merge_fwd", grid=(S // ts,),
        in_specs=[seg(blk_ga), seg(blk_gb), row, row],
        out_specs=row,
        out_shape=jax.ShapeDtypeStruct((S, D), BF16),
        compiler_params=_cparams(("parallel",)),
    )(proj, proj, ya, yb)


def _ln1_fwd(x, mix, mod, g, b, ts):
    S, D = x.shape

    def body(x_ref, mix_ref, mod_ref, g_ref, b_ref, xhat_ref, rstd_ref, u2_ref):
        r = DEEPNORM_ALPHA * x_ref[...] + mod_ref[2:3, :] * mix_ref[...]
        mu = jnp.mean(r, axis=-1, keepdims=True)
        d = r - mu
        rstd = lax.rsqrt(jnp.mean(d * d, axis=-1, keepdims=True) + LN_EPS)
        xhat = d * rstd
        xhat_ref[...] = xhat
        rstd_ref[...] = rstd
        x1 = xhat * g_ref[...] + b_ref[...]
        u2_ref[...] = (x1 * (1.0 + mod_ref[4:5, :]) + mod_ref[3:4, :]).astype(BF16)

    row = pl.BlockSpec((ts, D), lambda i: (i, 0))
    vec = lambda r: pl.BlockSpec((r, D), lambda i: (0, 0))
    return pl.pallas_call(
        body, name="ln1_fwd", grid=(S // ts,),
        in_specs=[row, row, vec(6), vec(1), vec(1)],
        out_specs=[row, pl.BlockSpec((ts, 1), lambda i: (i, 0)), row],
        out_shape=[jax.ShapeDtypeStruct((S, D), F32), jax.ShapeDtypeStruct((S, 1), F32),
                   jax.ShapeDtypeStruct((S, D), BF16)],
        compiler_params=_cparams(("parallel",)),
    )(x, mix, mod, g, b)


def _swiglu_fwd(h, ts, tb):
    S, F2 = h.shape
    F = F2 // 2
    nb = F // tb

    def body(hg_ref, hu_ref, a_ref):
        hg = hg_ref[...].astype(F32)
        a_ref[...] = (hg * _sigmoid(hg) * hu_ref[...].astype(F32)).astype(BF16)

    return pl.pallas_call(
        body, name="swiglu_fwd", grid=(S // ts, nb),
        in_specs=[pl.BlockSpec((ts, tb), lambda i, j: (i, j)), pl.BlockSpec((ts, tb), lambda i, j: (i, j + nb))],
        out_specs=pl.BlockSpec((ts, tb), lambda i, j: (i, j)),
        out_shape=jax.ShapeDtypeStruct((S, F), BF16),
        compiler_params=_cparams(("parallel", "parallel")),
    )(h, h)


def _ln2_loss(xhat1, ffn, tgt, mod, g1, b1, g2, b2, ts):
    S, D = xhat1.shape

    def body(xh_ref, ffn_ref, t_ref, mod_ref, g1_ref, b1_ref, g2_ref, b2_ref, loss_ref, dffn_ref, dx1_ref, vec_ref):
        i = pl.program_id(0)

        @pl.when(i == 0)
        def _():
            loss_ref[...] = jnp.zeros_like(loss_ref)
            vec_ref[...] = jnp.zeros_like(vec_ref)

        x1 = xh_ref[...] * g1_ref[...] + b1_ref[...]
        ffn = ffn_ref[...]
        r = DEEPNORM_ALPHA * x1 + mod_ref[5:6, :] * ffn
        mu = jnp.mean(r, axis=-1, keepdims=True)
        d = r - mu
        rstd = lax.rsqrt(jnp.mean(d * d, axis=-1, keepdims=True) + LN_EPS)
        xhat = d * rstd
        e = xhat * g2_ref[...] + b2_ref[...] - t_ref[...]
        loss_ref[...] += 0.5 * jnp.sum(jnp.mean(e * e, axis=-1, keepdims=True))
        dy = e * (1.0 / D)
        dxhat = dy * g2_ref[...]
        dr = rstd * (dxhat - jnp.mean(dxhat, axis=-1, keepdims=True)
                     - xhat * jnp.mean(dxhat * xhat, axis=-1, keepdims=True))
        dffn_ref[...] = (dr * mod_ref[5:6, :]).astype(BF16)
        dx1_ref[...] = DEEPNORM_ALPHA * dr
        vec_ref[0:1, :] += jnp.sum(dy * xhat, axis=0, keepdims=True)
        vec_ref[1:2, :] += jnp.sum(dy, axis=0, keepdims=True)
        vec_ref[2:3, :] += jnp.sum(dr * ffn, axis=0, keepdims=True)

    row = pl.BlockSpec((ts, D), lambda i: (i, 0))
    vec = lambda r: pl.BlockSpec((r, D), lambda i: (0, 0))
    return pl.pallas_call(
        body, name="ln2_loss", grid=(S // ts,),
        in_specs=[row, row, row, vec(6), vec(1), vec(1), vec(1), vec(1)],
        out_specs=[pl.BlockSpec((1, LANE), lambda i: (0, 0)), row, row, vec(8)],
        out_shape=[jax.ShapeDtypeStruct((1, LANE), F32), jax.ShapeDtypeStruct((S, D), BF16),
                   jax.ShapeDtypeStruct((S, D), F32), jax.ShapeDtypeStruct((8, D), F32)],
        compiler_params=_cparams(("arbitrary",)),
    )(xhat1, ffn, tgt, mod, g1, b1, g2, b2)


def _swiglu_bwd(da, h, ts, tb):
    S, F2 = h.shape
    nb = (F2 // 2) // tb

    def body(da_ref, hg_ref, hu_ref, dh_ref):
        hg, da = hg_ref[...].astype(F32), da_ref[...].astype(F32)
        sg = _sigmoid(hg)

        @pl.when(pl.program_id(2) == 0)
        def _():
            dh_ref[...] = (da * hu_ref[...].astype(F32) * (sg * (1.0 + hg * (1.0 - sg)))).astype(BF16)

        @pl.when(pl.program_id(2) == 1)
        def _():
            dh_ref[...] = (da * hg * sg).astype(BF16)

    lo = pl.BlockSpec((ts, tb), lambda i, j, k: (i, j))
    hi = pl.BlockSpec((ts, tb), lambda i, j, k: (i, j + nb))
    return pl.pallas_call(
        body, name="swiglu_bwd", grid=(S // ts, nb, 2),
        in_specs=[lo, lo, hi],
        out_specs=pl.BlockSpec((ts, tb), lambda i, j, k: (i, j + nb * k)),
        out_shape=jax.ShapeDtypeStruct((S, F2), BF16),
        compiler_params=_cparams(("parallel", "parallel", "arbitrary")),
    )(da, h, h)


def _ln1_bwd(du2, dx1a, xhat1, rstd1, mix, mod, g1, b1, ts):
    S, D = xhat1.shape

    def body(du2_ref, dx1a_ref, xh_ref, rstd_ref, mix_ref, mod_ref, g_ref, b_ref, dxa_ref, dmix_ref, vec_ref):
        i = pl.program_id(0)

        @pl.when(i == 0)
        def _():
            vec_ref[...] = jnp.zeros_like(vec_ref)

        xhat, du2, mix = xh_ref[...], du2_ref[...], mix_ref[...]
        x1 = xhat * g_ref[...] + b_ref[...]
        dx1 = dx1a_ref[...] + du2 * (1.0 + mod_ref[4:5, :])
        dxhat = dx1 * g_ref[...]
        dr = rstd_ref[...] * (dxhat - jnp.mean(dxhat, axis=-1, keepdims=True)
                              - xhat * jnp.mean(dxhat * xhat, axis=-1, keepdims=True))
        dxa_ref[...] = DEEPNORM_ALPHA * dr
        dmix_ref[...] = (dr * mod_ref[2:3, :]).astype(BF16)
        vec_ref[0:1, :] += jnp.sum(du2, axis=0, keepdims=True)
        vec_ref[1:2, :] += jnp.sum(du2 * x1, axis=0, keepdims=True)
        vec_ref[2:3, :] += jnp.sum(dx1 * xhat, axis=0, keepdims=True)
        vec_ref[3:4, :] += jnp.sum(dx1, axis=0, keepdims=True)
        vec_ref[4:5, :] += jnp.sum(dr * mix, axis=0, keepdims=True)

    row = pl.BlockSpec((ts, D), lambda i: (i, 0))
    vec = lambda r: pl.BlockSpec((r, D), lambda i: (0, 0))
    return pl.pallas_call(
        body, name="ln1_bwd", grid=(S // ts,),
        in_specs=[row, row, row, pl.BlockSpec((ts, 1), lambda i: (i, 0)), row, vec(6), vec(1), vec(1)],
        out_specs=[row, row, vec(8)],
        out_shape=[jax.ShapeDtypeStruct((S, D), F32), jax.ShapeDtypeStruct((S, D), BF16),
                   jax.ShapeDtypeStruct((8, D), F32)],
        compiler_params=_cparams(("arbitrary",)),
    )(du2, dx1a, xhat1, rstd1, mix, mod, g1, b1)


def _merge_bwd(dmerged, proj, ya, yb, blk_ga, blk_gb, ts):
    S, D = ya.shape
    nb = D // COL_BLOCK

    def body(dm_ref, ga_ref, gb_ref, ya_ref, yb_ref, dya_ref, dyb_ref, dga_ref, dgb_ref):
        dm = dm_ref[...].astype(F32)
        sa, sb = _sigmoid(ga_ref[...].astype(F32)), _sigmoid(gb_ref[...].astype(F32))
        dya_ref[...] = (dm * sa).astype(BF16)
        dyb_ref[...] = (dm * sb).astype(BF16)
        dga_ref[...] = (dm * ya_ref[...].astype(F32) * sa * (1.0 - sa)).astype(BF16)
        dgb_ref[...] = (dm * yb_ref[...].astype(F32) * sb * (1.0 - sb)).astype(BF16)

    row = pl.BlockSpec((ts, D), lambda i: (i, 0))
    seg = lambda blk: pl.BlockSpec((pl.Element(ts), pl.Element(D)), lambda i: (i * ts, blk * COL_BLOCK))
    out = jax.ShapeDtypeStruct((S, D), BF16)
    return pl.pallas_call(
        body, name="merge_bwd", grid=(S // ts,),
        in_specs=[row, seg(blk_ga), seg(blk_gb), row, row],
        out_specs=[row] * 4,
        out_shape=[out] * 4,
        compiler_params=_cparams(("parallel",)),
    )(dmerged, proj, proj, ya, yb)


def _conv_bwd(dcbc, proj, w_conv, blk_b, blk_c, blk_x):
    S = proj.shape[0]
    D = w_conv.shape[1]
    nb = D // COL_BLOCK

    def body(d_ref, cb_ref, cc_ref, cx_ref, w_ref, dcb_ref, dcc_ref, dcx_ref, dw_ref):
        d, cc, cx = d_ref[...].astype(F32), cc_ref[...].astype(F32), cx_ref[...].astype(F32)
        z = cc * cx
        z1, z2 = _shift_rows(z, 1), _shift_rows(z, 2)
        conv = w_ref[2:3, :] * z + w_ref[1:2, :] * z1 + w_ref[0:1, :] * z2
        dcb_ref[...] = (d * conv).astype(BF16)
        dconv = d * cb_ref[...].astype(F32)
        dz = w_ref[2:3, :] * dconv + w_ref[1:2, :] * _shift_rows(dconv, -1) + w_ref[0:1, :] * _shift_rows(dconv, -2)
        dcc_ref[...] = (dz * cx).astype(BF16)
        dcx_ref[...] = (dz * cc).astype(BF16)
        dw_ref[...] = jnp.zeros_like(dw_ref)
        dw_ref[0:1, :] = jnp.sum(dconv * z2, axis=0, keepdims=True)
        dw_ref[1:2, :] = jnp.sum(dconv * z1, axis=0, keepdims=True)
        dw_ref[2:3, :] = jnp.sum(dconv * z, axis=0, keepdims=True)

    col = lambda off: pl.BlockSpec((S, COL_BLOCK), lambda j: (0, off + j))
    out = jax.ShapeDtypeStruct((S, D), BF16)
    return pl.pallas_call(
        body, name="conv_bwd", grid=(nb,),
        in_specs=[col(0), col(blk_b), col(blk_c), col(blk_x), pl.BlockSpec((CONV_K, COL_BLOCK), lambda j: (0, j))],
        out_specs=[col(0), col(0), col(0), pl.BlockSpec((8, COL_BLOCK), lambda j: (0, j))],
        out_shape=[out, out, out, jax.ShapeDtypeStruct((8, D), F32)],
        compiler_params=_cparams(("parallel",)),
    )(dcbc, proj, proj, proj, w_conv)


def _attn_bwd(qc, kc, vh, do, o, lse, T):
    H, S, _ = qc.shape
    n = S // T

    def body(q_ref, k_ref, v_ref, do_ref, o_ref, lse_ref, dq_ref, dk_ref, dv_ref, d_ref, dq_acc, dk_acc, dv_acc):
        j = pl.program_id(1)

        @pl.when(j == 0)
        def _():
            dq_acc[...] = jnp.zeros_like(dq_acc)
            d_ref[...] = jnp.sum(do_ref[...].astype(F32) * o_ref[...].astype(F32), axis=-1, keepdims=True)

        dk_acc[...] = jnp.zeros_like(dk_acc)
        dv_acc[...] = jnp.zeros_like(dv_acc)
        k, v = k_ref[0], v_ref[0]

        def step(i, masked):
            rows = pl.ds(pl.multiple_of(i * T, T), T)
            q = q_ref[0, rows, :]
            do = do_ref[rows, :].astype(BF16)
            s = lax.dot_general(q, k, NT_DIMS, preferred_element_type=F32) * ATTN_SCALE
            if masked:
                s = jnp.where(_diag_mask(T), s, NEG_INF)
            p = jnp.exp(s - lse_ref[0, rows, :])
            dv_acc[...] += lax.dot_general(p.astype(BF16), do, TN_DIMS, preferred_element_type=F32)
            dp = lax.dot_general(do, v, NT_DIMS, preferred_element_type=F32)
            ds = (p * (dp - d_ref[rows, :]) * ATTN_SCALE).astype(BF16)
            dk_acc[...] += lax.dot_general(ds, q, TN_DIMS, preferred_element_type=F32)
            dq_acc[rows, :] += jnp.dot(ds, k, preferred_element_type=F32)

        def above(i, carry):
            step(i, False)
            return carry

        step(j, True)
        lax.fori_loop(j + 1, n, above, 0)
        dk_ref[0] = dk_acc[...].astype(BF16)
        dv_ref[0] = dv_acc[...].astype(BF16)

        @pl.when(j == n - 1)
        def _():
            dq_ref[0] = dq_acc[...].astype(BF16)

    head = lambda w: pl.BlockSpec((1, S, w), lambda h, j: (h, 0, 0))
    blk = lambda w: pl.BlockSpec((1, T, w), lambda h, j: (h, j, 0))
    ospec = pl.BlockSpec((S, V_HEAD), lambda h, j: (0, h))
    return pl.pallas_call(
        body, name="attn_bwd", grid=(H, n),
        in_specs=[head(QK_CAT), blk(QK_CAT), blk(V_HEAD), ospec, ospec, head(1)],
        out_specs=[head(QK_CAT), blk(QK_CAT), blk(V_HEAD)],
        out_shape=[jax.ShapeDtypeStruct((H, S, QK_CAT), BF16), jax.ShapeDtypeStruct((H, S, QK_CAT), BF16),
                   jax.ShapeDtypeStruct((H, S, V_HEAD), BF16)],
        scratch_shapes=[pltpu.VMEM((S, 1), F32), pltpu.VMEM((S, QK_CAT), F32), pltpu.VMEM((T, QK_CAT), F32),
                        pltpu.VMEM((T, V_HEAD), F32)],
        compiler_params=_cparams(("parallel", "arbitrary")),
    )(qc, kc, vh, do, o, lse)


def _qk_bwd(dqc, dkc, dvh, cos_q, sin_q, cos_k, sin_k, ts):
    H, S, _ = dqc.shape
    pair = 2 * QK_CAT
    kv_w = QK_NOPE + V_HEAD

    def body(dqc_ref, dkc_ref, dvh_ref, cq_ref, sq_ref, ck_ref, sk_ref, dq_ref, dkv_ref, dkr_ref, q_buf, kr_buf):
        for p in range(H // 2):
            q_buf[:, :QK_CAT] = dqc_ref[2 * p].astype(F32)
            q_buf[:, QK_CAT:] = dqc_ref[2 * p + 1].astype(F32)
            g = q_buf[...]
            dq_ref[:, p * pair:(p + 1) * pair] = (
                g * cq_ref[...] - _rope_partner(g, QK_CAT, QK_NOPE) * sq_ref[...]).astype(BF16)
        kr_sum = jnp.zeros((ts, QK_ROPE), F32)
        for h in range(H):
            dkv_ref[:, h * kv_w:h * kv_w + QK_NOPE] = dkc_ref[h, :, 0:QK_NOPE].astype(BF16)
            dkv_ref[:, h * kv_w + QK_NOPE:(h + 1) * kv_w] = dvh_ref[h].astype(BF16)
            kr_sum = kr_sum + dkc_ref[h, :, QK_NOPE:QK_CAT]
        kr_buf[...] = jnp.zeros_like(kr_buf)
        kr_buf[:, 0:QK_ROPE] = kr_sum
        kr = kr_buf[...]
        dkr_ref[...] = (kr * ck_ref[...] - _rope_partner(kr, QK_ROPE, 0) * sk_ref[...]).astype(BF16)

    row = lambda w: pl.BlockSpec((ts, w), lambda i: (i, 0))
    head = lambda w: pl.BlockSpec((H, ts, w), lambda i: (0, i, 0))
    return pl.pallas_call(
        body, name="qk_bwd", grid=(S // ts,),
        in_specs=[head(QK_CAT), head(QK_CAT), head(V_HEAD), row(pair), row(pair), row(COL_BLOCK), row(COL_BLOCK)],
        out_specs=[row(H * QK_CAT), row(H * kv_w), row(COL_BLOCK)],
        out_shape=[jax.ShapeDtypeStruct((S, H * QK_CAT), BF16), jax.ShapeDtypeStruct((S, H * kv_w), BF16),
                   jax.ShapeDtypeStruct((S, COL_BLOCK), BF16)],
        scratch_shapes=[pltpu.VMEM((ts, pair), F32), pltpu.VMEM((ts, COL_BLOCK), F32)],
        compiler_params=_cparams(("parallel",)),
    )(dqc, dkc, dvh, cos_q, sin_q, cos_k, sin_k)


def _rms_bwd(dy, proj, g, blk, L, ts, name):
    S = proj.shape[0]

    def body(dy_ref, a_ref, g_ref, da_ref, dg_ref):
        i = pl.program_id(0)

        @pl.when(i == 0)
        def _():
            dg_ref[...] = jnp.zeros_like(dg_ref)

        a, dy = a_ref[...].astype(F32), dy_ref[...]
        r = lax.rsqrt(jnp.mean(a * a, axis=-1, keepdims=True) + RMS_EPS)
        dyh = dy * g_ref[...]
        da = r * dyh - a * (r * r * r) * jnp.mean(dyh * a, axis=-1, keepdims=True)
        da_ref[...] = da.astype(BF16)
        dg_ref[0:1, :] += jnp.sum(dy * a * r, axis=0, keepdims=True)

    return pl.pallas_call(
        body, name=name, grid=(S // ts,),
        in_specs=[pl.BlockSpec((ts, L), lambda i: (i, 0)), pl.BlockSpec((ts, L), lambda i: (i, blk)),
                  pl.BlockSpec((1, L), lambda i: (0, 0))],
        out_specs=[pl.BlockSpec((ts, L), lambda i: (i, 0)), pl.BlockSpec((8, L), lambda i: (0, 0))],
        out_shape=[jax.ShapeDtypeStruct((S, L), BF16), jax.ShapeDtypeStruct((8, L), F32)],
        compiler_params=_cparams(("arbitrary",)),
    )(dy, proj, g)


def _grad_x(du, dxa, x, mod, ts):
    S, D = x.shape

    def body(du_ref, dxa_ref, x_ref, mod_ref, dx_ref, vec_ref):
        i = pl.program_id(0)

        @pl.when(i == 0)
        def _():
            vec_ref[...] = jnp.zeros_like(vec_ref)

        du = du_ref[...]
        dx_ref[...] = dxa_ref[...] + du * (1.0 + mod_ref[1:2, :])
        vec_ref[0:1, :] += jnp.sum(du, axis=0, keepdims=True)
        vec_ref[1:2, :] += jnp.sum(du * x_ref[...], axis=0, keepdims=True)

    row = pl.BlockSpec((ts, D), lambda i: (i, 0))
    vec = lambda r: pl.BlockSpec((r, D), lambda i: (0, 0))
    return pl.pallas_call(
        body, name="grad_x", grid=(S // ts,),
        in_specs=[row, row, row, vec(6)],
        out_specs=[row, vec(8)],
        out_shape=[jax.ShapeDtypeStruct((S, D), F32), jax.ShapeDtypeStruct((8, D), F32)],
        compiler_params=_cparams(("arbitrary",)),
    )(du, dxa, x, mod)


def _adamw(w, g, m, v, name):
    R, C = w.shape
    tr = _tile(R, max(8, (1 << 19) // C), 8)
    c1 = 1.0 / (1.0 - ADAM_B1 ** ADAM_STEP)
    c2 = 1.0 / (1.0 - ADAM_B2 ** ADAM_STEP)

    def body(w_ref, g_ref, m_ref, v_ref, d_ref, nm_ref, nv_ref):
        g = g_ref[...]
        m = ADAM_B1 * m_ref[...] + (1.0 - ADAM_B1) * g
        v = ADAM_B2 * v_ref[...] + (1.0 - ADAM_B2) * (g * g)
        nm_ref[...] = m
        nv_ref[...] = v
        d_ref[...] = -ADAM_LR * ((m * c1) / (jnp.sqrt(v * c2) + ADAM_EPS) + ADAM_WD * w_ref[...])

    spec = pl.BlockSpec((tr, C), lambda i: (i, 0))
    out = jax.ShapeDtypeStruct((R, C), F32)
    return pl.pallas_call(
        body, name=name, grid=(R // tr,),
        in_specs=[spec] * 4, out_specs=[spec] * 3, out_shape=[out] * 3,
        compiler_params=_cparams(("parallel",)),
    )(w, g, m, v)


def _adamw_ada(w, cact_t, dmod, m, v):
    R, C = w.shape
    tr = _tile(R, max(8, (1 << 18) // C), 8)
    c1 = 1.0 / (1.0 - ADAM_B1 ** ADAM_STEP)
    c2 = 1.0 / (1.0 - ADAM_B2 ** ADAM_STEP)

    def body(w_ref, ct_ref, dm_ref, m_ref, v_ref, g_ref, d_ref, nm_ref, nv_ref):
        ct = ct_ref[...].astype(BF16).astype(F32)
        dm = dm_ref[...].astype(BF16).astype(F32)
        g = ct[:, 0:1] * dm[0:1, :]
        for b in range(1, N_DEV):
            g = g + ct[:, b:b + 1] * dm[b:b + 1, :]
        m = ADAM_B1 * m_ref[...] + (1.0 - ADAM_B1) * g
        v = ADAM_B2 * v_ref[...] + (1.0 - ADAM_B2) * (g * g)
        g_ref[...] = g
        nm_ref[...] = m
        nv_ref[...] = v
        d_ref[...] = -ADAM_LR * ((m * c1) / (jnp.sqrt(v * c2) + ADAM_EPS) + ADAM_WD * w_ref[...])

    spec = pl.BlockSpec((tr, C), lambda i: (i, 0))
    out = jax.ShapeDtypeStruct((R, C), F32)
    return pl.pallas_call(
        body, name="adamw_w_ada", grid=(R // tr,),
        in_specs=[spec, pl.BlockSpec((tr, N_DEV), lambda i: (i, 0)), pl.BlockSpec((N_DEV, C), lambda i: (0, 0)),
                  spec, spec],
        out_specs=[spec] * 4, out_shape=[out] * 4,
        compiler_params=_cparams(("parallel",)),
    )(w, cact_t, dmod, m, v)


def _adamw_reduced(w, own, got, m, v, my_chip, name):
    R, C = w.shape
    tr = _tile(R, max(PACK_ROW_ALIGN, (1 << 18) // C), PACK_ROW_ALIGN)
    c1 = 1.0 / (1.0 - ADAM_B1 ** ADAM_STEP)
    c2 = 1.0 / (1.0 - ADAM_B2 ** ADAM_STEP)

    def body(chip_ref, w_ref, own_ref, g1_ref, g2_ref, g3_ref, m_ref, v_ref, g_ref, d_ref, nm_ref, nv_ref):
        g = own_ref[0].astype(F32) + g1_ref[0].astype(F32) + g2_ref[0].astype(F32) + g3_ref[0].astype(F32)
        m = ADAM_B1 * m_ref[...] + (1.0 - ADAM_B1) * g
        v = ADAM_B2 * v_ref[...] + (1.0 - ADAM_B2) * (g * g)
        g_ref[...] = g
        nm_ref[...] = m
        nv_ref[...] = v
        d_ref[...] = -ADAM_LR * ((m * c1) / (jnp.sqrt(v * c2) + ADAM_EPS) + ADAM_WD * w_ref[...])

    spec = pl.BlockSpec((tr, C), lambda i, chip: (i, 0))
    slot = lambda k: pl.BlockSpec((1, tr, C), lambda i, chip: (chip[0] ^ k, i, 0))
    out = jax.ShapeDtypeStruct((R, C), F32)
    return pl.pallas_call(
        body, name=name,
        grid_spec=pltpu.PrefetchScalarGridSpec(
            num_scalar_prefetch=1, grid=(R // tr,),
            in_specs=[spec, slot(0), slot(1), slot(2), slot(3), spec, spec],
            out_specs=[spec] * 4),
        out_shape=[out] * 4,
        compiler_params=_cparams(("parallel",)),
    )(my_chip, w, own, got, got, got, m, v)


def _my_place():
    return lax.axis_index("x"), lax.axis_index("y"), lax.axis_index("c")


def _peer(k):
    x, y, c = _my_place()
    return (x ^ ((k >> 2) & 1), y ^ ((k >> 1) & 1), c ^ (k & 1))


def _linear(place):
    return 4 * place[0] + 2 * place[1] + place[2]


def _ada_fwd(c_row, wconv_row, w_ada, b_row):
    D, CW = w_ada.shape
    WC = wconv_row.shape[-1]

    def body(c_ref, wc_ref, w_ref, b_ref, mod_ref, cact_ref, wcall_ref, send_buf, sems):
        me = _linear(_my_place())
        c = c_ref[0]
        cact_ref[me] = c * _sigmoid(c)
        wcall_ref[me] = wc_ref[0]

        def gather_copy(buf, k, grp):
            return pltpu.make_async_remote_copy(
                src_ref=buf.at[me], dst_ref=buf.at[me], send_sem=sems.at[0, grp, k], recv_sem=sems.at[1, grp, k],
                device_id=_peer(k), device_id_type=MESH_ID)

        def gather_recv(buf, k, grp):
            src = _linear(_peer(k))
            return pltpu.make_async_remote_copy(
                src_ref=buf.at[src], dst_ref=buf.at[src], send_sem=sems.at[0, grp, k], recv_sem=sems.at[1, grp, k],
                device_id=_peer(k), device_id_type=MESH_ID)

        for k in range(1, N_DEV):
            gather_copy(cact_ref, k, 0).start()
            gather_copy(wcall_ref, k, 1).start()
        for k in range(1, N_DEV):
            gather_recv(cact_ref, k, 0).wait_recv()
            gather_recv(wcall_ref, k, 1).wait_recv()
        for k in range(1, N_DEV):
            gather_copy(cact_ref, k, 0).wait_send()
            gather_copy(wcall_ref, k, 1).wait_send()

        cact = jnp.concatenate([cact_ref[b] for b in range(N_DEV)], axis=0)
        mod_all = jnp.dot(cact.astype(BF16), w_ref[...].astype(BF16), preferred_element_type=F32) + b_ref[0]
        for b in range(N_DEV):
            send_buf[b] = mod_all[b:b + 1, :]
        mod_ref[me] = send_buf[me]

        def scatter_copy(k):
            dst = _linear(_peer(k))
            return pltpu.make_async_remote_copy(
                src_ref=send_buf.at[dst], dst_ref=mod_ref.at[me], send_sem=sems.at[0, 2, k], recv_sem=sems.at[1, 2, k],
                device_id=_peer(k), device_id_type=MESH_ID)

        def scatter_recv(k):
            src = _linear(_peer(k))
            return pltpu.make_async_remote_copy(
                src_ref=send_buf.at[src], dst_ref=mod_ref.at[src], send_sem=sems.at[0, 2, k], recv_sem=sems.at[1, 2, k],
                device_id=_peer(k), device_id_type=MESH_ID)

        for k in range(1, N_DEV):
            scatter_copy(k).start()
        for k in range(1, N_DEV):
            scatter_recv(k).wait_recv()
        for k in range(1, N_DEV):
            scatter_copy(k).wait_send()

    vmem = pl.BlockSpec(memory_space=pltpu.VMEM)
    return pl.pallas_call(
        body, name="ada_fwd",
        in_specs=[vmem] * 4, out_specs=[vmem] * 3,
        out_shape=[jax.ShapeDtypeStruct((N_DEV, 1, CW), F32), jax.ShapeDtypeStruct((N_DEV, 1, D), F32),
                   jax.ShapeDtypeStruct((N_DEV, 1, WC), F32)],
        scratch_shapes=[pltpu.VMEM((N_DEV, 1, CW), F32), pltpu.SemaphoreType.DMA((2, 3, N_DEV))],
        compiler_params=pltpu.CompilerParams(vmem_limit_bytes=VMEM_LIMIT),
    )(c_row, wconv_row, w_ada, b_row)


def _ada_bwd(payload, deps=()):
    NCH, _, CW = payload.shape

    def body(p_ref, *rest):
        sum_ref, mine_ref, all_ref, sems = rest[-4:]
        me = _linear(_my_place())
        all_ref[me] = p_ref[...]

        def copy(k, slot):
            return pltpu.make_async_remote_copy(
                src_ref=all_ref.at[slot], dst_ref=all_ref.at[slot], send_sem=sems.at[0, k], recv_sem=sems.at[1, k],
                device_id=_peer(k), device_id_type=MESH_ID)

        for k in range(1, N_DEV):
            copy(k, me).start()
        for k in range(1, N_DEV):
            copy(k, _linear(_peer(k))).wait_recv()
        for k in range(1, N_DEV):
            copy(k, me).wait_send()

        total = all_ref[0]
        for b in range(1, N_DEV):
            total = total + all_ref[b]
        sum_ref[...] = total

        for b in range(N_DEV):
            mine_ref[b] = all_ref[b, me]

    vmem = pl.BlockSpec(memory_space=pltpu.VMEM)
    return pl.pallas_call(
        body, name="ada_bwd",
        in_specs=[vmem] + [ANY_SPEC] * len(deps), out_specs=[vmem, vmem],
        out_shape=[jax.ShapeDtypeStruct((NCH, 1, CW), F32), jax.ShapeDtypeStruct((N_DEV, 1, CW), F32)],
        scratch_shapes=[pltpu.VMEM((N_DEV, NCH, 1, CW), F32), pltpu.SemaphoreType.DMA((2, N_DEV))],
        compiler_params=pltpu.CompilerParams(vmem_limit_bytes=VMEM_LIMIT),
    )(payload, *deps)


def _exchange_in_chip(parts):
    W = len(parts)

    def body(*refs):
        p_refs, got_refs, (send_sems, recv_sems) = refs[:W], refs[W:2 * W], refs[2 * W:]
        x, y, c = _my_place()
        sibling = (x, y, 1 - c)
        copies = []
        for w in range(W):
            for q in range(4):
                copies.append(pltpu.make_async_remote_copy(
                    src_ref=p_refs[w].at[2 * q + (1 - c)], dst_ref=got_refs[w].at[q],
                    send_sem=send_sems.at[4 * w + q], recv_sem=recv_sems.at[4 * w + q],
                    device_id=sibling, device_id_type=MESH_ID))
        for cp in copies:
            cp.start()
        for cp in copies:
            cp.wait_recv()
        for cp in copies:
            cp.wait_send()

    return pl.pallas_call(
        body, name="grad_exchange_in_chip",
        in_specs=[HBM_SPEC] * W, out_specs=[HBM_SPEC] * W,
        out_shape=[jax.ShapeDtypeStruct((4,) + p.shape[1:], p.dtype) for p in parts],
        scratch_shapes=[pltpu.SemaphoreType.DMA((4 * W,)), pltpu.SemaphoreType.DMA((4 * W,))],
    )(*parts)


def _pair_sum(parts, got, core):
    _, R, C = parts.shape
    tr = _tile(R, max(PACK_ROW_ALIGN, PAIR_SUM_BLOCK // C), PACK_ROW_ALIGN)

    def body(c_ref, p_ref, g_ref, o_ref):
        o_ref[...] = (p_ref[...].astype(F32) + g_ref[...].astype(F32)).astype(o_ref.dtype)

    return pl.pallas_call(
        body, name="grad_pair_sum",
        grid_spec=pltpu.PrefetchScalarGridSpec(
            num_scalar_prefetch=1, grid=(4, R // tr),
            in_specs=[pl.BlockSpec((1, tr, C), lambda q, i, c_ref: (2 * q + c_ref[0], i, 0)),
                      pl.BlockSpec((1, tr, C), lambda q, i, c_ref: (q, i, 0))],
            out_specs=pl.BlockSpec((1, tr, C), lambda q, i, c_ref: (q, i, 0))),
        out_shape=jax.ShapeDtypeStruct((4, R, C), parts.dtype),
        compiler_params=_cparams(("parallel", "parallel")),
    )(core, parts, got)


HBM_SPEC = pl.BlockSpec(memory_space=pltpu.HBM)
SEM_SPEC = pl.BlockSpec(memory_space=pltpu.SEMAPHORE)
ANY_SPEC = pl.BlockSpec(memory_space=pl.ANY)
SPLIT_EFFECT = pltpu.SideEffectType.DATAFLOW_SIDE_EFFECTING


def _landing_zone(shape, dtype):
    return pltpu.with_memory_space_constraint(lax.empty(shape, dtype), pltpu.HBM)


def _split_start(name, arrays, lands, after, copies_of, per_array):
    W = len(arrays)
    after = tuple(after) if isinstance(after, (tuple, list)) else (after,)

    def body(*refs):
        x_refs, land_refs = refs[:W], refs[W:2 * W]
        send_sems, recv_sems = refs[2 * W + len(after)], refs[2 * W + len(after) + 1]
        token = refs[-1]
        k = 0
        for w in range(W):
            for src, dst, dev in copies_of(w, x_refs[w], land_refs[w]):
                pltpu.make_async_remote_copy(src_ref=src, dst_ref=dst, send_sem=send_sems.at[k], recv_sem=recv_sems.at[k],
                                             device_id=dev, device_id_type=MESH_ID).start()
                k += 1
        token[...] = jnp.zeros_like(token)

    n_copies = per_array * W
    hbm_of = lambda xs: tuple(pltpu.HBM(a.shape, a.dtype) for a in xs)
    out = pl.pallas_call(
        body, name=name,
        out_shape=(pltpu.SemaphoreType.DMA((n_copies,)), pltpu.SemaphoreType.DMA((n_copies,)))
        + hbm_of(arrays) + hbm_of(lands) + (jax.ShapeDtypeStruct((8, LANE), F32),),
        in_specs=(HBM_SPEC,) * (2 * W) + (ANY_SPEC,) * len(after),
        out_specs=(SEM_SPEC, SEM_SPEC) + (HBM_SPEC,) * (2 * W) + (pl.BlockSpec(memory_space=pltpu.VMEM),),
        input_output_aliases={i: 2 + i for i in range(2 * W)},
        compiler_params=pltpu.CompilerParams(has_side_effects=SPLIT_EFFECT),
    )(*[pltpu.with_memory_space_constraint(a, pltpu.HBM) for a in arrays], *lands, *after)
    return out[0], out[1], list(out[2:2 + W]), list(out[2 + W:2 + 2 * W]), out[-1]


def _split_wait(name, state, after, copies_of):
    send_sems, recv_sems, arrays, lands, _ = state
    W = len(arrays)
    after = tuple(after) if isinstance(after, (tuple, list)) else (after,)

    def body(*refs):
        x_refs, land_refs = refs[:W], refs[W:2 * W]
        send_sems, recv_sems = refs[2 * W], refs[2 * W + 1]
        k = 0
        for w in range(W):
            for src, dst, dev in copies_of(w, x_refs[w], land_refs[w]):
                cp = pltpu.make_async_remote_copy(src_ref=src, dst_ref=dst, send_sem=send_sems.at[k],
                                                  recv_sem=recv_sems.at[k], device_id=dev, device_id_type=MESH_ID)
                cp.wait_send()
                cp.wait_recv()
                k += 1

    out = pl.pallas_call(
        body, name=name,
        out_shape=tuple(pltpu.HBM(a.shape, a.dtype) for a in arrays + lands),
        in_specs=(HBM_SPEC,) * (2 * W) + (SEM_SPEC, SEM_SPEC) + (ANY_SPEC,) * len(after),
        out_specs=(HBM_SPEC,) * (2 * W),
        input_output_aliases={i: i for i in range(2 * W)},
        compiler_params=pltpu.CompilerParams(has_side_effects=SPLIT_EFFECT),
    )(*arrays, *lands, send_sems, recv_sems, *after)
    return list(out[:W]), list(out[W:])


def _scatter_copies(w, p_ref, land_ref):
    x, y, c = _my_place()
    my_chip = 2 * x + y
    return [(p_ref.at[2 * (x ^ (k >> 1)) + (y ^ (k & 1))], land_ref.at[my_chip], (x ^ (k >> 1), y ^ (k & 1), c))
            for k in range(1, 4)]


def _gather_copies(w, x_ref, land_ref):
    x, y, c = _my_place()
    me = _linear((x, y, c))
    devs = [(x, y, 1 - c)] + [(x ^ (k >> 1), y ^ (k & 1), c) for k in range(1, 4)]
    return [(x_ref, land_ref.at[me], d) for d in devs]


def _gather_forward(lands, name):
    W = len(lands)

    def body(*refs):
        land_refs, out_refs, (send_sems, recv_sems) = refs[:W], refs[W:2 * W], refs[2 * W:]
        x, y, c = _my_place()
        sibling = (x, y, 1 - c)
        sends, arrivals = [], []
        for w in range(W):
            for k in range(1, 4):
                px, py = x ^ (k >> 1), y ^ (k & 1)
                landed, theirs = _linear((px, py, c)), out_refs[w].at[_linear((px, py, 1 - c))]
                sem = 3 * w + k - 1
                sends.append(pltpu.make_async_remote_copy(
                    src_ref=land_refs[w].at[landed], dst_ref=out_refs[w].at[landed],
                    send_sem=send_sems.at[sem], recv_sem=recv_sems.at[sem], device_id=sibling, device_id_type=MESH_ID))
                arrivals.append(pltpu.make_async_remote_copy(
                    src_ref=theirs, dst_ref=theirs, send_sem=send_sems.at[sem], recv_sem=recv_sems.at[sem],
                    device_id=sibling, device_id_type=MESH_ID))
        for cp in sends:
            cp.start()
        for cp in arrivals:
            cp.wait_recv()
        for cp in sends:
            cp.wait_send()

    return pl.pallas_call(
        body, name=name,
        in_specs=[HBM_SPEC] * W, out_specs=[HBM_SPEC] * W,
        out_shape=[jax.ShapeDtypeStruct(l.shape, l.dtype) for l in lands],
        input_output_aliases={i: i for i in range(W)},
        scratch_shapes=[pltpu.SemaphoreType.DMA((3 * W,)), pltpu.SemaphoreType.DMA((3 * W,))],
    )(*lands)


def _with_own_slot(gathered, shard):
    return lax.dynamic_update_index_in_dim(gathered, shard[None], _linear(_my_place()), axis=0)


def _in_chip_copies(w, p_ref, land_ref):
    x, y, c = _my_place()
    return [(p_ref.at[2 * q + (1 - c)], land_ref.at[q], (x, y, 1 - c)) for q in range(4)]


def _in_chip_start(parts, tag):
    lands = [_landing_zone((4,) + p.shape[1:], p.dtype) for p in parts]
    return _split_start("grad_in_chip_start_" + tag, parts, lands, (), _in_chip_copies, 4)


def _reduce_scatter_begin(parts, tag, in_chip_state=None, after=()):
    if in_chip_state is None:
        got = _exchange_in_chip(parts)
    else:
        parts, got = _split_wait("grad_in_chip_wait_" + tag, in_chip_state, after, _in_chip_copies)
    core = lax.axis_index("c").astype(jnp.int32).reshape(1)
    chip_parts = [_pair_sum(p, g, core) for p, g in zip(parts, got)]
    lands = [_landing_zone(p.shape, p.dtype) for p in chip_parts]
    return _split_start("grad_scatter_start_" + tag, chip_parts, lands, got[0], _scatter_copies, 3)


def _reduce_scatter_end(state, after, tag):
    return _split_wait("grad_scatter_wait_" + tag, state, after, _scatter_copies)


def kernel(x, c, positions, w_ada, b_ada, w_in, g_q_a, w_q_b, g_kv_a, w_kv_b, w_o_a, w_conv, w_o_b, w_o, ln1_g, ln1_b, w_ffn_in, w_ffn_out, ln2_g, ln2_b, loss_target, m_w_ada, m_b_ada, m_w_in, m_g_q_a, m_w_q_b, m_g_kv_a, m_w_kv_b, m_w_o_a, m_w_conv, m_w_o_b, m_w_o, m_ln1_g, m_ln1_b, m_w_ffn_in, m_w_ffn_out, m_ln2_g, m_ln2_b, v_w_ada, v_b_ada, v_w_in, v_g_q_a, v_w_q_b, v_g_kv_a, v_w_kv_b, v_w_o_a, v_w_conv, v_w_o_b, v_w_o, v_ln1_g, v_ln1_b, v_w_ffn_in, v_w_ffn_out, v_ln2_g, v_ln2_b):
    x2, tgt = x[0], loss_target[0]
    S, D = x2.shape
    Lq, Lkv = g_q_a.shape[1], g_kv_a.shape[1]
    H = w_q_b.shape[2] * N_DEV // QK_CAT
    F = w_ffn_out.shape[1] * N_DEV
    assert Lq == Lkv and (Lq + Lkv) % COL_BLOCK == 0 and D % COL_BLOCK == 0
    front = Lq + Lkv + QK_ROPE
    front_pad = _round_up(front, COL_BLOCK)
    kr_blk = (Lq + Lkv) // COL_BLOCK
    blk_b = front_pad // COL_BLOCK
    nblk = D // COL_BLOCK
    blk_c, blk_x, blk_ga, blk_gb = blk_b + nblk, blk_b + 2 * nblk, blk_b + 3 * nblk, blk_b + 4 * nblk
    ts = _tile(S, 256, 8)
    T = _tile(S, min(512, S // 2), CHUNK)
    tb = _tile(F, 2816)
    me = _linear(_my_place())

    cw = w_ada.shape[2]
    b_mine = lax.dynamic_slice(b_ada, (0, me * cw), (1, cw)).reshape(1, 1, cw)
    mod_blocks, cact_all, wconv_all = _ada_fwd(c.reshape(1, 1, D), w_conv[0].reshape(1, 1, -1), w_ada[0], b_mine)
    mod = mod_blocks.reshape(6, D)
    cact_all = cact_all.reshape(N_DEV, D)
    w_conv_full = wconv_all.reshape(N_DEV, CONV_K, -1).transpose(1, 0, 2).reshape(CONV_K, D)

    landing = lambda shards: [_landing_zone((N_DEV,) + s.shape, BF16) for s in shards]
    gathered = lambda lands, shards, tag: [_with_own_slot(g, s) for g, s in
                                           zip(_gather_forward(lands, tag + "_gather_forward"), shards)]
    half = D // 2
    w_in_b = w_in[0].astype(BF16)
    first, second = [w_in_b[:half]], [w_in_b[half:], w_q_b[0].astype(BF16), w_kv_b[0].astype(BF16)]
    mid = [w[0].astype(BF16) for w in (w_o_a, w_o_b, w_o)]
    last = [w[0].astype(BF16) for w in (w_ffn_in, w_ffn_out)]
    first_state = _split_start("first_gather_start", first, landing(first), mod_blocks, _gather_copies, 4)
    second_state = _split_start("second_gather_start", second, landing(second), first_state[4], _gather_copies, 4)
    u = _modulate_in(x2, mod, ts)

    first_shards, first_lands = _split_wait("first_gather_wait", first_state, (u, second_state[4]), _gather_copies)
    (g_in_top,) = gathered(first_lands, first_shards, "first")
    w_in_top = _assemble_w_in(g_in_top, front, front_pad, D, 0)
    proj_top = _matmul(u, w_in_top, "nn", BF16, "proj_top", k_rows=(0, half))
    second_shards, second_lands = _split_wait("second_gather_wait", second_state, (proj_top,), _gather_copies)
    g_in_bottom, wq_s, wkv_s = gathered(second_lands, second_shards, "second")
    mid_state = _split_start("mid_gather_start", mid, landing(mid), g_in_bottom, _gather_copies, 4)
    last_state = _split_start("last_gather_start", last, landing(last), mid_state[4], _gather_copies, 4)
    w_in_p = _assemble_w_in(g_in_bottom, front, front_pad, D, half, into=w_in_top)

    inv_freq = 1.0 / (ROPE_THETA ** (jnp.arange(0, QK_ROPE, 2, dtype=F32) / QK_ROPE))
    ang = positions[0].astype(F32)[:, None] * inv_freq
    cos2 = jnp.concatenate([jnp.cos(ang), jnp.cos(ang)], axis=-1)
    sin2 = jnp.concatenate([jnp.sin(ang), jnp.sin(ang)], axis=-1)
    one, zero = jnp.ones((S, QK_NOPE), F32), jnp.zeros((S, QK_NOPE), F32)
    cos_q, sin_q = jnp.concatenate([one, cos2, one, cos2], axis=-1), jnp.concatenate([zero, sin2, zero, sin2], axis=-1)
    cos_k, sin_k = jnp.tile(cos2, (1, COL_BLOCK // QK_ROPE)), jnp.tile(sin2, (1, COL_BLOCK // QK_ROPE))

    proj = _matmul(u, w_in_p, "nn", BF16, "proj", k_rows=(half, half), init=proj_top, deps=(last_state[4],))
    qn = _rms_fwd(proj, g_q_a, 0, Lq, ts, "rms_q")
    kvn = _rms_fwd(proj, g_kv_a, 1, Lkv, ts, "rms_kv")
    q = _matmul(qn, wq_s, "nn", BF16, "q_up")
    kv = _matmul(kvn, wkv_s, "nn", BF16, "kv_up")
    qc, kc, vh = _qk_prep(q, kv, proj, kr_blk, cos_q, sin_q, cos_k, sin_k, H, ts)
    attn, lse = _attn_fwd(qc, kc, vh, T)
    mid_shards, mid_lands = _split_wait("mid_gather_wait", mid_state, lse, _gather_copies)
    w_oa_f, w_ob_f, w_o_f = [g.reshape(-1, D) for g in gathered(mid_lands, mid_shards, "mid")]
    ya = _matmul(attn, w_oa_f, "nn", BF16, "attn_out")
    cbc = _conv_fwd(proj, w_conv_full, blk_b, blk_c, blk_x)
    yb = _matmul(cbc, w_ob_f, "nn", BF16, "conv_out")
    merged = _merge_fwd(proj, ya, yb, blk_ga, blk_gb, ts)
    mix = _matmul(merged, w_o_f, "nn", F32, "mix_out")
    xhat1, rstd1, u2 = _ln1_fwd(x2, mix, mod, ln1_g, ln1_b, ts)
    last_shards, last_lands = _split_wait("last_gather_wait", last_state, u2, _gather_copies)
    w_fi_s, g_fo = gathered(last_lands, last_shards, "last")
    w_fo_f = g_fo.reshape(F, D)
    hh = _matmul(u2, w_fi_s, "nn", BF16, "ffn_in")
    act = _swiglu_fwd(hh, ts, tb)
    ffn = _matmul(act, w_fo_f, "nn", F32, "ffn_out")
    loss_part, dffn, dx1a, vec2 = _ln2_loss(xhat1, ffn, tgt, mod, ln1_g, ln1_b, ln2_g, ln2_b, ts)
    loss = lax.psum(loss_part[0, 0], AXES)

    gw_fo = _matmul(act, dffn, "tn", BF16, "grad_w_ffn_out")
    da = _matmul(dffn, w_fo_f, "nt", BF16, "d_act")
    dh = _swiglu_bwd(da, hh, ts, tb)
    gw_fi = _matmul(u2, dh, "tn", BF16, "grad_w_ffn_in", out_shards=True)
    ffn_in_chip = _in_chip_start([gw_fi, gw_fo.reshape(N_DEV, -1, D)], "ffn")
    du2 = _matmul(dh, w_fi_s, "nt", F32, "d_u2", deps=(ffn_in_chip[4],))
    ffn_state = _reduce_scatter_begin(None, "ffn", ffn_in_chip, after=(du2,))
    dxa, dmix, vec1 = _ln1_bwd(du2, dx1a, xhat1, rstd1, mix, mod, ln1_g, ln1_b, ts)
    gw_o = _matmul(merged, dmix, "tn", BF16, "grad_w_o", deps=(ffn_state[4],))
    dmerged = _matmul(dmix, w_o_f, "nt", BF16, "d_merged")
    dya, dyb, dga, dgb = _merge_bwd(dmerged, proj, ya, yb, blk_ga, blk_gb, ts)
    gw_ob = _matmul(cbc, dyb, "tn", BF16, "grad_w_o_b")
    dcbc = _matmul(dyb, w_ob_f, "nt", BF16, "d_conv")
    dcb, dcc, dcx, dwconv = _conv_bwd(dcbc, proj, w_conv_full, blk_b, blk_c, blk_x)
    gw_oa = _matmul(attn, dya, "tn", BF16, "grad_w_o_a")
    mix_in_chip = _in_chip_start([g.reshape(N_DEV, -1, D) for g in (gw_oa, gw_ob, gw_o)], "mix")
    dattn = _matmul(dya, w_oa_f, "nt", BF16, "d_attn", deps=(mix_in_chip[4],))
    dqc, dkc, dvh = _attn_bwd(qc, kc, vh, dattn, attn, lse, T)
    ffn_own, ffn_got = _reduce_scatter_end(ffn_state, dqc, "ffn")
    mix_state = _reduce_scatter_begin(None, "mix", mix_in_chip, after=(dqc,))
    dq, dkv, dkr = _qk_bwd(dqc, dkc, dvh, cos_q, sin_q, cos_k, sin_k, ts)
    gw_qb = _matmul(qn, dq, "tn", BF16, "grad_w_q_b", out_shards=True, deps=(mix_state[4],))
    dqn = _matmul(dq, wq_s, "nt", F32, "d_qn")
    gw_kvb = _matmul(kvn, dkv, "tn", BF16, "grad_w_kv_b", out_shards=True)
    dkvn = _matmul(dkv, wkv_s, "nt", F32, "d_kvn")
    dqa, dgq = _rms_bwd(dqn, proj, g_q_a, 0, Lq, ts, "rms_q_bwd")
    dkva, dgkv = _rms_bwd(dkvn, proj, g_kv_a, 1, Lkv, ts, "rms_kv_bwd")
    dproj = jnp.concatenate([dqa, dkva, dkr, dcb, dcc, dcx, dga, dgb], axis=1)
    gw_in_p = _matmul(u, dproj, "tn", BF16, "grad_w_in")
    mix_own, mix_got = _reduce_scatter_end(mix_state, gw_in_p, "mix")
    in_state = _reduce_scatter_begin([_split_w_in(gw_in_p, front, front_pad), gw_qb, gw_kvb], "in")
    du = _matmul(dproj, w_in_p, "nt", F32, "d_u", deps=(in_state[4],))
    grad_x, vec0 = _grad_x(du, dxa, x2, mod, ts)

    my_chip = (2 * lax.axis_index("x") + lax.axis_index("y")).astype(jnp.int32).reshape(1)
    arrived = {}
    for nm, w, m, v, own, got in (
            ("w_ffn_in", w_ffn_in, m_w_ffn_in, v_w_ffn_in, ffn_own[0], ffn_got[0]),
            ("w_ffn_out", w_ffn_out, m_w_ffn_out, v_w_ffn_out, ffn_own[1], ffn_got[1]),
            ("w_o_a", w_o_a, m_w_o_a, v_w_o_a, mix_own[0], mix_got[0]),
            ("w_o_b", w_o_b, m_w_o_b, v_w_o_b, mix_own[1], mix_got[1]),
            ("w_o", w_o, m_w_o, v_w_o, mix_own[2], mix_got[2])):
        arrived[nm] = [a[None] for a in _adamw_reduced(w[0], own, got, m[0], v[0], my_chip, "adamw_" + nm)]

    dmod = jnp.concatenate([vec0[0], vec0[1], vec1[4], vec1[0], vec1[1], vec2[2]])
    small = jnp.concatenate([dmod, dgq[0], dgkv[0], vec1[2], vec1[3], vec2[0], vec2[1], dwconv[:CONV_K].reshape(-1)])
    n_small = small.shape[0]
    nch = _round_up(n_small, cw) // cw
    payload = jnp.pad(small, (0, nch * cw - n_small)).reshape(nch, 1, cw)
    summed, dmod_mine = _ada_bwd(payload, deps=[res[1] for res in arrived.values()])
    arrived["w_ada"] = [a[None] for a in _adamw_ada(w_ada[0], cact_all.T, dmod_mine.reshape(N_DEV, cw),
                                                    m_w_ada[0], v_w_ada[0])]
    summed = summed.reshape(-1)
    offs = [0, 6 * D, 6 * D + Lq, 6 * D + Lq + Lkv]
    offs += [offs[-1] + D * k for k in range(1, 5)]
    g_b_ada = summed[offs[0]:offs[1]].reshape(1, -1)
    g_gq = summed[offs[1]:offs[2]].reshape(1, -1)
    g_gkv = summed[offs[2]:offs[3]].reshape(1, -1)
    g_ln1g, g_ln1b, g_ln2g, g_ln2b = [summed[offs[3 + k]:offs[4 + k]].reshape(1, -1) for k in range(4)]
    wc = w_conv.shape[2]
    g_wconv = lax.dynamic_slice(summed[offs[7]:offs[7] + CONV_K * D].reshape(CONV_K, D), (0, me * wc), (CONV_K, wc))

    names = ["w_ada", "b_ada", "w_in", "g_q_a", "w_q_b", "g_kv_a", "w_kv_b", "w_o_a", "w_conv", "w_o_b", "w_o",
             "ln1_g", "ln1_b", "w_ffn_in", "w_ffn_out", "ln2_g", "ln2_b"]
    weights = [w_ada, b_ada, w_in, g_q_a, w_q_b, g_kv_a, w_kv_b, w_o_a, w_conv, w_o_b, w_o, ln1_g, ln1_b,
               w_ffn_in, w_ffn_out, ln2_g, ln2_b]
    moms = [m_w_ada, m_b_ada, m_w_in, m_g_q_a, m_w_q_b, m_g_kv_a, m_w_kv_b, m_w_o_a, m_w_conv, m_w_o_b, m_w_o,
            m_ln1_g, m_ln1_b, m_w_ffn_in, m_w_ffn_out, m_ln2_g, m_ln2_b]
    vels = [v_w_ada, v_b_ada, v_w_in, v_g_q_a, v_w_q_b, v_g_kv_a, v_w_kv_b, v_w_o_a, v_w_conv, v_w_o_b, v_w_o,
            v_ln1_g, v_ln1_b, v_w_ffn_in, v_w_ffn_out, v_ln2_g, v_ln2_b]
    grad_of = {"b_ada": g_b_ada, "g_q_a": g_gq, "g_kv_a": g_gkv, "w_conv": g_wconv,
               "ln1_g": g_ln1g, "ln1_b": g_ln1b, "ln2_g": g_ln2g, "ln2_b": g_ln2b}
    state_of = dict(zip(names, zip(weights, moms, vels)))
    results = dict(arrived)

    def update(nm, reduced=None):
        w, m, v = state_of[nm]
        shp = w.shape
        w2 = w.reshape(shp[-2], shp[-1]) if w.ndim == 3 else w
        m2, v2 = m.reshape(w2.shape), v.reshape(w2.shape)
        if reduced is None:
            g2 = grad_of[nm].reshape(w2.shape)
            res = (g2,) + tuple(_adamw(w2, g2, m2, v2, "adamw_" + nm))
        else:
            res = _adamw_reduced(w2, reduced[0], reduced[1], m2, v2, my_chip, "adamw_" + nm)
        results[nm] = [a.reshape(shp) for a in res]

    for nm in grad_of:
        update(nm)
    in_own, in_got = _reduce_scatter_end(in_state, [res[1] for res in results.values()], "in")
    for nm, own, got in zip(("w_in", "w_q_b", "w_kv_b"), in_own, in_got):
        update(nm, (own, got))
    outs = [[results[nm][k] for nm in names] for k in range(4)]
    return (loss, grad_x.reshape(x.shape), *outs[0], *outs[1], *outs[2], *outs[3])
```

```python
import functools

import jax
import jax.numpy as jnp
from jax import lax
from jax.experimental import pallas as pl
from jax.experimental.pallas import tpu as pltpu

F32 = jnp.float32
BF16 = jnp.bfloat16
MESH_ID = pl.DeviceIdType.MESH
AXES = ("x", "y", "c")
N_DEV = 8

CHUNK = 64
QK_NOPE = 128
QK_ROPE = 64
V_HEAD = 128
QK_CAT = QK_NOPE + QK_ROPE
ROPE_THETA = 10000.0
ATTN_SCALE = (QK_NOPE + QK_ROPE) ** -0.5
CONV_K = 3
DEEPNORM_ALPHA = 2.0 ** 0.25
LN_EPS = 1e-5
RMS_EPS = 1e-6
NEG_INF = -1e30

ADAM_LR = 0.001
ADAM_B1 = 0.9
ADAM_B2 = 0.999
ADAM_EPS = 1e-08
ADAM_WD = 0.01
ADAM_STEP = 10

LANE = 128
COL_BLOCK = 256
PACK_ROW_ALIGN = 16
PAIR_SUM_BLOCK = 1 << 20
VMEM_LIMIT = 48 * 1024 * 1024


def _round_up(n, m):
    return (n + m - 1) // m * m


def _tile(n, pref, align=LANE):
    best = None
    t = align
    while t <= min(n, pref):
        if n % t == 0:
            best = t
        t += align
    return best if best is not None else n


def _cparams(sem=None):
    return pltpu.CompilerParams(dimension_semantics=sem, vmem_limit_bytes=VMEM_LIMIT)


def _sigmoid(x):
    return 0.5 * jnp.tanh(0.5 * x) + 0.5


def _matmul(a, b, mode, out_dtype, name, tm=1024, tn=1024, tk=2048, deps=(), out_shards=False, k_rows=None,
            init=None):
    b_shards = b.ndim == 3
    n = b.shape[2] if b_shards else (b.shape[1] // N_DEV if out_shards else None)
    if mode == "nn":
        (M, K), (K2, N) = a.shape, (b.shape[1], N_DEV * n) if b_shards else b.shape
    elif mode == "nt":
        (M, K), (N, K2) = a.shape, (b.shape[1], N_DEV * n) if b_shards else b.shape
    else:
        (K, M), (K2, N) = a.shape, b.shape
    assert K == K2, (a.shape, b.shape, mode)
    tm = _tile(M, tm)
    tn = n if (mode != "nt" and n is not None) else _tile(N, tn)
    k_row0, k_len = k_rows if k_rows is not None else (0, K)
    tk = n if (mode == "nt" and b_shards) else _tile(k_len, tk)
    nk, k0 = k_len // tk, k_row0 // tk
    if mode == "nn":
        a_spec = pl.BlockSpec((tm, tk), lambda i, j, k: (i, k0 + k))
        b_spec = (pl.BlockSpec((1, tk, n), lambda i, j, k: (j, k, 0)) if b_shards
                  else pl.BlockSpec((tk, tn), lambda i, j, k: (k0 + k, j)))
        dims = (((1,), (0,)), ((), ()))
    elif mode == "nt":
        a_spec = pl.BlockSpec((tm, tk), lambda i, j, k: (i, k))
        b_spec = (pl.BlockSpec((1, tn, n), lambda i, j, k: (k, j, 0)) if b_shards
                  else pl.BlockSpec((tn, tk), lambda i, j, k: (j, k)))
        dims = (((1,), (1,)), ((), ()))
    else:
        a_spec = pl.BlockSpec((tk, tm), lambda i, j, k: (k, i))
        b_spec = pl.BlockSpec((tk, tn), lambda i, j, k: (k, j))
        dims = (((0,), (0,)), ((), ()))
    if out_shards:
        out_spec = pl.BlockSpec((1, tm, n), lambda i, j, k: (j, i, 0))
        out_shape = jax.ShapeDtypeStruct((N_DEV, M, n), out_dtype)
    else:
        out_spec = pl.BlockSpec((tm, tn), lambda i, j, k: (i, j))
        out_shape = jax.ShapeDtypeStruct((M, N), out_dtype)

    def product(a_ref, b_ref):
        b_blk = b_ref[0] if b_shards else b_ref[...]
        return lax.dot_general(a_ref[...].astype(BF16), b_blk.astype(BF16), dims, preferred_element_type=F32)

    def write(o_ref, value):
        if out_shards:
            o_ref[0] = value.astype(o_ref.dtype)
        else:
            o_ref[...] = value.astype(o_ref.dtype)

    def body_whole_k(a_ref, b_ref, *rest):
        value = product(a_ref, b_ref)
        write(rest[-1], value if init is None else value + rest[0][...])

    def body_split_k(a_ref, b_ref, *rest):
        o_ref, acc_ref = rest[-2:]
        k = pl.program_id(2)

        @pl.when(k == 0)
        def _():
            acc_ref[...] = jnp.zeros_like(acc_ref) if init is None else rest[0][...].astype(F32)

        acc_ref[...] += product(a_ref, b_ref)

        @pl.when(k == nk - 1)
        def _():
            write(o_ref, acc_ref[...])

    return pl.pallas_call(
        body_whole_k if nk == 1 else body_split_k, name=name, grid=(M // tm, N // tn, nk),
        in_specs=[a_spec, b_spec] + ([] if init is None else [out_spec]) + [ANY_SPEC] * len(deps),
        out_specs=out_spec, out_shape=out_shape,
        scratch_shapes=[] if nk == 1 else [pltpu.VMEM((tm, tn), F32)],
        compiler_params=_cparams(("parallel", "parallel", "arbitrary")),
    )(a, b, *(() if init is None else (init,)), *deps)


def _assemble_w_in(shards, front, front_pad, rows, row0, into=None):
    _, K, n = shards.shape
    gap = front_pad - front
    tk = _tile(K, 256, PACK_ROW_ALIGN)
    blk0 = row0 // tk

    def body(g_ref, *rest):
        o_ref = rest[-1]
        if gap:
            o_ref[:, front:front_pad] = jnp.zeros((tk, gap), o_ref.dtype)
        for j in range(N_DEV):
            lo, hi = j * n, (j + 1) * n
            if lo < front < hi:
                o_ref[:, lo:front] = g_ref[j, :, 0:front - lo]
                o_ref[:, front_pad:hi + gap] = g_ref[j, :, front - lo:n]
            else:
                off = 0 if hi <= front else gap
                o_ref[:, lo + off:hi + off] = g_ref[j]

    return pl.pallas_call(
        body, name="assemble_w_in", grid=(K // tk,),
        in_specs=[pl.BlockSpec((N_DEV, tk, n), lambda i: (0, i, 0))] + ([] if into is None else [ANY_SPEC]),
        out_specs=pl.BlockSpec((tk, N_DEV * n + gap), lambda i: (blk0 + i, 0)),
        out_shape=jax.ShapeDtypeStruct((rows, N_DEV * n + gap), shards.dtype),
        input_output_aliases={} if into is None else {1: 0},
        compiler_params=_cparams(("parallel",)),
    )(*([shards] if into is None else [shards, into]))


def _split_w_in(w, front, front_pad):
    K, NP = w.shape
    gap = front_pad - front
    n = (NP - gap) // N_DEV
    tk = _tile(K, 256, PACK_ROW_ALIGN)

    def body(w_ref, o_ref):
        for j in range(N_DEV):
            lo, hi = j * n, (j + 1) * n
            if lo < front < hi:
                o_ref[j, :, 0:front - lo] = w_ref[:, lo:front]
                o_ref[j, :, front - lo:n] = w_ref[:, front_pad:hi + gap]
            else:
                off = 0 if hi <= front else gap
                o_ref[j] = w_ref[:, lo + off:hi + off]

    return pl.pallas_call(
        body, name="split_grad_w_in", grid=(K // tk,),
        in_specs=[pl.BlockSpec((tk, NP), lambda i: (i, 0))],
        out_specs=pl.BlockSpec((N_DEV, tk, n), lambda i: (0, i, 0)),
        out_shape=jax.ShapeDtypeStruct((N_DEV, K, n), w.dtype),
        compiler_params=_cparams(("parallel",)),
    )(w)


def _modulate_in(x, mod, ts):
    S, D = x.shape

    def body(x_ref, mod_ref, u_ref):
        u_ref[...] = (x_ref[...] * (1.0 + mod_ref[1:2, :]) + mod_ref[0:1, :]).astype(BF16)

    return pl.pallas_call(
        body, name="modulate_in", grid=(S // ts,),
        in_specs=[pl.BlockSpec((ts, D), lambda i: (i, 0)), pl.BlockSpec((6, D), lambda i: (0, 0))],
        out_specs=pl.BlockSpec((ts, D), lambda i: (i, 0)),
        out_shape=jax.ShapeDtypeStruct((S, D), BF16),
        compiler_params=_cparams(("parallel",)),
    )(x, mod)


def _rms_fwd(proj, g, blk, L, ts, name):
    S = proj.shape[0]

    def body(a_ref, g_ref, y_ref):
        a = a_ref[...].astype(F32)
        r = lax.rsqrt(jnp.mean(a * a, axis=-1, keepdims=True) + RMS_EPS)
        y_ref[...] = (a * r * g_ref[...]).astype(BF16)

    return pl.pallas_call(
        body, name=name, grid=(S // ts,),
        in_specs=[pl.BlockSpec((ts, L), lambda i: (i, blk)), pl.BlockSpec((1, L), lambda i: (0, 0))],
        out_specs=pl.BlockSpec((ts, L), lambda i: (i, 0)),
        out_shape=jax.ShapeDtypeStruct((S, L), BF16),
        compiler_params=_cparams(("parallel",)),
    )(proj, g)


def _rope_partner(x, period, start):
    w = x.shape[-1]
    lane = lax.broadcasted_iota(jnp.int32, x.shape, x.ndim - 1) % period
    first = (lane >= start) & (lane < start + QK_ROPE // 2)
    from_right = pltpu.roll(x, w - QK_ROPE // 2, axis=x.ndim - 1)
    from_left = pltpu.roll(x, QK_ROPE // 2, axis=x.ndim - 1)
    return jnp.where(first, -from_right, from_left)


def _qk_prep(q, kv, proj, kr_blk, cos_q, sin_q, cos_k, sin_k, H, ts):
    S = q.shape[0]
    pair = 2 * QK_CAT
    kv_w = QK_NOPE + V_HEAD

    def body(q_ref, kv_ref, kr_ref, cq_ref, sq_ref, ck_ref, sk_ref, qc_ref, kc_ref, vh_ref):
        kr = kr_ref[...].astype(F32)
        kr = kr * ck_ref[...] + _rope_partner(kr, QK_ROPE, 0) * sk_ref[...]
        kr = kr[:, :QK_ROPE].astype(BF16)
        for p in range(H // 2):
            x = q_ref[:, p * pair:(p + 1) * pair].astype(F32)
            x = x * cq_ref[...] + _rope_partner(x, QK_CAT, QK_NOPE) * sq_ref[...]
            qc_ref[2 * p] = x[:, :QK_CAT].astype(BF16)
            qc_ref[2 * p + 1] = x[:, QK_CAT:].astype(BF16)
        for h in range(H):
            kc_ref[h, :, 0:QK_NOPE] = kv_ref[:, h * kv_w:h * kv_w + QK_NOPE].astype(BF16)
            kc_ref[h, :, QK_NOPE:QK_CAT] = kr
            vh_ref[h, :, :] = kv_ref[:, h * kv_w + QK_NOPE:(h + 1) * kv_w].astype(BF16)

    row = lambda w: pl.BlockSpec((ts, w), lambda i: (i, 0))
    return pl.pallas_call(
        body, name="qk_prep", grid=(S // ts,),
        in_specs=[row(H * QK_CAT), row(H * kv_w),
                  pl.BlockSpec((ts, COL_BLOCK), lambda i: (i, kr_blk)),
                  row(pair), row(pair), row(COL_BLOCK), row(COL_BLOCK)],
        out_specs=[pl.BlockSpec((H, ts, QK_CAT), lambda i: (0, i, 0)),
                   pl.BlockSpec((H, ts, QK_CAT), lambda i: (0, i, 0)),
                   pl.BlockSpec((H, ts, V_HEAD), lambda i: (0, i, 0))],
        out_shape=[jax.ShapeDtypeStruct((H, S, QK_CAT), BF16), jax.ShapeDtypeStruct((H, S, QK_CAT), BF16),
                   jax.ShapeDtypeStruct((H, S, V_HEAD), BF16)],
        compiler_params=_cparams(("parallel",)),
    )(q, kv, proj, cos_q, sin_q, cos_k, sin_k)


NT_DIMS = (((1,), (1,)), ((), ()))
TN_DIMS = (((0,), (0,)), ((), ()))


def _diag_mask(T):
    rows = lax.broadcasted_iota(jnp.int32, (T, T), 0) // CHUNK
    cols = lax.broadcasted_iota(jnp.int32, (T, T), 1) // CHUNK
    return cols <= rows


def _attn_fwd(qc, kc, vh, T):
    H, S, _ = qc.shape
    n = S // T

    def body(q_ref, k_ref, v_ref, o_ref, lse_ref):
        q = q_ref[0]

        def block(i):
            L = (i + 1) * T
            s_old = lax.dot_general(q, k_ref[0, 0:i * T, :], NT_DIMS, preferred_element_type=F32) if i else None
            s_diag = lax.dot_general(q, k_ref[0, i * T:L, :], NT_DIMS, preferred_element_type=F32)
            s_diag = jnp.where(_diag_mask(T), s_diag, NEG_INF)
            m = jnp.max(s_diag, axis=-1, keepdims=True)
            if i:
                m = jnp.maximum(m, jnp.max(s_old, axis=-1, keepdims=True))
            p_diag = jnp.exp((s_diag - m) * ATTN_SCALE)
            l = jnp.sum(p_diag, axis=-1, keepdims=True)
            acc = jnp.dot(p_diag.astype(BF16), v_ref[0, i * T:L, :], preferred_element_type=F32)
            if i:
                p_old = jnp.exp((s_old - m) * ATTN_SCALE)
                l = l + jnp.sum(p_old, axis=-1, keepdims=True)
                acc = acc + jnp.dot(p_old.astype(BF16), v_ref[0, 0:i * T, :], preferred_element_type=F32)
            o_ref[...] = (acc / l).astype(o_ref.dtype)
            lse_ref[0] = m * ATTN_SCALE + jnp.log(l)

        for i in range(n):
            pl.when(pl.program_id(1) == i)(functools.partial(block, i))

    return pl.pallas_call(
        body, name="attn_fwd", grid=(H, n),
        in_specs=[pl.BlockSpec((1, T, QK_CAT), lambda h, i: (h, i, 0)),
                  pl.BlockSpec((1, S, QK_CAT), lambda h, i: (h, 0, 0)),
                  pl.BlockSpec((1, S, V_HEAD), lambda h, i: (h, 0, 0))],
        out_specs=[pl.BlockSpec((T, V_HEAD), lambda h, i: (i, h)),
                   pl.BlockSpec((1, T, 1), lambda h, i: (h, i, 0))],
        out_shape=[jax.ShapeDtypeStruct((S, H * V_HEAD), BF16), jax.ShapeDtypeStruct((H, S, 1), F32)],
        compiler_params=_cparams(("parallel", "arbitrary")),
    )(qc, kc, vh)


def _shift_rows(z, k):
    if k == 0:
        return z
    n = z.shape[0]
    row = lax.broadcasted_iota(jnp.int32, z.shape, 0)
    if k > 0:
        return jnp.where(row >= k, pltpu.roll(z, k, axis=0), 0.0)
    return jnp.where(row < n + k, pltpu.roll(z, n + k, axis=0), 0.0)


def _conv_fwd(proj, w_conv, blk_b, blk_c, blk_x):
    S = proj.shape[0]
    D = w_conv.shape[1]
    nb = D // COL_BLOCK

    def body(cb_ref, cc_ref, cx_ref, w_ref, o_ref):
        z = cc_ref[...].astype(F32) * cx_ref[...].astype(F32)
        conv = w_ref[2:3, :] * z + w_ref[1:2, :] * _shift_rows(z, 1) + w_ref[0:1, :] * _shift_rows(z, 2)
        o_ref[...] = (cb_ref[...].astype(F32) * conv).astype(BF16)

    col = lambda off: pl.BlockSpec((S, COL_BLOCK), lambda j: (0, off + j))
    return pl.pallas_call(
        body, name="conv_fwd", grid=(nb,),
        in_specs=[col(blk_b), col(blk_c), col(blk_x), pl.BlockSpec((CONV_K, COL_BLOCK), lambda j: (0, j))],
        out_specs=pl.BlockSpec((S, COL_BLOCK), lambda j: (0, j)),
        out_shape=jax.ShapeDtypeStruct((S, D), BF16),
        compiler_params=_cparams(("parallel",)),
    )(proj, proj, proj, w_conv)


def _merge_fwd(proj, ya, yb, blk_ga, blk_gb, ts):
    S, D = ya.shape
    nb = D // COL_BLOCK

    def body(ga_ref, gb_ref, ya_ref, yb_ref, o_ref):
        sa, sb = _sigmoid(ga_ref[...].astype(F32)), _sigmoid(gb_ref[...].astype(F32))
        o_ref[...] = (sa * ya_ref[...].astype(F32) + sb * yb_ref[...].astype(F32)).astype(BF16)

    row = pl.BlockSpec((ts, D), lambda i: (i, 0))
    seg = lambda blk: pl.BlockSpec((pl.Element(ts), pl.Element(D)), lambda i: (i * ts, blk * COL_BLOCK))
    return pl.pallas_call(
        body, name="merge_fwd", grid=(S // ts,),
        in_specs=[seg(blk_ga), seg(blk_gb), row, row],
        out_specs=row,
        out_shape=jax.ShapeDtypeStruct((S, D), BF16),
        compiler_params=_cparams(("parallel",)),
    )(proj, proj, ya, yb)


def _ln1_fwd(x, mix, mod, g, b, ts):
    S, D = x.shape

    def body(x_ref, mix_ref, mod_ref, g_ref, b_ref, xhat_ref, rstd_ref, u2_ref):
        r = DEEPNORM_ALPHA * x_ref[...] + mod_ref[2:3, :] * mix_ref[...]
        mu = jnp.mean(r, axis=-1, keepdims=True)
        d = r - mu
        rstd = lax.rsqrt(jnp.mean(d * d, axis=-1, keepdims=True) + LN_EPS)
        xhat = d * rstd
        xhat_ref[...] = xhat
        rstd_ref[...] = rstd
        x1 = xhat * g_ref[...] + b_ref[...]
        u2_ref[...] = (x1 * (1.0 + mod_ref[4:5, :]) + mod_ref[3:4, :]).astype(BF16)

    row = pl.BlockSpec((ts, D), lambda i: (i, 0))
    vec = lambda r: pl.BlockSpec((r, D), lambda i: (0, 0))
    return pl.pallas_call(
        body, name="ln1_fwd", grid=(S // ts,),
        in_specs=[row, row, vec(6), vec(1), vec(1)],
        out_specs=[row, pl.BlockSpec((ts, 1), lambda i: (i, 0)), row],
        out_shape=[jax.ShapeDtypeStruct((S, D), F32), jax.ShapeDtypeStruct((S, 1), F32),
                   jax.ShapeDtypeStruct((S, D), BF16)],
        compiler_params=_cparams(("parallel",)),
    )(x, mix, mod, g, b)


def _swiglu_fwd(h, ts, tb):
    S, F2 = h.shape
    F = F2 // 2
    nb = F // tb

    def body(hg_ref, hu_ref, a_ref):
        hg = hg_ref[...].astype(F32)
        a_ref[...] = (hg * _sigmoid(hg) * hu_ref[...].astype(F32)).astype(BF16)

    return pl.pallas_call(
        body, name="swiglu_fwd", grid=(S // ts, nb),
        in_specs=[pl.BlockSpec((ts, tb), lambda i, j: (i, j)), pl.BlockSpec((ts, tb), lambda i, j: (i, j + nb))],
        out_specs=pl.BlockSpec((ts, tb), lambda i, j: (i, j)),
        out_shape=jax.ShapeDtypeStruct((S, F), BF16),
        compiler_params=_cparams(("parallel", "parallel")),
    )(h, h)


def _ln2_loss(xhat1, ffn, tgt, mod, g1, b1, g2, b2, ts):
    S, D = xhat1.shape

    def body(xh_ref, ffn_ref, t_ref, mod_ref, g1_ref, b1_ref, g2_ref, b2_ref, loss_ref, dffn_ref, dx1_ref, vec_ref):
        i = pl.program_id(0)

        @pl.when(i == 0)
        def _():
            loss_ref[...] = jnp.zeros_like(loss_ref)
            vec_ref[...] = jnp.zeros_like(vec_ref)

        x1 = xh_ref[...] * g1_ref[...] + b1_ref[...]
        ffn = ffn_ref[...]
        r = DEEPNORM_ALPHA * x1 + mod_ref[5:6, :] * ffn
        mu = jnp.mean(r, axis=-1, keepdims=True)
        d = r - mu
        rstd = lax.rsqrt(jnp.mean(d * d, axis=-1, keepdims=True) + LN_EPS)
        xhat = d * rstd
        e = xhat * g2_ref[...] + b2_ref[...] - t_ref[...]
        loss_ref[...] += 0.5 * jnp.sum(jnp.mean(e * e, axis=-1, keepdims=True))
        dy = e * (1.0 / D)
        dxhat = dy * g2_ref[...]
        dr = rstd * (dxhat - jnp.mean(dxhat, axis=-1, keepdims=True)
                     - xhat * jnp.mean(dxhat * xhat, axis=-1, keepdims=True))
        dffn_ref[...] = (dr * mod_ref[5:6, :]).astype(BF16)
        dx1_ref[...] = DEEPNORM_ALPHA * dr
        vec_ref[0:1, :] += jnp.sum(dy * xhat, axis=0, keepdims=True)
        vec_ref[1:2, :] += jnp.sum(dy, axis=0, keepdims=True)
        vec_ref[2:3, :] += jnp.sum(dr * ffn, axis=0, keepdims=True)

    row = pl.BlockSpec((ts, D), lambda i: (i, 0))
    vec = lambda r: pl.BlockSpec((r, D), lambda i: (0, 0))
    return pl.pallas_call(
        body, name="ln2_loss", grid=(S // ts,),
        in_specs=[row, row, row, vec(6), vec(1), vec(1), vec(1), vec(1)],
        out_specs=[pl.BlockSpec((1, LANE), lambda i: (0, 0)), row, row, vec(8)],
        out_shape=[jax.ShapeDtypeStruct((1, LANE), F32), jax.ShapeDtypeStruct((S, D), BF16),
                   jax.ShapeDtypeStruct((S, D), F32), jax.ShapeDtypeStruct((8, D), F32)],
        compiler_params=_cparams(("arbitrary",)),
    )(xhat1, ffn, tgt, mod, g1, b1, g2, b2)


def _swiglu_bwd(da, h, ts, tb):
    S, F2 = h.shape
    nb = (F2 // 2) // tb

    def body(da_ref, hg_ref, hu_ref, dh_ref):
        hg, da = hg_ref[...].astype(F32), da_ref[...].astype(F32)
        sg = _sigmoid(hg)

        @pl.when(pl.program_id(2) == 0)
        def _():
            dh_ref[...] = (da * hu_ref[...].astype(F32) * (sg * (1.0 + hg * (1.0 - sg)))).astype(BF16)

        @pl.when(pl.program_id(2) == 1)
        def _():
            dh_ref[...] = (da * hg * sg).astype(BF16)

    lo = pl.BlockSpec((ts, tb), lambda i, j, k: (i, j))
    hi = pl.BlockSpec((ts, tb), lambda i, j, k: (i, j + nb))
    return pl.pallas_call(
        body, name="swiglu_bwd", grid=(S // ts, nb, 2),
        in_specs=[lo, lo, hi],
        out_specs=pl.BlockSpec((ts, tb), lambda i, j, k: (i, j + nb * k)),
        out_shape=jax.ShapeDtypeStruct((S, F2), BF16),
        compiler_params=_cparams(("parallel", "parallel", "arbitrary")),
    )(da, h, h)


def _ln1_bwd(du2, dx1a, xhat1, rstd1, mix, mod, g1, b1, ts):
    S, D = xhat1.shape

    def body(du2_ref, dx1a_ref, xh_ref, rstd_ref, mix_ref, mod_ref, g_ref, b_ref, dxa_ref, dmix_ref, vec_ref):
        i = pl.program_id(0)

        @pl.when(i == 0)
        def _():
            vec_ref[...] = jnp.zeros_like(vec_ref)

        xhat, du2, mix = xh_ref[...], du2_ref[...], mix_ref[...]
        x1 = xhat * g_ref[...] + b_ref[...]
        dx1 = dx1a_ref[...] + du2 * (1.0 + mod_ref[4:5, :])
        dxhat = dx1 * g_ref[...]
        dr = rstd_ref[...] * (dxhat - jnp.mean(dxhat, axis=-1, keepdims=True)
                              - xhat * jnp.mean(dxhat * xhat, axis=-1, keepdims=True))
        dxa_ref[...] = DEEPNORM_ALPHA * dr
        dmix_ref[...] = (dr * mod_ref[2:3, :]).astype(BF16)
        vec_ref[0:1, :] += jnp.sum(du2, axis=0, keepdims=True)
        vec_ref[1:2, :] += jnp.sum(du2 * x1, axis=0, keepdims=True)
        vec_ref[2:3, :] += jnp.sum(dx1 * xhat, axis=0, keepdims=True)
        vec_ref[3:4, :] += jnp.sum(dx1, axis=0, keepdims=True)
        vec_ref[4:5, :] += jnp.sum(dr * mix, axis=0, keepdims=True)

    row = pl.BlockSpec((ts, D), lambda i: (i, 0))
    vec = lambda r: pl.BlockSpec((r, D), lambda i: (0, 0))
    return pl.pallas_call(
        body, name="ln1_bwd", grid=(S // ts,),
        in_specs=[row, row, row, pl.BlockSpec((ts, 1), lambda i: (i, 0)), row, vec(6), vec(1), vec(1)],
        out_specs=[row, row, vec(8)],
        out_shape=[jax.ShapeDtypeStruct((S, D), F32), jax.ShapeDtypeStruct((S, D), BF16),
                   jax.ShapeDtypeStruct((8, D), F32)],
        compiler_params=_cparams(("arbitrary",)),
    )(du2, dx1a, xhat1, rstd1, mix, mod, g1, b1)


def _merge_bwd(dmerged, proj, ya, yb, blk_ga, blk_gb, ts):
    S, D = ya.shape
    nb = D // COL_BLOCK

    def body(dm_ref, ga_ref, gb_ref, ya_ref, yb_ref, dya_ref, dyb_ref, dga_ref, dgb_ref):
        dm = dm_ref[...].astype(F32)
        sa, sb = _sigmoid(ga_ref[...].astype(F32)), _sigmoid(gb_ref[...].astype(F32))
        dya_ref[...] = (dm * sa).astype(BF16)
        dyb_ref[...] = (dm * sb).astype(BF16)
        dga_ref[...] = (dm * ya_ref[...].astype(F32) * sa * (1.0 - sa)).astype(BF16)
        dgb_ref[...] = (dm * yb_ref[...].astype(F32) * sb * (1.0 - sb)).astype(BF16)

    row = pl.BlockSpec((ts, D), lambda i: (i, 0))
    seg = lambda blk: pl.BlockSpec((pl.Element(ts), pl.Element(D)), lambda i: (i * ts, blk * COL_BLOCK))
    out = jax.ShapeDtypeStruct((S, D), BF16)
    return pl.pallas_call(
        body, name="merge_bwd", grid=(S // ts,),
        in_specs=[row, seg(blk_ga), seg(blk_gb), row, row],
        out_specs=[row] * 4,
        out_shape=[out] * 4,
        compiler_params=_cparams(("parallel",)),
    )(dmerged, proj, proj, ya, yb)


def _conv_bwd(dcbc, proj, w_conv, blk_b, blk_c, blk_x):
    S = proj.shape[0]
    D = w_conv.shape[1]
    nb = D // COL_BLOCK

    def body(d_ref, cb_ref, cc_ref, cx_ref, w_ref, dcb_ref, dcc_ref, dcx_ref, dw_ref):
        d, cc, cx = d_ref[...].astype(F32), cc_ref[...].astype(F32), cx_ref[...].astype(F32)
        z = cc * cx
        z1, z2 = _shift_rows(z, 1), _shift_rows(z, 2)
        conv = w_ref[2:3, :] * z + w_ref[1:2, :] * z1 + w_ref[0:1, :] * z2
        dcb_ref[...] = (d * conv).astype(BF16)
        dconv = d * cb_ref[...].astype(F32)
        dz = w_ref[2:3, :] * dconv + w_ref[1:2, :] * _shift_rows(dconv, -1) + w_ref[0:1, :] * _shift_rows(dconv, -2)
        dcc_ref[...] = (dz * cx).astype(BF16)
        dcx_ref[...] = (dz * cc).astype(BF16)
        dw_ref[...] = jnp.zeros_like(dw_ref)
        dw_ref[0:1, :] = jnp.sum(dconv * z2, axis=0, keepdims=True)
        dw_ref[1:2, :] = jnp.sum(dconv * z1, axis=0, keepdims=True)
        dw_ref[2:3, :] = jnp.sum(dconv * z, axis=0, keepdims=True)

    col = lambda off: pl.BlockSpec((S, COL_BLOCK), lambda j: (0, off + j))
    out = jax.ShapeDtypeStruct((S, D), BF16)
    return pl.pallas_call(
        body, name="conv_bwd", grid=(nb,),
        in_specs=[col(0), col(blk_b), col(blk_c), col(blk_x), pl.BlockSpec((CONV_K, COL_BLOCK), lambda j: (0, j))],
        out_specs=[col(0), col(0), col(0), pl.BlockSpec((8, COL_BLOCK), lambda j: (0, j))],
        out_shape=[out, out, out, jax.ShapeDtypeStruct((8, D), F32)],
        compiler_params=_cparams(("parallel",)),
    )(dcbc, proj, proj, proj, w_conv)


def _attn_bwd(qc, kc, vh, do, o, lse, T):
    H, S, _ = qc.shape
    n = S // T

    def body(q_ref, k_ref, v_ref, do_ref, o_ref, lse_ref, dq_ref, dk_ref, dv_ref, d_ref, dq_acc, dk_acc, dv_acc):
        j = pl.program_id(1)

        @pl.when(j == 0)
        def _():
            dq_acc[...] = jnp.zeros_like(dq_acc)
            d_ref[...] = jnp.sum(do_ref[...].astype(F32) * o_ref[...].astype(F32), axis=-1, keepdims=True)

        dk_acc[...] = jnp.zeros_like(dk_acc)
        dv_acc[...] = jnp.zeros_like(dv_acc)
        k, v = k_ref[0], v_ref[0]

        def step(i, masked):
            rows = pl.ds(pl.multiple_of(i * T, T), T)
            q = q_ref[0, rows, :]
            do = do_ref[rows, :].astype(BF16)
            s = lax.dot_general(q, k, NT_DIMS, preferred_element_type=F32) * ATTN_SCALE
            if masked:
                s = jnp.where(_diag_mask(T), s, NEG_INF)
            p = jnp.exp(s - lse_ref[0, rows, :])
            dv_acc[...] += lax.dot_general(p.astype(BF16), do, TN_DIMS, preferred_element_type=F32)
            dp = lax.dot_general(do, v, NT_DIMS, preferred_element_type=F32)
            ds = (p * (dp - d_ref[rows, :]) * ATTN_SCALE).astype(BF16)
            dk_acc[...] += lax.dot_general(ds, q, TN_DIMS, preferred_element_type=F32)
            dq_acc[rows, :] += jnp.dot(ds, k, preferred_element_type=F32)

        def above(i, carry):
            step(i, False)
            return carry

        step(j, True)
        lax.fori_loop(j + 1, n, above, 0)
        dk_ref[0] = dk_acc[...].astype(BF16)
        dv_ref[0] = dv_acc[...].astype(BF16)

        @pl.when(j == n - 1)
        def _():
            dq_ref[0] = dq_acc[...].astype(BF16)

    head = lambda w: pl.BlockSpec((1, S, w), lambda h, j: (h, 0, 0))
    blk = lambda w: pl.BlockSpec((1, T, w), lambda h, j: (h, j, 0))
    ospec = pl.BlockSpec((S, V_HEAD), lambda h, j: (0, h))
    return pl.pallas_call(
        body, name="attn_bwd", grid=(H, n),
        in_specs=[head(QK_CAT), blk(QK_CAT), blk(V_HEAD), ospec, ospec, head(1)],
        out_specs=[head(QK_CAT), blk(QK_CAT), blk(V_HEAD)],
        out_shape=[jax.ShapeDtypeStruct((H, S, QK_CAT), BF16), jax.ShapeDtypeStruct((H, S, QK_CAT), BF16),
                   jax.ShapeDtypeStruct((H, S, V_HEAD), BF16)],
        scratch_shapes=[pltpu.VMEM((S, 1), F32), pltpu.VMEM((S, QK_CAT), F32), pltpu.VMEM((T, QK_CAT), F32),
                        pltpu.VMEM((T, V_HEAD), F32)],
        compiler_params=_cparams(("parallel", "arbitrary")),
    )(qc, kc, vh, do, o, lse)


def _qk_bwd(dqc, dkc, dvh, cos_q, sin_q, cos_k, sin_k, ts):
    H, S, _ = dqc.shape
    pair = 2 * QK_CAT
    kv_w = QK_NOPE + V_HEAD

    def body(dqc_ref, dkc_ref, dvh_ref, cq_ref, sq_ref, ck_ref, sk_ref, dq_ref, dkv_ref, dkr_ref, q_buf, kr_buf):
        for p in range(H // 2):
            q_buf[:, :QK_CAT] = dqc_ref[2 * p].astype(F32)
            q_buf[:, QK_CAT:] = dqc_ref[2 * p + 1].astype(F32)
            g = q_buf[...]
            dq_ref[:, p * pair:(p + 1) * pair] = (
                g * cq_ref[...] - _rope_partner(g, QK_CAT, QK_NOPE) * sq_ref[...]).astype(BF16)
        kr_sum = jnp.zeros((ts, QK_ROPE), F32)
        for h in range(H):
            dkv_ref[:, h * kv_w:h * kv_w + QK_NOPE] = dkc_ref[h, :, 0:QK_NOPE].astype(BF16)
            dkv_ref[:, h * kv_w + QK_NOPE:(h + 1) * kv_w] = dvh_ref[h].astype(BF16)
            kr_sum = kr_sum + dkc_ref[h, :, QK_NOPE:QK_CAT]
        kr_buf[...] = jnp.zeros_like(kr_buf)
        kr_buf[:, 0:QK_ROPE] = kr_sum
        kr = kr_buf[...]
        dkr_ref[...] = (kr * ck_ref[...] - _rope_partner(kr, QK_ROPE, 0) * sk_ref[...]).astype(BF16)

    row = lambda w: pl.BlockSpec((ts, w), lambda i: (i, 0))
    head = lambda w: pl.BlockSpec((H, ts, w), lambda i: (0, i, 0))
    return pl.pallas_call(
        body, name="qk_bwd", grid=(S // ts,),
        in_specs=[head(QK_CAT), head(QK_CAT), head(V_HEAD), row(pair), row(pair), row(COL_BLOCK), row(COL_BLOCK)],
        out_specs=[row(H * QK_CAT), row(H * kv_w), row(COL_BLOCK)],
        out_shape=[jax.ShapeDtypeStruct((S, H * QK_CAT), BF16), jax.ShapeDtypeStruct((S, H * kv_w), BF16),
                   jax.ShapeDtypeStruct((S, COL_BLOCK), BF16)],
        scratch_shapes=[pltpu.VMEM((ts, pair), F32), pltpu.VMEM((ts, COL_BLOCK), F32)],
        compiler_params=_cparams(("parallel",)),
    )(dqc, dkc, dvh, cos_q, sin_q, cos_k, sin_k)


def _rms_bwd(dy, proj, g, blk, L, ts, name):
    S = proj.shape[0]

    def body(dy_ref, a_ref, g_ref, da_ref, dg_ref):
        i = pl.program_id(0)

        @pl.when(i == 0)
        def _():
            dg_ref[...] = jnp.zeros_like(dg_ref)

        a, dy = a_ref[...].astype(F32), dy_ref[...]
        r = lax.rsqrt(jnp.mean(a * a, axis=-1, keepdims=True) + RMS_EPS)
        dyh = dy * g_ref[...]
        da = r * dyh - a * (r * r * r) * jnp.mean(dyh * a, axis=-1, keepdims=True)
        da_ref[...] = da.astype(BF16)
        dg_ref[0:1, :] += jnp.sum(dy * a * r, axis=0, keepdims=True)

    return pl.pallas_call(
        body, name=name, grid=(S // ts,),
        in_specs=[pl.BlockSpec((ts, L), lambda i: (i, 0)), pl.BlockSpec((ts, L), lambda i: (i, blk)),
                  pl.BlockSpec((1, L), lambda i: (0, 0))],
        out_specs=[pl.BlockSpec((ts, L), lambda i: (i, 0)), pl.BlockSpec((8, L), lambda i: (0, 0))],
        out_shape=[jax.ShapeDtypeStruct((S, L), BF16), jax.ShapeDtypeStruct((8, L), F32)],
        compiler_params=_cparams(("arbitrary",)),
    )(dy, proj, g)


def _grad_x(du, dxa, x, mod, ts):
    S, D = x.shape

    def body(du_ref, dxa_ref, x_ref, mod_ref, dx_ref, vec_ref):
        i = pl.program_id(0)

        @pl.when(i == 0)
        def _():
            vec_ref[...] = jnp.zeros_like(vec_ref)

        du = du_ref[...]
        dx_ref[...] = dxa_ref[...] + du * (1.0 + mod_ref[1:2, :])
        vec_ref[0:1, :] += jnp.sum(du, axis=0, keepdims=True)
        vec_ref[1:2, :] += jnp.sum(du * x_ref[...], axis=0, keepdims=True)

    row = pl.BlockSpec((ts, D), lambda i: (i, 0))
    vec = lambda r: pl.BlockSpec((r, D), lambda i: (0, 0))
    return pl.pallas_call(
        body, name="grad_x", grid=(S // ts,),
        in_specs=[row, row, row, vec(6)],
        out_specs=[row, vec(8)],
        out_shape=[jax.ShapeDtypeStruct((S, D), F32), jax.ShapeDtypeStruct((8, D), F32)],
        compiler_params=_cparams(("arbitrary",)),
    )(du, dxa, x, mod)


def _adamw(w, g, m, v, name):
    R, C = w.shape
    tr = _tile(R, max(8, (1 << 19) // C), 8)
    c1 = 1.0 / (1.0 - ADAM_B1 ** ADAM_STEP)
    c2 = 1.0 / (1.0 - ADAM_B2 ** ADAM_STEP)

    def body(w_ref, g_ref, m_ref, v_ref, d_ref, nm_ref, nv_ref):
        g = g_ref[...]
        m = ADAM_B1 * m_ref[...] + (1.0 - ADAM_B1) * g
        v = ADAM_B2 * v_ref[...] + (1.0 - ADAM_B2) * (g * g)
        nm_ref[...] = m
        nv_ref[...] = v
        d_ref[...] = -ADAM_LR * ((m * c1) / (jnp.sqrt(v * c2) + ADAM_EPS) + ADAM_WD * w_ref[...])

    spec = pl.BlockSpec((tr, C), lambda i: (i, 0))
    out = jax.ShapeDtypeStruct((R, C), F32)
    return pl.pallas_call(
        body, name=name, grid=(R // tr,),
        in_specs=[spec] * 4, out_specs=[spec] * 3, out_shape=[out] * 3,
        compiler_params=_cparams(("parallel",)),
    )(w, g, m, v)


def _adamw_ada(w, cact_t, dmod, m, v):
    R, C = w.shape
    tr = _tile(R, max(8, (1 << 18) // C), 8)
    c1 = 1.0 / (1.0 - ADAM_B1 ** ADAM_STEP)
    c2 = 1.0 / (1.0 - ADAM_B2 ** ADAM_STEP)

    def body(w_ref, ct_ref, dm_ref, m_ref, v_ref, g_ref, d_ref, nm_ref, nv_ref):
        ct = ct_ref[...].astype(BF16).astype(F32)
        dm = dm_ref[...].astype(BF16).astype(F32)
        g = ct[:, 0:1] * dm[0:1, :]
        for b in range(1, N_DEV):
            g = g + ct[:, b:b + 1] * dm[b:b + 1, :]
        m = ADAM_B1 * m_ref[...] + (1.0 - ADAM_B1) * g
        v = ADAM_B2 * v_ref[...] + (1.0 - ADAM_B2) * (g * g)
        g_ref[...] = g
        nm_ref[...] = m
        nv_ref[...] = v
        d_ref[...] = -ADAM_LR * ((m * c1) / (jnp.sqrt(v * c2) + ADAM_EPS) + ADAM_WD * w_ref[...])

    spec = pl.BlockSpec((tr, C), lambda i: (i, 0))
    out = jax.ShapeDtypeStruct((R, C), F32)
    return pl.pallas_call(
        body, name="adamw_w_ada", grid=(R // tr,),
        in_specs=[spec, pl.BlockSpec((tr, N_DEV), lambda i: (i, 0)), pl.BlockSpec((N_DEV, C), lambda i: (0, 0)),
                  spec, spec],
        out_specs=[spec] * 4, out_shape=[out] * 4,
        compiler_params=_cparams(("parallel",)),
    )(w, cact_t, dmod, m, v)


def _adamw_reduced(w, own, got, m, v, my_chip, name):
    R, C = w.shape
    tr = _tile(R, max(PACK_ROW_ALIGN, (1 << 18) // C), PACK_ROW_ALIGN)
    c1 = 1.0 / (1.0 - ADAM_B1 ** ADAM_STEP)
    c2 = 1.0 / (1.0 - ADAM_B2 ** ADAM_STEP)

    def body(chip_ref, w_ref, own_ref, g1_ref, g2_ref, g3_ref, m_ref, v_ref, g_ref, d_ref, nm_ref, nv_ref):
        g = own_ref[0].astype(F32) + g1_ref[0].astype(F32) + g2_ref[0].astype(F32) + g3_ref[0].astype(F32)
        m = ADAM_B1 * m_ref[...] + (1.0 - ADAM_B1) * g
        v = ADAM_B2 * v_ref[...] + (1.0 - ADAM_B2) * (g * g)
        g_ref[...] = g
        nm_ref[...] = m
        nv_ref[...] = v
        d_ref[...] = -ADAM_LR * ((m * c1) / (jnp.sqrt(v * c2) + ADAM_EPS) + ADAM_WD * w_ref[...])

    spec = pl.BlockSpec((tr, C), lambda i, chip: (i, 0))
    slot = lambda k: pl.BlockSpec((1, tr, C), lambda i, chip: (chip[0] ^ k, i, 0))
    out = jax.ShapeDtypeStruct((R, C), F32)
    return pl.pallas_call(
        body, name=name,
        grid_spec=pltpu.PrefetchScalarGridSpec(
            num_scalar_prefetch=1, grid=(R // tr,),
            in_specs=[spec, slot(0), slot(1), slot(2), slot(3), spec, spec],
            out_specs=[spec] * 4),
        out_shape=[out] * 4,
        compiler_params=_cparams(("parallel",)),
    )(my_chip, w, own, got, got, got, m, v)


def _my_place():
    return lax.axis_index("x"), lax.axis_index("y"), lax.axis_index("c")


def _peer(k):
    x, y, c = _my_place()
    return (x ^ ((k >> 2) & 1), y ^ ((k >> 1) & 1), c ^ (k & 1))


def _linear(place):
    return 4 * place[0] + 2 * place[1] + place[2]


def _ada_fwd(c_row, wconv_row, w_ada, b_row):
    D, CW = w_ada.shape
    WC = wconv_row.shape[-1]

    def body(c_ref, wc_ref, w_ref, b_ref, mod_ref, cact_ref, wcall_ref, send_buf, sems):
        me = _linear(_my_place())
        c = c_ref[0]
        cact_ref[me] = c * _sigmoid(c)
        wcall_ref[me] = wc_ref[0]

        def gather_copy(buf, k, grp):
            return pltpu.make_async_remote_copy(
                src_ref=buf.at[me], dst_ref=buf.at[me], send_sem=sems.at[0, grp, k], recv_sem=sems.at[1, grp, k],
                device_id=_peer(k), device_id_type=MESH_ID)

        def gather_recv(buf, k, grp):
            src = _linear(_peer(k))
            return pltpu.make_async_remote_copy(
                src_ref=buf.at[src], dst_ref=buf.at[src], send_sem=sems.at[0, grp, k], recv_sem=sems.at[1, grp, k],
                device_id=_peer(k), device_id_type=MESH_ID)

        for k in range(1, N_DEV):
            gather_copy(cact_ref, k, 0).start()
            gather_copy(wcall_ref, k, 1).start()
        for k in range(1, N_DEV):
            gather_recv(cact_ref, k, 0).wait_recv()
            gather_recv(wcall_ref, k, 1).wait_recv()
        for k in range(1, N_DEV):
            gather_copy(cact_ref, k, 0).wait_send()
            gather_copy(wcall_ref, k, 1).wait_send()

        cact = jnp.concatenate([cact_ref[b] for b in range(N_DEV)], axis=0)
        mod_all = jnp.dot(cact.astype(BF16), w_ref[...].astype(BF16), preferred_element_type=F32) + b_ref[0]
        for b in range(N_DEV):
            send_buf[b] = mod_all[b:b + 1, :]
        mod_ref[me] = send_buf[me]

        def scatter_copy(k):
            dst = _linear(_peer(k))
            return pltpu.make_async_remote_copy(
                src_ref=send_buf.at[dst], dst_ref=mod_ref.at[me], send_sem=sems.at[0, 2, k], recv_sem=sems.at[1, 2, k],
                device_id=_peer(k), device_id_type=MESH_ID)

        def scatter_recv(k):
            src = _linear(_peer(k))
            return pltpu.make_async_remote_copy(
                src_ref=send_buf.at[src], dst_ref=mod_ref.at[src], send_sem=sems.at[0, 2, k], recv_sem=sems.at[1, 2, k],
                device_id=_peer(k), device_id_type=MESH_ID)

        for k in range(1, N_DEV):
            scatter_copy(k).start()
        for k in range(1, N_DEV):
            scatter_recv(k).wait_recv()
        for k in range(1, N_DEV):
            scatter_copy(k).wait_send()

    vmem = pl.BlockSpec(memory_space=pltpu.VMEM)
    return pl.pallas_call(
        body, name="ada_fwd",
        in_specs=[vmem] * 4, out_specs=[vmem] * 3,
        out_shape=[jax.ShapeDtypeStruct((N_DEV, 1, CW), F32), jax.ShapeDtypeStruct((N_DEV, 1, D), F32),
                   jax.ShapeDtypeStruct((N_DEV, 1, WC), F32)],
        scratch_shapes=[pltpu.VMEM((N_DEV, 1, CW), F32), pltpu.SemaphoreType.DMA((2, 3, N_DEV))],
        compiler_params=pltpu.CompilerParams(vmem_limit_bytes=VMEM_LIMIT),
    )(c_row, wconv_row, w_ada, b_row)


def _ada_bwd(payload, deps=()):
    NCH, _, CW = payload.shape

    def body(p_ref, *rest):
        sum_ref, mine_ref, all_ref, sems = rest[-4:]
        me = _linear(_my_place())
        all_ref[me] = p_ref[...]

        def copy(k, slot):
            return pltpu.make_async_remote_copy(
                src_ref=all_ref.at[slot], dst_ref=all_ref.at[slot], send_sem=sems.at[0, k], recv_sem=sems.at[1, k],
                device_id=_peer(k), device_id_type=MESH_ID)

        for k in range(1, N_DEV):
            copy(k, me).start()
        for k in range(1, N_DEV):
            copy(k, _linear(_peer(k))).wait_recv()
        for k in range(1, N_DEV):
            copy(k, me).wait_send()

        total = all_ref[0]
        for b in range(1, N_DEV):
            total = total + all_ref[b]
        sum_ref[...] = total

        for b in range(N_DEV):
            mine_ref[b] = all_ref[b, me]

    vmem = pl.BlockSpec(memory_space=pltpu.VMEM)
    return pl.pallas_call(
        body, name="ada_bwd",
        in_specs=[vmem] + [ANY_SPEC] * len(deps), out_specs=[vmem, vmem],
        out_shape=[jax.ShapeDtypeStruct((NCH, 1, CW), F32), jax.ShapeDtypeStruct((N_DEV, 1, CW), F32)],
        scratch_shapes=[pltpu.VMEM((N_DEV, NCH, 1, CW), F32), pltpu.SemaphoreType.DMA((2, N_DEV))],
        compiler_params=pltpu.CompilerParams(vmem_limit_bytes=VMEM_LIMIT),
    )(payload, *deps)


def _exchange_in_chip(parts):
    W = len(parts)

    def body(*refs):
        p_refs, got_refs, (send_sems, recv_sems) = refs[:W], refs[W:2 * W], refs[2 * W:]
        x, y, c = _my_place()
        sibling = (x, y, 1 - c)
        copies = []
        for w in range(W):
            for q in range(4):
                copies.append(pltpu.make_async_remote_copy(
                    src_ref=p_refs[w].at[2 * q + (1 - c)], dst_ref=got_refs[w].at[q],
                    send_sem=send_sems.at[4 * w + q], recv_sem=recv_sems.at[4 * w + q],
                    device_id=sibling, device_id_type=MESH_ID))
        for cp in copies:
            cp.start()
        for cp in copies:
            cp.wait_recv()
        for cp in copies:
            cp.wait_send()

    return pl.pallas_call(
        body, name="grad_exchange_in_chip",
        in_specs=[HBM_SPEC] * W, out_specs=[HBM_SPEC] * W,
        out_shape=[jax.ShapeDtypeStruct((4,) + p.shape[1:], p.dtype) for p in parts],
        scratch_shapes=[pltpu.SemaphoreType.DMA((4 * W,)), pltpu.SemaphoreType.DMA((4 * W,))],
    )(*parts)


def _pair_sum(parts, got, core):
    _, R, C = parts.shape
    tr = _tile(R, max(PACK_ROW_ALIGN, PAIR_SUM_BLOCK // C), PACK_ROW_ALIGN)

    def body(c_ref, p_ref, g_ref, o_ref):
        o_ref[...] = (p_ref[...].astype(F32) + g_ref[...].astype(F32)).astype(o_ref.dtype)

    return pl.pallas_call(
        body, name="grad_pair_sum",
        grid_spec=pltpu.PrefetchScalarGridSpec(
            num_scalar_prefetch=1, grid=(4, R // tr),
            in_specs=[pl.BlockSpec((1, tr, C), lambda q, i, c_ref: (2 * q + c_ref[0], i, 0)),
                      pl.BlockSpec((1, tr, C), lambda q, i, c_ref: (q, i, 0))],
            out_specs=pl.BlockSpec((1, tr, C), lambda q, i, c_ref: (q, i, 0))),
        out_shape=jax.ShapeDtypeStruct((4, R, C), parts.dtype),
        compiler_params=_cparams(("parallel", "parallel")),
    )(core, parts, got)


HBM_SPEC = pl.BlockSpec(memory_space=pltpu.HBM)
SEM_SPEC = pl.BlockSpec(memory_space=pltpu.SEMAPHORE)
ANY_SPEC = pl.BlockSpec(memory_space=pl.ANY)
SPLIT_EFFECT = pltpu.SideEffectType.DATAFLOW_SIDE_EFFECTING


def _landing_zone(shape, dtype):
    return pltpu.with_memory_space_constraint(lax.empty(shape, dtype), pltpu.HBM)


def _split_start(name, arrays, lands, after, copies_of, per_array):
    W = len(arrays)
    after = tuple(after) if isinstance(after, (tuple, list)) else (after,)

    def body(*refs):
        x_refs, land_refs = refs[:W], refs[W:2 * W]
        send_sems, recv_sems = refs[2 * W + len(after)], refs[2 * W + len(after) + 1]
        token = refs[-1]
        k = 0
        for w in range(W):
            for src, dst, dev in copies_of(w, x_refs[w], land_refs[w]):
                pltpu.make_async_remote_copy(src_ref=src, dst_ref=dst, send_sem=send_sems.at[k], recv_sem=recv_sems.at[k],
                                             device_id=dev, device_id_type=MESH_ID).start()
                k += 1
        token[...] = jnp.zeros_like(token)

    n_copies = per_array * W
    hbm_of = lambda xs: tuple(pltpu.HBM(a.shape, a.dtype) for a in xs)
    out = pl.pallas_call(
        body, name=name,
        out_shape=(pltpu.SemaphoreType.DMA((n_copies,)), pltpu.SemaphoreType.DMA((n_copies,)))
        + hbm_of(arrays) + hbm_of(lands) + (jax.ShapeDtypeStruct((8, LANE), F32),),
        in_specs=(HBM_SPEC,) * (2 * W) + (ANY_SPEC,) * len(after),
        out_specs=(SEM_SPEC, SEM_SPEC) + (HBM_SPEC,) * (2 * W) + (pl.BlockSpec(memory_space=pltpu.VMEM),),
        input_output_aliases={i: 2 + i for i in range(2 * W)},
        compiler_params=pltpu.CompilerParams(has_side_effects=SPLIT_EFFECT),
    )(*[pltpu.with_memory_space_constraint(a, pltpu.HBM) for a in arrays], *lands, *after)
    return out[0], out[1], list(out[2:2 + W]), list(out[2 + W:2 + 2 * W]), out[-1]


def _split_wait(name, state, after, copies_of):
    send_sems, recv_sems, arrays, lands, _ = state
    W = len(arrays)
    after = tuple(after) if isinstance(after, (tuple, list)) else (after,)

    def body(*refs):
        x_refs, land_refs = refs[:W], refs[W:2 * W]
        send_sems, recv_sems = refs[2 * W], refs[2 * W + 1]
        k = 0
        for w in range(W):
            for src, dst, dev in copies_of(w, x_refs[w], land_refs[w]):
                cp = pltpu.make_async_remote_copy(src_ref=src, dst_ref=dst, send_sem=send_sems.at[k],
                                                  recv_sem=recv_sems.at[k], device_id=dev, device_id_type=MESH_ID)
                cp.wait_send()
                cp.wait_recv()
                k += 1

    out = pl.pallas_call(
        body, name=name,
        out_shape=tuple(pltpu.HBM(a.shape, a.dtype) for a in arrays + lands),
        in_specs=(HBM_SPEC,) * (2 * W) + (SEM_SPEC, SEM_SPEC) + (ANY_SPEC,) * len(after),
        out_specs=(HBM_SPEC,) * (2 * W),
        input_output_aliases={i: i for i in range(2 * W)},
        compiler_params=pltpu.CompilerParams(has_side_effects=SPLIT_EFFECT),
    )(*arrays, *lands, send_sems, recv_sems, *after)
    return list(out[:W]), list(out[W:])


def _scatter_copies(w, p_ref, land_ref):
    x, y, c = _my_place()
    my_chip = 2 * x + y
    return [(p_ref.at[2 * (x ^ (k >> 1)) + (y ^ (k & 1))], land_ref.at[my_chip], (x ^ (k >> 1), y ^ (k & 1), c))
            for k in range(1, 4)]


def _gather_copies(w, x_ref, land_ref):
    x, y, c = _my_place()
    me = _linear((x, y, c))
    devs = [(x, y, 1 - c)] + [(x ^ (k >> 1), y ^ (k & 1), c) for k in range(1, 4)]
    return [(x_ref, land_ref.at[me], d) for d in devs]


def _gather_forward(lands, name):
    W = len(lands)

    def body(*refs):
        land_refs, out_refs, (send_sems, recv_sems) = refs[:W], refs[W:2 * W], refs[2 * W:]
        x, y, c = _my_place()
        sibling = (x, y, 1 - c)
        sends, arrivals = [], []
        for w in range(W):
            for k in range(1, 4):
                px, py = x ^ (k >> 1), y ^ (k & 1)
                landed, theirs = _linear((px, py, c)), out_refs[w].at[_linear((px, py, 1 - c))]
                sem = 3 * w + k - 1
                sends.append(pltpu.make_async_remote_copy(
                    src_ref=land_refs[w].at[landed], dst_ref=out_refs[w].at[landed],
                    send_sem=send_sems.at[sem], recv_sem=recv_sems.at[sem], device_id=sibling, device_id_type=MESH_ID))
                arrivals.append(pltpu.make_async_remote_copy(
                    src_ref=theirs, dst_ref=theirs, send_sem=send_sems.at[sem], recv_sem=recv_sems.at[sem],
                    device_id=sibling, device_id_type=MESH_ID))
        for cp in sends:
            cp.start()
        for cp in arrivals:
            cp.wait_recv()
        for cp in sends:
            cp.wait_send()

    return pl.pallas_call(
        body, name=name,
        in_specs=[HBM_SPEC] * W, out_specs=[HBM_SPEC] * W,
        out_shape=[jax.ShapeDtypeStruct(l.shape, l.dtype) for l in lands],
        input_output_aliases={i: i for i in range(W)},
        scratch_shapes=[pltpu.SemaphoreType.DMA((3 * W,)), pltpu.SemaphoreType.DMA((3 * W,))],
    )(*lands)


def _with_own_slot(gathered, shard):
    return lax.dynamic_update_index_in_dim(gathered, shard[None], _linear(_my_place()), axis=0)


def _in_chip_copies(w, p_ref, land_ref):
    x, y, c = _my_place()
    return [(p_ref.at[2 * q + (1 - c)], land_ref.at[q], (x, y, 1 - c)) for q in range(4)]


def _in_chip_start(parts, tag):
    lands = [_landing_zone((4,) + p.shape[1:], p.dtype) for p in parts]
    return _split_start("grad_in_chip_start_" + tag, parts, lands, (), _in_chip_copies, 4)


def _reduce_scatter_begin(parts, tag, in_chip_state=None, after=()):
    parts, early, got = list(parts), [], []
    if in_chip_state is not None:
        early, got = _split_wait("grad_in_chip_wait_" + tag, in_chip_state, after, _in_chip_copies)
    if parts:
        got = got + list(_exchange_in_chip(parts))
    parts = early + parts
    core = lax.axis_index("c").astype(jnp.int32).reshape(1)
    chip_parts = [_pair_sum(p, g, core) for p, g in zip(parts, got)]
    lands = [_landing_zone(p.shape, p.dtype) for p in chip_parts]
    return _split_start("grad_scatter_start_" + tag, chip_parts, lands, got[0], _scatter_copies, 3)


def _reduce_scatter_end(state, after, tag):
    return _split_wait("grad_scatter_wait_" + tag, state, after, _scatter_copies)


def kernel(x, c, positions, w_ada, b_ada, w_in, g_q_a, w_q_b, g_kv_a, w_kv_b, w_o_a, w_conv, w_o_b, w_o, ln1_g, ln1_b, w_ffn_in, w_ffn_out, ln2_g, ln2_b, loss_target, m_w_ada, m_b_ada, m_w_in, m_g_q_a, m_w_q_b, m_g_kv_a, m_w_kv_b, m_w_o_a, m_w_conv, m_w_o_b, m_w_o, m_ln1_g, m_ln1_b, m_w_ffn_in, m_w_ffn_out, m_ln2_g, m_ln2_b, v_w_ada, v_b_ada, v_w_in, v_g_q_a, v_w_q_b, v_g_kv_a, v_w_kv_b, v_w_o_a, v_w_conv, v_w_o_b, v_w_o, v_ln1_g, v_ln1_b, v_w_ffn_in, v_w_ffn_out, v_ln2_g, v_ln2_b):
    x2, tgt = x[0], loss_target[0]
    S, D = x2.shape
    Lq, Lkv = g_q_a.shape[1], g_kv_a.shape[1]
    H = w_q_b.shape[2] * N_DEV // QK_CAT
    F = w_ffn_out.shape[1] * N_DEV
    assert Lq == Lkv and (Lq + Lkv) % COL_BLOCK == 0 and D % COL_BLOCK == 0
    front = Lq + Lkv + QK_ROPE
    front_pad = _round_up(front, COL_BLOCK)
    kr_blk = (Lq + Lkv) // COL_BLOCK
    blk_b = front_pad // COL_BLOCK
    nblk = D // COL_BLOCK
    blk_c, blk_x, blk_ga, blk_gb = blk_b + nblk, blk_b + 2 * nblk, blk_b + 3 * nblk, blk_b + 4 * nblk
    ts = _tile(S, 256, 8)
    T = _tile(S, min(512, S // 2), CHUNK)
    tb = _tile(F, 2816)
    me = _linear(_my_place())

    cw = w_ada.shape[2]
    b_mine = lax.dynamic_slice(b_ada, (0, me * cw), (1, cw)).reshape(1, 1, cw)
    mod_blocks, cact_all, wconv_all = _ada_fwd(c.reshape(1, 1, D), w_conv[0].reshape(1, 1, -1), w_ada[0], b_mine)
    mod = mod_blocks.reshape(6, D)
    cact_all = cact_all.reshape(N_DEV, D)
    w_conv_full = wconv_all.reshape(N_DEV, CONV_K, -1).transpose(1, 0, 2).reshape(CONV_K, D)

    landing = lambda shards: [_landing_zone((N_DEV,) + s.shape, BF16) for s in shards]
    gathered = lambda lands, shards, tag: [_with_own_slot(g, s) for g, s in
                                           zip(_gather_forward(lands, tag + "_gather_forward"), shards)]
    half = D // 2
    w_in_b = w_in[0].astype(BF16)
    first, second = [w_in_b[:half]], [w_in_b[half:], w_q_b[0].astype(BF16), w_kv_b[0].astype(BF16)]
    mid = [w[0].astype(BF16) for w in (w_o_a, w_o_b, w_o)]
    last = [w[0].astype(BF16) for w in (w_ffn_in, w_ffn_out)]
    first_state = _split_start("first_gather_start", first, landing(first), mod_blocks, _gather_copies, 4)
    second_state = _split_start("second_gather_start", second, landing(second), first_state[4], _gather_copies, 4)
    u = _modulate_in(x2, mod, ts)

    first_shards, first_lands = _split_wait("first_gather_wait", first_state, (u, second_state[4]), _gather_copies)
    (g_in_top,) = gathered(first_lands, first_shards, "first")
    w_in_top = _assemble_w_in(g_in_top, front, front_pad, D, 0)
    proj_top = _matmul(u, w_in_top, "nn", BF16, "proj_top", k_rows=(0, half))
    second_shards, second_lands = _split_wait("second_gather_wait", second_state, (proj_top,), _gather_copies)
    g_in_bottom, wq_s, wkv_s = gathered(second_lands, second_shards, "second")
    mid_state = _split_start("mid_gather_start", mid, landing(mid), g_in_bottom, _gather_copies, 4)
    last_state = _split_start("last_gather_start", last, landing(last), mid_state[4], _gather_copies, 4)
    w_in_p = _assemble_w_in(g_in_bottom, front, front_pad, D, half, into=w_in_top)

    inv_freq = 1.0 / (ROPE_THETA ** (jnp.arange(0, QK_ROPE, 2, dtype=F32) / QK_ROPE))
    ang = positions[0].astype(F32)[:, None] * inv_freq
    cos2 = jnp.concatenate([jnp.cos(ang), jnp.cos(ang)], axis=-1)
    sin2 = jnp.concatenate([jnp.sin(ang), jnp.sin(ang)], axis=-1)
    one, zero = jnp.ones((S, QK_NOPE), F32), jnp.zeros((S, QK_NOPE), F32)
    cos_q, sin_q = jnp.concatenate([one, cos2, one, cos2], axis=-1), jnp.concatenate([zero, sin2, zero, sin2], axis=-1)
    cos_k, sin_k = jnp.tile(cos2, (1, COL_BLOCK // QK_ROPE)), jnp.tile(sin2, (1, COL_BLOCK // QK_ROPE))

    proj = _matmul(u, w_in_p, "nn", BF16, "proj", k_rows=(half, half), init=proj_top, deps=(last_state[4],))
    qn = _rms_fwd(proj, g_q_a, 0, Lq, ts, "rms_q")
    kvn = _rms_fwd(proj, g_kv_a, 1, Lkv, ts, "rms_kv")
    q = _matmul(qn, wq_s, "nn", BF16, "q_up")
    kv = _matmul(kvn, wkv_s, "nn", BF16, "kv_up")
    qc, kc, vh = _qk_prep(q, kv, proj, kr_blk, cos_q, sin_q, cos_k, sin_k, H, ts)
    attn, lse = _attn_fwd(qc, kc, vh, T)
    mid_shards, mid_lands = _split_wait("mid_gather_wait", mid_state, lse, _gather_copies)
    w_oa_f, w_ob_f, w_o_f = [g.reshape(-1, D) for g in gathered(mid_lands, mid_shards, "mid")]
    ya = _matmul(attn, w_oa_f, "nn", BF16, "attn_out")
    cbc = _conv_fwd(proj, w_conv_full, blk_b, blk_c, blk_x)
    yb = _matmul(cbc, w_ob_f, "nn", BF16, "conv_out")
    merged = _merge_fwd(proj, ya, yb, blk_ga, blk_gb, ts)
    mix = _matmul(merged, w_o_f, "nn", F32, "mix_out")
    xhat1, rstd1, u2 = _ln1_fwd(x2, mix, mod, ln1_g, ln1_b, ts)
    last_shards, last_lands = _split_wait("last_gather_wait", last_state, u2, _gather_copies)
    w_fi_s, g_fo = gathered(last_lands, last_shards, "last")
    w_fo_f = g_fo.reshape(F, D)
    hh = _matmul(u2, w_fi_s, "nn", BF16, "ffn_in")
    act = _swiglu_fwd(hh, ts, tb)
    ffn = _matmul(act, w_fo_f, "nn", F32, "ffn_out")
    loss_part, dffn, dx1a, vec2 = _ln2_loss(xhat1, ffn, tgt, mod, ln1_g, ln1_b, ln2_g, ln2_b, ts)
    loss = lax.psum(loss_part[0, 0], AXES)

    gw_fo = _matmul(act, dffn, "tn", BF16, "grad_w_ffn_out")
    da = _matmul(dffn, w_fo_f, "nt", BF16, "d_act")
    dh = _swiglu_bwd(da, hh, ts, tb)
    gw_fi = _matmul(u2, dh, "tn", BF16, "grad_w_ffn_in", out_shards=True)
    ffn_in_chip = _in_chip_start([gw_fi, gw_fo.reshape(N_DEV, -1, D)], "ffn")
    du2 = _matmul(dh, w_fi_s, "nt", F32, "d_u2", deps=(ffn_in_chip[4],))
    ffn_state = _reduce_scatter_begin([], "ffn", ffn_in_chip, after=(du2,))
    dxa, dmix, vec1 = _ln1_bwd(du2, dx1a, xhat1, rstd1, mix, mod, ln1_g, ln1_b, ts)
    gw_o = _matmul(merged, dmix, "tn", BF16, "grad_w_o", deps=(ffn_state[4],))
    dmerged = _matmul(dmix, w_o_f, "nt", BF16, "d_merged")
    dya, dyb, dga, dgb = _merge_bwd(dmerged, proj, ya, yb, blk_ga, blk_gb, ts)
    gw_ob = _matmul(cbc, dyb, "tn", BF16, "grad_w_o_b")
    dcbc = _matmul(dyb, w_ob_f, "nt", BF16, "d_conv")
    dcb, dcc, dcx, dwconv = _conv_bwd(dcbc, proj, w_conv_full, blk_b, blk_c, blk_x)
    gw_oa = _matmul(attn, dya, "tn", BF16, "grad_w_o_a")
    mix_in_chip = _in_chip_start([g.reshape(N_DEV, -1, D) for g in (gw_oa, gw_ob, gw_o)], "mix")
    dattn = _matmul(dya, w_oa_f, "nt", BF16, "d_attn", deps=(mix_in_chip[4],))
    dqc, dkc, dvh = _attn_bwd(qc, kc, vh, dattn, attn, lse, T)
    ffn_own, ffn_got = _reduce_scatter_end(ffn_state, dqc, "ffn")
    dq, dkv, dkr = _qk_bwd(dqc, dkc, dvh, cos_q, sin_q, cos_k, sin_k, ts)
    gw_qb = _matmul(qn, dq, "tn", BF16, "grad_w_q_b", out_shards=True)
    gw_kvb = _matmul(kvn, dkv, "tn", BF16, "grad_w_kv_b", out_shards=True)
    mix_state = _reduce_scatter_begin([gw_qb, gw_kvb], "mix", mix_in_chip, after=(dqc,))
    dqn = _matmul(dq, wq_s, "nt", F32, "d_qn", deps=(mix_state[4],))
    dkvn = _matmul(dkv, wkv_s, "nt", F32, "d_kvn")
    dqa, dgq = _rms_bwd(dqn, proj, g_q_a, 0, Lq, ts, "rms_q_bwd")
    dkva, dgkv = _rms_bwd(dkvn, proj, g_kv_a, 1, Lkv, ts, "rms_kv_bwd")
    dproj = jnp.concatenate([dqa, dkva, dkr, dcb, dcc, dcx, dga, dgb], axis=1)
    gw_in_p = _matmul(u, dproj, "tn", BF16, "grad_w_in")
    mix_own, mix_got = _reduce_scatter_end(mix_state, gw_in_p, "mix")
    in_state = _reduce_scatter_begin([_split_w_in(gw_in_p, front, front_pad)], "in")
    du = _matmul(dproj, w_in_p, "nt", F32, "d_u", deps=(in_state[4],))
    grad_x, vec0 = _grad_x(du, dxa, x2, mod, ts)

    my_chip = (2 * lax.axis_index("x") + lax.axis_index("y")).astype(jnp.int32).reshape(1)
    arrived = {}
    for nm, w, m, v, own, got in (
            ("w_ffn_in", w_ffn_in, m_w_ffn_in, v_w_ffn_in, ffn_own[0], ffn_got[0]),
            ("w_ffn_out", w_ffn_out, m_w_ffn_out, v_w_ffn_out, ffn_own[1], ffn_got[1]),
            ("w_o_a", w_o_a, m_w_o_a, v_w_o_a, mix_own[0], mix_got[0]),
            ("w_o_b", w_o_b, m_w_o_b, v_w_o_b, mix_own[1], mix_got[1]),
            ("w_o", w_o, m_w_o, v_w_o, mix_own[2], mix_got[2]),
            ("w_q_b", w_q_b, m_w_q_b, v_w_q_b, mix_own[3], mix_got[3]),
            ("w_kv_b", w_kv_b, m_w_kv_b, v_w_kv_b, mix_own[4], mix_got[4])):
        arrived[nm] = [a[None] for a in _adamw_reduced(w[0], own, got, m[0], v[0], my_chip, "adamw_" + nm)]

    dmod = jnp.concatenate([vec0[0], vec0[1], vec1[4], vec1[0], vec1[1], vec2[2]])
    small = jnp.concatenate([dmod, dgq[0], dgkv[0], vec1[2], vec1[3], vec2[0], vec2[1], dwconv[:CONV_K].reshape(-1)])
    n_small = small.shape[0]
    nch = _round_up(n_small, cw) // cw
    payload = jnp.pad(small, (0, nch * cw - n_small)).reshape(nch, 1, cw)
    summed, dmod_mine = _ada_bwd(payload, deps=[res[1] for res in arrived.values()])
    arrived["w_ada"] = [a[None] for a in _adamw_ada(w_ada[0], cact_all.T, dmod_mine.reshape(N_DEV, cw),
                                                    m_w_ada[0], v_w_ada[0])]
    summed = summed.reshape(-1)
    offs = [0, 6 * D, 6 * D + Lq, 6 * D + Lq + Lkv]
    offs += [offs[-1] + D * k for k in range(1, 5)]
    g_b_ada = summed[offs[0]:offs[1]].reshape(1, -1)
    g_gq = summed[offs[1]:offs[2]].reshape(1, -1)
    g_gkv = summed[offs[2]:offs[3]].reshape(1, -1)
    g_ln1g, g_ln1b, g_ln2g, g_ln2b = [summed[offs[3 + k]:offs[4 + k]].reshape(1, -1) for k in range(4)]
    wc = w_conv.shape[2]
    g_wconv = lax.dynamic_slice(summed[offs[7]:offs[7] + CONV_K * D].reshape(CONV_K, D), (0, me * wc), (CONV_K, wc))

    names = ["w_ada", "b_ada", "w_in", "g_q_a", "w_q_b", "g_kv_a", "w_kv_b", "w_o_a", "w_conv", "w_o_b", "w_o",
             "ln1_g", "ln1_b", "w_ffn_in", "w_ffn_out", "ln2_g", "ln2_b"]
    weights = [w_ada, b_ada, w_in, g_q_a, w_q_b, g_kv_a, w_kv_b, w_o_a, w_conv, w_o_b, w_o, ln1_g, ln1_b,
               w_ffn_in, w_ffn_out, ln2_g, ln2_b]
    moms = [m_w_ada, m_b_ada, m_w_in, m_g_q_a, m_w_q_b, m_g_kv_a, m_w_kv_b, m_w_o_a, m_w_conv, m_w_o_b, m_w_o,
            m_ln1_g, m_ln1_b, m_w_ffn_in, m_w_ffn_out, m_ln2_g, m_ln2_b]
    vels = [v_w_ada, v_b_ada, v_w_in, v_g_q_a, v_w_q_b, v_g_kv_a, v_w_kv_b, v_w_o_a, v_w_conv, v_w_o_b, v_w_o,
            v_ln1_g, v_ln1_b, v_w_ffn_in, v_w_ffn_out, v_ln2_g, v_ln2_b]
    grad_of = {"b_ada": g_b_ada, "g_q_a": g_gq, "g_kv_a": g_gkv, "w_conv": g_wconv,
               "ln1_g": g_ln1g, "ln1_b": g_ln1b, "ln2_g": g_ln2g, "ln2_b": g_ln2b}
    state_of = dict(zip(names, zip(weights, moms, vels)))
    results = dict(arrived)

    def update(nm, reduced=None):
        w, m, v = state_of[nm]
        shp = w.shape
        w2 = w.reshape(shp[-2], shp[-1]) if w.ndim == 3 else w
        m2, v2 = m.reshape(w2.shape), v.reshape(w2.shape)
        if reduced is None:
            g2 = grad_of[nm].reshape(w2.shape)
            res = (g2,) + tuple(_adamw(w2, g2, m2, v2, "adamw_" + nm))
        else:
            res = _adamw_reduced(w2, reduced[0], reduced[1], m2, v2, my_chip, "adamw_" + nm)
        results[nm] = [a.reshape(shp) for a in res]

    for nm in grad_of:
        update(nm)
    in_own, in_got = _reduce_scatter_end(in_state, [res[1] for res in results.values()], "in")
    update("w_in", (in_own[0], in_got[0]))
    outs = [[results[nm][k] for nm in names] for k in range(4)]
    return (loss, grad_x.reshape(x.shape), *outs[0], *outs[1], *outs[2], *outs[3])
```

```python
import functools

import jax
import jax.numpy as jnp
from jax import lax
from jax.experimental import pallas as pl
from jax.experimental.pallas import tpu as pltpu

F32 = jnp.float32
BF16 = jnp.bfloat16
MESH_ID = pl.DeviceIdType.MESH
AXES = ("x", "y", "c")
N_DEV = 8

CHUNK = 64
QK_NOPE = 128
QK_ROPE = 64
V_HEAD = 128
QK_CAT = QK_NOPE + QK_ROPE
ROPE_THETA = 10000.0
ATTN_SCALE = (QK_NOPE + QK_ROPE) ** -0.5
CONV_K = 3
DEEPNORM_ALPHA = 2.0 ** 0.25
LN_EPS = 1e-5
RMS_EPS = 1e-6
NEG_INF = -1e30

ADAM_LR = 0.001
ADAM_B1 = 0.9
ADAM_B2 = 0.999
ADAM_EPS = 1e-08
ADAM_WD = 0.01
ADAM_STEP = 10

LANE = 128
COL_BLOCK = 256
PACK_ROW_ALIGN = 16
PAIR_SUM_BLOCK = 1 << 20
VMEM_LIMIT = 48 * 1024 * 1024


def _round_up(n, m):
    return (n + m - 1) // m * m


def _tile(n, pref, align=LANE):
    best = None
    t = align
    while t <= min(n, pref):
        if n % t == 0:
            best = t
        t += align
    return best if best is not None else n


def _cparams(sem=None):
    return pltpu.CompilerParams(dimension_semantics=sem, vmem_limit_bytes=VMEM_LIMIT)


def _sigmoid(x):
    return 0.5 * jnp.tanh(0.5 * x) + 0.5


def _matmul(a, b, mode, out_dtype, name, tm=1024, tn=1024, tk=2048, deps=(), out_shards=False, k_rows=None,
            init=None):
    b_shards = b.ndim == 3
    n = b.shape[2] if b_shards else (b.shape[1] // N_DEV if out_shards else None)
    if mode == "nn":
        (M, K), (K2, N) = a.shape, (b.shape[1], N_DEV * n) if b_shards else b.shape
    elif mode == "nt":
        (M, K), (N, K2) = a.shape, (b.shape[1], N_DEV * n) if b_shards else b.shape
    else:
        (K, M), (K2, N) = a.shape, b.shape
    assert K == K2, (a.shape, b.shape, mode)
    tm = _tile(M, tm)
    tn = n if (mode != "nt" and n is not None) else _tile(N, tn)
    k_row0, k_len = k_rows if k_rows is not None else (0, K)
    tk = n if (mode == "nt" and b_shards) else _tile(k_len, tk)
    nk, k0 = k_len // tk, k_row0 // tk
    if mode == "nn":
        a_spec = pl.BlockSpec((tm, tk), lambda i, j, k: (i, k0 + k))
        b_spec = (pl.BlockSpec((1, tk, n), lambda i, j, k: (j, k, 0)) if b_shards
                  else pl.BlockSpec((tk, tn), lambda i, j, k: (k0 + k, j)))
        dims = (((1,), (0,)), ((), ()))
    elif mode == "nt":
        a_spec = pl.BlockSpec((tm, tk), lambda i, j, k: (i, k))
        b_spec = (pl.BlockSpec((1, tn, n), lambda i, j, k: (k, j, 0)) if b_shards
                  else pl.BlockSpec((tn, tk), lambda i, j, k: (j, k)))
        dims = (((1,), (1,)), ((), ()))
    else:
        a_spec = pl.BlockSpec((tk, tm), lambda i, j, k: (k, i))
        b_spec = pl.BlockSpec((tk, tn), lambda i, j, k: (k, j))
        dims = (((0,), (0,)), ((), ()))
    if out_shards:
        out_spec = pl.BlockSpec((1, tm, n), lambda i, j, k: (j, i, 0))
        out_shape = jax.ShapeDtypeStruct((N_DEV, M, n), out_dtype)
    else:
        out_spec = pl.BlockSpec((tm, tn), lambda i, j, k: (i, j))
        out_shape = jax.ShapeDtypeStruct((M, N), out_dtype)

    def product(a_ref, b_ref):
        b_blk = b_ref[0] if b_shards else b_ref[...]
        return lax.dot_general(a_ref[...].astype(BF16), b_blk.astype(BF16), dims, preferred_element_type=F32)

    def write(o_ref, value):
        if out_shards:
            o_ref[0] = value.astype(o_ref.dtype)
        else:
            o_ref[...] = value.astype(o_ref.dtype)

    def body_whole_k(a_ref, b_ref, *rest):
        value = product(a_ref, b_ref)
        write(rest[-1], value if init is None else value + rest[0][...])

    def body_split_k(a_ref, b_ref, *rest):
        o_ref, acc_ref = rest[-2:]
        k = pl.program_id(2)

        @pl.when(k == 0)
        def _():
            acc_ref[...] = jnp.zeros_like(acc_ref) if init is None else rest[0][...].astype(F32)

        acc_ref[...] += product(a_ref, b_ref)

        @pl.when(k == nk - 1)
        def _():
            write(o_ref, acc_ref[...])

    return pl.pallas_call(
        body_whole_k if nk == 1 else body_split_k, name=name, grid=(M // tm, N // tn, nk),
        in_specs=[a_spec, b_spec] + ([] if init is None else [out_spec]) + [ANY_SPEC] * len(deps),
        out_specs=out_spec, out_shape=out_shape,
        scratch_shapes=[] if nk == 1 else [pltpu.VMEM((tm, tn), F32)],
        compiler_params=_cparams(("parallel", "parallel", "arbitrary")),
    )(a, b, *(() if init is None else (init,)), *deps)


def _assemble_w_in(shards, front, front_pad, rows, row0, into=None):
    _, K, n = shards.shape
    gap = front_pad - front
    tk = _tile(K, 256, PACK_ROW_ALIGN)
    blk0 = row0 // tk

    def body(g_ref, *rest):
        o_ref = rest[-1]
        if gap:
            o_ref[:, front:front_pad] = jnp.zeros((tk, gap), o_ref.dtype)
        for j in range(N_DEV):
            lo, hi = j * n, (j + 1) * n
            if lo < front < hi:
                o_ref[:, lo:front] = g_ref[j, :, 0:front - lo]
                o_ref[:, front_pad:hi + gap] = g_ref[j, :, front - lo:n]
            else:
                off = 0 if hi <= front else gap
                o_ref[:, lo + off:hi + off] = g_ref[j]

    return pl.pallas_call(
        body, name="assemble_w_in", grid=(K // tk,),
        in_specs=[pl.BlockSpec((N_DEV, tk, n), lambda i: (0, i, 0))] + ([] if into is None else [ANY_SPEC]),
        out_specs=pl.BlockSpec((tk, N_DEV * n + gap), lambda i: (blk0 + i, 0)),
        out_shape=jax.ShapeDtypeStruct((rows, N_DEV * n + gap), shards.dtype),
        input_output_aliases={} if into is None else {1: 0},
        compiler_params=_cparams(("parallel",)),
    )(*([shards] if into is None else [shards, into]))


def _split_w_in(w, front, front_pad):
    K, NP = w.shape
    gap = front_pad - front
    n = (NP - gap) // N_DEV
    tk = _tile(K, 256, PACK_ROW_ALIGN)

    def body(w_ref, o_ref):
        for j in range(N_DEV):
            lo, hi = j * n, (j + 1) * n
            if lo < front < hi:
                o_ref[j, :, 0:front - lo] = w_ref[:, lo:front]
                o_ref[j, :, front - lo:n] = w_ref[:, front_pad:hi + gap]
            else:
                off = 0 if hi <= front else gap
                o_ref[j] = w_ref[:, lo + off:hi + off]

    return pl.pallas_call(
        body, name="split_grad_w_in", grid=(K // tk,),
        in_specs=[pl.BlockSpec((tk, NP), lambda i: (i, 0))],
        out_specs=pl.BlockSpec((N_DEV, tk, n), lambda i: (0, i, 0)),
        out_shape=jax.ShapeDtypeStruct((N_DEV, K, n), w.dtype),
        compiler_params=_cparams(("parallel",)),
    )(w)


def _modulate_in(x, mod, ts):
    S, D = x.shape

    def body(x_ref, mod_ref, u_ref):
        u_ref[...] = (x_ref[...] * (1.0 + mod_ref[1:2, :]) + mod_ref[0:1, :]).astype(BF16)

    return pl.pallas_call(
        body, name="modulate_in", grid=(S // ts,),
        in_specs=[pl.BlockSpec((ts, D), lambda i: (i, 0)), pl.BlockSpec((6, D), lambda i: (0, 0))],
        out_specs=pl.BlockSpec((ts, D), lambda i: (i, 0)),
        out_shape=jax.ShapeDtypeStruct((S, D), BF16),
        compiler_params=_cparams(("parallel",)),
    )(x, mod)


def _rms_fwd(proj, g, blk, L, ts, name):
    S = proj.shape[0]

    def body(a_ref, g_ref, y_ref):
        a = a_ref[...].astype(F32)
        r = lax.rsqrt(jnp.mean(a * a, axis=-1, keepdims=True) + RMS_EPS)
        y_ref[...] = (a * r * g_ref[...]).astype(BF16)

    return pl.pallas_call(
        body, name=name, grid=(S // ts,),
        in_specs=[pl.BlockSpec((ts, L), lambda i: (i, blk)), pl.BlockSpec((1, L), lambda i: (0, 0))],
        out_specs=pl.BlockSpec((ts, L), lambda i: (i, 0)),
        out_shape=jax.ShapeDtypeStruct((S, L), BF16),
        compiler_params=_cparams(("parallel",)),
    )(proj, g)


def _rope_partner(x, period, start):
    w = x.shape[-1]
    lane = lax.broadcasted_iota(jnp.int32, x.shape, x.ndim - 1) % period
    first = (lane >= start) & (lane < start + QK_ROPE // 2)
    from_right = pltpu.roll(x, w - QK_ROPE // 2, axis=x.ndim - 1)
    from_left = pltpu.roll(x, QK_ROPE // 2, axis=x.ndim - 1)
    return jnp.where(first, -from_right, from_left)


def _qk_prep(q, kv, proj, kr_blk, cos_q, sin_q, cos_k, sin_k, H, ts):
    S = q.shape[0]
    pair = 2 * QK_CAT
    kv_w = QK_NOPE + V_HEAD

    def body(q_ref, kv_ref, kr_ref, cq_ref, sq_ref, ck_ref, sk_ref, qc_ref, kc_ref, vh_ref):
        kr = kr_ref[...].astype(F32)
        kr = kr * ck_ref[...] + _rope_partner(kr, QK_ROPE, 0) * sk_ref[...]
        kr = kr[:, :QK_ROPE].astype(BF16)
        for p in range(H // 2):
            x = q_ref[:, p * pair:(p + 1) * pair].astype(F32)
            x = x * cq_ref[...] + _rope_partner(x, QK_CAT, QK_NOPE) * sq_ref[...]
            qc_ref[2 * p] = x[:, :QK_CAT].astype(BF16)
            qc_ref[2 * p + 1] = x[:, QK_CAT:].astype(BF16)
        for h in range(H):
            kc_ref[h, :, 0:QK_NOPE] = kv_ref[:, h * kv_w:h * kv_w + QK_NOPE].astype(BF16)
            kc_ref[h, :, QK_NOPE:QK_CAT] = kr
            vh_ref[h, :, :] = kv_ref[:, h * kv_w + QK_NOPE:(h + 1) * kv_w].astype(BF16)

    row = lambda w: pl.BlockSpec((ts, w), lambda i: (i, 0))
    return pl.pallas_call(
        body, name="qk_prep", grid=(S // ts,),
        in_specs=[row(H * QK_CAT), row(H * kv_w),
                  pl.BlockSpec((ts, COL_BLOCK), lambda i: (i, kr_blk)),
                  row(pair), row(pair), row(COL_BLOCK), row(COL_BLOCK)],
        out_specs=[pl.BlockSpec((H, ts, QK_CAT), lambda i: (0, i, 0)),
                   pl.BlockSpec((H, ts, QK_CAT), lambda i: (0, i, 0)),
                   pl.BlockSpec((H, ts, V_HEAD), lambda i: (0, i, 0))],
        out_shape=[jax.ShapeDtypeStruct((H, S, QK_CAT), BF16), jax.ShapeDtypeStruct((H, S, QK_CAT), BF16),
                   jax.ShapeDtypeStruct((H, S, V_HEAD), BF16)],
        compiler_params=_cparams(("parallel",)),
    )(q, kv, proj, cos_q, sin_q, cos_k, sin_k)


NT_DIMS = (((1,), (1,)), ((), ()))
TN_DIMS = (((0,), (0,)), ((), ()))


def _diag_mask(T):
    rows = lax.broadcasted_iota(jnp.int32, (T, T), 0) // CHUNK
    cols = lax.broadcasted_iota(jnp.int32, (T, T), 1) // CHUNK
    return cols <= rows


def _attn_fwd(qc, kc, vh, T):
    H, S, _ = qc.shape
    n = S // T

    def body(q_ref, k_ref, v_ref, o_ref, lse_ref):
        q = q_ref[0]

        def block(i):
            L = (i + 1) * T
            s_old = lax.dot_general(q, k_ref[0, 0:i * T, :], NT_DIMS, preferred_element_type=F32) if i else None
            s_diag = lax.dot_general(q, k_ref[0, i * T:L, :], NT_DIMS, preferred_element_type=F32)
            s_diag = jnp.where(_diag_mask(T), s_diag, NEG_INF)
            m = jnp.max(s_diag, axis=-1, keepdims=True)
            if i:
                m = jnp.maximum(m, jnp.max(s_old, axis=-1, keepdims=True))
            p_diag = jnp.exp((s_diag - m) * ATTN_SCALE)
            l = jnp.sum(p_diag, axis=-1, keepdims=True)
            acc = jnp.dot(p_diag.astype(BF16), v_ref[0, i * T:L, :], preferred_element_type=F32)
            if i:
                p_old = jnp.exp((s_old - m) * ATTN_SCALE)
                l = l + jnp.sum(p_old, axis=-1, keepdims=True)
                acc = acc + jnp.dot(p_old.astype(BF16), v_ref[0, 0:i * T, :], preferred_element_type=F32)
            o_ref[...] = (acc / l).astype(o_ref.dtype)
            lse_ref[0] = m * ATTN_SCALE + jnp.log(l)

        for i in range(n):
            pl.when(pl.program_id(1) == i)(functools.partial(block, i))

    return pl.pallas_call(
        body, name="attn_fwd", grid=(H, n),
        in_specs=[pl.BlockSpec((1, T, QK_CAT), lambda h, i: (h, i, 0)),
                  pl.BlockSpec((1, S, QK_CAT), lambda h, i: (h, 0, 0)),
                  pl.BlockSpec((1, S, V_HEAD), lambda h, i: (h, 0, 0))],
        out_specs=[pl.BlockSpec((T, V_HEAD), lambda h, i: (i, h)),
                   pl.BlockSpec((1, T, 1), lambda h, i: (h, i, 0))],
        out_shape=[jax.ShapeDtypeStruct((S, H * V_HEAD), BF16), jax.ShapeDtypeStruct((H, S, 1), F32)],
        compiler_params=_cparams(("parallel", "arbitrary")),
    )(qc, kc, vh)


def _shift_rows(z, k):
    if k == 0:
        return z
    n = z.shape[0]
    row = lax.broadcasted_iota(jnp.int32, z.shape, 0)
    if k > 0:
        return jnp.where(row >= k, pltpu.roll(z, k, axis=0), 0.0)
    return jnp.where(row < n + k, pltpu.roll(z, n + k, axis=0), 0.0)


def _conv_fwd(proj, w_conv, blk_b, blk_c, blk_x):
    S = proj.shape[0]
    D = w_conv.shape[1]
    nb = D // COL_BLOCK

    def body(cb_ref, cc_ref, cx_ref, w_ref, o_ref):
        z = cc_ref[...].astype(F32) * cx_ref[...].astype(F32)
        conv = w_ref[2:3, :] * z + w_ref[1:2, :] * _shift_rows(z, 1) + w_ref[0:1, :] * _shift_rows(z, 2)
        o_ref[...] = (cb_ref[...].astype(F32) * conv).astype(BF16)

    col = lambda off: pl.BlockSpec((S, COL_BLOCK), lambda j: (0, off + j))
    return pl.pallas_call(
        body, name="conv_fwd", grid=(nb,),
        in_specs=[col(blk_b), col(blk_c), col(blk_x), pl.BlockSpec((CONV_K, COL_BLOCK), lambda j: (0, j))],
        out_specs=pl.BlockSpec((S, COL_BLOCK), lambda j: (0, j)),
        out_shape=jax.ShapeDtypeStruct((S, D), BF16),
        compiler_params=_cparams(("parallel",)),
    )(proj, proj, proj, w_conv)


def _merge_fwd(proj, ya, yb, blk_ga, blk_gb, ts):
    S, D = ya.shape
    nb = D // COL_BLOCK

    def body(ga_ref, gb_ref, ya_ref, yb_ref, o_ref):
        sa, sb = _sigmoid(ga_ref[...].astype(F32)), _sigmoid(gb_ref[...].astype(F32))
        o_ref[...] = (sa * ya_ref[...].astype(F32) + sb * yb_ref[...].astype(F32)).astype(BF16)

    row = pl.BlockSpec((ts, D), lambda i: (i, 0))
    seg = lambda blk: pl.BlockSpec((pl.Element(ts), pl.Element(D)), lambda i: (i * ts, blk * COL_BLOCK))
    return pl.pallas_call(
        body, name="merge_fwd", grid=(S // ts,),
        in_specs=[seg(blk_ga), seg(blk_gb), row, row],
        out_specs=row,
        out_shape=jax.ShapeDtypeStruct((S, D), BF16),
        compiler_params=_cparams(("parallel",)),
    )(proj, proj, ya, yb)


def _ln1_fwd(x, mix, mod, g, b, ts):
    S, D = x.shape

    def body(x_ref, mix_ref, mod_ref, g_ref, b_ref, xhat_ref, rstd_ref, u2_ref):
        r = DEEPNORM_ALPHA * x_ref[...] + mod_ref[2:3, :] * mix_ref[...]
        mu = jnp.mean(r, axis=-1, keepdims=True)
        d = r - mu
        rstd = lax.rsqrt(jnp.mean(d * d, axis=-1, keepdims=True) + LN_EPS)
        xhat = d * rstd
        xhat_ref[...] = xhat
        rstd_ref[...] = rstd
        x1 = xhat * g_ref[...] + b_ref[...]
        u2_ref[...] = (x1 * (1.0 + mod_ref[4:5, :]) + mod_ref[3:4, :]).astype(BF16)

    row = pl.BlockSpec((ts, D), lambda i: (i, 0))
    vec = lambda r: pl.BlockSpec((r, D), lambda i: (0, 0))
    return pl.pallas_call(
        body, name="ln1_fwd", grid=(S // ts,),
        in_specs=[row, row, vec(6), vec(1), vec(1)],
        out_specs=[row, pl.BlockSpec((ts, 1), lambda i: (i, 0)), row],
        out_shape=[jax.ShapeDtypeStruct((S, D), F32), jax.ShapeDtypeStruct((S, 1), F32),
                   jax.ShapeDtypeStruct((S, D), BF16)],
        compiler_params=_cparams(("parallel",)),
    )(x, mix, mod, g, b)


def _swiglu_fwd(h, ts, tb):
    S, F2 = h.shape
    F = F2 // 2
    nb = F // tb

    def body(hg_ref, hu_ref, a_ref):
        hg = hg_ref[...].astype(F32)
        a_ref[...] = (hg * _sigmoid(hg) * hu_ref[...].astype(F32)).astype(BF16)

    return pl.pallas_call(
        body, name="swiglu_fwd", grid=(S // ts, nb),
        in_specs=[pl.BlockSpec((ts, tb), lambda i, j: (i, j)), pl.BlockSpec((ts, tb), lambda i, j: (i, j + nb))],
        out_specs=pl.BlockSpec((ts, tb), lambda i, j: (i, j)),
        out_shape=jax.ShapeDtypeStruct((S, F), BF16),
        compiler_params=_cparams(("parallel", "parallel")),
    )(h, h)


def _ln2_loss(xhat1, ffn, tgt, mod, g1, b1, g2, b2, ts):
    S, D = xhat1.shape

    def body(xh_ref, ffn_ref, t_ref, mod_ref, g1_ref, b1_ref, g2_ref, b2_ref, loss_ref, dffn_ref, dx1_ref, vec_ref):
        i = pl.program_id(0)

        @pl.when(i == 0)
        def _():
            loss_ref[...] = jnp.zeros_like(loss_ref)
            vec_ref[...] = jnp.zeros_like(vec_ref)

        x1 = xh_ref[...] * g1_ref[...] + b1_ref[...]
        ffn = ffn_ref[...]
        r = DEEPNORM_ALPHA * x1 + mod_ref[5:6, :] * ffn
        mu = jnp.mean(r, axis=-1, keepdims=True)
        d = r - mu
        rstd = lax.rsqrt(jnp.mean(d * d, axis=-1, keepdims=True) + LN_EPS)
        xhat = d * rstd
        e = xhat * g2_ref[...] + b2_ref[...] - t_ref[...]
        loss_ref[...] += 0.5 * jnp.sum(jnp.mean(e * e, axis=-1, keepdims=True))
        dy = e * (1.0 / D)
        dxhat = dy * g2_ref[...]
        dr = rstd * (dxhat - jnp.mean(dxhat, axis=-1, keepdims=True)
                     - xhat * jnp.mean(dxhat * xhat, axis=-1, keepdims=True))
        dffn_ref[...] = (dr * mod_ref[5:6, :]).astype(BF16)
        dx1_ref[...] = DEEPNORM_ALPHA * dr
        vec_ref[0:1, :] += jnp.sum(dy * xhat, axis=0, keepdims=True)
        vec_ref[1:2, :] += jnp.sum(dy, axis=0, keepdims=True)
        vec_ref[2:3, :] += jnp.sum(dr * ffn, axis=0, keepdims=True)

    row = pl.BlockSpec((ts, D), lambda i: (i, 0))
    vec = lambda r: pl.BlockSpec((r, D), lambda i: (0, 0))
    return pl.pallas_call(
        body, name="ln2_loss", grid=(S // ts,),
        in_specs=[row, row, row, vec(6), vec(1), vec(1), vec(1), vec(1)],
        out_specs=[pl.BlockSpec((1, LANE), lambda i: (0, 0)), row, row, vec(8)],
        out_shape=[jax.ShapeDtypeStruct((1, LANE), F32), jax.ShapeDtypeStruct((S, D), BF16),
                   jax.ShapeDtypeStruct((S, D), F32), jax.ShapeDtypeStruct((8, D), F32)],
        compiler_params=_cparams(("arbitrary",)),
    )(xhat1, ffn, tgt, mod, g1, b1, g2, b2)


def _swiglu_bwd(da, h, ts, tb):
    S, F2 = h.shape
    nb = (F2 // 2) // tb

    def body(da_ref, hg_ref, hu_ref, dh_ref):
        hg, da = hg_ref[...].astype(F32), da_ref[...].astype(F32)
        sg = _sigmoid(hg)

        @pl.when(pl.program_id(2) == 0)
        def _():
            dh_ref[...] = (da * hu_ref[...].astype(F32) * (sg * (1.0 + hg * (1.0 - sg)))).astype(BF16)

        @pl.when(pl.program_id(2) == 1)
        def _():
            dh_ref[...] = (da * hg * sg).astype(BF16)

    lo = pl.BlockSpec((ts, tb), lambda i, j, k: (i, j))
    hi = pl.BlockSpec((ts, tb), lambda i, j, k: (i, j + nb))
    return pl.pallas_call(
        body, name="swiglu_bwd", grid=(S // ts, nb, 2),
        in_specs=[lo, lo, hi],
        out_specs=pl.BlockSpec((ts, tb), lambda i, j, k: (i, j + nb * k)),
        out_shape=jax.ShapeDtypeStruct((S, F2), BF16),
        compiler_params=_cparams(("parallel", "parallel", "arbitrary")),
    )(da, h, h)


def _ln1_bwd(du2, dx1a, xhat1, rstd1, mix, mod, g1, b1, ts):
    S, D = xhat1.shape

    def body(du2_ref, dx1a_ref, xh_ref, rstd_ref, mix_ref, mod_ref, g_ref, b_ref, dxa_ref, dmix_ref, vec_ref):
        i = pl.program_id(0)

        @pl.when(i == 0)
        def _():
            vec_ref[...] = jnp.zeros_like(vec_ref)

        xhat, du2, mix = xh_ref[...], du2_ref[...], mix_ref[...]
        x1 = xhat * g_ref[...] + b_ref[...]
        dx1 = dx1a_ref[...] + du2 * (1.0 + mod_ref[4:5, :])
        dxhat = dx1 * g_ref[...]
        dr = rstd_ref[...] * (dxhat - jnp.mean(dxhat, axis=-1, keepdims=True)
                              - xhat * jnp.mean(dxhat * xhat, axis=-1, keepdims=True))
        dxa_ref[...] = DEEPNORM_ALPHA * dr
        dmix_ref[...] = (dr * mod_ref[2:3, :]).astype(BF16)
        vec_ref[0:1, :] += jnp.sum(du2, axis=0, keepdims=True)
        vec_ref[1:2, :] += jnp.sum(du2 * x1, axis=0, keepdims=True)
        vec_ref[2:3, :] += jnp.sum(dx1 * xhat, axis=0, keepdims=True)
        vec_ref[3:4, :] += jnp.sum(dx1, axis=0, keepdims=True)
        vec_ref[4:5, :] += jnp.sum(dr * mix, axis=0, keepdims=True)

    row = pl.BlockSpec((ts, D), lambda i: (i, 0))
    vec = lambda r: pl.BlockSpec((r, D), lambda i: (0, 0))
    return pl.pallas_call(
        body, name="ln1_bwd", grid=(S // ts,),
        in_specs=[row, row, row, pl.BlockSpec((ts, 1), lambda i: (i, 0)), row, vec(6), vec(1), vec(1)],
        out_specs=[row, row, vec(8)],
        out_shape=[jax.ShapeDtypeStruct((S, D), F32), jax.ShapeDtypeStruct((S, D), BF16),
                   jax.ShapeDtypeStruct((8, D), F32)],
        compiler_params=_cparams(("arbitrary",)),
    )(du2, dx1a, xhat1, rstd1, mix, mod, g1, b1)


def _merge_bwd(dmerged, proj, ya, yb, blk_ga, blk_gb, ts):
    S, D = ya.shape
    nb = D // COL_BLOCK

    def body(dm_ref, ga_ref, gb_ref, ya_ref, yb_ref, dya_ref, dyb_ref, dga_ref, dgb_ref):
        dm = dm_ref[...].astype(F32)
        sa, sb = _sigmoid(ga_ref[...].astype(F32)), _sigmoid(gb_ref[...].astype(F32))
        dya_ref[...] = (dm * sa).astype(BF16)
        dyb_ref[...] = (dm * sb).astype(BF16)
        dga_ref[...] = (dm * ya_ref[...].astype(F32) * sa * (1.0 - sa)).astype(BF16)
        dgb_ref[...] = (dm * yb_ref[...].astype(F32) * sb * (1.0 - sb)).astype(BF16)

    row = pl.BlockSpec((ts, D), lambda i: (i, 0))
    seg = lambda blk: pl.BlockSpec((pl.Element(ts), pl.Element(D)), lambda i: (i * ts, blk * COL_BLOCK))
    out = jax.ShapeDtypeStruct((S, D), BF16)
    return pl.pallas_call(
        body, name="merge_bwd", grid=(S // ts,),
        in_specs=[row, seg(blk_ga), seg(blk_gb), row, row],
        out_specs=[row] * 4,
        out_shape=[out] * 4,
        compiler_params=_cparams(("parallel",)),
    )(dmerged, proj, proj, ya, yb)


def _conv_bwd(dcbc, proj, w_conv, blk_b, blk_c, blk_x):
    S = proj.shape[0]
    D = w_conv.shape[1]
    nb = D // COL_BLOCK

    def body(d_ref, cb_ref, cc_ref, cx_ref, w_ref, dcb_ref, dcc_ref, dcx_ref, dw_ref):
        d, cc, cx = d_ref[...].astype(F32), cc_ref[...].astype(F32), cx_ref[...].astype(F32)
        z = cc * cx
        z1, z2 = _shift_rows(z, 1), _shift_rows(z, 2)
        conv = w_ref[2:3, :] * z + w_ref[1:2, :] * z1 + w_ref[0:1, :] * z2
        dcb_ref[...] = (d * conv).astype(BF16)
        dconv = d * cb_ref[...].astype(F32)
        dz = w_ref[2:3, :] * dconv + w_ref[1:2, :] * _shift_rows(dconv, -1) + w_ref[0:1, :] * _shift_rows(dconv, -2)
        dcc_ref[...] = (dz * cx).astype(BF16)
        dcx_ref[...] = (dz * cc).astype(BF16)
        dw_ref[...] = jnp.zeros_like(dw_ref)
        dw_ref[0:1, :] = jnp.sum(dconv * z2, axis=0, keepdims=True)
        dw_ref[1:2, :] = jnp.sum(dconv * z1, axis=0, keepdims=True)
        dw_ref[2:3, :] = jnp.sum(dconv * z, axis=0, keepdims=True)

    col = lambda off: pl.BlockSpec((S, COL_BLOCK), lambda j: (0, off + j))
    out = jax.ShapeDtypeStruct((S, D), BF16)
    return pl.pallas_call(
        body, name="conv_bwd", grid=(nb,),
        in_specs=[col(0), col(blk_b), col(blk_c), col(blk_x), pl.BlockSpec((CONV_K, COL_BLOCK), lambda j: (0, j))],
        out_specs=[col(0), col(0), col(0), pl.BlockSpec((8, COL_BLOCK), lambda j: (0, j))],
        out_shape=[out, out, out, jax.ShapeDtypeStruct((8, D), F32)],
        compiler_params=_cparams(("parallel",)),
    )(dcbc, proj, proj, proj, w_conv)


def _attn_bwd(qc, kc, vh, do, o, lse, T):
    H, S, _ = qc.shape
    n = S // T

    def body(q_ref, k_ref, v_ref, do_ref, o_ref, lse_ref, dq_ref, dk_ref, dv_ref, d_ref, dq_acc, dk_acc, dv_acc):
        j = pl.program_id(1)

        @pl.when(j == 0)
        def _():
            dq_acc[...] = jnp.zeros_like(dq_acc)
            d_ref[...] = jnp.sum(do_ref[...].astype(F32) * o_ref[...].astype(F32), axis=-1, keepdims=True)

        dk_acc[...] = jnp.zeros_like(dk_acc)
        dv_acc[...] = jnp.zeros_like(dv_acc)
        k, v = k_ref[0], v_ref[0]

        def step(i, masked):
            rows = pl.ds(pl.multiple_of(i * T, T), T)
            q = q_ref[0, rows, :]
            do = do_ref[rows, :].astype(BF16)
            s = lax.dot_general(q, k, NT_DIMS, preferred_element_type=F32) * ATTN_SCALE
            if masked:
                s = jnp.where(_diag_mask(T), s, NEG_INF)
            p = jnp.exp(s - lse_ref[0, rows, :])
            dv_acc[...] += lax.dot_general(p.astype(BF16), do, TN_DIMS, preferred_element_type=F32)
            dp = lax.dot_general(do, v, NT_DIMS, preferred_element_type=F32)
            ds = (p * (dp - d_ref[rows, :]) * ATTN_SCALE).astype(BF16)
            dk_acc[...] += lax.dot_general(ds, q, TN_DIMS, preferred_element_type=F32)
            dq_acc[rows, :] += jnp.dot(ds, k, preferred_element_type=F32)

        def above(i, carry):
            step(i, False)
            return carry

        step(j, True)
        lax.fori_loop(j + 1, n, above, 0)
        dk_ref[0] = dk_acc[...].astype(BF16)
        dv_ref[0] = dv_acc[...].astype(BF16)

        @pl.when(j == n - 1)
        def _():
            dq_ref[0] = dq_acc[...].astype(BF16)

    head = lambda w: pl.BlockSpec((1, S, w), lambda h, j: (h, 0, 0))
    blk = lambda w: pl.BlockSpec((1, T, w), lambda h, j: (h, j, 0))
    ospec = pl.BlockSpec((S, V_HEAD), lambda h, j: (0, h))
    return pl.pallas_call(
        body, name="attn_bwd", grid=(H, n),
        in_specs=[head(QK_CAT), blk(QK_CAT), blk(V_HEAD), ospec, ospec, head(1)],
        out_specs=[head(QK_CAT), blk(QK_CAT), blk(V_HEAD)],
        out_shape=[jax.ShapeDtypeStruct((H, S, QK_CAT), BF16), jax.ShapeDtypeStruct((H, S, QK_CAT), BF16),
                   jax.ShapeDtypeStruct((H, S, V_HEAD), BF16)],
        scratch_shapes=[pltpu.VMEM((S, 1), F32), pltpu.VMEM((S, QK_CAT), F32), pltpu.VMEM((T, QK_CAT), F32),
                        pltpu.VMEM((T, V_HEAD), F32)],
        compiler_params=_cparams(("parallel", "arbitrary")),
    )(qc, kc, vh, do, o, lse)


def _qk_bwd(dqc, dkc, dvh, cos_q, sin_q, cos_k, sin_k, ts):
    H, S, _ = dqc.shape
    pair = 2 * QK_CAT
    kv_w = QK_NOPE + V_HEAD

    def body(dqc_ref, dkc_ref, dvh_ref, cq_ref, sq_ref, ck_ref, sk_ref, dq_ref, dkv_ref, dkr_ref, q_buf, kr_buf):
        for p in range(H // 2):
            q_buf[:, :QK_CAT] = dqc_ref[2 * p].astype(F32)
            q_buf[:, QK_CAT:] = dqc_ref[2 * p + 1].astype(F32)
            g = q_buf[...]
            dq_ref[:, p * pair:(p + 1) * pair] = (
                g * cq_ref[...] - _rope_partner(g, QK_CAT, QK_NOPE) * sq_ref[...]).astype(BF16)
        kr_sum = jnp.zeros((ts, QK_ROPE), F32)
        for h in range(H):
            dkv_ref[:, h * kv_w:h * kv_w + QK_NOPE] = dkc_ref[h, :, 0:QK_NOPE].astype(BF16)
            dkv_ref[:, h * kv_w + QK_NOPE:(h + 1) * kv_w] = dvh_ref[h].astype(BF16)
            kr_sum = kr_sum + dkc_ref[h, :, QK_NOPE:QK_CAT]
        kr_buf[...] = jnp.zeros_like(kr_buf)
        kr_buf[:, 0:QK_ROPE] = kr_sum
        kr = kr_buf[...]
        dkr_ref[...] = (kr * ck_ref[...] - _rope_partner(kr, QK_ROPE, 0) * sk_ref[...]).astype(BF16)

    row = lambda w: pl.BlockSpec((ts, w), lambda i: (i, 0))
    head = lambda w: pl.BlockSpec((H, ts, w), lambda i: (0, i, 0))
    return pl.pallas_call(
        body, name="qk_bwd", grid=(S // ts,),
        in_specs=[head(QK_CAT), head(QK_CAT), head(V_HEAD), row(pair), row(pair), row(COL_BLOCK), row(COL_BLOCK)],
        out_specs=[row(H * QK_CAT), row(H * kv_w), row(COL_BLOCK)],
        out_shape=[jax.ShapeDtypeStruct((S, H * QK_CAT), BF16), jax.ShapeDtypeStruct((S, H * kv_w), BF16),
                   jax.ShapeDtypeStruct((S, COL_BLOCK), BF16)],
        scratch_shapes=[pltpu.VMEM((ts, pair), F32), pltpu.VMEM((ts, COL_BLOCK), F32)],
        compiler_params=_cparams(("parallel",)),
    )(dqc, dkc, dvh, cos_q, sin_q, cos_k, sin_k)


def _rms_bwd(dy, proj, g, blk, L, ts, name):
    S = proj.shape[0]

    def body(dy_ref, a_ref, g_ref, da_ref, dg_ref):
        i = pl.program_id(0)

        @pl.when(i == 0)
        def _():
            dg_ref[...] = jnp.zeros_like(dg_ref)

        a, dy = a_ref[...].astype(F32), dy_ref[...]
        r = lax.rsqrt(jnp.mean(a * a, axis=-1, keepdims=True) + RMS_EPS)
        dyh = dy * g_ref[...]
        da = r * dyh - a * (r * r * r) * jnp.mean(dyh * a, axis=-1, keepdims=True)
        da_ref[...] = da.astype(BF16)
        dg_ref[0:1, :] += jnp.sum(dy * a * r, axis=0, keepdims=True)

    return pl.pallas_call(
        body, name=name, grid=(S // ts,),
        in_specs=[pl.BlockSpec((ts, L), lambda i: (i, 0)), pl.BlockSpec((ts, L), lambda i: (i, blk)),
                  pl.BlockSpec((1, L), lambda i: (0, 0))],
        out_specs=[pl.BlockSpec((ts, L), lambda i: (i, 0)), pl.BlockSpec((8, L), lambda i: (0, 0))],
        out_shape=[jax.ShapeDtypeStruct((S, L), BF16), jax.ShapeDtypeStruct((8, L), F32)],
        compiler_params=_cparams(("arbitrary",)),
    )(dy, proj, g)


def _grad_x(du, dxa, x, mod, ts):
    S, D = x.shape

    def body(du_ref, dxa_ref, x_ref, mod_ref, dx_ref, vec_ref):
        i = pl.program_id(0)

        @pl.when(i == 0)
        def _():
            vec_ref[...] = jnp.zeros_like(vec_ref)

        du = du_ref[...]
        dx_ref[...] = dxa_ref[...] + du * (1.0 + mod_ref[1:2, :])
        vec_ref[0:1, :] += jnp.sum(du, axis=0, keepdims=True)
        vec_ref[1:2, :] += jnp.sum(du * x_ref[...], axis=0, keepdims=True)

    row = pl.BlockSpec((ts, D), lambda i: (i, 0))
    vec = lambda r: pl.BlockSpec((r, D), lambda i: (0, 0))
    return pl.pallas_call(
        body, name="grad_x", grid=(S // ts,),
        in_specs=[row, row, row, vec(6)],
        out_specs=[row, vec(8)],
        out_shape=[jax.ShapeDtypeStruct((S, D), F32), jax.ShapeDtypeStruct((8, D), F32)],
        compiler_params=_cparams(("arbitrary",)),
    )(du, dxa, x, mod)


def _adamw(w, g, m, v, name):
    R, C = w.shape
    tr = _tile(R, max(8, (1 << 19) // C), 8)
    c1 = 1.0 / (1.0 - ADAM_B1 ** ADAM_STEP)
    c2 = 1.0 / (1.0 - ADAM_B2 ** ADAM_STEP)

    def body(w_ref, g_ref, m_ref, v_ref, d_ref, nm_ref, nv_ref):
        g = g_ref[...]
        m = ADAM_B1 * m_ref[...] + (1.0 - ADAM_B1) * g
        v = ADAM_B2 * v_ref[...] + (1.0 - ADAM_B2) * (g * g)
        nm_ref[...] = m
        nv_ref[...] = v
        d_ref[...] = -ADAM_LR * ((m * c1) / (jnp.sqrt(v * c2) + ADAM_EPS) + ADAM_WD * w_ref[...])

    spec = pl.BlockSpec((tr, C), lambda i: (i, 0))
    out = jax.ShapeDtypeStruct((R, C), F32)
    return pl.pallas_call(
        body, name=name, grid=(R // tr,),
        in_specs=[spec] * 4, out_specs=[spec] * 3, out_shape=[out] * 3,
        compiler_params=_cparams(("parallel",)),
    )(w, g, m, v)


def _adamw_ada(w, cact_t, dmod, m, v):
    R, C = w.shape
    tr = _tile(R, max(8, (1 << 18) // C), 8)
    c1 = 1.0 / (1.0 - ADAM_B1 ** ADAM_STEP)
    c2 = 1.0 / (1.0 - ADAM_B2 ** ADAM_STEP)

    def body(w_ref, ct_ref, dm_ref, m_ref, v_ref, g_ref, d_ref, nm_ref, nv_ref):
        ct = ct_ref[...].astype(BF16).astype(F32)
        dm = dm_ref[...].astype(BF16).astype(F32)
        g = ct[:, 0:1] * dm[0:1, :]
        for b in range(1, N_DEV):
            g = g + ct[:, b:b + 1] * dm[b:b + 1, :]
        m = ADAM_B1 * m_ref[...] + (1.0 - ADAM_B1) * g
        v = ADAM_B2 * v_ref[...] + (1.0 - ADAM_B2) * (g * g)
        g_ref[...] = g
        nm_ref[...] = m
        nv_ref[...] = v
        d_ref[...] = -ADAM_LR * ((m * c1) / (jnp.sqrt(v * c2) + ADAM_EPS) + ADAM_WD * w_ref[...])

    spec = pl.BlockSpec((tr, C), lambda i: (i, 0))
    out = jax.ShapeDtypeStruct((R, C), F32)
    return pl.pallas_call(
        body, name="adamw_w_ada", grid=(R // tr,),
        in_specs=[spec, pl.BlockSpec((tr, N_DEV), lambda i: (i, 0)), pl.BlockSpec((N_DEV, C), lambda i: (0, 0)),
                  spec, spec],
        out_specs=[spec] * 4, out_shape=[out] * 4,
        compiler_params=_cparams(("parallel",)),
    )(w, cact_t, dmod, m, v)


def _adamw_reduced(w, own, got, m, v, my_chip, name):
    R, C = w.shape
    tr = _tile(R, max(PACK_ROW_ALIGN, (1 << 18) // C), PACK_ROW_ALIGN)
    c1 = 1.0 / (1.0 - ADAM_B1 ** ADAM_STEP)
    c2 = 1.0 / (1.0 - ADAM_B2 ** ADAM_STEP)

    def body(chip_ref, w_ref, own_ref, g1_ref, g2_ref, g3_ref, m_ref, v_ref, g_ref, d_ref, nm_ref, nv_ref):
        g = own_ref[0].astype(F32) + g1_ref[0].astype(F32) + g2_ref[0].astype(F32) + g3_ref[0].astype(F32)
        m = ADAM_B1 * m_ref[...] + (1.0 - ADAM_B1) * g
        v = ADAM_B2 * v_ref[...] + (1.0 - ADAM_B2) * (g * g)
        g_ref[...] = g
        nm_ref[...] = m
        nv_ref[...] = v
        d_ref[...] = -ADAM_LR * ((m * c1) / (jnp.sqrt(v * c2) + ADAM_EPS) + ADAM_WD * w_ref[...])

    spec = pl.BlockSpec((tr, C), lambda i, chip: (i, 0))
    slot = lambda k: pl.BlockSpec((1, tr, C), lambda i, chip: (chip[0] ^ k, i, 0))
    out = jax.ShapeDtypeStruct((R, C), F32)
    return pl.pallas_call(
        body, name=name,
        grid_spec=pltpu.PrefetchScalarGridSpec(
            num_scalar_prefetch=1, grid=(R // tr,),
            in_specs=[spec, slot(0), slot(1), slot(2), slot(3), spec, spec],
            out_specs=[spec] * 4),
        out_shape=[out] * 4,
        compiler_params=_cparams(("parallel",)),
    )(my_chip, w, own, got, got, got, m, v)


def _my_place():
    return lax.axis_index("x"), lax.axis_index("y"), lax.axis_index("c")


def _peer(k):
    x, y, c = _my_place()
    return (x ^ ((k >> 2) & 1), y ^ ((k >> 1) & 1), c ^ (k & 1))


def _linear(place):
    return 4 * place[0] + 2 * place[1] + place[2]


def _ada_fwd(c_row, wconv_row, w_ada, b_row):
    D, CW = w_ada.shape
    WC = wconv_row.shape[-1]

    def body(c_ref, wc_ref, w_ref, b_ref, mod_ref, cact_ref, wcall_ref, send_buf, sems):
        me = _linear(_my_place())
        c = c_ref[0]
        cact_ref[me] = c * _sigmoid(c)
        wcall_ref[me] = wc_ref[0]

        def gather_copy(buf, k, grp):
            return pltpu.make_async_remote_copy(
                src_ref=buf.at[me], dst_ref=buf.at[me], send_sem=sems.at[0, grp, k], recv_sem=sems.at[1, grp, k],
                device_id=_peer(k), device_id_type=MESH_ID)

        def gather_recv(buf, k, grp):
            src = _linear(_peer(k))
            return pltpu.make_async_remote_copy(
                src_ref=buf.at[src], dst_ref=buf.at[src], send_sem=sems.at[0, grp, k], recv_sem=sems.at[1, grp, k],
                device_id=_peer(k), device_id_type=MESH_ID)

        for k in range(1, N_DEV):
            gather_copy(cact_ref, k, 0).start()
            gather_copy(wcall_ref, k, 1).start()
        for k in range(1, N_DEV):
            gather_recv(cact_ref, k, 0).wait_recv()
            gather_recv(wcall_ref, k, 1).wait_recv()
        for k in range(1, N_DEV):
            gather_copy(cact_ref, k, 0).wait_send()
            gather_copy(wcall_ref, k, 1).wait_send()

        cact = jnp.concatenate([cact_ref[b] for b in range(N_DEV)], axis=0)
        mod_all = jnp.dot(cact.astype(BF16), w_ref[...].astype(BF16), preferred_element_type=F32) + b_ref[0]
        for b in range(N_DEV):
            send_buf[b] = mod_all[b:b + 1, :]
        mod_ref[me] = send_buf[me]

        def scatter_copy(k):
            dst = _linear(_peer(k))
            return pltpu.make_async_remote_copy(
                src_ref=send_buf.at[dst], dst_ref=mod_ref.at[me], send_sem=sems.at[0, 2, k], recv_sem=sems.at[1, 2, k],
                device_id=_peer(k), device_id_type=MESH_ID)

        def scatter_recv(k):
            src = _linear(_peer(k))
            return pltpu.make_async_remote_copy(
                src_ref=send_buf.at[src], dst_ref=mod_ref.at[src], send_sem=sems.at[0, 2, k], recv_sem=sems.at[1, 2, k],
                device_id=_peer(k), device_id_type=MESH_ID)

        for k in range(1, N_DEV):
            scatter_copy(k).start()
        for k in range(1, N_DEV):
            scatter_recv(k).wait_recv()
        for k in range(1, N_DEV):
            scatter_copy(k).wait_send()

    vmem = pl.BlockSpec(memory_space=pltpu.VMEM)
    return pl.pallas_call(
        body, name="ada_fwd",
        in_specs=[vmem] * 4, out_specs=[vmem] * 3,
        out_shape=[jax.ShapeDtypeStruct((N_DEV, 1, CW), F32), jax.ShapeDtypeStruct((N_DEV, 1, D), F32),
                   jax.ShapeDtypeStruct((N_DEV, 1, WC), F32)],
        scratch_shapes=[pltpu.VMEM((N_DEV, 1, CW), F32), pltpu.SemaphoreType.DMA((2, 3, N_DEV))],
        compiler_params=pltpu.CompilerParams(vmem_limit_bytes=VMEM_LIMIT),
    )(c_row, wconv_row, w_ada, b_row)


def _ada_bwd(payload, deps=()):
    NCH, _, CW = payload.shape

    def body(p_ref, *rest):
        sum_ref, mine_ref, all_ref, sems = rest[-4:]
        me = _linear(_my_place())
        all_ref[me] = p_ref[...]

        def copy(k, slot):
            return pltpu.make_async_remote_copy(
                src_ref=all_ref.at[slot], dst_ref=all_ref.at[slot], send_sem=sems.at[0, k], recv_sem=sems.at[1, k],
                device_id=_peer(k), device_id_type=MESH_ID)

        for k in range(1, N_DEV):
            copy(k, me).start()
        for k in range(1, N_DEV):
            copy(k, _linear(_peer(k))).wait_recv()
        for k in range(1, N_DEV):
            copy(k, me).wait_send()

        total = all_ref[0]
        for b in range(1, N_DEV):
            total = total + all_ref[b]
        sum_ref[...] = total

        for b in range(N_DEV):
            mine_ref[b] = all_ref[b, me]

    vmem = pl.BlockSpec(memory_space=pltpu.VMEM)
    return pl.pallas_call(
        body, name="ada_bwd",
        in_specs=[vmem] + [ANY_SPEC] * len(deps), out_specs=[vmem, vmem],
        out_shape=[jax.ShapeDtypeStruct((NCH, 1, CW), F32), jax.ShapeDtypeStruct((N_DEV, 1, CW), F32)],
        scratch_shapes=[pltpu.VMEM((N_DEV, NCH, 1, CW), F32), pltpu.SemaphoreType.DMA((2, N_DEV))],
        compiler_params=pltpu.CompilerParams(vmem_limit_bytes=VMEM_LIMIT),
    )(payload, *deps)


def _exchange_in_chip(parts):
    W = len(parts)

    def body(*refs):
        p_refs, got_refs, (send_sems, recv_sems) = refs[:W], refs[W:2 * W], refs[2 * W:]
        x, y, c = _my_place()
        sibling = (x, y, 1 - c)
        copies = []
        for w in range(W):
            for q in range(4):
                copies.append(pltpu.make_async_remote_copy(
                    src_ref=p_refs[w].at[2 * q + (1 - c)], dst_ref=got_refs[w].at[q],
                    send_sem=send_sems.at[4 * w + q], recv_sem=recv_sems.at[4 * w + q],
                    device_id=sibling, device_id_type=MESH_ID))
        for cp in copies:
            cp.start()
        for cp in copies:
            cp.wait_recv()
        for cp in copies:
            cp.wait_send()

    return pl.pallas_call(
        body, name="grad_exchange_in_chip",
        in_specs=[HBM_SPEC] * W, out_specs=[HBM_SPEC] * W,
        out_shape=[jax.ShapeDtypeStruct((4,) + p.shape[1:], p.dtype) for p in parts],
        scratch_shapes=[pltpu.SemaphoreType.DMA((4 * W,)), pltpu.SemaphoreType.DMA((4 * W,))],
    )(*parts)


def _pair_sum(parts, got, core):
    _, R, C = parts.shape
    tr = _tile(R, max(PACK_ROW_ALIGN, PAIR_SUM_BLOCK // C), PACK_ROW_ALIGN)

    def body(c_ref, p_ref, g_ref, o_ref):
        o_ref[...] = (p_ref[...].astype(F32) + g_ref[...].astype(F32)).astype(o_ref.dtype)

    return pl.pallas_call(
        body, name="grad_pair_sum",
        grid_spec=pltpu.PrefetchScalarGridSpec(
            num_scalar_prefetch=1, grid=(4, R // tr),
            in_specs=[pl.BlockSpec((1, tr, C), lambda q, i, c_ref: (2 * q + c_ref[0], i, 0)),
                      pl.BlockSpec((1, tr, C), lambda q, i, c_ref: (q, i, 0))],
            out_specs=pl.BlockSpec((1, tr, C), lambda q, i, c_ref: (q, i, 0))),
        out_shape=jax.ShapeDtypeStruct((4, R, C), parts.dtype),
        compiler_params=_cparams(("parallel", "parallel")),
    )(core, parts, got)


HBM_SPEC = pl.BlockSpec(memory_space=pltpu.HBM)
SEM_SPEC = pl.BlockSpec(memory_space=pltpu.SEMAPHORE)
ANY_SPEC = pl.BlockSpec(memory_space=pl.ANY)
SPLIT_EFFECT = pltpu.SideEffectType.DATAFLOW_SIDE_EFFECTING


def _landing_zone(shape, dtype):
    return pltpu.with_memory_space_constraint(lax.empty(shape, dtype), pltpu.HBM)


def _split_start(name, arrays, lands, after, copies_of, per_array):
    W = len(arrays)
    after = tuple(after) if isinstance(after, (tuple, list)) else (after,)

    def body(*refs):
        x_refs, land_refs = refs[:W], refs[W:2 * W]
        send_sems, recv_sems = refs[2 * W + len(after)], refs[2 * W + len(after) + 1]
        token = refs[-1]
        k = 0
        for w in range(W):
            for src, dst, dev in copies_of(w, x_refs[w], land_refs[w]):
                pltpu.make_async_remote_copy(src_ref=src, dst_ref=dst, send_sem=send_sems.at[k], recv_sem=recv_sems.at[k],
                                             device_id=dev, device_id_type=MESH_ID).start()
                k += 1
        token[...] = jnp.zeros_like(token)

    n_copies = per_array * W
    hbm_of = lambda xs: tuple(pltpu.HBM(a.shape, a.dtype) for a in xs)
    out = pl.pallas_call(
        body, name=name,
        out_shape=(pltpu.SemaphoreType.DMA((n_copies,)), pltpu.SemaphoreType.DMA((n_copies,)))
        + hbm_of(arrays) + hbm_of(lands) + (jax.ShapeDtypeStruct((8, LANE), F32),),
        in_specs=(HBM_SPEC,) * (2 * W) + (ANY_SPEC,) * len(after),
        out_specs=(SEM_SPEC, SEM_SPEC) + (HBM_SPEC,) * (2 * W) + (pl.BlockSpec(memory_space=pltpu.VMEM),),
        input_output_aliases={i: 2 + i for i in range(2 * W)},
        compiler_params=pltpu.CompilerParams(has_side_effects=SPLIT_EFFECT),
    )(*[pltpu.with_memory_space_constraint(a, pltpu.HBM) for a in arrays], *lands, *after)
    return out[0], out[1], list(out[2:2 + W]), list(out[2 + W:2 + 2 * W]), out[-1]


def _split_wait(name, state, after, copies_of):
    send_sems, recv_sems, arrays, lands, _ = state
    W = len(arrays)
    after = tuple(after) if isinstance(after, (tuple, list)) else (after,)

    def body(*refs):
        x_refs, land_refs = refs[:W], refs[W:2 * W]
        send_sems, recv_sems = refs[2 * W], refs[2 * W + 1]
        k = 0
        for w in range(W):
            for src, dst, dev in copies_of(w, x_refs[w], land_refs[w]):
                cp = pltpu.make_async_remote_copy(src_ref=src, dst_ref=dst, send_sem=send_sems.at[k],
                                                  recv_sem=recv_sems.at[k], device_id=dev, device_id_type=MESH_ID)
                cp.wait_send()
                cp.wait_recv()
                k += 1

    out = pl.pallas_call(
        body, name=name,
        out_shape=tuple(pltpu.HBM(a.shape, a.dtype) for a in arrays + lands),
        in_specs=(HBM_SPEC,) * (2 * W) + (SEM_SPEC, SEM_SPEC) + (ANY_SPEC,) * len(after),
        out_specs=(HBM_SPEC,) * (2 * W),
        input_output_aliases={i: i for i in range(2 * W)},
        compiler_params=pltpu.CompilerParams(has_side_effects=SPLIT_EFFECT),
    )(*arrays, *lands, send_sems, recv_sems, *after)
    return list(out[:W]), list(out[W:])


def _scatter_copies(w, p_ref, land_ref):
    x, y, c = _my_place()
    my_chip = 2 * x + y
    return [(p_ref.at[2 * (x ^ (k >> 1)) + (y ^ (k & 1))], land_ref.at[my_chip], (x ^ (k >> 1), y ^ (k & 1), c))
            for k in range(1, 4)]


def _gather_copies(w, x_ref, land_ref):
    x, y, c = _my_place()
    me = _linear((x, y, c))
    devs = [(x, y, 1 - c)] + [(x ^ (k >> 1), y ^ (k & 1), c) for k in range(1, 4)]
    return [(x_ref, land_ref.at[me], d) for d in devs]


def _gather_forward(lands, name):
    W = len(lands)

    def body(*refs):
        land_refs, out_refs, (send_sems, recv_sems) = refs[:W], refs[W:2 * W], refs[2 * W:]
        x, y, c = _my_place()
        sibling = (x, y, 1 - c)
        sends, arrivals = [], []
        for w in range(W):
            for k in range(1, 4):
                px, py = x ^ (k >> 1), y ^ (k & 1)
                landed, theirs = _linear((px, py, c)), out_refs[w].at[_linear((px, py, 1 - c))]
                sem = 3 * w + k - 1
                sends.append(pltpu.make_async_remote_copy(
                    src_ref=land_refs[w].at[landed], dst_ref=out_refs[w].at[landed],
                    send_sem=send_sems.at[sem], recv_sem=recv_sems.at[sem], device_id=sibling, device_id_type=MESH_ID))
                arrivals.append(pltpu.make_async_remote_copy(
                    src_ref=theirs, dst_ref=theirs, send_sem=send_sems.at[sem], recv_sem=recv_sems.at[sem],
                    device_id=sibling, device_id_type=MESH_ID))
        for cp in sends:
            cp.start()
        for cp in arrivals:
            cp.wait_recv()
        for cp in sends:
            cp.wait_send()

    return pl.pallas_call(
        body, name=name,
        in_specs=[HBM_SPEC] * W, out_specs=[HBM_SPEC] * W,
        out_shape=[jax.ShapeDtypeStruct(l.shape, l.dtype) for l in lands],
        input_output_aliases={i: i for i in range(W)},
        scratch_shapes=[pltpu.SemaphoreType.DMA((3 * W,)), pltpu.SemaphoreType.DMA((3 * W,))],
    )(*lands)


def _with_own_slot(gathered, shard):
    return lax.dynamic_update_index_in_dim(gathered, shard[None], _linear(_my_place()), axis=0)


def _in_chip_copies(w, p_ref, land_ref):
    x, y, c = _my_place()
    return [(p_ref.at[2 * q + (1 - c)], land_ref.at[q], (x, y, 1 - c)) for q in range(4)]


def _in_chip_start(parts, tag):
    lands = [_landing_zone((4,) + p.shape[1:], p.dtype) for p in parts]
    return _split_start("grad_in_chip_start_" + tag, parts, lands, (), _in_chip_copies, 4)


def _reduce_scatter_begin(parts, tag, in_chip_state=None, after=()):
    parts, early, got = list(parts), [], []
    if in_chip_state is not None:
        early, got = _split_wait("grad_in_chip_wait_" + tag, in_chip_state, after, _in_chip_copies)
    if parts:
        got = got + list(_exchange_in_chip(parts))
    parts = early + parts
    core = lax.axis_index("c").astype(jnp.int32).reshape(1)
    chip_parts = [_pair_sum(p, g, core) for p, g in zip(parts, got)]
    lands = [_landing_zone(p.shape, p.dtype) for p in chip_parts]
    return _split_start("grad_scatter_start_" + tag, chip_parts, lands, got[0], _scatter_copies, 3)


def _reduce_scatter_end(state, after, tag):
    return _split_wait("grad_scatter_wait_" + tag, state, after, _scatter_copies)


def kernel(x, c, positions, w_ada, b_ada, w_in, g_q_a, w_q_b, g_kv_a, w_kv_b, w_o_a, w_conv, w_o_b, w_o, ln1_g, ln1_b, w_ffn_in, w_ffn_out, ln2_g, ln2_b, loss_target, m_w_ada, m_b_ada, m_w_in, m_g_q_a, m_w_q_b, m_g_kv_a, m_w_kv_b, m_w_o_a, m_w_conv, m_w_o_b, m_w_o, m_ln1_g, m_ln1_b, m_w_ffn_in, m_w_ffn_out, m_ln2_g, m_ln2_b, v_w_ada, v_b_ada, v_w_in, v_g_q_a, v_w_q_b, v_g_kv_a, v_w_kv_b, v_w_o_a, v_w_conv, v_w_o_b, v_w_o, v_ln1_g, v_ln1_b, v_w_ffn_in, v_w_ffn_out, v_ln2_g, v_ln2_b):
    x2, tgt = x[0], loss_target[0]
    S, D = x2.shape
    Lq, Lkv = g_q_a.shape[1], g_kv_a.shape[1]
    H = w_q_b.shape[2] * N_DEV // QK_CAT
    F = w_ffn_out.shape[1] * N_DEV
    assert Lq == Lkv and (Lq + Lkv) % COL_BLOCK == 0 and D % COL_BLOCK == 0
    front = Lq + Lkv + QK_ROPE
    front_pad = _round_up(front, COL_BLOCK)
    kr_blk = (Lq + Lkv) // COL_BLOCK
    blk_b = front_pad // COL_BLOCK
    nblk = D // COL_BLOCK
    blk_c, blk_x, blk_ga, blk_gb = blk_b + nblk, blk_b + 2 * nblk, blk_b + 3 * nblk, blk_b + 4 * nblk
    ts = _tile(S, 256, 8)
    T = _tile(S, min(512, S // 2), CHUNK)
    tb = _tile(F, 2816)
    me = _linear(_my_place())

    cw = w_ada.shape[2]
    b_mine = lax.dynamic_slice(b_ada, (0, me * cw), (1, cw)).reshape(1, 1, cw)
    mod_blocks, cact_all, wconv_all = _ada_fwd(c.reshape(1, 1, D), w_conv[0].reshape(1, 1, -1), w_ada[0], b_mine)
    mod = mod_blocks.reshape(6, D)
    cact_all = cact_all.reshape(N_DEV, D)
    w_conv_full = wconv_all.reshape(N_DEV, CONV_K, -1).transpose(1, 0, 2).reshape(CONV_K, D)

    landing = lambda shards: [_landing_zone((N_DEV,) + s.shape, BF16) for s in shards]
    gathered = lambda lands, shards, tag: [_with_own_slot(g, s) for g, s in
                                           zip(_gather_forward(lands, tag + "_gather_forward"), shards)]
    half = D // 2
    w_in_b = w_in[0].astype(BF16)
    first, second = [w_in_b[:half]], [w_in_b[half:], w_q_b[0].astype(BF16), w_kv_b[0].astype(BF16)]
    mid = [w[0].astype(BF16) for w in (w_o_a, w_o_b, w_o)]
    last = [w[0].astype(BF16) for w in (w_ffn_in, w_ffn_out)]
    first_state = _split_start("first_gather_start", first, landing(first), mod_blocks, _gather_copies, 4)
    second_state = _split_start("second_gather_start", second, landing(second), first_state[4], _gather_copies, 4)
    u = _modulate_in(x2, mod, ts)

    first_shards, first_lands = _split_wait("first_gather_wait", first_state, (u, second_state[4]), _gather_copies)
    (g_in_top,) = gathered(first_lands, first_shards, "first")
    w_in_top = _assemble_w_in(g_in_top, front, front_pad, D, 0)
    proj_top = _matmul(u, w_in_top, "nn", BF16, "proj_top", k_rows=(0, half))
    second_shards, second_lands = _split_wait("second_gather_wait", second_state, (proj_top,), _gather_copies)
    g_in_bottom, wq_s, wkv_s = gathered(second_lands, second_shards, "second")
    mid_state = _split_start("mid_gather_start", mid, landing(mid), g_in_bottom, _gather_copies, 4)
    last_state = _split_start("last_gather_start", last, landing(last), mid_state[4], _gather_copies, 4)
    w_in_p = _assemble_w_in(g_in_bottom, front, front_pad, D, half, into=w_in_top)

    inv_freq = 1.0 / (ROPE_THETA ** (jnp.arange(0, QK_ROPE, 2, dtype=F32) / QK_ROPE))
    ang = positions[0].astype(F32)[:, None] * inv_freq
    cos2 = jnp.concatenate([jnp.cos(ang), jnp.cos(ang)], axis=-1)
    sin2 = jnp.concatenate([jnp.sin(ang), jnp.sin(ang)], axis=-1)
    one, zero = jnp.ones((S, QK_NOPE), F32), jnp.zeros((S, QK_NOPE), F32)
    cos_q, sin_q = jnp.concatenate([one, cos2, one, cos2], axis=-1), jnp.concatenate([zero, sin2, zero, sin2], axis=-1)
    cos_k, sin_k = jnp.tile(cos2, (1, COL_BLOCK // QK_ROPE)), jnp.tile(sin2, (1, COL_BLOCK // QK_ROPE))

    proj = _matmul(u, w_in_p, "nn", BF16, "proj", k_rows=(half, half), init=proj_top, deps=(last_state[4],))
    qn = _rms_fwd(proj, g_q_a, 0, Lq, ts, "rms_q")
    kvn = _rms_fwd(proj, g_kv_a, 1, Lkv, ts, "rms_kv")
    q = _matmul(qn, wq_s, "nn", BF16, "q_up")
    kv = _matmul(kvn, wkv_s, "nn", BF16, "kv_up")
    qc, kc, vh = _qk_prep(q, kv, proj, kr_blk, cos_q, sin_q, cos_k, sin_k, H, ts)
    attn, lse = _attn_fwd(qc, kc, vh, T)
    mid_shards, mid_lands = _split_wait("mid_gather_wait", mid_state, lse, _gather_copies)
    w_oa_f, w_ob_f, w_o_f = [g.reshape(-1, D) for g in gathered(mid_lands, mid_shards, "mid")]
    ya = _matmul(attn, w_oa_f, "nn", BF16, "attn_out")
    cbc = _conv_fwd(proj, w_conv_full, blk_b, blk_c, blk_x)
    yb = _matmul(cbc, w_ob_f, "nn", BF16, "conv_out")
    merged = _merge_fwd(proj, ya, yb, blk_ga, blk_gb, ts)
    mix = _matmul(merged, w_o_f, "nn", F32, "mix_out")
    xhat1, rstd1, u2 = _ln1_fwd(x2, mix, mod, ln1_g, ln1_b, ts)
    last_shards, last_lands = _split_wait("last_gather_wait", last_state, u2, _gather_copies)
    w_fi_s, g_fo = gathered(last_lands, last_shards, "last")
    w_fo_f = g_fo.reshape(F, D)
    hh = _matmul(u2, w_fi_s, "nn", BF16, "ffn_in")
    act = _swiglu_fwd(hh, ts, tb)
    ffn = _matmul(act, w_fo_f, "nn", F32, "ffn_out")
    loss_part, dffn, dx1a, vec2 = _ln2_loss(xhat1, ffn, tgt, mod, ln1_g, ln1_b, ln2_g, ln2_b, ts)
    loss = lax.psum(loss_part[0, 0], AXES)

    gw_fo = _matmul(act, dffn, "tn", BF16, "grad_w_ffn_out")
    da = _matmul(dffn, w_fo_f, "nt", BF16, "d_act")
    dh = _swiglu_bwd(da, hh, ts, tb)
    gw_fi = _matmul(u2, dh, "tn", BF16, "grad_w_ffn_in", out_shards=True)
    ffn_in_chip = _in_chip_start([gw_fi, gw_fo.reshape(N_DEV, -1, D)], "ffn")
    du2 = _matmul(dh, w_fi_s, "nt", F32, "d_u2", deps=(ffn_in_chip[4],))
    ffn_state = _reduce_scatter_begin([], "ffn", ffn_in_chip, after=(du2,))
    dxa, dmix, vec1 = _ln1_bwd(du2, dx1a, xhat1, rstd1, mix, mod, ln1_g, ln1_b, ts)
    gw_o = _matmul(merged, dmix, "tn", BF16, "grad_w_o", deps=(ffn_state[4],))
    dmerged = _matmul(dmix, w_o_f, "nt", BF16, "d_merged")
    dya, dyb, dga, dgb = _merge_bwd(dmerged, proj, ya, yb, blk_ga, blk_gb, ts)
    gw_ob = _matmul(cbc, dyb, "tn", BF16, "grad_w_o_b")
    dcbc = _matmul(dyb, w_ob_f, "nt", BF16, "d_conv")
    dcb, dcc, dcx, dwconv = _conv_bwd(dcbc, proj, w_conv_full, blk_b, blk_c, blk_x)
    gw_oa = _matmul(attn, dya, "tn", BF16, "grad_w_o_a")
    mix_in_chip = _in_chip_start([g.reshape(N_DEV, -1, D) for g in (gw_oa, gw_ob, gw_o)], "mix")
    dattn = _matmul(dya, w_oa_f, "nt", BF16, "d_attn", deps=(mix_in_chip[4],))
    dqc, dkc, dvh = _attn_bwd(qc, kc, vh, dattn, attn, lse, T)
    ffn_own, ffn_got = _reduce_scatter_end(ffn_state, dqc, "ffn")
    dq, dkv, dkr = _qk_bwd(dqc, dkc, dvh, cos_q, sin_q, cos_k, sin_k, ts)
    gw_qb = _matmul(qn, dq, "tn", BF16, "grad_w_q_b", out_shards=True)
    gw_kvb = _matmul(kvn, dkv, "tn", BF16, "grad_w_kv_b", out_shards=True)
    mix_state = _reduce_scatter_begin([gw_qb, gw_kvb], "mix", mix_in_chip, after=(dqc,))
    dqn = _matmul(dq, wq_s, "nt", F32, "d_qn", deps=(mix_state[4],))
    dkvn = _matmul(dkv, wkv_s, "nt", F32, "d_kvn")
    dqa, dgq = _rms_bwd(dqn, proj, g_q_a, 0, Lq, ts, "rms_q_bwd")
    dkva, dgkv = _rms_bwd(dkvn, proj, g_kv_a, 1, Lkv, ts, "rms_kv_bwd")
    dproj = jnp.concatenate([dqa, dkva, dkr, dcb, dcc, dcx, dga, dgb], axis=1)
    gw_in_p = _matmul(u, dproj, "tn", BF16, "grad_w_in")
    mix_own, mix_got = _reduce_scatter_end(mix_state, gw_in_p, "mix")
    in_state = _reduce_scatter_begin([_split_w_in(gw_in_p, front, front_pad)], "in")
    du = _matmul(dproj, w_in_p, "nt", F32, "d_u", deps=(in_state[4],))
    grad_x, vec0 = _grad_x(du, dxa, x2, mod, ts)

    my_chip = (2 * lax.axis_index("x") + lax.axis_index("y")).astype(jnp.int32).reshape(1)
    arrived = {}
    for nm, w, m, v, own, got in (
            ("w_ffn_in", w_ffn_in, m_w_ffn_in, v_w_ffn_in, ffn_own[0], ffn_got[0]),
            ("w_ffn_out", w_ffn_out, m_w_ffn_out, v_w_ffn_out, ffn_own[1], ffn_got[1]),
            ("w_o_a", w_o_a, m_w_o_a, v_w_o_a, mix_own[0], mix_got[0]),
            ("w_o_b", w_o_b, m_w_o_b, v_w_o_b, mix_own[1], mix_got[1]),
            ("w_o", w_o, m_w_o, v_w_o, mix_own[2], mix_got[2]),
            ("w_q_b", w_q_b, m_w_q_b, v_w_q_b, mix_own[3], mix_got[3]),
            ("w_kv_b", w_kv_b, m_w_kv_b, v_w_kv_b, mix_own[4], mix_got[4])):
        arrived[nm] = [a[None] for a in _adamw_reduced(w[0], own, got, m[0], v[0], my_chip, "adamw_" + nm)]

    dmod = jnp.concatenate([vec0[0], vec0[1], vec1[4], vec1[0], vec1[1], vec2[2]])
    small = jnp.concatenate([dmod, dgq[0], dgkv[0], vec1[2], vec1[3], vec2[0], vec2[1], dwconv[:CONV_K].reshape(-1)])
    n_small = small.shape[0]
    nch = _round_up(n_small, cw) // cw
    payload = jnp.pad(small, (0, nch * cw - n_small)).reshape(nch, 1, cw)
    in_own, in_got = _reduce_scatter_end(in_state, [res[1] for res in arrived.values()], "in")
    arrived["w_in"] = [a[None] for a in _adamw_reduced(w_in[0], in_own[0], in_got[0], m_w_in[0], v_w_in[0], my_chip,
                                                       "adamw_w_in")]
    summed, dmod_mine = _ada_bwd(payload, deps=[arrived["w_in"][1]])
    arrived["w_ada"] = [a[None] for a in _adamw_ada(w_ada[0], cact_all.T, dmod_mine.reshape(N_DEV, cw),
                                                    m_w_ada[0], v_w_ada[0])]
    summed = summed.reshape(-1)
    offs = [0, 6 * D, 6 * D + Lq, 6 * D + Lq + Lkv]
    offs += [offs[-1] + D * k for k in range(1, 5)]
    g_b_ada = summed[offs[0]:offs[1]].reshape(1, -1)
    g_gq = summed[offs[1]:offs[2]].reshape(1, -1)
    g_gkv = summed[offs[2]:offs[3]].reshape(1, -1)
    g_ln1g, g_ln1b, g_ln2g, g_ln2b = [summed[offs[3 + k]:offs[4 + k]].reshape(1, -1) for k in range(4)]
    wc = w_conv.shape[2]
    g_wconv = lax.dynamic_slice(summed[offs[7]:offs[7] + CONV_K * D].reshape(CONV_K, D), (0, me * wc), (CONV_K, wc))

    names = ["w_ada", "b_ada", "w_in", "g_q_a", "w_q_b", "g_kv_a", "w_kv_b", "w_o_a", "w_conv", "w_o_b", "w_o",
             "ln1_g", "ln1_b", "w_ffn_in", "w_ffn_out", "ln2_g", "ln2_b"]
    weights = [w_ada, b_ada, w_in, g_q_a, w_q_b, g_kv_a, w_kv_b, w_o_a, w_conv, w_o_b, w_o, ln1_g, ln1_b,
               w_ffn_in, w_ffn_out, ln2_g, ln2_b]
    moms = [m_w_ada, m_b_ada, m_w_in, m_g_q_a, m_w_q_b, m_g_kv_a, m_w_kv_b, m_w_o_a, m_w_conv, m_w_o_b, m_w_o,
            m_ln1_g, m_ln1_b, m_w_ffn_in, m_w_ffn_out, m_ln2_g, m_ln2_b]
    vels = [v_w_ada, v_b_ada, v_w_in, v_g_q_a, v_w_q_b, v_g_kv_a, v_w_kv_b, v_w_o_a, v_w_conv, v_w_o_b, v_w_o,
            v_ln1_g, v_ln1_b, v_w_ffn_in, v_w_ffn_out, v_ln2_g, v_ln2_b]
    grad_of = {"b_ada": g_b_ada, "g_q_a": g_gq, "g_kv_a": g_gkv, "w_conv": g_wconv,
               "ln1_g": g_ln1g, "ln1_b": g_ln1b, "ln2_g": g_ln2g, "ln2_b": g_ln2b}
    state_of = dict(zip(names, zip(weights, moms, vels)))
    results = dict(arrived)

    def update(nm, reduced=None):
        w, m, v = state_of[nm]
        shp = w.shape
        w2 = w.reshape(shp[-2], shp[-1]) if w.ndim == 3 else w
        m2, v2 = m.reshape(w2.shape), v.reshape(w2.shape)
        if reduced is None:
            g2 = grad_of[nm].reshape(w2.shape)
            res = (g2,) + tuple(_adamw(w2, g2, m2, v2, "adamw_" + nm))
        else:
            res = _adamw_reduced(w2, reduced[0], reduced[1], m2, v2, my_chip, "adamw_" + nm)
        results[nm] = [a.reshape(shp) for a in res]

    for nm in grad_of:
        update(nm)
    outs = [[results[nm][k] for nm in names] for k in range(4)]
    return (loss, grad_x.reshape(x.shape), *outs[0], *outs[1], *outs[2], *outs[3])
```

```python
import functools

import jax
import jax.numpy as jnp
from jax import lax
from jax.experimental import pallas as pl
from jax.experimental.pallas import tpu as pltpu

F32 = jnp.float32
BF16 = jnp.bfloat16
MESH_ID = pl.DeviceIdType.MESH
AXES = ("x", "y", "c")
N_DEV = 8

CHUNK = 64
QK_NOPE = 128
QK_ROPE = 64
V_HEAD = 128
QK_CAT = QK_NOPE + QK_ROPE
ROPE_THETA = 10000.0
ATTN_SCALE = (QK_NOPE + QK_ROPE) ** -0.5
CONV_K = 3
DEEPNORM_ALPHA = 2.0 ** 0.25
LN_EPS = 1e-5
RMS_EPS = 1e-6
NEG_INF = -1e30

ADAM_LR = 0.001
ADAM_B1 = 0.9
ADAM_B2 = 0.999
ADAM_EPS = 1e-08
ADAM_WD = 0.01
ADAM_STEP = 10

LANE = 128
COL_BLOCK = 256
PACK_ROW_ALIGN = 16
PAIR_SUM_BLOCK = 1 << 20
VMEM_LIMIT = 48 * 1024 * 1024


def _round_up(n, m):
    return (n + m - 1) // m * m


def _tile(n, pref, align=LANE):
    best = None
    t = align
    while t <= min(n, pref):
        if n % t == 0:
            best = t
        t += align
    return best if best is not None else n


def _cparams(sem=None):
    return pltpu.CompilerParams(dimension_semantics=sem, vmem_limit_bytes=VMEM_LIMIT)


def _sigmoid(x):
    return 0.5 * jnp.tanh(0.5 * x) + 0.5


def _matmul(a, b, mode, out_dtype, name, tm=1024, tn=1024, tk=2048, deps=(), out_shards=False, k_rows=None,
            init=None):
    b_shards = b.ndim == 3
    n = b.shape[2] if b_shards else (b.shape[1] // N_DEV if out_shards else None)
    if mode == "nn":
        (M, K), (K2, N) = a.shape, (b.shape[1], N_DEV * n) if b_shards else b.shape
    elif mode == "nt":
        (M, K), (N, K2) = a.shape, (b.shape[1], N_DEV * n) if b_shards else b.shape
    else:
        (K, M), (K2, N) = a.shape, b.shape
    assert K == K2, (a.shape, b.shape, mode)
    tm = _tile(M, tm)
    tn = n if (mode != "nt" and n is not None) else _tile(N, tn)
    k_row0, k_len = k_rows if k_rows is not None else (0, K)
    tk = n if (mode == "nt" and b_shards) else _tile(k_len, tk)
    nk, k0 = k_len // tk, k_row0 // tk
    if mode == "nn":
        a_spec = pl.BlockSpec((tm, tk), lambda i, j, k: (i, k0 + k))
        b_spec = (pl.BlockSpec((1, tk, n), lambda i, j, k: (j, k, 0)) if b_shards
                  else pl.BlockSpec((tk, tn), lambda i, j, k: (k0 + k, j)))
        dims = (((1,), (0,)), ((), ()))
    elif mode == "nt":
        a_spec = pl.BlockSpec((tm, tk), lambda i, j, k: (i, k))
        b_spec = (pl.BlockSpec((1, tn, n), lambda i, j, k: (k, j, 0)) if b_shards
                  else pl.BlockSpec((tn, tk), lambda i, j, k: (j, k)))
        dims = (((1,), (1,)), ((), ()))
    else:
        a_spec = pl.BlockSpec((tk, tm), lambda i, j, k: (k, i))
        b_spec = pl.BlockSpec((tk, tn), lambda i, j, k: (k, j))
        dims = (((0,), (0,)), ((), ()))
    if out_shards:
        out_spec = pl.BlockSpec((1, tm, n), lambda i, j, k: (j, i, 0))
        out_shape = jax.ShapeDtypeStruct((N_DEV, M, n), out_dtype)
    else:
        out_spec = pl.BlockSpec((tm, tn), lambda i, j, k: (i, j))
        out_shape = jax.ShapeDtypeStruct((M, N), out_dtype)

    def product(a_ref, b_ref):
        b_blk = b_ref[0] if b_shards else b_ref[...]
        return lax.dot_general(a_ref[...].astype(BF16), b_blk.astype(BF16), dims, preferred_element_type=F32)

    def write(o_ref, value):
        if out_shards:
            o_ref[0] = value.astype(o_ref.dtype)
        else:
            o_ref[...] = value.astype(o_ref.dtype)

    def body_whole_k(a_ref, b_ref, *rest):
        value = product(a_ref, b_ref)
        write(rest[-1], value if init is None else value + rest[0][...])

    def body_split_k(a_ref, b_ref, *rest):
        o_ref, acc_ref = rest[-2:]
        k = pl.program_id(2)

        @pl.when(k == 0)
        def _():
            acc_ref[...] = jnp.zeros_like(acc_ref) if init is None else rest[0][...].astype(F32)

        acc_ref[...] += product(a_ref, b_ref)

        @pl.when(k == nk - 1)
        def _():
            write(o_ref, acc_ref[...])

    return pl.pallas_call(
        body_whole_k if nk == 1 else body_split_k, name=name, grid=(M // tm, N // tn, nk),
        in_specs=[a_spec, b_spec] + ([] if init is None else [out_spec]) + [ANY_SPEC] * len(deps),
        out_specs=out_spec, out_shape=out_shape,
        scratch_shapes=[] if nk == 1 else [pltpu.VMEM((tm, tn), F32)],
        compiler_params=_cparams(("parallel", "parallel", "arbitrary")),
    )(a, b, *(() if init is None else (init,)), *deps)


def _assemble_w_in(shards, front, front_pad, rows, row0, into=None):
    _, K, n = shards.shape
    gap = front_pad - front
    tk = _tile(K, 256, PACK_ROW_ALIGN)
    blk0 = row0 // tk

    def body(g_ref, *rest):
        o_ref = rest[-1]
        if gap:
            o_ref[:, front:front_pad] = jnp.zeros((tk, gap), o_ref.dtype)
        for j in range(N_DEV):
            lo, hi = j * n, (j + 1) * n
            if lo < front < hi:
                o_ref[:, lo:front] = g_ref[j, :, 0:front - lo]
                o_ref[:, front_pad:hi + gap] = g_ref[j, :, front - lo:n]
            else:
                off = 0 if hi <= front else gap
                o_ref[:, lo + off:hi + off] = g_ref[j]

    return pl.pallas_call(
        body, name="assemble_w_in", grid=(K // tk,),
        in_specs=[pl.BlockSpec((N_DEV, tk, n), lambda i: (0, i, 0))] + ([] if into is None else [ANY_SPEC]),
        out_specs=pl.BlockSpec((tk, N_DEV * n + gap), lambda i: (blk0 + i, 0)),
        out_shape=jax.ShapeDtypeStruct((rows, N_DEV * n + gap), shards.dtype),
        input_output_aliases={} if into is None else {1: 0},
        compiler_params=_cparams(("parallel",)),
    )(*([shards] if into is None else [shards, into]))


def _split_w_in(w, front, front_pad):
    K, NP = w.shape
    gap = front_pad - front
    n = (NP - gap) // N_DEV
    tk = _tile(K, 256, PACK_ROW_ALIGN)

    def body(w_ref, o_ref):
        for j in range(N_DEV):
            lo, hi = j * n, (j + 1) * n
            if lo < front < hi:
                o_ref[j, :, 0:front - lo] = w_ref[:, lo:front]
                o_ref[j, :, front - lo:n] = w_ref[:, front_pad:hi + gap]
            else:
                off = 0 if hi <= front else gap
                o_ref[j] = w_ref[:, lo + off:hi + off]

    return pl.pallas_call(
        body, name="split_grad_w_in", grid=(K // tk,),
        in_specs=[pl.BlockSpec((tk, NP), lambda i: (i, 0))],
        out_specs=pl.BlockSpec((N_DEV, tk, n), lambda i: (0, i, 0)),
        out_shape=jax.ShapeDtypeStruct((N_DEV, K, n), w.dtype),
        compiler_params=_cparams(("parallel",)),
    )(w)


def _modulate_in(x, mod, ts):
    S, D = x.shape

    def body(x_ref, mod_ref, u_ref):
        u_ref[...] = (x_ref[...] * (1.0 + mod_ref[1:2, :]) + mod_ref[0:1, :]).astype(BF16)

    return pl.pallas_call(
        body, name="modulate_in", grid=(S // ts,),
        in_specs=[pl.BlockSpec((ts, D), lambda i: (i, 0)), pl.BlockSpec((6, D), lambda i: (0, 0))],
        out_specs=pl.BlockSpec((ts, D), lambda i: (i, 0)),
        out_shape=jax.ShapeDtypeStruct((S, D), BF16),
        compiler_params=_cparams(("parallel",)),
    )(x, mod)


def _rms_fwd(proj, g, blk, L, ts, name):
    S = proj.shape[0]

    def body(a_ref, g_ref, y_ref):
        a = a_ref[...].astype(F32)
        r = lax.rsqrt(jnp.mean(a * a, axis=-1, keepdims=True) + RMS_EPS)
        y_ref[...] = (a * r * g_ref[...]).astype(BF16)

    return pl.pallas_call(
        body, name=name, grid=(S // ts,),
        in_specs=[pl.BlockSpec((ts, L), lambda i: (i, blk)), pl.BlockSpec((1, L), lambda i: (0, 0))],
        out_specs=pl.BlockSpec((ts, L), lambda i: (i, 0)),
        out_shape=jax.ShapeDtypeStruct((S, L), BF16),
        compiler_params=_cparams(("parallel",)),
    )(proj, g)


def _rope_partner(x, period, start):
    w = x.shape[-1]
    lane = lax.broadcasted_iota(jnp.int32, x.shape, x.ndim - 1) % period
    first = (lane >= start) & (lane < start + QK_ROPE // 2)
    from_right = pltpu.roll(x, w - QK_ROPE // 2, axis=x.ndim - 1)
    from_left = pltpu.roll(x, QK_ROPE // 2, axis=x.ndim - 1)
    return jnp.where(first, -from_right, from_left)


def _qk_prep(q, kv, proj, kr_blk, cos_q, sin_q, cos_k, sin_k, H, ts):
    S = q.shape[0]
    pair = 2 * QK_CAT
    kv_w = QK_NOPE + V_HEAD

    def body(q_ref, kv_ref, kr_ref, cq_ref, sq_ref, ck_ref, sk_ref, qc_ref, kc_ref, vh_ref):
        kr = kr_ref[...].astype(F32)
        kr = kr * ck_ref[...] + _rope_partner(kr, QK_ROPE, 0) * sk_ref[...]
        kr = kr[:, :QK_ROPE].astype(BF16)
        for p in range(H // 2):
            x = q_ref[:, p * pair:(p + 1) * pair].astype(F32)
            x = x * cq_ref[...] + _rope_partner(x, QK_CAT, QK_NOPE) * sq_ref[...]
            qc_ref[2 * p] = x[:, :QK_CAT].astype(BF16)
            qc_ref[2 * p + 1] = x[:, QK_CAT:].astype(BF16)
        for h in range(H):
            kc_ref[h, :, 0:QK_NOPE] = kv_ref[:, h * kv_w:h * kv_w + QK_NOPE].astype(BF16)
            kc_ref[h, :, QK_NOPE:QK_CAT] = kr
            vh_ref[h, :, :] = kv_ref[:, h * kv_w + QK_NOPE:(h + 1) * kv_w].astype(BF16)

    row = lambda w: pl.BlockSpec((ts, w), lambda i: (i, 0))
    return pl.pallas_call(
        body, name="qk_prep", grid=(S // ts,),
        in_specs=[row(H * QK_CAT), row(H * kv_w),
                  pl.BlockSpec((ts, COL_BLOCK), lambda i: (i, kr_blk)),
                  row(pair), row(pair), row(COL_BLOCK), row(COL_BLOCK)],
        out_specs=[pl.BlockSpec((H, ts, QK_CAT), lambda i: (0, i, 0)),
                   pl.BlockSpec((H, ts, QK_CAT), lambda i: (0, i, 0)),
                   pl.BlockSpec((H, ts, V_HEAD), lambda i: (0, i, 0))],
        out_shape=[jax.ShapeDtypeStruct((H, S, QK_CAT), BF16), jax.ShapeDtypeStruct((H, S, QK_CAT), BF16),
                   jax.ShapeDtypeStruct((H, S, V_HEAD), BF16)],
        compiler_params=_cparams(("parallel",)),
    )(q, kv, proj, cos_q, sin_q, cos_k, sin_k)


NT_DIMS = (((1,), (1,)), ((), ()))
TN_DIMS = (((0,), (0,)), ((), ()))


def _diag_mask(T):
    rows = lax.broadcasted_iota(jnp.int32, (T, T), 0) // CHUNK
    cols = lax.broadcasted_iota(jnp.int32, (T, T), 1) // CHUNK
    return cols <= rows


def _attn_fwd(qc, kc, vh, T):
    H, S, _ = qc.shape
    n = S // T

    def body(q_ref, k_ref, v_ref, o_ref, lse_ref):
        q = q_ref[0]

        def block(i):
            L = (i + 1) * T
            s_old = lax.dot_general(q, k_ref[0, 0:i * T, :], NT_DIMS, preferred_element_type=F32) if i else None
            s_diag = lax.dot_general(q, k_ref[0, i * T:L, :], NT_DIMS, preferred_element_type=F32)
            s_diag = jnp.where(_diag_mask(T), s_diag, NEG_INF)
            m = jnp.max(s_diag, axis=-1, keepdims=True)
            if i:
                m = jnp.maximum(m, jnp.max(s_old, axis=-1, keepdims=True))
            p_diag = jnp.exp((s_diag - m) * ATTN_SCALE)
            l = jnp.sum(p_diag, axis=-1, keepdims=True)
            acc = jnp.dot(p_diag.astype(BF16), v_ref[0, i * T:L, :], preferred_element_type=F32)
            if i:
                p_old = jnp.exp((s_old - m) * ATTN_SCALE)
                l = l + jnp.sum(p_old, axis=-1, keepdims=True)
                acc = acc + jnp.dot(p_old.astype(BF16), v_ref[0, 0:i * T, :], preferred_element_type=F32)
            o_ref[...] = (acc / l).astype(o_ref.dtype)
            lse_ref[0] = m * ATTN_SCALE + jnp.log(l)

        for i in range(n):
            pl.when(pl.program_id(1) == i)(functools.partial(block, i))

    return pl.pallas_call(
        body, name="attn_fwd", grid=(H, n),
        in_specs=[pl.BlockSpec((1, T, QK_CAT), lambda h, i: (h, i, 0)),
                  pl.BlockSpec((1, S, QK_CAT), lambda h, i: (h, 0, 0)),
                  pl.BlockSpec((1, S, V_HEAD), lambda h, i: (h, 0, 0))],
        out_specs=[pl.BlockSpec((T, V_HEAD), lambda h, i: (i, h)),
                   pl.BlockSpec((1, T, 1), lambda h, i: (h, i, 0))],
        out_shape=[jax.ShapeDtypeStruct((S, H * V_HEAD), BF16), jax.ShapeDtypeStruct((H, S, 1), F32)],
        compiler_params=_cparams(("parallel", "arbitrary")),
    )(qc, kc, vh)


def _shift_rows(z, k):
    if k == 0:
        return z
    n = z.shape[0]
    row = lax.broadcasted_iota(jnp.int32, z.shape, 0)
    if k > 0:
        return jnp.where(row >= k, pltpu.roll(z, k, axis=0), 0.0)
    return jnp.where(row < n + k, pltpu.roll(z, n + k, axis=0), 0.0)


def _conv_fwd(proj, w_conv, blk_b, blk_c, blk_x):
    S = proj.shape[0]
    D = w_conv.shape[1]
    nb = D // COL_BLOCK

    def body(cb_ref, cc_ref, cx_ref, w_ref, o_ref):
        z = cc_ref[...].astype(F32) * cx_ref[...].astype(F32)
        conv = w_ref[2:3, :] * z + w_ref[1:2, :] * _shift_rows(z, 1) + w_ref[0:1, :] * _shift_rows(z, 2)
        o_ref[...] = (cb_ref[...].astype(F32) * conv).astype(BF16)

    col = lambda off: pl.BlockSpec((S, COL_BLOCK), lambda j: (0, off + j))
    return pl.pallas_call(
        body, name="conv_fwd", grid=(nb,),
        in_specs=[col(blk_b), col(blk_c), col(blk_x), pl.BlockSpec((CONV_K, COL_BLOCK), lambda j: (0, j))],
        out_specs=pl.BlockSpec((S, COL_BLOCK), lambda j: (0, j)),
        out_shape=jax.ShapeDtypeStruct((S, D), BF16),
        compiler_params=_cparams(("parallel",)),
    )(proj, proj, proj, w_conv)


def _merge_fwd(proj, ya, yb, blk_ga, blk_gb, ts):
    S, D = ya.shape
    nb = D // COL_BLOCK

    def body(ga_ref, gb_ref, ya_ref, yb_ref, o_ref):
        sa, sb = _sigmoid(ga_ref[...].astype(F32)), _sigmoid(gb_ref[...].astype(F32))
        o_ref[...] = (sa * ya_ref[...].astype(F32) + sb * yb_ref[...].astype(F32)).astype(BF16)

    row = pl.BlockSpec((ts, D), lambda i: (i, 0))
    seg = lambda blk: pl.BlockSpec((pl.Element(ts), pl.Element(D)), lambda i: (i * ts, blk * COL_BLOCK))
    return pl.pallas_call(
        body, name="merge_fwd", grid=(S // ts,),
        in_specs=[seg(blk_ga), seg(blk_gb), row, row],
        out_specs=row,
        out_shape=jax.ShapeDtypeStruct((S, D), BF16),
        compiler_params=_cparams(("parallel",)),
    )(proj, proj, ya, yb)


def _ln1_fwd(x, mix, mod, g, b, ts):
    S, D = x.shape

    def body(x_ref, mix_ref, mod_ref, g_ref, b_ref, xhat_ref, rstd_ref, u2_ref):
        r = DEEPNORM_ALPHA * x_ref[...] + mod_ref[2:3, :] * mix_ref[...]
        mu = jnp.mean(r, axis=-1, keepdims=True)
        d = r - mu
        rstd = lax.rsqrt(jnp.mean(d * d, axis=-1, keepdims=True) + LN_EPS)
        xhat = d * rstd
        xhat_ref[...] = xhat
        rstd_ref[...] = rstd
        x1 = xhat * g_ref[...] + b_ref[...]
        u2_ref[...] = (x1 * (1.0 + mod_ref[4:5, :]) + mod_ref[3:4, :]).astype(BF16)

    row = pl.BlockSpec((ts, D), lambda i: (i, 0))
    vec = lambda r: pl.BlockSpec((r, D), lambda i: (0, 0))
    return pl.pallas_call(
        body, name="ln1_fwd", grid=(S // ts,),
        in_specs=[row, row, vec(6), vec(1), vec(1)],
        out_specs=[row, pl.BlockSpec((ts, 1), lambda i: (i, 0)), row],
        out_shape=[jax.ShapeDtypeStruct((S, D), F32), jax.ShapeDtypeStruct((S, 1), F32),
                   jax.ShapeDtypeStruct((S, D), BF16)],
        compiler_params=_cparams(("parallel",)),
    )(x, mix, mod, g, b)


def _swiglu_fwd(h, ts, tb):
    S, F2 = h.shape
    F = F2 // 2
    nb = F // tb

    def body(hg_ref, hu_ref, a_ref):
        hg = hg_ref[...].astype(F32)
        a_ref[...] = (hg * _sigmoid(hg) * hu_ref[...].astype(F32)).astype(BF16)

    return pl.pallas_call(
        body, name="swiglu_fwd", grid=(S // ts, nb),
        in_specs=[pl.BlockSpec((ts, tb), lambda i, j: (i, j)), pl.BlockSpec((ts, tb), lambda i, j: (i, j + nb))],
        out_specs=pl.BlockSpec((ts, tb), lambda i, j: (i, j)),
        out_shape=jax.ShapeDtypeStruct((S, F), BF16),
        compiler_params=_cparams(("parallel", "parallel")),
    )(h, h)


def _ln2_loss(xhat1, ffn, tgt, mod, g1, b1, g2, b2, ts):
    S, D = xhat1.shape

    def body(xh_ref, ffn_ref, t_ref, mod_ref, g1_ref, b1_ref, g2_ref, b2_ref, loss_ref, dffn_ref, dx1_ref, vec_ref):
        i = pl.program_id(0)

        @pl.when(i == 0)
        def _():
            loss_ref[...] = jnp.zeros_like(loss_ref)
            vec_ref[...] = jnp.zeros_like(vec_ref)

        x1 = xh_ref[...] * g1_ref[...] + b1_ref[...]
        ffn = ffn_ref[...]
        r = DEEPNORM_ALPHA * x1 + mod_ref[5:6, :] * ffn
        mu = jnp.mean(r, axis=-1, keepdims=True)
        d = r - mu
        rstd = lax.rsqrt(jnp.mean(d * d, axis=-1, keepdims=True) + LN_EPS)
        xhat = d * rstd
        e = xhat * g2_ref[...] + b2_ref[...] - t_ref[...]
        loss_ref[...] += 0.5 * jnp.sum(jnp.mean(e * e, axis=-1, keepdims=True))
        dy = e * (1.0 / D)
        dxhat = dy * g2_ref[...]
        dr = rstd * (dxhat - jnp.mean(dxhat, axis=-1, keepdims=True)
                     - xhat * jnp.mean(dxhat * xhat, axis=-1, keepdims=True))
        dffn_ref[...] = (dr * mod_ref[5:6, :]).astype(BF16)
        dx1_ref[...] = DEEPNORM_ALPHA * dr
        vec_ref[0:1, :] += jnp.sum(dy * xhat, axis=0, keepdims=True)
        vec_ref[1:2, :] += jnp.sum(dy, axis=0, keepdims=True)
        vec_ref[2:3, :] += jnp.sum(dr * ffn, axis=0, keepdims=True)

    row = pl.BlockSpec((ts, D), lambda i: (i, 0))
    vec = lambda r: pl.BlockSpec((r, D), lambda i: (0, 0))
    return pl.pallas_call(
        body, name="ln2_loss", grid=(S // ts,),
        in_specs=[row, row, row, vec(6), vec(1), vec(1), vec(1), vec(1)],
        out_specs=[pl.BlockSpec((1, LANE), lambda i: (0, 0)), row, row, vec(8)],
        out_shape=[jax.ShapeDtypeStruct((1, LANE), F32), jax.ShapeDtypeStruct((S, D), BF16),
                   jax.ShapeDtypeStruct((S, D), F32), jax.ShapeDtypeStruct((8, D), F32)],
        compiler_params=_cparams(("arbitrary",)),
    )(xhat1, ffn, tgt, mod, g1, b1, g2, b2)


def _swiglu_bwd(da, h, ts, tb):
    S, F2 = h.shape
    nb = (F2 // 2) // tb

    def body(da_ref, hg_ref, hu_ref, dh_ref):
        hg, da = hg_ref[...].astype(F32), da_ref[...].astype(F32)
        sg = _sigmoid(hg)

        @pl.when(pl.program_id(2) == 0)
        def _():
            dh_ref[...] = (da * hu_ref[...].astype(F32) * (sg * (1.0 + hg * (1.0 - sg)))).astype(BF16)

        @pl.when(pl.program_id(2) == 1)
        def _():
            dh_ref[...] = (da * hg * sg).astype(BF16)

    lo = pl.BlockSpec((ts, tb), lambda i, j, k: (i, j))
    hi = pl.BlockSpec((ts, tb), lambda i, j, k: (i, j + nb))
    return pl.pallas_call(
        body, name="swiglu_bwd", grid=(S // ts, nb, 2),
        in_specs=[lo, lo, hi],
        out_specs=pl.BlockSpec((ts, tb), lambda i, j, k: (i, j + nb * k)),
        out_shape=jax.ShapeDtypeStruct((S, F2), BF16),
        compiler_params=_cparams(("parallel", "parallel", "arbitrary")),
    )(da, h, h)


def _ln1_bwd(du2, dx1a, xhat1, rstd1, mix, mod, g1, b1, ts):
    S, D = xhat1.shape

    def body(du2_ref, dx1a_ref, xh_ref, rstd_ref, mix_ref, mod_ref, g_ref, b_ref, dxa_ref, dmix_ref, vec_ref):
        i = pl.program_id(0)

        @pl.when(i == 0)
        def _():
            vec_ref[...] = jnp.zeros_like(vec_ref)

        xhat, du2, mix = xh_ref[...], du2_ref[...], mix_ref[...]
        x1 = xhat * g_ref[...] + b_ref[...]
        dx1 = dx1a_ref[...] + du2 * (1.0 + mod_ref[4:5, :])
        dxhat = dx1 * g_ref[...]
        dr = rstd_ref[...] * (dxhat - jnp.mean(dxhat, axis=-1, keepdims=True)
                              - xhat * jnp.mean(dxhat * xhat, axis=-1, keepdims=True))
        dxa_ref[...] = DEEPNORM_ALPHA * dr
        dmix_ref[...] = (dr * mod_ref[2:3, :]).astype(BF16)
        vec_ref[0:1, :] += jnp.sum(du2, axis=0, keepdims=True)
        vec_ref[1:2, :] += jnp.sum(du2 * x1, axis=0, keepdims=True)
        vec_ref[2:3, :] += jnp.sum(dx1 * xhat, axis=0, keepdims=True)
        vec_ref[3:4, :] += jnp.sum(dx1, axis=0, keepdims=True)
        vec_ref[4:5, :] += jnp.sum(dr * mix, axis=0, keepdims=True)

    row = pl.BlockSpec((ts, D), lambda i: (i, 0))
    vec = lambda r: pl.BlockSpec((r, D), lambda i: (0, 0))
    return pl.pallas_call(
        body, name="ln1_bwd", grid=(S // ts,),
        in_specs=[row, row, row, pl.BlockSpec((ts, 1), lambda i: (i, 0)), row, vec(6), vec(1), vec(1)],
        out_specs=[row, row, vec(8)],
        out_shape=[jax.ShapeDtypeStruct((S, D), F32), jax.ShapeDtypeStruct((S, D), BF16),
                   jax.ShapeDtypeStruct((8, D), F32)],
        compiler_params=_cparams(("arbitrary",)),
    )(du2, dx1a, xhat1, rstd1, mix, mod, g1, b1)


def _merge_bwd(dmerged, proj, ya, yb, blk_ga, blk_gb, ts):
    S, D = ya.shape
    nb = D // COL_BLOCK

    def body(dm_ref, ga_ref, gb_ref, ya_ref, yb_ref, dya_ref, dyb_ref, dga_ref, dgb_ref):
        dm = dm_ref[...].astype(F32)
        sa, sb = _sigmoid(ga_ref[...].astype(F32)), _sigmoid(gb_ref[...].astype(F32))
        dya_ref[...] = (dm * sa).astype(BF16)
        dyb_ref[...] = (dm * sb).astype(BF16)
        dga_ref[...] = (dm * ya_ref[...].astype(F32) * sa * (1.0 - sa)).astype(BF16)
        dgb_ref[...] = (dm * yb_ref[...].astype(F32) * sb * (1.0 - sb)).astype(BF16)

    row = pl.BlockSpec((ts, D), lambda i: (i, 0))
    seg = lambda blk: pl.BlockSpec((pl.Element(ts), pl.Element(D)), lambda i: (i * ts, blk * COL_BLOCK))
    out = jax.ShapeDtypeStruct((S, D), BF16)
    return pl.pallas_call(
        body, name="merge_bwd", grid=(S // ts,),
        in_specs=[row, seg(blk_ga), seg(blk_gb), row, row],
        out_specs=[row] * 4,
        out_shape=[out] * 4,
        compiler_params=_cparams(("parallel",)),
    )(dmerged, proj, proj, ya, yb)


def _conv_bwd(dcbc, proj, w_conv, blk_b, blk_c, blk_x):
    S = proj.shape[0]
    D = w_conv.shape[1]
    nb = D // COL_BLOCK

    def body(d_ref, cb_ref, cc_ref, cx_ref, w_ref, dcb_ref, dcc_ref, dcx_ref, dw_ref):
        d, cc, cx = d_ref[...].astype(F32), cc_ref[...].astype(F32), cx_ref[...].astype(F32)
        z = cc * cx
        z1, z2 = _shift_rows(z, 1), _shift_rows(z, 2)
        conv = w_ref[2:3, :] * z + w_ref[1:2, :] * z1 + w_ref[0:1, :] * z2
        dcb_ref[...] = (d * conv).astype(BF16)
        dconv = d * cb_ref[...].astype(F32)
        dz = w_ref[2:3, :] * dconv + w_ref[1:2, :] * _shift_rows(dconv, -1) + w_ref[0:1, :] * _shift_rows(dconv, -2)
        dcc_ref[...] = (dz * cx).astype(BF16)
        dcx_ref[...] = (dz * cc).astype(BF16)
        dw_ref[...] = jnp.zeros_like(dw_ref)
        dw_ref[0:1, :] = jnp.sum(dconv * z2, axis=0, keepdims=True)
        dw_ref[1:2, :] = jnp.sum(dconv * z1, axis=0, keepdims=True)
        dw_ref[2:3, :] = jnp.sum(dconv * z, axis=0, keepdims=True)

    col = lambda off: pl.BlockSpec((S, COL_BLOCK), lambda j: (0, off + j))
    out = jax.ShapeDtypeStruct((S, D), BF16)
    return pl.pallas_call(
        body, name="conv_bwd", grid=(nb,),
        in_specs=[col(0), col(blk_b), col(blk_c), col(blk_x), pl.BlockSpec((CONV_K, COL_BLOCK), lambda j: (0, j))],
        out_specs=[col(0), col(0), col(0), pl.BlockSpec((8, COL_BLOCK), lambda j: (0, j))],
        out_shape=[out, out, out, jax.ShapeDtypeStruct((8, D), F32)],
        compiler_params=_cparams(("parallel",)),
    )(dcbc, proj, proj, proj, w_conv)


def _attn_bwd(qc, kc, vh, do, o, lse, T):
    H, S, _ = qc.shape
    n = S // T

    def body(q_ref, k_ref, v_ref, do_ref, o_ref, lse_ref, dq_ref, dk_ref, dv_ref, d_ref, dq_acc, dk_acc, dv_acc):
        j = pl.program_id(1)

        @pl.when(j == 0)
        def _():
            dq_acc[...] = jnp.zeros_like(dq_acc)
            d_ref[...] = jnp.sum(do_ref[...].astype(F32) * o_ref[...].astype(F32), axis=-1, keepdims=True)

        dk_acc[...] = jnp.zeros_like(dk_acc)
        dv_acc[...] = jnp.zeros_like(dv_acc)
        k, v = k_ref[0], v_ref[0]

        def step(i, masked):
            rows = pl.ds(pl.multiple_of(i * T, T), T)
            q = q_ref[0, rows, :]
            do = do_ref[rows, :].astype(BF16)
            s = lax.dot_general(q, k, NT_DIMS, preferred_element_type=F32) * ATTN_SCALE
            if masked:
                s = jnp.where(_diag_mask(T), s, NEG_INF)
            p = jnp.exp(s - lse_ref[0, rows, :])
            dv_acc[...] += lax.dot_general(p.astype(BF16), do, TN_DIMS, preferred_element_type=F32)
            dp = lax.dot_general(do, v, NT_DIMS, preferred_element_type=F32)
            ds = (p * (dp - d_ref[rows, :]) * ATTN_SCALE).astype(BF16)
            dk_acc[...] += lax.dot_general(ds, q, TN_DIMS, preferred_element_type=F32)
            dq_acc[rows, :] += jnp.dot(ds, k, preferred_element_type=F32)

        def above(i, carry):
            step(i, False)
            return carry

        step(j, True)
        lax.fori_loop(j + 1, n, above, 0)
        dk_ref[0] = dk_acc[...].astype(BF16)
        dv_ref[0] = dv_acc[...].astype(BF16)

        @pl.when(j == n - 1)
        def _():
            dq_ref[0] = dq_acc[...].astype(BF16)

    head = lambda w: pl.BlockSpec((1, S, w), lambda h, j: (h, 0, 0))
    blk = lambda w: pl.BlockSpec((1, T, w), lambda h, j: (h, j, 0))
    ospec = pl.BlockSpec((S, V_HEAD), lambda h, j: (0, h))
    return pl.pallas_call(
        body, name="attn_bwd", grid=(H, n),
        in_specs=[head(QK_CAT), blk(QK_CAT), blk(V_HEAD), ospec, ospec, head(1)],
        out_specs=[head(QK_CAT), blk(QK_CAT), blk(V_HEAD)],
        out_shape=[jax.ShapeDtypeStruct((H, S, QK_CAT), BF16), jax.ShapeDtypeStruct((H, S, QK_CAT), BF16),
                   jax.ShapeDtypeStruct((H, S, V_HEAD), BF16)],
        scratch_shapes=[pltpu.VMEM((S, 1), F32), pltpu.VMEM((S, QK_CAT), F32), pltpu.VMEM((T, QK_CAT), F32),
                        pltpu.VMEM((T, V_HEAD), F32)],
        compiler_params=_cparams(("parallel", "arbitrary")),
    )(qc, kc, vh, do, o, lse)


def _qk_bwd(dqc, dkc, dvh, cos_q, sin_q, cos_k, sin_k, ts):
    H, S, _ = dqc.shape
    pair = 2 * QK_CAT
    kv_w = QK_NOPE + V_HEAD

    def body(dqc_ref, dkc_ref, dvh_ref, cq_ref, sq_ref, ck_ref, sk_ref, dq_ref, dkv_ref, dkr_ref, q_buf, kr_buf):
        for p in range(H // 2):
            q_buf[:, :QK_CAT] = dqc_ref[2 * p].astype(F32)
            q_buf[:, QK_CAT:] = dqc_ref[2 * p + 1].astype(F32)
            g = q_buf[...]
            dq_ref[:, p * pair:(p + 1) * pair] = (
                g * cq_ref[...] - _rope_partner(g, QK_CAT, QK_NOPE) * sq_ref[...]).astype(BF16)
        kr_sum = jnp.zeros((ts, QK_ROPE), F32)
        for h in range(H):
            dkv_ref[:, h * kv_w:h * kv_w + QK_NOPE] = dkc_ref[h, :, 0:QK_NOPE].astype(BF16)
            dkv_ref[:, h * kv_w + QK_NOPE:(h + 1) * kv_w] = dvh_ref[h].astype(BF16)
            kr_sum = kr_sum + dkc_ref[h, :, QK_NOPE:QK_CAT]
        kr_buf[...] = jnp.zeros_like(kr_buf)
        kr_buf[:, 0:QK_ROPE] = kr_sum
        kr = kr_buf[...]
        dkr_ref[...] = (kr * ck_ref[...] - _rope_partner(kr, QK_ROPE, 0) * sk_ref[...]).astype(BF16)

    row = lambda w: pl.BlockSpec((ts, w), lambda i: (i, 0))
    head = lambda w: pl.BlockSpec((H, ts, w), lambda i: (0, i, 0))
    return pl.pallas_call(
        body, name="qk_bwd", grid=(S // ts,),
        in_specs=[head(QK_CAT), head(QK_CAT), head(V_HEAD), row(pair), row(pair), row(COL_BLOCK), row(COL_BLOCK)],
        out_specs=[row(H * QK_CAT), row(H * kv_w), row(COL_BLOCK)],
        out_shape=[jax.ShapeDtypeStruct((S, H * QK_CAT), BF16), jax.ShapeDtypeStruct((S, H * kv_w), BF16),
                   jax.ShapeDtypeStruct((S, COL_BLOCK), BF16)],
        scratch_shapes=[pltpu.VMEM((ts, pair), F32), pltpu.VMEM((ts, COL_BLOCK), F32)],
        compiler_params=_cparams(("parallel",)),
    )(dqc, dkc, dvh, cos_q, sin_q, cos_k, sin_k)


def _rms_bwd(dy, proj, g, blk, L, ts, name):
    S = proj.shape[0]

    def body(dy_ref, a_ref, g_ref, da_ref, dg_ref):
        i = pl.program_id(0)

        @pl.when(i == 0)
        def _():
            dg_ref[...] = jnp.zeros_like(dg_ref)

        a, dy = a_ref[...].astype(F32), dy_ref[...]
        r = lax.rsqrt(jnp.mean(a * a, axis=-1, keepdims=True) + RMS_EPS)
        dyh = dy * g_ref[...]
        da = r * dyh - a * (r * r * r) * jnp.mean(dyh * a, axis=-1, keepdims=True)
        da_ref[...] = da.astype(BF16)
        dg_ref[0:1, :] += jnp.sum(dy * a * r, axis=0, keepdims=True)

    return pl.pallas_call(
        body, name=name, grid=(S // ts,),
        in_specs=[pl.BlockSpec((ts, L), lambda i: (i, 0)), pl.BlockSpec((ts, L), lambda i: (i, blk)),
                  pl.BlockSpec((1, L), lambda i: (0, 0))],
        out_specs=[pl.BlockSpec((ts, L), lambda i: (i, 0)), pl.BlockSpec((8, L), lambda i: (0, 0))],
        out_shape=[jax.ShapeDtypeStruct((S, L), BF16), jax.ShapeDtypeStruct((8, L), F32)],
        compiler_params=_cparams(("arbitrary",)),
    )(dy, proj, g)


def _grad_x(du, dxa, x, mod, ts):
    S, D = x.shape

    def body(du_ref, dxa_ref, x_ref, mod_ref, dx_ref, vec_ref):
        i = pl.program_id(0)

        @pl.when(i == 0)
        def _():
            vec_ref[...] = jnp.zeros_like(vec_ref)

        du = du_ref[...]
        dx_ref[...] = dxa_ref[...] + du * (1.0 + mod_ref[1:2, :])
        vec_ref[0:1, :] += jnp.sum(du, axis=0, keepdims=True)
        vec_ref[1:2, :] += jnp.sum(du * x_ref[...], axis=0, keepdims=True)

    row = pl.BlockSpec((ts, D), lambda i: (i, 0))
    vec = lambda r: pl.BlockSpec((r, D), lambda i: (0, 0))
    return pl.pallas_call(
        body, name="grad_x", grid=(S // ts,),
        in_specs=[row, row, row, vec(6)],
        out_specs=[row, vec(8)],
        out_shape=[jax.ShapeDtypeStruct((S, D), F32), jax.ShapeDtypeStruct((8, D), F32)],
        compiler_params=_cparams(("arbitrary",)),
    )(du, dxa, x, mod)


def _adamw(w, g, m, v, name):
    R, C = w.shape
    tr = _tile(R, max(8, (1 << 19) // C), 8)
    c1 = 1.0 / (1.0 - ADAM_B1 ** ADAM_STEP)
    c2 = 1.0 / (1.0 - ADAM_B2 ** ADAM_STEP)

    def body(w_ref, g_ref, m_ref, v_ref, d_ref, nm_ref, nv_ref):
        g = g_ref[...]
        m = ADAM_B1 * m_ref[...] + (1.0 - ADAM_B1) * g
        v = ADAM_B2 * v_ref[...] + (1.0 - ADAM_B2) * (g * g)
        nm_ref[...] = m
        nv_ref[...] = v
        d_ref[...] = -ADAM_LR * ((m * c1) / (jnp.sqrt(v * c2) + ADAM_EPS) + ADAM_WD * w_ref[...])

    spec = pl.BlockSpec((tr, C), lambda i: (i, 0))
    out = jax.ShapeDtypeStruct((R, C), F32)
    return pl.pallas_call(
        body, name=name, grid=(R // tr,),
        in_specs=[spec] * 4, out_specs=[spec] * 3, out_shape=[out] * 3,
        compiler_params=_cparams(("parallel",)),
    )(w, g, m, v)


def _adamw_ada(w, cact_t, dmod, m, v):
    R, C = w.shape
    tr = _tile(R, max(8, (1 << 18) // C), 8)
    c1 = 1.0 / (1.0 - ADAM_B1 ** ADAM_STEP)
    c2 = 1.0 / (1.0 - ADAM_B2 ** ADAM_STEP)

    def body(w_ref, ct_ref, dm_ref, m_ref, v_ref, g_ref, d_ref, nm_ref, nv_ref):
        ct = ct_ref[...].astype(BF16).astype(F32)
        dm = dm_ref[...].astype(BF16).astype(F32)
        g = ct[:, 0:1] * dm[0:1, :]
        for b in range(1, N_DEV):
            g = g + ct[:, b:b + 1] * dm[b:b + 1, :]
        m = ADAM_B1 * m_ref[...] + (1.0 - ADAM_B1) * g
        v = ADAM_B2 * v_ref[...] + (1.0 - ADAM_B2) * (g * g)
        g_ref[...] = g
        nm_ref[...] = m
        nv_ref[...] = v
        d_ref[...] = -ADAM_LR * ((m * c1) / (jnp.sqrt(v * c2) + ADAM_EPS) + ADAM_WD * w_ref[...])

    spec = pl.BlockSpec((tr, C), lambda i: (i, 0))
    out = jax.ShapeDtypeStruct((R, C), F32)
    return pl.pallas_call(
        body, name="adamw_w_ada", grid=(R // tr,),
        in_specs=[spec, pl.BlockSpec((tr, N_DEV), lambda i: (i, 0)), pl.BlockSpec((N_DEV, C), lambda i: (0, 0)),
                  spec, spec],
        out_specs=[spec] * 4, out_shape=[out] * 4,
        compiler_params=_cparams(("parallel",)),
    )(w, cact_t, dmod, m, v)


def _adamw_reduced(w, own, got, m, v, my_chip, name):
    R, C = w.shape
    tr = _tile(R, max(PACK_ROW_ALIGN, (1 << 18) // C), PACK_ROW_ALIGN)
    c1 = 1.0 / (1.0 - ADAM_B1 ** ADAM_STEP)
    c2 = 1.0 / (1.0 - ADAM_B2 ** ADAM_STEP)

    def body(chip_ref, w_ref, own_ref, g1_ref, g2_ref, g3_ref, m_ref, v_ref, g_ref, d_ref, nm_ref, nv_ref):
        g = own_ref[0].astype(F32) + g1_ref[0].astype(F32) + g2_ref[0].astype(F32) + g3_ref[0].astype(F32)
        m = ADAM_B1 * m_ref[...] + (1.0 - ADAM_B1) * g
        v = ADAM_B2 * v_ref[...] + (1.0 - ADAM_B2) * (g * g)
        g_ref[...] = g
        nm_ref[...] = m
        nv_ref[...] = v
        d_ref[...] = -ADAM_LR * ((m * c1) / (jnp.sqrt(v * c2) + ADAM_EPS) + ADAM_WD * w_ref[...])

    spec = pl.BlockSpec((tr, C), lambda i, chip: (i, 0))
    slot = lambda k: pl.BlockSpec((1, tr, C), lambda i, chip: (chip[0] ^ k, i, 0))
    out = jax.ShapeDtypeStruct((R, C), F32)
    return pl.pallas_call(
        body, name=name,
        grid_spec=pltpu.PrefetchScalarGridSpec(
            num_scalar_prefetch=1, grid=(R // tr,),
            in_specs=[spec, slot(0), slot(1), slot(2), slot(3), spec, spec],
            out_specs=[spec] * 4),
        out_shape=[out] * 4,
        compiler_params=_cparams(("parallel",)),
    )(my_chip, w, own, got, got, got, m, v)


def _my_place():
    return lax.axis_index("x"), lax.axis_index("y"), lax.axis_index("c")


def _peer(k):
    x, y, c = _my_place()
    return (x ^ ((k >> 2) & 1), y ^ ((k >> 1) & 1), c ^ (k & 1))


def _linear(place):
    return 4 * place[0] + 2 * place[1] + place[2]


def _ada_fwd(c_row, wconv_row, w_ada, b_row):
    D, CW = w_ada.shape
    WC = wconv_row.shape[-1]

    def body(c_ref, wc_ref, w_ref, b_ref, mod_ref, cact_ref, wcall_ref, send_buf, sems):
        me = _linear(_my_place())
        c = c_ref[0]
        cact_ref[me] = c * _sigmoid(c)
        wcall_ref[me] = wc_ref[0]

        def gather_copy(buf, k, grp):
            return pltpu.make_async_remote_copy(
                src_ref=buf.at[me], dst_ref=buf.at[me], send_sem=sems.at[0, grp, k], recv_sem=sems.at[1, grp, k],
                device_id=_peer(k), device_id_type=MESH_ID)

        def gather_recv(buf, k, grp):
            src = _linear(_peer(k))
            return pltpu.make_async_remote_copy(
                src_ref=buf.at[src], dst_ref=buf.at[src], send_sem=sems.at[0, grp, k], recv_sem=sems.at[1, grp, k],
                device_id=_peer(k), device_id_type=MESH_ID)

        for k in range(1, N_DEV):
            gather_copy(cact_ref, k, 0).start()
            gather_copy(wcall_ref, k, 1).start()
        for k in range(1, N_DEV):
            gather_recv(cact_ref, k, 0).wait_recv()
            gather_recv(wcall_ref, k, 1).wait_recv()
        for k in range(1, N_DEV):
            gather_copy(cact_ref, k, 0).wait_send()
            gather_copy(wcall_ref, k, 1).wait_send()

        cact = jnp.concatenate([cact_ref[b] for b in range(N_DEV)], axis=0)
        mod_all = jnp.dot(cact.astype(BF16), w_ref[...].astype(BF16), preferred_element_type=F32) + b_ref[0]
        for b in range(N_DEV):
            send_buf[b] = mod_all[b:b + 1, :]
        mod_ref[me] = send_buf[me]

        def scatter_copy(k):
            dst = _linear(_peer(k))
            return pltpu.make_async_remote_copy(
                src_ref=send_buf.at[dst], dst_ref=mod_ref.at[me], send_sem=sems.at[0, 2, k], recv_sem=sems.at[1, 2, k],
                device_id=_peer(k), device_id_type=MESH_ID)

        def scatter_recv(k):
            src = _linear(_peer(k))
            return pltpu.make_async_remote_copy(
                src_ref=send_buf.at[src], dst_ref=mod_ref.at[src], send_sem=sems.at[0, 2, k], recv_sem=sems.at[1, 2, k],
                device_id=_peer(k), device_id_type=MESH_ID)

        for k in range(1, N_DEV):
            scatter_copy(k).start()
        for k in range(1, N_DEV):
            scatter_recv(k).wait_recv()
        for k in range(1, N_DEV):
            scatter_copy(k).wait_send()

    vmem = pl.BlockSpec(memory_space=pltpu.VMEM)
    return pl.pallas_call(
        body, name="ada_fwd",
        in_specs=[vmem] * 4, out_specs=[vmem] * 3,
        out_shape=[jax.ShapeDtypeStruct((N_DEV, 1, CW), F32), jax.ShapeDtypeStruct((N_DEV, 1, D), F32),
                   jax.ShapeDtypeStruct((N_DEV, 1, WC), F32)],
        scratch_shapes=[pltpu.VMEM((N_DEV, 1, CW), F32), pltpu.SemaphoreType.DMA((2, 3, N_DEV))],
        compiler_params=pltpu.CompilerParams(vmem_limit_bytes=VMEM_LIMIT),
    )(c_row, wconv_row, w_ada, b_row)


def _ada_bwd(payload, deps=()):
    NCH, _, CW = payload.shape

    def body(p_ref, *rest):
        sum_ref, mine_ref, all_ref, sems = rest[-4:]
        me = _linear(_my_place())
        all_ref[me] = p_ref[...]

        def copy(k, slot):
            return pltpu.make_async_remote_copy(
                src_ref=all_ref.at[slot], dst_ref=all_ref.at[slot], send_sem=sems.at[0, k], recv_sem=sems.at[1, k],
                device_id=_peer(k), device_id_type=MESH_ID)

        for k in range(1, N_DEV):
            copy(k, me).start()
        for k in range(1, N_DEV):
            copy(k, _linear(_peer(k))).wait_recv()
        for k in range(1, N_DEV):
            copy(k, me).wait_send()

        total = all_ref[0]
        for b in range(1, N_DEV):
            total = total + all_ref[b]
        sum_ref[...] = total

        for b in range(N_DEV):
            mine_ref[b] = all_ref[b, me]

    vmem = pl.BlockSpec(memory_space=pltpu.VMEM)
    return pl.pallas_call(
        body, name="ada_bwd",
        in_specs=[vmem] + [ANY_SPEC] * len(deps), out_specs=[vmem, vmem],
        out_shape=[jax.ShapeDtypeStruct((NCH, 1, CW), F32), jax.ShapeDtypeStruct((N_DEV, 1, CW), F32)],
        scratch_shapes=[pltpu.VMEM((N_DEV, NCH, 1, CW), F32), pltpu.SemaphoreType.DMA((2, N_DEV))],
        compiler_params=pltpu.CompilerParams(vmem_limit_bytes=VMEM_LIMIT),
    )(payload, *deps)


def _exchange_in_chip(parts):
    W = len(parts)

    def body(*refs):
        p_refs, got_refs, (send_sems, recv_sems) = refs[:W], refs[W:2 * W], refs[2 * W:]
        x, y, c = _my_place()
        sibling = (x, y, 1 - c)
        copies = []
        for w in range(W):
            for q in range(4):
                copies.append(pltpu.make_async_remote_copy(
                    src_ref=p_refs[w].at[2 * q + (1 - c)], dst_ref=got_refs[w].at[q],
                    send_sem=send_sems.at[4 * w + q], recv_sem=recv_sems.at[4 * w + q],
                    device_id=sibling, device_id_type=MESH_ID))
        for cp in copies:
            cp.start()
        for cp in copies:
            cp.wait_recv()
        for cp in copies:
            cp.wait_send()

    return pl.pallas_call(
        body, name="grad_exchange_in_chip",
        in_specs=[HBM_SPEC] * W, out_specs=[HBM_SPEC] * W,
        out_shape=[jax.ShapeDtypeStruct((4,) + p.shape[1:], p.dtype) for p in parts],
        scratch_shapes=[pltpu.SemaphoreType.DMA((4 * W,)), pltpu.SemaphoreType.DMA((4 * W,))],
    )(*parts)


def _pair_sum(parts, got, core):
    _, R, C = parts.shape
    tr = _tile(R, max(PACK_ROW_ALIGN, PAIR_SUM_BLOCK // C), PACK_ROW_ALIGN)

    def body(c_ref, p_ref, g_ref, o_ref):
        o_ref[...] = (p_ref[...].astype(F32) + g_ref[...].astype(F32)).astype(o_ref.dtype)

    return pl.pallas_call(
        body, name="grad_pair_sum",
        grid_spec=pltpu.PrefetchScalarGridSpec(
            num_scalar_prefetch=1, grid=(4, R // tr),
            in_specs=[pl.BlockSpec((1, tr, C), lambda q, i, c_ref: (2 * q + c_ref[0], i, 0)),
                      pl.BlockSpec((1, tr, C), lambda q, i, c_ref: (q, i, 0))],
            out_specs=pl.BlockSpec((1, tr, C), lambda q, i, c_ref: (q, i, 0))),
        out_shape=jax.ShapeDtypeStruct((4, R, C), parts.dtype),
        compiler_params=_cparams(("parallel", "parallel")),
    )(core, parts, got)


HBM_SPEC = pl.BlockSpec(memory_space=pltpu.HBM)
SEM_SPEC = pl.BlockSpec(memory_space=pltpu.SEMAPHORE)
ANY_SPEC = pl.BlockSpec(memory_space=pl.ANY)
SPLIT_EFFECT = pltpu.SideEffectType.DATAFLOW_SIDE_EFFECTING


def _landing_zone(shape, dtype):
    return pltpu.with_memory_space_constraint(lax.empty(shape, dtype), pltpu.HBM)


def _split_start(name, arrays, lands, after, copies_of, per_array):
    W = len(arrays)
    after = tuple(after) if isinstance(after, (tuple, list)) else (after,)

    def body(*refs):
        x_refs, land_refs = refs[:W], refs[W:2 * W]
        send_sems, recv_sems = refs[2 * W + len(after)], refs[2 * W + len(after) + 1]
        token = refs[-1]
        k = 0
        for w in range(W):
            for src, dst, dev in copies_of(w, x_refs[w], land_refs[w]):
                pltpu.make_async_remote_copy(src_ref=src, dst_ref=dst, send_sem=send_sems.at[k], recv_sem=recv_sems.at[k],
                                             device_id=dev, device_id_type=MESH_ID).start()
                k += 1
        token[...] = jnp.zeros_like(token)

    n_copies = per_array * W
    hbm_of = lambda xs: tuple(pltpu.HBM(a.shape, a.dtype) for a in xs)
    out = pl.pallas_call(
        body, name=name,
        out_shape=(pltpu.SemaphoreType.DMA((n_copies,)), pltpu.SemaphoreType.DMA((n_copies,)))
        + hbm_of(arrays) + hbm_of(lands) + (jax.ShapeDtypeStruct((8, LANE), F32),),
        in_specs=(HBM_SPEC,) * (2 * W) + (ANY_SPEC,) * len(after),
        out_specs=(SEM_SPEC, SEM_SPEC) + (HBM_SPEC,) * (2 * W) + (pl.BlockSpec(memory_space=pltpu.VMEM),),
        input_output_aliases={i: 2 + i for i in range(2 * W)},
        compiler_params=pltpu.CompilerParams(has_side_effects=SPLIT_EFFECT),
    )(*[pltpu.with_memory_space_constraint(a, pltpu.HBM) for a in arrays], *lands, *after)
    return out[0], out[1], list(out[2:2 + W]), list(out[2 + W:2 + 2 * W]), out[-1]


def _split_wait(name, state, after, copies_of):
    send_sems, recv_sems, arrays, lands, _ = state
    W = len(arrays)
    after = tuple(after) if isinstance(after, (tuple, list)) else (after,)

    def body(*refs):
        x_refs, land_refs = refs[:W], refs[W:2 * W]
        send_sems, recv_sems = refs[2 * W], refs[2 * W + 1]
        k = 0
        for w in range(W):
            for src, dst, dev in copies_of(w, x_refs[w], land_refs[w]):
                cp = pltpu.make_async_remote_copy(src_ref=src, dst_ref=dst, send_sem=send_sems.at[k],
                                                  recv_sem=recv_sems.at[k], device_id=dev, device_id_type=MESH_ID)
                cp.wait_send()
                cp.wait_recv()
                k += 1

    out = pl.pallas_call(
        body, name=name,
        out_shape=tuple(pltpu.HBM(a.shape, a.dtype) for a in arrays + lands),
        in_specs=(HBM_SPEC,) * (2 * W) + (SEM_SPEC, SEM_SPEC) + (ANY_SPEC,) * len(after),
        out_specs=(HBM_SPEC,) * (2 * W),
        input_output_aliases={i: i for i in range(2 * W)},
        compiler_params=pltpu.CompilerParams(has_side_effects=SPLIT_EFFECT),
    )(*arrays, *lands, send_sems, recv_sems, *after)
    return list(out[:W]), list(out[W:])


def _scatter_copies(w, p_ref, land_ref):
    x, y, c = _my_place()
    my_chip = 2 * x + y
    return [(p_ref.at[2 * (x ^ (k >> 1)) + (y ^ (k & 1))], land_ref.at[my_chip], (x ^ (k >> 1), y ^ (k & 1), c))
            for k in range(1, 4)]


def _gather_copies(w, x_ref, land_ref):
    x, y, c = _my_place()
    me = _linear((x, y, c))
    devs = [(x, y, 1 - c)] + [(x ^ (k >> 1), y ^ (k & 1), c) for k in range(1, 4)]
    return [(x_ref, land_ref.at[me], d) for d in devs]


def _gather_forward(lands, name):
    W = len(lands)

    def body(*refs):
        land_refs, out_refs, (send_sems, recv_sems) = refs[:W], refs[W:2 * W], refs[2 * W:]
        x, y, c = _my_place()
        sibling = (x, y, 1 - c)
        sends, arrivals = [], []
        for w in range(W):
            for k in range(1, 4):
                px, py = x ^ (k >> 1), y ^ (k & 1)
                landed, theirs = _linear((px, py, c)), out_refs[w].at[_linear((px, py, 1 - c))]
                sem = 3 * w + k - 1
                sends.append(pltpu.make_async_remote_copy(
                    src_ref=land_refs[w].at[landed], dst_ref=out_refs[w].at[landed],
                    send_sem=send_sems.at[sem], recv_sem=recv_sems.at[sem], device_id=sibling, device_id_type=MESH_ID))
                arrivals.append(pltpu.make_async_remote_copy(
                    src_ref=theirs, dst_ref=theirs, send_sem=send_sems.at[sem], recv_sem=recv_sems.at[sem],
                    device_id=sibling, device_id_type=MESH_ID))
        for cp in sends:
            cp.start()
        for cp in arrivals:
            cp.wait_recv()
        for cp in sends:
            cp.wait_send()

    return pl.pallas_call(
        body, name=name,
        in_specs=[HBM_SPEC] * W, out_specs=[HBM_SPEC] * W,
        out_shape=[jax.ShapeDtypeStruct(l.shape, l.dtype) for l in lands],
        input_output_aliases={i: i for i in range(W)},
        scratch_shapes=[pltpu.SemaphoreType.DMA((3 * W,)), pltpu.SemaphoreType.DMA((3 * W,))],
    )(*lands)


def _with_own_slot(gathered, shard):
    return lax.dynamic_update_index_in_dim(gathered, shard[None], _linear(_my_place()), axis=0)


def _in_chip_copies(w, p_ref, land_ref):
    x, y, c = _my_place()
    return [(p_ref.at[2 * q + (1 - c)], land_ref.at[q], (x, y, 1 - c)) for q in range(4)]


def _in_chip_start(parts, tag):
    lands = [_landing_zone((4,) + p.shape[1:], p.dtype) for p in parts]
    return _split_start("grad_in_chip_start_" + tag, parts, lands, (), _in_chip_copies, 4)


def _reduce_scatter_begin(parts, tag, in_chip_state=None, after=()):
    parts, early, got = list(parts), [], []
    if in_chip_state is not None:
        early, got = _split_wait("grad_in_chip_wait_" + tag, in_chip_state, after, _in_chip_copies)
    if parts:
        got = got + list(_exchange_in_chip(parts))
    parts = early + parts
    core = lax.axis_index("c").astype(jnp.int32).reshape(1)
    chip_parts = [_pair_sum(p, g, core) for p, g in zip(parts, got)]
    lands = [_landing_zone(p.shape, p.dtype) for p in chip_parts]
    return _split_start("grad_scatter_start_" + tag, chip_parts, lands, got[0], _scatter_copies, 3)


def _reduce_scatter_end(state, after, tag):
    return _split_wait("grad_scatter_wait_" + tag, state, after, _scatter_copies)


def kernel(x, c, positions, w_ada, b_ada, w_in, g_q_a, w_q_b, g_kv_a, w_kv_b, w_o_a, w_conv, w_o_b, w_o, ln1_g, ln1_b, w_ffn_in, w_ffn_out, ln2_g, ln2_b, loss_target, m_w_ada, m_b_ada, m_w_in, m_g_q_a, m_w_q_b, m_g_kv_a, m_w_kv_b, m_w_o_a, m_w_conv, m_w_o_b, m_w_o, m_ln1_g, m_ln1_b, m_w_ffn_in, m_w_ffn_out, m_ln2_g, m_ln2_b, v_w_ada, v_b_ada, v_w_in, v_g_q_a, v_w_q_b, v_g_kv_a, v_w_kv_b, v_w_o_a, v_w_conv, v_w_o_b, v_w_o, v_ln1_g, v_ln1_b, v_w_ffn_in, v_w_ffn_out, v_ln2_g, v_ln2_b):
    x2, tgt = x[0], loss_target[0]
    S, D = x2.shape
    Lq, Lkv = g_q_a.shape[1], g_kv_a.shape[1]
    H = w_q_b.shape[2] * N_DEV // QK_CAT
    F = w_ffn_out.shape[1] * N_DEV
    assert Lq == Lkv and (Lq + Lkv) % COL_BLOCK == 0 and D % COL_BLOCK == 0
    front = Lq + Lkv + QK_ROPE
    front_pad = _round_up(front, COL_BLOCK)
    kr_blk = (Lq + Lkv) // COL_BLOCK
    blk_b = front_pad // COL_BLOCK
    nblk = D // COL_BLOCK
    blk_c, blk_x, blk_ga, blk_gb = blk_b + nblk, blk_b + 2 * nblk, blk_b + 3 * nblk, blk_b + 4 * nblk
    ts = _tile(S, 256, 8)
    T = _tile(S, min(512, S // 2), CHUNK)
    tb = _tile(F, 2816)
    me = _linear(_my_place())

    cw = w_ada.shape[2]
    b_mine = lax.dynamic_slice(b_ada, (0, me * cw), (1, cw)).reshape(1, 1, cw)
    mod_blocks, cact_all, wconv_all = _ada_fwd(c.reshape(1, 1, D), w_conv[0].reshape(1, 1, -1), w_ada[0], b_mine)
    mod = mod_blocks.reshape(6, D)
    cact_all = cact_all.reshape(N_DEV, D)
    w_conv_full = wconv_all.reshape(N_DEV, CONV_K, -1).transpose(1, 0, 2).reshape(CONV_K, D)

    landing = lambda shards: [_landing_zone((N_DEV,) + s.shape, BF16) for s in shards]
    gathered = lambda lands, shards, tag: [_with_own_slot(g, s) for g, s in
                                           zip(_gather_forward(lands, tag + "_gather_forward"), shards)]
    half = D // 2
    w_in_b = w_in[0].astype(BF16)
    first, second = [w_in_b[:half]], [w_in_b[half:], w_q_b[0].astype(BF16), w_kv_b[0].astype(BF16)]
    mid = [w[0].astype(BF16) for w in (w_o_a, w_o_b, w_o)]
    last = [w[0].astype(BF16) for w in (w_ffn_in, w_ffn_out)]
    first_state = _split_start("first_gather_start", first, landing(first), mod_blocks, _gather_copies, 4)
    second_state = _split_start("second_gather_start", second, landing(second), first_state[4], _gather_copies, 4)
    u = _modulate_in(x2, mod, ts)

    first_shards, first_lands = _split_wait("first_gather_wait", first_state, (u, second_state[4]), _gather_copies)
    (g_in_top,) = gathered(first_lands, first_shards, "first")
    w_in_top = _assemble_w_in(g_in_top, front, front_pad, D, 0)
    proj_top = _matmul(u, w_in_top, "nn", BF16, "proj_top", k_rows=(0, half))
    second_shards, second_lands = _split_wait("second_gather_wait", second_state, (proj_top,), _gather_copies)
    g_in_bottom, wq_s, wkv_s = gathered(second_lands, second_shards, "second")
    mid_state = _split_start("mid_gather_start", mid, landing(mid), g_in_bottom, _gather_copies, 4)
    last_state = _split_start("last_gather_start", last, landing(last), mid_state[4], _gather_copies, 4)
    w_in_p = _assemble_w_in(g_in_bottom, front, front_pad, D, half, into=w_in_top)

    inv_freq = 1.0 / (ROPE_THETA ** (jnp.arange(0, QK_ROPE, 2, dtype=F32) / QK_ROPE))
    ang = positions[0].astype(F32)[:, None] * inv_freq
    cos2 = jnp.concatenate([jnp.cos(ang), jnp.cos(ang)], axis=-1)
    sin2 = jnp.concatenate([jnp.sin(ang), jnp.sin(ang)], axis=-1)
    one, zero = jnp.ones((S, QK_NOPE), F32), jnp.zeros((S, QK_NOPE), F32)
    cos_q, sin_q = jnp.concatenate([one, cos2, one, cos2], axis=-1), jnp.concatenate([zero, sin2, zero, sin2], axis=-1)
    cos_k, sin_k = jnp.tile(cos2, (1, COL_BLOCK // QK_ROPE)), jnp.tile(sin2, (1, COL_BLOCK // QK_ROPE))

    proj = _matmul(u, w_in_p, "nn", BF16, "proj", k_rows=(half, half), init=proj_top, deps=(last_state[4],))
    qn = _rms_fwd(proj, g_q_a, 0, Lq, ts, "rms_q")
    kvn = _rms_fwd(proj, g_kv_a, 1, Lkv, ts, "rms_kv")
    q = _matmul(qn, wq_s, "nn", BF16, "q_up")
    kv = _matmul(kvn, wkv_s, "nn", BF16, "kv_up")
    qc, kc, vh = _qk_prep(q, kv, proj, kr_blk, cos_q, sin_q, cos_k, sin_k, H, ts)
    attn, lse = _attn_fwd(qc, kc, vh, T)
    mid_shards, mid_lands = _split_wait("mid_gather_wait", mid_state, lse, _gather_copies)
    w_oa_f, w_ob_f, w_o_f = [g.reshape(-1, D) for g in gathered(mid_lands, mid_shards, "mid")]
    ya = _matmul(attn, w_oa_f, "nn", BF16, "attn_out")
    cbc = _conv_fwd(proj, w_conv_full, blk_b, blk_c, blk_x)
    yb = _matmul(cbc, w_ob_f, "nn", BF16, "conv_out")
    merged = _merge_fwd(proj, ya, yb, blk_ga, blk_gb, ts)
    mix = _matmul(merged, w_o_f, "nn", F32, "mix_out")
    xhat1, rstd1, u2 = _ln1_fwd(x2, mix, mod, ln1_g, ln1_b, ts)
    last_shards, last_lands = _split_wait("last_gather_wait", last_state, u2, _gather_copies)
    w_fi_s, g_fo = gathered(last_lands, last_shards, "last")
    w_fo_f = g_fo.reshape(F, D)
    hh = _matmul(u2, w_fi_s, "nn", BF16, "ffn_in")
    act = _swiglu_fwd(hh, ts, tb)
    ffn = _matmul(act, w_fo_f, "nn", F32, "ffn_out")
    loss_part, dffn, dx1a, vec2 = _ln2_loss(xhat1, ffn, tgt, mod, ln1_g, ln1_b, ln2_g, ln2_b, ts)
    loss = lax.psum(loss_part[0, 0], AXES)

    gw_fo = _matmul(act, dffn, "tn", BF16, "grad_w_ffn_out")
    da = _matmul(dffn, w_fo_f, "nt", BF16, "d_act")
    dh = _swiglu_bwd(da, hh, ts, tb)
    gw_fi = _matmul(u2, dh, "tn", BF16, "grad_w_ffn_in", out_shards=True)
    ffn_in_chip = _in_chip_start([gw_fi, gw_fo.reshape(N_DEV, -1, D)], "ffn")
    du2 = _matmul(dh, w_fi_s, "nt", F32, "d_u2", deps=(ffn_in_chip[4],))
    ffn_state = _reduce_scatter_begin([], "ffn", ffn_in_chip, after=(du2,))
    dxa, dmix, vec1 = _ln1_bwd(du2, dx1a, xhat1, rstd1, mix, mod, ln1_g, ln1_b, ts)
    gw_o = _matmul(merged, dmix, "tn", BF16, "grad_w_o", deps=(ffn_state[4],))
    dmerged = _matmul(dmix, w_o_f, "nt", BF16, "d_merged")
    dya, dyb, dga, dgb = _merge_bwd(dmerged, proj, ya, yb, blk_ga, blk_gb, ts)
    gw_ob = _matmul(cbc, dyb, "tn", BF16, "grad_w_o_b")
    dcbc = _matmul(dyb, w_ob_f, "nt", BF16, "d_conv")
    dcb, dcc, dcx, dwconv = _conv_bwd(dcbc, proj, w_conv_full, blk_b, blk_c, blk_x)
    gw_oa = _matmul(attn, dya, "tn", BF16, "grad_w_o_a")
    mix_in_chip = _in_chip_start([g.reshape(N_DEV, -1, D) for g in (gw_oa, gw_ob, gw_o)], "mix")
    dattn = _matmul(dya, w_oa_f, "nt", BF16, "d_attn", deps=(mix_in_chip[4],))
    dqc, dkc, dvh = _attn_bwd(qc, kc, vh, dattn, attn, lse, T)
    ffn_own, ffn_got = _reduce_scatter_end(ffn_state, dqc, "ffn")
    dq, dkv, dkr = _qk_bwd(dqc, dkc, dvh, cos_q, sin_q, cos_k, sin_k, ts)
    gw_qb = _matmul(qn, dq, "tn", BF16, "grad_w_q_b", out_shards=True)
    gw_kvb = _matmul(kvn, dkv, "tn", BF16, "grad_w_kv_b", out_shards=True)
    mix_state = _reduce_scatter_begin([gw_qb, gw_kvb], "mix", mix_in_chip, after=(dqc,))
    dqn = _matmul(dq, wq_s, "nt", F32, "d_qn", deps=(mix_state[4],))
    dkvn = _matmul(dkv, wkv_s, "nt", F32, "d_kvn")
    dqa, dgq = _rms_bwd(dqn, proj, g_q_a, 0, Lq, ts, "rms_q_bwd")
    dkva, dgkv = _rms_bwd(dkvn, proj, g_kv_a, 1, Lkv, ts, "rms_kv_bwd")
    dproj = jnp.concatenate([dqa, dkva, dkr, dcb, dcc, dcx, dga, dgb], axis=1)
    gw_in_p = _matmul(u, dproj, "tn", BF16, "grad_w_in")
    mix_own, mix_got = _reduce_scatter_end(mix_state, gw_in_p, "mix")
    in_state = _reduce_scatter_begin([_split_w_in(gw_in_p, front, front_pad)], "in")
    du = _matmul(dproj, w_in_p, "nt", F32, "d_u", deps=(in_state[4],))
    grad_x, vec0 = _grad_x(du, dxa, x2, mod, ts)

    my_chip = (2 * lax.axis_index("x") + lax.axis_index("y")).astype(jnp.int32).reshape(1)
    arrived = {}
    for nm, w, m, v, own, got in (
            ("w_ffn_in", w_ffn_in, m_w_ffn_in, v_w_ffn_in, ffn_own[0], ffn_got[0]),
            ("w_ffn_out", w_ffn_out, m_w_ffn_out, v_w_ffn_out, ffn_own[1], ffn_got[1]),
            ("w_o_a", w_o_a, m_w_o_a, v_w_o_a, mix_own[0], mix_got[0]),
            ("w_o_b", w_o_b, m_w_o_b, v_w_o_b, mix_own[1], mix_got[1]),
            ("w_o", w_o, m_w_o, v_w_o, mix_own[2], mix_got[2]),
            ("w_q_b", w_q_b, m_w_q_b, v_w_q_b, mix_own[3], mix_got[3]),
            ("w_kv_b", w_kv_b, m_w_kv_b, v_w_kv_b, mix_own[4], mix_got[4])):
        arrived[nm] = [a[None] for a in _adamw_reduced(w[0], own, got, m[0], v[0], my_chip, "adamw_" + nm)]

    dmod = jnp.concatenate([vec0[0], vec0[1], vec1[4], vec1[0], vec1[1], vec2[2]])
    small = jnp.concatenate([dmod, dgq[0], dgkv[0], vec1[2], vec1[3], vec2[0], vec2[1], dwconv[:CONV_K].reshape(-1)])
    n_small = small.shape[0]
    nch = _round_up(n_small, cw) // cw
    payload = jnp.pad(small, (0, nch * cw - n_small)).reshape(nch, 1, cw)
    in_own, in_got = _reduce_scatter_end(in_state, [res[1] for res in arrived.values()] + [grad_x, payload], "in")
    arrived["w_in"] = [a[None] for a in _adamw_reduced(w_in[0], in_own[0], in_got[0], m_w_in[0], v_w_in[0], my_chip,
                                                       "adamw_w_in")]
    summed, dmod_mine = _ada_bwd(payload, deps=[arrived["w_in"][1]])
    arrived["w_ada"] = [a[None] for a in _adamw_ada(w_ada[0], cact_all.T, dmod_mine.reshape(N_DEV, cw),
                                                    m_w_ada[0], v_w_ada[0])]
    summed = summed.reshape(-1)
    offs = [0, 6 * D, 6 * D + Lq, 6 * D + Lq + Lkv]
    offs += [offs[-1] + D * k for k in range(1, 5)]
    g_b_ada = summed[offs[0]:offs[1]].reshape(1, -1)
    g_gq = summed[offs[1]:offs[2]].reshape(1, -1)
    g_gkv = summed[offs[2]:offs[3]].reshape(1, -1)
    g_ln1g, g_ln1b, g_ln2g, g_ln2b = [summed[offs[3 + k]:offs[4 + k]].reshape(1, -1) for k in range(4)]
    wc = w_conv.shape[2]
    g_wconv = lax.dynamic_slice(summed[offs[7]:offs[7] + CONV_K * D].reshape(CONV_K, D), (0, me * wc), (CONV_K, wc))

    names = ["w_ada", "b_ada", "w_in", "g_q_a", "w_q_b", "g_kv_a", "w_kv_b", "w_o_a", "w_conv", "w_o_b", "w_o",
             "ln1_g", "ln1_b", "w_ffn_in", "w_ffn_out", "ln2_g", "ln2_b"]
    weights = [w_ada, b_ada, w_in, g_q_a, w_q_b, g_kv_a, w_kv_b, w_o_a, w_conv, w_o_b, w_o, ln1_g, ln1_b,
               w_ffn_in, w_ffn_out, ln2_g, ln2_b]
    moms = [m_w_ada, m_b_ada, m_w_in, m_g_q_a, m_w_q_b, m_g_kv_a, m_w_kv_b, m_w_o_a, m_w_conv, m_w_o_b, m_w_o,
            m_ln1_g, m_ln1_b, m_w_ffn_in, m_w_ffn_out, m_ln2_g, m_ln2_b]
    vels = [v_w_ada, v_b_ada, v_w_in, v_g_q_a, v_w_q_b, v_g_kv_a, v_w_kv_b, v_w_o_a, v_w_conv, v_w_o_b, v_w_o,
            v_ln1_g, v_ln1_b, v_w_ffn_in, v_w_ffn_out, v_ln2_g, v_ln2_b]
    grad_of = {"b_ada": g_b_ada, "g_q_a": g_gq, "g_kv_a": g_gkv, "w_conv": g_wconv,
               "ln1_g": g_ln1g, "ln1_b": g_ln1b, "ln2_g": g_ln2g, "ln2_b": g_ln2b}
    state_of = dict(zip(names, zip(weights, moms, vels)))
    results = dict(arrived)

    def update(nm, reduced=None):
        w, m, v = state_of[nm]
        shp = w.shape
        w2 = w.reshape(shp[-2], shp[-1]) if w.ndim == 3 else w
        m2, v2 = m.reshape(w2.shape), v.reshape(w2.shape)
        if reduced is None:
            g2 = grad_of[nm].reshape(w2.shape)
            res = (g2,) + tuple(_adamw(w2, g2, m2, v2, "adamw_" + nm))
        else:
            res = _adamw_reduced(w2, reduced[0], reduced[1], m2, v2, my_chip, "adamw_" + nm)
        results[nm] = [a.reshape(shp) for a in res]

    for nm in grad_of:
        update(nm)
    outs = [[results[nm][k] for nm in names] for k in range(4)]
    return (loss, grad_x.reshape(x.shape), *outs[0], *outs[1], *outs[2], *outs[3])
```

```python
import functools

import jax
import jax.numpy as jnp
from jax import lax
from jax.experimental import pallas as pl
from jax.experimental.pallas import tpu as pltpu

F32 = jnp.float32
BF16 = jnp.bfloat16
MESH_ID = pl.DeviceIdType.MESH
AXES = ("x", "y", "c")
N_DEV = 8

CHUNK = 64
QK_NOPE = 128
QK_ROPE = 64
V_HEAD = 128
QK_CAT = QK_NOPE + QK_ROPE
ROPE_THETA = 10000.0
ATTN_SCALE = (QK_NOPE + QK_ROPE) ** -0.5
CONV_K = 3
DEEPNORM_ALPHA = 2.0 ** 0.25
LN_EPS = 1e-5
RMS_EPS = 1e-6
NEG_INF = -1e30

ADAM_LR = 0.001
ADAM_B1 = 0.9
ADAM_B2 = 0.999
ADAM_EPS = 1e-08
ADAM_WD = 0.01
ADAM_STEP = 10

LANE = 128
COL_BLOCK = 256
PACK_ROW_ALIGN = 16
PAIR_SUM_BLOCK = 1 << 20
WIDE_TN = 1280
VMEM_LIMIT = 48 * 1024 * 1024


def _round_up(n, m):
    return (n + m - 1) // m * m


def _tile(n, pref, align=LANE):
    best = None
    t = align
    while t <= min(n, pref):
        if n % t == 0:
            best = t
        t += align
    return best if best is not None else n


def _cparams(sem=None):
    return pltpu.CompilerParams(dimension_semantics=sem, vmem_limit_bytes=VMEM_LIMIT)


def _sigmoid(x):
    return 0.5 * jnp.tanh(0.5 * x) + 0.5


def _matmul(a, b, mode, out_dtype, name, tm=1024, tn=1024, tk=2048, deps=(), out_shards=False, k_rows=None,
            init=None):
    b_shards = b.ndim == 3
    n = b.shape[2] if b_shards else (b.shape[1] // N_DEV if out_shards else None)
    if mode == "nn":
        (M, K), (K2, N) = a.shape, (b.shape[1], N_DEV * n) if b_shards else b.shape
    elif mode == "nt":
        (M, K), (N, K2) = a.shape, (b.shape[1], N_DEV * n) if b_shards else b.shape
    else:
        (K, M), (K2, N) = a.shape, b.shape
    assert K == K2, (a.shape, b.shape, mode)
    tm = _tile(M, tm)
    tn = n if (mode != "nt" and n is not None) else _tile(N, tn)
    k_row0, k_len = k_rows if k_rows is not None else (0, K)
    tk = n if (mode == "nt" and b_shards) else _tile(k_len, tk)
    nk, k0 = k_len // tk, k_row0 // tk
    if mode == "nn":
        a_spec = pl.BlockSpec((tm, tk), lambda i, j, k: (i, k0 + k))
        b_spec = (pl.BlockSpec((1, tk, n), lambda i, j, k: (j, k, 0)) if b_shards
                  else pl.BlockSpec((tk, tn), lambda i, j, k: (k0 + k, j)))
        dims = (((1,), (0,)), ((), ()))
    elif mode == "nt":
        a_spec = pl.BlockSpec((tm, tk), lambda i, j, k: (i, k))
        b_spec = (pl.BlockSpec((1, tn, n), lambda i, j, k: (k, j, 0)) if b_shards
                  else pl.BlockSpec((tn, tk), lambda i, j, k: (j, k)))
        dims = (((1,), (1,)), ((), ()))
    else:
        a_spec = pl.BlockSpec((tk, tm), lambda i, j, k: (k, i))
        b_spec = pl.BlockSpec((tk, tn), lambda i, j, k: (k, j))
        dims = (((0,), (0,)), ((), ()))
    if out_shards:
        out_spec = pl.BlockSpec((1, tm, n), lambda i, j, k: (j, i, 0))
        out_shape = jax.ShapeDtypeStruct((N_DEV, M, n), out_dtype)
    else:
        out_spec = pl.BlockSpec((tm, tn), lambda i, j, k: (i, j))
        out_shape = jax.ShapeDtypeStruct((M, N), out_dtype)

    def product(a_ref, b_ref):
        b_blk = b_ref[0] if b_shards else b_ref[...]
        return lax.dot_general(a_ref[...].astype(BF16), b_blk.astype(BF16), dims, preferred_element_type=F32)

    def write(o_ref, value):
        if out_shards:
            o_ref[0] = value.astype(o_ref.dtype)
        else:
            o_ref[...] = value.astype(o_ref.dtype)

    def body_whole_k(a_ref, b_ref, *rest):
        value = product(a_ref, b_ref)
        write(rest[-1], value if init is None else value + rest[0][...])

    def body_split_k(a_ref, b_ref, *rest):
        o_ref, acc_ref = rest[-2:]
        k = pl.program_id(2)

        @pl.when(k == 0)
        def _():
            acc_ref[...] = jnp.zeros_like(acc_ref) if init is None else rest[0][...].astype(F32)

        acc_ref[...] += product(a_ref, b_ref)

        @pl.when(k == nk - 1)
        def _():
            write(o_ref, acc_ref[...])

    return pl.pallas_call(
        body_whole_k if nk == 1 else body_split_k, name=name, grid=(M // tm, N // tn, nk),
        in_specs=[a_spec, b_spec] + ([] if init is None else [out_spec]) + [ANY_SPEC] * len(deps),
        out_specs=out_spec, out_shape=out_shape,
        scratch_shapes=[] if nk == 1 else [pltpu.VMEM((tm, tn), F32)],
        compiler_params=_cparams(("parallel", "parallel", "arbitrary")),
    )(a, b, *(() if init is None else (init,)), *deps)


def _assemble_w_in(shards, front, front_pad, rows, row0, into=None):
    _, K, n = shards.shape
    gap = front_pad - front
    tk = _tile(K, 256, PACK_ROW_ALIGN)
    blk0 = row0 // tk

    def body(g_ref, *rest):
        o_ref = rest[-1]
        if gap:
            o_ref[:, front:front_pad] = jnp.zeros((tk, gap), o_ref.dtype)
        for j in range(N_DEV):
            lo, hi = j * n, (j + 1) * n
            if lo < front < hi:
                o_ref[:, lo:front] = g_ref[j, :, 0:front - lo]
                o_ref[:, front_pad:hi + gap] = g_ref[j, :, front - lo:n]
            else:
                off = 0 if hi <= front else gap
                o_ref[:, lo + off:hi + off] = g_ref[j]

    return pl.pallas_call(
        body, name="assemble_w_in", grid=(K // tk,),
        in_specs=[pl.BlockSpec((N_DEV, tk, n), lambda i: (0, i, 0))] + ([] if into is None else [ANY_SPEC]),
        out_specs=pl.BlockSpec((tk, N_DEV * n + gap), lambda i: (blk0 + i, 0)),
        out_shape=jax.ShapeDtypeStruct((rows, N_DEV * n + gap), shards.dtype),
        input_output_aliases={} if into is None else {1: 0},
        compiler_params=_cparams(("parallel",)),
    )(*([shards] if into is None else [shards, into]))


def _split_w_in(w, front, front_pad):
    K, NP = w.shape
    gap = front_pad - front
    n = (NP - gap) // N_DEV
    tk = _tile(K, 256, PACK_ROW_ALIGN)

    def body(w_ref, o_ref):
        for j in range(N_DEV):
            lo, hi = j * n, (j + 1) * n
            if lo < front < hi:
                o_ref[j, :, 0:front - lo] = w_ref[:, lo:front]
                o_ref[j, :, front - lo:n] = w_ref[:, front_pad:hi + gap]
            else:
                off = 0 if hi <= front else gap
                o_ref[j] = w_ref[:, lo + off:hi + off]

    return pl.pallas_call(
        body, name="split_grad_w_in", grid=(K // tk,),
        in_specs=[pl.BlockSpec((tk, NP), lambda i: (i, 0))],
        out_specs=pl.BlockSpec((N_DEV, tk, n), lambda i: (0, i, 0)),
        out_shape=jax.ShapeDtypeStruct((N_DEV, K, n), w.dtype),
        compiler_params=_cparams(("parallel",)),
    )(w)


def _modulate_in(x, mod, ts):
    S, D = x.shape

    def body(x_ref, mod_ref, u_ref):
        u_ref[...] = (x_ref[...] * (1.0 + mod_ref[1:2, :]) + mod_ref[0:1, :]).astype(BF16)

    return pl.pallas_call(
        body, name="modulate_in", grid=(S // ts,),
        in_specs=[pl.BlockSpec((ts, D), lambda i: (i, 0)), pl.BlockSpec((6, D), lambda i: (0, 0))],
        out_specs=pl.BlockSpec((ts, D), lambda i: (i, 0)),
        out_shape=jax.ShapeDtypeStruct((S, D), BF16),
        compiler_params=_cparams(("parallel",)),
    )(x, mod)


def _rms_fwd(proj, g, blk, L, ts, name):
    S = proj.shape[0]

    def body(a_ref, g_ref, y_ref):
        a = a_ref[...].astype(F32)
        r = lax.rsqrt(jnp.mean(a * a, axis=-1, keepdims=True) + RMS_EPS)
        y_ref[...] = (a * r * g_ref[...]).astype(BF16)

    return pl.pallas_call(
        body, name=name, grid=(S // ts,),
        in_specs=[pl.BlockSpec((ts, L), lambda i: (i, blk)), pl.BlockSpec((1, L), lambda i: (0, 0))],
        out_specs=pl.BlockSpec((ts, L), lambda i: (i, 0)),
        out_shape=jax.ShapeDtypeStruct((S, L), BF16),
        compiler_params=_cparams(("parallel",)),
    )(proj, g)


def _rope_partner(x, period, start):
    w = x.shape[-1]
    lane = lax.broadcasted_iota(jnp.int32, x.shape, x.ndim - 1) % period
    first = (lane >= start) & (lane < start + QK_ROPE // 2)
    from_right = pltpu.roll(x, w - QK_ROPE // 2, axis=x.ndim - 1)
    from_left = pltpu.roll(x, QK_ROPE // 2, axis=x.ndim - 1)
    return jnp.where(first, -from_right, from_left)


def _qk_prep(q, kv, proj, kr_blk, cos_q, sin_q, cos_k, sin_k, H, ts):
    S = q.shape[0]
    pair = 2 * QK_CAT
    kv_w = QK_NOPE + V_HEAD

    def body(q_ref, kv_ref, kr_ref, cq_ref, sq_ref, ck_ref, sk_ref, qc_ref, kc_ref, vh_ref):
        kr = kr_ref[...].astype(F32)
        kr = kr * ck_ref[...] + _rope_partner(kr, QK_ROPE, 0) * sk_ref[...]
        kr = kr[:, :QK_ROPE].astype(BF16)
        for p in range(H // 2):
            x = q_ref[:, p * pair:(p + 1) * pair].astype(F32)
            x = x * cq_ref[...] + _rope_partner(x, QK_CAT, QK_NOPE) * sq_ref[...]
            qc_ref[2 * p] = x[:, :QK_CAT].astype(BF16)
            qc_ref[2 * p + 1] = x[:, QK_CAT:].astype(BF16)
        for h in range(H):
            kc_ref[h, :, 0:QK_NOPE] = kv_ref[:, h * kv_w:h * kv_w + QK_NOPE].astype(BF16)
            kc_ref[h, :, QK_NOPE:QK_CAT] = kr
            vh_ref[h, :, :] = kv_ref[:, h * kv_w + QK_NOPE:(h + 1) * kv_w].astype(BF16)

    row = lambda w: pl.BlockSpec((ts, w), lambda i: (i, 0))
    return pl.pallas_call(
        body, name="qk_prep", grid=(S // ts,),
        in_specs=[row(H * QK_CAT), row(H * kv_w),
                  pl.BlockSpec((ts, COL_BLOCK), lambda i: (i, kr_blk)),
                  row(pair), row(pair), row(COL_BLOCK), row(COL_BLOCK)],
        out_specs=[pl.BlockSpec((H, ts, QK_CAT), lambda i: (0, i, 0)),
                   pl.BlockSpec((H, ts, QK_CAT), lambda i: (0, i, 0)),
                   pl.BlockSpec((H, ts, V_HEAD), lambda i: (0, i, 0))],
        out_shape=[jax.ShapeDtypeStruct((H, S, QK_CAT), BF16), jax.ShapeDtypeStruct((H, S, QK_CAT), BF16),
                   jax.ShapeDtypeStruct((H, S, V_HEAD), BF16)],
        compiler_params=_cparams(("parallel",)),
    )(q, kv, proj, cos_q, sin_q, cos_k, sin_k)


NT_DIMS = (((1,), (1,)), ((), ()))
TN_DIMS = (((0,), (0,)), ((), ()))


def _diag_mask(T):
    rows = lax.broadcasted_iota(jnp.int32, (T, T), 0) // CHUNK
    cols = lax.broadcasted_iota(jnp.int32, (T, T), 1) // CHUNK
    return cols <= rows


def _attn_fwd(qc, kc, vh, T):
    H, S, _ = qc.shape
    n = S // T

    def body(q_ref, k_ref, v_ref, o_ref, lse_ref):
        q = q_ref[0]

        def block(i):
            L = (i + 1) * T
            s_old = lax.dot_general(q, k_ref[0, 0:i * T, :], NT_DIMS, preferred_element_type=F32) if i else None
            s_diag = lax.dot_general(q, k_ref[0, i * T:L, :], NT_DIMS, preferred_element_type=F32)
            s_diag = jnp.where(_diag_mask(T), s_diag, NEG_INF)
            m = jnp.max(s_diag, axis=-1, keepdims=True)
            if i:
                m = jnp.maximum(m, jnp.max(s_old, axis=-1, keepdims=True))
            p_diag = jnp.exp((s_diag - m) * ATTN_SCALE)
            l = jnp.sum(p_diag, axis=-1, keepdims=True)
            acc = jnp.dot(p_diag.astype(BF16), v_ref[0, i * T:L, :], preferred_element_type=F32)
            if i:
                p_old = jnp.exp((s_old - m) * ATTN_SCALE)
                l = l + jnp.sum(p_old, axis=-1, keepdims=True)
                acc = acc + jnp.dot(p_old.astype(BF16), v_ref[0, 0:i * T, :], preferred_element_type=F32)
            o_ref[...] = (acc / l).astype(o_ref.dtype)
            lse_ref[0] = m * ATTN_SCALE + jnp.log(l)

        for i in range(n):
            pl.when(pl.program_id(1) == i)(functools.partial(block, i))

    return pl.pallas_call(
        body, name="attn_fwd", grid=(H, n),
        in_specs=[pl.BlockSpec((1, T, QK_CAT), lambda h, i: (h, i, 0)),
                  pl.BlockSpec((1, S, QK_CAT), lambda h, i: (h, 0, 0)),
                  pl.BlockSpec((1, S, V_HEAD), lambda h, i: (h, 0, 0))],
        out_specs=[pl.BlockSpec((T, V_HEAD), lambda h, i: (i, h)),
                   pl.BlockSpec((1, T, 1), lambda h, i: (h, i, 0))],
        out_shape=[jax.ShapeDtypeStruct((S, H * V_HEAD), BF16), jax.ShapeDtypeStruct((H, S, 1), F32)],
        compiler_params=_cparams(("parallel", "arbitrary")),
    )(qc, kc, vh)


def _shift_rows(z, k):
    if k == 0:
        return z
    n = z.shape[0]
    row = lax.broadcasted_iota(jnp.int32, z.shape, 0)
    if k > 0:
        return jnp.where(row >= k, pltpu.roll(z, k, axis=0), 0.0)
    return jnp.where(row < n + k, pltpu.roll(z, n + k, axis=0), 0.0)


def _conv_fwd(proj, w_conv, blk_b, blk_c, blk_x):
    S = proj.shape[0]
    D = w_conv.shape[1]
    nb = D // COL_BLOCK

    def body(cb_ref, cc_ref, cx_ref, w_ref, o_ref):
        z = cc_ref[...].astype(F32) * cx_ref[...].astype(F32)
        conv = w_ref[2:3, :] * z + w_ref[1:2, :] * _shift_rows(z, 1) + w_ref[0:1, :] * _shift_rows(z, 2)
        o_ref[...] = (cb_ref[...].astype(F32) * conv).astype(BF16)

    col = lambda off: pl.BlockSpec((S, COL_BLOCK), lambda j: (0, off + j))
    return pl.pallas_call(
        body, name="conv_fwd", grid=(nb,),
        in_specs=[col(blk_b), col(blk_c), col(blk_x), pl.BlockSpec((CONV_K, COL_BLOCK), lambda j: (0, j))],
        out_specs=pl.BlockSpec((S, COL_BLOCK), lambda j: (0, j)),
        out_shape=jax.ShapeDtypeStruct((S, D), BF16),
        compiler_params=_cparams(("parallel",)),
    )(proj, proj, proj, w_conv)


def _merge_fwd(proj, ya, yb, blk_ga, blk_gb, ts):
    S, D = ya.shape
    nb = D // COL_BLOCK

    def body(ga_ref, gb_ref, ya_ref, yb_ref, o_ref):
        sa, sb = _sigmoid(ga_ref[...].astype(F32)), _sigmoid(gb_ref[...].astype(F32))
        o_ref[...] = (sa * ya_ref[...].astype(F32) + sb * yb_ref[...].astype(F32)).astype(BF16)

    row = pl.BlockSpec((ts, D), lambda i: (i, 0))
    seg = lambda blk: pl.BlockSpec((pl.Element(ts), pl.Element(D)), lambda i: (i * ts, blk * COL_BLOCK))
    return pl.pallas_call(
        body, name="merge_fwd", grid=(S // ts,),
        in_specs=[seg(blk_ga), seg(blk_gb), row, row],
        out_specs=row,
        out_shape=jax.ShapeDtypeStruct((S, D), BF16),
        compiler_params=_cparams(("parallel",)),
    )(proj, proj, ya, yb)


def _ln1_fwd(x, mix, mod, g, b, ts):
    S, D = x.shape

    def body(x_ref, mix_ref, mod_ref, g_ref, b_ref, xhat_ref, rstd_ref, u2_ref):
        r = DEEPNORM_ALPHA * x_ref[...] + mod_ref[2:3, :] * mix_ref[...]
        mu = jnp.mean(r, axis=-1, keepdims=True)
        d = r - mu
        rstd = lax.rsqrt(jnp.mean(d * d, axis=-1, keepdims=True) + LN_EPS)
        xhat = d * rstd
        xhat_ref[...] = xhat
        rstd_ref[...] = rstd
        x1 = xhat * g_ref[...] + b_ref[...]
        u2_ref[...] = (x1 * (1.0 + mod_ref[4:5, :]) + mod_ref[3:4, :]).astype(BF16)

    row = pl.BlockSpec((ts, D), lambda i: (i, 0))
    vec = lambda r: pl.BlockSpec((r, D), lambda i: (0, 0))
    return pl.pallas_call(
        body, name="ln1_fwd", grid=(S // ts,),
        in_specs=[row, row, vec(6), vec(1), vec(1)],
        out_specs=[row, pl.BlockSpec((ts, 1), lambda i: (i, 0)), row],
        out_shape=[jax.ShapeDtypeStruct((S, D), F32), jax.ShapeDtypeStruct((S, 1), F32),
                   jax.ShapeDtypeStruct((S, D), BF16)],
        compiler_params=_cparams(("parallel",)),
    )(x, mix, mod, g, b)


def _swiglu_fwd(h, ts, tb):
    S, F2 = h.shape
    F = F2 // 2
    nb = F // tb

    def body(hg_ref, hu_ref, a_ref):
        hg = hg_ref[...].astype(F32)
        a_ref[...] = (hg * _sigmoid(hg) * hu_ref[...].astype(F32)).astype(BF16)

    return pl.pallas_call(
        body, name="swiglu_fwd", grid=(S // ts, nb),
        in_specs=[pl.BlockSpec((ts, tb), lambda i, j: (i, j)), pl.BlockSpec((ts, tb), lambda i, j: (i, j + nb))],
        out_specs=pl.BlockSpec((ts, tb), lambda i, j: (i, j)),
        out_shape=jax.ShapeDtypeStruct((S, F), BF16),
        compiler_params=_cparams(("parallel", "parallel")),
    )(h, h)


def _ln2_loss(xhat1, ffn, tgt, mod, g1, b1, g2, b2, ts):
    S, D = xhat1.shape

    def body(xh_ref, ffn_ref, t_ref, mod_ref, g1_ref, b1_ref, g2_ref, b2_ref, loss_ref, dffn_ref, dx1_ref, vec_ref):
        i = pl.program_id(0)

        @pl.when(i == 0)
        def _():
            loss_ref[...] = jnp.zeros_like(loss_ref)
            vec_ref[...] = jnp.zeros_like(vec_ref)

        x1 = xh_ref[...] * g1_ref[...] + b1_ref[...]
        ffn = ffn_ref[...]
        r = DEEPNORM_ALPHA * x1 + mod_ref[5:6, :] * ffn
        mu = jnp.mean(r, axis=-1, keepdims=True)
        d = r - mu
        rstd = lax.rsqrt(jnp.mean(d * d, axis=-1, keepdims=True) + LN_EPS)
        xhat = d * rstd
        e = xhat * g2_ref[...] + b2_ref[...] - t_ref[...]
        loss_ref[...] += 0.5 * jnp.sum(jnp.mean(e * e, axis=-1, keepdims=True))
        dy = e * (1.0 / D)
        dxhat = dy * g2_ref[...]
        dr = rstd * (dxhat - jnp.mean(dxhat, axis=-1, keepdims=True)
                     - xhat * jnp.mean(dxhat * xhat, axis=-1, keepdims=True))
        dffn_ref[...] = (dr * mod_ref[5:6, :]).astype(BF16)
        dx1_ref[...] = DEEPNORM_ALPHA * dr
        vec_ref[0:1, :] += jnp.sum(dy * xhat, axis=0, keepdims=True)
        vec_ref[1:2, :] += jnp.sum(dy, axis=0, keepdims=True)
        vec_ref[2:3, :] += jnp.sum(dr * ffn, axis=0, keepdims=True)

    row = pl.BlockSpec((ts, D), lambda i: (i, 0))
    vec = lambda r: pl.BlockSpec((r, D), lambda i: (0, 0))
    return pl.pallas_call(
        body, name="ln2_loss", grid=(S // ts,),
        in_specs=[row, row, row, vec(6), vec(1), vec(1), vec(1), vec(1)],
        out_specs=[pl.BlockSpec((1, LANE), lambda i: (0, 0)), row, row, vec(8)],
        out_shape=[jax.ShapeDtypeStruct((1, LANE), F32), jax.ShapeDtypeStruct((S, D), BF16),
                   jax.ShapeDtypeStruct((S, D), F32), jax.ShapeDtypeStruct((8, D), F32)],
        compiler_params=_cparams(("arbitrary",)),
    )(xhat1, ffn, tgt, mod, g1, b1, g2, b2)


def _swiglu_bwd(da, h, ts, tb):
    S, F2 = h.shape
    nb = (F2 // 2) // tb

    def body(da_ref, hg_ref, hu_ref, dh_ref):
        hg, da = hg_ref[...].astype(F32), da_ref[...].astype(F32)
        sg = _sigmoid(hg)

        @pl.when(pl.program_id(2) == 0)
        def _():
            dh_ref[...] = (da * hu_ref[...].astype(F32) * (sg * (1.0 + hg * (1.0 - sg)))).astype(BF16)

        @pl.when(pl.program_id(2) == 1)
        def _():
            dh_ref[...] = (da * hg * sg).astype(BF16)

    lo = pl.BlockSpec((ts, tb), lambda i, j, k: (i, j))
    hi = pl.BlockSpec((ts, tb), lambda i, j, k: (i, j + nb))
    return pl.pallas_call(
        body, name="swiglu_bwd", grid=(S // ts, nb, 2),
        in_specs=[lo, lo, hi],
        out_specs=pl.BlockSpec((ts, tb), lambda i, j, k: (i, j + nb * k)),
        out_shape=jax.ShapeDtypeStruct((S, F2), BF16),
        compiler_params=_cparams(("parallel", "parallel", "arbitrary")),
    )(da, h, h)


def _ln1_bwd(du2, dx1a, xhat1, rstd1, mix, mod, g1, b1, ts):
    S, D = xhat1.shape

    def body(du2_ref, dx1a_ref, xh_ref, rstd_ref, mix_ref, mod_ref, g_ref, b_ref, dxa_ref, dmix_ref, vec_ref):
        i = pl.program_id(0)

        @pl.when(i == 0)
        def _():
            vec_ref[...] = jnp.zeros_like(vec_ref)

        xhat, du2, mix = xh_ref[...], du2_ref[...], mix_ref[...]
        x1 = xhat * g_ref[...] + b_ref[...]
        dx1 = dx1a_ref[...] + du2 * (1.0 + mod_ref[4:5, :])
        dxhat = dx1 * g_ref[...]
        dr = rstd_ref[...] * (dxhat - jnp.mean(dxhat, axis=-1, keepdims=True)
                              - xhat * jnp.mean(dxhat * xhat, axis=-1, keepdims=True))
        dxa_ref[...] = DEEPNORM_ALPHA * dr
        dmix_ref[...] = (dr * mod_ref[2:3, :]).astype(BF16)
        vec_ref[0:1, :] += jnp.sum(du2, axis=0, keepdims=True)
        vec_ref[1:2, :] += jnp.sum(du2 * x1, axis=0, keepdims=True)
        vec_ref[2:3, :] += jnp.sum(dx1 * xhat, axis=0, keepdims=True)
        vec_ref[3:4, :] += jnp.sum(dx1, axis=0, keepdims=True)
        vec_ref[4:5, :] += jnp.sum(dr * mix, axis=0, keepdims=True)

    row = pl.BlockSpec((ts, D), lambda i: (i, 0))
    vec = lambda r: pl.BlockSpec((r, D), lambda i: (0, 0))
    return pl.pallas_call(
        body, name="ln1_bwd", grid=(S // ts,),
        in_specs=[row, row, row, pl.BlockSpec((ts, 1), lambda i: (i, 0)), row, vec(6), vec(1), vec(1)],
        out_specs=[row, row, vec(8)],
        out_shape=[jax.ShapeDtypeStruct((S, D), F32), jax.ShapeDtypeStruct((S, D), BF16),
                   jax.ShapeDtypeStruct((8, D), F32)],
        compiler_params=_cparams(("arbitrary",)),
    )(du2, dx1a, xhat1, rstd1, mix, mod, g1, b1)


def _merge_bwd(dmerged, proj, ya, yb, blk_ga, blk_gb, ts):
    S, D = ya.shape
    nb = D // COL_BLOCK

    def body(dm_ref, ga_ref, gb_ref, ya_ref, yb_ref, dya_ref, dyb_ref, dga_ref, dgb_ref):
        dm = dm_ref[...].astype(F32)
        sa, sb = _sigmoid(ga_ref[...].astype(F32)), _sigmoid(gb_ref[...].astype(F32))
        dya_ref[...] = (dm * sa).astype(BF16)
        dyb_ref[...] = (dm * sb).astype(BF16)
        dga_ref[...] = (dm * ya_ref[...].astype(F32) * sa * (1.0 - sa)).astype(BF16)
        dgb_ref[...] = (dm * yb_ref[...].astype(F32) * sb * (1.0 - sb)).astype(BF16)

    row = pl.BlockSpec((ts, D), lambda i: (i, 0))
    seg = lambda blk: pl.BlockSpec((pl.Element(ts), pl.Element(D)), lambda i: (i * ts, blk * COL_BLOCK))
    out = jax.ShapeDtypeStruct((S, D), BF16)
    return pl.pallas_call(
        body, name="merge_bwd", grid=(S // ts,),
        in_specs=[row, seg(blk_ga), seg(blk_gb), row, row],
        out_specs=[row] * 4,
        out_shape=[out] * 4,
        compiler_params=_cparams(("parallel",)),
    )(dmerged, proj, proj, ya, yb)


def _conv_bwd(dcbc, proj, w_conv, blk_b, blk_c, blk_x):
    S = proj.shape[0]
    D = w_conv.shape[1]
    nb = D // COL_BLOCK

    def body(d_ref, cb_ref, cc_ref, cx_ref, w_ref, dcb_ref, dcc_ref, dcx_ref, dw_ref):
        d, cc, cx = d_ref[...].astype(F32), cc_ref[...].astype(F32), cx_ref[...].astype(F32)
        z = cc * cx
        z1, z2 = _shift_rows(z, 1), _shift_rows(z, 2)
        conv = w_ref[2:3, :] * z + w_ref[1:2, :] * z1 + w_ref[0:1, :] * z2
        dcb_ref[...] = (d * conv).astype(BF16)
        dconv = d * cb_ref[...].astype(F32)
        dz = w_ref[2:3, :] * dconv + w_ref[1:2, :] * _shift_rows(dconv, -1) + w_ref[0:1, :] * _shift_rows(dconv, -2)
        dcc_ref[...] = (dz * cx).astype(BF16)
        dcx_ref[...] = (dz * cc).astype(BF16)
        dw_ref[...] = jnp.zeros_like(dw_ref)
        dw_ref[0:1, :] = jnp.sum(dconv * z2, axis=0, keepdims=True)
        dw_ref[1:2, :] = jnp.sum(dconv * z1, axis=0, keepdims=True)
        dw_ref[2:3, :] = jnp.sum(dconv * z, axis=0, keepdims=True)

    col = lambda off: pl.BlockSpec((S, COL_BLOCK), lambda j: (0, off + j))
    out = jax.ShapeDtypeStruct((S, D), BF16)
    return pl.pallas_call(
        body, name="conv_bwd", grid=(nb,),
        in_specs=[col(0), col(blk_b), col(blk_c), col(blk_x), pl.BlockSpec((CONV_K, COL_BLOCK), lambda j: (0, j))],
        out_specs=[col(0), col(0), col(0), pl.BlockSpec((8, COL_BLOCK), lambda j: (0, j))],
        out_shape=[out, out, out, jax.ShapeDtypeStruct((8, D), F32)],
        compiler_params=_cparams(("parallel",)),
    )(dcbc, proj, proj, proj, w_conv)


def _attn_bwd(qc, kc, vh, do, o, lse, T):
    H, S, _ = qc.shape
    n = S // T

    def body(q_ref, k_ref, v_ref, do_ref, o_ref, lse_ref, dq_ref, dk_ref, dv_ref, d_ref, dq_acc, dk_acc, dv_acc):
        j = pl.program_id(1)

        @pl.when(j == 0)
        def _():
            dq_acc[...] = jnp.zeros_like(dq_acc)
            d_ref[...] = jnp.sum(do_ref[...].astype(F32) * o_ref[...].astype(F32), axis=-1, keepdims=True)

        dk_acc[...] = jnp.zeros_like(dk_acc)
        dv_acc[...] = jnp.zeros_like(dv_acc)
        k, v = k_ref[0], v_ref[0]

        def step(i, masked):
            rows = pl.ds(pl.multiple_of(i * T, T), T)
            q = q_ref[0, rows, :]
            do = do_ref[rows, :].astype(BF16)
            s = lax.dot_general(q, k, NT_DIMS, preferred_element_type=F32) * ATTN_SCALE
            if masked:
                s = jnp.where(_diag_mask(T), s, NEG_INF)
            p = jnp.exp(s - lse_ref[0, rows, :])
            dv_acc[...] += lax.dot_general(p.astype(BF16), do, TN_DIMS, preferred_element_type=F32)
            dp = lax.dot_general(do, v, NT_DIMS, preferred_element_type=F32)
            ds = (p * (dp - d_ref[rows, :]) * ATTN_SCALE).astype(BF16)
            dk_acc[...] += lax.dot_general(ds, q, TN_DIMS, preferred_element_type=F32)
            dq_acc[rows, :] += jnp.dot(ds, k, preferred_element_type=F32)

        def above(i, carry):
            step(i, False)
            return carry

        step(j, True)
        lax.fori_loop(j + 1, n, above, 0)
        dk_ref[0] = dk_acc[...].astype(BF16)
        dv_ref[0] = dv_acc[...].astype(BF16)

        @pl.when(j == n - 1)
        def _():
            dq_ref[0] = dq_acc[...].astype(BF16)

    head = lambda w: pl.BlockSpec((1, S, w), lambda h, j: (h, 0, 0))
    blk = lambda w: pl.BlockSpec((1, T, w), lambda h, j: (h, j, 0))
    ospec = pl.BlockSpec((S, V_HEAD), lambda h, j: (0, h))
    return pl.pallas_call(
        body, name="attn_bwd", grid=(H, n),
        in_specs=[head(QK_CAT), blk(QK_CAT), blk(V_HEAD), ospec, ospec, head(1)],
        out_specs=[head(QK_CAT), blk(QK_CAT), blk(V_HEAD)],
        out_shape=[jax.ShapeDtypeStruct((H, S, QK_CAT), BF16), jax.ShapeDtypeStruct((H, S, QK_CAT), BF16),
                   jax.ShapeDtypeStruct((H, S, V_HEAD), BF16)],
        scratch_shapes=[pltpu.VMEM((S, 1), F32), pltpu.VMEM((S, QK_CAT), F32), pltpu.VMEM((T, QK_CAT), F32),
                        pltpu.VMEM((T, V_HEAD), F32)],
        compiler_params=_cparams(("parallel", "arbitrary")),
    )(qc, kc, vh, do, o, lse)


def _qk_bwd(dqc, dkc, dvh, cos_q, sin_q, cos_k, sin_k, ts):
    H, S, _ = dqc.shape
    pair = 2 * QK_CAT
    kv_w = QK_NOPE + V_HEAD

    def body(dqc_ref, dkc_ref, dvh_ref, cq_ref, sq_ref, ck_ref, sk_ref, dq_ref, dkv_ref, dkr_ref, q_buf, kr_buf):
        for p in range(H // 2):
            q_buf[:, :QK_CAT] = dqc_ref[2 * p].astype(F32)
            q_buf[:, QK_CAT:] = dqc_ref[2 * p + 1].astype(F32)
            g = q_buf[...]
            dq_ref[:, p * pair:(p + 1) * pair] = (
                g * cq_ref[...] - _rope_partner(g, QK_CAT, QK_NOPE) * sq_ref[...]).astype(BF16)
        kr_sum = jnp.zeros((ts, QK_ROPE), F32)
        for h in range(H):
            dkv_ref[:, h * kv_w:h * kv_w + QK_NOPE] = dkc_ref[h, :, 0:QK_NOPE].astype(BF16)
            dkv_ref[:, h * kv_w + QK_NOPE:(h + 1) * kv_w] = dvh_ref[h].astype(BF16)
            kr_sum = kr_sum + dkc_ref[h, :, QK_NOPE:QK_CAT]
        kr_buf[...] = jnp.zeros_like(kr_buf)
        kr_buf[:, 0:QK_ROPE] = kr_sum
        kr = kr_buf[...]
        dkr_ref[...] = (kr * ck_ref[...] - _rope_partner(kr, QK_ROPE, 0) * sk_ref[...]).astype(BF16)

    row = lambda w: pl.BlockSpec((ts, w), lambda i: (i, 0))
    head = lambda w: pl.BlockSpec((H, ts, w), lambda i: (0, i, 0))
    return pl.pallas_call(
        body, name="qk_bwd", grid=(S // ts,),
        in_specs=[head(QK_CAT), head(QK_CAT), head(V_HEAD), row(pair), row(pair), row(COL_BLOCK), row(COL_BLOCK)],
        out_specs=[row(H * QK_CAT), row(H * kv_w), row(COL_BLOCK)],
        out_shape=[jax.ShapeDtypeStruct((S, H * QK_CAT), BF16), jax.ShapeDtypeStruct((S, H * kv_w), BF16),
                   jax.ShapeDtypeStruct((S, COL_BLOCK), BF16)],
        scratch_shapes=[pltpu.VMEM((ts, pair), F32), pltpu.VMEM((ts, COL_BLOCK), F32)],
        compiler_params=_cparams(("parallel",)),
    )(dqc, dkc, dvh, cos_q, sin_q, cos_k, sin_k)


def _rms_bwd(dy, proj, g, blk, L, ts, name):
    S = proj.shape[0]

    def body(dy_ref, a_ref, g_ref, da_ref, dg_ref):
        i = pl.program_id(0)

        @pl.when(i == 0)
        def _():
            dg_ref[...] = jnp.zeros_like(dg_ref)

        a, dy = a_ref[...].astype(F32), dy_ref[...]
        r = lax.rsqrt(jnp.mean(a * a, axis=-1, keepdims=True) + RMS_EPS)
        dyh = dy * g_ref[...]
        da = r * dyh - a * (r * r * r) * jnp.mean(dyh * a, axis=-1, keepdims=True)
        da_ref[...] = da.astype(BF16)
        dg_ref[0:1, :] += jnp.sum(dy * a * r, axis=0, keepdims=True)

    return pl.pallas_call(
        body, name=name, grid=(S // ts,),
        in_specs=[pl.BlockSpec((ts, L), lambda i: (i, 0)), pl.BlockSpec((ts, L), lambda i: (i, blk)),
                  pl.BlockSpec((1, L), lambda i: (0, 0))],
        out_specs=[pl.BlockSpec((ts, L), lambda i: (i, 0)), pl.BlockSpec((8, L), lambda i: (0, 0))],
        out_shape=[jax.ShapeDtypeStruct((S, L), BF16), jax.ShapeDtypeStruct((8, L), F32)],
        compiler_params=_cparams(("arbitrary",)),
    )(dy, proj, g)


def _grad_x(du, dxa, x, mod, ts):
    S, D = x.shape

    def body(du_ref, dxa_ref, x_ref, mod_ref, dx_ref, vec_ref):
        i = pl.program_id(0)

        @pl.when(i == 0)
        def _():
            vec_ref[...] = jnp.zeros_like(vec_ref)

        du = du_ref[...]
        dx_ref[...] = dxa_ref[...] + du * (1.0 + mod_ref[1:2, :])
        vec_ref[0:1, :] += jnp.sum(du, axis=0, keepdims=True)
        vec_ref[1:2, :] += jnp.sum(du * x_ref[...], axis=0, keepdims=True)

    row = pl.BlockSpec((ts, D), lambda i: (i, 0))
    vec = lambda r: pl.BlockSpec((r, D), lambda i: (0, 0))
    return pl.pallas_call(
        body, name="grad_x", grid=(S // ts,),
        in_specs=[row, row, row, vec(6)],
        out_specs=[row, vec(8)],
        out_shape=[jax.ShapeDtypeStruct((S, D), F32), jax.ShapeDtypeStruct((8, D), F32)],
        compiler_params=_cparams(("arbitrary",)),
    )(du, dxa, x, mod)


def _adamw(w, g, m, v, name):
    R, C = w.shape
    tr = _tile(R, max(8, (1 << 19) // C), 8)
    c1 = 1.0 / (1.0 - ADAM_B1 ** ADAM_STEP)
    c2 = 1.0 / (1.0 - ADAM_B2 ** ADAM_STEP)

    def body(w_ref, g_ref, m_ref, v_ref, d_ref, nm_ref, nv_ref):
        g = g_ref[...]
        m = ADAM_B1 * m_ref[...] + (1.0 - ADAM_B1) * g
        v = ADAM_B2 * v_ref[...] + (1.0 - ADAM_B2) * (g * g)
        nm_ref[...] = m
        nv_ref[...] = v
        d_ref[...] = -ADAM_LR * ((m * c1) / (jnp.sqrt(v * c2) + ADAM_EPS) + ADAM_WD * w_ref[...])

    spec = pl.BlockSpec((tr, C), lambda i: (i, 0))
    out = jax.ShapeDtypeStruct((R, C), F32)
    return pl.pallas_call(
        body, name=name, grid=(R // tr,),
        in_specs=[spec] * 4, out_specs=[spec] * 3, out_shape=[out] * 3,
        compiler_params=_cparams(("parallel",)),
    )(w, g, m, v)


def _adamw_ada(w, cact_t, dmod, m, v):
    R, C = w.shape
    tr = _tile(R, max(8, (1 << 18) // C), 8)
    c1 = 1.0 / (1.0 - ADAM_B1 ** ADAM_STEP)
    c2 = 1.0 / (1.0 - ADAM_B2 ** ADAM_STEP)

    def body(w_ref, ct_ref, dm_ref, m_ref, v_ref, g_ref, d_ref, nm_ref, nv_ref):
        ct = ct_ref[...].astype(BF16).astype(F32)
        dm = dm_ref[...].astype(BF16).astype(F32)
        g = ct[:, 0:1] * dm[0:1, :]
        for b in range(1, N_DEV):
            g = g + ct[:, b:b + 1] * dm[b:b + 1, :]
        m = ADAM_B1 * m_ref[...] + (1.0 - ADAM_B1) * g
        v = ADAM_B2 * v_ref[...] + (1.0 - ADAM_B2) * (g * g)
        g_ref[...] = g
        nm_ref[...] = m
        nv_ref[...] = v
        d_ref[...] = -ADAM_LR * ((m * c1) / (jnp.sqrt(v * c2) + ADAM_EPS) + ADAM_WD * w_ref[...])

    spec = pl.BlockSpec((tr, C), lambda i: (i, 0))
    out = jax.ShapeDtypeStruct((R, C), F32)
    return pl.pallas_call(
        body, name="adamw_w_ada", grid=(R // tr,),
        in_specs=[spec, pl.BlockSpec((tr, N_DEV), lambda i: (i, 0)), pl.BlockSpec((N_DEV, C), lambda i: (0, 0)),
                  spec, spec],
        out_specs=[spec] * 4, out_shape=[out] * 4,
        compiler_params=_cparams(("parallel",)),
    )(w, cact_t, dmod, m, v)


def _adamw_reduced(w, own, got, m, v, my_chip, name):
    R, C = w.shape
    tr = _tile(R, max(PACK_ROW_ALIGN, (1 << 18) // C), PACK_ROW_ALIGN)
    c1 = 1.0 / (1.0 - ADAM_B1 ** ADAM_STEP)
    c2 = 1.0 / (1.0 - ADAM_B2 ** ADAM_STEP)

    def body(chip_ref, w_ref, own_ref, g1_ref, g2_ref, g3_ref, m_ref, v_ref, g_ref, d_ref, nm_ref, nv_ref):
        g = own_ref[0].astype(F32) + g1_ref[0].astype(F32) + g2_ref[0].astype(F32) + g3_ref[0].astype(F32)
        m = ADAM_B1 * m_ref[...] + (1.0 - ADAM_B1) * g
        v = ADAM_B2 * v_ref[...] + (1.0 - ADAM_B2) * (g * g)
        g_ref[...] = g
        nm_ref[...] = m
        nv_ref[...] = v
        d_ref[...] = -ADAM_LR * ((m * c1) / (jnp.sqrt(v * c2) + ADAM_EPS) + ADAM_WD * w_ref[...])

    spec = pl.BlockSpec((tr, C), lambda i, chip: (i, 0))
    slot = lambda k: pl.BlockSpec((1, tr, C), lambda i, chip: (chip[0] ^ k, i, 0))
    out = jax.ShapeDtypeStruct((R, C), F32)
    return pl.pallas_call(
        body, name=name,
        grid_spec=pltpu.PrefetchScalarGridSpec(
            num_scalar_prefetch=1, grid=(R // tr,),
            in_specs=[spec, slot(0), slot(1), slot(2), slot(3), spec, spec],
            out_specs=[spec] * 4),
        out_shape=[out] * 4,
        compiler_params=_cparams(("parallel",)),
    )(my_chip, w, own, got, got, got, m, v)


def _my_place():
    return lax.axis_index("x"), lax.axis_index("y"), lax.axis_index("c")


def _peer(k):
    x, y, c = _my_place()
    return (x ^ ((k >> 2) & 1), y ^ ((k >> 1) & 1), c ^ (k & 1))


def _linear(place):
    return 4 * place[0] + 2 * place[1] + place[2]


def _ada_fwd(c_row, wconv_row, w_ada, b_row):
    D, CW = w_ada.shape
    WC = wconv_row.shape[-1]

    def body(c_ref, wc_ref, w_ref, b_ref, mod_ref, cact_ref, wcall_ref, send_buf, sems):
        me = _linear(_my_place())
        c = c_ref[0]
        cact_ref[me] = c * _sigmoid(c)
        wcall_ref[me] = wc_ref[0]

        def gather_copy(buf, k, grp):
            return pltpu.make_async_remote_copy(
                src_ref=buf.at[me], dst_ref=buf.at[me], send_sem=sems.at[0, grp, k], recv_sem=sems.at[1, grp, k],
                device_id=_peer(k), device_id_type=MESH_ID)

        def gather_recv(buf, k, grp):
            src = _linear(_peer(k))
            return pltpu.make_async_remote_copy(
                src_ref=buf.at[src], dst_ref=buf.at[src], send_sem=sems.at[0, grp, k], recv_sem=sems.at[1, grp, k],
                device_id=_peer(k), device_id_type=MESH_ID)

        for k in range(1, N_DEV):
            gather_copy(cact_ref, k, 0).start()
            gather_copy(wcall_ref, k, 1).start()
        for k in range(1, N_DEV):
            gather_recv(cact_ref, k, 0).wait_recv()
            gather_recv(wcall_ref, k, 1).wait_recv()
        for k in range(1, N_DEV):
            gather_copy(cact_ref, k, 0).wait_send()
            gather_copy(wcall_ref, k, 1).wait_send()

        cact = jnp.concatenate([cact_ref[b] for b in range(N_DEV)], axis=0)
        mod_all = jnp.dot(cact.astype(BF16), w_ref[...].astype(BF16), preferred_element_type=F32) + b_ref[0]
        for b in range(N_DEV):
            send_buf[b] = mod_all[b:b + 1, :]
        mod_ref[me] = send_buf[me]

        def scatter_copy(k):
            dst = _linear(_peer(k))
            return pltpu.make_async_remote_copy(
                src_ref=send_buf.at[dst], dst_ref=mod_ref.at[me], send_sem=sems.at[0, 2, k], recv_sem=sems.at[1, 2, k],
                device_id=_peer(k), device_id_type=MESH_ID)

        def scatter_recv(k):
            src = _linear(_peer(k))
            return pltpu.make_async_remote_copy(
                src_ref=send_buf.at[src], dst_ref=mod_ref.at[src], send_sem=sems.at[0, 2, k], recv_sem=sems.at[1, 2, k],
                device_id=_peer(k), device_id_type=MESH_ID)

        for k in range(1, N_DEV):
            scatter_copy(k).start()
        for k in range(1, N_DEV):
            scatter_recv(k).wait_recv()
        for k in range(1, N_DEV):
            scatter_copy(k).wait_send()

    vmem = pl.BlockSpec(memory_space=pltpu.VMEM)
    return pl.pallas_call(
        body, name="ada_fwd",
        in_specs=[vmem] * 4, out_specs=[vmem] * 3,
        out_shape=[jax.ShapeDtypeStruct((N_DEV, 1, CW), F32), jax.ShapeDtypeStruct((N_DEV, 1, D), F32),
                   jax.ShapeDtypeStruct((N_DEV, 1, WC), F32)],
        scratch_shapes=[pltpu.VMEM((N_DEV, 1, CW), F32), pltpu.SemaphoreType.DMA((2, 3, N_DEV))],
        compiler_params=pltpu.CompilerParams(vmem_limit_bytes=VMEM_LIMIT),
    )(c_row, wconv_row, w_ada, b_row)


def _ada_bwd(payload, deps=()):
    NCH, _, CW = payload.shape

    def body(p_ref, *rest):
        sum_ref, mine_ref, all_ref, sems = rest[-4:]
        me = _linear(_my_place())
        all_ref[me] = p_ref[...]

        def copy(k, slot):
            return pltpu.make_async_remote_copy(
                src_ref=all_ref.at[slot], dst_ref=all_ref.at[slot], send_sem=sems.at[0, k], recv_sem=sems.at[1, k],
                device_id=_peer(k), device_id_type=MESH_ID)

        for k in range(1, N_DEV):
            copy(k, me).start()
        for k in range(1, N_DEV):
            copy(k, _linear(_peer(k))).wait_recv()
        for k in range(1, N_DEV):
            copy(k, me).wait_send()

        total = all_ref[0]
        for b in range(1, N_DEV):
            total = total + all_ref[b]
        sum_ref[...] = total

        for b in range(N_DEV):
            mine_ref[b] = all_ref[b, me]

    vmem = pl.BlockSpec(memory_space=pltpu.VMEM)
    return pl.pallas_call(
        body, name="ada_bwd",
        in_specs=[vmem] + [ANY_SPEC] * len(deps), out_specs=[vmem, vmem],
        out_shape=[jax.ShapeDtypeStruct((NCH, 1, CW), F32), jax.ShapeDtypeStruct((N_DEV, 1, CW), F32)],
        scratch_shapes=[pltpu.VMEM((N_DEV, NCH, 1, CW), F32), pltpu.SemaphoreType.DMA((2, N_DEV))],
        compiler_params=pltpu.CompilerParams(vmem_limit_bytes=VMEM_LIMIT),
    )(payload, *deps)


def _exchange_in_chip(parts):
    W = len(parts)

    def body(*refs):
        p_refs, got_refs, (send_sems, recv_sems) = refs[:W], refs[W:2 * W], refs[2 * W:]
        x, y, c = _my_place()
        sibling = (x, y, 1 - c)
        copies = []
        for w in range(W):
            for q in range(4):
                copies.append(pltpu.make_async_remote_copy(
                    src_ref=p_refs[w].at[2 * q + (1 - c)], dst_ref=got_refs[w].at[q],
                    send_sem=send_sems.at[4 * w + q], recv_sem=recv_sems.at[4 * w + q],
                    device_id=sibling, device_id_type=MESH_ID))
        for cp in copies:
            cp.start()
        for cp in copies:
            cp.wait_recv()
        for cp in copies:
            cp.wait_send()

    return pl.pallas_call(
        body, name="grad_exchange_in_chip",
        in_specs=[HBM_SPEC] * W, out_specs=[HBM_SPEC] * W,
        out_shape=[jax.ShapeDtypeStruct((4,) + p.shape[1:], p.dtype) for p in parts],
        scratch_shapes=[pltpu.SemaphoreType.DMA((4 * W,)), pltpu.SemaphoreType.DMA((4 * W,))],
    )(*parts)


def _pair_sum(parts, got, core):
    _, R, C = parts.shape
    tr = _tile(R, max(PACK_ROW_ALIGN, PAIR_SUM_BLOCK // C), PACK_ROW_ALIGN)

    def body(c_ref, p_ref, g_ref, o_ref):
        o_ref[...] = (p_ref[...].astype(F32) + g_ref[...].astype(F32)).astype(o_ref.dtype)

    return pl.pallas_call(
        body, name="grad_pair_sum",
        grid_spec=pltpu.PrefetchScalarGridSpec(
            num_scalar_prefetch=1, grid=(4, R // tr),
            in_specs=[pl.BlockSpec((1, tr, C), lambda q, i, c_ref: (2 * q + c_ref[0], i, 0)),
                      pl.BlockSpec((1, tr, C), lambda q, i, c_ref: (q, i, 0))],
            out_specs=pl.BlockSpec((1, tr, C), lambda q, i, c_ref: (q, i, 0))),
        out_shape=jax.ShapeDtypeStruct((4, R, C), parts.dtype),
        compiler_params=_cparams(("parallel", "parallel")),
    )(core, parts, got)


HBM_SPEC = pl.BlockSpec(memory_space=pltpu.HBM)
SEM_SPEC = pl.BlockSpec(memory_space=pltpu.SEMAPHORE)
ANY_SPEC = pl.BlockSpec(memory_space=pl.ANY)
SPLIT_EFFECT = pltpu.SideEffectType.DATAFLOW_SIDE_EFFECTING


def _landing_zone(shape, dtype):
    return pltpu.with_memory_space_constraint(lax.empty(shape, dtype), pltpu.HBM)


def _split_start(name, arrays, lands, after, copies_of, per_array):
    W = len(arrays)
    after = tuple(after) if isinstance(after, (tuple, list)) else (after,)

    def body(*refs):
        x_refs, land_refs = refs[:W], refs[W:2 * W]
        send_sems, recv_sems = refs[2 * W + len(after)], refs[2 * W + len(after) + 1]
        token = refs[-1]
        k = 0
        for w in range(W):
            for src, dst, dev in copies_of(w, x_refs[w], land_refs[w]):
                pltpu.make_async_remote_copy(src_ref=src, dst_ref=dst, send_sem=send_sems.at[k], recv_sem=recv_sems.at[k],
                                             device_id=dev, device_id_type=MESH_ID).start()
                k += 1
        token[...] = jnp.zeros_like(token)

    n_copies = per_array * W
    hbm_of = lambda xs: tuple(pltpu.HBM(a.shape, a.dtype) for a in xs)
    out = pl.pallas_call(
        body, name=name,
        out_shape=(pltpu.SemaphoreType.DMA((n_copies,)), pltpu.SemaphoreType.DMA((n_copies,)))
        + hbm_of(arrays) + hbm_of(lands) + (jax.ShapeDtypeStruct((8, LANE), F32),),
        in_specs=(HBM_SPEC,) * (2 * W) + (ANY_SPEC,) * len(after),
        out_specs=(SEM_SPEC, SEM_SPEC) + (HBM_SPEC,) * (2 * W) + (pl.BlockSpec(memory_space=pltpu.VMEM),),
        input_output_aliases={i: 2 + i for i in range(2 * W)},
        compiler_params=pltpu.CompilerParams(has_side_effects=SPLIT_EFFECT),
    )(*[pltpu.with_memory_space_constraint(a, pltpu.HBM) for a in arrays], *lands, *after)
    return out[0], out[1], list(out[2:2 + W]), list(out[2 + W:2 + 2 * W]), out[-1]


def _split_wait(name, state, after, copies_of):
    send_sems, recv_sems, arrays, lands, _ = state
    W = len(arrays)
    after = tuple(after) if isinstance(after, (tuple, list)) else (after,)

    def body(*refs):
        x_refs, land_refs = refs[:W], refs[W:2 * W]
        send_sems, recv_sems = refs[2 * W], refs[2 * W + 1]
        k = 0
        for w in range(W):
            for src, dst, dev in copies_of(w, x_refs[w], land_refs[w]):
                cp = pltpu.make_async_remote_copy(src_ref=src, dst_ref=dst, send_sem=send_sems.at[k],
                                                  recv_sem=recv_sems.at[k], device_id=dev, device_id_type=MESH_ID)
                cp.wait_send()
                cp.wait_recv()
                k += 1

    out = pl.pallas_call(
        body, name=name,
        out_shape=tuple(pltpu.HBM(a.shape, a.dtype) for a in arrays + lands),
        in_specs=(HBM_SPEC,) * (2 * W) + (SEM_SPEC, SEM_SPEC) + (ANY_SPEC,) * len(after),
        out_specs=(HBM_SPEC,) * (2 * W),
        input_output_aliases={i: i for i in range(2 * W)},
        compiler_params=pltpu.CompilerParams(has_side_effects=SPLIT_EFFECT),
    )(*arrays, *lands, send_sems, recv_sems, *after)
    return list(out[:W]), list(out[W:])


def _scatter_copies(w, p_ref, land_ref):
    x, y, c = _my_place()
    my_chip = 2 * x + y
    return [(p_ref.at[2 * (x ^ (k >> 1)) + (y ^ (k & 1))], land_ref.at[my_chip], (x ^ (k >> 1), y ^ (k & 1), c))
            for k in range(1, 4)]


def _gather_copies(w, x_ref, land_ref):
    x, y, c = _my_place()
    me = _linear((x, y, c))
    devs = [(x, y, 1 - c)] + [(x ^ (k >> 1), y ^ (k & 1), c) for k in range(1, 4)]
    return [(x_ref, land_ref.at[me], d) for d in devs]


def _gather_forward(lands, name):
    W = len(lands)

    def body(*refs):
        land_refs, out_refs, (send_sems, recv_sems) = refs[:W], refs[W:2 * W], refs[2 * W:]
        x, y, c = _my_place()
        sibling = (x, y, 1 - c)
        sends, arrivals = [], []
        for w in range(W):
            for k in range(1, 4):
                px, py = x ^ (k >> 1), y ^ (k & 1)
                landed, theirs = _linear((px, py, c)), out_refs[w].at[_linear((px, py, 1 - c))]
                sem = 3 * w + k - 1
                sends.append(pltpu.make_async_remote_copy(
                    src_ref=land_refs[w].at[landed], dst_ref=out_refs[w].at[landed],
                    send_sem=send_sems.at[sem], recv_sem=recv_sems.at[sem], device_id=sibling, device_id_type=MESH_ID))
                arrivals.append(pltpu.make_async_remote_copy(
                    src_ref=theirs, dst_ref=theirs, send_sem=send_sems.at[sem], recv_sem=recv_sems.at[sem],
                    device_id=sibling, device_id_type=MESH_ID))
        for cp in sends:
            cp.start()
        for cp in arrivals:
            cp.wait_recv()
        for cp in sends:
            cp.wait_send()

    return pl.pallas_call(
        body, name=name,
        in_specs=[HBM_SPEC] * W, out_specs=[HBM_SPEC] * W,
        out_shape=[jax.ShapeDtypeStruct(l.shape, l.dtype) for l in lands],
        input_output_aliases={i: i for i in range(W)},
        scratch_shapes=[pltpu.SemaphoreType.DMA((3 * W,)), pltpu.SemaphoreType.DMA((3 * W,))],
    )(*lands)


def _with_own_slot(gathered, shard):
    return lax.dynamic_update_index_in_dim(gathered, shard[None], _linear(_my_place()), axis=0)


def _in_chip_copies(w, p_ref, land_ref):
    x, y, c = _my_place()
    return [(p_ref.at[2 * q + (1 - c)], land_ref.at[q], (x, y, 1 - c)) for q in range(4)]


def _in_chip_start(parts, tag):
    lands = [_landing_zone((4,) + p.shape[1:], p.dtype) for p in parts]
    return _split_start("grad_in_chip_start_" + tag, parts, lands, (), _in_chip_copies, 4)


def _reduce_scatter_begin(parts, tag, in_chip_state=None, after=()):
    parts, early, got = list(parts), [], []
    if in_chip_state is not None:
        early, got = _split_wait("grad_in_chip_wait_" + tag, in_chip_state, after, _in_chip_copies)
    if parts:
        got = got + list(_exchange_in_chip(parts))
    parts = early + parts
    core = lax.axis_index("c").astype(jnp.int32).reshape(1)
    chip_parts = [_pair_sum(p, g, core) for p, g in zip(parts, got)]
    lands = [_landing_zone(p.shape, p.dtype) for p in chip_parts]
    return _split_start("grad_scatter_start_" + tag, chip_parts, lands, got[0], _scatter_copies, 3)


def _reduce_scatter_end(state, after, tag):
    return _split_wait("grad_scatter_wait_" + tag, state, after, _scatter_copies)


def kernel(x, c, positions, w_ada, b_ada, w_in, g_q_a, w_q_b, g_kv_a, w_kv_b, w_o_a, w_conv, w_o_b, w_o, ln1_g, ln1_b, w_ffn_in, w_ffn_out, ln2_g, ln2_b, loss_target, m_w_ada, m_b_ada, m_w_in, m_g_q_a, m_w_q_b, m_g_kv_a, m_w_kv_b, m_w_o_a, m_w_conv, m_w_o_b, m_w_o, m_ln1_g, m_ln1_b, m_w_ffn_in, m_w_ffn_out, m_ln2_g, m_ln2_b, v_w_ada, v_b_ada, v_w_in, v_g_q_a, v_w_q_b, v_g_kv_a, v_w_kv_b, v_w_o_a, v_w_conv, v_w_o_b, v_w_o, v_ln1_g, v_ln1_b, v_w_ffn_in, v_w_ffn_out, v_ln2_g, v_ln2_b):
    x2, tgt = x[0], loss_target[0]
    S, D = x2.shape
    Lq, Lkv = g_q_a.shape[1], g_kv_a.shape[1]
    H = w_q_b.shape[2] * N_DEV // QK_CAT
    F = w_ffn_out.shape[1] * N_DEV
    assert Lq == Lkv and (Lq + Lkv) % COL_BLOCK == 0 and D % COL_BLOCK == 0
    front = Lq + Lkv + QK_ROPE
    front_pad = _round_up(front, COL_BLOCK)
    kr_blk = (Lq + Lkv) // COL_BLOCK
    blk_b = front_pad // COL_BLOCK
    nblk = D // COL_BLOCK
    blk_c, blk_x, blk_ga, blk_gb = blk_b + nblk, blk_b + 2 * nblk, blk_b + 3 * nblk, blk_b + 4 * nblk
    ts = _tile(S, 256, 8)
    T = _tile(S, min(512, S // 2), CHUNK)
    tb = _tile(F, 2816)
    me = _linear(_my_place())

    cw = w_ada.shape[2]
    b_mine = lax.dynamic_slice(b_ada, (0, me * cw), (1, cw)).reshape(1, 1, cw)
    mod_blocks, cact_all, wconv_all = _ada_fwd(c.reshape(1, 1, D), w_conv[0].reshape(1, 1, -1), w_ada[0], b_mine)
    mod = mod_blocks.reshape(6, D)
    cact_all = cact_all.reshape(N_DEV, D)
    w_conv_full = wconv_all.reshape(N_DEV, CONV_K, -1).transpose(1, 0, 2).reshape(CONV_K, D)

    landing = lambda shards: [_landing_zone((N_DEV,) + s.shape, BF16) for s in shards]
    gathered = lambda lands, shards, tag: [_with_own_slot(g, s) for g, s in
                                           zip(_gather_forward(lands, tag + "_gather_forward"), shards)]
    half = D // 2
    w_in_b = w_in[0].astype(BF16)
    first, second = [w_in_b[:half]], [w_in_b[half:], w_q_b[0].astype(BF16), w_kv_b[0].astype(BF16)]
    mid = [w[0].astype(BF16) for w in (w_o_a, w_o_b, w_o)]
    last = [w[0].astype(BF16) for w in (w_ffn_in, w_ffn_out)]
    first_state = _split_start("first_gather_start", first, landing(first), mod_blocks, _gather_copies, 4)
    second_state = _split_start("second_gather_start", second, landing(second), first_state[4], _gather_copies, 4)
    u = _modulate_in(x2, mod, ts)

    first_shards, first_lands = _split_wait("first_gather_wait", first_state, (u, second_state[4]), _gather_copies)
    (g_in_top,) = gathered(first_lands, first_shards, "first")
    w_in_top = _assemble_w_in(g_in_top, front, front_pad, D, 0)
    proj_top = _matmul(u, w_in_top, "nn", BF16, "proj_top", k_rows=(0, half), tn=WIDE_TN)
    second_shards, second_lands = _split_wait("second_gather_wait", second_state, (proj_top,), _gather_copies)
    g_in_bottom, wq_s, wkv_s = gathered(second_lands, second_shards, "second")
    mid_state = _split_start("mid_gather_start", mid, landing(mid), g_in_bottom, _gather_copies, 4)
    last_state = _split_start("last_gather_start", last, landing(last), mid_state[4], _gather_copies, 4)
    w_in_p = _assemble_w_in(g_in_bottom, front, front_pad, D, half, into=w_in_top)

    inv_freq = 1.0 / (ROPE_THETA ** (jnp.arange(0, QK_ROPE, 2, dtype=F32) / QK_ROPE))
    ang = positions[0].astype(F32)[:, None] * inv_freq
    cos2 = jnp.concatenate([jnp.cos(ang), jnp.cos(ang)], axis=-1)
    sin2 = jnp.concatenate([jnp.sin(ang), jnp.sin(ang)], axis=-1)
    one, zero = jnp.ones((S, QK_NOPE), F32), jnp.zeros((S, QK_NOPE), F32)
    cos_q, sin_q = jnp.concatenate([one, cos2, one, cos2], axis=-1), jnp.concatenate([zero, sin2, zero, sin2], axis=-1)
    cos_k, sin_k = jnp.tile(cos2, (1, COL_BLOCK // QK_ROPE)), jnp.tile(sin2, (1, COL_BLOCK // QK_ROPE))

    proj = _matmul(u, w_in_p, "nn", BF16, "proj", k_rows=(half, half), init=proj_top, deps=(last_state[4],),
                   tn=WIDE_TN)
    qn = _rms_fwd(proj, g_q_a, 0, Lq, ts, "rms_q")
    kvn = _rms_fwd(proj, g_kv_a, 1, Lkv, ts, "rms_kv")
    q = _matmul(qn, wq_s, "nn", BF16, "q_up")
    kv = _matmul(kvn, wkv_s, "nn", BF16, "kv_up")
    qc, kc, vh = _qk_prep(q, kv, proj, kr_blk, cos_q, sin_q, cos_k, sin_k, H, ts)
    attn, lse = _attn_fwd(qc, kc, vh, T)
    mid_shards, mid_lands = _split_wait("mid_gather_wait", mid_state, lse, _gather_copies)
    w_oa_f, w_ob_f, w_o_f = [g.reshape(-1, D) for g in gathered(mid_lands, mid_shards, "mid")]
    ya = _matmul(attn, w_oa_f, "nn", BF16, "attn_out")
    cbc = _conv_fwd(proj, w_conv_full, blk_b, blk_c, blk_x)
    yb = _matmul(cbc, w_ob_f, "nn", BF16, "conv_out")
    merged = _merge_fwd(proj, ya, yb, blk_ga, blk_gb, ts)
    mix = _matmul(merged, w_o_f, "nn", F32, "mix_out")
    xhat1, rstd1, u2 = _ln1_fwd(x2, mix, mod, ln1_g, ln1_b, ts)
    last_shards, last_lands = _split_wait("last_gather_wait", last_state, u2, _gather_copies)
    w_fi_s, g_fo = gathered(last_lands, last_shards, "last")
    w_fo_f = g_fo.reshape(F, D)
    hh = _matmul(u2, w_fi_s, "nn", BF16, "ffn_in")
    act = _swiglu_fwd(hh, ts, tb)
    ffn = _matmul(act, w_fo_f, "nn", F32, "ffn_out")
    loss_part, dffn, dx1a, vec2 = _ln2_loss(xhat1, ffn, tgt, mod, ln1_g, ln1_b, ln2_g, ln2_b, ts)
    loss = lax.psum(loss_part[0, 0], AXES)

    gw_fo = _matmul(act, dffn, "tn", BF16, "grad_w_ffn_out")
    da = _matmul(dffn, w_fo_f, "nt", BF16, "d_act")
    dh = _swiglu_bwd(da, hh, ts, tb)
    gw_fi = _matmul(u2, dh, "tn", BF16, "grad_w_ffn_in", out_shards=True)
    ffn_in_chip = _in_chip_start([gw_fi, gw_fo.reshape(N_DEV, -1, D)], "ffn")
    du2 = _matmul(dh, w_fi_s, "nt", F32, "d_u2", deps=(ffn_in_chip[4],))
    ffn_state = _reduce_scatter_begin([], "ffn", ffn_in_chip, after=(du2,))
    dxa, dmix, vec1 = _ln1_bwd(du2, dx1a, xhat1, rstd1, mix, mod, ln1_g, ln1_b, ts)
    gw_o = _matmul(merged, dmix, "tn", BF16, "grad_w_o", deps=(ffn_state[4],))
    dmerged = _matmul(dmix, w_o_f, "nt", BF16, "d_merged")
    dya, dyb, dga, dgb = _merge_bwd(dmerged, proj, ya, yb, blk_ga, blk_gb, ts)
    gw_ob = _matmul(cbc, dyb, "tn", BF16, "grad_w_o_b")
    dcbc = _matmul(dyb, w_ob_f, "nt", BF16, "d_conv")
    dcb, dcc, dcx, dwconv = _conv_bwd(dcbc, proj, w_conv_full, blk_b, blk_c, blk_x)
    gw_oa = _matmul(attn, dya, "tn", BF16, "grad_w_o_a")
    mix_in_chip = _in_chip_start([g.reshape(N_DEV, -1, D) for g in (gw_oa, gw_ob, gw_o)], "mix")
    dattn = _matmul(dya, w_oa_f, "nt", BF16, "d_attn", deps=(mix_in_chip[4],))
    dqc, dkc, dvh = _attn_bwd(qc, kc, vh, dattn, attn, lse, T)
    ffn_own, ffn_got = _reduce_scatter_end(ffn_state, dqc, "ffn")
    dq, dkv, dkr = _qk_bwd(dqc, dkc, dvh, cos_q, sin_q, cos_k, sin_k, ts)
    gw_qb = _matmul(qn, dq, "tn", BF16, "grad_w_q_b", out_shards=True)
    gw_kvb = _matmul(kvn, dkv, "tn", BF16, "grad_w_kv_b", out_shards=True)
    mix_state = _reduce_scatter_begin([gw_qb, gw_kvb], "mix", mix_in_chip, after=(dqc,))
    dqn = _matmul(dq, wq_s, "nt", F32, "d_qn", deps=(mix_state[4],))
    dkvn = _matmul(dkv, wkv_s, "nt", F32, "d_kvn")
    dqa, dgq = _rms_bwd(dqn, proj, g_q_a, 0, Lq, ts, "rms_q_bwd")
    dkva, dgkv = _rms_bwd(dkvn, proj, g_kv_a, 1, Lkv, ts, "rms_kv_bwd")
    dproj = jnp.concatenate([dqa, dkva, dkr, dcb, dcc, dcx, dga, dgb], axis=1)
    gw_in_p = _matmul(u, dproj, "tn", BF16, "grad_w_in", tn=WIDE_TN)
    mix_own, mix_got = _reduce_scatter_end(mix_state, gw_in_p, "mix")
    in_state = _reduce_scatter_begin([_split_w_in(gw_in_p, front, front_pad)], "in")
    du = _matmul(dproj, w_in_p, "nt", F32, "d_u", deps=(in_state[4],))
    grad_x, vec0 = _grad_x(du, dxa, x2, mod, ts)

    my_chip = (2 * lax.axis_index("x") + lax.axis_index("y")).astype(jnp.int32).reshape(1)
    arrived = {}
    for nm, w, m, v, own, got in (
            ("w_ffn_in", w_ffn_in, m_w_ffn_in, v_w_ffn_in, ffn_own[0], ffn_got[0]),
            ("w_ffn_out", w_ffn_out, m_w_ffn_out, v_w_ffn_out, ffn_own[1], ffn_got[1]),
            ("w_o_a", w_o_a, m_w_o_a, v_w_o_a, mix_own[0], mix_got[0]),
            ("w_o_b", w_o_b, m_w_o_b, v_w_o_b, mix_own[1], mix_got[1]),
            ("w_o", w_o, m_w_o, v_w_o, mix_own[2], mix_got[2]),
            ("w_q_b", w_q_b, m_w_q_b, v_w_q_b, mix_own[3], mix_got[3]),
            ("w_kv_b", w_kv_b, m_w_kv_b, v_w_kv_b, mix_own[4], mix_got[4])):
        arrived[nm] = [a[None] for a in _adamw_reduced(w[0], own, got, m[0], v[0], my_chip, "adamw_" + nm)]

    dmod = jnp.concatenate([vec0[0], vec0[1], vec1[4], vec1[0], vec1[1], vec2[2]])
    small = jnp.concatenate([dmod, dgq[0], dgkv[0], vec1[2], vec1[3], vec2[0], vec2[1], dwconv[:CONV_K].reshape(-1)])
    n_small = small.shape[0]
    nch = _round_up(n_small, cw) // cw
    payload = jnp.pad(small, (0, nch * cw - n_small)).reshape(nch, 1, cw)
    in_own, in_got = _reduce_scatter_end(in_state, [res[1] for res in arrived.values()] + [grad_x, payload], "in")
    arrived["w_in"] = [a[None] for a in _adamw_reduced(w_in[0], in_own[0], in_got[0], m_w_in[0], v_w_in[0], my_chip,
                                                       "adamw_w_in")]
    summed, dmod_mine = _ada_bwd(payload, deps=[arrived["w_in"][1]])
    arrived["w_ada"] = [a[None] for a in _adamw_ada(w_ada[0], cact_all.T, dmod_mine.reshape(N_DEV, cw),
                                                    m_w_ada[0], v_w_ada[0])]
    summed = summed.reshape(-1)
    offs = [0, 6 * D, 6 * D + Lq, 6 * D + Lq + Lkv]
    offs += [offs[-1] + D * k for k in range(1, 5)]
    g_b_ada = summed[offs[0]:offs[1]].reshape(1, -1)
    g_gq = summed[offs[1]:offs[2]].reshape(1, -1)
    g_gkv = summed[offs[2]:offs[3]].reshape(1, -1)
    g_ln1g, g_ln1b, g_ln2g, g_ln2b = [summed[offs[3 + k]:offs[4 + k]].reshape(1, -1) for k in range(4)]
    wc = w_conv.shape[2]
    g_wconv = lax.dynamic_slice(summed[offs[7]:offs[7] + CONV_K * D].reshape(CONV_K, D), (0, me * wc), (CONV_K, wc))

    names = ["w_ada", "b_ada", "w_in", "g_q_a", "w_q_b", "g_kv_a", "w_kv_b", "w_o_a", "w_conv", "w_o_b", "w_o",
             "ln1_g", "ln1_b", "w_ffn_in", "w_ffn_out", "ln2_g", "ln2_b"]
    weights = [w_ada, b_ada, w_in, g_q_a, w_q_b, g_kv_a, w_kv_b, w_o_a, w_conv, w_o_b, w_o, ln1_g, ln1_b,
               w_ffn_in, w_ffn_out, ln2_g, ln2_b]
    moms = [m_w_ada, m_b_ada, m_w_in, m_g_q_a, m_w_q_b, m_g_kv_a, m_w_kv_b, m_w_o_a, m_w_conv, m_w_o_b, m_w_o,
            m_ln1_g, m_ln1_b, m_w_ffn_in, m_w_ffn_out, m_ln2_g, m_ln2_b]
    vels = [v_w_ada, v_b_ada, v_w_in, v_g_q_a, v_w_q_b, v_g_kv_a, v_w_kv_b, v_w_o_a, v_w_conv, v_w_o_b, v_w_o,
            v_ln1_g, v_ln1_b, v_w_ffn_in, v_w_ffn_out, v_ln2_g, v_ln2_b]
    grad_of = {"b_ada": g_b_ada, "g_q_a": g_gq, "g_kv_a": g_gkv, "w_conv": g_wconv,
               "ln1_g": g_ln1g, "ln1_b": g_ln1b, "ln2_g": g_ln2g, "ln2_b": g_ln2b}
    state_of = dict(zip(names, zip(weights, moms, vels)))
    results = dict(arrived)

    def update(nm, reduced=None):
        w, m, v = state_of[nm]
        shp = w.shape
        w2 = w.reshape(shp[-2], shp[-1]) if w.ndim == 3 else w
        m2, v2 = m.reshape(w2.shape), v.reshape(w2.shape)
        if reduced is None:
            g2 = grad_of[nm].reshape(w2.shape)
            res = (g2,) + tuple(_adamw(w2, g2, m2, v2, "adamw_" + nm))
        else:
            res = _adamw_reduced(w2, reduced[0], reduced[1], m2, v2, my_chip, "adamw_" + nm)
        results[nm] = [a.reshape(shp) for a in res]

    for nm in grad_of:
        update(nm)
    outs = [[results[nm][k] for nm in names] for k in range(4)]
    return (loss, grad_x.reshape(x.shape), *outs[0], *outs[1], *outs[2], *outs[3])
```

```python
import functools

import jax
import jax.numpy as jnp
from jax import lax
from jax.experimental import pallas as pl
from jax.experimental.pallas import tpu as pltpu

F32 = jnp.float32
BF16 = jnp.bfloat16
MESH_ID = pl.DeviceIdType.MESH
AXES = ("x", "y", "c")
N_DEV = 8

CHUNK = 64
QK_NOPE = 128
QK_ROPE = 64
V_HEAD = 128
QK_CAT = QK_NOPE + QK_ROPE
ROPE_THETA = 10000.0
ATTN_SCALE = (QK_NOPE + QK_ROPE) ** -0.5
CONV_K = 3
DEEPNORM_ALPHA = 2.0 ** 0.25
LN_EPS = 1e-5
RMS_EPS = 1e-6
NEG_INF = -1e30

ADAM_LR = 0.001
ADAM_B1 = 0.9
ADAM_B2 = 0.999
ADAM_EPS = 1e-08
ADAM_WD = 0.01
ADAM_STEP = 10

LANE = 128
COL_BLOCK = 256
PACK_ROW_ALIGN = 16
PAIR_SUM_BLOCK = 1 << 20
WIDE_TN = 1280
WIDE_TK = 3840
VMEM_LIMIT = 48 * 1024 * 1024


def _round_up(n, m):
    return (n + m - 1) // m * m


def _tile(n, pref, align=LANE):
    best = None
    t = align
    while t <= min(n, pref):
        if n % t == 0:
            best = t
        t += align
    return best if best is not None else n


def _cparams(sem=None):
    return pltpu.CompilerParams(dimension_semantics=sem, vmem_limit_bytes=VMEM_LIMIT)


def _sigmoid(x):
    return 0.5 * jnp.tanh(0.5 * x) + 0.5


def _matmul(a, b, mode, out_dtype, name, tm=1024, tn=1024, tk=2048, deps=(), out_shards=False, k_rows=None,
            init=None):
    b_shards = b.ndim == 3
    n = b.shape[2] if b_shards else (b.shape[1] // N_DEV if out_shards else None)
    if mode == "nn":
        (M, K), (K2, N) = a.shape, (b.shape[1], N_DEV * n) if b_shards else b.shape
    elif mode == "nt":
        (M, K), (N, K2) = a.shape, (b.shape[1], N_DEV * n) if b_shards else b.shape
    else:
        (K, M), (K2, N) = a.shape, b.shape
    assert K == K2, (a.shape, b.shape, mode)
    tm = _tile(M, tm)
    tn = n if (mode != "nt" and n is not None) else _tile(N, tn)
    k_row0, k_len = k_rows if k_rows is not None else (0, K)
    tk = n if (mode == "nt" and b_shards) else _tile(k_len, tk)
    nk, k0 = k_len // tk, k_row0 // tk
    if mode == "nn":
        a_spec = pl.BlockSpec((tm, tk), lambda i, j, k: (i, k0 + k))
        b_spec = (pl.BlockSpec((1, tk, n), lambda i, j, k: (j, k, 0)) if b_shards
                  else pl.BlockSpec((tk, tn), lambda i, j, k: (k0 + k, j)))
        dims = (((1,), (0,)), ((), ()))
    elif mode == "nt":
        a_spec = pl.BlockSpec((tm, tk), lambda i, j, k: (i, k))
        b_spec = (pl.BlockSpec((1, tn, n), lambda i, j, k: (k, j, 0)) if b_shards
                  else pl.BlockSpec((tn, tk), lambda i, j, k: (j, k)))
        dims = (((1,), (1,)), ((), ()))
    else:
        a_spec = pl.BlockSpec((tk, tm), lambda i, j, k: (k, i))
        b_spec = pl.BlockSpec((tk, tn), lambda i, j, k: (k, j))
        dims = (((0,), (0,)), ((), ()))
    if out_shards:
        out_spec = pl.BlockSpec((1, tm, n), lambda i, j, k: (j, i, 0))
        out_shape = jax.ShapeDtypeStruct((N_DEV, M, n), out_dtype)
    else:
        out_spec = pl.BlockSpec((tm, tn), lambda i, j, k: (i, j))
        out_shape = jax.ShapeDtypeStruct((M, N), out_dtype)

    def product(a_ref, b_ref):
        b_blk = b_ref[0] if b_shards else b_ref[...]
        return lax.dot_general(a_ref[...].astype(BF16), b_blk.astype(BF16), dims, preferred_element_type=F32)

    def write(o_ref, value):
        if out_shards:
            o_ref[0] = value.astype(o_ref.dtype)
        else:
            o_ref[...] = value.astype(o_ref.dtype)

    def body_whole_k(a_ref, b_ref, *rest):
        value = product(a_ref, b_ref)
        write(rest[-1], value if init is None else value + rest[0][...])

    def body_split_k(a_ref, b_ref, *rest):
        o_ref, acc_ref = rest[-2:]
        k = pl.program_id(2)

        @pl.when(k == 0)
        def _():
            acc_ref[...] = jnp.zeros_like(acc_ref) if init is None else rest[0][...].astype(F32)

        acc_ref[...] += product(a_ref, b_ref)

        @pl.when(k == nk - 1)
        def _():
            write(o_ref, acc_ref[...])

    return pl.pallas_call(
        body_whole_k if nk == 1 else body_split_k, name=name, grid=(M // tm, N // tn, nk),
        in_specs=[a_spec, b_spec] + ([] if init is None else [out_spec]) + [ANY_SPEC] * len(deps),
        out_specs=out_spec, out_shape=out_shape,
        scratch_shapes=[] if nk == 1 else [pltpu.VMEM((tm, tn), F32)],
        compiler_params=_cparams(("parallel", "parallel", "arbitrary")),
    )(a, b, *(() if init is None else (init,)), *deps)


def _assemble_w_in(shards, front, front_pad, rows, row0, into=None):
    _, K, n = shards.shape
    gap = front_pad - front
    tk = _tile(K, 256, PACK_ROW_ALIGN)
    blk0 = row0 // tk

    def body(g_ref, *rest):
        o_ref = rest[-1]
        if gap:
            o_ref[:, front:front_pad] = jnp.zeros((tk, gap), o_ref.dtype)
        for j in range(N_DEV):
            lo, hi = j * n, (j + 1) * n
            if lo < front < hi:
                o_ref[:, lo:front] = g_ref[j, :, 0:front - lo]
                o_ref[:, front_pad:hi + gap] = g_ref[j, :, front - lo:n]
            else:
                off = 0 if hi <= front else gap
                o_ref[:, lo + off:hi + off] = g_ref[j]

    return pl.pallas_call(
        body, name="assemble_w_in", grid=(K // tk,),
        in_specs=[pl.BlockSpec((N_DEV, tk, n), lambda i: (0, i, 0))] + ([] if into is None else [ANY_SPEC]),
        out_specs=pl.BlockSpec((tk, N_DEV * n + gap), lambda i: (blk0 + i, 0)),
        out_shape=jax.ShapeDtypeStruct((rows, N_DEV * n + gap), shards.dtype),
        input_output_aliases={} if into is None else {1: 0},
        compiler_params=_cparams(("parallel",)),
    )(*([shards] if into is None else [shards, into]))


def _split_w_in(w, front, front_pad):
    K, NP = w.shape
    gap = front_pad - front
    n = (NP - gap) // N_DEV
    tk = _tile(K, 256, PACK_ROW_ALIGN)

    def body(w_ref, o_ref):
        for j in range(N_DEV):
            lo, hi = j * n, (j + 1) * n
            if lo < front < hi:
                o_ref[j, :, 0:front - lo] = w_ref[:, lo:front]
                o_ref[j, :, front - lo:n] = w_ref[:, front_pad:hi + gap]
            else:
                off = 0 if hi <= front else gap
                o_ref[j] = w_ref[:, lo + off:hi + off]

    return pl.pallas_call(
        body, name="split_grad_w_in", grid=(K // tk,),
        in_specs=[pl.BlockSpec((tk, NP), lambda i: (i, 0))],
        out_specs=pl.BlockSpec((N_DEV, tk, n), lambda i: (0, i, 0)),
        out_shape=jax.ShapeDtypeStruct((N_DEV, K, n), w.dtype),
        compiler_params=_cparams(("parallel",)),
    )(w)


def _modulate_in(x, mod, ts):
    S, D = x.shape

    def body(x_ref, mod_ref, u_ref):
        u_ref[...] = (x_ref[...] * (1.0 + mod_ref[1:2, :]) + mod_ref[0:1, :]).astype(BF16)

    return pl.pallas_call(
        body, name="modulate_in", grid=(S // ts,),
        in_specs=[pl.BlockSpec((ts, D), lambda i: (i, 0)), pl.BlockSpec((6, D), lambda i: (0, 0))],
        out_specs=pl.BlockSpec((ts, D), lambda i: (i, 0)),
        out_shape=jax.ShapeDtypeStruct((S, D), BF16),
        compiler_params=_cparams(("parallel",)),
    )(x, mod)


def _rms_fwd(proj, g, blk, L, ts, name):
    S = proj.shape[0]

    def body(a_ref, g_ref, y_ref):
        a = a_ref[...].astype(F32)
        r = lax.rsqrt(jnp.mean(a * a, axis=-1, keepdims=True) + RMS_EPS)
        y_ref[...] = (a * r * g_ref[...]).astype(BF16)

    return pl.pallas_call(
        body, name=name, grid=(S // ts,),
        in_specs=[pl.BlockSpec((ts, L), lambda i: (i, blk)), pl.BlockSpec((1, L), lambda i: (0, 0))],
        out_specs=pl.BlockSpec((ts, L), lambda i: (i, 0)),
        out_shape=jax.ShapeDtypeStruct((S, L), BF16),
        compiler_params=_cparams(("parallel",)),
    )(proj, g)


def _rope_partner(x, period, start):
    w = x.shape[-1]
    lane = lax.broadcasted_iota(jnp.int32, x.shape, x.ndim - 1) % period
    first = (lane >= start) & (lane < start + QK_ROPE // 2)
    from_right = pltpu.roll(x, w - QK_ROPE // 2, axis=x.ndim - 1)
    from_left = pltpu.roll(x, QK_ROPE // 2, axis=x.ndim - 1)
    return jnp.where(first, -from_right, from_left)


def _qk_prep(q, kv, proj, kr_blk, cos_q, sin_q, cos_k, sin_k, H, ts):
    S = q.shape[0]
    pair = 2 * QK_CAT
    kv_w = QK_NOPE + V_HEAD

    def body(q_ref, kv_ref, kr_ref, cq_ref, sq_ref, ck_ref, sk_ref, qc_ref, kc_ref, vh_ref):
        kr = kr_ref[...].astype(F32)
        kr = kr * ck_ref[...] + _rope_partner(kr, QK_ROPE, 0) * sk_ref[...]
        kr = kr[:, :QK_ROPE].astype(BF16)
        for p in range(H // 2):
            x = q_ref[:, p * pair:(p + 1) * pair].astype(F32)
            x = x * cq_ref[...] + _rope_partner(x, QK_CAT, QK_NOPE) * sq_ref[...]
            qc_ref[2 * p] = x[:, :QK_CAT].astype(BF16)
            qc_ref[2 * p + 1] = x[:, QK_CAT:].astype(BF16)
        for h in range(H):
            kc_ref[h, :, 0:QK_NOPE] = kv_ref[:, h * kv_w:h * kv_w + QK_NOPE].astype(BF16)
            kc_ref[h, :, QK_NOPE:QK_CAT] = kr
            vh_ref[h, :, :] = kv_ref[:, h * kv_w + QK_NOPE:(h + 1) * kv_w].astype(BF16)

    row = lambda w: pl.BlockSpec((ts, w), lambda i: (i, 0))
    return pl.pallas_call(
        body, name="qk_prep", grid=(S // ts,),
        in_specs=[row(H * QK_CAT), row(H * kv_w),
                  pl.BlockSpec((ts, COL_BLOCK), lambda i: (i, kr_blk)),
                  row(pair), row(pair), row(COL_BLOCK), row(COL_BLOCK)],
        out_specs=[pl.BlockSpec((H, ts, QK_CAT), lambda i: (0, i, 0)),
                   pl.BlockSpec((H, ts, QK_CAT), lambda i: (0, i, 0)),
                   pl.BlockSpec((H, ts, V_HEAD), lambda i: (0, i, 0))],
        out_shape=[jax.ShapeDtypeStruct((H, S, QK_CAT), BF16), jax.ShapeDtypeStruct((H, S, QK_CAT), BF16),
                   jax.ShapeDtypeStruct((H, S, V_HEAD), BF16)],
        compiler_params=_cparams(("parallel",)),
    )(q, kv, proj, cos_q, sin_q, cos_k, sin_k)


NT_DIMS = (((1,), (1,)), ((), ()))
TN_DIMS = (((0,), (0,)), ((), ()))


def _diag_mask(T):
    rows = lax.broadcasted_iota(jnp.int32, (T, T), 0) // CHUNK
    cols = lax.broadcasted_iota(jnp.int32, (T, T), 1) // CHUNK
    return cols <= rows


def _attn_fwd(qc, kc, vh, T):
    H, S, _ = qc.shape
    n = S // T

    def body(q_ref, k_ref, v_ref, o_ref, lse_ref):
        q = q_ref[0]

        def block(i):
            L = (i + 1) * T
            s_old = lax.dot_general(q, k_ref[0, 0:i * T, :], NT_DIMS, preferred_element_type=F32) if i else None
            s_diag = lax.dot_general(q, k_ref[0, i * T:L, :], NT_DIMS, preferred_element_type=F32)
            s_diag = jnp.where(_diag_mask(T), s_diag, NEG_INF)
            m = jnp.max(s_diag, axis=-1, keepdims=True)
            if i:
                m = jnp.maximum(m, jnp.max(s_old, axis=-1, keepdims=True))
            p_diag = jnp.exp((s_diag - m) * ATTN_SCALE)
            l = jnp.sum(p_diag, axis=-1, keepdims=True)
            acc = jnp.dot(p_diag.astype(BF16), v_ref[0, i * T:L, :], preferred_element_type=F32)
            if i:
                p_old = jnp.exp((s_old - m) * ATTN_SCALE)
                l = l + jnp.sum(p_old, axis=-1, keepdims=True)
                acc = acc + jnp.dot(p_old.astype(BF16), v_ref[0, 0:i * T, :], preferred_element_type=F32)
            o_ref[...] = (acc / l).astype(o_ref.dtype)
            lse_ref[0] = m * ATTN_SCALE + jnp.log(l)

        for i in range(n):
            pl.when(pl.program_id(1) == i)(functools.partial(block, i))

    return pl.pallas_call(
        body, name="attn_fwd", grid=(H, n),
        in_specs=[pl.BlockSpec((1, T, QK_CAT), lambda h, i: (h, i, 0)),
                  pl.BlockSpec((1, S, QK_CAT), lambda h, i: (h, 0, 0)),
                  pl.BlockSpec((1, S, V_HEAD), lambda h, i: (h, 0, 0))],
        out_specs=[pl.BlockSpec((T, V_HEAD), lambda h, i: (i, h)),
                   pl.BlockSpec((1, T, 1), lambda h, i: (h, i, 0))],
        out_shape=[jax.ShapeDtypeStruct((S, H * V_HEAD), BF16), jax.ShapeDtypeStruct((H, S, 1), F32)],
        compiler_params=_cparams(("parallel", "arbitrary")),
    )(qc, kc, vh)


def _shift_rows(z, k):
    if k == 0:
        return z
    n = z.shape[0]
    row = lax.broadcasted_iota(jnp.int32, z.shape, 0)
    if k > 0:
        return jnp.where(row >= k, pltpu.roll(z, k, axis=0), 0.0)
    return jnp.where(row < n + k, pltpu.roll(z, n + k, axis=0), 0.0)


def _conv_fwd(proj, w_conv, blk_b, blk_c, blk_x):
    S = proj.shape[0]
    D = w_conv.shape[1]
    nb = D // COL_BLOCK

    def body(cb_ref, cc_ref, cx_ref, w_ref, o_ref):
        z = cc_ref[...].astype(F32) * cx_ref[...].astype(F32)
        conv = w_ref[2:3, :] * z + w_ref[1:2, :] * _shift_rows(z, 1) + w_ref[0:1, :] * _shift_rows(z, 2)
        o_ref[...] = (cb_ref[...].astype(F32) * conv).astype(BF16)

    col = lambda off: pl.BlockSpec((S, COL_BLOCK), lambda j: (0, off + j))
    return pl.pallas_call(
        body, name="conv_fwd", grid=(nb,),
        in_specs=[col(blk_b), col(blk_c), col(blk_x), pl.BlockSpec((CONV_K, COL_BLOCK), lambda j: (0, j))],
        out_specs=pl.BlockSpec((S, COL_BLOCK), lambda j: (0, j)),
        out_shape=jax.ShapeDtypeStruct((S, D), BF16),
        compiler_params=_cparams(("parallel",)),
    )(proj, proj, proj, w_conv)


def _merge_fwd(proj, ya, yb, blk_ga, blk_gb, ts):
    S, D = ya.shape
    nb = D // COL_BLOCK

    def body(ga_ref, gb_ref, ya_ref, yb_ref, o_ref):
        sa, sb = _sigmoid(ga_ref[...].astype(F32)), _sigmoid(gb_ref[...].astype(F32))
        o_ref[...] = (sa * ya_ref[...].astype(F32) + sb * yb_ref[...].astype(F32)).astype(BF16)

    row = pl.BlockSpec((ts, D), lambda i: (i, 0))
    seg = lambda blk: pl.BlockSpec((pl.Element(ts), pl.Element(D)), lambda i: (i * ts, blk * COL_BLOCK))
    return pl.pallas_call(
        body, name="merge_fwd", grid=(S // ts,),
        in_specs=[seg(blk_ga), seg(blk_gb), row, row],
        out_specs=row,
        out_shape=jax.ShapeDtypeStruct((S, D), BF16),
        compiler_params=_cparams(("parallel",)),
    )(proj, proj, ya, yb)


def _ln1_fwd(x, mix, mod, g, b, ts):
    S, D = x.shape

    def body(x_ref, mix_ref, mod_ref, g_ref, b_ref, xhat_ref, rstd_ref, u2_ref):
        r = DEEPNORM_ALPHA * x_ref[...] + mod_ref[2:3, :] * mix_ref[...]
        mu = jnp.mean(r, axis=-1, keepdims=True)
        d = r - mu
        rstd = lax.rsqrt(jnp.mean(d * d, axis=-1, keepdims=True) + LN_EPS)
        xhat = d * rstd
        xhat_ref[...] = xhat
        rstd_ref[...] = rstd
        x1 = xhat * g_ref[...] + b_ref[...]
        u2_ref[...] = (x1 * (1.0 + mod_ref[4:5, :]) + mod_ref[3:4, :]).astype(BF16)

    row = pl.BlockSpec((ts, D), lambda i: (i, 0))
    vec = lambda r: pl.BlockSpec((r, D), lambda i: (0, 0))
    return pl.pallas_call(
        body, name="ln1_fwd", grid=(S // ts,),
        in_specs=[row, row, vec(6), vec(1), vec(1)],
        out_specs=[row, pl.BlockSpec((ts, 1), lambda i: (i, 0)), row],
        out_shape=[jax.ShapeDtypeStruct((S, D), F32), jax.ShapeDtypeStruct((S, 1), F32),
                   jax.ShapeDtypeStruct((S, D), BF16)],
        compiler_params=_cparams(("parallel",)),
    )(x, mix, mod, g, b)


def _swiglu_fwd(h, ts, tb):
    S, F2 = h.shape
    F = F2 // 2
    nb = F // tb

    def body(hg_ref, hu_ref, a_ref):
        hg = hg_ref[...].astype(F32)
        a_ref[...] = (hg * _sigmoid(hg) * hu_ref[...].astype(F32)).astype(BF16)

    return pl.pallas_call(
        body, name="swiglu_fwd", grid=(S // ts, nb),
        in_specs=[pl.BlockSpec((ts, tb), lambda i, j: (i, j)), pl.BlockSpec((ts, tb), lambda i, j: (i, j + nb))],
        out_specs=pl.BlockSpec((ts, tb), lambda i, j: (i, j)),
        out_shape=jax.ShapeDtypeStruct((S, F), BF16),
        compiler_params=_cparams(("parallel", "parallel")),
    )(h, h)


def _ln2_loss(xhat1, ffn, tgt, mod, g1, b1, g2, b2, ts):
    S, D = xhat1.shape

    def body(xh_ref, ffn_ref, t_ref, mod_ref, g1_ref, b1_ref, g2_ref, b2_ref, loss_ref, dffn_ref, dx1_ref, vec_ref):
        i = pl.program_id(0)

        @pl.when(i == 0)
        def _():
            loss_ref[...] = jnp.zeros_like(loss_ref)
            vec_ref[...] = jnp.zeros_like(vec_ref)

        x1 = xh_ref[...] * g1_ref[...] + b1_ref[...]
        ffn = ffn_ref[...]
        r = DEEPNORM_ALPHA * x1 + mod_ref[5:6, :] * ffn
        mu = jnp.mean(r, axis=-1, keepdims=True)
        d = r - mu
        rstd = lax.rsqrt(jnp.mean(d * d, axis=-1, keepdims=True) + LN_EPS)
        xhat = d * rstd
        e = xhat * g2_ref[...] + b2_ref[...] - t_ref[...]
        loss_ref[...] += 0.5 * jnp.sum(jnp.mean(e * e, axis=-1, keepdims=True))
        dy = e * (1.0 / D)
        dxhat = dy * g2_ref[...]
        dr = rstd * (dxhat - jnp.mean(dxhat, axis=-1, keepdims=True)
                     - xhat * jnp.mean(dxhat * xhat, axis=-1, keepdims=True))
        dffn_ref[...] = (dr * mod_ref[5:6, :]).astype(BF16)
        dx1_ref[...] = DEEPNORM_ALPHA * dr
        vec_ref[0:1, :] += jnp.sum(dy * xhat, axis=0, keepdims=True)
        vec_ref[1:2, :] += jnp.sum(dy, axis=0, keepdims=True)
        vec_ref[2:3, :] += jnp.sum(dr * ffn, axis=0, keepdims=True)

    row = pl.BlockSpec((ts, D), lambda i: (i, 0))
    vec = lambda r: pl.BlockSpec((r, D), lambda i: (0, 0))
    return pl.pallas_call(
        body, name="ln2_loss", grid=(S // ts,),
        in_specs=[row, row, row, vec(6), vec(1), vec(1), vec(1), vec(1)],
        out_specs=[pl.BlockSpec((1, LANE), lambda i: (0, 0)), row, row, vec(8)],
        out_shape=[jax.ShapeDtypeStruct((1, LANE), F32), jax.ShapeDtypeStruct((S, D), BF16),
                   jax.ShapeDtypeStruct((S, D), F32), jax.ShapeDtypeStruct((8, D), F32)],
        compiler_params=_cparams(("arbitrary",)),
    )(xhat1, ffn, tgt, mod, g1, b1, g2, b2)


def _swiglu_bwd(da, h, ts, tb):
    S, F2 = h.shape
    nb = (F2 // 2) // tb

    def body(da_ref, hg_ref, hu_ref, dh_ref):
        hg, da = hg_ref[...].astype(F32), da_ref[...].astype(F32)
        sg = _sigmoid(hg)

        @pl.when(pl.program_id(2) == 0)
        def _():
            dh_ref[...] = (da * hu_ref[...].astype(F32) * (sg * (1.0 + hg * (1.0 - sg)))).astype(BF16)

        @pl.when(pl.program_id(2) == 1)
        def _():
            dh_ref[...] = (da * hg * sg).astype(BF16)

    lo = pl.BlockSpec((ts, tb), lambda i, j, k: (i, j))
    hi = pl.BlockSpec((ts, tb), lambda i, j, k: (i, j + nb))
    return pl.pallas_call(
        body, name="swiglu_bwd", grid=(S // ts, nb, 2),
        in_specs=[lo, lo, hi],
        out_specs=pl.BlockSpec((ts, tb), lambda i, j, k: (i, j + nb * k)),
        out_shape=jax.ShapeDtypeStruct((S, F2), BF16),
        compiler_params=_cparams(("parallel", "parallel", "arbitrary")),
    )(da, h, h)


def _ln1_bwd(du2, dx1a, xhat1, rstd1, mix, mod, g1, b1, ts):
    S, D = xhat1.shape

    def body(du2_ref, dx1a_ref, xh_ref, rstd_ref, mix_ref, mod_ref, g_ref, b_ref, dxa_ref, dmix_ref, vec_ref):
        i = pl.program_id(0)

        @pl.when(i == 0)
        def _():
            vec_ref[...] = jnp.zeros_like(vec_ref)

        xhat, du2, mix = xh_ref[...], du2_ref[...], mix_ref[...]
        x1 = xhat * g_ref[...] + b_ref[...]
        dx1 = dx1a_ref[...] + du2 * (1.0 + mod_ref[4:5, :])
        dxhat = dx1 * g_ref[...]
        dr = rstd_ref[...] * (dxhat - jnp.mean(dxhat, axis=-1, keepdims=True)
                              - xhat * jnp.mean(dxhat * xhat, axis=-1, keepdims=True))
        dxa_ref[...] = DEEPNORM_ALPHA * dr
        dmix_ref[...] = (dr * mod_ref[2:3, :]).astype(BF16)
        vec_ref[0:1, :] += jnp.sum(du2, axis=0, keepdims=True)
        vec_ref[1:2, :] += jnp.sum(du2 * x1, axis=0, keepdims=True)
        vec_ref[2:3, :] += jnp.sum(dx1 * xhat, axis=0, keepdims=True)
        vec_ref[3:4, :] += jnp.sum(dx1, axis=0, keepdims=True)
        vec_ref[4:5, :] += jnp.sum(dr * mix, axis=0, keepdims=True)

    row = pl.BlockSpec((ts, D), lambda i: (i, 0))
    vec = lambda r: pl.BlockSpec((r, D), lambda i: (0, 0))
    return pl.pallas_call(
        body, name="ln1_bwd", grid=(S // ts,),
        in_specs=[row, row, row, pl.BlockSpec((ts, 1), lambda i: (i, 0)), row, vec(6), vec(1), vec(1)],
        out_specs=[row, row, vec(8)],
        out_shape=[jax.ShapeDtypeStruct((S, D), F32), jax.ShapeDtypeStruct((S, D), BF16),
                   jax.ShapeDtypeStruct((8, D), F32)],
        compiler_params=_cparams(("arbitrary",)),
    )(du2, dx1a, xhat1, rstd1, mix, mod, g1, b1)


def _merge_bwd(dmerged, proj, ya, yb, blk_ga, blk_gb, ts):
    S, D = ya.shape
    nb = D // COL_BLOCK

    def body(dm_ref, ga_ref, gb_ref, ya_ref, yb_ref, dya_ref, dyb_ref, dga_ref, dgb_ref):
        dm = dm_ref[...].astype(F32)
        sa, sb = _sigmoid(ga_ref[...].astype(F32)), _sigmoid(gb_ref[...].astype(F32))
        dya_ref[...] = (dm * sa).astype(BF16)
        dyb_ref[...] = (dm * sb).astype(BF16)
        dga_ref[...] = (dm * ya_ref[...].astype(F32) * sa * (1.0 - sa)).astype(BF16)
        dgb_ref[...] = (dm * yb_ref[...].astype(F32) * sb * (1.0 - sb)).astype(BF16)

    row = pl.BlockSpec((ts, D), lambda i: (i, 0))
    seg = lambda blk: pl.BlockSpec((pl.Element(ts), pl.Element(D)), lambda i: (i * ts, blk * COL_BLOCK))
    out = jax.ShapeDtypeStruct((S, D), BF16)
    return pl.pallas_call(
        body, name="merge_bwd", grid=(S // ts,),
        in_specs=[row, seg(blk_ga), seg(blk_gb), row, row],
        out_specs=[row] * 4,
        out_shape=[out] * 4,
        compiler_params=_cparams(("parallel",)),
    )(dmerged, proj, proj, ya, yb)


def _conv_bwd(dcbc, proj, w_conv, blk_b, blk_c, blk_x):
    S = proj.shape[0]
    D = w_conv.shape[1]
    nb = D // COL_BLOCK

    def body(d_ref, cb_ref, cc_ref, cx_ref, w_ref, dcb_ref, dcc_ref, dcx_ref, dw_ref):
        d, cc, cx = d_ref[...].astype(F32), cc_ref[...].astype(F32), cx_ref[...].astype(F32)
        z = cc * cx
        z1, z2 = _shift_rows(z, 1), _shift_rows(z, 2)
        conv = w_ref[2:3, :] * z + w_ref[1:2, :] * z1 + w_ref[0:1, :] * z2
        dcb_ref[...] = (d * conv).astype(BF16)
        dconv = d * cb_ref[...].astype(F32)
        dz = w_ref[2:3, :] * dconv + w_ref[1:2, :] * _shift_rows(dconv, -1) + w_ref[0:1, :] * _shift_rows(dconv, -2)
        dcc_ref[...] = (dz * cx).astype(BF16)
        dcx_ref[...] = (dz * cc).astype(BF16)
        dw_ref[...] = jnp.zeros_like(dw_ref)
        dw_ref[0:1, :] = jnp.sum(dconv * z2, axis=0, keepdims=True)
        dw_ref[1:2, :] = jnp.sum(dconv * z1, axis=0, keepdims=True)
        dw_ref[2:3, :] = jnp.sum(dconv * z, axis=0, keepdims=True)

    col = lambda off: pl.BlockSpec((S, COL_BLOCK), lambda j: (0, off + j))
    out = jax.ShapeDtypeStruct((S, D), BF16)
    return pl.pallas_call(
        body, name="conv_bwd", grid=(nb,),
        in_specs=[col(0), col(blk_b), col(blk_c), col(blk_x), pl.BlockSpec((CONV_K, COL_BLOCK), lambda j: (0, j))],
        out_specs=[col(0), col(0), col(0), pl.BlockSpec((8, COL_BLOCK), lambda j: (0, j))],
        out_shape=[out, out, out, jax.ShapeDtypeStruct((8, D), F32)],
        compiler_params=_cparams(("parallel",)),
    )(dcbc, proj, proj, proj, w_conv)


def _attn_bwd(qc, kc, vh, do, o, lse, T):
    H, S, _ = qc.shape
    n = S // T

    def body(q_ref, k_ref, v_ref, do_ref, o_ref, lse_ref, dq_ref, dk_ref, dv_ref, d_ref, dq_acc, dk_acc, dv_acc):
        j = pl.program_id(1)

        @pl.when(j == 0)
        def _():
            dq_acc[...] = jnp.zeros_like(dq_acc)
            d_ref[...] = jnp.sum(do_ref[...].astype(F32) * o_ref[...].astype(F32), axis=-1, keepdims=True)

        dk_acc[...] = jnp.zeros_like(dk_acc)
        dv_acc[...] = jnp.zeros_like(dv_acc)
        k, v = k_ref[0], v_ref[0]

        def step(i, masked):
            rows = pl.ds(pl.multiple_of(i * T, T), T)
            q = q_ref[0, rows, :]
            do = do_ref[rows, :].astype(BF16)
            s = lax.dot_general(q, k, NT_DIMS, preferred_element_type=F32) * ATTN_SCALE
            if masked:
                s = jnp.where(_diag_mask(T), s, NEG_INF)
            p = jnp.exp(s - lse_ref[0, rows, :])
            dv_acc[...] += lax.dot_general(p.astype(BF16), do, TN_DIMS, preferred_element_type=F32)
            dp = lax.dot_general(do, v, NT_DIMS, preferred_element_type=F32)
            ds = (p * (dp - d_ref[rows, :]) * ATTN_SCALE).astype(BF16)
            dk_acc[...] += lax.dot_general(ds, q, TN_DIMS, preferred_element_type=F32)
            dq_acc[rows, :] += jnp.dot(ds, k, preferred_element_type=F32)

        def above(i, carry):
            step(i, False)
            return carry

        step(j, True)
        lax.fori_loop(j + 1, n, above, 0)
        dk_ref[0] = dk_acc[...].astype(BF16)
        dv_ref[0] = dv_acc[...].astype(BF16)

        @pl.when(j == n - 1)
        def _():
            dq_ref[0] = dq_acc[...].astype(BF16)

    head = lambda w: pl.BlockSpec((1, S, w), lambda h, j: (h, 0, 0))
    blk = lambda w: pl.BlockSpec((1, T, w), lambda h, j: (h, j, 0))
    ospec = pl.BlockSpec((S, V_HEAD), lambda h, j: (0, h))
    return pl.pallas_call(
        body, name="attn_bwd", grid=(H, n),
        in_specs=[head(QK_CAT), blk(QK_CAT), blk(V_HEAD), ospec, ospec, head(1)],
        out_specs=[head(QK_CAT), blk(QK_CAT), blk(V_HEAD)],
        out_shape=[jax.ShapeDtypeStruct((H, S, QK_CAT), BF16), jax.ShapeDtypeStruct((H, S, QK_CAT), BF16),
                   jax.ShapeDtypeStruct((H, S, V_HEAD), BF16)],
        scratch_shapes=[pltpu.VMEM((S, 1), F32), pltpu.VMEM((S, QK_CAT), F32), pltpu.VMEM((T, QK_CAT), F32),
                        pltpu.VMEM((T, V_HEAD), F32)],
        compiler_params=_cparams(("parallel", "arbitrary")),
    )(qc, kc, vh, do, o, lse)


def _qk_bwd(dqc, dkc, dvh, cos_q, sin_q, cos_k, sin_k, ts):
    H, S, _ = dqc.shape
    pair = 2 * QK_CAT
    kv_w = QK_NOPE + V_HEAD

    def body(dqc_ref, dkc_ref, dvh_ref, cq_ref, sq_ref, ck_ref, sk_ref, dq_ref, dkv_ref, dkr_ref, q_buf, kr_buf):
        for p in range(H // 2):
            q_buf[:, :QK_CAT] = dqc_ref[2 * p].astype(F32)
            q_buf[:, QK_CAT:] = dqc_ref[2 * p + 1].astype(F32)
            g = q_buf[...]
            dq_ref[:, p * pair:(p + 1) * pair] = (
                g * cq_ref[...] - _rope_partner(g, QK_CAT, QK_NOPE) * sq_ref[...]).astype(BF16)
        kr_sum = jnp.zeros((ts, QK_ROPE), F32)
        for h in range(H):
            dkv_ref[:, h * kv_w:h * kv_w + QK_NOPE] = dkc_ref[h, :, 0:QK_NOPE].astype(BF16)
            dkv_ref[:, h * kv_w + QK_NOPE:(h + 1) * kv_w] = dvh_ref[h].astype(BF16)
            kr_sum = kr_sum + dkc_ref[h, :, QK_NOPE:QK_CAT]
        kr_buf[...] = jnp.zeros_like(kr_buf)
        kr_buf[:, 0:QK_ROPE] = kr_sum
        kr = kr_buf[...]
        dkr_ref[...] = (kr * ck_ref[...] - _rope_partner(kr, QK_ROPE, 0) * sk_ref[...]).astype(BF16)

    row = lambda w: pl.BlockSpec((ts, w), lambda i: (i, 0))
    head = lambda w: pl.BlockSpec((H, ts, w), lambda i: (0, i, 0))
    return pl.pallas_call(
        body, name="qk_bwd", grid=(S // ts,),
        in_specs=[head(QK_CAT), head(QK_CAT), head(V_HEAD), row(pair), row(pair), row(COL_BLOCK), row(COL_BLOCK)],
        out_specs=[row(H * QK_CAT), row(H * kv_w), row(COL_BLOCK)],
        out_shape=[jax.ShapeDtypeStruct((S, H * QK_CAT), BF16), jax.ShapeDtypeStruct((S, H * kv_w), BF16),
                   jax.ShapeDtypeStruct((S, COL_BLOCK), BF16)],
        scratch_shapes=[pltpu.VMEM((ts, pair), F32), pltpu.VMEM((ts, COL_BLOCK), F32)],
        compiler_params=_cparams(("parallel",)),
    )(dqc, dkc, dvh, cos_q, sin_q, cos_k, sin_k)


def _rms_bwd(dy, proj, g, blk, L, ts, name):
    S = proj.shape[0]

    def body(dy_ref, a_ref, g_ref, da_ref, dg_ref):
        i = pl.program_id(0)

        @pl.when(i == 0)
        def _():
            dg_ref[...] = jnp.zeros_like(dg_ref)

        a, dy = a_ref[...].astype(F32), dy_ref[...]
        r = lax.rsqrt(jnp.mean(a * a, axis=-1, keepdims=True) + RMS_EPS)
        dyh = dy * g_ref[...]
        da = r * dyh - a * (r * r * r) * jnp.mean(dyh * a, axis=-1, keepdims=True)
        da_ref[...] = da.astype(BF16)
        dg_ref[0:1, :] += jnp.sum(dy * a * r, axis=0, keepdims=True)

    return pl.pallas_call(
        body, name=name, grid=(S // ts,),
        in_specs=[pl.BlockSpec((ts, L), lambda i: (i, 0)), pl.BlockSpec((ts, L), lambda i: (i, blk)),
                  pl.BlockSpec((1, L), lambda i: (0, 0))],
        out_specs=[pl.BlockSpec((ts, L), lambda i: (i, 0)), pl.BlockSpec((8, L), lambda i: (0, 0))],
        out_shape=[jax.ShapeDtypeStruct((S, L), BF16), jax.ShapeDtypeStruct((8, L), F32)],
        compiler_params=_cparams(("arbitrary",)),
    )(dy, proj, g)


def _grad_x(du, dxa, x, mod, ts):
    S, D = x.shape

    def body(du_ref, dxa_ref, x_ref, mod_ref, dx_ref, vec_ref):
        i = pl.program_id(0)

        @pl.when(i == 0)
        def _():
            vec_ref[...] = jnp.zeros_like(vec_ref)

        du = du_ref[...]
        dx_ref[...] = dxa_ref[...] + du * (1.0 + mod_ref[1:2, :])
        vec_ref[0:1, :] += jnp.sum(du, axis=0, keepdims=True)
        vec_ref[1:2, :] += jnp.sum(du * x_ref[...], axis=0, keepdims=True)

    row = pl.BlockSpec((ts, D), lambda i: (i, 0))
    vec = lambda r: pl.BlockSpec((r, D), lambda i: (0, 0))
    return pl.pallas_call(
        body, name="grad_x", grid=(S // ts,),
        in_specs=[row, row, row, vec(6)],
        out_specs=[row, vec(8)],
        out_shape=[jax.ShapeDtypeStruct((S, D), F32), jax.ShapeDtypeStruct((8, D), F32)],
        compiler_params=_cparams(("arbitrary",)),
    )(du, dxa, x, mod)


def _adamw(w, g, m, v, name):
    R, C = w.shape
    tr = _tile(R, max(8, (1 << 19) // C), 8)
    c1 = 1.0 / (1.0 - ADAM_B1 ** ADAM_STEP)
    c2 = 1.0 / (1.0 - ADAM_B2 ** ADAM_STEP)

    def body(w_ref, g_ref, m_ref, v_ref, d_ref, nm_ref, nv_ref):
        g = g_ref[...]
        m = ADAM_B1 * m_ref[...] + (1.0 - ADAM_B1) * g
        v = ADAM_B2 * v_ref[...] + (1.0 - ADAM_B2) * (g * g)
        nm_ref[...] = m
        nv_ref[...] = v
        d_ref[...] = -ADAM_LR * ((m * c1) / (jnp.sqrt(v * c2) + ADAM_EPS) + ADAM_WD * w_ref[...])

    spec = pl.BlockSpec((tr, C), lambda i: (i, 0))
    out = jax.ShapeDtypeStruct((R, C), F32)
    return pl.pallas_call(
        body, name=name, grid=(R // tr,),
        in_specs=[spec] * 4, out_specs=[spec] * 3, out_shape=[out] * 3,
        compiler_params=_cparams(("parallel",)),
    )(w, g, m, v)


def _adamw_ada(w, cact_t, dmod, m, v):
    R, C = w.shape
    tr = _tile(R, max(8, (1 << 18) // C), 8)
    c1 = 1.0 / (1.0 - ADAM_B1 ** ADAM_STEP)
    c2 = 1.0 / (1.0 - ADAM_B2 ** ADAM_STEP)

    def body(w_ref, ct_ref, dm_ref, m_ref, v_ref, g_ref, d_ref, nm_ref, nv_ref):
        ct = ct_ref[...].astype(BF16).astype(F32)
        dm = dm_ref[...].astype(BF16).astype(F32)
        g = ct[:, 0:1] * dm[0:1, :]
        for b in range(1, N_DEV):
            g = g + ct[:, b:b + 1] * dm[b:b + 1, :]
        m = ADAM_B1 * m_ref[...] + (1.0 - ADAM_B1) * g
        v = ADAM_B2 * v_ref[...] + (1.0 - ADAM_B2) * (g * g)
        g_ref[...] = g
        nm_ref[...] = m
        nv_ref[...] = v
        d_ref[...] = -ADAM_LR * ((m * c1) / (jnp.sqrt(v * c2) + ADAM_EPS) + ADAM_WD * w_ref[...])

    spec = pl.BlockSpec((tr, C), lambda i: (i, 0))
    out = jax.ShapeDtypeStruct((R, C), F32)
    return pl.pallas_call(
        body, name="adamw_w_ada", grid=(R // tr,),
        in_specs=[spec, pl.BlockSpec((tr, N_DEV), lambda i: (i, 0)), pl.BlockSpec((N_DEV, C), lambda i: (0, 0)),
                  spec, spec],
        out_specs=[spec] * 4, out_shape=[out] * 4,
        compiler_params=_cparams(("parallel",)),
    )(w, cact_t, dmod, m, v)


def _adamw_reduced(w, own, got, m, v, my_chip, name):
    R, C = w.shape
    tr = _tile(R, max(PACK_ROW_ALIGN, (1 << 18) // C), PACK_ROW_ALIGN)
    c1 = 1.0 / (1.0 - ADAM_B1 ** ADAM_STEP)
    c2 = 1.0 / (1.0 - ADAM_B2 ** ADAM_STEP)

    def body(chip_ref, w_ref, own_ref, g1_ref, g2_ref, g3_ref, m_ref, v_ref, g_ref, d_ref, nm_ref, nv_ref):
        g = own_ref[0].astype(F32) + g1_ref[0].astype(F32) + g2_ref[0].astype(F32) + g3_ref[0].astype(F32)
        m = ADAM_B1 * m_ref[...] + (1.0 - ADAM_B1) * g
        v = ADAM_B2 * v_ref[...] + (1.0 - ADAM_B2) * (g * g)
        g_ref[...] = g
        nm_ref[...] = m
        nv_ref[...] = v
        d_ref[...] = -ADAM_LR * ((m * c1) / (jnp.sqrt(v * c2) + ADAM_EPS) + ADAM_WD * w_ref[...])

    spec = pl.BlockSpec((tr, C), lambda i, chip: (i, 0))
    slot = lambda k: pl.BlockSpec((1, tr, C), lambda i, chip: (chip[0] ^ k, i, 0))
    out = jax.ShapeDtypeStruct((R, C), F32)
    return pl.pallas_call(
        body, name=name,
        grid_spec=pltpu.PrefetchScalarGridSpec(
            num_scalar_prefetch=1, grid=(R // tr,),
            in_specs=[spec, slot(0), slot(1), slot(2), slot(3), spec, spec],
            out_specs=[spec] * 4),
        out_shape=[out] * 4,
        compiler_params=_cparams(("parallel",)),
    )(my_chip, w, own, got, got, got, m, v)


def _my_place():
    return lax.axis_index("x"), lax.axis_index("y"), lax.axis_index("c")


def _peer(k):
    x, y, c = _my_place()
    return (x ^ ((k >> 2) & 1), y ^ ((k >> 1) & 1), c ^ (k & 1))


def _linear(place):
    return 4 * place[0] + 2 * place[1] + place[2]


def _ada_fwd(c_row, wconv_row, w_ada, b_row):
    D, CW = w_ada.shape
    WC = wconv_row.shape[-1]

    def body(c_ref, wc_ref, w_ref, b_ref, mod_ref, cact_ref, wcall_ref, send_buf, sems):
        me = _linear(_my_place())
        c = c_ref[0]
        cact_ref[me] = c * _sigmoid(c)
        wcall_ref[me] = wc_ref[0]

        def gather_copy(buf, k, grp):
            return pltpu.make_async_remote_copy(
                src_ref=buf.at[me], dst_ref=buf.at[me], send_sem=sems.at[0, grp, k], recv_sem=sems.at[1, grp, k],
                device_id=_peer(k), device_id_type=MESH_ID)

        def gather_recv(buf, k, grp):
            src = _linear(_peer(k))
            return pltpu.make_async_remote_copy(
                src_ref=buf.at[src], dst_ref=buf.at[src], send_sem=sems.at[0, grp, k], recv_sem=sems.at[1, grp, k],
                device_id=_peer(k), device_id_type=MESH_ID)

        for k in range(1, N_DEV):
            gather_copy(cact_ref, k, 0).start()
            gather_copy(wcall_ref, k, 1).start()
        for k in range(1, N_DEV):
            gather_recv(cact_ref, k, 0).wait_recv()
            gather_recv(wcall_ref, k, 1).wait_recv()
        for k in range(1, N_DEV):
            gather_copy(cact_ref, k, 0).wait_send()
            gather_copy(wcall_ref, k, 1).wait_send()

        cact = jnp.concatenate([cact_ref[b] for b in range(N_DEV)], axis=0)
        mod_all = jnp.dot(cact.astype(BF16), w_ref[...].astype(BF16), preferred_element_type=F32) + b_ref[0]
        for b in range(N_DEV):
            send_buf[b] = mod_all[b:b + 1, :]
        mod_ref[me] = send_buf[me]

        def scatter_copy(k):
            dst = _linear(_peer(k))
            return pltpu.make_async_remote_copy(
                src_ref=send_buf.at[dst], dst_ref=mod_ref.at[me], send_sem=sems.at[0, 2, k], recv_sem=sems.at[1, 2, k],
                device_id=_peer(k), device_id_type=MESH_ID)

        def scatter_recv(k):
            src = _linear(_peer(k))
            return pltpu.make_async_remote_copy(
                src_ref=send_buf.at[src], dst_ref=mod_ref.at[src], send_sem=sems.at[0, 2, k], recv_sem=sems.at[1, 2, k],
                device_id=_peer(k), device_id_type=MESH_ID)

        for k in range(1, N_DEV):
            scatter_copy(k).start()
        for k in range(1, N_DEV):
            scatter_recv(k).wait_recv()
        for k in range(1, N_DEV):
            scatter_copy(k).wait_send()

    vmem = pl.BlockSpec(memory_space=pltpu.VMEM)
    return pl.pallas_call(
        body, name="ada_fwd",
        in_specs=[vmem] * 4, out_specs=[vmem] * 3,
        out_shape=[jax.ShapeDtypeStruct((N_DEV, 1, CW), F32), jax.ShapeDtypeStruct((N_DEV, 1, D), F32),
                   jax.ShapeDtypeStruct((N_DEV, 1, WC), F32)],
        scratch_shapes=[pltpu.VMEM((N_DEV, 1, CW), F32), pltpu.SemaphoreType.DMA((2, 3, N_DEV))],
        compiler_params=pltpu.CompilerParams(vmem_limit_bytes=VMEM_LIMIT),
    )(c_row, wconv_row, w_ada, b_row)


def _ada_bwd(payload, deps=()):
    NCH, _, CW = payload.shape

    def body(p_ref, *rest):
        sum_ref, mine_ref, all_ref, sems = rest[-4:]
        me = _linear(_my_place())
        all_ref[me] = p_ref[...]

        def copy(k, slot):
            return pltpu.make_async_remote_copy(
                src_ref=all_ref.at[slot], dst_ref=all_ref.at[slot], send_sem=sems.at[0, k], recv_sem=sems.at[1, k],
                device_id=_peer(k), device_id_type=MESH_ID)

        for k in range(1, N_DEV):
            copy(k, me).start()
        for k in range(1, N_DEV):
            copy(k, _linear(_peer(k))).wait_recv()
        for k in range(1, N_DEV):
            copy(k, me).wait_send()

        total = all_ref[0]
        for b in range(1, N_DEV):
            total = total + all_ref[b]
        sum_ref[...] = total

        for b in range(N_DEV):
            mine_ref[b] = all_ref[b, me]

    vmem = pl.BlockSpec(memory_space=pltpu.VMEM)
    return pl.pallas_call(
        body, name="ada_bwd",
        in_specs=[vmem] + [ANY_SPEC] * len(deps), out_specs=[vmem, vmem],
        out_shape=[jax.ShapeDtypeStruct((NCH, 1, CW), F32), jax.ShapeDtypeStruct((N_DEV, 1, CW), F32)],
        scratch_shapes=[pltpu.VMEM((N_DEV, NCH, 1, CW), F32), pltpu.SemaphoreType.DMA((2, N_DEV))],
        compiler_params=pltpu.CompilerParams(vmem_limit_bytes=VMEM_LIMIT),
    )(payload, *deps)


def _exchange_in_chip(parts):
    W = len(parts)

    def body(*refs):
        p_refs, got_refs, (send_sems, recv_sems) = refs[:W], refs[W:2 * W], refs[2 * W:]
        x, y, c = _my_place()
        sibling = (x, y, 1 - c)
        copies = []
        for w in range(W):
            for q in range(4):
                copies.append(pltpu.make_async_remote_copy(
                    src_ref=p_refs[w].at[2 * q + (1 - c)], dst_ref=got_refs[w].at[q],
                    send_sem=send_sems.at[4 * w + q], recv_sem=recv_sems.at[4 * w + q],
                    device_id=sibling, device_id_type=MESH_ID))
        for cp in copies:
            cp.start()
        for cp in copies:
            cp.wait_recv()
        for cp in copies:
            cp.wait_send()

    return pl.pallas_call(
        body, name="grad_exchange_in_chip",
        in_specs=[HBM_SPEC] * W, out_specs=[HBM_SPEC] * W,
        out_shape=[jax.ShapeDtypeStruct((4,) + p.shape[1:], p.dtype) for p in parts],
        scratch_shapes=[pltpu.SemaphoreType.DMA((4 * W,)), pltpu.SemaphoreType.DMA((4 * W,))],
    )(*parts)


def _pair_sum(parts, got, core):
    _, R, C = parts.shape
    tr = _tile(R, max(PACK_ROW_ALIGN, PAIR_SUM_BLOCK // C), PACK_ROW_ALIGN)

    def body(c_ref, p_ref, g_ref, o_ref):
        o_ref[...] = (p_ref[...].astype(F32) + g_ref[...].astype(F32)).astype(o_ref.dtype)

    return pl.pallas_call(
        body, name="grad_pair_sum",
        grid_spec=pltpu.PrefetchScalarGridSpec(
            num_scalar_prefetch=1, grid=(4, R // tr),
            in_specs=[pl.BlockSpec((1, tr, C), lambda q, i, c_ref: (2 * q + c_ref[0], i, 0)),
                      pl.BlockSpec((1, tr, C), lambda q, i, c_ref: (q, i, 0))],
            out_specs=pl.BlockSpec((1, tr, C), lambda q, i, c_ref: (q, i, 0))),
        out_shape=jax.ShapeDtypeStruct((4, R, C), parts.dtype),
        compiler_params=_cparams(("parallel", "parallel")),
    )(core, parts, got)


HBM_SPEC = pl.BlockSpec(memory_space=pltpu.HBM)
SEM_SPEC = pl.BlockSpec(memory_space=pltpu.SEMAPHORE)
ANY_SPEC = pl.BlockSpec(memory_space=pl.ANY)
SPLIT_EFFECT = pltpu.SideEffectType.DATAFLOW_SIDE_EFFECTING


def _landing_zone(shape, dtype):
    return pltpu.with_memory_space_constraint(lax.empty(shape, dtype), pltpu.HBM)


def _split_start(name, arrays, lands, after, copies_of, per_array):
    W = len(arrays)
    after = tuple(after) if isinstance(after, (tuple, list)) else (after,)

    def body(*refs):
        x_refs, land_refs = refs[:W], refs[W:2 * W]
        send_sems, recv_sems = refs[2 * W + len(after)], refs[2 * W + len(after) + 1]
        token = refs[-1]
        k = 0
        for w in range(W):
            for src, dst, dev in copies_of(w, x_refs[w], land_refs[w]):
                pltpu.make_async_remote_copy(src_ref=src, dst_ref=dst, send_sem=send_sems.at[k], recv_sem=recv_sems.at[k],
                                             device_id=dev, device_id_type=MESH_ID).start()
                k += 1
        token[...] = jnp.zeros_like(token)

    n_copies = per_array * W
    hbm_of = lambda xs: tuple(pltpu.HBM(a.shape, a.dtype) for a in xs)
    out = pl.pallas_call(
        body, name=name,
        out_shape=(pltpu.SemaphoreType.DMA((n_copies,)), pltpu.SemaphoreType.DMA((n_copies,)))
        + hbm_of(arrays) + hbm_of(lands) + (jax.ShapeDtypeStruct((8, LANE), F32),),
        in_specs=(HBM_SPEC,) * (2 * W) + (ANY_SPEC,) * len(after),
        out_specs=(SEM_SPEC, SEM_SPEC) + (HBM_SPEC,) * (2 * W) + (pl.BlockSpec(memory_space=pltpu.VMEM),),
        input_output_aliases={i: 2 + i for i in range(2 * W)},
        compiler_params=pltpu.CompilerParams(has_side_effects=SPLIT_EFFECT),
    )(*[pltpu.with_memory_space_constraint(a, pltpu.HBM) for a in arrays], *lands, *after)
    return out[0], out[1], list(out[2:2 + W]), list(out[2 + W:2 + 2 * W]), out[-1]


def _split_wait(name, state, after, copies_of):
    send_sems, recv_sems, arrays, lands, _ = state
    W = len(arrays)
    after = tuple(after) if isinstance(after, (tuple, list)) else (after,)

    def body(*refs):
        x_refs, land_refs = refs[:W], refs[W:2 * W]
        send_sems, recv_sems = refs[2 * W], refs[2 * W + 1]
        k = 0
        for w in range(W):
            for src, dst, dev in copies_of(w, x_refs[w], land_refs[w]):
                cp = pltpu.make_async_remote_copy(src_ref=src, dst_ref=dst, send_sem=send_sems.at[k],
                                                  recv_sem=recv_sems.at[k], device_id=dev, device_id_type=MESH_ID)
                cp.wait_send()
                cp.wait_recv()
                k += 1

    out = pl.pallas_call(
        body, name=name,
        out_shape=tuple(pltpu.HBM(a.shape, a.dtype) for a in arrays + lands),
        in_specs=(HBM_SPEC,) * (2 * W) + (SEM_SPEC, SEM_SPEC) + (ANY_SPEC,) * len(after),
        out_specs=(HBM_SPEC,) * (2 * W),
        input_output_aliases={i: i for i in range(2 * W)},
        compiler_params=pltpu.CompilerParams(has_side_effects=SPLIT_EFFECT),
    )(*arrays, *lands, send_sems, recv_sems, *after)
    return list(out[:W]), list(out[W:])


def _scatter_copies(w, p_ref, land_ref):
    x, y, c = _my_place()
    my_chip = 2 * x + y
    return [(p_ref.at[2 * (x ^ (k >> 1)) + (y ^ (k & 1))], land_ref.at[my_chip], (x ^ (k >> 1), y ^ (k & 1), c))
            for k in range(1, 4)]


def _gather_copies(w, x_ref, land_ref):
    x, y, c = _my_place()
    me = _linear((x, y, c))
    devs = [(x, y, 1 - c)] + [(x ^ (k >> 1), y ^ (k & 1), c) for k in range(1, 4)]
    return [(x_ref, land_ref.at[me], d) for d in devs]


def _gather_forward(lands, name):
    W = len(lands)

    def body(*refs):
        land_refs, out_refs, (send_sems, recv_sems) = refs[:W], refs[W:2 * W], refs[2 * W:]
        x, y, c = _my_place()
        sibling = (x, y, 1 - c)
        sends, arrivals = [], []
        for w in range(W):
            for k in range(1, 4):
                px, py = x ^ (k >> 1), y ^ (k & 1)
                landed, theirs = _linear((px, py, c)), out_refs[w].at[_linear((px, py, 1 - c))]
                sem = 3 * w + k - 1
                sends.append(pltpu.make_async_remote_copy(
                    src_ref=land_refs[w].at[landed], dst_ref=out_refs[w].at[landed],
                    send_sem=send_sems.at[sem], recv_sem=recv_sems.at[sem], device_id=sibling, device_id_type=MESH_ID))
                arrivals.append(pltpu.make_async_remote_copy(
                    src_ref=theirs, dst_ref=theirs, send_sem=send_sems.at[sem], recv_sem=recv_sems.at[sem],
                    device_id=sibling, device_id_type=MESH_ID))
        for cp in sends:
            cp.start()
        for cp in arrivals:
            cp.wait_recv()
        for cp in sends:
            cp.wait_send()

    return pl.pallas_call(
        body, name=name,
        in_specs=[HBM_SPEC] * W, out_specs=[HBM_SPEC] * W,
        out_shape=[jax.ShapeDtypeStruct(l.shape, l.dtype) for l in lands],
        input_output_aliases={i: i for i in range(W)},
        scratch_shapes=[pltpu.SemaphoreType.DMA((3 * W,)), pltpu.SemaphoreType.DMA((3 * W,))],
    )(*lands)


def _with_own_slot(gathered, shard):
    return lax.dynamic_update_index_in_dim(gathered, shard[None], _linear(_my_place()), axis=0)


def _in_chip_copies(w, p_ref, land_ref):
    x, y, c = _my_place()
    return [(p_ref.at[2 * q + (1 - c)], land_ref.at[q], (x, y, 1 - c)) for q in range(4)]


def _in_chip_start(parts, tag):
    lands = [_landing_zone((4,) + p.shape[1:], p.dtype) for p in parts]
    return _split_start("grad_in_chip_start_" + tag, parts, lands, (), _in_chip_copies, 4)


def _reduce_scatter_begin(parts, tag, in_chip_state=None, after=()):
    parts, early, got = list(parts), [], []
    if in_chip_state is not None:
        early, got = _split_wait("grad_in_chip_wait_" + tag, in_chip_state, after, _in_chip_copies)
    if parts:
        got = got + list(_exchange_in_chip(parts))
    parts = early + parts
    core = lax.axis_index("c").astype(jnp.int32).reshape(1)
    chip_parts = [_pair_sum(p, g, core) for p, g in zip(parts, got)]
    lands = [_landing_zone(p.shape, p.dtype) for p in chip_parts]
    return _split_start("grad_scatter_start_" + tag, chip_parts, lands, got[0], _scatter_copies, 3)


def _reduce_scatter_end(state, after, tag):
    return _split_wait("grad_scatter_wait_" + tag, state, after, _scatter_copies)


def kernel(x, c, positions, w_ada, b_ada, w_in, g_q_a, w_q_b, g_kv_a, w_kv_b, w_o_a, w_conv, w_o_b, w_o, ln1_g, ln1_b, w_ffn_in, w_ffn_out, ln2_g, ln2_b, loss_target, m_w_ada, m_b_ada, m_w_in, m_g_q_a, m_w_q_b, m_g_kv_a, m_w_kv_b, m_w_o_a, m_w_conv, m_w_o_b, m_w_o, m_ln1_g, m_ln1_b, m_w_ffn_in, m_w_ffn_out, m_ln2_g, m_ln2_b, v_w_ada, v_b_ada, v_w_in, v_g_q_a, v_w_q_b, v_g_kv_a, v_w_kv_b, v_w_o_a, v_w_conv, v_w_o_b, v_w_o, v_ln1_g, v_ln1_b, v_w_ffn_in, v_w_ffn_out, v_ln2_g, v_ln2_b):
    x2, tgt = x[0], loss_target[0]
    S, D = x2.shape
    Lq, Lkv = g_q_a.shape[1], g_kv_a.shape[1]
    H = w_q_b.shape[2] * N_DEV // QK_CAT
    F = w_ffn_out.shape[1] * N_DEV
    assert Lq == Lkv and (Lq + Lkv) % COL_BLOCK == 0 and D % COL_BLOCK == 0
    front = Lq + Lkv + QK_ROPE
    front_pad = _round_up(front, COL_BLOCK)
    kr_blk = (Lq + Lkv) // COL_BLOCK
    blk_b = front_pad // COL_BLOCK
    nblk = D // COL_BLOCK
    blk_c, blk_x, blk_ga, blk_gb = blk_b + nblk, blk_b + 2 * nblk, blk_b + 3 * nblk, blk_b + 4 * nblk
    ts = _tile(S, 256, 8)
    T = _tile(S, min(512, S // 2), CHUNK)
    tb = _tile(F, 2816)
    me = _linear(_my_place())

    cw = w_ada.shape[2]
    b_mine = lax.dynamic_slice(b_ada, (0, me * cw), (1, cw)).reshape(1, 1, cw)
    mod_blocks, cact_all, wconv_all = _ada_fwd(c.reshape(1, 1, D), w_conv[0].reshape(1, 1, -1), w_ada[0], b_mine)
    mod = mod_blocks.reshape(6, D)
    cact_all = cact_all.reshape(N_DEV, D)
    w_conv_full = wconv_all.reshape(N_DEV, CONV_K, -1).transpose(1, 0, 2).reshape(CONV_K, D)

    landing = lambda shards: [_landing_zone((N_DEV,) + s.shape, BF16) for s in shards]
    gathered = lambda lands, shards, tag: [_with_own_slot(g, s) for g, s in
                                           zip(_gather_forward(lands, tag + "_gather_forward"), shards)]
    half = D // 2
    w_in_b = w_in[0].astype(BF16)
    first, second = [w_in_b[:half]], [w_in_b[half:], w_q_b[0].astype(BF16), w_kv_b[0].astype(BF16)]
    mid = [w[0].astype(BF16) for w in (w_o_a, w_o_b, w_o)]
    last = [w[0].astype(BF16) for w in (w_ffn_in, w_ffn_out)]
    first_state = _split_start("first_gather_start", first, landing(first), mod_blocks, _gather_copies, 4)
    second_state = _split_start("second_gather_start", second, landing(second), first_state[4], _gather_copies, 4)
    u = _modulate_in(x2, mod, ts)

    first_shards, first_lands = _split_wait("first_gather_wait", first_state, (u, second_state[4]), _gather_copies)
    (g_in_top,) = gathered(first_lands, first_shards, "first")
    w_in_top = _assemble_w_in(g_in_top, front, front_pad, D, 0)
    proj_top = _matmul(u, w_in_top, "nn", BF16, "proj_top", k_rows=(0, half), tn=WIDE_TN)
    second_shards, second_lands = _split_wait("second_gather_wait", second_state, (proj_top,), _gather_copies)
    g_in_bottom, wq_s, wkv_s = gathered(second_lands, second_shards, "second")
    mid_state = _split_start("mid_gather_start", mid, landing(mid), g_in_bottom, _gather_copies, 4)
    last_state = _split_start("last_gather_start", last, landing(last), mid_state[4], _gather_copies, 4)
    w_in_p = _assemble_w_in(g_in_bottom, front, front_pad, D, half, into=w_in_top)

    inv_freq = 1.0 / (ROPE_THETA ** (jnp.arange(0, QK_ROPE, 2, dtype=F32) / QK_ROPE))
    ang = positions[0].astype(F32)[:, None] * inv_freq
    cos2 = jnp.concatenate([jnp.cos(ang), jnp.cos(ang)], axis=-1)
    sin2 = jnp.concatenate([jnp.sin(ang), jnp.sin(ang)], axis=-1)
    one, zero = jnp.ones((S, QK_NOPE), F32), jnp.zeros((S, QK_NOPE), F32)
    cos_q, sin_q = jnp.concatenate([one, cos2, one, cos2], axis=-1), jnp.concatenate([zero, sin2, zero, sin2], axis=-1)
    cos_k, sin_k = jnp.tile(cos2, (1, COL_BLOCK // QK_ROPE)), jnp.tile(sin2, (1, COL_BLOCK // QK_ROPE))

    proj = _matmul(u, w_in_p, "nn", BF16, "proj", k_rows=(half, half), init=proj_top, deps=(last_state[4],),
                   tn=WIDE_TN)
    qn = _rms_fwd(proj, g_q_a, 0, Lq, ts, "rms_q")
    kvn = _rms_fwd(proj, g_kv_a, 1, Lkv, ts, "rms_kv")
    q = _matmul(qn, wq_s, "nn", BF16, "q_up")
    kv = _matmul(kvn, wkv_s, "nn", BF16, "kv_up")
    qc, kc, vh = _qk_prep(q, kv, proj, kr_blk, cos_q, sin_q, cos_k, sin_k, H, ts)
    attn, lse = _attn_fwd(qc, kc, vh, T)
    mid_shards, mid_lands = _split_wait("mid_gather_wait", mid_state, lse, _gather_copies)
    w_oa_f, w_ob_f, w_o_f = [g.reshape(-1, D) for g in gathered(mid_lands, mid_shards, "mid")]
    ya = _matmul(attn, w_oa_f, "nn", BF16, "attn_out")
    cbc = _conv_fwd(proj, w_conv_full, blk_b, blk_c, blk_x)
    yb = _matmul(cbc, w_ob_f, "nn", BF16, "conv_out")
    merged = _merge_fwd(proj, ya, yb, blk_ga, blk_gb, ts)
    mix = _matmul(merged, w_o_f, "nn", F32, "mix_out")
    xhat1, rstd1, u2 = _ln1_fwd(x2, mix, mod, ln1_g, ln1_b, ts)
    last_shards, last_lands = _split_wait("last_gather_wait", last_state, u2, _gather_copies)
    w_fi_s, g_fo = gathered(last_lands, last_shards, "last")
    w_fo_f = g_fo.reshape(F, D)
    hh = _matmul(u2, w_fi_s, "nn", BF16, "ffn_in")
    act = _swiglu_fwd(hh, ts, tb)
    ffn = _matmul(act, w_fo_f, "nn", F32, "ffn_out")
    loss_part, dffn, dx1a, vec2 = _ln2_loss(xhat1, ffn, tgt, mod, ln1_g, ln1_b, ln2_g, ln2_b, ts)
    loss = lax.psum(loss_part[0, 0], AXES)

    gw_fo = _matmul(act, dffn, "tn", BF16, "grad_w_ffn_out")
    da = _matmul(dffn, w_fo_f, "nt", BF16, "d_act")
    dh = _swiglu_bwd(da, hh, ts, tb)
    gw_fi = _matmul(u2, dh, "tn", BF16, "grad_w_ffn_in", out_shards=True)
    ffn_in_chip = _in_chip_start([gw_fi, gw_fo.reshape(N_DEV, -1, D)], "ffn")
    du2 = _matmul(dh, w_fi_s, "nt", F32, "d_u2", deps=(ffn_in_chip[4],))
    ffn_state = _reduce_scatter_begin([], "ffn", ffn_in_chip, after=(du2,))
    dxa, dmix, vec1 = _ln1_bwd(du2, dx1a, xhat1, rstd1, mix, mod, ln1_g, ln1_b, ts)
    gw_o = _matmul(merged, dmix, "tn", BF16, "grad_w_o", deps=(ffn_state[4],))
    dmerged = _matmul(dmix, w_o_f, "nt", BF16, "d_merged")
    dya, dyb, dga, dgb = _merge_bwd(dmerged, proj, ya, yb, blk_ga, blk_gb, ts)
    gw_ob = _matmul(cbc, dyb, "tn", BF16, "grad_w_o_b")
    dcbc = _matmul(dyb, w_ob_f, "nt", BF16, "d_conv")
    dcb, dcc, dcx, dwconv = _conv_bwd(dcbc, proj, w_conv_full, blk_b, blk_c, blk_x)
    gw_oa = _matmul(attn, dya, "tn", BF16, "grad_w_o_a")
    mix_in_chip = _in_chip_start([g.reshape(N_DEV, -1, D) for g in (gw_oa, gw_ob, gw_o)], "mix")
    dattn = _matmul(dya, w_oa_f, "nt", BF16, "d_attn", deps=(mix_in_chip[4],))
    dqc, dkc, dvh = _attn_bwd(qc, kc, vh, dattn, attn, lse, T)
    ffn_own, ffn_got = _reduce_scatter_end(ffn_state, dqc, "ffn")
    dq, dkv, dkr = _qk_bwd(dqc, dkc, dvh, cos_q, sin_q, cos_k, sin_k, ts)
    gw_qb = _matmul(qn, dq, "tn", BF16, "grad_w_q_b", out_shards=True)
    gw_kvb = _matmul(kvn, dkv, "tn", BF16, "grad_w_kv_b", out_shards=True)
    mix_state = _reduce_scatter_begin([gw_qb, gw_kvb], "mix", mix_in_chip, after=(dqc,))
    dqn = _matmul(dq, wq_s, "nt", F32, "d_qn", deps=(mix_state[4],))
    dkvn = _matmul(dkv, wkv_s, "nt", F32, "d_kvn")
    dqa, dgq = _rms_bwd(dqn, proj, g_q_a, 0, Lq, ts, "rms_q_bwd")
    dkva, dgkv = _rms_bwd(dkvn, proj, g_kv_a, 1, Lkv, ts, "rms_kv_bwd")
    dproj = jnp.concatenate([dqa, dkva, dkr, dcb, dcc, dcx, dga, dgb], axis=1)
    gw_in_p = _matmul(u, dproj, "tn", BF16, "grad_w_in", tn=WIDE_TN)
    mix_own, mix_got = _reduce_scatter_end(mix_state, gw_in_p, "mix")
    in_state = _reduce_scatter_begin([_split_w_in(gw_in_p, front, front_pad)], "in")
    du = _matmul(dproj, w_in_p, "nt", F32, "d_u", deps=(in_state[4],), tk=WIDE_TK)
    grad_x, vec0 = _grad_x(du, dxa, x2, mod, ts)

    my_chip = (2 * lax.axis_index("x") + lax.axis_index("y")).astype(jnp.int32).reshape(1)
    arrived = {}
    for nm, w, m, v, own, got in (
            ("w_ffn_in", w_ffn_in, m_w_ffn_in, v_w_ffn_in, ffn_own[0], ffn_got[0]),
            ("w_ffn_out", w_ffn_out, m_w_ffn_out, v_w_ffn_out, ffn_own[1], ffn_got[1]),
            ("w_o_a", w_o_a, m_w_o_a, v_w_o_a, mix_own[0], mix_got[0]),
            ("w_o_b", w_o_b, m_w_o_b, v_w_o_b, mix_own[1], mix_got[1]),
            ("w_o", w_o, m_w_o, v_w_o, mix_own[2], mix_got[2]),
            ("w_q_b", w_q_b, m_w_q_b, v_w_q_b, mix_own[3], mix_got[3]),
            ("w_kv_b", w_kv_b, m_w_kv_b, v_w_kv_b, mix_own[4], mix_got[4])):
        arrived[nm] = [a[None] for a in _adamw_reduced(w[0], own, got, m[0], v[0], my_chip, "adamw_" + nm)]

    dmod = jnp.concatenate([vec0[0], vec0[1], vec1[4], vec1[0], vec1[1], vec2[2]])
    small = jnp.concatenate([dmod, dgq[0], dgkv[0], vec1[2], vec1[3], vec2[0], vec2[1], dwconv[:CONV_K].reshape(-1)])
    n_small = small.shape[0]
    nch = _round_up(n_small, cw) // cw
    payload = jnp.pad(small, (0, nch * cw - n_small)).reshape(nch, 1, cw)
    in_own, in_got = _reduce_scatter_end(in_state, [res[1] for res in arrived.values()] + [grad_x, payload], "in")
    arrived["w_in"] = [a[None] for a in _adamw_reduced(w_in[0], in_own[0], in_got[0], m_w_in[0], v_w_in[0], my_chip,
                                                       "adamw_w_in")]
    summed, dmod_mine = _ada_bwd(payload, deps=[arrived["w_in"][1]])
    arrived["w_ada"] = [a[None] for a in _adamw_ada(w_ada[0], cact_all.T, dmod_mine.reshape(N_DEV, cw),
                                                    m_w_ada[0], v_w_ada[0])]
    summed = summed.reshape(-1)
    offs = [0, 6 * D, 6 * D + Lq, 6 * D + Lq + Lkv]
    offs += [offs[-1] + D * k for k in range(1, 5)]
    g_b_ada = summed[offs[0]:offs[1]].reshape(1, -1)
    g_gq = summed[offs[1]:offs[2]].reshape(1, -1)
    g_gkv = summed[offs[2]:offs[3]].reshape(1, -1)
    g_ln1g, g_ln1b, g_ln2g, g_ln2b = [summed[offs[3 + k]:offs[4 + k]].reshape(1, -1) for k in range(4)]
    wc = w_conv.shape[2]
    g_wconv = lax.dynamic_slice(summed[offs[7]:offs[7] + CONV_K * D].reshape(CONV_K, D), (0, me * wc), (CONV_K, wc))

    names = ["w_ada", "b_ada", "w_in", "g_q_a", "w_q_b", "g_kv_a", "w_kv_b", "w_o_a", "w_conv", "w_o_b", "w_o",
             "ln1_g", "ln1_b", "w_ffn_in", "w_ffn_out", "ln2_g", "ln2_b"]
    weights = [w_ada, b_ada, w_in, g_q_a, w_q_b, g_kv_a, w_kv_b, w_o_a, w_conv, w_o_b, w_o, ln1_g, ln1_b,
               w_ffn_in, w_ffn_out, ln2_g, ln2_b]
    moms = [m_w_ada, m_b_ada, m_w_in, m_g_q_a, m_w_q_b, m_g_kv_a, m_w_kv_b, m_w_o_a, m_w_conv, m_w_o_b, m_w_o,
            m_ln1_g, m_ln1_b, m_w_ffn_in, m_w_ffn_out, m_ln2_g, m_ln2_b]
    vels = [v_w_ada, v_b_ada, v_w_in, v_g_q_a, v_w_q_b, v_g_kv_a, v_w_kv_b, v_w_o_a, v_w_conv, v_w_o_b, v_w_o,
            v_ln1_g, v_ln1_b, v_w_ffn_in, v_w_ffn_out, v_ln2_g, v_ln2_b]
    grad_of = {"b_ada": g_b_ada, "g_q_a": g_gq, "g_kv_a": g_gkv, "w_conv": g_wconv,
               "ln1_g": g_ln1g, "ln1_b": g_ln1b, "ln2_g": g_ln2g, "ln2_b": g_ln2b}
    state_of = dict(zip(names, zip(weights, moms, vels)))
    results = dict(arrived)

    def update(nm, reduced=None):
        w, m, v = state_of[nm]
        shp = w.shape
        w2 = w.reshape(shp[-2], shp[-1]) if w.ndim == 3 else w
        m2, v2 = m.reshape(w2.shape), v.reshape(w2.shape)
        if reduced is None:
            g2 = grad_of[nm].reshape(w2.shape)
            res = (g2,) + tuple(_adamw(w2, g2, m2, v2, "adamw_" + nm))
        else:
            res = _adamw_reduced(w2, reduced[0], reduced[1], m2, v2, my_chip, "adamw_" + nm)
        results[nm] = [a.reshape(shp) for a in res]

    for nm in grad_of:
        update(nm)
    outs = [[results[nm][k] for nm in names] for k in range(4)]
    return (loss, grad_x.reshape(x.shape), *outs[0], *outs[1], *outs[2], *outs[3])
```

```python
import functools

import jax
import jax.numpy as jnp
from jax import lax
from jax.experimental import pallas as pl
from jax.experimental.pallas import tpu as pltpu

F32 = jnp.float32
BF16 = jnp.bfloat16
MESH_ID = pl.DeviceIdType.MESH
AXES = ("x", "y", "c")
N_DEV = 8

CHUNK = 64
QK_NOPE = 128
QK_ROPE = 64
V_HEAD = 128
QK_CAT = QK_NOPE + QK_ROPE
ROPE_THETA = 10000.0
ATTN_SCALE = (QK_NOPE + QK_ROPE) ** -0.5
CONV_K = 3
DEEPNORM_ALPHA = 2.0 ** 0.25
LN_EPS = 1e-5
RMS_EPS = 1e-6
NEG_INF = -1e30

ADAM_LR = 0.001
ADAM_B1 = 0.9
ADAM_B2 = 0.999
ADAM_EPS = 1e-08
ADAM_WD = 0.01
ADAM_STEP = 10

LANE = 128
COL_BLOCK = 256
PACK_ROW_ALIGN = 16
PAIR_SUM_BLOCK = 1 << 20
WIDE_TN = 1280
WIDE_TK = 3840
FFN_TILE = 1408
VMEM_LIMIT = 48 * 1024 * 1024


def _round_up(n, m):
    return (n + m - 1) // m * m


def _tile(n, pref, align=LANE):
    best = None
    t = align
    while t <= min(n, pref):
        if n % t == 0:
            best = t
        t += align
    return best if best is not None else n


def _cparams(sem=None):
    return pltpu.CompilerParams(dimension_semantics=sem, vmem_limit_bytes=VMEM_LIMIT)


def _sigmoid(x):
    return 0.5 * jnp.tanh(0.5 * x) + 0.5


def _matmul(a, b, mode, out_dtype, name, tm=1024, tn=1024, tk=2048, deps=(), out_shards=False, k_rows=None,
            init=None):
    b_shards = b.ndim == 3
    n = b.shape[2] if b_shards else (b.shape[1] // N_DEV if out_shards else None)
    if mode == "nn":
        (M, K), (K2, N) = a.shape, (b.shape[1], N_DEV * n) if b_shards else b.shape
    elif mode == "nt":
        (M, K), (N, K2) = a.shape, (b.shape[1], N_DEV * n) if b_shards else b.shape
    else:
        (K, M), (K2, N) = a.shape, b.shape
    assert K == K2, (a.shape, b.shape, mode)
    tm = _tile(M, tm)
    tn = n if (mode != "nt" and n is not None) else _tile(N, tn)
    k_row0, k_len = k_rows if k_rows is not None else (0, K)
    tk = n if (mode == "nt" and b_shards) else _tile(k_len, tk)
    nk, k0 = k_len // tk, k_row0 // tk
    if mode == "nn":
        a_spec = pl.BlockSpec((tm, tk), lambda i, j, k: (i, k0 + k))
        b_spec = (pl.BlockSpec((1, tk, n), lambda i, j, k: (j, k, 0)) if b_shards
                  else pl.BlockSpec((tk, tn), lambda i, j, k: (k0 + k, j)))
        dims = (((1,), (0,)), ((), ()))
    elif mode == "nt":
        a_spec = pl.BlockSpec((tm, tk), lambda i, j, k: (i, k))
        b_spec = (pl.BlockSpec((1, tn, n), lambda i, j, k: (k, j, 0)) if b_shards
                  else pl.BlockSpec((tn, tk), lambda i, j, k: (j, k)))
        dims = (((1,), (1,)), ((), ()))
    else:
        a_spec = pl.BlockSpec((tk, tm), lambda i, j, k: (k, i))
        b_spec = pl.BlockSpec((tk, tn), lambda i, j, k: (k, j))
        dims = (((0,), (0,)), ((), ()))
    if out_shards:
        out_spec = pl.BlockSpec((1, tm, n), lambda i, j, k: (j, i, 0))
        out_shape = jax.ShapeDtypeStruct((N_DEV, M, n), out_dtype)
    else:
        out_spec = pl.BlockSpec((tm, tn), lambda i, j, k: (i, j))
        out_shape = jax.ShapeDtypeStruct((M, N), out_dtype)

    def product(a_ref, b_ref):
        b_blk = b_ref[0] if b_shards else b_ref[...]
        return lax.dot_general(a_ref[...].astype(BF16), b_blk.astype(BF16), dims, preferred_element_type=F32)

    def write(o_ref, value):
        if out_shards:
            o_ref[0] = value.astype(o_ref.dtype)
        else:
            o_ref[...] = value.astype(o_ref.dtype)

    def body_whole_k(a_ref, b_ref, *rest):
        value = product(a_ref, b_ref)
        write(rest[-1], value if init is None else value + rest[0][...])

    def body_split_k(a_ref, b_ref, *rest):
        o_ref, acc_ref = rest[-2:]
        k = pl.program_id(2)

        @pl.when(k == 0)
        def _():
            acc_ref[...] = jnp.zeros_like(acc_ref) if init is None else rest[0][...].astype(F32)

        acc_ref[...] += product(a_ref, b_ref)

        @pl.when(k == nk - 1)
        def _():
            write(o_ref, acc_ref[...])

    return pl.pallas_call(
        body_whole_k if nk == 1 else body_split_k, name=name, grid=(M // tm, N // tn, nk),
        in_specs=[a_spec, b_spec] + ([] if init is None else [out_spec]) + [ANY_SPEC] * len(deps),
        out_specs=out_spec, out_shape=out_shape,
        scratch_shapes=[] if nk == 1 else [pltpu.VMEM((tm, tn), F32)],
        compiler_params=_cparams(("parallel", "parallel", "arbitrary")),
    )(a, b, *(() if init is None else (init,)), *deps)


def _assemble_w_in(shards, front, front_pad, rows, row0, into=None):
    _, K, n = shards.shape
    gap = front_pad - front
    tk = _tile(K, 256, PACK_ROW_ALIGN)
    blk0 = row0 // tk

    def body(g_ref, *rest):
        o_ref = rest[-1]
        if gap:
            o_ref[:, front:front_pad] = jnp.zeros((tk, gap), o_ref.dtype)
        for j in range(N_DEV):
            lo, hi = j * n, (j + 1) * n
            if lo < front < hi:
                o_ref[:, lo:front] = g_ref[j, :, 0:front - lo]
                o_ref[:, front_pad:hi + gap] = g_ref[j, :, front - lo:n]
            else:
                off = 0 if hi <= front else gap
                o_ref[:, lo + off:hi + off] = g_ref[j]

    return pl.pallas_call(
        body, name="assemble_w_in", grid=(K // tk,),
        in_specs=[pl.BlockSpec((N_DEV, tk, n), lambda i: (0, i, 0))] + ([] if into is None else [ANY_SPEC]),
        out_specs=pl.BlockSpec((tk, N_DEV * n + gap), lambda i: (blk0 + i, 0)),
        out_shape=jax.ShapeDtypeStruct((rows, N_DEV * n + gap), shards.dtype),
        input_output_aliases={} if into is None else {1: 0},
        compiler_params=_cparams(("parallel",)),
    )(*([shards] if into is None else [shards, into]))


def _split_w_in(w, front, front_pad):
    K, NP = w.shape
    gap = front_pad - front
    n = (NP - gap) // N_DEV
    tk = _tile(K, 256, PACK_ROW_ALIGN)

    def body(w_ref, o_ref):
        for j in range(N_DEV):
            lo, hi = j * n, (j + 1) * n
            if lo < front < hi:
                o_ref[j, :, 0:front - lo] = w_ref[:, lo:front]
                o_ref[j, :, front - lo:n] = w_ref[:, front_pad:hi + gap]
            else:
                off = 0 if hi <= front else gap
                o_ref[j] = w_ref[:, lo + off:hi + off]

    return pl.pallas_call(
        body, name="split_grad_w_in", grid=(K // tk,),
        in_specs=[pl.BlockSpec((tk, NP), lambda i: (i, 0))],
        out_specs=pl.BlockSpec((N_DEV, tk, n), lambda i: (0, i, 0)),
        out_shape=jax.ShapeDtypeStruct((N_DEV, K, n), w.dtype),
        compiler_params=_cparams(("parallel",)),
    )(w)


def _modulate_in(x, mod, ts):
    S, D = x.shape

    def body(x_ref, mod_ref, u_ref):
        u_ref[...] = (x_ref[...] * (1.0 + mod_ref[1:2, :]) + mod_ref[0:1, :]).astype(BF16)

    return pl.pallas_call(
        body, name="modulate_in", grid=(S // ts,),
        in_specs=[pl.BlockSpec((ts, D), lambda i: (i, 0)), pl.BlockSpec((6, D), lambda i: (0, 0))],
        out_specs=pl.BlockSpec((ts, D), lambda i: (i, 0)),
        out_shape=jax.ShapeDtypeStruct((S, D), BF16),
        compiler_params=_cparams(("parallel",)),
    )(x, mod)


def _rms_fwd(proj, g, blk, L, ts, name):
    S = proj.shape[0]

    def body(a_ref, g_ref, y_ref):
        a = a_ref[...].astype(F32)
        r = lax.rsqrt(jnp.mean(a * a, axis=-1, keepdims=True) + RMS_EPS)
        y_ref[...] = (a * r * g_ref[...]).astype(BF16)

    return pl.pallas_call(
        body, name=name, grid=(S // ts,),
        in_specs=[pl.BlockSpec((ts, L), lambda i: (i, blk)), pl.BlockSpec((1, L), lambda i: (0, 0))],
        out_specs=pl.BlockSpec((ts, L), lambda i: (i, 0)),
        out_shape=jax.ShapeDtypeStruct((S, L), BF16),
        compiler_params=_cparams(("parallel",)),
    )(proj, g)


def _rope_partner(x, period, start):
    w = x.shape[-1]
    lane = lax.broadcasted_iota(jnp.int32, x.shape, x.ndim - 1) % period
    first = (lane >= start) & (lane < start + QK_ROPE // 2)
    from_right = pltpu.roll(x, w - QK_ROPE // 2, axis=x.ndim - 1)
    from_left = pltpu.roll(x, QK_ROPE // 2, axis=x.ndim - 1)
    return jnp.where(first, -from_right, from_left)


def _qk_prep(q, kv, proj, kr_blk, cos_q, sin_q, cos_k, sin_k, H, ts):
    S = q.shape[0]
    pair = 2 * QK_CAT
    kv_w = QK_NOPE + V_HEAD

    def body(q_ref, kv_ref, kr_ref, cq_ref, sq_ref, ck_ref, sk_ref, qc_ref, kc_ref, vh_ref):
        kr = kr_ref[...].astype(F32)
        kr = kr * ck_ref[...] + _rope_partner(kr, QK_ROPE, 0) * sk_ref[...]
        kr = kr[:, :QK_ROPE].astype(BF16)
        for p in range(H // 2):
            x = q_ref[:, p * pair:(p + 1) * pair].astype(F32)
            x = x * cq_ref[...] + _rope_partner(x, QK_CAT, QK_NOPE) * sq_ref[...]
            qc_ref[2 * p] = x[:, :QK_CAT].astype(BF16)
            qc_ref[2 * p + 1] = x[:, QK_CAT:].astype(BF16)
        for h in range(H):
            kc_ref[h, :, 0:QK_NOPE] = kv_ref[:, h * kv_w:h * kv_w + QK_NOPE].astype(BF16)
            kc_ref[h, :, QK_NOPE:QK_CAT] = kr
            vh_ref[h, :, :] = kv_ref[:, h * kv_w + QK_NOPE:(h + 1) * kv_w].astype(BF16)

    row = lambda w: pl.BlockSpec((ts, w), lambda i: (i, 0))
    return pl.pallas_call(
        body, name="qk_prep", grid=(S // ts,),
        in_specs=[row(H * QK_CAT), row(H * kv_w),
                  pl.BlockSpec((ts, COL_BLOCK), lambda i: (i, kr_blk)),
                  row(pair), row(pair), row(COL_BLOCK), row(COL_BLOCK)],
        out_specs=[pl.BlockSpec((H, ts, QK_CAT), lambda i: (0, i, 0)),
                   pl.BlockSpec((H, ts, QK_CAT), lambda i: (0, i, 0)),
                   pl.BlockSpec((H, ts, V_HEAD), lambda i: (0, i, 0))],
        out_shape=[jax.ShapeDtypeStruct((H, S, QK_CAT), BF16), jax.ShapeDtypeStruct((H, S, QK_CAT), BF16),
                   jax.ShapeDtypeStruct((H, S, V_HEAD), BF16)],
        compiler_params=_cparams(("parallel",)),
    )(q, kv, proj, cos_q, sin_q, cos_k, sin_k)


NT_DIMS = (((1,), (1,)), ((), ()))
TN_DIMS = (((0,), (0,)), ((), ()))


def _diag_mask(T):
    rows = lax.broadcasted_iota(jnp.int32, (T, T), 0) // CHUNK
    cols = lax.broadcasted_iota(jnp.int32, (T, T), 1) // CHUNK
    return cols <= rows


def _attn_fwd(qc, kc, vh, T):
    H, S, _ = qc.shape
    n = S // T

    def body(q_ref, k_ref, v_ref, o_ref, lse_ref):
        q = q_ref[0]

        def block(i):
            L = (i + 1) * T
            s_old = lax.dot_general(q, k_ref[0, 0:i * T, :], NT_DIMS, preferred_element_type=F32) if i else None
            s_diag = lax.dot_general(q, k_ref[0, i * T:L, :], NT_DIMS, preferred_element_type=F32)
            s_diag = jnp.where(_diag_mask(T), s_diag, NEG_INF)
            m = jnp.max(s_diag, axis=-1, keepdims=True)
            if i:
                m = jnp.maximum(m, jnp.max(s_old, axis=-1, keepdims=True))
            p_diag = jnp.exp((s_diag - m) * ATTN_SCALE)
            l = jnp.sum(p_diag, axis=-1, keepdims=True)
            acc = jnp.dot(p_diag.astype(BF16), v_ref[0, i * T:L, :], preferred_element_type=F32)
            if i:
                p_old = jnp.exp((s_old - m) * ATTN_SCALE)
                l = l + jnp.sum(p_old, axis=-1, keepdims=True)
                acc = acc + jnp.dot(p_old.astype(BF16), v_ref[0, 0:i * T, :], preferred_element_type=F32)
            o_ref[...] = (acc / l).astype(o_ref.dtype)
            lse_ref[0] = m * ATTN_SCALE + jnp.log(l)

        for i in range(n):
            pl.when(pl.program_id(1) == i)(functools.partial(block, i))

    return pl.pallas_call(
        body, name="attn_fwd", grid=(H, n),
        in_specs=[pl.BlockSpec((1, T, QK_CAT), lambda h, i: (h, i, 0)),
                  pl.BlockSpec((1, S, QK_CAT), lambda h, i: (h, 0, 0)),
                  pl.BlockSpec((1, S, V_HEAD), lambda h, i: (h, 0, 0))],
        out_specs=[pl.BlockSpec((T, V_HEAD), lambda h, i: (i, h)),
                   pl.BlockSpec((1, T, 1), lambda h, i: (h, i, 0))],
        out_shape=[jax.ShapeDtypeStruct((S, H * V_HEAD), BF16), jax.ShapeDtypeStruct((H, S, 1), F32)],
        compiler_params=_cparams(("parallel", "arbitrary")),
    )(qc, kc, vh)


def _shift_rows(z, k):
    if k == 0:
        return z
    n = z.shape[0]
    row = lax.broadcasted_iota(jnp.int32, z.shape, 0)
    if k > 0:
        return jnp.where(row >= k, pltpu.roll(z, k, axis=0), 0.0)
    return jnp.where(row < n + k, pltpu.roll(z, n + k, axis=0), 0.0)


def _conv_fwd(proj, w_conv, blk_b, blk_c, blk_x):
    S = proj.shape[0]
    D = w_conv.shape[1]
    nb = D // COL_BLOCK

    def body(cb_ref, cc_ref, cx_ref, w_ref, o_ref):
        z = cc_ref[...].astype(F32) * cx_ref[...].astype(F32)
        conv = w_ref[2:3, :] * z + w_ref[1:2, :] * _shift_rows(z, 1) + w_ref[0:1, :] * _shift_rows(z, 2)
        o_ref[...] = (cb_ref[...].astype(F32) * conv).astype(BF16)

    col = lambda off: pl.BlockSpec((S, COL_BLOCK), lambda j: (0, off + j))
    return pl.pallas_call(
        body, name="conv_fwd", grid=(nb,),
        in_specs=[col(blk_b), col(blk_c), col(blk_x), pl.BlockSpec((CONV_K, COL_BLOCK), lambda j: (0, j))],
        out_specs=pl.BlockSpec((S, COL_BLOCK), lambda j: (0, j)),
        out_shape=jax.ShapeDtypeStruct((S, D), BF16),
        compiler_params=_cparams(("parallel",)),
    )(proj, proj, proj, w_conv)


def _merge_fwd(proj, ya, yb, blk_ga, blk_gb, ts):
    S, D = ya.shape
    nb = D // COL_BLOCK

    def body(ga_ref, gb_ref, ya_ref, yb_ref, o_ref):
        sa, sb = _sigmoid(ga_ref[...].astype(F32)), _sigmoid(gb_ref[...].astype(F32))
        o_ref[...] = (sa * ya_ref[...].astype(F32) + sb * yb_ref[...].astype(F32)).astype(BF16)

    row = pl.BlockSpec((ts, D), lambda i: (i, 0))
    seg = lambda blk: pl.BlockSpec((pl.Element(ts), pl.Element(D)), lambda i: (i * ts, blk * COL_BLOCK))
    return pl.pallas_call(
        body, name="merge_fwd", grid=(S // ts,),
        in_specs=[seg(blk_ga), seg(blk_gb), row, row],
        out_specs=row,
        out_shape=jax.ShapeDtypeStruct((S, D), BF16),
        compiler_params=_cparams(("parallel",)),
    )(proj, proj, ya, yb)


def _ln1_fwd(x, mix, mod, g, b, ts):
    S, D = x.shape

    def body(x_ref, mix_ref, mod_ref, g_ref, b_ref, xhat_ref, rstd_ref, u2_ref):
        r = DEEPNORM_ALPHA * x_ref[...] + mod_ref[2:3, :] * mix_ref[...]
        mu = jnp.mean(r, axis=-1, keepdims=True)
        d = r - mu
        rstd = lax.rsqrt(jnp.mean(d * d, axis=-1, keepdims=True) + LN_EPS)
        xhat = d * rstd
        xhat_ref[...] = xhat
        rstd_ref[...] = rstd
        x1 = xhat * g_ref[...] + b_ref[...]
        u2_ref[...] = (x1 * (1.0 + mod_ref[4:5, :]) + mod_ref[3:4, :]).astype(BF16)

    row = pl.BlockSpec((ts, D), lambda i: (i, 0))
    vec = lambda r: pl.BlockSpec((r, D), lambda i: (0, 0))
    return pl.pallas_call(
        body, name="ln1_fwd", grid=(S // ts,),
        in_specs=[row, row, vec(6), vec(1), vec(1)],
        out_specs=[row, pl.BlockSpec((ts, 1), lambda i: (i, 0)), row],
        out_shape=[jax.ShapeDtypeStruct((S, D), F32), jax.ShapeDtypeStruct((S, 1), F32),
                   jax.ShapeDtypeStruct((S, D), BF16)],
        compiler_params=_cparams(("parallel",)),
    )(x, mix, mod, g, b)


def _swiglu_fwd(h, ts, tb):
    S, F2 = h.shape
    F = F2 // 2
    nb = F // tb

    def body(hg_ref, hu_ref, a_ref):
        hg = hg_ref[...].astype(F32)
        a_ref[...] = (hg * _sigmoid(hg) * hu_ref[...].astype(F32)).astype(BF16)

    return pl.pallas_call(
        body, name="swiglu_fwd", grid=(S // ts, nb),
        in_specs=[pl.BlockSpec((ts, tb), lambda i, j: (i, j)), pl.BlockSpec((ts, tb), lambda i, j: (i, j + nb))],
        out_specs=pl.BlockSpec((ts, tb), lambda i, j: (i, j)),
        out_shape=jax.ShapeDtypeStruct((S, F), BF16),
        compiler_params=_cparams(("parallel", "parallel")),
    )(h, h)


def _ln2_loss(xhat1, ffn, tgt, mod, g1, b1, g2, b2, ts):
    S, D = xhat1.shape

    def body(xh_ref, ffn_ref, t_ref, mod_ref, g1_ref, b1_ref, g2_ref, b2_ref, loss_ref, dffn_ref, dx1_ref, vec_ref):
        i = pl.program_id(0)

        @pl.when(i == 0)
        def _():
            loss_ref[...] = jnp.zeros_like(loss_ref)
            vec_ref[...] = jnp.zeros_like(vec_ref)

        x1 = xh_ref[...] * g1_ref[...] + b1_ref[...]
        ffn = ffn_ref[...]
        r = DEEPNORM_ALPHA * x1 + mod_ref[5:6, :] * ffn
        mu = jnp.mean(r, axis=-1, keepdims=True)
        d = r - mu
        rstd = lax.rsqrt(jnp.mean(d * d, axis=-1, keepdims=True) + LN_EPS)
        xhat = d * rstd
        e = xhat * g2_ref[...] + b2_ref[...] - t_ref[...]
        loss_ref[...] += 0.5 * jnp.sum(jnp.mean(e * e, axis=-1, keepdims=True))
        dy = e * (1.0 / D)
        dxhat = dy * g2_ref[...]
        dr = rstd * (dxhat - jnp.mean(dxhat, axis=-1, keepdims=True)
                     - xhat * jnp.mean(dxhat * xhat, axis=-1, keepdims=True))
        dffn_ref[...] = (dr * mod_ref[5:6, :]).astype(BF16)
        dx1_ref[...] = DEEPNORM_ALPHA * dr
        vec_ref[0:1, :] += jnp.sum(dy * xhat, axis=0, keepdims=True)
        vec_ref[1:2, :] += jnp.sum(dy, axis=0, keepdims=True)
        vec_ref[2:3, :] += jnp.sum(dr * ffn, axis=0, keepdims=True)

    row = pl.BlockSpec((ts, D), lambda i: (i, 0))
    vec = lambda r: pl.BlockSpec((r, D), lambda i: (0, 0))
    return pl.pallas_call(
        body, name="ln2_loss", grid=(S // ts,),
        in_specs=[row, row, row, vec(6), vec(1), vec(1), vec(1), vec(1)],
        out_specs=[pl.BlockSpec((1, LANE), lambda i: (0, 0)), row, row, vec(8)],
        out_shape=[jax.ShapeDtypeStruct((1, LANE), F32), jax.ShapeDtypeStruct((S, D), BF16),
                   jax.ShapeDtypeStruct((S, D), F32), jax.ShapeDtypeStruct((8, D), F32)],
        compiler_params=_cparams(("arbitrary",)),
    )(xhat1, ffn, tgt, mod, g1, b1, g2, b2)


def _swiglu_bwd(da, h, ts, tb):
    S, F2 = h.shape
    nb = (F2 // 2) // tb

    def body(da_ref, hg_ref, hu_ref, dh_ref):
        hg, da = hg_ref[...].astype(F32), da_ref[...].astype(F32)
        sg = _sigmoid(hg)

        @pl.when(pl.program_id(2) == 0)
        def _():
            dh_ref[...] = (da * hu_ref[...].astype(F32) * (sg * (1.0 + hg * (1.0 - sg)))).astype(BF16)

        @pl.when(pl.program_id(2) == 1)
        def _():
            dh_ref[...] = (da * hg * sg).astype(BF16)

    lo = pl.BlockSpec((ts, tb), lambda i, j, k: (i, j))
    hi = pl.BlockSpec((ts, tb), lambda i, j, k: (i, j + nb))
    return pl.pallas_call(
        body, name="swiglu_bwd", grid=(S // ts, nb, 2),
        in_specs=[lo, lo, hi],
        out_specs=pl.BlockSpec((ts, tb), lambda i, j, k: (i, j + nb * k)),
        out_shape=jax.ShapeDtypeStruct((S, F2), BF16),
        compiler_params=_cparams(("parallel", "parallel", "arbitrary")),
    )(da, h, h)


def _ln1_bwd(du2, dx1a, xhat1, rstd1, mix, mod, g1, b1, ts):
    S, D = xhat1.shape

    def body(du2_ref, dx1a_ref, xh_ref, rstd_ref, mix_ref, mod_ref, g_ref, b_ref, dxa_ref, dmix_ref, vec_ref):
        i = pl.program_id(0)

        @pl.when(i == 0)
        def _():
            vec_ref[...] = jnp.zeros_like(vec_ref)

        xhat, du2, mix = xh_ref[...], du2_ref[...], mix_ref[...]
        x1 = xhat * g_ref[...] + b_ref[...]
        dx1 = dx1a_ref[...] + du2 * (1.0 + mod_ref[4:5, :])
        dxhat = dx1 * g_ref[...]
        dr = rstd_ref[...] * (dxhat - jnp.mean(dxhat, axis=-1, keepdims=True)
                              - xhat * jnp.mean(dxhat * xhat, axis=-1, keepdims=True))
        dxa_ref[...] = DEEPNORM_ALPHA * dr
        dmix_ref[...] = (dr * mod_ref[2:3, :]).astype(BF16)
        vec_ref[0:1, :] += jnp.sum(du2, axis=0, keepdims=True)
        vec_ref[1:2, :] += jnp.sum(du2 * x1, axis=0, keepdims=True)
        vec_ref[2:3, :] += jnp.sum(dx1 * xhat, axis=0, keepdims=True)
        vec_ref[3:4, :] += jnp.sum(dx1, axis=0, keepdims=True)
        vec_ref[4:5, :] += jnp.sum(dr * mix, axis=0, keepdims=True)

    row = pl.BlockSpec((ts, D), lambda i: (i, 0))
    vec = lambda r: pl.BlockSpec((r, D), lambda i: (0, 0))
    return pl.pallas_call(
        body, name="ln1_bwd", grid=(S // ts,),
        in_specs=[row, row, row, pl.BlockSpec((ts, 1), lambda i: (i, 0)), row, vec(6), vec(1), vec(1)],
        out_specs=[row, row, vec(8)],
        out_shape=[jax.ShapeDtypeStruct((S, D), F32), jax.ShapeDtypeStruct((S, D), BF16),
                   jax.ShapeDtypeStruct((8, D), F32)],
        compiler_params=_cparams(("arbitrary",)),
    )(du2, dx1a, xhat1, rstd1, mix, mod, g1, b1)


def _merge_bwd(dmerged, proj, ya, yb, blk_ga, blk_gb, ts):
    S, D = ya.shape
    nb = D // COL_BLOCK

    def body(dm_ref, ga_ref, gb_ref, ya_ref, yb_ref, dya_ref, dyb_ref, dga_ref, dgb_ref):
        dm = dm_ref[...].astype(F32)
        sa, sb = _sigmoid(ga_ref[...].astype(F32)), _sigmoid(gb_ref[...].astype(F32))
        dya_ref[...] = (dm * sa).astype(BF16)
        dyb_ref[...] = (dm * sb).astype(BF16)
        dga_ref[...] = (dm * ya_ref[...].astype(F32) * sa * (1.0 - sa)).astype(BF16)
        dgb_ref[...] = (dm * yb_ref[...].astype(F32) * sb * (1.0 - sb)).astype(BF16)

    row = pl.BlockSpec((ts, D), lambda i: (i, 0))
    seg = lambda blk: pl.BlockSpec((pl.Element(ts), pl.Element(D)), lambda i: (i * ts, blk * COL_BLOCK))
    out = jax.ShapeDtypeStruct((S, D), BF16)
    return pl.pallas_call(
        body, name="merge_bwd", grid=(S // ts,),
        in_specs=[row, seg(blk_ga), seg(blk_gb), row, row],
        out_specs=[row] * 4,
        out_shape=[out] * 4,
        compiler_params=_cparams(("parallel",)),
    )(dmerged, proj, proj, ya, yb)


def _conv_bwd(dcbc, proj, w_conv, blk_b, blk_c, blk_x):
    S = proj.shape[0]
    D = w_conv.shape[1]
    nb = D // COL_BLOCK

    def body(d_ref, cb_ref, cc_ref, cx_ref, w_ref, dcb_ref, dcc_ref, dcx_ref, dw_ref):
        d, cc, cx = d_ref[...].astype(F32), cc_ref[...].astype(F32), cx_ref[...].astype(F32)
        z = cc * cx
        z1, z2 = _shift_rows(z, 1), _shift_rows(z, 2)
        conv = w_ref[2:3, :] * z + w_ref[1:2, :] * z1 + w_ref[0:1, :] * z2
        dcb_ref[...] = (d * conv).astype(BF16)
        dconv = d * cb_ref[...].astype(F32)
        dz = w_ref[2:3, :] * dconv + w_ref[1:2, :] * _shift_rows(dconv, -1) + w_ref[0:1, :] * _shift_rows(dconv, -2)
        dcc_ref[...] = (dz * cx).astype(BF16)
        dcx_ref[...] = (dz * cc).astype(BF16)
        dw_ref[...] = jnp.zeros_like(dw_ref)
        dw_ref[0:1, :] = jnp.sum(dconv * z2, axis=0, keepdims=True)
        dw_ref[1:2, :] = jnp.sum(dconv * z1, axis=0, keepdims=True)
        dw_ref[2:3, :] = jnp.sum(dconv * z, axis=0, keepdims=True)

    col = lambda off: pl.BlockSpec((S, COL_BLOCK), lambda j: (0, off + j))
    out = jax.ShapeDtypeStruct((S, D), BF16)
    return pl.pallas_call(
        body, name="conv_bwd", grid=(nb,),
        in_specs=[col(0), col(blk_b), col(blk_c), col(blk_x), pl.BlockSpec((CONV_K, COL_BLOCK), lambda j: (0, j))],
        out_specs=[col(0), col(0), col(0), pl.BlockSpec((8, COL_BLOCK), lambda j: (0, j))],
        out_shape=[out, out, out, jax.ShapeDtypeStruct((8, D), F32)],
        compiler_params=_cparams(("parallel",)),
    )(dcbc, proj, proj, proj, w_conv)


def _attn_bwd(qc, kc, vh, do, o, lse, T):
    H, S, _ = qc.shape
    n = S // T

    def body(q_ref, k_ref, v_ref, do_ref, o_ref, lse_ref, dq_ref, dk_ref, dv_ref, d_ref, dq_acc, dk_acc, dv_acc):
        j = pl.program_id(1)

        @pl.when(j == 0)
        def _():
            dq_acc[...] = jnp.zeros_like(dq_acc)
            d_ref[...] = jnp.sum(do_ref[...].astype(F32) * o_ref[...].astype(F32), axis=-1, keepdims=True)

        dk_acc[...] = jnp.zeros_like(dk_acc)
        dv_acc[...] = jnp.zeros_like(dv_acc)
        k, v = k_ref[0], v_ref[0]

        def step(i, masked):
            rows = pl.ds(pl.multiple_of(i * T, T), T)
            q = q_ref[0, rows, :]
            do = do_ref[rows, :].astype(BF16)
            s = lax.dot_general(q, k, NT_DIMS, preferred_element_type=F32) * ATTN_SCALE
            if masked:
                s = jnp.where(_diag_mask(T), s, NEG_INF)
            p = jnp.exp(s - lse_ref[0, rows, :])
            dv_acc[...] += lax.dot_general(p.astype(BF16), do, TN_DIMS, preferred_element_type=F32)
            dp = lax.dot_general(do, v, NT_DIMS, preferred_element_type=F32)
            ds = (p * (dp - d_ref[rows, :]) * ATTN_SCALE).astype(BF16)
            dk_acc[...] += lax.dot_general(ds, q, TN_DIMS, preferred_element_type=F32)
            dq_acc[rows, :] += jnp.dot(ds, k, preferred_element_type=F32)

        def above(i, carry):
            step(i, False)
            return carry

        step(j, True)
        lax.fori_loop(j + 1, n, above, 0)
        dk_ref[0] = dk_acc[...].astype(BF16)
        dv_ref[0] = dv_acc[...].astype(BF16)

        @pl.when(j == n - 1)
        def _():
            dq_ref[0] = dq_acc[...].astype(BF16)

    head = lambda w: pl.BlockSpec((1, S, w), lambda h, j: (h, 0, 0))
    blk = lambda w: pl.BlockSpec((1, T, w), lambda h, j: (h, j, 0))
    ospec = pl.BlockSpec((S, V_HEAD), lambda h, j: (0, h))
    return pl.pallas_call(
        body, name="attn_bwd", grid=(H, n),
        in_specs=[head(QK_CAT), blk(QK_CAT), blk(V_HEAD), ospec, ospec, head(1)],
        out_specs=[head(QK_CAT), blk(QK_CAT), blk(V_HEAD)],
        out_shape=[jax.ShapeDtypeStruct((H, S, QK_CAT), BF16), jax.ShapeDtypeStruct((H, S, QK_CAT), BF16),
                   jax.ShapeDtypeStruct((H, S, V_HEAD), BF16)],
        scratch_shapes=[pltpu.VMEM((S, 1), F32), pltpu.VMEM((S, QK_CAT), F32), pltpu.VMEM((T, QK_CAT), F32),
                        pltpu.VMEM((T, V_HEAD), F32)],
        compiler_params=_cparams(("parallel", "arbitrary")),
    )(qc, kc, vh, do, o, lse)


def _qk_bwd(dqc, dkc, dvh, cos_q, sin_q, cos_k, sin_k, ts):
    H, S, _ = dqc.shape
    pair = 2 * QK_CAT
    kv_w = QK_NOPE + V_HEAD

    def body(dqc_ref, dkc_ref, dvh_ref, cq_ref, sq_ref, ck_ref, sk_ref, dq_ref, dkv_ref, dkr_ref, q_buf, kr_buf):
        for p in range(H // 2):
            q_buf[:, :QK_CAT] = dqc_ref[2 * p].astype(F32)
            q_buf[:, QK_CAT:] = dqc_ref[2 * p + 1].astype(F32)
            g = q_buf[...]
            dq_ref[:, p * pair:(p + 1) * pair] = (
                g * cq_ref[...] - _rope_partner(g, QK_CAT, QK_NOPE) * sq_ref[...]).astype(BF16)
        kr_sum = jnp.zeros((ts, QK_ROPE), F32)
        for h in range(H):
            dkv_ref[:, h * kv_w:h * kv_w + QK_NOPE] = dkc_ref[h, :, 0:QK_NOPE].astype(BF16)
            dkv_ref[:, h * kv_w + QK_NOPE:(h + 1) * kv_w] = dvh_ref[h].astype(BF16)
            kr_sum = kr_sum + dkc_ref[h, :, QK_NOPE:QK_CAT]
        kr_buf[...] = jnp.zeros_like(kr_buf)
        kr_buf[:, 0:QK_ROPE] = kr_sum
        kr = kr_buf[...]
        dkr_ref[...] = (kr * ck_ref[...] - _rope_partner(kr, QK_ROPE, 0) * sk_ref[...]).astype(BF16)

    row = lambda w: pl.BlockSpec((ts, w), lambda i: (i, 0))
    head = lambda w: pl.BlockSpec((H, ts, w), lambda i: (0, i, 0))
    return pl.pallas_call(
        body, name="qk_bwd", grid=(S // ts,),
        in_specs=[head(QK_CAT), head(QK_CAT), head(V_HEAD), row(pair), row(pair), row(COL_BLOCK), row(COL_BLOCK)],
        out_specs=[row(H * QK_CAT), row(H * kv_w), row(COL_BLOCK)],
        out_shape=[jax.ShapeDtypeStruct((S, H * QK_CAT), BF16), jax.ShapeDtypeStruct((S, H * kv_w), BF16),
                   jax.ShapeDtypeStruct((S, COL_BLOCK), BF16)],
        scratch_shapes=[pltpu.VMEM((ts, pair), F32), pltpu.VMEM((ts, COL_BLOCK), F32)],
        compiler_params=_cparams(("parallel",)),
    )(dqc, dkc, dvh, cos_q, sin_q, cos_k, sin_k)


def _rms_bwd(dy, proj, g, blk, L, ts, name):
    S = proj.shape[0]

    def body(dy_ref, a_ref, g_ref, da_ref, dg_ref):
        i = pl.program_id(0)

        @pl.when(i == 0)
        def _():
            dg_ref[...] = jnp.zeros_like(dg_ref)

        a, dy = a_ref[...].astype(F32), dy_ref[...]
        r = lax.rsqrt(jnp.mean(a * a, axis=-1, keepdims=True) + RMS_EPS)
        dyh = dy * g_ref[...]
        da = r * dyh - a * (r * r * r) * jnp.mean(dyh * a, axis=-1, keepdims=True)
        da_ref[...] = da.astype(BF16)
        dg_ref[0:1, :] += jnp.sum(dy * a * r, axis=0, keepdims=True)

    return pl.pallas_call(
        body, name=name, grid=(S // ts,),
        in_specs=[pl.BlockSpec((ts, L), lambda i: (i, 0)), pl.BlockSpec((ts, L), lambda i: (i, blk)),
                  pl.BlockSpec((1, L), lambda i: (0, 0))],
        out_specs=[pl.BlockSpec((ts, L), lambda i: (i, 0)), pl.BlockSpec((8, L), lambda i: (0, 0))],
        out_shape=[jax.ShapeDtypeStruct((S, L), BF16), jax.ShapeDtypeStruct((8, L), F32)],
        compiler_params=_cparams(("arbitrary",)),
    )(dy, proj, g)


def _grad_x(du, dxa, x, mod, ts):
    S, D = x.shape

    def body(du_ref, dxa_ref, x_ref, mod_ref, dx_ref, vec_ref):
        i = pl.program_id(0)

        @pl.when(i == 0)
        def _():
            vec_ref[...] = jnp.zeros_like(vec_ref)

        du = du_ref[...]
        dx_ref[...] = dxa_ref[...] + du * (1.0 + mod_ref[1:2, :])
        vec_ref[0:1, :] += jnp.sum(du, axis=0, keepdims=True)
        vec_ref[1:2, :] += jnp.sum(du * x_ref[...], axis=0, keepdims=True)

    row = pl.BlockSpec((ts, D), lambda i: (i, 0))
    vec = lambda r: pl.BlockSpec((r, D), lambda i: (0, 0))
    return pl.pallas_call(
        body, name="grad_x", grid=(S // ts,),
        in_specs=[row, row, row, vec(6)],
        out_specs=[row, vec(8)],
        out_shape=[jax.ShapeDtypeStruct((S, D), F32), jax.ShapeDtypeStruct((8, D), F32)],
        compiler_params=_cparams(("arbitrary",)),
    )(du, dxa, x, mod)


def _adamw(w, g, m, v, name):
    R, C = w.shape
    tr = _tile(R, max(8, (1 << 19) // C), 8)
    c1 = 1.0 / (1.0 - ADAM_B1 ** ADAM_STEP)
    c2 = 1.0 / (1.0 - ADAM_B2 ** ADAM_STEP)

    def body(w_ref, g_ref, m_ref, v_ref, d_ref, nm_ref, nv_ref):
        g = g_ref[...]
        m = ADAM_B1 * m_ref[...] + (1.0 - ADAM_B1) * g
        v = ADAM_B2 * v_ref[...] + (1.0 - ADAM_B2) * (g * g)
        nm_ref[...] = m
        nv_ref[...] = v
        d_ref[...] = -ADAM_LR * ((m * c1) / (jnp.sqrt(v * c2) + ADAM_EPS) + ADAM_WD * w_ref[...])

    spec = pl.BlockSpec((tr, C), lambda i: (i, 0))
    out = jax.ShapeDtypeStruct((R, C), F32)
    return pl.pallas_call(
        body, name=name, grid=(R // tr,),
        in_specs=[spec] * 4, out_specs=[spec] * 3, out_shape=[out] * 3,
        compiler_params=_cparams(("parallel",)),
    )(w, g, m, v)


def _adamw_ada(w, cact_t, dmod, m, v):
    R, C = w.shape
    tr = _tile(R, max(8, (1 << 18) // C), 8)
    c1 = 1.0 / (1.0 - ADAM_B1 ** ADAM_STEP)
    c2 = 1.0 / (1.0 - ADAM_B2 ** ADAM_STEP)

    def body(w_ref, ct_ref, dm_ref, m_ref, v_ref, g_ref, d_ref, nm_ref, nv_ref):
        ct = ct_ref[...].astype(BF16).astype(F32)
        dm = dm_ref[...].astype(BF16).astype(F32)
        g = ct[:, 0:1] * dm[0:1, :]
        for b in range(1, N_DEV):
            g = g + ct[:, b:b + 1] * dm[b:b + 1, :]
        m = ADAM_B1 * m_ref[...] + (1.0 - ADAM_B1) * g
        v = ADAM_B2 * v_ref[...] + (1.0 - ADAM_B2) * (g * g)
        g_ref[...] = g
        nm_ref[...] = m
        nv_ref[...] = v
        d_ref[...] = -ADAM_LR * ((m * c1) / (jnp.sqrt(v * c2) + ADAM_EPS) + ADAM_WD * w_ref[...])

    spec = pl.BlockSpec((tr, C), lambda i: (i, 0))
    out = jax.ShapeDtypeStruct((R, C), F32)
    return pl.pallas_call(
        body, name="adamw_w_ada", grid=(R // tr,),
        in_specs=[spec, pl.BlockSpec((tr, N_DEV), lambda i: (i, 0)), pl.BlockSpec((N_DEV, C), lambda i: (0, 0)),
                  spec, spec],
        out_specs=[spec] * 4, out_shape=[out] * 4,
        compiler_params=_cparams(("parallel",)),
    )(w, cact_t, dmod, m, v)


def _adamw_reduced(w, own, got, m, v, my_chip, name):
    R, C = w.shape
    tr = _tile(R, max(PACK_ROW_ALIGN, (1 << 18) // C), PACK_ROW_ALIGN)
    c1 = 1.0 / (1.0 - ADAM_B1 ** ADAM_STEP)
    c2 = 1.0 / (1.0 - ADAM_B2 ** ADAM_STEP)

    def body(chip_ref, w_ref, own_ref, g1_ref, g2_ref, g3_ref, m_ref, v_ref, g_ref, d_ref, nm_ref, nv_ref):
        g = own_ref[0].astype(F32) + g1_ref[0].astype(F32) + g2_ref[0].astype(F32) + g3_ref[0].astype(F32)
        m = ADAM_B1 * m_ref[...] + (1.0 - ADAM_B1) * g
        v = ADAM_B2 * v_ref[...] + (1.0 - ADAM_B2) * (g * g)
        g_ref[...] = g
        nm_ref[...] = m
        nv_ref[...] = v
        d_ref[...] = -ADAM_LR * ((m * c1) / (jnp.sqrt(v * c2) + ADAM_EPS) + ADAM_WD * w_ref[...])

    spec = pl.BlockSpec((tr, C), lambda i, chip: (i, 0))
    slot = lambda k: pl.BlockSpec((1, tr, C), lambda i, chip: (chip[0] ^ k, i, 0))
    out = jax.ShapeDtypeStruct((R, C), F32)
    return pl.pallas_call(
        body, name=name,
        grid_spec=pltpu.PrefetchScalarGridSpec(
            num_scalar_prefetch=1, grid=(R // tr,),
            in_specs=[spec, slot(0), slot(1), slot(2), slot(3), spec, spec],
            out_specs=[spec] * 4),
        out_shape=[out] * 4,
        compiler_params=_cparams(("parallel",)),
    )(my_chip, w, own, got, got, got, m, v)


def _my_place():
    return lax.axis_index("x"), lax.axis_index("y"), lax.axis_index("c")


def _peer(k):
    x, y, c = _my_place()
    return (x ^ ((k >> 2) & 1), y ^ ((k >> 1) & 1), c ^ (k & 1))


def _linear(place):
    return 4 * place[0] + 2 * place[1] + place[2]


def _ada_fwd(c_row, wconv_row, w_ada, b_row):
    D, CW = w_ada.shape
    WC = wconv_row.shape[-1]

    def body(c_ref, wc_ref, w_ref, b_ref, mod_ref, cact_ref, wcall_ref, send_buf, sems):
        me = _linear(_my_place())
        c = c_ref[0]
        cact_ref[me] = c * _sigmoid(c)
        wcall_ref[me] = wc_ref[0]

        def gather_copy(buf, k, grp):
            return pltpu.make_async_remote_copy(
                src_ref=buf.at[me], dst_ref=buf.at[me], send_sem=sems.at[0, grp, k], recv_sem=sems.at[1, grp, k],
                device_id=_peer(k), device_id_type=MESH_ID)

        def gather_recv(buf, k, grp):
            src = _linear(_peer(k))
            return pltpu.make_async_remote_copy(
                src_ref=buf.at[src], dst_ref=buf.at[src], send_sem=sems.at[0, grp, k], recv_sem=sems.at[1, grp, k],
                device_id=_peer(k), device_id_type=MESH_ID)

        for k in range(1, N_DEV):
            gather_copy(cact_ref, k, 0).start()
            gather_copy(wcall_ref, k, 1).start()
        for k in range(1, N_DEV):
            gather_recv(cact_ref, k, 0).wait_recv()
            gather_recv(wcall_ref, k, 1).wait_recv()
        for k in range(1, N_DEV):
            gather_copy(cact_ref, k, 0).wait_send()
            gather_copy(wcall_ref, k, 1).wait_send()

        cact = jnp.concatenate([cact_ref[b] for b in range(N_DEV)], axis=0)
        mod_all = jnp.dot(cact.astype(BF16), w_ref[...].astype(BF16), preferred_element_type=F32) + b_ref[0]
        for b in range(N_DEV):
            send_buf[b] = mod_all[b:b + 1, :]
        mod_ref[me] = send_buf[me]

        def scatter_copy(k):
            dst = _linear(_peer(k))
            return pltpu.make_async_remote_copy(
                src_ref=send_buf.at[dst], dst_ref=mod_ref.at[me], send_sem=sems.at[0, 2, k], recv_sem=sems.at[1, 2, k],
                device_id=_peer(k), device_id_type=MESH_ID)

        def scatter_recv(k):
            src = _linear(_peer(k))
            return pltpu.make_async_remote_copy(
                src_ref=send_buf.at[src], dst_ref=mod_ref.at[src], send_sem=sems.at[0, 2, k], recv_sem=sems.at[1, 2, k],
                device_id=_peer(k), device_id_type=MESH_ID)

        for k in range(1, N_DEV):
            scatter_copy(k).start()
        for k in range(1, N_DEV):
            scatter_recv(k).wait_recv()
        for k in range(1, N_DEV):
            scatter_copy(k).wait_send()

    vmem = pl.BlockSpec(memory_space=pltpu.VMEM)
    return pl.pallas_call(
        body, name="ada_fwd",
        in_specs=[vmem] * 4, out_specs=[vmem] * 3,
        out_shape=[jax.ShapeDtypeStruct((N_DEV, 1, CW), F32), jax.ShapeDtypeStruct((N_DEV, 1, D), F32),
                   jax.ShapeDtypeStruct((N_DEV, 1, WC), F32)],
        scratch_shapes=[pltpu.VMEM((N_DEV, 1, CW), F32), pltpu.SemaphoreType.DMA((2, 3, N_DEV))],
        compiler_params=pltpu.CompilerParams(vmem_limit_bytes=VMEM_LIMIT),
    )(c_row, wconv_row, w_ada, b_row)


def _ada_bwd(payload, deps=()):
    NCH, _, CW = payload.shape

    def body(p_ref, *rest):
        sum_ref, mine_ref, all_ref, sems = rest[-4:]
        me = _linear(_my_place())
        all_ref[me] = p_ref[...]

        def copy(k, slot):
            return pltpu.make_async_remote_copy(
                src_ref=all_ref.at[slot], dst_ref=all_ref.at[slot], send_sem=sems.at[0, k], recv_sem=sems.at[1, k],
                device_id=_peer(k), device_id_type=MESH_ID)

        for k in range(1, N_DEV):
            copy(k, me).start()
        for k in range(1, N_DEV):
            copy(k, _linear(_peer(k))).wait_recv()
        for k in range(1, N_DEV):
            copy(k, me).wait_send()

        total = all_ref[0]
        for b in range(1, N_DEV):
            total = total + all_ref[b]
        sum_ref[...] = total

        for b in range(N_DEV):
            mine_ref[b] = all_ref[b, me]

    vmem = pl.BlockSpec(memory_space=pltpu.VMEM)
    return pl.pallas_call(
        body, name="ada_bwd",
        in_specs=[vmem] + [ANY_SPEC] * len(deps), out_specs=[vmem, vmem],
        out_shape=[jax.ShapeDtypeStruct((NCH, 1, CW), F32), jax.ShapeDtypeStruct((N_DEV, 1, CW), F32)],
        scratch_shapes=[pltpu.VMEM((N_DEV, NCH, 1, CW), F32), pltpu.SemaphoreType.DMA((2, N_DEV))],
        compiler_params=pltpu.CompilerParams(vmem_limit_bytes=VMEM_LIMIT),
    )(payload, *deps)


def _exchange_in_chip(parts):
    W = len(parts)

    def body(*refs):
        p_refs, got_refs, (send_sems, recv_sems) = refs[:W], refs[W:2 * W], refs[2 * W:]
        x, y, c = _my_place()
        sibling = (x, y, 1 - c)
        copies = []
        for w in range(W):
            for q in range(4):
                copies.append(pltpu.make_async_remote_copy(
                    src_ref=p_refs[w].at[2 * q + (1 - c)], dst_ref=got_refs[w].at[q],
                    send_sem=send_sems.at[4 * w + q], recv_sem=recv_sems.at[4 * w + q],
                    device_id=sibling, device_id_type=MESH_ID))
        for cp in copies:
            cp.start()
        for cp in copies:
            cp.wait_recv()
        for cp in copies:
            cp.wait_send()

    return pl.pallas_call(
        body, name="grad_exchange_in_chip",
        in_specs=[HBM_SPEC] * W, out_specs=[HBM_SPEC] * W,
        out_shape=[jax.ShapeDtypeStruct((4,) + p.shape[1:], p.dtype) for p in parts],
        scratch_shapes=[pltpu.SemaphoreType.DMA((4 * W,)), pltpu.SemaphoreType.DMA((4 * W,))],
    )(*parts)


def _pair_sum(parts, got, core):
    _, R, C = parts.shape
    tr = _tile(R, max(PACK_ROW_ALIGN, PAIR_SUM_BLOCK // C), PACK_ROW_ALIGN)

    def body(c_ref, p_ref, g_ref, o_ref):
        o_ref[...] = (p_ref[...].astype(F32) + g_ref[...].astype(F32)).astype(o_ref.dtype)

    return pl.pallas_call(
        body, name="grad_pair_sum",
        grid_spec=pltpu.PrefetchScalarGridSpec(
            num_scalar_prefetch=1, grid=(4, R // tr),
            in_specs=[pl.BlockSpec((1, tr, C), lambda q, i, c_ref: (2 * q + c_ref[0], i, 0)),
                      pl.BlockSpec((1, tr, C), lambda q, i, c_ref: (q, i, 0))],
            out_specs=pl.BlockSpec((1, tr, C), lambda q, i, c_ref: (q, i, 0))),
        out_shape=jax.ShapeDtypeStruct((4, R, C), parts.dtype),
        compiler_params=_cparams(("parallel", "parallel")),
    )(core, parts, got)


HBM_SPEC = pl.BlockSpec(memory_space=pltpu.HBM)
SEM_SPEC = pl.BlockSpec(memory_space=pltpu.SEMAPHORE)
ANY_SPEC = pl.BlockSpec(memory_space=pl.ANY)
SPLIT_EFFECT = pltpu.SideEffectType.DATAFLOW_SIDE_EFFECTING


def _landing_zone(shape, dtype):
    return pltpu.with_memory_space_constraint(lax.empty(shape, dtype), pltpu.HBM)


def _split_start(name, arrays, lands, after, copies_of, per_array):
    W = len(arrays)
    after = tuple(after) if isinstance(after, (tuple, list)) else (after,)

    def body(*refs):
        x_refs, land_refs = refs[:W], refs[W:2 * W]
        send_sems, recv_sems = refs[2 * W + len(after)], refs[2 * W + len(after) + 1]
        token = refs[-1]
        k = 0
        for w in range(W):
            for src, dst, dev in copies_of(w, x_refs[w], land_refs[w]):
                pltpu.make_async_remote_copy(src_ref=src, dst_ref=dst, send_sem=send_sems.at[k], recv_sem=recv_sems.at[k],
                                             device_id=dev, device_id_type=MESH_ID).start()
                k += 1
        token[...] = jnp.zeros_like(token)

    n_copies = per_array * W
    hbm_of = lambda xs: tuple(pltpu.HBM(a.shape, a.dtype) for a in xs)
    out = pl.pallas_call(
        body, name=name,
        out_shape=(pltpu.SemaphoreType.DMA((n_copies,)), pltpu.SemaphoreType.DMA((n_copies,)))
        + hbm_of(arrays) + hbm_of(lands) + (jax.ShapeDtypeStruct((8, LANE), F32),),
        in_specs=(HBM_SPEC,) * (2 * W) + (ANY_SPEC,) * len(after),
        out_specs=(SEM_SPEC, SEM_SPEC) + (HBM_SPEC,) * (2 * W) + (pl.BlockSpec(memory_space=pltpu.VMEM),),
        input_output_aliases={i: 2 + i for i in range(2 * W)},
        compiler_params=pltpu.CompilerParams(has_side_effects=SPLIT_EFFECT),
    )(*[pltpu.with_memory_space_constraint(a, pltpu.HBM) for a in arrays], *lands, *after)
    return out[0], out[1], list(out[2:2 + W]), list(out[2 + W:2 + 2 * W]), out[-1]


def _split_wait(name, state, after, copies_of):
    send_sems, recv_sems, arrays, lands, _ = state
    W = len(arrays)
    after = tuple(after) if isinstance(after, (tuple, list)) else (after,)

    def body(*refs):
        x_refs, land_refs = refs[:W], refs[W:2 * W]
        send_sems, recv_sems = refs[2 * W], refs[2 * W + 1]
        k = 0
        for w in range(W):
            for src, dst, dev in copies_of(w, x_refs[w], land_refs[w]):
                cp = pltpu.make_async_remote_copy(src_ref=src, dst_ref=dst, send_sem=send_sems.at[k],
                                                  recv_sem=recv_sems.at[k], device_id=dev, device_id_type=MESH_ID)
                cp.wait_send()
                cp.wait_recv()
                k += 1

    out = pl.pallas_call(
        body, name=name,
        out_shape=tuple(pltpu.HBM(a.shape, a.dtype) for a in arrays + lands),
        in_specs=(HBM_SPEC,) * (2 * W) + (SEM_SPEC, SEM_SPEC) + (ANY_SPEC,) * len(after),
        out_specs=(HBM_SPEC,) * (2 * W),
        input_output_aliases={i: i for i in range(2 * W)},
        compiler_params=pltpu.CompilerParams(has_side_effects=SPLIT_EFFECT),
    )(*arrays, *lands, send_sems, recv_sems, *after)
    return list(out[:W]), list(out[W:])


def _scatter_copies(w, p_ref, land_ref):
    x, y, c = _my_place()
    my_chip = 2 * x + y
    return [(p_ref.at[2 * (x ^ (k >> 1)) + (y ^ (k & 1))], land_ref.at[my_chip], (x ^ (k >> 1), y ^ (k & 1), c))
            for k in range(1, 4)]


def _gather_copies(w, x_ref, land_ref):
    x, y, c = _my_place()
    me = _linear((x, y, c))
    devs = [(x, y, 1 - c)] + [(x ^ (k >> 1), y ^ (k & 1), c) for k in range(1, 4)]
    return [(x_ref, land_ref.at[me], d) for d in devs]


def _gather_forward(lands, name):
    W = len(lands)

    def body(*refs):
        land_refs, out_refs, (send_sems, recv_sems) = refs[:W], refs[W:2 * W], refs[2 * W:]
        x, y, c = _my_place()
        sibling = (x, y, 1 - c)
        sends, arrivals = [], []
        for w in range(W):
            for k in range(1, 4):
                px, py = x ^ (k >> 1), y ^ (k & 1)
                landed, theirs = _linear((px, py, c)), out_refs[w].at[_linear((px, py, 1 - c))]
                sem = 3 * w + k - 1
                sends.append(pltpu.make_async_remote_copy(
                    src_ref=land_refs[w].at[landed], dst_ref=out_refs[w].at[landed],
                    send_sem=send_sems.at[sem], recv_sem=recv_sems.at[sem], device_id=sibling, device_id_type=MESH_ID))
                arrivals.append(pltpu.make_async_remote_copy(
                    src_ref=theirs, dst_ref=theirs, send_sem=send_sems.at[sem], recv_sem=recv_sems.at[sem],
                    device_id=sibling, device_id_type=MESH_ID))
        for cp in sends:
            cp.start()
        for cp in arrivals:
            cp.wait_recv()
        for cp in sends:
            cp.wait_send()

    return pl.pallas_call(
        body, name=name,
        in_specs=[HBM_SPEC] * W, out_specs=[HBM_SPEC] * W,
        out_shape=[jax.ShapeDtypeStruct(l.shape, l.dtype) for l in lands],
        input_output_aliases={i: i for i in range(W)},
        scratch_shapes=[pltpu.SemaphoreType.DMA((3 * W,)), pltpu.SemaphoreType.DMA((3 * W,))],
    )(*lands)


def _with_own_slot(gathered, shard):
    return lax.dynamic_update_index_in_dim(gathered, shard[None], _linear(_my_place()), axis=0)


def _in_chip_copies(w, p_ref, land_ref):
    x, y, c = _my_place()
    return [(p_ref.at[2 * q + (1 - c)], land_ref.at[q], (x, y, 1 - c)) for q in range(4)]


def _in_chip_start(parts, tag):
    lands = [_landing_zone((4,) + p.shape[1:], p.dtype) for p in parts]
    return _split_start("grad_in_chip_start_" + tag, parts, lands, (), _in_chip_copies, 4)


def _reduce_scatter_begin(parts, tag, in_chip_state=None, after=()):
    parts, early, got = list(parts), [], []
    if in_chip_state is not None:
        early, got = _split_wait("grad_in_chip_wait_" + tag, in_chip_state, after, _in_chip_copies)
    if parts:
        got = got + list(_exchange_in_chip(parts))
    parts = early + parts
    core = lax.axis_index("c").astype(jnp.int32).reshape(1)
    chip_parts = [_pair_sum(p, g, core) for p, g in zip(parts, got)]
    lands = [_landing_zone(p.shape, p.dtype) for p in chip_parts]
    return _split_start("grad_scatter_start_" + tag, chip_parts, lands, got[0], _scatter_copies, 3)


def _reduce_scatter_end(state, after, tag):
    return _split_wait("grad_scatter_wait_" + tag, state, after, _scatter_copies)


def kernel(x, c, positions, w_ada, b_ada, w_in, g_q_a, w_q_b, g_kv_a, w_kv_b, w_o_a, w_conv, w_o_b, w_o, ln1_g, ln1_b, w_ffn_in, w_ffn_out, ln2_g, ln2_b, loss_target, m_w_ada, m_b_ada, m_w_in, m_g_q_a, m_w_q_b, m_g_kv_a, m_w_kv_b, m_w_o_a, m_w_conv, m_w_o_b, m_w_o, m_ln1_g, m_ln1_b, m_w_ffn_in, m_w_ffn_out, m_ln2_g, m_ln2_b, v_w_ada, v_b_ada, v_w_in, v_g_q_a, v_w_q_b, v_g_kv_a, v_w_kv_b, v_w_o_a, v_w_conv, v_w_o_b, v_w_o, v_ln1_g, v_ln1_b, v_w_ffn_in, v_w_ffn_out, v_ln2_g, v_ln2_b):
    x2, tgt = x[0], loss_target[0]
    S, D = x2.shape
    Lq, Lkv = g_q_a.shape[1], g_kv_a.shape[1]
    H = w_q_b.shape[2] * N_DEV // QK_CAT
    F = w_ffn_out.shape[1] * N_DEV
    assert Lq == Lkv and (Lq + Lkv) % COL_BLOCK == 0 and D % COL_BLOCK == 0
    front = Lq + Lkv + QK_ROPE
    front_pad = _round_up(front, COL_BLOCK)
    kr_blk = (Lq + Lkv) // COL_BLOCK
    blk_b = front_pad // COL_BLOCK
    nblk = D // COL_BLOCK
    blk_c, blk_x, blk_ga, blk_gb = blk_b + nblk, blk_b + 2 * nblk, blk_b + 3 * nblk, blk_b + 4 * nblk
    ts = _tile(S, 256, 8)
    T = _tile(S, min(512, S // 2), CHUNK)
    tb = _tile(F, 2816)
    me = _linear(_my_place())

    cw = w_ada.shape[2]
    b_mine = lax.dynamic_slice(b_ada, (0, me * cw), (1, cw)).reshape(1, 1, cw)
    mod_blocks, cact_all, wconv_all = _ada_fwd(c.reshape(1, 1, D), w_conv[0].reshape(1, 1, -1), w_ada[0], b_mine)
    mod = mod_blocks.reshape(6, D)
    cact_all = cact_all.reshape(N_DEV, D)
    w_conv_full = wconv_all.reshape(N_DEV, CONV_K, -1).transpose(1, 0, 2).reshape(CONV_K, D)

    landing = lambda shards: [_landing_zone((N_DEV,) + s.shape, BF16) for s in shards]
    gathered = lambda lands, shards, tag: [_with_own_slot(g, s) for g, s in
                                           zip(_gather_forward(lands, tag + "_gather_forward"), shards)]
    half = D // 2
    w_in_b = w_in[0].astype(BF16)
    first, second = [w_in_b[:half]], [w_in_b[half:], w_q_b[0].astype(BF16), w_kv_b[0].astype(BF16)]
    mid = [w[0].astype(BF16) for w in (w_o_a, w_o_b, w_o)]
    last = [w[0].astype(BF16) for w in (w_ffn_in, w_ffn_out)]
    first_state = _split_start("first_gather_start", first, landing(first), mod_blocks, _gather_copies, 4)
    second_state = _split_start("second_gather_start", second, landing(second), first_state[4], _gather_copies, 4)
    u = _modulate_in(x2, mod, ts)

    first_shards, first_lands = _split_wait("first_gather_wait", first_state, (u, second_state[4]), _gather_copies)
    (g_in_top,) = gathered(first_lands, first_shards, "first")
    w_in_top = _assemble_w_in(g_in_top, front, front_pad, D, 0)
    proj_top = _matmul(u, w_in_top, "nn", BF16, "proj_top", k_rows=(0, half), tn=WIDE_TN)
    second_shards, second_lands = _split_wait("second_gather_wait", second_state, (proj_top,), _gather_copies)
    g_in_bottom, wq_s, wkv_s = gathered(second_lands, second_shards, "second")
    mid_state = _split_start("mid_gather_start", mid, landing(mid), g_in_bottom, _gather_copies, 4)
    last_state = _split_start("last_gather_start", last, landing(last), mid_state[4], _gather_copies, 4)
    w_in_p = _assemble_w_in(g_in_bottom, front, front_pad, D, half, into=w_in_top)

    inv_freq = 1.0 / (ROPE_THETA ** (jnp.arange(0, QK_ROPE, 2, dtype=F32) / QK_ROPE))
    ang = positions[0].astype(F32)[:, None] * inv_freq
    cos2 = jnp.concatenate([jnp.cos(ang), jnp.cos(ang)], axis=-1)
    sin2 = jnp.concatenate([jnp.sin(ang), jnp.sin(ang)], axis=-1)
    one, zero = jnp.ones((S, QK_NOPE), F32), jnp.zeros((S, QK_NOPE), F32)
    cos_q, sin_q = jnp.concatenate([one, cos2, one, cos2], axis=-1), jnp.concatenate([zero, sin2, zero, sin2], axis=-1)
    cos_k, sin_k = jnp.tile(cos2, (1, COL_BLOCK // QK_ROPE)), jnp.tile(sin2, (1, COL_BLOCK // QK_ROPE))

    proj = _matmul(u, w_in_p, "nn", BF16, "proj", k_rows=(half, half), init=proj_top, deps=(last_state[4],),
                   tn=WIDE_TN)
    qn = _rms_fwd(proj, g_q_a, 0, Lq, ts, "rms_q")
    kvn = _rms_fwd(proj, g_kv_a, 1, Lkv, ts, "rms_kv")
    q = _matmul(qn, wq_s, "nn", BF16, "q_up")
    kv = _matmul(kvn, wkv_s, "nn", BF16, "kv_up")
    qc, kc, vh = _qk_prep(q, kv, proj, kr_blk, cos_q, sin_q, cos_k, sin_k, H, ts)
    attn, lse = _attn_fwd(qc, kc, vh, T)
    mid_shards, mid_lands = _split_wait("mid_gather_wait", mid_state, lse, _gather_copies)
    w_oa_f, w_ob_f, w_o_f = [g.reshape(-1, D) for g in gathered(mid_lands, mid_shards, "mid")]
    ya = _matmul(attn, w_oa_f, "nn", BF16, "attn_out")
    cbc = _conv_fwd(proj, w_conv_full, blk_b, blk_c, blk_x)
    yb = _matmul(cbc, w_ob_f, "nn", BF16, "conv_out")
    merged = _merge_fwd(proj, ya, yb, blk_ga, blk_gb, ts)
    mix = _matmul(merged, w_o_f, "nn", F32, "mix_out")
    xhat1, rstd1, u2 = _ln1_fwd(x2, mix, mod, ln1_g, ln1_b, ts)
    last_shards, last_lands = _split_wait("last_gather_wait", last_state, u2, _gather_copies)
    w_fi_s, g_fo = gathered(last_lands, last_shards, "last")
    w_fo_f = g_fo.reshape(F, D)
    hh = _matmul(u2, w_fi_s, "nn", BF16, "ffn_in")
    act = _swiglu_fwd(hh, ts, tb)
    ffn = _matmul(act, w_fo_f, "nn", F32, "ffn_out")
    loss_part, dffn, dx1a, vec2 = _ln2_loss(xhat1, ffn, tgt, mod, ln1_g, ln1_b, ln2_g, ln2_b, ts)
    loss = lax.psum(loss_part[0, 0], AXES)

    gw_fo = _matmul(act, dffn, "tn", BF16, "grad_w_ffn_out", tm=FFN_TILE)
    da = _matmul(dffn, w_fo_f, "nt", BF16, "d_act", tn=FFN_TILE)
    dh = _swiglu_bwd(da, hh, ts, tb)
    gw_fi = _matmul(u2, dh, "tn", BF16, "grad_w_ffn_in", out_shards=True)
    ffn_in_chip = _in_chip_start([gw_fi, gw_fo.reshape(N_DEV, -1, D)], "ffn")
    du2 = _matmul(dh, w_fi_s, "nt", F32, "d_u2", deps=(ffn_in_chip[4],))
    ffn_state = _reduce_scatter_begin([], "ffn", ffn_in_chip, after=(du2,))
    dxa, dmix, vec1 = _ln1_bwd(du2, dx1a, xhat1, rstd1, mix, mod, ln1_g, ln1_b, ts)
    gw_o = _matmul(merged, dmix, "tn", BF16, "grad_w_o", deps=(ffn_state[4],))
    dmerged = _matmul(dmix, w_o_f, "nt", BF16, "d_merged")
    dya, dyb, dga, dgb = _merge_bwd(dmerged, proj, ya, yb, blk_ga, blk_gb, ts)
    gw_ob = _matmul(cbc, dyb, "tn", BF16, "grad_w_o_b")
    dcbc = _matmul(dyb, w_ob_f, "nt", BF16, "d_conv")
    dcb, dcc, dcx, dwconv = _conv_bwd(dcbc, proj, w_conv_full, blk_b, blk_c, blk_x)
    gw_oa = _matmul(attn, dya, "tn", BF16, "grad_w_o_a")
    mix_in_chip = _in_chip_start([g.reshape(N_DEV, -1, D) for g in (gw_oa, gw_ob, gw_o)], "mix")
    dattn = _matmul(dya, w_oa_f, "nt", BF16, "d_attn", deps=(mix_in_chip[4],))
    dqc, dkc, dvh = _attn_bwd(qc, kc, vh, dattn, attn, lse, T)
    ffn_own, ffn_got = _reduce_scatter_end(ffn_state, dqc, "ffn")
    dq, dkv, dkr = _qk_bwd(dqc, dkc, dvh, cos_q, sin_q, cos_k, sin_k, ts)
    gw_qb = _matmul(qn, dq, "tn", BF16, "grad_w_q_b", out_shards=True)
    gw_kvb = _matmul(kvn, dkv, "tn", BF16, "grad_w_kv_b", out_shards=True)
    mix_state = _reduce_scatter_begin([gw_qb, gw_kvb], "mix", mix_in_chip, after=(dqc,))
    dqn = _matmul(dq, wq_s, "nt", F32, "d_qn", deps=(mix_state[4],))
    dkvn = _matmul(dkv, wkv_s, "nt", F32, "d_kvn")
    dqa, dgq = _rms_bwd(dqn, proj, g_q_a, 0, Lq, ts, "rms_q_bwd")
    dkva, dgkv = _rms_bwd(dkvn, proj, g_kv_a, 1, Lkv, ts, "rms_kv_bwd")
    dproj = jnp.concatenate([dqa, dkva, dkr, dcb, dcc, dcx, dga, dgb], axis=1)
    gw_in_p = _matmul(u, dproj, "tn", BF16, "grad_w_in", tn=WIDE_TN)
    mix_own, mix_got = _reduce_scatter_end(mix_state, gw_in_p, "mix")
    in_state = _reduce_scatter_begin([_split_w_in(gw_in_p, front, front_pad)], "in")
    du = _matmul(dproj, w_in_p, "nt", F32, "d_u", deps=(in_state[4],), tk=WIDE_TK)
    grad_x, vec0 = _grad_x(du, dxa, x2, mod, ts)

    my_chip = (2 * lax.axis_index("x") + lax.axis_index("y")).astype(jnp.int32).reshape(1)
    arrived = {}
    for nm, w, m, v, own, got in (
            ("w_ffn_in", w_ffn_in, m_w_ffn_in, v_w_ffn_in, ffn_own[0], ffn_got[0]),
            ("w_ffn_out", w_ffn_out, m_w_ffn_out, v_w_ffn_out, ffn_own[1], ffn_got[1]),
            ("w_o_a", w_o_a, m_w_o_a, v_w_o_a, mix_own[0], mix_got[0]),
            ("w_o_b", w_o_b, m_w_o_b, v_w_o_b, mix_own[1], mix_got[1]),
            ("w_o", w_o, m_w_o, v_w_o, mix_own[2], mix_got[2]),
            ("w_q_b", w_q_b, m_w_q_b, v_w_q_b, mix_own[3], mix_got[3]),
            ("w_kv_b", w_kv_b, m_w_kv_b, v_w_kv_b, mix_own[4], mix_got[4])):
        arrived[nm] = [a[None] for a in _adamw_reduced(w[0], own, got, m[0], v[0], my_chip, "adamw_" + nm)]

    dmod = jnp.concatenate([vec0[0], vec0[1], vec1[4], vec1[0], vec1[1], vec2[2]])
    small = jnp.concatenate([dmod, dgq[0], dgkv[0], vec1[2], vec1[3], vec2[0], vec2[1], dwconv[:CONV_K].reshape(-1)])
    n_small = small.shape[0]
    nch = _round_up(n_small, cw) // cw
    payload = jnp.pad(small, (0, nch * cw - n_small)).reshape(nch, 1, cw)
    in_own, in_got = _reduce_scatter_end(in_state, [res[1] for res in arrived.values()] + [grad_x, payload], "in")
    arrived["w_in"] = [a[None] for a in _adamw_reduced(w_in[0], in_own[0], in_got[0], m_w_in[0], v_w_in[0], my_chip,
                                                       "adamw_w_in")]
    summed, dmod_mine = _ada_bwd(payload, deps=[arrived["w_in"][1]])
    arrived["w_ada"] = [a[None] for a in _adamw_ada(w_ada[0], cact_all.T, dmod_mine.reshape(N_DEV, cw),
                                                    m_w_ada[0], v_w_ada[0])]
    summed = summed.reshape(-1)
    offs = [0, 6 * D, 6 * D + Lq, 6 * D + Lq + Lkv]
    offs += [offs[-1] + D * k for k in range(1, 5)]
    g_b_ada = summed[offs[0]:offs[1]].reshape(1, -1)
    g_gq = summed[offs[1]:offs[2]].reshape(1, -1)
    g_gkv = summed[offs[2]:offs[3]].reshape(1, -1)
    g_ln1g, g_ln1b, g_ln2g, g_ln2b = [summed[offs[3 + k]:offs[4 + k]].reshape(1, -1) for k in range(4)]
    wc = w_conv.shape[2]
    g_wconv = lax.dynamic_slice(summed[offs[7]:offs[7] + CONV_K * D].reshape(CONV_K, D), (0, me * wc), (CONV_K, wc))

    names = ["w_ada", "b_ada", "w_in", "g_q_a", "w_q_b", "g_kv_a", "w_kv_b", "w_o_a", "w_conv", "w_o_b", "w_o",
             "ln1_g", "ln1_b", "w_ffn_in", "w_ffn_out", "ln2_g", "ln2_b"]
    weights = [w_ada, b_ada, w_in, g_q_a, w_q_b, g_kv_a, w_kv_b, w_o_a, w_conv, w_o_b, w_o, ln1_g, ln1_b,
               w_ffn_in, w_ffn_out, ln2_g, ln2_b]
    moms = [m_w_ada, m_b_ada, m_w_in, m_g_q_a, m_w_q_b, m_g_kv_a, m_w_kv_b, m_w_o_a, m_w_conv, m_w_o_b, m_w_o,
            m_ln1_g, m_ln1_b, m_w_ffn_in, m_w_ffn_out, m_ln2_g, m_ln2_b]
    vels = [v_w_ada, v_b_ada, v_w_in, v_g_q_a, v_w_q_b, v_g_kv_a, v_w_kv_b, v_w_o_a, v_w_conv, v_w_o_b, v_w_o,
            v_ln1_g, v_ln1_b, v_w_ffn_in, v_w_ffn_out, v_ln2_g, v_ln2_b]
    grad_of = {"b_ada": g_b_ada, "g_q_a": g_gq, "g_kv_a": g_gkv, "w_conv": g_wconv,
               "ln1_g": g_ln1g, "ln1_b": g_ln1b, "ln2_g": g_ln2g, "ln2_b": g_ln2b}
    state_of = dict(zip(names, zip(weights, moms, vels)))
    results = dict(arrived)

    def update(nm, reduced=None):
        w, m, v = state_of[nm]
        shp = w.shape
        w2 = w.reshape(shp[-2], shp[-1]) if w.ndim == 3 else w
        m2, v2 = m.reshape(w2.shape), v.reshape(w2.shape)
        if reduced is None:
            g2 = grad_of[nm].reshape(w2.shape)
            res = (g2,) + tuple(_adamw(w2, g2, m2, v2, "adamw_" + nm))
        else:
            res = _adamw_reduced(w2, reduced[0], reduced[1], m2, v2, my_chip, "adamw_" + nm)
        results[nm] = [a.reshape(shp) for a in res]

    for nm in grad_of:
        update(nm)
    outs = [[results[nm][k] for nm in names] for k in range(4)]
    return (loss, grad_x.reshape(x.shape), *outs[0], *outs[1], *outs[2], *outs[3])
```

```python
import functools

import jax
import jax.numpy as jnp
from jax import lax
from jax.experimental import pallas as pl
from jax.experimental.pallas import tpu as pltpu

F32 = jnp.float32
BF16 = jnp.bfloat16
MESH_ID = pl.DeviceIdType.MESH
AXES = ("x", "y", "c")
N_DEV = 8

CHUNK = 64
QK_NOPE = 128
QK_ROPE = 64
V_HEAD = 128
QK_CAT = QK_NOPE + QK_ROPE
ROPE_THETA = 10000.0
ATTN_SCALE = (QK_NOPE + QK_ROPE) ** -0.5
CONV_K = 3
DEEPNORM_ALPHA = 2.0 ** 0.25
LN_EPS = 1e-5
RMS_EPS = 1e-6
NEG_INF = -1e30

ADAM_LR = 0.001
ADAM_B1 = 0.9
ADAM_B2 = 0.999
ADAM_EPS = 1e-08
ADAM_WD = 0.01
ADAM_STEP = 10

LANE = 128
COL_BLOCK = 256
PACK_ROW_ALIGN = 16
PAIR_SUM_BLOCK = 1 << 20
WIDE_TN = 1280
WIDE_TK = 3840
FFN_TILE = 1408
VMEM_LIMIT = 48 * 1024 * 1024


def _round_up(n, m):
    return (n + m - 1) // m * m


def _tile(n, pref, align=LANE):
    best = None
    t = align
    while t <= min(n, pref):
        if n % t == 0:
            best = t
        t += align
    return best if best is not None else n


def _cparams(sem=None):
    return pltpu.CompilerParams(dimension_semantics=sem, vmem_limit_bytes=VMEM_LIMIT)


def _sigmoid(x):
    return 0.5 * jnp.tanh(0.5 * x) + 0.5


def _matmul(a, b, mode, out_dtype, name, tm=1024, tn=1024, tk=2048, deps=(), out_shards=False, k_rows=None,
            init=None):
    b_shards = b.ndim == 3
    n = b.shape[2] if b_shards else (b.shape[1] // N_DEV if out_shards else None)
    if mode == "nn":
        (M, K), (K2, N) = a.shape, (b.shape[1], N_DEV * n) if b_shards else b.shape
    elif mode == "nt":
        (M, K), (N, K2) = a.shape, (b.shape[1], N_DEV * n) if b_shards else b.shape
    else:
        (K, M), (K2, N) = a.shape, b.shape
    assert K == K2, (a.shape, b.shape, mode)
    tm = _tile(M, tm)
    tn = n if (mode != "nt" and n is not None) else _tile(N, tn)
    k_row0, k_len = k_rows if k_rows is not None else (0, K)
    tk = n if (mode == "nt" and b_shards) else _tile(k_len, tk)
    nk, k0 = k_len // tk, k_row0 // tk
    if mode == "nn":
        a_spec = pl.BlockSpec((tm, tk), lambda i, j, k: (i, k0 + k))
        b_spec = (pl.BlockSpec((1, tk, n), lambda i, j, k: (j, k, 0)) if b_shards
                  else pl.BlockSpec((tk, tn), lambda i, j, k: (k0 + k, j)))
        dims = (((1,), (0,)), ((), ()))
    elif mode == "nt":
        a_spec = pl.BlockSpec((tm, tk), lambda i, j, k: (i, k))
        b_spec = (pl.BlockSpec((1, tn, n), lambda i, j, k: (k, j, 0)) if b_shards
                  else pl.BlockSpec((tn, tk), lambda i, j, k: (j, k)))
        dims = (((1,), (1,)), ((), ()))
    else:
        a_spec = pl.BlockSpec((tk, tm), lambda i, j, k: (k, i))
        b_spec = pl.BlockSpec((tk, tn), lambda i, j, k: (k, j))
        dims = (((0,), (0,)), ((), ()))
    if out_shards:
        out_spec = pl.BlockSpec((1, tm, n), lambda i, j, k: (j, i, 0))
        out_shape = jax.ShapeDtypeStruct((N_DEV, M, n), out_dtype)
    else:
        out_spec = pl.BlockSpec((tm, tn), lambda i, j, k: (i, j))
        out_shape = jax.ShapeDtypeStruct((M, N), out_dtype)

    def product(a_ref, b_ref):
        b_blk = b_ref[0] if b_shards else b_ref[...]
        return lax.dot_general(a_ref[...].astype(BF16), b_blk.astype(BF16), dims, preferred_element_type=F32)

    def write(o_ref, value):
        if out_shards:
            o_ref[0] = value.astype(o_ref.dtype)
        else:
            o_ref[...] = value.astype(o_ref.dtype)

    def body_whole_k(a_ref, b_ref, *rest):
        value = product(a_ref, b_ref)
        write(rest[-1], value if init is None else value + rest[0][...])

    def body_split_k(a_ref, b_ref, *rest):
        o_ref, acc_ref = rest[-2:]
        k = pl.program_id(2)

        @pl.when(k == 0)
        def _():
            acc_ref[...] = jnp.zeros_like(acc_ref) if init is None else rest[0][...].astype(F32)

        acc_ref[...] += product(a_ref, b_ref)

        @pl.when(k == nk - 1)
        def _():
            write(o_ref, acc_ref[...])

    return pl.pallas_call(
        body_whole_k if nk == 1 else body_split_k, name=name, grid=(M // tm, N // tn, nk),
        in_specs=[a_spec, b_spec] + ([] if init is None else [out_spec]) + [ANY_SPEC] * len(deps),
        out_specs=out_spec, out_shape=out_shape,
        scratch_shapes=[] if nk == 1 else [pltpu.VMEM((tm, tn), F32)],
        compiler_params=_cparams(("parallel", "parallel", "arbitrary")),
    )(a, b, *(() if init is None else (init,)), *deps)


def _assemble_w_in(shards, front, front_pad, rows, row0, into=None):
    _, K, n = shards.shape
    gap = front_pad - front
    tk = _tile(K, 256, PACK_ROW_ALIGN)
    blk0 = row0 // tk

    def body(g_ref, *rest):
        o_ref = rest[-1]
        if gap:
            o_ref[:, front:front_pad] = jnp.zeros((tk, gap), o_ref.dtype)
        for j in range(N_DEV):
            lo, hi = j * n, (j + 1) * n
            if lo < front < hi:
                o_ref[:, lo:front] = g_ref[j, :, 0:front - lo]
                o_ref[:, front_pad:hi + gap] = g_ref[j, :, front - lo:n]
            else:
                off = 0 if hi <= front else gap
                o_ref[:, lo + off:hi + off] = g_ref[j]

    return pl.pallas_call(
        body, name="assemble_w_in", grid=(K // tk,),
        in_specs=[pl.BlockSpec((N_DEV, tk, n), lambda i: (0, i, 0))] + ([] if into is None else [ANY_SPEC]),
        out_specs=pl.BlockSpec((tk, N_DEV * n + gap), lambda i: (blk0 + i, 0)),
        out_shape=jax.ShapeDtypeStruct((rows, N_DEV * n + gap), shards.dtype),
        input_output_aliases={} if into is None else {1: 0},
        compiler_params=_cparams(("parallel",)),
    )(*([shards] if into is None else [shards, into]))


def _split_w_in(w, front, front_pad):
    K, NP = w.shape
    gap = front_pad - front
    n = (NP - gap) // N_DEV
    tk = _tile(K, 256, PACK_ROW_ALIGN)

    def body(w_ref, o_ref):
        for j in range(N_DEV):
            lo, hi = j * n, (j + 1) * n
            if lo < front < hi:
                o_ref[j, :, 0:front - lo] = w_ref[:, lo:front]
                o_ref[j, :, front - lo:n] = w_ref[:, front_pad:hi + gap]
            else:
                off = 0 if hi <= front else gap
                o_ref[j] = w_ref[:, lo + off:hi + off]

    return pl.pallas_call(
        body, name="split_grad_w_in", grid=(K // tk,),
        in_specs=[pl.BlockSpec((tk, NP), lambda i: (i, 0))],
        out_specs=pl.BlockSpec((N_DEV, tk, n), lambda i: (0, i, 0)),
        out_shape=jax.ShapeDtypeStruct((N_DEV, K, n), w.dtype),
        compiler_params=_cparams(("parallel",)),
    )(w)


def _modulate_in(x, mod, ts):
    S, D = x.shape

    def body(x_ref, mod_ref, u_ref):
        u_ref[...] = (x_ref[...] * (1.0 + mod_ref[1:2, :]) + mod_ref[0:1, :]).astype(BF16)

    return pl.pallas_call(
        body, name="modulate_in", grid=(S // ts,),
        in_specs=[pl.BlockSpec((ts, D), lambda i: (i, 0)), pl.BlockSpec((6, D), lambda i: (0, 0))],
        out_specs=pl.BlockSpec((ts, D), lambda i: (i, 0)),
        out_shape=jax.ShapeDtypeStruct((S, D), BF16),
        compiler_params=_cparams(("parallel",)),
    )(x, mod)


def _rms_fwd(proj, g, blk, L, ts, name):
    S = proj.shape[0]

    def body(a_ref, g_ref, y_ref):
        a = a_ref[...].astype(F32)
        r = lax.rsqrt(jnp.mean(a * a, axis=-1, keepdims=True) + RMS_EPS)
        y_ref[...] = (a * r * g_ref[...]).astype(BF16)

    return pl.pallas_call(
        body, name=name, grid=(S // ts,),
        in_specs=[pl.BlockSpec((ts, L), lambda i: (i, blk)), pl.BlockSpec((1, L), lambda i: (0, 0))],
        out_specs=pl.BlockSpec((ts, L), lambda i: (i, 0)),
        out_shape=jax.ShapeDtypeStruct((S, L), BF16),
        compiler_params=_cparams(("parallel",)),
    )(proj, g)


def _rope_partner(x, period, start):
    w = x.shape[-1]
    lane = lax.broadcasted_iota(jnp.int32, x.shape, x.ndim - 1) % period
    first = (lane >= start) & (lane < start + QK_ROPE // 2)
    from_right = pltpu.roll(x, w - QK_ROPE // 2, axis=x.ndim - 1)
    from_left = pltpu.roll(x, QK_ROPE // 2, axis=x.ndim - 1)
    return jnp.where(first, -from_right, from_left)


def _qk_prep(q, kv, proj, kr_blk, cos_q, sin_q, cos_k, sin_k, H, ts):
    S = q.shape[0]
    pair = 2 * QK_CAT
    kv_w = QK_NOPE + V_HEAD

    def body(q_ref, kv_ref, kr_ref, cq_ref, sq_ref, ck_ref, sk_ref, qc_ref, kc_ref, vh_ref):
        kr = kr_ref[...].astype(F32)
        kr = kr * ck_ref[...] + _rope_partner(kr, QK_ROPE, 0) * sk_ref[...]
        kr = kr[:, :QK_ROPE].astype(BF16)
        for p in range(H // 2):
            x = q_ref[:, p * pair:(p + 1) * pair].astype(F32)
            x = x * cq_ref[...] + _rope_partner(x, QK_CAT, QK_NOPE) * sq_ref[...]
            qc_ref[2 * p] = x[:, :QK_CAT].astype(BF16)
            qc_ref[2 * p + 1] = x[:, QK_CAT:].astype(BF16)
        for h in range(H):
            kc_ref[h, :, 0:QK_NOPE] = kv_ref[:, h * kv_w:h * kv_w + QK_NOPE].astype(BF16)
            kc_ref[h, :, QK_NOPE:QK_CAT] = kr
            vh_ref[h, :, :] = kv_ref[:, h * kv_w + QK_NOPE:(h + 1) * kv_w].astype(BF16)

    row = lambda w: pl.BlockSpec((ts, w), lambda i: (i, 0))
    return pl.pallas_call(
        body, name="qk_prep", grid=(S // ts,),
        in_specs=[row(H * QK_CAT), row(H * kv_w),
                  pl.BlockSpec((ts, COL_BLOCK), lambda i: (i, kr_blk)),
                  row(pair), row(pair), row(COL_BLOCK), row(COL_BLOCK)],
        out_specs=[pl.BlockSpec((H, ts, QK_CAT), lambda i: (0, i, 0)),
                   pl.BlockSpec((H, ts, QK_CAT), lambda i: (0, i, 0)),
                   pl.BlockSpec((H, ts, V_HEAD), lambda i: (0, i, 0))],
        out_shape=[jax.ShapeDtypeStruct((H, S, QK_CAT), BF16), jax.ShapeDtypeStruct((H, S, QK_CAT), BF16),
                   jax.ShapeDtypeStruct((H, S, V_HEAD), BF16)],
        compiler_params=_cparams(("parallel",)),
    )(q, kv, proj, cos_q, sin_q, cos_k, sin_k)


NT_DIMS = (((1,), (1,)), ((), ()))
TN_DIMS = (((0,), (0,)), ((), ()))


def _diag_mask(T):
    rows = lax.broadcasted_iota(jnp.int32, (T, T), 0) // CHUNK
    cols = lax.broadcasted_iota(jnp.int32, (T, T), 1) // CHUNK
    return cols <= rows


def _attn_fwd(qc, kc, vh, T):
    H, S, _ = qc.shape
    n = S // T

    def body(q_ref, k_ref, v_ref, o_ref, lse_ref):
        q = q_ref[0]

        def block(i):
            L = (i + 1) * T
            s_old = lax.dot_general(q, k_ref[0, 0:i * T, :], NT_DIMS, preferred_element_type=F32) if i else None
            s_diag = lax.dot_general(q, k_ref[0, i * T:L, :], NT_DIMS, preferred_element_type=F32)
            s_diag = jnp.where(_diag_mask(T), s_diag, NEG_INF)
            m = jnp.max(s_diag, axis=-1, keepdims=True)
            if i:
                m = jnp.maximum(m, jnp.max(s_old, axis=-1, keepdims=True))
            p_diag = jnp.exp((s_diag - m) * ATTN_SCALE)
            l = jnp.sum(p_diag, axis=-1, keepdims=True)
            acc = jnp.dot(p_diag.astype(BF16), v_ref[0, i * T:L, :], preferred_element_type=F32)
            if i:
                p_old = jnp.exp((s_old - m) * ATTN_SCALE)
                l = l + jnp.sum(p_old, axis=-1, keepdims=True)
                acc = acc + jnp.dot(p_old.astype(BF16), v_ref[0, 0:i * T, :], preferred_element_type=F32)
            o_ref[...] = (acc / l).astype(o_ref.dtype)
            lse_ref[0] = m * ATTN_SCALE + jnp.log(l)

        for i in range(n):
            pl.when(pl.program_id(1) == i)(functools.partial(block, i))

    return pl.pallas_call(
        body, name="attn_fwd", grid=(H, n),
        in_specs=[pl.BlockSpec((1, T, QK_CAT), lambda h, i: (h, i, 0)),
                  pl.BlockSpec((1, S, QK_CAT), lambda h, i: (h, 0, 0)),
                  pl.BlockSpec((1, S, V_HEAD), lambda h, i: (h, 0, 0))],
        out_specs=[pl.BlockSpec((T, V_HEAD), lambda h, i: (i, h)),
                   pl.BlockSpec((1, T, 1), lambda h, i: (h, i, 0))],
        out_shape=[jax.ShapeDtypeStruct((S, H * V_HEAD), BF16), jax.ShapeDtypeStruct((H, S, 1), F32)],
        compiler_params=_cparams(("parallel", "arbitrary")),
    )(qc, kc, vh)


def _shift_rows(z, k):
    if k == 0:
        return z
    n = z.shape[0]
    row = lax.broadcasted_iota(jnp.int32, z.shape, 0)
    if k > 0:
        return jnp.where(row >= k, pltpu.roll(z, k, axis=0), 0.0)
    return jnp.where(row < n + k, pltpu.roll(z, n + k, axis=0), 0.0)


def _conv_fwd(proj, w_conv, blk_b, blk_c, blk_x):
    S = proj.shape[0]
    D = w_conv.shape[1]
    nb = D // COL_BLOCK

    def body(cb_ref, cc_ref, cx_ref, w_ref, o_ref):
        z = cc_ref[...].astype(F32) * cx_ref[...].astype(F32)
        conv = w_ref[2:3, :] * z + w_ref[1:2, :] * _shift_rows(z, 1) + w_ref[0:1, :] * _shift_rows(z, 2)
        o_ref[...] = (cb_ref[...].astype(F32) * conv).astype(BF16)

    col = lambda off: pl.BlockSpec((S, COL_BLOCK), lambda j: (0, off + j))
    return pl.pallas_call(
        body, name="conv_fwd", grid=(nb,),
        in_specs=[col(blk_b), col(blk_c), col(blk_x), pl.BlockSpec((CONV_K, COL_BLOCK), lambda j: (0, j))],
        out_specs=pl.BlockSpec((S, COL_BLOCK), lambda j: (0, j)),
        out_shape=jax.ShapeDtypeStruct((S, D), BF16),
        compiler_params=_cparams(("parallel",)),
    )(proj, proj, proj, w_conv)


def _merge_fwd(proj, ya, yb, blk_ga, blk_gb, ts):
    S, D = ya.shape
    nb = D // COL_BLOCK

    def body(ga_ref, gb_ref, ya_ref, yb_ref, o_ref):
        sa, sb = _sigmoid(ga_ref[...].astype(F32)), _sigmoid(gb_ref[...].astype(F32))
        o_ref[...] = (sa * ya_ref[...].astype(F32) + sb * yb_ref[...].astype(F32)).astype(BF16)

    row = pl.BlockSpec((ts, D), lambda i: (i, 0))
    seg = lambda blk: pl.BlockSpec((pl.Element(ts), pl.Element(D)), lambda i: (i * ts, blk * COL_BLOCK))
    return pl.pallas_call(
        body, name="merge_fwd", grid=(S // ts,),
        in_specs=[seg(blk_ga), seg(blk_gb), row, row],
        out_specs=row,
        out_shape=jax.ShapeDtypeStruct((S, D), BF16),
        compiler_params=_cparams(("parallel",)),
    )(proj, proj, ya, yb)


def _ln1_fwd(x, mix, mod, g, b, ts):
    S, D = x.shape

    def body(x_ref, mix_ref, mod_ref, g_ref, b_ref, xhat_ref, rstd_ref, u2_ref):
        r = DEEPNORM_ALPHA * x_ref[...] + mod_ref[2:3, :] * mix_ref[...]
        mu = jnp.mean(r, axis=-1, keepdims=True)
        d = r - mu
        rstd = lax.rsqrt(jnp.mean(d * d, axis=-1, keepdims=True) + LN_EPS)
        xhat = d * rstd
        xhat_ref[...] = xhat
        rstd_ref[...] = rstd
        x1 = xhat * g_ref[...] + b_ref[...]
        u2_ref[...] = (x1 * (1.0 + mod_ref[4:5, :]) + mod_ref[3:4, :]).astype(BF16)

    row = pl.BlockSpec((ts, D), lambda i: (i, 0))
    vec = lambda r: pl.BlockSpec((r, D), lambda i: (0, 0))
    return pl.pallas_call(
        body, name="ln1_fwd", grid=(S // ts,),
        in_specs=[row, row, vec(6), vec(1), vec(1)],
        out_specs=[row, pl.BlockSpec((ts, 1), lambda i: (i, 0)), row],
        out_shape=[jax.ShapeDtypeStruct((S, D), F32), jax.ShapeDtypeStruct((S, 1), F32),
                   jax.ShapeDtypeStruct((S, D), BF16)],
        compiler_params=_cparams(("parallel",)),
    )(x, mix, mod, g, b)


def _swiglu_fwd(h, ts, tb):
    S, F2 = h.shape
    F = F2 // 2
    nb = F // tb

    def body(hg_ref, hu_ref, a_ref):
        hg = hg_ref[...].astype(F32)
        a_ref[...] = (hg * _sigmoid(hg) * hu_ref[...].astype(F32)).astype(BF16)

    return pl.pallas_call(
        body, name="swiglu_fwd", grid=(S // ts, nb),
        in_specs=[pl.BlockSpec((ts, tb), lambda i, j: (i, j)), pl.BlockSpec((ts, tb), lambda i, j: (i, j + nb))],
        out_specs=pl.BlockSpec((ts, tb), lambda i, j: (i, j)),
        out_shape=jax.ShapeDtypeStruct((S, F), BF16),
        compiler_params=_cparams(("parallel", "parallel")),
    )(h, h)


def _ln2_loss(xhat1, ffn, tgt, mod, g1, b1, g2, b2, ts):
    S, D = xhat1.shape

    def body(xh_ref, ffn_ref, t_ref, mod_ref, g1_ref, b1_ref, g2_ref, b2_ref, loss_ref, dffn_ref, dx1_ref, vec_ref):
        i = pl.program_id(0)

        @pl.when(i == 0)
        def _():
            loss_ref[...] = jnp.zeros_like(loss_ref)
            vec_ref[...] = jnp.zeros_like(vec_ref)

        x1 = xh_ref[...] * g1_ref[...] + b1_ref[...]
        ffn = ffn_ref[...]
        r = DEEPNORM_ALPHA * x1 + mod_ref[5:6, :] * ffn
        mu = jnp.mean(r, axis=-1, keepdims=True)
        d = r - mu
        rstd = lax.rsqrt(jnp.mean(d * d, axis=-1, keepdims=True) + LN_EPS)
        xhat = d * rstd
        e = xhat * g2_ref[...] + b2_ref[...] - t_ref[...]
        loss_ref[...] += 0.5 * jnp.sum(jnp.mean(e * e, axis=-1, keepdims=True))
        dy = e * (1.0 / D)
        dxhat = dy * g2_ref[...]
        dr = rstd * (dxhat - jnp.mean(dxhat, axis=-1, keepdims=True)
                     - xhat * jnp.mean(dxhat * xhat, axis=-1, keepdims=True))
        dffn_ref[...] = (dr * mod_ref[5:6, :]).astype(BF16)
        dx1_ref[...] = DEEPNORM_ALPHA * dr
        vec_ref[0:1, :] += jnp.sum(dy * xhat, axis=0, keepdims=True)
        vec_ref[1:2, :] += jnp.sum(dy, axis=0, keepdims=True)
        vec_ref[2:3, :] += jnp.sum(dr * ffn, axis=0, keepdims=True)

    row = pl.BlockSpec((ts, D), lambda i: (i, 0))
    vec = lambda r: pl.BlockSpec((r, D), lambda i: (0, 0))
    return pl.pallas_call(
        body, name="ln2_loss", grid=(S // ts,),
        in_specs=[row, row, row, vec(6), vec(1), vec(1), vec(1), vec(1)],
        out_specs=[pl.BlockSpec((1, LANE), lambda i: (0, 0)), row, row, vec(8)],
        out_shape=[jax.ShapeDtypeStruct((1, LANE), F32), jax.ShapeDtypeStruct((S, D), BF16),
                   jax.ShapeDtypeStruct((S, D), F32), jax.ShapeDtypeStruct((8, D), F32)],
        compiler_params=_cparams(("arbitrary",)),
    )(xhat1, ffn, tgt, mod, g1, b1, g2, b2)


def _swiglu_bwd(da, h, ts, tb):
    S, F2 = h.shape
    nb = (F2 // 2) // tb

    def body(da_ref, hg_ref, hu_ref, dh_ref):
        hg, da = hg_ref[...].astype(F32), da_ref[...].astype(F32)
        sg = _sigmoid(hg)

        @pl.when(pl.program_id(2) == 0)
        def _():
            dh_ref[...] = (da * hu_ref[...].astype(F32) * (sg * (1.0 + hg * (1.0 - sg)))).astype(BF16)

        @pl.when(pl.program_id(2) == 1)
        def _():
            dh_ref[...] = (da * hg * sg).astype(BF16)

    lo = pl.BlockSpec((ts, tb), lambda i, j, k: (i, j))
    hi = pl.BlockSpec((ts, tb), lambda i, j, k: (i, j + nb))
    return pl.pallas_call(
        body, name="swiglu_bwd", grid=(S // ts, nb, 2),
        in_specs=[lo, lo, hi],
        out_specs=pl.BlockSpec((ts, tb), lambda i, j, k: (i, j + nb * k)),
        out_shape=jax.ShapeDtypeStruct((S, F2), BF16),
        compiler_params=_cparams(("parallel", "parallel", "arbitrary")),
    )(da, h, h)


def _ln1_bwd(du2, dx1a, xhat1, rstd1, mix, mod, g1, b1, ts):
    S, D = xhat1.shape

    def body(du2_ref, dx1a_ref, xh_ref, rstd_ref, mix_ref, mod_ref, g_ref, b_ref, dxa_ref, dmix_ref, vec_ref):
        i = pl.program_id(0)

        @pl.when(i == 0)
        def _():
            vec_ref[...] = jnp.zeros_like(vec_ref)

        xhat, du2, mix = xh_ref[...], du2_ref[...], mix_ref[...]
        x1 = xhat * g_ref[...] + b_ref[...]
        dx1 = dx1a_ref[...] + du2 * (1.0 + mod_ref[4:5, :])
        dxhat = dx1 * g_ref[...]
        dr = rstd_ref[...] * (dxhat - jnp.mean(dxhat, axis=-1, keepdims=True)
                              - xhat * jnp.mean(dxhat * xhat, axis=-1, keepdims=True))
        dxa_ref[...] = DEEPNORM_ALPHA * dr
        dmix_ref[...] = (dr * mod_ref[2:3, :]).astype(BF16)
        vec_ref[0:1, :] += jnp.sum(du2, axis=0, keepdims=True)
        vec_ref[1:2, :] += jnp.sum(du2 * x1, axis=0, keepdims=True)
        vec_ref[2:3, :] += jnp.sum(dx1 * xhat, axis=0, keepdims=True)
        vec_ref[3:4, :] += jnp.sum(dx1, axis=0, keepdims=True)
        vec_ref[4:5, :] += jnp.sum(dr * mix, axis=0, keepdims=True)

    row = pl.BlockSpec((ts, D), lambda i: (i, 0))
    vec = lambda r: pl.BlockSpec((r, D), lambda i: (0, 0))
    return pl.pallas_call(
        body, name="ln1_bwd", grid=(S // ts,),
        in_specs=[row, row, row, pl.BlockSpec((ts, 1), lambda i: (i, 0)), row, vec(6), vec(1), vec(1)],
        out_specs=[row, row, vec(8)],
        out_shape=[jax.ShapeDtypeStruct((S, D), F32), jax.ShapeDtypeStruct((S, D), BF16),
                   jax.ShapeDtypeStruct((8, D), F32)],
        compiler_params=_cparams(("arbitrary",)),
    )(du2, dx1a, xhat1, rstd1, mix, mod, g1, b1)


def _merge_bwd(dmerged, proj, ya, yb, blk_ga, blk_gb, ts):
    S, D = ya.shape
    nb = D // COL_BLOCK

    def body(dm_ref, ga_ref, gb_ref, ya_ref, yb_ref, dya_ref, dyb_ref, dga_ref, dgb_ref):
        dm = dm_ref[...].astype(F32)
        sa, sb = _sigmoid(ga_ref[...].astype(F32)), _sigmoid(gb_ref[...].astype(F32))
        dya_ref[...] = (dm * sa).astype(BF16)
        dyb_ref[...] = (dm * sb).astype(BF16)
        dga_ref[...] = (dm * ya_ref[...].astype(F32) * sa * (1.0 - sa)).astype(BF16)
        dgb_ref[...] = (dm * yb_ref[...].astype(F32) * sb * (1.0 - sb)).astype(BF16)

    row = pl.BlockSpec((ts, D), lambda i: (i, 0))
    seg = lambda blk: pl.BlockSpec((pl.Element(ts), pl.Element(D)), lambda i: (i * ts, blk * COL_BLOCK))
    out = jax.ShapeDtypeStruct((S, D), BF16)
    return pl.pallas_call(
        body, name="merge_bwd", grid=(S // ts,),
        in_specs=[row, seg(blk_ga), seg(blk_gb), row, row],
        out_specs=[row] * 4,
        out_shape=[out] * 4,
        compiler_params=_cparams(("parallel",)),
    )(dmerged, proj, proj, ya, yb)


def _conv_bwd(dcbc, proj, w_conv, blk_b, blk_c, blk_x):
    S = proj.shape[0]
    D = w_conv.shape[1]
    nb = D // COL_BLOCK

    def body(d_ref, cb_ref, cc_ref, cx_ref, w_ref, dcb_ref, dcc_ref, dcx_ref, dw_ref):
        d, cc, cx = d_ref[...].astype(F32), cc_ref[...].astype(F32), cx_ref[...].astype(F32)
        z = cc * cx
        z1, z2 = _shift_rows(z, 1), _shift_rows(z, 2)
        conv = w_ref[2:3, :] * z + w_ref[1:2, :] * z1 + w_ref[0:1, :] * z2
        dcb_ref[...] = (d * conv).astype(BF16)
        dconv = d * cb_ref[...].astype(F32)
        dz = w_ref[2:3, :] * dconv + w_ref[1:2, :] * _shift_rows(dconv, -1) + w_ref[0:1, :] * _shift_rows(dconv, -2)
        dcc_ref[...] = (dz * cx).astype(BF16)
        dcx_ref[...] = (dz * cc).astype(BF16)
        dw_ref[...] = jnp.zeros_like(dw_ref)
        dw_ref[0:1, :] = jnp.sum(dconv * z2, axis=0, keepdims=True)
        dw_ref[1:2, :] = jnp.sum(dconv * z1, axis=0, keepdims=True)
        dw_ref[2:3, :] = jnp.sum(dconv * z, axis=0, keepdims=True)

    col = lambda off: pl.BlockSpec((S, COL_BLOCK), lambda j: (0, off + j))
    out = jax.ShapeDtypeStruct((S, D), BF16)
    return pl.pallas_call(
        body, name="conv_bwd", grid=(nb,),
        in_specs=[col(0), col(blk_b), col(blk_c), col(blk_x), pl.BlockSpec((CONV_K, COL_BLOCK), lambda j: (0, j))],
        out_specs=[col(0), col(0), col(0), pl.BlockSpec((8, COL_BLOCK), lambda j: (0, j))],
        out_shape=[out, out, out, jax.ShapeDtypeStruct((8, D), F32)],
        compiler_params=_cparams(("parallel",)),
    )(dcbc, proj, proj, proj, w_conv)


def _attn_bwd(qc, kc, vh, do, o, lse, T):
    H, S, _ = qc.shape
    n = S // T

    def body(q_ref, k_ref, v_ref, do_ref, o_ref, lse_ref, dq_ref, dk_ref, dv_ref, d_ref, dq_acc, dk_acc, dv_acc):
        j = pl.program_id(1)

        @pl.when(j == 0)
        def _():
            dq_acc[...] = jnp.zeros_like(dq_acc)
            d_ref[...] = jnp.sum(do_ref[...].astype(F32) * o_ref[...].astype(F32), axis=-1, keepdims=True)

        dk_acc[...] = jnp.zeros_like(dk_acc)
        dv_acc[...] = jnp.zeros_like(dv_acc)
        k, v = k_ref[0], v_ref[0]

        def step(i, masked):
            rows = pl.ds(pl.multiple_of(i * T, T), T)
            q = q_ref[0, rows, :]
            do = do_ref[rows, :].astype(BF16)
            s = lax.dot_general(q, k, NT_DIMS, preferred_element_type=F32) * ATTN_SCALE
            if masked:
                s = jnp.where(_diag_mask(T), s, NEG_INF)
            p = jnp.exp(s - lse_ref[0, rows, :])
            dv_acc[...] += lax.dot_general(p.astype(BF16), do, TN_DIMS, preferred_element_type=F32)
            dp = lax.dot_general(do, v, NT_DIMS, preferred_element_type=F32)
            ds = (p * (dp - d_ref[rows, :]) * ATTN_SCALE).astype(BF16)
            dk_acc[...] += lax.dot_general(ds, q, TN_DIMS, preferred_element_type=F32)
            dq_acc[rows, :] += jnp.dot(ds, k, preferred_element_type=F32)

        def above(i, carry):
            step(i, False)
            return carry

        step(j, True)
        lax.fori_loop(j + 1, n, above, 0)
        dk_ref[0] = dk_acc[...].astype(BF16)
        dv_ref[0] = dv_acc[...].astype(BF16)

        @pl.when(j == n - 1)
        def _():
            dq_ref[0] = dq_acc[...].astype(BF16)

    head = lambda w: pl.BlockSpec((1, S, w), lambda h, j: (h, 0, 0))
    blk = lambda w: pl.BlockSpec((1, T, w), lambda h, j: (h, j, 0))
    ospec = pl.BlockSpec((S, V_HEAD), lambda h, j: (0, h))
    return pl.pallas_call(
        body, name="attn_bwd", grid=(H, n),
        in_specs=[head(QK_CAT), blk(QK_CAT), blk(V_HEAD), ospec, ospec, head(1)],
        out_specs=[head(QK_CAT), blk(QK_CAT), blk(V_HEAD)],
        out_shape=[jax.ShapeDtypeStruct((H, S, QK_CAT), BF16), jax.ShapeDtypeStruct((H, S, QK_CAT), BF16),
                   jax.ShapeDtypeStruct((H, S, V_HEAD), BF16)],
        scratch_shapes=[pltpu.VMEM((S, 1), F32), pltpu.VMEM((S, QK_CAT), F32), pltpu.VMEM((T, QK_CAT), F32),
                        pltpu.VMEM((T, V_HEAD), F32)],
        compiler_params=_cparams(("parallel", "arbitrary")),
    )(qc, kc, vh, do, o, lse)


def _qk_bwd(dqc, dkc, dvh, cos_q, sin_q, cos_k, sin_k, ts):
    H, S, _ = dqc.shape
    pair = 2 * QK_CAT
    kv_w = QK_NOPE + V_HEAD

    def body(dqc_ref, dkc_ref, dvh_ref, cq_ref, sq_ref, ck_ref, sk_ref, dq_ref, dkv_ref, dkr_ref, q_buf, kr_buf):
        for p in range(H // 2):
            q_buf[:, :QK_CAT] = dqc_ref[2 * p].astype(F32)
            q_buf[:, QK_CAT:] = dqc_ref[2 * p + 1].astype(F32)
            g = q_buf[...]
            dq_ref[:, p * pair:(p + 1) * pair] = (
                g * cq_ref[...] - _rope_partner(g, QK_CAT, QK_NOPE) * sq_ref[...]).astype(BF16)
        kr_sum = jnp.zeros((ts, QK_ROPE), F32)
        for h in range(H):
            dkv_ref[:, h * kv_w:h * kv_w + QK_NOPE] = dkc_ref[h, :, 0:QK_NOPE].astype(BF16)
            dkv_ref[:, h * kv_w + QK_NOPE:(h + 1) * kv_w] = dvh_ref[h].astype(BF16)
            kr_sum = kr_sum + dkc_ref[h, :, QK_NOPE:QK_CAT]
        kr_buf[...] = jnp.zeros_like(kr_buf)
        kr_buf[:, 0:QK_ROPE] = kr_sum
        kr = kr_buf[...]
        dkr_ref[...] = (kr * ck_ref[...] - _rope_partner(kr, QK_ROPE, 0) * sk_ref[...]).astype(BF16)

    row = lambda w: pl.BlockSpec((ts, w), lambda i: (i, 0))
    head = lambda w: pl.BlockSpec((H, ts, w), lambda i: (0, i, 0))
    return pl.pallas_call(
        body, name="qk_bwd", grid=(S // ts,),
        in_specs=[head(QK_CAT), head(QK_CAT), head(V_HEAD), row(pair), row(pair), row(COL_BLOCK), row(COL_BLOCK)],
        out_specs=[row(H * QK_CAT), row(H * kv_w), row(COL_BLOCK)],
        out_shape=[jax.ShapeDtypeStruct((S, H * QK_CAT), BF16), jax.ShapeDtypeStruct((S, H * kv_w), BF16),
                   jax.ShapeDtypeStruct((S, COL_BLOCK), BF16)],
        scratch_shapes=[pltpu.VMEM((ts, pair), F32), pltpu.VMEM((ts, COL_BLOCK), F32)],
        compiler_params=_cparams(("parallel",)),
    )(dqc, dkc, dvh, cos_q, sin_q, cos_k, sin_k)


def _rms_bwd(dy, proj, g, blk, L, ts, name):
    S = proj.shape[0]

    def body(dy_ref, a_ref, g_ref, da_ref, dg_ref):
        i = pl.program_id(0)

        @pl.when(i == 0)
        def _():
            dg_ref[...] = jnp.zeros_like(dg_ref)

        a, dy = a_ref[...].astype(F32), dy_ref[...]
        r = lax.rsqrt(jnp.mean(a * a, axis=-1, keepdims=True) + RMS_EPS)
        dyh = dy * g_ref[...]
        da = r * dyh - a * (r * r * r) * jnp.mean(dyh * a, axis=-1, keepdims=True)
        da_ref[...] = da.astype(BF16)
        dg_ref[0:1, :] += jnp.sum(dy * a * r, axis=0, keepdims=True)

    return pl.pallas_call(
        body, name=name, grid=(S // ts,),
        in_specs=[pl.BlockSpec((ts, L), lambda i: (i, 0)), pl.BlockSpec((ts, L), lambda i: (i, blk)),
                  pl.BlockSpec((1, L), lambda i: (0, 0))],
        out_specs=[pl.BlockSpec((ts, L), lambda i: (i, 0)), pl.BlockSpec((8, L), lambda i: (0, 0))],
        out_shape=[jax.ShapeDtypeStruct((S, L), BF16), jax.ShapeDtypeStruct((8, L), F32)],
        compiler_params=_cparams(("arbitrary",)),
    )(dy, proj, g)


def _grad_x(du, dxa, x, mod, ts):
    S, D = x.shape

    def body(du_ref, dxa_ref, x_ref, mod_ref, dx_ref, vec_ref):
        i = pl.program_id(0)

        @pl.when(i == 0)
        def _():
            vec_ref[...] = jnp.zeros_like(vec_ref)

        du = du_ref[...]
        dx_ref[...] = dxa_ref[...] + du * (1.0 + mod_ref[1:2, :])
        vec_ref[0:1, :] += jnp.sum(du, axis=0, keepdims=True)
        vec_ref[1:2, :] += jnp.sum(du * x_ref[...], axis=0, keepdims=True)

    row = pl.BlockSpec((ts, D), lambda i: (i, 0))
    vec = lambda r: pl.BlockSpec((r, D), lambda i: (0, 0))
    return pl.pallas_call(
        body, name="grad_x", grid=(S // ts,),
        in_specs=[row, row, row, vec(6)],
        out_specs=[row, vec(8)],
        out_shape=[jax.ShapeDtypeStruct((S, D), F32), jax.ShapeDtypeStruct((8, D), F32)],
        compiler_params=_cparams(("arbitrary",)),
    )(du, dxa, x, mod)


def _adamw(w, g, m, v, name):
    R, C = w.shape
    tr = _tile(R, max(8, (1 << 19) // C), 8)
    c1 = 1.0 / (1.0 - ADAM_B1 ** ADAM_STEP)
    c2 = 1.0 / (1.0 - ADAM_B2 ** ADAM_STEP)

    def body(w_ref, g_ref, m_ref, v_ref, d_ref, nm_ref, nv_ref):
        g = g_ref[...]
        m = ADAM_B1 * m_ref[...] + (1.0 - ADAM_B1) * g
        v = ADAM_B2 * v_ref[...] + (1.0 - ADAM_B2) * (g * g)
        nm_ref[...] = m
        nv_ref[...] = v
        d_ref[...] = -ADAM_LR * ((m * c1) / (jnp.sqrt(v * c2) + ADAM_EPS) + ADAM_WD * w_ref[...])

    spec = pl.BlockSpec((tr, C), lambda i: (i, 0))
    out = jax.ShapeDtypeStruct((R, C), F32)
    return pl.pallas_call(
        body, name=name, grid=(R // tr,),
        in_specs=[spec] * 4, out_specs=[spec] * 3, out_shape=[out] * 3,
        compiler_params=_cparams(("parallel",)),
    )(w, g, m, v)


def _adamw_ada(w, cact_t, dmod, m, v):
    R, C = w.shape
    tr = _tile(R, max(8, (1 << 18) // C), 8)
    c1 = 1.0 / (1.0 - ADAM_B1 ** ADAM_STEP)
    c2 = 1.0 / (1.0 - ADAM_B2 ** ADAM_STEP)

    def body(w_ref, ct_ref, dm_ref, m_ref, v_ref, g_ref, d_ref, nm_ref, nv_ref):
        ct = ct_ref[...].astype(BF16).astype(F32)
        dm = dm_ref[...].astype(BF16).astype(F32)
        g = ct[:, 0:1] * dm[0:1, :]
        for b in range(1, N_DEV):
            g = g + ct[:, b:b + 1] * dm[b:b + 1, :]
        m = ADAM_B1 * m_ref[...] + (1.0 - ADAM_B1) * g
        v = ADAM_B2 * v_ref[...] + (1.0 - ADAM_B2) * (g * g)
        g_ref[...] = g
        nm_ref[...] = m
        nv_ref[...] = v
        d_ref[...] = -ADAM_LR * ((m * c1) / (jnp.sqrt(v * c2) + ADAM_EPS) + ADAM_WD * w_ref[...])

    spec = pl.BlockSpec((tr, C), lambda i: (i, 0))
    out = jax.ShapeDtypeStruct((R, C), F32)
    return pl.pallas_call(
        body, name="adamw_w_ada", grid=(R // tr,),
        in_specs=[spec, pl.BlockSpec((tr, N_DEV), lambda i: (i, 0)), pl.BlockSpec((N_DEV, C), lambda i: (0, 0)),
                  spec, spec],
        out_specs=[spec] * 4, out_shape=[out] * 4,
        compiler_params=_cparams(("parallel",)),
    )(w, cact_t, dmod, m, v)


def _adamw_reduced(w, own, got, m, v, my_chip, name):
    R, C = w.shape
    tr = _tile(R, max(PACK_ROW_ALIGN, (1 << 18) // C), PACK_ROW_ALIGN)
    c1 = 1.0 / (1.0 - ADAM_B1 ** ADAM_STEP)
    c2 = 1.0 / (1.0 - ADAM_B2 ** ADAM_STEP)

    def body(chip_ref, w_ref, own_ref, g1_ref, g2_ref, g3_ref, m_ref, v_ref, g_ref, d_ref, nm_ref, nv_ref):
        g = own_ref[0].astype(F32) + g1_ref[0].astype(F32) + g2_ref[0].astype(F32) + g3_ref[0].astype(F32)
        m = ADAM_B1 * m_ref[...] + (1.0 - ADAM_B1) * g
        v = ADAM_B2 * v_ref[...] + (1.0 - ADAM_B2) * (g * g)
        g_ref[...] = g
        nm_ref[...] = m
        nv_ref[...] = v
        d_ref[...] = -ADAM_LR * ((m * c1) / (jnp.sqrt(v * c2) + ADAM_EPS) + ADAM_WD * w_ref[...])

    spec = pl.BlockSpec((tr, C), lambda i, chip: (i, 0))
    slot = lambda k: pl.BlockSpec((1, tr, C), lambda i, chip: (chip[0] ^ k, i, 0))
    out = jax.ShapeDtypeStruct((R, C), F32)
    return pl.pallas_call(
        body, name=name,
        grid_spec=pltpu.PrefetchScalarGridSpec(
            num_scalar_prefetch=1, grid=(R // tr,),
            in_specs=[spec, slot(0), slot(1), slot(2), slot(3), spec, spec],
            out_specs=[spec] * 4),
        out_shape=[out] * 4,
        compiler_params=_cparams(("parallel",)),
    )(my_chip, w, own, got, got, got, m, v)


def _my_place():
    return lax.axis_index("x"), lax.axis_index("y"), lax.axis_index("c")


def _peer(k):
    x, y, c = _my_place()
    return (x ^ ((k >> 2) & 1), y ^ ((k >> 1) & 1), c ^ (k & 1))


def _linear(place):
    return 4 * place[0] + 2 * place[1] + place[2]


def _ada_fwd(c_row, wconv_row, w_ada, b_row):
    D, CW = w_ada.shape
    WC = wconv_row.shape[-1]

    def body(c_ref, wc_ref, w_ref, b_ref, mod_ref, cact_ref, wcall_ref, send_buf, sems):
        me = _linear(_my_place())
        c = c_ref[0]
        cact_ref[me] = c * _sigmoid(c)
        wcall_ref[me] = wc_ref[0]

        def gather_copy(buf, k, grp):
            return pltpu.make_async_remote_copy(
                src_ref=buf.at[me], dst_ref=buf.at[me], send_sem=sems.at[0, grp, k], recv_sem=sems.at[1, grp, k],
                device_id=_peer(k), device_id_type=MESH_ID)

        def gather_recv(buf, k, grp):
            src = _linear(_peer(k))
            return pltpu.make_async_remote_copy(
                src_ref=buf.at[src], dst_ref=buf.at[src], send_sem=sems.at[0, grp, k], recv_sem=sems.at[1, grp, k],
                device_id=_peer(k), device_id_type=MESH_ID)

        for k in range(1, N_DEV):
            gather_copy(cact_ref, k, 0).start()
            gather_copy(wcall_ref, k, 1).start()
        for k in range(1, N_DEV):
            gather_recv(cact_ref, k, 0).wait_recv()
            gather_recv(wcall_ref, k, 1).wait_recv()
        for k in range(1, N_DEV):
            gather_copy(cact_ref, k, 0).wait_send()
            gather_copy(wcall_ref, k, 1).wait_send()

        cact = jnp.concatenate([cact_ref[b] for b in range(N_DEV)], axis=0)
        mod_all = jnp.dot(cact.astype(BF16), w_ref[...].astype(BF16), preferred_element_type=F32) + b_ref[0]
        for b in range(N_DEV):
            send_buf[b] = mod_all[b:b + 1, :]
        mod_ref[me] = send_buf[me]

        def scatter_copy(k):
            dst = _linear(_peer(k))
            return pltpu.make_async_remote_copy(
                src_ref=send_buf.at[dst], dst_ref=mod_ref.at[me], send_sem=sems.at[0, 2, k], recv_sem=sems.at[1, 2, k],
                device_id=_peer(k), device_id_type=MESH_ID)

        def scatter_recv(k):
            src = _linear(_peer(k))
            return pltpu.make_async_remote_copy(
                src_ref=send_buf.at[src], dst_ref=mod_ref.at[src], send_sem=sems.at[0, 2, k], recv_sem=sems.at[1, 2, k],
                device_id=_peer(k), device_id_type=MESH_ID)

        for k in range(1, N_DEV):
            scatter_copy(k).start()
        for k in range(1, N_DEV):
            scatter_recv(k).wait_recv()
        for k in range(1, N_DEV):
            scatter_copy(k).wait_send()

    vmem = pl.BlockSpec(memory_space=pltpu.VMEM)
    return pl.pallas_call(
        body, name="ada_fwd",
        in_specs=[vmem] * 4, out_specs=[vmem] * 3,
        out_shape=[jax.ShapeDtypeStruct((N_DEV, 1, CW), F32), jax.ShapeDtypeStruct((N_DEV, 1, D), F32),
                   jax.ShapeDtypeStruct((N_DEV, 1, WC), F32)],
        scratch_shapes=[pltpu.VMEM((N_DEV, 1, CW), F32), pltpu.SemaphoreType.DMA((2, 3, N_DEV))],
        compiler_params=pltpu.CompilerParams(vmem_limit_bytes=VMEM_LIMIT),
    )(c_row, wconv_row, w_ada, b_row)


def _ada_bwd(payload, deps=()):
    NCH, _, CW = payload.shape

    def body(p_ref, *rest):
        sum_ref, mine_ref, all_ref, sems = rest[-4:]
        me = _linear(_my_place())
        all_ref[me] = p_ref[...]

        def copy(k, slot):
            return pltpu.make_async_remote_copy(
                src_ref=all_ref.at[slot], dst_ref=all_ref.at[slot], send_sem=sems.at[0, k], recv_sem=sems.at[1, k],
                device_id=_peer(k), device_id_type=MESH_ID)

        for k in range(1, N_DEV):
            copy(k, me).start()
        for k in range(1, N_DEV):
            copy(k, _linear(_peer(k))).wait_recv()
        for k in range(1, N_DEV):
            copy(k, me).wait_send()

        total = all_ref[0]
        for b in range(1, N_DEV):
            total = total + all_ref[b]
        sum_ref[...] = total

        for b in range(N_DEV):
            mine_ref[b] = all_ref[b, me]

    vmem = pl.BlockSpec(memory_space=pltpu.VMEM)
    return pl.pallas_call(
        body, name="ada_bwd",
        in_specs=[vmem] + [ANY_SPEC] * len(deps), out_specs=[vmem, vmem],
        out_shape=[jax.ShapeDtypeStruct((NCH, 1, CW), F32), jax.ShapeDtypeStruct((N_DEV, 1, CW), F32)],
        scratch_shapes=[pltpu.VMEM((N_DEV, NCH, 1, CW), F32), pltpu.SemaphoreType.DMA((2, N_DEV))],
        compiler_params=pltpu.CompilerParams(vmem_limit_bytes=VMEM_LIMIT),
    )(payload, *deps)


def _exchange_in_chip(parts):
    W = len(parts)

    def body(*refs):
        p_refs, got_refs, (send_sems, recv_sems) = refs[:W], refs[W:2 * W], refs[2 * W:]
        x, y, c = _my_place()
        sibling = (x, y, 1 - c)
        copies = []
        for w in range(W):
            for q in range(4):
                copies.append(pltpu.make_async_remote_copy(
                    src_ref=p_refs[w].at[2 * q + (1 - c)], dst_ref=got_refs[w].at[q],
                    send_sem=send_sems.at[4 * w + q], recv_sem=recv_sems.at[4 * w + q],
                    device_id=sibling, device_id_type=MESH_ID))
        for cp in copies:
            cp.start()
        for cp in copies:
            cp.wait_recv()
        for cp in copies:
            cp.wait_send()

    return pl.pallas_call(
        body, name="grad_exchange_in_chip",
        in_specs=[HBM_SPEC] * W, out_specs=[HBM_SPEC] * W,
        out_shape=[jax.ShapeDtypeStruct((4,) + p.shape[1:], p.dtype) for p in parts],
        scratch_shapes=[pltpu.SemaphoreType.DMA((4 * W,)), pltpu.SemaphoreType.DMA((4 * W,))],
    )(*parts)


def _pair_sum(parts, got, core):
    _, R, C = parts.shape
    tr = _tile(R, max(PACK_ROW_ALIGN, PAIR_SUM_BLOCK // C), PACK_ROW_ALIGN)

    def body(c_ref, p_ref, g_ref, o_ref):
        o_ref[...] = (p_ref[...].astype(F32) + g_ref[...].astype(F32)).astype(o_ref.dtype)

    return pl.pallas_call(
        body, name="grad_pair_sum",
        grid_spec=pltpu.PrefetchScalarGridSpec(
            num_scalar_prefetch=1, grid=(4, R // tr),
            in_specs=[pl.BlockSpec((1, tr, C), lambda q, i, c_ref: (2 * q + c_ref[0], i, 0)),
                      pl.BlockSpec((1, tr, C), lambda q, i, c_ref: (q, i, 0))],
            out_specs=pl.BlockSpec((1, tr, C), lambda q, i, c_ref: (q, i, 0))),
        out_shape=jax.ShapeDtypeStruct((4, R, C), parts.dtype),
        compiler_params=_cparams(("parallel", "parallel")),
    )(core, parts, got)


HBM_SPEC = pl.BlockSpec(memory_space=pltpu.HBM)
SEM_SPEC = pl.BlockSpec(memory_space=pltpu.SEMAPHORE)
ANY_SPEC = pl.BlockSpec(memory_space=pl.ANY)
SPLIT_EFFECT = pltpu.SideEffectType.DATAFLOW_SIDE_EFFECTING


def _landing_zone(shape, dtype):
    return pltpu.with_memory_space_constraint(lax.empty(shape, dtype), pltpu.HBM)


def _split_start(name, arrays, lands, after, copies_of, per_array):
    W = len(arrays)
    after = tuple(after) if isinstance(after, (tuple, list)) else (after,)

    def body(*refs):
        x_refs, land_refs = refs[:W], refs[W:2 * W]
        send_sems, recv_sems = refs[2 * W + len(after)], refs[2 * W + len(after) + 1]
        token = refs[-1]
        k = 0
        for w in range(W):
            for src, dst, dev in copies_of(w, x_refs[w], land_refs[w]):
                pltpu.make_async_remote_copy(src_ref=src, dst_ref=dst, send_sem=send_sems.at[k], recv_sem=recv_sems.at[k],
                                             device_id=dev, device_id_type=MESH_ID).start()
                k += 1
        token[...] = jnp.zeros_like(token)

    n_copies = per_array * W
    hbm_of = lambda xs: tuple(pltpu.HBM(a.shape, a.dtype) for a in xs)
    out = pl.pallas_call(
        body, name=name,
        out_shape=(pltpu.SemaphoreType.DMA((n_copies,)), pltpu.SemaphoreType.DMA((n_copies,)))
        + hbm_of(arrays) + hbm_of(lands) + (jax.ShapeDtypeStruct((8, LANE), F32),),
        in_specs=(HBM_SPEC,) * (2 * W) + (ANY_SPEC,) * len(after),
        out_specs=(SEM_SPEC, SEM_SPEC) + (HBM_SPEC,) * (2 * W) + (pl.BlockSpec(memory_space=pltpu.VMEM),),
        input_output_aliases={i: 2 + i for i in range(2 * W)},
        compiler_params=pltpu.CompilerParams(has_side_effects=SPLIT_EFFECT),
    )(*[pltpu.with_memory_space_constraint(a, pltpu.HBM) for a in arrays], *lands, *after)
    return out[0], out[1], list(out[2:2 + W]), list(out[2 + W:2 + 2 * W]), out[-1]


def _split_wait(name, state, after, copies_of):
    send_sems, recv_sems, arrays, lands, _ = state
    W = len(arrays)
    after = tuple(after) if isinstance(after, (tuple, list)) else (after,)

    def body(*refs):
        x_refs, land_refs = refs[:W], refs[W:2 * W]
        send_sems, recv_sems = refs[2 * W], refs[2 * W + 1]
        k = 0
        for w in range(W):
            for src, dst, dev in copies_of(w, x_refs[w], land_refs[w]):
                cp = pltpu.make_async_remote_copy(src_ref=src, dst_ref=dst, send_sem=send_sems.at[k],
                                                  recv_sem=recv_sems.at[k], device_id=dev, device_id_type=MESH_ID)
                cp.wait_send()
                cp.wait_recv()
                k += 1

    out = pl.pallas_call(
        body, name=name,
        out_shape=tuple(pltpu.HBM(a.shape, a.dtype) for a in arrays + lands),
        in_specs=(HBM_SPEC,) * (2 * W) + (SEM_SPEC, SEM_SPEC) + (ANY_SPEC,) * len(after),
        out_specs=(HBM_SPEC,) * (2 * W),
        input_output_aliases={i: i for i in range(2 * W)},
        compiler_params=pltpu.CompilerParams(has_side_effects=SPLIT_EFFECT),
    )(*arrays, *lands, send_sems, recv_sems, *after)
    return list(out[:W]), list(out[W:])


def _scatter_copies(w, p_ref, land_ref):
    x, y, c = _my_place()
    my_chip = 2 * x + y
    return [(p_ref.at[2 * (x ^ (k >> 1)) + (y ^ (k & 1))], land_ref.at[my_chip], (x ^ (k >> 1), y ^ (k & 1), c))
            for k in range(1, 4)]


def _gather_copies(w, x_ref, land_ref):
    x, y, c = _my_place()
    me = _linear((x, y, c))
    devs = [(x, y, 1 - c)] + [(x ^ (k >> 1), y ^ (k & 1), c) for k in range(1, 4)]
    return [(x_ref, land_ref.at[me], d) for d in devs]


def _gather_forward(lands, name):
    W = len(lands)

    def body(*refs):
        land_refs, out_refs, (send_sems, recv_sems) = refs[:W], refs[W:2 * W], refs[2 * W:]
        x, y, c = _my_place()
        sibling = (x, y, 1 - c)
        sends, arrivals = [], []
        for w in range(W):
            for k in range(1, 4):
                px, py = x ^ (k >> 1), y ^ (k & 1)
                landed, theirs = _linear((px, py, c)), out_refs[w].at[_linear((px, py, 1 - c))]
                sem = 3 * w + k - 1
                sends.append(pltpu.make_async_remote_copy(
                    src_ref=land_refs[w].at[landed], dst_ref=out_refs[w].at[landed],
                    send_sem=send_sems.at[sem], recv_sem=recv_sems.at[sem], device_id=sibling, device_id_type=MESH_ID))
                arrivals.append(pltpu.make_async_remote_copy(
                    src_ref=theirs, dst_ref=theirs, send_sem=send_sems.at[sem], recv_sem=recv_sems.at[sem],
                    device_id=sibling, device_id_type=MESH_ID))
        for cp in sends:
            cp.start()
        for cp in arrivals:
            cp.wait_recv()
        for cp in sends:
            cp.wait_send()

    return pl.pallas_call(
        body, name=name,
        in_specs=[HBM_SPEC] * W, out_specs=[HBM_SPEC] * W,
        out_shape=[jax.ShapeDtypeStruct(l.shape, l.dtype) for l in lands],
        input_output_aliases={i: i for i in range(W)},
        scratch_shapes=[pltpu.SemaphoreType.DMA((3 * W,)), pltpu.SemaphoreType.DMA((3 * W,))],
    )(*lands)


def _with_own_slot(gathered, shard):
    return lax.dynamic_update_index_in_dim(gathered, shard[None], _linear(_my_place()), axis=0)


def _in_chip_copies(w, p_ref, land_ref):
    x, y, c = _my_place()
    return [(p_ref.at[2 * q + (1 - c)], land_ref.at[q], (x, y, 1 - c)) for q in range(4)]


def _in_chip_start(parts, tag):
    lands = [_landing_zone((4,) + p.shape[1:], p.dtype) for p in parts]
    return _split_start("grad_in_chip_start_" + tag, parts, lands, (), _in_chip_copies, 4)


def _reduce_scatter_begin(parts, tag, in_chip_state=None, after=()):
    parts, early, got = list(parts), [], []
    if in_chip_state is not None:
        early, got = _split_wait("grad_in_chip_wait_" + tag, in_chip_state, after, _in_chip_copies)
    if parts:
        got = got + list(_exchange_in_chip(parts))
    parts = early + parts
    core = lax.axis_index("c").astype(jnp.int32).reshape(1)
    chip_parts = [_pair_sum(p, g, core) for p, g in zip(parts, got)]
    lands = [_landing_zone(p.shape, p.dtype) for p in chip_parts]
    return _split_start("grad_scatter_start_" + tag, chip_parts, lands, got[0], _scatter_copies, 3)


def _reduce_scatter_end(state, after, tag):
    return _split_wait("grad_scatter_wait_" + tag, state, after, _scatter_copies)


def kernel(x, c, positions, w_ada, b_ada, w_in, g_q_a, w_q_b, g_kv_a, w_kv_b, w_o_a, w_conv, w_o_b, w_o, ln1_g, ln1_b, w_ffn_in, w_ffn_out, ln2_g, ln2_b, loss_target, m_w_ada, m_b_ada, m_w_in, m_g_q_a, m_w_q_b, m_g_kv_a, m_w_kv_b, m_w_o_a, m_w_conv, m_w_o_b, m_w_o, m_ln1_g, m_ln1_b, m_w_ffn_in, m_w_ffn_out, m_ln2_g, m_ln2_b, v_w_ada, v_b_ada, v_w_in, v_g_q_a, v_w_q_b, v_g_kv_a, v_w_kv_b, v_w_o_a, v_w_conv, v_w_o_b, v_w_o, v_ln1_g, v_ln1_b, v_w_ffn_in, v_w_ffn_out, v_ln2_g, v_ln2_b):
    x2, tgt = x[0], loss_target[0]
    S, D = x2.shape
    Lq, Lkv = g_q_a.shape[1], g_kv_a.shape[1]
    H = w_q_b.shape[2] * N_DEV // QK_CAT
    F = w_ffn_out.shape[1] * N_DEV
    assert Lq == Lkv and (Lq + Lkv) % COL_BLOCK == 0 and D % COL_BLOCK == 0
    front = Lq + Lkv + QK_ROPE
    front_pad = _round_up(front, COL_BLOCK)
    kr_blk = (Lq + Lkv) // COL_BLOCK
    blk_b = front_pad // COL_BLOCK
    nblk = D // COL_BLOCK
    blk_c, blk_x, blk_ga, blk_gb = blk_b + nblk, blk_b + 2 * nblk, blk_b + 3 * nblk, blk_b + 4 * nblk
    ts = _tile(S, 256, 8)
    T = _tile(S, min(512, S // 2), CHUNK)
    tb = _tile(F, 2816)
    me = _linear(_my_place())

    cw = w_ada.shape[2]
    b_mine = lax.dynamic_slice(b_ada, (0, me * cw), (1, cw)).reshape(1, 1, cw)
    mod_blocks, cact_all, wconv_all = _ada_fwd(c.reshape(1, 1, D), w_conv[0].reshape(1, 1, -1), w_ada[0], b_mine)
    mod = mod_blocks.reshape(6, D)
    cact_all = cact_all.reshape(N_DEV, D)
    w_conv_full = wconv_all.reshape(N_DEV, CONV_K, -1).transpose(1, 0, 2).reshape(CONV_K, D)

    landing = lambda shards: [_landing_zone((N_DEV,) + s.shape, BF16) for s in shards]
    gathered = lambda lands, shards, tag: [_with_own_slot(g, s) for g, s in
                                           zip(_gather_forward(lands, tag + "_gather_forward"), shards)]
    half = D // 2
    w_in_b = w_in[0].astype(BF16)
    first, second = [w_in_b[:half]], [w_in_b[half:], w_q_b[0].astype(BF16), w_kv_b[0].astype(BF16)]
    mid = [w[0].astype(BF16) for w in (w_o_a, w_o_b, w_o)]
    last = [w[0].astype(BF16) for w in (w_ffn_in, w_ffn_out)]
    first_state = _split_start("first_gather_start", first, landing(first), mod_blocks, _gather_copies, 4)
    second_state = _split_start("second_gather_start", second, landing(second), first_state[4], _gather_copies, 4)
    u = _modulate_in(x2, mod, ts)

    first_shards, first_lands = _split_wait("first_gather_wait", first_state, (u, second_state[4]), _gather_copies)
    (g_in_top,) = gathered(first_lands, first_shards, "first")
    w_in_top = _assemble_w_in(g_in_top, front, front_pad, D, 0)
    proj_top = _matmul(u, w_in_top, "nn", BF16, "proj_top", k_rows=(0, half), tn=WIDE_TN)
    second_shards, second_lands = _split_wait("second_gather_wait", second_state, (proj_top,), _gather_copies)
    g_in_bottom, wq_s, wkv_s = gathered(second_lands, second_shards, "second")
    mid_state = _split_start("mid_gather_start", mid, landing(mid), g_in_bottom, _gather_copies, 4)
    last_state = _split_start("last_gather_start", last, landing(last), mid_state[4], _gather_copies, 4)
    w_in_p = _assemble_w_in(g_in_bottom, front, front_pad, D, half, into=w_in_top)

    inv_freq = 1.0 / (ROPE_THETA ** (jnp.arange(0, QK_ROPE, 2, dtype=F32) / QK_ROPE))
    ang = positions[0].astype(F32)[:, None] * inv_freq
    cos2 = jnp.concatenate([jnp.cos(ang), jnp.cos(ang)], axis=-1)
    sin2 = jnp.concatenate([jnp.sin(ang), jnp.sin(ang)], axis=-1)
    one, zero = jnp.ones((S, QK_NOPE), F32), jnp.zeros((S, QK_NOPE), F32)
    cos_q, sin_q = jnp.concatenate([one, cos2, one, cos2], axis=-1), jnp.concatenate([zero, sin2, zero, sin2], axis=-1)
    cos_k, sin_k = jnp.tile(cos2, (1, COL_BLOCK // QK_ROPE)), jnp.tile(sin2, (1, COL_BLOCK // QK_ROPE))

    proj = _matmul(u, w_in_p, "nn", BF16, "proj", k_rows=(half, half), init=proj_top, deps=(last_state[4],),
                   tn=WIDE_TN)
    qn = _rms_fwd(proj, g_q_a, 0, Lq, ts, "rms_q")
    kvn = _rms_fwd(proj, g_kv_a, 1, Lkv, ts, "rms_kv")
    q = _matmul(qn, wq_s, "nn", BF16, "q_up")
    kv = _matmul(kvn, wkv_s, "nn", BF16, "kv_up")
    qc, kc, vh = _qk_prep(q, kv, proj, kr_blk, cos_q, sin_q, cos_k, sin_k, H, ts)
    attn, lse = _attn_fwd(qc, kc, vh, T)
    mid_shards, mid_lands = _split_wait("mid_gather_wait", mid_state, lse, _gather_copies)
    w_oa_f, w_ob_f, w_o_f = [g.reshape(-1, D) for g in gathered(mid_lands, mid_shards, "mid")]
    ya = _matmul(attn, w_oa_f, "nn", BF16, "attn_out")
    cbc = _conv_fwd(proj, w_conv_full, blk_b, blk_c, blk_x)
    yb = _matmul(cbc, w_ob_f, "nn", BF16, "conv_out")
    merged = _merge_fwd(proj, ya, yb, blk_ga, blk_gb, ts)
    mix = _matmul(merged, w_o_f, "nn", F32, "mix_out")
    xhat1, rstd1, u2 = _ln1_fwd(x2, mix, mod, ln1_g, ln1_b, ts)
    last_shards, last_lands = _split_wait("last_gather_wait", last_state, u2, _gather_copies)
    w_fi_s, g_fo = gathered(last_lands, last_shards, "last")
    w_fo_f = g_fo.reshape(F, D)
    hh = _matmul(u2, w_fi_s, "nn", BF16, "ffn_in")
    act = _swiglu_fwd(hh, min(S, 2 * ts), tb)
    ffn = _matmul(act, w_fo_f, "nn", F32, "ffn_out", tk=2 * FFN_TILE)
    loss_part, dffn, dx1a, vec2 = _ln2_loss(xhat1, ffn, tgt, mod, ln1_g, ln1_b, ln2_g, ln2_b, ts)
    loss = lax.psum(loss_part[0, 0], AXES)

    gw_fo = _matmul(act, dffn, "tn", BF16, "grad_w_ffn_out", tm=FFN_TILE)
    da = _matmul(dffn, w_fo_f, "nt", BF16, "d_act", tn=FFN_TILE)
    dh = _swiglu_bwd(da, hh, min(S, 2 * ts), tb)
    gw_fi = _matmul(u2, dh, "tn", BF16, "grad_w_ffn_in", out_shards=True)
    ffn_in_chip = _in_chip_start([gw_fi, gw_fo.reshape(N_DEV, -1, D)], "ffn")
    du2 = _matmul(dh, w_fi_s, "nt", F32, "d_u2", deps=(ffn_in_chip[4],))
    ffn_state = _reduce_scatter_begin([], "ffn", ffn_in_chip, after=(du2,))
    dxa, dmix, vec1 = _ln1_bwd(du2, dx1a, xhat1, rstd1, mix, mod, ln1_g, ln1_b, ts)
    gw_o = _matmul(merged, dmix, "tn", BF16, "grad_w_o", deps=(ffn_state[4],))
    dmerged = _matmul(dmix, w_o_f, "nt", BF16, "d_merged")
    dya, dyb, dga, dgb = _merge_bwd(dmerged, proj, ya, yb, blk_ga, blk_gb, ts)
    gw_ob = _matmul(cbc, dyb, "tn", BF16, "grad_w_o_b")
    dcbc = _matmul(dyb, w_ob_f, "nt", BF16, "d_conv")
    dcb, dcc, dcx, dwconv = _conv_bwd(dcbc, proj, w_conv_full, blk_b, blk_c, blk_x)
    gw_oa = _matmul(attn, dya, "tn", BF16, "grad_w_o_a")
    mix_in_chip = _in_chip_start([g.reshape(N_DEV, -1, D) for g in (gw_oa, gw_ob, gw_o)], "mix")
    dattn = _matmul(dya, w_oa_f, "nt", BF16, "d_attn", deps=(mix_in_chip[4],))
    dqc, dkc, dvh = _attn_bwd(qc, kc, vh, dattn, attn, lse, T)
    ffn_own, ffn_got = _reduce_scatter_end(ffn_state, dqc, "ffn")
    dq, dkv, dkr = _qk_bwd(dqc, dkc, dvh, cos_q, sin_q, cos_k, sin_k, ts)
    gw_qb = _matmul(qn, dq, "tn", BF16, "grad_w_q_b", out_shards=True)
    gw_kvb = _matmul(kvn, dkv, "tn", BF16, "grad_w_kv_b", out_shards=True)
    mix_state = _reduce_scatter_begin([gw_qb, gw_kvb], "mix", mix_in_chip, after=(dqc,))
    dqn = _matmul(dq, wq_s, "nt", F32, "d_qn", deps=(mix_state[4],))
    dkvn = _matmul(dkv, wkv_s, "nt", F32, "d_kvn")
    dqa, dgq = _rms_bwd(dqn, proj, g_q_a, 0, Lq, ts, "rms_q_bwd")
    dkva, dgkv = _rms_bwd(dkvn, proj, g_kv_a, 1, Lkv, ts, "rms_kv_bwd")
    dproj = jnp.concatenate([dqa, dkva, dkr, dcb, dcc, dcx, dga, dgb], axis=1)
    gw_in_p = _matmul(u, dproj, "tn", BF16, "grad_w_in", tn=WIDE_TN)
    mix_own, mix_got = _reduce_scatter_end(mix_state, gw_in_p, "mix")
    in_state = _reduce_scatter_begin([_split_w_in(gw_in_p, front, front_pad)], "in")
    du = _matmul(dproj, w_in_p, "nt", F32, "d_u", deps=(in_state[4],), tk=WIDE_TK)
    grad_x, vec0 = _grad_x(du, dxa, x2, mod, ts)

    my_chip = (2 * lax.axis_index("x") + lax.axis_index("y")).astype(jnp.int32).reshape(1)
    arrived = {}
    for nm, w, m, v, own, got in (
            ("w_ffn_in", w_ffn_in, m_w_ffn_in, v_w_ffn_in, ffn_own[0], ffn_got[0]),
            ("w_ffn_out", w_ffn_out, m_w_ffn_out, v_w_ffn_out, ffn_own[1], ffn_got[1]),
            ("w_o_a", w_o_a, m_w_o_a, v_w_o_a, mix_own[0], mix_got[0]),
            ("w_o_b", w_o_b, m_w_o_b, v_w_o_b, mix_own[1], mix_got[1]),
            ("w_o", w_o, m_w_o, v_w_o, mix_own[2], mix_got[2]),
            ("w_q_b", w_q_b, m_w_q_b, v_w_q_b, mix_own[3], mix_got[3]),
            ("w_kv_b", w_kv_b, m_w_kv_b, v_w_kv_b, mix_own[4], mix_got[4])):
        arrived[nm] = [a[None] for a in _adamw_reduced(w[0], own, got, m[0], v[0], my_chip, "adamw_" + nm)]

    dmod = jnp.concatenate([vec0[0], vec0[1], vec1[4], vec1[0], vec1[1], vec2[2]])
    small = jnp.concatenate([dmod, dgq[0], dgkv[0], vec1[2], vec1[3], vec2[0], vec2[1], dwconv[:CONV_K].reshape(-1)])
    n_small = small.shape[0]
    nch = _round_up(n_small, cw) // cw
    payload = jnp.pad(small, (0, nch * cw - n_small)).reshape(nch, 1, cw)
    in_own, in_got = _reduce_scatter_end(in_state, [res[1] for res in arrived.values()] + [grad_x, payload], "in")
    arrived["w_in"] = [a[None] for a in _adamw_reduced(w_in[0], in_own[0], in_got[0], m_w_in[0], v_w_in[0], my_chip,
                                                       "adamw_w_in")]
    summed, dmod_mine = _ada_bwd(payload, deps=[arrived["w_in"][1]])
    arrived["w_ada"] = [a[None] for a in _adamw_ada(w_ada[0], cact_all.T, dmod_mine.reshape(N_DEV, cw),
                                                    m_w_ada[0], v_w_ada[0])]
    summed = summed.reshape(-1)
    offs = [0, 6 * D, 6 * D + Lq, 6 * D + Lq + Lkv]
    offs += [offs[-1] + D * k for k in range(1, 5)]
    g_b_ada = summed[offs[0]:offs[1]].reshape(1, -1)
    g_gq = summed[offs[1]:offs[2]].reshape(1, -1)
    g_gkv = summed[offs[2]:offs[3]].reshape(1, -1)
    g_ln1g, g_ln1b, g_ln2g, g_ln2b = [summed[offs[3 + k]:offs[4 + k]].reshape(1, -1) for k in range(4)]
    wc = w_conv.shape[2]
    g_wconv = lax.dynamic_slice(summed[offs[7]:offs[7] + CONV_K * D].reshape(CONV_K, D), (0, me * wc), (CONV_K, wc))

    names = ["w_ada", "b_ada", "w_in", "g_q_a", "w_q_b", "g_kv_a", "w_kv_b", "w_o_a", "w_conv", "w_o_b", "w_o",
             "ln1_g", "ln1_b", "w_ffn_in", "w_ffn_out", "ln2_g", "ln2_b"]
    weights = [w_ada, b_ada, w_in, g_q_a, w_q_b, g_kv_a, w_kv_b, w_o_a, w_conv, w_o_b, w_o, ln1_g, ln1_b,
               w_ffn_in, w_ffn_out, ln2_g, ln2_b]
    moms = [m_w_ada, m_b_ada, m_w_in, m_g_q_a, m_w_q_b, m_g_kv_a, m_w_kv_b, m_w_o_a, m_w_conv, m_w_o_b, m_w_o,
            m_ln1_g, m_ln1_b, m_w_ffn_in, m_w_ffn_out, m_ln2_g, m_ln2_b]
    vels = [v_w_ada, v_b_ada, v_w_in, v_g_q_a, v_w_q_b, v_g_kv_a, v_w_kv_b, v_w_o_a, v_w_conv, v_w_o_b, v_w_o,
            v_ln1_g, v_ln1_b, v_w_ffn_in, v_w_ffn_out, v_ln2_g, v_ln2_b]
    grad_of = {"b_ada": g_b_ada, "g_q_a": g_gq, "g_kv_a": g_gkv, "w_conv": g_wconv,
               "ln1_g": g_ln1g, "ln1_b": g_ln1b, "ln2_g": g_ln2g, "ln2_b": g_ln2b}
    state_of = dict(zip(names, zip(weights, moms, vels)))
    results = dict(arrived)

    def update(nm, reduced=None):
        w, m, v = state_of[nm]
        shp = w.shape
        w2 = w.reshape(shp[-2], shp[-1]) if w.ndim == 3 else w
        m2, v2 = m.reshape(w2.shape), v.reshape(w2.shape)
        if reduced is None:
            g2 = grad_of[nm].reshape(w2.shape)
            res = (g2,) + tuple(_adamw(w2, g2, m2, v2, "adamw_" + nm))
        else:
            res = _adamw_reduced(w2, reduced[0], reduced[1], m2, v2, my_chip, "adamw_" + nm)
        results[nm] = [a.reshape(shp) for a in res]

    for nm in grad_of:
        update(nm)
    outs = [[results[nm][k] for nm in names] for k in range(4)]
    return (loss, grad_x.reshape(x.shape), *outs[0], *outs[1], *outs[2], *outs[3])
```
